```python
import math
import jax, jax.numpy as jnp
from jax import lax
import numpy as np

D_MODEL = 2048
BATCH = 8
SEQ = 2048
DEPTH = 1

HEAD_DIM = 128
N_HEADS_A = 8
D_A = N_HEADS_A * HEAD_DIM
DILATION_PATTERNS = ((128, 1), (512, 4), (2048, 16))
N_BUCKETS = 32
MAX_DISTANCE = 2048
CHUNK = 128
N_GROUPS_B = 8
D_GROUP_B = 128
D_B = N_GROUPS_B * D_GROUP_B
D_FF = 4 * D_MODEL
D_IN = 3 * D_A + 2 * D_B + 2 * D_MODEL
SPLITS = tuple(np.cumsum([D_A, D_A, D_A, D_B, D_B, D_MODEL]).tolist())
ALPHA = (2 * DEPTH) ** 0.25
BETA = (8 * DEPTH) ** -0.25
LN_EPS = 1e-5
NEG_INF = -1e30

kernel_name = "hybrid_dilated_attn_gmlp_block"


def layer_norm(x, gain, bias):
    xf = x.astype(jnp.float32)
    mean = jnp.mean(xf, axis=-1, keepdims=True)
    var = jnp.mean(jnp.square(xf - mean), axis=-1, keepdims=True)
    y = (xf - mean) * lax.rsqrt(var + LN_EPS) * gain.astype(jnp.float32) + bias.astype(jnp.float32)
    return y.astype(x.dtype)


def t5_causal_bucket(n):
    max_exact = N_BUCKETS // 2
    nf = jnp.maximum(n, 1).astype(jnp.float32)
    large = max_exact + (jnp.log(nf / max_exact) / math.log(MAX_DISTANCE / max_exact)
                         * (N_BUCKETS - max_exact)).astype(jnp.int32)
    large = jnp.minimum(large, N_BUCKETS - 1)
    return jnp.where(n < max_exact, n, large)


def dilated_window_attention(q, k, v, rel_bias, window, dilation):
    B, S, H, Dh = q.shape
    nb = window // dilation
    L = S // dilation
    nblk = -(-L // nb)
    Lp = nblk * nb

    def to_sub(t):
        t = t.reshape(B, L, dilation, H, Dh).transpose(0, 2, 1, 3, 4)
        t = jnp.pad(t, ((0, 0), (0, 0), (0, Lp - L), (0, 0), (0, 0)))
        return t.reshape(B, dilation, nblk, nb, H, Dh)

    def with_prev(t):
        prev = jnp.concatenate([jnp.zeros_like(t[:, :, :1]), t[:, :, :-1]], axis=2)
        return jnp.concatenate([prev, t], axis=3)

    qs = to_sub(q)
    kw = with_prev(to_sub(k))
    vw = with_prev(to_sub(v))

    scores = jnp.einsum('brnqhd,brnkhd->brnhqk', qs, kw).astype(jnp.float32) * (HEAD_DIM ** -0.5)

    qi = jnp.arange(nb)[:, None]
    kj = jnp.arange(2 * nb)[None, :]
    steps = nb + qi - kj
    band = (steps >= 0) & (steps <= nb)
    blk = jnp.arange(nblk)[:, None, None]
    key_ok = (blk * nb + kj[None] - nb) >= 0
    mask = band[None] & key_ok
    bucket = t5_causal_bucket(jnp.clip(steps, 0, nb) * dilation)
    bias = jnp.transpose(rel_bias[bucket].astype(jnp.float32), (2, 0, 1))

    scores = jnp.where(mask[None, None, :, None], scores + bias[None, None, None], NEG_INF)
    m = jnp.max(scores, axis=-1, keepdims=True)
    p = jnp.exp(scores - m)
    den = jnp.sum(p, axis=-1, keepdims=True)
    out = jnp.einsum('brnhqk,brnkhd->brnqhd', p, vw.astype(jnp.float32))
    den_t = jnp.transpose(den[..., 0], (0, 1, 2, 4, 3))
    out = out / den_t[..., None]
    lse = jnp.transpose(m[..., 0], (0, 1, 2, 4, 3)) + jnp.log(den_t)

    def from_sub(t):
        rest = t.shape[5:]
        t = t.reshape((B, dilation, Lp, H) + rest)[:, :, :L]
        t = jnp.moveaxis(t, 1, 2)
        return t.reshape((B, S, H) + rest)

    return from_sub(out), from_sub(lse)


def token_mixers(x, w_in, rel_bias, ln_v_gain, ln_v_bias, w_spatial, b_spatial,
                 w_proj_a, w_proj_b, w_out):
    B, S, _ = x.shape
    proj = jnp.einsum('bsd,de->bse', x, w_in)
    q, k, v, u, vb, ga, gb = jnp.split(proj, SPLITS, axis=-1)

    q = q.reshape(B, S, N_HEADS_A, HEAD_DIM)
    k = k.reshape(B, S, N_HEADS_A, HEAD_DIM)
    v = v.reshape(B, S, N_HEADS_A, HEAD_DIM)
    outs, lses = [], []
    for window, dilation in DILATION_PATTERNS:
        o, l = dilated_window_attention(q, k, v, rel_bias, window, dilation)
        outs.append(o)
        lses.append(l)
    mix_w = jax.nn.softmax(jnp.stack(lses, axis=0), axis=0)
    attn = jnp.sum(mix_w[..., None] * jnp.stack(outs, axis=0), axis=0)
    attn = attn.astype(x.dtype).reshape(B, S, D_A)

    u = jax.nn.gelu(u)
    vb = layer_norm(jax.nn.gelu(vb), ln_v_gain, ln_v_bias)
    nc = S // CHUNK
    vr = vb.reshape(B, nc, CHUNK, N_GROUPS_B, D_GROUP_B)
    causal = jnp.tril(jnp.ones((CHUNK, CHUNK), dtype=bool))
    ws = jnp.where(causal[None], w_spatial, 0.0).astype(vr.dtype)
    z = jnp.einsum('gij,bcjgd->bcigd', ws, vr) + jnp.transpose(b_spatial)[None, None, :, :, None]
    gmlp = (u.reshape(B, nc, CHUNK, N_GROUPS_B, D_GROUP_B) * z).reshape(B, S, D_B)

    y_a = jnp.einsum('bse,ed->bsd', attn, w_proj_a)
    y_b = jnp.einsum('bse,ed->bsd', gmlp, w_proj_b)
    merged = jax.nn.sigmoid(ga) * y_a + jax.nn.sigmoid(gb) * y_b
    return jnp.einsum('bsd,de->bse', merged, w_out)


def squared_relu_mlp(h, w_ff1, b_ff1, w_ff2, b_ff2):
    a = jnp.square(jax.nn.relu(jnp.einsum('bsd,df->bsf', h, w_ff1) + b_ff1))
    return jnp.einsum('bsf,fd->bsd', a, w_ff2) + b_ff2


def _fwd_setup_inputs(seed: int = 0) -> dict:
    key = jax.random.key(seed)
    ks = jax.random.split(key, 20)
    nrm = jax.random.normal
    f32 = jnp.float32
    x = nrm(ks[0], (BATCH, SEQ, D_MODEL), f32)
    col_scale = jnp.concatenate([
        jnp.ones((2 * D_A,), f32),
        BETA * jnp.ones((D_A,), f32),
        jnp.ones((2 * D_B + 2 * D_MODEL,), f32)]) * (D_MODEL ** -0.5)
    w_in = nrm(ks[1], (DEPTH, D_MODEL, D_IN), f32) * col_scale
    rel_bias = 0.1 * nrm(ks[2], (N_BUCKETS, N_HEADS_A), f32)
    ln_v_gain = 1.0 + 0.01 * nrm(ks[3], (DEPTH, D_B), f32)
    ln_v_bias = 0.01 * nrm(ks[4], (DEPTH, D_B), f32)
    w_spatial = nrm(ks[5], (DEPTH, N_GROUPS_B, CHUNK, CHUNK), f32) * (CHUNK ** -0.5)
    b_spatial = 1.0 + 0.01 * nrm(ks[6], (DEPTH, N_GROUPS_B, CHUNK), f32)
    w_proj_a = nrm(ks[7], (DEPTH, D_A, D_MODEL), f32) * (D_A ** -0.5) * BETA
    w_proj_b = nrm(ks[8], (DEPTH, D_B, D_MODEL), f32) * (D_B ** -0.5) * BETA
    w_out = nrm(ks[9], (DEPTH, D_MODEL, D_MODEL), f32) * (D_MODEL ** -0.5) * BETA
    ln1_gain = 1.0 + 0.01 * nrm(ks[10], (DEPTH, D_MODEL), f32)
    ln1_bias = 0.01 * nrm(ks[11], (DEPTH, D_MODEL), f32)
    w_ff1 = nrm(ks[12], (DEPTH, D_MODEL, D_FF), f32) * (D_MODEL ** -0.5) * BETA
    b_ff1 = 0.01 * nrm(ks[13], (DEPTH, D_FF), f32)
    w_ff2 = nrm(ks[14], (DEPTH, D_FF, D_MODEL), f32) * (D_FF ** -0.5) * BETA
    b_ff2 = 0.01 * nrm(ks[15], (DEPTH, D_MODEL), f32)
    ln2_gain = 1.0 + 0.01 * nrm(ks[16], (DEPTH, D_MODEL), f32)
    ln2_bias = 0.01 * nrm(ks[17], (DEPTH, D_MODEL), f32)
    return {"x": x, "w_in": w_in, "rel_bias": rel_bias, "ln_v_gain": ln_v_gain,
            "ln_v_bias": ln_v_bias, "w_spatial": w_spatial, "b_spatial": b_spatial,
            "w_proj_a": w_proj_a, "w_proj_b": w_proj_b, "w_out": w_out,
            "ln1_gain": ln1_gain, "ln1_bias": ln1_bias, "w_ff1": w_ff1, "b_ff1": b_ff1,
            "w_ff2": w_ff2, "b_ff2": b_ff2, "ln2_gain": ln2_gain, "ln2_bias": ln2_bias}


def _fwd_reference(x, w_in, rel_bias, ln_v_gain, ln_v_bias, w_spatial, b_spatial,
              w_proj_a, w_proj_b, w_out, ln1_gain, ln1_bias, w_ff1, b_ff1,
              w_ff2, b_ff2, ln2_gain, ln2_bias):
    h = x
    for layer in range(DEPTH):
        mix = token_mixers(h, w_in[layer], rel_bias, ln_v_gain[layer], ln_v_bias[layer],
                           w_spatial[layer], b_spatial[layer], w_proj_a[layer],
                           w_proj_b[layer], w_out[layer])
        h = layer_norm(ALPHA * h + mix, ln1_gain[layer], ln1_bias[layer])
        ff = squared_relu_mlp(h, w_ff1[layer], b_ff1[layer], w_ff2[layer], b_ff2[layer])
        h = layer_norm(ALPHA * h + ff, ln2_gain[layer], ln2_bias[layer])
    return h


import jax as _jax
import jax.numpy as _jnp

TWIN_FORMAT = 'train_step'
FWD_PARAMS = ['x', 'w_in', 'rel_bias', 'ln_v_gain', 'ln_v_bias', 'w_spatial', 'b_spatial', 'w_proj_a', 'w_proj_b', 'w_out', 'ln1_gain', 'ln1_bias', 'w_ff1', 'b_ff1', 'w_ff2', 'b_ff2', 'ln2_gain', 'ln2_bias']
TWIN_WEIGHTS = ['w_in', 'rel_bias', 'ln_v_gain', 'ln_v_bias', 'w_spatial', 'b_spatial', 'w_proj_a', 'w_proj_b', 'w_out', 'ln1_gain', 'ln1_bias', 'w_ff1', 'b_ff1', 'w_ff2', 'b_ff2', 'ln2_gain', 'ln2_bias']
TWIN_DIFF_INPUT = 'x'
TWIN_INPUTS = ['x', 'w_in', 'rel_bias', 'ln_v_gain', 'ln_v_bias', 'w_spatial', 'b_spatial', 'w_proj_a', 'w_proj_b', 'w_out', 'ln1_gain', 'ln1_bias', 'w_ff1', 'b_ff1', 'w_ff2', 'b_ff2', 'ln2_gain', 'ln2_bias', 'loss_target', 'm_w_in', 'm_rel_bias', 'm_ln_v_gain', 'm_ln_v_bias', 'm_w_spatial', 'm_b_spatial', 'm_w_proj_a', 'm_w_proj_b', 'm_w_out', 'm_ln1_gain', 'm_ln1_bias', 'm_w_ff1', 'm_b_ff1', 'm_w_ff2', 'm_b_ff2', 'm_ln2_gain', 'm_ln2_bias', 'v_w_in', 'v_rel_bias', 'v_ln_v_gain', 'v_ln_v_bias', 'v_w_spatial', 'v_b_spatial', 'v_w_proj_a', 'v_w_proj_b', 'v_w_out', 'v_ln1_gain', 'v_ln1_bias', 'v_w_ff1', 'v_b_ff1', 'v_w_ff2', 'v_b_ff2', 'v_ln2_gain', 'v_ln2_bias']
TWIN_OUTPUTS = ['loss', 'grad_x', 'grad_w_in', 'grad_rel_bias', 'grad_ln_v_gain', 'grad_ln_v_bias', 'grad_w_spatial', 'grad_b_spatial', 'grad_w_proj_a', 'grad_w_proj_b', 'grad_w_out', 'grad_ln1_gain', 'grad_ln1_bias', 'grad_w_ff1', 'grad_b_ff1', 'grad_w_ff2', 'grad_b_ff2', 'grad_ln2_gain', 'grad_ln2_bias', 'delta_w_in', 'delta_rel_bias', 'delta_ln_v_gain', 'delta_ln_v_bias', 'delta_w_spatial', 'delta_b_spatial', 'delta_w_proj_a', 'delta_w_proj_b', 'delta_w_out', 'delta_ln1_gain', 'delta_ln1_bias', 'delta_w_ff1', 'delta_b_ff1', 'delta_w_ff2', 'delta_b_ff2', 'delta_ln2_gain', 'delta_ln2_bias', 'new_m_w_in', 'new_m_rel_bias', 'new_m_ln_v_gain', 'new_m_ln_v_bias', 'new_m_w_spatial', 'new_m_b_spatial', 'new_m_w_proj_a', 'new_m_w_proj_b', 'new_m_w_out', 'new_m_ln1_gain', 'new_m_ln1_bias', 'new_m_w_ff1', 'new_m_b_ff1', 'new_m_w_ff2', 'new_m_b_ff2', 'new_m_ln2_gain', 'new_m_ln2_bias', 'new_v_w_in', 'new_v_rel_bias', 'new_v_ln_v_gain', 'new_v_ln_v_bias', 'new_v_w_spatial', 'new_v_b_spatial', 'new_v_w_proj_a', 'new_v_w_proj_b', 'new_v_w_out', 'new_v_ln1_gain', 'new_v_ln1_bias', 'new_v_w_ff1', 'new_v_b_ff1', 'new_v_w_ff2', 'new_v_b_ff2', 'new_v_ln2_gain', 'new_v_ln2_bias']
TWIN_LEAF_KINDS = {'loss': 'loss', 'grad_x': 'grad_x', 'grad_w_in': 'grad_w', 'grad_rel_bias': 'grad_w', 'grad_ln_v_gain': 'grad_w', 'grad_ln_v_bias': 'grad_w', 'grad_w_spatial': 'grad_w', 'grad_b_spatial': 'grad_w', 'grad_w_proj_a': 'grad_w', 'grad_w_proj_b': 'grad_w', 'grad_w_out': 'grad_w', 'grad_ln1_gain': 'grad_w', 'grad_ln1_bias': 'grad_w', 'grad_w_ff1': 'grad_w', 'grad_b_ff1': 'grad_w', 'grad_w_ff2': 'grad_w', 'grad_b_ff2': 'grad_w', 'grad_ln2_gain': 'grad_w', 'grad_ln2_bias': 'grad_w', 'delta_w_in': 'delta_w', 'delta_rel_bias': 'delta_w', 'delta_ln_v_gain': 'delta_w', 'delta_ln_v_bias': 'delta_w', 'delta_w_spatial': 'delta_w', 'delta_b_spatial': 'delta_w', 'delta_w_proj_a': 'delta_w', 'delta_w_proj_b': 'delta_w', 'delta_w_out': 'delta_w', 'delta_ln1_gain': 'delta_w', 'delta_ln1_bias': 'delta_w', 'delta_w_ff1': 'delta_w', 'delta_b_ff1': 'delta_w', 'delta_w_ff2': 'delta_w', 'delta_b_ff2': 'delta_w', 'delta_ln2_gain': 'delta_w', 'delta_ln2_bias': 'delta_w', 'new_m_w_in': 'new_m', 'new_m_rel_bias': 'new_m', 'new_m_ln_v_gain': 'new_m', 'new_m_ln_v_bias': 'new_m', 'new_m_w_spatial': 'new_m', 'new_m_b_spatial': 'new_m', 'new_m_w_proj_a': 'new_m', 'new_m_w_proj_b': 'new_m', 'new_m_w_out': 'new_m', 'new_m_ln1_gain': 'new_m', 'new_m_ln1_bias': 'new_m', 'new_m_w_ff1': 'new_m', 'new_m_b_ff1': 'new_m', 'new_m_w_ff2': 'new_m', 'new_m_b_ff2': 'new_m', 'new_m_ln2_gain': 'new_m', 'new_m_ln2_bias': 'new_m', 'new_v_w_in': 'new_v', 'new_v_rel_bias': 'new_v', 'new_v_ln_v_gain': 'new_v', 'new_v_ln_v_bias': 'new_v', 'new_v_w_spatial': 'new_v', 'new_v_b_spatial': 'new_v', 'new_v_w_proj_a': 'new_v', 'new_v_w_proj_b': 'new_v', 'new_v_w_out': 'new_v', 'new_v_ln1_gain': 'new_v', 'new_v_ln1_bias': 'new_v', 'new_v_w_ff1': 'new_v', 'new_v_b_ff1': 'new_v', 'new_v_w_ff2': 'new_v', 'new_v_b_ff2': 'new_v', 'new_v_ln2_gain': 'new_v', 'new_v_ln2_bias': 'new_v'}


def _forward(args):
    return _fwd_reference(*[args[k] for k in FWD_PARAMS])


def _output_shape():
    out = _jax.eval_shape(lambda: _forward(_fwd_setup_inputs(0)))
    return out.shape, out.dtype

N_MICROBATCH = 1
ADAM_LR = 0.001
ADAM_B1 = 0.9
ADAM_B2 = 0.999
ADAM_EPS = 1e-08
ADAM_WD = 0.01
ADAM_STEP = 10
PER_EXAMPLE_BATCH_AXIS = {'x': 0, 'loss_target': 0}
SHARED_INPUTS = []
_WEIGHT_DTYPES = {'w_in': _jnp.float32, 'rel_bias': _jnp.float32, 'ln_v_gain': _jnp.float32, 'ln_v_bias': _jnp.float32, 'w_spatial': _jnp.float32, 'b_spatial': _jnp.float32, 'w_proj_a': _jnp.float32, 'w_proj_b': _jnp.float32, 'w_out': _jnp.float32, 'ln1_gain': _jnp.float32, 'ln1_bias': _jnp.float32, 'w_ff1': _jnp.float32, 'b_ff1': _jnp.float32, 'w_ff2': _jnp.float32, 'b_ff2': _jnp.float32, 'ln2_gain': _jnp.float32, 'ln2_bias': _jnp.float32}
MOMENT_SCALE = {'w_in': 5.086502e-03, 'rel_bias': 3.362058e-03, 'ln_v_gain': 6.952222e-03, 'ln_v_bias': 6.393416e-03, 'w_spatial': 6.632960e-03, 'b_spatial': 9.501544e-03, 'w_proj_a': 2.369067e-03, 'w_proj_b': 1.401504e-02, 'w_out': 1.406737e-02, 'ln1_gain': 1.350581e-01, 'ln1_bias': 8.723166e-02, 'w_ff1': 1.285398e-02, 'b_ff1': 1.447783e-02, 'w_ff2': 2.379691e-02, 'b_ff2': 7.233267e-02, 'ln2_gain': 8.001405e+00, 'ln2_bias': 7.331175e-01}


def _to_microbatches(a, axis):
    t = _jnp.moveaxis(a, axis, 0)
    t = t.reshape((N_MICROBATCH, t.shape[0] // N_MICROBATCH) + t.shape[1:])
    return _jnp.moveaxis(t, 1, axis + 1)


def setup_inputs(seed: int = 0) -> dict:
    inp = _fwd_setup_inputs(seed)
    key = _jax.random.fold_in(_jax.random.key(seed), 7919)
    shape, _ = _output_shape()
    out = dict(inp)
    out["loss_target"] = _jax.random.normal(_jax.random.fold_in(key, 0), shape, _jnp.float32)
    for i, name in enumerate(TWIN_WEIGHTS):
        w = inp[name].astype(_jnp.float32)
        if MOMENT_SCALE is None:
            s = _jnp.sqrt(_jnp.mean(_jnp.square(w)) + 1e-30)
        else:
            s = MOMENT_SCALE[name]
        km, kv = _jax.random.split(_jax.random.fold_in(key, i + 1))
        out[name] = w
        out["m_" + name] = s * _jax.random.normal(km, w.shape, _jnp.float32)
        out["v_" + name] = (s * s) * _jax.random.uniform(kv, w.shape, _jnp.float32, 0.5, 1.5)
    if N_MICROBATCH > 1:
        for name, axis in PER_EXAMPLE_BATCH_AXIS.items():
            out[name] = _to_microbatches(out[name], axis)
    return {'x': out['x'], 'w_in': out['w_in'], 'rel_bias': out['rel_bias'], 'ln_v_gain': out['ln_v_gain'], 'ln_v_bias': out['ln_v_bias'], 'w_spatial': out['w_spatial'], 'b_spatial': out['b_spatial'], 'w_proj_a': out['w_proj_a'], 'w_proj_b': out['w_proj_b'], 'w_out': out['w_out'], 'ln1_gain': out['ln1_gain'], 'ln1_bias': out['ln1_bias'], 'w_ff1': out['w_ff1'], 'b_ff1': out['b_ff1'], 'w_ff2': out['w_ff2'], 'b_ff2': out['b_ff2'], 'ln2_gain': out['ln2_gain'], 'ln2_bias': out['ln2_bias'], 'loss_target': out['loss_target'], 'm_w_in': out['m_w_in'], 'm_rel_bias': out['m_rel_bias'], 'm_ln_v_gain': out['m_ln_v_gain'], 'm_ln_v_bias': out['m_ln_v_bias'], 'm_w_spatial': out['m_w_spatial'], 'm_b_spatial': out['m_b_spatial'], 'm_w_proj_a': out['m_w_proj_a'], 'm_w_proj_b': out['m_w_proj_b'], 'm_w_out': out['m_w_out'], 'm_ln1_gain': out['m_ln1_gain'], 'm_ln1_bias': out['m_ln1_bias'], 'm_w_ff1': out['m_w_ff1'], 'm_b_ff1': out['m_b_ff1'], 'm_w_ff2': out['m_w_ff2'], 'm_b_ff2': out['m_b_ff2'], 'm_ln2_gain': out['m_ln2_gain'], 'm_ln2_bias': out['m_ln2_bias'], 'v_w_in': out['v_w_in'], 'v_rel_bias': out['v_rel_bias'], 'v_ln_v_gain': out['v_ln_v_gain'], 'v_ln_v_bias': out['v_ln_v_bias'], 'v_w_spatial': out['v_w_spatial'], 'v_b_spatial': out['v_b_spatial'], 'v_w_proj_a': out['v_w_proj_a'], 'v_w_proj_b': out['v_w_proj_b'], 'v_w_out': out['v_w_out'], 'v_ln1_gain': out['v_ln1_gain'], 'v_ln1_bias': out['v_ln1_bias'], 'v_w_ff1': out['v_w_ff1'], 'v_b_ff1': out['v_b_ff1'], 'v_w_ff2': out['v_w_ff2'], 'v_b_ff2': out['v_b_ff2'], 'v_ln2_gain': out['v_ln2_gain'], 'v_ln2_bias': out['v_ln2_bias']}


def _loss(weights, diff, rest, loss_target):
    with _jax.named_scope("forward"):
        args = {**rest, TWIN_DIFF_INPUT: diff, **{k: w.astype(_WEIGHT_DTYPES[k]) for k, w in weights.items()}}
        y = _forward(args)
    with _jax.named_scope("loss_head"):
        err = _jnp.square(y.astype(_jnp.float32) - loss_target)
        return 0.5 * _jnp.sum(_jnp.mean(err, axis=-1)) if err.ndim else 0.5 * err


def _adamw(w, g, m, v):
    m = ADAM_B1 * m + (1.0 - ADAM_B1) * g
    v = ADAM_B2 * v + (1.0 - ADAM_B2) * _jnp.square(g)
    m_hat = m / (1.0 - ADAM_B1 ** ADAM_STEP)
    v_hat = v / (1.0 - ADAM_B2 ** ADAM_STEP)
    delta = -ADAM_LR * (m_hat / (_jnp.sqrt(v_hat) + ADAM_EPS) + ADAM_WD * w)
    return delta, m, v


def reference(x, w_in, rel_bias, ln_v_gain, ln_v_bias, w_spatial, b_spatial, w_proj_a, w_proj_b, w_out, ln1_gain, ln1_bias, w_ff1, b_ff1, w_ff2, b_ff2, ln2_gain, ln2_bias, loss_target, m_w_in, m_rel_bias, m_ln_v_gain, m_ln_v_bias, m_w_spatial, m_b_spatial, m_w_proj_a, m_w_proj_b, m_w_out, m_ln1_gain, m_ln1_bias, m_w_ff1, m_b_ff1, m_w_ff2, m_b_ff2, m_ln2_gain, m_ln2_bias, v_w_in, v_rel_bias, v_ln_v_gain, v_ln_v_bias, v_w_spatial, v_b_spatial, v_w_proj_a, v_w_proj_b, v_w_out, v_ln1_gain, v_ln1_bias, v_w_ff1, v_b_ff1, v_w_ff2, v_b_ff2, v_ln2_gain, v_ln2_bias):
    given = dict(x=x, w_in=w_in, rel_bias=rel_bias, ln_v_gain=ln_v_gain, ln_v_bias=ln_v_bias, w_spatial=w_spatial, b_spatial=b_spatial, w_proj_a=w_proj_a, w_proj_b=w_proj_b, w_out=w_out, ln1_gain=ln1_gain, ln1_bias=ln1_bias, w_ff1=w_ff1, b_ff1=b_ff1, w_ff2=w_ff2, b_ff2=b_ff2, ln2_gain=ln2_gain, ln2_bias=ln2_bias, loss_target=loss_target, m_w_in=m_w_in, m_rel_bias=m_rel_bias, m_ln_v_gain=m_ln_v_gain, m_ln_v_bias=m_ln_v_bias, m_w_spatial=m_w_spatial, m_b_spatial=m_b_spatial, m_w_proj_a=m_w_proj_a, m_w_proj_b=m_w_proj_b, m_w_out=m_w_out, m_ln1_gain=m_ln1_gain, m_ln1_bias=m_ln1_bias, m_w_ff1=m_w_ff1, m_b_ff1=m_b_ff1, m_w_ff2=m_w_ff2, m_b_ff2=m_b_ff2, m_ln2_gain=m_ln2_gain, m_ln2_bias=m_ln2_bias, v_w_in=v_w_in, v_rel_bias=v_rel_bias, v_ln_v_gain=v_ln_v_gain, v_ln_v_bias=v_ln_v_bias, v_w_spatial=v_w_spatial, v_b_spatial=v_b_spatial, v_w_proj_a=v_w_proj_a, v_w_proj_b=v_w_proj_b, v_w_out=v_w_out, v_ln1_gain=v_ln1_gain, v_ln1_bias=v_ln1_bias, v_w_ff1=v_w_ff1, v_b_ff1=v_b_ff1, v_w_ff2=v_w_ff2, v_b_ff2=v_b_ff2, v_ln2_gain=v_ln2_gain, v_ln2_bias=v_ln2_bias)
    weights = {n: given[n] for n in TWIN_WEIGHTS}
    shared = {n: given[n] for n in SHARED_INPUTS}
    per_example = {n: given[n] for n in ['x']}
    grad_fn = _jax.value_and_grad(_loss, argnums=(0, 1))

    def one_microbatch(ex, loss_target):
        ex = dict(ex)
        diff = ex.pop(TWIN_DIFF_INPUT)
        return grad_fn(weights, diff, {**shared, **ex}, loss_target)

    if N_MICROBATCH == 1:
        loss, (grad_w, grad_x) = one_microbatch(per_example, given["loss_target"])
    else:
        def body(carry, xs):
            loss_sum, grad_sum = carry
            l_k, (gw_k, gx_k) = one_microbatch(xs[0], xs[1])
            with _jax.named_scope("update"):
                return (loss_sum + l_k, _jax.tree.map(_jnp.add, grad_sum, gw_k)), gx_k

        init = (_jnp.zeros((), _jnp.float32), _jax.tree.map(_jnp.zeros_like, weights))
        (loss, grad_w), grad_x = _jax.lax.scan(body, init, (per_example, given["loss_target"]))
    with _jax.named_scope("update"):
        delta_w, new_m, new_v = {}, {}, {}
        for n in TWIN_WEIGHTS:
            delta_w[n], new_m[n], new_v[n] = _adamw(weights[n], grad_w[n], given["m_" + n], given["v_" + n])
    return (loss, grad_x, *[grad_w[n] for n in TWIN_WEIGHTS], *[delta_w[n] for n in TWIN_WEIGHTS],
            *[new_m[n] for n in TWIN_WEIGHTS], *[new_v[n] for n in TWIN_WEIGHTS])
```

```python
import functools
import math

import numpy as np
import jax
import jax.numpy as jnp
from jax import lax
from jax.experimental import pallas as pl
from jax.experimental.pallas import tpu as pltpu

F32 = jnp.float32
BF16 = jnp.bfloat16

S = 2048
D = 2048
DA = 1024
DB = 1024
DFF = 8192
DIN = 9216
NH = 8
HD = 128
NBLK = 16
PATTERNS = ((128, 1), (512, 4), (2048, 16))
N_BUCKETS = 32
MAX_DISTANCE = 2048
ALPHA = 2.0 ** 0.25
LN_EPS = 1e-5
NEG_INF = -1e30
SCALE = HD ** -0.5
N_CHIPS = 4

ADAM_LR = 0.001
ADAM_B1 = 0.9
ADAM_B2 = 0.999
ADAM_EPS = 1e-08
ADAM_WD = 0.01
ADAM_STEP = 10

VMEM_LIMIT = 56 * 1024 * 1024
MESH = pl.DeviceIdType.MESH
ANY = pl.BlockSpec(memory_space=pl.ANY)


def _params(n_axes, vmem=VMEM_LIMIT):
    return pltpu.CompilerParams(dimension_semantics=("arbitrary",) * n_axes, vmem_limit_bytes=vmem)


def _bucket_tile(dilation):
    qi = np.arange(128)[:, None]
    kj = np.arange(256)[None, :]
    n = np.clip(128 + qi - kj, 0, 128) * dilation
    max_exact = N_BUCKETS // 2
    nf = np.maximum(n, 1).astype(np.float32)
    large = max_exact + (np.log(nf / np.float32(max_exact)) / np.float32(math.log(MAX_DISTANCE / max_exact))
                         * np.float32(N_BUCKETS - max_exact)).astype(np.int32)
    large = np.minimum(large, N_BUCKETS - 1)
    return np.where(n < max_exact, n, large).astype(np.int32)


def _gelu(x):
    c = math.sqrt(2.0 / math.pi)
    t = jnp.tanh(c * (x + 0.044715 * x * x * x))
    return 0.5 * x * (1.0 + t), t


def _gelu_grad(x, t):
    c = math.sqrt(2.0 / math.pi)
    return 0.5 * (1.0 + t) + 0.5 * x * (1.0 - t * t) * c * (1.0 + 3.0 * 0.044715 * x * x)


def _sigmoid(x):
    return 1.0 / (1.0 + jnp.exp(-x))


def _dot(a, b):
    return jnp.dot(a, b, preferred_element_type=F32)


def _dot_nt(a, b):
    return lax.dot_general(a, b, (((1,), (1,)), ((), ())), preferred_element_type=F32)


def _proj(xb, win_g):
    tn = 768
    per = 2304 // tn

    def body(x_ref, w_ref, o_ref):
        o_ref[...] = _dot(x_ref[...], w_ref[...])

    return pl.pallas_call(
        body, name="proj", grid=(DIN // tn,),
        in_specs=[pl.BlockSpec((S, D), lambda j: (0, 0)),
                  pl.BlockSpec((None, D, tn), lambda j: (j // per, 0, j % per))],
        out_specs=pl.BlockSpec((S, tn), lambda j: (0, j)),
        out_shape=jax.ShapeDtypeStruct((S, DIN), F32),
        compiler_params=_params(1),
    )(xb, win_g)


def _bias_tiles(rb_ref, bucket, bias_scr):
    qi = lax.broadcasted_iota(jnp.int32, (128, 256), 0)
    kj = lax.broadcasted_iota(jnp.int32, (128, 256), 1)
    steps = 128 + qi - kj
    band = (steps >= 0) & (steps <= 128)
    bias_scr[...] = jnp.zeros_like(bias_scr)

    def one_bucket(t, carry):
        hit = bucket == t
        for h in range(NH):
            bias_scr[h] = jnp.where(hit, rb_ref[t, h], bias_scr[h])
        return carry

    lax.fori_loop(0, N_BUCKETS, one_bucket, 0)
    for h in range(NH):
        bias_scr[h] = jnp.where(band, bias_scr[h], NEG_INF)


def _attn_fwd(qkv, rel_bias, dilation, name):
    nblk = NBLK // dilation
    bucket = jnp.asarray(_bucket_tile(dilation))

    def body(rb_ref, bucket_ref, q_ref, kc_ref, kp_ref, vc_ref, vp_ref, o_ref, lse_ref, bias_scr):
        b = pl.program_id(0)

        @pl.when(b == 0)
        def _():
            _bias_tiles(rb_ref, bucket_ref[...], bias_scr)

        has_prev = (b % nblk) != 0
        kj = lax.broadcasted_iota(jnp.int32, (128, 256), 1)
        key_ok = (kj >= 128) | has_prev
        for h in range(NH):
            cols = slice(h * HD, (h + 1) * HD)
            q = q_ref[:, cols].astype(BF16)
            s = jnp.concatenate([_dot_nt(q, kp_ref[:, cols].astype(BF16)),
                                 _dot_nt(q, kc_ref[:, cols].astype(BF16))], axis=1) * SCALE
            s = jnp.where(key_ok, s + bias_scr[h], NEG_INF)
            m = jnp.max(s, axis=1, keepdims=True)
            p = jnp.exp(s - m)
            den = jnp.sum(p, axis=1, keepdims=True)
            pb = p.astype(BF16)
            o = _dot(pb[:, :128], vp_ref[:, cols].astype(BF16)) + _dot(pb[:, 128:], vc_ref[:, cols].astype(BF16))
            o_ref[:, cols] = o / den
            lse_ref[:, cols] = jnp.broadcast_to(m + jnp.log(den), (128, HD))

    blk = lambda col, prev: pl.BlockSpec(
        (128, DA), (lambda b: (jnp.maximum(b - 1, 0), col)) if prev else (lambda b: (b, col)))
    return pl.pallas_call(
        body, name=name, grid=(NBLK,),
        in_specs=[pl.BlockSpec(memory_space=pltpu.SMEM),
                  pl.BlockSpec((128, 256), lambda b: (0, 0)),
                  blk(0, False), blk(1, False), blk(1, True), blk(2, False), blk(2, True)],
        out_specs=[pl.BlockSpec((128, DA), lambda b: (b, 0)), pl.BlockSpec((128, DA), lambda b: (b, 0))],
        out_shape=[jax.ShapeDtypeStruct((S, DA), F32), jax.ShapeDtypeStruct((S, DA), F32)],
        scratch_shapes=[pltpu.VMEM((NH, 128, 256), F32)],
        compiler_params=_params(1),
    )(rel_bias, bucket, qkv, qkv, qkv, qkv, qkv)


def _attn_combine(outs, lses):
    tm = 256

    def body(o1, o2, o3, l1, l2, l3, attn_ref, lse_ref):
        a, b, c = l1[...], l2[...], l3[...]
        m = jnp.maximum(jnp.maximum(a, b), c)
        wa, wb, wc = jnp.exp(a - m), jnp.exp(b - m), jnp.exp(c - m)
        den = wa + wb + wc
        attn_ref[...] = ((wa * o1[...] + wb * o2[...] + wc * o3[...]) / den).astype(BF16)
        lse_ref[...] = m + jnp.log(den)

    spec = pl.BlockSpec((tm, DA), lambda i: (i, 0))
    return pl.pallas_call(
        body, name="attn_combine", grid=(S // tm,),
        in_specs=[spec] * 6, out_specs=[spec, spec],
        out_shape=[jax.ShapeDtypeStruct((S, DA), BF16), jax.ShapeDtypeStruct((S, DA), F32)],
        compiler_params=_params(1),
    )(*outs, *lses)


def _gmlp_parts(u_ref, vb_ref, g_ref, be_ref):
    u = u_ref[...]
    u_act, tu = _gelu(u)
    vb = vb_ref[...]
    gv, tv = _gelu(vb)
    mean = jnp.mean(gv, axis=1, keepdims=True)
    cen = gv - mean
    var = jnp.mean(cen * cen, axis=1, keepdims=True)
    rstd = lax.rsqrt(var + LN_EPS)
    xhat = cen * rstd
    vn = xhat * g_ref[...] + be_ref[...]
    return u, tu, u_act, vb, tv, rstd, xhat, vn


def _gmlp_fwd(proj, ws, bsp_b, gain_v, bias_v):
    def body(u_ref, vb_ref, ws_ref, bsp_ref, g_ref, be_ref, o_ref):
        _, _, u_act, _, _, _, _, vn = _gmlp_parts(u_ref, vb_ref, g_ref, be_ref)
        row = lax.broadcasted_iota(jnp.int32, (128, 128), 0)
        col = lax.broadcasted_iota(jnp.int32, (128, 128), 1)
        causal = row >= col
        for g in range(NH):
            cols = slice(g * 128, (g + 1) * 128)
            wsg = jnp.where(causal, ws_ref[g], 0.0).astype(BF16)
            z = _dot(wsg, vn[:, cols].astype(BF16)) + bsp_ref[g]
            o_ref[:, cols] = (u_act[:, cols] * z).astype(BF16)

    return pl.pallas_call(
        body, name="gmlp_fwd", grid=(NBLK,),
        in_specs=[pl.BlockSpec((128, DB), lambda c: (c, 3)), pl.BlockSpec((128, DB), lambda c: (c, 4)),
                  pl.BlockSpec((NH, 128, 128), lambda c: (0, 0, 0)), pl.BlockSpec((NH, 128, 128), lambda c: (0, 0, 0)),
                  pl.BlockSpec((1, DB), lambda c: (0, 0)), pl.BlockSpec((1, DB), lambda c: (0, 0))],
        out_specs=pl.BlockSpec((128, DB), lambda c: (c, 0)),
        out_shape=jax.ShapeDtypeStruct((S, DB), BF16),
        compiler_params=_params(1),
    )(proj, proj, ws, bsp_b, gain_v, bias_v)


def _branch(attn, gmlp, wpa_g, wpb_g, proj):
    tn = 512

    def body(a_ref, g_ref, wa_ref, wb_ref, ga_ref, gb_ref, ya_ref, yb_ref, mg_ref):
        ya = _dot(a_ref[...], wa_ref[...])
        yb = _dot(g_ref[...], wb_ref[...])
        ya_ref[...] = ya.astype(BF16)
        yb_ref[...] = yb.astype(BF16)
        mg_ref[...] = (_sigmoid(ga_ref[...]) * ya + _sigmoid(gb_ref[...]) * yb).astype(BF16)

    out = pl.BlockSpec((S, tn), lambda j: (0, j))
    return pl.pallas_call(
        body, name="branch", grid=(D // tn,),
        in_specs=[pl.BlockSpec((S, DA), lambda j: (0, 0)), pl.BlockSpec((S, DB), lambda j: (0, 0)),
                  pl.BlockSpec((None, DA, tn), lambda j: (j, 0, 0)), pl.BlockSpec((None, DB, tn), lambda j: (j, 0, 0)),
                  pl.BlockSpec((S, tn), lambda j: (0, 5120 // tn + j)), pl.BlockSpec((S, tn), lambda j: (0, 7168 // tn + j))],
        out_specs=[out, out, out],
        out_shape=[jax.ShapeDtypeStruct((S, D), BF16)] * 3,
        compiler_params=_params(1),
    )(attn, gmlp, wpa_g, wpb_g, proj, proj)


def _out_ln1(merged, wout_g, x, gain, bias):
    tm = 256

    def body(m_ref, w_ref, x_ref, g_ref, b_ref, xh_ref, rs_ref, h_ref):
        pre = ALPHA * x_ref[...] + _dot(m_ref[...], w_ref[...])
        mean = jnp.mean(pre, axis=1, keepdims=True)
        cen = pre - mean
        var = jnp.mean(cen * cen, axis=1, keepdims=True)
        rstd = lax.rsqrt(var + LN_EPS)
        xhat = cen * rstd
        xh_ref[...] = xhat
        rs_ref[...] = jnp.broadcast_to(rstd, (tm, 128))
        h_ref[...] = (xhat * g_ref[...] + b_ref[...]).astype(BF16)

    row = pl.BlockSpec((tm, D), lambda i: (i, 0))
    vec = pl.BlockSpec((1, D), lambda i: (0, 0))
    return pl.pallas_call(
        body, name="out_ln1", grid=(S // tm,),
        in_specs=[row, pl.BlockSpec((D, D), lambda i: (0, 0)), row, vec, vec],
        out_specs=[row, pl.BlockSpec((tm, 128), lambda i: (i, 0)), row],
        out_shape=[jax.ShapeDtypeStruct((S, D), F32), jax.ShapeDtypeStruct((S, 128), F32),
                   jax.ShapeDtypeStruct((S, D), BF16)],
        compiler_params=_params(1),
    )(merged, wout_g, x, gain, bias)


def _ff1(h1b, w1_g, b1):
    tn = 512
    per = D // tn

    def body(h_ref, w_ref, b_ref, a_ref, r_ref):
        r = jnp.maximum(_dot(h_ref[...], w_ref[...]) + b_ref[...], 0.0)
        r_ref[...] = r.astype(BF16)
        a_ref[...] = (r * r).astype(BF16)

    out = pl.BlockSpec((S, tn), lambda j: (0, j))
    return pl.pallas_call(
        body, name="ff1", grid=(DFF // tn,),
        in_specs=[pl.BlockSpec((S, D), lambda j: (0, 0)),
                  pl.BlockSpec((None, D, tn), lambda j: (j // per, 0, j % per)),
                  pl.BlockSpec((1, tn), lambda j: (0, j))],
        out_specs=[out, out],
        out_shape=[jax.ShapeDtypeStruct((S, DFF), BF16)] * 2,
        compiler_params=_params(1),
    )(h1b, w1_g, b1)


def _ff2_ln2_loss(a, w2_g, xhat1, g1, b1, b2, g2, be2, target):
    tm, tk = 256, 1024
    nk = DFF // tk

    def body(a_ref, w_ref, xh_ref, g1_ref, b1_ref, b2_ref, g2_ref, be2_ref, t_ref, d_ref, db_ref, st_ref, acc):
        i, k = pl.program_id(0), pl.program_id(1)

        @pl.when(k == 0)
        def _():
            acc[...] = jnp.zeros_like(acc)

        @pl.when((i == 0) & (k == 0))
        def _():
            st_ref[...] = jnp.zeros_like(st_ref)

        acc[...] += _dot(a_ref[...], w_ref[...])

        @pl.when(k == nk - 1)
        def _():
            h1 = xh_ref[...] * g1_ref[...] + b1_ref[...]
            pre = ALPHA * h1 + acc[...] + b2_ref[...]
            mean = jnp.mean(pre, axis=1, keepdims=True)
            cen = pre - mean
            var = jnp.mean(cen * cen, axis=1, keepdims=True)
            rstd = lax.rsqrt(var + LN_EPS)
            xhat = cen * rstd
            y = xhat * g2_ref[...] + be2_ref[...]
            err = y - t_ref[...]
            dy = err * (1.0 / D)
            g = dy * g2_ref[...]
            dpre = rstd * (g - jnp.mean(g, axis=1, keepdims=True)
                           - xhat * jnp.mean(g * xhat, axis=1, keepdims=True))
            d_ref[...] = dpre
            db_ref[...] = dpre.astype(BF16)
            st_ref[0:1, :] += jnp.sum(dy * xhat, axis=0, keepdims=True)
            st_ref[1:2, :] += jnp.sum(dy, axis=0, keepdims=True)
            st_ref[2:3, :] += jnp.sum(dpre, axis=0, keepdims=True)
            st_ref[3:4, :] += jnp.broadcast_to(jnp.sum(err * err).reshape(1, 1), (1, D))

    row = pl.BlockSpec((tm, D), lambda i, k: (i, 0))
    vec = pl.BlockSpec((1, D), lambda i, k: (0, 0))
    return pl.pallas_call(
        body, name="ff2_ln2_loss", grid=(S // tm, nk),
        in_specs=[pl.BlockSpec((tm, tk), lambda i, k: (i, k)), pl.BlockSpec((tk, D), lambda i, k: (k, 0)),
                  row, vec, vec, vec, vec, vec, row],
        out_specs=[row, row, pl.BlockSpec((8, D), lambda i, k: (0, 0))],
        out_shape=[jax.ShapeDtypeStruct((S, D), F32), jax.ShapeDtypeStruct((S, D), BF16),
                   jax.ShapeDtypeStruct((8, D), F32)],
        scratch_shapes=[pltpu.VMEM((tm, D), F32)],
        compiler_params=_params(2),
    )(a, w2_g, xhat1, g1, b1, b2, g2, be2, target)


def _grad_w(act, dout, name, ti, tj, sharded):
    m, n = act.shape[1], dout.shape[1]
    ns = n // N_CHIPS
    per = ns // tj if sharded else None

    def body(a_ref, b_ref, o_ref, at_scr):
        @pl.when(pl.program_id(1) == 0)
        def _():
            at_scr[...] = a_ref[...].astype(F32).T.astype(BF16)

        o_ref[...] = _dot(at_scr[...], b_ref[...])

    if sharded:
        out_spec = pl.BlockSpec((None, ti, tj), lambda i, j: (j // per, i, j % per))
        out_shape = jax.ShapeDtypeStruct((N_CHIPS, m, ns), F32)
    else:
        out_spec = pl.BlockSpec((ti, tj), lambda i, j: (i, j))
        out_shape = jax.ShapeDtypeStruct((m, n), F32)
    return pl.pallas_call(
        body, name=name, grid=(m // ti, n // tj),
        in_specs=[pl.BlockSpec((S, ti), lambda i, j: (0, i)), pl.BlockSpec((S, tj), lambda i, j: (0, j))],
        out_specs=out_spec, out_shape=out_shape,
        scratch_shapes=[pltpu.VMEM((ti, S), BF16)],
        compiler_params=_params(2),
    )(act, dout)


def _d_ff1(dpre2b, w2_g, r):
    tn = 512

    def body(d_ref, w_ref, r_ref, o_ref, gb_ref):
        da = _dot_nt(d_ref[...], w_ref[...])
        dp = da * (2.0 * r_ref[...].astype(F32))
        o_ref[...] = dp.astype(BF16)
        gb_ref[...] = jnp.sum(dp, axis=0, keepdims=True)

    return pl.pallas_call(
        body, name="d_ff1", grid=(DFF // tn,),
        in_specs=[pl.BlockSpec((S, D), lambda j: (0, 0)), pl.BlockSpec((tn, D), lambda j: (j, 0)),
                  pl.BlockSpec((S, tn), lambda j: (0, j))],
        out_specs=[pl.BlockSpec((S, tn), lambda j: (0, j)), pl.BlockSpec((1, tn), lambda j: (0, j))],
        out_shape=[jax.ShapeDtypeStruct((S, DFF), BF16), jax.ShapeDtypeStruct((1, DFF), F32)],
        compiler_params=_params(1),
    )(dpre2b, w2_g, r)


def _d_h1_ln1(dprea, w1_g, dpre2, xhat1, rstd1, g1):
    tm, tk = 256, 512
    per = D // tk
    nk = DFF // tk

    def body(a_ref, w_ref, d2_ref, xh_ref, rs_ref, g_ref, d_ref, db_ref, st_ref, acc):
        i, k = pl.program_id(0), pl.program_id(1)

        @pl.when(k == 0)
        def _():
            acc[...] = jnp.zeros_like(acc)

        @pl.when((i == 0) & (k == 0))
        def _():
            st_ref[...] = jnp.zeros_like(st_ref)

        acc[...] += _dot_nt(a_ref[...], w_ref[...])

        @pl.when(k == nk - 1)
        def _():
            dh = ALPHA * d2_ref[...] + acc[...]
            xhat = xh_ref[...]
            g = dh * g_ref[...]
            dpre = rs_ref[:, 0:1] * (g - jnp.mean(g, axis=1, keepdims=True)
                                     - xhat * jnp.mean(g * xhat, axis=1, keepdims=True))
            d_ref[...] = dpre
            db_ref[...] = dpre.astype(BF16)
            st_ref[0:1, :] += jnp.sum(dh * xhat, axis=0, keepdims=True)
            st_ref[1:2, :] += jnp.sum(dh, axis=0, keepdims=True)

    row = pl.BlockSpec((tm, D), lambda i, k: (i, 0))
    return pl.pallas_call(
        body, name="d_h1_ln1", grid=(S // tm, nk),
        in_specs=[pl.BlockSpec((tm, tk), lambda i, k: (i, k)),
                  pl.BlockSpec((None, D, tk), lambda i, k: (k // per, 0, k % per)),
                  row, row, pl.BlockSpec((tm, 128), lambda i, k: (i, 0)), pl.BlockSpec((1, D), lambda i, k: (0, 0))],
        out_specs=[row, row, pl.BlockSpec((8, D), lambda i, k: (0, 0))],
        out_shape=[jax.ShapeDtypeStruct((S, D), F32), jax.ShapeDtypeStruct((S, D), BF16),
                   jax.ShapeDtypeStruct((8, D), F32)],
        scratch_shapes=[pltpu.VMEM((tm, D), F32)],
        compiler_params=_params(2),
    )(dprea, w1_g, dpre2, xhat1, rstd1, g1)


def _d_merged(dpre1b, wout_g, proj, ya, yb):
    tm, tn = 512, 1024

    def body(d_ref, w_ref, ga_ref, gb_ref, ya_ref, yb_ref, dya_ref, dyb_ref, dga_ref, dgb_ref):
        dm = _dot_nt(d_ref[...], w_ref[...])
        sa = _sigmoid(ga_ref[...])
        sb = _sigmoid(gb_ref[...])
        dya_ref[...] = (dm * sa).astype(BF16)
        dyb_ref[...] = (dm * sb).astype(BF16)
        dga_ref[...] = (dm * ya_ref[...].astype(F32) * sa * (1.0 - sa)).astype(BF16)
        dgb_ref[...] = (dm * yb_ref[...].astype(F32) * sb * (1.0 - sb)).astype(BF16)

    tile = pl.BlockSpec((tm, tn), lambda i, j: (i, j))
    return pl.pallas_call(
        body, name="d_merged", grid=(S // tm, D // tn),
        in_specs=[pl.BlockSpec((tm, D), lambda i, j: (i, 0)), pl.BlockSpec((tn, D), lambda i, j: (j, 0)),
                  pl.BlockSpec((tm, tn), lambda i, j: (i, 5 + j)), pl.BlockSpec((tm, tn), lambda i, j: (i, 7 + j)),
                  tile, tile],
        out_specs=[tile] * 4,
        out_shape=[jax.ShapeDtypeStruct((S, D), BF16)] * 4,
        compiler_params=_params(2),
    )(dpre1b, wout_g, proj, proj, ya, yb)


def _d_branches(dya, dyb, wpa_g, wpb_g):
    tk = 512

    def body(da_ref, db_ref, wa_ref, wb_ref, oa_ref, ob_ref):
        @pl.when(pl.program_id(0) == 0)
        def _():
            oa_ref[...] = jnp.zeros_like(oa_ref)
            ob_ref[...] = jnp.zeros_like(ob_ref)

        oa_ref[...] += _dot_nt(da_ref[...], wa_ref[...])
        ob_ref[...] += _dot_nt(db_ref[...], wb_ref[...])

    return pl.pallas_call(
        body, name="d_branches", grid=(D // tk,),
        in_specs=[pl.BlockSpec((S, tk), lambda k: (0, k)), pl.BlockSpec((S, tk), lambda k: (0, k)),
                  pl.BlockSpec((None, DA, tk), lambda k: (k, 0, 0)), pl.BlockSpec((None, DB, tk), lambda k: (k, 0, 0))],
        out_specs=[pl.BlockSpec((S, DA), lambda k: (0, 0)), pl.BlockSpec((S, DB), lambda k: (0, 0))],
        out_shape=[jax.ShapeDtypeStruct((S, DA), F32), jax.ShapeDtypeStruct((S, DB), F32)],
        compiler_params=_params(1),
    )(dya, dyb, wpa_g, wpb_g)


def _gmlp_bwd(proj, dgmlp, ws, ws_t, bsp_b, gain_v, bias_v):
    def body(u_ref, vb_ref, dg_ref, ws_ref, wst_ref, bsp_ref, g_ref, be_ref, duv_ref, gws_ref, gbs_ref, st_ref):
        @pl.when(pl.program_id(0) == 0)
        def _():
            gws_ref[...] = jnp.zeros_like(gws_ref)
            gbs_ref[...] = jnp.zeros_like(gbs_ref)
            st_ref[...] = jnp.zeros_like(st_ref)

        u, tu, u_act, vb, tv, rstd, xhat, vn = _gmlp_parts(u_ref, vb_ref, g_ref, be_ref)
        dg = dg_ref[...]
        dz = dg * u_act
        row = lax.broadcasted_iota(jnp.int32, (128, 128), 0)
        col = lax.broadcasted_iota(jnp.int32, (128, 128), 1)
        causal = row >= col
        causal_t = row <= col
        dvn_parts = []
        z_parts = []
        for g in range(NH):
            cols = slice(g * 128, (g + 1) * 128)
            vng = vn[:, cols].astype(BF16)
            dzg = dz[:, cols]
            dzb = dzg.astype(BF16)
            wsg = jnp.where(causal, ws_ref[g], 0.0).astype(BF16)
            wsg_t = jnp.where(causal_t, wst_ref[g], 0.0).astype(BF16)
            z_parts.append(_dot(wsg, vng) + bsp_ref[g])
            gws_ref[g] += jnp.where(causal, _dot_nt(dzb, vng), 0.0)
            gbs_ref[g] += jnp.broadcast_to(jnp.sum(dzg, axis=1, keepdims=True), (128, 128))
            dvn_parts.append(_dot(wsg_t, dzb))
        z = jnp.concatenate(z_parts, axis=1)
        dvn = jnp.concatenate(dvn_parts, axis=1)
        du = dg * z * _gelu_grad(u, tu)
        st_ref[0:1, :] += jnp.sum(dvn * xhat, axis=0, keepdims=True)
        st_ref[1:2, :] += jnp.sum(dvn, axis=0, keepdims=True)
        gg = dvn * g_ref[...]
        dgv = rstd * (gg - jnp.mean(gg, axis=1, keepdims=True) - xhat * jnp.mean(gg * xhat, axis=1, keepdims=True))
        dvb = dgv * _gelu_grad(vb, tv)
        duv_ref[:, 0:DB] = du.astype(BF16)
        duv_ref[:, DB:2 * DB] = dvb.astype(BF16)

    full3 = pl.BlockSpec((NH, 128, 128), lambda c: (0, 0, 0))
    vec = pl.BlockSpec((1, DB), lambda c: (0, 0))
    return pl.pallas_call(
        body, name="gmlp_bwd", grid=(NBLK,),
        in_specs=[pl.BlockSpec((128, DB), lambda c: (c, 3)), pl.BlockSpec((128, DB), lambda c: (c, 4)),
                  pl.BlockSpec((128, DB), lambda c: (c, 0)), full3, full3, full3, vec, vec],
        out_specs=[pl.BlockSpec((128, 2 * DB), lambda c: (c, 0)), full3, full3, pl.BlockSpec((8, DB), lambda c: (0, 0))],
        out_shape=[jax.ShapeDtypeStruct((S, 2 * DB), BF16), jax.ShapeDtypeStruct((NH, 128, 128), F32),
                   jax.ShapeDtypeStruct((NH, 128, 128), F32), jax.ShapeDtypeStruct((8, DB), F32)],
        compiler_params=_params(1),
    )(proj, proj, dgmlp, ws, ws_t, bsp_b, gain_v, bias_v)


def _attn_delta(dattn, attn):
    tm = 256

    def body(d_ref, o_ref, dl_ref, db_ref):
        d = d_ref[...]
        prod = d * o_ref[...].astype(F32)
        for h in range(NH):
            cols = slice(h * HD, (h + 1) * HD)
            dl_ref[:, cols] = jnp.broadcast_to(jnp.sum(prod[:, cols], axis=1, keepdims=True), (tm, HD))
        db_ref[...] = d.astype(BF16)

    spec = pl.BlockSpec((tm, DA), lambda i: (i, 0))
    return pl.pallas_call(
        body, name="attn_delta", grid=(S // tm,),
        in_specs=[spec, spec], out_specs=[spec, spec],
        out_shape=[jax.ShapeDtypeStruct((S, DA), F32), jax.ShapeDtypeStruct((S, DA), BF16)],
        compiler_params=_params(1),
    )(dattn, attn)


def _attn_bwd(qkv, dob, lse, delta, rel_bias, dilation, name):
    nblk = NBLK // dilation
    bucket = jnp.asarray(_bucket_tile(dilation))

    def body(rb_ref, bucket_ref, q_ref, qn_ref, kc_ref, kp_ref, vc_ref, vp_ref, do_ref, don_ref,
             l_ref, ln_ref, dl_ref, dln_ref, dq_ref, dk_ref, dv_ref, ds_ref, bias_scr):
        b = pl.program_id(0)

        @pl.when(b == 0)
        def _():
            _bias_tiles(rb_ref, bucket_ref[...], bias_scr)
            ds_ref[...] = jnp.zeros_like(ds_ref)

        has_prev = (b % nblk) != 0
        has_next = ((b + 1) % nblk) != 0
        for h in range(NH):
            cols = slice(h * HD, (h + 1) * HD)
            q = q_ref[:, cols].astype(BF16)
            kc = kc_ref[:, cols].astype(BF16)
            kp = kp_ref[:, cols].astype(BF16)
            vc = vc_ref[:, cols].astype(BF16)
            vp = vp_ref[:, cols].astype(BF16)
            do = do_ref[:, cols]
            bias_p = bias_scr[h, :, 0:128]
            bias_c = bias_scr[h, :, 128:256]
            lse_b = l_ref[:, cols]
            dl_b = dl_ref[:, cols]
            p_c = jnp.exp(_dot_nt(q, kc) * SCALE + bias_c - lse_b)
            p_p = jnp.where(has_prev, jnp.exp(_dot_nt(q, kp) * SCALE + bias_p - lse_b), 0.0)
            ds_c = p_c * (_dot_nt(do, vc) - dl_b)
            ds_p = p_p * (_dot_nt(do, vp) - dl_b)
            ds_ref[h, :, 0:128] += ds_p
            ds_ref[h, :, 128:256] += ds_c
            ds_cb = ds_c.astype(BF16)
            dq_ref[:, cols] = (_dot(ds_cb, kc) + _dot(ds_p.astype(BF16), kp)) * SCALE
            qn = qn_ref[:, cols].astype(BF16)
            don = don_ref[:, cols]
            p_n = jnp.where(has_next, jnp.exp(_dot_nt(qn, kc) * SCALE + bias_p - ln_ref[:, cols]), 0.0)
            ds_n = p_n * (_dot_nt(don, vc) - dln_ref[:, cols])
            dk_ref[:, cols] = (_dot(ds_c.T.astype(BF16), q) + _dot(ds_n.T.astype(BF16), qn)) * SCALE
            dv_ref[:, cols] = _dot(p_c.T.astype(BF16), do) + _dot(p_n.T.astype(BF16), don)

    cur = lambda col: pl.BlockSpec((128, DA), lambda b: (b, col))
    prev = lambda col: pl.BlockSpec((128, DA), lambda b: (jnp.maximum(b - 1, 0), col))
    nxt = lambda col: pl.BlockSpec((128, DA), lambda b: (jnp.minimum(b + 1, NBLK - 1), col))
    return pl.pallas_call(
        body, name=name, grid=(NBLK,),
        in_specs=[pl.BlockSpec(memory_space=pltpu.SMEM), pl.BlockSpec((128, 256), lambda b: (0, 0)),
                  cur(0), nxt(0), cur(1), prev(1), cur(2), prev(2), cur(0), nxt(0), cur(0), nxt(0), cur(0), nxt(0)],
        out_specs=[cur(0), cur(0), cur(0), pl.BlockSpec((NH, 128, 256), lambda b: (0, 0, 0))],
        out_shape=[jax.ShapeDtypeStruct((S, DA), F32)] * 3 + [jax.ShapeDtypeStruct((NH, 128, 256), F32)],
        scratch_shapes=[pltpu.VMEM((NH, 128, 256), F32)],
        compiler_params=_params(1),
    )(rel_bias, bucket, qkv, qkv, qkv, qkv, qkv, qkv, dob, dob, lse, lse, delta, delta)


def _sum3_bf16(a, b, c):
    tm = 256
    n = a.shape[1]

    def body(a_ref, b_ref, c_ref, o_ref):
        o_ref[...] = (a_ref[...] + b_ref[...] + c_ref[...]).astype(BF16)

    spec = pl.BlockSpec((tm, n), lambda i: (i, 0))
    return pl.pallas_call(
        body, name="dqkv_sum", grid=(S // tm,), in_specs=[spec] * 3, out_specs=spec,
        out_shape=jax.ShapeDtypeStruct((S, n), BF16), compiler_params=_params(1),
    )(a, b, c)


def _rel_bias_grad(ds_sums):
    buckets = jnp.asarray(np.stack([_bucket_tile(d) for _, d in PATTERNS]))

    def body(bk_ref, d1, d2, d3, o_ref):
        row = lax.broadcasted_iota(jnp.int32, (N_BUCKETS, 128), 0)
        lane = lax.broadcasted_iota(jnp.int32, (N_BUCKETS, 128), 1)

        def one_bucket(t, out):
            hits = [bk_ref[p] == t for p in range(3)]
            for h in range(NH):
                tot = jnp.zeros((128, 256), F32)
                for p, d in enumerate((d1, d2, d3)):
                    tot = tot + jnp.where(hits[p], d[h], 0.0)
                out = jnp.where((row == t) & (lane == h), jnp.sum(tot), out)
            return out

        o_ref[...] = lax.fori_loop(0, N_BUCKETS, one_bucket, jnp.zeros((N_BUCKETS, 128), F32))

    return pl.pallas_call(
        body, name="rel_bias_grad",
        in_specs=[pl.BlockSpec(memory_space=pltpu.VMEM)] * 4, out_specs=pl.BlockSpec(memory_space=pltpu.VMEM),
        out_shape=jax.ShapeDtypeStruct((N_BUCKETS, 128), F32),
        compiler_params=pltpu.CompilerParams(vmem_limit_bytes=VMEM_LIMIT),
    )(buckets, *ds_sums)


def _d_x(dproj, win_g, dpre1):
    tm, tk = 512, 768
    per = 2304 // tk
    nk = DIN // tk

    def body(a_ref, w_ref, d_ref, o_ref, acc):
        k = pl.program_id(1)

        @pl.when(k == 0)
        def _():
            acc[...] = ALPHA * d_ref[...]

        acc[...] += _dot_nt(a_ref[...], w_ref[...])

        @pl.when(k == nk - 1)
        def _():
            o_ref[...] = acc[...]

    row = pl.BlockSpec((tm, D), lambda i, k: (i, 0))
    return pl.pallas_call(
        body, name="d_x", grid=(S // tm, nk),
        in_specs=[pl.BlockSpec((tm, tk), lambda i, k: (i, k)),
                  pl.BlockSpec((None, D, tk), lambda i, k: (k // per, 0, k % per)), row],
        out_specs=row, out_shape=jax.ShapeDtypeStruct((S, D), F32),
        scratch_shapes=[pltpu.VMEM((tm, D), F32)],
        compiler_params=_params(2),
    )(dproj, win_g, dpre1)


def _adamw(w, g, m, v, name):
    rows, cols = w.shape
    tm = max(t for t in range(8, 257, 8) if rows % t == 0)

    def body(w_ref, g_ref, m_ref, v_ref, d_ref, nm_ref, nv_ref):
        g = g_ref[...]
        m = ADAM_B1 * m_ref[...] + (1.0 - ADAM_B1) * g
        v = ADAM_B2 * v_ref[...] + (1.0 - ADAM_B2) * (g * g)
        m_hat = m / (1.0 - ADAM_B1 ** ADAM_STEP)
        v_hat = v / (1.0 - ADAM_B2 ** ADAM_STEP)
        d_ref[...] = -ADAM_LR * (m_hat / (jnp.sqrt(v_hat) + ADAM_EPS) + ADAM_WD * w_ref[...])
        nm_ref[...] = m
        nv_ref[...] = v

    spec = pl.BlockSpec((tm, cols), lambda i: (i, 0))
    return pl.pallas_call(
        body, name=name, grid=(rows // tm,), in_specs=[spec] * 4, out_specs=[spec] * 3,
        out_shape=[jax.ShapeDtypeStruct((rows, cols), F32)] * 3, compiler_params=_params(1),
    )(w, g, m, v)


def _position():
    x, y, c = lax.axis_index("x"), lax.axis_index("y"), lax.axis_index("c")
    chips = [(1 - x, y), (x, 1 - y), (1 - x, 1 - y)]
    return x, y, c, chips


def _remote(src, dst, send_sems, recv_sems, k, to):
    return pltpu.make_async_remote_copy(src_ref=src, dst_ref=dst, send_sem=send_sems.at[k], recv_sem=recv_sems.at[k],
                                        device_id=to, device_id_type=MESH)


def _allgather_weights(shards):
    n = len(shards)

    def body(*refs):
        ins, outs = refs[:n], refs[n:2 * n]
        send_sems, recv_sems, local_sems = refs[2 * n:]
        x, y, c, chips = _position()
        me, sibling = (x, y, c), (x, y, 1 - c)
        my_chip = 2 * x + y

        def piece(w, chip, half):
            hr = ins[w].shape[0] // 2
            return outs[w].at[chip, pl.ds(half * hr, hr), :]

        local, first, passed = [], [], []
        for w in range(n):
            hr = ins[w].shape[0] // 2
            cp = pltpu.make_async_copy(ins[w], outs[w].at[my_chip], local_sems.at[w])
            cp.start()
            local.append(cp)
            for j, (cx, cy) in enumerate(chips):
                cp = _remote(ins[w].at[pl.ds(c * hr, hr), :], piece(w, my_chip, c), send_sems, recv_sems,
                             3 * w + j, (cx, cy, c))
                cp.start()
                first.append(cp)
        for w in range(n):
            for j, (cx, cy) in enumerate(chips):
                got = piece(w, 2 * cx + cy, c)
                _remote(got, got, send_sems, recv_sems, 3 * w + j, me).wait_recv()
                cp = _remote(got, got, send_sems, recv_sems, 3 * n + 3 * w + j, sibling)
                cp.start()
                passed.append(cp)
        for w in range(n):
            for j, (cx, cy) in enumerate(chips):
                got = piece(w, 2 * cx + cy, 1 - c)
                _remote(got, got, send_sems, recv_sems, 3 * n + 3 * w + j, me).wait_recv()
        for cp in first + passed:
            cp.wait_send()
        for cp in local:
            cp.wait()

    return pl.pallas_call(
        body, name="allgather_weights",
        in_specs=[ANY] * n, out_specs=[ANY] * n,
        out_shape=[jax.ShapeDtypeStruct((N_CHIPS,) + s.shape, s.dtype) for s in shards],
        scratch_shapes=[pltpu.SemaphoreType.DMA((6 * n,)), pltpu.SemaphoreType.DMA((6 * n,)),
                        pltpu.SemaphoreType.DMA((n,))],
    )(*shards)


def _pair_exchange(grads):
    n = len(grads)

    def body(*refs):
        ins, outs = refs[:n], refs[n:2 * n]
        send_sems, recv_sems = refs[2 * n:]
        x, y, c, _ = _position()
        copies = []
        for w in range(n):
            hr = ins[w].shape[1] // 2
            cp = _remote(ins[w].at[:, pl.ds((1 - c) * hr, hr), :], outs[w], send_sems, recv_sems, w, (x, y, 1 - c))
            cp.start()
            copies.append(cp)
        for cp in copies:
            cp.wait()

    return pl.pallas_call(
        body, name="grad_pair_exchange",
        in_specs=[ANY] * n, out_specs=[ANY] * n,
        out_shape=[jax.ShapeDtypeStruct((N_CHIPS, g.shape[1] // 2, g.shape[2]), F32) for g in grads],
        scratch_shapes=[pltpu.SemaphoreType.DMA((n,)), pltpu.SemaphoreType.DMA((n,))],
    )(*grads)


def _pair_sum(grad, got, name):
    _, rows, cols = grad.shape
    hr = rows // 2
    tm = min(hr, 256)
    nb = hr // tm
    c = lax.axis_index("c")

    def body(c_ref, g_ref, o_ref, out_ref):
        out_ref[...] = (g_ref[...] + o_ref[...]).astype(BF16)

    return pl.pallas_call(
        body, name=name,
        grid_spec=pltpu.PrefetchScalarGridSpec(
            num_scalar_prefetch=1, grid=(N_CHIPS, nb),
            in_specs=[pl.BlockSpec((None, tm, cols), lambda s, i, c_ref: (s, c_ref[0] * nb + i, 0)),
                      pl.BlockSpec((None, tm, cols), lambda s, i, c_ref: (s, i, 0))],
            out_specs=pl.BlockSpec((None, tm, cols), lambda s, i, c_ref: (s, i, 0))),
        out_shape=jax.ShapeDtypeStruct((N_CHIPS, hr, cols), BF16),
        compiler_params=_params(2),
    )(jnp.reshape(c, (1,)).astype(jnp.int32), grad, got)


def _chip_exchange(pair_sums):
    n = len(pair_sums)

    def body(*refs):
        ins, outs = refs[:n], refs[n:2 * n]
        send_sems, recv_sems, local_sems = refs[2 * n:]
        x, y, c, chips = _position()
        me = (x, y, c)
        my_chip = 2 * x + y
        local, sent = [], []
        for w in range(n):
            cp = pltpu.make_async_copy(ins[w].at[my_chip], outs[w].at[my_chip], local_sems.at[w])
            cp.start()
            local.append(cp)
            for j, (cx, cy) in enumerate(chips):
                cp = _remote(ins[w].at[2 * cx + cy], outs[w].at[my_chip], send_sems, recv_sems, 3 * w + j, (cx, cy, c))
                cp.start()
                sent.append(cp)
        for w in range(n):
            for j, (cx, cy) in enumerate(chips):
                got = outs[w].at[2 * cx + cy]
                _remote(got, got, send_sems, recv_sems, 3 * w + j, me).wait_recv()
        for cp in sent:
            cp.wait_send()
        for cp in local:
            cp.wait()

    return pl.pallas_call(
        body, name="grad_chip_exchange",
        in_specs=[ANY] * n, out_specs=[ANY] * n,
        out_shape=[jax.ShapeDtypeStruct(p.shape, p.dtype) for p in pair_sums],
        scratch_shapes=[pltpu.SemaphoreType.DMA((3 * n,)), pltpu.SemaphoreType.DMA((3 * n,)),
                        pltpu.SemaphoreType.DMA((n,))],
    )(*pair_sums)


def _chip_sum(parts, name):
    _, hr, cols = parts.shape
    tm = min(hr, 256)

    def body(p_ref, o_ref):
        o_ref[...] = ((p_ref[0].astype(F32) + p_ref[1].astype(F32)) + p_ref[2].astype(F32)) + p_ref[3].astype(F32)

    return pl.pallas_call(
        body, name=name, grid=(hr // tm,),
        in_specs=[pl.BlockSpec((N_CHIPS, tm, cols), lambda i: (0, i, 0))],
        out_specs=pl.BlockSpec((tm, cols), lambda i: (i, 0)),
        out_shape=jax.ShapeDtypeStruct((hr, cols), F32), compiler_params=_params(1),
    )(parts)


def _share_halves(halves):
    n = len(halves)

    def body(*refs):
        ins, outs = refs[:n], refs[n:2 * n]
        send_sems, recv_sems, local_sems = refs[2 * n:]
        x, y, c, _ = _position()
        copies, local = [], []
        for w in range(n):
            hr = ins[w].shape[0]
            mine = outs[w].at[pl.ds(c * hr, hr), :]
            cp = pltpu.make_async_copy(ins[w], mine, local_sems.at[w])
            cp.start()
            local.append(cp)
            cp = _remote(ins[w], mine, send_sems, recv_sems, w, (x, y, 1 - c))
            cp.start()
            copies.append(cp)
        for w in range(n):
            hr = ins[w].shape[0]
            theirs = outs[w].at[pl.ds((1 - c) * hr, hr), :]
            _remote(theirs, theirs, send_sems, recv_sems, w, (x, y, c)).wait_recv()
        for cp in copies:
            cp.wait_send()
        for cp in local:
            cp.wait()

    return pl.pallas_call(
        body, name="grad_share_halves",
        in_specs=[ANY] * n, out_specs=[ANY] * n,
        out_shape=[jax.ShapeDtypeStruct((2 * h.shape[0], h.shape[1]), F32) for h in halves],
        scratch_shapes=[pltpu.SemaphoreType.DMA((n,)), pltpu.SemaphoreType.DMA((n,)), pltpu.SemaphoreType.DMA((n,))],
    )(*halves)


def _allreduce_small(g):
    rows = g.shape[0]

    def body(g_ref, o_ref, sib, slots, send_sems, recv_sems):
        x, y, c, chips = _position()
        me = (x, y, c)
        my_chip = 2 * x + y
        pair = _remote(g_ref, sib, send_sems, recv_sems, 0, (x, y, 1 - c))
        pair.start()
        pair.wait()
        slots[my_chip] = g_ref[...] + sib[...]
        sent = []
        for j, (cx, cy) in enumerate(chips):
            cp = _remote(slots.at[my_chip], slots.at[my_chip], send_sems, recv_sems, 1 + j, (cx, cy, c))
            cp.start()
            sent.append(cp)
        for j, (cx, cy) in enumerate(chips):
            got = slots.at[2 * cx + cy]
            _remote(got, got, send_sems, recv_sems, 1 + j, me).wait_recv()
        for cp in sent:
            cp.wait_send()
        o_ref[...] = ((slots[0] + slots[1]) + slots[2]) + slots[3]

    vm = pl.BlockSpec(memory_space=pltpu.VMEM)
    return pl.pallas_call(
        body, name="allreduce_small",
        in_specs=[vm], out_specs=vm, out_shape=jax.ShapeDtypeStruct((rows, 128), F32),
        scratch_shapes=[pltpu.VMEM((rows, 128), F32), pltpu.VMEM((N_CHIPS, rows, 128), F32),
                        pltpu.SemaphoreType.DMA((4,)), pltpu.SemaphoreType.DMA((4,))],
        compiler_params=pltpu.CompilerParams(vmem_limit_bytes=VMEM_LIMIT),
    )(g)


def _permute(a, d):
    return a if d == 1 else a.reshape(S // d, d, a.shape[1]).transpose(1, 0, 2).reshape(S, a.shape[1])


def _unpermute(a, d):
    return a if d == 1 else a.reshape(d, S // d, a.shape[1]).transpose(1, 0, 2).reshape(S, a.shape[1])


_SMALL = ("rel_bias", "ln_v_gain", "ln_v_bias", "w_spatial", "b_spatial", "ln1_gain", "ln1_bias",
          "b_ff1", "b_ff2", "ln2_gain", "ln2_bias")
_SMALL_ROWS = 1200


def _pack_small(parts):
    flat = jnp.concatenate([parts[k].reshape(-1).astype(F32) for k in _SMALL])
    flat = jnp.pad(flat, (0, _SMALL_ROWS * 128 - flat.shape[0]))
    return flat.reshape(_SMALL_ROWS, 128)


def _unpack_small(packed, like):
    flat = packed.reshape(-1)
    out, at = {}, 0
    for k in _SMALL:
        n = math.prod(like[k].shape)
        out[k] = flat[at:at + n].reshape(like[k].shape)
        at += n
    return out


def kernel(x, w_in, rel_bias, ln_v_gain, ln_v_bias, w_spatial, b_spatial, w_proj_a, w_proj_b, w_out, ln1_gain, ln1_bias, w_ff1, b_ff1, w_ff2, b_ff2, ln2_gain, ln2_bias, loss_target, m_w_in, m_rel_bias, m_ln_v_gain, m_ln_v_bias, m_w_spatial, m_b_spatial, m_w_proj_a, m_w_proj_b, m_w_out, m_ln1_gain, m_ln1_bias, m_w_ff1, m_b_ff1, m_w_ff2, m_b_ff2, m_ln2_gain, m_ln2_bias, v_w_in, v_rel_bias, v_ln_v_gain, v_ln_v_bias, v_w_spatial, v_b_spatial, v_w_proj_a, v_w_proj_b, v_w_out, v_ln1_gain, v_ln1_bias, v_w_ff1, v_b_ff1, v_w_ff2, v_b_ff2, v_ln2_gain, v_ln2_bias):
    args = dict(locals())
    big = ("w_in", "w_proj_a", "w_proj_b", "w_out", "w_ff1", "w_ff2")
    weights = ("w_in", "rel_bias", "ln_v_gain", "ln_v_bias", "w_spatial", "b_spatial", "w_proj_a", "w_proj_b", "w_out",
               "ln1_gain", "ln1_bias", "w_ff1", "b_ff1", "w_ff2", "b_ff2", "ln2_gain", "ln2_bias")

    xs = x[0]
    target = loss_target[0]

    win_g, wpa_g, wpb_g, wout_g, w1_g, w2_g = _allgather_weights([args[k][0].astype(BF16) for k in big])
    wout_full = wout_g.reshape(D, D)
    w2_full = w2_g.reshape(DFF, D)

    xb = xs.astype(BF16)
    proj = _proj(xb, win_g)
    qkv_p = [proj] + [_permute(proj[:, :3 * DA], d) for _, d in PATTERNS[1:]]
    outs, lses = [], []
    for p, (_, d) in enumerate(PATTERNS):
        o, l = _attn_fwd(qkv_p[p], rel_bias, d, f"attn_fwd_{p}")
        outs.append(_unpermute(o, d))
        lses.append(_unpermute(l, d))
    attn, lse = _attn_combine(outs, lses)
    ws = w_spatial[0]
    ws_t = jnp.transpose(ws, (0, 2, 1))
    bsp_b = jnp.broadcast_to(b_spatial[0][:, :, None], (NH, 128, 128))
    gmlp = _gmlp_fwd(proj, ws, bsp_b, ln_v_gain, ln_v_bias)
    ya, yb, merged = _branch(attn, gmlp, wpa_g, wpb_g, proj)
    xhat1, rstd1, h1b = _out_ln1(merged, wout_full, xs, ln1_gain, ln1_bias)
    a, r = _ff1(h1b, w1_g, b_ff1)
    dpre2, dpre2b, st2 = _ff2_ln2_loss(a, w2_full, xhat1, ln1_gain, ln1_bias, b_ff2, ln2_gain, ln2_bias, target)

    g_w2 = _grad_w(a, dpre2b, "grad_w_ff2", 512, 1024, False)
    dprea, g_b1 = _d_ff1(dpre2b, w2_full, r)
    g_w1 = _grad_w(h1b, dprea, "grad_w_ff1", 512, 1024, True)
    dpre1, dpre1b, st1 = _d_h1_ln1(dprea, w1_g, dpre2, xhat1, rstd1, ln1_gain)
    g_wout = _grad_w(merged, dpre1b, "grad_w_out", 512, 1024, False)
    dya, dyb, dga, dgb = _d_merged(dpre1b, wout_full, proj, ya, yb)
    g_wpa = _grad_w(attn, dya, "grad_w_proj_a", 512, 512, True)
    g_wpb = _grad_w(gmlp, dyb, "grad_w_proj_b", 512, 512, True)
    dattn, dgmlp = _d_branches(dya, dyb, wpa_g, wpb_g)
    duv, g_ws, g_bs, stv = _gmlp_bwd(proj, dgmlp, ws, ws_t, bsp_b, ln_v_gain, ln_v_bias)
    delta, dob = _attn_delta(dattn, attn)
    dqkv, ds_sums = [], []
    for p, (_, d) in enumerate(PATTERNS):
        dq, dk, dv, ds = _attn_bwd(qkv_p[p], _permute(dob, d), _permute(lse, d), _permute(delta, d), rel_bias, d,
                                   f"attn_bwd_{p}")
        dqkv.append(_unpermute(jnp.concatenate([dq, dk, dv], axis=1), d))
        ds_sums.append(ds)
    dqkv_b = _sum3_bf16(*dqkv)
    g_rb = _rel_bias_grad(ds_sums)[:, :NH]
    dproj = jnp.concatenate([dqkv_b, duv, dga, dgb], axis=1)
    g_win = _grad_w(xb, dproj, "grad_w_in", 512, 768, True)
    grad_x = _d_x(dproj, win_g, dpre1)

    small_g = dict(rel_bias=g_rb, ln_v_gain=stv[0], ln_v_bias=stv[1], w_spatial=g_ws, b_spatial=g_bs[:, :, 0],
                   ln1_gain=st1[0], ln1_bias=st1[1], b_ff1=g_b1, b_ff2=st2[2], ln2_gain=st2[0], ln2_bias=st2[1])
    gs = _allreduce_small(_pack_small(small_g))
    ds_, ms_, vs_ = _adamw(_pack_small({k: args[k] for k in _SMALL}), gs,
                           _pack_small({k: args["m_" + k] for k in _SMALL}),
                           _pack_small({k: args["v_" + k] for k in _SMALL}), "adamw_small")
    like = {k: args[k] for k in _SMALL}
    grads, deltas, new_m, new_v = (_unpack_small(t, like) for t in (gs, ds_, ms_, vs_))

    local = [g_win, g_wpa, g_wpb, g_wout.reshape(N_CHIPS, D // N_CHIPS, D), g_w1, g_w2.reshape(N_CHIPS, DFF // N_CHIPS, D)]
    from_sibling = _pair_exchange(local)
    pair_sums = [_pair_sum(g, o, f"pair_sum_{k}") for g, o, k in zip(local, from_sibling, big)]
    from_chips = _chip_exchange(pair_sums)
    halves = [_chip_sum(p, f"chip_sum_{k}") for p, k in zip(from_chips, big)]
    full = _share_halves(halves)
    for k, g in zip(big, full):
        d_, m_, v_ = _adamw(args[k][0], g, args["m_" + k][0], args["v_" + k][0], f"adamw_{k}")
        grads[k], deltas[k], new_m[k], new_v[k] = g[None], d_[None], m_[None], v_[None]

    loss = lax.psum(st2[3, 0] * (0.5 / D), ("x", "y", "c"))
    return (loss, grad_x[None], *[grads[k] for k in weights], *[deltas[k] for k in weights],
            *[new_m[k] for k in weights], *[new_v[k] for k in weights])
```

```python
import functools
import math

import numpy as np
import jax
import jax.numpy as jnp
from jax import lax
from jax.experimental import pallas as pl
from jax.experimental.pallas import tpu as pltpu

F32 = jnp.float32
BF16 = jnp.bfloat16

S = 2048
D = 2048
DA = 1024
DB = 1024
DFF = 8192
DIN = 9216
NH = 8
HD = 128
NBLK = 16
PATTERNS = ((128, 1), (512, 4), (2048, 16))
N_BUCKETS = 32
MAX_DISTANCE = 2048
ALPHA = 2.0 ** 0.25
LN_EPS = 1e-5
NEG_INF = -1e30
SCALE = HD ** -0.5
N_CHIPS = 4

ADAM_LR = 0.001
ADAM_B1 = 0.9
ADAM_B2 = 0.999
ADAM_EPS = 1e-08
ADAM_WD = 0.01
ADAM_STEP = 10

VMEM_LIMIT = 56 * 1024 * 1024
MESH = pl.DeviceIdType.MESH
ANY = pl.BlockSpec(memory_space=pl.ANY)


def _params(n_axes, vmem=VMEM_LIMIT):
    return pltpu.CompilerParams(dimension_semantics=("arbitrary",) * n_axes, vmem_limit_bytes=vmem)


def _bucket_tile(dilation):
    qi = np.arange(128)[:, None]
    kj = np.arange(256)[None, :]
    n = np.clip(128 + qi - kj, 0, 128) * dilation
    max_exact = N_BUCKETS // 2
    nf = np.maximum(n, 1).astype(np.float32)
    large = max_exact + (np.log(nf / np.float32(max_exact)) / np.float32(math.log(MAX_DISTANCE / max_exact))
                         * np.float32(N_BUCKETS - max_exact)).astype(np.int32)
    large = np.minimum(large, N_BUCKETS - 1)
    return np.where(n < max_exact, n, large).astype(np.int32)


def _gelu(x):
    c = math.sqrt(2.0 / math.pi)
    t = jnp.tanh(c * (x + 0.044715 * x * x * x))
    return 0.5 * x * (1.0 + t), t


def _gelu_grad(x, t):
    c = math.sqrt(2.0 / math.pi)
    return 0.5 * (1.0 + t) + 0.5 * x * (1.0 - t * t) * c * (1.0 + 3.0 * 0.044715 * x * x)


def _sigmoid(x):
    return 1.0 / (1.0 + jnp.exp(-x))


def _dot(a, b):
    return jnp.dot(a, b, preferred_element_type=F32)


def _dot_nt(a, b):
    return lax.dot_general(a, b, (((1,), (1,)), ((), ())), preferred_element_type=F32)


def _proj(xb, win_g):
    tn = 768
    per = 2304 // tn

    def body(x_ref, w_ref, o_ref):
        o_ref[...] = _dot(x_ref[...], w_ref[...])

    return pl.pallas_call(
        body, name="proj", grid=(DIN // tn,),
        in_specs=[pl.BlockSpec((S, D), lambda j: (0, 0)),
                  pl.BlockSpec((None, D, tn), lambda j: (j // per, 0, j % per))],
        out_specs=pl.BlockSpec((S, tn), lambda j: (0, j)),
        out_shape=jax.ShapeDtypeStruct((S, DIN), F32),
        compiler_params=_params(1),
    )(xb, win_g)


def _bias_tiles(rb_ref, bucket, bias_scr):
    qi = lax.broadcasted_iota(jnp.int32, (128, 256), 0)
    kj = lax.broadcasted_iota(jnp.int32, (128, 256), 1)
    steps = 128 + qi - kj
    band = (steps >= 0) & (steps <= 128)
    bias_scr[...] = jnp.zeros_like(bias_scr)

    def one_bucket(t, carry):
        hit = bucket == t
        for h in range(NH):
            bias_scr[h] = jnp.where(hit, rb_ref[t, h], bias_scr[h])
        return carry

    lax.fori_loop(0, N_BUCKETS, one_bucket, 0)
    for h in range(NH):
        bias_scr[h] = jnp.where(band, bias_scr[h], NEG_INF)


def _attn_fwd(qkv, rel_bias, dilation, name):
    nblk = NBLK // dilation
    bucket = jnp.asarray(_bucket_tile(dilation))

    def body(rb_ref, bucket_ref, q_ref, kc_ref, kp_ref, vc_ref, vp_ref, o_ref, lse_ref, bias_scr):
        b = pl.program_id(0)

        @pl.when(b == 0)
        def _():
            _bias_tiles(rb_ref, bucket_ref[...], bias_scr)

        has_prev = (b % nblk) != 0
        kj = lax.broadcasted_iota(jnp.int32, (128, 256), 1)
        key_ok = (kj >= 128) | has_prev
        for h in range(NH):
            cols = slice(h * HD, (h + 1) * HD)
            q = q_ref[:, cols].astype(BF16)
            s = jnp.concatenate([_dot_nt(q, kp_ref[:, cols].astype(BF16)),
                                 _dot_nt(q, kc_ref[:, cols].astype(BF16))], axis=1) * SCALE
            s = jnp.where(key_ok, s + bias_scr[h], NEG_INF)
            m = jnp.max(s, axis=1, keepdims=True)
            p = jnp.exp(s - m)
            den = jnp.sum(p, axis=1, keepdims=True)
            pb = p.astype(BF16)
            o = _dot(pb[:, :128], vp_ref[:, cols].astype(BF16)) + _dot(pb[:, 128:], vc_ref[:, cols].astype(BF16))
            o_ref[:, cols] = o / den
            lse_ref[:, cols] = jnp.broadcast_to(m + jnp.log(den), (128, HD))

    blk = lambda col, prev: pl.BlockSpec(
        (128, DA), (lambda b: (jnp.maximum(b - 1, 0), col)) if prev else (lambda b: (b, col)))
    return pl.pallas_call(
        body, name=name, grid=(NBLK,),
        in_specs=[pl.BlockSpec(memory_space=pltpu.SMEM),
                  pl.BlockSpec((128, 256), lambda b: (0, 0)),
                  blk(0, False), blk(1, False), blk(1, True), blk(2, False), blk(2, True)],
        out_specs=[pl.BlockSpec((128, DA), lambda b: (b, 0)), pl.BlockSpec((128, DA), lambda b: (b, 0))],
        out_shape=[jax.ShapeDtypeStruct((S, DA), F32), jax.ShapeDtypeStruct((S, DA), F32)],
        scratch_shapes=[pltpu.VMEM((NH, 128, 256), F32)],
        compiler_params=_params(1),
    )(rel_bias, bucket, qkv, qkv, qkv, qkv, qkv)


def _attn_combine(outs, lses):
    tm = 256

    def body(o1, o2, o3, l1, l2, l3, attn_ref, lse_ref):
        a, b, c = l1[...], l2[...], l3[...]
        m = jnp.maximum(jnp.maximum(a, b), c)
        wa, wb, wc = jnp.exp(a - m), jnp.exp(b - m), jnp.exp(c - m)
        den = wa + wb + wc
        attn_ref[...] = ((wa * o1[...] + wb * o2[...] + wc * o3[...]) / den).astype(BF16)
        lse_ref[...] = m + jnp.log(den)

    spec = pl.BlockSpec((tm, DA), lambda i: (i, 0))
    return pl.pallas_call(
        body, name="attn_combine", grid=(S // tm,),
        in_specs=[spec] * 6, out_specs=[spec, spec],
        out_shape=[jax.ShapeDtypeStruct((S, DA), BF16), jax.ShapeDtypeStruct((S, DA), F32)],
        compiler_params=_params(1),
    )(*outs, *lses)


def _gmlp_parts(u_ref, vb_ref, g_ref, be_ref):
    u = u_ref[...]
    u_act, tu = _gelu(u)
    vb = vb_ref[...]
    gv, tv = _gelu(vb)
    mean = jnp.mean(gv, axis=1, keepdims=True)
    cen = gv - mean
    var = jnp.mean(cen * cen, axis=1, keepdims=True)
    rstd = lax.rsqrt(var + LN_EPS)
    xhat = cen * rstd
    vn = xhat * g_ref[...] + be_ref[...]
    return u, tu, u_act, vb, tv, rstd, xhat, vn


def _gmlp_fwd(proj, ws, bsp_b, gain_v, bias_v):
    def body(u_ref, vb_ref, ws_ref, bsp_ref, g_ref, be_ref, o_ref):
        _, _, u_act, _, _, _, _, vn = _gmlp_parts(u_ref, vb_ref, g_ref, be_ref)
        row = lax.broadcasted_iota(jnp.int32, (128, 128), 0)
        col = lax.broadcasted_iota(jnp.int32, (128, 128), 1)
        causal = row >= col
        for g in range(NH):
            cols = slice(g * 128, (g + 1) * 128)
            wsg = jnp.where(causal, ws_ref[g], 0.0).astype(BF16)
            z = _dot(wsg, vn[:, cols].astype(BF16)) + bsp_ref[g]
            o_ref[:, cols] = (u_act[:, cols] * z).astype(BF16)

    return pl.pallas_call(
        body, name="gmlp_fwd", grid=(NBLK,),
        in_specs=[pl.BlockSpec((128, DB), lambda c: (c, 3)), pl.BlockSpec((128, DB), lambda c: (c, 4)),
                  pl.BlockSpec((NH, 128, 128), lambda c: (0, 0, 0)), pl.BlockSpec((NH, 128, 128), lambda c: (0, 0, 0)),
                  pl.BlockSpec((1, DB), lambda c: (0, 0)), pl.BlockSpec((1, DB), lambda c: (0, 0))],
        out_specs=pl.BlockSpec((128, DB), lambda c: (c, 0)),
        out_shape=jax.ShapeDtypeStruct((S, DB), BF16),
        compiler_params=_params(1),
    )(proj, proj, ws, bsp_b, gain_v, bias_v)


def _branch(attn, gmlp, wpa_g, wpb_g, proj):
    tn = 512

    def body(a_ref, g_ref, wa_ref, wb_ref, ga_ref, gb_ref, ya_ref, yb_ref, mg_ref):
        ya = _dot(a_ref[...], wa_ref[...])
        yb = _dot(g_ref[...], wb_ref[...])
        ya_ref[...] = ya.astype(BF16)
        yb_ref[...] = yb.astype(BF16)
        mg_ref[...] = (_sigmoid(ga_ref[...]) * ya + _sigmoid(gb_ref[...]) * yb).astype(BF16)

    out = pl.BlockSpec((S, tn), lambda j: (0, j))
    return pl.pallas_call(
        body, name="branch", grid=(D // tn,),
        in_specs=[pl.BlockSpec((S, DA), lambda j: (0, 0)), pl.BlockSpec((S, DB), lambda j: (0, 0)),
                  pl.BlockSpec((None, DA, tn), lambda j: (j, 0, 0)), pl.BlockSpec((None, DB, tn), lambda j: (j, 0, 0)),
                  pl.BlockSpec((S, tn), lambda j: (0, 5120 // tn + j)), pl.BlockSpec((S, tn), lambda j: (0, 7168 // tn + j))],
        out_specs=[out, out, out],
        out_shape=[jax.ShapeDtypeStruct((S, D), BF16)] * 3,
        compiler_params=_params(1),
    )(attn, gmlp, wpa_g, wpb_g, proj, proj)


def _out_ln1(merged, wout_g, x, gain, bias):
    tm = 256

    def body(m_ref, w_ref, x_ref, g_ref, b_ref, xh_ref, rs_ref, h_ref):
        pre = ALPHA * x_ref[...] + _dot(m_ref[...], w_ref[...])
        mean = jnp.mean(pre, axis=1, keepdims=True)
        cen = pre - mean
        var = jnp.mean(cen * cen, axis=1, keepdims=True)
        rstd = lax.rsqrt(var + LN_EPS)
        xhat = cen * rstd
        xh_ref[...] = xhat
        rs_ref[...] = jnp.broadcast_to(rstd, (tm, 128))
        h_ref[...] = (xhat * g_ref[...] + b_ref[...]).astype(BF16)

    row = pl.BlockSpec((tm, D), lambda i: (i, 0))
    vec = pl.BlockSpec((1, D), lambda i: (0, 0))
    return pl.pallas_call(
        body, name="out_ln1", grid=(S // tm,),
        in_specs=[row, pl.BlockSpec((D, D), lambda i: (0, 0)), row, vec, vec],
        out_specs=[row, pl.BlockSpec((tm, 128), lambda i: (i, 0)), row],
        out_shape=[jax.ShapeDtypeStruct((S, D), F32), jax.ShapeDtypeStruct((S, 128), F32),
                   jax.ShapeDtypeStruct((S, D), BF16)],
        compiler_params=_params(1),
    )(merged, wout_g, x, gain, bias)


def _ff1(h1b, w1_g, b1):
    tn = 512
    per = D // tn

    def body(h_ref, w_ref, b_ref, a_ref, r_ref):
        r = jnp.maximum(_dot(h_ref[...], w_ref[...]) + b_ref[...], 0.0)
        r_ref[...] = r.astype(BF16)
        a_ref[...] = (r * r).astype(BF16)

    out = pl.BlockSpec((S, tn), lambda j: (0, j))
    return pl.pallas_call(
        body, name="ff1", grid=(DFF // tn,),
        in_specs=[pl.BlockSpec((S, D), lambda j: (0, 0)),
                  pl.BlockSpec((None, D, tn), lambda j: (j // per, 0, j % per)),
                  pl.BlockSpec((1, tn), lambda j: (0, j))],
        out_specs=[out, out],
        out_shape=[jax.ShapeDtypeStruct((S, DFF), BF16)] * 2,
        compiler_params=_params(1),
    )(h1b, w1_g, b1)


def _ff2_ln2_loss(a, w2_g, xhat1, g1, b1, b2, g2, be2, target):
    tm, tk = 256, 1024
    nk = DFF // tk

    def body(a_ref, w_ref, xh_ref, g1_ref, b1_ref, b2_ref, g2_ref, be2_ref, t_ref, d_ref, db_ref, st_ref, acc):
        i, k = pl.program_id(0), pl.program_id(1)

        @pl.when(k == 0)
        def _():
            acc[...] = jnp.zeros_like(acc)

        @pl.when((i == 0) & (k == 0))
        def _():
            st_ref[...] = jnp.zeros_like(st_ref)

        acc[...] += _dot(a_ref[...], w_ref[...])

        @pl.when(k == nk - 1)
        def _():
            h1 = xh_ref[...] * g1_ref[...] + b1_ref[...]
            pre = ALPHA * h1 + acc[...] + b2_ref[...]
            mean = jnp.mean(pre, axis=1, keepdims=True)
            cen = pre - mean
            var = jnp.mean(cen * cen, axis=1, keepdims=True)
            rstd = lax.rsqrt(var + LN_EPS)
            xhat = cen * rstd
            y = xhat * g2_ref[...] + be2_ref[...]
            err = y - t_ref[...]
            dy = err * (1.0 / D)
            g = dy * g2_ref[...]
            dpre = rstd * (g - jnp.mean(g, axis=1, keepdims=True)
                           - xhat * jnp.mean(g * xhat, axis=1, keepdims=True))
            d_ref[...] = dpre
            db_ref[...] = dpre.astype(BF16)
            st_ref[0:1, :] += jnp.sum(dy * xhat, axis=0, keepdims=True)
            st_ref[1:2, :] += jnp.sum(dy, axis=0, keepdims=True)
            st_ref[2:3, :] += jnp.sum(dpre, axis=0, keepdims=True)
            st_ref[3:4, :] += jnp.broadcast_to(jnp.sum(err * err).reshape(1, 1), (1, D))

    row = pl.BlockSpec((tm, D), lambda i, k: (i, 0))
    vec = pl.BlockSpec((1, D), lambda i, k: (0, 0))
    return pl.pallas_call(
        body, name="ff2_ln2_loss", grid=(S // tm, nk),
        in_specs=[pl.BlockSpec((tm, tk), lambda i, k: (i, k)), pl.BlockSpec((tk, D), lambda i, k: (k, 0)),
                  row, vec, vec, vec, vec, vec, row],
        out_specs=[row, row, pl.BlockSpec((8, D), lambda i, k: (0, 0))],
        out_shape=[jax.ShapeDtypeStruct((S, D), F32), jax.ShapeDtypeStruct((S, D), BF16),
                   jax.ShapeDtypeStruct((8, D), F32)],
        scratch_shapes=[pltpu.VMEM((tm, D), F32)],
        compiler_params=_params(2),
    )(a, w2_g, xhat1, g1, b1, b2, g2, be2, target)


def _grad_w(act, dout, name, ti, tj, sharded):
    m, n = act.shape[1], dout.shape[1]
    ns = n // N_CHIPS
    per = ns // tj if sharded else None

    def body(a_ref, b_ref, o_ref, at_scr):
        @pl.when(pl.program_id(1) == 0)
        def _():
            at_scr[...] = a_ref[...].astype(F32).T.astype(BF16)

        o_ref[...] = _dot(at_scr[...], b_ref[...])

    if sharded:
        out_spec = pl.BlockSpec((None, ti, tj), lambda i, j: (j // per, i, j % per))
        out_shape = jax.ShapeDtypeStruct((N_CHIPS, m, ns), F32)
    else:
        out_spec = pl.BlockSpec((ti, tj), lambda i, j: (i, j))
        out_shape = jax.ShapeDtypeStruct((m, n), F32)
    return pl.pallas_call(
        body, name=name, grid=(m // ti, n // tj),
        in_specs=[pl.BlockSpec((S, ti), lambda i, j: (0, i)), pl.BlockSpec((S, tj), lambda i, j: (0, j))],
        out_specs=out_spec, out_shape=out_shape,
        scratch_shapes=[pltpu.VMEM((ti, S), BF16)],
        compiler_params=_params(2),
    )(act, dout)


def _d_ff1(dpre2b, w2_g, r):
    tn = 512

    def body(d_ref, w_ref, r_ref, o_ref, gb_ref):
        da = _dot_nt(d_ref[...], w_ref[...])
        dp = da * (2.0 * r_ref[...].astype(F32))
        o_ref[...] = dp.astype(BF16)
        gb_ref[...] = jnp.sum(dp, axis=0, keepdims=True)

    return pl.pallas_call(
        body, name="d_ff1", grid=(DFF // tn,),
        in_specs=[pl.BlockSpec((S, D), lambda j: (0, 0)), pl.BlockSpec((tn, D), lambda j: (j, 0)),
                  pl.BlockSpec((S, tn), lambda j: (0, j))],
        out_specs=[pl.BlockSpec((S, tn), lambda j: (0, j)), pl.BlockSpec((1, tn), lambda j: (0, j))],
        out_shape=[jax.ShapeDtypeStruct((S, DFF), BF16), jax.ShapeDtypeStruct((1, DFF), F32)],
        compiler_params=_params(1),
    )(dpre2b, w2_g, r)


def _d_h1_ln1(dprea, w1_g, dpre2, xhat1, rstd1, g1):
    tm, tk = 256, 512
    per = D // tk
    nk = DFF // tk

    def body(a_ref, w_ref, d2_ref, xh_ref, rs_ref, g_ref, d_ref, db_ref, st_ref, acc):
        i, k = pl.program_id(0), pl.program_id(1)

        @pl.when(k == 0)
        def _():
            acc[...] = jnp.zeros_like(acc)

        @pl.when((i == 0) & (k == 0))
        def _():
            st_ref[...] = jnp.zeros_like(st_ref)

        acc[...] += _dot_nt(a_ref[...], w_ref[...])

        @pl.when(k == nk - 1)
        def _():
            dh = ALPHA * d2_ref[...] + acc[...]
            xhat = xh_ref[...]
            g = dh * g_ref[...]
            dpre = rs_ref[:, 0:1] * (g - jnp.mean(g, axis=1, keepdims=True)
                                     - xhat * jnp.mean(g * xhat, axis=1, keepdims=True))
            d_ref[...] = dpre
            db_ref[...] = dpre.astype(BF16)
            st_ref[0:1, :] += jnp.sum(dh * xhat, axis=0, keepdims=True)
            st_ref[1:2, :] += jnp.sum(dh, axis=0, keepdims=True)

    row = pl.BlockSpec((tm, D), lambda i, k: (i, 0))
    return pl.pallas_call(
        body, name="d_h1_ln1", grid=(S // tm, nk),
        in_specs=[pl.BlockSpec((tm, tk), lambda i, k: (i, k)),
                  pl.BlockSpec((None, D, tk), lambda i, k: (k // per, 0, k % per)),
                  row, row, pl.BlockSpec((tm, 128), lambda i, k: (i, 0)), pl.BlockSpec((1, D), lambda i, k: (0, 0))],
        out_specs=[row, row, pl.BlockSpec((8, D), lambda i, k: (0, 0))],
        out_shape=[jax.ShapeDtypeStruct((S, D), F32), jax.ShapeDtypeStruct((S, D), BF16),
                   jax.ShapeDtypeStruct((8, D), F32)],
        scratch_shapes=[pltpu.VMEM((tm, D), F32)],
        compiler_params=_params(2),
    )(dprea, w1_g, dpre2, xhat1, rstd1, g1)


def _d_merged(dpre1b, wout_g, proj, ya, yb):
    tm, tn = 512, 1024

    def body(d_ref, w_ref, ga_ref, gb_ref, ya_ref, yb_ref, dya_ref, dyb_ref, dga_ref, dgb_ref):
        dm = _dot_nt(d_ref[...], w_ref[...])
        sa = _sigmoid(ga_ref[...])
        sb = _sigmoid(gb_ref[...])
        dya_ref[...] = (dm * sa).astype(BF16)
        dyb_ref[...] = (dm * sb).astype(BF16)
        dga_ref[...] = (dm * ya_ref[...].astype(F32) * sa * (1.0 - sa)).astype(BF16)
        dgb_ref[...] = (dm * yb_ref[...].astype(F32) * sb * (1.0 - sb)).astype(BF16)

    tile = pl.BlockSpec((tm, tn), lambda i, j: (i, j))
    return pl.pallas_call(
        body, name="d_merged", grid=(S // tm, D // tn),
        in_specs=[pl.BlockSpec((tm, D), lambda i, j: (i, 0)), pl.BlockSpec((tn, D), lambda i, j: (j, 0)),
                  pl.BlockSpec((tm, tn), lambda i, j: (i, 5 + j)), pl.BlockSpec((tm, tn), lambda i, j: (i, 7 + j)),
                  tile, tile],
        out_specs=[tile] * 4,
        out_shape=[jax.ShapeDtypeStruct((S, D), BF16)] * 4,
        compiler_params=_params(2),
    )(dpre1b, wout_g, proj, proj, ya, yb)


def _d_branches(dya, dyb, wpa_g, wpb_g):
    tk = 512

    def body(da_ref, db_ref, wa_ref, wb_ref, oa_ref, ob_ref):
        @pl.when(pl.program_id(0) == 0)
        def _():
            oa_ref[...] = jnp.zeros_like(oa_ref)
            ob_ref[...] = jnp.zeros_like(ob_ref)

        oa_ref[...] += _dot_nt(da_ref[...], wa_ref[...])
        ob_ref[...] += _dot_nt(db_ref[...], wb_ref[...])

    return pl.pallas_call(
        body, name="d_branches", grid=(D // tk,),
        in_specs=[pl.BlockSpec((S, tk), lambda k: (0, k)), pl.BlockSpec((S, tk), lambda k: (0, k)),
                  pl.BlockSpec((None, DA, tk), lambda k: (k, 0, 0)), pl.BlockSpec((None, DB, tk), lambda k: (k, 0, 0))],
        out_specs=[pl.BlockSpec((S, DA), lambda k: (0, 0)), pl.BlockSpec((S, DB), lambda k: (0, 0))],
        out_shape=[jax.ShapeDtypeStruct((S, DA), F32), jax.ShapeDtypeStruct((S, DB), F32)],
        compiler_params=_params(1),
    )(dya, dyb, wpa_g, wpb_g)


def _gmlp_bwd(proj, dgmlp, ws, ws_t, bsp_b, gain_v, bias_v):
    def body(u_ref, vb_ref, dg_ref, ws_ref, wst_ref, bsp_ref, g_ref, be_ref, duv_ref, gws_ref, gbs_ref, st_ref):
        @pl.when(pl.program_id(0) == 0)
        def _():
            gws_ref[...] = jnp.zeros_like(gws_ref)
            gbs_ref[...] = jnp.zeros_like(gbs_ref)
            st_ref[...] = jnp.zeros_like(st_ref)

        u, tu, u_act, vb, tv, rstd, xhat, vn = _gmlp_parts(u_ref, vb_ref, g_ref, be_ref)
        dg = dg_ref[...]
        dz = dg * u_act
        row = lax.broadcasted_iota(jnp.int32, (128, 128), 0)
        col = lax.broadcasted_iota(jnp.int32, (128, 128), 1)
        causal = row >= col
        causal_t = row <= col
        dvn_parts = []
        z_parts = []
        for g in range(NH):
            cols = slice(g * 128, (g + 1) * 128)
            vng = vn[:, cols].astype(BF16)
            dzg = dz[:, cols]
            dzb = dzg.astype(BF16)
            wsg = jnp.where(causal, ws_ref[g], 0.0).astype(BF16)
            wsg_t = jnp.where(causal_t, wst_ref[g], 0.0).astype(BF16)
            z_parts.append(_dot(wsg, vng) + bsp_ref[g])
            gws_ref[g] += jnp.where(causal, _dot_nt(dzb, vng), 0.0)
            gbs_ref[g] += jnp.broadcast_to(jnp.sum(dzg, axis=1, keepdims=True), (128, 128))
            dvn_parts.append(_dot(wsg_t, dzb))
        z = jnp.concatenate(z_parts, axis=1)
        dvn = jnp.concatenate(dvn_parts, axis=1)
        du = dg * z * _gelu_grad(u, tu)
        st_ref[0:1, :] += jnp.sum(dvn * xhat, axis=0, keepdims=True)
        st_ref[1:2, :] += jnp.sum(dvn, axis=0, keepdims=True)
        gg = dvn * g_ref[...]
        dgv = rstd * (gg - jnp.mean(gg, axis=1, keepdims=True) - xhat * jnp.mean(gg * xhat, axis=1, keepdims=True))
        dvb = dgv * _gelu_grad(vb, tv)
        duv_ref[:, 0:DB] = du.astype(BF16)
        duv_ref[:, DB:2 * DB] = dvb.astype(BF16)

    full3 = pl.BlockSpec((NH, 128, 128), lambda c: (0, 0, 0))
    vec = pl.BlockSpec((1, DB), lambda c: (0, 0))
    return pl.pallas_call(
        body, name="gmlp_bwd", grid=(NBLK,),
        in_specs=[pl.BlockSpec((128, DB), lambda c: (c, 3)), pl.BlockSpec((128, DB), lambda c: (c, 4)),
                  pl.BlockSpec((128, DB), lambda c: (c, 0)), full3, full3, full3, vec, vec],
        out_specs=[pl.BlockSpec((128, 2 * DB), lambda c: (c, 0)), full3, full3, pl.BlockSpec((8, DB), lambda c: (0, 0))],
        out_shape=[jax.ShapeDtypeStruct((S, 2 * DB), BF16), jax.ShapeDtypeStruct((NH, 128, 128), F32),
                   jax.ShapeDtypeStruct((NH, 128, 128), F32), jax.ShapeDtypeStruct((8, DB), F32)],
        compiler_params=_params(1),
    )(proj, proj, dgmlp, ws, ws_t, bsp_b, gain_v, bias_v)


def _attn_delta(dattn, attn):
    tm = 256

    def body(d_ref, o_ref, dl_ref, db_ref):
        d = d_ref[...]
        prod = d * o_ref[...].astype(F32)
        for h in range(NH):
            cols = slice(h * HD, (h + 1) * HD)
            dl_ref[:, cols] = jnp.broadcast_to(jnp.sum(prod[:, cols], axis=1, keepdims=True), (tm, HD))
        db_ref[...] = d.astype(BF16)

    spec = pl.BlockSpec((tm, DA), lambda i: (i, 0))
    return pl.pallas_call(
        body, name="attn_delta", grid=(S // tm,),
        in_specs=[spec, spec], out_specs=[spec, spec],
        out_shape=[jax.ShapeDtypeStruct((S, DA), F32), jax.ShapeDtypeStruct((S, DA), BF16)],
        compiler_params=_params(1),
    )(dattn, attn)


def _attn_bwd(qkv, dob, lse, delta, rel_bias, dilation, name):
    nblk = NBLK // dilation
    bucket = jnp.asarray(_bucket_tile(dilation))

    def body(rb_ref, bucket_ref, q_ref, qn_ref, kc_ref, kp_ref, vc_ref, vp_ref, do_ref, don_ref,
             l_ref, ln_ref, dl_ref, dln_ref, dq_ref, dk_ref, dv_ref, ds_ref, bias_scr):
        b = pl.program_id(0)

        @pl.when(b == 0)
        def _():
            _bias_tiles(rb_ref, bucket_ref[...], bias_scr)
            ds_ref[...] = jnp.zeros_like(ds_ref)

        has_prev = (b % nblk) != 0
        has_next = ((b + 1) % nblk) != 0
        for h in range(NH):
            cols = slice(h * HD, (h + 1) * HD)
            q = q_ref[:, cols].astype(BF16)
            kc = kc_ref[:, cols].astype(BF16)
            kp = kp_ref[:, cols].astype(BF16)
            vc = vc_ref[:, cols].astype(BF16)
            vp = vp_ref[:, cols].astype(BF16)
            do = do_ref[:, cols]
            bias_p = bias_scr[h, :, 0:128]
            bias_c = bias_scr[h, :, 128:256]
            lse_b = l_ref[:, cols]
            dl_b = dl_ref[:, cols]
            p_c = jnp.exp(_dot_nt(q, kc) * SCALE + bias_c - lse_b)
            p_p = jnp.where(has_prev, jnp.exp(_dot_nt(q, kp) * SCALE + bias_p - lse_b), 0.0)
            ds_c = p_c * (_dot_nt(do, vc) - dl_b)
            ds_p = p_p * (_dot_nt(do, vp) - dl_b)
            ds_ref[h, :, 0:128] += ds_p
            ds_ref[h, :, 128:256] += ds_c
            ds_cb = ds_c.astype(BF16)
            dq_ref[:, cols] = (_dot(ds_cb, kc) + _dot(ds_p.astype(BF16), kp)) * SCALE
            qn = qn_ref[:, cols].astype(BF16)
            don = don_ref[:, cols]
            p_n = jnp.where(has_next, jnp.exp(_dot_nt(qn, kc) * SCALE + bias_p - ln_ref[:, cols]), 0.0)
            ds_n = p_n * (_dot_nt(don, vc) - dln_ref[:, cols])
            dk_ref[:, cols] = (_dot(ds_c.T.astype(BF16), q) + _dot(ds_n.T.astype(BF16), qn)) * SCALE
            dv_ref[:, cols] = _dot(p_c.T.astype(BF16), do) + _dot(p_n.T.astype(BF16), don)

    cur = lambda col: pl.BlockSpec((128, DA), lambda b: (b, col))
    prev = lambda col: pl.BlockSpec((128, DA), lambda b: (jnp.maximum(b - 1, 0), col))
    nxt = lambda col: pl.BlockSpec((128, DA), lambda b: (jnp.minimum(b + 1, NBLK - 1), col))
    return pl.pallas_call(
        body, name=name, grid=(NBLK,),
        in_specs=[pl.BlockSpec(memory_space=pltpu.SMEM), pl.BlockSpec((128, 256), lambda b: (0, 0)),
                  cur(0), nxt(0), cur(1), prev(1), cur(2), prev(2), cur(0), nxt(0), cur(0), nxt(0), cur(0), nxt(0)],
        out_specs=[cur(0), cur(0), cur(0), pl.BlockSpec((NH, 128, 256), lambda b: (0, 0, 0))],
        out_shape=[jax.ShapeDtypeStruct((S, DA), F32)] * 3 + [jax.ShapeDtypeStruct((NH, 128, 256), F32)],
        scratch_shapes=[pltpu.VMEM((NH, 128, 256), F32)],
        compiler_params=_params(1),
    )(rel_bias, bucket, qkv, qkv, qkv, qkv, qkv, qkv, dob, dob, lse, lse, delta, delta)


def _sum3_bf16(a, b, c):
    tm = 256
    n = a.shape[1]

    def body(a_ref, b_ref, c_ref, o_ref):
        o_ref[...] = (a_ref[...] + b_ref[...] + c_ref[...]).astype(BF16)

    spec = pl.BlockSpec((tm, n), lambda i: (i, 0))
    return pl.pallas_call(
        body, name="dqkv_sum", grid=(S // tm,), in_specs=[spec] * 3, out_specs=spec,
        out_shape=jax.ShapeDtypeStruct((S, n), BF16), compiler_params=_params(1),
    )(a, b, c)


def _rel_bias_grad(ds_sums):
    buckets = jnp.asarray(np.stack([_bucket_tile(d) for _, d in PATTERNS]))

    def body(bk_ref, d1, d2, d3, o_ref):
        row = lax.broadcasted_iota(jnp.int32, (N_BUCKETS, 128), 0)
        lane = lax.broadcasted_iota(jnp.int32, (N_BUCKETS, 128), 1)

        def one_bucket(t, out):
            hits = [bk_ref[p] == t for p in range(3)]
            for h in range(NH):
                tot = jnp.zeros((128, 256), F32)
                for p, d in enumerate((d1, d2, d3)):
                    tot = tot + jnp.where(hits[p], d[h], 0.0)
                out = jnp.where((row == t) & (lane == h), jnp.sum(tot), out)
            return out

        o_ref[...] = lax.fori_loop(0, N_BUCKETS, one_bucket, jnp.zeros((N_BUCKETS, 128), F32))

    return pl.pallas_call(
        body, name="rel_bias_grad",
        in_specs=[pl.BlockSpec(memory_space=pltpu.VMEM)] * 4, out_specs=pl.BlockSpec(memory_space=pltpu.VMEM),
        out_shape=jax.ShapeDtypeStruct((N_BUCKETS, 128), F32),
        compiler_params=pltpu.CompilerParams(vmem_limit_bytes=VMEM_LIMIT),
    )(buckets, *ds_sums)


def _d_x(dproj, win_g, dpre1):
    tm, tk = 512, 768
    per = 2304 // tk
    nk = DIN // tk

    def body(a_ref, w_ref, d_ref, o_ref, acc):
        k = pl.program_id(1)

        @pl.when(k == 0)
        def _():
            acc[...] = ALPHA * d_ref[...]

        acc[...] += _dot_nt(a_ref[...], w_ref[...])

        @pl.when(k == nk - 1)
        def _():
            o_ref[...] = acc[...]

    row = pl.BlockSpec((tm, D), lambda i, k: (i, 0))
    return pl.pallas_call(
        body, name="d_x", grid=(S // tm, nk),
        in_specs=[pl.BlockSpec((tm, tk), lambda i, k: (i, k)),
                  pl.BlockSpec((None, D, tk), lambda i, k: (k // per, 0, k % per)), row],
        out_specs=row, out_shape=jax.ShapeDtypeStruct((S, D), F32),
        scratch_shapes=[pltpu.VMEM((tm, D), F32)],
        compiler_params=_params(2),
    )(dproj, win_g, dpre1)


def _adamw(w, g, m, v, name):
    rows, cols = w.shape
    tm = max(t for t in range(8, 257, 8) if rows % t == 0)

    def body(w_ref, g_ref, m_ref, v_ref, d_ref, nm_ref, nv_ref):
        g = g_ref[...]
        m = ADAM_B1 * m_ref[...] + (1.0 - ADAM_B1) * g
        v = ADAM_B2 * v_ref[...] + (1.0 - ADAM_B2) * (g * g)
        m_hat = m / (1.0 - ADAM_B1 ** ADAM_STEP)
        v_hat = v / (1.0 - ADAM_B2 ** ADAM_STEP)
        d_ref[...] = -ADAM_LR * (m_hat / (jnp.sqrt(v_hat) + ADAM_EPS) + ADAM_WD * w_ref[...])
        nm_ref[...] = m
        nv_ref[...] = v

    spec = pl.BlockSpec((tm, cols), lambda i: (i, 0))
    return pl.pallas_call(
        body, name=name, grid=(rows // tm,), in_specs=[spec] * 4, out_specs=[spec] * 3,
        out_shape=[jax.ShapeDtypeStruct((rows, cols), F32)] * 3, compiler_params=_params(1),
    )(w, g, m, v)


def _position():
    x, y, c = lax.axis_index("x"), lax.axis_index("y"), lax.axis_index("c")
    chips = [(1 - x, y), (x, 1 - y), (1 - x, 1 - y)]
    return x, y, c, chips


def _remote(src, dst, send_sems, recv_sems, k, to):
    return pltpu.make_async_remote_copy(src_ref=src, dst_ref=dst, send_sem=send_sems.at[k], recv_sem=recv_sems.at[k],
                                        device_id=to, device_id_type=MESH)


def _place_shard(w, name):
    rows, cols = w.shape
    tm = 256
    x, y = lax.axis_index("x"), lax.axis_index("y")

    def body(chip_ref, w_ref, o_ref):
        o_ref[...] = w_ref[...].astype(BF16)

    return pl.pallas_call(
        body, name=name,
        grid_spec=pltpu.PrefetchScalarGridSpec(
            num_scalar_prefetch=1, grid=(rows // tm,),
            in_specs=[pl.BlockSpec((tm, cols), lambda i, chip: (i, 0))],
            out_specs=pl.BlockSpec((None, tm, cols), lambda i, chip: (chip[0], i, 0))),
        out_shape=jax.ShapeDtypeStruct((N_CHIPS, rows, cols), BF16),
        compiler_params=_params(1),
    )(jnp.reshape(2 * x + y, (1,)).astype(jnp.int32), w)


def _allgather_weights(bufs):
    n = len(bufs)

    def body(*refs):
        outs = refs[n:2 * n]
        send_sems, recv_sems = refs[2 * n:]
        x, y, c, chips = _position()
        me, sibling = (x, y, c), (x, y, 1 - c)
        my_chip = 2 * x + y

        def piece(w, chip, half):
            hr = outs[w].shape[1] // 2
            return outs[w].at[chip, pl.ds(half * hr, hr), :]

        first, passed = [], []
        for w in range(n):
            for j, (cx, cy) in enumerate(chips):
                mine = piece(w, my_chip, c)
                cp = _remote(mine, mine, send_sems, recv_sems, 3 * w + j, (cx, cy, c))
                cp.start()
                first.append(cp)
        for w in range(n):
            for j, (cx, cy) in enumerate(chips):
                got = piece(w, 2 * cx + cy, c)
                _remote(got, got, send_sems, recv_sems, 3 * w + j, me).wait_recv()
                cp = _remote(got, got, send_sems, recv_sems, 3 * n + 3 * w + j, sibling)
                cp.start()
                passed.append(cp)
        for w in range(n):
            for j, (cx, cy) in enumerate(chips):
                got = piece(w, 2 * cx + cy, 1 - c)
                _remote(got, got, send_sems, recv_sems, 3 * n + 3 * w + j, me).wait_recv()
        for cp in first + passed:
            cp.wait_send()

    return pl.pallas_call(
        body, name="allgather_weights",
        in_specs=[ANY] * n, out_specs=[ANY] * n,
        out_shape=[jax.ShapeDtypeStruct(b.shape, b.dtype) for b in bufs],
        input_output_aliases={w: w for w in range(n)},
        scratch_shapes=[pltpu.SemaphoreType.DMA((6 * n,)), pltpu.SemaphoreType.DMA((6 * n,))],
    )(*bufs)


def _pair_exchange(grads):
    n = len(grads)

    def body(*refs):
        ins, outs = refs[:n], refs[n:2 * n]
        send_sems, recv_sems = refs[2 * n:]
        x, y, c, _ = _position()
        copies = []
        for w in range(n):
            hr = ins[w].shape[1] // 2
            cp = _remote(ins[w].at[:, pl.ds((1 - c) * hr, hr), :], outs[w], send_sems, recv_sems, w, (x, y, 1 - c))
            cp.start()
            copies.append(cp)
        for cp in copies:
            cp.wait()

    return pl.pallas_call(
        body, name="grad_pair_exchange",
        in_specs=[ANY] * n, out_specs=[ANY] * n,
        out_shape=[jax.ShapeDtypeStruct((N_CHIPS, g.shape[1] // 2, g.shape[2]), F32) for g in grads],
        scratch_shapes=[pltpu.SemaphoreType.DMA((n,)), pltpu.SemaphoreType.DMA((n,))],
    )(*grads)


def _pair_sum(grad, got, name):
    _, rows, cols = grad.shape
    hr = rows // 2
    tm = min(hr, 256)
    nb = hr // tm
    c = lax.axis_index("c")

    def body(c_ref, g_ref, o_ref, out_ref):
        out_ref[...] = (g_ref[...] + o_ref[...]).astype(BF16)

    return pl.pallas_call(
        body, name=name,
        grid_spec=pltpu.PrefetchScalarGridSpec(
            num_scalar_prefetch=1, grid=(N_CHIPS, nb),
            in_specs=[pl.BlockSpec((None, tm, cols), lambda s, i, c_ref: (s, c_ref[0] * nb + i, 0)),
                      pl.BlockSpec((None, tm, cols), lambda s, i, c_ref: (s, i, 0))],
            out_specs=pl.BlockSpec((None, tm, cols), lambda s, i, c_ref: (s, i, 0))),
        out_shape=jax.ShapeDtypeStruct((N_CHIPS, hr, cols), BF16),
        compiler_params=_params(2),
    )(jnp.reshape(c, (1,)).astype(jnp.int32), grad, got)


def _chip_exchange(pair_sums):
    n = len(pair_sums)

    def body(*refs):
        ins, outs = refs[:n], refs[n:2 * n]
        send_sems, recv_sems = refs[2 * n:]
        x, y, c, chips = _position()
        me = (x, y, c)
        my_chip = 2 * x + y
        sent = []
        for w in range(n):
            for j, (cx, cy) in enumerate(chips):
                cp = _remote(ins[w].at[2 * cx + cy], outs[w].at[my_chip], send_sems, recv_sems, 3 * w + j, (cx, cy, c))
                cp.start()
                sent.append(cp)
        for w in range(n):
            for j, (cx, cy) in enumerate(chips):
                got = outs[w].at[2 * cx + cy]
                _remote(got, got, send_sems, recv_sems, 3 * w + j, me).wait_recv()
        for cp in sent:
            cp.wait_send()

    return pl.pallas_call(
        body, name="grad_chip_exchange",
        in_specs=[ANY] * n, out_specs=[ANY] * n,
        out_shape=[jax.ShapeDtypeStruct(p.shape, p.dtype) for p in pair_sums],
        scratch_shapes=[pltpu.SemaphoreType.DMA((3 * n,)), pltpu.SemaphoreType.DMA((3 * n,))],
    )(*pair_sums)


def _chip_sum(parts, pair_sums, name):
    _, hr, cols = parts.shape
    tm = min(hr, 256)
    nb = hr // tm
    x, y, c = lax.axis_index("x"), lax.axis_index("y"), lax.axis_index("c")

    def body(pos_ref, p_ref, own_ref, o_ref):
        chip = pos_ref[0]
        own = own_ref[...].astype(F32)
        term = lambda s: jnp.where(chip == s, own, p_ref[s].astype(F32))
        o_ref[...] = ((term(0) + term(1)) + term(2)) + term(3)

    return pl.pallas_call(
        body, name=name,
        grid_spec=pltpu.PrefetchScalarGridSpec(
            num_scalar_prefetch=1, grid=(nb,),
            in_specs=[pl.BlockSpec((N_CHIPS, tm, cols), lambda i, pos: (0, i, 0)),
                      pl.BlockSpec((None, tm, cols), lambda i, pos: (pos[0], i, 0))],
            out_specs=pl.BlockSpec((tm, cols), lambda i, pos: (pos[1] * nb + i, 0))),
        out_shape=jax.ShapeDtypeStruct((2 * hr, cols), F32), compiler_params=_params(1),
    )(jnp.stack([2 * x + y, c]).astype(jnp.int32), parts, pair_sums)


def _share_halves(bufs):
    n = len(bufs)

    def body(*refs):
        outs = refs[n:2 * n]
        send_sems, recv_sems = refs[2 * n:]
        x, y, c, _ = _position()
        copies = []
        for w in range(n):
            hr = outs[w].shape[0] // 2
            mine = outs[w].at[pl.ds(c * hr, hr), :]
            cp = _remote(mine, mine, send_sems, recv_sems, w, (x, y, 1 - c))
            cp.start()
            copies.append(cp)
        for w in range(n):
            hr = outs[w].shape[0] // 2
            theirs = outs[w].at[pl.ds((1 - c) * hr, hr), :]
            _remote(theirs, theirs, send_sems, recv_sems, w, (x, y, c)).wait_recv()
        for cp in copies:
            cp.wait_send()

    return pl.pallas_call(
        body, name="grad_share_halves",
        in_specs=[ANY] * n, out_specs=[ANY] * n,
        out_shape=[jax.ShapeDtypeStruct(b.shape, b.dtype) for b in bufs],
        input_output_aliases={w: w for w in range(n)},
        scratch_shapes=[pltpu.SemaphoreType.DMA((n,)), pltpu.SemaphoreType.DMA((n,))],
    )(*bufs)


def _allreduce_small(g):
    rows = g.shape[0]

    def body(g_ref, o_ref, sib, slots, send_sems, recv_sems):
        x, y, c, chips = _position()
        me = (x, y, c)
        my_chip = 2 * x + y
        pair = _remote(g_ref, sib, send_sems, recv_sems, 0, (x, y, 1 - c))
        pair.start()
        pair.wait()
        slots[my_chip] = g_ref[...] + sib[...]
        sent = []
        for j, (cx, cy) in enumerate(chips):
            cp = _remote(slots.at[my_chip], slots.at[my_chip], send_sems, recv_sems, 1 + j, (cx, cy, c))
            cp.start()
            sent.append(cp)
        for j, (cx, cy) in enumerate(chips):
            got = slots.at[2 * cx + cy]
            _remote(got, got, send_sems, recv_sems, 1 + j, me).wait_recv()
        for cp in sent:
            cp.wait_send()
        o_ref[...] = ((slots[0] + slots[1]) + slots[2]) + slots[3]

    vm = pl.BlockSpec(memory_space=pltpu.VMEM)
    return pl.pallas_call(
        body, name="allreduce_small",
        in_specs=[vm], out_specs=vm, out_shape=jax.ShapeDtypeStruct((rows, 128), F32),
        scratch_shapes=[pltpu.VMEM((rows, 128), F32), pltpu.VMEM((N_CHIPS, rows, 128), F32),
                        pltpu.SemaphoreType.DMA((4,)), pltpu.SemaphoreType.DMA((4,))],
        compiler_params=pltpu.CompilerParams(vmem_limit_bytes=VMEM_LIMIT),
    )(g)


def _permute(a, d):
    return a if d == 1 else a.reshape(S // d, d, a.shape[1]).transpose(1, 0, 2).reshape(S, a.shape[1])


def _unpermute(a, d):
    return a if d == 1 else a.reshape(d, S // d, a.shape[1]).transpose(1, 0, 2).reshape(S, a.shape[1])


_SMALL = ("rel_bias", "ln_v_gain", "ln_v_bias", "w_spatial", "b_spatial", "ln1_gain", "ln1_bias",
          "b_ff1", "b_ff2", "ln2_gain", "ln2_bias")
_SMALL_ROWS = 1200


def _pack_small(parts):
    flat = jnp.concatenate([parts[k].reshape(-1).astype(F32) for k in _SMALL])
    flat = jnp.pad(flat, (0, _SMALL_ROWS * 128 - flat.shape[0]))
    return flat.reshape(_SMALL_ROWS, 128)


def _unpack_small(packed, like):
    flat = packed.reshape(-1)
    out, at = {}, 0
    for k in _SMALL:
        n = math.prod(like[k].shape)
        out[k] = flat[at:at + n].reshape(like[k].shape)
        at += n
    return out


def kernel(x, w_in, rel_bias, ln_v_gain, ln_v_bias, w_spatial, b_spatial, w_proj_a, w_proj_b, w_out, ln1_gain, ln1_bias, w_ff1, b_ff1, w_ff2, b_ff2, ln2_gain, ln2_bias, loss_target, m_w_in, m_rel_bias, m_ln_v_gain, m_ln_v_bias, m_w_spatial, m_b_spatial, m_w_proj_a, m_w_proj_b, m_w_out, m_ln1_gain, m_ln1_bias, m_w_ff1, m_b_ff1, m_w_ff2, m_b_ff2, m_ln2_gain, m_ln2_bias, v_w_in, v_rel_bias, v_ln_v_gain, v_ln_v_bias, v_w_spatial, v_b_spatial, v_w_proj_a, v_w_proj_b, v_w_out, v_ln1_gain, v_ln1_bias, v_w_ff1, v_b_ff1, v_w_ff2, v_b_ff2, v_ln2_gain, v_ln2_bias):
    args = dict(locals())
    big = ("w_in", "w_proj_a", "w_proj_b", "w_out", "w_ff1", "w_ff2")
    weights = ("w_in", "rel_bias", "ln_v_gain", "ln_v_bias", "w_spatial", "b_spatial", "w_proj_a", "w_proj_b", "w_out",
               "ln1_gain", "ln1_bias", "w_ff1", "b_ff1", "w_ff2", "b_ff2", "ln2_gain", "ln2_bias")

    xs = x[0]
    target = loss_target[0]

    win_g, wpa_g, wpb_g, wout_g, w1_g, w2_g = _allgather_weights([_place_shard(args[k][0], f"place_{k}") for k in big])
    wout_full = wout_g.reshape(D, D)
    w2_full = w2_g.reshape(DFF, D)

    xb = xs.astype(BF16)
    proj = _proj(xb, win_g)
    qkv_p = [proj] + [_permute(proj[:, :3 * DA], d) for _, d in PATTERNS[1:]]
    outs, lses = [], []
    for p, (_, d) in enumerate(PATTERNS):
        o, l = _attn_fwd(qkv_p[p], rel_bias, d, f"attn_fwd_{p}")
        outs.append(_unpermute(o, d))
        lses.append(_unpermute(l, d))
    attn, lse = _attn_combine(outs, lses)
    ws = w_spatial[0]
    ws_t = jnp.transpose(ws, (0, 2, 1))
    bsp_b = jnp.broadcast_to(b_spatial[0][:, :, None], (NH, 128, 128))
    gmlp = _gmlp_fwd(proj, ws, bsp_b, ln_v_gain, ln_v_bias)
    ya, yb, merged = _branch(attn, gmlp, wpa_g, wpb_g, proj)
    xhat1, rstd1, h1b = _out_ln1(merged, wout_full, xs, ln1_gain, ln1_bias)
    a, r = _ff1(h1b, w1_g, b_ff1)
    dpre2, dpre2b, st2 = _ff2_ln2_loss(a, w2_full, xhat1, ln1_gain, ln1_bias, b_ff2, ln2_gain, ln2_bias, target)

    g_w2 = _grad_w(a, dpre2b, "grad_w_ff2", 512, 1024, False)
    dprea, g_b1 = _d_ff1(dpre2b, w2_full, r)
    g_w1 = _grad_w(h1b, dprea, "grad_w_ff1", 512, 1024, True)
    dpre1, dpre1b, st1 = _d_h1_ln1(dprea, w1_g, dpre2, xhat1, rstd1, ln1_gain)
    g_wout = _grad_w(merged, dpre1b, "grad_w_out", 512, 1024, False)
    dya, dyb, dga, dgb = _d_merged(dpre1b, wout_full, proj, ya, yb)
    g_wpa = _grad_w(attn, dya, "grad_w_proj_a", 512, 512, True)
    g_wpb = _grad_w(gmlp, dyb, "grad_w_proj_b", 512, 512, True)
    dattn, dgmlp = _d_branches(dya, dyb, wpa_g, wpb_g)
    duv, g_ws, g_bs, stv = _gmlp_bwd(proj, dgmlp, ws, ws_t, bsp_b, ln_v_gain, ln_v_bias)
    delta, dob = _attn_delta(dattn, attn)
    dqkv, ds_sums = [], []
    for p, (_, d) in enumerate(PATTERNS):
        dq, dk, dv, ds = _attn_bwd(qkv_p[p], _permute(dob, d), _permute(lse, d), _permute(delta, d), rel_bias, d,
                                   f"attn_bwd_{p}")
        dqkv.append(_unpermute(jnp.concatenate([dq, dk, dv], axis=1), d))
        ds_sums.append(ds)
    dqkv_b = _sum3_bf16(*dqkv)
    g_rb = _rel_bias_grad(ds_sums)[:, :NH]
    dproj = jnp.concatenate([dqkv_b, duv, dga, dgb], axis=1)
    g_win = _grad_w(xb, dproj, "grad_w_in", 512, 768, True)
    grad_x = _d_x(dproj, win_g, dpre1)

    small_g = dict(rel_bias=g_rb, ln_v_gain=stv[0], ln_v_bias=stv[1], w_spatial=g_ws, b_spatial=g_bs[:, :, 0],
                   ln1_gain=st1[0], ln1_bias=st1[1], b_ff1=g_b1, b_ff2=st2[2], ln2_gain=st2[0], ln2_bias=st2[1])
    gs = _allreduce_small(_pack_small(small_g))
    ds_, ms_, vs_ = _adamw(_pack_small({k: args[k] for k in _SMALL}), gs,
                           _pack_small({k: args["m_" + k] for k in _SMALL}),
                           _pack_small({k: args["v_" + k] for k in _SMALL}), "adamw_small")
    like = {k: args[k] for k in _SMALL}
    grads, deltas, new_m, new_v = (_unpack_small(t, like) for t in (gs, ds_, ms_, vs_))

    local = [g_win, g_wpa, g_wpb, g_wout.reshape(N_CHIPS, D // N_CHIPS, D), g_w1, g_w2.reshape(N_CHIPS, DFF // N_CHIPS, D)]
    from_sibling = _pair_exchange(local)
    pair_sums = [_pair_sum(g, o, f"pair_sum_{k}") for g, o, k in zip(local, from_sibling, big)]
    from_chips = _chip_exchange(pair_sums)
    halves = [_chip_sum(p, own, f"chip_sum_{k}") for p, own, k in zip(from_chips, pair_sums, big)]
    full = _share_halves(halves)
    for k, g in zip(big, full):
        d_, m_, v_ = _adamw(args[k][0], g, args["m_" + k][0], args["v_" + k][0], f"adamw_{k}")
        grads[k], deltas[k], new_m[k], new_v[k] = g[None], d_[None], m_[None], v_[None]

    loss = lax.psum(st2[3, 0] * (0.5 / D), ("x", "y", "c"))
    return (loss, grad_x[None], *[grads[k] for k in weights], *[deltas[k] for k in weights],
            *[new_m[k] for k in weights], *[new_v[k] for k in weights])
```

```python
import functools
import math

import numpy as np
import jax
import jax.numpy as jnp
from jax import lax
from jax.experimental import pallas as pl
from jax.experimental.pallas import tpu as pltpu

F32 = jnp.float32
BF16 = jnp.bfloat16

S = 2048
D = 2048
DA = 1024
DB = 1024
DFF = 8192
DIN = 9216
NH = 8
HD = 128
NBLK = 16
PATTERNS = ((128, 1), (512, 4), (2048, 16))
N_BUCKETS = 32
MAX_DISTANCE = 2048
ALPHA = 2.0 ** 0.25
LN_EPS = 1e-5
NEG_INF = -1e30
SCALE = HD ** -0.5
N_CHIPS = 4

ADAM_LR = 0.001
ADAM_B1 = 0.9
ADAM_B2 = 0.999
ADAM_EPS = 1e-08
ADAM_WD = 0.01
ADAM_STEP = 10

VMEM_LIMIT = 56 * 1024 * 1024
MESH = pl.DeviceIdType.MESH
ANY = pl.BlockSpec(memory_space=pl.ANY)


def _params(n_axes, vmem=VMEM_LIMIT):
    return pltpu.CompilerParams(dimension_semantics=("arbitrary",) * n_axes, vmem_limit_bytes=vmem)


def _bucket_tile(dilation):
    qi = np.arange(128)[:, None]
    kj = np.arange(256)[None, :]
    n = np.clip(128 + qi - kj, 0, 128) * dilation
    max_exact = N_BUCKETS // 2
    nf = np.maximum(n, 1).astype(np.float32)
    large = max_exact + (np.log(nf / np.float32(max_exact)) / np.float32(math.log(MAX_DISTANCE / max_exact))
                         * np.float32(N_BUCKETS - max_exact)).astype(np.int32)
    large = np.minimum(large, N_BUCKETS - 1)
    return np.where(n < max_exact, n, large).astype(np.int32)


def _gelu(x):
    c = math.sqrt(2.0 / math.pi)
    t = jnp.tanh(c * (x + 0.044715 * x * x * x))
    return 0.5 * x * (1.0 + t), t


def _gelu_grad(x, t):
    c = math.sqrt(2.0 / math.pi)
    return 0.5 * (1.0 + t) + 0.5 * x * (1.0 - t * t) * c * (1.0 + 3.0 * 0.044715 * x * x)


def _sigmoid(x):
    return 1.0 / (1.0 + jnp.exp(-x))


def _dot(a, b):
    return jnp.dot(a, b, preferred_element_type=F32)


def _dot_nt(a, b):
    return lax.dot_general(a, b, (((1,), (1,)), ((), ())), preferred_element_type=F32)


def _proj(xb, win_g):
    tn = 768
    per = 2304 // tn

    def body(x_ref, w_ref, o_ref):
        o_ref[...] = _dot(x_ref[...], w_ref[...])

    return pl.pallas_call(
        body, name="proj", grid=(DIN // tn,),
        in_specs=[pl.BlockSpec((S, D), lambda j: (0, 0)),
                  pl.BlockSpec((None, D, tn), lambda j: (j // per, 0, j % per))],
        out_specs=pl.BlockSpec((S, tn), lambda j: (0, j)),
        out_shape=jax.ShapeDtypeStruct((S, DIN), F32),
        compiler_params=_params(1),
    )(xb, win_g)


def _bias_tiles(rb_ref, bucket, bias_scr):
    qi = lax.broadcasted_iota(jnp.int32, (128, 256), 0)
    kj = lax.broadcasted_iota(jnp.int32, (128, 256), 1)
    steps = 128 + qi - kj
    band = (steps >= 0) & (steps <= 128)
    bias_scr[...] = jnp.zeros_like(bias_scr)

    def one_bucket(t, carry):
        hit = bucket == t
        for h in range(NH):
            bias_scr[h] = jnp.where(hit, rb_ref[t, h], bias_scr[h])
        return carry

    lax.fori_loop(0, N_BUCKETS, one_bucket, 0)
    for h in range(NH):
        bias_scr[h] = jnp.where(band, bias_scr[h], NEG_INF)


def _attn_fwd(qkv, rel_bias, dilation, name):
    nblk = NBLK // dilation
    bucket = jnp.asarray(_bucket_tile(dilation))

    def body(rb_ref, bucket_ref, q_ref, kc_ref, kp_ref, vc_ref, vp_ref, o_ref, lse_ref, bias_scr):
        b = pl.program_id(0)

        @pl.when(b == 0)
        def _():
            _bias_tiles(rb_ref, bucket_ref[...], bias_scr)

        has_prev = (b % nblk) != 0
        kj = lax.broadcasted_iota(jnp.int32, (128, 256), 1)
        key_ok = (kj >= 128) | has_prev
        for h in range(NH):
            cols = slice(h * HD, (h + 1) * HD)
            q = q_ref[:, cols].astype(BF16)
            s = jnp.concatenate([_dot_nt(q, kp_ref[:, cols].astype(BF16)),
                                 _dot_nt(q, kc_ref[:, cols].astype(BF16))], axis=1) * SCALE
            s = jnp.where(key_ok, s + bias_scr[h], NEG_INF)
            m = jnp.max(s, axis=1, keepdims=True)
            p = jnp.exp(s - m)
            den = jnp.sum(p, axis=1, keepdims=True)
            pb = p.astype(BF16)
            o = _dot(pb[:, :128], vp_ref[:, cols].astype(BF16)) + _dot(pb[:, 128:], vc_ref[:, cols].astype(BF16))
            o_ref[:, cols] = o / den
            lse_ref[:, cols] = jnp.broadcast_to(m + jnp.log(den), (128, HD))

    blk = lambda col, prev: pl.BlockSpec(
        (128, DA), (lambda b: (jnp.maximum(b - 1, 0), col)) if prev else (lambda b: (b, col)))
    return pl.pallas_call(
        body, name=name, grid=(NBLK,),
        in_specs=[pl.BlockSpec(memory_space=pltpu.SMEM),
                  pl.BlockSpec((128, 256), lambda b: (0, 0)),
                  blk(0, False), blk(1, False), blk(1, True), blk(2, False), blk(2, True)],
        out_specs=[pl.BlockSpec((128, DA), lambda b: (b, 0)), pl.BlockSpec((128, DA), lambda b: (b, 0))],
        out_shape=[jax.ShapeDtypeStruct((S, DA), F32), jax.ShapeDtypeStruct((S, DA), F32)],
        scratch_shapes=[pltpu.VMEM((NH, 128, 256), F32)],
        compiler_params=_params(1),
    )(rel_bias, bucket, qkv, qkv, qkv, qkv, qkv)


def _attn_combine(outs, lses):
    tm = 256

    def body(o1, o2, o3, l1, l2, l3, attn_ref, lse_ref):
        a, b, c = l1[...], l2[...], l3[...]
        m = jnp.maximum(jnp.maximum(a, b), c)
        wa, wb, wc = jnp.exp(a - m), jnp.exp(b - m), jnp.exp(c - m)
        den = wa + wb + wc
        attn_ref[...] = ((wa * o1[...] + wb * o2[...] + wc * o3[...]) / den).astype(BF16)
        lse_ref[...] = m + jnp.log(den)

    spec = pl.BlockSpec((tm, DA), lambda i: (i, 0))
    return pl.pallas_call(
        body, name="attn_combine", grid=(S // tm,),
        in_specs=[spec] * 6, out_specs=[spec, spec],
        out_shape=[jax.ShapeDtypeStruct((S, DA), BF16), jax.ShapeDtypeStruct((S, DA), F32)],
        compiler_params=_params(1),
    )(*outs, *lses)


def _gmlp_parts(u_ref, vb_ref, g_ref, be_ref):
    u = u_ref[...]
    u_act, tu = _gelu(u)
    vb = vb_ref[...]
    gv, tv = _gelu(vb)
    mean = jnp.mean(gv, axis=1, keepdims=True)
    cen = gv - mean
    var = jnp.mean(cen * cen, axis=1, keepdims=True)
    rstd = lax.rsqrt(var + LN_EPS)
    xhat = cen * rstd
    vn = xhat * g_ref[...] + be_ref[...]
    return u, tu, u_act, vb, tv, rstd, xhat, vn


def _gmlp_fwd(proj, ws, bsp_b, gain_v, bias_v):
    def body(u_ref, vb_ref, ws_ref, bsp_ref, g_ref, be_ref, o_ref):
        _, _, u_act, _, _, _, _, vn = _gmlp_parts(u_ref, vb_ref, g_ref, be_ref)
        row = lax.broadcasted_iota(jnp.int32, (128, 128), 0)
        col = lax.broadcasted_iota(jnp.int32, (128, 128), 1)
        causal = row >= col
        for g in range(NH):
            cols = slice(g * 128, (g + 1) * 128)
            wsg = jnp.where(causal, ws_ref[g], 0.0).astype(BF16)
            z = _dot(wsg, vn[:, cols].astype(BF16)) + bsp_ref[g]
            o_ref[:, cols] = (u_act[:, cols] * z).astype(BF16)

    return pl.pallas_call(
        body, name="gmlp_fwd", grid=(NBLK,),
        in_specs=[pl.BlockSpec((128, DB), lambda c: (c, 3)), pl.BlockSpec((128, DB), lambda c: (c, 4)),
                  pl.BlockSpec((NH, 128, 128), lambda c: (0, 0, 0)), pl.BlockSpec((NH, 128, 128), lambda c: (0, 0, 0)),
                  pl.BlockSpec((1, DB), lambda c: (0, 0)), pl.BlockSpec((1, DB), lambda c: (0, 0))],
        out_specs=pl.BlockSpec((128, DB), lambda c: (c, 0)),
        out_shape=jax.ShapeDtypeStruct((S, DB), BF16),
        compiler_params=_params(1),
    )(proj, proj, ws, bsp_b, gain_v, bias_v)


def _branch(attn, gmlp, wpa_g, wpb_g, proj):
    tn = 512

    def body(a_ref, g_ref, wa_ref, wb_ref, ga_ref, gb_ref, ya_ref, yb_ref, mg_ref):
        ya = _dot(a_ref[...], wa_ref[...])
        yb = _dot(g_ref[...], wb_ref[...])
        ya_ref[...] = ya.astype(BF16)
        yb_ref[...] = yb.astype(BF16)
        mg_ref[...] = (_sigmoid(ga_ref[...]) * ya + _sigmoid(gb_ref[...]) * yb).astype(BF16)

    out = pl.BlockSpec((S, tn), lambda j: (0, j))
    return pl.pallas_call(
        body, name="branch", grid=(D // tn,),
        in_specs=[pl.BlockSpec((S, DA), lambda j: (0, 0)), pl.BlockSpec((S, DB), lambda j: (0, 0)),
                  pl.BlockSpec((None, DA, tn), lambda j: (j, 0, 0)), pl.BlockSpec((None, DB, tn), lambda j: (j, 0, 0)),
                  pl.BlockSpec((S, tn), lambda j: (0, 5120 // tn + j)), pl.BlockSpec((S, tn), lambda j: (0, 7168 // tn + j))],
        out_specs=[out, out, out],
        out_shape=[jax.ShapeDtypeStruct((S, D), BF16)] * 3,
        compiler_params=_params(1),
    )(attn, gmlp, wpa_g, wpb_g, proj, proj)


def _out_ln1(merged, wout_g, x, gain, bias):
    tm = 256

    def body(m_ref, w_ref, x_ref, g_ref, b_ref, xh_ref, rs_ref, h_ref):
        pre = ALPHA * x_ref[...] + _dot(m_ref[...], w_ref[...])
        mean = jnp.mean(pre, axis=1, keepdims=True)
        cen = pre - mean
        var = jnp.mean(cen * cen, axis=1, keepdims=True)
        rstd = lax.rsqrt(var + LN_EPS)
        xhat = cen * rstd
        xh_ref[...] = xhat
        rs_ref[...] = jnp.broadcast_to(rstd, (tm, 128))
        h_ref[...] = (xhat * g_ref[...] + b_ref[...]).astype(BF16)

    row = pl.BlockSpec((tm, D), lambda i: (i, 0))
    vec = pl.BlockSpec((1, D), lambda i: (0, 0))
    return pl.pallas_call(
        body, name="out_ln1", grid=(S // tm,),
        in_specs=[row, pl.BlockSpec((D, D), lambda i: (0, 0)), row, vec, vec],
        out_specs=[row, pl.BlockSpec((tm, 128), lambda i: (i, 0)), row],
        out_shape=[jax.ShapeDtypeStruct((S, D), F32), jax.ShapeDtypeStruct((S, 128), F32),
                   jax.ShapeDtypeStruct((S, D), BF16)],
        compiler_params=_params(1),
    )(merged, wout_g, x, gain, bias)


def _ff1(h1b, w1_g, b1):
    tn = 512
    per = D // tn

    def body(h_ref, w_ref, b_ref, a_ref, r_ref):
        r = jnp.maximum(_dot(h_ref[...], w_ref[...]) + b_ref[...], 0.0)
        r_ref[...] = r.astype(BF16)
        a_ref[...] = (r * r).astype(BF16)

    out = pl.BlockSpec((S, tn), lambda j: (0, j))
    return pl.pallas_call(
        body, name="ff1", grid=(DFF // tn,),
        in_specs=[pl.BlockSpec((S, D), lambda j: (0, 0)),
                  pl.BlockSpec((None, D, tn), lambda j: (j // per, 0, j % per)),
                  pl.BlockSpec((1, tn), lambda j: (0, j))],
        out_specs=[out, out],
        out_shape=[jax.ShapeDtypeStruct((S, DFF), BF16)] * 2,
        compiler_params=_params(1),
    )(h1b, w1_g, b1)


def _ff2_ln2_loss(a, w2_g, xhat1, g1, b1, b2, g2, be2, target):
    tm, tk = 256, 1024
    nk = DFF // tk

    def body(a_ref, w_ref, xh_ref, g1_ref, b1_ref, b2_ref, g2_ref, be2_ref, t_ref, d_ref, db_ref, st_ref, acc):
        i, k = pl.program_id(0), pl.program_id(1)

        @pl.when(k == 0)
        def _():
            acc[...] = jnp.zeros_like(acc)

        @pl.when((i == 0) & (k == 0))
        def _():
            st_ref[...] = jnp.zeros_like(st_ref)

        acc[...] += _dot(a_ref[...], w_ref[...])

        @pl.when(k == nk - 1)
        def _():
            h1 = xh_ref[...] * g1_ref[...] + b1_ref[...]
            pre = ALPHA * h1 + acc[...] + b2_ref[...]
            mean = jnp.mean(pre, axis=1, keepdims=True)
            cen = pre - mean
            var = jnp.mean(cen * cen, axis=1, keepdims=True)
            rstd = lax.rsqrt(var + LN_EPS)
            xhat = cen * rstd
            y = xhat * g2_ref[...] + be2_ref[...]
            err = y - t_ref[...]
            dy = err * (1.0 / D)
            g = dy * g2_ref[...]
            dpre = rstd * (g - jnp.mean(g, axis=1, keepdims=True)
                           - xhat * jnp.mean(g * xhat, axis=1, keepdims=True))
            d_ref[...] = dpre
            db_ref[...] = dpre.astype(BF16)
            st_ref[0:1, :] += jnp.sum(dy * xhat, axis=0, keepdims=True)
            st_ref[1:2, :] += jnp.sum(dy, axis=0, keepdims=True)
            st_ref[2:3, :] += jnp.sum(dpre, axis=0, keepdims=True)
            st_ref[3:4, :] += jnp.broadcast_to(jnp.sum(err * err).reshape(1, 1), (1, D))

    row = pl.BlockSpec((tm, D), lambda i, k: (i, 0))
    vec = pl.BlockSpec((1, D), lambda i, k: (0, 0))
    return pl.pallas_call(
        body, name="ff2_ln2_loss", grid=(S // tm, nk),
        in_specs=[pl.BlockSpec((tm, tk), lambda i, k: (i, k)), pl.BlockSpec((tk, D), lambda i, k: (k, 0)),
                  row, vec, vec, vec, vec, vec, row],
        out_specs=[row, row, pl.BlockSpec((8, D), lambda i, k: (0, 0))],
        out_shape=[jax.ShapeDtypeStruct((S, D), F32), jax.ShapeDtypeStruct((S, D), BF16),
                   jax.ShapeDtypeStruct((8, D), F32)],
        scratch_shapes=[pltpu.VMEM((tm, D), F32)],
        compiler_params=_params(2),
    )(a, w2_g, xhat1, g1, b1, b2, g2, be2, target)


def _grad_w(act, dout, name, ti, tj, sharded):
    m, n = act.shape[1], dout.shape[1]
    ns = n // N_CHIPS
    per = ns // tj if sharded else None

    def body(a_ref, b_ref, o_ref, at_scr):
        @pl.when(pl.program_id(1) == 0)
        def _():
            at_scr[...] = a_ref[...].astype(F32).T.astype(BF16)

        o_ref[...] = _dot(at_scr[...], b_ref[...])

    if sharded:
        out_spec = pl.BlockSpec((None, ti, tj), lambda i, j: (j // per, i, j % per))
        out_shape = jax.ShapeDtypeStruct((N_CHIPS, m, ns), F32)
    else:
        out_spec = pl.BlockSpec((ti, tj), lambda i, j: (i, j))
        out_shape = jax.ShapeDtypeStruct((m, n), F32)
    return pl.pallas_call(
        body, name=name, grid=(m // ti, n // tj),
        in_specs=[pl.BlockSpec((S, ti), lambda i, j: (0, i)), pl.BlockSpec((S, tj), lambda i, j: (0, j))],
        out_specs=out_spec, out_shape=out_shape,
        scratch_shapes=[pltpu.VMEM((ti, S), BF16)],
        compiler_params=_params(2),
    )(act, dout)


def _d_ff1(dpre2b, w2_g, r):
    tn = 512

    def body(d_ref, w_ref, r_ref, o_ref, gb_ref):
        da = _dot_nt(d_ref[...], w_ref[...])
        dp = da * (2.0 * r_ref[...].astype(F32))
        o_ref[...] = dp.astype(BF16)
        gb_ref[...] = jnp.sum(dp, axis=0, keepdims=True)

    return pl.pallas_call(
        body, name="d_ff1", grid=(DFF // tn,),
        in_specs=[pl.BlockSpec((S, D), lambda j: (0, 0)), pl.BlockSpec((tn, D), lambda j: (j, 0)),
                  pl.BlockSpec((S, tn), lambda j: (0, j))],
        out_specs=[pl.BlockSpec((S, tn), lambda j: (0, j)), pl.BlockSpec((1, tn), lambda j: (0, j))],
        out_shape=[jax.ShapeDtypeStruct((S, DFF), BF16), jax.ShapeDtypeStruct((1, DFF), F32)],
        compiler_params=_params(1),
    )(dpre2b, w2_g, r)


def _d_h1_ln1(dprea, w1_g, dpre2, xhat1, rstd1, g1):
    tm, tk = 256, 512
    per = D // tk
    nk = DFF // tk

    def body(a_ref, w_ref, d2_ref, xh_ref, rs_ref, g_ref, d_ref, db_ref, st_ref, acc):
        i, k = pl.program_id(0), pl.program_id(1)

        @pl.when(k == 0)
        def _():
            acc[...] = jnp.zeros_like(acc)

        @pl.when((i == 0) & (k == 0))
        def _():
            st_ref[...] = jnp.zeros_like(st_ref)

        acc[...] += _dot_nt(a_ref[...], w_ref[...])

        @pl.when(k == nk - 1)
        def _():
            dh = ALPHA * d2_ref[...] + acc[...]
            xhat = xh_ref[...]
            g = dh * g_ref[...]
            dpre = rs_ref[:, 0:1] * (g - jnp.mean(g, axis=1, keepdims=True)
                                     - xhat * jnp.mean(g * xhat, axis=1, keepdims=True))
            d_ref[...] = dpre
            db_ref[...] = dpre.astype(BF16)
            st_ref[0:1, :] += jnp.sum(dh * xhat, axis=0, keepdims=True)
            st_ref[1:2, :] += jnp.sum(dh, axis=0, keepdims=True)

    row = pl.BlockSpec((tm, D), lambda i, k: (i, 0))
    return pl.pallas_call(
        body, name="d_h1_ln1", grid=(S // tm, nk),
        in_specs=[pl.BlockSpec((tm, tk), lambda i, k: (i, k)),
                  pl.BlockSpec((None, D, tk), lambda i, k: (k // per, 0, k % per)),
                  row, row, pl.BlockSpec((tm, 128), lambda i, k: (i, 0)), pl.BlockSpec((1, D), lambda i, k: (0, 0))],
        out_specs=[row, row, pl.BlockSpec((8, D), lambda i, k: (0, 0))],
        out_shape=[jax.ShapeDtypeStruct((S, D), F32), jax.ShapeDtypeStruct((S, D), BF16),
                   jax.ShapeDtypeStruct((8, D), F32)],
        scratch_shapes=[pltpu.VMEM((tm, D), F32)],
        compiler_params=_params(2),
    )(dprea, w1_g, dpre2, xhat1, rstd1, g1)


def _d_merged(dpre1b, wout_g, proj, ya, yb):
    tm, tn = 512, 1024

    def body(d_ref, w_ref, ga_ref, gb_ref, ya_ref, yb_ref, dya_ref, dyb_ref, dga_ref, dgb_ref):
        dm = _dot_nt(d_ref[...], w_ref[...])
        sa = _sigmoid(ga_ref[...])
        sb = _sigmoid(gb_ref[...])
        dya_ref[...] = (dm * sa).astype(BF16)
        dyb_ref[...] = (dm * sb).astype(BF16)
        dga_ref[...] = (dm * ya_ref[...].astype(F32) * sa * (1.0 - sa)).astype(BF16)
        dgb_ref[...] = (dm * yb_ref[...].astype(F32) * sb * (1.0 - sb)).astype(BF16)

    tile = pl.BlockSpec((tm, tn), lambda i, j: (i, j))
    return pl.pallas_call(
        body, name="d_merged", grid=(S // tm, D // tn),
        in_specs=[pl.BlockSpec((tm, D), lambda i, j: (i, 0)), pl.BlockSpec((tn, D), lambda i, j: (j, 0)),
                  pl.BlockSpec((tm, tn), lambda i, j: (i, 5 + j)), pl.BlockSpec((tm, tn), lambda i, j: (i, 7 + j)),
                  tile, tile],
        out_specs=[tile] * 4,
        out_shape=[jax.ShapeDtypeStruct((S, D), BF16)] * 4,
        compiler_params=_params(2),
    )(dpre1b, wout_g, proj, proj, ya, yb)


def _d_branches(dya, dyb, wpa_g, wpb_g):
    tk = 512

    def body(da_ref, db_ref, wa_ref, wb_ref, oa_ref, ob_ref):
        @pl.when(pl.program_id(0) == 0)
        def _():
            oa_ref[...] = jnp.zeros_like(oa_ref)
            ob_ref[...] = jnp.zeros_like(ob_ref)

        oa_ref[...] += _dot_nt(da_ref[...], wa_ref[...])
        ob_ref[...] += _dot_nt(db_ref[...], wb_ref[...])

    return pl.pallas_call(
        body, name="d_branches", grid=(D // tk,),
        in_specs=[pl.BlockSpec((S, tk), lambda k: (0, k)), pl.BlockSpec((S, tk), lambda k: (0, k)),
                  pl.BlockSpec((None, DA, tk), lambda k: (k, 0, 0)), pl.BlockSpec((None, DB, tk), lambda k: (k, 0, 0))],
        out_specs=[pl.BlockSpec((S, DA), lambda k: (0, 0)), pl.BlockSpec((S, DB), lambda k: (0, 0))],
        out_shape=[jax.ShapeDtypeStruct((S, DA), F32), jax.ShapeDtypeStruct((S, DB), F32)],
        compiler_params=_params(1),
    )(dya, dyb, wpa_g, wpb_g)


def _gmlp_bwd(proj, dgmlp, ws, ws_t, bsp_b, gain_v, bias_v):
    def body(u_ref, vb_ref, dg_ref, ws_ref, wst_ref, bsp_ref, g_ref, be_ref, duv_ref, gws_ref, gbs_ref, st_ref):
        @pl.when(pl.program_id(0) == 0)
        def _():
            gws_ref[...] = jnp.zeros_like(gws_ref)
            gbs_ref[...] = jnp.zeros_like(gbs_ref)
            st_ref[...] = jnp.zeros_like(st_ref)

        u, tu, u_act, vb, tv, rstd, xhat, vn = _gmlp_parts(u_ref, vb_ref, g_ref, be_ref)
        dg = dg_ref[...]
        dz = dg * u_act
        row = lax.broadcasted_iota(jnp.int32, (128, 128), 0)
        col = lax.broadcasted_iota(jnp.int32, (128, 128), 1)
        causal = row >= col
        causal_t = row <= col
        dvn_parts = []
        z_parts = []
        for g in range(NH):
            cols = slice(g * 128, (g + 1) * 128)
            vng = vn[:, cols].astype(BF16)
            dzg = dz[:, cols]
            dzb = dzg.astype(BF16)
            wsg = jnp.where(causal, ws_ref[g], 0.0).astype(BF16)
            wsg_t = jnp.where(causal_t, wst_ref[g], 0.0).astype(BF16)
            z_parts.append(_dot(wsg, vng) + bsp_ref[g])
            gws_ref[g] += jnp.where(causal, _dot_nt(dzb, vng), 0.0)
            gbs_ref[g] += jnp.broadcast_to(jnp.sum(dzg, axis=1, keepdims=True), (128, 128))
            dvn_parts.append(_dot(wsg_t, dzb))
        z = jnp.concatenate(z_parts, axis=1)
        dvn = jnp.concatenate(dvn_parts, axis=1)
        du = dg * z * _gelu_grad(u, tu)
        st_ref[0:1, :] += jnp.sum(dvn * xhat, axis=0, keepdims=True)
        st_ref[1:2, :] += jnp.sum(dvn, axis=0, keepdims=True)
        gg = dvn * g_ref[...]
        dgv = rstd * (gg - jnp.mean(gg, axis=1, keepdims=True) - xhat * jnp.mean(gg * xhat, axis=1, keepdims=True))
        dvb = dgv * _gelu_grad(vb, tv)
        duv_ref[:, 0:DB] = du.astype(BF16)
        duv_ref[:, DB:2 * DB] = dvb.astype(BF16)

    full3 = pl.BlockSpec((NH, 128, 128), lambda c: (0, 0, 0))
    vec = pl.BlockSpec((1, DB), lambda c: (0, 0))
    return pl.pallas_call(
        body, name="gmlp_bwd", grid=(NBLK,),
        in_specs=[pl.BlockSpec((128, DB), lambda c: (c, 3)), pl.BlockSpec((128, DB), lambda c: (c, 4)),
                  pl.BlockSpec((128, DB), lambda c: (c, 0)), full3, full3, full3, vec, vec],
        out_specs=[pl.BlockSpec((128, 2 * DB), lambda c: (c, 0)), full3, full3, pl.BlockSpec((8, DB), lambda c: (0, 0))],
        out_shape=[jax.ShapeDtypeStruct((S, 2 * DB), BF16), jax.ShapeDtypeStruct((NH, 128, 128), F32),
                   jax.ShapeDtypeStruct((NH, 128, 128), F32), jax.ShapeDtypeStruct((8, DB), F32)],
        compiler_params=_params(1),
    )(proj, proj, dgmlp, ws, ws_t, bsp_b, gain_v, bias_v)


def _attn_delta(dattn, attn):
    tm = 256

    def body(d_ref, o_ref, dl_ref, db_ref):
        d = d_ref[...]
        prod = d * o_ref[...].astype(F32)
        for h in range(NH):
            cols = slice(h * HD, (h + 1) * HD)
            dl_ref[:, cols] = jnp.broadcast_to(jnp.sum(prod[:, cols], axis=1, keepdims=True), (tm, HD))
        db_ref[...] = d.astype(BF16)

    spec = pl.BlockSpec((tm, DA), lambda i: (i, 0))
    return pl.pallas_call(
        body, name="attn_delta", grid=(S // tm,),
        in_specs=[spec, spec], out_specs=[spec, spec],
        out_shape=[jax.ShapeDtypeStruct((S, DA), F32), jax.ShapeDtypeStruct((S, DA), BF16)],
        compiler_params=_params(1),
    )(dattn, attn)


def _attn_bwd(qkv, dob, lse, delta, rel_bias, dilation, name):
    nblk = NBLK // dilation
    bucket = jnp.asarray(_bucket_tile(dilation))

    def body(rb_ref, bucket_ref, q_ref, qn_ref, kc_ref, kp_ref, vc_ref, vp_ref, do_ref, don_ref,
             l_ref, ln_ref, dl_ref, dln_ref, dq_ref, dk_ref, dv_ref, ds_ref, bias_scr):
        b = pl.program_id(0)

        @pl.when(b == 0)
        def _():
            _bias_tiles(rb_ref, bucket_ref[...], bias_scr)
            ds_ref[...] = jnp.zeros_like(ds_ref)

        has_prev = (b % nblk) != 0
        has_next = ((b + 1) % nblk) != 0
        for h in range(NH):
            cols = slice(h * HD, (h + 1) * HD)
            q = q_ref[:, cols].astype(BF16)
            kc = kc_ref[:, cols].astype(BF16)
            kp = kp_ref[:, cols].astype(BF16)
            vc = vc_ref[:, cols].astype(BF16)
            vp = vp_ref[:, cols].astype(BF16)
            do = do_ref[:, cols]
            bias_p = bias_scr[h, :, 0:128]
            bias_c = bias_scr[h, :, 128:256]
            lse_b = l_ref[:, cols]
            dl_b = dl_ref[:, cols]
            p_c = jnp.exp(_dot_nt(q, kc) * SCALE + bias_c - lse_b)
            p_p = jnp.where(has_prev, jnp.exp(_dot_nt(q, kp) * SCALE + bias_p - lse_b), 0.0)
            ds_c = p_c * (_dot_nt(do, vc) - dl_b)
            ds_p = p_p * (_dot_nt(do, vp) - dl_b)
            ds_ref[h, :, 0:128] += ds_p
            ds_ref[h, :, 128:256] += ds_c
            ds_cb = ds_c.astype(BF16)
            dq_ref[:, cols] = (_dot(ds_cb, kc) + _dot(ds_p.astype(BF16), kp)) * SCALE
            qn = qn_ref[:, cols].astype(BF16)
            don = don_ref[:, cols]
            p_n = jnp.where(has_next, jnp.exp(_dot_nt(qn, kc) * SCALE + bias_p - ln_ref[:, cols]), 0.0)
            ds_n = p_n * (_dot_nt(don, vc) - dln_ref[:, cols])
            dk_ref[:, cols] = (_dot(ds_c.T.astype(BF16), q) + _dot(ds_n.T.astype(BF16), qn)) * SCALE
            dv_ref[:, cols] = _dot(p_c.T.astype(BF16), do) + _dot(p_n.T.astype(BF16), don)

    cur = lambda col: pl.BlockSpec((128, DA), lambda b: (b, col))
    prev = lambda col: pl.BlockSpec((128, DA), lambda b: (jnp.maximum(b - 1, 0), col))
    nxt = lambda col: pl.BlockSpec((128, DA), lambda b: (jnp.minimum(b + 1, NBLK - 1), col))
    return pl.pallas_call(
        body, name=name, grid=(NBLK,),
        in_specs=[pl.BlockSpec(memory_space=pltpu.SMEM), pl.BlockSpec((128, 256), lambda b: (0, 0)),
                  cur(0), nxt(0), cur(1), prev(1), cur(2), prev(2), cur(0), nxt(0), cur(0), nxt(0), cur(0), nxt(0)],
        out_specs=[cur(0), cur(0), cur(0), pl.BlockSpec((NH, 128, 256), lambda b: (0, 0, 0))],
        out_shape=[jax.ShapeDtypeStruct((S, DA), F32)] * 3 + [jax.ShapeDtypeStruct((NH, 128, 256), F32)],
        scratch_shapes=[pltpu.VMEM((NH, 128, 256), F32)],
        compiler_params=_params(1),
    )(rel_bias, bucket, qkv, qkv, qkv, qkv, qkv, qkv, dob, dob, lse, lse, delta, delta)


def _sum3_bf16(a, b, c):
    tm = 256
    n = a.shape[1]

    def body(a_ref, b_ref, c_ref, o_ref):
        o_ref[...] = (a_ref[...] + b_ref[...] + c_ref[...]).astype(BF16)

    spec = pl.BlockSpec((tm, n), lambda i: (i, 0))
    return pl.pallas_call(
        body, name="dqkv_sum", grid=(S // tm,), in_specs=[spec] * 3, out_specs=spec,
        out_shape=jax.ShapeDtypeStruct((S, n), BF16), compiler_params=_params(1),
    )(a, b, c)


def _rel_bias_grad(ds_sums):
    buckets = jnp.asarray(np.stack([_bucket_tile(d) for _, d in PATTERNS]))

    def body(bk_ref, d1, d2, d3, o_ref):
        row = lax.broadcasted_iota(jnp.int32, (N_BUCKETS, 128), 0)
        lane = lax.broadcasted_iota(jnp.int32, (N_BUCKETS, 128), 1)

        def one_bucket(t, out):
            hits = [bk_ref[p] == t for p in range(3)]
            for h in range(NH):
                tot = jnp.zeros((128, 256), F32)
                for p, d in enumerate((d1, d2, d3)):
                    tot = tot + jnp.where(hits[p], d[h], 0.0)
                out = jnp.where((row == t) & (lane == h), jnp.sum(tot), out)
            return out

        o_ref[...] = lax.fori_loop(0, N_BUCKETS, one_bucket, jnp.zeros((N_BUCKETS, 128), F32))

    return pl.pallas_call(
        body, name="rel_bias_grad",
        in_specs=[pl.BlockSpec(memory_space=pltpu.VMEM)] * 4, out_specs=pl.BlockSpec(memory_space=pltpu.VMEM),
        out_shape=jax.ShapeDtypeStruct((N_BUCKETS, 128), F32),
        compiler_params=pltpu.CompilerParams(vmem_limit_bytes=VMEM_LIMIT),
    )(buckets, *ds_sums)


def _d_x(dproj, win_g, dpre1):
    tm, tk = 512, 768
    per = 2304 // tk
    nk = DIN // tk

    def body(a_ref, w_ref, d_ref, o_ref, acc):
        k = pl.program_id(1)

        @pl.when(k == 0)
        def _():
            acc[...] = ALPHA * d_ref[...]

        acc[...] += _dot_nt(a_ref[...], w_ref[...])

        @pl.when(k == nk - 1)
        def _():
            o_ref[...] = acc[...]

    row = pl.BlockSpec((tm, D), lambda i, k: (i, 0))
    return pl.pallas_call(
        body, name="d_x", grid=(S // tm, nk),
        in_specs=[pl.BlockSpec((tm, tk), lambda i, k: (i, k)),
                  pl.BlockSpec((None, D, tk), lambda i, k: (k // per, 0, k % per)), row],
        out_specs=row, out_shape=jax.ShapeDtypeStruct((S, D), F32),
        scratch_shapes=[pltpu.VMEM((tm, D), F32)],
        compiler_params=_params(2),
    )(dproj, win_g, dpre1)


def _adamw(w, g, m, v, name):
    rows, cols = w.shape
    tm = max(t for t in range(8, 257, 8) if rows % t == 0)

    def body(w_ref, g_ref, m_ref, v_ref, d_ref, nm_ref, nv_ref):
        g = g_ref[...]
        m = ADAM_B1 * m_ref[...] + (1.0 - ADAM_B1) * g
        v = ADAM_B2 * v_ref[...] + (1.0 - ADAM_B2) * (g * g)
        m_hat = m / (1.0 - ADAM_B1 ** ADAM_STEP)
        v_hat = v / (1.0 - ADAM_B2 ** ADAM_STEP)
        d_ref[...] = -ADAM_LR * (m_hat / (jnp.sqrt(v_hat) + ADAM_EPS) + ADAM_WD * w_ref[...])
        nm_ref[...] = m
        nv_ref[...] = v

    spec = pl.BlockSpec((tm, cols), lambda i: (i, 0))
    return pl.pallas_call(
        body, name=name, grid=(rows // tm,), in_specs=[spec] * 4, out_specs=[spec] * 3,
        out_shape=[jax.ShapeDtypeStruct((rows, cols), F32)] * 3, compiler_params=_params(1),
    )(w, g, m, v)


def _position():
    x, y, c = lax.axis_index("x"), lax.axis_index("y"), lax.axis_index("c")
    chips = [(1 - x, y), (x, 1 - y), (1 - x, 1 - y)]
    return x, y, c, chips


def _remote(src, dst, send_sems, recv_sems, k, to):
    return pltpu.make_async_remote_copy(src_ref=src, dst_ref=dst, send_sem=send_sems.at[k], recv_sem=recv_sems.at[k],
                                        device_id=to, device_id_type=MESH)


def _place_shard(w, name):
    rows, cols = w.shape
    tm = 256
    x, y = lax.axis_index("x"), lax.axis_index("y")

    def body(chip_ref, w_ref, o_ref):
        o_ref[...] = w_ref[...].astype(BF16)

    return pl.pallas_call(
        body, name=name,
        grid_spec=pltpu.PrefetchScalarGridSpec(
            num_scalar_prefetch=1, grid=(rows // tm,),
            in_specs=[pl.BlockSpec((tm, cols), lambda i, chip: (i, 0))],
            out_specs=pl.BlockSpec((None, tm, cols), lambda i, chip: (chip[0], i, 0))),
        out_shape=jax.ShapeDtypeStruct((N_CHIPS, rows, cols), BF16),
        compiler_params=_params(1),
    )(jnp.reshape(2 * x + y, (1,)).astype(jnp.int32), w)


HBM =pl.BlockSpec(memory_space=pltpu.HBM)
SEM = pl.BlockSpec(memory_space=pltpu.SEMAPHORE)
EFFECT = pltpu.SideEffectType.DATAFLOW_SIDE_EFFECTING


def _comm_call(name, body, bufs, sems_in, sems_out, after=None, token=False):
    nb, ns, no = len(bufs), len(sems_in), len(sems_out)
    n_in = nb + ns + (after is not None)

    def wrapped(*refs):
        body(refs[:nb], refs[nb:nb + ns], refs[n_in + nb:n_in + nb + no])
        if token:
            refs[-1][...] = jnp.zeros((8, 128), F32)

    outs = pl.pallas_call(
        wrapped, name=name,
        in_specs=[HBM] * nb + [SEM] * ns + ([ANY] if after is not None else []),
        out_specs=[HBM] * nb + [SEM] * no + ([pl.BlockSpec(memory_space=pltpu.VMEM)] if token else []),
        out_shape=[pltpu.HBM(b.shape, b.dtype) for b in bufs] + [pltpu.SemaphoreType.DMA((k,)) for k in sems_out]
        + ([jax.ShapeDtypeStruct((8, 128), F32)] if token else []),
        input_output_aliases={i: i for i in range(nb)},
        compiler_params=pltpu.CompilerParams(has_side_effects=EFFECT),
    )(*[pltpu.with_memory_space_constraint(b, pltpu.HBM) for b in bufs], *sems_in, *([after] if after is not None else []))
    return list(outs[:nb]), list(outs[nb:nb + no]), (outs[-1] if token else None)


def _ag_copies(buf, send_sems, recv_sems, k0, stage):
    x, y, c, chips = _position()
    hr = buf.shape[1] // 2
    half = lambda chip, h: buf.at[chip, pl.ds(h * hr, hr), :]
    sends, arrivals = [], []
    for j, (cx, cy) in enumerate(chips):
        if stage == "ici":
            mine = half(2 * x + y, c)
            sends.append(_remote(mine, mine, send_sems, recv_sems, k0 + j, (cx, cy, c)))
            got = half(2 * cx + cy, c)
        else:
            landed = half(2 * cx + cy, c)
            sends.append(_remote(landed, landed, send_sems, recv_sems, k0 + j, (x, y, 1 - c)))
            got = half(2 * cx + cy, 1 - c)
        arrivals.append(_remote(got, got, send_sems, recv_sems, k0 + j, (x, y, c)))
    return sends, arrivals


def _ag_start(groups):
    flat = [b for g in groups for b in g]

    def body(bufs, _, sems):
        at = 0
        for gi, g in enumerate(groups):
            for wi in range(len(g)):
                for cp in _ag_copies(bufs[at], sems[2 * gi], sems[2 * gi + 1], 3 * wi, "ici")[0]:
                    cp.start()
                at += 1

    bufs, sems, _ = _comm_call("allgather_start", body, flat, [], [3 * len(g) for g in groups for _ in (0, 1)])
    out, at = [], 0
    for gi, g in enumerate(groups):
        out.append((bufs[at:at + len(g)], sems[2 * gi], sems[2 * gi + 1]))
        at += len(g)
    return out


def _ag_step(name, finish, advance, after=None):
    fin_bufs = list(finish[0]) if finish else []
    adv_bufs = list(advance[0]) if advance else []
    nf = len(fin_bufs)

    def body(bufs, sems_in, sems_out):
        if advance:
            ici_s, ici_r = sems_in[-2], sems_in[-1]
            for wi in range(len(adv_bufs)):
                buf = bufs[nf + wi]
                ici_sends, ici_arrivals = _ag_copies(buf, ici_s, ici_r, 3 * wi, "ici")
                d2d_sends, _ = _ag_copies(buf, sems_out[0], sems_out[1], 3 * wi, "d2d")
                for arrived, onward in zip(ici_arrivals, d2d_sends):
                    arrived.wait_recv()
                    onward.start()
                for cp in ici_sends:
                    cp.wait_send()
        if finish:
            for wi in range(nf):
                d2d_sends, d2d_arrivals = _ag_copies(bufs[wi], sems_in[0], sems_in[1], 3 * wi, "d2d")
                for cp in d2d_arrivals:
                    cp.wait_recv()
                for cp in d2d_sends:
                    cp.wait_send()

    sems_in = (list(finish[1:]) if finish else []) + (list(advance[1:]) if advance else [])
    bufs, sems, _ = _comm_call(name, body, fin_bufs + adv_bufs, sems_in, [3 * len(adv_bufs)] * 2 if advance else [], after)
    return bufs[:nf], ((bufs[nf:], sems[0], sems[1]) if advance else None)


def _cx_copies(src, dst, send_sems, recv_sems, k0):
    x, y, c, chips = _position()
    sends = [_remote(src.at[2 * cx + cy], dst.at[2 * x + y], send_sems, recv_sems, k0 + j, (cx, cy, c))
             for j, (cx, cy) in enumerate(chips)]
    arrivals = [_remote(dst.at[2 * cx + cy], dst.at[2 * cx + cy], send_sems, recv_sems, k0 + j, (x, y, c))
                for j, (cx, cy) in enumerate(chips)]
    return sends, arrivals


def _cx_start(name, pair_sums):
    n = len(pair_sums)
    landing = [lax.empty(p.shape, p.dtype) for p in pair_sums]

    def body(bufs, _, sems):
        for w in range(n):
            for cp in _cx_copies(bufs[w], bufs[n + w], sems[0], sems[1], 3 * w)[0]:
                cp.start()

    bufs, sems, token = _comm_call(name, body, list(pair_sums) + landing, [], [3 * n, 3 * n], token=True)
    return (bufs, sems), token


def _cx_wait(name, state, after):
    bufs, sems = state
    n = len(bufs) // 2

    def body(refs, sems_in, _):
        for w in range(n):
            sends, arrivals = _cx_copies(refs[w], refs[n + w], sems_in[0], sems_in[1], 3 * w)
            for cp in arrivals:
                cp.wait_recv()
            for cp in sends:
                cp.wait_send()

    bufs, _, _ = _comm_call(name, body, bufs, sems, [], after)
    return bufs[:n], bufs[n:]


def _pair_exchange(grads, name):
    n = len(grads)

    def body(*refs):
        ins, outs = refs[:n], refs[n:2 * n]
        send_sems, recv_sems = refs[2 * n:]
        x, y, c, _ = _position()
        copies = []
        for w in range(n):
            hr = ins[w].shape[1] // 2
            cp = _remote(ins[w].at[:, pl.ds((1 - c) * hr, hr), :], outs[w], send_sems, recv_sems, w, (x, y, 1 - c))
            cp.start()
            copies.append(cp)
        for cp in copies:
            cp.wait()

    return pl.pallas_call(
        body, name=name,
        in_specs=[ANY] * n, out_specs=[ANY] * n,
        out_shape=[jax.ShapeDtypeStruct((N_CHIPS, g.shape[1] // 2, g.shape[2]), F32) for g in grads],
        scratch_shapes=[pltpu.SemaphoreType.DMA((n,)), pltpu.SemaphoreType.DMA((n,))],
    )(*grads)


def _pair_sum(grad, got, name):
    _, rows, cols = grad.shape
    hr = rows // 2
    tm = min(hr, 256)
    nb = hr // tm
    c = lax.axis_index("c")

    def body(c_ref, g_ref, o_ref, out_ref):
        out_ref[...] = (g_ref[...] + o_ref[...]).astype(BF16)

    return pl.pallas_call(
        body, name=name,
        grid_spec=pltpu.PrefetchScalarGridSpec(
            num_scalar_prefetch=1, grid=(N_CHIPS, nb),
            in_specs=[pl.BlockSpec((None, tm, cols), lambda s, i, c_ref: (s, c_ref[0] * nb + i, 0)),
                      pl.BlockSpec((None, tm, cols), lambda s, i, c_ref: (s, i, 0))],
            out_specs=pl.BlockSpec((None, tm, cols), lambda s, i, c_ref: (s, i, 0))),
        out_shape=jax.ShapeDtypeStruct((N_CHIPS, hr, cols), BF16),
        compiler_params=_params(2),
    )(jnp.reshape(c, (1,)).astype(jnp.int32), grad, got)


def _chip_sum(parts, pair_sums, name):
    _, hr, cols = parts.shape
    tm = min(hr, 256)
    nb = hr // tm
    x, y, c = lax.axis_index("x"), lax.axis_index("y"), lax.axis_index("c")

    def body(pos_ref, p_ref, own_ref, o_ref):
        chip = pos_ref[0]
        own = own_ref[...].astype(F32)
        term = lambda s: jnp.where(chip == s, own, p_ref[s].astype(F32))
        o_ref[...] = ((term(0) + term(1)) + term(2)) + term(3)

    return pl.pallas_call(
        body, name=name,
        grid_spec=pltpu.PrefetchScalarGridSpec(
            num_scalar_prefetch=1, grid=(nb,),
            in_specs=[pl.BlockSpec((N_CHIPS, tm, cols), lambda i, pos: (0, i, 0)),
                      pl.BlockSpec((None, tm, cols), lambda i, pos: (pos[0], i, 0))],
            out_specs=pl.BlockSpec((tm, cols), lambda i, pos: (pos[1] * nb + i, 0))),
        out_shape=jax.ShapeDtypeStruct((2 * hr, cols), F32), compiler_params=_params(1),
    )(jnp.stack([2 * x + y, c]).astype(jnp.int32), parts, pair_sums)


def _share_halves(bufs, name):
    n = len(bufs)

    def body(*refs):
        outs = refs[n:2 * n]
        send_sems, recv_sems = refs[2 * n:]
        x, y, c, _ = _position()
        copies = []
        for w in range(n):
            hr = outs[w].shape[0] // 2
            mine = outs[w].at[pl.ds(c * hr, hr), :]
            cp = _remote(mine, mine, send_sems, recv_sems, w, (x, y, 1 - c))
            cp.start()
            copies.append(cp)
        for w in range(n):
            hr = outs[w].shape[0] // 2
            theirs = outs[w].at[pl.ds((1 - c) * hr, hr), :]
            _remote(theirs, theirs, send_sems, recv_sems, w, (x, y, c)).wait_recv()
        for cp in copies:
            cp.wait_send()

    return pl.pallas_call(
        body, name=name,
        in_specs=[ANY] * n, out_specs=[ANY] * n,
        out_shape=[jax.ShapeDtypeStruct(b.shape, b.dtype) for b in bufs],
        input_output_aliases={w: w for w in range(n)},
        scratch_shapes=[pltpu.SemaphoreType.DMA((n,)), pltpu.SemaphoreType.DMA((n,))],
    )(*bufs)


def _allreduce_small(g):
    rows = g.shape[0]

    def body(g_ref, o_ref, sib, slots, send_sems, recv_sems):
        x, y, c, chips = _position()
        me = (x, y, c)
        my_chip = 2 * x + y
        pair = _remote(g_ref, sib, send_sems, recv_sems, 0, (x, y, 1 - c))
        pair.start()
        pair.wait()
        slots[my_chip] = g_ref[...] + sib[...]
        sent = []
        for j, (cx, cy) in enumerate(chips):
            cp = _remote(slots.at[my_chip], slots.at[my_chip], send_sems, recv_sems, 1 + j, (cx, cy, c))
            cp.start()
            sent.append(cp)
        for j, (cx, cy) in enumerate(chips):
            got = slots.at[2 * cx + cy]
            _remote(got, got, send_sems, recv_sems, 1 + j, me).wait_recv()
        for cp in sent:
            cp.wait_send()
        o_ref[...] = ((slots[0] + slots[1]) + slots[2]) + slots[3]

    vm = pl.BlockSpec(memory_space=pltpu.VMEM)
    return pl.pallas_call(
        body, name="allreduce_small",
        in_specs=[vm], out_specs=vm, out_shape=jax.ShapeDtypeStruct((rows, 128), F32),
        scratch_shapes=[pltpu.VMEM((rows, 128), F32), pltpu.VMEM((N_CHIPS, rows, 128), F32),
                        pltpu.SemaphoreType.DMA((4,)), pltpu.SemaphoreType.DMA((4,))],
        compiler_params=pltpu.CompilerParams(vmem_limit_bytes=VMEM_LIMIT),
    )(g)


def _permute(a, d):
    return a if d == 1 else a.reshape(S // d, d, a.shape[1]).transpose(1, 0, 2).reshape(S, a.shape[1])


def _unpermute(a, d):
    return a if d == 1 else a.reshape(d, S // d, a.shape[1]).transpose(1, 0, 2).reshape(S, a.shape[1])


_SMALL = ("rel_bias", "ln_v_gain", "ln_v_bias", "w_spatial", "b_spatial", "ln1_gain", "ln1_bias",
          "b_ff1", "b_ff2", "ln2_gain", "ln2_bias")
_SMALL_ROWS = 1200


def _pack_small(parts):
    flat = jnp.concatenate([parts[k].reshape(-1).astype(F32) for k in _SMALL])
    flat = jnp.pad(flat, (0, _SMALL_ROWS * 128 - flat.shape[0]))
    return flat.reshape(_SMALL_ROWS, 128)


def _unpack_small(packed, like):
    flat = packed.reshape(-1)
    out, at = {}, 0
    for k in _SMALL:
        n = math.prod(like[k].shape)
        out[k] = flat[at:at + n].reshape(like[k].shape)
        at += n
    return out


def kernel(x, w_in, rel_bias, ln_v_gain, ln_v_bias, w_spatial, b_spatial, w_proj_a, w_proj_b, w_out, ln1_gain, ln1_bias, w_ff1, b_ff1, w_ff2, b_ff2, ln2_gain, ln2_bias, loss_target, m_w_in, m_rel_bias, m_ln_v_gain, m_ln_v_bias, m_w_spatial, m_b_spatial, m_w_proj_a, m_w_proj_b, m_w_out, m_ln1_gain, m_ln1_bias, m_w_ff1, m_b_ff1, m_w_ff2, m_b_ff2, m_ln2_gain, m_ln2_bias, v_w_in, v_rel_bias, v_ln_v_gain, v_ln_v_bias, v_w_spatial, v_b_spatial, v_w_proj_a, v_w_proj_b, v_w_out, v_ln1_gain, v_ln1_bias, v_w_ff1, v_b_ff1, v_w_ff2, v_b_ff2, v_ln2_gain, v_ln2_bias):
    args = dict(locals())
    big = ("w_in", "w_proj_a", "w_proj_b", "w_out", "w_ff1", "w_ff2")
    weights = ("w_in", "rel_bias", "ln_v_gain", "ln_v_bias", "w_spatial", "b_spatial", "w_proj_a", "w_proj_b", "w_out",
               "ln1_gain", "ln1_bias", "w_ff1", "b_ff1", "w_ff2", "b_ff2", "ln2_gain", "ln2_bias")

    xs = x[0]
    target = loss_target[0]

    placed = [_place_shard(args[k][0], f"place_{k}") for k in big]
    in_a, in_b, in_c, in_d = _ag_start([placed[0:1], placed[1:4], placed[4:5], placed[5:6]])
    _, d2d_a = _ag_step("allgather_w_in_pass", None, in_a)
    (win_g,), _ = _ag_step("allgather_w_in_done", d2d_a, None)

    xb = xs.astype(BF16)
    proj = _proj(xb, win_g)
    _, d2d_b = _ag_step("allgather_b_pass", None, in_b, after=proj)
    qkv_p = [proj] + [_permute(proj[:, :3 * DA], d) for _, d in PATTERNS[1:]]
    outs, lses = [], []
    for p, (_, d) in enumerate(PATTERNS):
        o, l = _attn_fwd(qkv_p[p], rel_bias, d, f"attn_fwd_{p}")
        outs.append(_unpermute(o, d))
        lses.append(_unpermute(l, d))
    attn, lse = _attn_combine(outs, lses)
    ws = w_spatial[0]
    ws_t = jnp.transpose(ws, (0, 2, 1))
    bsp_b = jnp.broadcast_to(b_spatial[0][:, :, None], (NH, 128, 128))
    gmlp = _gmlp_fwd(proj, ws, bsp_b, ln_v_gain, ln_v_bias)
    (wpa_g, wpb_g, wout_g), d2d_c = _ag_step("allgather_b_done_c_pass", d2d_b, in_c, after=gmlp)
    wout_full = wout_g.reshape(D, D)
    ya, yb, merged = _branch(attn, gmlp, wpa_g, wpb_g, proj)
    xhat1, rstd1, h1b = _out_ln1(merged, wout_full, xs, ln1_gain, ln1_bias)
    (w1_g,), d2d_d = _ag_step("allgather_c_done_d_pass", d2d_c, in_d, after=h1b)
    a, r = _ff1(h1b, w1_g, b_ff1)
    (w2_g,), _ = _ag_step("allgather_d_done", d2d_d, None, after=a)
    w2_full = w2_g.reshape(DFF, D)
    dpre2, dpre2b, st2 = _ff2_ln2_loss(a, w2_full, xhat1, ln1_gain, ln1_bias, b_ff2, ln2_gain, ln2_bias, target)

    def reduce_start(tag, local):
        from_sibling = _pair_exchange(local, f"pair_exchange_{tag}")
        pair_sums = [_pair_sum(g, o, f"pair_sum_{tag}_{i}") for i, (g, o) in enumerate(zip(local, from_sibling))]
        return _cx_start(f"chip_exchange_start_{tag}", pair_sums)

    def behind(value, token):
        return lax.optimization_barrier((value, token))[0]

    g_w2 = _grad_w(a, dpre2b, "grad_w_ff2", 512, 1024, False)
    cx_w2, tok = reduce_start("w_ff2", [g_w2.reshape(N_CHIPS, DFF // N_CHIPS, D)])
    dprea, g_b1 = _d_ff1(behind(dpre2b, tok), w2_full, r)
    g_w1 = _grad_w(h1b, dprea, "grad_w_ff1", 512, 1024, True)
    cx_w1, tok = reduce_start("w_ff1", [g_w1])
    dpre1, dpre1b, st1 = _d_h1_ln1(behind(dprea, tok), w1_g, dpre2, xhat1, rstd1, ln1_gain)
    g_wout = _grad_w(merged, dpre1b, "grad_w_out", 512, 1024, False)
    dya, dyb, dga, dgb = _d_merged(dpre1b, wout_full, proj, ya, yb)
    g_wpa = _grad_w(attn, dya, "grad_w_proj_a", 512, 512, True)
    g_wpb = _grad_w(gmlp, dyb, "grad_w_proj_b", 512, 512, True)
    cx_b, tok = reduce_start("b", [g_wpa, g_wpb, g_wout.reshape(N_CHIPS, D // N_CHIPS, D)])
    dattn, dgmlp = _d_branches(behind(dya, tok), dyb, wpa_g, wpb_g)
    duv, g_ws, g_bs, stv = _gmlp_bwd(proj, dgmlp, ws, ws_t, bsp_b, ln_v_gain, ln_v_bias)
    delta, dob = _attn_delta(dattn, attn)
    dqkv, ds_sums = [], []
    for p, (_, d) in enumerate(PATTERNS):
        dq, dk, dv, ds = _attn_bwd(qkv_p[p], _permute(dob, d), _permute(lse, d), _permute(delta, d), rel_bias, d,
                                   f"attn_bwd_{p}")
        dqkv.append(_unpermute(jnp.concatenate([dq, dk, dv], axis=1), d))
        ds_sums.append(ds)
    dqkv_b = _sum3_bf16(*dqkv)
    g_rb = _rel_bias_grad(ds_sums)[:, :NH]
    dproj = jnp.concatenate([dqkv_b, duv, dga, dgb], axis=1)
    g_win = _grad_w(xb, dproj, "grad_w_in", 512, 768, True)
    cx_in, tok = reduce_start("w_in", [g_win])
    grad_x = _d_x(behind(dproj, tok), win_g, dpre1)

    small_g = dict(rel_bias=g_rb, ln_v_gain=stv[0], ln_v_bias=stv[1], w_spatial=g_ws, b_spatial=g_bs[:, :, 0],
                   ln1_gain=st1[0], ln1_bias=st1[1], b_ff1=g_b1, b_ff2=st2[2], ln2_gain=st2[0], ln2_bias=st2[1])
    gs = _allreduce_small(_pack_small(small_g))
    ds_, ms_, vs_ = _adamw(_pack_small({k: args[k] for k in _SMALL}), gs,
                           _pack_small({k: args["m_" + k] for k in _SMALL}),
                           _pack_small({k: args["v_" + k] for k in _SMALL}), "adamw_small")
    like = {k: args[k] for k in _SMALL}
    grads, deltas, new_m, new_v = (_unpack_small(t, like) for t in (gs, ds_, ms_, vs_))

    def reduce_finish(tag, state, names, after):
        pair_sums, from_chips = _cx_wait(f"chip_exchange_wait_{tag}", state, after)
        halves = [_chip_sum(p, own, f"chip_sum_{k}") for p, own, k in zip(from_chips, pair_sums, names)]
        last = None
        for k, g in zip(names, _share_halves(halves, f"share_halves_{tag}")):
            d_, m_, v_ = _adamw(args[k][0], g, args["m_" + k][0], args["v_" + k][0], f"adamw_{k}")
            grads[k], deltas[k], new_m[k], new_v[k] = g[None], d_[None], m_[None], v_[None]
            last = d_
        return last

    done = reduce_finish("w_ff2", cx_w2, ["w_ff2"], grad_x)
    done = reduce_finish("w_ff1", cx_w1, ["w_ff1"], done)
    done = reduce_finish("b", cx_b, ["w_proj_a", "w_proj_b", "w_out"], done)
    reduce_finish("w_in", cx_in, ["w_in"], done)

    loss = lax.psum(st2[3, 0] * (0.5 / D), ("x", "y", "c"))
    return (loss, grad_x[None], *[grads[k] for k in weights], *[deltas[k] for k in weights],
            *[new_m[k] for k in weights], *[new_v[k] for k in weights])
```

```python
import functools
import math

import numpy as np
import jax
import jax.numpy as jnp
from jax import lax
from jax.experimental import pallas as pl
from jax.experimental.pallas import tpu as pltpu

F32 = jnp.float32
BF16 = jnp.bfloat16

S = 2048
D = 2048
DA = 1024
DB = 1024
DFF = 8192
DIN = 9216
NH = 8
HD = 128
NBLK = 16
PATTERNS = ((128, 1), (512, 4), (2048, 16))
N_BUCKETS = 32
MAX_DISTANCE = 2048
ALPHA = 2.0 ** 0.25
LN_EPS = 1e-5
NEG_INF = -1e30
SCALE = HD ** -0.5
N_CHIPS = 4

ADAM_LR = 0.001
ADAM_B1 = 0.9
ADAM_B2 = 0.999
ADAM_EPS = 1e-08
ADAM_WD = 0.01
ADAM_STEP = 10

VMEM_LIMIT = 56 * 1024 * 1024
MESH = pl.DeviceIdType.MESH
ANY = pl.BlockSpec(memory_space=pl.ANY)


def _params(n_axes, vmem=VMEM_LIMIT):
    return pltpu.CompilerParams(dimension_semantics=("arbitrary",) * n_axes, vmem_limit_bytes=vmem)


def _bucket_tile(dilation):
    qi = np.arange(128)[:, None]
    kj = np.arange(256)[None, :]
    n = np.clip(128 + qi - kj, 0, 128) * dilation
    max_exact = N_BUCKETS // 2
    nf = np.maximum(n, 1).astype(np.float32)
    large = max_exact + (np.log(nf / np.float32(max_exact)) / np.float32(math.log(MAX_DISTANCE / max_exact))
                         * np.float32(N_BUCKETS - max_exact)).astype(np.int32)
    large = np.minimum(large, N_BUCKETS - 1)
    return np.where(n < max_exact, n, large).astype(np.int32)


def _gelu(x):
    c = math.sqrt(2.0 / math.pi)
    t = jnp.tanh(c * (x + 0.044715 * x * x * x))
    return 0.5 * x * (1.0 + t), t


def _gelu_grad(x, t):
    c = math.sqrt(2.0 / math.pi)
    return 0.5 * (1.0 + t) + 0.5 * x * (1.0 - t * t) * c * (1.0 + 3.0 * 0.044715 * x * x)


def _sigmoid(x):
    return 1.0 / (1.0 + jnp.exp(-x))


def _dot(a, b):
    return jnp.dot(a, b, preferred_element_type=F32)


def _behind(body, n_in, after):
    if after is None:
        return body, [], []
    return (lambda *refs: body(*refs[:n_in], *refs[n_in + 1:])), [ANY], [after]


def _dot_nt(a, b):
    return lax.dot_general(a, b, (((1,), (1,)), ((), ())), preferred_element_type=F32)


def _proj(xb, win_g):
    tn = 768
    per = 2304 // tn

    def body(x_ref, w_ref, o_ref):
        o_ref[...] = _dot(x_ref[...], w_ref[...])

    return pl.pallas_call(
        body, name="proj", grid=(DIN // tn,),
        in_specs=[pl.BlockSpec((S, D), lambda j: (0, 0)),
                  pl.BlockSpec((None, D, tn), lambda j: (j // per, 0, j % per))],
        out_specs=pl.BlockSpec((S, tn), lambda j: (0, j)),
        out_shape=jax.ShapeDtypeStruct((S, DIN), F32),
        compiler_params=_params(1),
    )(xb, win_g)


def _bias_tiles(rb_ref, bucket, bias_scr):
    qi = lax.broadcasted_iota(jnp.int32, (128, 256), 0)
    kj = lax.broadcasted_iota(jnp.int32, (128, 256), 1)
    steps = 128 + qi - kj
    band = (steps >= 0) & (steps <= 128)
    bias_scr[...] = jnp.zeros_like(bias_scr)

    def one_bucket(t, carry):
        hit = bucket == t
        for h in range(NH):
            bias_scr[h] = jnp.where(hit, rb_ref[t, h], bias_scr[h])
        return carry

    lax.fori_loop(0, N_BUCKETS, one_bucket, 0)
    for h in range(NH):
        bias_scr[h] = jnp.where(band, bias_scr[h], NEG_INF)


def _attn_fwd(qkv, rel_bias, dilation, name):
    nblk = NBLK // dilation
    bucket = jnp.asarray(_bucket_tile(dilation))

    def body(rb_ref, bucket_ref, q_ref, kc_ref, kp_ref, vc_ref, vp_ref, o_ref, lse_ref, bias_scr):
        b = pl.program_id(0)

        @pl.when(b == 0)
        def _():
            _bias_tiles(rb_ref, bucket_ref[...], bias_scr)

        has_prev = (b % nblk) != 0
        kj = lax.broadcasted_iota(jnp.int32, (128, 256), 1)
        key_ok = (kj >= 128) | has_prev
        for h in range(NH):
            cols = slice(h * HD, (h + 1) * HD)
            q = q_ref[:, cols].astype(BF16)
            s = jnp.concatenate([_dot_nt(q, kp_ref[:, cols].astype(BF16)),
                                 _dot_nt(q, kc_ref[:, cols].astype(BF16))], axis=1) * SCALE
            s = jnp.where(key_ok, s + bias_scr[h], NEG_INF)
            m = jnp.max(s, axis=1, keepdims=True)
            p = jnp.exp(s - m)
            den = jnp.sum(p, axis=1, keepdims=True)
            pb = p.astype(BF16)
            o = _dot(pb[:, :128], vp_ref[:, cols].astype(BF16)) + _dot(pb[:, 128:], vc_ref[:, cols].astype(BF16))
            o_ref[:, cols] = o / den
            lse_ref[:, cols] = jnp.broadcast_to(m + jnp.log(den), (128, HD))

    blk = lambda col, prev: pl.BlockSpec(
        (128, DA), (lambda b: (jnp.maximum(b - 1, 0), col)) if prev else (lambda b: (b, col)))
    return pl.pallas_call(
        body, name=name, grid=(NBLK,),
        in_specs=[pl.BlockSpec(memory_space=pltpu.SMEM),
                  pl.BlockSpec((128, 256), lambda b: (0, 0)),
                  blk(0, False), blk(1, False), blk(1, True), blk(2, False), blk(2, True)],
        out_specs=[pl.BlockSpec((128, DA), lambda b: (b, 0)), pl.BlockSpec((128, DA), lambda b: (b, 0))],
        out_shape=[jax.ShapeDtypeStruct((S, DA), F32), jax.ShapeDtypeStruct((S, DA), F32)],
        scratch_shapes=[pltpu.VMEM((NH, 128, 256), F32)],
        compiler_params=_params(1),
    )(rel_bias, bucket, qkv, qkv, qkv, qkv, qkv)


def _attn_combine(outs, lses):
    tm = 256

    def body(o1, o2, o3, l1, l2, l3, attn_ref, lse_ref):
        a, b, c = l1[...], l2[...], l3[...]
        m = jnp.maximum(jnp.maximum(a, b), c)
        wa, wb, wc = jnp.exp(a - m), jnp.exp(b - m), jnp.exp(c - m)
        den = wa + wb + wc
        attn_ref[...] = ((wa * o1[...] + wb * o2[...] + wc * o3[...]) / den).astype(BF16)
        lse_ref[...] = m + jnp.log(den)

    spec = pl.BlockSpec((tm, DA), lambda i: (i, 0))
    return pl.pallas_call(
        body, name="attn_combine", grid=(S // tm,),
        in_specs=[spec] * 6, out_specs=[spec, spec],
        out_shape=[jax.ShapeDtypeStruct((S, DA), BF16), jax.ShapeDtypeStruct((S, DA), F32)],
        compiler_params=_params(1),
    )(*outs, *lses)


def _gmlp_parts(u_ref, vb_ref, g_ref, be_ref):
    u = u_ref[...]
    u_act, tu = _gelu(u)
    vb = vb_ref[...]
    gv, tv = _gelu(vb)
    mean = jnp.mean(gv, axis=1, keepdims=True)
    cen = gv - mean
    var = jnp.mean(cen * cen, axis=1, keepdims=True)
    rstd = lax.rsqrt(var + LN_EPS)
    xhat = cen * rstd
    vn = xhat * g_ref[...] + be_ref[...]
    return u, tu, u_act, vb, tv, rstd, xhat, vn


def _gmlp_fwd(proj, ws, bsp_b, gain_v, bias_v):
    def body(u_ref, vb_ref, ws_ref, bsp_ref, g_ref, be_ref, o_ref):
        _, _, u_act, _, _, _, _, vn = _gmlp_parts(u_ref, vb_ref, g_ref, be_ref)
        row = lax.broadcasted_iota(jnp.int32, (128, 128), 0)
        col = lax.broadcasted_iota(jnp.int32, (128, 128), 1)
        causal = row >= col
        for g in range(NH):
            cols = slice(g * 128, (g + 1) * 128)
            wsg = jnp.where(causal, ws_ref[g], 0.0).astype(BF16)
            z = _dot(wsg, vn[:, cols].astype(BF16)) + bsp_ref[g]
            o_ref[:, cols] = (u_act[:, cols] * z).astype(BF16)

    return pl.pallas_call(
        body, name="gmlp_fwd", grid=(NBLK,),
        in_specs=[pl.BlockSpec((128, DB), lambda c: (c, 3)), pl.BlockSpec((128, DB), lambda c: (c, 4)),
                  pl.BlockSpec((NH, 128, 128), lambda c: (0, 0, 0)), pl.BlockSpec((NH, 128, 128), lambda c: (0, 0, 0)),
                  pl.BlockSpec((1, DB), lambda c: (0, 0)), pl.BlockSpec((1, DB), lambda c: (0, 0))],
        out_specs=pl.BlockSpec((128, DB), lambda c: (c, 0)),
        out_shape=jax.ShapeDtypeStruct((S, DB), BF16),
        compiler_params=_params(1),
    )(proj, proj, ws, bsp_b, gain_v, bias_v)


def _branch(attn, gmlp, wpa_g, wpb_g, proj):
    tn = 512

    def body(a_ref, g_ref, wa_ref, wb_ref, ga_ref, gb_ref, ya_ref, yb_ref, mg_ref):
        ya = _dot(a_ref[...], wa_ref[...])
        yb = _dot(g_ref[...], wb_ref[...])
        ya_ref[...] = ya.astype(BF16)
        yb_ref[...] = yb.astype(BF16)
        mg_ref[...] = (_sigmoid(ga_ref[...]) * ya + _sigmoid(gb_ref[...]) * yb).astype(BF16)

    out = pl.BlockSpec((S, tn), lambda j: (0, j))
    return pl.pallas_call(
        body, name="branch", grid=(D // tn,),
        in_specs=[pl.BlockSpec((S, DA), lambda j: (0, 0)), pl.BlockSpec((S, DB), lambda j: (0, 0)),
                  pl.BlockSpec((None, DA, tn), lambda j: (j, 0, 0)), pl.BlockSpec((None, DB, tn), lambda j: (j, 0, 0)),
                  pl.BlockSpec((S, tn), lambda j: (0, 5120 // tn + j)), pl.BlockSpec((S, tn), lambda j: (0, 7168 // tn + j))],
        out_specs=[out, out, out],
        out_shape=[jax.ShapeDtypeStruct((S, D), BF16)] * 3,
        compiler_params=_params(1),
    )(attn, gmlp, wpa_g, wpb_g, proj, proj)


def _out_ln1(merged, wout_g, x, gain, bias):
    tm = 256

    def body(m_ref, w_ref, x_ref, g_ref, b_ref, xh_ref, rs_ref, h_ref):
        pre = ALPHA * x_ref[...] + _dot(m_ref[...], w_ref[...])
        mean = jnp.mean(pre, axis=1, keepdims=True)
        cen = pre - mean
        var = jnp.mean(cen * cen, axis=1, keepdims=True)
        rstd = lax.rsqrt(var + LN_EPS)
        xhat = cen * rstd
        xh_ref[...] = xhat
        rs_ref[...] = jnp.broadcast_to(rstd, (tm, 128))
        h_ref[...] = (xhat * g_ref[...] + b_ref[...]).astype(BF16)

    row = pl.BlockSpec((tm, D), lambda i: (i, 0))
    vec = pl.BlockSpec((1, D), lambda i: (0, 0))
    return pl.pallas_call(
        body, name="out_ln1", grid=(S // tm,),
        in_specs=[row, pl.BlockSpec((D, D), lambda i: (0, 0)), row, vec, vec],
        out_specs=[row, pl.BlockSpec((tm, 128), lambda i: (i, 0)), row],
        out_shape=[jax.ShapeDtypeStruct((S, D), F32), jax.ShapeDtypeStruct((S, 128), F32),
                   jax.ShapeDtypeStruct((S, D), BF16)],
        compiler_params=_params(1),
    )(merged, wout_g, x, gain, bias)


def _ff1(h1b, w1_g, b1):
    tn = 512
    per = D // tn

    def body(h_ref, w_ref, b_ref, a_ref, r_ref):
        r = jnp.maximum(_dot(h_ref[...], w_ref[...]) + b_ref[...], 0.0)
        r_ref[...] = r.astype(BF16)
        a_ref[...] = (r * r).astype(BF16)

    out = pl.BlockSpec((S, tn), lambda j: (0, j))
    return pl.pallas_call(
        body, name="ff1", grid=(DFF // tn,),
        in_specs=[pl.BlockSpec((S, D), lambda j: (0, 0)),
                  pl.BlockSpec((None, D, tn), lambda j: (j // per, 0, j % per)),
                  pl.BlockSpec((1, tn), lambda j: (0, j))],
        out_specs=[out, out],
        out_shape=[jax.ShapeDtypeStruct((S, DFF), BF16)] * 2,
        compiler_params=_params(1),
    )(h1b, w1_g, b1)


def _ff2_ln2_loss(a, w2_g, xhat1, g1, b1, b2, g2, be2, target):
    tm, tk = 256, 1024
    nk = DFF // tk

    def body(a_ref, w_ref, xh_ref, g1_ref, b1_ref, b2_ref, g2_ref, be2_ref, t_ref, d_ref, db_ref, st_ref, acc):
        i, k = pl.program_id(0), pl.program_id(1)

        @pl.when(k == 0)
        def _():
            acc[...] = jnp.zeros_like(acc)

        @pl.when((i == 0) & (k == 0))
        def _():
            st_ref[...] = jnp.zeros_like(st_ref)

        acc[...] += _dot(a_ref[...], w_ref[...])

        @pl.when(k == nk - 1)
        def _():
            h1 = xh_ref[...] * g1_ref[...] + b1_ref[...]
            pre = ALPHA * h1 + acc[...] + b2_ref[...]
            mean = jnp.mean(pre, axis=1, keepdims=True)
            cen = pre - mean
            var = jnp.mean(cen * cen, axis=1, keepdims=True)
            rstd = lax.rsqrt(var + LN_EPS)
            xhat = cen * rstd
            y = xhat * g2_ref[...] + be2_ref[...]
            err = y - t_ref[...]
            dy = err * (1.0 / D)
            g = dy * g2_ref[...]
            dpre = rstd * (g - jnp.mean(g, axis=1, keepdims=True)
                           - xhat * jnp.mean(g * xhat, axis=1, keepdims=True))
            d_ref[...] = dpre
            db_ref[...] = dpre.astype(BF16)
            st_ref[0:1, :] += jnp.sum(dy * xhat, axis=0, keepdims=True)
            st_ref[1:2, :] += jnp.sum(dy, axis=0, keepdims=True)
            st_ref[2:3, :] += jnp.sum(dpre, axis=0, keepdims=True)
            st_ref[3:4, :] += jnp.broadcast_to(jnp.sum(err * err).reshape(1, 1), (1, D))

    row = pl.BlockSpec((tm, D), lambda i, k: (i, 0))
    vec = pl.BlockSpec((1, D), lambda i, k: (0, 0))
    return pl.pallas_call(
        body, name="ff2_ln2_loss", grid=(S // tm, nk),
        in_specs=[pl.BlockSpec((tm, tk), lambda i, k: (i, k)), pl.BlockSpec((tk, D), lambda i, k: (k, 0)),
                  row, vec, vec, vec, vec, vec, row],
        out_specs=[row, row, pl.BlockSpec((8, D), lambda i, k: (0, 0))],
        out_shape=[jax.ShapeDtypeStruct((S, D), F32), jax.ShapeDtypeStruct((S, D), BF16),
                   jax.ShapeDtypeStruct((8, D), F32)],
        scratch_shapes=[pltpu.VMEM((tm, D), F32)],
        compiler_params=_params(2),
    )(a, w2_g, xhat1, g1, b1, b2, g2, be2, target)


def _grad_w(act, dout, name, ti, tj, sharded, after=None):
    m, n = act.shape[1], dout.shape[1]
    ns = n // N_CHIPS
    per = ns // tj if sharded else None

    def body(a_ref, b_ref, o_ref, at_scr):
        @pl.when(pl.program_id(1) == 0)
        def _():
            at_scr[...] = a_ref[...].astype(F32).T.astype(BF16)

        o_ref[...] = _dot(at_scr[...], b_ref[...])

    if sharded:
        out_spec = pl.BlockSpec((None, ti, tj), lambda i, j: (j // per, i, j % per))
        out_shape = jax.ShapeDtypeStruct((N_CHIPS, m, ns), F32)
    else:
        out_spec = pl.BlockSpec((ti, tj), lambda i, j: (i, j))
        out_shape = jax.ShapeDtypeStruct((m, n), F32)
    body, more_specs, more = _behind(body, 2, after)
    return pl.pallas_call(
        body, name=name, grid=(m // ti, n // tj),
        in_specs=[pl.BlockSpec((S, ti), lambda i, j: (0, i)), pl.BlockSpec((S, tj), lambda i, j: (0, j))] + more_specs,
        out_specs=out_spec, out_shape=out_shape,
        scratch_shapes=[pltpu.VMEM((ti, S), BF16)],
        compiler_params=_params(2),
    )(act, dout, *more)


def _d_ff1(dpre2b, w2_g, r, after=None):
    tn = 512

    def body(d_ref, w_ref, r_ref, o_ref, gb_ref):
        da = _dot_nt(d_ref[...], w_ref[...])
        dp = da * (2.0 * r_ref[...].astype(F32))
        o_ref[...] = dp.astype(BF16)
        gb_ref[...] = jnp.sum(dp, axis=0, keepdims=True)

    body, more_specs, more = _behind(body, 3, after)
    return pl.pallas_call(
        body, name="d_ff1", grid=(DFF // tn,),
        in_specs=[pl.BlockSpec((S, D), lambda j: (0, 0)), pl.BlockSpec((tn, D), lambda j: (j, 0)),
                  pl.BlockSpec((S, tn), lambda j: (0, j))] + more_specs,
        out_specs=[pl.BlockSpec((S, tn), lambda j: (0, j)), pl.BlockSpec((1, tn), lambda j: (0, j))],
        out_shape=[jax.ShapeDtypeStruct((S, DFF), BF16), jax.ShapeDtypeStruct((1, DFF), F32)],
        compiler_params=_params(1),
    )(dpre2b, w2_g, r, *more)


def _d_h1_ln1(dprea, w1_g, dpre2, xhat1, rstd1, g1, after=None):
    tm, tk = 256, 512
    per = D // tk
    nk = DFF // tk

    def body(a_ref, w_ref, d2_ref, xh_ref, rs_ref, g_ref, d_ref, db_ref, st_ref, acc):
        i, k = pl.program_id(0), pl.program_id(1)

        @pl.when(k == 0)
        def _():
            acc[...] = jnp.zeros_like(acc)

        @pl.when((i == 0) & (k == 0))
        def _():
            st_ref[...] = jnp.zeros_like(st_ref)

        acc[...] += _dot_nt(a_ref[...], w_ref[...])

        @pl.when(k == nk - 1)
        def _():
            dh = ALPHA * d2_ref[...] + acc[...]
            xhat = xh_ref[...]
            g = dh * g_ref[...]
            dpre = rs_ref[:, 0:1] * (g - jnp.mean(g, axis=1, keepdims=True)
                                     - xhat * jnp.mean(g * xhat, axis=1, keepdims=True))
            d_ref[...] = dpre
            db_ref[...] = dpre.astype(BF16)
            st_ref[0:1, :] += jnp.sum(dh * xhat, axis=0, keepdims=True)
            st_ref[1:2, :] += jnp.sum(dh, axis=0, keepdims=True)

    row = pl.BlockSpec((tm, D), lambda i, k: (i, 0))
    body, more_specs, more = _behind(body, 6, after)
    return pl.pallas_call(
        body, name="d_h1_ln1", grid=(S // tm, nk),
        in_specs=[pl.BlockSpec((tm, tk), lambda i, k: (i, k)),
                  pl.BlockSpec((None, D, tk), lambda i, k: (k // per, 0, k % per)),
                  row, row, pl.BlockSpec((tm, 128), lambda i, k: (i, 0)), pl.BlockSpec((1, D), lambda i, k: (0, 0))]
        + more_specs,
        out_specs=[row, row, pl.BlockSpec((8, D), lambda i, k: (0, 0))],
        out_shape=[jax.ShapeDtypeStruct((S, D), F32), jax.ShapeDtypeStruct((S, D), BF16),
                   jax.ShapeDtypeStruct((8, D), F32)],
        scratch_shapes=[pltpu.VMEM((tm, D), F32)],
        compiler_params=_params(2),
    )(dprea, w1_g, dpre2, xhat1, rstd1, g1, *more)


def _d_merged(dpre1b, wout_g, proj, ya, yb):
    tm, tn = 512, 1024

    def body(d_ref, w_ref, ga_ref, gb_ref, ya_ref, yb_ref, dya_ref, dyb_ref, dga_ref, dgb_ref):
        dm = _dot_nt(d_ref[...], w_ref[...])
        sa = _sigmoid(ga_ref[...])
        sb = _sigmoid(gb_ref[...])
        dya_ref[...] = (dm * sa).astype(BF16)
        dyb_ref[...] = (dm * sb).astype(BF16)
        dga_ref[...] = (dm * ya_ref[...].astype(F32) * sa * (1.0 - sa)).astype(BF16)
        dgb_ref[...] = (dm * yb_ref[...].astype(F32) * sb * (1.0 - sb)).astype(BF16)

    tile = pl.BlockSpec((tm, tn), lambda i, j: (i, j))
    return pl.pallas_call(
        body, name="d_merged", grid=(S // tm, D // tn),
        in_specs=[pl.BlockSpec((tm, D), lambda i, j: (i, 0)), pl.BlockSpec((tn, D), lambda i, j: (j, 0)),
                  pl.BlockSpec((tm, tn), lambda i, j: (i, 5 + j)), pl.BlockSpec((tm, tn), lambda i, j: (i, 7 + j)),
                  tile, tile],
        out_specs=[tile] * 4,
        out_shape=[jax.ShapeDtypeStruct((S, D), BF16)] * 4,
        compiler_params=_params(2),
    )(dpre1b, wout_g, proj, proj, ya, yb)


def _d_branches(dya, dyb, wpa_g, wpb_g, after=None):
    tk = 512

    def body(da_ref, db_ref, wa_ref, wb_ref, oa_ref, ob_ref):
        @pl.when(pl.program_id(0) == 0)
        def _():
            oa_ref[...] = jnp.zeros_like(oa_ref)
            ob_ref[...] = jnp.zeros_like(ob_ref)

        oa_ref[...] += _dot_nt(da_ref[...], wa_ref[...])
        ob_ref[...] += _dot_nt(db_ref[...], wb_ref[...])

    body, more_specs, more = _behind(body, 4, after)
    return pl.pallas_call(
        body, name="d_branches", grid=(D // tk,),
        in_specs=[pl.BlockSpec((S, tk), lambda k: (0, k)), pl.BlockSpec((S, tk), lambda k: (0, k)),
                  pl.BlockSpec((None, DA, tk), lambda k: (k, 0, 0)), pl.BlockSpec((None, DB, tk), lambda k: (k, 0, 0))]
        + more_specs,
        out_specs=[pl.BlockSpec((S, DA), lambda k: (0, 0)), pl.BlockSpec((S, DB), lambda k: (0, 0))],
        out_shape=[jax.ShapeDtypeStruct((S, DA), F32), jax.ShapeDtypeStruct((S, DB), F32)],
        compiler_params=_params(1),
    )(dya, dyb, wpa_g, wpb_g, *more)


def _gmlp_bwd(proj, dgmlp, ws, ws_t, bsp_b, gain_v, bias_v):
    def body(u_ref, vb_ref, dg_ref, ws_ref, wst_ref, bsp_ref, g_ref, be_ref, duv_ref, gws_ref, gbs_ref, st_ref):
        @pl.when(pl.program_id(0) == 0)
        def _():
            gws_ref[...] = jnp.zeros_like(gws_ref)
            gbs_ref[...] = jnp.zeros_like(gbs_ref)
            st_ref[...] = jnp.zeros_like(st_ref)

        u, tu, u_act, vb, tv, rstd, xhat, vn = _gmlp_parts(u_ref, vb_ref, g_ref, be_ref)
        dg = dg_ref[...]
        dz = dg * u_act
        row = lax.broadcasted_iota(jnp.int32, (128, 128), 0)
        col = lax.broadcasted_iota(jnp.int32, (128, 128), 1)
        causal = row >= col
        causal_t = row <= col
        dvn_parts = []
        z_parts = []
        for g in range(NH):
            cols = slice(g * 128, (g + 1) * 128)
            vng = vn[:, cols].astype(BF16)
            dzg = dz[:, cols]
            dzb = dzg.astype(BF16)
            wsg = jnp.where(causal, ws_ref[g], 0.0).astype(BF16)
            wsg_t = jnp.where(causal_t, wst_ref[g], 0.0).astype(BF16)
            z_parts.append(_dot(wsg, vng) + bsp_ref[g])
            gws_ref[g] += jnp.where(causal, _dot_nt(dzb, vng), 0.0)
            gbs_ref[g] += jnp.broadcast_to(jnp.sum(dzg, axis=1, keepdims=True), (128, 128))
            dvn_parts.append(_dot(wsg_t, dzb))
        z = jnp.concatenate(z_parts, axis=1)
        dvn = jnp.concatenate(dvn_parts, axis=1)
        du = dg * z * _gelu_grad(u, tu)
        st_ref[0:1, :] += jnp.sum(dvn * xhat, axis=0, keepdims=True)
        st_ref[1:2, :] += jnp.sum(dvn, axis=0, keepdims=True)
        gg = dvn * g_ref[...]
        dgv = rstd * (gg - jnp.mean(gg, axis=1, keepdims=True) - xhat * jnp.mean(gg * xhat, axis=1, keepdims=True))
        dvb = dgv * _gelu_grad(vb, tv)
        duv_ref[:, 0:DB] = du.astype(BF16)
        duv_ref[:, DB:2 * DB] = dvb.astype(BF16)

    full3 = pl.BlockSpec((NH, 128, 128), lambda c: (0, 0, 0))
    vec = pl.BlockSpec((1, DB), lambda c: (0, 0))
    return pl.pallas_call(
        body, name="gmlp_bwd", grid=(NBLK,),
        in_specs=[pl.BlockSpec((128, DB), lambda c: (c, 3)), pl.BlockSpec((128, DB), lambda c: (c, 4)),
                  pl.BlockSpec((128, DB), lambda c: (c, 0)), full3, full3, full3, vec, vec],
        out_specs=[pl.BlockSpec((128, 2 * DB), lambda c: (c, 0)), full3, full3, pl.BlockSpec((8, DB), lambda c: (0, 0))],
        out_shape=[jax.ShapeDtypeStruct((S, 2 * DB), BF16), jax.ShapeDtypeStruct((NH, 128, 128), F32),
                   jax.ShapeDtypeStruct((NH, 128, 128), F32), jax.ShapeDtypeStruct((8, DB), F32)],
        compiler_params=_params(1),
    )(proj, proj, dgmlp, ws, ws_t, bsp_b, gain_v, bias_v)


def _attn_delta(dattn, attn, after=None):
    tm = 256

    def body(d_ref, o_ref, dl_ref, db_ref):
        d = d_ref[...]
        prod = d * o_ref[...].astype(F32)
        for h in range(NH):
            cols = slice(h * HD, (h + 1) * HD)
            dl_ref[:, cols] = jnp.broadcast_to(jnp.sum(prod[:, cols], axis=1, keepdims=True), (tm, HD))
        db_ref[...] = d.astype(BF16)

    spec = pl.BlockSpec((tm, DA), lambda i: (i, 0))
    body, more_specs, more = _behind(body, 2, after)
    return pl.pallas_call(
        body, name="attn_delta", grid=(S // tm,),
        in_specs=[spec, spec] + more_specs, out_specs=[spec, spec],
        out_shape=[jax.ShapeDtypeStruct((S, DA), F32), jax.ShapeDtypeStruct((S, DA), BF16)],
        compiler_params=_params(1),
    )(dattn, attn, *more)


def _attn_bwd(qkv, dob, lse, delta, rel_bias, dilation, name):
    nblk = NBLK // dilation
    bucket = jnp.asarray(_bucket_tile(dilation))

    def body(rb_ref, bucket_ref, q_ref, qn_ref, kc_ref, kp_ref, vc_ref, vp_ref, do_ref, don_ref,
             l_ref, ln_ref, dl_ref, dln_ref, dqkv_ref, ds_ref, bias_scr):
        b = pl.program_id(0)

        @pl.when(b == 0)
        def _():
            _bias_tiles(rb_ref, bucket_ref[...], bias_scr)
            ds_ref[...] = jnp.zeros_like(ds_ref)

        has_prev = (b % nblk) != 0
        has_next = ((b + 1) % nblk) != 0
        for h in range(NH):
            cols = slice(h * HD, (h + 1) * HD)
            q = q_ref[:, cols].astype(BF16)
            kc = kc_ref[:, cols].astype(BF16)
            kp = kp_ref[:, cols].astype(BF16)
            vc = vc_ref[:, cols].astype(BF16)
            vp = vp_ref[:, cols].astype(BF16)
            do = do_ref[:, cols]
            bias_p = bias_scr[h, :, 0:128]
            bias_c = bias_scr[h, :, 128:256]
            lse_b = l_ref[:, cols]
            dl_b = dl_ref[:, cols]
            p_c = jnp.exp(_dot_nt(q, kc) * SCALE + bias_c - lse_b)
            p_p = jnp.where(has_prev, jnp.exp(_dot_nt(q, kp) * SCALE + bias_p - lse_b), 0.0)
            ds_c = p_c * (_dot_nt(do, vc) - dl_b)
            ds_p = p_p * (_dot_nt(do, vp) - dl_b)
            ds_ref[h, :, 0:128] += ds_p
            ds_ref[h, :, 128:256] += ds_c
            ds_cb = ds_c.astype(BF16)
            dqkv_ref[:, cols] = (_dot(ds_cb, kc) + _dot(ds_p.astype(BF16), kp)) * SCALE
            qn = qn_ref[:, cols].astype(BF16)
            don = don_ref[:, cols]
            p_n = jnp.where(has_next, jnp.exp(_dot_nt(qn, kc) * SCALE + bias_p - ln_ref[:, cols]), 0.0)
            ds_n = p_n * (_dot_nt(don, vc) - dln_ref[:, cols])
            dqkv_ref[:, DA + h * HD:DA + (h + 1) * HD] = (_dot(ds_c.T.astype(BF16), q) + _dot(ds_n.T.astype(BF16), qn)) * SCALE
            dqkv_ref[:, 2 * DA + h * HD:2 * DA + (h + 1) * HD] = _dot(p_c.T.astype(BF16), do) + _dot(p_n.T.astype(BF16), don)

    cur = lambda col: pl.BlockSpec((128, DA), lambda b: (b, col))
    prev = lambda col: pl.BlockSpec((128, DA), lambda b: (jnp.maximum(b - 1, 0), col))
    nxt = lambda col: pl.BlockSpec((128, DA), lambda b: (jnp.minimum(b + 1, NBLK - 1), col))
    return pl.pallas_call(
        body, name=name, grid=(NBLK,),
        in_specs=[pl.BlockSpec(memory_space=pltpu.SMEM), pl.BlockSpec((128, 256), lambda b: (0, 0)),
                  cur(0), nxt(0), cur(1), prev(1), cur(2), prev(2), cur(0), nxt(0), cur(0), nxt(0), cur(0), nxt(0)],
        out_specs=[pl.BlockSpec((128, 3 * DA), lambda b: (b, 0)), pl.BlockSpec((NH, 128, 256), lambda b: (0, 0, 0))],
        out_shape=[jax.ShapeDtypeStruct((S, 3 * DA), F32), jax.ShapeDtypeStruct((NH, 128, 256), F32)],
        scratch_shapes=[pltpu.VMEM((NH, 128, 256), F32)],
        compiler_params=_params(1),
    )(rel_bias, bucket, qkv, qkv, qkv, qkv, qkv, qkv, dob, dob, lse, lse, delta, delta)


def _sum3_bf16(a, b, c):
    tm = 256
    n = a.shape[1]

    def body(a_ref, b_ref, c_ref, o_ref):
        o_ref[...] = (a_ref[...] + b_ref[...] + c_ref[...]).astype(BF16)

    spec = pl.BlockSpec((tm, n), lambda i: (i, 0))
    return pl.pallas_call(
        body, name="dqkv_sum", grid=(S // tm,), in_specs=[spec] * 3, out_specs=spec,
        out_shape=jax.ShapeDtypeStruct((S, n), BF16), compiler_params=_params(1),
    )(a, b, c)


def _rel_bias_grad(ds_sums):
    buckets = jnp.asarray(np.stack([_bucket_tile(d) for _, d in PATTERNS]))

    def body(bk_ref, d1, d2, d3, o_ref):
        row = lax.broadcasted_iota(jnp.int32, (N_BUCKETS, 128), 0)
        lane = lax.broadcasted_iota(jnp.int32, (N_BUCKETS, 128), 1)

        def one_bucket(t, out):
            hits = [bk_ref[p] == t for p in range(3)]
            for h in range(NH):
                tot = jnp.zeros((128, 256), F32)
                for p, d in enumerate((d1, d2, d3)):
                    tot = tot + jnp.where(hits[p], d[h], 0.0)
                out = jnp.where((row == t) & (lane == h), jnp.sum(tot), out)
            return out

        o_ref[...] = lax.fori_loop(0, N_BUCKETS, one_bucket, jnp.zeros((N_BUCKETS, 128), F32))

    return pl.pallas_call(
        body, name="rel_bias_grad",
        in_specs=[pl.BlockSpec(memory_space=pltpu.VMEM)] * 4, out_specs=pl.BlockSpec(memory_space=pltpu.VMEM),
        out_shape=jax.ShapeDtypeStruct((N_BUCKETS, 128), F32),
        compiler_params=pltpu.CompilerParams(vmem_limit_bytes=VMEM_LIMIT),
    )(buckets, *ds_sums)


def _d_x(dproj, win_g, dpre1, after=None):
    tm, tk = 512, 768
    per = 2304 // tk
    nk = DIN // tk

    def body(a_ref, w_ref, d_ref, o_ref, acc):
        k = pl.program_id(1)

        @pl.when(k == 0)
        def _():
            acc[...] = ALPHA * d_ref[...]

        acc[...] += _dot_nt(a_ref[...], w_ref[...])

        @pl.when(k == nk - 1)
        def _():
            o_ref[...] = acc[...]

    row = pl.BlockSpec((tm, D), lambda i, k: (i, 0))
    body, more_specs, more = _behind(body, 3, after)
    return pl.pallas_call(
        body, name="d_x", grid=(S // tm, nk),
        in_specs=[pl.BlockSpec((tm, tk), lambda i, k: (i, k)),
                  pl.BlockSpec((None, D, tk), lambda i, k: (k // per, 0, k % per)), row] + more_specs,
        out_specs=row, out_shape=jax.ShapeDtypeStruct((S, D), F32),
        scratch_shapes=[pltpu.VMEM((tm, D), F32)],
        compiler_params=_params(2),
    )(dproj, win_g, dpre1, *more)


def _adamw(w, g, m, v, name):
    rows, cols = w.shape
    tm = max(t for t in range(8, 257, 8) if rows % t == 0)

    def body(w_ref, g_ref, m_ref, v_ref, d_ref, nm_ref, nv_ref):
        g = g_ref[...]
        m = ADAM_B1 * m_ref[...] + (1.0 - ADAM_B1) * g
        v = ADAM_B2 * v_ref[...] + (1.0 - ADAM_B2) * (g * g)
        m_hat = m / (1.0 - ADAM_B1 ** ADAM_STEP)
        v_hat = v / (1.0 - ADAM_B2 ** ADAM_STEP)
        d_ref[...] = -ADAM_LR * (m_hat / (jnp.sqrt(v_hat) + ADAM_EPS) + ADAM_WD * w_ref[...])
        nm_ref[...] = m
        nv_ref[...] = v

    spec = pl.BlockSpec((tm, cols), lambda i: (i, 0))
    return pl.pallas_call(
        body, name=name, grid=(rows // tm,), in_specs=[spec] * 4, out_specs=[spec] * 3,
        out_shape=[jax.ShapeDtypeStruct((rows, cols), F32)] * 3, compiler_params=_params(1),
    )(w, g, m, v)


def _position():
    x, y, c = lax.axis_index("x"), lax.axis_index("y"), lax.axis_index("c")
    chips = [(1 - x, y), (x, 1 - y), (1 - x, 1 - y)]
    return x, y, c, chips


def _remote(src, dst, send_sems, recv_sems, k, to):
    return pltpu.make_async_remote_copy(src_ref=src, dst_ref=dst, send_sem=send_sems.at[k], recv_sem=recv_sems.at[k],
                                        device_id=to, device_id_type=MESH)


def _place_shard(w, name):
    rows, cols = w.shape
    tm = 256
    x, y = lax.axis_index("x"), lax.axis_index("y")

    def body(chip_ref, w_ref, o_ref):
        o_ref[...] = w_ref[...].astype(BF16)

    return pl.pallas_call(
        body, name=name,
        grid_spec=pltpu.PrefetchScalarGridSpec(
            num_scalar_prefetch=1, grid=(rows // tm,),
            in_specs=[pl.BlockSpec((tm, cols), lambda i, chip: (i, 0))],
            out_specs=pl.BlockSpec((None, tm, cols), lambda i, chip: (chip[0], i, 0))),
        out_shape=jax.ShapeDtypeStruct((N_CHIPS, rows, cols), BF16),
        compiler_params=_params(1),
    )(jnp.reshape(2 * x + y, (1,)).astype(jnp.int32), w)


HBM =pl.BlockSpec(memory_space=pltpu.HBM)
SEM = pl.BlockSpec(memory_space=pltpu.SEMAPHORE)
EFFECT = pltpu.SideEffectType.DATAFLOW_SIDE_EFFECTING


def _comm_call(name, body, bufs, sems_in, sems_out, after=None, token=False):
    nb, ns, no = len(bufs), len(sems_in), len(sems_out)
    n_in = nb + ns + (after is not None)

    def wrapped(*refs):
        body(refs[:nb], refs[nb:nb + ns], refs[n_in + nb:n_in + nb + no])
        if token:
            refs[-1][...] = jnp.zeros((8, 128), F32)

    outs = pl.pallas_call(
        wrapped, name=name,
        in_specs=[HBM] * nb + [SEM] * ns + ([ANY] if after is not None else []),
        out_specs=[HBM] * nb + [SEM] * no + ([pl.BlockSpec(memory_space=pltpu.VMEM)] if token else []),
        out_shape=[pltpu.HBM(b.shape, b.dtype) for b in bufs] + [pltpu.SemaphoreType.DMA((k,)) for k in sems_out]
        + ([jax.ShapeDtypeStruct((8, 128), F32)] if token else []),
        input_output_aliases={i: i for i in range(nb)},
        compiler_params=pltpu.CompilerParams(has_side_effects=EFFECT),
    )(*[pltpu.with_memory_space_constraint(b, pltpu.HBM) for b in bufs], *sems_in, *([after] if after is not None else []))
    return list(outs[:nb]), list(outs[nb:nb + no]), (outs[-1] if token else None)


def _ag_copies(buf, send_sems, recv_sems, k0, stage):
    x, y, c, chips = _position()
    hr = buf.shape[1] // 2
    half = lambda chip, h: buf.at[chip, pl.ds(h * hr, hr), :]
    sends, arrivals = [], []
    for j, (cx, cy) in enumerate(chips):
        if stage == "ici":
            mine = half(2 * x + y, c)
            sends.append(_remote(mine, mine, send_sems, recv_sems, k0 + j, (cx, cy, c)))
            got = half(2 * cx + cy, c)
        else:
            landed = half(2 * cx + cy, c)
            sends.append(_remote(landed, landed, send_sems, recv_sems, k0 + j, (x, y, 1 - c)))
            got = half(2 * cx + cy, 1 - c)
        arrivals.append(_remote(got, got, send_sems, recv_sems, k0 + j, (x, y, c)))
    return sends, arrivals


def _ag_start(groups):
    flat = [b for g in groups for b in g]

    def body(bufs, _, sems):
        at = 0
        for gi, g in enumerate(groups):
            for wi in range(len(g)):
                for cp in _ag_copies(bufs[at], sems[2 * gi], sems[2 * gi + 1], 3 * wi, "ici")[0]:
                    cp.start()
                at += 1

    bufs, sems, _ = _comm_call("allgather_start", body, flat, [], [3 * len(g) for g in groups for _ in (0, 1)])
    out, at = [], 0
    for gi, g in enumerate(groups):
        out.append((bufs[at:at + len(g)], sems[2 * gi], sems[2 * gi + 1]))
        at += len(g)
    return out


def _ag_step(name, finish, advance, after=None):
    fin_bufs = list(finish[0]) if finish else []
    adv_bufs = list(advance[0]) if advance else []
    nf = len(fin_bufs)

    def body(bufs, sems_in, sems_out):
        if advance:
            ici_s, ici_r = sems_in[-2], sems_in[-1]
            for wi in range(len(adv_bufs)):
                buf = bufs[nf + wi]
                ici_sends, ici_arrivals = _ag_copies(buf, ici_s, ici_r, 3 * wi, "ici")
                d2d_sends, _ = _ag_copies(buf, sems_out[0], sems_out[1], 3 * wi, "d2d")
                for arrived, onward in zip(ici_arrivals, d2d_sends):
                    arrived.wait_recv()
                    onward.start()
                for cp in ici_sends:
                    cp.wait_send()
        if finish:
            for wi in range(nf):
                d2d_sends, d2d_arrivals = _ag_copies(bufs[wi], sems_in[0], sems_in[1], 3 * wi, "d2d")
                for cp in d2d_arrivals:
                    cp.wait_recv()
                for cp in d2d_sends:
                    cp.wait_send()

    sems_in = (list(finish[1:]) if finish else []) + (list(advance[1:]) if advance else [])
    bufs, sems, _ = _comm_call(name, body, fin_bufs + adv_bufs, sems_in, [3 * len(adv_bufs)] * 2 if advance else [], after)
    return bufs[:nf], ((bufs[nf:], sems[0], sems[1]) if advance else None)


def _cx_copies(src, dst, send_sems, recv_sems, k0):
    x, y, c, chips = _position()
    sends = [_remote(src.at[2 * cx + cy], dst.at[2 * x + y], send_sems, recv_sems, k0 + j, (cx, cy, c))
             for j, (cx, cy) in enumerate(chips)]
    arrivals = [_remote(dst.at[2 * cx + cy], dst.at[2 * cx + cy], send_sems, recv_sems, k0 + j, (x, y, c))
                for j, (cx, cy) in enumerate(chips)]
    return sends, arrivals


def _cx_start(name, pair_sums):
    n = len(pair_sums)
    landing = [lax.empty(p.shape, p.dtype) for p in pair_sums]

    def body(bufs, _, sems):
        for w in range(n):
            for cp in _cx_copies(bufs[w], bufs[n + w], sems[0], sems[1], 3 * w)[0]:
                cp.start()

    bufs, sems, token = _comm_call(name, body, list(pair_sums) + landing, [], [3 * n, 3 * n], token=True)
    return (bufs, sems), token


def _cx_wait(name, state, after):
    bufs, sems = state
    n = len(bufs) // 2

    def body(refs, sems_in, _):
        for w in range(n):
            sends, arrivals = _cx_copies(refs[w], refs[n + w], sems_in[0], sems_in[1], 3 * w)
            for cp in arrivals:
                cp.wait_recv()
            for cp in sends:
                cp.wait_send()

    bufs, _, _ = _comm_call(name, body, bufs, sems, [], after)
    return bufs[:n], bufs[n:]


def _px_copies(src, dst, send_sems, recv_sems, k):
    x, y, c, _ = _position()
    hr = src.shape[1] // 2
    send = _remote(src.at[:, pl.ds((1 - c) * hr, hr), :], dst, send_sems, recv_sems, k, (x, y, 1 - c))
    arrival = _remote(dst, dst, send_sems, recv_sems, k, (x, y, c))
    return send, arrival


def _px_start(name, grads):
    n = len(grads)
    landing = [lax.empty((N_CHIPS, g.shape[1] // 2, g.shape[2]), F32) for g in grads]

    def body(bufs, _, sems):
        for w in range(n):
            _px_copies(bufs[w], bufs[n + w], sems[0], sems[1], w)[0].start()

    bufs, sems, token = _comm_call(name, body, list(grads) + landing, [], [n, n], token=True)
    return (bufs, sems), token


def _px_wait(name, state, after):
    bufs, sems = state
    n = len(bufs) // 2

    def body(refs, sems_in, _):
        for w in range(n):
            send, arrival = _px_copies(refs[w], refs[n + w], sems_in[0], sems_in[1], w)
            arrival.wait_recv()
            send.wait_send()

    bufs, _, _ = _comm_call(name, body, bufs, sems, [], after)
    return bufs[:n], bufs[n:]


def _pair_sum(grad, got, name):
    _, rows, cols = grad.shape
    hr = rows // 2
    tm = min(hr, 256)
    nb = hr // tm
    c = lax.axis_index("c")

    def body(c_ref, g_ref, o_ref, out_ref):
        out_ref[...] = (g_ref[...] + o_ref[...]).astype(BF16)

    return pl.pallas_call(
        body, name=name,
        grid_spec=pltpu.PrefetchScalarGridSpec(
            num_scalar_prefetch=1, grid=(N_CHIPS, nb),
            in_specs=[pl.BlockSpec((None, tm, cols), lambda s, i, c_ref: (s, c_ref[0] * nb + i, 0)),
                      pl.BlockSpec((None, tm, cols), lambda s, i, c_ref: (s, i, 0))],
            out_specs=pl.BlockSpec((None, tm, cols), lambda s, i, c_ref: (s, i, 0))),
        out_shape=jax.ShapeDtypeStruct((N_CHIPS, hr, cols), BF16),
        compiler_params=_params(2),
    )(jnp.reshape(c, (1,)).astype(jnp.int32), grad, got)


def _chip_sum(parts, pair_sums, name):
    _, hr, cols = parts.shape
    tm = min(hr, 256)
    nb = hr // tm
    x, y, c = lax.axis_index("x"), lax.axis_index("y"), lax.axis_index("c")

    def body(pos_ref, p_ref, own_ref, o_ref):
        chip = pos_ref[0]
        own = own_ref[...].astype(F32)
        term = lambda s: jnp.where(chip == s, own, p_ref[s].astype(F32))
        o_ref[...] = ((term(0) + term(1)) + term(2)) + term(3)

    return pl.pallas_call(
        body, name=name,
        grid_spec=pltpu.PrefetchScalarGridSpec(
            num_scalar_prefetch=1, grid=(nb,),
            in_specs=[pl.BlockSpec((N_CHIPS, tm, cols), lambda i, pos: (0, i, 0)),
                      pl.BlockSpec((None, tm, cols), lambda i, pos: (pos[0], i, 0))],
            out_specs=pl.BlockSpec((tm, cols), lambda i, pos: (pos[1] * nb + i, 0))),
        out_shape=jax.ShapeDtypeStruct((2 * hr, cols), F32), compiler_params=_params(1),
    )(jnp.stack([2 * x + y, c]).astype(jnp.int32), parts, pair_sums)


def _share_halves(bufs, name):
    n = len(bufs)

    def body(*refs):
        outs = refs[n:2 * n]
        send_sems, recv_sems = refs[2 * n:]
        x, y, c, _ = _position()
        copies = []
        for w in range(n):
            hr = outs[w].shape[0] // 2
            mine = outs[w].at[pl.ds(c * hr, hr), :]
            cp = _remote(mine, mine, send_sems, recv_sems, w, (x, y, 1 - c))
            cp.start()
            copies.append(cp)
        for w in range(n):
            hr = outs[w].shape[0] // 2
            theirs = outs[w].at[pl.ds((1 - c) * hr, hr), :]
            _remote(theirs, theirs, send_sems, recv_sems, w, (x, y, c)).wait_recv()
        for cp in copies:
            cp.wait_send()

    return pl.pallas_call(
        body, name=name,
        in_specs=[ANY] * n, out_specs=[ANY] * n,
        out_shape=[jax.ShapeDtypeStruct(b.shape, b.dtype) for b in bufs],
        input_output_aliases={w: w for w in range(n)},
        scratch_shapes=[pltpu.SemaphoreType.DMA((n,)), pltpu.SemaphoreType.DMA((n,))],
    )(*bufs)


def _allreduce_small(g):
    rows = g.shape[0]

    def body(g_ref, o_ref, sib, slots, send_sems, recv_sems):
        x, y, c, chips = _position()
        me = (x, y, c)
        my_chip = 2 * x + y
        pair = _remote(g_ref, sib, send_sems, recv_sems, 0, (x, y, 1 - c))
        pair.start()
        pair.wait()
        slots[my_chip] = g_ref[...] + sib[...]
        sent = []
        for j, (cx, cy) in enumerate(chips):
            cp = _remote(slots.at[my_chip], slots.at[my_chip], send_sems, recv_sems, 1 + j, (cx, cy, c))
            cp.start()
            sent.append(cp)
        for j, (cx, cy) in enumerate(chips):
            got = slots.at[2 * cx + cy]
            _remote(got, got, send_sems, recv_sems, 1 + j, me).wait_recv()
        for cp in sent:
            cp.wait_send()
        o_ref[...] = ((slots[0] + slots[1]) + slots[2]) + slots[3]

    vm = pl.BlockSpec(memory_space=pltpu.VMEM)
    return pl.pallas_call(
        body, name="allreduce_small",
        in_specs=[vm], out_specs=vm, out_shape=jax.ShapeDtypeStruct((rows, 128), F32),
        scratch_shapes=[pltpu.VMEM((rows, 128), F32), pltpu.VMEM((N_CHIPS, rows, 128), F32),
                        pltpu.SemaphoreType.DMA((4,)), pltpu.SemaphoreType.DMA((4,))],
        compiler_params=pltpu.CompilerParams(vmem_limit_bytes=VMEM_LIMIT),
    )(g)


def _permute(a, d):
    return a if d == 1 else a.reshape(S // d, d, a.shape[1]).transpose(1, 0, 2).reshape(S, a.shape[1])


def _unpermute(a, d):
    return a if d == 1 else a.reshape(d, S // d, a.shape[1]).transpose(1, 0, 2).reshape(S, a.shape[1])


_SMALL = ("rel_bias", "ln_v_gain", "ln_v_bias", "w_spatial", "b_spatial", "ln1_gain", "ln1_bias",
          "b_ff1", "b_ff2", "ln2_gain", "ln2_bias")
_SMALL_ROWS = 1200


def _pack_small(parts):
    flat = jnp.concatenate([parts[k].reshape(-1).astype(F32) for k in _SMALL])
    flat = jnp.pad(flat, (0, _SMALL_ROWS * 128 - flat.shape[0]))
    return flat.reshape(_SMALL_ROWS, 128)


def _unpack_small(packed, like):
    flat = packed.reshape(-1)
    out, at = {}, 0
    for k in _SMALL:
        n = math.prod(like[k].shape)
        out[k] = flat[at:at + n].reshape(like[k].shape)
        at += n
    return out


def kernel(x, w_in, rel_bias, ln_v_gain, ln_v_bias, w_spatial, b_spatial, w_proj_a, w_proj_b, w_out, ln1_gain, ln1_bias, w_ff1, b_ff1, w_ff2, b_ff2, ln2_gain, ln2_bias, loss_target, m_w_in, m_rel_bias, m_ln_v_gain, m_ln_v_bias, m_w_spatial, m_b_spatial, m_w_proj_a, m_w_proj_b, m_w_out, m_ln1_gain, m_ln1_bias, m_w_ff1, m_b_ff1, m_w_ff2, m_b_ff2, m_ln2_gain, m_ln2_bias, v_w_in, v_rel_bias, v_ln_v_gain, v_ln_v_bias, v_w_spatial, v_b_spatial, v_w_proj_a, v_w_proj_b, v_w_out, v_ln1_gain, v_ln1_bias, v_w_ff1, v_b_ff1, v_w_ff2, v_b_ff2, v_ln2_gain, v_ln2_bias):
    args = dict(locals())
    big = ("w_in", "w_proj_a", "w_proj_b", "w_out", "w_ff1", "w_ff2")
    weights = ("w_in", "rel_bias", "ln_v_gain", "ln_v_bias", "w_spatial", "b_spatial", "w_proj_a", "w_proj_b", "w_out",
               "ln1_gain", "ln1_bias", "w_ff1", "b_ff1", "w_ff2", "b_ff2", "ln2_gain", "ln2_bias")

    xs = x[0]
    target = loss_target[0]

    placed = [_place_shard(args[k][0], f"place_{k}") for k in big]
    in_a, in_b, in_c, in_d = _ag_start([placed[0:1], placed[1:4], placed[4:5], placed[5:6]])
    _, d2d_a = _ag_step("allgather_w_in_pass", None, in_a)
    (win_g,), _ = _ag_step("allgather_w_in_done", d2d_a, None)

    xb = xs.astype(BF16)
    proj = _proj(xb, win_g)
    _, d2d_b = _ag_step("allgather_b_pass", None, in_b, after=proj)
    qkv_p = [proj] + [_permute(proj[:, :3 * DA], d) for _, d in PATTERNS[1:]]
    outs, lses = [], []
    for p, (_, d) in enumerate(PATTERNS):
        o, l = _attn_fwd(qkv_p[p], rel_bias, d, f"attn_fwd_{p}")
        outs.append(_unpermute(o, d))
        lses.append(_unpermute(l, d))
    attn, lse = _attn_combine(outs, lses)
    ws = w_spatial[0]
    ws_t = jnp.transpose(ws, (0, 2, 1))
    bsp_b = jnp.broadcast_to(b_spatial[0][:, :, None], (NH, 128, 128))
    gmlp = _gmlp_fwd(proj, ws, bsp_b, ln_v_gain, ln_v_bias)
    (wpa_g, wpb_g, wout_g), d2d_c = _ag_step("allgather_b_done_c_pass", d2d_b, in_c, after=gmlp)
    wout_full = wout_g.reshape(D, D)
    ya, yb, merged = _branch(attn, gmlp, wpa_g, wpb_g, proj)
    xhat1, rstd1, h1b = _out_ln1(merged, wout_full, xs, ln1_gain, ln1_bias)
    (w1_g,), d2d_d = _ag_step("allgather_c_done_d_pass", d2d_c, in_d, after=h1b)
    a, r = _ff1(h1b, w1_g, b_ff1)
    (w2_g,), _ = _ag_step("allgather_d_done", d2d_d, None, after=a)
    w2_full = w2_g.reshape(DFF, D)
    dpre2, dpre2b, st2 = _ff2_ln2_loss(a, w2_full, xhat1, ln1_gain, ln1_bias, b_ff2, ln2_gain, ln2_bias, target)

    def pair_and_chip(tag, state, after):
        local, from_sibling = _px_wait(f"pair_exchange_wait_{tag}", state, after)
        pair_sums = [_pair_sum(g, o, f"pair_sum_{tag}_{i}") for i, (g, o) in enumerate(zip(local, from_sibling))]
        return _cx_start(f"chip_exchange_start_{tag}", pair_sums)

    g_w2 = _grad_w(a, dpre2b, "grad_w_ff2", 512, 1024, False)
    px, tok = _px_start("pair_exchange_start_w_ff2", [g_w2.reshape(N_CHIPS, DFF // N_CHIPS, D)])
    dprea, g_b1 = _d_ff1(dpre2b, w2_full, r, after=tok)
    cx_w2, tok = pair_and_chip("w_ff2", px, dprea)
    g_w1 = _grad_w(h1b, dprea, "grad_w_ff1", 512, 1024, True, after=tok)
    px, tok = _px_start("pair_exchange_start_w_ff1", [g_w1])
    dpre1, dpre1b, st1 = _d_h1_ln1(dprea, w1_g, dpre2, xhat1, rstd1, ln1_gain, after=tok)
    cx_w1, tok = pair_and_chip("w_ff1", px, dpre1b)
    g_wout = _grad_w(merged, dpre1b, "grad_w_out", 512, 1024, False, after=tok)
    dya, dyb, dga, dgb = _d_merged(dpre1b, wout_full, proj, ya, yb)
    g_wpa = _grad_w(attn, dya, "grad_w_proj_a", 512, 512, True)
    g_wpb = _grad_w(gmlp, dyb, "grad_w_proj_b", 512, 512, True)
    px, tok = _px_start("pair_exchange_start_b", [g_wpa, g_wpb, g_wout.reshape(N_CHIPS, D // N_CHIPS, D)])
    dattn, dgmlp = _d_branches(dya, dyb, wpa_g, wpb_g, after=tok)
    duv, g_ws, g_bs, stv = _gmlp_bwd(proj, dgmlp, ws, ws_t, bsp_b, ln_v_gain, ln_v_bias)
    cx_b, tok = pair_and_chip("b", px, duv)
    delta, dob = _attn_delta(dattn, attn, after=tok)
    dqkv, ds_sums = [], []
    for p, (_, d) in enumerate(PATTERNS):
        dqkv_p, ds = _attn_bwd(qkv_p[p], _permute(dob, d), _permute(lse, d), _permute(delta, d), rel_bias, d,
                               f"attn_bwd_{p}")
        dqkv.append(_unpermute(dqkv_p, d))
        ds_sums.append(ds)
    dqkv_b = _sum3_bf16(*dqkv)
    g_rb = _rel_bias_grad(ds_sums)[:, :NH]

    small_g = dict(rel_bias=g_rb, ln_v_gain=stv[0], ln_v_bias=stv[1], w_spatial=g_ws, b_spatial=g_bs[:, :, 0],
                   ln1_gain=st1[0], ln1_bias=st1[1], b_ff1=g_b1, b_ff2=st2[2], ln2_gain=st2[0], ln2_bias=st2[1])
    gs = _allreduce_small(_pack_small(small_g))
    ds_, ms_, vs_ = _adamw(_pack_small({k: args[k] for k in _SMALL}), gs,
                           _pack_small({k: args["m_" + k] for k in _SMALL}),
                           _pack_small({k: args["v_" + k] for k in _SMALL}), "adamw_small")
    like = {k: args[k] for k in _SMALL}
    grads, deltas, new_m, new_v = (_unpack_small(t, like) for t in (gs, ds_, ms_, vs_))

    dproj = jnp.concatenate([dqkv_b, duv, dga, dgb], axis=1)
    g_win = _grad_w(xb, dproj, "grad_w_in", 512, 768, True, after=gs)
    px, tok = _px_start("pair_exchange_start_w_in", [g_win])
    grad_x = _d_x(dproj, win_g, dpre1, after=tok)
    cx_in, tok = pair_and_chip("w_in", px, grad_x)

    def reduce_finish(tag, state, names, after):
        pair_sums, from_chips = _cx_wait(f"chip_exchange_wait_{tag}", state, after)
        halves = [_chip_sum(p, own, f"chip_sum_{k}") for p, own, k in zip(from_chips, pair_sums, names)]
        last = None
        for k, g in zip(names, _share_halves(halves, f"share_halves_{tag}")):
            d_, m_, v_ = _adamw(args[k][0], g, args["m_" + k][0], args["v_" + k][0], f"adamw_{k}")
            grads[k], deltas[k], new_m[k], new_v[k] = g[None], d_[None], m_[None], v_[None]
            last = d_
        return last

    done = reduce_finish("w_ff2", cx_w2, ["w_ff2"], tok)
    done = reduce_finish("w_ff1", cx_w1, ["w_ff1"], done)
    done = reduce_finish("b", cx_b, ["w_proj_a", "w_proj_b", "w_out"], done)
    reduce_finish("w_in", cx_in, ["w_in"], done)

    loss = lax.psum(st2[3, 0] * (0.5 / D), ("x", "y", "c"))
    return (loss, grad_x[None], *[grads[k] for k in weights], *[deltas[k] for k in weights],
            *[new_m[k] for k in weights], *[new_v[k] for k in weights])
```

```python
import functools
import math

import numpy as np
import jax
import jax.numpy as jnp
from jax import lax
from jax.experimental import pallas as pl
from jax.experimental.pallas import tpu as pltpu

F32 = jnp.float32
BF16 = jnp.bfloat16

S = 2048
D = 2048
DA = 1024
DB = 1024
DFF = 8192
DIN = 9216
NH = 8
HD = 128
NBLK = 16
PATTERNS = ((128, 1), (512, 4), (2048, 16))
N_BUCKETS = 32
MAX_DISTANCE = 2048
ALPHA = 2.0 ** 0.25
LN_EPS = 1e-5
NEG_INF = -1e30
SCALE = HD ** -0.5
N_CHIPS = 4

ADAM_LR = 0.001
ADAM_B1 = 0.9
ADAM_B2 = 0.999
ADAM_EPS = 1e-08
ADAM_WD = 0.01
ADAM_STEP = 10

VMEM_LIMIT = 56 * 1024 * 1024
MESH = pl.DeviceIdType.MESH
ANY = pl.BlockSpec(memory_space=pl.ANY)


def _params(n_axes, vmem=VMEM_LIMIT):
    return pltpu.CompilerParams(dimension_semantics=("arbitrary",) * n_axes, vmem_limit_bytes=vmem)


def _bucket_tile(dilation):
    qi = np.arange(128)[:, None]
    kj = np.arange(256)[None, :]
    n = np.clip(128 + qi - kj, 0, 128) * dilation
    max_exact = N_BUCKETS // 2
    nf = np.maximum(n, 1).astype(np.float32)
    large = max_exact + (np.log(nf / np.float32(max_exact)) / np.float32(math.log(MAX_DISTANCE / max_exact))
                         * np.float32(N_BUCKETS - max_exact)).astype(np.int32)
    large = np.minimum(large, N_BUCKETS - 1)
    return np.where(n < max_exact, n, large).astype(np.int32)


def _gelu(x):
    c = math.sqrt(2.0 / math.pi)
    t = jnp.tanh(c * (x + 0.044715 * x * x * x))
    return 0.5 * x * (1.0 + t), t


def _gelu_grad(x, t):
    c = math.sqrt(2.0 / math.pi)
    return 0.5 * (1.0 + t) + 0.5 * x * (1.0 - t * t) * c * (1.0 + 3.0 * 0.044715 * x * x)


def _sigmoid(x):
    return 1.0 / (1.0 + jnp.exp(-x))


def _dot(a, b):
    return jnp.dot(a, b, preferred_element_type=F32)


def _behind(body, n_in, after):
    if after is None:
        return body, [], []
    return (lambda *refs: body(*refs[:n_in], *refs[n_in + 1:])), [ANY], [after]


def _dot_nt(a, b):
    return lax.dot_general(a, b, (((1,), (1,)), ((), ())), preferred_element_type=F32)


def _proj(xb, win_g):
    tn = 768
    per = 2304 // tn

    def body(x_ref, w_ref, o_ref):
        o_ref[...] = _dot(x_ref[...], w_ref[...])

    return pl.pallas_call(
        body, name="proj", grid=(DIN // tn,),
        in_specs=[pl.BlockSpec((S, D), lambda j: (0, 0)),
                  pl.BlockSpec((None, D, tn), lambda j: (j // per, 0, j % per))],
        out_specs=pl.BlockSpec((S, tn), lambda j: (0, j)),
        out_shape=jax.ShapeDtypeStruct((S, DIN), F32),
        compiler_params=_params(1),
    )(xb, win_g)


def _bias_tiles(rb_ref, bucket, bias_scr):
    qi = lax.broadcasted_iota(jnp.int32, (128, 256), 0)
    kj = lax.broadcasted_iota(jnp.int32, (128, 256), 1)
    steps = 128 + qi - kj
    band = (steps >= 0) & (steps <= 128)
    bias_scr[...] = jnp.zeros_like(bias_scr)

    def one_bucket(t, carry):
        hit = bucket == t
        for h in range(NH):
            bias_scr[h] = jnp.where(hit, rb_ref[t, h], bias_scr[h])
        return carry

    lax.fori_loop(0, N_BUCKETS, one_bucket, 0)
    for h in range(NH):
        bias_scr[h] = jnp.where(band, bias_scr[h], NEG_INF)


def _attn_fwd(qkv, rel_bias, dilation, name):
    nblk = NBLK // dilation
    bucket = jnp.asarray(_bucket_tile(dilation))

    def body(rb_ref, bucket_ref, q_ref, kc_ref, kp_ref, vc_ref, vp_ref, o_ref, lse_ref, bias_scr):
        b = pl.program_id(0)

        @pl.when(b == 0)
        def _():
            _bias_tiles(rb_ref, bucket_ref[...], bias_scr)

        has_prev = (b % nblk) != 0
        kj = lax.broadcasted_iota(jnp.int32, (128, 256), 1)
        key_ok = (kj >= 128) | has_prev
        for h in range(NH):
            cols = slice(h * HD, (h + 1) * HD)
            q = q_ref[:, cols].astype(BF16)
            s = jnp.concatenate([_dot_nt(q, kp_ref[:, cols].astype(BF16)),
                                 _dot_nt(q, kc_ref[:, cols].astype(BF16))], axis=1) * SCALE
            s = jnp.where(key_ok, s + bias_scr[h], NEG_INF)
            m = jnp.max(s, axis=1, keepdims=True)
            p = jnp.exp(s - m)
            den = jnp.sum(p, axis=1, keepdims=True)
            pb = p.astype(BF16)
            o = _dot(pb[:, :128], vp_ref[:, cols].astype(BF16)) + _dot(pb[:, 128:], vc_ref[:, cols].astype(BF16))
            o_ref[:, cols] = o / den
            lse_ref[:, cols] = jnp.broadcast_to(m + jnp.log(den), (128, HD))

    blk = lambda col, prev: pl.BlockSpec(
        (128, DA), (lambda b: (jnp.maximum(b - 1, 0), col)) if prev else (lambda b: (b, col)))
    return pl.pallas_call(
        body, name=name, grid=(NBLK,),
        in_specs=[pl.BlockSpec(memory_space=pltpu.SMEM),
                  pl.BlockSpec((128, 256), lambda b: (0, 0)),
                  blk(0, False), blk(1, False), blk(1, True), blk(2, False), blk(2, True)],
        out_specs=[pl.BlockSpec((128, DA), lambda b: (b, 0)), pl.BlockSpec((128, DA), lambda b: (b, 0))],
        out_shape=[jax.ShapeDtypeStruct((S, DA), F32), jax.ShapeDtypeStruct((S, DA), F32)],
        scratch_shapes=[pltpu.VMEM((NH, 128, 256), F32)],
        compiler_params=_params(1),
    )(rel_bias, bucket, qkv, qkv, qkv, qkv, qkv)


def _attn_combine(outs, lses):
    tm = 256

    def body(o1, o2, o3, l1, l2, l3, attn_ref, lse_ref):
        a, b, c = l1[...], l2[...], l3[...]
        m = jnp.maximum(jnp.maximum(a, b), c)
        wa, wb, wc = jnp.exp(a - m), jnp.exp(b - m), jnp.exp(c - m)
        den = wa + wb + wc
        attn_ref[...] = ((wa * o1[...] + wb * o2[...] + wc * o3[...]) / den).astype(BF16)
        lse_ref[...] = m + jnp.log(den)

    spec = pl.BlockSpec((tm, DA), lambda i: (i, 0))
    return pl.pallas_call(
        body, name="attn_combine", grid=(S // tm,),
        in_specs=[spec] * 6, out_specs=[spec, spec],
        out_shape=[jax.ShapeDtypeStruct((S, DA), BF16), jax.ShapeDtypeStruct((S, DA), F32)],
        compiler_params=_params(1),
    )(*outs, *lses)


def _gmlp_parts(u_ref, vb_ref, g_ref, be_ref):
    u = u_ref[...]
    u_act, tu = _gelu(u)
    vb = vb_ref[...]
    gv, tv = _gelu(vb)
    mean = jnp.mean(gv, axis=1, keepdims=True)
    cen = gv - mean
    var = jnp.mean(cen * cen, axis=1, keepdims=True)
    rstd = lax.rsqrt(var + LN_EPS)
    xhat = cen * rstd
    vn = xhat * g_ref[...] + be_ref[...]
    return u, tu, u_act, vb, tv, rstd, xhat, vn


def _gmlp_fwd(proj, ws, bsp_b, gain_v, bias_v):
    def body(u_ref, vb_ref, ws_ref, bsp_ref, g_ref, be_ref, o_ref):
        _, _, u_act, _, _, _, _, vn = _gmlp_parts(u_ref, vb_ref, g_ref, be_ref)
        row = lax.broadcasted_iota(jnp.int32, (128, 128), 0)
        col = lax.broadcasted_iota(jnp.int32, (128, 128), 1)
        causal = row >= col
        for g in range(NH):
            cols = slice(g * 128, (g + 1) * 128)
            wsg = jnp.where(causal, ws_ref[g], 0.0).astype(BF16)
            z = _dot(wsg, vn[:, cols].astype(BF16)) + bsp_ref[g]
            o_ref[:, cols] = (u_act[:, cols] * z).astype(BF16)

    return pl.pallas_call(
        body, name="gmlp_fwd", grid=(NBLK,),
        in_specs=[pl.BlockSpec((128, DB), lambda c: (c, 3)), pl.BlockSpec((128, DB), lambda c: (c, 4)),
                  pl.BlockSpec((NH, 128, 128), lambda c: (0, 0, 0)), pl.BlockSpec((NH, 128, 128), lambda c: (0, 0, 0)),
                  pl.BlockSpec((1, DB), lambda c: (0, 0)), pl.BlockSpec((1, DB), lambda c: (0, 0))],
        out_specs=pl.BlockSpec((128, DB), lambda c: (c, 0)),
        out_shape=jax.ShapeDtypeStruct((S, DB), BF16),
        compiler_params=_params(1),
    )(proj, proj, ws, bsp_b, gain_v, bias_v)


def _branch(attn, gmlp, wpa_g, wpb_g, proj):
    tn = 512

    def body(a_ref, g_ref, wa_ref, wb_ref, ga_ref, gb_ref, ya_ref, yb_ref, mg_ref):
        ya = _dot(a_ref[...], wa_ref[...])
        yb = _dot(g_ref[...], wb_ref[...])
        ya_ref[...] = ya.astype(BF16)
        yb_ref[...] = yb.astype(BF16)
        mg_ref[...] = (_sigmoid(ga_ref[...]) * ya + _sigmoid(gb_ref[...]) * yb).astype(BF16)

    out = pl.BlockSpec((S, tn), lambda j: (0, j))
    return pl.pallas_call(
        body, name="branch", grid=(D // tn,),
        in_specs=[pl.BlockSpec((S, DA), lambda j: (0, 0)), pl.BlockSpec((S, DB), lambda j: (0, 0)),
                  pl.BlockSpec((None, DA, tn), lambda j: (j, 0, 0)), pl.BlockSpec((None, DB, tn), lambda j: (j, 0, 0)),
                  pl.BlockSpec((S, tn), lambda j: (0, 5120 // tn + j)), pl.BlockSpec((S, tn), lambda j: (0, 7168 // tn + j))],
        out_specs=[out, out, out],
        out_shape=[jax.ShapeDtypeStruct((S, D), BF16)] * 3,
        compiler_params=_params(1),
    )(attn, gmlp, wpa_g, wpb_g, proj, proj)


def _out_ln1(merged, wout_g, x, gain, bias):
    tm = 256

    def body(m_ref, w_ref, x_ref, g_ref, b_ref, xh_ref, rs_ref, h_ref):
        pre = ALPHA * x_ref[...] + _dot(m_ref[...], w_ref[...])
        mean = jnp.mean(pre, axis=1, keepdims=True)
        cen = pre - mean
        var = jnp.mean(cen * cen, axis=1, keepdims=True)
        rstd = lax.rsqrt(var + LN_EPS)
        xhat = cen * rstd
        xh_ref[...] = xhat
        rs_ref[...] = jnp.broadcast_to(rstd, (tm, 128))
        h_ref[...] = (xhat * g_ref[...] + b_ref[...]).astype(BF16)

    row = pl.BlockSpec((tm, D), lambda i: (i, 0))
    vec = pl.BlockSpec((1, D), lambda i: (0, 0))
    return pl.pallas_call(
        body, name="out_ln1", grid=(S // tm,),
        in_specs=[row, pl.BlockSpec((D, D), lambda i: (0, 0)), row, vec, vec],
        out_specs=[row, pl.BlockSpec((tm, 128), lambda i: (i, 0)), row],
        out_shape=[jax.ShapeDtypeStruct((S, D), F32), jax.ShapeDtypeStruct((S, 128), F32),
                   jax.ShapeDtypeStruct((S, D), BF16)],
        compiler_params=_params(1),
    )(merged, wout_g, x, gain, bias)


def _ff1(h1b, w1_g, b1):
    tn = 512
    per = D // tn

    def body(h_ref, w_ref, b_ref, a_ref, r_ref):
        r = jnp.maximum(_dot(h_ref[...], w_ref[...]) + b_ref[...], 0.0)
        r_ref[...] = r.astype(BF16)
        a_ref[...] = (r * r).astype(BF16)

    out = pl.BlockSpec((S, tn), lambda j: (0, j))
    return pl.pallas_call(
        body, name="ff1", grid=(DFF // tn,),
        in_specs=[pl.BlockSpec((S, D), lambda j: (0, 0)),
                  pl.BlockSpec((None, D, tn), lambda j: (j // per, 0, j % per)),
                  pl.BlockSpec((1, tn), lambda j: (0, j))],
        out_specs=[out, out],
        out_shape=[jax.ShapeDtypeStruct((S, DFF), BF16)] * 2,
        compiler_params=_params(1),
    )(h1b, w1_g, b1)


def _ff2_ln2_loss(a, w2_g, xhat1, g1, b1, b2, g2, be2, target):
    tm, tk = 512, 1024
    nk = DFF // tk

    def body(a_ref, w_ref, xh_ref, g1_ref, b1_ref, b2_ref, g2_ref, be2_ref, t_ref, d_ref, db_ref, st_ref, acc):
        i, k = pl.program_id(0), pl.program_id(1)

        @pl.when(k == 0)
        def _():
            acc[...] = jnp.zeros_like(acc)

        @pl.when((i == 0) & (k == 0))
        def _():
            st_ref[...] = jnp.zeros_like(st_ref)

        acc[...] += _dot(a_ref[...], w_ref[...])

        @pl.when(k == nk - 1)
        def _():
            def rows_chunk(ci, carry):
                rows = pl.ds(pl.multiple_of(ci * 128, 128), 128)
                h1 = xh_ref[rows, :] * g1_ref[...] + b1_ref[...]
                pre = ALPHA * h1 + acc[rows, :] + b2_ref[...]
                mean = jnp.mean(pre, axis=1, keepdims=True)
                cen = pre - mean
                var = jnp.mean(cen * cen, axis=1, keepdims=True)
                rstd = lax.rsqrt(var + LN_EPS)
                xhat = cen * rstd
                y = xhat * g2_ref[...] + be2_ref[...]
                err = y - t_ref[rows, :]
                dy = err * (1.0 / D)
                g = dy * g2_ref[...]
                dpre = rstd * (g - jnp.mean(g, axis=1, keepdims=True)
                               - xhat * jnp.mean(g * xhat, axis=1, keepdims=True))
                d_ref[rows, :] = dpre
                db_ref[rows, :] = dpre.astype(BF16)
                st_ref[0:1, :] += jnp.sum(dy * xhat, axis=0, keepdims=True)
                st_ref[1:2, :] += jnp.sum(dy, axis=0, keepdims=True)
                st_ref[2:3, :] += jnp.sum(dpre, axis=0, keepdims=True)
                st_ref[3:4, :] += jnp.broadcast_to(jnp.sum(err * err).reshape(1, 1), (1, D))
                return carry

            lax.fori_loop(0, tm // 128, rows_chunk, 0)

    row = pl.BlockSpec((tm, D), lambda i, k: (i, 0))
    vec = pl.BlockSpec((1, D), lambda i, k: (0, 0))
    return pl.pallas_call(
        body, name="ff2_ln2_loss", grid=(S // tm, nk),
        in_specs=[pl.BlockSpec((tm, tk), lambda i, k: (i, k)), pl.BlockSpec((tk, D), lambda i, k: (k, 0)),
                  row, vec, vec, vec, vec, vec, row],
        out_specs=[row, row, pl.BlockSpec((8, D), lambda i, k: (0, 0))],
        out_shape=[jax.ShapeDtypeStruct((S, D), F32), jax.ShapeDtypeStruct((S, D), BF16),
                   jax.ShapeDtypeStruct((8, D), F32)],
        scratch_shapes=[pltpu.VMEM((tm, D), F32)],
        compiler_params=_params(2),
    )(a, w2_g, xhat1, g1, b1, b2, g2, be2, target)


def _grad_w(act, dout, name, ti, tj, sharded, after=None):
    m, n = act.shape[1], dout.shape[1]
    ns = n // N_CHIPS
    per = ns // tj if sharded else None

    def body(a_ref, b_ref, o_ref, at_scr):
        @pl.when(pl.program_id(1) == 0)
        def _():
            at_scr[...] = a_ref[...].T

        o_ref[...] = _dot(at_scr[...], b_ref[...])

    if sharded:
        out_spec = pl.BlockSpec((None, ti, tj), lambda i, j: (j // per, i, j % per))
        out_shape = jax.ShapeDtypeStruct((N_CHIPS, m, ns), F32)
    else:
        out_spec = pl.BlockSpec((ti, tj), lambda i, j: (i, j))
        out_shape = jax.ShapeDtypeStruct((m, n), F32)
    body, more_specs, more = _behind(body, 2, after)
    return pl.pallas_call(
        body, name=name, grid=(m // ti, n // tj),
        in_specs=[pl.BlockSpec((S, ti), lambda i, j: (0, i)), pl.BlockSpec((S, tj), lambda i, j: (0, j))] + more_specs,
        out_specs=out_spec, out_shape=out_shape,
        scratch_shapes=[pltpu.VMEM((ti, S), BF16)],
        compiler_params=_params(2),
    )(act, dout, *more)


def _d_ff1(dpre2b, w2_g, r, after=None):
    tn = 512

    def body(d_ref, w_ref, r_ref, o_ref, gb_ref):
        da = _dot_nt(d_ref[...], w_ref[...])
        dp = da * (2.0 * r_ref[...].astype(F32))
        o_ref[...] = dp.astype(BF16)
        gb_ref[...] = jnp.sum(dp, axis=0, keepdims=True)

    body, more_specs, more = _behind(body, 3, after)
    return pl.pallas_call(
        body, name="d_ff1", grid=(DFF // tn,),
        in_specs=[pl.BlockSpec((S, D), lambda j: (0, 0)), pl.BlockSpec((tn, D), lambda j: (j, 0)),
                  pl.BlockSpec((S, tn), lambda j: (0, j))] + more_specs,
        out_specs=[pl.BlockSpec((S, tn), lambda j: (0, j)), pl.BlockSpec((1, tn), lambda j: (0, j))],
        out_shape=[jax.ShapeDtypeStruct((S, DFF), BF16), jax.ShapeDtypeStruct((1, DFF), F32)],
        compiler_params=_params(1),
    )(dpre2b, w2_g, r, *more)


def _d_h1_ln1(dprea, w1_g, dpre2, xhat1, rstd1, g1, after=None):
    tm, tk = 512, 1024
    per = D // tk
    nk = DFF // tk

    def body(a_ref, w_ref, d2_ref, xh_ref, rs_ref, g_ref, d_ref, db_ref, st_ref, acc):
        i, k = pl.program_id(0), pl.program_id(1)

        @pl.when(k == 0)
        def _():
            acc[...] = jnp.zeros_like(acc)

        @pl.when((i == 0) & (k == 0))
        def _():
            st_ref[...] = jnp.zeros_like(st_ref)

        acc[...] += _dot_nt(a_ref[...], w_ref[...])

        @pl.when(k == nk - 1)
        def _():
            def rows_chunk(ci, carry):
                rows = pl.ds(pl.multiple_of(ci * 128, 128), 128)
                dh = ALPHA * d2_ref[rows, :] + acc[rows, :]
                xhat = xh_ref[rows, :]
                g = dh * g_ref[...]
                dpre = rs_ref[rows, 0:1] * (g - jnp.mean(g, axis=1, keepdims=True)
                                            - xhat * jnp.mean(g * xhat, axis=1, keepdims=True))
                d_ref[rows, :] = dpre
                db_ref[rows, :] = dpre.astype(BF16)
                st_ref[0:1, :] += jnp.sum(dh * xhat, axis=0, keepdims=True)
                st_ref[1:2, :] += jnp.sum(dh, axis=0, keepdims=True)
                return carry

            lax.fori_loop(0, tm // 128, rows_chunk, 0)

    row = pl.BlockSpec((tm, D), lambda i, k: (i, 0))
    body, more_specs, more = _behind(body, 6, after)
    return pl.pallas_call(
        body, name="d_h1_ln1", grid=(S // tm, nk),
        in_specs=[pl.BlockSpec((tm, tk), lambda i, k: (i, k)),
                  pl.BlockSpec((None, D, tk), lambda i, k: (k // per, 0, k % per)),
                  row, row, pl.BlockSpec((tm, 128), lambda i, k: (i, 0)), pl.BlockSpec((1, D), lambda i, k: (0, 0))]
        + more_specs,
        out_specs=[row, row, pl.BlockSpec((8, D), lambda i, k: (0, 0))],
        out_shape=[jax.ShapeDtypeStruct((S, D), F32), jax.ShapeDtypeStruct((S, D), BF16),
                   jax.ShapeDtypeStruct((8, D), F32)],
        scratch_shapes=[pltpu.VMEM((tm, D), F32)],
        compiler_params=_params(2),
    )(dprea, w1_g, dpre2, xhat1, rstd1, g1, *more)


def _d_merged(dpre1b, wout_g, proj, ya, yb):
    tm, tn = 512, 1024

    def body(d_ref, w_ref, ga_ref, gb_ref, ya_ref, yb_ref, dya_ref, dyb_ref, dga_ref, dgb_ref):
        dm = _dot_nt(d_ref[...], w_ref[...])
        sa = _sigmoid(ga_ref[...])
        sb = _sigmoid(gb_ref[...])
        dya_ref[...] = (dm * sa).astype(BF16)
        dyb_ref[...] = (dm * sb).astype(BF16)
        dga_ref[...] = (dm * ya_ref[...].astype(F32) * sa * (1.0 - sa)).astype(BF16)
        dgb_ref[...] = (dm * yb_ref[...].astype(F32) * sb * (1.0 - sb)).astype(BF16)

    tile = pl.BlockSpec((tm, tn), lambda i, j: (i, j))
    return pl.pallas_call(
        body, name="d_merged", grid=(S // tm, D // tn),
        in_specs=[pl.BlockSpec((tm, D), lambda i, j: (i, 0)), pl.BlockSpec((tn, D), lambda i, j: (j, 0)),
                  pl.BlockSpec((tm, tn), lambda i, j: (i, 5 + j)), pl.BlockSpec((tm, tn), lambda i, j: (i, 7 + j)),
                  tile, tile],
        out_specs=[tile] * 4,
        out_shape=[jax.ShapeDtypeStruct((S, D), BF16)] * 4,
        compiler_params=_params(2),
    )(dpre1b, wout_g, proj, proj, ya, yb)


def _d_branches(dya, dyb, wpa_g, wpb_g, after=None):
    tk = 512

    def body(da_ref, db_ref, wa_ref, wb_ref, oa_ref, ob_ref):
        @pl.when(pl.program_id(0) == 0)
        def _():
            oa_ref[...] = jnp.zeros_like(oa_ref)
            ob_ref[...] = jnp.zeros_like(ob_ref)

        oa_ref[...] += _dot_nt(da_ref[...], wa_ref[...])
        ob_ref[...] += _dot_nt(db_ref[...], wb_ref[...])

    body, more_specs, more = _behind(body, 4, after)
    return pl.pallas_call(
        body, name="d_branches", grid=(D // tk,),
        in_specs=[pl.BlockSpec((S, tk), lambda k: (0, k)), pl.BlockSpec((S, tk), lambda k: (0, k)),
                  pl.BlockSpec((None, DA, tk), lambda k: (k, 0, 0)), pl.BlockSpec((None, DB, tk), lambda k: (k, 0, 0))]
        + more_specs,
        out_specs=[pl.BlockSpec((S, DA), lambda k: (0, 0)), pl.BlockSpec((S, DB), lambda k: (0, 0))],
        out_shape=[jax.ShapeDtypeStruct((S, DA), F32), jax.ShapeDtypeStruct((S, DB), F32)],
        compiler_params=_params(1),
    )(dya, dyb, wpa_g, wpb_g, *more)


def _gmlp_bwd(proj, dgmlp, ws, ws_t, bsp_b, gain_v, bias_v):
    def body(u_ref, vb_ref, dg_ref, ws_ref, wst_ref, bsp_ref, g_ref, be_ref, duv_ref, gws_ref, gbs_ref, st_ref):
        @pl.when(pl.program_id(0) == 0)
        def _():
            gws_ref[...] = jnp.zeros_like(gws_ref)
            gbs_ref[...] = jnp.zeros_like(gbs_ref)
            st_ref[...] = jnp.zeros_like(st_ref)

        u, tu, u_act, vb, tv, rstd, xhat, vn = _gmlp_parts(u_ref, vb_ref, g_ref, be_ref)
        dg = dg_ref[...]
        dz = dg * u_act
        row = lax.broadcasted_iota(jnp.int32, (128, 128), 0)
        col = lax.broadcasted_iota(jnp.int32, (128, 128), 1)
        causal = row >= col
        causal_t = row <= col
        dvn_parts = []
        z_parts = []
        for g in range(NH):
            cols = slice(g * 128, (g + 1) * 128)
            vng = vn[:, cols].astype(BF16)
            dzg = dz[:, cols]
            dzb = dzg.astype(BF16)
            wsg = jnp.where(causal, ws_ref[g], 0.0).astype(BF16)
            wsg_t = jnp.where(causal_t, wst_ref[g], 0.0).astype(BF16)
            z_parts.append(_dot(wsg, vng) + bsp_ref[g])
            gws_ref[g] += jnp.where(causal, _dot_nt(dzb, vng), 0.0)
            gbs_ref[g] += jnp.broadcast_to(jnp.sum(dzg, axis=1, keepdims=True), (128, 128))
            dvn_parts.append(_dot(wsg_t, dzb))
        z = jnp.concatenate(z_parts, axis=1)
        dvn = jnp.concatenate(dvn_parts, axis=1)
        du = dg * z * _gelu_grad(u, tu)
        st_ref[0:1, :] += jnp.sum(dvn * xhat, axis=0, keepdims=True)
        st_ref[1:2, :] += jnp.sum(dvn, axis=0, keepdims=True)
        gg = dvn * g_ref[...]
        dgv = rstd * (gg - jnp.mean(gg, axis=1, keepdims=True) - xhat * jnp.mean(gg * xhat, axis=1, keepdims=True))
        dvb = dgv * _gelu_grad(vb, tv)
        duv_ref[:, 0:DB] = du.astype(BF16)
        duv_ref[:, DB:2 * DB] = dvb.astype(BF16)

    full3 = pl.BlockSpec((NH, 128, 128), lambda c: (0, 0, 0))
    vec = pl.BlockSpec((1, DB), lambda c: (0, 0))
    return pl.pallas_call(
        body, name="gmlp_bwd", grid=(NBLK,),
        in_specs=[pl.BlockSpec((128, DB), lambda c: (c, 3)), pl.BlockSpec((128, DB), lambda c: (c, 4)),
                  pl.BlockSpec((128, DB), lambda c: (c, 0)), full3, full3, full3, vec, vec],
        out_specs=[pl.BlockSpec((128, 2 * DB), lambda c: (c, 0)), full3, full3, pl.BlockSpec((8, DB), lambda c: (0, 0))],
        out_shape=[jax.ShapeDtypeStruct((S, 2 * DB), BF16), jax.ShapeDtypeStruct((NH, 128, 128), F32),
                   jax.ShapeDtypeStruct((NH, 128, 128), F32), jax.ShapeDtypeStruct((8, DB), F32)],
        compiler_params=_params(1),
    )(proj, proj, dgmlp, ws, ws_t, bsp_b, gain_v, bias_v)


def _attn_delta(dattn, attn, after=None):
    tm = 256

    def body(d_ref, o_ref, dl_ref, db_ref):
        d = d_ref[...]
        prod = d * o_ref[...].astype(F32)
        for h in range(NH):
            cols = slice(h * HD, (h + 1) * HD)
            dl_ref[:, cols] = jnp.broadcast_to(jnp.sum(prod[:, cols], axis=1, keepdims=True), (tm, HD))
        db_ref[...] = d.astype(BF16)

    spec = pl.BlockSpec((tm, DA), lambda i: (i, 0))
    body, more_specs, more = _behind(body, 2, after)
    return pl.pallas_call(
        body, name="attn_delta", grid=(S // tm,),
        in_specs=[spec, spec] + more_specs, out_specs=[spec, spec],
        out_shape=[jax.ShapeDtypeStruct((S, DA), F32), jax.ShapeDtypeStruct((S, DA), BF16)],
        compiler_params=_params(1),
    )(dattn, attn, *more)


def _attn_bwd(qkv, dob, lse, delta, rel_bias, dilation, name):
    nblk = NBLK // dilation
    bucket = jnp.asarray(_bucket_tile(dilation))

    def body(rb_ref, bucket_ref, q_ref, qn_ref, kc_ref, kp_ref, vc_ref, vp_ref, do_ref, don_ref,
             l_ref, ln_ref, dl_ref, dln_ref, dqkv_ref, ds_ref, bias_scr):
        b = pl.program_id(0)

        @pl.when(b == 0)
        def _():
            _bias_tiles(rb_ref, bucket_ref[...], bias_scr)
            ds_ref[...] = jnp.zeros_like(ds_ref)

        has_prev = (b % nblk) != 0
        has_next = ((b + 1) % nblk) != 0
        for h in range(NH):
            cols = slice(h * HD, (h + 1) * HD)
            q = q_ref[:, cols].astype(BF16)
            kc = kc_ref[:, cols].astype(BF16)
            kp = kp_ref[:, cols].astype(BF16)
            vc = vc_ref[:, cols].astype(BF16)
            vp = vp_ref[:, cols].astype(BF16)
            do = do_ref[:, cols]
            bias_p = bias_scr[h, :, 0:128]
            bias_c = bias_scr[h, :, 128:256]
            lse_b = l_ref[:, cols]
            dl_b = dl_ref[:, cols]
            p_c = jnp.exp(_dot_nt(q, kc) * SCALE + bias_c - lse_b)
            p_p = jnp.where(has_prev, jnp.exp(_dot_nt(q, kp) * SCALE + bias_p - lse_b), 0.0)
            ds_c = p_c * (_dot_nt(do, vc) - dl_b)
            ds_p = p_p * (_dot_nt(do, vp) - dl_b)
            ds_ref[h, :, 0:128] += ds_p
            ds_ref[h, :, 128:256] += ds_c
            ds_cb = ds_c.astype(BF16)
            dqkv_ref[:, cols] = (_dot(ds_cb, kc) + _dot(ds_p.astype(BF16), kp)) * SCALE
            qn = qn_ref[:, cols].astype(BF16)
            don = don_ref[:, cols]
            p_n = jnp.where(has_next, jnp.exp(_dot_nt(qn, kc) * SCALE + bias_p - ln_ref[:, cols]), 0.0)
            ds_n = p_n * (_dot_nt(don, vc) - dln_ref[:, cols])
            dqkv_ref[:, DA + h * HD:DA + (h + 1) * HD] = (_dot(ds_c.T.astype(BF16), q) + _dot(ds_n.T.astype(BF16), qn)) * SCALE
            dqkv_ref[:, 2 * DA + h * HD:2 * DA + (h + 1) * HD] = _dot(p_c.T.astype(BF16), do) + _dot(p_n.T.astype(BF16), don)

    cur = lambda col: pl.BlockSpec((128, DA), lambda b: (b, col))
    prev = lambda col: pl.BlockSpec((128, DA), lambda b: (jnp.maximum(b - 1, 0), col))
    nxt = lambda col: pl.BlockSpec((128, DA), lambda b: (jnp.minimum(b + 1, NBLK - 1), col))
    return pl.pallas_call(
        body, name=name, grid=(NBLK,),
        in_specs=[pl.BlockSpec(memory_space=pltpu.SMEM), pl.BlockSpec((128, 256), lambda b: (0, 0)),
                  cur(0), nxt(0), cur(1), prev(1), cur(2), prev(2), cur(0), nxt(0), cur(0), nxt(0), cur(0), nxt(0)],
        out_specs=[pl.BlockSpec((128, 3 * DA), lambda b: (b, 0)), pl.BlockSpec((NH, 128, 256), lambda b: (0, 0, 0))],
        out_shape=[jax.ShapeDtypeStruct((S, 3 * DA), F32), jax.ShapeDtypeStruct((NH, 128, 256), F32)],
        scratch_shapes=[pltpu.VMEM((NH, 128, 256), F32)],
        compiler_params=_params(1),
    )(rel_bias, bucket, qkv, qkv, qkv, qkv, qkv, qkv, dob, dob, lse, lse, delta, delta)


def _sum3_bf16(a, b, c):
    tm = 256
    n = a.shape[1]

    def body(a_ref, b_ref, c_ref, o_ref):
        o_ref[...] = (a_ref[...] + b_ref[...] + c_ref[...]).astype(BF16)

    spec = pl.BlockSpec((tm, n), lambda i: (i, 0))
    return pl.pallas_call(
        body, name="dqkv_sum", grid=(S // tm,), in_specs=[spec] * 3, out_specs=spec,
        out_shape=jax.ShapeDtypeStruct((S, n), BF16), compiler_params=_params(1),
    )(a, b, c)


def _rel_bias_grad(ds_sums):
    buckets = jnp.asarray(np.stack([_bucket_tile(d) for _, d in PATTERNS]))

    def body(bk_ref, d1, d2, d3, o_ref):
        row = lax.broadcasted_iota(jnp.int32, (N_BUCKETS, 128), 0)
        lane = lax.broadcasted_iota(jnp.int32, (N_BUCKETS, 128), 1)

        def one_bucket(t, out):
            hits = [bk_ref[p] == t for p in range(3)]
            for h in range(NH):
                tot = jnp.zeros((128, 256), F32)
                for p, d in enumerate((d1, d2, d3)):
                    tot = tot + jnp.where(hits[p], d[h], 0.0)
                out = jnp.where((row == t) & (lane == h), jnp.sum(tot), out)
            return out

        o_ref[...] = lax.fori_loop(0, N_BUCKETS, one_bucket, jnp.zeros((N_BUCKETS, 128), F32))

    return pl.pallas_call(
        body, name="rel_bias_grad",
        in_specs=[pl.BlockSpec(memory_space=pltpu.VMEM)] * 4, out_specs=pl.BlockSpec(memory_space=pltpu.VMEM),
        out_shape=jax.ShapeDtypeStruct((N_BUCKETS, 128), F32),
        compiler_params=pltpu.CompilerParams(vmem_limit_bytes=VMEM_LIMIT),
    )(buckets, *ds_sums)


def _d_x(dproj, win_g, dpre1, after=None):
    tm, tk = 512, 768
    per = 2304 // tk
    nk = DIN // tk

    def body(a_ref, w_ref, d_ref, o_ref, acc):
        k = pl.program_id(1)

        @pl.when(k == 0)
        def _():
            acc[...] = ALPHA * d_ref[...]

        acc[...] += _dot_nt(a_ref[...], w_ref[...])

        @pl.when(k == nk - 1)
        def _():
            o_ref[...] = acc[...]

    row = pl.BlockSpec((tm, D), lambda i, k: (i, 0))
    body, more_specs, more = _behind(body, 3, after)
    return pl.pallas_call(
        body, name="d_x", grid=(S // tm, nk),
        in_specs=[pl.BlockSpec((tm, tk), lambda i, k: (i, k)),
                  pl.BlockSpec((None, D, tk), lambda i, k: (k // per, 0, k % per)), row] + more_specs,
        out_specs=row, out_shape=jax.ShapeDtypeStruct((S, D), F32),
        scratch_shapes=[pltpu.VMEM((tm, D), F32)],
        compiler_params=_params(2),
    )(dproj, win_g, dpre1, *more)


def _adamw(w, g, m, v, name):
    rows, cols = w.shape
    tm = max(t for t in range(8, 257, 8) if rows % t == 0)

    def body(w_ref, g_ref, m_ref, v_ref, d_ref, nm_ref, nv_ref):
        g = g_ref[...]
        m = ADAM_B1 * m_ref[...] + (1.0 - ADAM_B1) * g
        v = ADAM_B2 * v_ref[...] + (1.0 - ADAM_B2) * (g * g)
        m_hat = m / (1.0 - ADAM_B1 ** ADAM_STEP)
        v_hat = v / (1.0 - ADAM_B2 ** ADAM_STEP)
        d_ref[...] = -ADAM_LR * (m_hat / (jnp.sqrt(v_hat) + ADAM_EPS) + ADAM_WD * w_ref[...])
        nm_ref[...] = m
        nv_ref[...] = v

    spec = pl.BlockSpec((tm, cols), lambda i: (i, 0))
    return pl.pallas_call(
        body, name=name, grid=(rows // tm,), in_specs=[spec] * 4, out_specs=[spec] * 3,
        out_shape=[jax.ShapeDtypeStruct((rows, cols), F32)] * 3, compiler_params=_params(1),
    )(w, g, m, v)


def _position():
    x, y, c = lax.axis_index("x"), lax.axis_index("y"), lax.axis_index("c")
    chips = [(1 - x, y), (x, 1 - y), (1 - x, 1 - y)]
    return x, y, c, chips


def _remote(src, dst, send_sems, recv_sems, k, to):
    return pltpu.make_async_remote_copy(src_ref=src, dst_ref=dst, send_sem=send_sems.at[k], recv_sem=recv_sems.at[k],
                                        device_id=to, device_id_type=MESH)


def _place_shard(w, name):
    rows, cols = w.shape
    tm = 256
    x, y = lax.axis_index("x"), lax.axis_index("y")

    def body(chip_ref, w_ref, o_ref):
        o_ref[...] = w_ref[...].astype(BF16)

    return pl.pallas_call(
        body, name=name,
        grid_spec=pltpu.PrefetchScalarGridSpec(
            num_scalar_prefetch=1, grid=(rows // tm,),
            in_specs=[pl.BlockSpec((tm, cols), lambda i, chip: (i, 0))],
            out_specs=pl.BlockSpec((None, tm, cols), lambda i, chip: (chip[0], i, 0))),
        out_shape=jax.ShapeDtypeStruct((N_CHIPS, rows, cols), BF16),
        compiler_params=_params(1),
    )(jnp.reshape(2 * x + y, (1,)).astype(jnp.int32), w)


HBM =pl.BlockSpec(memory_space=pltpu.HBM)
SEM = pl.BlockSpec(memory_space=pltpu.SEMAPHORE)
EFFECT = pltpu.SideEffectType.DATAFLOW_SIDE_EFFECTING


def _comm_call(name, body, bufs, sems_in, sems_out, after=None, token=False):
    nb, ns, no = len(bufs), len(sems_in), len(sems_out)
    n_in = nb + ns + (after is not None)

    def wrapped(*refs):
        body(refs[:nb], refs[nb:nb + ns], refs[n_in + nb:n_in + nb + no])
        if token:
            refs[-1][...] = jnp.zeros((8, 128), F32)

    outs = pl.pallas_call(
        wrapped, name=name,
        in_specs=[HBM] * nb + [SEM] * ns + ([ANY] if after is not None else []),
        out_specs=[HBM] * nb + [SEM] * no + ([pl.BlockSpec(memory_space=pltpu.VMEM)] if token else []),
        out_shape=[pltpu.HBM(b.shape, b.dtype) for b in bufs] + [pltpu.SemaphoreType.DMA((k,)) for k in sems_out]
        + ([jax.ShapeDtypeStruct((8, 128), F32)] if token else []),
        input_output_aliases={i: i for i in range(nb)},
        compiler_params=pltpu.CompilerParams(has_side_effects=EFFECT),
    )(*[pltpu.with_memory_space_constraint(b, pltpu.HBM) for b in bufs], *sems_in, *([after] if after is not None else []))
    return list(outs[:nb]), list(outs[nb:nb + no]), (outs[-1] if token else None)


def _ag_copies(buf, send_sems, recv_sems, k0, stage):
    x, y, c, chips = _position()
    hr = buf.shape[1] // 2
    half = lambda chip, h: buf.at[chip, pl.ds(h * hr, hr), :]
    sends, arrivals = [], []
    for j, (cx, cy) in enumerate(chips):
        if stage == "ici":
            mine = half(2 * x + y, c)
            sends.append(_remote(mine, mine, send_sems, recv_sems, k0 + j, (cx, cy, c)))
            got = half(2 * cx + cy, c)
        else:
            landed = half(2 * cx + cy, c)
            sends.append(_remote(landed, landed, send_sems, recv_sems, k0 + j, (x, y, 1 - c)))
            got = half(2 * cx + cy, 1 - c)
        arrivals.append(_remote(got, got, send_sems, recv_sems, k0 + j, (x, y, c)))
    return sends, arrivals


def _ag_start(groups):
    flat = [b for g in groups for b in g]

    def body(bufs, _, sems):
        at = 0
        for gi, g in enumerate(groups):
            for wi in range(len(g)):
                for cp in _ag_copies(bufs[at], sems[2 * gi], sems[2 * gi + 1], 3 * wi, "ici")[0]:
                    cp.start()
                at += 1

    bufs, sems, _ = _comm_call("allgather_start", body, flat, [], [3 * len(g) for g in groups for _ in (0, 1)])
    out, at = [], 0
    for gi, g in enumerate(groups):
        out.append((bufs[at:at + len(g)], sems[2 * gi], sems[2 * gi + 1]))
        at += len(g)
    return out


def _ag_step(name, finish, advance, after=None):
    fin_bufs = list(finish[0]) if finish else []
    adv_bufs = list(advance[0]) if advance else []
    nf = len(fin_bufs)

    def body(bufs, sems_in, sems_out):
        if advance:
            ici_s, ici_r = sems_in[-2], sems_in[-1]
            for wi in range(len(adv_bufs)):
                buf = bufs[nf + wi]
                ici_sends, ici_arrivals = _ag_copies(buf, ici_s, ici_r, 3 * wi, "ici")
                d2d_sends, _ = _ag_copies(buf, sems_out[0], sems_out[1], 3 * wi, "d2d")
                for arrived, onward in zip(ici_arrivals, d2d_sends):
                    arrived.wait_recv()
                    onward.start()
                for cp in ici_sends:
                    cp.wait_send()
        if finish:
            for wi in range(nf):
                d2d_sends, d2d_arrivals = _ag_copies(bufs[wi], sems_in[0], sems_in[1], 3 * wi, "d2d")
                for cp in d2d_arrivals:
                    cp.wait_recv()
                for cp in d2d_sends:
                    cp.wait_send()

    sems_in = (list(finish[1:]) if finish else []) + (list(advance[1:]) if advance else [])
    bufs, sems, _ = _comm_call(name, body, fin_bufs + adv_bufs, sems_in, [3 * len(adv_bufs)] * 2 if advance else [], after)
    return bufs[:nf], ((bufs[nf:], sems[0], sems[1]) if advance else None)


def _cx_copies(src, dst, send_sems, recv_sems, k0):
    x, y, c, chips = _position()
    sends = [_remote(src.at[2 * cx + cy], dst.at[2 * x + y], send_sems, recv_sems, k0 + j, (cx, cy, c))
             for j, (cx, cy) in enumerate(chips)]
    arrivals = [_remote(dst.at[2 * cx + cy], dst.at[2 * cx + cy], send_sems, recv_sems, k0 + j, (x, y, c))
                for j, (cx, cy) in enumerate(chips)]
    return sends, arrivals


def _cx_start(name, pair_sums):
    n = len(pair_sums)
    landing = [lax.empty(p.shape, p.dtype) for p in pair_sums]

    def body(bufs, _, sems):
        for w in range(n):
            for cp in _cx_copies(bufs[w], bufs[n + w], sems[0], sems[1], 3 * w)[0]:
                cp.start()

    bufs, sems, token = _comm_call(name, body, list(pair_sums) + landing, [], [3 * n, 3 * n], token=True)
    return (bufs, sems), token


def _cx_wait(name, state, after):
    bufs, sems = state
    n = len(bufs) // 2

    def body(refs, sems_in, _):
        for w in range(n):
            sends, arrivals = _cx_copies(refs[w], refs[n + w], sems_in[0], sems_in[1], 3 * w)
            for cp in arrivals:
                cp.wait_recv()
            for cp in sends:
                cp.wait_send()

    bufs, _, _ = _comm_call(name, body, bufs, sems, [], after)
    return bufs[:n], bufs[n:]


def _px_copies(src, dst, send_sems, recv_sems, k):
    x, y, c, _ = _position()
    hr = src.shape[1] // 2
    send = _remote(src.at[:, pl.ds((1 - c) * hr, hr), :], dst, send_sems, recv_sems, k, (x, y, 1 - c))
    arrival = _remote(dst, dst, send_sems, recv_sems, k, (x, y, c))
    return send, arrival


def _px_start(name, grads):
    n = len(grads)
    landing = [lax.empty((N_CHIPS, g.shape[1] // 2, g.shape[2]), F32) for g in grads]

    def body(bufs, _, sems):
        for w in range(n):
            _px_copies(bufs[w], bufs[n + w], sems[0], sems[1], w)[0].start()

    bufs, sems, token = _comm_call(name, body, list(grads) + landing, [], [n, n], token=True)
    return (bufs, sems), token


def _px_wait(name, state, after):
    bufs, sems = state
    n = len(bufs) // 2

    def body(refs, sems_in, _):
        for w in range(n):
            send, arrival = _px_copies(refs[w], refs[n + w], sems_in[0], sems_in[1], w)
            arrival.wait_recv()
            send.wait_send()

    bufs, _, _ = _comm_call(name, body, bufs, sems, [], after)
    return bufs[:n], bufs[n:]


def _pair_sum(grad, got, name):
    _, rows, cols = grad.shape
    hr = rows // 2
    tm = min(hr, 256)
    nb = hr // tm
    c = lax.axis_index("c")

    def body(c_ref, g_ref, o_ref, out_ref):
        out_ref[...] = (g_ref[...] + o_ref[...]).astype(BF16)

    return pl.pallas_call(
        body, name=name,
        grid_spec=pltpu.PrefetchScalarGridSpec(
            num_scalar_prefetch=1, grid=(N_CHIPS, nb),
            in_specs=[pl.BlockSpec((None, tm, cols), lambda s, i, c_ref: (s, c_ref[0] * nb + i, 0)),
                      pl.BlockSpec((None, tm, cols), lambda s, i, c_ref: (s, i, 0))],
            out_specs=pl.BlockSpec((None, tm, cols), lambda s, i, c_ref: (s, i, 0))),
        out_shape=jax.ShapeDtypeStruct((N_CHIPS, hr, cols), BF16),
        compiler_params=_params(2),
    )(jnp.reshape(c, (1,)).astype(jnp.int32), grad, got)


def _chip_sum(parts, pair_sums, name):
    _, hr, cols = parts.shape
    tm = min(hr, 256)
    nb = hr // tm
    x, y, c = lax.axis_index("x"), lax.axis_index("y"), lax.axis_index("c")

    def body(pos_ref, p_ref, own_ref, o_ref):
        chip = pos_ref[0]
        own = own_ref[...].astype(F32)
        term = lambda s: jnp.where(chip == s, own, p_ref[s].astype(F32))
        o_ref[...] = ((term(0) + term(1)) + term(2)) + term(3)

    return pl.pallas_call(
        body, name=name,
        grid_spec=pltpu.PrefetchScalarGridSpec(
            num_scalar_prefetch=1, grid=(nb,),
            in_specs=[pl.BlockSpec((N_CHIPS, tm, cols), lambda i, pos: (0, i, 0)),
                      pl.BlockSpec((None, tm, cols), lambda i, pos: (pos[0], i, 0))],
            out_specs=pl.BlockSpec((tm, cols), lambda i, pos: (pos[1] * nb + i, 0))),
        out_shape=jax.ShapeDtypeStruct((2 * hr, cols), F32), compiler_params=_params(1),
    )(jnp.stack([2 * x + y, c]).astype(jnp.int32), parts, pair_sums)


def _share_halves(bufs, name):
    n = len(bufs)

    def body(*refs):
        outs = refs[n:2 * n]
        send_sems, recv_sems = refs[2 * n:]
        x, y, c, _ = _position()
        copies = []
        for w in range(n):
            hr = outs[w].shape[0] // 2
            mine = outs[w].at[pl.ds(c * hr, hr), :]
            cp = _remote(mine, mine, send_sems, recv_sems, w, (x, y, 1 - c))
            cp.start()
            copies.append(cp)
        for w in range(n):
            hr = outs[w].shape[0] // 2
            theirs = outs[w].at[pl.ds((1 - c) * hr, hr), :]
            _remote(theirs, theirs, send_sems, recv_sems, w, (x, y, c)).wait_recv()
        for cp in copies:
            cp.wait_send()

    return pl.pallas_call(
        body, name=name,
        in_specs=[ANY] * n, out_specs=[ANY] * n,
        out_shape=[jax.ShapeDtypeStruct(b.shape, b.dtype) for b in bufs],
        input_output_aliases={w: w for w in range(n)},
        scratch_shapes=[pltpu.SemaphoreType.DMA((n,)), pltpu.SemaphoreType.DMA((n,))],
    )(*bufs)


def _allreduce_small(g):
    rows = g.shape[0]

    def body(g_ref, o_ref, sib, slots, send_sems, recv_sems):
        x, y, c, chips = _position()
        me = (x, y, c)
        my_chip = 2 * x + y
        pair = _remote(g_ref, sib, send_sems, recv_sems, 0, (x, y, 1 - c))
        pair.start()
        pair.wait()
        slots[my_chip] = g_ref[...] + sib[...]
        sent = []
        for j, (cx, cy) in enumerate(chips):
            cp = _remote(slots.at[my_chip], slots.at[my_chip], send_sems, recv_sems, 1 + j, (cx, cy, c))
            cp.start()
            sent.append(cp)
        for j, (cx, cy) in enumerate(chips):
            got = slots.at[2 * cx + cy]
            _remote(got, got, send_sems, recv_sems, 1 + j, me).wait_recv()
        for cp in sent:
            cp.wait_send()
        o_ref[...] = ((slots[0] + slots[1]) + slots[2]) + slots[3]

    vm = pl.BlockSpec(memory_space=pltpu.VMEM)
    return pl.pallas_call(
        body, name="allreduce_small",
        in_specs=[vm], out_specs=vm, out_shape=jax.ShapeDtypeStruct((rows, 128), F32),
        scratch_shapes=[pltpu.VMEM((rows, 128), F32), pltpu.VMEM((N_CHIPS, rows, 128), F32),
                        pltpu.SemaphoreType.DMA((4,)), pltpu.SemaphoreType.DMA((4,))],
        compiler_params=pltpu.CompilerParams(vmem_limit_bytes=VMEM_LIMIT),
    )(g)


def _permute(a, d):
    return a if d == 1 else a.reshape(S // d, d, a.shape[1]).transpose(1, 0, 2).reshape(S, a.shape[1])


def _unpermute(a, d):
    return a if d == 1 else a.reshape(d, S // d, a.shape[1]).transpose(1, 0, 2).reshape(S, a.shape[1])


_SMALL = ("rel_bias", "ln_v_gain", "ln_v_bias", "w_spatial", "b_spatial", "ln1_gain", "ln1_bias",
          "b_ff1", "b_ff2", "ln2_gain", "ln2_bias")
_SMALL_ROWS = 1200
_LOSS_AT = (152832 // 128, 0)


def _pack_small(parts):
    flat = jnp.concatenate([parts[k].reshape(-1).astype(F32) for k in _SMALL])
    flat = jnp.pad(flat, (0, _SMALL_ROWS * 128 - flat.shape[0]))
    return flat.reshape(_SMALL_ROWS, 128)


def _unpack_small(packed, like):
    flat = packed.reshape(-1)
    out, at = {}, 0
    for k in _SMALL:
        n = math.prod(like[k].shape)
        out[k] = flat[at:at + n].reshape(like[k].shape)
        at += n
    return out


def kernel(x, w_in, rel_bias, ln_v_gain, ln_v_bias, w_spatial, b_spatial, w_proj_a, w_proj_b, w_out, ln1_gain, ln1_bias, w_ff1, b_ff1, w_ff2, b_ff2, ln2_gain, ln2_bias, loss_target, m_w_in, m_rel_bias, m_ln_v_gain, m_ln_v_bias, m_w_spatial, m_b_spatial, m_w_proj_a, m_w_proj_b, m_w_out, m_ln1_gain, m_ln1_bias, m_w_ff1, m_b_ff1, m_w_ff2, m_b_ff2, m_ln2_gain, m_ln2_bias, v_w_in, v_rel_bias, v_ln_v_gain, v_ln_v_bias, v_w_spatial, v_b_spatial, v_w_proj_a, v_w_proj_b, v_w_out, v_ln1_gain, v_ln1_bias, v_w_ff1, v_b_ff1, v_w_ff2, v_b_ff2, v_ln2_gain, v_ln2_bias):
    args = dict(locals())
    big = ("w_in", "w_proj_a", "w_proj_b", "w_out", "w_ff1", "w_ff2")
    weights = ("w_in", "rel_bias", "ln_v_gain", "ln_v_bias", "w_spatial", "b_spatial", "w_proj_a", "w_proj_b", "w_out",
               "ln1_gain", "ln1_bias", "w_ff1", "b_ff1", "w_ff2", "b_ff2", "ln2_gain", "ln2_bias")

    xs = x[0]
    target = loss_target[0]

    placed = [_place_shard(args[k][0], f"place_{k}") for k in big]
    in_a, in_b, in_c, in_d = _ag_start([placed[0:1], placed[1:4], placed[4:5], placed[5:6]])
    _, d2d_a = _ag_step("allgather_w_in_pass", None, in_a)
    (win_g,), _ = _ag_step("allgather_w_in_done", d2d_a, None)

    xb = xs.astype(BF16)
    proj = _proj(xb, win_g)
    _, d2d_b = _ag_step("allgather_b_pass", None, in_b, after=proj)
    qkv_p = [proj] + [_permute(proj[:, :3 * DA], d) for _, d in PATTERNS[1:]]
    outs, lses = [], []
    for p, (_, d) in enumerate(PATTERNS):
        o, l = _attn_fwd(qkv_p[p], rel_bias, d, f"attn_fwd_{p}")
        outs.append(_unpermute(o, d))
        lses.append(_unpermute(l, d))
    attn, lse = _attn_combine(outs, lses)
    ws = w_spatial[0]
    ws_t = jnp.transpose(ws, (0, 2, 1))
    bsp_b = jnp.broadcast_to(b_spatial[0][:, :, None], (NH, 128, 128))
    gmlp = _gmlp_fwd(proj, ws, bsp_b, ln_v_gain, ln_v_bias)
    (wpa_g, wpb_g, wout_g), d2d_c = _ag_step("allgather_b_done_c_pass", d2d_b, in_c, after=gmlp)
    wout_full = wout_g.reshape(D, D)
    ya, yb, merged = _branch(attn, gmlp, wpa_g, wpb_g, proj)
    xhat1, rstd1, h1b = _out_ln1(merged, wout_full, xs, ln1_gain, ln1_bias)
    (w1_g,), d2d_d = _ag_step("allgather_c_done_d_pass", d2d_c, in_d, after=h1b)
    a, r = _ff1(h1b, w1_g, b_ff1)
    (w2_g,), _ = _ag_step("allgather_d_done", d2d_d, None, after=a)
    w2_full = w2_g.reshape(DFF, D)
    dpre2, dpre2b, st2 = _ff2_ln2_loss(a, w2_full, xhat1, ln1_gain, ln1_bias, b_ff2, ln2_gain, ln2_bias, target)

    def pair_and_chip(tag, state, after):
        local, from_sibling = _px_wait(f"pair_exchange_wait_{tag}", state, after)
        pair_sums = [_pair_sum(g, o, f"pair_sum_{tag}_{i}") for i, (g, o) in enumerate(zip(local, from_sibling))]
        return _cx_start(f"chip_exchange_start_{tag}", pair_sums)

    g_w2 = _grad_w(a, dpre2b, "grad_w_ff2", 512, 1024, False)
    px, tok = _px_start("pair_exchange_start_w_ff2", [g_w2.reshape(N_CHIPS, DFF // N_CHIPS, D)])
    dprea, g_b1 = _d_ff1(dpre2b, w2_full, r, after=tok)
    cx_w2, tok = pair_and_chip("w_ff2", px, dprea)
    g_w1 = _grad_w(h1b, dprea, "grad_w_ff1", 512, 1024, True, after=tok)
    px, tok = _px_start("pair_exchange_start_w_ff1", [g_w1])
    dpre1, dpre1b, st1 = _d_h1_ln1(dprea, w1_g, dpre2, xhat1, rstd1, ln1_gain, after=tok)
    cx_w1, tok = pair_and_chip("w_ff1", px, dpre1b)
    g_wout = _grad_w(merged, dpre1b, "grad_w_out", 512, 1024, False, after=tok)
    dya, dyb, dga, dgb = _d_merged(dpre1b, wout_full, proj, ya, yb)
    g_wpa = _grad_w(attn, dya, "grad_w_proj_a", 512, 512, True)
    g_wpb = _grad_w(gmlp, dyb, "grad_w_proj_b", 512, 512, True)
    px, tok = _px_start("pair_exchange_start_b", [g_wpa, g_wpb, g_wout.reshape(N_CHIPS, D // N_CHIPS, D)])
    dattn, dgmlp = _d_branches(dya, dyb, wpa_g, wpb_g, after=tok)
    duv, g_ws, g_bs, stv = _gmlp_bwd(proj, dgmlp, ws, ws_t, bsp_b, ln_v_gain, ln_v_bias)
    cx_b, tok = pair_and_chip("b", px, duv)
    delta, dob = _attn_delta(dattn, attn, after=tok)
    dqkv, ds_sums = [], []
    for p, (_, d) in enumerate(PATTERNS):
        dqkv_p, ds = _attn_bwd(qkv_p[p], _permute(dob, d), _permute(lse, d), _permute(delta, d), rel_bias, d,
                               f"attn_bwd_{p}")
        dqkv.append(_unpermute(dqkv_p, d))
        ds_sums.append(ds)
    dqkv_b = _sum3_bf16(*dqkv)
    g_rb = _rel_bias_grad(ds_sums)[:, :NH]

    small_g = dict(rel_bias=g_rb, ln_v_gain=stv[0], ln_v_bias=stv[1], w_spatial=g_ws, b_spatial=g_bs[:, :, 0],
                   ln1_gain=st1[0], ln1_bias=st1[1], b_ff1=g_b1, b_ff2=st2[2], ln2_gain=st2[0], ln2_bias=st2[1])
    gs = _allreduce_small(_pack_small(small_g).at[_LOSS_AT].set(st2[3, 0]))
    ds_, ms_, vs_ = _adamw(_pack_small({k: args[k] for k in _SMALL}), gs,
                           _pack_small({k: args["m_" + k] for k in _SMALL}),
                           _pack_small({k: args["v_" + k] for k in _SMALL}), "adamw_small")
    like = {k: args[k] for k in _SMALL}
    grads, deltas, new_m, new_v = (_unpack_small(t, like) for t in (gs, ds_, ms_, vs_))

    dproj = jnp.concatenate([dqkv_b, duv, dga, dgb], axis=1)
    g_win = _grad_w(xb, dproj, "grad_w_in", 512, 768, True, after=gs)
    px, tok = _px_start("pair_exchange_start_w_in", [g_win])
    grad_x = _d_x(dproj, win_g, dpre1, after=tok)
    cx_in, tok = pair_and_chip("w_in", px, grad_x)

    def reduce_finish(tag, state, names, after):
        pair_sums, from_chips = _cx_wait(f"chip_exchange_wait_{tag}", state, after)
        halves = [_chip_sum(p, own, f"chip_sum_{k}") for p, own, k in zip(from_chips, pair_sums, names)]
        last = None
        for k, g in zip(names, _share_halves(halves, f"share_halves_{tag}")):
            d_, m_, v_ = _adamw(args[k][0], g, args["m_" + k][0], args["v_" + k][0], f"adamw_{k}")
            grads[k], deltas[k], new_m[k], new_v[k] = g[None], d_[None], m_[None], v_[None]
            last = d_
        return last

    done = reduce_finish("w_ff2", cx_w2, ["w_ff2"], tok)
    done = reduce_finish("w_ff1", cx_w1, ["w_ff1"], done)
    done = reduce_finish("b", cx_b, ["w_proj_a", "w_proj_b", "w_out"], done)
    reduce_finish("w_in", cx_in, ["w_in"], done)

    loss = gs[_LOSS_AT] * (0.5 / D)
    return (loss, grad_x[None], *[grads[k] for k in weights], *[deltas[k] for k in weights],
            *[new_m[k] for k in weights], *[new_v[k] for k in weights])
```

```python
import functools
import math

import numpy as np
import jax
import jax.numpy as jnp
from jax import lax
from jax.experimental import pallas as pl
from jax.experimental.pallas import tpu as pltpu

F32 = jnp.float32
BF16 = jnp.bfloat16

S = 2048
D = 2048
DA = 1024
DB = 1024
DFF = 8192
DIN = 9216
NH = 8
HD = 128
NBLK = 16
PATTERNS = ((128, 1), (512, 4), (2048, 16))
N_BUCKETS = 32
MAX_DISTANCE = 2048
ALPHA = 2.0 ** 0.25
LN_EPS = 1e-5
NEG_INF = -1e30
SCALE = HD ** -0.5
N_CHIPS = 4

ADAM_LR = 0.001
ADAM_B1 = 0.9
ADAM_B2 = 0.999
ADAM_EPS = 1e-08
ADAM_WD = 0.01
ADAM_STEP = 10

VMEM_LIMIT = 56 * 1024 * 1024
MESH = pl.DeviceIdType.MESH
ANY = pl.BlockSpec(memory_space=pl.ANY)


def _params(n_axes, vmem=VMEM_LIMIT):
    return pltpu.CompilerParams(dimension_semantics=("arbitrary",) * n_axes, vmem_limit_bytes=vmem)


def _bucket_tile(dilation):
    qi = np.arange(128)[:, None]
    kj = np.arange(256)[None, :]
    n = np.clip(128 + qi - kj, 0, 128) * dilation
    max_exact = N_BUCKETS // 2
    nf = np.maximum(n, 1).astype(np.float32)
    large = max_exact + (np.log(nf / np.float32(max_exact)) / np.float32(math.log(MAX_DISTANCE / max_exact))
                         * np.float32(N_BUCKETS - max_exact)).astype(np.int32)
    large = np.minimum(large, N_BUCKETS - 1)
    return np.where(n < max_exact, n, large).astype(np.int32)


def _gelu(x):
    c = math.sqrt(2.0 / math.pi)
    t = jnp.tanh(c * (x + 0.044715 * x * x * x))
    return 0.5 * x * (1.0 + t), t


def _gelu_grad(x, t):
    c = math.sqrt(2.0 / math.pi)
    return 0.5 * (1.0 + t) + 0.5 * x * (1.0 - t * t) * c * (1.0 + 3.0 * 0.044715 * x * x)


def _sigmoid(x):
    return 1.0 / (1.0 + jnp.exp(-x))


def _dot(a, b):
    return jnp.dot(a, b, preferred_element_type=F32)


def _behind(body, n_in, after):
    if after is None:
        return body, [], []
    return (lambda *refs: body(*refs[:n_in], *refs[n_in + 1:])), [ANY], [after]


def _dot_nt(a, b):
    return lax.dot_general(a, b, (((1,), (1,)), ((), ())), preferred_element_type=F32)


def _proj(xb, win_g):
    tn = 768
    per = 2304 // tn
    n_qkv = 3 * DA // tn

    def body(x_ref, w_ref, o_ref, qkv_ref):
        acc = _dot(x_ref[...], w_ref[...])
        o_ref[...] = acc

        @pl.when(pl.program_id(0) < n_qkv)
        def _():
            qkv_ref[...] = acc.astype(BF16)

    return pl.pallas_call(
        body, name="proj", grid=(DIN // tn,),
        in_specs=[pl.BlockSpec((S, D), lambda j: (0, 0)),
                  pl.BlockSpec((None, D, tn), lambda j: (j // per, 0, j % per))],
        out_specs=[pl.BlockSpec((S, tn), lambda j: (0, j)),
                   pl.BlockSpec((S, tn), lambda j: (0, jnp.minimum(j, n_qkv - 1)))],
        out_shape=[jax.ShapeDtypeStruct((S, DIN), F32), jax.ShapeDtypeStruct((S, 3 * DA), BF16)],
        compiler_params=_params(1),
    )(xb, win_g)


def _bias_tiles(rb_ref, bucket, bias_scr):
    qi = lax.broadcasted_iota(jnp.int32, (128, 256), 0)
    kj = lax.broadcasted_iota(jnp.int32, (128, 256), 1)
    steps = 128 + qi - kj
    band = (steps >= 0) & (steps <= 128)
    bias_scr[...] = jnp.zeros_like(bias_scr)

    def one_bucket(t, carry):
        hit = bucket == t
        for h in range(NH):
            bias_scr[h] = jnp.where(hit, rb_ref[t, h], bias_scr[h])
        return carry

    lax.fori_loop(0, N_BUCKETS, one_bucket, 0)
    for h in range(NH):
        bias_scr[h] = jnp.where(band, bias_scr[h], NEG_INF)


def _attn_fwd(qkv, rel_bias, dilation, name):
    nblk = NBLK // dilation
    bucket = jnp.asarray(_bucket_tile(dilation))

    def body(rb_ref, bucket_ref, q_ref, kc_ref, kp_ref, vc_ref, vp_ref, o_ref, lse_ref, bias_scr):
        b = pl.program_id(0)

        @pl.when(b == 0)
        def _():
            _bias_tiles(rb_ref, bucket_ref[...], bias_scr)

        has_prev = (b % nblk) != 0
        kj = lax.broadcasted_iota(jnp.int32, (128, 256), 1)
        key_ok = (kj >= 128) | has_prev
        for h in range(NH):
            cols = slice(h * HD, (h + 1) * HD)
            q = q_ref[:, cols]
            s = jnp.concatenate([_dot_nt(q, kp_ref[:, cols]), _dot_nt(q, kc_ref[:, cols])], axis=1) * SCALE
            s = jnp.where(key_ok, s + bias_scr[h], NEG_INF)
            m = jnp.max(s, axis=1, keepdims=True)
            p = jnp.exp(s - m)
            den = jnp.sum(p, axis=1, keepdims=True)
            pb = p.astype(BF16)
            o = _dot(pb[:, :128], vp_ref[:, cols]) + _dot(pb[:, 128:], vc_ref[:, cols])
            o_ref[:, cols] = (o / den).astype(BF16)
            lse_ref[:, cols] = jnp.broadcast_to(m + jnp.log(den), (128, HD))

    blk = lambda col, prev: pl.BlockSpec(
        (128, DA), (lambda b: (jnp.maximum(b - 1, 0), col)) if prev else (lambda b: (b, col)))
    return pl.pallas_call(
        body, name=name, grid=(NBLK,),
        in_specs=[pl.BlockSpec(memory_space=pltpu.SMEM),
                  pl.BlockSpec((128, 256), lambda b: (0, 0)),
                  blk(0, False), blk(1, False), blk(1, True), blk(2, False), blk(2, True)],
        out_specs=[pl.BlockSpec((128, DA), lambda b: (b, 0)), pl.BlockSpec((128, DA), lambda b: (b, 0))],
        out_shape=[jax.ShapeDtypeStruct((S, DA), BF16), jax.ShapeDtypeStruct((S, DA), F32)],
        scratch_shapes=[pltpu.VMEM((NH, 128, 256), F32)],
        compiler_params=_params(1),
    )(rel_bias, bucket, qkv, qkv, qkv, qkv, qkv)


def _attn_combine(outs, lses):
    tm = 256

    def body(o1, o2, o3, l1, l2, l3, attn_ref, lse_ref):
        a, b, c = l1[...], l2[...], l3[...]
        m = jnp.maximum(jnp.maximum(a, b), c)
        wa, wb, wc = jnp.exp(a - m), jnp.exp(b - m), jnp.exp(c - m)
        den = wa + wb + wc
        mixed = wa * o1[...].astype(F32) + wb * o2[...].astype(F32) + wc * o3[...].astype(F32)
        attn_ref[...] = (mixed / den).astype(BF16)
        lse_ref[...] = m + jnp.log(den)

    spec = pl.BlockSpec((tm, DA), lambda i: (i, 0))
    return pl.pallas_call(
        body, name="attn_combine", grid=(S // tm,),
        in_specs=[spec] * 6, out_specs=[spec, spec],
        out_shape=[jax.ShapeDtypeStruct((S, DA), BF16), jax.ShapeDtypeStruct((S, DA), F32)],
        compiler_params=_params(1),
    )(*outs, *lses)


def _gmlp_parts(u_ref, vb_ref, g_ref, be_ref):
    u = u_ref[...]
    u_act, tu = _gelu(u)
    vb = vb_ref[...]
    gv, tv = _gelu(vb)
    mean = jnp.mean(gv, axis=1, keepdims=True)
    cen = gv - mean
    var = jnp.mean(cen * cen, axis=1, keepdims=True)
    rstd = lax.rsqrt(var + LN_EPS)
    xhat = cen * rstd
    vn = xhat * g_ref[...] + be_ref[...]
    return u, tu, u_act, vb, tv, rstd, xhat, vn


def _gmlp_fwd(proj, ws, bsp_b, gain_v, bias_v):
    def body(u_ref, vb_ref, ws_ref, bsp_ref, g_ref, be_ref, o_ref):
        _, _, u_act, _, _, _, _, vn = _gmlp_parts(u_ref, vb_ref, g_ref, be_ref)
        row = lax.broadcasted_iota(jnp.int32, (128, 128), 0)
        col = lax.broadcasted_iota(jnp.int32, (128, 128), 1)
        causal = row >= col
        for g in range(NH):
            cols = slice(g * 128, (g + 1) * 128)
            wsg = jnp.where(causal, ws_ref[g], 0.0).astype(BF16)
            z = _dot(wsg, vn[:, cols].astype(BF16)) + bsp_ref[g]
            o_ref[:, cols] = (u_act[:, cols] * z).astype(BF16)

    return pl.pallas_call(
        body, name="gmlp_fwd", grid=(NBLK,),
        in_specs=[pl.BlockSpec((128, DB), lambda c: (c, 3)), pl.BlockSpec((128, DB), lambda c: (c, 4)),
                  pl.BlockSpec((NH, 128, 128), lambda c: (0, 0, 0)), pl.BlockSpec((NH, 128, 128), lambda c: (0, 0, 0)),
                  pl.BlockSpec((1, DB), lambda c: (0, 0)), pl.BlockSpec((1, DB), lambda c: (0, 0))],
        out_specs=pl.BlockSpec((128, DB), lambda c: (c, 0)),
        out_shape=jax.ShapeDtypeStruct((S, DB), BF16),
        compiler_params=_params(1),
    )(proj, proj, ws, bsp_b, gain_v, bias_v)


def _branch(attn, gmlp, wpa_g, wpb_g, proj):
    tn = 512

    def body(a_ref, g_ref, wa_ref, wb_ref, ga_ref, gb_ref, ya_ref, yb_ref, mg_ref):
        ya = _dot(a_ref[...], wa_ref[...])
        yb = _dot(g_ref[...], wb_ref[...])
        ya_ref[...] = ya.astype(BF16)
        yb_ref[...] = yb.astype(BF16)
        mg_ref[...] = (_sigmoid(ga_ref[...]) * ya + _sigmoid(gb_ref[...]) * yb).astype(BF16)

    out = pl.BlockSpec((S, tn), lambda j: (0, j))
    return pl.pallas_call(
        body, name="branch", grid=(D // tn,),
        in_specs=[pl.BlockSpec((S, DA), lambda j: (0, 0)), pl.BlockSpec((S, DB), lambda j: (0, 0)),
                  pl.BlockSpec((None, DA, tn), lambda j: (j, 0, 0)), pl.BlockSpec((None, DB, tn), lambda j: (j, 0, 0)),
                  pl.BlockSpec((S, tn), lambda j: (0, 5120 // tn + j)), pl.BlockSpec((S, tn), lambda j: (0, 7168 // tn + j))],
        out_specs=[out, out, out],
        out_shape=[jax.ShapeDtypeStruct((S, D), BF16)] * 3,
        compiler_params=_params(1),
    )(attn, gmlp, wpa_g, wpb_g, proj, proj)


def _out_ln1(merged, wout_g, x, gain, bias):
    tm = 256

    def body(m_ref, w_ref, x_ref, g_ref, b_ref, xh_ref, rs_ref, h_ref):
        pre = ALPHA * x_ref[...] + _dot(m_ref[...], w_ref[...])
        mean = jnp.mean(pre, axis=1, keepdims=True)
        cen = pre - mean
        var = jnp.mean(cen * cen, axis=1, keepdims=True)
        rstd = lax.rsqrt(var + LN_EPS)
        xhat = cen * rstd
        xh_ref[...] = xhat
        rs_ref[...] = jnp.broadcast_to(rstd, (tm, 128))
        h_ref[...] = (xhat * g_ref[...] + b_ref[...]).astype(BF16)

    row = pl.BlockSpec((tm, D), lambda i: (i, 0))
    vec = pl.BlockSpec((1, D), lambda i: (0, 0))
    return pl.pallas_call(
        body, name="out_ln1", grid=(S // tm,),
        in_specs=[row, pl.BlockSpec((D, D), lambda i: (0, 0)), row, vec, vec],
        out_specs=[row, pl.BlockSpec((tm, 128), lambda i: (i, 0)), row],
        out_shape=[jax.ShapeDtypeStruct((S, D), F32), jax.ShapeDtypeStruct((S, 128), F32),
                   jax.ShapeDtypeStruct((S, D), BF16)],
        compiler_params=_params(1),
    )(merged, wout_g, x, gain, bias)


def _ff1(h1b, w1_g, b1):
    tn = 512
    per = D // tn

    def body(h_ref, w_ref, b_ref, a_ref, r_ref):
        r = jnp.maximum(_dot(h_ref[...], w_ref[...]) + b_ref[...], 0.0)
        r_ref[...] = r.astype(BF16)
        a_ref[...] = (r * r).astype(BF16)

    out = pl.BlockSpec((S, tn), lambda j: (0, j))
    return pl.pallas_call(
        body, name="ff1", grid=(DFF // tn,),
        in_specs=[pl.BlockSpec((S, D), lambda j: (0, 0)),
                  pl.BlockSpec((None, D, tn), lambda j: (j // per, 0, j % per)),
                  pl.BlockSpec((1, tn), lambda j: (0, j))],
        out_specs=[out, out],
        out_shape=[jax.ShapeDtypeStruct((S, DFF), BF16)] * 2,
        compiler_params=_params(1),
    )(h1b, w1_g, b1)


def _ff2_ln2_loss(a, w2_g, xhat1, g1, b1, b2, g2, be2, target):
    tm, tk = 512, 1024
    nk = DFF // tk

    def body(a_ref, w_ref, xh_ref, g1_ref, b1_ref, b2_ref, g2_ref, be2_ref, t_ref, d_ref, db_ref, st_ref, acc):
        i, k = pl.program_id(0), pl.program_id(1)

        @pl.when(k == 0)
        def _():
            acc[...] = jnp.zeros_like(acc)

        @pl.when((i == 0) & (k == 0))
        def _():
            st_ref[...] = jnp.zeros_like(st_ref)

        acc[...] += _dot(a_ref[...], w_ref[...])

        @pl.when(k == nk - 1)
        def _():
            def rows_chunk(ci, carry):
                rows = pl.ds(pl.multiple_of(ci * 128, 128), 128)
                h1 = xh_ref[rows, :] * g1_ref[...] + b1_ref[...]
                pre = ALPHA * h1 + acc[rows, :] + b2_ref[...]
                mean = jnp.mean(pre, axis=1, keepdims=True)
                cen = pre - mean
                var = jnp.mean(cen * cen, axis=1, keepdims=True)
                rstd = lax.rsqrt(var + LN_EPS)
                xhat = cen * rstd
                y = xhat * g2_ref[...] + be2_ref[...]
                err = y - t_ref[rows, :]
                dy = err * (1.0 / D)
                g = dy * g2_ref[...]
                dpre = rstd * (g - jnp.mean(g, axis=1, keepdims=True)
                               - xhat * jnp.mean(g * xhat, axis=1, keepdims=True))
                d_ref[rows, :] = dpre
                db_ref[rows, :] = dpre.astype(BF16)
                st_ref[0:1, :] += jnp.sum(dy * xhat, axis=0, keepdims=True)
                st_ref[1:2, :] += jnp.sum(dy, axis=0, keepdims=True)
                st_ref[2:3, :] += jnp.sum(dpre, axis=0, keepdims=True)
                st_ref[3:4, :] += jnp.broadcast_to(jnp.sum(err * err).reshape(1, 1), (1, D))
                return carry

            lax.fori_loop(0, tm // 128, rows_chunk, 0)

    row = pl.BlockSpec((tm, D), lambda i, k: (i, 0))
    vec = pl.BlockSpec((1, D), lambda i, k: (0, 0))
    return pl.pallas_call(
        body, name="ff2_ln2_loss", grid=(S // tm, nk),
        in_specs=[pl.BlockSpec((tm, tk), lambda i, k: (i, k)), pl.BlockSpec((tk, D), lambda i, k: (k, 0)),
                  row, vec, vec, vec, vec, vec, row],
        out_specs=[row, row, pl.BlockSpec((8, D), lambda i, k: (0, 0))],
        out_shape=[jax.ShapeDtypeStruct((S, D), F32), jax.ShapeDtypeStruct((S, D), BF16),
                   jax.ShapeDtypeStruct((8, D), F32)],
        scratch_shapes=[pltpu.VMEM((tm, D), F32)],
        compiler_params=_params(2),
    )(a, w2_g, xhat1, g1, b1, b2, g2, be2, target)


def _grad_w(act, dout, name, ti, tj, sharded, after=None):
    m, n = act.shape[1], dout.shape[1]
    ns = n // N_CHIPS
    per = ns // tj if sharded else None

    def body(a_ref, b_ref, o_ref, at_scr):
        @pl.when(pl.program_id(1) == 0)
        def _():
            at_scr[...] = a_ref[...].T

        o_ref[...] = _dot(at_scr[...], b_ref[...])

    if sharded:
        out_spec = pl.BlockSpec((None, ti, tj), lambda i, j: (j // per, i, j % per))
        out_shape = jax.ShapeDtypeStruct((N_CHIPS, m, ns), F32)
    else:
        out_spec = pl.BlockSpec((ti, tj), lambda i, j: (i, j))
        out_shape = jax.ShapeDtypeStruct((m, n), F32)
    body, more_specs, more = _behind(body, 2, after)
    return pl.pallas_call(
        body, name=name, grid=(m // ti, n // tj),
        in_specs=[pl.BlockSpec((S, ti), lambda i, j: (0, i)), pl.BlockSpec((S, tj), lambda i, j: (0, j))] + more_specs,
        out_specs=out_spec, out_shape=out_shape,
        scratch_shapes=[pltpu.VMEM((ti, S), BF16)],
        compiler_params=_params(2),
    )(act, dout, *more)


def _d_ff1(dpre2b, w2_g, r, after=None):
    tn = 512

    def body(d_ref, w_ref, r_ref, o_ref, gb_ref):
        da = _dot_nt(d_ref[...], w_ref[...])
        dp = da * (2.0 * r_ref[...].astype(F32))
        o_ref[...] = dp.astype(BF16)
        gb_ref[...] = jnp.sum(dp, axis=0, keepdims=True)

    body, more_specs, more = _behind(body, 3, after)
    return pl.pallas_call(
        body, name="d_ff1", grid=(DFF // tn,),
        in_specs=[pl.BlockSpec((S, D), lambda j: (0, 0)), pl.BlockSpec((tn, D), lambda j: (j, 0)),
                  pl.BlockSpec((S, tn), lambda j: (0, j))] + more_specs,
        out_specs=[pl.BlockSpec((S, tn), lambda j: (0, j)), pl.BlockSpec((1, tn), lambda j: (0, j))],
        out_shape=[jax.ShapeDtypeStruct((S, DFF), BF16), jax.ShapeDtypeStruct((1, DFF), F32)],
        compiler_params=_params(1),
    )(dpre2b, w2_g, r, *more)


def _d_h1_ln1(dprea, w1_g, dpre2, xhat1, rstd1, g1, after=None):
    tm, tk = 512, 1024
    per = D // tk
    nk = DFF // tk

    def body(a_ref, w_ref, d2_ref, xh_ref, rs_ref, g_ref, d_ref, db_ref, st_ref, acc):
        i, k = pl.program_id(0), pl.program_id(1)

        @pl.when(k == 0)
        def _():
            acc[...] = jnp.zeros_like(acc)

        @pl.when((i == 0) & (k == 0))
        def _():
            st_ref[...] = jnp.zeros_like(st_ref)

        acc[...] += _dot_nt(a_ref[...], w_ref[...])

        @pl.when(k == nk - 1)
        def _():
            def rows_chunk(ci, carry):
                rows = pl.ds(pl.multiple_of(ci * 128, 128), 128)
                dh = ALPHA * d2_ref[rows, :] + acc[rows, :]
                xhat = xh_ref[rows, :]
                g = dh * g_ref[...]
                dpre = rs_ref[rows, 0:1] * (g - jnp.mean(g, axis=1, keepdims=True)
                                            - xhat * jnp.mean(g * xhat, axis=1, keepdims=True))
                d_ref[rows, :] = dpre
                db_ref[rows, :] = dpre.astype(BF16)
                st_ref[0:1, :] += jnp.sum(dh * xhat, axis=0, keepdims=True)
                st_ref[1:2, :] += jnp.sum(dh, axis=0, keepdims=True)
                return carry

            lax.fori_loop(0, tm // 128, rows_chunk, 0)

    row = pl.BlockSpec((tm, D), lambda i, k: (i, 0))
    body, more_specs, more = _behind(body, 6, after)
    return pl.pallas_call(
        body, name="d_h1_ln1", grid=(S // tm, nk),
        in_specs=[pl.BlockSpec((tm, tk), lambda i, k: (i, k)),
                  pl.BlockSpec((None, D, tk), lambda i, k: (k // per, 0, k % per)),
                  row, row, pl.BlockSpec((tm, 128), lambda i, k: (i, 0)), pl.BlockSpec((1, D), lambda i, k: (0, 0))]
        + more_specs,
        out_specs=[row, row, pl.BlockSpec((8, D), lambda i, k: (0, 0))],
        out_shape=[jax.ShapeDtypeStruct((S, D), F32), jax.ShapeDtypeStruct((S, D), BF16),
                   jax.ShapeDtypeStruct((8, D), F32)],
        scratch_shapes=[pltpu.VMEM((tm, D), F32)],
        compiler_params=_params(2),
    )(dprea, w1_g, dpre2, xhat1, rstd1, g1, *more)


def _d_merged(dpre1b, wout_g, proj, ya, yb):
    tm, tn = 512, 1024

    def body(d_ref, w_ref, ga_ref, gb_ref, ya_ref, yb_ref, dya_ref, dyb_ref, dga_ref, dgb_ref):
        dm = _dot_nt(d_ref[...], w_ref[...])
        sa = _sigmoid(ga_ref[...])
        sb = _sigmoid(gb_ref[...])
        dya_ref[...] = (dm * sa).astype(BF16)
        dyb_ref[...] = (dm * sb).astype(BF16)
        dga_ref[...] = (dm * ya_ref[...].astype(F32) * sa * (1.0 - sa)).astype(BF16)
        dgb_ref[...] = (dm * yb_ref[...].astype(F32) * sb * (1.0 - sb)).astype(BF16)

    tile = pl.BlockSpec((tm, tn), lambda i, j: (i, j))
    return pl.pallas_call(
        body, name="d_merged", grid=(S // tm, D // tn),
        in_specs=[pl.BlockSpec((tm, D), lambda i, j: (i, 0)), pl.BlockSpec((tn, D), lambda i, j: (j, 0)),
                  pl.BlockSpec((tm, tn), lambda i, j: (i, 5 + j)), pl.BlockSpec((tm, tn), lambda i, j: (i, 7 + j)),
                  tile, tile],
        out_specs=[tile] * 4,
        out_shape=[jax.ShapeDtypeStruct((S, D), BF16)] * 4,
        compiler_params=_params(2),
    )(dpre1b, wout_g, proj, proj, ya, yb)


def _d_branches(dya, dyb, wpa_g, wpb_g, after=None):
    tk = 512

    def body(da_ref, db_ref, wa_ref, wb_ref, oa_ref, ob_ref):
        @pl.when(pl.program_id(0) == 0)
        def _():
            oa_ref[...] = jnp.zeros_like(oa_ref)
            ob_ref[...] = jnp.zeros_like(ob_ref)

        oa_ref[...] += _dot_nt(da_ref[...], wa_ref[...])
        ob_ref[...] += _dot_nt(db_ref[...], wb_ref[...])

    body, more_specs, more = _behind(body, 4, after)
    return pl.pallas_call(
        body, name="d_branches", grid=(D // tk,),
        in_specs=[pl.BlockSpec((S, tk), lambda k: (0, k)), pl.BlockSpec((S, tk), lambda k: (0, k)),
                  pl.BlockSpec((None, DA, tk), lambda k: (k, 0, 0)), pl.BlockSpec((None, DB, tk), lambda k: (k, 0, 0))]
        + more_specs,
        out_specs=[pl.BlockSpec((S, DA), lambda k: (0, 0)), pl.BlockSpec((S, DB), lambda k: (0, 0))],
        out_shape=[jax.ShapeDtypeStruct((S, DA), F32), jax.ShapeDtypeStruct((S, DB), F32)],
        compiler_params=_params(1),
    )(dya, dyb, wpa_g, wpb_g, *more)


def _gmlp_bwd(proj, dgmlp, ws, ws_t, bsp_b, gain_v, bias_v):
    def body(u_ref, vb_ref, dg_ref, ws_ref, wst_ref, bsp_ref, g_ref, be_ref, duv_ref, gws_ref, gbs_ref, st_ref):
        @pl.when(pl.program_id(0) == 0)
        def _():
            gws_ref[...] = jnp.zeros_like(gws_ref)
            gbs_ref[...] = jnp.zeros_like(gbs_ref)
            st_ref[...] = jnp.zeros_like(st_ref)

        u, tu, u_act, vb, tv, rstd, xhat, vn = _gmlp_parts(u_ref, vb_ref, g_ref, be_ref)
        dg = dg_ref[...]
        dz = dg * u_act
        row = lax.broadcasted_iota(jnp.int32, (128, 128), 0)
        col = lax.broadcasted_iota(jnp.int32, (128, 128), 1)
        causal = row >= col
        causal_t = row <= col
        dvn_parts = []
        z_parts = []
        for g in range(NH):
            cols = slice(g * 128, (g + 1) * 128)
            vng = vn[:, cols].astype(BF16)
            dzg = dz[:, cols]
            dzb = dzg.astype(BF16)
            wsg = jnp.where(causal, ws_ref[g], 0.0).astype(BF16)
            wsg_t = jnp.where(causal_t, wst_ref[g], 0.0).astype(BF16)
            z_parts.append(_dot(wsg, vng) + bsp_ref[g])
            gws_ref[g] += jnp.where(causal, _dot_nt(dzb, vng), 0.0)
            gbs_ref[g] += jnp.broadcast_to(jnp.sum(dzg, axis=1, keepdims=True), (128, 128))
            dvn_parts.append(_dot(wsg_t, dzb))
        z = jnp.concatenate(z_parts, axis=1)
        dvn = jnp.concatenate(dvn_parts, axis=1)
        du = dg * z * _gelu_grad(u, tu)
        st_ref[0:1, :] += jnp.sum(dvn * xhat, axis=0, keepdims=True)
        st_ref[1:2, :] += jnp.sum(dvn, axis=0, keepdims=True)
        gg = dvn * g_ref[...]
        dgv = rstd * (gg - jnp.mean(gg, axis=1, keepdims=True) - xhat * jnp.mean(gg * xhat, axis=1, keepdims=True))
        dvb = dgv * _gelu_grad(vb, tv)
        duv_ref[:, 0:DB] = du.astype(BF16)
        duv_ref[:, DB:2 * DB] = dvb.astype(BF16)

    full3 = pl.BlockSpec((NH, 128, 128), lambda c: (0, 0, 0))
    vec = pl.BlockSpec((1, DB), lambda c: (0, 0))
    return pl.pallas_call(
        body, name="gmlp_bwd", grid=(NBLK,),
        in_specs=[pl.BlockSpec((128, DB), lambda c: (c, 3)), pl.BlockSpec((128, DB), lambda c: (c, 4)),
                  pl.BlockSpec((128, DB), lambda c: (c, 0)), full3, full3, full3, vec, vec],
        out_specs=[pl.BlockSpec((128, 2 * DB), lambda c: (c, 0)), full3, full3, pl.BlockSpec((8, DB), lambda c: (0, 0))],
        out_shape=[jax.ShapeDtypeStruct((S, 2 * DB), BF16), jax.ShapeDtypeStruct((NH, 128, 128), F32),
                   jax.ShapeDtypeStruct((NH, 128, 128), F32), jax.ShapeDtypeStruct((8, DB), F32)],
        compiler_params=_params(1),
    )(proj, proj, dgmlp, ws, ws_t, bsp_b, gain_v, bias_v)


def _attn_delta(dattn, attn, after=None):
    tm = 256

    def body(d_ref, o_ref, dl_ref, db_ref):
        d = d_ref[...]
        prod = d * o_ref[...].astype(F32)
        for h in range(NH):
            cols = slice(h * HD, (h + 1) * HD)
            dl_ref[:, cols] = jnp.broadcast_to(jnp.sum(prod[:, cols], axis=1, keepdims=True), (tm, HD))
        db_ref[...] = d.astype(BF16)

    spec = pl.BlockSpec((tm, DA), lambda i: (i, 0))
    body, more_specs, more = _behind(body, 2, after)
    return pl.pallas_call(
        body, name="attn_delta", grid=(S // tm,),
        in_specs=[spec, spec] + more_specs, out_specs=[spec, spec],
        out_shape=[jax.ShapeDtypeStruct((S, DA), F32), jax.ShapeDtypeStruct((S, DA), BF16)],
        compiler_params=_params(1),
    )(dattn, attn, *more)


def _attn_bwd(qkv, dob, lse, delta, rel_bias, dilation, name):
    nblk = NBLK // dilation
    bucket = jnp.asarray(_bucket_tile(dilation))

    def body(rb_ref, bucket_ref, q_ref, qn_ref, kc_ref, kp_ref, vc_ref, vp_ref, do_ref, don_ref,
             l_ref, ln_ref, dl_ref, dln_ref, dqkv_ref, ds_ref, bias_scr):
        b = pl.program_id(0)

        @pl.when(b == 0)
        def _():
            _bias_tiles(rb_ref, bucket_ref[...], bias_scr)
            ds_ref[...] = jnp.zeros_like(ds_ref)

        has_prev = (b % nblk) != 0
        has_next = ((b + 1) % nblk) != 0
        for h in range(NH):
            cols = slice(h * HD, (h + 1) * HD)
            q = q_ref[:, cols]
            kc = kc_ref[:, cols]
            kp = kp_ref[:, cols]
            vc = vc_ref[:, cols]
            vp = vp_ref[:, cols]
            do = do_ref[:, cols]
            bias_p = bias_scr[h, :, 0:128]
            bias_c = bias_scr[h, :, 128:256]
            lse_b = l_ref[:, cols]
            dl_b = dl_ref[:, cols]
            p_c = jnp.exp(_dot_nt(q, kc) * SCALE + bias_c - lse_b)
            p_p = jnp.where(has_prev, jnp.exp(_dot_nt(q, kp) * SCALE + bias_p - lse_b), 0.0)
            ds_c = p_c * (_dot_nt(do, vc) - dl_b)
            ds_p = p_p * (_dot_nt(do, vp) - dl_b)
            ds_ref[h, :, 0:128] += ds_p
            ds_ref[h, :, 128:256] += ds_c
            ds_cb = ds_c.astype(BF16)
            dqkv_ref[:, cols] = ((_dot(ds_cb, kc) + _dot(ds_p.astype(BF16), kp)) * SCALE).astype(BF16)
            qn = qn_ref[:, cols]
            don = don_ref[:, cols]
            p_n = jnp.where(has_next, jnp.exp(_dot_nt(qn, kc) * SCALE + bias_p - ln_ref[:, cols]), 0.0)
            ds_n = p_n * (_dot_nt(don, vc) - dln_ref[:, cols])
            dk = (_dot(ds_c.T.astype(BF16), q) + _dot(ds_n.T.astype(BF16), qn)) * SCALE
            dv = _dot(p_c.T.astype(BF16), do) + _dot(p_n.T.astype(BF16), don)
            dqkv_ref[:, DA + h * HD:DA + (h + 1) * HD] = dk.astype(BF16)
            dqkv_ref[:, 2 * DA + h * HD:2 * DA + (h + 1) * HD] = dv.astype(BF16)

    cur = lambda col: pl.BlockSpec((128, DA), lambda b: (b, col))
    prev = lambda col: pl.BlockSpec((128, DA), lambda b: (jnp.maximum(b - 1, 0), col))
    nxt = lambda col: pl.BlockSpec((128, DA), lambda b: (jnp.minimum(b + 1, NBLK - 1), col))
    return pl.pallas_call(
        body, name=name, grid=(NBLK,),
        in_specs=[pl.BlockSpec(memory_space=pltpu.SMEM), pl.BlockSpec((128, 256), lambda b: (0, 0)),
                  cur(0), nxt(0), cur(1), prev(1), cur(2), prev(2), cur(0), nxt(0), cur(0), nxt(0), cur(0), nxt(0)],
        out_specs=[pl.BlockSpec((128, 3 * DA), lambda b: (b, 0)), pl.BlockSpec((NH, 128, 256), lambda b: (0, 0, 0))],
        out_shape=[jax.ShapeDtypeStruct((S, 3 * DA), BF16), jax.ShapeDtypeStruct((NH, 128, 256), F32)],
        scratch_shapes=[pltpu.VMEM((NH, 128, 256), F32)],
        compiler_params=_params(1),
    )(rel_bias, bucket, qkv, qkv, qkv, qkv, qkv, qkv, dob, dob, lse, lse, delta, delta)


def _sum3_bf16(a, b, c):
    tm = 256
    n = a.shape[1]

    def body(a_ref, b_ref, c_ref, o_ref):
        o_ref[...] = (a_ref[...].astype(F32) + b_ref[...].astype(F32) + c_ref[...].astype(F32)).astype(BF16)

    spec = pl.BlockSpec((tm, n), lambda i: (i, 0))
    return pl.pallas_call(
        body, name="dqkv_sum", grid=(S // tm,), in_specs=[spec] * 3, out_specs=spec,
        out_shape=jax.ShapeDtypeStruct((S, n), BF16), compiler_params=_params(1),
    )(a, b, c)


def _rel_bias_grad(ds_sums):
    buckets = jnp.asarray(np.stack([_bucket_tile(d) for _, d in PATTERNS]))

    def body(bk_ref, d1, d2, d3, o_ref):
        row = lax.broadcasted_iota(jnp.int32, (N_BUCKETS, 128), 0)
        lane = lax.broadcasted_iota(jnp.int32, (N_BUCKETS, 128), 1)

        def one_bucket(t, out):
            hits = [bk_ref[p] == t for p in range(3)]
            for h in range(NH):
                tot = jnp.zeros((128, 256), F32)
                for p, d in enumerate((d1, d2, d3)):
                    tot = tot + jnp.where(hits[p], d[h], 0.0)
                out = jnp.where((row == t) & (lane == h), jnp.sum(tot), out)
            return out

        o_ref[...] = lax.fori_loop(0, N_BUCKETS, one_bucket, jnp.zeros((N_BUCKETS, 128), F32))

    return pl.pallas_call(
        body, name="rel_bias_grad",
        in_specs=[pl.BlockSpec(memory_space=pltpu.VMEM)] * 4, out_specs=pl.BlockSpec(memory_space=pltpu.VMEM),
        out_shape=jax.ShapeDtypeStruct((N_BUCKETS, 128), F32),
        compiler_params=pltpu.CompilerParams(vmem_limit_bytes=VMEM_LIMIT),
    )(buckets, *ds_sums)


def _d_x(dproj, win_g, dpre1, after=None):
    tm, tk = 512, 2304
    per = 2304 // tk
    nk = DIN // tk

    def body(a_ref, w_ref, d_ref, o_ref, acc):
        k = pl.program_id(1)

        @pl.when(k == 0)
        def _():
            acc[...] = ALPHA * d_ref[...]

        acc[...] += _dot_nt(a_ref[...], w_ref[...])

        @pl.when(k == nk - 1)
        def _():
            o_ref[...] = acc[...]

    row = pl.BlockSpec((tm, D), lambda i, k: (i, 0))
    body, more_specs, more = _behind(body, 3, after)
    return pl.pallas_call(
        body, name="d_x", grid=(S // tm, nk),
        in_specs=[pl.BlockSpec((tm, tk), lambda i, k: (i, k)),
                  pl.BlockSpec((None, D, tk), lambda i, k: (k // per, 0, k % per)), row] + more_specs,
        out_specs=row, out_shape=jax.ShapeDtypeStruct((S, D), F32),
        scratch_shapes=[pltpu.VMEM((tm, D), F32)],
        compiler_params=_params(2),
    )(dproj, win_g, dpre1, *more)


def _adamw(w, g, m, v, name):
    rows, cols = w.shape
    tm = max(t for t in range(8, 257, 8) if rows % t == 0)

    def body(w_ref, g_ref, m_ref, v_ref, d_ref, nm_ref, nv_ref):
        g = g_ref[...]
        m = ADAM_B1 * m_ref[...] + (1.0 - ADAM_B1) * g
        v = ADAM_B2 * v_ref[...] + (1.0 - ADAM_B2) * (g * g)
        m_hat = m / (1.0 - ADAM_B1 ** ADAM_STEP)
        v_hat = v / (1.0 - ADAM_B2 ** ADAM_STEP)
        d_ref[...] = -ADAM_LR * (m_hat / (jnp.sqrt(v_hat) + ADAM_EPS) + ADAM_WD * w_ref[...])
        nm_ref[...] = m
        nv_ref[...] = v

    spec = pl.BlockSpec((tm, cols), lambda i: (i, 0))
    return pl.pallas_call(
        body, name=name, grid=(rows // tm,), in_specs=[spec] * 4, out_specs=[spec] * 3,
        out_shape=[jax.ShapeDtypeStruct((rows, cols), F32)] * 3, compiler_params=_params(1),
    )(w, g, m, v)


def _position():
    x, y, c = lax.axis_index("x"), lax.axis_index("y"), lax.axis_index("c")
    chips = [(1 - x, y), (x, 1 - y), (1 - x, 1 - y)]
    return x, y, c, chips


def _remote(src, dst, send_sems, recv_sems, k, to):
    return pltpu.make_async_remote_copy(src_ref=src, dst_ref=dst, send_sem=send_sems.at[k], recv_sem=recv_sems.at[k],
                                        device_id=to, device_id_type=MESH)


def _place_shard(w, name, after=None):
    rows, cols = w.shape
    tm = 256
    x, y = lax.axis_index("x"), lax.axis_index("y")

    def body(chip_ref, w_ref, o_ref):
        o_ref[...] = w_ref[...].astype(BF16)

    more_specs, more = ([ANY], [after]) if after is not None else ([], [])
    if after is not None:
        inner = body
        body = lambda chip_ref, w_ref, after_ref, o_ref: inner(chip_ref, w_ref, o_ref)
    return pl.pallas_call(
        body, name=name,
        grid_spec=pltpu.PrefetchScalarGridSpec(
            num_scalar_prefetch=1, grid=(rows // tm,),
            in_specs=[pl.BlockSpec((tm, cols), lambda i, chip: (i, 0))] + more_specs,
            out_specs=pl.BlockSpec((None, tm, cols), lambda i, chip: (chip[0], i, 0))),
        out_shape=jax.ShapeDtypeStruct((N_CHIPS, rows, cols), BF16),
        compiler_params=_params(1),
    )(jnp.reshape(2 * x + y, (1,)).astype(jnp.int32), w, *more)


def _to_bf16(x, name, after=None):
    tm = 256

    def body(x_ref, o_ref):
        o_ref[...] = x_ref[...].astype(BF16)

    spec = pl.BlockSpec((tm, x.shape[1]), lambda i: (i, 0))
    body, more_specs, more = _behind(body, 1, after)
    return pl.pallas_call(
        body, name=name, grid=(x.shape[0] // tm,), in_specs=[spec] + more_specs, out_specs=spec,
        out_shape=jax.ShapeDtypeStruct(x.shape, BF16), compiler_params=_params(1),
    )(x, *more)


HBM = pl.BlockSpec(memory_space=pltpu.HBM)
SEM = pl.BlockSpec(memory_space=pltpu.SEMAPHORE)
EFFECT = pltpu.SideEffectType.DATAFLOW_SIDE_EFFECTING


def _comm_call(name, body, bufs, sems_in, sems_out, after=None, token=False):
    nb, ns, no = len(bufs), len(sems_in), len(sems_out)
    n_in = nb + ns + (after is not None)

    def wrapped(*refs):
        body(refs[:nb], refs[nb:nb + ns], refs[n_in + nb:n_in + nb + no])
        if token:
            refs[-1][...] = jnp.zeros((8, 128), F32)

    outs = pl.pallas_call(
        wrapped, name=name,
        in_specs=[HBM] * nb + [SEM] * ns + ([ANY] if after is not None else []),
        out_specs=[HBM] * nb + [SEM] * no + ([pl.BlockSpec(memory_space=pltpu.VMEM)] if token else []),
        out_shape=[pltpu.HBM(b.shape, b.dtype) for b in bufs] + [pltpu.SemaphoreType.DMA((k,)) for k in sems_out]
        + ([jax.ShapeDtypeStruct((8, 128), F32)] if token else []),
        input_output_aliases={i: i for i in range(nb)},
        compiler_params=pltpu.CompilerParams(has_side_effects=EFFECT),
    )(*[pltpu.with_memory_space_constraint(b, pltpu.HBM) for b in bufs], *sems_in, *([after] if after is not None else []))
    return list(outs[:nb]), list(outs[nb:nb + no]), (outs[-1] if token else None)


def _ag_copies(buf, send_sems, recv_sems, k0, stage):
    x, y, c, chips = _position()
    hr = buf.shape[1] // 2
    half = lambda chip, h: buf.at[chip, pl.ds(h * hr, hr), :]
    sends, arrivals = [], []
    for j, (cx, cy) in enumerate(chips):
        if stage == "ici":
            mine = half(2 * x + y, c)
            sends.append(_remote(mine, mine, send_sems, recv_sems, k0 + j, (cx, cy, c)))
            got = half(2 * cx + cy, c)
        else:
            landed = half(2 * cx + cy, c)
            sends.append(_remote(landed, landed, send_sems, recv_sems, k0 + j, (x, y, 1 - c)))
            got = half(2 * cx + cy, 1 - c)
        arrivals.append(_remote(got, got, send_sems, recv_sems, k0 + j, (x, y, c)))
    return sends, arrivals


def _ag_start(name, groups):
    flat = [b for g in groups for b in g]

    def body(bufs, _, sems):
        at = 0
        for gi, g in enumerate(groups):
            for wi in range(len(g)):
                for cp in _ag_copies(bufs[at], sems[2 * gi], sems[2 * gi + 1], 3 * wi, "ici")[0]:
                    cp.start()
                at += 1

    bufs, sems, token = _comm_call(name, body, flat, [], [3 * len(g) for g in groups for _ in (0, 1)], token=True)
    out, at = [], 0
    for gi, g in enumerate(groups):
        out.append((bufs[at:at + len(g)], sems[2 * gi], sems[2 * gi + 1]))
        at += len(g)
    return out, token


def _ag_step(name, finish, advance, after=None):
    fin_bufs = list(finish[0]) if finish else []
    adv_bufs = list(advance[0]) if advance else []
    nf = len(fin_bufs)

    def body(bufs, sems_in, sems_out):
        if advance:
            ici_s, ici_r = sems_in[-2], sems_in[-1]
            for wi in range(len(adv_bufs)):
                buf = bufs[nf + wi]
                ici_sends, ici_arrivals = _ag_copies(buf, ici_s, ici_r, 3 * wi, "ici")
                d2d_sends, _ = _ag_copies(buf, sems_out[0], sems_out[1], 3 * wi, "d2d")
                for arrived, onward in zip(ici_arrivals, d2d_sends):
                    arrived.wait_recv()
                    onward.start()
                for cp in ici_sends:
                    cp.wait_send()
        if finish:
            for wi in range(nf):
                d2d_sends, d2d_arrivals = _ag_copies(bufs[wi], sems_in[0], sems_in[1], 3 * wi, "d2d")
                for cp in d2d_arrivals:
                    cp.wait_recv()
                for cp in d2d_sends:
                    cp.wait_send()

    sems_in = (list(finish[1:]) if finish else []) + (list(advance[1:]) if advance else [])
    bufs, sems, _ = _comm_call(name, body, fin_bufs + adv_bufs, sems_in, [3 * len(adv_bufs)] * 2 if advance else [], after)
    return bufs[:nf], ((bufs[nf:], sems[0], sems[1]) if advance else None)


def _cx_copies(src, dst, send_sems, recv_sems, k0):
    x, y, c, chips = _position()
    sends = [_remote(src.at[2 * cx + cy], dst.at[2 * x + y], send_sems, recv_sems, k0 + j, (cx, cy, c))
             for j, (cx, cy) in enumerate(chips)]
    arrivals = [_remote(dst.at[2 * cx + cy], dst.at[2 * cx + cy], send_sems, recv_sems, k0 + j, (x, y, c))
                for j, (cx, cy) in enumerate(chips)]
    return sends, arrivals


def _cx_start(name, pair_sums):
    n = len(pair_sums)
    landing = [lax.empty(p.shape, p.dtype) for p in pair_sums]

    def body(bufs, _, sems):
        for w in range(n):
            for cp in _cx_copies(bufs[w], bufs[n + w], sems[0], sems[1], 3 * w)[0]:
                cp.start()

    bufs, sems, token = _comm_call(name, body, list(pair_sums) + landing, [], [3 * n, 3 * n], token=True)
    return (bufs, sems), token


def _cx_wait(name, state, after):
    bufs, sems = state
    n = len(bufs) // 2

    def body(refs, sems_in, _):
        for w in range(n):
            sends, arrivals = _cx_copies(refs[w], refs[n + w], sems_in[0], sems_in[1], 3 * w)
            for cp in arrivals:
                cp.wait_recv()
            for cp in sends:
                cp.wait_send()

    bufs, _, _ = _comm_call(name, body, bufs, sems, [], after)
    return bufs[:n], bufs[n:]


def _px_copies(src, dst, send_sems, recv_sems, k):
    x, y, c, _ = _position()
    hr = src.shape[1] // 2
    send = _remote(src.at[:, pl.ds((1 - c) * hr, hr), :], dst, send_sems, recv_sems, k, (x, y, 1 - c))
    arrival = _remote(dst, dst, send_sems, recv_sems, k, (x, y, c))
    return send, arrival


def _px_start(name, grads):
    n = len(grads)
    landing = [lax.empty((N_CHIPS, g.shape[1] // 2, g.shape[2]), F32) for g in grads]

    def body(bufs, _, sems):
        for w in range(n):
            _px_copies(bufs[w], bufs[n + w], sems[0], sems[1], w)[0].start()

    bufs, sems, token = _comm_call(name, body, list(grads) + landing, [], [n, n], token=True)
    return (bufs, sems), token


def _px_wait(name, state, after):
    bufs, sems = state
    n = len(bufs) // 2

    def body(refs, sems_in, _):
        for w in range(n):
            send, arrival = _px_copies(refs[w], refs[n + w], sems_in[0], sems_in[1], w)
            arrival.wait_recv()
            send.wait_send()

    bufs, _, _ = _comm_call(name, body, bufs, sems, [], after)
    return bufs[:n], bufs[n:]


def _pair_sum(grad, got, name):
    _, rows, cols = grad.shape
    hr = rows // 2
    tm = min(hr, 256)
    nb = hr // tm
    c = lax.axis_index("c")

    def body(c_ref, g_ref, o_ref, out_ref):
        out_ref[...] = (g_ref[...] + o_ref[...]).astype(BF16)

    return pl.pallas_call(
        body, name=name,
        grid_spec=pltpu.PrefetchScalarGridSpec(
            num_scalar_prefetch=1, grid=(N_CHIPS, nb),
            in_specs=[pl.BlockSpec((None, tm, cols), lambda s, i, c_ref: (s, c_ref[0] * nb + i, 0)),
                      pl.BlockSpec((None, tm, cols), lambda s, i, c_ref: (s, i, 0))],
            out_specs=pl.BlockSpec((None, tm, cols), lambda s, i, c_ref: (s, i, 0))),
        out_shape=jax.ShapeDtypeStruct((N_CHIPS, hr, cols), BF16),
        compiler_params=_params(2),
    )(jnp.reshape(c, (1,)).astype(jnp.int32), grad, got)


def _chip_sum(parts, pair_sums, name):
    _, hr, cols = parts.shape
    tm = min(hr, 256)
    nb = hr // tm
    x, y, c = lax.axis_index("x"), lax.axis_index("y"), lax.axis_index("c")

    def body(pos_ref, p_ref, own_ref, o_ref):
        chip = pos_ref[0]
        own = own_ref[...].astype(F32)
        term = lambda s: jnp.where(chip == s, own, p_ref[s].astype(F32))
        o_ref[...] = ((term(0) + term(1)) + term(2)) + term(3)

    return pl.pallas_call(
        body, name=name,
        grid_spec=pltpu.PrefetchScalarGridSpec(
            num_scalar_prefetch=1, grid=(nb,),
            in_specs=[pl.BlockSpec((N_CHIPS, tm, cols), lambda i, pos: (0, i, 0)),
                      pl.BlockSpec((None, tm, cols), lambda i, pos: (pos[0], i, 0))],
            out_specs=pl.BlockSpec((tm, cols), lambda i, pos: (pos[1] * nb + i, 0))),
        out_shape=jax.ShapeDtypeStruct((2 * hr, cols), F32), compiler_params=_params(1),
    )(jnp.stack([2 * x + y, c]).astype(jnp.int32), parts, pair_sums)


def _share_halves(bufs, name):
    n = len(bufs)

    def body(*refs):
        outs = refs[n:2 * n]
        send_sems, recv_sems = refs[2 * n:]
        x, y, c, _ = _position()
        copies = []
        for w in range(n):
            hr = outs[w].shape[0] // 2
            mine = outs[w].at[pl.ds(c * hr, hr), :]
            cp = _remote(mine, mine, send_sems, recv_sems, w, (x, y, 1 - c))
            cp.start()
            copies.append(cp)
        for w in range(n):
            hr = outs[w].shape[0] // 2
            theirs = outs[w].at[pl.ds((1 - c) * hr, hr), :]
            _remote(theirs, theirs, send_sems, recv_sems, w, (x, y, c)).wait_recv()
        for cp in copies:
            cp.wait_send()

    return pl.pallas_call(
        body, name=name,
        in_specs=[ANY] * n, out_specs=[ANY] * n,
        out_shape=[jax.ShapeDtypeStruct(b.shape, b.dtype) for b in bufs],
        input_output_aliases={w: w for w in range(n)},
        scratch_shapes=[pltpu.SemaphoreType.DMA((n,)), pltpu.SemaphoreType.DMA((n,))],
    )(*bufs)


def _allreduce_small(g):
    rows = g.shape[0]

    def body(g_ref, o_ref, sib, slots, send_sems, recv_sems):
        x, y, c, chips = _position()
        me = (x, y, c)
        my_chip = 2 * x + y
        pair = _remote(g_ref, sib, send_sems, recv_sems, 0, (x, y, 1 - c))
        pair.start()
        pair.wait()
        slots[my_chip] = g_ref[...] + sib[...]
        sent = []
        for j, (cx, cy) in enumerate(chips):
            cp = _remote(slots.at[my_chip], slots.at[my_chip], send_sems, recv_sems, 1 + j, (cx, cy, c))
            cp.start()
            sent.append(cp)
        for j, (cx, cy) in enumerate(chips):
            got = slots.at[2 * cx + cy]
            _remote(got, got, send_sems, recv_sems, 1 + j, me).wait_recv()
        for cp in sent:
            cp.wait_send()
        o_ref[...] = ((slots[0] + slots[1]) + slots[2]) + slots[3]

    vm = pl.BlockSpec(memory_space=pltpu.VMEM)
    return pl.pallas_call(
        body, name="allreduce_small",
        in_specs=[vm], out_specs=vm, out_shape=jax.ShapeDtypeStruct((rows, 128), F32),
        scratch_shapes=[pltpu.VMEM((rows, 128), F32), pltpu.VMEM((N_CHIPS, rows, 128), F32),
                        pltpu.SemaphoreType.DMA((4,)), pltpu.SemaphoreType.DMA((4,))],
        compiler_params=pltpu.CompilerParams(vmem_limit_bytes=VMEM_LIMIT),
    )(g)


def _permute(a, d):
    return a if d == 1 else a.reshape(S // d, d, a.shape[1]).transpose(1, 0, 2).reshape(S, a.shape[1])


def _unpermute(a, d):
    return a if d == 1 else a.reshape(d, S // d, a.shape[1]).transpose(1, 0, 2).reshape(S, a.shape[1])


_SMALL = ("rel_bias", "ln_v_gain", "ln_v_bias", "w_spatial", "b_spatial", "ln1_gain", "ln1_bias",
          "b_ff1", "b_ff2", "ln2_gain", "ln2_bias")
_SMALL_ROWS = 1200
_LOSS_AT = (152832 // 128, 0)


def _pack_small(parts):
    flat = jnp.concatenate([parts[k].reshape(-1).astype(F32) for k in _SMALL])
    flat = jnp.pad(flat, (0, _SMALL_ROWS * 128 - flat.shape[0]))
    return flat.reshape(_SMALL_ROWS, 128)


def _unpack_small(packed, like):
    flat = packed.reshape(-1)
    out, at = {}, 0
    for k in _SMALL:
        n = math.prod(like[k].shape)
        out[k] = flat[at:at + n].reshape(like[k].shape)
        at += n
    return out


def kernel(x, w_in, rel_bias, ln_v_gain, ln_v_bias, w_spatial, b_spatial, w_proj_a, w_proj_b, w_out, ln1_gain, ln1_bias, w_ff1, b_ff1, w_ff2, b_ff2, ln2_gain, ln2_bias, loss_target, m_w_in, m_rel_bias, m_ln_v_gain, m_ln_v_bias, m_w_spatial, m_b_spatial, m_w_proj_a, m_w_proj_b, m_w_out, m_ln1_gain, m_ln1_bias, m_w_ff1, m_b_ff1, m_w_ff2, m_b_ff2, m_ln2_gain, m_ln2_bias, v_w_in, v_rel_bias, v_ln_v_gain, v_ln_v_bias, v_w_spatial, v_b_spatial, v_w_proj_a, v_w_proj_b, v_w_out, v_ln1_gain, v_ln1_bias, v_w_ff1, v_b_ff1, v_w_ff2, v_b_ff2, v_ln2_gain, v_ln2_bias):
    args = dict(locals())
    big = ("w_in", "w_proj_a", "w_proj_b", "w_out", "w_ff1", "w_ff2")
    weights = ("w_in", "rel_bias", "ln_v_gain", "ln_v_bias", "w_spatial", "b_spatial", "w_proj_a", "w_proj_b", "w_out",
               "ln1_gain", "ln1_bias", "w_ff1", "b_ff1", "w_ff2", "b_ff2", "ln2_gain", "ln2_bias")

    xs = x[0]
    target = loss_target[0]

    (in_a,), tok = _ag_start("allgather_start_w_in", [[_place_shard(w_in[0], "place_w_in")]])
    placed = [_place_shard(args[k][0], f"place_{k}", after=tok) for k in big[1:]]
    xb = _to_bf16(xs, "x_to_bf16", after=tok)
    (in_b, in_c, in_d), tok = _ag_start("allgather_start_rest", [placed[0:3], placed[3:4], placed[4:5]])
    _, d2d_a = _ag_step("allgather_w_in_pass", None, in_a, after=tok)
    (win_g,), _ = _ag_step("allgather_w_in_done", d2d_a, None)

    proj, qkv_b = _proj(xb, win_g)
    _, d2d_b = _ag_step("allgather_b_pass", None, in_b, after=proj)
    qkv_p = [_permute(qkv_b, d) for _, d in PATTERNS]
    outs, lses = [], []
    for p, (_, d) in enumerate(PATTERNS):
        o, l = _attn_fwd(qkv_p[p], rel_bias, d, f"attn_fwd_{p}")
        outs.append(_unpermute(o, d))
        lses.append(_unpermute(l, d))
    attn, lse = _attn_combine(outs, lses)
    ws = w_spatial[0]
    ws_t = jnp.transpose(ws, (0, 2, 1))
    bsp_b = jnp.broadcast_to(b_spatial[0][:, :, None], (NH, 128, 128))
    gmlp = _gmlp_fwd(proj, ws, bsp_b, ln_v_gain, ln_v_bias)
    (wpa_g, wpb_g, wout_g), d2d_c = _ag_step("allgather_b_done_c_pass", d2d_b, in_c, after=gmlp)
    wout_full = wout_g.reshape(D, D)
    ya, yb, merged = _branch(attn, gmlp, wpa_g, wpb_g, proj)
    xhat1, rstd1, h1b = _out_ln1(merged, wout_full, xs, ln1_gain, ln1_bias)
    (w1_g,), d2d_d = _ag_step("allgather_c_done_d_pass", d2d_c, in_d, after=h1b)
    a, r = _ff1(h1b, w1_g, b_ff1)
    (w2_g,), _ = _ag_step("allgather_d_done", d2d_d, None, after=a)
    w2_full = w2_g.reshape(DFF, D)
    dpre2, dpre2b, st2 = _ff2_ln2_loss(a, w2_full, xhat1, ln1_gain, ln1_bias, b_ff2, ln2_gain, ln2_bias, target)

    def pair_and_chip(tag, state, after):
        local, from_sibling = _px_wait(f"pair_exchange_wait_{tag}", state, after)
        pair_sums = [_pair_sum(g, o, f"pair_sum_{tag}_{i}") for i, (g, o) in enumerate(zip(local, from_sibling))]
        return _cx_start(f"chip_exchange_start_{tag}", pair_sums)

    g_w2 = _grad_w(a, dpre2b, "grad_w_ff2", 512, 2048, False)
    px, tok = _px_start("pair_exchange_start_w_ff2", [g_w2.reshape(N_CHIPS, DFF // N_CHIPS, D)])
    dprea, g_b1 = _d_ff1(dpre2b, w2_full, r, after=tok)
    cx_w2, tok = pair_and_chip("w_ff2", px, dprea)
    g_w1 = _grad_w(h1b, dprea, "grad_w_ff1", 512, 2048, True, after=tok)
    px, tok = _px_start("pair_exchange_start_w_ff1", [g_w1])
    dpre1, dpre1b, st1 = _d_h1_ln1(dprea, w1_g, dpre2, xhat1, rstd1, ln1_gain, after=tok)
    cx_w1, tok = pair_and_chip("w_ff1", px, dpre1b)
    g_wout = _grad_w(merged, dpre1b, "grad_w_out", 512, 2048, False, after=tok)
    dya, dyb, dga, dgb = _d_merged(dpre1b, wout_full, proj, ya, yb)
    g_wpa = _grad_w(attn, dya, "grad_w_proj_a", 1024, 512, True)
    g_wpb = _grad_w(gmlp, dyb, "grad_w_proj_b", 1024, 512, True)
    px, tok = _px_start("pair_exchange_start_b", [g_wpa, g_wpb, g_wout.reshape(N_CHIPS, D // N_CHIPS, D)])
    dattn, dgmlp = _d_branches(dya, dyb, wpa_g, wpb_g, after=tok)
    duv, g_ws, g_bs, stv = _gmlp_bwd(proj, dgmlp, ws, ws_t, bsp_b, ln_v_gain, ln_v_bias)
    cx_b, tok = pair_and_chip("b", px, duv)
    delta, dob = _attn_delta(dattn, attn, after=tok)
    dqkv, ds_sums = [], []
    for p, (_, d) in enumerate(PATTERNS):
        dqkv_p, ds = _attn_bwd(qkv_p[p], _permute(dob, d), _permute(lse, d), _permute(delta, d), rel_bias, d,
                               f"attn_bwd_{p}")
        dqkv.append(_unpermute(dqkv_p, d))
        ds_sums.append(ds)
    dqkv_b = _sum3_bf16(*dqkv)
    g_rb = _rel_bias_grad(ds_sums)[:, :NH]

    small_g = dict(rel_bias=g_rb, ln_v_gain=stv[0], ln_v_bias=stv[1], w_spatial=g_ws, b_spatial=g_bs[:, :, 0],
                   ln1_gain=st1[0], ln1_bias=st1[1], b_ff1=g_b1, b_ff2=st2[2], ln2_gain=st2[0], ln2_bias=st2[1])
    gs = _allreduce_small(_pack_small(small_g).at[_LOSS_AT].set(st2[3, 0]))
    ds_, ms_, vs_ = _adamw(_pack_small({k: args[k] for k in _SMALL}), gs,
                           _pack_small({k: args["m_" + k] for k in _SMALL}),
                           _pack_small({k: args["v_" + k] for k in _SMALL}), "adamw_small")
    like = {k: args[k] for k in _SMALL}
    grads, deltas, new_m, new_v = (_unpack_small(t, like) for t in (gs, ds_, ms_, vs_))

    dproj = jnp.concatenate([dqkv_b, duv, dga, dgb], axis=1)
    g_win = _grad_w(xb, dproj, "grad_w_in", 512, 2304, True, after=gs)
    px, tok = _px_start("pair_exchange_start_w_in", [g_win])
    grad_x = _d_x(dproj, win_g, dpre1, after=tok)
    cx_in, tok = pair_and_chip("w_in", px, grad_x)

    def reduce_finish(tag, state, names, after):
        pair_sums, from_chips = _cx_wait(f"chip_exchange_wait_{tag}", state, after)
        halves = [_chip_sum(p, own, f"chip_sum_{k}") for p, own, k in zip(from_chips, pair_sums, names)]
        last = None
        for k, g in zip(names, _share_halves(halves, f"share_halves_{tag}")):
            d_, m_, v_ = _adamw(args[k][0], g, args["m_" + k][0], args["v_" + k][0], f"adamw_{k}")
            grads[k], deltas[k], new_m[k], new_v[k] = g[None], d_[None], m_[None], v_[None]
            last = d_
        return last

    done = reduce_finish("w_ff2", cx_w2, ["w_ff2"], tok)
    done = reduce_finish("w_ff1", cx_w1, ["w_ff1"], done)
    done = reduce_finish("b", cx_b, ["w_proj_a", "w_proj_b", "w_out"], done)
    reduce_finish("w_in", cx_in, ["w_in"], done)

    loss = gs[_LOSS_AT] * (0.5 / D)
    return (loss, grad_x[None], *[grads[k] for k in weights], *[deltas[k] for k in weights],
            *[new_m[k] for k in weights], *[new_v[k] for k in weights])
```

```python
import functools
import math

import numpy as np
import jax
import jax.numpy as jnp
from jax import lax
from jax.experimental import pallas as pl
from jax.experimental.pallas import tpu as pltpu

F32 = jnp.float32
BF16 = jnp.bfloat16

S = 2048
D = 2048
DA = 1024
DB = 1024
DFF = 8192
DIN = 9216
NH = 8
HD = 128
NBLK = 16
PATTERNS = ((128, 1), (512, 4), (2048, 16))
N_BUCKETS = 32
MAX_DISTANCE = 2048
ALPHA = 2.0 ** 0.25
LN_EPS = 1e-5
NEG_INF = -1e30
SCALE = HD ** -0.5
N_CHIPS = 4

ADAM_LR = 0.001
ADAM_B1 = 0.9
ADAM_B2 = 0.999
ADAM_EPS = 1e-08
ADAM_WD = 0.01
ADAM_STEP = 10

VMEM_LIMIT = 56 * 1024 * 1024
MESH = pl.DeviceIdType.MESH
ANY = pl.BlockSpec(memory_space=pl.ANY)


def _params(n_axes, vmem=VMEM_LIMIT):
    return pltpu.CompilerParams(dimension_semantics=("arbitrary",) * n_axes, vmem_limit_bytes=vmem)


def _bucket_tile(dilation):
    qi = np.arange(128)[:, None]
    kj = np.arange(256)[None, :]
    n = np.clip(128 + qi - kj, 0, 128) * dilation
    max_exact = N_BUCKETS // 2
    nf = np.maximum(n, 1).astype(np.float32)
    large = max_exact + (np.log(nf / np.float32(max_exact)) / np.float32(math.log(MAX_DISTANCE / max_exact))
                         * np.float32(N_BUCKETS - max_exact)).astype(np.int32)
    large = np.minimum(large, N_BUCKETS - 1)
    return np.where(n < max_exact, n, large).astype(np.int32)


def _gelu(x):
    c = math.sqrt(2.0 / math.pi)
    t = jnp.tanh(c * (x + 0.044715 * x * x * x))
    return 0.5 * x * (1.0 + t), t


def _gelu_grad(x, t):
    c = math.sqrt(2.0 / math.pi)
    return 0.5 * (1.0 + t) + 0.5 * x * (1.0 - t * t) * c * (1.0 + 3.0 * 0.044715 * x * x)


def _sigmoid(x):
    return 1.0 / (1.0 + jnp.exp(-x))


def _dot(a, b):
    return jnp.dot(a, b, preferred_element_type=F32)


def _behind(body, n_in, after):
    if after is None:
        return body, [], []
    return (lambda *refs: body(*refs[:n_in], *refs[n_in + 1:])), [ANY], [after]


def _dot_nt(a, b):
    return lax.dot_general(a, b, (((1,), (1,)), ((), ())), preferred_element_type=F32)


def _proj(xb, win_g):
    tn = 768
    per = 2304 // tn

    def body(x_ref, w_ref, o_ref):
        o_ref[...] = _dot(x_ref[...], w_ref[...])

    return pl.pallas_call(
        body, name="proj", grid=(DIN // tn,),
        in_specs=[pl.BlockSpec((S, D), lambda j: (0, 0)),
                  pl.BlockSpec((None, D, tn), lambda j: (j // per, 0, j % per))],
        out_specs=pl.BlockSpec((S, tn), lambda j: (0, j)),
        out_shape=jax.ShapeDtypeStruct((S, DIN), F32),
        compiler_params=_params(1),
    )(xb, win_g)


HEADS_PER_STEP = 2


def _head_bias_tiles(rb_ref, bk_ref, bias_scr, first_head):
    qi = lax.broadcasted_iota(jnp.int32, (128, 256), 0)
    kj = lax.broadcasted_iota(jnp.int32, (128, 256), 1)
    steps = 128 + qi - kj
    band = (steps >= 0) & (steps <= 128)
    bias_scr[...] = jnp.zeros_like(bias_scr)
    for p in range(len(PATTERNS)):
        bucket = bk_ref[p]

        def one_bucket(t, carry):
            hit = bucket == t
            for j in range(HEADS_PER_STEP):
                bias_scr[p, j] = jnp.where(hit, rb_ref[t, first_head + j], bias_scr[p, j])
            return carry

        lax.fori_loop(0, N_BUCKETS, one_bucket, 0)
        for j in range(HEADS_PER_STEP):
            bias_scr[p, j] = jnp.where(band, bias_scr[p, j], NEG_INF)


def _block_rows(b, dilation):
    nblk = NBLK // dilation
    r, n = b // nblk, b % nblk
    start = r + n * (128 * dilation)
    prev_start = jnp.maximum(start - 128 * dilation, r)
    if dilation == 1:
        return pl.ds(pl.multiple_of(start, 128), 128), pl.ds(pl.multiple_of(prev_start, 128), 128), n > 0
    return pl.ds(start, 128, stride=dilation), pl.ds(prev_start, 128, stride=dilation), n > 0


def _head_specs(first):
    return [pl.BlockSpec((S, HD), lambda g, j=j: (0, first + g * HEADS_PER_STEP + j)) for j in range(HEADS_PER_STEP)]


def _pair_spec():
    return pl.BlockSpec((S, HEADS_PER_STEP * HD), lambda g: (0, g))


def _attention_fwd(proj, rel_bias):
    hps = HEADS_PER_STEP
    buckets = jnp.asarray(np.stack([_bucket_tile(d) for _, d in PATTERNS]))

    def body(rb_ref, bk_ref, *refs):
        q_refs, k_refs, v_refs = (refs[i * hps:(i + 1) * hps] for i in range(3))
        o_ref, lse_ref, bias_scr = refs[3 * hps:3 * hps + 3]
        acc_scrs, m_scrs, l_scrs = (refs[3 * hps + 3 + i * hps:3 * hps + 3 + (i + 1) * hps] for i in range(3))
        _head_bias_tiles(rb_ref, bk_ref, bias_scr, pl.program_id(0) * hps)
        kj = lax.broadcasted_iota(jnp.int32, (128, 256), 1)
        for p, (_, d) in enumerate(PATTERNS):
            def block(b, carry):
                rows, prows, has_prev = _block_rows(b, d)
                key_ok = (kj >= 128) | has_prev
                for j in range(hps):
                    q_ref, k_ref, v_ref = q_refs[j], k_refs[j], v_refs[j]
                    acc_scr, m_scr, l_scr = acc_scrs[j], m_scrs[j], l_scrs[j]
                    q = q_ref[rows, :].astype(BF16)
                    s = jnp.concatenate([_dot_nt(q, k_ref[prows, :].astype(BF16)),
                                         _dot_nt(q, k_ref[rows, :].astype(BF16))], axis=1) * SCALE
                    s = jnp.where(key_ok, s + bias_scr[p, j], NEG_INF)
                    m = jnp.max(s, axis=1, keepdims=True)
                    e = jnp.exp(s - m)
                    den = jnp.sum(e, axis=1, keepdims=True)
                    eb = e.astype(BF16)
                    o = _dot(eb[:, :128], v_ref[prows, :].astype(BF16)) + _dot(eb[:, 128:], v_ref[rows, :].astype(BF16))
                    if p == 0:
                        acc_scr[rows, :] = o
                        m_scr[rows, :] = jnp.broadcast_to(m, (128, HD))
                        l_scr[rows, :] = jnp.broadcast_to(den, (128, HD))
                    else:
                        m_old = m_scr[rows, :]
                        m_new = jnp.maximum(m_old, m)
                        w_old, w_new = jnp.exp(m_old - m_new), jnp.exp(m - m_new)
                        acc_scr[rows, :] = acc_scr[rows, :] * w_old + o * w_new
                        l_scr[rows, :] = l_scr[rows, :] * w_old + den * w_new
                        m_scr[rows, :] = m_new
                return carry

            lax.fori_loop(0, NBLK, block, 0)
        for j in range(hps):
            cols = slice(j * HD, (j + 1) * HD)
            den = l_scrs[j][...]
            o_ref[:, cols] = (acc_scrs[j][...] / den).astype(BF16)
            lse_ref[:, cols] = m_scrs[j][...] + jnp.log(den)

    return pl.pallas_call(
        body, name="attention_fwd", grid=(NH // hps,),
        in_specs=[pl.BlockSpec(memory_space=pltpu.SMEM), pl.BlockSpec((3, 128, 256), lambda g: (0, 0, 0))]
        + _head_specs(0) + _head_specs(NH) + _head_specs(2 * NH),
        out_specs=[_pair_spec(), _pair_spec()],
        out_shape=[jax.ShapeDtypeStruct((S, DA), BF16), jax.ShapeDtypeStruct((S, DA), F32)],
        scratch_shapes=[pltpu.VMEM((3, hps, 128, 256), F32)] + [pltpu.VMEM((S, HD), F32)] * (3 * hps),
        compiler_params=_params(1),
    )(rel_bias, buckets, *([proj] * (3 * hps)))


def _attention_bwd(proj, dattn, attn, lse, rel_bias, after=None):
    hps = HEADS_PER_STEP

    def body(rb_ref, bk_ref, *refs):
        q_refs, k_refs, v_refs, do_refs, o_refs, lse_refs = (refs[i * hps:(i + 1) * hps] for i in range(6))
        dq_ref, dk_ref, dv_ref, ds_ref, bias_scr = refs[6 * hps:6 * hps + 5]
        dl_scrs, dq_scrs, dk_scrs, dv_scrs = (refs[6 * hps + 5 + i * hps:6 * hps + 5 + (i + 1) * hps] for i in range(4))
        _head_bias_tiles(rb_ref, bk_ref, bias_scr, pl.program_id(0) * hps)
        ds_ref[...] = jnp.zeros_like(ds_ref)
        for j in range(hps):
            dq_scrs[j][...] = jnp.zeros((S, HD), F32)
            dk_scrs[j][...] = jnp.zeros((S, HD), F32)
            dv_scrs[j][...] = jnp.zeros((S, HD), F32)
            prod = do_refs[j][...] * o_refs[j][...].astype(F32)
            dl_scrs[j][...] = jnp.broadcast_to(jnp.sum(prod, axis=1, keepdims=True), (S, HD))
        for p, (_, d) in enumerate(PATTERNS):
            def block(b, carry):
                rows, prows, has_prev = _block_rows(b, d)
                for j in range(hps):
                    q_ref, k_ref, v_ref, do_ref = q_refs[j], k_refs[j], v_refs[j], do_refs[j]
                    dq_scr, dk_scr, dv_scr = dq_scrs[j], dk_scrs[j], dv_scrs[j]
                    q = q_ref[rows, :].astype(BF16)
                    kc, kp = k_ref[rows, :].astype(BF16), k_ref[prows, :].astype(BF16)
                    vc, vp = v_ref[rows, :].astype(BF16), v_ref[prows, :].astype(BF16)
                    do = do_ref[rows, :].astype(BF16)
                    lse_b, dl_b = lse_refs[j][rows, :], dl_scrs[j][rows, :]
                    p_c = jnp.exp(_dot_nt(q, kc) * SCALE + bias_scr[p, j, :, 128:256] - lse_b)
                    p_p = jnp.where(has_prev, jnp.exp(_dot_nt(q, kp) * SCALE + bias_scr[p, j, :, 0:128] - lse_b), 0.0)
                    ds_c = p_c * (_dot_nt(do, vc) - dl_b)
                    ds_p = p_p * (_dot_nt(do, vp) - dl_b)
                    ds_ref[p, j, :, 0:128] += ds_p
                    ds_ref[p, j, :, 128:256] += ds_c
                    dq_scr[rows, :] += (_dot(ds_c.astype(BF16), kc) + _dot(ds_p.astype(BF16), kp)) * SCALE
                    dk_scr[rows, :] += _dot(ds_c.T.astype(BF16), q) * SCALE
                    dk_scr[prows, :] += _dot(ds_p.T.astype(BF16), q) * SCALE
                    dv_scr[rows, :] += _dot(p_c.T.astype(BF16), do)
                    dv_scr[prows, :] += _dot(p_p.T.astype(BF16), do)
                return carry

            lax.fori_loop(0, NBLK, block, 0)
        for j in range(hps):
            cols = slice(j * HD, (j + 1) * HD)
            dq_ref[:, cols] = dq_scrs[j][...].astype(BF16)
            dk_ref[:, cols] = dk_scrs[j][...].astype(BF16)
            dv_ref[:, cols] = dv_scrs[j][...].astype(BF16)

    buckets = jnp.asarray(np.stack([_bucket_tile(d) for _, d in PATTERNS]))
    body, more_specs, more = _behind(body, 2 + 6 * hps, after)
    return pl.pallas_call(
        body, name="attention_bwd", grid=(NH // hps,),
        in_specs=[pl.BlockSpec(memory_space=pltpu.SMEM), pl.BlockSpec((3, 128, 256), lambda g: (0, 0, 0))]
        + _head_specs(0) + _head_specs(NH) + _head_specs(2 * NH) + _head_specs(0) + _head_specs(0) + _head_specs(0)
        + more_specs,
        out_specs=[_pair_spec(), _pair_spec(), _pair_spec(), pl.BlockSpec((3, hps, 128, 256), lambda g: (0, g, 0, 0))],
        out_shape=[jax.ShapeDtypeStruct((S, DA), BF16)] * 3 + [jax.ShapeDtypeStruct((3, NH, 128, 256), F32)],
        scratch_shapes=[pltpu.VMEM((3, hps, 128, 256), F32)] + [pltpu.VMEM((S, HD), F32)] * (4 * hps),
        compiler_params=_params(1),
    )(rel_bias, buckets, *([proj] * (3 * hps)), *([dattn] * hps), *([attn] * hps), *([lse] * hps), *more)


def _gmlp_parts(u_ref, vb_ref, g_ref, be_ref):
    u = u_ref[...]
    u_act, tu = _gelu(u)
    vb = vb_ref[...]
    gv, tv = _gelu(vb)
    mean = jnp.mean(gv, axis=1, keepdims=True)
    cen = gv - mean
    var = jnp.mean(cen * cen, axis=1, keepdims=True)
    rstd = lax.rsqrt(var + LN_EPS)
    xhat = cen * rstd
    vn = xhat * g_ref[...] + be_ref[...]
    return u, tu, u_act, vb, tv, rstd, xhat, vn


def _gmlp_fwd(proj, ws, bsp_b, gain_v, bias_v):
    def body(u_ref, vb_ref, ws_ref, bsp_ref, g_ref, be_ref, o_ref):
        _, _, u_act, _, _, _, _, vn = _gmlp_parts(u_ref, vb_ref, g_ref, be_ref)
        row = lax.broadcasted_iota(jnp.int32, (128, 128), 0)
        col = lax.broadcasted_iota(jnp.int32, (128, 128), 1)
        causal = row >= col
        for g in range(NH):
            cols = slice(g * 128, (g + 1) * 128)
            wsg = jnp.where(causal, ws_ref[g], 0.0).astype(BF16)
            z = _dot(wsg, vn[:, cols].astype(BF16)) + bsp_ref[g]
            o_ref[:, cols] = (u_act[:, cols] * z).astype(BF16)

    return pl.pallas_call(
        body, name="gmlp_fwd", grid=(NBLK,),
        in_specs=[pl.BlockSpec((128, DB), lambda c: (c, 3)), pl.BlockSpec((128, DB), lambda c: (c, 4)),
                  pl.BlockSpec((NH, 128, 128), lambda c: (0, 0, 0)), pl.BlockSpec((NH, 128, 128), lambda c: (0, 0, 0)),
                  pl.BlockSpec((1, DB), lambda c: (0, 0)), pl.BlockSpec((1, DB), lambda c: (0, 0))],
        out_specs=pl.BlockSpec((128, DB), lambda c: (c, 0)),
        out_shape=jax.ShapeDtypeStruct((S, DB), BF16),
        compiler_params=_params(1),
    )(proj, proj, ws, bsp_b, gain_v, bias_v)


def _branch(attn, gmlp, wpa_g, wpb_g, proj):
    tn = 512

    def body(a_ref, g_ref, wa_ref, wb_ref, ga_ref, gb_ref, ya_ref, yb_ref, mg_ref):
        ya = _dot(a_ref[...], wa_ref[...])
        yb = _dot(g_ref[...], wb_ref[...])
        ya_ref[...] = ya.astype(BF16)
        yb_ref[...] = yb.astype(BF16)
        mg_ref[...] = (_sigmoid(ga_ref[...]) * ya + _sigmoid(gb_ref[...]) * yb).astype(BF16)

    out = pl.BlockSpec((S, tn), lambda j: (0, j))
    return pl.pallas_call(
        body, name="branch", grid=(D // tn,),
        in_specs=[pl.BlockSpec((S, DA), lambda j: (0, 0)), pl.BlockSpec((S, DB), lambda j: (0, 0)),
                  pl.BlockSpec((None, DA, tn), lambda j: (j, 0, 0)), pl.BlockSpec((None, DB, tn), lambda j: (j, 0, 0)),
                  pl.BlockSpec((S, tn), lambda j: (0, 5120 // tn + j)), pl.BlockSpec((S, tn), lambda j: (0, 7168 // tn + j))],
        out_specs=[out, out, out],
        out_shape=[jax.ShapeDtypeStruct((S, D), BF16)] * 3,
        compiler_params=_params(1),
    )(attn, gmlp, wpa_g, wpb_g, proj, proj)


def _out_ln1(merged, wout_g, x, gain, bias):
    tm = 256

    def body(m_ref, w_ref, x_ref, g_ref, b_ref, xh_ref, rs_ref, h_ref):
        pre = ALPHA * x_ref[...] + _dot(m_ref[...], w_ref[...])
        mean = jnp.mean(pre, axis=1, keepdims=True)
        cen = pre - mean
        var = jnp.mean(cen * cen, axis=1, keepdims=True)
        rstd = lax.rsqrt(var + LN_EPS)
        xhat = cen * rstd
        xh_ref[...] = xhat
        rs_ref[...] = jnp.broadcast_to(rstd, (tm, 128))
        h_ref[...] = (xhat * g_ref[...] + b_ref[...]).astype(BF16)

    row = pl.BlockSpec((tm, D), lambda i: (i, 0))
    vec = pl.BlockSpec((1, D), lambda i: (0, 0))
    return pl.pallas_call(
        body, name="out_ln1", grid=(S // tm,),
        in_specs=[row, pl.BlockSpec((D, D), lambda i: (0, 0)), row, vec, vec],
        out_specs=[row, pl.BlockSpec((tm, 128), lambda i: (i, 0)), row],
        out_shape=[jax.ShapeDtypeStruct((S, D), F32), jax.ShapeDtypeStruct((S, 128), F32),
                   jax.ShapeDtypeStruct((S, D), BF16)],
        compiler_params=_params(1),
    )(merged, wout_g, x, gain, bias)


def _ff1(h1b, w1_g, b1):
    tn = 512
    per = D // tn

    def body(h_ref, w_ref, b_ref, a_ref, r_ref):
        r = jnp.maximum(_dot(h_ref[...], w_ref[...]) + b_ref[...], 0.0)
        r_ref[...] = r.astype(BF16)
        a_ref[...] = (r * r).astype(BF16)

    out = pl.BlockSpec((S, tn), lambda j: (0, j))
    return pl.pallas_call(
        body, name="ff1", grid=(DFF // tn,),
        in_specs=[pl.BlockSpec((S, D), lambda j: (0, 0)),
                  pl.BlockSpec((None, D, tn), lambda j: (j // per, 0, j % per)),
                  pl.BlockSpec((1, tn), lambda j: (0, j))],
        out_specs=[out, out],
        out_shape=[jax.ShapeDtypeStruct((S, DFF), BF16)] * 2,
        compiler_params=_params(1),
    )(h1b, w1_g, b1)


def _ff2_ln2_loss(a, w2_g, xhat1, g1, b1, b2, g2, be2, target):
    tm, tk = 512, 1024
    nk = DFF // tk

    def body(a_ref, w_ref, xh_ref, g1_ref, b1_ref, b2_ref, g2_ref, be2_ref, t_ref, d_ref, db_ref, st_ref, acc):
        i, k = pl.program_id(0), pl.program_id(1)

        @pl.when(k == 0)
        def _():
            acc[...] = jnp.zeros_like(acc)

        @pl.when((i == 0) & (k == 0))
        def _():
            st_ref[...] = jnp.zeros_like(st_ref)

        acc[...] += _dot(a_ref[...], w_ref[...])

        @pl.when(k == nk - 1)
        def _():
            def rows_chunk(ci, carry):
                rows = pl.ds(pl.multiple_of(ci * 128, 128), 128)
                h1 = xh_ref[rows, :] * g1_ref[...] + b1_ref[...]
                pre = ALPHA * h1 + acc[rows, :] + b2_ref[...]
                mean = jnp.mean(pre, axis=1, keepdims=True)
                cen = pre - mean
                var = jnp.mean(cen * cen, axis=1, keepdims=True)
                rstd = lax.rsqrt(var + LN_EPS)
                xhat = cen * rstd
                y = xhat * g2_ref[...] + be2_ref[...]
                err = y - t_ref[rows, :]
                dy = err * (1.0 / D)
                g = dy * g2_ref[...]
                dpre = rstd * (g - jnp.mean(g, axis=1, keepdims=True)
                               - xhat * jnp.mean(g * xhat, axis=1, keepdims=True))
                d_ref[rows, :] = dpre
                db_ref[rows, :] = dpre.astype(BF16)
                st_ref[0:1, :] += jnp.sum(dy * xhat, axis=0, keepdims=True)
                st_ref[1:2, :] += jnp.sum(dy, axis=0, keepdims=True)
                st_ref[2:3, :] += jnp.sum(dpre, axis=0, keepdims=True)
                st_ref[3:4, :] += jnp.broadcast_to(jnp.sum(err * err).reshape(1, 1), (1, D))
                return carry

            lax.fori_loop(0, tm // 128, rows_chunk, 0)

    row = pl.BlockSpec((tm, D), lambda i, k: (i, 0))
    vec = pl.BlockSpec((1, D), lambda i, k: (0, 0))
    return pl.pallas_call(
        body, name="ff2_ln2_loss", grid=(S // tm, nk),
        in_specs=[pl.BlockSpec((tm, tk), lambda i, k: (i, k)), pl.BlockSpec((tk, D), lambda i, k: (k, 0)),
                  row, vec, vec, vec, vec, vec, row],
        out_specs=[row, row, pl.BlockSpec((8, D), lambda i, k: (0, 0))],
        out_shape=[jax.ShapeDtypeStruct((S, D), F32), jax.ShapeDtypeStruct((S, D), BF16),
                   jax.ShapeDtypeStruct((8, D), F32)],
        scratch_shapes=[pltpu.VMEM((tm, D), F32)],
        compiler_params=_params(2),
    )(a, w2_g, xhat1, g1, b1, b2, g2, be2, target)


def _grad_w(act, dout, name, ti, tj, sharded, after=None):
    m, n = act.shape[1], dout.shape[1]
    ns = n // N_CHIPS
    per = ns // tj if sharded else None

    def body(a_ref, b_ref, o_ref, at_scr):
        @pl.when(pl.program_id(1) == 0)
        def _():
            at_scr[...] = a_ref[...].T

        o_ref[...] = _dot(at_scr[...], b_ref[...])

    if sharded:
        out_spec = pl.BlockSpec((None, ti, tj), lambda i, j: (j // per, i, j % per))
        out_shape = jax.ShapeDtypeStruct((N_CHIPS, m, ns), F32)
    else:
        out_spec = pl.BlockSpec((ti, tj), lambda i, j: (i, j))
        out_shape = jax.ShapeDtypeStruct((m, n), F32)
    body, more_specs, more = _behind(body, 2, after)
    return pl.pallas_call(
        body, name=name, grid=(m // ti, n // tj),
        in_specs=[pl.BlockSpec((S, ti), lambda i, j: (0, i)), pl.BlockSpec((S, tj), lambda i, j: (0, j))] + more_specs,
        out_specs=out_spec, out_shape=out_shape,
        scratch_shapes=[pltpu.VMEM((ti, S), BF16)],
        compiler_params=_params(2),
    )(act, dout, *more)


def _d_ff1(dpre2b, w2_g, r, after=None):
    tn = 512

    def body(d_ref, w_ref, r_ref, o_ref, gb_ref):
        da = _dot_nt(d_ref[...], w_ref[...])
        dp = da * (2.0 * r_ref[...].astype(F32))
        o_ref[...] = dp.astype(BF16)
        gb_ref[...] = jnp.sum(dp, axis=0, keepdims=True)

    body, more_specs, more = _behind(body, 3, after)
    return pl.pallas_call(
        body, name="d_ff1", grid=(DFF // tn,),
        in_specs=[pl.BlockSpec((S, D), lambda j: (0, 0)), pl.BlockSpec((tn, D), lambda j: (j, 0)),
                  pl.BlockSpec((S, tn), lambda j: (0, j))] + more_specs,
        out_specs=[pl.BlockSpec((S, tn), lambda j: (0, j)), pl.BlockSpec((1, tn), lambda j: (0, j))],
        out_shape=[jax.ShapeDtypeStruct((S, DFF), BF16), jax.ShapeDtypeStruct((1, DFF), F32)],
        compiler_params=_params(1),
    )(dpre2b, w2_g, r, *more)


def _d_h1_ln1(dprea, w1_g, dpre2, xhat1, rstd1, g1, after=None):
    tm, tk = 512, 1024
    per = D // tk
    nk = DFF // tk

    def body(a_ref, w_ref, d2_ref, xh_ref, rs_ref, g_ref, d_ref, db_ref, st_ref, acc):
        i, k = pl.program_id(0), pl.program_id(1)

        @pl.when(k == 0)
        def _():
            acc[...] = jnp.zeros_like(acc)

        @pl.when((i == 0) & (k == 0))
        def _():
            st_ref[...] = jnp.zeros_like(st_ref)

        acc[...] += _dot_nt(a_ref[...], w_ref[...])

        @pl.when(k == nk - 1)
        def _():
            def rows_chunk(ci, carry):
                rows = pl.ds(pl.multiple_of(ci * 128, 128), 128)
                dh = ALPHA * d2_ref[rows, :] + acc[rows, :]
                xhat = xh_ref[rows, :]
                g = dh * g_ref[...]
                dpre = rs_ref[rows, 0:1] * (g - jnp.mean(g, axis=1, keepdims=True)
                                            - xhat * jnp.mean(g * xhat, axis=1, keepdims=True))
                d_ref[rows, :] = dpre
                db_ref[rows, :] = dpre.astype(BF16)
                st_ref[0:1, :] += jnp.sum(dh * xhat, axis=0, keepdims=True)
                st_ref[1:2, :] += jnp.sum(dh, axis=0, keepdims=True)
                return carry

            lax.fori_loop(0, tm // 128, rows_chunk, 0)

    row = pl.BlockSpec((tm, D), lambda i, k: (i, 0))
    body, more_specs, more = _behind(body, 6, after)
    return pl.pallas_call(
        body, name="d_h1_ln1", grid=(S // tm, nk),
        in_specs=[pl.BlockSpec((tm, tk), lambda i, k: (i, k)),
                  pl.BlockSpec((None, D, tk), lambda i, k: (k // per, 0, k % per)),
                  row, row, pl.BlockSpec((tm, 128), lambda i, k: (i, 0)), pl.BlockSpec((1, D), lambda i, k: (0, 0))]
        + more_specs,
        out_specs=[row, row, pl.BlockSpec((8, D), lambda i, k: (0, 0))],
        out_shape=[jax.ShapeDtypeStruct((S, D), F32), jax.ShapeDtypeStruct((S, D), BF16),
                   jax.ShapeDtypeStruct((8, D), F32)],
        scratch_shapes=[pltpu.VMEM((tm, D), F32)],
        compiler_params=_params(2),
    )(dprea, w1_g, dpre2, xhat1, rstd1, g1, *more)


def _d_merged(dpre1b, wout_g, proj, ya, yb):
    tm, tn = 512, 1024

    def body(d_ref, w_ref, ga_ref, gb_ref, ya_ref, yb_ref, dya_ref, dyb_ref, dga_ref, dgb_ref):
        dm = _dot_nt(d_ref[...], w_ref[...])
        sa = _sigmoid(ga_ref[...])
        sb = _sigmoid(gb_ref[...])
        dya_ref[...] = (dm * sa).astype(BF16)
        dyb_ref[...] = (dm * sb).astype(BF16)
        dga_ref[...] = (dm * ya_ref[...].astype(F32) * sa * (1.0 - sa)).astype(BF16)
        dgb_ref[...] = (dm * yb_ref[...].astype(F32) * sb * (1.0 - sb)).astype(BF16)

    tile = pl.BlockSpec((tm, tn), lambda i, j: (i, j))
    return pl.pallas_call(
        body, name="d_merged", grid=(S // tm, D // tn),
        in_specs=[pl.BlockSpec((tm, D), lambda i, j: (i, 0)), pl.BlockSpec((tn, D), lambda i, j: (j, 0)),
                  pl.BlockSpec((tm, tn), lambda i, j: (i, 5 + j)), pl.BlockSpec((tm, tn), lambda i, j: (i, 7 + j)),
                  tile, tile],
        out_specs=[tile] * 4,
        out_shape=[jax.ShapeDtypeStruct((S, D), BF16)] * 4,
        compiler_params=_params(2),
    )(dpre1b, wout_g, proj, proj, ya, yb)


def _d_branches(dya, dyb, wpa_g, wpb_g, after=None):
    tk = 512

    def body(da_ref, db_ref, wa_ref, wb_ref, oa_ref, ob_ref):
        @pl.when(pl.program_id(0) == 0)
        def _():
            oa_ref[...] = jnp.zeros_like(oa_ref)
            ob_ref[...] = jnp.zeros_like(ob_ref)

        oa_ref[...] += _dot_nt(da_ref[...], wa_ref[...])
        ob_ref[...] += _dot_nt(db_ref[...], wb_ref[...])

    body, more_specs, more = _behind(body, 4, after)
    return pl.pallas_call(
        body, name="d_branches", grid=(D // tk,),
        in_specs=[pl.BlockSpec((S, tk), lambda k: (0, k)), pl.BlockSpec((S, tk), lambda k: (0, k)),
                  pl.BlockSpec((None, DA, tk), lambda k: (k, 0, 0)), pl.BlockSpec((None, DB, tk), lambda k: (k, 0, 0))]
        + more_specs,
        out_specs=[pl.BlockSpec((S, DA), lambda k: (0, 0)), pl.BlockSpec((S, DB), lambda k: (0, 0))],
        out_shape=[jax.ShapeDtypeStruct((S, DA), F32), jax.ShapeDtypeStruct((S, DB), F32)],
        compiler_params=_params(1),
    )(dya, dyb, wpa_g, wpb_g, *more)


def _gmlp_bwd(proj, dgmlp, ws, ws_t, bsp_b, gain_v, bias_v):
    def body(u_ref, vb_ref, dg_ref, ws_ref, wst_ref, bsp_ref, g_ref, be_ref, duv_ref, gws_ref, gbs_ref, st_ref):
        @pl.when(pl.program_id(0) == 0)
        def _():
            gws_ref[...] = jnp.zeros_like(gws_ref)
            gbs_ref[...] = jnp.zeros_like(gbs_ref)
            st_ref[...] = jnp.zeros_like(st_ref)

        u, tu, u_act, vb, tv, rstd, xhat, vn = _gmlp_parts(u_ref, vb_ref, g_ref, be_ref)
        dg = dg_ref[...]
        dz = dg * u_act
        row = lax.broadcasted_iota(jnp.int32, (128, 128), 0)
        col = lax.broadcasted_iota(jnp.int32, (128, 128), 1)
        causal = row >= col
        causal_t = row <= col
        dvn_parts = []
        z_parts = []
        for g in range(NH):
            cols = slice(g * 128, (g + 1) * 128)
            vng = vn[:, cols].astype(BF16)
            dzg = dz[:, cols]
            dzb = dzg.astype(BF16)
            wsg = jnp.where(causal, ws_ref[g], 0.0).astype(BF16)
            wsg_t = jnp.where(causal_t, wst_ref[g], 0.0).astype(BF16)
            z_parts.append(_dot(wsg, vng) + bsp_ref[g])
            gws_ref[g] += jnp.where(causal, _dot_nt(dzb, vng), 0.0)
            gbs_ref[g] += jnp.broadcast_to(jnp.sum(dzg, axis=1, keepdims=True), (128, 128))
            dvn_parts.append(_dot(wsg_t, dzb))
        z = jnp.concatenate(z_parts, axis=1)
        dvn = jnp.concatenate(dvn_parts, axis=1)
        du = dg * z * _gelu_grad(u, tu)
        st_ref[0:1, :] += jnp.sum(dvn * xhat, axis=0, keepdims=True)
        st_ref[1:2, :] += jnp.sum(dvn, axis=0, keepdims=True)
        gg = dvn * g_ref[...]
        dgv = rstd * (gg - jnp.mean(gg, axis=1, keepdims=True) - xhat * jnp.mean(gg * xhat, axis=1, keepdims=True))
        dvb = dgv * _gelu_grad(vb, tv)
        duv_ref[:, 0:DB] = du.astype(BF16)
        duv_ref[:, DB:2 * DB] = dvb.astype(BF16)

    full3 = pl.BlockSpec((NH, 128, 128), lambda c: (0, 0, 0))
    vec = pl.BlockSpec((1, DB), lambda c: (0, 0))
    return pl.pallas_call(
        body, name="gmlp_bwd", grid=(NBLK,),
        in_specs=[pl.BlockSpec((128, DB), lambda c: (c, 3)), pl.BlockSpec((128, DB), lambda c: (c, 4)),
                  pl.BlockSpec((128, DB), lambda c: (c, 0)), full3, full3, full3, vec, vec],
        out_specs=[pl.BlockSpec((128, 2 * DB), lambda c: (c, 0)), full3, full3, pl.BlockSpec((8, DB), lambda c: (0, 0))],
        out_shape=[jax.ShapeDtypeStruct((S, 2 * DB), BF16), jax.ShapeDtypeStruct((NH, 128, 128), F32),
                   jax.ShapeDtypeStruct((NH, 128, 128), F32), jax.ShapeDtypeStruct((8, DB), F32)],
        compiler_params=_params(1),
    )(proj, proj, dgmlp, ws, ws_t, bsp_b, gain_v, bias_v)


def _rel_bias_grad(ds_sums):
    buckets = jnp.asarray(np.stack([_bucket_tile(d) for _, d in PATTERNS]))

    def body(bk_ref, ds_ref, o_ref):
        row = lax.broadcasted_iota(jnp.int32, (N_BUCKETS, 128), 0)
        lane = lax.broadcasted_iota(jnp.int32, (N_BUCKETS, 128), 1)

        def one_bucket(t, out):
            hits = [bk_ref[p] == t for p in range(3)]
            for h in range(NH):
                tot = jnp.zeros((128, 256), F32)
                for p in range(3):
                    tot = tot + jnp.where(hits[p], ds_ref[p, h], 0.0)
                out = jnp.where((row == t) & (lane == h), jnp.sum(tot), out)
            return out

        o_ref[...] = lax.fori_loop(0, N_BUCKETS, one_bucket, jnp.zeros((N_BUCKETS, 128), F32))

    return pl.pallas_call(
        body, name="rel_bias_grad",
        in_specs=[pl.BlockSpec(memory_space=pltpu.VMEM)] * 2, out_specs=pl.BlockSpec(memory_space=pltpu.VMEM),
        out_shape=jax.ShapeDtypeStruct((N_BUCKETS, 128), F32),
        compiler_params=pltpu.CompilerParams(vmem_limit_bytes=VMEM_LIMIT),
    )(buckets, ds_sums)


def _d_x(dproj, win_g, dpre1, after=None):
    tm, tk = 512, 2304
    per = 2304 // tk
    nk = DIN // tk

    def body(a_ref, w_ref, d_ref, o_ref, acc):
        k = pl.program_id(1)

        @pl.when(k == 0)
        def _():
            acc[...] = ALPHA * d_ref[...]

        acc[...] += _dot_nt(a_ref[...], w_ref[...])

        @pl.when(k == nk - 1)
        def _():
            o_ref[...] = acc[...]

    row = pl.BlockSpec((tm, D), lambda i, k: (i, 0))
    body, more_specs, more = _behind(body, 3, after)
    return pl.pallas_call(
        body, name="d_x", grid=(S // tm, nk),
        in_specs=[pl.BlockSpec((tm, tk), lambda i, k: (i, k)),
                  pl.BlockSpec((None, D, tk), lambda i, k: (k // per, 0, k % per)), row] + more_specs,
        out_specs=row, out_shape=jax.ShapeDtypeStruct((S, D), F32),
        scratch_shapes=[pltpu.VMEM((tm, D), F32)],
        compiler_params=_params(2),
    )(dproj, win_g, dpre1, *more)


def _adamw(w, g, m, v, name):
    rows, cols = w.shape
    tm = max(t for t in range(8, 257, 8) if rows % t == 0)

    def body(w_ref, g_ref, m_ref, v_ref, d_ref, nm_ref, nv_ref):
        g = g_ref[...]
        m = ADAM_B1 * m_ref[...] + (1.0 - ADAM_B1) * g
        v = ADAM_B2 * v_ref[...] + (1.0 - ADAM_B2) * (g * g)
        m_hat = m / (1.0 - ADAM_B1 ** ADAM_STEP)
        v_hat = v / (1.0 - ADAM_B2 ** ADAM_STEP)
        d_ref[...] = -ADAM_LR * (m_hat / (jnp.sqrt(v_hat) + ADAM_EPS) + ADAM_WD * w_ref[...])
        nm_ref[...] = m
        nv_ref[...] = v

    spec = pl.BlockSpec((tm, cols), lambda i: (i, 0))
    return pl.pallas_call(
        body, name=name, grid=(rows // tm,), in_specs=[spec] * 4, out_specs=[spec] * 3,
        out_shape=[jax.ShapeDtypeStruct((rows, cols), F32)] * 3, compiler_params=_params(1),
    )(w, g, m, v)


def _position():
    x, y, c = lax.axis_index("x"), lax.axis_index("y"), lax.axis_index("c")
    chips = [(1 - x, y), (x, 1 - y), (1 - x, 1 - y)]
    return x, y, c, chips


def _remote(src, dst, send_sems, recv_sems, k, to):
    return pltpu.make_async_remote_copy(src_ref=src, dst_ref=dst, send_sem=send_sems.at[k], recv_sem=recv_sems.at[k],
                                        device_id=to, device_id_type=MESH)


def _place_shard(w, name, after=None):
    rows, cols = w.shape
    tm = 256
    x, y = lax.axis_index("x"), lax.axis_index("y")

    def body(chip_ref, w_ref, o_ref):
        o_ref[...] = w_ref[...].astype(BF16)

    more_specs, more = ([ANY], [after]) if after is not None else ([], [])
    if after is not None:
        inner = body
        body = lambda chip_ref, w_ref, after_ref, o_ref: inner(chip_ref, w_ref, o_ref)
    return pl.pallas_call(
        body, name=name,
        grid_spec=pltpu.PrefetchScalarGridSpec(
            num_scalar_prefetch=1, grid=(rows // tm,),
            in_specs=[pl.BlockSpec((tm, cols), lambda i, chip: (i, 0))] + more_specs,
            out_specs=pl.BlockSpec((None, tm, cols), lambda i, chip: (chip[0], i, 0))),
        out_shape=jax.ShapeDtypeStruct((N_CHIPS, rows, cols), BF16),
        compiler_params=_params(1),
    )(jnp.reshape(2 * x + y, (1,)).astype(jnp.int32), w, *more)


def _to_bf16(x, name, after=None):
    tm = 256

    def body(x_ref, o_ref):
        o_ref[...] = x_ref[...].astype(BF16)

    spec = pl.BlockSpec((tm, x.shape[1]), lambda i: (i, 0))
    body, more_specs, more = _behind(body, 1, after)
    return pl.pallas_call(
        body, name=name, grid=(x.shape[0] // tm,), in_specs=[spec] + more_specs, out_specs=spec,
        out_shape=jax.ShapeDtypeStruct(x.shape, BF16), compiler_params=_params(1),
    )(x, *more)


HBM = pl.BlockSpec(memory_space=pltpu.HBM)
SEM = pl.BlockSpec(memory_space=pltpu.SEMAPHORE)
EFFECT = pltpu.SideEffectType.DATAFLOW_SIDE_EFFECTING


def _comm_call(name, body, bufs, sems_in, sems_out, after=None, token=False):
    nb, ns, no = len(bufs), len(sems_in), len(sems_out)
    n_in = nb + ns + (after is not None)

    def wrapped(*refs):
        body(refs[:nb], refs[nb:nb + ns], refs[n_in + nb:n_in + nb + no])
        if token:
            refs[-1][...] = jnp.zeros((8, 128), F32)

    outs = pl.pallas_call(
        wrapped, name=name,
        in_specs=[HBM] * nb + [SEM] * ns + ([ANY] if after is not None else []),
        out_specs=[HBM] * nb + [SEM] * no + ([pl.BlockSpec(memory_space=pltpu.VMEM)] if token else []),
        out_shape=[pltpu.HBM(b.shape, b.dtype) for b in bufs] + [pltpu.SemaphoreType.DMA((k,)) for k in sems_out]
        + ([jax.ShapeDtypeStruct((8, 128), F32)] if token else []),
        input_output_aliases={i: i for i in range(nb)},
        compiler_params=pltpu.CompilerParams(has_side_effects=EFFECT),
    )(*[pltpu.with_memory_space_constraint(b, pltpu.HBM) for b in bufs], *sems_in, *([after] if after is not None else []))
    return list(outs[:nb]), list(outs[nb:nb + no]), (outs[-1] if token else None)


def _ag_copies(buf, send_sems, recv_sems, k0, stage):
    x, y, c, chips = _position()
    hr = buf.shape[1] // 2
    half = lambda chip, h: buf.at[chip, pl.ds(h * hr, hr), :]
    sends, arrivals = [], []
    for j, (cx, cy) in enumerate(chips):
        if stage == "ici":
            mine = half(2 * x + y, c)
            sends.append(_remote(mine, mine, send_sems, recv_sems, k0 + j, (cx, cy, c)))
            got = half(2 * cx + cy, c)
        else:
            landed = half(2 * cx + cy, c)
            sends.append(_remote(landed, landed, send_sems, recv_sems, k0 + j, (x, y, 1 - c)))
            got = half(2 * cx + cy, 1 - c)
        arrivals.append(_remote(got, got, send_sems, recv_sems, k0 + j, (x, y, c)))
    return sends, arrivals


def _ag_start(name, groups):
    flat = [b for g in groups for b in g]

    def body(bufs, _, sems):
        at = 0
        for gi, g in enumerate(groups):
            for wi in range(len(g)):
                for cp in _ag_copies(bufs[at], sems[2 * gi], sems[2 * gi + 1], 3 * wi, "ici")[0]:
                    cp.start()
                at += 1

    bufs, sems, token = _comm_call(name, body, flat, [], [3 * len(g) for g in groups for _ in (0, 1)], token=True)
    out, at = [], 0
    for gi, g in enumerate(groups):
        out.append((bufs[at:at + len(g)], sems[2 * gi], sems[2 * gi + 1]))
        at += len(g)
    return out, token


def _ag_step(name, finish, advance, after=None):
    fin_bufs = list(finish[0]) if finish else []
    adv_bufs = list(advance[0]) if advance else []
    nf = len(fin_bufs)

    def body(bufs, sems_in, sems_out):
        if advance:
            ici_s, ici_r = sems_in[-2], sems_in[-1]
            for wi in range(len(adv_bufs)):
                buf = bufs[nf + wi]
                ici_sends, ici_arrivals = _ag_copies(buf, ici_s, ici_r, 3 * wi, "ici")
                d2d_sends, _ = _ag_copies(buf, sems_out[0], sems_out[1], 3 * wi, "d2d")
                for arrived, onward in zip(ici_arrivals, d2d_sends):
                    arrived.wait_recv()
                    onward.start()
                for cp in ici_sends:
                    cp.wait_send()
        if finish:
            for wi in range(nf):
                d2d_sends, d2d_arrivals = _ag_copies(bufs[wi], sems_in[0], sems_in[1], 3 * wi, "d2d")
                for cp in d2d_arrivals:
                    cp.wait_recv()
                for cp in d2d_sends:
                    cp.wait_send()

    sems_in = (list(finish[1:]) if finish else []) + (list(advance[1:]) if advance else [])
    bufs, sems, _ = _comm_call(name, body, fin_bufs + adv_bufs, sems_in, [3 * len(adv_bufs)] * 2 if advance else [], after)
    return bufs[:nf], ((bufs[nf:], sems[0], sems[1]) if advance else None)


def _cx_copies(src, dst, send_sems, recv_sems, k0):
    x, y, c, chips = _position()
    sends = [_remote(src.at[2 * cx + cy], dst.at[2 * x + y], send_sems, recv_sems, k0 + j, (cx, cy, c))
             for j, (cx, cy) in enumerate(chips)]
    arrivals = [_remote(dst.at[2 * cx + cy], dst.at[2 * cx + cy], send_sems, recv_sems, k0 + j, (x, y, c))
                for j, (cx, cy) in enumerate(chips)]
    return sends, arrivals


def _cx_start(name, pair_sums):
    n = len(pair_sums)
    landing = [lax.empty(p.shape, p.dtype) for p in pair_sums]

    def body(bufs, _, sems):
        for w in range(n):
            for cp in _cx_copies(bufs[w], bufs[n + w], sems[0], sems[1], 3 * w)[0]:
                cp.start()

    bufs, sems, token = _comm_call(name, body, list(pair_sums) + landing, [], [3 * n, 3 * n], token=True)
    return (bufs, sems), token


def _cx_wait(name, state, after):
    bufs, sems = state
    n = len(bufs) // 2

    def body(refs, sems_in, _):
        for w in range(n):
            sends, arrivals = _cx_copies(refs[w], refs[n + w], sems_in[0], sems_in[1], 3 * w)
            for cp in arrivals:
                cp.wait_recv()
            for cp in sends:
                cp.wait_send()

    bufs, _, _ = _comm_call(name, body, bufs, sems, [], after)
    return bufs[:n], bufs[n:]


def _px_copies(src, dst, send_sems, recv_sems, k):
    x, y, c, _ = _position()
    hr = src.shape[1] // 2
    send = _remote(src.at[:, pl.ds((1 - c) * hr, hr), :], dst, send_sems, recv_sems, k, (x, y, 1 - c))
    arrival = _remote(dst, dst, send_sems, recv_sems, k, (x, y, c))
    return send, arrival


def _px_start(name, grads):
    n = len(grads)
    landing = [lax.empty((N_CHIPS, g.shape[1] // 2, g.shape[2]), F32) for g in grads]

    def body(bufs, _, sems):
        for w in range(n):
            _px_copies(bufs[w], bufs[n + w], sems[0], sems[1], w)[0].start()

    bufs, sems, token = _comm_call(name, body, list(grads) + landing, [], [n, n], token=True)
    return (bufs, sems), token


def _px_wait(name, state, after):
    bufs, sems = state
    n = len(bufs) // 2

    def body(refs, sems_in, _):
        for w in range(n):
            send, arrival = _px_copies(refs[w], refs[n + w], sems_in[0], sems_in[1], w)
            arrival.wait_recv()
            send.wait_send()

    bufs, _, _ = _comm_call(name, body, bufs, sems, [], after)
    return bufs[:n], bufs[n:]


def _pair_sum(grad, got, name):
    _, rows, cols = grad.shape
    hr = rows // 2
    tm = min(hr, 256)
    nb = hr // tm
    c = lax.axis_index("c")

    def body(c_ref, g_ref, o_ref, out_ref):
        out_ref[...] = (g_ref[...] + o_ref[...]).astype(BF16)

    return pl.pallas_call(
        body, name=name,
        grid_spec=pltpu.PrefetchScalarGridSpec(
            num_scalar_prefetch=1, grid=(N_CHIPS, nb),
            in_specs=[pl.BlockSpec((None, tm, cols), lambda s, i, c_ref: (s, c_ref[0] * nb + i, 0)),
                      pl.BlockSpec((None, tm, cols), lambda s, i, c_ref: (s, i, 0))],
            out_specs=pl.BlockSpec((None, tm, cols), lambda s, i, c_ref: (s, i, 0))),
        out_shape=jax.ShapeDtypeStruct((N_CHIPS, hr, cols), BF16),
        compiler_params=_params(2),
    )(jnp.reshape(c, (1,)).astype(jnp.int32), grad, got)


def _chip_sum(parts, pair_sums, name):
    _, hr, cols = parts.shape
    tm = min(hr, 256)
    nb = hr // tm
    x, y, c = lax.axis_index("x"), lax.axis_index("y"), lax.axis_index("c")

    def body(pos_ref, p_ref, own_ref, o_ref):
        chip = pos_ref[0]
        own = own_ref[...].astype(F32)
        term = lambda s: jnp.where(chip == s, own, p_ref[s].astype(F32))
        o_ref[...] = ((term(0) + term(1)) + term(2)) + term(3)

    return pl.pallas_call(
        body, name=name,
        grid_spec=pltpu.PrefetchScalarGridSpec(
            num_scalar_prefetch=1, grid=(nb,),
            in_specs=[pl.BlockSpec((N_CHIPS, tm, cols), lambda i, pos: (0, i, 0)),
                      pl.BlockSpec((None, tm, cols), lambda i, pos: (pos[0], i, 0))],
            out_specs=pl.BlockSpec((tm, cols), lambda i, pos: (pos[1] * nb + i, 0))),
        out_shape=jax.ShapeDtypeStruct((2 * hr, cols), F32), compiler_params=_params(1),
    )(jnp.stack([2 * x + y, c]).astype(jnp.int32), parts, pair_sums)


def _share_halves(bufs, name):
    n = len(bufs)

    def body(*refs):
        outs = refs[n:2 * n]
        send_sems, recv_sems = refs[2 * n:]
        x, y, c, _ = _position()
        copies = []
        for w in range(n):
            hr = outs[w].shape[0] // 2
            mine = outs[w].at[pl.ds(c * hr, hr), :]
            cp = _remote(mine, mine, send_sems, recv_sems, w, (x, y, 1 - c))
            cp.start()
            copies.append(cp)
        for w in range(n):
            hr = outs[w].shape[0] // 2
            theirs = outs[w].at[pl.ds((1 - c) * hr, hr), :]
            _remote(theirs, theirs, send_sems, recv_sems, w, (x, y, c)).wait_recv()
        for cp in copies:
            cp.wait_send()

    return pl.pallas_call(
        body, name=name,
        in_specs=[ANY] * n, out_specs=[ANY] * n,
        out_shape=[jax.ShapeDtypeStruct(b.shape, b.dtype) for b in bufs],
        input_output_aliases={w: w for w in range(n)},
        scratch_shapes=[pltpu.SemaphoreType.DMA((n,)), pltpu.SemaphoreType.DMA((n,))],
    )(*bufs)


def _allreduce_small(g):
    rows = g.shape[0]

    def body(g_ref, o_ref, sib, slots, send_sems, recv_sems):
        x, y, c, chips = _position()
        me = (x, y, c)
        my_chip = 2 * x + y
        pair = _remote(g_ref, sib, send_sems, recv_sems, 0, (x, y, 1 - c))
        pair.start()
        pair.wait()
        slots[my_chip] = g_ref[...] + sib[...]
        sent = []
        for j, (cx, cy) in enumerate(chips):
            cp = _remote(slots.at[my_chip], slots.at[my_chip], send_sems, recv_sems, 1 + j, (cx, cy, c))
            cp.start()
            sent.append(cp)
        for j, (cx, cy) in enumerate(chips):
            got = slots.at[2 * cx + cy]
            _remote(got, got, send_sems, recv_sems, 1 + j, me).wait_recv()
        for cp in sent:
            cp.wait_send()
        o_ref[...] = ((slots[0] + slots[1]) + slots[2]) + slots[3]

    vm = pl.BlockSpec(memory_space=pltpu.VMEM)
    return pl.pallas_call(
        body, name="allreduce_small",
        in_specs=[vm], out_specs=vm, out_shape=jax.ShapeDtypeStruct((rows, 128), F32),
        scratch_shapes=[pltpu.VMEM((rows, 128), F32), pltpu.VMEM((N_CHIPS, rows, 128), F32),
                        pltpu.SemaphoreType.DMA((4,)), pltpu.SemaphoreType.DMA((4,))],
        compiler_params=pltpu.CompilerParams(vmem_limit_bytes=VMEM_LIMIT),
    )(g)


_SMALL =("rel_bias", "ln_v_gain", "ln_v_bias", "w_spatial", "b_spatial", "ln1_gain", "ln1_bias",
          "b_ff1", "b_ff2", "ln2_gain", "ln2_bias")
_SMALL_ROWS = 1200
_LOSS_AT = (152832 // 128, 0)


def _pack_small(parts):
    flat = jnp.concatenate([parts[k].reshape(-1).astype(F32) for k in _SMALL])
    flat = jnp.pad(flat, (0, _SMALL_ROWS * 128 - flat.shape[0]))
    return flat.reshape(_SMALL_ROWS, 128)


def _unpack_small(packed, like):
    flat = packed.reshape(-1)
    out, at = {}, 0
    for k in _SMALL:
        n = math.prod(like[k].shape)
        out[k] = flat[at:at + n].reshape(like[k].shape)
        at += n
    return out


def kernel(x, w_in, rel_bias, ln_v_gain, ln_v_bias, w_spatial, b_spatial, w_proj_a, w_proj_b, w_out, ln1_gain, ln1_bias, w_ff1, b_ff1, w_ff2, b_ff2, ln2_gain, ln2_bias, loss_target, m_w_in, m_rel_bias, m_ln_v_gain, m_ln_v_bias, m_w_spatial, m_b_spatial, m_w_proj_a, m_w_proj_b, m_w_out, m_ln1_gain, m_ln1_bias, m_w_ff1, m_b_ff1, m_w_ff2, m_b_ff2, m_ln2_gain, m_ln2_bias, v_w_in, v_rel_bias, v_ln_v_gain, v_ln_v_bias, v_w_spatial, v_b_spatial, v_w_proj_a, v_w_proj_b, v_w_out, v_ln1_gain, v_ln1_bias, v_w_ff1, v_b_ff1, v_w_ff2, v_b_ff2, v_ln2_gain, v_ln2_bias):
    args = dict(locals())
    big = ("w_in", "w_proj_a", "w_proj_b", "w_out", "w_ff1", "w_ff2")
    weights = ("w_in", "rel_bias", "ln_v_gain", "ln_v_bias", "w_spatial", "b_spatial", "w_proj_a", "w_proj_b", "w_out",
               "ln1_gain", "ln1_bias", "w_ff1", "b_ff1", "w_ff2", "b_ff2", "ln2_gain", "ln2_bias")

    xs = x[0]
    target = loss_target[0]

    (in_a,), tok = _ag_start("allgather_start_w_in", [[_place_shard(w_in[0], "place_w_in")]])
    placed = [_place_shard(args[k][0], f"place_{k}", after=tok) for k in big[1:]]
    xb = _to_bf16(xs, "x_to_bf16", after=tok)
    (in_b, in_c, in_d), tok = _ag_start("allgather_start_rest", [placed[0:3], placed[3:4], placed[4:5]])
    _, d2d_a = _ag_step("allgather_w_in_pass", None, in_a, after=tok)
    (win_g,), _ = _ag_step("allgather_w_in_done", d2d_a, None)

    proj = _proj(xb, win_g)
    _, d2d_b = _ag_step("allgather_b_pass", None, in_b, after=proj)
    attn, lse = _attention_fwd(proj, rel_bias)
    ws = w_spatial[0]
    ws_t = jnp.transpose(ws, (0, 2, 1))
    bsp_b = jnp.broadcast_to(b_spatial[0][:, :, None], (NH, 128, 128))
    gmlp = _gmlp_fwd(proj, ws, bsp_b, ln_v_gain, ln_v_bias)
    (wpa_g, wpb_g, wout_g), d2d_c = _ag_step("allgather_b_done_c_pass", d2d_b, in_c, after=gmlp)
    wout_full = wout_g.reshape(D, D)
    ya, yb, merged = _branch(attn, gmlp, wpa_g, wpb_g, proj)
    xhat1, rstd1, h1b = _out_ln1(merged, wout_full, xs, ln1_gain, ln1_bias)
    (w1_g,), d2d_d = _ag_step("allgather_c_done_d_pass", d2d_c, in_d, after=h1b)
    a, r = _ff1(h1b, w1_g, b_ff1)
    (w2_g,), _ = _ag_step("allgather_d_done", d2d_d, None, after=a)
    w2_full = w2_g.reshape(DFF, D)
    dpre2, dpre2b, st2 = _ff2_ln2_loss(a, w2_full, xhat1, ln1_gain, ln1_bias, b_ff2, ln2_gain, ln2_bias, target)

    def pair_and_chip(tag, state, after):
        local, from_sibling = _px_wait(f"pair_exchange_wait_{tag}", state, after)
        pair_sums = [_pair_sum(g, o, f"pair_sum_{tag}_{i}") for i, (g, o) in enumerate(zip(local, from_sibling))]
        return _cx_start(f"chip_exchange_start_{tag}", pair_sums)

    g_w2 = _grad_w(a, dpre2b, "grad_w_ff2", 512, 2048, False)
    px, tok = _px_start("pair_exchange_start_w_ff2", [g_w2.reshape(N_CHIPS, DFF // N_CHIPS, D)])
    dprea, g_b1 = _d_ff1(dpre2b, w2_full, r, after=tok)
    cx_w2, tok = pair_and_chip("w_ff2", px, dprea)
    g_w1 = _grad_w(h1b, dprea, "grad_w_ff1", 512, 2048, True, after=tok)
    px, tok = _px_start("pair_exchange_start_w_ff1", [g_w1])
    dpre1, dpre1b, st1 = _d_h1_ln1(dprea, w1_g, dpre2, xhat1, rstd1, ln1_gain, after=tok)
    cx_w1, tok = pair_and_chip("w_ff1", px, dpre1b)
    g_wout = _grad_w(merged, dpre1b, "grad_w_out", 512, 2048, False, after=tok)
    dya, dyb, dga, dgb = _d_merged(dpre1b, wout_full, proj, ya, yb)
    g_wpa = _grad_w(attn, dya, "grad_w_proj_a", 1024, 512, True)
    g_wpb = _grad_w(gmlp, dyb, "grad_w_proj_b", 1024, 512, True)
    px, tok = _px_start("pair_exchange_start_b", [g_wpa, g_wpb, g_wout.reshape(N_CHIPS, D // N_CHIPS, D)])
    dattn, dgmlp = _d_branches(dya, dyb, wpa_g, wpb_g, after=tok)
    duv, g_ws, g_bs, stv = _gmlp_bwd(proj, dgmlp, ws, ws_t, bsp_b, ln_v_gain, ln_v_bias)
    cx_b, tok = pair_and_chip("b", px, duv)
    dq, dk, dv, ds_sums = _attention_bwd(proj, dattn, attn, lse, rel_bias, after=tok)
    g_rb = _rel_bias_grad(ds_sums)[:, :NH]

    small_g = dict(rel_bias=g_rb, ln_v_gain=stv[0], ln_v_bias=stv[1], w_spatial=g_ws, b_spatial=g_bs[:, :, 0],
                   ln1_gain=st1[0], ln1_bias=st1[1], b_ff1=g_b1, b_ff2=st2[2], ln2_gain=st2[0], ln2_bias=st2[1])
    gs = _allreduce_small(_pack_small(small_g).at[_LOSS_AT].set(st2[3, 0]))
    ds_, ms_, vs_ = _adamw(_pack_small({k: args[k] for k in _SMALL}), gs,
                           _pack_small({k: args["m_" + k] for k in _SMALL}),
                           _pack_small({k: args["v_" + k] for k in _SMALL}), "adamw_small")
    like = {k: args[k] for k in _SMALL}
    grads, deltas, new_m, new_v = (_unpack_small(t, like) for t in (gs, ds_, ms_, vs_))

    dproj = jnp.concatenate([dq, dk, dv, duv, dga, dgb], axis=1)
    g_win = _grad_w(xb, dproj, "grad_w_in", 512, 2304, True, after=gs)
    px, tok = _px_start("pair_exchange_start_w_in", [g_win])
    grad_x = _d_x(dproj, win_g, dpre1, after=tok)
    cx_in, tok = pair_and_chip("w_in", px, grad_x)

    def reduce_finish(tag, state, names, after):
        pair_sums, from_chips = _cx_wait(f"chip_exchange_wait_{tag}", state, after)
        halves = [_chip_sum(p, own, f"chip_sum_{k}") for p, own, k in zip(from_chips, pair_sums, names)]
        last = None
        for k, g in zip(names, _share_halves(halves, f"share_halves_{tag}")):
            d_, m_, v_ = _adamw(args[k][0], g, args["m_" + k][0], args["v_" + k][0], f"adamw_{k}")
            grads[k], deltas[k], new_m[k], new_v[k] = g[None], d_[None], m_[None], v_[None]
            last = d_
        return last

    done = reduce_finish("w_ff2", cx_w2, ["w_ff2"], tok)
    done = reduce_finish("w_ff1", cx_w1, ["w_ff1"], done)
    done = reduce_finish("b", cx_b, ["w_proj_a", "w_proj_b", "w_out"], done)
    reduce_finish("w_in", cx_in, ["w_in"], done)

    loss = gs[_LOSS_AT] * (0.5 / D)
    return (loss, grad_x[None], *[grads[k] for k in weights], *[deltas[k] for k in weights],
            *[new_m[k] for k in weights], *[new_v[k] for k in weights])
```

```python
import functools
import math

import numpy as np
import jax
import jax.numpy as jnp
from jax import lax
from jax.experimental import pallas as pl
from jax.experimental.pallas import tpu as pltpu

F32 = jnp.float32
BF16 = jnp.bfloat16

S = 2048
D = 2048
DA = 1024
DB = 1024
DFF = 8192
DIN = 9216
NH = 8
HD = 128
NBLK = 16
PATTERNS = ((128, 1), (512, 4), (2048, 16))
N_BUCKETS = 32
MAX_DISTANCE = 2048
ALPHA = 2.0 ** 0.25
LN_EPS = 1e-5
NEG_INF = -1e30
SCALE = HD ** -0.5
N_CHIPS = 4

ADAM_LR = 0.001
ADAM_B1 = 0.9
ADAM_B2 = 0.999
ADAM_EPS = 1e-08
ADAM_WD = 0.01
ADAM_STEP = 10

VMEM_LIMIT = 56 * 1024 * 1024
MESH = pl.DeviceIdType.MESH
ANY = pl.BlockSpec(memory_space=pl.ANY)


def _params(n_axes, vmem=VMEM_LIMIT):
    return pltpu.CompilerParams(dimension_semantics=("arbitrary",) * n_axes, vmem_limit_bytes=vmem)


def _bucket_tile(dilation):
    qi = np.arange(128)[:, None]
    kj = np.arange(256)[None, :]
    n = np.clip(128 + qi - kj, 0, 128) * dilation
    max_exact = N_BUCKETS // 2
    nf = np.maximum(n, 1).astype(np.float32)
    large = max_exact + (np.log(nf / np.float32(max_exact)) / np.float32(math.log(MAX_DISTANCE / max_exact))
                         * np.float32(N_BUCKETS - max_exact)).astype(np.int32)
    large = np.minimum(large, N_BUCKETS - 1)
    return np.where(n < max_exact, n, large).astype(np.int32)


def _gelu(x):
    c = math.sqrt(2.0 / math.pi)
    t = jnp.tanh(c * (x + 0.044715 * x * x * x))
    return 0.5 * x * (1.0 + t), t


def _gelu_grad(x, t):
    c = math.sqrt(2.0 / math.pi)
    return 0.5 * (1.0 + t) + 0.5 * x * (1.0 - t * t) * c * (1.0 + 3.0 * 0.044715 * x * x)


def _sigmoid(x):
    return 1.0 / (1.0 + jnp.exp(-x))


def _dot(a, b):
    return jnp.dot(a, b, preferred_element_type=F32)


def _behind(body, n_in, after):
    if after is None:
        return body, [], []
    return (lambda *refs: body(*refs[:n_in], *refs[n_in + 1:])), [ANY], [after]


def _dot_nt(a, b):
    return lax.dot_general(a, b, (((1,), (1,)), ((), ())), preferred_element_type=F32)


def _proj(xb, win_g):
    tn = 768
    per = 2304 // tn

    def body(x_ref, w_ref, o_ref):
        o_ref[...] = _dot(x_ref[...], w_ref[...])

    return pl.pallas_call(
        body, name="proj", grid=(DIN // tn,),
        in_specs=[pl.BlockSpec((S, D), lambda j: (0, 0)),
                  pl.BlockSpec((None, D, tn), lambda j: (j // per, 0, j % per))],
        out_specs=pl.BlockSpec((S, tn), lambda j: (0, j)),
        out_shape=jax.ShapeDtypeStruct((S, DIN), F32),
        compiler_params=_params(1),
    )(xb, win_g)


HEADS_PER_STEP = 2


def _head_bias_tiles(rb_ref, bk_ref, bias_scr, first_head):
    qi = lax.broadcasted_iota(jnp.int32, (128, 256), 0)
    kj = lax.broadcasted_iota(jnp.int32, (128, 256), 1)
    steps = 128 + qi - kj
    band = (steps >= 0) & (steps <= 128)
    bias_scr[...] = jnp.zeros_like(bias_scr)
    for p in range(len(PATTERNS)):
        bucket = bk_ref[p]

        def one_bucket(t, carry):
            hit = bucket == t
            for j in range(HEADS_PER_STEP):
                bias_scr[p, j] = jnp.where(hit, rb_ref[t, first_head + j], bias_scr[p, j])
            return carry

        lax.fori_loop(0, N_BUCKETS, one_bucket, 0)
        for j in range(HEADS_PER_STEP):
            bias_scr[p, j] = jnp.where(band, bias_scr[p, j], NEG_INF)


def _block_rows(b, dilation):
    nblk = NBLK // dilation
    r, n = b // nblk, b % nblk
    start = r + n * (128 * dilation)
    prev_start = jnp.maximum(start - 128 * dilation, r)
    if dilation == 1:
        return pl.ds(pl.multiple_of(start, 128), 128), pl.ds(pl.multiple_of(prev_start, 128), 128), n > 0
    return pl.ds(start, 128, stride=dilation), pl.ds(prev_start, 128, stride=dilation), n > 0


def _head_specs(first):
    return [pl.BlockSpec((S, HD), lambda g, j=j: (0, first + g * HEADS_PER_STEP + j)) for j in range(HEADS_PER_STEP)]


def _pair_spec():
    return pl.BlockSpec((S, HEADS_PER_STEP * HD), lambda g: (0, g))


def _attention_fwd(proj, rel_bias):
    hps = HEADS_PER_STEP
    buckets = jnp.asarray(np.stack([_bucket_tile(d) for _, d in PATTERNS]))

    def body(rb_ref, bk_ref, *refs):
        q_refs, k_refs, v_refs = (refs[i * hps:(i + 1) * hps] for i in range(3))
        o_ref, lse_ref, bias_scr = refs[3 * hps:3 * hps + 3]
        acc_scrs, m_scrs, l_scrs = (refs[3 * hps + 3 + i * hps:3 * hps + 3 + (i + 1) * hps] for i in range(3))
        _head_bias_tiles(rb_ref, bk_ref, bias_scr, pl.program_id(0) * hps)
        kj = lax.broadcasted_iota(jnp.int32, (128, 256), 1)
        for p, (_, d) in enumerate(PATTERNS):
            def block(b, carry):
                rows, prows, has_prev = _block_rows(b, d)
                key_ok = (kj >= 128) | has_prev
                scores = []
                for j in range(hps):
                    q = q_refs[j][rows, :].astype(BF16)
                    scores.append(jnp.concatenate([_dot_nt(q, k_refs[j][prows, :].astype(BF16)),
                                                   _dot_nt(q, k_refs[j][rows, :].astype(BF16))], axis=1))
                soft = []
                for j in range(hps):
                    s = jnp.where(key_ok, scores[j] * SCALE + bias_scr[p, j], NEG_INF)
                    m = jnp.max(s, axis=1, keepdims=True)
                    e = jnp.exp(s - m)
                    soft.append((m, jnp.sum(e, axis=1, keepdims=True), e.astype(BF16)))
                outs = [_dot(soft[j][2][:, :128], v_refs[j][prows, :].astype(BF16))
                        + _dot(soft[j][2][:, 128:], v_refs[j][rows, :].astype(BF16)) for j in range(hps)]
                for j in range(hps):
                    acc_scr, m_scr, l_scr = acc_scrs[j], m_scrs[j], l_scrs[j]
                    (m, den, _), o = soft[j], outs[j]
                    if p == 0:
                        acc_scr[rows, :] = o
                        m_scr[rows, :] = jnp.broadcast_to(m, (128, HD))
                        l_scr[rows, :] = jnp.broadcast_to(den, (128, HD))
                    else:
                        m_old = m_scr[rows, :]
                        m_new = jnp.maximum(m_old, m)
                        w_old, w_new = jnp.exp(m_old - m_new), jnp.exp(m - m_new)
                        acc_scr[rows, :] = acc_scr[rows, :] * w_old + o * w_new
                        l_scr[rows, :] = l_scr[rows, :] * w_old + den * w_new
                        m_scr[rows, :] = m_new
                return carry

            lax.fori_loop(0, NBLK, block, 0)
        for j in range(hps):
            cols = slice(j * HD, (j + 1) * HD)
            den = l_scrs[j][...]
            o_ref[:, cols] = (acc_scrs[j][...] / den).astype(BF16)
            lse_ref[:, cols] = m_scrs[j][...] + jnp.log(den)

    return pl.pallas_call(
        body, name="attention_fwd", grid=(NH // hps,),
        in_specs=[pl.BlockSpec(memory_space=pltpu.SMEM), pl.BlockSpec((3, 128, 256), lambda g: (0, 0, 0))]
        + _head_specs(0) + _head_specs(NH) + _head_specs(2 * NH),
        out_specs=[_pair_spec(), _pair_spec()],
        out_shape=[jax.ShapeDtypeStruct((S, DA), BF16), jax.ShapeDtypeStruct((S, DA), F32)],
        scratch_shapes=[pltpu.VMEM((3, hps, 128, 256), F32)] + [pltpu.VMEM((S, HD), F32)] * (3 * hps),
        compiler_params=_params(1),
    )(rel_bias, buckets, *([proj] * (3 * hps)))


def _attention_bwd(proj, dattn, attn, lse, rel_bias, after=None):
    hps = HEADS_PER_STEP

    def body(rb_ref, bk_ref, *refs):
        q_refs, k_refs, v_refs, do_refs, o_refs, lse_refs = (refs[i * hps:(i + 1) * hps] for i in range(6))
        dq_ref, dk_ref, dv_ref, ds_ref, bias_scr = refs[6 * hps:6 * hps + 5]
        dl_scrs, dq_scrs, dk_scrs, dv_scrs = (refs[6 * hps + 5 + i * hps:6 * hps + 5 + (i + 1) * hps] for i in range(4))
        _head_bias_tiles(rb_ref, bk_ref, bias_scr, pl.program_id(0) * hps)
        ds_ref[...] = jnp.zeros_like(ds_ref)
        for j in range(hps):
            dq_scrs[j][...] = jnp.zeros((S, HD), F32)
            dk_scrs[j][...] = jnp.zeros((S, HD), F32)
            dv_scrs[j][...] = jnp.zeros((S, HD), F32)
            prod = do_refs[j][...] * o_refs[j][...].astype(F32)
            dl_scrs[j][...] = jnp.broadcast_to(jnp.sum(prod, axis=1, keepdims=True), (S, HD))
        for p, (_, d) in enumerate(PATTERNS):
            def block(b, carry):
                rows, prows, has_prev = _block_rows(b, d)
                ops, raw = [], []
                for j in range(hps):
                    q = q_refs[j][rows, :].astype(BF16)
                    kc, kp = k_refs[j][rows, :].astype(BF16), k_refs[j][prows, :].astype(BF16)
                    vc, vp = v_refs[j][rows, :].astype(BF16), v_refs[j][prows, :].astype(BF16)
                    do = do_refs[j][rows, :].astype(BF16)
                    ops.append((q, kc, kp, do))
                    raw.append((_dot_nt(q, kc), _dot_nt(q, kp), _dot_nt(do, vc), _dot_nt(do, vp)))
                probs = []
                for j in range(hps):
                    s_c, s_p, dp_c, dp_p = raw[j]
                    lse_b, dl_b = lse_refs[j][rows, :], dl_scrs[j][rows, :]
                    p_c = jnp.exp(s_c * SCALE + bias_scr[p, j, :, 128:256] - lse_b)
                    p_p = jnp.where(has_prev, jnp.exp(s_p * SCALE + bias_scr[p, j, :, 0:128] - lse_b), 0.0)
                    ds_c, ds_p = p_c * (dp_c - dl_b), p_p * (dp_p - dl_b)
                    ds_ref[p, j, :, 0:128] += ds_p
                    ds_ref[p, j, :, 128:256] += ds_c
                    probs.append((p_c, p_p, ds_c, ds_p))
                grads = []
                for j in range(hps):
                    (q, kc, kp, do), (p_c, p_p, ds_c, ds_p) = ops[j], probs[j]
                    grads.append(((_dot(ds_c.astype(BF16), kc) + _dot(ds_p.astype(BF16), kp)) * SCALE,
                                  _dot(ds_c.T.astype(BF16), q) * SCALE, _dot(ds_p.T.astype(BF16), q) * SCALE,
                                  _dot(p_c.T.astype(BF16), do), _dot(p_p.T.astype(BF16), do)))
                for j in range(hps):
                    dq, dk_c, dk_p, dv_c, dv_p = grads[j]
                    dq_scrs[j][rows, :] += dq
                    dk_scrs[j][rows, :] += dk_c
                    dk_scrs[j][prows, :] += dk_p
                    dv_scrs[j][rows, :] += dv_c
                    dv_scrs[j][prows, :] += dv_p
                return carry

            lax.fori_loop(0, NBLK, block, 0)
        for j in range(hps):
            cols = slice(j * HD, (j + 1) * HD)
            dq_ref[:, cols] = dq_scrs[j][...].astype(BF16)
            dk_ref[:, cols] = dk_scrs[j][...].astype(BF16)
            dv_ref[:, cols] = dv_scrs[j][...].astype(BF16)

    buckets = jnp.asarray(np.stack([_bucket_tile(d) for _, d in PATTERNS]))
    body, more_specs, more = _behind(body, 2 + 6 * hps, after)
    return pl.pallas_call(
        body, name="attention_bwd", grid=(NH // hps,),
        in_specs=[pl.BlockSpec(memory_space=pltpu.SMEM), pl.BlockSpec((3, 128, 256), lambda g: (0, 0, 0))]
        + _head_specs(0) + _head_specs(NH) + _head_specs(2 * NH) + _head_specs(0) + _head_specs(0) + _head_specs(0)
        + more_specs,
        out_specs=[_pair_spec(), _pair_spec(), _pair_spec(), pl.BlockSpec((3, hps, 128, 256), lambda g: (0, g, 0, 0))],
        out_shape=[jax.ShapeDtypeStruct((S, DA), BF16)] * 3 + [jax.ShapeDtypeStruct((3, NH, 128, 256), F32)],
        scratch_shapes=[pltpu.VMEM((3, hps, 128, 256), F32)] + [pltpu.VMEM((S, HD), F32)] * (4 * hps),
        compiler_params=_params(1),
    )(rel_bias, buckets, *([proj] * (3 * hps)), *([dattn] * hps), *([attn] * hps), *([lse] * hps), *more)


def _gmlp_parts(u_ref, vb_ref, g_ref, be_ref):
    u = u_ref[...]
    u_act, tu = _gelu(u)
    vb = vb_ref[...]
    gv, tv = _gelu(vb)
    mean = jnp.mean(gv, axis=1, keepdims=True)
    cen = gv - mean
    var = jnp.mean(cen * cen, axis=1, keepdims=True)
    rstd = lax.rsqrt(var + LN_EPS)
    xhat = cen * rstd
    vn = xhat * g_ref[...] + be_ref[...]
    return u, tu, u_act, vb, tv, rstd, xhat, vn


def _gmlp_fwd(proj, ws, bsp_b, gain_v, bias_v):
    def body(u_ref, vb_ref, ws_ref, bsp_ref, g_ref, be_ref, o_ref):
        _, _, u_act, _, _, _, _, vn = _gmlp_parts(u_ref, vb_ref, g_ref, be_ref)
        row = lax.broadcasted_iota(jnp.int32, (128, 128), 0)
        col = lax.broadcasted_iota(jnp.int32, (128, 128), 1)
        causal = row >= col
        for g in range(NH):
            cols = slice(g * 128, (g + 1) * 128)
            wsg = jnp.where(causal, ws_ref[g], 0.0).astype(BF16)
            z = _dot(wsg, vn[:, cols].astype(BF16)) + bsp_ref[g]
            o_ref[:, cols] = (u_act[:, cols] * z).astype(BF16)

    return pl.pallas_call(
        body, name="gmlp_fwd", grid=(NBLK,),
        in_specs=[pl.BlockSpec((128, DB), lambda c: (c, 3)), pl.BlockSpec((128, DB), lambda c: (c, 4)),
                  pl.BlockSpec((NH, 128, 128), lambda c: (0, 0, 0)), pl.BlockSpec((NH, 128, 128), lambda c: (0, 0, 0)),
                  pl.BlockSpec((1, DB), lambda c: (0, 0)), pl.BlockSpec((1, DB), lambda c: (0, 0))],
        out_specs=pl.BlockSpec((128, DB), lambda c: (c, 0)),
        out_shape=jax.ShapeDtypeStruct((S, DB), BF16),
        compiler_params=_params(1),
    )(proj, proj, ws, bsp_b, gain_v, bias_v)


def _branch(attn, gmlp, wpa_g, wpb_g, proj):
    tn = 512

    def body(a_ref, g_ref, wa_ref, wb_ref, ga_ref, gb_ref, ya_ref, yb_ref, mg_ref):
        ya = _dot(a_ref[...], wa_ref[...])
        yb = _dot(g_ref[...], wb_ref[...])
        ya_ref[...] = ya.astype(BF16)
        yb_ref[...] = yb.astype(BF16)
        mg_ref[...] = (_sigmoid(ga_ref[...]) * ya + _sigmoid(gb_ref[...]) * yb).astype(BF16)

    out = pl.BlockSpec((S, tn), lambda j: (0, j))
    return pl.pallas_call(
        body, name="branch", grid=(D // tn,),
        in_specs=[pl.BlockSpec((S, DA), lambda j: (0, 0)), pl.BlockSpec((S, DB), lambda j: (0, 0)),
                  pl.BlockSpec((None, DA, tn), lambda j: (j, 0, 0)), pl.BlockSpec((None, DB, tn), lambda j: (j, 0, 0)),
                  pl.BlockSpec((S, tn), lambda j: (0, 5120 // tn + j)), pl.BlockSpec((S, tn), lambda j: (0, 7168 // tn + j))],
        out_specs=[out, out, out],
        out_shape=[jax.ShapeDtypeStruct((S, D), BF16)] * 3,
        compiler_params=_params(1),
    )(attn, gmlp, wpa_g, wpb_g, proj, proj)


def _out_ln1(merged, wout_g, x, gain, bias):
    tm = 256

    def body(m_ref, w_ref, x_ref, g_ref, b_ref, xh_ref, rs_ref, h_ref):
        pre = ALPHA * x_ref[...] + _dot(m_ref[...], w_ref[...])
        mean = jnp.mean(pre, axis=1, keepdims=True)
        cen = pre - mean
        var = jnp.mean(cen * cen, axis=1, keepdims=True)
        rstd = lax.rsqrt(var + LN_EPS)
        xhat = cen * rstd
        xh_ref[...] = xhat
        rs_ref[...] = jnp.broadcast_to(rstd, (tm, 128))
        h_ref[...] = (xhat * g_ref[...] + b_ref[...]).astype(BF16)

    row = pl.BlockSpec((tm, D), lambda i: (i, 0))
    vec = pl.BlockSpec((1, D), lambda i: (0, 0))
    return pl.pallas_call(
        body, name="out_ln1", grid=(S // tm,),
        in_specs=[row, pl.BlockSpec((D, D), lambda i: (0, 0)), row, vec, vec],
        out_specs=[row, pl.BlockSpec((tm, 128), lambda i: (i, 0)), row],
        out_shape=[jax.ShapeDtypeStruct((S, D), F32), jax.ShapeDtypeStruct((S, 128), F32),
                   jax.ShapeDtypeStruct((S, D), BF16)],
        compiler_params=_params(1),
    )(merged, wout_g, x, gain, bias)


def _ff1(h1b, w1_g, b1):
    tn = 512
    per = D // tn

    def body(h_ref, w_ref, b_ref, a_ref, r_ref):
        r = jnp.maximum(_dot(h_ref[...], w_ref[...]) + b_ref[...], 0.0)
        r_ref[...] = r.astype(BF16)
        a_ref[...] = (r * r).astype(BF16)

    out = pl.BlockSpec((S, tn), lambda j: (0, j))
    return pl.pallas_call(
        body, name="ff1", grid=(DFF // tn,),
        in_specs=[pl.BlockSpec((S, D), lambda j: (0, 0)),
                  pl.BlockSpec((None, D, tn), lambda j: (j // per, 0, j % per)),
                  pl.BlockSpec((1, tn), lambda j: (0, j))],
        out_specs=[out, out],
        out_shape=[jax.ShapeDtypeStruct((S, DFF), BF16)] * 2,
        compiler_params=_params(1),
    )(h1b, w1_g, b1)


def _ff2_ln2_loss(a, w2_g, xhat1, g1, b1, b2, g2, be2, target):
    tm, tk = 512, 1024
    nk = DFF // tk

    def body(a_ref, w_ref, xh_ref, g1_ref, b1_ref, b2_ref, g2_ref, be2_ref, t_ref, d_ref, db_ref, st_ref, acc):
        i, k = pl.program_id(0), pl.program_id(1)

        @pl.when(k == 0)
        def _():
            acc[...] = jnp.zeros_like(acc)

        @pl.when((i == 0) & (k == 0))
        def _():
            st_ref[...] = jnp.zeros_like(st_ref)

        acc[...] += _dot(a_ref[...], w_ref[...])

        @pl.when(k == nk - 1)
        def _():
            def rows_chunk(ci, carry):
                rows = pl.ds(pl.multiple_of(ci * 128, 128), 128)
                h1 = xh_ref[rows, :] * g1_ref[...] + b1_ref[...]
                pre = ALPHA * h1 + acc[rows, :] + b2_ref[...]
                mean = jnp.mean(pre, axis=1, keepdims=True)
                cen = pre - mean
                var = jnp.mean(cen * cen, axis=1, keepdims=True)
                rstd = lax.rsqrt(var + LN_EPS)
                xhat = cen * rstd
                y = xhat * g2_ref[...] + be2_ref[...]
                err = y - t_ref[rows, :]
                dy = err * (1.0 / D)
                g = dy * g2_ref[...]
                dpre = rstd * (g - jnp.mean(g, axis=1, keepdims=True)
                               - xhat * jnp.mean(g * xhat, axis=1, keepdims=True))
                d_ref[rows, :] = dpre
                db_ref[rows, :] = dpre.astype(BF16)
                st_ref[0:1, :] += jnp.sum(dy * xhat, axis=0, keepdims=True)
                st_ref[1:2, :] += jnp.sum(dy, axis=0, keepdims=True)
                st_ref[2:3, :] += jnp.sum(dpre, axis=0, keepdims=True)
                st_ref[3:4, :] += jnp.broadcast_to(jnp.sum(err * err).reshape(1, 1), (1, D))
                return carry

            lax.fori_loop(0, tm // 128, rows_chunk, 0)

    row = pl.BlockSpec((tm, D), lambda i, k: (i, 0))
    vec = pl.BlockSpec((1, D), lambda i, k: (0, 0))
    return pl.pallas_call(
        body, name="ff2_ln2_loss", grid=(S // tm, nk),
        in_specs=[pl.BlockSpec((tm, tk), lambda i, k: (i, k)), pl.BlockSpec((tk, D), lambda i, k: (k, 0)),
                  row, vec, vec, vec, vec, vec, row],
        out_specs=[row, row, pl.BlockSpec((8, D), lambda i, k: (0, 0))],
        out_shape=[jax.ShapeDtypeStruct((S, D), F32), jax.ShapeDtypeStruct((S, D), BF16),
                   jax.ShapeDtypeStruct((8, D), F32)],
        scratch_shapes=[pltpu.VMEM((tm, D), F32)],
        compiler_params=_params(2),
    )(a, w2_g, xhat1, g1, b1, b2, g2, be2, target)


def _grad_w(act, dout, name, ti, tj, sharded, after=None):
    m, n = act.shape[1], dout.shape[1]
    ns = n // N_CHIPS
    per = ns // tj if sharded else None

    def body(a_ref, b_ref, o_ref, at_scr):
        @pl.when(pl.program_id(1) == 0)
        def _():
            at_scr[...] = a_ref[...].T

        o_ref[...] = _dot(at_scr[...], b_ref[...])

    if sharded:
        out_spec = pl.BlockSpec((None, ti, tj), lambda i, j: (j // per, i, j % per))
        out_shape = jax.ShapeDtypeStruct((N_CHIPS, m, ns), F32)
    else:
        out_spec = pl.BlockSpec((ti, tj), lambda i, j: (i, j))
        out_shape = jax.ShapeDtypeStruct((m, n), F32)
    body, more_specs, more = _behind(body, 2, after)
    return pl.pallas_call(
        body, name=name, grid=(m // ti, n // tj),
        in_specs=[pl.BlockSpec((S, ti), lambda i, j: (0, i)), pl.BlockSpec((S, tj), lambda i, j: (0, j))] + more_specs,
        out_specs=out_spec, out_shape=out_shape,
        scratch_shapes=[pltpu.VMEM((ti, S), BF16)],
        compiler_params=_params(2),
    )(act, dout, *more)


def _d_ff1(dpre2b, w2_g, r, after=None):
    tn = 512

    def body(d_ref, w_ref, r_ref, o_ref, gb_ref):
        da = _dot_nt(d_ref[...], w_ref[...])
        dp = da * (2.0 * r_ref[...].astype(F32))
        o_ref[...] = dp.astype(BF16)
        gb_ref[...] = jnp.sum(dp, axis=0, keepdims=True)

    body, more_specs, more = _behind(body, 3, after)
    return pl.pallas_call(
        body, name="d_ff1", grid=(DFF // tn,),
        in_specs=[pl.BlockSpec((S, D), lambda j: (0, 0)), pl.BlockSpec((tn, D), lambda j: (j, 0)),
                  pl.BlockSpec((S, tn), lambda j: (0, j))] + more_specs,
        out_specs=[pl.BlockSpec((S, tn), lambda j: (0, j)), pl.BlockSpec((1, tn), lambda j: (0, j))],
        out_shape=[jax.ShapeDtypeStruct((S, DFF), BF16), jax.ShapeDtypeStruct((1, DFF), F32)],
        compiler_params=_params(1),
    )(dpre2b, w2_g, r, *more)


def _d_h1_ln1(dprea, w1_g, dpre2, xhat1, rstd1, g1, after=None):
    tm, tk = 512, 1024
    per = D // tk
    nk = DFF // tk

    def body(a_ref, w_ref, d2_ref, xh_ref, rs_ref, g_ref, d_ref, db_ref, st_ref, acc):
        i, k = pl.program_id(0), pl.program_id(1)

        @pl.when(k == 0)
        def _():
            acc[...] = jnp.zeros_like(acc)

        @pl.when((i == 0) & (k == 0))
        def _():
            st_ref[...] = jnp.zeros_like(st_ref)

        acc[...] += _dot_nt(a_ref[...], w_ref[...])

        @pl.when(k == nk - 1)
        def _():
            def rows_chunk(ci, carry):
                rows = pl.ds(pl.multiple_of(ci * 128, 128), 128)
                dh = ALPHA * d2_ref[rows, :] + acc[rows, :]
                xhat = xh_ref[rows, :]
                g = dh * g_ref[...]
                dpre = rs_ref[rows, 0:1] * (g - jnp.mean(g, axis=1, keepdims=True)
                                            - xhat * jnp.mean(g * xhat, axis=1, keepdims=True))
                d_ref[rows, :] = dpre
                db_ref[rows, :] = dpre.astype(BF16)
                st_ref[0:1, :] += jnp.sum(dh * xhat, axis=0, keepdims=True)
                st_ref[1:2, :] += jnp.sum(dh, axis=0, keepdims=True)
                return carry

            lax.fori_loop(0, tm // 128, rows_chunk, 0)

    row = pl.BlockSpec((tm, D), lambda i, k: (i, 0))
    body, more_specs, more = _behind(body, 6, after)
    return pl.pallas_call(
        body, name="d_h1_ln1", grid=(S // tm, nk),
        in_specs=[pl.BlockSpec((tm, tk), lambda i, k: (i, k)),
                  pl.BlockSpec((None, D, tk), lambda i, k: (k // per, 0, k % per)),
                  row, row, pl.BlockSpec((tm, 128), lambda i, k: (i, 0)), pl.BlockSpec((1, D), lambda i, k: (0, 0))]
        + more_specs,
        out_specs=[row, row, pl.BlockSpec((8, D), lambda i, k: (0, 0))],
        out_shape=[jax.ShapeDtypeStruct((S, D), F32), jax.ShapeDtypeStruct((S, D), BF16),
                   jax.ShapeDtypeStruct((8, D), F32)],
        scratch_shapes=[pltpu.VMEM((tm, D), F32)],
        compiler_params=_params(2),
    )(dprea, w1_g, dpre2, xhat1, rstd1, g1, *more)


def _d_merged(dpre1b, wout_g, proj, ya, yb):
    tm, tn = 512, 1024

    def body(d_ref, w_ref, ga_ref, gb_ref, ya_ref, yb_ref, dya_ref, dyb_ref, dga_ref, dgb_ref):
        dm = _dot_nt(d_ref[...], w_ref[...])
        sa = _sigmoid(ga_ref[...])
        sb = _sigmoid(gb_ref[...])
        dya_ref[...] = (dm * sa).astype(BF16)
        dyb_ref[...] = (dm * sb).astype(BF16)
        dga_ref[...] = (dm * ya_ref[...].astype(F32) * sa * (1.0 - sa)).astype(BF16)
        dgb_ref[...] = (dm * yb_ref[...].astype(F32) * sb * (1.0 - sb)).astype(BF16)

    tile = pl.BlockSpec((tm, tn), lambda i, j: (i, j))
    return pl.pallas_call(
        body, name="d_merged", grid=(S // tm, D // tn),
        in_specs=[pl.BlockSpec((tm, D), lambda i, j: (i, 0)), pl.BlockSpec((tn, D), lambda i, j: (j, 0)),
                  pl.BlockSpec((tm, tn), lambda i, j: (i, 5 + j)), pl.BlockSpec((tm, tn), lambda i, j: (i, 7 + j)),
                  tile, tile],
        out_specs=[tile] * 4,
        out_shape=[jax.ShapeDtypeStruct((S, D), BF16)] * 4,
        compiler_params=_params(2),
    )(dpre1b, wout_g, proj, proj, ya, yb)


def _d_branches(dya, dyb, wpa_g, wpb_g, after=None):
    tk = 512

    def body(da_ref, db_ref, wa_ref, wb_ref, oa_ref, ob_ref):
        @pl.when(pl.program_id(0) == 0)
        def _():
            oa_ref[...] = jnp.zeros_like(oa_ref)
            ob_ref[...] = jnp.zeros_like(ob_ref)

        oa_ref[...] += _dot_nt(da_ref[...], wa_ref[...])
        ob_ref[...] += _dot_nt(db_ref[...], wb_ref[...])

    body, more_specs, more = _behind(body, 4, after)
    return pl.pallas_call(
        body, name="d_branches", grid=(D // tk,),
        in_specs=[pl.BlockSpec((S, tk), lambda k: (0, k)), pl.BlockSpec((S, tk), lambda k: (0, k)),
                  pl.BlockSpec((None, DA, tk), lambda k: (k, 0, 0)), pl.BlockSpec((None, DB, tk), lambda k: (k, 0, 0))]
        + more_specs,
        out_specs=[pl.BlockSpec((S, DA), lambda k: (0, 0)), pl.BlockSpec((S, DB), lambda k: (0, 0))],
        out_shape=[jax.ShapeDtypeStruct((S, DA), F32), jax.ShapeDtypeStruct((S, DB), F32)],
        compiler_params=_params(1),
    )(dya, dyb, wpa_g, wpb_g, *more)


def _gmlp_bwd(proj, dgmlp, ws, ws_t, bsp_b, gain_v, bias_v):
    def body(u_ref, vb_ref, dg_ref, ws_ref, wst_ref, bsp_ref, g_ref, be_ref, duv_ref, gws_ref, gbs_ref, st_ref):
        @pl.when(pl.program_id(0) == 0)
        def _():
            gws_ref[...] = jnp.zeros_like(gws_ref)
            gbs_ref[...] = jnp.zeros_like(gbs_ref)
            st_ref[...] = jnp.zeros_like(st_ref)

        u, tu, u_act, vb, tv, rstd, xhat, vn = _gmlp_parts(u_ref, vb_ref, g_ref, be_ref)
        dg = dg_ref[...]
        dz = dg * u_act
        row = lax.broadcasted_iota(jnp.int32, (128, 128), 0)
        col = lax.broadcasted_iota(jnp.int32, (128, 128), 1)
        causal = row >= col
        causal_t = row <= col
        dvn_parts = []
        z_parts = []
        for g in range(NH):
            cols = slice(g * 128, (g + 1) * 128)
            vng = vn[:, cols].astype(BF16)
            dzg = dz[:, cols]
            dzb = dzg.astype(BF16)
            wsg = jnp.where(causal, ws_ref[g], 0.0).astype(BF16)
            wsg_t = jnp.where(causal_t, wst_ref[g], 0.0).astype(BF16)
            z_parts.append(_dot(wsg, vng) + bsp_ref[g])
            gws_ref[g] += jnp.where(causal, _dot_nt(dzb, vng), 0.0)
            gbs_ref[g] += jnp.broadcast_to(jnp.sum(dzg, axis=1, keepdims=True), (128, 128))
            dvn_parts.append(_dot(wsg_t, dzb))
        z = jnp.concatenate(z_parts, axis=1)
        dvn = jnp.concatenate(dvn_parts, axis=1)
        du = dg * z * _gelu_grad(u, tu)
        st_ref[0:1, :] += jnp.sum(dvn * xhat, axis=0, keepdims=True)
        st_ref[1:2, :] += jnp.sum(dvn, axis=0, keepdims=True)
        gg = dvn * g_ref[...]
        dgv = rstd * (gg - jnp.mean(gg, axis=1, keepdims=True) - xhat * jnp.mean(gg * xhat, axis=1, keepdims=True))
        dvb = dgv * _gelu_grad(vb, tv)
        duv_ref[:, 0:DB] = du.astype(BF16)
        duv_ref[:, DB:2 * DB] = dvb.astype(BF16)

    full3 = pl.BlockSpec((NH, 128, 128), lambda c: (0, 0, 0))
    vec = pl.BlockSpec((1, DB), lambda c: (0, 0))
    return pl.pallas_call(
        body, name="gmlp_bwd", grid=(NBLK,),
        in_specs=[pl.BlockSpec((128, DB), lambda c: (c, 3)), pl.BlockSpec((128, DB), lambda c: (c, 4)),
                  pl.BlockSpec((128, DB), lambda c: (c, 0)), full3, full3, full3, vec, vec],
        out_specs=[pl.BlockSpec((128, 2 * DB), lambda c: (c, 0)), full3, full3, pl.BlockSpec((8, DB), lambda c: (0, 0))],
        out_shape=[jax.ShapeDtypeStruct((S, 2 * DB), BF16), jax.ShapeDtypeStruct((NH, 128, 128), F32),
                   jax.ShapeDtypeStruct((NH, 128, 128), F32), jax.ShapeDtypeStruct((8, DB), F32)],
        compiler_params=_params(1),
    )(proj, proj, dgmlp, ws, ws_t, bsp_b, gain_v, bias_v)


def _rel_bias_grad(ds_sums):
    buckets = jnp.asarray(np.stack([_bucket_tile(d) for _, d in PATTERNS]))

    def body(bk_ref, ds_ref, o_ref):
        row = lax.broadcasted_iota(jnp.int32, (N_BUCKETS, 128), 0)
        lane = lax.broadcasted_iota(jnp.int32, (N_BUCKETS, 128), 1)

        def one_bucket(t, out):
            hits = [bk_ref[p] == t for p in range(3)]
            for h in range(NH):
                tot = jnp.zeros((128, 256), F32)
                for p in range(3):
                    tot = tot + jnp.where(hits[p], ds_ref[p, h], 0.0)
                out = jnp.where((row == t) & (lane == h), jnp.sum(tot), out)
            return out

        o_ref[...] = lax.fori_loop(0, N_BUCKETS, one_bucket, jnp.zeros((N_BUCKETS, 128), F32))

    return pl.pallas_call(
        body, name="rel_bias_grad",
        in_specs=[pl.BlockSpec(memory_space=pltpu.VMEM)] * 2, out_specs=pl.BlockSpec(memory_space=pltpu.VMEM),
        out_shape=jax.ShapeDtypeStruct((N_BUCKETS, 128), F32),
        compiler_params=pltpu.CompilerParams(vmem_limit_bytes=VMEM_LIMIT),
    )(buckets, ds_sums)


def _d_x(dproj, win_g, dpre1, after=None):
    tm, tk = 512, 2304
    per = 2304 // tk
    nk = DIN // tk

    def body(a_ref, w_ref, d_ref, o_ref, acc):
        k = pl.program_id(1)

        @pl.when(k == 0)
        def _():
            acc[...] = ALPHA * d_ref[...]

        acc[...] += _dot_nt(a_ref[...], w_ref[...])

        @pl.when(k == nk - 1)
        def _():
            o_ref[...] = acc[...]

    row = pl.BlockSpec((tm, D), lambda i, k: (i, 0))
    body, more_specs, more = _behind(body, 3, after)
    return pl.pallas_call(
        body, name="d_x", grid=(S // tm, nk),
        in_specs=[pl.BlockSpec((tm, tk), lambda i, k: (i, k)),
                  pl.BlockSpec((None, D, tk), lambda i, k: (k // per, 0, k % per)), row] + more_specs,
        out_specs=row, out_shape=jax.ShapeDtypeStruct((S, D), F32),
        scratch_shapes=[pltpu.VMEM((tm, D), F32)],
        compiler_params=_params(2),
    )(dproj, win_g, dpre1, *more)


def _adamw(w, g, m, v, name):
    rows, cols = w.shape
    tm = max(t for t in range(8, 257, 8) if rows % t == 0)

    def body(w_ref, g_ref, m_ref, v_ref, d_ref, nm_ref, nv_ref, go_ref):
        g = g_ref[...]
        m = ADAM_B1 * m_ref[...] + (1.0 - ADAM_B1) * g
        v = ADAM_B2 * v_ref[...] + (1.0 - ADAM_B2) * (g * g)
        m_hat = m / (1.0 - ADAM_B1 ** ADAM_STEP)
        v_hat = v / (1.0 - ADAM_B2 ** ADAM_STEP)
        d_ref[...] = -ADAM_LR * (m_hat / (jnp.sqrt(v_hat) + ADAM_EPS) + ADAM_WD * w_ref[...])
        nm_ref[...] = m
        nv_ref[...] = v
        go_ref[...] = g

    spec = pl.BlockSpec((tm, cols), lambda i: (i, 0))
    return pl.pallas_call(
        body, name=name, grid=(rows // tm,), in_specs=[spec] * 4, out_specs=[spec] * 4,
        out_shape=[jax.ShapeDtypeStruct((rows, cols), F32)] * 4, compiler_params=_params(1),
    )(w, g, m, v)


def _position():
    x, y, c = lax.axis_index("x"), lax.axis_index("y"), lax.axis_index("c")
    chips = [(1 - x, y), (x, 1 - y), (1 - x, 1 - y)]
    return x, y, c, chips


def _remote(src, dst, send_sems, recv_sems, k, to):
    return pltpu.make_async_remote_copy(src_ref=src, dst_ref=dst, send_sem=send_sems.at[k], recv_sem=recv_sems.at[k],
                                        device_id=to, device_id_type=MESH)


def _place_shard(w, name, after=None):
    rows, cols = w.shape
    tm = 256
    x, y = lax.axis_index("x"), lax.axis_index("y")

    def body(chip_ref, w_ref, o_ref):
        o_ref[...] = w_ref[...].astype(BF16)

    more_specs, more = ([ANY], [after]) if after is not None else ([], [])
    if after is not None:
        inner = body
        body = lambda chip_ref, w_ref, after_ref, o_ref: inner(chip_ref, w_ref, o_ref)
    return pl.pallas_call(
        body, name=name,
        grid_spec=pltpu.PrefetchScalarGridSpec(
            num_scalar_prefetch=1, grid=(rows // tm,),
            in_specs=[pl.BlockSpec((tm, cols), lambda i, chip: (i, 0))] + more_specs,
            out_specs=pl.BlockSpec((None, tm, cols), lambda i, chip: (chip[0], i, 0))),
        out_shape=jax.ShapeDtypeStruct((N_CHIPS, rows, cols), BF16),
        compiler_params=_params(1),
    )(jnp.reshape(2 * x + y, (1,)).astype(jnp.int32), w, *more)


def _to_bf16(x, name, after=None):
    tm = 256

    def body(x_ref, o_ref):
        o_ref[...] = x_ref[...].astype(BF16)

    spec = pl.BlockSpec((tm, x.shape[1]), lambda i: (i, 0))
    body, more_specs, more = _behind(body, 1, after)
    return pl.pallas_call(
        body, name=name, grid=(x.shape[0] // tm,), in_specs=[spec] + more_specs, out_specs=spec,
        out_shape=jax.ShapeDtypeStruct(x.shape, BF16), compiler_params=_params(1),
    )(x, *more)


HBM = pl.BlockSpec(memory_space=pltpu.HBM)
SEM = pl.BlockSpec(memory_space=pltpu.SEMAPHORE)
EFFECT = pltpu.SideEffectType.DATAFLOW_SIDE_EFFECTING


def _comm_call(name, body, bufs, sems_in, sems_out, after=None, token=False):
    nb, ns, no = len(bufs), len(sems_in), len(sems_out)
    n_in = nb + ns + (after is not None)

    def wrapped(*refs):
        body(refs[:nb], refs[nb:nb + ns], refs[n_in + nb:n_in + nb + no])
        if token:
            refs[-1][...] = jnp.zeros((8, 128), F32)

    outs = pl.pallas_call(
        wrapped, name=name,
        in_specs=[HBM] * nb + [SEM] * ns + ([ANY] if after is not None else []),
        out_specs=[HBM] * nb + [SEM] * no + ([pl.BlockSpec(memory_space=pltpu.VMEM)] if token else []),
        out_shape=[pltpu.HBM(b.shape, b.dtype) for b in bufs] + [pltpu.SemaphoreType.DMA((k,)) for k in sems_out]
        + ([jax.ShapeDtypeStruct((8, 128), F32)] if token else []),
        input_output_aliases={i: i for i in range(nb)},
        compiler_params=pltpu.CompilerParams(has_side_effects=EFFECT),
    )(*[pltpu.with_memory_space_constraint(b, pltpu.HBM) for b in bufs], *sems_in, *([after] if after is not None else []))
    return list(outs[:nb]), list(outs[nb:nb + no]), (outs[-1] if token else None)


def _ag_copies(buf, send_sems, recv_sems, k0, stage):
    x, y, c, chips = _position()
    hr = buf.shape[1] // 2
    half = lambda chip, h: buf.at[chip, pl.ds(h * hr, hr), :]
    sends, arrivals = [], []
    for j, (cx, cy) in enumerate(chips):
        if stage == "ici":
            mine = half(2 * x + y, c)
            sends.append(_remote(mine, mine, send_sems, recv_sems, k0 + j, (cx, cy, c)))
            got = half(2 * cx + cy, c)
        else:
            landed = half(2 * cx + cy, c)
            sends.append(_remote(landed, landed, send_sems, recv_sems, k0 + j, (x, y, 1 - c)))
            got = half(2 * cx + cy, 1 - c)
        arrivals.append(_remote(got, got, send_sems, recv_sems, k0 + j, (x, y, c)))
    return sends, arrivals


def _ag_start(name, groups):
    flat = [b for g in groups for b in g]

    def body(bufs, _, sems):
        at = 0
        for gi, g in enumerate(groups):
            for wi in range(len(g)):
                for cp in _ag_copies(bufs[at], sems[2 * gi], sems[2 * gi + 1], 3 * wi, "ici")[0]:
                    cp.start()
                at += 1

    bufs, sems, token = _comm_call(name, body, flat, [], [3 * len(g) for g in groups for _ in (0, 1)], token=True)
    out, at = [], 0
    for gi, g in enumerate(groups):
        out.append((bufs[at:at + len(g)], sems[2 * gi], sems[2 * gi + 1]))
        at += len(g)
    return out, token


def _ag_step(name, finish, advance, after=None):
    fin_bufs = list(finish[0]) if finish else []
    adv_bufs = list(advance[0]) if advance else []
    nf = len(fin_bufs)

    def body(bufs, sems_in, sems_out):
        if advance:
            ici_s, ici_r = sems_in[-2], sems_in[-1]
            for wi in range(len(adv_bufs)):
                buf = bufs[nf + wi]
                ici_sends, ici_arrivals = _ag_copies(buf, ici_s, ici_r, 3 * wi, "ici")
                d2d_sends, _ = _ag_copies(buf, sems_out[0], sems_out[1], 3 * wi, "d2d")
                for arrived, onward in zip(ici_arrivals, d2d_sends):
                    arrived.wait_recv()
                    onward.start()
                for cp in ici_sends:
                    cp.wait_send()
        if finish:
            for wi in range(nf):
                d2d_sends, d2d_arrivals = _ag_copies(bufs[wi], sems_in[0], sems_in[1], 3 * wi, "d2d")
                for cp in d2d_arrivals:
                    cp.wait_recv()
                for cp in d2d_sends:
                    cp.wait_send()

    sems_in = (list(finish[1:]) if finish else []) + (list(advance[1:]) if advance else [])
    bufs, sems, _ = _comm_call(name, body, fin_bufs + adv_bufs, sems_in, [3 * len(adv_bufs)] * 2 if advance else [], after)
    return bufs[:nf], ((bufs[nf:], sems[0], sems[1]) if advance else None)


def _cx_copies(src, dst, send_sems, recv_sems, k0):
    x, y, c, chips = _position()
    sends = [_remote(src.at[2 * cx + cy], dst.at[2 * x + y], send_sems, recv_sems, k0 + j, (cx, cy, c))
             for j, (cx, cy) in enumerate(chips)]
    arrivals = [_remote(dst.at[2 * cx + cy], dst.at[2 * cx + cy], send_sems, recv_sems, k0 + j, (x, y, c))
                for j, (cx, cy) in enumerate(chips)]
    return sends, arrivals


def _cx_start(name, pair_sums):
    n = len(pair_sums)
    landing = [lax.empty(p.shape, p.dtype) for p in pair_sums]

    def body(bufs, _, sems):
        for w in range(n):
            for cp in _cx_copies(bufs[w], bufs[n + w], sems[0], sems[1], 3 * w)[0]:
                cp.start()

    bufs, sems, token = _comm_call(name, body, list(pair_sums) + landing, [], [3 * n, 3 * n], token=True)
    return (bufs, sems), token


def _cx_wait(name, state, after):
    bufs, sems = state
    n = len(bufs) // 2

    def body(refs, sems_in, _):
        for w in range(n):
            sends, arrivals = _cx_copies(refs[w], refs[n + w], sems_in[0], sems_in[1], 3 * w)
            for cp in arrivals:
                cp.wait_recv()
            for cp in sends:
                cp.wait_send()

    bufs, _, _ = _comm_call(name, body, bufs, sems, [], after)
    return bufs[:n], bufs[n:]


def _px_copies(src, dst, send_sems, recv_sems, k):
    x, y, c, _ = _position()
    hr = src.shape[1] // 2
    send = _remote(src.at[:, pl.ds((1 - c) * hr, hr), :], dst, send_sems, recv_sems, k, (x, y, 1 - c))
    arrival = _remote(dst, dst, send_sems, recv_sems, k, (x, y, c))
    return send, arrival


def _px_start(name, grads):
    n = len(grads)
    landing = [lax.empty((N_CHIPS, g.shape[1] // 2, g.shape[2]), F32) for g in grads]

    def body(bufs, _, sems):
        for w in range(n):
            _px_copies(bufs[w], bufs[n + w], sems[0], sems[1], w)[0].start()

    bufs, sems, token = _comm_call(name, body, list(grads) + landing, [], [n, n], token=True)
    return (bufs, sems), token


def _px_wait(name, state, after):
    bufs, sems = state
    n = len(bufs) // 2

    def body(refs, sems_in, _):
        for w in range(n):
            send, arrival = _px_copies(refs[w], refs[n + w], sems_in[0], sems_in[1], w)
            arrival.wait_recv()
            send.wait_send()

    bufs, _, _ = _comm_call(name, body, bufs, sems, [], after)
    return bufs[:n], bufs[n:]


def _pair_sum(grad, got, name):
    _, rows, cols = grad.shape
    hr = rows // 2
    tm = min(hr, 256)
    nb = hr // tm
    c = lax.axis_index("c")

    def body(c_ref, g_ref, o_ref, out_ref):
        out_ref[...] = (g_ref[...] + o_ref[...]).astype(BF16)

    return pl.pallas_call(
        body, name=name,
        grid_spec=pltpu.PrefetchScalarGridSpec(
            num_scalar_prefetch=1, grid=(N_CHIPS, nb),
            in_specs=[pl.BlockSpec((None, tm, cols), lambda s, i, c_ref: (s, c_ref[0] * nb + i, 0)),
                      pl.BlockSpec((None, tm, cols), lambda s, i, c_ref: (s, i, 0))],
            out_specs=pl.BlockSpec((None, tm, cols), lambda s, i, c_ref: (s, i, 0))),
        out_shape=jax.ShapeDtypeStruct((N_CHIPS, hr, cols), BF16),
        compiler_params=_params(2),
    )(jnp.reshape(c, (1,)).astype(jnp.int32), grad, got)


def _chip_sum(parts, pair_sums, name):
    _, hr, cols = parts.shape
    tm = min(hr, 256)
    nb = hr // tm
    x, y, c = lax.axis_index("x"), lax.axis_index("y"), lax.axis_index("c")

    def body(pos_ref, p_ref, own_ref, o_ref):
        chip = pos_ref[0]
        own = own_ref[...].astype(F32)
        term = lambda s: jnp.where(chip == s, own, p_ref[s].astype(F32))
        o_ref[...] = ((term(0) + term(1)) + term(2)) + term(3)

    return pl.pallas_call(
        body, name=name,
        grid_spec=pltpu.PrefetchScalarGridSpec(
            num_scalar_prefetch=1, grid=(nb,),
            in_specs=[pl.BlockSpec((N_CHIPS, tm, cols), lambda i, pos: (0, i, 0)),
                      pl.BlockSpec((None, tm, cols), lambda i, pos: (pos[0], i, 0))],
            out_specs=pl.BlockSpec((tm, cols), lambda i, pos: (pos[1] * nb + i, 0))),
        out_shape=jax.ShapeDtypeStruct((2 * hr, cols), F32), compiler_params=_params(1),
    )(jnp.stack([2 * x + y, c]).astype(jnp.int32), parts, pair_sums)


def _share_halves(bufs, name):
    n = len(bufs)

    def body(*refs):
        outs = refs[n:2 * n]
        send_sems, recv_sems = refs[2 * n:]
        x, y, c, _ = _position()
        copies = []
        for w in range(n):
            hr = outs[w].shape[0] // 2
            mine = outs[w].at[pl.ds(c * hr, hr), :]
            cp = _remote(mine, mine, send_sems, recv_sems, w, (x, y, 1 - c))
            cp.start()
            copies.append(cp)
        for w in range(n):
            hr = outs[w].shape[0] // 2
            theirs = outs[w].at[pl.ds((1 - c) * hr, hr), :]
            _remote(theirs, theirs, send_sems, recv_sems, w, (x, y, c)).wait_recv()
        for cp in copies:
            cp.wait_send()

    return pl.pallas_call(
        body, name=name,
        in_specs=[ANY] * n, out_specs=[ANY] * n,
        out_shape=[jax.ShapeDtypeStruct(b.shape, b.dtype) for b in bufs],
        input_output_aliases={w: w for w in range(n)},
        scratch_shapes=[pltpu.SemaphoreType.DMA((n,)), pltpu.SemaphoreType.DMA((n,))],
    )(*bufs)


def _allreduce_small(g):
    rows = g.shape[0]

    def body(g_ref, o_ref, sib, slots, send_sems, recv_sems):
        x, y, c, chips = _position()
        me = (x, y, c)
        my_chip = 2 * x + y
        pair = _remote(g_ref, sib, send_sems, recv_sems, 0, (x, y, 1 - c))
        pair.start()
        pair.wait()
        slots[my_chip] = g_ref[...] + sib[...]
        sent = []
        for j, (cx, cy) in enumerate(chips):
            cp = _remote(slots.at[my_chip], slots.at[my_chip], send_sems, recv_sems, 1 + j, (cx, cy, c))
            cp.start()
            sent.append(cp)
        for j, (cx, cy) in enumerate(chips):
            got = slots.at[2 * cx + cy]
            _remote(got, got, send_sems, recv_sems, 1 + j, me).wait_recv()
        for cp in sent:
            cp.wait_send()
        o_ref[...] = ((slots[0] + slots[1]) + slots[2]) + slots[3]

    vm = pl.BlockSpec(memory_space=pltpu.VMEM)
    return pl.pallas_call(
        body, name="allreduce_small",
        in_specs=[vm], out_specs=vm, out_shape=jax.ShapeDtypeStruct((rows, 128), F32),
        scratch_shapes=[pltpu.VMEM((rows, 128), F32), pltpu.VMEM((N_CHIPS, rows, 128), F32),
                        pltpu.SemaphoreType.DMA((4,)), pltpu.SemaphoreType.DMA((4,))],
        compiler_params=pltpu.CompilerParams(vmem_limit_bytes=VMEM_LIMIT),
    )(g)


_SMALL =("rel_bias", "ln_v_gain", "ln_v_bias", "w_spatial", "b_spatial", "ln1_gain", "ln1_bias",
          "b_ff1", "b_ff2", "ln2_gain", "ln2_bias")
_SMALL_ROWS = 1200
_LOSS_AT = (152832 // 128, 0)


def _pack_small(parts):
    flat = jnp.concatenate([parts[k].reshape(-1).astype(F32) for k in _SMALL])
    flat = jnp.pad(flat, (0, _SMALL_ROWS * 128 - flat.shape[0]))
    return flat.reshape(_SMALL_ROWS, 128)


def _unpack_small(packed, like):
    flat = packed.reshape(-1)
    out, at = {}, 0
    for k in _SMALL:
        n = math.prod(like[k].shape)
        out[k] = flat[at:at + n].reshape(like[k].shape)
        at += n
    return out


def kernel(x, w_in, rel_bias, ln_v_gain, ln_v_bias, w_spatial, b_spatial, w_proj_a, w_proj_b, w_out, ln1_gain, ln1_bias, w_ff1, b_ff1, w_ff2, b_ff2, ln2_gain, ln2_bias, loss_target, m_w_in, m_rel_bias, m_ln_v_gain, m_ln_v_bias, m_w_spatial, m_b_spatial, m_w_proj_a, m_w_proj_b, m_w_out, m_ln1_gain, m_ln1_bias, m_w_ff1, m_b_ff1, m_w_ff2, m_b_ff2, m_ln2_gain, m_ln2_bias, v_w_in, v_rel_bias, v_ln_v_gain, v_ln_v_bias, v_w_spatial, v_b_spatial, v_w_proj_a, v_w_proj_b, v_w_out, v_ln1_gain, v_ln1_bias, v_w_ff1, v_b_ff1, v_w_ff2, v_b_ff2, v_ln2_gain, v_ln2_bias):
    args = dict(locals())
    big = ("w_in", "w_proj_a", "w_proj_b", "w_out", "w_ff1", "w_ff2")
    weights = ("w_in", "rel_bias", "ln_v_gain", "ln_v_bias", "w_spatial", "b_spatial", "w_proj_a", "w_proj_b", "w_out",
               "ln1_gain", "ln1_bias", "w_ff1", "b_ff1", "w_ff2", "b_ff2", "ln2_gain", "ln2_bias")

    xs = x[0]
    target = loss_target[0]

    (in_a,), tok = _ag_start("allgather_start_w_in", [[_place_shard(w_in[0], "place_w_in")]])
    placed = [_place_shard(args[k][0], f"place_{k}", after=tok) for k in big[1:]]
    (in_b, in_c, in_d), tok = _ag_start("allgather_start_rest", [placed[0:3], placed[3:4], placed[4:5]])
    xb = _to_bf16(xs, "x_to_bf16", after=tok)
    _, d2d_a = _ag_step("allgather_w_in_pass", None, in_a, after=xb)
    (win_g,), _ = _ag_step("allgather_w_in_done", d2d_a, None)

    proj = _proj(xb, win_g)
    _, d2d_b = _ag_step("allgather_b_pass", None, in_b, after=proj)
    ws = w_spatial[0]
    ws_t = jnp.transpose(ws, (0, 2, 1))
    bsp_b = jnp.broadcast_to(b_spatial[0][:, :, None], (NH, 128, 128))
    gmlp = _gmlp_fwd(proj, ws, bsp_b, ln_v_gain, ln_v_bias)
    attn, lse = _attention_fwd(proj, rel_bias)
    (wpa_g, wpb_g, wout_g), d2d_c = _ag_step("allgather_b_done_c_pass", d2d_b, in_c, after=attn)
    wout_full = wout_g.reshape(D, D)
    ya, yb, merged = _branch(attn, gmlp, wpa_g, wpb_g, proj)
    xhat1, rstd1, h1b = _out_ln1(merged, wout_full, xs, ln1_gain, ln1_bias)
    (w1_g,), _ = _ag_step("allgather_c_done", d2d_c, None, after=h1b)
    a, r = _ff1(h1b, w1_g, b_ff1)
    _, d2d_d = _ag_step("allgather_d_pass", None, in_d, after=a)
    (w2_g,), _ = _ag_step("allgather_d_done", d2d_d, None)
    w2_full = w2_g.reshape(DFF, D)
    dpre2, dpre2b, st2 = _ff2_ln2_loss(a, w2_full, xhat1, ln1_gain, ln1_bias, b_ff2, ln2_gain, ln2_bias, target)

    def pair_and_chip(tag, state, after):
        local, from_sibling = _px_wait(f"pair_exchange_wait_{tag}", state, after)
        pair_sums = [_pair_sum(g, o, f"pair_sum_{tag}_{i}") for i, (g, o) in enumerate(zip(local, from_sibling))]
        return _cx_start(f"chip_exchange_start_{tag}", pair_sums)

    g_w2 = _grad_w(a, dpre2b, "grad_w_ff2", 512, 2048, False)
    px, tok = _px_start("pair_exchange_start_w_ff2", [g_w2.reshape(N_CHIPS, DFF // N_CHIPS, D)])
    dprea, g_b1 = _d_ff1(dpre2b, w2_full, r, after=tok)
    cx_w2, tok = pair_and_chip("w_ff2", px, dprea)
    g_w1 = _grad_w(h1b, dprea, "grad_w_ff1", 512, 2048, True, after=tok)
    px, tok = _px_start("pair_exchange_start_w_ff1", [g_w1])
    dpre1, dpre1b, st1 = _d_h1_ln1(dprea, w1_g, dpre2, xhat1, rstd1, ln1_gain, after=tok)
    cx_w1, tok = pair_and_chip("w_ff1", px, dpre1b)
    g_wout = _grad_w(merged, dpre1b, "grad_w_out", 512, 2048, False, after=tok)
    dya, dyb, dga, dgb = _d_merged(dpre1b, wout_full, proj, ya, yb)
    g_wpa = _grad_w(attn, dya, "grad_w_proj_a", 1024, 512, True)
    g_wpb = _grad_w(gmlp, dyb, "grad_w_proj_b", 1024, 512, True)
    px, tok = _px_start("pair_exchange_start_b", [g_wpa, g_wpb, g_wout.reshape(N_CHIPS, D // N_CHIPS, D)])
    dattn, dgmlp = _d_branches(dya, dyb, wpa_g, wpb_g, after=tok)
    duv, g_ws, g_bs, stv = _gmlp_bwd(proj, dgmlp, ws, ws_t, bsp_b, ln_v_gain, ln_v_bias)
    cx_b, tok = pair_and_chip("b", px, duv)
    dq, dk, dv, ds_sums = _attention_bwd(proj, dattn, attn, lse, rel_bias, after=tok)
    g_rb = _rel_bias_grad(ds_sums)[:, :NH]

    small_g = dict(rel_bias=g_rb, ln_v_gain=stv[0], ln_v_bias=stv[1], w_spatial=g_ws, b_spatial=g_bs[:, :, 0],
                   ln1_gain=st1[0], ln1_bias=st1[1], b_ff1=g_b1, b_ff2=st2[2], ln2_gain=st2[0], ln2_bias=st2[1])
    gs = _allreduce_small(_pack_small(small_g).at[_LOSS_AT].set(st2[3, 0]))
    ds_, ms_, vs_, _ = _adamw(_pack_small({k: args[k] for k in _SMALL}), gs,
                           _pack_small({k: args["m_" + k] for k in _SMALL}),
                           _pack_small({k: args["v_" + k] for k in _SMALL}), "adamw_small")
    like = {k: args[k] for k in _SMALL}
    grads, deltas, new_m, new_v = (_unpack_small(t, like) for t in (gs, ds_, ms_, vs_))

    dproj = jnp.concatenate([dq, dk, dv, duv, dga, dgb], axis=1)
    g_win = _grad_w(xb, dproj, "grad_w_in", 512, 2304, True, after=gs)
    px, tok = _px_start("pair_exchange_start_w_in", [g_win])
    grad_x = _d_x(dproj, win_g, dpre1, after=tok)
    cx_in, tok = pair_and_chip("w_in", px, grad_x)

    def reduce_finish(tag, state, names, after):
        pair_sums, from_chips = _cx_wait(f"chip_exchange_wait_{tag}", state, after)
        halves = [_chip_sum(p, own, f"chip_sum_{k}") for p, own, k in zip(from_chips, pair_sums, names)]
        last = None
        for k, g in zip(names, _share_halves(halves, f"share_halves_{tag}")):
            d_, m_, v_, g_ = _adamw(args[k][0], g, args["m_" + k][0], args["v_" + k][0], f"adamw_{k}")
            grads[k], deltas[k], new_m[k], new_v[k] = g_[None], d_[None], m_[None], v_[None]
            last = d_
        return last

    done = reduce_finish("w_ff2", cx_w2, ["w_ff2"], tok)
    done = reduce_finish("w_ff1", cx_w1, ["w_ff1"], done)
    done = reduce_finish("b", cx_b, ["w_proj_a", "w_proj_b", "w_out"], done)
    reduce_finish("w_in", cx_in, ["w_in"], done)

    loss = gs[_LOSS_AT] * (0.5 / D)
    return (loss, grad_x[None], *[grads[k] for k in weights], *[deltas[k] for k in weights],
            *[new_m[k] for k in weights], *[new_v[k] for k in weights])
```

```python
import functools
import math

import numpy as np
import jax
import jax.numpy as jnp
from jax import lax
from jax.experimental import pallas as pl
from jax.experimental.pallas import tpu as pltpu

F32 = jnp.float32
BF16 = jnp.bfloat16

S = 2048
D = 2048
DA = 1024
DB = 1024
DFF = 8192
DIN = 9216
NH = 8
HD = 128
NBLK = 16
PATTERNS = ((128, 1), (512, 4), (2048, 16))
N_BUCKETS = 32
MAX_DISTANCE = 2048
ALPHA = 2.0 ** 0.25
LN_EPS = 1e-5
NEG_INF = -1e30
SCALE = HD ** -0.5
N_CHIPS = 4

ADAM_LR = 0.001
ADAM_B1 = 0.9
ADAM_B2 = 0.999
ADAM_EPS = 1e-08
ADAM_WD = 0.01
ADAM_STEP = 10

VMEM_LIMIT = 56 * 1024 * 1024
MESH = pl.DeviceIdType.MESH
ANY = pl.BlockSpec(memory_space=pl.ANY)


def _params(n_axes, vmem=VMEM_LIMIT):
    return pltpu.CompilerParams(dimension_semantics=("arbitrary",) * n_axes, vmem_limit_bytes=vmem)


def _bucket_tile(dilation):
    qi = np.arange(128)[:, None]
    kj = np.arange(256)[None, :]
    n = np.clip(128 + qi - kj, 0, 128) * dilation
    max_exact = N_BUCKETS // 2
    nf = np.maximum(n, 1).astype(np.float32)
    large = max_exact + (np.log(nf / np.float32(max_exact)) / np.float32(math.log(MAX_DISTANCE / max_exact))
                         * np.float32(N_BUCKETS - max_exact)).astype(np.int32)
    large = np.minimum(large, N_BUCKETS - 1)
    return np.where(n < max_exact, n, large).astype(np.int32)


def _gelu(x):
    c = math.sqrt(2.0 / math.pi)
    t = jnp.tanh(c * (x + 0.044715 * x * x * x))
    return 0.5 * x * (1.0 + t), t


def _gelu_grad(x, t):
    c = math.sqrt(2.0 / math.pi)
    return 0.5 * (1.0 + t) + 0.5 * x * (1.0 - t * t) * c * (1.0 + 3.0 * 0.044715 * x * x)


def _sigmoid(x):
    return 1.0 / (1.0 + jnp.exp(-x))


def _dot(a, b):
    return jnp.dot(a, b, preferred_element_type=F32)


def _behind(body, n_in, after):
    if after is None:
        return body, [], []
    return (lambda *refs: body(*refs[:n_in], *refs[n_in + 1:])), [ANY], [after]


def _dot_nt(a, b):
    return lax.dot_general(a, b, (((1,), (1,)), ((), ())), preferred_element_type=F32)


def _proj(xb, win_g):
    tn = 768
    per = 2304 // tn

    def body(x_ref, w_ref, o_ref):
        o_ref[...] = _dot(x_ref[...], w_ref[...])

    return pl.pallas_call(
        body, name="proj", grid=(DIN // tn,),
        in_specs=[pl.BlockSpec((S, D), lambda j: (0, 0)),
                  pl.BlockSpec((None, D, tn), lambda j: (j // per, 0, j % per))],
        out_specs=pl.BlockSpec((S, tn), lambda j: (0, j)),
        out_shape=jax.ShapeDtypeStruct((S, DIN), F32),
        compiler_params=_params(1),
    )(xb, win_g)


FWD_HEADS_PER_STEP = 4
BWD_HEADS_PER_STEP = 2


def _head_bias_tiles(rb_ref, bk_ref, bias_scr, first_head, hps):
    qi = lax.broadcasted_iota(jnp.int32, (128, 256), 0)
    kj = lax.broadcasted_iota(jnp.int32, (128, 256), 1)
    steps = 128 + qi - kj
    band = (steps >= 0) & (steps <= 128)
    bias_scr[...] = jnp.zeros_like(bias_scr)
    for p in range(len(PATTERNS)):
        bucket = bk_ref[p]

        def one_bucket(t, carry):
            hit = bucket == t
            for j in range(hps):
                bias_scr[p, j] = jnp.where(hit, rb_ref[t, first_head + j], bias_scr[p, j])
            return carry

        lax.fori_loop(0, N_BUCKETS, one_bucket, 0)
        for j in range(hps):
            bias_scr[p, j] = jnp.where(band, bias_scr[p, j], NEG_INF)


def _block_rows(b, dilation):
    nblk = NBLK // dilation
    r, n = b // nblk, b % nblk
    start = r + n * (128 * dilation)
    prev_start = jnp.maximum(start - 128 * dilation, r)
    if dilation == 1:
        return pl.ds(pl.multiple_of(start, 128), 128), pl.ds(pl.multiple_of(prev_start, 128), 128), n > 0
    return pl.ds(start, 128, stride=dilation), pl.ds(prev_start, 128, stride=dilation), n > 0


def _head_specs(first, hps):
    return [pl.BlockSpec((S, HD), lambda g, j=j: (0, first + g * hps + j)) for j in range(hps)]


def _heads_spec(hps):
    return pl.BlockSpec((S, hps * HD), lambda g: (0, g))


def _attention_fwd(proj, rel_bias):
    hps = FWD_HEADS_PER_STEP
    buckets = jnp.asarray(np.stack([_bucket_tile(d) for _, d in PATTERNS]))

    def body(rb_ref, bk_ref, *refs):
        q_refs, k_refs, v_refs = (refs[i * hps:(i + 1) * hps] for i in range(3))
        o_ref, lse_ref, bias_scr = refs[3 * hps:3 * hps + 3]
        acc_scrs, m_scrs, l_scrs = (refs[3 * hps + 3 + i * hps:3 * hps + 3 + (i + 1) * hps] for i in range(3))
        _head_bias_tiles(rb_ref, bk_ref, bias_scr, pl.program_id(0) * hps, hps)
        kj = lax.broadcasted_iota(jnp.int32, (128, 256), 1)
        for p, (_, d) in enumerate(PATTERNS):
            prev_blocks = NBLK // d > 1

            def block(b, carry):
                rows, prows, has_prev = _block_rows(b, d)
                key_ok = (kj >= 128) | has_prev
                scores = []
                for j in range(hps):
                    q = q_refs[j][rows, :].astype(BF16)
                    cur = _dot_nt(q, k_refs[j][rows, :].astype(BF16))
                    if prev_blocks:
                        cur = jnp.concatenate([_dot_nt(q, k_refs[j][prows, :].astype(BF16)), cur], axis=1)
                    scores.append(cur)
                soft = []
                for j in range(hps):
                    if prev_blocks:
                        s = jnp.where(key_ok, scores[j] * SCALE + bias_scr[p, j], NEG_INF)
                    else:
                        s = scores[j] * SCALE + bias_scr[p, j, :, 128:256]
                    m = jnp.max(s, axis=1, keepdims=True)
                    e = jnp.exp(s - m)
                    soft.append((m, jnp.sum(e, axis=1, keepdims=True), e.astype(BF16)))
                outs = []
                for j in range(hps):
                    e = soft[j][2]
                    if prev_blocks:
                        outs.append(_dot(e[:, :128], v_refs[j][prows, :].astype(BF16))
                                    + _dot(e[:, 128:], v_refs[j][rows, :].astype(BF16)))
                    else:
                        outs.append(_dot(e, v_refs[j][rows, :].astype(BF16)))
                for j in range(hps):
                    acc_scr, m_scr, l_scr = acc_scrs[j], m_scrs[j], l_scrs[j]
                    (m, den, _), o = soft[j], outs[j]
                    if p == 0:
                        acc_scr[rows, :] = o
                        m_scr[rows, :] = jnp.broadcast_to(m, (128, HD))
                        l_scr[rows, :] = jnp.broadcast_to(den, (128, HD))
                    else:
                        m_old = m_scr[rows, :]
                        m_new = jnp.maximum(m_old, m)
                        w_old, w_new = jnp.exp(m_old - m_new), jnp.exp(m - m_new)
                        acc_scr[rows, :] = acc_scr[rows, :] * w_old + o * w_new
                        l_scr[rows, :] = l_scr[rows, :] * w_old + den * w_new
                        m_scr[rows, :] = m_new
                return carry

            lax.fori_loop(0, NBLK, block, 0)
        for j in range(hps):
            cols = slice(j * HD, (j + 1) * HD)
            den = l_scrs[j][...]
            o_ref[:, cols] = (acc_scrs[j][...] / den).astype(BF16)
            lse_ref[:, cols] = m_scrs[j][...] + jnp.log(den)

    return pl.pallas_call(
        body, name="attention_fwd", grid=(NH // hps,),
        in_specs=[pl.BlockSpec(memory_space=pltpu.SMEM), pl.BlockSpec((3, 128, 256), lambda g: (0, 0, 0))]
        + _head_specs(0, hps) + _head_specs(NH, hps) + _head_specs(2 * NH, hps),
        out_specs=[_heads_spec(hps), _heads_spec(hps)],
        out_shape=[jax.ShapeDtypeStruct((S, DA), BF16), jax.ShapeDtypeStruct((S, DA), F32)],
        scratch_shapes=[pltpu.VMEM((3, hps, 128, 256), F32)] + [pltpu.VMEM((S, HD), F32)] * (3 * hps),
        compiler_params=_params(1),
    )(rel_bias, buckets, *([proj] * (3 * hps)))


def _attention_bwd(proj, dattn, attn, lse, rel_bias, after=None):
    hps = BWD_HEADS_PER_STEP

    def body(rb_ref, bk_ref, *refs):
        q_refs, k_refs, v_refs, do_refs, o_refs, lse_refs = (refs[i * hps:(i + 1) * hps] for i in range(6))
        dq_ref, dk_ref, dv_ref, ds_ref, bias_scr = refs[6 * hps:6 * hps + 5]
        dl_scrs, dq_scrs, dk_scrs, dv_scrs = (refs[6 * hps + 5 + i * hps:6 * hps + 5 + (i + 1) * hps] for i in range(4))
        _head_bias_tiles(rb_ref, bk_ref, bias_scr, pl.program_id(0) * hps, hps)
        ds_ref[...] = jnp.zeros_like(ds_ref)
        for j in range(hps):
            dq_scrs[j][...] = jnp.zeros((S, HD), F32)
            dk_scrs[j][...] = jnp.zeros((S, HD), F32)
            dv_scrs[j][...] = jnp.zeros((S, HD), F32)
            prod = do_refs[j][...] * o_refs[j][...].astype(F32)
            dl_scrs[j][...] = jnp.broadcast_to(jnp.sum(prod, axis=1, keepdims=True), (S, HD))
        for p, (_, d) in enumerate(PATTERNS):
            prev_blocks = NBLK // d > 1

            def block(b, carry):
                rows, prows, has_prev = _block_rows(b, d)
                ops, raw = [], []
                for j in range(hps):
                    q, do = q_refs[j][rows, :].astype(BF16), do_refs[j][rows, :].astype(BF16)
                    kc, vc = k_refs[j][rows, :].astype(BF16), v_refs[j][rows, :].astype(BF16)
                    if prev_blocks:
                        kp, vp = k_refs[j][prows, :].astype(BF16), v_refs[j][prows, :].astype(BF16)
                        ops.append((q, do, kc, kp))
                        raw.append((_dot_nt(q, kc), _dot_nt(do, vc), _dot_nt(q, kp), _dot_nt(do, vp)))
                    else:
                        ops.append((q, do, kc))
                        raw.append((_dot_nt(q, kc), _dot_nt(do, vc)))
                probs = []
                for j in range(hps):
                    lse_b, dl_b = lse_refs[j][rows, :], dl_scrs[j][rows, :]
                    p_c = jnp.exp(raw[j][0] * SCALE + bias_scr[p, j, :, 128:256] - lse_b)
                    ds_c = p_c * (raw[j][1] - dl_b)
                    ds_ref[p, j, :, 128:256] += ds_c
                    if prev_blocks:
                        p_p = jnp.where(has_prev, jnp.exp(raw[j][2] * SCALE + bias_scr[p, j, :, 0:128] - lse_b), 0.0)
                        ds_p = p_p * (raw[j][3] - dl_b)
                        ds_ref[p, j, :, 0:128] += ds_p
                        probs.append((p_c, ds_c, p_p, ds_p))
                    else:
                        probs.append((p_c, ds_c))
                grads = []
                for j in range(hps):
                    q, do, kc = ops[j][:3]
                    p_c, ds_c = probs[j][:2]
                    dq = _dot(ds_c.astype(BF16), kc)
                    cur = (_dot(ds_c.T.astype(BF16), q) * SCALE, _dot(p_c.T.astype(BF16), do))
                    if prev_blocks:
                        p_p, ds_p = probs[j][2:]
                        dq = dq + _dot(ds_p.astype(BF16), ops[j][3])
                        cur = cur + (_dot(ds_p.T.astype(BF16), q) * SCALE, _dot(p_p.T.astype(BF16), do))
                    grads.append((dq * SCALE,) + cur)
                for j in range(hps):
                    dq_scrs[j][rows, :] += grads[j][0]
                    dk_scrs[j][rows, :] += grads[j][1]
                    dv_scrs[j][rows, :] += grads[j][2]
                    if prev_blocks:
                        dk_scrs[j][prows, :] += grads[j][3]
                        dv_scrs[j][prows, :] += grads[j][4]
                return carry

            lax.fori_loop(0, NBLK, block, 0)
        for j in range(hps):
            cols = slice(j * HD, (j + 1) * HD)
            dq_ref[:, cols] = dq_scrs[j][...].astype(BF16)
            dk_ref[:, cols] = dk_scrs[j][...].astype(BF16)
            dv_ref[:, cols] = dv_scrs[j][...].astype(BF16)

    buckets = jnp.asarray(np.stack([_bucket_tile(d) for _, d in PATTERNS]))
    body, more_specs, more = _behind(body, 2 + 6 * hps, after)
    return pl.pallas_call(
        body, name="attention_bwd", grid=(NH // hps,),
        in_specs=[pl.BlockSpec(memory_space=pltpu.SMEM), pl.BlockSpec((3, 128, 256), lambda g: (0, 0, 0))]
        + _head_specs(0, hps) + _head_specs(NH, hps) + _head_specs(2 * NH, hps) + 3 * _head_specs(0, hps)
        + more_specs,
        out_specs=3 * [_heads_spec(hps)] + [pl.BlockSpec((3, hps, 128, 256), lambda g: (0, g, 0, 0))],
        out_shape=[jax.ShapeDtypeStruct((S, DA), BF16)] * 3 + [jax.ShapeDtypeStruct((3, NH, 128, 256), F32)],
        scratch_shapes=[pltpu.VMEM((3, hps, 128, 256), F32)] + [pltpu.VMEM((S, HD), F32)] * (4 * hps),
        compiler_params=_params(1),
    )(rel_bias, buckets, *([proj] * (3 * hps)), *([dattn] * hps), *([attn] * hps), *([lse] * hps), *more)


def _gmlp_parts(u_ref, vb_ref, g_ref, be_ref):
    u = u_ref[...]
    u_act, tu = _gelu(u)
    vb = vb_ref[...]
    gv, tv = _gelu(vb)
    mean = jnp.mean(gv, axis=1, keepdims=True)
    cen = gv - mean
    var = jnp.mean(cen * cen, axis=1, keepdims=True)
    rstd = lax.rsqrt(var + LN_EPS)
    xhat = cen * rstd
    vn = xhat * g_ref[...] + be_ref[...]
    return u, tu, u_act, vb, tv, rstd, xhat, vn


def _gmlp_fwd(proj, ws, bsp_b, gain_v, bias_v):
    def body(u_ref, vb_ref, ws_ref, bsp_ref, g_ref, be_ref, o_ref):
        _, _, u_act, _, _, _, _, vn = _gmlp_parts(u_ref, vb_ref, g_ref, be_ref)
        row = lax.broadcasted_iota(jnp.int32, (128, 128), 0)
        col = lax.broadcasted_iota(jnp.int32, (128, 128), 1)
        causal = row >= col
        for g in range(NH):
            cols = slice(g * 128, (g + 1) * 128)
            wsg = jnp.where(causal, ws_ref[g], 0.0).astype(BF16)
            z = _dot(wsg, vn[:, cols].astype(BF16)) + bsp_ref[g]
            o_ref[:, cols] = (u_act[:, cols] * z).astype(BF16)

    return pl.pallas_call(
        body, name="gmlp_fwd", grid=(NBLK,),
        in_specs=[pl.BlockSpec((128, DB), lambda c: (c, 3)), pl.BlockSpec((128, DB), lambda c: (c, 4)),
                  pl.BlockSpec((NH, 128, 128), lambda c: (0, 0, 0)), pl.BlockSpec((NH, 128, 128), lambda c: (0, 0, 0)),
                  pl.BlockSpec((1, DB), lambda c: (0, 0)), pl.BlockSpec((1, DB), lambda c: (0, 0))],
        out_specs=pl.BlockSpec((128, DB), lambda c: (c, 0)),
        out_shape=jax.ShapeDtypeStruct((S, DB), BF16),
        compiler_params=_params(1),
    )(proj, proj, ws, bsp_b, gain_v, bias_v)


def _branch(attn, gmlp, wpa_g, wpb_g, proj):
    tn = 512

    def body(a_ref, g_ref, wa_ref, wb_ref, ga_ref, gb_ref, ya_ref, yb_ref, mg_ref):
        ya = _dot(a_ref[...], wa_ref[...])
        yb = _dot(g_ref[...], wb_ref[...])
        ya_ref[...] = ya.astype(BF16)
        yb_ref[...] = yb.astype(BF16)
        mg_ref[...] = (_sigmoid(ga_ref[...]) * ya + _sigmoid(gb_ref[...]) * yb).astype(BF16)

    out = pl.BlockSpec((S, tn), lambda j: (0, j))
    return pl.pallas_call(
        body, name="branch", grid=(D // tn,),
        in_specs=[pl.BlockSpec((S, DA), lambda j: (0, 0)), pl.BlockSpec((S, DB), lambda j: (0, 0)),
                  pl.BlockSpec((None, DA, tn), lambda j: (j, 0, 0)), pl.BlockSpec((None, DB, tn), lambda j: (j, 0, 0)),
                  pl.BlockSpec((S, tn), lambda j: (0, 5120 // tn + j)), pl.BlockSpec((S, tn), lambda j: (0, 7168 // tn + j))],
        out_specs=[out, out, out],
        out_shape=[jax.ShapeDtypeStruct((S, D), BF16)] * 3,
        compiler_params=_params(1),
    )(attn, gmlp, wpa_g, wpb_g, proj, proj)


def _out_ln1(merged, wout_g, x, gain, bias):
    tm = 256

    def body(m_ref, w_ref, x_ref, g_ref, b_ref, xh_ref, rs_ref, h_ref):
        pre = ALPHA * x_ref[...] + _dot(m_ref[...], w_ref[...])
        mean = jnp.mean(pre, axis=1, keepdims=True)
        cen = pre - mean
        var = jnp.mean(cen * cen, axis=1, keepdims=True)
        rstd = lax.rsqrt(var + LN_EPS)
        xhat = cen * rstd
        xh_ref[...] = xhat
        rs_ref[...] = jnp.broadcast_to(rstd, (tm, 128))
        h_ref[...] = (xhat * g_ref[...] + b_ref[...]).astype(BF16)

    row = pl.BlockSpec((tm, D), lambda i: (i, 0))
    vec = pl.BlockSpec((1, D), lambda i: (0, 0))
    return pl.pallas_call(
        body, name="out_ln1", grid=(S // tm,),
        in_specs=[row, pl.BlockSpec((D, D), lambda i: (0, 0)), row, vec, vec],
        out_specs=[row, pl.BlockSpec((tm, 128), lambda i: (i, 0)), row],
        out_shape=[jax.ShapeDtypeStruct((S, D), F32), jax.ShapeDtypeStruct((S, 128), F32),
                   jax.ShapeDtypeStruct((S, D), BF16)],
        compiler_params=_params(1),
    )(merged, wout_g, x, gain, bias)


def _ff1(h1b, w1_g, b1):
    tn = 512
    per = D // tn

    def body(h_ref, w_ref, b_ref, a_ref, r_ref):
        r = jnp.maximum(_dot(h_ref[...], w_ref[...]) + b_ref[...], 0.0)
        r_ref[...] = r.astype(BF16)
        a_ref[...] = (r * r).astype(BF16)

    out = pl.BlockSpec((S, tn), lambda j: (0, j))
    return pl.pallas_call(
        body, name="ff1", grid=(DFF // tn,),
        in_specs=[pl.BlockSpec((S, D), lambda j: (0, 0)),
                  pl.BlockSpec((None, D, tn), lambda j: (j // per, 0, j % per)),
                  pl.BlockSpec((1, tn), lambda j: (0, j))],
        out_specs=[out, out],
        out_shape=[jax.ShapeDtypeStruct((S, DFF), BF16)] * 2,
        compiler_params=_params(1),
    )(h1b, w1_g, b1)


def _ff2_ln2_loss(a, w2_g, xhat1, g1, b1, b2, g2, be2, target):
    tm, tk = 512, 1024
    nk = DFF // tk

    def body(a_ref, w_ref, xh_ref, g1_ref, b1_ref, b2_ref, g2_ref, be2_ref, t_ref, d_ref, db_ref, st_ref, acc):
        i, k = pl.program_id(0), pl.program_id(1)

        @pl.when(k == 0)
        def _():
            acc[...] = jnp.zeros_like(acc)

        @pl.when((i == 0) & (k == 0))
        def _():
            st_ref[...] = jnp.zeros_like(st_ref)

        acc[...] += _dot(a_ref[...], w_ref[...])

        @pl.when(k == nk - 1)
        def _():
            def rows_chunk(ci, carry):
                rows = pl.ds(pl.multiple_of(ci * 128, 128), 128)
                h1 = xh_ref[rows, :] * g1_ref[...] + b1_ref[...]
                pre = ALPHA * h1 + acc[rows, :] + b2_ref[...]
                mean = jnp.mean(pre, axis=1, keepdims=True)
                cen = pre - mean
                var = jnp.mean(cen * cen, axis=1, keepdims=True)
                rstd = lax.rsqrt(var + LN_EPS)
                xhat = cen * rstd
                y = xhat * g2_ref[...] + be2_ref[...]
                err = y - t_ref[rows, :]
                dy = err * (1.0 / D)
                g = dy * g2_ref[...]
                dpre = rstd * (g - jnp.mean(g, axis=1, keepdims=True)
                               - xhat * jnp.mean(g * xhat, axis=1, keepdims=True))
                d_ref[rows, :] = dpre
                db_ref[rows, :] = dpre.astype(BF16)
                st_ref[0:1, :] += jnp.sum(dy * xhat, axis=0, keepdims=True)
                st_ref[1:2, :] += jnp.sum(dy, axis=0, keepdims=True)
                st_ref[2:3, :] += jnp.sum(dpre, axis=0, keepdims=True)
                st_ref[3:4, :] += jnp.broadcast_to(jnp.sum(err * err).reshape(1, 1), (1, D))
                return carry

            lax.fori_loop(0, tm // 128, rows_chunk, 0)

    row = pl.BlockSpec((tm, D), lambda i, k: (i, 0))
    vec = pl.BlockSpec((1, D), lambda i, k: (0, 0))
    return pl.pallas_call(
        body, name="ff2_ln2_loss", grid=(S // tm, nk),
        in_specs=[pl.BlockSpec((tm, tk), lambda i, k: (i, k)), pl.BlockSpec((tk, D), lambda i, k: (k, 0)),
                  row, vec, vec, vec, vec, vec, row],
        out_specs=[row, row, pl.BlockSpec((8, D), lambda i, k: (0, 0))],
        out_shape=[jax.ShapeDtypeStruct((S, D), F32), jax.ShapeDtypeStruct((S, D), BF16),
                   jax.ShapeDtypeStruct((8, D), F32)],
        scratch_shapes=[pltpu.VMEM((tm, D), F32)],
        compiler_params=_params(2),
    )(a, w2_g, xhat1, g1, b1, b2, g2, be2, target)


def _grad_w(act, dout, name, ti, tj, sharded, after=None):
    m, n = act.shape[1], dout.shape[1]
    ns = n // N_CHIPS
    per = ns // tj if sharded else None

    def body(a_ref, b_ref, o_ref, at_scr):
        @pl.when(pl.program_id(1) == 0)
        def _():
            at_scr[...] = a_ref[...].T

        o_ref[...] = _dot(at_scr[...], b_ref[...])

    if sharded:
        out_spec = pl.BlockSpec((None, ti, tj), lambda i, j: (j // per, i, j % per))
        out_shape = jax.ShapeDtypeStruct((N_CHIPS, m, ns), F32)
    else:
        out_spec = pl.BlockSpec((ti, tj), lambda i, j: (i, j))
        out_shape = jax.ShapeDtypeStruct((m, n), F32)
    body, more_specs, more = _behind(body, 2, after)
    return pl.pallas_call(
        body, name=name, grid=(m // ti, n // tj),
        in_specs=[pl.BlockSpec((S, ti), lambda i, j: (0, i)), pl.BlockSpec((S, tj), lambda i, j: (0, j))] + more_specs,
        out_specs=out_spec, out_shape=out_shape,
        scratch_shapes=[pltpu.VMEM((ti, S), BF16)],
        compiler_params=_params(2),
    )(act, dout, *more)


def _d_ff1(dpre2b, w2_g, r, after=None):
    tn = 512

    def body(d_ref, w_ref, r_ref, o_ref, gb_ref):
        da = _dot_nt(d_ref[...], w_ref[...])
        dp = da * (2.0 * r_ref[...].astype(F32))
        o_ref[...] = dp.astype(BF16)
        gb_ref[...] = jnp.sum(dp, axis=0, keepdims=True)

    body, more_specs, more = _behind(body, 3, after)
    return pl.pallas_call(
        body, name="d_ff1", grid=(DFF // tn,),
        in_specs=[pl.BlockSpec((S, D), lambda j: (0, 0)), pl.BlockSpec((tn, D), lambda j: (j, 0)),
                  pl.BlockSpec((S, tn), lambda j: (0, j))] + more_specs,
        out_specs=[pl.BlockSpec((S, tn), lambda j: (0, j)), pl.BlockSpec((1, tn), lambda j: (0, j))],
        out_shape=[jax.ShapeDtypeStruct((S, DFF), BF16), jax.ShapeDtypeStruct((1, DFF), F32)],
        compiler_params=_params(1),
    )(dpre2b, w2_g, r, *more)


def _d_h1_ln1(dprea, w1_g, dpre2, xhat1, rstd1, g1, after=None):
    tm, tk = 512, 1024
    per = D // tk
    nk = DFF // tk

    def body(a_ref, w_ref, d2_ref, xh_ref, rs_ref, g_ref, d_ref, db_ref, st_ref, acc):
        i, k = pl.program_id(0), pl.program_id(1)

        @pl.when(k == 0)
        def _():
            acc[...] = jnp.zeros_like(acc)

        @pl.when((i == 0) & (k == 0))
        def _():
            st_ref[...] = jnp.zeros_like(st_ref)

        acc[...] += _dot_nt(a_ref[...], w_ref[...])

        @pl.when(k == nk - 1)
        def _():
            def rows_chunk(ci, carry):
                rows = pl.ds(pl.multiple_of(ci * 128, 128), 128)
                dh = ALPHA * d2_ref[rows, :] + acc[rows, :]
                xhat = xh_ref[rows, :]
                g = dh * g_ref[...]
                dpre = rs_ref[rows, 0:1] * (g - jnp.mean(g, axis=1, keepdims=True)
                                            - xhat * jnp.mean(g * xhat, axis=1, keepdims=True))
                d_ref[rows, :] = dpre
                db_ref[rows, :] = dpre.astype(BF16)
                st_ref[0:1, :] += jnp.sum(dh * xhat, axis=0, keepdims=True)
                st_ref[1:2, :] += jnp.sum(dh, axis=0, keepdims=True)
                return carry

            lax.fori_loop(0, tm // 128, rows_chunk, 0)

    row = pl.BlockSpec((tm, D), lambda i, k: (i, 0))
    body, more_specs, more = _behind(body, 6, after)
    return pl.pallas_call(
        body, name="d_h1_ln1", grid=(S // tm, nk),
        in_specs=[pl.BlockSpec((tm, tk), lambda i, k: (i, k)),
                  pl.BlockSpec((None, D, tk), lambda i, k: (k // per, 0, k % per)),
                  row, row, pl.BlockSpec((tm, 128), lambda i, k: (i, 0)), pl.BlockSpec((1, D), lambda i, k: (0, 0))]
        + more_specs,
        out_specs=[row, row, pl.BlockSpec((8, D), lambda i, k: (0, 0))],
        out_shape=[jax.ShapeDtypeStruct((S, D), F32), jax.ShapeDtypeStruct((S, D), BF16),
                   jax.ShapeDtypeStruct((8, D), F32)],
        scratch_shapes=[pltpu.VMEM((tm, D), F32)],
        compiler_params=_params(2),
    )(dprea, w1_g, dpre2, xhat1, rstd1, g1, *more)


def _d_merged(dpre1b, wout_g, proj, ya, yb):
    tm, tn = 512, 1024

    def body(d_ref, w_ref, ga_ref, gb_ref, ya_ref, yb_ref, dya_ref, dyb_ref, dga_ref, dgb_ref):
        dm = _dot_nt(d_ref[...], w_ref[...])
        sa = _sigmoid(ga_ref[...])
        sb = _sigmoid(gb_ref[...])
        dya_ref[...] = (dm * sa).astype(BF16)
        dyb_ref[...] = (dm * sb).astype(BF16)
        dga_ref[...] = (dm * ya_ref[...].astype(F32) * sa * (1.0 - sa)).astype(BF16)
        dgb_ref[...] = (dm * yb_ref[...].astype(F32) * sb * (1.0 - sb)).astype(BF16)

    tile = pl.BlockSpec((tm, tn), lambda i, j: (i, j))
    return pl.pallas_call(
        body, name="d_merged", grid=(S // tm, D // tn),
        in_specs=[pl.BlockSpec((tm, D), lambda i, j: (i, 0)), pl.BlockSpec((tn, D), lambda i, j: (j, 0)),
                  pl.BlockSpec((tm, tn), lambda i, j: (i, 5 + j)), pl.BlockSpec((tm, tn), lambda i, j: (i, 7 + j)),
                  tile, tile],
        out_specs=[tile] * 4,
        out_shape=[jax.ShapeDtypeStruct((S, D), BF16)] * 4,
        compiler_params=_params(2),
    )(dpre1b, wout_g, proj, proj, ya, yb)


def _d_branches(dya, dyb, wpa_g, wpb_g, after=None):
    tk = 512

    def body(da_ref, db_ref, wa_ref, wb_ref, oa_ref, ob_ref):
        @pl.when(pl.program_id(0) == 0)
        def _():
            oa_ref[...] = jnp.zeros_like(oa_ref)
            ob_ref[...] = jnp.zeros_like(ob_ref)

        oa_ref[...] += _dot_nt(da_ref[...], wa_ref[...])
        ob_ref[...] += _dot_nt(db_ref[...], wb_ref[...])

    body, more_specs, more = _behind(body, 4, after)
    return pl.pallas_call(
        body, name="d_branches", grid=(D // tk,),
        in_specs=[pl.BlockSpec((S, tk), lambda k: (0, k)), pl.BlockSpec((S, tk), lambda k: (0, k)),
                  pl.BlockSpec((None, DA, tk), lambda k: (k, 0, 0)), pl.BlockSpec((None, DB, tk), lambda k: (k, 0, 0))]
        + more_specs,
        out_specs=[pl.BlockSpec((S, DA), lambda k: (0, 0)), pl.BlockSpec((S, DB), lambda k: (0, 0))],
        out_shape=[jax.ShapeDtypeStruct((S, DA), F32), jax.ShapeDtypeStruct((S, DB), F32)],
        compiler_params=_params(1),
    )(dya, dyb, wpa_g, wpb_g, *more)


def _gmlp_bwd(proj, dgmlp, ws, ws_t, bsp_b, gain_v, bias_v):
    def body(u_ref, vb_ref, dg_ref, ws_ref, wst_ref, bsp_ref, g_ref, be_ref, duv_ref, gws_ref, gbs_ref, st_ref):
        @pl.when(pl.program_id(0) == 0)
        def _():
            gws_ref[...] = jnp.zeros_like(gws_ref)
            gbs_ref[...] = jnp.zeros_like(gbs_ref)
            st_ref[...] = jnp.zeros_like(st_ref)

        u, tu, u_act, vb, tv, rstd, xhat, vn = _gmlp_parts(u_ref, vb_ref, g_ref, be_ref)
        dg = dg_ref[...]
        dz = dg * u_act
        row = lax.broadcasted_iota(jnp.int32, (128, 128), 0)
        col = lax.broadcasted_iota(jnp.int32, (128, 128), 1)
        causal = row >= col
        causal_t = row <= col
        dvn_parts = []
        z_parts = []
        for g in range(NH):
            cols = slice(g * 128, (g + 1) * 128)
            vng = vn[:, cols].astype(BF16)
            dzg = dz[:, cols]
            dzb = dzg.astype(BF16)
            wsg = jnp.where(causal, ws_ref[g], 0.0).astype(BF16)
            wsg_t = jnp.where(causal_t, wst_ref[g], 0.0).astype(BF16)
            z_parts.append(_dot(wsg, vng) + bsp_ref[g])
            gws_ref[g] += jnp.where(causal, _dot_nt(dzb, vng), 0.0)
            gbs_ref[g] += jnp.broadcast_to(jnp.sum(dzg, axis=1, keepdims=True), (128, 128))
            dvn_parts.append(_dot(wsg_t, dzb))
        z = jnp.concatenate(z_parts, axis=1)
        dvn = jnp.concatenate(dvn_parts, axis=1)
        du = dg * z * _gelu_grad(u, tu)
        st_ref[0:1, :] += jnp.sum(dvn * xhat, axis=0, keepdims=True)
        st_ref[1:2, :] += jnp.sum(dvn, axis=0, keepdims=True)
        gg = dvn * g_ref[...]
        dgv = rstd * (gg - jnp.mean(gg, axis=1, keepdims=True) - xhat * jnp.mean(gg * xhat, axis=1, keepdims=True))
        dvb = dgv * _gelu_grad(vb, tv)
        duv_ref[:, 0:DB] = du.astype(BF16)
        duv_ref[:, DB:2 * DB] = dvb.astype(BF16)

    full3 = pl.BlockSpec((NH, 128, 128), lambda c: (0, 0, 0))
    vec = pl.BlockSpec((1, DB), lambda c: (0, 0))
    return pl.pallas_call(
        body, name="gmlp_bwd", grid=(NBLK,),
        in_specs=[pl.BlockSpec((128, DB), lambda c: (c, 3)), pl.BlockSpec((128, DB), lambda c: (c, 4)),
                  pl.BlockSpec((128, DB), lambda c: (c, 0)), full3, full3, full3, vec, vec],
        out_specs=[pl.BlockSpec((128, 2 * DB), lambda c: (c, 0)), full3, full3, pl.BlockSpec((8, DB), lambda c: (0, 0))],
        out_shape=[jax.ShapeDtypeStruct((S, 2 * DB), BF16), jax.ShapeDtypeStruct((NH, 128, 128), F32),
                   jax.ShapeDtypeStruct((NH, 128, 128), F32), jax.ShapeDtypeStruct((8, DB), F32)],
        compiler_params=_params(1),
    )(proj, proj, dgmlp, ws, ws_t, bsp_b, gain_v, bias_v)


def _rel_bias_grad(ds_sums):
    buckets = jnp.asarray(np.stack([_bucket_tile(d) for _, d in PATTERNS]))

    def body(bk_ref, ds_ref, o_ref):
        row = lax.broadcasted_iota(jnp.int32, (N_BUCKETS, 128), 0)
        lane = lax.broadcasted_iota(jnp.int32, (N_BUCKETS, 128), 1)

        def one_bucket(t, out):
            hits = [bk_ref[p] == t for p in range(3)]
            for h in range(NH):
                tot = jnp.zeros((128, 256), F32)
                for p in range(3):
                    tot = tot + jnp.where(hits[p], ds_ref[p, h], 0.0)
                out = jnp.where((row == t) & (lane == h), jnp.sum(tot), out)
            return out

        o_ref[...] = lax.fori_loop(0, N_BUCKETS, one_bucket, jnp.zeros((N_BUCKETS, 128), F32))

    return pl.pallas_call(
        body, name="rel_bias_grad",
        in_specs=[pl.BlockSpec(memory_space=pltpu.VMEM)] * 2, out_specs=pl.BlockSpec(memory_space=pltpu.VMEM),
        out_shape=jax.ShapeDtypeStruct((N_BUCKETS, 128), F32),
        compiler_params=pltpu.CompilerParams(vmem_limit_bytes=VMEM_LIMIT),
    )(buckets, ds_sums)


def _d_x(dproj, win_g, dpre1, after=None):
    tm, tk = 512, 2304
    per = 2304 // tk
    nk = DIN // tk

    def body(a_ref, w_ref, d_ref, o_ref, acc):
        k = pl.program_id(1)

        @pl.when(k == 0)
        def _():
            acc[...] = ALPHA * d_ref[...]

        acc[...] += _dot_nt(a_ref[...], w_ref[...])

        @pl.when(k == nk - 1)
        def _():
            o_ref[...] = acc[...]

    row = pl.BlockSpec((tm, D), lambda i, k: (i, 0))
    body, more_specs, more = _behind(body, 3, after)
    return pl.pallas_call(
        body, name="d_x", grid=(S // tm, nk),
        in_specs=[pl.BlockSpec((tm, tk), lambda i, k: (i, k)),
                  pl.BlockSpec((None, D, tk), lambda i, k: (k // per, 0, k % per)), row] + more_specs,
        out_specs=row, out_shape=jax.ShapeDtypeStruct((S, D), F32),
        scratch_shapes=[pltpu.VMEM((tm, D), F32)],
        compiler_params=_params(2),
    )(dproj, win_g, dpre1, *more)


def _adamw(w, g, m, v, name):
    rows, cols = w.shape
    tm = max(t for t in range(8, 257, 8) if rows % t == 0)

    def body(w_ref, g_ref, m_ref, v_ref, d_ref, nm_ref, nv_ref, go_ref):
        g = g_ref[...]
        m = ADAM_B1 * m_ref[...] + (1.0 - ADAM_B1) * g
        v = ADAM_B2 * v_ref[...] + (1.0 - ADAM_B2) * (g * g)
        m_hat = m / (1.0 - ADAM_B1 ** ADAM_STEP)
        v_hat = v / (1.0 - ADAM_B2 ** ADAM_STEP)
        d_ref[...] = -ADAM_LR * (m_hat / (jnp.sqrt(v_hat) + ADAM_EPS) + ADAM_WD * w_ref[...])
        nm_ref[...] = m
        nv_ref[...] = v
        go_ref[...] = g

    spec = pl.BlockSpec((tm, cols), lambda i: (i, 0))
    return pl.pallas_call(
        body, name=name, grid=(rows // tm,), in_specs=[spec] * 4, out_specs=[spec] * 4,
        out_shape=[jax.ShapeDtypeStruct((rows, cols), F32)] * 4, compiler_params=_params(1),
    )(w, g, m, v)


def _position():
    x, y, c = lax.axis_index("x"), lax.axis_index("y"), lax.axis_index("c")
    chips = [(1 - x, y), (x, 1 - y), (1 - x, 1 - y)]
    return x, y, c, chips


def _remote(src, dst, send_sems, recv_sems, k, to):
    return pltpu.make_async_remote_copy(src_ref=src, dst_ref=dst, send_sem=send_sems.at[k], recv_sem=recv_sems.at[k],
                                        device_id=to, device_id_type=MESH)


def _place_shard(w, name, after=None):
    rows, cols = w.shape
    tm = 256
    x, y = lax.axis_index("x"), lax.axis_index("y")

    def body(chip_ref, w_ref, o_ref):
        o_ref[...] = w_ref[...].astype(BF16)

    more_specs, more = ([ANY], [after]) if after is not None else ([], [])
    if after is not None:
        inner = body
        body = lambda chip_ref, w_ref, after_ref, o_ref: inner(chip_ref, w_ref, o_ref)
    return pl.pallas_call(
        body, name=name,
        grid_spec=pltpu.PrefetchScalarGridSpec(
            num_scalar_prefetch=1, grid=(rows // tm,),
            in_specs=[pl.BlockSpec((tm, cols), lambda i, chip: (i, 0))] + more_specs,
            out_specs=pl.BlockSpec((None, tm, cols), lambda i, chip: (chip[0], i, 0))),
        out_shape=jax.ShapeDtypeStruct((N_CHIPS, rows, cols), BF16),
        compiler_params=_params(1),
    )(jnp.reshape(2 * x + y, (1,)).astype(jnp.int32), w, *more)


def _to_bf16(x, name, after=None):
    tm = 256

    def body(x_ref, o_ref):
        o_ref[...] = x_ref[...].astype(BF16)

    spec = pl.BlockSpec((tm, x.shape[1]), lambda i: (i, 0))
    body, more_specs, more = _behind(body, 1, after)
    return pl.pallas_call(
        body, name=name, grid=(x.shape[0] // tm,), in_specs=[spec] + more_specs, out_specs=spec,
        out_shape=jax.ShapeDtypeStruct(x.shape, BF16), compiler_params=_params(1),
    )(x, *more)


HBM = pl.BlockSpec(memory_space=pltpu.HBM)
SEM = pl.BlockSpec(memory_space=pltpu.SEMAPHORE)
EFFECT = pltpu.SideEffectType.DATAFLOW_SIDE_EFFECTING


def _comm_call(name, body, bufs, sems_in, sems_out, after=None, token=False):
    nb, ns, no = len(bufs), len(sems_in), len(sems_out)
    n_in = nb + ns + (after is not None)

    def wrapped(*refs):
        body(refs[:nb], refs[nb:nb + ns], refs[n_in + nb:n_in + nb + no])
        if token:
            refs[-1][...] = jnp.zeros((8, 128), F32)

    outs = pl.pallas_call(
        wrapped, name=name,
        in_specs=[HBM] * nb + [SEM] * ns + ([ANY] if after is not None else []),
        out_specs=[HBM] * nb + [SEM] * no + ([pl.BlockSpec(memory_space=pltpu.VMEM)] if token else []),
        out_shape=[pltpu.HBM(b.shape, b.dtype) for b in bufs] + [pltpu.SemaphoreType.DMA((k,)) for k in sems_out]
        + ([jax.ShapeDtypeStruct((8, 128), F32)] if token else []),
        input_output_aliases={i: i for i in range(nb)},
        compiler_params=pltpu.CompilerParams(has_side_effects=EFFECT),
    )(*[pltpu.with_memory_space_constraint(b, pltpu.HBM) for b in bufs], *sems_in, *([after] if after is not None else []))
    return list(outs[:nb]), list(outs[nb:nb + no]), (outs[-1] if token else None)


def _ag_copies(buf, send_sems, recv_sems, k0, stage):
    x, y, c, chips = _position()
    hr = buf.shape[1] // 2
    half = lambda chip, h: buf.at[chip, pl.ds(h * hr, hr), :]
    sends, arrivals = [], []
    for j, (cx, cy) in enumerate(chips):
        if stage == "ici":
            mine = half(2 * x + y, c)
            sends.append(_remote(mine, mine, send_sems, recv_sems, k0 + j, (cx, cy, c)))
            got = half(2 * cx + cy, c)
        else:
            landed = half(2 * cx + cy, c)
            sends.append(_remote(landed, landed, send_sems, recv_sems, k0 + j, (x, y, 1 - c)))
            got = half(2 * cx + cy, 1 - c)
        arrivals.append(_remote(got, got, send_sems, recv_sems, k0 + j, (x, y, c)))
    return sends, arrivals


def _ag_start(name, groups):
    flat = [b for g in groups for b in g]

    def body(bufs, _, sems):
        at = 0
        for gi, g in enumerate(groups):
            for wi in range(len(g)):
                for cp in _ag_copies(bufs[at], sems[2 * gi], sems[2 * gi + 1], 3 * wi, "ici")[0]:
                    cp.start()
                at += 1

    bufs, sems, token = _comm_call(name, body, flat, [], [3 * len(g) for g in groups for _ in (0, 1)], token=True)
    out, at = [], 0
    for gi, g in enumerate(groups):
        out.append((bufs[at:at + len(g)], sems[2 * gi], sems[2 * gi + 1]))
        at += len(g)
    return out, token


def _ag_step(name, finish, advance, after=None):
    fin_bufs = list(finish[0]) if finish else []
    adv_bufs = list(advance[0]) if advance else []
    nf = len(fin_bufs)

    def body(bufs, sems_in, sems_out):
        if advance:
            ici_s, ici_r = sems_in[-2], sems_in[-1]
            for wi in range(len(adv_bufs)):
                buf = bufs[nf + wi]
                ici_sends, ici_arrivals = _ag_copies(buf, ici_s, ici_r, 3 * wi, "ici")
                d2d_sends, _ = _ag_copies(buf, sems_out[0], sems_out[1], 3 * wi, "d2d")
                for arrived, onward in zip(ici_arrivals, d2d_sends):
                    arrived.wait_recv()
                    onward.start()
                for cp in ici_sends:
                    cp.wait_send()
        if finish:
            for wi in range(nf):
                d2d_sends, d2d_arrivals = _ag_copies(bufs[wi], sems_in[0], sems_in[1], 3 * wi, "d2d")
                for cp in d2d_arrivals:
                    cp.wait_recv()
                for cp in d2d_sends:
                    cp.wait_send()

    sems_in = (list(finish[1:]) if finish else []) + (list(advance[1:]) if advance else [])
    bufs, sems, _ = _comm_call(name, body, fin_bufs + adv_bufs, sems_in, [3 * len(adv_bufs)] * 2 if advance else [], after)
    return bufs[:nf], ((bufs[nf:], sems[0], sems[1]) if advance else None)


def _cx_copies(src, dst, send_sems, recv_sems, k0):
    x, y, c, chips = _position()
    sends = [_remote(src.at[2 * cx + cy], dst.at[2 * x + y], send_sems, recv_sems, k0 + j, (cx, cy, c))
             for j, (cx, cy) in enumerate(chips)]
    arrivals = [_remote(dst.at[2 * cx + cy], dst.at[2 * cx + cy], send_sems, recv_sems, k0 + j, (x, y, c))
                for j, (cx, cy) in enumerate(chips)]
    return sends, arrivals


def _cx_start(name, pair_sums):
    n = len(pair_sums)
    landing = [lax.empty(p.shape, p.dtype) for p in pair_sums]

    def body(bufs, _, sems):
        for w in range(n):
            for cp in _cx_copies(bufs[w], bufs[n + w], sems[0], sems[1], 3 * w)[0]:
                cp.start()

    bufs, sems, token = _comm_call(name, body, list(pair_sums) + landing, [], [3 * n, 3 * n], token=True)
    return (bufs, sems), token


def _cx_wait(name, state, after):
    bufs, sems = state
    n = len(bufs) // 2

    def body(refs, sems_in, _):
        for w in range(n):
            sends, arrivals = _cx_copies(refs[w], refs[n + w], sems_in[0], sems_in[1], 3 * w)
            for cp in arrivals:
                cp.wait_recv()
            for cp in sends:
                cp.wait_send()

    bufs, _, _ = _comm_call(name, body, bufs, sems, [], after)
    return bufs[:n], bufs[n:]


def _px_copies(src, dst, send_sems, recv_sems, k):
    x, y, c, _ = _position()
    hr = src.shape[1] // 2
    send = _remote(src.at[:, pl.ds((1 - c) * hr, hr), :], dst, send_sems, recv_sems, k, (x, y, 1 - c))
    arrival = _remote(dst, dst, send_sems, recv_sems, k, (x, y, c))
    return send, arrival


def _px_start(name, grads):
    n = len(grads)
    landing = [lax.empty((N_CHIPS, g.shape[1] // 2, g.shape[2]), F32) for g in grads]

    def body(bufs, _, sems):
        for w in range(n):
            _px_copies(bufs[w], bufs[n + w], sems[0], sems[1], w)[0].start()

    bufs, sems, token = _comm_call(name, body, list(grads) + landing, [], [n, n], token=True)
    return (bufs, sems), token


def _px_wait(name, state, after):
    bufs, sems = state
    n = len(bufs) // 2

    def body(refs, sems_in, _):
        for w in range(n):
            send, arrival = _px_copies(refs[w], refs[n + w], sems_in[0], sems_in[1], w)
            arrival.wait_recv()
            send.wait_send()

    bufs, _, _ = _comm_call(name, body, bufs, sems, [], after)
    return bufs[:n], bufs[n:]


def _pair_sum(grad, got, name):
    _, rows, cols = grad.shape
    hr = rows // 2
    tm = min(hr, 256)
    nb = hr // tm
    c = lax.axis_index("c")

    def body(c_ref, g_ref, o_ref, out_ref):
        out_ref[...] = (g_ref[...] + o_ref[...]).astype(BF16)

    return pl.pallas_call(
        body, name=name,
        grid_spec=pltpu.PrefetchScalarGridSpec(
            num_scalar_prefetch=1, grid=(N_CHIPS, nb),
            in_specs=[pl.BlockSpec((None, tm, cols), lambda s, i, c_ref: (s, c_ref[0] * nb + i, 0)),
                      pl.BlockSpec((None, tm, cols), lambda s, i, c_ref: (s, i, 0))],
            out_specs=pl.BlockSpec((None, tm, cols), lambda s, i, c_ref: (s, i, 0))),
        out_shape=jax.ShapeDtypeStruct((N_CHIPS, hr, cols), BF16),
        compiler_params=_params(2),
    )(jnp.reshape(c, (1,)).astype(jnp.int32), grad, got)


def _chip_sum(parts, pair_sums, name):
    _, hr, cols = parts.shape
    tm = min(hr, 256)
    nb = hr // tm
    x, y, c = lax.axis_index("x"), lax.axis_index("y"), lax.axis_index("c")

    def body(pos_ref, p_ref, own_ref, o_ref):
        chip = pos_ref[0]
        own = own_ref[...].astype(F32)
        term = lambda s: jnp.where(chip == s, own, p_ref[s].astype(F32))
        o_ref[...] = ((term(0) + term(1)) + term(2)) + term(3)

    return pl.pallas_call(
        body, name=name,
        grid_spec=pltpu.PrefetchScalarGridSpec(
            num_scalar_prefetch=1, grid=(nb,),
            in_specs=[pl.BlockSpec((N_CHIPS, tm, cols), lambda i, pos: (0, i, 0)),
                      pl.BlockSpec((None, tm, cols), lambda i, pos: (pos[0], i, 0))],
            out_specs=pl.BlockSpec((tm, cols), lambda i, pos: (pos[1] * nb + i, 0))),
        out_shape=jax.ShapeDtypeStruct((2 * hr, cols), F32), compiler_params=_params(1),
    )(jnp.stack([2 * x + y, c]).astype(jnp.int32), parts, pair_sums)


def _share_halves(bufs, name):
    n = len(bufs)

    def body(*refs):
        outs = refs[n:2 * n]
        send_sems, recv_sems = refs[2 * n:]
        x, y, c, _ = _position()
        copies = []
        for w in range(n):
            hr = outs[w].shape[0] // 2
            mine = outs[w].at[pl.ds(c * hr, hr), :]
            cp = _remote(mine, mine, send_sems, recv_sems, w, (x, y, 1 - c))
            cp.start()
            copies.append(cp)
        for w in range(n):
            hr = outs[w].shape[0] // 2
            theirs = outs[w].at[pl.ds((1 - c) * hr, hr), :]
            _remote(theirs, theirs, send_sems, recv_sems, w, (x, y, c)).wait_recv()
        for cp in copies:
            cp.wait_send()

    return pl.pallas_call(
        body, name=name,
        in_specs=[ANY] * n, out_specs=[ANY] * n,
        out_shape=[jax.ShapeDtypeStruct(b.shape, b.dtype) for b in bufs],
        input_output_aliases={w: w for w in range(n)},
        scratch_shapes=[pltpu.SemaphoreType.DMA((n,)), pltpu.SemaphoreType.DMA((n,))],
    )(*bufs)


def _allreduce_small(g):
    rows = g.shape[0]

    def body(g_ref, o_ref, sib, slots, send_sems, recv_sems):
        x, y, c, chips = _position()
        me = (x, y, c)
        my_chip = 2 * x + y
        pair = _remote(g_ref, sib, send_sems, recv_sems, 0, (x, y, 1 - c))
        pair.start()
        pair.wait()
        slots[my_chip] = g_ref[...] + sib[...]
        sent = []
        for j, (cx, cy) in enumerate(chips):
            cp = _remote(slots.at[my_chip], slots.at[my_chip], send_sems, recv_sems, 1 + j, (cx, cy, c))
            cp.start()
            sent.append(cp)
        for j, (cx, cy) in enumerate(chips):
            got = slots.at[2 * cx + cy]
            _remote(got, got, send_sems, recv_sems, 1 + j, me).wait_recv()
        for cp in sent:
            cp.wait_send()
        o_ref[...] = ((slots[0] + slots[1]) + slots[2]) + slots[3]

    vm = pl.BlockSpec(memory_space=pltpu.VMEM)
    return pl.pallas_call(
        body, name="allreduce_small",
        in_specs=[vm], out_specs=vm, out_shape=jax.ShapeDtypeStruct((rows, 128), F32),
        scratch_shapes=[pltpu.VMEM((rows, 128), F32), pltpu.VMEM((N_CHIPS, rows, 128), F32),
                        pltpu.SemaphoreType.DMA((4,)), pltpu.SemaphoreType.DMA((4,))],
        compiler_params=pltpu.CompilerParams(vmem_limit_bytes=VMEM_LIMIT),
    )(g)


_SMALL =("rel_bias", "ln_v_gain", "ln_v_bias", "w_spatial", "b_spatial", "ln1_gain", "ln1_bias",
          "b_ff1", "b_ff2", "ln2_gain", "ln2_bias")
_SMALL_ROWS = 1200
_LOSS_AT = (152832 // 128, 0)


def _pack_small(parts):
    flat = jnp.concatenate([parts[k].reshape(-1).astype(F32) for k in _SMALL])
    flat = jnp.pad(flat, (0, _SMALL_ROWS * 128 - flat.shape[0]))
    return flat.reshape(_SMALL_ROWS, 128)


def _unpack_small(packed, like):
    flat = packed.reshape(-1)
    out, at = {}, 0
    for k in _SMALL:
        n = math.prod(like[k].shape)
        out[k] = flat[at:at + n].reshape(like[k].shape)
        at += n
    return out


def kernel(x, w_in, rel_bias, ln_v_gain, ln_v_bias, w_spatial, b_spatial, w_proj_a, w_proj_b, w_out, ln1_gain, ln1_bias, w_ff1, b_ff1, w_ff2, b_ff2, ln2_gain, ln2_bias, loss_target, m_w_in, m_rel_bias, m_ln_v_gain, m_ln_v_bias, m_w_spatial, m_b_spatial, m_w_proj_a, m_w_proj_b, m_w_out, m_ln1_gain, m_ln1_bias, m_w_ff1, m_b_ff1, m_w_ff2, m_b_ff2, m_ln2_gain, m_ln2_bias, v_w_in, v_rel_bias, v_ln_v_gain, v_ln_v_bias, v_w_spatial, v_b_spatial, v_w_proj_a, v_w_proj_b, v_w_out, v_ln1_gain, v_ln1_bias, v_w_ff1, v_b_ff1, v_w_ff2, v_b_ff2, v_ln2_gain, v_ln2_bias):
    args = dict(locals())
    big = ("w_in", "w_proj_a", "w_proj_b", "w_out", "w_ff1", "w_ff2")
    weights = ("w_in", "rel_bias", "ln_v_gain", "ln_v_bias", "w_spatial", "b_spatial", "w_proj_a", "w_proj_b", "w_out",
               "ln1_gain", "ln1_bias", "w_ff1", "b_ff1", "w_ff2", "b_ff2", "ln2_gain", "ln2_bias")

    xs = x[0]
    target = loss_target[0]

    (in_a,), tok = _ag_start("allgather_start_w_in", [[_place_shard(w_in[0], "place_w_in")]])
    placed = [_place_shard(args[k][0], f"place_{k}", after=tok) for k in big[1:]]
    (in_b, in_c, in_d), tok = _ag_start("allgather_start_rest", [placed[0:3], placed[3:4], placed[4:5]])
    xb = _to_bf16(xs, "x_to_bf16", after=tok)
    _, d2d_a = _ag_step("allgather_w_in_pass", None, in_a, after=xb)
    (win_g,), _ = _ag_step("allgather_w_in_done", d2d_a, None)

    proj = _proj(xb, win_g)
    _, d2d_b = _ag_step("allgather_b_pass", None, in_b, after=proj)
    ws = w_spatial[0]
    ws_t = jnp.transpose(ws, (0, 2, 1))
    bsp_b = jnp.broadcast_to(b_spatial[0][:, :, None], (NH, 128, 128))
    gmlp = _gmlp_fwd(proj, ws, bsp_b, ln_v_gain, ln_v_bias)
    attn, lse = _attention_fwd(proj, rel_bias)
    (wpa_g, wpb_g, wout_g), d2d_c = _ag_step("allgather_b_done_c_pass", d2d_b, in_c, after=attn)
    wout_full = wout_g.reshape(D, D)
    ya, yb, merged = _branch(attn, gmlp, wpa_g, wpb_g, proj)
    xhat1, rstd1, h1b = _out_ln1(merged, wout_full, xs, ln1_gain, ln1_bias)
    (w1_g,), _ = _ag_step("allgather_c_done", d2d_c, None, after=h1b)
    a, r = _ff1(h1b, w1_g, b_ff1)
    _, d2d_d = _ag_step("allgather_d_pass", None, in_d, after=a)
    (w2_g,), _ = _ag_step("allgather_d_done", d2d_d, None)
    w2_full = w2_g.reshape(DFF, D)
    dpre2, dpre2b, st2 = _ff2_ln2_loss(a, w2_full, xhat1, ln1_gain, ln1_bias, b_ff2, ln2_gain, ln2_bias, target)

    def pair_and_chip(tag, state, after):
        local, from_sibling = _px_wait(f"pair_exchange_wait_{tag}", state, after)
        pair_sums = [_pair_sum(g, o, f"pair_sum_{tag}_{i}") for i, (g, o) in enumerate(zip(local, from_sibling))]
        return _cx_start(f"chip_exchange_start_{tag}", pair_sums)

    g_w2 = _grad_w(a, dpre2b, "grad_w_ff2", 512, 2048, False)
    px, tok = _px_start("pair_exchange_start_w_ff2", [g_w2.reshape(N_CHIPS, DFF // N_CHIPS, D)])
    dprea, g_b1 = _d_ff1(dpre2b, w2_full, r, after=tok)
    cx_w2, tok = pair_and_chip("w_ff2", px, dprea)
    g_w1 = _grad_w(h1b, dprea, "grad_w_ff1", 512, 2048, True, after=tok)
    px, tok = _px_start("pair_exchange_start_w_ff1", [g_w1])
    dpre1, dpre1b, st1 = _d_h1_ln1(dprea, w1_g, dpre2, xhat1, rstd1, ln1_gain, after=tok)
    cx_w1, tok = pair_and_chip("w_ff1", px, dpre1b)
    g_wout = _grad_w(merged, dpre1b, "grad_w_out", 512, 2048, False, after=tok)
    dya, dyb, dga, dgb = _d_merged(dpre1b, wout_full, proj, ya, yb)
    g_wpa = _grad_w(attn, dya, "grad_w_proj_a", 1024, 512, True)
    g_wpb = _grad_w(gmlp, dyb, "grad_w_proj_b", 1024, 512, True)
    px, tok = _px_start("pair_exchange_start_b", [g_wpa, g_wpb, g_wout.reshape(N_CHIPS, D // N_CHIPS, D)])
    dattn, dgmlp = _d_branches(dya, dyb, wpa_g, wpb_g, after=tok)
    duv, g_ws, g_bs, stv = _gmlp_bwd(proj, dgmlp, ws, ws_t, bsp_b, ln_v_gain, ln_v_bias)
    cx_b, tok = pair_and_chip("b", px, duv)
    dq, dk, dv, ds_sums = _attention_bwd(proj, dattn, attn, lse, rel_bias, after=tok)
    g_rb = _rel_bias_grad(ds_sums)[:, :NH]

    small_g = dict(rel_bias=g_rb, ln_v_gain=stv[0], ln_v_bias=stv[1], w_spatial=g_ws, b_spatial=g_bs[:, :, 0],
                   ln1_gain=st1[0], ln1_bias=st1[1], b_ff1=g_b1, b_ff2=st2[2], ln2_gain=st2[0], ln2_bias=st2[1])
    gs = _allreduce_small(_pack_small(small_g).at[_LOSS_AT].set(st2[3, 0]))
    ds_, ms_, vs_, _ = _adamw(_pack_small({k: args[k] for k in _SMALL}), gs,
                           _pack_small({k: args["m_" + k] for k in _SMALL}),
                           _pack_small({k: args["v_" + k] for k in _SMALL}), "adamw_small")
    like = {k: args[k] for k in _SMALL}
    grads, deltas, new_m, new_v = (_unpack_small(t, like) for t in (gs, ds_, ms_, vs_))

    dproj = jnp.concatenate([dq, dk, dv, duv, dga, dgb], axis=1)
    g_win = _grad_w(xb, dproj, "grad_w_in", 512, 2304, True, after=gs)
    px, tok = _px_start("pair_exchange_start_w_in", [g_win])
    grad_x = _d_x(dproj, win_g, dpre1, after=tok)
    cx_in, tok = pair_and_chip("w_in", px, grad_x)

    def reduce_finish(tag, state, names, after):
        pair_sums, from_chips = _cx_wait(f"chip_exchange_wait_{tag}", state, after)
        halves = [_chip_sum(p, own, f"chip_sum_{k}") for p, own, k in zip(from_chips, pair_sums, names)]
        last = None
        for k, g in zip(names, _share_halves(halves, f"share_halves_{tag}")):
            d_, m_, v_, g_ = _adamw(args[k][0], g, args["m_" + k][0], args["v_" + k][0], f"adamw_{k}")
            grads[k], deltas[k], new_m[k], new_v[k] = g_[None], d_[None], m_[None], v_[None]
            last = d_
        return last

    done = reduce_finish("w_ff2", cx_w2, ["w_ff2"], tok)
    done = reduce_finish("w_ff1", cx_w1, ["w_ff1"], done)
    done = reduce_finish("b", cx_b, ["w_proj_a", "w_proj_b", "w_out"], done)
    reduce_finish("w_in", cx_in, ["w_in"], done)

    loss = gs[_LOSS_AT] * (0.5 / D)
    return (loss, grad_x[None], *[grads[k] for k in weights], *[deltas[k] for k in weights],
            *[new_m[k] for k in weights], *[new_v[k] for k in weights])
```

```python
import functools
import math

import numpy as np
import jax
import jax.numpy as jnp
from jax import lax
from jax.experimental import pallas as pl
from jax.experimental.pallas import tpu as pltpu

F32 = jnp.float32
BF16 = jnp.bfloat16

S = 2048
D = 2048
DA = 1024
DB = 1024
DFF = 8192
DIN = 9216
NH = 8
HD = 128
NBLK = 16
PATTERNS = ((128, 1), (512, 4), (2048, 16))
N_BUCKETS = 32
MAX_DISTANCE = 2048
ALPHA = 2.0 ** 0.25
LN_EPS = 1e-5
NEG_INF = -1e30
SCALE = HD ** -0.5
N_CHIPS = 4

ADAM_LR = 0.001
ADAM_B1 = 0.9
ADAM_B2 = 0.999
ADAM_EPS = 1e-08
ADAM_WD = 0.01
ADAM_STEP = 10

VMEM_LIMIT = 56 * 1024 * 1024
MESH = pl.DeviceIdType.MESH
ANY = pl.BlockSpec(memory_space=pl.ANY)


def _params(n_axes, vmem=VMEM_LIMIT):
    return pltpu.CompilerParams(dimension_semantics=("arbitrary",) * n_axes, vmem_limit_bytes=vmem)


def _bucket_tile(dilation):
    qi = np.arange(128)[:, None]
    kj = np.arange(256)[None, :]
    n = np.clip(128 + qi - kj, 0, 128) * dilation
    max_exact = N_BUCKETS // 2
    nf = np.maximum(n, 1).astype(np.float32)
    large = max_exact + (np.log(nf / np.float32(max_exact)) / np.float32(math.log(MAX_DISTANCE / max_exact))
                         * np.float32(N_BUCKETS - max_exact)).astype(np.int32)
    large = np.minimum(large, N_BUCKETS - 1)
    return np.where(n < max_exact, n, large).astype(np.int32)


def _gelu(x):
    c = math.sqrt(2.0 / math.pi)
    t = jnp.tanh(c * (x + 0.044715 * x * x * x))
    return 0.5 * x * (1.0 + t), t


def _gelu_grad(x, t):
    c = math.sqrt(2.0 / math.pi)
    return 0.5 * (1.0 + t) + 0.5 * x * (1.0 - t * t) * c * (1.0 + 3.0 * 0.044715 * x * x)


def _sigmoid(x):
    return 1.0 / (1.0 + jnp.exp(-x))


def _dot(a, b):
    return jnp.dot(a, b, preferred_element_type=F32)


def _behind(body, n_in, after):
    if after is None:
        return body, [], []
    return (lambda *refs: body(*refs[:n_in], *refs[n_in + 1:])), [ANY], [after]


def _dot_nt(a, b):
    return lax.dot_general(a, b, (((1,), (1,)), ((), ())), preferred_element_type=F32)


def _proj(xb, win_g):
    tn = 768
    per = 2304 // tn

    def body(x_ref, w_ref, o_ref):
        o_ref[...] = _dot(x_ref[...], w_ref[...])

    return pl.pallas_call(
        body, name="proj", grid=(DIN // tn,),
        in_specs=[pl.BlockSpec((S, D), lambda j: (0, 0)),
                  pl.BlockSpec((None, D, tn), lambda j: (j // per, 0, j % per))],
        out_specs=pl.BlockSpec((S, tn), lambda j: (0, j)),
        out_shape=jax.ShapeDtypeStruct((S, DIN), F32),
        compiler_params=_params(1),
    )(xb, win_g)


FWD_HEADS_PER_STEP = 4
BWD_HEADS_PER_STEP = 2


def _head_bias_tiles(rb_ref, bk_ref, bias_scr, first_head, hps):
    qi = lax.broadcasted_iota(jnp.int32, (128, 256), 0)
    kj = lax.broadcasted_iota(jnp.int32, (128, 256), 1)
    steps = 128 + qi - kj
    band = (steps >= 0) & (steps <= 128)
    bias_scr[...] = jnp.zeros_like(bias_scr)
    for p in range(len(PATTERNS)):
        bucket = bk_ref[p]

        def one_bucket(t, carry):
            hit = bucket == t
            for j in range(hps):
                bias_scr[p, j] = jnp.where(hit, rb_ref[t, first_head + j], bias_scr[p, j])
            return carry

        lax.fori_loop(0, N_BUCKETS, one_bucket, 0)
        for j in range(hps):
            bias_scr[p, j] = jnp.where(band, bias_scr[p, j], NEG_INF)


def _block_rows(b, dilation):
    nblk = NBLK // dilation
    r, n = b // nblk, b % nblk
    start = r + n * (128 * dilation)
    prev_start = jnp.maximum(start - 128 * dilation, r)
    if dilation == 1:
        return pl.ds(pl.multiple_of(start, 128), 128), pl.ds(pl.multiple_of(prev_start, 128), 128), n > 0
    return pl.ds(start, 128, stride=dilation), pl.ds(prev_start, 128, stride=dilation), n > 0


def _head_specs(first, hps):
    return [pl.BlockSpec((S, HD), lambda g, j=j: (0, first + g * hps + j)) for j in range(hps)]


def _heads_spec(hps):
    return pl.BlockSpec((S, hps * HD), lambda g: (0, g))


def _attention_fwd(proj, rel_bias):
    hps = FWD_HEADS_PER_STEP
    buckets = jnp.asarray(np.stack([_bucket_tile(d) for _, d in PATTERNS]))

    def body(rb_ref, bk_ref, *refs):
        q_refs, k_refs, v_refs = (refs[i * hps:(i + 1) * hps] for i in range(3))
        o_ref, lse_ref, bias_scr = refs[3 * hps:3 * hps + 3]
        acc_scrs, m_scrs, l_scrs = (refs[3 * hps + 3 + i * hps:3 * hps + 3 + (i + 1) * hps] for i in range(3))
        _head_bias_tiles(rb_ref, bk_ref, bias_scr, pl.program_id(0) * hps, hps)
        kj = lax.broadcasted_iota(jnp.int32, (128, 256), 1)
        for p, (_, d) in enumerate(PATTERNS):
            prev_blocks = NBLK // d > 1

            def block(b, carry):
                rows, prows, has_prev = _block_rows(b, d)
                key_ok = (kj >= 128) | has_prev
                scores = []
                for j in range(hps):
                    q = q_refs[j][rows, :].astype(BF16)
                    cur = _dot_nt(q, k_refs[j][rows, :].astype(BF16))
                    if prev_blocks:
                        cur = jnp.concatenate([_dot_nt(q, k_refs[j][prows, :].astype(BF16)), cur], axis=1)
                    scores.append(cur)
                soft = []
                for j in range(hps):
                    if prev_blocks:
                        s = jnp.where(key_ok, scores[j] * SCALE + bias_scr[p, j], NEG_INF)
                    else:
                        s = scores[j] * SCALE + bias_scr[p, j, :, 128:256]
                    m = jnp.max(s, axis=1, keepdims=True)
                    e = jnp.exp(s - m)
                    soft.append((m, jnp.sum(e, axis=1, keepdims=True), e.astype(BF16)))
                outs = []
                for j in range(hps):
                    e = soft[j][2]
                    if prev_blocks:
                        outs.append(_dot(e[:, :128], v_refs[j][prows, :].astype(BF16))
                                    + _dot(e[:, 128:], v_refs[j][rows, :].astype(BF16)))
                    else:
                        outs.append(_dot(e, v_refs[j][rows, :].astype(BF16)))
                for j in range(hps):
                    acc_scr, m_scr, l_scr = acc_scrs[j], m_scrs[j], l_scrs[j]
                    (m, den, _), o = soft[j], outs[j]
                    if p == 0:
                        acc_scr[rows, :] = o
                        m_scr[rows, :] = jnp.broadcast_to(m, (128, HD))
                        l_scr[rows, :] = jnp.broadcast_to(den, (128, HD))
                    else:
                        m_old = m_scr[rows, :]
                        m_new = jnp.maximum(m_old, m)
                        w_old, w_new = jnp.exp(m_old - m_new), jnp.exp(m - m_new)
                        acc_scr[rows, :] = acc_scr[rows, :] * w_old + o * w_new
                        l_scr[rows, :] = l_scr[rows, :] * w_old + den * w_new
                        m_scr[rows, :] = m_new
                return carry

            lax.fori_loop(0, NBLK, block, 0)
        for j in range(hps):
            cols = slice(j * HD, (j + 1) * HD)
            den = l_scrs[j][...]
            o_ref[:, cols] = (acc_scrs[j][...] / den).astype(BF16)
            lse_ref[:, cols] = m_scrs[j][...] + jnp.log(den)

    return pl.pallas_call(
        body, name="attention_fwd", grid=(NH // hps,),
        in_specs=[pl.BlockSpec(memory_space=pltpu.SMEM), pl.BlockSpec((3, 128, 256), lambda g: (0, 0, 0))]
        + _head_specs(0, hps) + _head_specs(NH, hps) + _head_specs(2 * NH, hps),
        out_specs=[_heads_spec(hps), _heads_spec(hps)],
        out_shape=[jax.ShapeDtypeStruct((S, DA), BF16), jax.ShapeDtypeStruct((S, DA), F32)],
        scratch_shapes=[pltpu.VMEM((3, hps, 128, 256), F32)] + [pltpu.VMEM((S, HD), F32)] * (3 * hps),
        compiler_params=_params(1),
    )(rel_bias, buckets, *([proj] * (3 * hps)))


def _attention_bwd(proj, dattn, attn, lse, rel_bias, after=None):
    hps = BWD_HEADS_PER_STEP

    def body(rb_ref, bk_ref, *refs):
        q_refs, k_refs, v_refs, do_refs, o_refs, lse_refs = (refs[i * hps:(i + 1) * hps] for i in range(6))
        dq_ref, dk_ref, dv_ref, ds_ref, bias_scr = refs[6 * hps:6 * hps + 5]
        dl_scrs, dq_scrs, dk_scrs, dv_scrs = (refs[6 * hps + 5 + i * hps:6 * hps + 5 + (i + 1) * hps] for i in range(4))
        _head_bias_tiles(rb_ref, bk_ref, bias_scr, pl.program_id(0) * hps, hps)
        ds_ref[...] = jnp.zeros_like(ds_ref)
        for j in range(hps):
            dq_scrs[j][...] = jnp.zeros((S, HD), F32)
            dk_scrs[j][...] = jnp.zeros((S, HD), F32)
            dv_scrs[j][...] = jnp.zeros((S, HD), F32)
            prod = do_refs[j][...] * o_refs[j][...].astype(F32)
            dl_scrs[j][...] = jnp.broadcast_to(jnp.sum(prod, axis=1, keepdims=True), (S, HD))
        for p, (_, d) in enumerate(PATTERNS):
            prev_blocks = NBLK // d > 1

            def block(b, carry):
                rows, prows, has_prev = _block_rows(b, d)
                ops, raw = [], []
                for j in range(hps):
                    q, do = q_refs[j][rows, :].astype(BF16), do_refs[j][rows, :].astype(BF16)
                    kc, vc = k_refs[j][rows, :].astype(BF16), v_refs[j][rows, :].astype(BF16)
                    if prev_blocks:
                        kp, vp = k_refs[j][prows, :].astype(BF16), v_refs[j][prows, :].astype(BF16)
                        ops.append((q, do, kc, kp))
                        raw.append((_dot_nt(q, kc), _dot_nt(do, vc), _dot_nt(q, kp), _dot_nt(do, vp)))
                    else:
                        ops.append((q, do, kc))
                        raw.append((_dot_nt(q, kc), _dot_nt(do, vc)))
                probs = []
                for j in range(hps):
                    lse_b, dl_b = lse_refs[j][rows, :], dl_scrs[j][rows, :]
                    p_c = jnp.exp(raw[j][0] * SCALE + bias_scr[p, j, :, 128:256] - lse_b)
                    ds_c = p_c * (raw[j][1] - dl_b)
                    ds_ref[p, j, :, 128:256] += ds_c
                    if prev_blocks:
                        p_p = jnp.where(has_prev, jnp.exp(raw[j][2] * SCALE + bias_scr[p, j, :, 0:128] - lse_b), 0.0)
                        ds_p = p_p * (raw[j][3] - dl_b)
                        ds_ref[p, j, :, 0:128] += ds_p
                        probs.append((p_c, ds_c, p_p, ds_p))
                    else:
                        probs.append((p_c, ds_c))
                grads = []
                for j in range(hps):
                    q, do, kc = ops[j][:3]
                    p_c, ds_c = probs[j][:2]
                    dq = _dot(ds_c.astype(BF16), kc)
                    cur = (_dot(ds_c.T.astype(BF16), q) * SCALE, _dot(p_c.T.astype(BF16), do))
                    if prev_blocks:
                        p_p, ds_p = probs[j][2:]
                        dq = dq + _dot(ds_p.astype(BF16), ops[j][3])
                        cur = cur + (_dot(ds_p.T.astype(BF16), q) * SCALE, _dot(p_p.T.astype(BF16), do))
                    grads.append((dq * SCALE,) + cur)
                for j in range(hps):
                    dq_scrs[j][rows, :] += grads[j][0]
                    dk_scrs[j][rows, :] += grads[j][1]
                    dv_scrs[j][rows, :] += grads[j][2]
                    if prev_blocks:
                        dk_scrs[j][prows, :] += grads[j][3]
                        dv_scrs[j][prows, :] += grads[j][4]
                return carry

            lax.fori_loop(0, NBLK, block, 0)
        for j in range(hps):
            cols = slice(j * HD, (j + 1) * HD)
            dq_ref[:, cols] = dq_scrs[j][...].astype(BF16)
            dk_ref[:, cols] = dk_scrs[j][...].astype(BF16)
            dv_ref[:, cols] = dv_scrs[j][...].astype(BF16)

    buckets = jnp.asarray(np.stack([_bucket_tile(d) for _, d in PATTERNS]))
    body, more_specs, more = _behind(body, 2 + 6 * hps, after)
    return pl.pallas_call(
        body, name="attention_bwd", grid=(NH // hps,),
        in_specs=[pl.BlockSpec(memory_space=pltpu.SMEM), pl.BlockSpec((3, 128, 256), lambda g: (0, 0, 0))]
        + _head_specs(0, hps) + _head_specs(NH, hps) + _head_specs(2 * NH, hps) + 3 * _head_specs(0, hps)
        + more_specs,
        out_specs=3 * [_heads_spec(hps)] + [pl.BlockSpec((3, hps, 128, 256), lambda g: (0, g, 0, 0))],
        out_shape=[jax.ShapeDtypeStruct((S, DA), BF16)] * 3 + [jax.ShapeDtypeStruct((3, NH, 128, 256), F32)],
        scratch_shapes=[pltpu.VMEM((3, hps, 128, 256), F32)] + [pltpu.VMEM((S, HD), F32)] * (4 * hps),
        compiler_params=_params(1),
    )(rel_bias, buckets, *([proj] * (3 * hps)), *([dattn] * hps), *([attn] * hps), *([lse] * hps), *more)


def _gmlp_parts(u_ref, vb_ref, g_ref, be_ref):
    u = u_ref[...]
    u_act, tu = _gelu(u)
    vb = vb_ref[...]
    gv, tv = _gelu(vb)
    mean = jnp.mean(gv, axis=1, keepdims=True)
    cen = gv - mean
    var = jnp.mean(cen * cen, axis=1, keepdims=True)
    rstd = lax.rsqrt(var + LN_EPS)
    xhat = cen * rstd
    vn = xhat * g_ref[...] + be_ref[...]
    return u, tu, u_act, vb, tv, rstd, xhat, vn


def _gmlp_fwd(proj, ws, bsp_b, gain_v, bias_v):
    def body(u_ref, vb_ref, ws_ref, bsp_ref, g_ref, be_ref, o_ref):
        _, _, u_act, _, _, _, _, vn = _gmlp_parts(u_ref, vb_ref, g_ref, be_ref)
        row = lax.broadcasted_iota(jnp.int32, (128, 128), 0)
        col = lax.broadcasted_iota(jnp.int32, (128, 128), 1)
        causal = row >= col
        for g in range(NH):
            cols = slice(g * 128, (g + 1) * 128)
            wsg = jnp.where(causal, ws_ref[g], 0.0).astype(BF16)
            z = _dot(wsg, vn[:, cols].astype(BF16)) + bsp_ref[g]
            o_ref[:, cols] = (u_act[:, cols] * z).astype(BF16)

    return pl.pallas_call(
        body, name="gmlp_fwd", grid=(NBLK,),
        in_specs=[pl.BlockSpec((128, DB), lambda c: (c, 3)), pl.BlockSpec((128, DB), lambda c: (c, 4)),
                  pl.BlockSpec((NH, 128, 128), lambda c: (0, 0, 0)), pl.BlockSpec((NH, 128, 128), lambda c: (0, 0, 0)),
                  pl.BlockSpec((1, DB), lambda c: (0, 0)), pl.BlockSpec((1, DB), lambda c: (0, 0))],
        out_specs=pl.BlockSpec((128, DB), lambda c: (c, 0)),
        out_shape=jax.ShapeDtypeStruct((S, DB), BF16),
        compiler_params=_params(1),
    )(proj, proj, ws, bsp_b, gain_v, bias_v)


def _branch(attn, gmlp, wpa_g, wpb_g, proj):
    tn = 512

    def body(a_ref, g_ref, wa_ref, wb_ref, ga_ref, gb_ref, ya_ref, yb_ref, mg_ref):
        ya = _dot(a_ref[...], wa_ref[...])
        yb = _dot(g_ref[...], wb_ref[...])
        ya_ref[...] = ya.astype(BF16)
        yb_ref[...] = yb.astype(BF16)
        mg_ref[...] = (_sigmoid(ga_ref[...]) * ya + _sigmoid(gb_ref[...]) * yb).astype(BF16)

    out = pl.BlockSpec((S, tn), lambda j: (0, j))
    return pl.pallas_call(
        body, name="branch", grid=(D // tn,),
        in_specs=[pl.BlockSpec((S, DA), lambda j: (0, 0)), pl.BlockSpec((S, DB), lambda j: (0, 0)),
                  pl.BlockSpec((None, DA, tn), lambda j: (j, 0, 0)), pl.BlockSpec((None, DB, tn), lambda j: (j, 0, 0)),
                  pl.BlockSpec((S, tn), lambda j: (0, 5120 // tn + j)), pl.BlockSpec((S, tn), lambda j: (0, 7168 // tn + j))],
        out_specs=[out, out, out],
        out_shape=[jax.ShapeDtypeStruct((S, D), BF16)] * 3,
        compiler_params=_params(1),
    )(attn, gmlp, wpa_g, wpb_g, proj, proj)


def _out_ln1(merged, wout_g, x, gain, bias):
    tm = 256

    def body(m_ref, w_ref, x_ref, g_ref, b_ref, xh_ref, rs_ref, h_ref):
        pre = ALPHA * x_ref[...] + _dot(m_ref[...], w_ref[...])
        mean = jnp.mean(pre, axis=1, keepdims=True)
        cen = pre - mean
        var = jnp.mean(cen * cen, axis=1, keepdims=True)
        rstd = lax.rsqrt(var + LN_EPS)
        xhat = cen * rstd
        xh_ref[...] = xhat
        rs_ref[...] = jnp.broadcast_to(rstd, (tm, 128))
        h_ref[...] = (xhat * g_ref[...] + b_ref[...]).astype(BF16)

    row = pl.BlockSpec((tm, D), lambda i: (i, 0))
    vec = pl.BlockSpec((1, D), lambda i: (0, 0))
    return pl.pallas_call(
        body, name="out_ln1", grid=(S // tm,),
        in_specs=[row, pl.BlockSpec((D, D), lambda i: (0, 0)), row, vec, vec],
        out_specs=[row, pl.BlockSpec((tm, 128), lambda i: (i, 0)), row],
        out_shape=[jax.ShapeDtypeStruct((S, D), F32), jax.ShapeDtypeStruct((S, 128), F32),
                   jax.ShapeDtypeStruct((S, D), BF16)],
        compiler_params=_params(1),
    )(merged, wout_g, x, gain, bias)


def _ff1(h1b, w1_g, b1):
    tn = 512
    per = D // tn

    def body(h_ref, w_ref, b_ref, a_ref, r_ref):
        r = jnp.maximum(_dot(h_ref[...], w_ref[...]) + b_ref[...], 0.0)
        r_ref[...] = r.astype(BF16)
        a_ref[...] = (r * r).astype(BF16)

    out = pl.BlockSpec((S, tn), lambda j: (0, j))
    return pl.pallas_call(
        body, name="ff1", grid=(DFF // tn,),
        in_specs=[pl.BlockSpec((S, D), lambda j: (0, 0)),
                  pl.BlockSpec((None, D, tn), lambda j: (j // per, 0, j % per)),
                  pl.BlockSpec((1, tn), lambda j: (0, j))],
        out_specs=[out, out],
        out_shape=[jax.ShapeDtypeStruct((S, DFF), BF16)] * 2,
        compiler_params=_params(1),
    )(h1b, w1_g, b1)


def _ff2_ln2_loss(a, w2_g, xhat1, g1, b1, b2, g2, be2, target):
    tm, tk = 512, 1024
    nk = DFF // tk

    def body(a_ref, w_ref, xh_ref, g1_ref, b1_ref, b2_ref, g2_ref, be2_ref, t_ref, d_ref, db_ref, st_ref, acc):
        i, k = pl.program_id(0), pl.program_id(1)

        @pl.when(k == 0)
        def _():
            acc[...] = jnp.zeros_like(acc)

        @pl.when((i == 0) & (k == 0))
        def _():
            st_ref[...] = jnp.zeros_like(st_ref)

        acc[...] += _dot(a_ref[...], w_ref[...])

        @pl.when(k == nk - 1)
        def _():
            def rows_chunk(ci, carry):
                rows = pl.ds(pl.multiple_of(ci * 128, 128), 128)
                h1 = xh_ref[rows, :] * g1_ref[...] + b1_ref[...]
                pre = ALPHA * h1 + acc[rows, :] + b2_ref[...]
                mean = jnp.mean(pre, axis=1, keepdims=True)
                cen = pre - mean
                var = jnp.mean(cen * cen, axis=1, keepdims=True)
                rstd = lax.rsqrt(var + LN_EPS)
                xhat = cen * rstd
                y = xhat * g2_ref[...] + be2_ref[...]
                err = y - t_ref[rows, :]
                dy = err * (1.0 / D)
                g = dy * g2_ref[...]
                dpre = rstd * (g - jnp.mean(g, axis=1, keepdims=True)
                               - xhat * jnp.mean(g * xhat, axis=1, keepdims=True))
                d_ref[rows, :] = dpre
                db_ref[rows, :] = dpre.astype(BF16)
                st_ref[0:1, :] += jnp.sum(dy * xhat, axis=0, keepdims=True)
                st_ref[1:2, :] += jnp.sum(dy, axis=0, keepdims=True)
                st_ref[2:3, :] += jnp.sum(dpre, axis=0, keepdims=True)
                st_ref[3:4, :] += jnp.broadcast_to(jnp.sum(err * err).reshape(1, 1), (1, D))
                return carry

            lax.fori_loop(0, tm // 128, rows_chunk, 0)

    row = pl.BlockSpec((tm, D), lambda i, k: (i, 0))
    vec = pl.BlockSpec((1, D), lambda i, k: (0, 0))
    return pl.pallas_call(
        body, name="ff2_ln2_loss", grid=(S // tm, nk),
        in_specs=[pl.BlockSpec((tm, tk), lambda i, k: (i, k)), pl.BlockSpec((tk, D), lambda i, k: (k, 0)),
                  row, vec, vec, vec, vec, vec, row],
        out_specs=[row, row, pl.BlockSpec((8, D), lambda i, k: (0, 0))],
        out_shape=[jax.ShapeDtypeStruct((S, D), F32), jax.ShapeDtypeStruct((S, D), BF16),
                   jax.ShapeDtypeStruct((8, D), F32)],
        scratch_shapes=[pltpu.VMEM((tm, D), F32)],
        compiler_params=_params(2),
    )(a, w2_g, xhat1, g1, b1, b2, g2, be2, target)


def _grad_w(act, dout, name, ti, tj, sharded, after=None):
    m, n = act.shape[1], dout.shape[1]
    ns = n // N_CHIPS
    per = ns // tj if sharded else None

    def body(a_ref, b_ref, o_ref, at_scr):
        @pl.when(pl.program_id(1) == 0)
        def _():
            at_scr[...] = a_ref[...].T

        o_ref[...] = _dot(at_scr[...], b_ref[...])

    if sharded:
        out_spec = pl.BlockSpec((None, ti, tj), lambda i, j: (j // per, i, j % per))
        out_shape = jax.ShapeDtypeStruct((N_CHIPS, m, ns), F32)
    else:
        out_spec = pl.BlockSpec((ti, tj), lambda i, j: (i, j))
        out_shape = jax.ShapeDtypeStruct((m, n), F32)
    body, more_specs, more = _behind(body, 2, after)
    return pl.pallas_call(
        body, name=name, grid=(m // ti, n // tj),
        in_specs=[pl.BlockSpec((S, ti), lambda i, j: (0, i)), pl.BlockSpec((S, tj), lambda i, j: (0, j))] + more_specs,
        out_specs=out_spec, out_shape=out_shape,
        scratch_shapes=[pltpu.VMEM((ti, S), BF16)],
        compiler_params=_params(2),
    )(act, dout, *more)


def _d_ff1(dpre2b, w2_g, r, after=None):
    tn = 512

    def body(d_ref, w_ref, r_ref, o_ref, gb_ref):
        da = _dot_nt(d_ref[...], w_ref[...])
        dp = da * (2.0 * r_ref[...].astype(F32))
        o_ref[...] = dp.astype(BF16)
        gb_ref[...] = jnp.sum(dp, axis=0, keepdims=True)

    body, more_specs, more = _behind(body, 3, after)
    return pl.pallas_call(
        body, name="d_ff1", grid=(DFF // tn,),
        in_specs=[pl.BlockSpec((S, D), lambda j: (0, 0)), pl.BlockSpec((tn, D), lambda j: (j, 0)),
                  pl.BlockSpec((S, tn), lambda j: (0, j))] + more_specs,
        out_specs=[pl.BlockSpec((S, tn), lambda j: (0, j)), pl.BlockSpec((1, tn), lambda j: (0, j))],
        out_shape=[jax.ShapeDtypeStruct((S, DFF), BF16), jax.ShapeDtypeStruct((1, DFF), F32)],
        compiler_params=_params(1),
    )(dpre2b, w2_g, r, *more)


def _d_h1_ln1(dprea, w1_g, dpre2, xhat1, rstd1, g1, after=None):
    tm, tk = 512, 1024
    per = D // tk
    nk = DFF // tk

    def body(a_ref, w_ref, d2_ref, xh_ref, rs_ref, g_ref, d_ref, db_ref, st_ref, acc):
        i, k = pl.program_id(0), pl.program_id(1)

        @pl.when(k == 0)
        def _():
            acc[...] = jnp.zeros_like(acc)

        @pl.when((i == 0) & (k == 0))
        def _():
            st_ref[...] = jnp.zeros_like(st_ref)

        acc[...] += _dot_nt(a_ref[...], w_ref[...])

        @pl.when(k == nk - 1)
        def _():
            def rows_chunk(ci, carry):
                rows = pl.ds(pl.multiple_of(ci * 128, 128), 128)
                dh = ALPHA * d2_ref[rows, :] + acc[rows, :]
                xhat = xh_ref[rows, :]
                g = dh * g_ref[...]
                dpre = rs_ref[rows, 0:1] * (g - jnp.mean(g, axis=1, keepdims=True)
                                            - xhat * jnp.mean(g * xhat, axis=1, keepdims=True))
                d_ref[rows, :] = dpre
                db_ref[rows, :] = dpre.astype(BF16)
                st_ref[0:1, :] += jnp.sum(dh * xhat, axis=0, keepdims=True)
                st_ref[1:2, :] += jnp.sum(dh, axis=0, keepdims=True)
                return carry

            lax.fori_loop(0, tm // 128, rows_chunk, 0)

    row = pl.BlockSpec((tm, D), lambda i, k: (i, 0))
    body, more_specs, more = _behind(body, 6, after)
    return pl.pallas_call(
        body, name="d_h1_ln1", grid=(S // tm, nk),
        in_specs=[pl.BlockSpec((tm, tk), lambda i, k: (i, k)),
                  pl.BlockSpec((None, D, tk), lambda i, k: (k // per, 0, k % per)),
                  row, row, pl.BlockSpec((tm, 128), lambda i, k: (i, 0)), pl.BlockSpec((1, D), lambda i, k: (0, 0))]
        + more_specs,
        out_specs=[row, row, pl.BlockSpec((8, D), lambda i, k: (0, 0))],
        out_shape=[jax.ShapeDtypeStruct((S, D), F32), jax.ShapeDtypeStruct((S, D), BF16),
                   jax.ShapeDtypeStruct((8, D), F32)],
        scratch_shapes=[pltpu.VMEM((tm, D), F32)],
        compiler_params=_params(2),
    )(dprea, w1_g, dpre2, xhat1, rstd1, g1, *more)


def _d_merged(dpre1b, wout_g, proj, ya, yb):
    tm, tn = 512, 1024

    def body(d_ref, w_ref, ga_ref, gb_ref, ya_ref, yb_ref, dya_ref, dyb_ref, dga_ref, dgb_ref):
        dm = _dot_nt(d_ref[...], w_ref[...])
        sa = _sigmoid(ga_ref[...])
        sb = _sigmoid(gb_ref[...])
        dya_ref[...] = (dm * sa).astype(BF16)
        dyb_ref[...] = (dm * sb).astype(BF16)
        dga_ref[...] = (dm * ya_ref[...].astype(F32) * sa * (1.0 - sa)).astype(BF16)
        dgb_ref[...] = (dm * yb_ref[...].astype(F32) * sb * (1.0 - sb)).astype(BF16)

    tile = pl.BlockSpec((tm, tn), lambda i, j: (i, j))
    return pl.pallas_call(
        body, name="d_merged", grid=(S // tm, D // tn),
        in_specs=[pl.BlockSpec((tm, D), lambda i, j: (i, 0)), pl.BlockSpec((tn, D), lambda i, j: (j, 0)),
                  pl.BlockSpec((tm, tn), lambda i, j: (i, 5 + j)), pl.BlockSpec((tm, tn), lambda i, j: (i, 7 + j)),
                  tile, tile],
        out_specs=[tile] * 4,
        out_shape=[jax.ShapeDtypeStruct((S, D), BF16)] * 4,
        compiler_params=_params(2),
    )(dpre1b, wout_g, proj, proj, ya, yb)


def _d_branches(dya, dyb, wpa_g, wpb_g, after=None):
    tk = 512

    def body(da_ref, db_ref, wa_ref, wb_ref, oa_ref, ob_ref):
        @pl.when(pl.program_id(0) == 0)
        def _():
            oa_ref[...] = jnp.zeros_like(oa_ref)
            ob_ref[...] = jnp.zeros_like(ob_ref)

        oa_ref[...] += _dot_nt(da_ref[...], wa_ref[...])
        ob_ref[...] += _dot_nt(db_ref[...], wb_ref[...])

    body, more_specs, more = _behind(body, 4, after)
    return pl.pallas_call(
        body, name="d_branches", grid=(D // tk,),
        in_specs=[pl.BlockSpec((S, tk), lambda k: (0, k)), pl.BlockSpec((S, tk), lambda k: (0, k)),
                  pl.BlockSpec((None, DA, tk), lambda k: (k, 0, 0)), pl.BlockSpec((None, DB, tk), lambda k: (k, 0, 0))]
        + more_specs,
        out_specs=[pl.BlockSpec((S, DA), lambda k: (0, 0)), pl.BlockSpec((S, DB), lambda k: (0, 0))],
        out_shape=[jax.ShapeDtypeStruct((S, DA), F32), jax.ShapeDtypeStruct((S, DB), F32)],
        compiler_params=_params(1),
    )(dya, dyb, wpa_g, wpb_g, *more)


def _gmlp_bwd(proj, dgmlp, ws, ws_t, bsp_b, gain_v, bias_v):
    def body(u_ref, vb_ref, dg_ref, ws_ref, wst_ref, bsp_ref, g_ref, be_ref, duv_ref, gws_ref, gbs_ref, st_ref):
        @pl.when(pl.program_id(0) == 0)
        def _():
            gws_ref[...] = jnp.zeros_like(gws_ref)
            gbs_ref[...] = jnp.zeros_like(gbs_ref)
            st_ref[...] = jnp.zeros_like(st_ref)

        u, tu, u_act, vb, tv, rstd, xhat, vn = _gmlp_parts(u_ref, vb_ref, g_ref, be_ref)
        dg = dg_ref[...]
        dz = dg * u_act
        row = lax.broadcasted_iota(jnp.int32, (128, 128), 0)
        col = lax.broadcasted_iota(jnp.int32, (128, 128), 1)
        causal = row >= col
        causal_t = row <= col
        dvn_parts = []
        z_parts = []
        for g in range(NH):
            cols = slice(g * 128, (g + 1) * 128)
            vng = vn[:, cols].astype(BF16)
            dzg = dz[:, cols]
            dzb = dzg.astype(BF16)
            wsg = jnp.where(causal, ws_ref[g], 0.0).astype(BF16)
            wsg_t = jnp.where(causal_t, wst_ref[g], 0.0).astype(BF16)
            z_parts.append(_dot(wsg, vng) + bsp_ref[g])
            gws_ref[g] += jnp.where(causal, _dot_nt(dzb, vng), 0.0)
            gbs_ref[g] += jnp.broadcast_to(jnp.sum(dzg, axis=1, keepdims=True), (128, 128))
            dvn_parts.append(_dot(wsg_t, dzb))
        z = jnp.concatenate(z_parts, axis=1)
        dvn = jnp.concatenate(dvn_parts, axis=1)
        du = dg * z * _gelu_grad(u, tu)
        st_ref[0:1, :] += jnp.sum(dvn * xhat, axis=0, keepdims=True)
        st_ref[1:2, :] += jnp.sum(dvn, axis=0, keepdims=True)
        gg = dvn * g_ref[...]
        dgv = rstd * (gg - jnp.mean(gg, axis=1, keepdims=True) - xhat * jnp.mean(gg * xhat, axis=1, keepdims=True))
        dvb = dgv * _gelu_grad(vb, tv)
        duv_ref[:, 0:DB] = du.astype(BF16)
        duv_ref[:, DB:2 * DB] = dvb.astype(BF16)

    full3 = pl.BlockSpec((NH, 128, 128), lambda c: (0, 0, 0))
    vec = pl.BlockSpec((1, DB), lambda c: (0, 0))
    return pl.pallas_call(
        body, name="gmlp_bwd", grid=(NBLK,),
        in_specs=[pl.BlockSpec((128, DB), lambda c: (c, 3)), pl.BlockSpec((128, DB), lambda c: (c, 4)),
                  pl.BlockSpec((128, DB), lambda c: (c, 0)), full3, full3, full3, vec, vec],
        out_specs=[pl.BlockSpec((128, 2 * DB), lambda c: (c, 0)), full3, full3, pl.BlockSpec((8, DB), lambda c: (0, 0))],
        out_shape=[jax.ShapeDtypeStruct((S, 2 * DB), BF16), jax.ShapeDtypeStruct((NH, 128, 128), F32),
                   jax.ShapeDtypeStruct((NH, 128, 128), F32), jax.ShapeDtypeStruct((8, DB), F32)],
        compiler_params=_params(1),
    )(proj, proj, dgmlp, ws, ws_t, bsp_b, gain_v, bias_v)


def _rel_bias_grad(ds_sums):
    buckets = jnp.asarray(np.stack([_bucket_tile(d) for _, d in PATTERNS]))

    def body(bk_ref, ds_ref, o_ref):
        row = lax.broadcasted_iota(jnp.int32, (N_BUCKETS, 128), 0)
        lane = lax.broadcasted_iota(jnp.int32, (N_BUCKETS, 128), 1)

        def one_bucket(t, out):
            hits = [bk_ref[p] == t for p in range(3)]
            for h in range(NH):
                tot = jnp.zeros((128, 256), F32)
                for p in range(3):
                    tot = tot + jnp.where(hits[p], ds_ref[p, h], 0.0)
                out = jnp.where((row == t) & (lane == h), jnp.sum(tot), out)
            return out

        o_ref[...] = lax.fori_loop(0, N_BUCKETS, one_bucket, jnp.zeros((N_BUCKETS, 128), F32))

    return pl.pallas_call(
        body, name="rel_bias_grad",
        in_specs=[pl.BlockSpec(memory_space=pltpu.VMEM)] * 2, out_specs=pl.BlockSpec(memory_space=pltpu.VMEM),
        out_shape=jax.ShapeDtypeStruct((N_BUCKETS, 128), F32),
        compiler_params=pltpu.CompilerParams(vmem_limit_bytes=VMEM_LIMIT),
    )(buckets, ds_sums)


def _d_x(dproj, win_g, dpre1, after=None):
    tm, tk = 512, 2304
    per = 2304 // tk
    nk = DIN // tk

    def body(a_ref, w_ref, d_ref, o_ref, acc):
        k = pl.program_id(1)

        @pl.when(k == 0)
        def _():
            acc[...] = ALPHA * d_ref[...]

        acc[...] += _dot_nt(a_ref[...], w_ref[...])

        @pl.when(k == nk - 1)
        def _():
            o_ref[...] = acc[...]

    row = pl.BlockSpec((tm, D), lambda i, k: (i, 0))
    body, more_specs, more = _behind(body, 3, after)
    return pl.pallas_call(
        body, name="d_x", grid=(S // tm, nk),
        in_specs=[pl.BlockSpec((tm, tk), lambda i, k: (i, k)),
                  pl.BlockSpec((None, D, tk), lambda i, k: (k // per, 0, k % per)), row] + more_specs,
        out_specs=row, out_shape=jax.ShapeDtypeStruct((S, D), F32),
        scratch_shapes=[pltpu.VMEM((tm, D), F32)],
        compiler_params=_params(2),
    )(dproj, win_g, dpre1, *more)


def _adamw(w, g, m, v, name):
    rows, cols = w.shape
    tm = max(t for t in range(8, 257, 8) if rows % t == 0)

    def body(w_ref, g_ref, m_ref, v_ref, d_ref, nm_ref, nv_ref, go_ref):
        g = g_ref[...]
        m = ADAM_B1 * m_ref[...] + (1.0 - ADAM_B1) * g
        v = ADAM_B2 * v_ref[...] + (1.0 - ADAM_B2) * (g * g)
        m_hat = m / (1.0 - ADAM_B1 ** ADAM_STEP)
        v_hat = v / (1.0 - ADAM_B2 ** ADAM_STEP)
        d_ref[...] = -ADAM_LR * (m_hat / (jnp.sqrt(v_hat) + ADAM_EPS) + ADAM_WD * w_ref[...])
        nm_ref[...] = m
        nv_ref[...] = v
        go_ref[...] = g

    spec = pl.BlockSpec((tm, cols), lambda i: (i, 0))
    return pl.pallas_call(
        body, name=name, grid=(rows // tm,), in_specs=[spec] * 4, out_specs=[spec] * 4,
        out_shape=[jax.ShapeDtypeStruct((rows, cols), F32)] * 4, compiler_params=_params(1),
    )(w, g, m, v)


def _position():
    x, y, c = lax.axis_index("x"), lax.axis_index("y"), lax.axis_index("c")
    chips = [(1 - x, y), (x, 1 - y), (1 - x, 1 - y)]
    return x, y, c, chips


def _remote(src, dst, send_sems, recv_sems, k, to):
    return pltpu.make_async_remote_copy(src_ref=src, dst_ref=dst, send_sem=send_sems.at[k], recv_sem=recv_sems.at[k],
                                        device_id=to, device_id_type=MESH)


def _place_shard(w, name, after=None):
    rows, cols = w.shape
    tm = 256
    x, y = lax.axis_index("x"), lax.axis_index("y")

    def body(chip_ref, w_ref, o_ref):
        o_ref[...] = w_ref[...].astype(BF16)

    more_specs, more = ([ANY], [after]) if after is not None else ([], [])
    if after is not None:
        inner = body
        body = lambda chip_ref, w_ref, after_ref, o_ref: inner(chip_ref, w_ref, o_ref)
    return pl.pallas_call(
        body, name=name,
        grid_spec=pltpu.PrefetchScalarGridSpec(
            num_scalar_prefetch=1, grid=(rows // tm,),
            in_specs=[pl.BlockSpec((tm, cols), lambda i, chip: (i, 0))] + more_specs,
            out_specs=pl.BlockSpec((None, tm, cols), lambda i, chip: (chip[0], i, 0))),
        out_shape=jax.ShapeDtypeStruct((N_CHIPS, rows, cols), BF16),
        compiler_params=_params(1),
    )(jnp.reshape(2 * x + y, (1,)).astype(jnp.int32), w, *more)


def _to_bf16(x, name, after=None):
    tm = 256

    def body(x_ref, o_ref):
        o_ref[...] = x_ref[...].astype(BF16)

    spec = pl.BlockSpec((tm, x.shape[1]), lambda i: (i, 0))
    body, more_specs, more = _behind(body, 1, after)
    return pl.pallas_call(
        body, name=name, grid=(x.shape[0] // tm,), in_specs=[spec] + more_specs, out_specs=spec,
        out_shape=jax.ShapeDtypeStruct(x.shape, BF16), compiler_params=_params(1),
    )(x, *more)


HBM = pl.BlockSpec(memory_space=pltpu.HBM)
SEM = pl.BlockSpec(memory_space=pltpu.SEMAPHORE)
EFFECT = pltpu.SideEffectType.DATAFLOW_SIDE_EFFECTING


def _comm_call(name, body, bufs, sems_in, sems_out, after=None, token=False):
    nb, ns, no = len(bufs), len(sems_in), len(sems_out)
    n_in = nb + ns + (after is not None)

    def wrapped(*refs):
        body(refs[:nb], refs[nb:nb + ns], refs[n_in + nb:n_in + nb + no])
        if token:
            refs[-1][...] = jnp.zeros((8, 128), F32)

    outs = pl.pallas_call(
        wrapped, name=name,
        in_specs=[HBM] * nb + [SEM] * ns + ([ANY] if after is not None else []),
        out_specs=[HBM] * nb + [SEM] * no + ([pl.BlockSpec(memory_space=pltpu.VMEM)] if token else []),
        out_shape=[pltpu.HBM(b.shape, b.dtype) for b in bufs] + [pltpu.SemaphoreType.DMA((k,)) for k in sems_out]
        + ([jax.ShapeDtypeStruct((8, 128), F32)] if token else []),
        input_output_aliases={i: i for i in range(nb)},
        compiler_params=pltpu.CompilerParams(has_side_effects=EFFECT),
    )(*[pltpu.with_memory_space_constraint(b, pltpu.HBM) for b in bufs], *sems_in, *([after] if after is not None else []))
    return list(outs[:nb]), list(outs[nb:nb + no]), (outs[-1] if token else None)


RING_STAGES = {"ici_near": 2, "ici_far": 2, "d2d_near": 2, "d2d_far": 1}


def _ring_copies(buf, send_sems, recv_sems, k0, stage):
    x, y, c, _ = _position()
    hr = buf.shape[1] // 2
    qr = hr // 2
    half = lambda chip, h: buf.at[chip, pl.ds(h * hr, hr), :]
    quarter = lambda chip, h, q: buf.at[chip, pl.ds(h * hr + q * qr, qr), :]
    mine, x_chip, y_chip, far_chip = 2 * x + y, 2 * (1 - x) + y, 2 * x + (1 - y), 2 * (1 - x) + (1 - y)
    to_x, to_y, sibling = (1 - x, y, c), (x, 1 - y, c), (x, y, 1 - c)
    if stage == "ici_near":
        moves = [(half(mine, c), to_x, half(x_chip, c)), (half(mine, c), to_y, half(y_chip, c))]
    elif stage == "ici_far":
        moves = [(quarter(x_chip, c, 0), to_y, quarter(far_chip, c, 0)),
                 (quarter(y_chip, c, 1), to_x, quarter(far_chip, c, 1))]
    elif stage == "d2d_near":
        moves = [(half(x_chip, c), sibling, half(x_chip, 1 - c)), (half(y_chip, c), sibling, half(y_chip, 1 - c))]
    else:
        moves = [(half(far_chip, c), sibling, half(far_chip, 1 - c))]
    sends = [_remote(src, src, send_sems, recv_sems, k0 + i, to) for i, (src, to, _) in enumerate(moves)]
    arrivals = [_remote(got, got, send_sems, recv_sems, k0 + i, (x, y, c)) for i, (_, _, got) in enumerate(moves)]
    return sends, arrivals


def _ring_call(name, groups, actions, after=None):
    tags = list(dict.fromkeys(t for _, t, _ in actions))
    counts = {t: len(groups[t]["bufs"]) for t in tags}
    first = {t: sum(counts[u] for u in tags[:i]) for i, t in enumerate(tags)}
    waits = [(t, s) for v, t, s in actions if v == "wait"]
    starts = [(t, s) for v, t, s in actions if v == "start"]

    def body(bufs, sems_in, sems_out):
        for verb, t, s in actions:
            at, sems = (starts.index((t, s)), sems_out) if verb == "start" else (waits.index((t, s)), sems_in)
            for w in range(counts[t]):
                sends, arrivals = _ring_copies(bufs[first[t] + w], sems[2 * at], sems[2 * at + 1], RING_STAGES[s] * w, s)
                if verb == "start":
                    for cp in sends:
                        cp.start()
                else:
                    for cp in arrivals:
                        cp.wait_recv()
                    for cp in sends:
                        cp.wait_send()

    bufs, sems, token = _comm_call(
        name, body, [b for t in tags for b in groups[t]["bufs"]],
        [sem for t, s in waits for sem in groups[t]["sems"][s]],
        [RING_STAGES[s] * counts[t] for t, s in starts for _ in (0, 1)], after, token=True)
    for t in tags:
        groups[t]["bufs"] = bufs[first[t]:first[t] + counts[t]]
    for t, s in waits:
        del groups[t]["sems"][s]
    for i, (t, s) in enumerate(starts):
        groups[t]["sems"][s] = (sems[2 * i], sems[2 * i + 1])
    return token


def _cx_copies(src, dst, send_sems, recv_sems, k0):
    x, y, c, chips = _position()
    sends = [_remote(src.at[2 * cx + cy], dst.at[2 * x + y], send_sems, recv_sems, k0 + j, (cx, cy, c))
             for j, (cx, cy) in enumerate(chips)]
    arrivals = [_remote(dst.at[2 * cx + cy], dst.at[2 * cx + cy], send_sems, recv_sems, k0 + j, (x, y, c))
                for j, (cx, cy) in enumerate(chips)]
    return sends, arrivals


def _cx_start(name, pair_sums):
    n = len(pair_sums)
    landing = [lax.empty(p.shape, p.dtype) for p in pair_sums]

    def body(bufs, _, sems):
        for w in range(n):
            for cp in _cx_copies(bufs[w], bufs[n + w], sems[0], sems[1], 3 * w)[0]:
                cp.start()

    bufs, sems, token = _comm_call(name, body, list(pair_sums) + landing, [], [3 * n, 3 * n], token=True)
    return (bufs, sems), token


def _cx_wait(name, state, after):
    bufs, sems = state
    n = len(bufs) // 2

    def body(refs, sems_in, _):
        for w in range(n):
            sends, arrivals = _cx_copies(refs[w], refs[n + w], sems_in[0], sems_in[1], 3 * w)
            for cp in arrivals:
                cp.wait_recv()
            for cp in sends:
                cp.wait_send()

    bufs, _, _ = _comm_call(name, body, bufs, sems, [], after)
    return bufs[:n], bufs[n:]


def _px_copies(src, dst, send_sems, recv_sems, k):
    x, y, c, _ = _position()
    hr = src.shape[1] // 2
    send = _remote(src.at[:, pl.ds((1 - c) * hr, hr), :], dst, send_sems, recv_sems, k, (x, y, 1 - c))
    arrival = _remote(dst, dst, send_sems, recv_sems, k, (x, y, c))
    return send, arrival


def _px_start(name, grads):
    n = len(grads)
    landing = [lax.empty((N_CHIPS, g.shape[1] // 2, g.shape[2]), F32) for g in grads]

    def body(bufs, _, sems):
        for w in range(n):
            _px_copies(bufs[w], bufs[n + w], sems[0], sems[1], w)[0].start()

    bufs, sems, token = _comm_call(name, body, list(grads) + landing, [], [n, n], token=True)
    return (bufs, sems), token


def _px_wait(name, state, after):
    bufs, sems = state
    n = len(bufs) // 2

    def body(refs, sems_in, _):
        for w in range(n):
            send, arrival = _px_copies(refs[w], refs[n + w], sems_in[0], sems_in[1], w)
            arrival.wait_recv()
            send.wait_send()

    bufs, _, _ = _comm_call(name, body, bufs, sems, [], after)
    return bufs[:n], bufs[n:]


def _pair_sum(grad, got, name):
    _, rows, cols = grad.shape
    hr = rows // 2
    tm = min(hr, 256)
    nb = hr // tm
    c = lax.axis_index("c")

    def body(c_ref, g_ref, o_ref, out_ref):
        out_ref[...] = (g_ref[...] + o_ref[...]).astype(BF16)

    return pl.pallas_call(
        body, name=name,
        grid_spec=pltpu.PrefetchScalarGridSpec(
            num_scalar_prefetch=1, grid=(N_CHIPS, nb),
            in_specs=[pl.BlockSpec((None, tm, cols), lambda s, i, c_ref: (s, c_ref[0] * nb + i, 0)),
                      pl.BlockSpec((None, tm, cols), lambda s, i, c_ref: (s, i, 0))],
            out_specs=pl.BlockSpec((None, tm, cols), lambda s, i, c_ref: (s, i, 0))),
        out_shape=jax.ShapeDtypeStruct((N_CHIPS, hr, cols), BF16),
        compiler_params=_params(2),
    )(jnp.reshape(c, (1,)).astype(jnp.int32), grad, got)


def _chip_sum(parts, pair_sums, name):
    _, hr, cols = parts.shape
    tm = min(hr, 256)
    nb = hr // tm
    x, y, c = lax.axis_index("x"), lax.axis_index("y"), lax.axis_index("c")

    def body(pos_ref, p_ref, own_ref, o_ref):
        chip = pos_ref[0]
        own = own_ref[...].astype(F32)
        term = lambda s: jnp.where(chip == s, own, p_ref[s].astype(F32))
        o_ref[...] = ((term(0) + term(1)) + term(2)) + term(3)

    return pl.pallas_call(
        body, name=name,
        grid_spec=pltpu.PrefetchScalarGridSpec(
            num_scalar_prefetch=1, grid=(nb,),
            in_specs=[pl.BlockSpec((N_CHIPS, tm, cols), lambda i, pos: (0, i, 0)),
                      pl.BlockSpec((None, tm, cols), lambda i, pos: (pos[0], i, 0))],
            out_specs=pl.BlockSpec((tm, cols), lambda i, pos: (pos[1] * nb + i, 0))),
        out_shape=jax.ShapeDtypeStruct((2 * hr, cols), F32), compiler_params=_params(1),
    )(jnp.stack([2 * x + y, c]).astype(jnp.int32), parts, pair_sums)


def _share_halves(bufs, name):
    n = len(bufs)

    def body(*refs):
        outs = refs[n:2 * n]
        send_sems, recv_sems = refs[2 * n:]
        x, y, c, _ = _position()
        copies = []
        for w in range(n):
            hr = outs[w].shape[0] // 2
            mine = outs[w].at[pl.ds(c * hr, hr), :]
            cp = _remote(mine, mine, send_sems, recv_sems, w, (x, y, 1 - c))
            cp.start()
            copies.append(cp)
        for w in range(n):
            hr = outs[w].shape[0] // 2
            theirs = outs[w].at[pl.ds((1 - c) * hr, hr), :]
            _remote(theirs, theirs, send_sems, recv_sems, w, (x, y, c)).wait_recv()
        for cp in copies:
            cp.wait_send()

    return pl.pallas_call(
        body, name=name,
        in_specs=[ANY] * n, out_specs=[ANY] * n,
        out_shape=[jax.ShapeDtypeStruct(b.shape, b.dtype) for b in bufs],
        input_output_aliases={w: w for w in range(n)},
        scratch_shapes=[pltpu.SemaphoreType.DMA((n,)), pltpu.SemaphoreType.DMA((n,))],
    )(*bufs)


def _allreduce_small(g):
    rows = g.shape[0]

    def body(g_ref, o_ref, sib, slots, send_sems, recv_sems):
        x, y, c, chips = _position()
        me = (x, y, c)
        my_chip = 2 * x + y
        pair = _remote(g_ref, sib, send_sems, recv_sems, 0, (x, y, 1 - c))
        pair.start()
        pair.wait()
        slots[my_chip] = g_ref[...] + sib[...]
        sent = []
        for j, (cx, cy) in enumerate(chips):
            cp = _remote(slots.at[my_chip], slots.at[my_chip], send_sems, recv_sems, 1 + j, (cx, cy, c))
            cp.start()
            sent.append(cp)
        for j, (cx, cy) in enumerate(chips):
            got = slots.at[2 * cx + cy]
            _remote(got, got, send_sems, recv_sems, 1 + j, me).wait_recv()
        for cp in sent:
            cp.wait_send()
        o_ref[...] = ((slots[0] + slots[1]) + slots[2]) + slots[3]

    vm = pl.BlockSpec(memory_space=pltpu.VMEM)
    return pl.pallas_call(
        body, name="allreduce_small",
        in_specs=[vm], out_specs=vm, out_shape=jax.ShapeDtypeStruct((rows, 128), F32),
        scratch_shapes=[pltpu.VMEM((rows, 128), F32), pltpu.VMEM((N_CHIPS, rows, 128), F32),
                        pltpu.SemaphoreType.DMA((4,)), pltpu.SemaphoreType.DMA((4,))],
        compiler_params=pltpu.CompilerParams(vmem_limit_bytes=VMEM_LIMIT),
    )(g)


_SMALL =("rel_bias", "ln_v_gain", "ln_v_bias", "w_spatial", "b_spatial", "ln1_gain", "ln1_bias",
          "b_ff1", "b_ff2", "ln2_gain", "ln2_bias")
_SMALL_ROWS = 1200
_LOSS_AT = (152832 // 128, 0)


def _pack_small(parts):
    flat = jnp.concatenate([parts[k].reshape(-1).astype(F32) for k in _SMALL])
    flat = jnp.pad(flat, (0, _SMALL_ROWS * 128 - flat.shape[0]))
    return flat.reshape(_SMALL_ROWS, 128)


def _unpack_small(packed, like):
    flat = packed.reshape(-1)
    out, at = {}, 0
    for k in _SMALL:
        n = math.prod(like[k].shape)
        out[k] = flat[at:at + n].reshape(like[k].shape)
        at += n
    return out


def kernel(x, w_in, rel_bias, ln_v_gain, ln_v_bias, w_spatial, b_spatial, w_proj_a, w_proj_b, w_out, ln1_gain, ln1_bias, w_ff1, b_ff1, w_ff2, b_ff2, ln2_gain, ln2_bias, loss_target, m_w_in, m_rel_bias, m_ln_v_gain, m_ln_v_bias, m_w_spatial, m_b_spatial, m_w_proj_a, m_w_proj_b, m_w_out, m_ln1_gain, m_ln1_bias, m_w_ff1, m_b_ff1, m_w_ff2, m_b_ff2, m_ln2_gain, m_ln2_bias, v_w_in, v_rel_bias, v_ln_v_gain, v_ln_v_bias, v_w_spatial, v_b_spatial, v_w_proj_a, v_w_proj_b, v_w_out, v_ln1_gain, v_ln1_bias, v_w_ff1, v_b_ff1, v_w_ff2, v_b_ff2, v_ln2_gain, v_ln2_bias):
    args = dict(locals())
    big = ("w_in", "w_proj_a", "w_proj_b", "w_out", "w_ff1", "w_ff2")
    weights = ("w_in", "rel_bias", "ln_v_gain", "ln_v_bias", "w_spatial", "b_spatial", "w_proj_a", "w_proj_b", "w_out",
               "ln1_gain", "ln1_bias", "w_ff1", "b_ff1", "w_ff2", "b_ff2", "ln2_gain", "ln2_bias")

    xs = x[0]
    target = loss_target[0]

    ring = {"a": {"bufs": [_place_shard(w_in[0], "place_w_in")], "sems": {}}}
    tok = _ring_call("allgather_a_near", ring, [("start", "a", "ici_near")])
    placed = [_place_shard(args[k][0], f"place_{k}", after=tok) for k in big[1:]]
    for tag, bufs in (("b", placed[0:3]), ("c", placed[3:4]), ("d", placed[4:5])):
        ring[tag] = {"bufs": bufs, "sems": {}}
    xb = _to_bf16(xs, "x_to_bf16", after=tok)
    _ring_call("allgather_a_far", ring, [("wait", "a", "ici_near"), ("start", "a", "ici_far"), ("start", "a", "d2d_near"),
                                         ("start", "b", "ici_near"), ("start", "c", "ici_near"), ("start", "d", "ici_near")],
               after=xb)
    _ring_call("allgather_a_last", ring, [("wait", "a", "ici_far"), ("start", "a", "d2d_far")])
    _ring_call("allgather_a_done", ring, [("wait", "a", "d2d_near"), ("wait", "a", "d2d_far")])
    (win_g,) = ring["a"]["bufs"]

    proj = _proj(xb, win_g)
    _ring_call("allgather_b_far", ring, [("wait", "b", "ici_near"), ("start", "b", "ici_far"), ("start", "b", "d2d_near")],
               after=proj)
    ws = w_spatial[0]
    ws_t = jnp.transpose(ws, (0, 2, 1))
    bsp_b = jnp.broadcast_to(b_spatial[0][:, :, None], (NH, 128, 128))
    gmlp = _gmlp_fwd(proj, ws, bsp_b, ln_v_gain, ln_v_bias)
    attn, lse = _attention_fwd(proj, rel_bias)
    _ring_call("allgather_b_last_cd_far", ring,
               [("wait", "b", "ici_far"), ("start", "b", "d2d_far"),
                ("wait", "c", "ici_near"), ("start", "c", "ici_far"), ("start", "c", "d2d_near"),
                ("wait", "d", "ici_near"), ("start", "d", "ici_far"), ("start", "d", "d2d_near")], after=attn)
    _ring_call("allgather_b_done", ring, [("wait", "b", "d2d_near"), ("wait", "b", "d2d_far")])
    wpa_g, wpb_g, wout_g = ring["b"]["bufs"]
    wout_full = wout_g.reshape(D, D)
    ya, yb, merged = _branch(attn, gmlp, wpa_g, wpb_g, proj)
    xhat1, rstd1, h1b = _out_ln1(merged, wout_full, xs, ln1_gain, ln1_bias)
    _ring_call("allgather_cd_last", ring, [("wait", "c", "ici_far"), ("start", "c", "d2d_far"),
                                           ("wait", "d", "ici_far"), ("start", "d", "d2d_far")], after=h1b)
    _ring_call("allgather_c_done", ring, [("wait", "c", "d2d_near"), ("wait", "c", "d2d_far")])
    (w1_g,) = ring["c"]["bufs"]
    a, r = _ff1(h1b, w1_g, b_ff1)
    _ring_call("allgather_d_done", ring, [("wait", "d", "d2d_near"), ("wait", "d", "d2d_far")], after=a)
    (w2_g,) = ring["d"]["bufs"]
    w2_full = w2_g.reshape(DFF, D)
    dpre2, dpre2b, st2 = _ff2_ln2_loss(a, w2_full, xhat1, ln1_gain, ln1_bias, b_ff2, ln2_gain, ln2_bias, target)

    def pair_and_chip(tag, state, after):
        local, from_sibling = _px_wait(f"pair_exchange_wait_{tag}", state, after)
        pair_sums = [_pair_sum(g, o, f"pair_sum_{tag}_{i}") for i, (g, o) in enumerate(zip(local, from_sibling))]
        return _cx_start(f"chip_exchange_start_{tag}", pair_sums)

    g_w2 = _grad_w(a, dpre2b, "grad_w_ff2", 512, 2048, False)
    px, tok = _px_start("pair_exchange_start_w_ff2", [g_w2.reshape(N_CHIPS, DFF // N_CHIPS, D)])
    dprea, g_b1 = _d_ff1(dpre2b, w2_full, r, after=tok)
    cx_w2, tok = pair_and_chip("w_ff2", px, dprea)
    g_w1 = _grad_w(h1b, dprea, "grad_w_ff1", 512, 2048, True, after=tok)
    px, tok = _px_start("pair_exchange_start_w_ff1", [g_w1])
    dpre1, dpre1b, st1 = _d_h1_ln1(dprea, w1_g, dpre2, xhat1, rstd1, ln1_gain, after=tok)
    cx_w1, tok = pair_and_chip("w_ff1", px, dpre1b)
    g_wout = _grad_w(merged, dpre1b, "grad_w_out", 512, 2048, False, after=tok)
    dya, dyb, dga, dgb = _d_merged(dpre1b, wout_full, proj, ya, yb)
    g_wpa = _grad_w(attn, dya, "grad_w_proj_a", 1024, 512, True)
    g_wpb = _grad_w(gmlp, dyb, "grad_w_proj_b", 1024, 512, True)
    px, tok = _px_start("pair_exchange_start_b", [g_wpa, g_wpb, g_wout.reshape(N_CHIPS, D // N_CHIPS, D)])
    dattn, dgmlp = _d_branches(dya, dyb, wpa_g, wpb_g, after=tok)
    duv, g_ws, g_bs, stv = _gmlp_bwd(proj, dgmlp, ws, ws_t, bsp_b, ln_v_gain, ln_v_bias)
    cx_b, tok = pair_and_chip("b", px, duv)
    dq, dk, dv, ds_sums = _attention_bwd(proj, dattn, attn, lse, rel_bias, after=tok)
    g_rb = _rel_bias_grad(ds_sums)[:, :NH]

    small_g = dict(rel_bias=g_rb, ln_v_gain=stv[0], ln_v_bias=stv[1], w_spatial=g_ws, b_spatial=g_bs[:, :, 0],
                   ln1_gain=st1[0], ln1_bias=st1[1], b_ff1=g_b1, b_ff2=st2[2], ln2_gain=st2[0], ln2_bias=st2[1])
    gs = _allreduce_small(_pack_small(small_g).at[_LOSS_AT].set(st2[3, 0]))
    ds_, ms_, vs_, _ = _adamw(_pack_small({k: args[k] for k in _SMALL}), gs,
                           _pack_small({k: args["m_" + k] for k in _SMALL}),
                           _pack_small({k: args["v_" + k] for k in _SMALL}), "adamw_small")
    like = {k: args[k] for k in _SMALL}
    grads, deltas, new_m, new_v = (_unpack_small(t, like) for t in (gs, ds_, ms_, vs_))

    dproj = jnp.concatenate([dq, dk, dv, duv, dga, dgb], axis=1)
    g_win = _grad_w(xb, dproj, "grad_w_in", 512, 2304, True, after=gs)
    px, tok = _px_start("pair_exchange_start_w_in", [g_win])
    grad_x = _d_x(dproj, win_g, dpre1, after=tok)
    cx_in, tok = pair_and_chip("w_in", px, grad_x)

    def reduce_finish(tag, state, names, after):
        pair_sums, from_chips = _cx_wait(f"chip_exchange_wait_{tag}", state, after)
        halves = [_chip_sum(p, own, f"chip_sum_{k}") for p, own, k in zip(from_chips, pair_sums, names)]
        last = None
        for k, g in zip(names, _share_halves(halves, f"share_halves_{tag}")):
            d_, m_, v_, g_ = _adamw(args[k][0], g, args["m_" + k][0], args["v_" + k][0], f"adamw_{k}")
            grads[k], deltas[k], new_m[k], new_v[k] = g_[None], d_[None], m_[None], v_[None]
            last = d_
        return last

    done = reduce_finish("w_ff2", cx_w2, ["w_ff2"], tok)
    done = reduce_finish("w_ff1", cx_w1, ["w_ff1"], done)
    done = reduce_finish("b", cx_b, ["w_proj_a", "w_proj_b", "w_out"], done)
    reduce_finish("w_in", cx_in, ["w_in"], done)

    loss = gs[_LOSS_AT] * (0.5 / D)
    return (loss, grad_x[None], *[grads[k] for k in weights], *[deltas[k] for k in weights],
            *[new_m[k] for k in weights], *[new_v[k] for k in weights])
```

```python
import functools
import math

import numpy as np
import jax
import jax.numpy as jnp
from jax import lax
from jax.experimental import pallas as pl
from jax.experimental.pallas import tpu as pltpu

F32 = jnp.float32
BF16 = jnp.bfloat16

S = 2048
D = 2048
DA = 1024
DB = 1024
DFF = 8192
DIN = 9216
NH = 8
HD = 128
NBLK = 16
PATTERNS = ((128, 1), (512, 4), (2048, 16))
N_BUCKETS = 32
MAX_DISTANCE = 2048
ALPHA = 2.0 ** 0.25
LN_EPS = 1e-5
NEG_INF = -1e30
SCALE = HD ** -0.5
N_CHIPS = 4

ADAM_LR = 0.001
ADAM_B1 = 0.9
ADAM_B2 = 0.999
ADAM_EPS = 1e-08
ADAM_WD = 0.01
ADAM_STEP = 10

VMEM_LIMIT = 56 * 1024 * 1024
MESH = pl.DeviceIdType.MESH
ANY = pl.BlockSpec(memory_space=pl.ANY)


def _params(n_axes, vmem=VMEM_LIMIT):
    return pltpu.CompilerParams(dimension_semantics=("arbitrary",) * n_axes, vmem_limit_bytes=vmem)


def _bucket_tile(dilation):
    qi = np.arange(128)[:, None]
    kj = np.arange(256)[None, :]
    n = np.clip(128 + qi - kj, 0, 128) * dilation
    max_exact = N_BUCKETS // 2
    nf = np.maximum(n, 1).astype(np.float32)
    large = max_exact + (np.log(nf / np.float32(max_exact)) / np.float32(math.log(MAX_DISTANCE / max_exact))
                         * np.float32(N_BUCKETS - max_exact)).astype(np.int32)
    large = np.minimum(large, N_BUCKETS - 1)
    return np.where(n < max_exact, n, large).astype(np.int32)


def _gelu(x):
    c = math.sqrt(2.0 / math.pi)
    t = jnp.tanh(c * (x + 0.044715 * x * x * x))
    return 0.5 * x * (1.0 + t), t


def _gelu_grad(x, t):
    c = math.sqrt(2.0 / math.pi)
    return 0.5 * (1.0 + t) + 0.5 * x * (1.0 - t * t) * c * (1.0 + 3.0 * 0.044715 * x * x)


def _sigmoid(x):
    return 1.0 / (1.0 + jnp.exp(-x))


def _dot(a, b):
    return jnp.dot(a, b, preferred_element_type=F32)


def _behind(body, n_in, after):
    if after is None:
        return body, [], []
    return (lambda *refs: body(*refs[:n_in], *refs[n_in + 1:])), [ANY], [after]


def _dot_nt(a, b):
    return lax.dot_general(a, b, (((1,), (1,)), ((), ())), preferred_element_type=F32)


def _proj(xb, win_g):
    tn = 768
    per = 2304 // tn

    def body(x_ref, w_ref, o_ref):
        o_ref[...] = _dot(x_ref[...], w_ref[...])

    return pl.pallas_call(
        body, name="proj", grid=(DIN // tn,),
        in_specs=[pl.BlockSpec((S, D), lambda j: (0, 0)),
                  pl.BlockSpec((None, D, tn), lambda j: (j // per, 0, j % per))],
        out_specs=pl.BlockSpec((S, tn), lambda j: (0, j)),
        out_shape=jax.ShapeDtypeStruct((S, DIN), F32),
        compiler_params=_params(1),
    )(xb, win_g)


FWD_HEADS_PER_STEP = 4
BWD_HEADS_PER_STEP = 2


def _head_bias_tiles(rb_ref, bk_ref, bias_scr, first_head, hps):
    qi = lax.broadcasted_iota(jnp.int32, (128, 256), 0)
    kj = lax.broadcasted_iota(jnp.int32, (128, 256), 1)
    steps = 128 + qi - kj
    band = (steps >= 0) & (steps <= 128)
    bias_scr[...] = jnp.zeros_like(bias_scr)
    for p in range(len(PATTERNS)):
        bucket = bk_ref[p]

        def one_bucket(t, carry):
            hit = bucket == t
            for j in range(hps):
                bias_scr[p, j] = jnp.where(hit, rb_ref[t, first_head + j], bias_scr[p, j])
            return carry

        lax.fori_loop(0, N_BUCKETS, one_bucket, 0)
        for j in range(hps):
            bias_scr[p, j] = jnp.where(band, bias_scr[p, j], NEG_INF)


def _block_rows(b, dilation):
    nblk = NBLK // dilation
    r, n = b // nblk, b % nblk
    start = r + n * (128 * dilation)
    prev_start = jnp.maximum(start - 128 * dilation, r)
    if dilation == 1:
        return pl.ds(pl.multiple_of(start, 128), 128), pl.ds(pl.multiple_of(prev_start, 128), 128), n > 0
    return pl.ds(start, 128, stride=dilation), pl.ds(prev_start, 128, stride=dilation), n > 0


def _head_specs(first, hps):
    return [pl.BlockSpec((S, HD), lambda g, j=j: (0, first + g * hps + j)) for j in range(hps)]


def _heads_spec(hps):
    return pl.BlockSpec((S, hps * HD), lambda g: (0, g))


def _attention_fwd(proj, rel_bias):
    hps = FWD_HEADS_PER_STEP
    buckets = jnp.asarray(np.stack([_bucket_tile(d) for _, d in PATTERNS]))

    def body(rb_ref, bk_ref, *refs):
        q_refs, k_refs, v_refs = (refs[i * hps:(i + 1) * hps] for i in range(3))
        o_ref, lse_ref, bias_scr = refs[3 * hps:3 * hps + 3]
        acc_scrs, m_scrs, l_scrs = (refs[3 * hps + 3 + i * hps:3 * hps + 3 + (i + 1) * hps] for i in range(3))
        _head_bias_tiles(rb_ref, bk_ref, bias_scr, pl.program_id(0) * hps, hps)
        kj = lax.broadcasted_iota(jnp.int32, (128, 256), 1)
        for p, (_, d) in enumerate(PATTERNS):
            prev_blocks = NBLK // d > 1

            def block(b, carry):
                rows, prows, has_prev = _block_rows(b, d)
                key_ok = (kj >= 128) | has_prev
                scores = []
                for j in range(hps):
                    q = q_refs[j][rows, :].astype(BF16)
                    cur = _dot_nt(q, k_refs[j][rows, :].astype(BF16))
                    if prev_blocks:
                        cur = jnp.concatenate([_dot_nt(q, k_refs[j][prows, :].astype(BF16)), cur], axis=1)
                    scores.append(cur)
                soft = []
                for j in range(hps):
                    if prev_blocks:
                        s = jnp.where(key_ok, scores[j] * SCALE + bias_scr[p, j], NEG_INF)
                    else:
                        s = scores[j] * SCALE + bias_scr[p, j, :, 128:256]
                    m = jnp.max(s, axis=1, keepdims=True)
                    e = jnp.exp(s - m)
                    soft.append((m, jnp.sum(e, axis=1, keepdims=True), e.astype(BF16)))
                outs = []
                for j in range(hps):
                    e = soft[j][2]
                    if prev_blocks:
                        outs.append(_dot(e[:, :128], v_refs[j][prows, :].astype(BF16))
                                    + _dot(e[:, 128:], v_refs[j][rows, :].astype(BF16)))
                    else:
                        outs.append(_dot(e, v_refs[j][rows, :].astype(BF16)))
                for j in range(hps):
                    acc_scr, m_scr, l_scr = acc_scrs[j], m_scrs[j], l_scrs[j]
                    (m, den, _), o = soft[j], outs[j]
                    if p == 0:
                        acc_scr[rows, :] = o
                        m_scr[rows, :] = jnp.broadcast_to(m, (128, HD))
                        l_scr[rows, :] = jnp.broadcast_to(den, (128, HD))
                    else:
                        m_old = m_scr[rows, :]
                        m_new = jnp.maximum(m_old, m)
                        w_old, w_new = jnp.exp(m_old - m_new), jnp.exp(m - m_new)
                        acc_scr[rows, :] = acc_scr[rows, :] * w_old + o * w_new
                        l_scr[rows, :] = l_scr[rows, :] * w_old + den * w_new
                        m_scr[rows, :] = m_new
                return carry

            lax.fori_loop(0, NBLK, block, 0)
        for j in range(hps):
            cols = slice(j * HD, (j + 1) * HD)
            den = l_scrs[j][...]
            o_ref[:, cols] = (acc_scrs[j][...] / den).astype(BF16)
            lse_ref[:, cols] = m_scrs[j][...] + jnp.log(den)

    return pl.pallas_call(
        body, name="attention_fwd", grid=(NH // hps,),
        in_specs=[pl.BlockSpec(memory_space=pltpu.SMEM), pl.BlockSpec((3, 128, 256), lambda g: (0, 0, 0))]
        + _head_specs(0, hps) + _head_specs(NH, hps) + _head_specs(2 * NH, hps),
        out_specs=[_heads_spec(hps), _heads_spec(hps)],
        out_shape=[jax.ShapeDtypeStruct((S, DA), BF16), jax.ShapeDtypeStruct((S, DA), F32)],
        scratch_shapes=[pltpu.VMEM((3, hps, 128, 256), F32)] + [pltpu.VMEM((S, HD), F32)] * (3 * hps),
        compiler_params=_params(1),
    )(rel_bias, buckets, *([proj] * (3 * hps)))


def _attention_bwd(proj, dattn, attn, lse, rel_bias, after=None):
    hps = BWD_HEADS_PER_STEP

    def body(rb_ref, bk_ref, *refs):
        q_refs, k_refs, v_refs, do_refs, o_refs, lse_refs = (refs[i * hps:(i + 1) * hps] for i in range(6))
        dq_ref, dk_ref, dv_ref, ds_ref, bias_scr = refs[6 * hps:6 * hps + 5]
        dl_scrs, dq_scrs, dk_scrs, dv_scrs = (refs[6 * hps + 5 + i * hps:6 * hps + 5 + (i + 1) * hps] for i in range(4))
        _head_bias_tiles(rb_ref, bk_ref, bias_scr, pl.program_id(0) * hps, hps)
        ds_ref[...] = jnp.zeros_like(ds_ref)
        for j in range(hps):
            dq_scrs[j][...] = jnp.zeros((S, HD), F32)
            dk_scrs[j][...] = jnp.zeros((S, HD), F32)
            dv_scrs[j][...] = jnp.zeros((S, HD), F32)
            prod = do_refs[j][...] * o_refs[j][...].astype(F32)
            dl_scrs[j][...] = jnp.broadcast_to(jnp.sum(prod, axis=1, keepdims=True), (S, HD))
        for p, (_, d) in enumerate(PATTERNS):
            prev_blocks = NBLK // d > 1

            def block(b, carry):
                rows, prows, has_prev = _block_rows(b, d)
                ops, raw = [], []
                for j in range(hps):
                    q, do = q_refs[j][rows, :].astype(BF16), do_refs[j][rows, :].astype(BF16)
                    kc, vc = k_refs[j][rows, :].astype(BF16), v_refs[j][rows, :].astype(BF16)
                    if prev_blocks:
                        kp, vp = k_refs[j][prows, :].astype(BF16), v_refs[j][prows, :].astype(BF16)
                        ops.append((q, do, kc, kp))
                        raw.append((_dot_nt(q, kc), _dot_nt(do, vc), _dot_nt(q, kp), _dot_nt(do, vp)))
                    else:
                        ops.append((q, do, kc))
                        raw.append((_dot_nt(q, kc), _dot_nt(do, vc)))
                probs = []
                for j in range(hps):
                    lse_b, dl_b = lse_refs[j][rows, :], dl_scrs[j][rows, :]
                    p_c = jnp.exp(raw[j][0] * SCALE + bias_scr[p, j, :, 128:256] - lse_b)
                    ds_c = p_c * (raw[j][1] - dl_b)
                    ds_ref[p, j, :, 128:256] += ds_c
                    if prev_blocks:
                        p_p = jnp.where(has_prev, jnp.exp(raw[j][2] * SCALE + bias_scr[p, j, :, 0:128] - lse_b), 0.0)
                        ds_p = p_p * (raw[j][3] - dl_b)
                        ds_ref[p, j, :, 0:128] += ds_p
                        probs.append((p_c, ds_c, p_p, ds_p))
                    else:
                        probs.append((p_c, ds_c))
                grads = []
                for j in range(hps):
                    q, do, kc = ops[j][:3]
                    p_c, ds_c = probs[j][:2]
                    dq = _dot(ds_c.astype(BF16), kc)
                    cur = (_dot(ds_c.T.astype(BF16), q) * SCALE, _dot(p_c.T.astype(BF16), do))
                    if prev_blocks:
                        p_p, ds_p = probs[j][2:]
                        dq = dq + _dot(ds_p.astype(BF16), ops[j][3])
                        cur = cur + (_dot(ds_p.T.astype(BF16), q) * SCALE, _dot(p_p.T.astype(BF16), do))
                    grads.append((dq * SCALE,) + cur)
                for j in range(hps):
                    dq_scrs[j][rows, :] += grads[j][0]
                    dk_scrs[j][rows, :] += grads[j][1]
                    dv_scrs[j][rows, :] += grads[j][2]
                    if prev_blocks:
                        dk_scrs[j][prows, :] += grads[j][3]
                        dv_scrs[j][prows, :] += grads[j][4]
                return carry

            lax.fori_loop(0, NBLK, block, 0)
        for j in range(hps):
            cols = slice(j * HD, (j + 1) * HD)
            dq_ref[:, cols] = dq_scrs[j][...].astype(BF16)
            dk_ref[:, cols] = dk_scrs[j][...].astype(BF16)
            dv_ref[:, cols] = dv_scrs[j][...].astype(BF16)

    buckets = jnp.asarray(np.stack([_bucket_tile(d) for _, d in PATTERNS]))
    body, more_specs, more = _behind(body, 2 + 6 * hps, after)
    return pl.pallas_call(
        body, name="attention_bwd", grid=(NH // hps,),
        in_specs=[pl.BlockSpec(memory_space=pltpu.SMEM), pl.BlockSpec((3, 128, 256), lambda g: (0, 0, 0))]
        + _head_specs(0, hps) + _head_specs(NH, hps) + _head_specs(2 * NH, hps) + 3 * _head_specs(0, hps)
        + more_specs,
        out_specs=3 * [_heads_spec(hps)] + [pl.BlockSpec((3, hps, 128, 256), lambda g: (0, g, 0, 0))],
        out_shape=[jax.ShapeDtypeStruct((S, DA), BF16)] * 3 + [jax.ShapeDtypeStruct((3, NH, 128, 256), F32)],
        scratch_shapes=[pltpu.VMEM((3, hps, 128, 256), F32)] + [pltpu.VMEM((S, HD), F32)] * (4 * hps),
        compiler_params=_params(1),
    )(rel_bias, buckets, *([proj] * (3 * hps)), *([dattn] * hps), *([attn] * hps), *([lse] * hps), *more)


def _gmlp_parts(u_ref, vb_ref, g_ref, be_ref):
    u = u_ref[...]
    u_act, tu = _gelu(u)
    vb = vb_ref[...]
    gv, tv = _gelu(vb)
    mean = jnp.mean(gv, axis=1, keepdims=True)
    cen = gv - mean
    var = jnp.mean(cen * cen, axis=1, keepdims=True)
    rstd = lax.rsqrt(var + LN_EPS)
    xhat = cen * rstd
    vn = xhat * g_ref[...] + be_ref[...]
    return u, tu, u_act, vb, tv, rstd, xhat, vn


def _gmlp_fwd(proj, ws, bsp_b, gain_v, bias_v):
    def body(u_ref, vb_ref, ws_ref, bsp_ref, g_ref, be_ref, o_ref):
        _, _, u_act, _, _, _, _, vn = _gmlp_parts(u_ref, vb_ref, g_ref, be_ref)
        row = lax.broadcasted_iota(jnp.int32, (128, 128), 0)
        col = lax.broadcasted_iota(jnp.int32, (128, 128), 1)
        causal = row >= col
        for g in range(NH):
            cols = slice(g * 128, (g + 1) * 128)
            wsg = jnp.where(causal, ws_ref[g], 0.0).astype(BF16)
            z = _dot(wsg, vn[:, cols].astype(BF16)) + bsp_ref[g]
            o_ref[:, cols] = (u_act[:, cols] * z).astype(BF16)

    return pl.pallas_call(
        body, name="gmlp_fwd", grid=(NBLK,),
        in_specs=[pl.BlockSpec((128, DB), lambda c: (c, 3)), pl.BlockSpec((128, DB), lambda c: (c, 4)),
                  pl.BlockSpec((NH, 128, 128), lambda c: (0, 0, 0)), pl.BlockSpec((NH, 128, 128), lambda c: (0, 0, 0)),
                  pl.BlockSpec((1, DB), lambda c: (0, 0)), pl.BlockSpec((1, DB), lambda c: (0, 0))],
        out_specs=pl.BlockSpec((128, DB), lambda c: (c, 0)),
        out_shape=jax.ShapeDtypeStruct((S, DB), BF16),
        compiler_params=_params(1),
    )(proj, proj, ws, bsp_b, gain_v, bias_v)


def _branch(attn, gmlp, wpa_g, wpb_g, proj):
    tn = 512

    def body(a_ref, g_ref, wa_ref, wb_ref, ga_ref, gb_ref, ya_ref, yb_ref, mg_ref):
        ya = _dot(a_ref[...], wa_ref[...])
        yb = _dot(g_ref[...], wb_ref[...])
        ya_ref[...] = ya.astype(BF16)
        yb_ref[...] = yb.astype(BF16)
        mg_ref[...] = (_sigmoid(ga_ref[...]) * ya + _sigmoid(gb_ref[...]) * yb).astype(BF16)

    out = pl.BlockSpec((S, tn), lambda j: (0, j))
    return pl.pallas_call(
        body, name="branch", grid=(D // tn,),
        in_specs=[pl.BlockSpec((S, DA), lambda j: (0, 0)), pl.BlockSpec((S, DB), lambda j: (0, 0)),
                  pl.BlockSpec((None, DA, tn), lambda j: (j, 0, 0)), pl.BlockSpec((None, DB, tn), lambda j: (j, 0, 0)),
                  pl.BlockSpec((S, tn), lambda j: (0, 5120 // tn + j)), pl.BlockSpec((S, tn), lambda j: (0, 7168 // tn + j))],
        out_specs=[out, out, out],
        out_shape=[jax.ShapeDtypeStruct((S, D), BF16)] * 3,
        compiler_params=_params(1),
    )(attn, gmlp, wpa_g, wpb_g, proj, proj)


def _out_ln1(merged, wout_g, x, gain, bias):
    tm = 256

    def body(m_ref, w_ref, x_ref, g_ref, b_ref, xh_ref, rs_ref, h_ref):
        pre = ALPHA * x_ref[...] + _dot(m_ref[...], w_ref[...])
        mean = jnp.mean(pre, axis=1, keepdims=True)
        cen = pre - mean
        var = jnp.mean(cen * cen, axis=1, keepdims=True)
        rstd = lax.rsqrt(var + LN_EPS)
        xhat = cen * rstd
        xh_ref[...] = xhat
        rs_ref[...] = jnp.broadcast_to(rstd, (tm, 128))
        h_ref[...] = (xhat * g_ref[...] + b_ref[...]).astype(BF16)

    row = pl.BlockSpec((tm, D), lambda i: (i, 0))
    vec = pl.BlockSpec((1, D), lambda i: (0, 0))
    return pl.pallas_call(
        body, name="out_ln1", grid=(S // tm,),
        in_specs=[row, pl.BlockSpec((D, D), lambda i: (0, 0)), row, vec, vec],
        out_specs=[row, pl.BlockSpec((tm, 128), lambda i: (i, 0)), row],
        out_shape=[jax.ShapeDtypeStruct((S, D), F32), jax.ShapeDtypeStruct((S, 128), F32),
                   jax.ShapeDtypeStruct((S, D), BF16)],
        compiler_params=_params(1),
    )(merged, wout_g, x, gain, bias)


def _ff1(h1b, w1_g, b1):
    tn = 512
    per = D // tn

    def body(h_ref, w_ref, b_ref, a_ref, r_ref):
        r = jnp.maximum(_dot(h_ref[...], w_ref[...]) + b_ref[...], 0.0)
        r_ref[...] = r.astype(BF16)
        a_ref[...] = (r * r).astype(BF16)

    out = pl.BlockSpec((S, tn), lambda j: (0, j))
    return pl.pallas_call(
        body, name="ff1", grid=(DFF // tn,),
        in_specs=[pl.BlockSpec((S, D), lambda j: (0, 0)),
                  pl.BlockSpec((None, D, tn), lambda j: (j // per, 0, j % per)),
                  pl.BlockSpec((1, tn), lambda j: (0, j))],
        out_specs=[out, out],
        out_shape=[jax.ShapeDtypeStruct((S, DFF), BF16)] * 2,
        compiler_params=_params(1),
    )(h1b, w1_g, b1)


def _ff2_ln2_loss(a, w2_g, xhat1, g1, b1, b2, g2, be2, target):
    tm, tk = 512, 1024
    nk = DFF // tk

    def body(a_ref, w_ref, xh_ref, g1_ref, b1_ref, b2_ref, g2_ref, be2_ref, t_ref, d_ref, db_ref, st_ref, acc):
        i, k = pl.program_id(0), pl.program_id(1)

        @pl.when(k == 0)
        def _():
            acc[...] = jnp.zeros_like(acc)

        @pl.when((i == 0) & (k == 0))
        def _():
            st_ref[...] = jnp.zeros_like(st_ref)

        acc[...] += _dot(a_ref[...], w_ref[...])

        @pl.when(k == nk - 1)
        def _():
            def rows_chunk(ci, carry):
                rows = pl.ds(pl.multiple_of(ci * 128, 128), 128)
                h1 = xh_ref[rows, :] * g1_ref[...] + b1_ref[...]
                pre = ALPHA * h1 + acc[rows, :] + b2_ref[...]
                mean = jnp.mean(pre, axis=1, keepdims=True)
                cen = pre - mean
                var = jnp.mean(cen * cen, axis=1, keepdims=True)
                rstd = lax.rsqrt(var + LN_EPS)
                xhat = cen * rstd
                y = xhat * g2_ref[...] + be2_ref[...]
                err = y - t_ref[rows, :]
                dy = err * (1.0 / D)
                g = dy * g2_ref[...]
                dpre = rstd * (g - jnp.mean(g, axis=1, keepdims=True)
                               - xhat * jnp.mean(g * xhat, axis=1, keepdims=True))
                d_ref[rows, :] = dpre
                db_ref[rows, :] = dpre.astype(BF16)
                st_ref[0:1, :] += jnp.sum(dy * xhat, axis=0, keepdims=True)
                st_ref[1:2, :] += jnp.sum(dy, axis=0, keepdims=True)
                st_ref[2:3, :] += jnp.sum(dpre, axis=0, keepdims=True)
                st_ref[3:4, :] += jnp.broadcast_to(jnp.sum(err * err).reshape(1, 1), (1, D))
                return carry

            lax.fori_loop(0, tm // 128, rows_chunk, 0)

    row = pl.BlockSpec((tm, D), lambda i, k: (i, 0))
    vec = pl.BlockSpec((1, D), lambda i, k: (0, 0))
    return pl.pallas_call(
        body, name="ff2_ln2_loss", grid=(S // tm, nk),
        in_specs=[pl.BlockSpec((tm, tk), lambda i, k: (i, k)), pl.BlockSpec((tk, D), lambda i, k: (k, 0)),
                  row, vec, vec, vec, vec, vec, row],
        out_specs=[row, row, pl.BlockSpec((8, D), lambda i, k: (0, 0))],
        out_shape=[jax.ShapeDtypeStruct((S, D), F32), jax.ShapeDtypeStruct((S, D), BF16),
                   jax.ShapeDtypeStruct((8, D), F32)],
        scratch_shapes=[pltpu.VMEM((tm, D), F32)],
        compiler_params=_params(2),
    )(a, w2_g, xhat1, g1, b1, b2, g2, be2, target)


def _grad_w(act, dout, name, ti, tj, sharded, after=None):
    m, n = act.shape[1], dout.shape[1]
    ns = n // N_CHIPS
    per = ns // tj if sharded else None

    def body(a_ref, b_ref, o_ref, at_scr):
        @pl.when(pl.program_id(1) == 0)
        def _():
            at_scr[...] = a_ref[...].T

        o_ref[...] = _dot(at_scr[...], b_ref[...])

    if sharded:
        out_spec = pl.BlockSpec((None, ti, tj), lambda i, j: (j // per, i, j % per))
        out_shape = jax.ShapeDtypeStruct((N_CHIPS, m, ns), F32)
    else:
        out_spec = pl.BlockSpec((ti, tj), lambda i, j: (i, j))
        out_shape = jax.ShapeDtypeStruct((m, n), F32)
    body, more_specs, more = _behind(body, 2, after)
    return pl.pallas_call(
        body, name=name, grid=(m // ti, n // tj),
        in_specs=[pl.BlockSpec((S, ti), lambda i, j: (0, i)), pl.BlockSpec((S, tj), lambda i, j: (0, j))] + more_specs,
        out_specs=out_spec, out_shape=out_shape,
        scratch_shapes=[pltpu.VMEM((ti, S), BF16)],
        compiler_params=_params(2),
    )(act, dout, *more)


def _d_ff1(dpre2b, w2_g, r, after=None):
    tn = 512

    def body(d_ref, w_ref, r_ref, o_ref, gb_ref):
        da = _dot_nt(d_ref[...], w_ref[...])
        dp = da * (2.0 * r_ref[...].astype(F32))
        o_ref[...] = dp.astype(BF16)
        gb_ref[...] = jnp.sum(dp, axis=0, keepdims=True)

    body, more_specs, more = _behind(body, 3, after)
    return pl.pallas_call(
        body, name="d_ff1", grid=(DFF // tn,),
        in_specs=[pl.BlockSpec((S, D), lambda j: (0, 0)), pl.BlockSpec((tn, D), lambda j: (j, 0)),
                  pl.BlockSpec((S, tn), lambda j: (0, j))] + more_specs,
        out_specs=[pl.BlockSpec((S, tn), lambda j: (0, j)), pl.BlockSpec((1, tn), lambda j: (0, j))],
        out_shape=[jax.ShapeDtypeStruct((S, DFF), BF16), jax.ShapeDtypeStruct((1, DFF), F32)],
        compiler_params=_params(1),
    )(dpre2b, w2_g, r, *more)


def _d_h1_ln1(dprea, w1_g, dpre2, xhat1, rstd1, g1, after=None):
    tm, tk = 512, 1024
    per = D // tk
    nk = DFF // tk

    def body(a_ref, w_ref, d2_ref, xh_ref, rs_ref, g_ref, d_ref, db_ref, st_ref, acc):
        i, k = pl.program_id(0), pl.program_id(1)

        @pl.when(k == 0)
        def _():
            acc[...] = jnp.zeros_like(acc)

        @pl.when((i == 0) & (k == 0))
        def _():
            st_ref[...] = jnp.zeros_like(st_ref)

        acc[...] += _dot_nt(a_ref[...], w_ref[...])

        @pl.when(k == nk - 1)
        def _():
            def rows_chunk(ci, carry):
                rows = pl.ds(pl.multiple_of(ci * 128, 128), 128)
                dh = ALPHA * d2_ref[rows, :] + acc[rows, :]
                xhat = xh_ref[rows, :]
                g = dh * g_ref[...]
                dpre = rs_ref[rows, 0:1] * (g - jnp.mean(g, axis=1, keepdims=True)
                                            - xhat * jnp.mean(g * xhat, axis=1, keepdims=True))
                d_ref[rows, :] = dpre
                db_ref[rows, :] = dpre.astype(BF16)
                st_ref[0:1, :] += jnp.sum(dh * xhat, axis=0, keepdims=True)
                st_ref[1:2, :] += jnp.sum(dh, axis=0, keepdims=True)
                return carry

            lax.fori_loop(0, tm // 128, rows_chunk, 0)

    row = pl.BlockSpec((tm, D), lambda i, k: (i, 0))
    body, more_specs, more = _behind(body, 6, after)
    return pl.pallas_call(
        body, name="d_h1_ln1", grid=(S // tm, nk),
        in_specs=[pl.BlockSpec((tm, tk), lambda i, k: (i, k)),
                  pl.BlockSpec((None, D, tk), lambda i, k: (k // per, 0, k % per)),
                  row, row, pl.BlockSpec((tm, 128), lambda i, k: (i, 0)), pl.BlockSpec((1, D), lambda i, k: (0, 0))]
        + more_specs,
        out_specs=[row, row, pl.BlockSpec((8, D), lambda i, k: (0, 0))],
        out_shape=[jax.ShapeDtypeStruct((S, D), F32), jax.ShapeDtypeStruct((S, D), BF16),
                   jax.ShapeDtypeStruct((8, D), F32)],
        scratch_shapes=[pltpu.VMEM((tm, D), F32)],
        compiler_params=_params(2),
    )(dprea, w1_g, dpre2, xhat1, rstd1, g1, *more)


def _d_merged(dpre1b, wout_g, proj, ya, yb):
    tm, tn = 512, 1024

    def body(d_ref, w_ref, ga_ref, gb_ref, ya_ref, yb_ref, dya_ref, dyb_ref, dga_ref, dgb_ref):
        dm = _dot_nt(d_ref[...], w_ref[...])
        sa = _sigmoid(ga_ref[...])
        sb = _sigmoid(gb_ref[...])
        dya_ref[...] = (dm * sa).astype(BF16)
        dyb_ref[...] = (dm * sb).astype(BF16)
        dga_ref[...] = (dm * ya_ref[...].astype(F32) * sa * (1.0 - sa)).astype(BF16)
        dgb_ref[...] = (dm * yb_ref[...].astype(F32) * sb * (1.0 - sb)).astype(BF16)

    tile = pl.BlockSpec((tm, tn), lambda i, j: (i, j))
    return pl.pallas_call(
        body, name="d_merged", grid=(S // tm, D // tn),
        in_specs=[pl.BlockSpec((tm, D), lambda i, j: (i, 0)), pl.BlockSpec((tn, D), lambda i, j: (j, 0)),
                  pl.BlockSpec((tm, tn), lambda i, j: (i, 5 + j)), pl.BlockSpec((tm, tn), lambda i, j: (i, 7 + j)),
                  tile, tile],
        out_specs=[tile] * 4,
        out_shape=[jax.ShapeDtypeStruct((S, D), BF16)] * 4,
        compiler_params=_params(2),
    )(dpre1b, wout_g, proj, proj, ya, yb)


def _d_branches(dya, dyb, wpa_g, wpb_g, after=None):
    tk = 512

    def body(da_ref, db_ref, wa_ref, wb_ref, oa_ref, ob_ref):
        @pl.when(pl.program_id(0) == 0)
        def _():
            oa_ref[...] = jnp.zeros_like(oa_ref)
            ob_ref[...] = jnp.zeros_like(ob_ref)

        oa_ref[...] += _dot_nt(da_ref[...], wa_ref[...])
        ob_ref[...] += _dot_nt(db_ref[...], wb_ref[...])

    body, more_specs, more = _behind(body, 4, after)
    return pl.pallas_call(
        body, name="d_branches", grid=(D // tk,),
        in_specs=[pl.BlockSpec((S, tk), lambda k: (0, k)), pl.BlockSpec((S, tk), lambda k: (0, k)),
                  pl.BlockSpec((None, DA, tk), lambda k: (k, 0, 0)), pl.BlockSpec((None, DB, tk), lambda k: (k, 0, 0))]
        + more_specs,
        out_specs=[pl.BlockSpec((S, DA), lambda k: (0, 0)), pl.BlockSpec((S, DB), lambda k: (0, 0))],
        out_shape=[jax.ShapeDtypeStruct((S, DA), F32), jax.ShapeDtypeStruct((S, DB), F32)],
        compiler_params=_params(1),
    )(dya, dyb, wpa_g, wpb_g, *more)


def _gmlp_bwd(proj, dgmlp, ws, ws_t, bsp_b, gain_v, bias_v):
    def body(u_ref, vb_ref, dg_ref, ws_ref, wst_ref, bsp_ref, g_ref, be_ref, duv_ref, gws_ref, gbs_ref, st_ref):
        @pl.when(pl.program_id(0) == 0)
        def _():
            gws_ref[...] = jnp.zeros_like(gws_ref)
            gbs_ref[...] = jnp.zeros_like(gbs_ref)
            st_ref[...] = jnp.zeros_like(st_ref)

        u, tu, u_act, vb, tv, rstd, xhat, vn = _gmlp_parts(u_ref, vb_ref, g_ref, be_ref)
        dg = dg_ref[...]
        dz = dg * u_act
        row = lax.broadcasted_iota(jnp.int32, (128, 128), 0)
        col = lax.broadcasted_iota(jnp.int32, (128, 128), 1)
        causal = row >= col
        causal_t = row <= col
        dvn_parts = []
        z_parts = []
        for g in range(NH):
            cols = slice(g * 128, (g + 1) * 128)
            vng = vn[:, cols].astype(BF16)
            dzg = dz[:, cols]
            dzb = dzg.astype(BF16)
            wsg = jnp.where(causal, ws_ref[g], 0.0).astype(BF16)
            wsg_t = jnp.where(causal_t, wst_ref[g], 0.0).astype(BF16)
            z_parts.append(_dot(wsg, vng) + bsp_ref[g])
            gws_ref[g] += jnp.where(causal, _dot_nt(dzb, vng), 0.0)
            gbs_ref[g] += jnp.broadcast_to(jnp.sum(dzg, axis=1, keepdims=True), (128, 128))
            dvn_parts.append(_dot(wsg_t, dzb))
        z = jnp.concatenate(z_parts, axis=1)
        dvn = jnp.concatenate(dvn_parts, axis=1)
        du = dg * z * _gelu_grad(u, tu)
        st_ref[0:1, :] += jnp.sum(dvn * xhat, axis=0, keepdims=True)
        st_ref[1:2, :] += jnp.sum(dvn, axis=0, keepdims=True)
        gg = dvn * g_ref[...]
        dgv = rstd * (gg - jnp.mean(gg, axis=1, keepdims=True) - xhat * jnp.mean(gg * xhat, axis=1, keepdims=True))
        dvb = dgv * _gelu_grad(vb, tv)
        duv_ref[:, 0:DB] = du.astype(BF16)
        duv_ref[:, DB:2 * DB] = dvb.astype(BF16)

    full3 = pl.BlockSpec((NH, 128, 128), lambda c: (0, 0, 0))
    vec = pl.BlockSpec((1, DB), lambda c: (0, 0))
    return pl.pallas_call(
        body, name="gmlp_bwd", grid=(NBLK,),
        in_specs=[pl.BlockSpec((128, DB), lambda c: (c, 3)), pl.BlockSpec((128, DB), lambda c: (c, 4)),
                  pl.BlockSpec((128, DB), lambda c: (c, 0)), full3, full3, full3, vec, vec],
        out_specs=[pl.BlockSpec((128, 2 * DB), lambda c: (c, 0)), full3, full3, pl.BlockSpec((8, DB), lambda c: (0, 0))],
        out_shape=[jax.ShapeDtypeStruct((S, 2 * DB), BF16), jax.ShapeDtypeStruct((NH, 128, 128), F32),
                   jax.ShapeDtypeStruct((NH, 128, 128), F32), jax.ShapeDtypeStruct((8, DB), F32)],
        compiler_params=_params(1),
    )(proj, proj, dgmlp, ws, ws_t, bsp_b, gain_v, bias_v)


def _rel_bias_grad(ds_sums):
    buckets = jnp.asarray(np.stack([_bucket_tile(d) for _, d in PATTERNS]))

    def body(bk_ref, ds_ref, o_ref):
        row = lax.broadcasted_iota(jnp.int32, (N_BUCKETS, 128), 0)
        lane = lax.broadcasted_iota(jnp.int32, (N_BUCKETS, 128), 1)

        def one_bucket(t, out):
            hits = [bk_ref[p] == t for p in range(3)]
            for h in range(NH):
                tot = jnp.zeros((128, 256), F32)
                for p in range(3):
                    tot = tot + jnp.where(hits[p], ds_ref[p, h], 0.0)
                out = jnp.where((row == t) & (lane == h), jnp.sum(tot), out)
            return out

        o_ref[...] = lax.fori_loop(0, N_BUCKETS, one_bucket, jnp.zeros((N_BUCKETS, 128), F32))

    return pl.pallas_call(
        body, name="rel_bias_grad",
        in_specs=[pl.BlockSpec(memory_space=pltpu.VMEM)] * 2, out_specs=pl.BlockSpec(memory_space=pltpu.VMEM),
        out_shape=jax.ShapeDtypeStruct((N_BUCKETS, 128), F32),
        compiler_params=pltpu.CompilerParams(vmem_limit_bytes=VMEM_LIMIT),
    )(buckets, ds_sums)


def _d_x(dproj, win_g, dpre1, after=None):
    tm, tk = 512, 2304
    per = 2304 // tk
    nk = DIN // tk

    def body(a_ref, w_ref, d_ref, o_ref, acc):
        k = pl.program_id(1)

        @pl.when(k == 0)
        def _():
            acc[...] = ALPHA * d_ref[...]

        acc[...] += _dot_nt(a_ref[...], w_ref[...])

        @pl.when(k == nk - 1)
        def _():
            o_ref[...] = acc[...]

    row = pl.BlockSpec((tm, D), lambda i, k: (i, 0))
    body, more_specs, more = _behind(body, 3, after)
    return pl.pallas_call(
        body, name="d_x", grid=(S // tm, nk),
        in_specs=[pl.BlockSpec((tm, tk), lambda i, k: (i, k)),
                  pl.BlockSpec((None, D, tk), lambda i, k: (k // per, 0, k % per)), row] + more_specs,
        out_specs=row, out_shape=jax.ShapeDtypeStruct((S, D), F32),
        scratch_shapes=[pltpu.VMEM((tm, D), F32)],
        compiler_params=_params(2),
    )(dproj, win_g, dpre1, *more)


def _adamw(w, g, m, v, name):
    rows, cols = w.shape
    tm = max(t for t in range(8, 257, 8) if rows % t == 0)

    def body(w_ref, g_ref, m_ref, v_ref, d_ref, nm_ref, nv_ref, go_ref):
        g = g_ref[...]
        m = ADAM_B1 * m_ref[...] + (1.0 - ADAM_B1) * g
        v = ADAM_B2 * v_ref[...] + (1.0 - ADAM_B2) * (g * g)
        m_hat = m / (1.0 - ADAM_B1 ** ADAM_STEP)
        v_hat = v / (1.0 - ADAM_B2 ** ADAM_STEP)
        d_ref[...] = -ADAM_LR * (m_hat / (jnp.sqrt(v_hat) + ADAM_EPS) + ADAM_WD * w_ref[...])
        nm_ref[...] = m
        nv_ref[...] = v
        go_ref[...] = g

    spec = pl.BlockSpec((tm, cols), lambda i: (i, 0))
    return pl.pallas_call(
        body, name=name, grid=(rows // tm,), in_specs=[spec] * 4, out_specs=[spec] * 4,
        out_shape=[jax.ShapeDtypeStruct((rows, cols), F32)] * 4, compiler_params=_params(1),
    )(w, g, m, v)


def _position():
    x, y, c = lax.axis_index("x"), lax.axis_index("y"), lax.axis_index("c")
    chips = [(1 - x, y), (x, 1 - y), (1 - x, 1 - y)]
    return x, y, c, chips


def _remote(src, dst, send_sems, recv_sems, k, to):
    return pltpu.make_async_remote_copy(src_ref=src, dst_ref=dst, send_sem=send_sems.at[k], recv_sem=recv_sems.at[k],
                                        device_id=to, device_id_type=MESH)


def _place_shard(w, name, after=None):
    rows, cols = w.shape
    tm = 256
    x, y = lax.axis_index("x"), lax.axis_index("y")

    def body(chip_ref, w_ref, o_ref):
        o_ref[...] = w_ref[...].astype(BF16)

    more_specs, more = ([ANY], [after]) if after is not None else ([], [])
    if after is not None:
        inner = body
        body = lambda chip_ref, w_ref, after_ref, o_ref: inner(chip_ref, w_ref, o_ref)
    return pl.pallas_call(
        body, name=name,
        grid_spec=pltpu.PrefetchScalarGridSpec(
            num_scalar_prefetch=1, grid=(rows // tm,),
            in_specs=[pl.BlockSpec((tm, cols), lambda i, chip: (i, 0))] + more_specs,
            out_specs=pl.BlockSpec((None, tm, cols), lambda i, chip: (chip[0], i, 0))),
        out_shape=jax.ShapeDtypeStruct((N_CHIPS, rows, cols), BF16),
        compiler_params=_params(1),
    )(jnp.reshape(2 * x + y, (1,)).astype(jnp.int32), w, *more)


def _to_bf16(x, name, after=None):
    tm = 256

    def body(x_ref, o_ref):
        o_ref[...] = x_ref[...].astype(BF16)

    spec = pl.BlockSpec((tm, x.shape[1]), lambda i: (i, 0))
    body, more_specs, more = _behind(body, 1, after)
    return pl.pallas_call(
        body, name=name, grid=(x.shape[0] // tm,), in_specs=[spec] + more_specs, out_specs=spec,
        out_shape=jax.ShapeDtypeStruct(x.shape, BF16), compiler_params=_params(1),
    )(x, *more)


HBM = pl.BlockSpec(memory_space=pltpu.HBM)
SEM = pl.BlockSpec(memory_space=pltpu.SEMAPHORE)
EFFECT = pltpu.SideEffectType.DATAFLOW_SIDE_EFFECTING


def _comm_call(name, body, bufs, sems_in, sems_out, after=None, token=False):
    nb, ns, no = len(bufs), len(sems_in), len(sems_out)
    n_in = nb + ns + (after is not None)

    def wrapped(*refs):
        body(refs[:nb], refs[nb:nb + ns], refs[n_in + nb:n_in + nb + no])
        if token:
            refs[-1][...] = jnp.zeros((8, 128), F32)

    outs = pl.pallas_call(
        wrapped, name=name,
        in_specs=[HBM] * nb + [SEM] * ns + ([ANY] if after is not None else []),
        out_specs=[HBM] * nb + [SEM] * no + ([pl.BlockSpec(memory_space=pltpu.VMEM)] if token else []),
        out_shape=[pltpu.HBM(b.shape, b.dtype) for b in bufs] + [pltpu.SemaphoreType.DMA((k,)) for k in sems_out]
        + ([jax.ShapeDtypeStruct((8, 128), F32)] if token else []),
        input_output_aliases={i: i for i in range(nb)},
        compiler_params=pltpu.CompilerParams(has_side_effects=EFFECT),
    )(*[pltpu.with_memory_space_constraint(b, pltpu.HBM) for b in bufs], *sems_in, *([after] if after is not None else []))
    return list(outs[:nb]), list(outs[nb:nb + no]), (outs[-1] if token else None)


RING_STAGES = {"ici_near": 2, "ici_far": 2, "d2d_near": 2, "d2d_far": 1}


def _ring_copies(buf, send_sems, recv_sems, k0, stage):
    x, y, c, _ = _position()
    hr = buf.shape[1] // 2
    qr = hr // 2
    half = lambda chip, h: buf.at[chip, pl.ds(h * hr, hr), :]
    quarter = lambda chip, h, q: buf.at[chip, pl.ds(h * hr + q * qr, qr), :]
    mine, x_chip, y_chip, far_chip = 2 * x + y, 2 * (1 - x) + y, 2 * x + (1 - y), 2 * (1 - x) + (1 - y)
    to_x, to_y, sibling = (1 - x, y, c), (x, 1 - y, c), (x, y, 1 - c)
    if stage == "ici_near":
        moves = [(half(mine, c), to_x, half(x_chip, c)), (half(mine, c), to_y, half(y_chip, c))]
    elif stage == "ici_far":
        moves = [(quarter(x_chip, c, 0), to_y, quarter(far_chip, c, 0)),
                 (quarter(y_chip, c, 1), to_x, quarter(far_chip, c, 1))]
    elif stage == "d2d_near":
        moves = [(half(x_chip, c), sibling, half(x_chip, 1 - c)), (half(y_chip, c), sibling, half(y_chip, 1 - c))]
    else:
        moves = [(half(far_chip, c), sibling, half(far_chip, 1 - c))]
    sends = [_remote(src, src, send_sems, recv_sems, k0 + i, to) for i, (src, to, _) in enumerate(moves)]
    arrivals = [_remote(got, got, send_sems, recv_sems, k0 + i, (x, y, c)) for i, (_, _, got) in enumerate(moves)]
    return sends, arrivals


def _ring_call(name, groups, actions, after=None):
    tags = list(dict.fromkeys(t for _, t, _ in actions))
    counts = {t: len(groups[t]["bufs"]) for t in tags}
    first = {t: sum(counts[u] for u in tags[:i]) for i, t in enumerate(tags)}
    waits = [(t, s) for v, t, s in actions if v == "wait"]
    starts = [(t, s) for v, t, s in actions if v == "start"]

    def body(bufs, sems_in, sems_out):
        for verb, t, s in actions:
            at, sems = (starts.index((t, s)), sems_out) if verb == "start" else (waits.index((t, s)), sems_in)
            for w in range(counts[t]):
                sends, arrivals = _ring_copies(bufs[first[t] + w], sems[2 * at], sems[2 * at + 1], RING_STAGES[s] * w, s)
                if verb == "start":
                    for cp in sends:
                        cp.start()
                else:
                    for cp in arrivals:
                        cp.wait_recv()
                    for cp in sends:
                        cp.wait_send()

    bufs, sems, token = _comm_call(
        name, body, [b for t in tags for b in groups[t]["bufs"]],
        [sem for t, s in waits for sem in groups[t]["sems"][s]],
        [RING_STAGES[s] * counts[t] for t, s in starts for _ in (0, 1)], after, token=True)
    for t in tags:
        groups[t]["bufs"] = bufs[first[t]:first[t] + counts[t]]
    for t, s in waits:
        del groups[t]["sems"][s]
    for i, (t, s) in enumerate(starts):
        groups[t]["sems"][s] = (sems[2 * i], sems[2 * i + 1])
    return token


def _cx_copies(src, dst, send_sems, recv_sems, k0):
    x, y, c, chips = _position()
    sends = [_remote(src.at[2 * cx + cy], dst.at[2 * x + y], send_sems, recv_sems, k0 + j, (cx, cy, c))
             for j, (cx, cy) in enumerate(chips)]
    arrivals = [_remote(dst.at[2 * cx + cy], dst.at[2 * cx + cy], send_sems, recv_sems, k0 + j, (x, y, c))
                for j, (cx, cy) in enumerate(chips)]
    return sends, arrivals


def _cx_start(name, pair_sums):
    n = len(pair_sums)
    landing = [lax.empty(p.shape, p.dtype) for p in pair_sums]

    def body(bufs, _, sems):
        for w in range(n):
            for cp in _cx_copies(bufs[w], bufs[n + w], sems[0], sems[1], 3 * w)[0]:
                cp.start()

    bufs, sems, token = _comm_call(name, body, list(pair_sums) + landing, [], [3 * n, 3 * n], token=True)
    return (bufs, sems), token


def _cx_wait(name, state, after):
    bufs, sems = state
    n = len(bufs) // 2

    def body(refs, sems_in, _):
        for w in range(n):
            sends, arrivals = _cx_copies(refs[w], refs[n + w], sems_in[0], sems_in[1], 3 * w)
            for cp in arrivals:
                cp.wait_recv()
            for cp in sends:
                cp.wait_send()

    bufs, _, _ = _comm_call(name, body, bufs, sems, [], after)
    return bufs[:n], bufs[n:]


def _px_copies(src, dst, send_sems, recv_sems, k):
    x, y, c, _ = _position()
    hr = src.shape[1] // 2
    send = _remote(src.at[:, pl.ds((1 - c) * hr, hr), :], dst, send_sems, recv_sems, k, (x, y, 1 - c))
    arrival = _remote(dst, dst, send_sems, recv_sems, k, (x, y, c))
    return send, arrival


def _px_start(name, grads):
    n = len(grads)
    landing = [lax.empty((N_CHIPS, g.shape[1] // 2, g.shape[2]), F32) for g in grads]

    def body(bufs, _, sems):
        for w in range(n):
            _px_copies(bufs[w], bufs[n + w], sems[0], sems[1], w)[0].start()

    bufs, sems, token = _comm_call(name, body, list(grads) + landing, [], [n, n], token=True)
    return (bufs, sems), token


def _px_wait(name, state, after):
    bufs, sems = state
    n = len(bufs) // 2

    def body(refs, sems_in, _):
        for w in range(n):
            send, arrival = _px_copies(refs[w], refs[n + w], sems_in[0], sems_in[1], w)
            arrival.wait_recv()
            send.wait_send()

    bufs, _, _ = _comm_call(name, body, bufs, sems, [], after)
    return bufs[:n], bufs[n:]


def _pair_sum(grad, got, name):
    _, rows, cols = grad.shape
    hr = rows // 2
    tm = min(hr, 256)
    nb = hr // tm
    c = lax.axis_index("c")

    def body(c_ref, g_ref, o_ref, out_ref):
        out_ref[...] = (g_ref[...] + o_ref[...]).astype(BF16)

    return pl.pallas_call(
        body, name=name,
        grid_spec=pltpu.PrefetchScalarGridSpec(
            num_scalar_prefetch=1, grid=(N_CHIPS, nb),
            in_specs=[pl.BlockSpec((None, tm, cols), lambda s, i, c_ref: (s, c_ref[0] * nb + i, 0)),
                      pl.BlockSpec((None, tm, cols), lambda s, i, c_ref: (s, i, 0))],
            out_specs=pl.BlockSpec((None, tm, cols), lambda s, i, c_ref: (s, i, 0))),
        out_shape=jax.ShapeDtypeStruct((N_CHIPS, hr, cols), BF16),
        compiler_params=_params(2),
    )(jnp.reshape(c, (1,)).astype(jnp.int32), grad, got)


def _chip_sum(parts, pair_sums, name):
    _, hr, cols = parts.shape
    tm = min(hr, 256)
    nb = hr // tm
    x, y, c = lax.axis_index("x"), lax.axis_index("y"), lax.axis_index("c")

    def body(pos_ref, p_ref, own_ref, o_ref):
        chip = pos_ref[0]
        own = own_ref[...].astype(F32)
        term = lambda s: jnp.where(chip == s, own, p_ref[s].astype(F32))
        o_ref[...] = ((term(0) + term(1)) + term(2)) + term(3)

    return pl.pallas_call(
        body, name=name,
        grid_spec=pltpu.PrefetchScalarGridSpec(
            num_scalar_prefetch=1, grid=(nb,),
            in_specs=[pl.BlockSpec((N_CHIPS, tm, cols), lambda i, pos: (0, i, 0)),
                      pl.BlockSpec((None, tm, cols), lambda i, pos: (pos[0], i, 0))],
            out_specs=pl.BlockSpec((tm, cols), lambda i, pos: (pos[1] * nb + i, 0))),
        out_shape=jax.ShapeDtypeStruct((2 * hr, cols), F32), compiler_params=_params(1),
    )(jnp.stack([2 * x + y, c]).astype(jnp.int32), parts, pair_sums)


def _share_halves(bufs, name):
    n = len(bufs)

    def body(*refs):
        outs = refs[n:2 * n]
        send_sems, recv_sems = refs[2 * n:]
        x, y, c, _ = _position()
        copies = []
        for w in range(n):
            hr = outs[w].shape[0] // 2
            mine = outs[w].at[pl.ds(c * hr, hr), :]
            cp = _remote(mine, mine, send_sems, recv_sems, w, (x, y, 1 - c))
            cp.start()
            copies.append(cp)
        for w in range(n):
            hr = outs[w].shape[0] // 2
            theirs = outs[w].at[pl.ds((1 - c) * hr, hr), :]
            _remote(theirs, theirs, send_sems, recv_sems, w, (x, y, c)).wait_recv()
        for cp in copies:
            cp.wait_send()

    return pl.pallas_call(
        body, name=name,
        in_specs=[ANY] * n, out_specs=[ANY] * n,
        out_shape=[jax.ShapeDtypeStruct(b.shape, b.dtype) for b in bufs],
        input_output_aliases={w: w for w in range(n)},
        scratch_shapes=[pltpu.SemaphoreType.DMA((n,)), pltpu.SemaphoreType.DMA((n,))],
    )(*bufs)


def _allreduce_small(g):
    rows = g.shape[0]

    def body(g_ref, o_ref, sib, slots, send_sems, recv_sems):
        x, y, c, chips = _position()
        me = (x, y, c)
        my_chip = 2 * x + y
        pair = _remote(g_ref, sib, send_sems, recv_sems, 0, (x, y, 1 - c))
        pair.start()
        pair.wait()
        slots[my_chip] = g_ref[...] + sib[...]
        sent = []
        for j, (cx, cy) in enumerate(chips):
            cp = _remote(slots.at[my_chip], slots.at[my_chip], send_sems, recv_sems, 1 + j, (cx, cy, c))
            cp.start()
            sent.append(cp)
        for j, (cx, cy) in enumerate(chips):
            got = slots.at[2 * cx + cy]
            _remote(got, got, send_sems, recv_sems, 1 + j, me).wait_recv()
        for cp in sent:
            cp.wait_send()
        o_ref[...] = ((slots[0] + slots[1]) + slots[2]) + slots[3]

    vm = pl.BlockSpec(memory_space=pltpu.VMEM)
    return pl.pallas_call(
        body, name="allreduce_small",
        in_specs=[vm], out_specs=vm, out_shape=jax.ShapeDtypeStruct((rows, 128), F32),
        scratch_shapes=[pltpu.VMEM((rows, 128), F32), pltpu.VMEM((N_CHIPS, rows, 128), F32),
                        pltpu.SemaphoreType.DMA((4,)), pltpu.SemaphoreType.DMA((4,))],
        compiler_params=pltpu.CompilerParams(vmem_limit_bytes=VMEM_LIMIT),
    )(g)


_SMALL =("rel_bias", "ln_v_gain", "ln_v_bias", "w_spatial", "b_spatial", "ln1_gain", "ln1_bias",
          "b_ff1", "b_ff2", "ln2_gain", "ln2_bias")
_SMALL_ROWS = 1200
_LOSS_AT = (152832 // 128, 0)


def _pack_small(parts):
    flat = jnp.concatenate([parts[k].reshape(-1).astype(F32) for k in _SMALL])
    flat = jnp.pad(flat, (0, _SMALL_ROWS * 128 - flat.shape[0]))
    return flat.reshape(_SMALL_ROWS, 128)


def _unpack_small(packed, like):
    flat = packed.reshape(-1)
    out, at = {}, 0
    for k in _SMALL:
        n = math.prod(like[k].shape)
        out[k] = flat[at:at + n].reshape(like[k].shape)
        at += n
    return out


def kernel(x, w_in, rel_bias, ln_v_gain, ln_v_bias, w_spatial, b_spatial, w_proj_a, w_proj_b, w_out, ln1_gain, ln1_bias, w_ff1, b_ff1, w_ff2, b_ff2, ln2_gain, ln2_bias, loss_target, m_w_in, m_rel_bias, m_ln_v_gain, m_ln_v_bias, m_w_spatial, m_b_spatial, m_w_proj_a, m_w_proj_b, m_w_out, m_ln1_gain, m_ln1_bias, m_w_ff1, m_b_ff1, m_w_ff2, m_b_ff2, m_ln2_gain, m_ln2_bias, v_w_in, v_rel_bias, v_ln_v_gain, v_ln_v_bias, v_w_spatial, v_b_spatial, v_w_proj_a, v_w_proj_b, v_w_out, v_ln1_gain, v_ln1_bias, v_w_ff1, v_b_ff1, v_w_ff2, v_b_ff2, v_ln2_gain, v_ln2_bias):
    args = dict(locals())
    big = ("w_in", "w_proj_a", "w_proj_b", "w_out", "w_ff1", "w_ff2")
    weights = ("w_in", "rel_bias", "ln_v_gain", "ln_v_bias", "w_spatial", "b_spatial", "w_proj_a", "w_proj_b", "w_out",
               "ln1_gain", "ln1_bias", "w_ff1", "b_ff1", "w_ff2", "b_ff2", "ln2_gain", "ln2_bias")

    xs = x[0]
    target = loss_target[0]

    ring = {"a": {"bufs": [_place_shard(w_in[0], "place_w_in")], "sems": {}}}
    tok = _ring_call("allgather_a_near", ring, [("start", "a", "ici_near")])
    placed = [_place_shard(args[k][0], f"place_{k}", after=tok) for k in big[1:]]
    for tag, bufs in (("b", placed[0:3]), ("c", placed[3:4]), ("d", placed[4:5])):
        ring[tag] = {"bufs": bufs, "sems": {}}
    xb = _to_bf16(xs, "x_to_bf16", after=tok)
    _ring_call("allgather_a_far", ring, [("wait", "a", "ici_near"), ("start", "a", "ici_far"), ("start", "a", "d2d_near"),
                                         ("start", "b", "ici_near"), ("start", "c", "ici_near"), ("start", "d", "ici_near")],
               after=xb)
    _ring_call("allgather_a_last", ring, [("wait", "a", "ici_far"), ("start", "a", "d2d_far")])
    _ring_call("allgather_a_done", ring, [("wait", "a", "d2d_near"), ("wait", "a", "d2d_far")])
    (win_g,) = ring["a"]["bufs"]

    proj = _proj(xb, win_g)
    _ring_call("allgather_b_far", ring, [("wait", "b", "ici_near"), ("start", "b", "ici_far"), ("start", "b", "d2d_near")],
               after=proj)
    ws = w_spatial[0]
    ws_t = jnp.transpose(ws, (0, 2, 1))
    bsp_b = jnp.broadcast_to(b_spatial[0][:, :, None], (NH, 128, 128))
    gmlp = _gmlp_fwd(proj, ws, bsp_b, ln_v_gain, ln_v_bias)
    attn, lse = _attention_fwd(proj, rel_bias)
    _ring_call("allgather_b_last_c_far", ring,
               [("wait", "b", "ici_far"), ("start", "b", "d2d_far"),
                ("wait", "c", "ici_near"), ("start", "c", "ici_far"), ("start", "c", "d2d_near")], after=attn)
    _ring_call("allgather_b_done", ring, [("wait", "b", "d2d_near"), ("wait", "b", "d2d_far")])
    wpa_g, wpb_g, wout_g = ring["b"]["bufs"]
    wout_full = wout_g.reshape(D, D)
    ya, yb, merged = _branch(attn, gmlp, wpa_g, wpb_g, proj)
    xhat1, rstd1, h1b = _out_ln1(merged, wout_full, xs, ln1_gain, ln1_bias)
    _ring_call("allgather_c_last_d_far", ring,
               [("wait", "c", "ici_far"), ("start", "c", "d2d_far"),
                ("wait", "d", "ici_near"), ("start", "d", "ici_far"), ("start", "d", "d2d_near")], after=h1b)
    _ring_call("allgather_c_done", ring, [("wait", "c", "d2d_near"), ("wait", "c", "d2d_far")])
    (w1_g,) = ring["c"]["bufs"]
    a, r = _ff1(h1b, w1_g, b_ff1)
    _ring_call("allgather_d_last", ring, [("wait", "d", "ici_far"), ("start", "d", "d2d_far")], after=a)
    _ring_call("allgather_d_done", ring, [("wait", "d", "d2d_near"), ("wait", "d", "d2d_far")])
    (w2_g,) = ring["d"]["bufs"]
    w2_full = w2_g.reshape(DFF, D)
    dpre2, dpre2b, st2 = _ff2_ln2_loss(a, w2_full, xhat1, ln1_gain, ln1_bias, b_ff2, ln2_gain, ln2_bias, target)

    def pair_and_chip(tag, state, after):
        local, from_sibling = _px_wait(f"pair_exchange_wait_{tag}", state, after)
        pair_sums = [_pair_sum(g, o, f"pair_sum_{tag}_{i}") for i, (g, o) in enumerate(zip(local, from_sibling))]
        return _cx_start(f"chip_exchange_start_{tag}", pair_sums)

    g_w2 = _grad_w(a, dpre2b, "grad_w_ff2", 512, 2048, False)
    px, tok = _px_start("pair_exchange_start_w_ff2", [g_w2.reshape(N_CHIPS, DFF // N_CHIPS, D)])
    dprea, g_b1 = _d_ff1(dpre2b, w2_full, r, after=tok)
    cx_w2, tok = pair_and_chip("w_ff2", px, dprea)
    g_w1 = _grad_w(h1b, dprea, "grad_w_ff1", 512, 2048, True, after=tok)
    px, tok = _px_start("pair_exchange_start_w_ff1", [g_w1])
    dpre1, dpre1b, st1 = _d_h1_ln1(dprea, w1_g, dpre2, xhat1, rstd1, ln1_gain, after=tok)
    cx_w1, tok = pair_and_chip("w_ff1", px, dpre1b)
    g_wout = _grad_w(merged, dpre1b, "grad_w_out", 512, 2048, False, after=tok)
    dya, dyb, dga, dgb = _d_merged(dpre1b, wout_full, proj, ya, yb)
    g_wpa = _grad_w(attn, dya, "grad_w_proj_a", 1024, 512, True)
    g_wpb = _grad_w(gmlp, dyb, "grad_w_proj_b", 1024, 512, True)
    px, tok = _px_start("pair_exchange_start_b", [g_wpa, g_wpb, g_wout.reshape(N_CHIPS, D // N_CHIPS, D)])
    dattn, dgmlp = _d_branches(dya, dyb, wpa_g, wpb_g, after=tok)
    duv, g_ws, g_bs, stv = _gmlp_bwd(proj, dgmlp, ws, ws_t, bsp_b, ln_v_gain, ln_v_bias)
    cx_b, tok = pair_and_chip("b", px, duv)
    dq, dk, dv, ds_sums = _attention_bwd(proj, dattn, attn, lse, rel_bias, after=tok)
    g_rb = _rel_bias_grad(ds_sums)[:, :NH]

    small_g = dict(rel_bias=g_rb, ln_v_gain=stv[0], ln_v_bias=stv[1], w_spatial=g_ws, b_spatial=g_bs[:, :, 0],
                   ln1_gain=st1[0], ln1_bias=st1[1], b_ff1=g_b1, b_ff2=st2[2], ln2_gain=st2[0], ln2_bias=st2[1])
    gs = _allreduce_small(_pack_small(small_g).at[_LOSS_AT].set(st2[3, 0]))
    ds_, ms_, vs_, _ = _adamw(_pack_small({k: args[k] for k in _SMALL}), gs,
                           _pack_small({k: args["m_" + k] for k in _SMALL}),
                           _pack_small({k: args["v_" + k] for k in _SMALL}), "adamw_small")
    like = {k: args[k] for k in _SMALL}
    grads, deltas, new_m, new_v = (_unpack_small(t, like) for t in (gs, ds_, ms_, vs_))

    dproj = jnp.concatenate([dq, dk, dv, duv, dga, dgb], axis=1)
    g_win = _grad_w(xb, dproj, "grad_w_in", 512, 2304, True, after=gs)
    px, tok = _px_start("pair_exchange_start_w_in", [g_win])
    grad_x = _d_x(dproj, win_g, dpre1, after=tok)
    cx_in, tok = pair_and_chip("w_in", px, grad_x)

    def reduce_finish(tag, state, names, after):
        pair_sums, from_chips = _cx_wait(f"chip_exchange_wait_{tag}", state, after)
        halves = [_chip_sum(p, own, f"chip_sum_{k}") for p, own, k in zip(from_chips, pair_sums, names)]
        last = None
        for k, g in zip(names, _share_halves(halves, f"share_halves_{tag}")):
            d_, m_, v_, g_ = _adamw(args[k][0], g, args["m_" + k][0], args["v_" + k][0], f"adamw_{k}")
            grads[k], deltas[k], new_m[k], new_v[k] = g_[None], d_[None], m_[None], v_[None]
            last = d_
        return last

    done = reduce_finish("w_ff2", cx_w2, ["w_ff2"], tok)
    done = reduce_finish("w_ff1", cx_w1, ["w_ff1"], done)
    done = reduce_finish("b", cx_b, ["w_proj_a", "w_proj_b", "w_out"], done)
    reduce_finish("w_in", cx_in, ["w_in"], done)

    loss = gs[_LOSS_AT] * (0.5 / D)
    return (loss, grad_x[None], *[grads[k] for k in weights], *[deltas[k] for k in weights],
            *[new_m[k] for k in weights], *[new_v[k] for k in weights])
```

```python
import functools
import math

import numpy as np
import jax
import jax.numpy as jnp
from jax import lax
from jax.experimental import pallas as pl
from jax.experimental.pallas import tpu as pltpu

F32 = jnp.float32
BF16 = jnp.bfloat16

S = 2048
D = 2048
DA = 1024
DB = 1024
DFF = 8192
DIN = 9216
NH = 8
HD = 128
NBLK = 16
PATTERNS = ((128, 1), (512, 4), (2048, 16))
N_BUCKETS = 32
MAX_DISTANCE = 2048
ALPHA = 2.0 ** 0.25
LN_EPS = 1e-5
NEG_INF = -1e30
SCALE = HD ** -0.5
N_CHIPS = 4

ADAM_LR = 0.001
ADAM_B1 = 0.9
ADAM_B2 = 0.999
ADAM_EPS = 1e-08
ADAM_WD = 0.01
ADAM_STEP = 10

VMEM_LIMIT = 56 * 1024 * 1024
MESH = pl.DeviceIdType.MESH
ANY = pl.BlockSpec(memory_space=pl.ANY)


def _params(n_axes, vmem=VMEM_LIMIT):
    return pltpu.CompilerParams(dimension_semantics=("arbitrary",) * n_axes, vmem_limit_bytes=vmem)


def _bucket_tile(dilation):
    qi = np.arange(128)[:, None]
    kj = np.arange(256)[None, :]
    n = np.clip(128 + qi - kj, 0, 128) * dilation
    max_exact = N_BUCKETS // 2
    nf = np.maximum(n, 1).astype(np.float32)
    large = max_exact + (np.log(nf / np.float32(max_exact)) / np.float32(math.log(MAX_DISTANCE / max_exact))
                         * np.float32(N_BUCKETS - max_exact)).astype(np.int32)
    large = np.minimum(large, N_BUCKETS - 1)
    return np.where(n < max_exact, n, large).astype(np.int32)


def _gelu(x):
    c = math.sqrt(2.0 / math.pi)
    t = jnp.tanh(c * (x + 0.044715 * x * x * x))
    return 0.5 * x * (1.0 + t), t


def _gelu_grad(x, t):
    c = math.sqrt(2.0 / math.pi)
    return 0.5 * (1.0 + t) + 0.5 * x * (1.0 - t * t) * c * (1.0 + 3.0 * 0.044715 * x * x)


def _sigmoid(x):
    return 1.0 / (1.0 + jnp.exp(-x))


def _dot(a, b):
    return jnp.dot(a, b, preferred_element_type=F32)


def _behind(body, n_in, after):
    if after is None:
        return body, [], []
    return (lambda *refs: body(*refs[:n_in], *refs[n_in + 1:])), [ANY], [after]


def _dot_nt(a, b):
    return lax.dot_general(a, b, (((1,), (1,)), ((), ())), preferred_element_type=F32)


def _proj(xb, win_g):
    tn = 768
    per = 2304 // tn

    def body(x_ref, w_ref, o_ref):
        o_ref[...] = _dot(x_ref[...], w_ref[...])

    return pl.pallas_call(
        body, name="proj", grid=(DIN // tn,),
        in_specs=[pl.BlockSpec((S, D), lambda j: (0, 0)),
                  pl.BlockSpec((None, D, tn), lambda j: (j // per, 0, j % per))],
        out_specs=pl.BlockSpec((S, tn), lambda j: (0, j)),
        out_shape=jax.ShapeDtypeStruct((S, DIN), F32),
        compiler_params=_params(1),
    )(xb, win_g)


FWD_HEADS_PER_STEP = 4
BWD_HEADS_PER_STEP = 2


def _head_bias_tiles(rb_ref, bk_ref, bias_scr, first_head, hps):
    qi = lax.broadcasted_iota(jnp.int32, (128, 256), 0)
    kj = lax.broadcasted_iota(jnp.int32, (128, 256), 1)
    steps = 128 + qi - kj
    band = (steps >= 0) & (steps <= 128)
    bias_scr[...] = jnp.zeros_like(bias_scr)
    for p in range(len(PATTERNS)):
        bucket = bk_ref[p]

        def one_bucket(t, carry):
            hit = bucket == t
            for j in range(hps):
                bias_scr[p, j] = jnp.where(hit, rb_ref[t, first_head + j], bias_scr[p, j])
            return carry

        lax.fori_loop(0, N_BUCKETS, one_bucket, 0)
        for j in range(hps):
            bias_scr[p, j] = jnp.where(band, bias_scr[p, j], NEG_INF)


def _block_rows(b, dilation):
    nblk = NBLK // dilation
    r, n = b // nblk, b % nblk
    start = r + n * (128 * dilation)
    prev_start = jnp.maximum(start - 128 * dilation, r)
    if dilation == 1:
        return pl.ds(pl.multiple_of(start, 128), 128), pl.ds(pl.multiple_of(prev_start, 128), 128), n > 0
    return pl.ds(start, 128, stride=dilation), pl.ds(prev_start, 128, stride=dilation), n > 0


def _head_specs(first, hps):
    return [pl.BlockSpec((S, HD), lambda g, j=j: (0, first + g * hps + j)) for j in range(hps)]


def _heads_spec(hps):
    return pl.BlockSpec((S, hps * HD), lambda g: (0, g))


def _attention_fwd(proj, rel_bias):
    hps = FWD_HEADS_PER_STEP
    buckets = jnp.asarray(np.stack([_bucket_tile(d) for _, d in PATTERNS]))

    def body(rb_ref, bk_ref, *refs):
        q_refs, k_refs, v_refs = (refs[i * hps:(i + 1) * hps] for i in range(3))
        o_ref, lse_ref, bias_scr = refs[3 * hps:3 * hps + 3]
        acc_scrs, m_scrs, l_scrs = (refs[3 * hps + 3 + i * hps:3 * hps + 3 + (i + 1) * hps] for i in range(3))
        _head_bias_tiles(rb_ref, bk_ref, bias_scr, pl.program_id(0) * hps, hps)
        kj = lax.broadcasted_iota(jnp.int32, (128, 256), 1)
        for p, (_, d) in enumerate(PATTERNS):
            prev_blocks = NBLK // d > 1

            def block(b, carry):
                rows, prows, has_prev = _block_rows(b, d)
                key_ok = (kj >= 128) | has_prev
                scores = []
                for j in range(hps):
                    q = q_refs[j][rows, :].astype(BF16)
                    cur = _dot_nt(q, k_refs[j][rows, :].astype(BF16))
                    if prev_blocks:
                        cur = jnp.concatenate([_dot_nt(q, k_refs[j][prows, :].astype(BF16)), cur], axis=1)
                    scores.append(cur)
                soft = []
                for j in range(hps):
                    if prev_blocks:
                        s = jnp.where(key_ok, scores[j] * SCALE + bias_scr[p, j], NEG_INF)
                    else:
                        s = scores[j] * SCALE + bias_scr[p, j, :, 128:256]
                    m = jnp.max(s, axis=1, keepdims=True)
                    e = jnp.exp(s - m)
                    soft.append((m, jnp.sum(e, axis=1, keepdims=True), e.astype(BF16)))
                outs = []
                for j in range(hps):
                    e = soft[j][2]
                    if prev_blocks:
                        outs.append(_dot(e[:, :128], v_refs[j][prows, :].astype(BF16))
                                    + _dot(e[:, 128:], v_refs[j][rows, :].astype(BF16)))
                    else:
                        outs.append(_dot(e, v_refs[j][rows, :].astype(BF16)))
                for j in range(hps):
                    acc_scr, m_scr, l_scr = acc_scrs[j], m_scrs[j], l_scrs[j]
                    (m, den, _), o = soft[j], outs[j]
                    if p == 0:
                        acc_scr[rows, :] = o
                        m_scr[rows, :] = jnp.broadcast_to(m, (128, HD))
                        l_scr[rows, :] = jnp.broadcast_to(den, (128, HD))
                    else:
                        m_old = m_scr[rows, :]
                        m_new = jnp.maximum(m_old, m)
                        w_old, w_new = jnp.exp(m_old - m_new), jnp.exp(m - m_new)
                        acc_scr[rows, :] = acc_scr[rows, :] * w_old + o * w_new
                        l_scr[rows, :] = l_scr[rows, :] * w_old + den * w_new
                        m_scr[rows, :] = m_new
                return carry

            lax.fori_loop(0, NBLK, block, 0)
        for j in range(hps):
            cols = slice(j * HD, (j + 1) * HD)
            den = l_scrs[j][...]
            o_ref[:, cols] = (acc_scrs[j][...] / den).astype(BF16)
            lse_ref[:, cols] = m_scrs[j][...] + jnp.log(den)

    return pl.pallas_call(
        body, name="attention_fwd", grid=(NH // hps,),
        in_specs=[pl.BlockSpec(memory_space=pltpu.SMEM), pl.BlockSpec((3, 128, 256), lambda g: (0, 0, 0))]
        + _head_specs(0, hps) + _head_specs(NH, hps) + _head_specs(2 * NH, hps),
        out_specs=[_heads_spec(hps), _heads_spec(hps)],
        out_shape=[jax.ShapeDtypeStruct((S, DA), BF16), jax.ShapeDtypeStruct((S, DA), F32)],
        scratch_shapes=[pltpu.VMEM((3, hps, 128, 256), F32)] + [pltpu.VMEM((S, HD), F32)] * (3 * hps),
        compiler_params=_params(1),
    )(rel_bias, buckets, *([proj] * (3 * hps)))


def _attention_bwd(proj, dattn, attn, lse, rel_bias, after=None):
    hps = BWD_HEADS_PER_STEP

    def body(rb_ref, bk_ref, *refs):
        q_refs, k_refs, v_refs, do_refs, o_refs, lse_refs = (refs[i * hps:(i + 1) * hps] for i in range(6))
        dq_ref, dk_ref, dv_ref, ds_ref, bias_scr = refs[6 * hps:6 * hps + 5]
        dl_scrs, dq_scrs, dk_scrs, dv_scrs = (refs[6 * hps + 5 + i * hps:6 * hps + 5 + (i + 1) * hps] for i in range(4))
        _head_bias_tiles(rb_ref, bk_ref, bias_scr, pl.program_id(0) * hps, hps)
        ds_ref[...] = jnp.zeros_like(ds_ref)
        for j in range(hps):
            dq_scrs[j][...] = jnp.zeros((S, HD), F32)
            dk_scrs[j][...] = jnp.zeros((S, HD), F32)
            dv_scrs[j][...] = jnp.zeros((S, HD), F32)
            prod = do_refs[j][...] * o_refs[j][...].astype(F32)
            dl_scrs[j][...] = jnp.broadcast_to(jnp.sum(prod, axis=1, keepdims=True), (S, HD))
        for p, (_, d) in enumerate(PATTERNS):
            prev_blocks = NBLK // d > 1

            def block(b, carry):
                rows, prows, has_prev = _block_rows(b, d)
                ops, raw = [], []
                for j in range(hps):
                    q, do = q_refs[j][rows, :].astype(BF16), do_refs[j][rows, :].astype(BF16)
                    kc, vc = k_refs[j][rows, :].astype(BF16), v_refs[j][rows, :].astype(BF16)
                    if prev_blocks:
                        kp, vp = k_refs[j][prows, :].astype(BF16), v_refs[j][prows, :].astype(BF16)
                        ops.append((q, do, kc, kp))
                        raw.append((_dot_nt(q, kc), _dot_nt(do, vc), _dot_nt(q, kp), _dot_nt(do, vp)))
                    else:
                        ops.append((q, do, kc))
                        raw.append((_dot_nt(q, kc), _dot_nt(do, vc)))
                probs = []
                for j in range(hps):
                    lse_b, dl_b = lse_refs[j][rows, :], dl_scrs[j][rows, :]
                    p_c = jnp.exp(raw[j][0] * SCALE + bias_scr[p, j, :, 128:256] - lse_b)
                    ds_c = p_c * (raw[j][1] - dl_b)
                    ds_ref[p, j, :, 128:256] += ds_c
                    if prev_blocks:
                        p_p = jnp.where(has_prev, jnp.exp(raw[j][2] * SCALE + bias_scr[p, j, :, 0:128] - lse_b), 0.0)
                        ds_p = p_p * (raw[j][3] - dl_b)
                        ds_ref[p, j, :, 0:128] += ds_p
                        probs.append((p_c, ds_c, p_p, ds_p))
                    else:
                        probs.append((p_c, ds_c))
                grads = []
                for j in range(hps):
                    q, do, kc = ops[j][:3]
                    p_c, ds_c = probs[j][:2]
                    dq = _dot(ds_c.astype(BF16), kc)
                    cur = (_dot(ds_c.T.astype(BF16), q) * SCALE, _dot(p_c.T.astype(BF16), do))
                    if prev_blocks:
                        p_p, ds_p = probs[j][2:]
                        dq = dq + _dot(ds_p.astype(BF16), ops[j][3])
                        cur = cur + (_dot(ds_p.T.astype(BF16), q) * SCALE, _dot(p_p.T.astype(BF16), do))
                    grads.append((dq * SCALE,) + cur)
                for j in range(hps):
                    dq_scrs[j][rows, :] += grads[j][0]
                    dk_scrs[j][rows, :] += grads[j][1]
                    dv_scrs[j][rows, :] += grads[j][2]
                    if prev_blocks:
                        dk_scrs[j][prows, :] += grads[j][3]
                        dv_scrs[j][prows, :] += grads[j][4]
                return carry

            lax.fori_loop(0, NBLK, block, 0)
        for j in range(hps):
            cols = slice(j * HD, (j + 1) * HD)
            dq_ref[:, cols] = dq_scrs[j][...].astype(BF16)
            dk_ref[:, cols] = dk_scrs[j][...].astype(BF16)
            dv_ref[:, cols] = dv_scrs[j][...].astype(BF16)

    buckets = jnp.asarray(np.stack([_bucket_tile(d) for _, d in PATTERNS]))
    body, more_specs, more = _behind(body, 2 + 6 * hps, after)
    return pl.pallas_call(
        body, name="attention_bwd", grid=(NH // hps,),
        in_specs=[pl.BlockSpec(memory_space=pltpu.SMEM), pl.BlockSpec((3, 128, 256), lambda g: (0, 0, 0))]
        + _head_specs(0, hps) + _head_specs(NH, hps) + _head_specs(2 * NH, hps) + 3 * _head_specs(0, hps)
        + more_specs,
        out_specs=3 * [_heads_spec(hps)] + [pl.BlockSpec((3, hps, 128, 256), lambda g: (0, g, 0, 0))],
        out_shape=[jax.ShapeDtypeStruct((S, DA), BF16)] * 3 + [jax.ShapeDtypeStruct((3, NH, 128, 256), F32)],
        scratch_shapes=[pltpu.VMEM((3, hps, 128, 256), F32)] + [pltpu.VMEM((S, HD), F32)] * (4 * hps),
        compiler_params=_params(1),
    )(rel_bias, buckets, *([proj] * (3 * hps)), *([dattn] * hps), *([attn] * hps), *([lse] * hps), *more)


def _gmlp_parts(u_ref, vb_ref, g_ref, be_ref):
    u = u_ref[...]
    u_act, tu = _gelu(u)
    vb = vb_ref[...]
    gv, tv = _gelu(vb)
    mean = jnp.mean(gv, axis=1, keepdims=True)
    cen = gv - mean
    var = jnp.mean(cen * cen, axis=1, keepdims=True)
    rstd = lax.rsqrt(var + LN_EPS)
    xhat = cen * rstd
    vn = xhat * g_ref[...] + be_ref[...]
    return u, tu, u_act, vb, tv, rstd, xhat, vn


def _gmlp_fwd(proj, ws, bsp_b, gain_v, bias_v):
    def body(u_ref, vb_ref, ws_ref, bsp_ref, g_ref, be_ref, o_ref):
        _, _, u_act, _, _, _, _, vn = _gmlp_parts(u_ref, vb_ref, g_ref, be_ref)
        row = lax.broadcasted_iota(jnp.int32, (128, 128), 0)
        col = lax.broadcasted_iota(jnp.int32, (128, 128), 1)
        causal = row >= col
        for g in range(NH):
            cols = slice(g * 128, (g + 1) * 128)
            wsg = jnp.where(causal, ws_ref[g], 0.0).astype(BF16)
            z = _dot(wsg, vn[:, cols].astype(BF16)) + bsp_ref[g]
            o_ref[:, cols] = (u_act[:, cols] * z).astype(BF16)

    return pl.pallas_call(
        body, name="gmlp_fwd", grid=(NBLK,),
        in_specs=[pl.BlockSpec((128, DB), lambda c: (c, 3)), pl.BlockSpec((128, DB), lambda c: (c, 4)),
                  pl.BlockSpec((NH, 128, 128), lambda c: (0, 0, 0)), pl.BlockSpec((NH, 128, 128), lambda c: (0, 0, 0)),
                  pl.BlockSpec((1, DB), lambda c: (0, 0)), pl.BlockSpec((1, DB), lambda c: (0, 0))],
        out_specs=pl.BlockSpec((128, DB), lambda c: (c, 0)),
        out_shape=jax.ShapeDtypeStruct((S, DB), BF16),
        compiler_params=_params(1),
    )(proj, proj, ws, bsp_b, gain_v, bias_v)


def _branch(attn, gmlp, wpa_g, wpb_g, proj):
    tn = 512

    def body(a_ref, g_ref, wa_ref, wb_ref, ga_ref, gb_ref, ya_ref, yb_ref, mg_ref):
        ya = _dot(a_ref[...], wa_ref[...])
        yb = _dot(g_ref[...], wb_ref[...])
        ya_ref[...] = ya.astype(BF16)
        yb_ref[...] = yb.astype(BF16)
        mg_ref[...] = (_sigmoid(ga_ref[...]) * ya + _sigmoid(gb_ref[...]) * yb).astype(BF16)

    out = pl.BlockSpec((S, tn), lambda j: (0, j))
    return pl.pallas_call(
        body, name="branch", grid=(D // tn,),
        in_specs=[pl.BlockSpec((S, DA), lambda j: (0, 0)), pl.BlockSpec((S, DB), lambda j: (0, 0)),
                  pl.BlockSpec((None, DA, tn), lambda j: (j, 0, 0)), pl.BlockSpec((None, DB, tn), lambda j: (j, 0, 0)),
                  pl.BlockSpec((S, tn), lambda j: (0, 5120 // tn + j)), pl.BlockSpec((S, tn), lambda j: (0, 7168 // tn + j))],
        out_specs=[out, out, out],
        out_shape=[jax.ShapeDtypeStruct((S, D), BF16)] * 3,
        compiler_params=_params(1),
    )(attn, gmlp, wpa_g, wpb_g, proj, proj)


def _out_ln1(merged, wout_g, x, gain, bias):
    tm = 256

    def body(m_ref, w_ref, x_ref, g_ref, b_ref, xh_ref, rs_ref, h_ref):
        pre = ALPHA * x_ref[...] + _dot(m_ref[...], w_ref[...])
        mean = jnp.mean(pre, axis=1, keepdims=True)
        cen = pre - mean
        var = jnp.mean(cen * cen, axis=1, keepdims=True)
        rstd = lax.rsqrt(var + LN_EPS)
        xhat = cen * rstd
        xh_ref[...] = xhat
        rs_ref[...] = jnp.broadcast_to(rstd, (tm, 128))
        h_ref[...] = (xhat * g_ref[...] + b_ref[...]).astype(BF16)

    row = pl.BlockSpec((tm, D), lambda i: (i, 0))
    vec = pl.BlockSpec((1, D), lambda i: (0, 0))
    return pl.pallas_call(
        body, name="out_ln1", grid=(S // tm,),
        in_specs=[row, pl.BlockSpec((D, D), lambda i: (0, 0)), row, vec, vec],
        out_specs=[row, pl.BlockSpec((tm, 128), lambda i: (i, 0)), row],
        out_shape=[jax.ShapeDtypeStruct((S, D), F32), jax.ShapeDtypeStruct((S, 128), F32),
                   jax.ShapeDtypeStruct((S, D), BF16)],
        compiler_params=_params(1),
    )(merged, wout_g, x, gain, bias)


def _ff1(h1b, w1_g, b1):
    tn = 512
    per = D // tn

    def body(h_ref, w_ref, b_ref, a_ref, r_ref):
        r = jnp.maximum(_dot(h_ref[...], w_ref[...]) + b_ref[...], 0.0)
        r_ref[...] = r.astype(BF16)
        a_ref[...] = (r * r).astype(BF16)

    out = pl.BlockSpec((S, tn), lambda j: (0, j))
    return pl.pallas_call(
        body, name="ff1", grid=(DFF // tn,),
        in_specs=[pl.BlockSpec((S, D), lambda j: (0, 0)),
                  pl.BlockSpec((None, D, tn), lambda j: (j // per, 0, j % per)),
                  pl.BlockSpec((1, tn), lambda j: (0, j))],
        out_specs=[out, out],
        out_shape=[jax.ShapeDtypeStruct((S, DFF), BF16)] * 2,
        compiler_params=_params(1),
    )(h1b, w1_g, b1)


def _ff2_ln2_loss(a, w2_g, xhat1, g1, b1, b2, g2, be2, target):
    tm, tk = 512, 1024
    nk = DFF // tk

    def body(a_ref, w_ref, xh_ref, g1_ref, b1_ref, b2_ref, g2_ref, be2_ref, t_ref, d_ref, db_ref, st_ref, acc):
        i, k = pl.program_id(0), pl.program_id(1)

        @pl.when(k == 0)
        def _():
            acc[...] = jnp.zeros_like(acc)

        @pl.when((i == 0) & (k == 0))
        def _():
            st_ref[...] = jnp.zeros_like(st_ref)

        acc[...] += _dot(a_ref[...], w_ref[...])

        @pl.when(k == nk - 1)
        def _():
            def rows_chunk(ci, carry):
                rows = pl.ds(pl.multiple_of(ci * 128, 128), 128)
                h1 = xh_ref[rows, :] * g1_ref[...] + b1_ref[...]
                pre = ALPHA * h1 + acc[rows, :] + b2_ref[...]
                mean = jnp.mean(pre, axis=1, keepdims=True)
                cen = pre - mean
                var = jnp.mean(cen * cen, axis=1, keepdims=True)
                rstd = lax.rsqrt(var + LN_EPS)
                xhat = cen * rstd
                y = xhat * g2_ref[...] + be2_ref[...]
                err = y - t_ref[rows, :]
                dy = err * (1.0 / D)
                g = dy * g2_ref[...]
                dpre = rstd * (g - jnp.mean(g, axis=1, keepdims=True)
                               - xhat * jnp.mean(g * xhat, axis=1, keepdims=True))
                d_ref[rows, :] = dpre
                db_ref[rows, :] = dpre.astype(BF16)
                st_ref[0:1, :] += jnp.sum(dy * xhat, axis=0, keepdims=True)
                st_ref[1:2, :] += jnp.sum(dy, axis=0, keepdims=True)
                st_ref[2:3, :] += jnp.sum(dpre, axis=0, keepdims=True)
                st_ref[3:4, :] += jnp.broadcast_to(jnp.sum(err * err).reshape(1, 1), (1, D))
                return carry

            lax.fori_loop(0, tm // 128, rows_chunk, 0)

    row = pl.BlockSpec((tm, D), lambda i, k: (i, 0))
    vec = pl.BlockSpec((1, D), lambda i, k: (0, 0))
    return pl.pallas_call(
        body, name="ff2_ln2_loss", grid=(S // tm, nk),
        in_specs=[pl.BlockSpec((tm, tk), lambda i, k: (i, k)), pl.BlockSpec((tk, D), lambda i, k: (k, 0)),
                  row, vec, vec, vec, vec, vec, row],
        out_specs=[row, row, pl.BlockSpec((8, D), lambda i, k: (0, 0))],
        out_shape=[jax.ShapeDtypeStruct((S, D), F32), jax.ShapeDtypeStruct((S, D), BF16),
                   jax.ShapeDtypeStruct((8, D), F32)],
        scratch_shapes=[pltpu.VMEM((tm, D), F32)],
        compiler_params=_params(2),
    )(a, w2_g, xhat1, g1, b1, b2, g2, be2, target)


def _grad_w(act, dout, name, ti, tj, sharded, after=None):
    m, n = act.shape[1], dout.shape[1]
    ns = n // N_CHIPS
    per = ns // tj if sharded else None

    def body(a_ref, b_ref, o_ref, at_scr):
        @pl.when(pl.program_id(1) == 0)
        def _():
            at_scr[...] = a_ref[...].T

        o_ref[...] = _dot(at_scr[...], b_ref[...])

    if sharded:
        out_spec = pl.BlockSpec((None, ti, tj), lambda i, j: (j // per, i, j % per))
        out_shape = jax.ShapeDtypeStruct((N_CHIPS, m, ns), F32)
    else:
        out_spec = pl.BlockSpec((ti, tj), lambda i, j: (i, j))
        out_shape = jax.ShapeDtypeStruct((m, n), F32)
    body, more_specs, more = _behind(body, 2, after)
    return pl.pallas_call(
        body, name=name, grid=(m // ti, n // tj),
        in_specs=[pl.BlockSpec((S, ti), lambda i, j: (0, i)), pl.BlockSpec((S, tj), lambda i, j: (0, j))] + more_specs,
        out_specs=out_spec, out_shape=out_shape,
        scratch_shapes=[pltpu.VMEM((ti, S), BF16)],
        compiler_params=_params(2),
    )(act, dout, *more)


def _d_ff1(dpre2b, w2_g, r, after=None):
    tn = 512

    def body(d_ref, w_ref, r_ref, o_ref, gb_ref):
        da = _dot_nt(d_ref[...], w_ref[...])
        dp = da * (2.0 * r_ref[...].astype(F32))
        o_ref[...] = dp.astype(BF16)
        gb_ref[...] = jnp.sum(dp, axis=0, keepdims=True)

    body, more_specs, more = _behind(body, 3, after)
    return pl.pallas_call(
        body, name="d_ff1", grid=(DFF // tn,),
        in_specs=[pl.BlockSpec((S, D), lambda j: (0, 0)), pl.BlockSpec((tn, D), lambda j: (j, 0)),
                  pl.BlockSpec((S, tn), lambda j: (0, j))] + more_specs,
        out_specs=[pl.BlockSpec((S, tn), lambda j: (0, j)), pl.BlockSpec((1, tn), lambda j: (0, j))],
        out_shape=[jax.ShapeDtypeStruct((S, DFF), BF16), jax.ShapeDtypeStruct((1, DFF), F32)],
        compiler_params=_params(1),
    )(dpre2b, w2_g, r, *more)


def _d_h1_ln1(dprea, w1_g, dpre2, xhat1, rstd1, g1, after=None):
    tm, tk = 512, 1024
    per = D // tk
    nk = DFF // tk

    def body(a_ref, w_ref, d2_ref, xh_ref, rs_ref, g_ref, d_ref, db_ref, st_ref, acc):
        i, k = pl.program_id(0), pl.program_id(1)

        @pl.when(k == 0)
        def _():
            acc[...] = jnp.zeros_like(acc)

        @pl.when((i == 0) & (k == 0))
        def _():
            st_ref[...] = jnp.zeros_like(st_ref)

        acc[...] += _dot_nt(a_ref[...], w_ref[...])

        @pl.when(k == nk - 1)
        def _():
            def rows_chunk(ci, carry):
                rows = pl.ds(pl.multiple_of(ci * 128, 128), 128)
                dh = ALPHA * d2_ref[rows, :] + acc[rows, :]
                xhat = xh_ref[rows, :]
                g = dh * g_ref[...]
                dpre = rs_ref[rows, 0:1] * (g - jnp.mean(g, axis=1, keepdims=True)
                                            - xhat * jnp.mean(g * xhat, axis=1, keepdims=True))
                d_ref[rows, :] = dpre
                db_ref[rows, :] = dpre.astype(BF16)
                st_ref[0:1, :] += jnp.sum(dh * xhat, axis=0, keepdims=True)
                st_ref[1:2, :] += jnp.sum(dh, axis=0, keepdims=True)
                return carry

            lax.fori_loop(0, tm // 128, rows_chunk, 0)

    row = pl.BlockSpec((tm, D), lambda i, k: (i, 0))
    body, more_specs, more = _behind(body, 6, after)
    return pl.pallas_call(
        body, name="d_h1_ln1", grid=(S // tm, nk),
        in_specs=[pl.BlockSpec((tm, tk), lambda i, k: (i, k)),
                  pl.BlockSpec((None, D, tk), lambda i, k: (k // per, 0, k % per)),
                  row, row, pl.BlockSpec((tm, 128), lambda i, k: (i, 0)), pl.BlockSpec((1, D), lambda i, k: (0, 0))]
        + more_specs,
        out_specs=[row, row, pl.BlockSpec((8, D), lambda i, k: (0, 0))],
        out_shape=[jax.ShapeDtypeStruct((S, D), F32), jax.ShapeDtypeStruct((S, D), BF16),
                   jax.ShapeDtypeStruct((8, D), F32)],
        scratch_shapes=[pltpu.VMEM((tm, D), F32)],
        compiler_params=_params(2),
    )(dprea, w1_g, dpre2, xhat1, rstd1, g1, *more)


def _d_merged(dpre1b, wout_g, proj, ya, yb):
    tm, tn = 512, 1024

    def body(d_ref, w_ref, ga_ref, gb_ref, ya_ref, yb_ref, dya_ref, dyb_ref, dga_ref, dgb_ref):
        dm = _dot_nt(d_ref[...], w_ref[...])
        sa = _sigmoid(ga_ref[...])
        sb = _sigmoid(gb_ref[...])
        dya_ref[...] = (dm * sa).astype(BF16)
        dyb_ref[...] = (dm * sb).astype(BF16)
        dga_ref[...] = (dm * ya_ref[...].astype(F32) * sa * (1.0 - sa)).astype(BF16)
        dgb_ref[...] = (dm * yb_ref[...].astype(F32) * sb * (1.0 - sb)).astype(BF16)

    tile = pl.BlockSpec((tm, tn), lambda i, j: (i, j))
    return pl.pallas_call(
        body, name="d_merged", grid=(S // tm, D // tn),
        in_specs=[pl.BlockSpec((tm, D), lambda i, j: (i, 0)), pl.BlockSpec((tn, D), lambda i, j: (j, 0)),
                  pl.BlockSpec((tm, tn), lambda i, j: (i, 5 + j)), pl.BlockSpec((tm, tn), lambda i, j: (i, 7 + j)),
                  tile, tile],
        out_specs=[tile] * 4,
        out_shape=[jax.ShapeDtypeStruct((S, D), BF16)] * 4,
        compiler_params=_params(2),
    )(dpre1b, wout_g, proj, proj, ya, yb)


def _d_branches(dya, dyb, wpa_g, wpb_g, after=None):
    tk = 512

    def body(da_ref, db_ref, wa_ref, wb_ref, oa_ref, ob_ref):
        @pl.when(pl.program_id(0) == 0)
        def _():
            oa_ref[...] = jnp.zeros_like(oa_ref)
            ob_ref[...] = jnp.zeros_like(ob_ref)

        oa_ref[...] += _dot_nt(da_ref[...], wa_ref[...])
        ob_ref[...] += _dot_nt(db_ref[...], wb_ref[...])

    body, more_specs, more = _behind(body, 4, after)
    return pl.pallas_call(
        body, name="d_branches", grid=(D // tk,),
        in_specs=[pl.BlockSpec((S, tk), lambda k: (0, k)), pl.BlockSpec((S, tk), lambda k: (0, k)),
                  pl.BlockSpec((None, DA, tk), lambda k: (k, 0, 0)), pl.BlockSpec((None, DB, tk), lambda k: (k, 0, 0))]
        + more_specs,
        out_specs=[pl.BlockSpec((S, DA), lambda k: (0, 0)), pl.BlockSpec((S, DB), lambda k: (0, 0))],
        out_shape=[jax.ShapeDtypeStruct((S, DA), F32), jax.ShapeDtypeStruct((S, DB), F32)],
        compiler_params=_params(1),
    )(dya, dyb, wpa_g, wpb_g, *more)


def _gmlp_bwd(proj, dgmlp, ws, ws_t, bsp_b, gain_v, bias_v):
    def body(u_ref, vb_ref, dg_ref, ws_ref, wst_ref, bsp_ref, g_ref, be_ref, duv_ref, gws_ref, gbs_ref, st_ref):
        @pl.when(pl.program_id(0) == 0)
        def _():
            gws_ref[...] = jnp.zeros_like(gws_ref)
            gbs_ref[...] = jnp.zeros_like(gbs_ref)
            st_ref[...] = jnp.zeros_like(st_ref)

        u, tu, u_act, vb, tv, rstd, xhat, vn = _gmlp_parts(u_ref, vb_ref, g_ref, be_ref)
        dg = dg_ref[...]
        dz = dg * u_act
        row = lax.broadcasted_iota(jnp.int32, (128, 128), 0)
        col = lax.broadcasted_iota(jnp.int32, (128, 128), 1)
        causal = row >= col
        causal_t = row <= col
        dvn_parts = []
        z_parts = []
        for g in range(NH):
            cols = slice(g * 128, (g + 1) * 128)
            vng = vn[:, cols].astype(BF16)
            dzg = dz[:, cols]
            dzb = dzg.astype(BF16)
            wsg = jnp.where(causal, ws_ref[g], 0.0).astype(BF16)
            wsg_t = jnp.where(causal_t, wst_ref[g], 0.0).astype(BF16)
            z_parts.append(_dot(wsg, vng) + bsp_ref[g])
            gws_ref[g] += jnp.where(causal, _dot_nt(dzb, vng), 0.0)
            gbs_ref[g] += jnp.broadcast_to(jnp.sum(dzg, axis=1, keepdims=True), (128, 128))
            dvn_parts.append(_dot(wsg_t, dzb))
        z = jnp.concatenate(z_parts, axis=1)
        dvn = jnp.concatenate(dvn_parts, axis=1)
        du = dg * z * _gelu_grad(u, tu)
        st_ref[0:1, :] += jnp.sum(dvn * xhat, axis=0, keepdims=True)
        st_ref[1:2, :] += jnp.sum(dvn, axis=0, keepdims=True)
        gg = dvn * g_ref[...]
        dgv = rstd * (gg - jnp.mean(gg, axis=1, keepdims=True) - xhat * jnp.mean(gg * xhat, axis=1, keepdims=True))
        dvb = dgv * _gelu_grad(vb, tv)
        duv_ref[:, 0:DB] = du.astype(BF16)
        duv_ref[:, DB:2 * DB] = dvb.astype(BF16)

    full3 = pl.BlockSpec((NH, 128, 128), lambda c: (0, 0, 0))
    vec = pl.BlockSpec((1, DB), lambda c: (0, 0))
    return pl.pallas_call(
        body, name="gmlp_bwd", grid=(NBLK,),
        in_specs=[pl.BlockSpec((128, DB), lambda c: (c, 3)), pl.BlockSpec((128, DB), lambda c: (c, 4)),
                  pl.BlockSpec((128, DB), lambda c: (c, 0)), full3, full3, full3, vec, vec],
        out_specs=[pl.BlockSpec((128, 2 * DB), lambda c: (c, 0)), full3, full3, pl.BlockSpec((8, DB), lambda c: (0, 0))],
        out_shape=[jax.ShapeDtypeStruct((S, 2 * DB), BF16), jax.ShapeDtypeStruct((NH, 128, 128), F32),
                   jax.ShapeDtypeStruct((NH, 128, 128), F32), jax.ShapeDtypeStruct((8, DB), F32)],
        compiler_params=_params(1),
    )(proj, proj, dgmlp, ws, ws_t, bsp_b, gain_v, bias_v)


def _rel_bias_grad(ds_sums):
    buckets = jnp.asarray(np.stack([_bucket_tile(d) for _, d in PATTERNS]))

    def body(bk_ref, ds_ref, o_ref):
        row = lax.broadcasted_iota(jnp.int32, (N_BUCKETS, 128), 0)
        lane = lax.broadcasted_iota(jnp.int32, (N_BUCKETS, 128), 1)

        def one_bucket(t, out):
            hits = [bk_ref[p] == t for p in range(3)]
            for h in range(NH):
                tot = jnp.zeros((128, 256), F32)
                for p in range(3):
                    tot = tot + jnp.where(hits[p], ds_ref[p, h], 0.0)
                out = jnp.where((row == t) & (lane == h), jnp.sum(tot), out)
            return out

        o_ref[...] = lax.fori_loop(0, N_BUCKETS, one_bucket, jnp.zeros((N_BUCKETS, 128), F32))

    return pl.pallas_call(
        body, name="rel_bias_grad",
        in_specs=[pl.BlockSpec(memory_space=pltpu.VMEM)] * 2, out_specs=pl.BlockSpec(memory_space=pltpu.VMEM),
        out_shape=jax.ShapeDtypeStruct((N_BUCKETS, 128), F32),
        compiler_params=pltpu.CompilerParams(vmem_limit_bytes=VMEM_LIMIT),
    )(buckets, ds_sums)


def _d_x(dproj, win_g, dpre1, after=None):
    tm, tk = 512, 2304
    per = 2304 // tk
    nk = DIN // tk

    def body(a_ref, w_ref, d_ref, o_ref, acc):
        k = pl.program_id(1)

        @pl.when(k == 0)
        def _():
            acc[...] = ALPHA * d_ref[...]

        acc[...] += _dot_nt(a_ref[...], w_ref[...])

        @pl.when(k == nk - 1)
        def _():
            o_ref[...] = acc[...]

    row = pl.BlockSpec((tm, D), lambda i, k: (i, 0))
    body, more_specs, more = _behind(body, 3, after)
    return pl.pallas_call(
        body, name="d_x", grid=(S // tm, nk),
        in_specs=[pl.BlockSpec((tm, tk), lambda i, k: (i, k)),
                  pl.BlockSpec((None, D, tk), lambda i, k: (k // per, 0, k % per)), row] + more_specs,
        out_specs=row, out_shape=jax.ShapeDtypeStruct((S, D), F32),
        scratch_shapes=[pltpu.VMEM((tm, D), F32)],
        compiler_params=_params(2),
    )(dproj, win_g, dpre1, *more)


def _adamw(w, g, m, v, name):
    rows, cols = w.shape
    tm = max(t for t in range(8, 257, 8) if rows % t == 0)

    def body(w_ref, g_ref, m_ref, v_ref, d_ref, nm_ref, nv_ref, go_ref):
        g = g_ref[...]
        m = ADAM_B1 * m_ref[...] + (1.0 - ADAM_B1) * g
        v = ADAM_B2 * v_ref[...] + (1.0 - ADAM_B2) * (g * g)
        m_hat = m / (1.0 - ADAM_B1 ** ADAM_STEP)
        v_hat = v / (1.0 - ADAM_B2 ** ADAM_STEP)
        d_ref[...] = -ADAM_LR * (m_hat / (jnp.sqrt(v_hat) + ADAM_EPS) + ADAM_WD * w_ref[...])
        nm_ref[...] = m
        nv_ref[...] = v
        go_ref[...] = g

    spec = pl.BlockSpec((tm, cols), lambda i: (i, 0))
    return pl.pallas_call(
        body, name=name, grid=(rows // tm,), in_specs=[spec] * 4, out_specs=[spec] * 4,
        out_shape=[jax.ShapeDtypeStruct((rows, cols), F32)] * 4, compiler_params=_params(1),
    )(w, g, m, v)


def _position():
    x, y, c = lax.axis_index("x"), lax.axis_index("y"), lax.axis_index("c")
    chips = [(1 - x, y), (x, 1 - y), (1 - x, 1 - y)]
    return x, y, c, chips


def _remote(src, dst, send_sems, recv_sems, k, to):
    return pltpu.make_async_remote_copy(src_ref=src, dst_ref=dst, send_sem=send_sems.at[k], recv_sem=recv_sems.at[k],
                                        device_id=to, device_id_type=MESH)


def _place_shard(w, name, after=None):
    rows, cols = w.shape
    tm = 256
    x, y = lax.axis_index("x"), lax.axis_index("y")

    def body(chip_ref, w_ref, o_ref):
        o_ref[...] = w_ref[...].astype(BF16)

    more_specs, more = ([ANY], [after]) if after is not None else ([], [])
    if after is not None:
        inner = body
        body = lambda chip_ref, w_ref, after_ref, o_ref: inner(chip_ref, w_ref, o_ref)
    return pl.pallas_call(
        body, name=name,
        grid_spec=pltpu.PrefetchScalarGridSpec(
            num_scalar_prefetch=1, grid=(rows // tm,),
            in_specs=[pl.BlockSpec((tm, cols), lambda i, chip: (i, 0))] + more_specs,
            out_specs=pl.BlockSpec((None, tm, cols), lambda i, chip: (chip[0], i, 0))),
        out_shape=jax.ShapeDtypeStruct((N_CHIPS, rows, cols), BF16),
        compiler_params=_params(1),
    )(jnp.reshape(2 * x + y, (1,)).astype(jnp.int32), w, *more)


def _to_bf16(x, name, after=None):
    tm = 256

    def body(x_ref, o_ref):
        o_ref[...] = x_ref[...].astype(BF16)

    spec = pl.BlockSpec((tm, x.shape[1]), lambda i: (i, 0))
    body, more_specs, more = _behind(body, 1, after)
    return pl.pallas_call(
        body, name=name, grid=(x.shape[0] // tm,), in_specs=[spec] + more_specs, out_specs=spec,
        out_shape=jax.ShapeDtypeStruct(x.shape, BF16), compiler_params=_params(1),
    )(x, *more)


HBM = pl.BlockSpec(memory_space=pltpu.HBM)
SEM = pl.BlockSpec(memory_space=pltpu.SEMAPHORE)
EFFECT = pltpu.SideEffectType.DATAFLOW_SIDE_EFFECTING


def _comm_call(name, body, bufs, sems_in, sems_out, after=None, token=False):
    nb, ns, no = len(bufs), len(sems_in), len(sems_out)
    n_in = nb + ns + (after is not None)

    def wrapped(*refs):
        body(refs[:nb], refs[nb:nb + ns], refs[n_in + nb:n_in + nb + no])
        if token:
            refs[-1][...] = jnp.zeros((8, 128), F32)

    outs = pl.pallas_call(
        wrapped, name=name,
        in_specs=[HBM] * nb + [SEM] * ns + ([ANY] if after is not None else []),
        out_specs=[HBM] * nb + [SEM] * no + ([pl.BlockSpec(memory_space=pltpu.VMEM)] if token else []),
        out_shape=[pltpu.HBM(b.shape, b.dtype) for b in bufs] + [pltpu.SemaphoreType.DMA((k,)) for k in sems_out]
        + ([jax.ShapeDtypeStruct((8, 128), F32)] if token else []),
        input_output_aliases={i: i for i in range(nb)},
        compiler_params=pltpu.CompilerParams(has_side_effects=EFFECT),
    )(*[pltpu.with_memory_space_constraint(b, pltpu.HBM) for b in bufs], *sems_in, *([after] if after is not None else []))
    return list(outs[:nb]), list(outs[nb:nb + no]), (outs[-1] if token else None)


RING_STAGES = {"ici_near": 2, "ici_far": 2, "d2d_near": 2, "d2d_far": 1}


def _ring_copies(buf, send_sems, recv_sems, k0, stage):
    x, y, c, _ = _position()
    hr = buf.shape[1] // 2
    qr = hr // 2
    half = lambda chip, h: buf.at[chip, pl.ds(h * hr, hr), :]
    quarter = lambda chip, h, q: buf.at[chip, pl.ds(h * hr + q * qr, qr), :]
    mine, x_chip, y_chip, far_chip = 2 * x + y, 2 * (1 - x) + y, 2 * x + (1 - y), 2 * (1 - x) + (1 - y)
    to_x, to_y, sibling = (1 - x, y, c), (x, 1 - y, c), (x, y, 1 - c)
    if stage == "ici_near":
        moves = [(half(mine, c), to_x, half(x_chip, c)), (half(mine, c), to_y, half(y_chip, c))]
    elif stage == "ici_far":
        moves = [(quarter(x_chip, c, 0), to_y, quarter(far_chip, c, 0)),
                 (quarter(y_chip, c, 1), to_x, quarter(far_chip, c, 1))]
    elif stage == "d2d_near":
        moves = [(half(x_chip, c), sibling, half(x_chip, 1 - c)), (half(y_chip, c), sibling, half(y_chip, 1 - c))]
    else:
        moves = [(half(far_chip, c), sibling, half(far_chip, 1 - c))]
    sends = [_remote(src, src, send_sems, recv_sems, k0 + i, to) for i, (src, to, _) in enumerate(moves)]
    arrivals = [_remote(got, got, send_sems, recv_sems, k0 + i, (x, y, c)) for i, (_, _, got) in enumerate(moves)]
    return sends, arrivals


def _ring_call(name, groups, actions, after=None):
    tags = list(dict.fromkeys(t for _, t, _ in actions))
    counts = {t: len(groups[t]["bufs"]) for t in tags}
    first = {t: sum(counts[u] for u in tags[:i]) for i, t in enumerate(tags)}
    waits = [(t, s) for v, t, s in actions if v == "wait"]
    starts = [(t, s) for v, t, s in actions if v == "start"]

    def body(bufs, sems_in, sems_out):
        for verb, t, s in actions:
            at, sems = (starts.index((t, s)), sems_out) if verb == "start" else (waits.index((t, s)), sems_in)
            for w in range(counts[t]):
                sends, arrivals = _ring_copies(bufs[first[t] + w], sems[2 * at], sems[2 * at + 1], RING_STAGES[s] * w, s)
                if verb == "start":
                    for cp in sends:
                        cp.start()
                else:
                    for cp in arrivals:
                        cp.wait_recv()
                    for cp in sends:
                        cp.wait_send()

    bufs, sems, token = _comm_call(
        name, body, [b for t in tags for b in groups[t]["bufs"]],
        [sem for t, s in waits for sem in groups[t]["sems"][s]],
        [RING_STAGES[s] * counts[t] for t, s in starts for _ in (0, 1)], after, token=True)
    for t in tags:
        groups[t]["bufs"] = bufs[first[t]:first[t] + counts[t]]
    for t, s in waits:
        del groups[t]["sems"][s]
    for i, (t, s) in enumerate(starts):
        groups[t]["sems"][s] = (sems[2 * i], sems[2 * i + 1])
    return token


def _cx_copies(src, dst, send_sems, recv_sems, k0):
    x, y, c, chips = _position()
    sends = [_remote(src.at[2 * cx + cy], dst.at[2 * x + y], send_sems, recv_sems, k0 + j, (cx, cy, c))
             for j, (cx, cy) in enumerate(chips)]
    arrivals = [_remote(dst.at[2 * cx + cy], dst.at[2 * cx + cy], send_sems, recv_sems, k0 + j, (x, y, c))
                for j, (cx, cy) in enumerate(chips)]
    return sends, arrivals


def _cx_start(name, pair_sums):
    n = len(pair_sums)
    landing = [lax.empty(p.shape, p.dtype) for p in pair_sums]

    def body(bufs, _, sems):
        for w in range(n):
            for cp in _cx_copies(bufs[w], bufs[n + w], sems[0], sems[1], 3 * w)[0]:
                cp.start()

    bufs, sems, token = _comm_call(name, body, list(pair_sums) + landing, [], [3 * n, 3 * n], token=True)
    return (bufs, sems), token


def _cx_wait(name, state, after):
    bufs, sems = state
    n = len(bufs) // 2

    def body(refs, sems_in, _):
        for w in range(n):
            sends, arrivals = _cx_copies(refs[w], refs[n + w], sems_in[0], sems_in[1], 3 * w)
            for cp in arrivals:
                cp.wait_recv()
            for cp in sends:
                cp.wait_send()

    bufs, _, _ = _comm_call(name, body, bufs, sems, [], after)
    return bufs[:n], bufs[n:]


def _px_copies(src, dst, send_sems, recv_sems, k):
    x, y, c, _ = _position()
    hr = src.shape[1] // 2
    send = _remote(src.at[:, pl.ds((1 - c) * hr, hr), :], dst, send_sems, recv_sems, k, (x, y, 1 - c))
    arrival = _remote(dst, dst, send_sems, recv_sems, k, (x, y, c))
    return send, arrival


def _px_start(name, grads):
    n = len(grads)
    landing = [lax.empty((N_CHIPS, g.shape[1] // 2, g.shape[2]), F32) for g in grads]

    def body(bufs, _, sems):
        for w in range(n):
            _px_copies(bufs[w], bufs[n + w], sems[0], sems[1], w)[0].start()

    bufs, sems, token = _comm_call(name, body, list(grads) + landing, [], [n, n], token=True)
    return (bufs, sems), token


def _px_wait(name, state, after):
    bufs, sems = state
    n = len(bufs) // 2

    def body(refs, sems_in, _):
        for w in range(n):
            send, arrival = _px_copies(refs[w], refs[n + w], sems_in[0], sems_in[1], w)
            arrival.wait_recv()
            send.wait_send()

    bufs, _, _ = _comm_call(name, body, bufs, sems, [], after)
    return bufs[:n], bufs[n:]


def _pair_sum(grad, got, name):
    _, rows, cols = grad.shape
    hr = rows // 2
    tm = min(hr, 256)
    nb = hr // tm
    c = lax.axis_index("c")

    def body(c_ref, g_ref, o_ref, out_ref):
        out_ref[...] = (g_ref[...] + o_ref[...]).astype(BF16)

    return pl.pallas_call(
        body, name=name,
        grid_spec=pltpu.PrefetchScalarGridSpec(
            num_scalar_prefetch=1, grid=(N_CHIPS, nb),
            in_specs=[pl.BlockSpec((None, tm, cols), lambda s, i, c_ref: (s, c_ref[0] * nb + i, 0)),
                      pl.BlockSpec((None, tm, cols), lambda s, i, c_ref: (s, i, 0))],
            out_specs=pl.BlockSpec((None, tm, cols), lambda s, i, c_ref: (s, i, 0))),
        out_shape=jax.ShapeDtypeStruct((N_CHIPS, hr, cols), BF16),
        compiler_params=_params(2),
    )(jnp.reshape(c, (1,)).astype(jnp.int32), grad, got)


def _chip_sum(parts, pair_sums, name):
    _, hr, cols = parts.shape
    tm = min(hr, 256)
    nb = hr // tm
    x, y, c = lax.axis_index("x"), lax.axis_index("y"), lax.axis_index("c")

    def body(pos_ref, p_ref, own_ref, o_ref):
        chip = pos_ref[0]
        own = own_ref[...].astype(F32)
        term = lambda s: jnp.where(chip == s, own, p_ref[s].astype(F32))
        o_ref[...] = ((term(0) + term(1)) + term(2)) + term(3)

    return pl.pallas_call(
        body, name=name,
        grid_spec=pltpu.PrefetchScalarGridSpec(
            num_scalar_prefetch=1, grid=(nb,),
            in_specs=[pl.BlockSpec((N_CHIPS, tm, cols), lambda i, pos: (0, i, 0)),
                      pl.BlockSpec((None, tm, cols), lambda i, pos: (pos[0], i, 0))],
            out_specs=pl.BlockSpec((tm, cols), lambda i, pos: (pos[1] * nb + i, 0))),
        out_shape=jax.ShapeDtypeStruct((2 * hr, cols), F32), compiler_params=_params(1),
    )(jnp.stack([2 * x + y, c]).astype(jnp.int32), parts, pair_sums)


def _share_halves(bufs, name):
    n = len(bufs)

    def body(*refs):
        outs = refs[n:2 * n]
        send_sems, recv_sems = refs[2 * n:]
        x, y, c, _ = _position()
        copies = []
        for w in range(n):
            hr = outs[w].shape[0] // 2
            mine = outs[w].at[pl.ds(c * hr, hr), :]
            cp = _remote(mine, mine, send_sems, recv_sems, w, (x, y, 1 - c))
            cp.start()
            copies.append(cp)
        for w in range(n):
            hr = outs[w].shape[0] // 2
            theirs = outs[w].at[pl.ds((1 - c) * hr, hr), :]
            _remote(theirs, theirs, send_sems, recv_sems, w, (x, y, c)).wait_recv()
        for cp in copies:
            cp.wait_send()

    return pl.pallas_call(
        body, name=name,
        in_specs=[ANY] * n, out_specs=[ANY] * n,
        out_shape=[jax.ShapeDtypeStruct(b.shape, b.dtype) for b in bufs],
        input_output_aliases={w: w for w in range(n)},
        scratch_shapes=[pltpu.SemaphoreType.DMA((n,)), pltpu.SemaphoreType.DMA((n,))],
    )(*bufs)


def _allreduce_small(g):
    rows = g.shape[0]

    def body(g_ref, o_ref, sib, slots, send_sems, recv_sems):
        x, y, c, chips = _position()
        me = (x, y, c)
        my_chip = 2 * x + y
        pair = _remote(g_ref, sib, send_sems, recv_sems, 0, (x, y, 1 - c))
        pair.start()
        pair.wait()
        slots[my_chip] = g_ref[...] + sib[...]
        sent = []
        for j, (cx, cy) in enumerate(chips):
            cp = _remote(slots.at[my_chip], slots.at[my_chip], send_sems, recv_sems, 1 + j, (cx, cy, c))
            cp.start()
            sent.append(cp)
        for j, (cx, cy) in enumerate(chips):
            got = slots.at[2 * cx + cy]
            _remote(got, got, send_sems, recv_sems, 1 + j, me).wait_recv()
        for cp in sent:
            cp.wait_send()
        o_ref[...] = ((slots[0] + slots[1]) + slots[2]) + slots[3]

    vm = pl.BlockSpec(memory_space=pltpu.VMEM)
    return pl.pallas_call(
        body, name="allreduce_small",
        in_specs=[vm], out_specs=vm, out_shape=jax.ShapeDtypeStruct((rows, 128), F32),
        scratch_shapes=[pltpu.VMEM((rows, 128), F32), pltpu.VMEM((N_CHIPS, rows, 128), F32),
                        pltpu.SemaphoreType.DMA((4,)), pltpu.SemaphoreType.DMA((4,))],
        compiler_params=pltpu.CompilerParams(vmem_limit_bytes=VMEM_LIMIT),
    )(g)


_SMALL =("rel_bias", "ln_v_gain", "ln_v_bias", "w_spatial", "b_spatial", "ln1_gain", "ln1_bias",
          "b_ff1", "b_ff2", "ln2_gain", "ln2_bias")
_SMALL_ROWS = 1200
_LOSS_AT = (152832 // 128, 0)


def _pack_small(parts):
    flat = jnp.concatenate([parts[k].reshape(-1).astype(F32) for k in _SMALL])
    flat = jnp.pad(flat, (0, _SMALL_ROWS * 128 - flat.shape[0]))
    return flat.reshape(_SMALL_ROWS, 128)


def _unpack_small(packed, like):
    flat = packed.reshape(-1)
    out, at = {}, 0
    for k in _SMALL:
        n = math.prod(like[k].shape)
        out[k] = flat[at:at + n].reshape(like[k].shape)
        at += n
    return out


def kernel(x, w_in, rel_bias, ln_v_gain, ln_v_bias, w_spatial, b_spatial, w_proj_a, w_proj_b, w_out, ln1_gain, ln1_bias, w_ff1, b_ff1, w_ff2, b_ff2, ln2_gain, ln2_bias, loss_target, m_w_in, m_rel_bias, m_ln_v_gain, m_ln_v_bias, m_w_spatial, m_b_spatial, m_w_proj_a, m_w_proj_b, m_w_out, m_ln1_gain, m_ln1_bias, m_w_ff1, m_b_ff1, m_w_ff2, m_b_ff2, m_ln2_gain, m_ln2_bias, v_w_in, v_rel_bias, v_ln_v_gain, v_ln_v_bias, v_w_spatial, v_b_spatial, v_w_proj_a, v_w_proj_b, v_w_out, v_ln1_gain, v_ln1_bias, v_w_ff1, v_b_ff1, v_w_ff2, v_b_ff2, v_ln2_gain, v_ln2_bias):
    args = dict(locals())
    big = ("w_in", "w_proj_a", "w_proj_b", "w_out", "w_ff1", "w_ff2")
    weights = ("w_in", "rel_bias", "ln_v_gain", "ln_v_bias", "w_spatial", "b_spatial", "w_proj_a", "w_proj_b", "w_out",
               "ln1_gain", "ln1_bias", "w_ff1", "b_ff1", "w_ff2", "b_ff2", "ln2_gain", "ln2_bias")

    xs = x[0]
    target = loss_target[0]

    ring = {"a": {"bufs": [_place_shard(w_in[0], "place_w_in")], "sems": {}}}
    tok = _ring_call("allgather_a_near", ring, [("start", "a", "ici_near")])
    placed = [_place_shard(args[k][0], f"place_{k}", after=tok) for k in big[1:]]
    for tag, bufs in (("b", placed[0:3]), ("c", placed[3:4]), ("d", placed[4:5])):
        ring[tag] = {"bufs": bufs, "sems": {}}
    xb = _to_bf16(xs, "x_to_bf16", after=tok)
    _ring_call("allgather_a_far", ring, [("wait", "a", "ici_near"), ("start", "a", "ici_far"), ("start", "a", "d2d_near"),
                                         ("start", "b", "ici_near"), ("start", "c", "ici_near")], after=xb)
    _ring_call("allgather_a_last", ring, [("wait", "a", "ici_far"), ("start", "a", "d2d_far")])
    _ring_call("allgather_a_done", ring, [("wait", "a", "d2d_near"), ("wait", "a", "d2d_far")])
    (win_g,) = ring["a"]["bufs"]

    proj = _proj(xb, win_g)
    _ring_call("allgather_b_far", ring, [("wait", "b", "ici_near"), ("start", "b", "ici_far"), ("start", "b", "d2d_near"),
                                         ("start", "d", "ici_near")], after=proj)
    ws = w_spatial[0]
    ws_t = jnp.transpose(ws, (0, 2, 1))
    bsp_b = jnp.broadcast_to(b_spatial[0][:, :, None], (NH, 128, 128))
    gmlp = _gmlp_fwd(proj, ws, bsp_b, ln_v_gain, ln_v_bias)
    attn, lse = _attention_fwd(proj, rel_bias)
    _ring_call("allgather_b_last_c_far", ring,
               [("wait", "b", "ici_far"), ("start", "b", "d2d_far"),
                ("wait", "c", "ici_near"), ("start", "c", "ici_far"), ("start", "c", "d2d_near")], after=attn)
    _ring_call("allgather_b_done", ring, [("wait", "b", "d2d_near"), ("wait", "b", "d2d_far")])
    wpa_g, wpb_g, wout_g = ring["b"]["bufs"]
    wout_full = wout_g.reshape(D, D)
    ya, yb, merged = _branch(attn, gmlp, wpa_g, wpb_g, proj)
    xhat1, rstd1, h1b = _out_ln1(merged, wout_full, xs, ln1_gain, ln1_bias)
    _ring_call("allgather_c_last_d_far", ring,
               [("wait", "c", "ici_far"), ("start", "c", "d2d_far"),
                ("wait", "d", "ici_near"), ("start", "d", "ici_far"), ("start", "d", "d2d_near")], after=h1b)
    _ring_call("allgather_c_done", ring, [("wait", "c", "d2d_near"), ("wait", "c", "d2d_far")])
    (w1_g,) = ring["c"]["bufs"]
    a, r = _ff1(h1b, w1_g, b_ff1)
    _ring_call("allgather_d_last", ring, [("wait", "d", "ici_far"), ("start", "d", "d2d_far")], after=a)
    _ring_call("allgather_d_done", ring, [("wait", "d", "d2d_near"), ("wait", "d", "d2d_far")])
    (w2_g,) = ring["d"]["bufs"]
    w2_full = w2_g.reshape(DFF, D)
    dpre2, dpre2b, st2 = _ff2_ln2_loss(a, w2_full, xhat1, ln1_gain, ln1_bias, b_ff2, ln2_gain, ln2_bias, target)

    def pair_and_chip(tag, state, after):
        local, from_sibling = _px_wait(f"pair_exchange_wait_{tag}", state, after)
        pair_sums = [_pair_sum(g, o, f"pair_sum_{tag}_{i}") for i, (g, o) in enumerate(zip(local, from_sibling))]
        return _cx_start(f"chip_exchange_start_{tag}", pair_sums)

    g_w2 = _grad_w(a, dpre2b, "grad_w_ff2", 512, 2048, False)
    px, tok = _px_start("pair_exchange_start_w_ff2", [g_w2.reshape(N_CHIPS, DFF // N_CHIPS, D)])
    dprea, g_b1 = _d_ff1(dpre2b, w2_full, r, after=tok)
    cx_w2, tok = pair_and_chip("w_ff2", px, dprea)
    g_w1 = _grad_w(h1b, dprea, "grad_w_ff1", 512, 2048, True, after=tok)
    px, tok = _px_start("pair_exchange_start_w_ff1", [g_w1])
    dpre1, dpre1b, st1 = _d_h1_ln1(dprea, w1_g, dpre2, xhat1, rstd1, ln1_gain, after=tok)
    cx_w1, tok = pair_and_chip("w_ff1", px, dpre1b)
    g_wout = _grad_w(merged, dpre1b, "grad_w_out", 512, 2048, False, after=tok)
    dya, dyb, dga, dgb = _d_merged(dpre1b, wout_full, proj, ya, yb)
    g_wpa = _grad_w(attn, dya, "grad_w_proj_a", 1024, 512, True)
    g_wpb = _grad_w(gmlp, dyb, "grad_w_proj_b", 1024, 512, True)
    px, tok = _px_start("pair_exchange_start_b", [g_wpa, g_wpb, g_wout.reshape(N_CHIPS, D // N_CHIPS, D)])
    dattn, dgmlp = _d_branches(dya, dyb, wpa_g, wpb_g, after=tok)
    duv, g_ws, g_bs, stv = _gmlp_bwd(proj, dgmlp, ws, ws_t, bsp_b, ln_v_gain, ln_v_bias)
    cx_b, tok = pair_and_chip("b", px, duv)
    dq, dk, dv, ds_sums = _attention_bwd(proj, dattn, attn, lse, rel_bias, after=tok)
    g_rb = _rel_bias_grad(ds_sums)[:, :NH]

    small_g = dict(rel_bias=g_rb, ln_v_gain=stv[0], ln_v_bias=stv[1], w_spatial=g_ws, b_spatial=g_bs[:, :, 0],
                   ln1_gain=st1[0], ln1_bias=st1[1], b_ff1=g_b1, b_ff2=st2[2], ln2_gain=st2[0], ln2_bias=st2[1])
    gs = _allreduce_small(_pack_small(small_g).at[_LOSS_AT].set(st2[3, 0]))
    ds_, ms_, vs_, _ = _adamw(_pack_small({k: args[k] for k in _SMALL}), gs,
                           _pack_small({k: args["m_" + k] for k in _SMALL}),
                           _pack_small({k: args["v_" + k] for k in _SMALL}), "adamw_small")
    like = {k: args[k] for k in _SMALL}
    grads, deltas, new_m, new_v = (_unpack_small(t, like) for t in (gs, ds_, ms_, vs_))

    dproj = jnp.concatenate([dq, dk, dv, duv, dga, dgb], axis=1)
    g_win = _grad_w(xb, dproj, "grad_w_in", 512, 2304, True, after=gs)
    px, tok = _px_start("pair_exchange_start_w_in", [g_win])
    grad_x = _d_x(dproj, win_g, dpre1, after=tok)
    cx_in, tok = pair_and_chip("w_in", px, grad_x)

    def reduce_finish(tag, state, names, after):
        pair_sums, from_chips = _cx_wait(f"chip_exchange_wait_{tag}", state, after)
        halves = [_chip_sum(p, own, f"chip_sum_{k}") for p, own, k in zip(from_chips, pair_sums, names)]
        last = None
        for k, g in zip(names, _share_halves(halves, f"share_halves_{tag}")):
            d_, m_, v_, g_ = _adamw(args[k][0], g, args["m_" + k][0], args["v_" + k][0], f"adamw_{k}")
            grads[k], deltas[k], new_m[k], new_v[k] = g_[None], d_[None], m_[None], v_[None]
            last = d_
        return last

    done = reduce_finish("w_ff2", cx_w2, ["w_ff2"], tok)
    done = reduce_finish("w_ff1", cx_w1, ["w_ff1"], done)
    done = reduce_finish("b", cx_b, ["w_proj_a", "w_proj_b", "w_out"], done)
    reduce_finish("w_in", cx_in, ["w_in"], done)

    loss = gs[_LOSS_AT] * (0.5 / D)
    return (loss, grad_x[None], *[grads[k] for k in weights], *[deltas[k] for k in weights],
            *[new_m[k] for k in weights], *[new_v[k] for k in weights])
```

```python
import functools
import math

import numpy as np
import jax
import jax.numpy as jnp
from jax import lax
from jax.experimental import pallas as pl
from jax.experimental.pallas import tpu as pltpu

F32 = jnp.float32
BF16 = jnp.bfloat16

S = 2048
D = 2048
DA = 1024
DB = 1024
DFF = 8192
DIN = 9216
NH = 8
HD = 128
NBLK = 16
PATTERNS = ((128, 1), (512, 4), (2048, 16))
N_BUCKETS = 32
MAX_DISTANCE = 2048
ALPHA = 2.0 ** 0.25
LN_EPS = 1e-5
NEG_INF = -1e30
SCALE = HD ** -0.5
N_CHIPS = 4

ADAM_LR = 0.001
ADAM_B1 = 0.9
ADAM_B2 = 0.999
ADAM_EPS = 1e-08
ADAM_WD = 0.01
ADAM_STEP = 10

VMEM_LIMIT = 56 * 1024 * 1024
MESH = pl.DeviceIdType.MESH
ANY = pl.BlockSpec(memory_space=pl.ANY)


def _params(n_axes, vmem=VMEM_LIMIT):
    return pltpu.CompilerParams(dimension_semantics=("arbitrary",) * n_axes, vmem_limit_bytes=vmem)


def _bucket_tile(dilation):
    qi = np.arange(128)[:, None]
    kj = np.arange(256)[None, :]
    n = np.clip(128 + qi - kj, 0, 128) * dilation
    max_exact = N_BUCKETS // 2
    nf = np.maximum(n, 1).astype(np.float32)
    large = max_exact + (np.log(nf / np.float32(max_exact)) / np.float32(math.log(MAX_DISTANCE / max_exact))
                         * np.float32(N_BUCKETS - max_exact)).astype(np.int32)
    large = np.minimum(large, N_BUCKETS - 1)
    return np.where(n < max_exact, n, large).astype(np.int32)


def _gelu(x):
    c = math.sqrt(2.0 / math.pi)
    t = jnp.tanh(c * (x + 0.044715 * x * x * x))
    return 0.5 * x * (1.0 + t), t


def _gelu_grad(x, t):
    c = math.sqrt(2.0 / math.pi)
    return 0.5 * (1.0 + t) + 0.5 * x * (1.0 - t * t) * c * (1.0 + 3.0 * 0.044715 * x * x)


def _sigmoid(x):
    return 1.0 / (1.0 + jnp.exp(-x))


def _dot(a, b):
    return jnp.dot(a, b, preferred_element_type=F32)


def _behind(body, n_in, after):
    if after is None:
        return body, [], []
    return (lambda *refs: body(*refs[:n_in], *refs[n_in + 1:])), [ANY], [after]


def _dot_nt(a, b):
    return lax.dot_general(a, b, (((1,), (1,)), ((), ())), preferred_element_type=F32)


def _proj(xb, win_g):
    tn = 768
    per = 2304 // tn

    def body(x_ref, w_ref, o_ref):
        o_ref[...] = _dot(x_ref[...], w_ref[...])

    return pl.pallas_call(
        body, name="proj", grid=(DIN // tn,),
        in_specs=[pl.BlockSpec((S, D), lambda j: (0, 0)),
                  pl.BlockSpec((None, D, tn), lambda j: (j // per, 0, j % per))],
        out_specs=pl.BlockSpec((S, tn), lambda j: (0, j)),
        out_shape=jax.ShapeDtypeStruct((S, DIN), F32),
        compiler_params=_params(1),
    )(xb, win_g)


FWD_HEADS_PER_STEP = 4
BWD_HEADS_PER_STEP = 2


def _head_bias_tiles(rb_ref, bk_ref, bias_scr, first_head, hps):
    qi = lax.broadcasted_iota(jnp.int32, (128, 256), 0)
    kj = lax.broadcasted_iota(jnp.int32, (128, 256), 1)
    steps = 128 + qi - kj
    band = (steps >= 0) & (steps <= 128)
    bias_scr[...] = jnp.zeros_like(bias_scr)
    for p in range(len(PATTERNS)):
        bucket = bk_ref[p]

        def one_bucket(t, carry):
            hit = bucket == t
            for j in range(hps):
                bias_scr[p, j] = jnp.where(hit, rb_ref[t, first_head + j], bias_scr[p, j])
            return carry

        lax.fori_loop(0, N_BUCKETS, one_bucket, 0)
        for j in range(hps):
            bias_scr[p, j] = jnp.where(band, bias_scr[p, j], NEG_INF)


def _block_rows(b, dilation):
    nblk = NBLK // dilation
    r, n = b // nblk, b % nblk
    start = r + n * (128 * dilation)
    prev_start = jnp.maximum(start - 128 * dilation, r)
    if dilation == 1:
        return pl.ds(pl.multiple_of(start, 128), 128), pl.ds(pl.multiple_of(prev_start, 128), 128), n > 0
    return pl.ds(start, 128, stride=dilation), pl.ds(prev_start, 128, stride=dilation), n > 0


def _head_specs(first, hps):
    return [pl.BlockSpec((S, HD), lambda g, j=j: (0, first + g * hps + j)) for j in range(hps)]


def _heads_spec(hps):
    return pl.BlockSpec((S, hps * HD), lambda g: (0, g))


def _attention_fwd(proj, rel_bias):
    hps = FWD_HEADS_PER_STEP
    buckets = jnp.asarray(np.stack([_bucket_tile(d) for _, d in PATTERNS]))

    def body(rb_ref, bk_ref, *refs):
        q_refs, k_refs, v_refs = (refs[i * hps:(i + 1) * hps] for i in range(3))
        o_ref, lse_ref, bias_scr = refs[3 * hps:3 * hps + 3]
        acc_scrs, m_scrs, l_scrs = (refs[3 * hps + 3 + i * hps:3 * hps + 3 + (i + 1) * hps] for i in range(3))
        _head_bias_tiles(rb_ref, bk_ref, bias_scr, pl.program_id(0) * hps, hps)
        kj = lax.broadcasted_iota(jnp.int32, (128, 256), 1)
        for p, (_, d) in enumerate(PATTERNS):
            prev_blocks = NBLK // d > 1

            def block(b, carry):
                units = [(j,) + _block_rows(blk, d) for blk in (b, b + NBLK // 2) for j in range(hps)]
                scores = []
                for j, rows, prows, _ in units:
                    q = q_refs[j][rows, :].astype(BF16)
                    cur = _dot_nt(q, k_refs[j][rows, :].astype(BF16))
                    if prev_blocks:
                        cur = jnp.concatenate([_dot_nt(q, k_refs[j][prows, :].astype(BF16)), cur], axis=1)
                    scores.append(cur)
                soft = []
                for u, (j, _, _, has_prev) in enumerate(units):
                    if prev_blocks:
                        s = jnp.where((kj >= 128) | has_prev, scores[u] * SCALE + bias_scr[p, j], NEG_INF)
                    else:
                        s = scores[u] * SCALE + bias_scr[p, j, :, 128:256]
                    m = jnp.max(s, axis=1, keepdims=True)
                    e = jnp.exp(s - m)
                    soft.append((m, jnp.sum(e, axis=1, keepdims=True), e.astype(BF16)))
                outs = []
                for u, (j, rows, prows, _) in enumerate(units):
                    e = soft[u][2]
                    if prev_blocks:
                        outs.append(_dot(e[:, :128], v_refs[j][prows, :].astype(BF16))
                                    + _dot(e[:, 128:], v_refs[j][rows, :].astype(BF16)))
                    else:
                        outs.append(_dot(e, v_refs[j][rows, :].astype(BF16)))
                for u, (j, rows, _, _) in enumerate(units):
                    acc_scr, m_scr, l_scr = acc_scrs[j], m_scrs[j], l_scrs[j]
                    (m, den, _), o = soft[u], outs[u]
                    if p == 0:
                        acc_scr[rows, :] = o
                        m_scr[rows, :] = jnp.broadcast_to(m, (128, HD))
                        l_scr[rows, :] = jnp.broadcast_to(den, (128, HD))
                    else:
                        m_old = m_scr[rows, :]
                        m_new = jnp.maximum(m_old, m)
                        w_old, w_new = jnp.exp(m_old - m_new), jnp.exp(m - m_new)
                        acc_scr[rows, :] = acc_scr[rows, :] * w_old + o * w_new
                        l_scr[rows, :] = l_scr[rows, :] * w_old + den * w_new
                        m_scr[rows, :] = m_new
                return carry

            lax.fori_loop(0, NBLK // 2, block, 0)
        for j in range(hps):
            cols = slice(j * HD, (j + 1) * HD)
            den = l_scrs[j][...]
            o_ref[:, cols] = (acc_scrs[j][...] / den).astype(BF16)
            lse_ref[:, cols] = m_scrs[j][...] + jnp.log(den)

    return pl.pallas_call(
        body, name="attention_fwd", grid=(NH // hps,),
        in_specs=[pl.BlockSpec(memory_space=pltpu.SMEM), pl.BlockSpec((3, 128, 256), lambda g: (0, 0, 0))]
        + _head_specs(0, hps) + _head_specs(NH, hps) + _head_specs(2 * NH, hps),
        out_specs=[_heads_spec(hps), _heads_spec(hps)],
        out_shape=[jax.ShapeDtypeStruct((S, DA), BF16), jax.ShapeDtypeStruct((S, DA), F32)],
        scratch_shapes=[pltpu.VMEM((3, hps, 128, 256), F32)] + [pltpu.VMEM((S, HD), F32)] * (3 * hps),
        compiler_params=_params(1),
    )(rel_bias, buckets, *([proj] * (3 * hps)))


def _attention_bwd(proj, dattn, attn, lse, rel_bias, after=None):
    hps = BWD_HEADS_PER_STEP

    def body(rb_ref, bk_ref, *refs):
        q_refs, k_refs, v_refs, do_refs, o_refs, lse_refs = (refs[i * hps:(i + 1) * hps] for i in range(6))
        dq_ref, dk_ref, dv_ref, ds_ref, bias_scr = refs[6 * hps:6 * hps + 5]
        dl_scrs, dq_scrs, dk_scrs, dv_scrs = (refs[6 * hps + 5 + i * hps:6 * hps + 5 + (i + 1) * hps] for i in range(4))
        _head_bias_tiles(rb_ref, bk_ref, bias_scr, pl.program_id(0) * hps, hps)
        ds_ref[...] = jnp.zeros_like(ds_ref)
        for j in range(hps):
            dq_scrs[j][...] = jnp.zeros((S, HD), F32)
            dk_scrs[j][...] = jnp.zeros((S, HD), F32)
            dv_scrs[j][...] = jnp.zeros((S, HD), F32)
            prod = do_refs[j][...] * o_refs[j][...].astype(F32)
            dl_scrs[j][...] = jnp.broadcast_to(jnp.sum(prod, axis=1, keepdims=True), (S, HD))
        for p, (_, d) in enumerate(PATTERNS):
            prev_blocks = NBLK // d > 1

            def block(b, carry):
                units = [(j,) + _block_rows(blk, d) for blk in (b, b + NBLK // 2) for j in range(hps)]
                ops, raw = [], []
                for j, rows, prows, _ in units:
                    q, do = q_refs[j][rows, :].astype(BF16), do_refs[j][rows, :].astype(BF16)
                    kc, vc = k_refs[j][rows, :].astype(BF16), v_refs[j][rows, :].astype(BF16)
                    if prev_blocks:
                        kp, vp = k_refs[j][prows, :].astype(BF16), v_refs[j][prows, :].astype(BF16)
                        ops.append((q, do, kc, kp))
                        raw.append((_dot_nt(q, kc), _dot_nt(do, vc), _dot_nt(q, kp), _dot_nt(do, vp)))
                    else:
                        ops.append((q, do, kc))
                        raw.append((_dot_nt(q, kc), _dot_nt(do, vc)))
                probs = []
                for u, (j, rows, _, has_prev) in enumerate(units):
                    lse_b, dl_b = lse_refs[j][rows, :], dl_scrs[j][rows, :]
                    p_c = jnp.exp(raw[u][0] * SCALE + bias_scr[p, j, :, 128:256] - lse_b)
                    ds_c = p_c * (raw[u][1] - dl_b)
                    ds_ref[p, j, :, 128:256] += ds_c
                    if prev_blocks:
                        p_p = jnp.where(has_prev, jnp.exp(raw[u][2] * SCALE + bias_scr[p, j, :, 0:128] - lse_b), 0.0)
                        ds_p = p_p * (raw[u][3] - dl_b)
                        ds_ref[p, j, :, 0:128] += ds_p
                        probs.append((p_c, ds_c, p_p, ds_p))
                    else:
                        probs.append((p_c, ds_c))
                grads = []
                for u in range(len(units)):
                    q, do, kc = ops[u][:3]
                    p_c, ds_c = probs[u][:2]
                    dq = _dot(ds_c.astype(BF16), kc)
                    cur = (_dot(ds_c.T.astype(BF16), q) * SCALE, _dot(p_c.T.astype(BF16), do))
                    if prev_blocks:
                        p_p, ds_p = probs[u][2:]
                        dq = dq + _dot(ds_p.astype(BF16), ops[u][3])
                        cur = cur + (_dot(ds_p.T.astype(BF16), q) * SCALE, _dot(p_p.T.astype(BF16), do))
                    grads.append((dq * SCALE,) + cur)
                for u, (j, rows, prows, _) in enumerate(units):
                    dq_scrs[j][rows, :] += grads[u][0]
                    dk_scrs[j][rows, :] += grads[u][1]
                    dv_scrs[j][rows, :] += grads[u][2]
                    if prev_blocks:
                        dk_scrs[j][prows, :] += grads[u][3]
                        dv_scrs[j][prows, :] += grads[u][4]
                return carry

            lax.fori_loop(0, NBLK // 2, block, 0)
        for j in range(hps):
            cols = slice(j * HD, (j + 1) * HD)
            dq_ref[:, cols] = dq_scrs[j][...].astype(BF16)
            dk_ref[:, cols] = dk_scrs[j][...].astype(BF16)
            dv_ref[:, cols] = dv_scrs[j][...].astype(BF16)

    buckets = jnp.asarray(np.stack([_bucket_tile(d) for _, d in PATTERNS]))
    body, more_specs, more = _behind(body, 2 + 6 * hps, after)
    return pl.pallas_call(
        body, name="attention_bwd", grid=(NH // hps,),
        in_specs=[pl.BlockSpec(memory_space=pltpu.SMEM), pl.BlockSpec((3, 128, 256), lambda g: (0, 0, 0))]
        + _head_specs(0, hps) + _head_specs(NH, hps) + _head_specs(2 * NH, hps) + 3 * _head_specs(0, hps)
        + more_specs,
        out_specs=3 * [_heads_spec(hps)] + [pl.BlockSpec((3, hps, 128, 256), lambda g: (0, g, 0, 0))],
        out_shape=[jax.ShapeDtypeStruct((S, DA), BF16)] * 3 + [jax.ShapeDtypeStruct((3, NH, 128, 256), F32)],
        scratch_shapes=[pltpu.VMEM((3, hps, 128, 256), F32)] + [pltpu.VMEM((S, HD), F32)] * (4 * hps),
        compiler_params=_params(1),
    )(rel_bias, buckets, *([proj] * (3 * hps)), *([dattn] * hps), *([attn] * hps), *([lse] * hps), *more)


def _gmlp_parts(u_ref, vb_ref, g_ref, be_ref):
    u = u_ref[...]
    u_act, tu = _gelu(u)
    vb = vb_ref[...]
    gv, tv = _gelu(vb)
    mean = jnp.mean(gv, axis=1, keepdims=True)
    cen = gv - mean
    var = jnp.mean(cen * cen, axis=1, keepdims=True)
    rstd = lax.rsqrt(var + LN_EPS)
    xhat = cen * rstd
    vn = xhat * g_ref[...] + be_ref[...]
    return u, tu, u_act, vb, tv, rstd, xhat, vn


def _gmlp_fwd(proj, ws, bsp_b, gain_v, bias_v):
    def body(u_ref, vb_ref, ws_ref, bsp_ref, g_ref, be_ref, o_ref):
        _, _, u_act, _, _, _, _, vn = _gmlp_parts(u_ref, vb_ref, g_ref, be_ref)
        row = lax.broadcasted_iota(jnp.int32, (128, 128), 0)
        col = lax.broadcasted_iota(jnp.int32, (128, 128), 1)
        causal = row >= col
        for g in range(NH):
            cols = slice(g * 128, (g + 1) * 128)
            wsg = jnp.where(causal, ws_ref[g], 0.0).astype(BF16)
            z = _dot(wsg, vn[:, cols].astype(BF16)) + bsp_ref[g]
            o_ref[:, cols] = (u_act[:, cols] * z).astype(BF16)

    return pl.pallas_call(
        body, name="gmlp_fwd", grid=(NBLK,),
        in_specs=[pl.BlockSpec((128, DB), lambda c: (c, 3)), pl.BlockSpec((128, DB), lambda c: (c, 4)),
                  pl.BlockSpec((NH, 128, 128), lambda c: (0, 0, 0)), pl.BlockSpec((NH, 128, 128), lambda c: (0, 0, 0)),
                  pl.BlockSpec((1, DB), lambda c: (0, 0)), pl.BlockSpec((1, DB), lambda c: (0, 0))],
        out_specs=pl.BlockSpec((128, DB), lambda c: (c, 0)),
        out_shape=jax.ShapeDtypeStruct((S, DB), BF16),
        compiler_params=_params(1),
    )(proj, proj, ws, bsp_b, gain_v, bias_v)


def _branch(attn, gmlp, wpa_g, wpb_g, proj):
    tn = 512

    def body(a_ref, g_ref, wa_ref, wb_ref, ga_ref, gb_ref, ya_ref, yb_ref, mg_ref):
        ya = _dot(a_ref[...], wa_ref[...])
        yb = _dot(g_ref[...], wb_ref[...])
        ya_ref[...] = ya.astype(BF16)
        yb_ref[...] = yb.astype(BF16)
        mg_ref[...] = (_sigmoid(ga_ref[...]) * ya + _sigmoid(gb_ref[...]) * yb).astype(BF16)

    out = pl.BlockSpec((S, tn), lambda j: (0, j))
    return pl.pallas_call(
        body, name="branch", grid=(D // tn,),
        in_specs=[pl.BlockSpec((S, DA), lambda j: (0, 0)), pl.BlockSpec((S, DB), lambda j: (0, 0)),
                  pl.BlockSpec((None, DA, tn), lambda j: (j, 0, 0)), pl.BlockSpec((None, DB, tn), lambda j: (j, 0, 0)),
                  pl.BlockSpec((S, tn), lambda j: (0, 5120 // tn + j)), pl.BlockSpec((S, tn), lambda j: (0, 7168 // tn + j))],
        out_specs=[out, out, out],
        out_shape=[jax.ShapeDtypeStruct((S, D), BF16)] * 3,
        compiler_params=_params(1),
    )(attn, gmlp, wpa_g, wpb_g, proj, proj)


def _out_ln1(merged, wout_g, x, gain, bias):
    tm = 256

    def body(m_ref, w_ref, x_ref, g_ref, b_ref, xh_ref, rs_ref, h_ref):
        pre = ALPHA * x_ref[...] + _dot(m_ref[...], w_ref[...])
        mean = jnp.mean(pre, axis=1, keepdims=True)
        cen = pre - mean
        var = jnp.mean(cen * cen, axis=1, keepdims=True)
        rstd = lax.rsqrt(var + LN_EPS)
        xhat = cen * rstd
        xh_ref[...] = xhat
        rs_ref[...] = jnp.broadcast_to(rstd, (tm, 128))
        h_ref[...] = (xhat * g_ref[...] + b_ref[...]).astype(BF16)

    row = pl.BlockSpec((tm, D), lambda i: (i, 0))
    vec = pl.BlockSpec((1, D), lambda i: (0, 0))
    return pl.pallas_call(
        body, name="out_ln1", grid=(S // tm,),
        in_specs=[row, pl.BlockSpec((D, D), lambda i: (0, 0)), row, vec, vec],
        out_specs=[row, pl.BlockSpec((tm, 128), lambda i: (i, 0)), row],
        out_shape=[jax.ShapeDtypeStruct((S, D), F32), jax.ShapeDtypeStruct((S, 128), F32),
                   jax.ShapeDtypeStruct((S, D), BF16)],
        compiler_params=_params(1),
    )(merged, wout_g, x, gain, bias)


def _ff1(h1b, w1_g, b1):
    tn = 512
    per = D // tn

    def body(h_ref, w_ref, b_ref, a_ref, r_ref):
        r = jnp.maximum(_dot(h_ref[...], w_ref[...]) + b_ref[...], 0.0)
        r_ref[...] = r.astype(BF16)
        a_ref[...] = (r * r).astype(BF16)

    out = pl.BlockSpec((S, tn), lambda j: (0, j))
    return pl.pallas_call(
        body, name="ff1", grid=(DFF // tn,),
        in_specs=[pl.BlockSpec((S, D), lambda j: (0, 0)),
                  pl.BlockSpec((None, D, tn), lambda j: (j // per, 0, j % per)),
                  pl.BlockSpec((1, tn), lambda j: (0, j))],
        out_specs=[out, out],
        out_shape=[jax.ShapeDtypeStruct((S, DFF), BF16)] * 2,
        compiler_params=_params(1),
    )(h1b, w1_g, b1)


def _ff2_ln2_loss(a, w2_g, xhat1, g1, b1, b2, g2, be2, target):
    tm, tk = 512, 1024
    nk = DFF // tk

    def body(a_ref, w_ref, xh_ref, g1_ref, b1_ref, b2_ref, g2_ref, be2_ref, t_ref, d_ref, db_ref, st_ref, acc):
        i, k = pl.program_id(0), pl.program_id(1)

        @pl.when(k == 0)
        def _():
            acc[...] = jnp.zeros_like(acc)

        @pl.when((i == 0) & (k == 0))
        def _():
            st_ref[...] = jnp.zeros_like(st_ref)

        acc[...] += _dot(a_ref[...], w_ref[...])

        @pl.when(k == nk - 1)
        def _():
            def rows_chunk(ci, carry):
                rows = pl.ds(pl.multiple_of(ci * 128, 128), 128)
                h1 = xh_ref[rows, :] * g1_ref[...] + b1_ref[...]
                pre = ALPHA * h1 + acc[rows, :] + b2_ref[...]
                mean = jnp.mean(pre, axis=1, keepdims=True)
                cen = pre - mean
                var = jnp.mean(cen * cen, axis=1, keepdims=True)
                rstd = lax.rsqrt(var + LN_EPS)
                xhat = cen * rstd
                y = xhat * g2_ref[...] + be2_ref[...]
                err = y - t_ref[rows, :]
                dy = err * (1.0 / D)
                g = dy * g2_ref[...]
                dpre = rstd * (g - jnp.mean(g, axis=1, keepdims=True)
                               - xhat * jnp.mean(g * xhat, axis=1, keepdims=True))
                d_ref[rows, :] = dpre
                db_ref[rows, :] = dpre.astype(BF16)
                st_ref[0:1, :] += jnp.sum(dy * xhat, axis=0, keepdims=True)
                st_ref[1:2, :] += jnp.sum(dy, axis=0, keepdims=True)
                st_ref[2:3, :] += jnp.sum(dpre, axis=0, keepdims=True)
                st_ref[3:4, :] += jnp.broadcast_to(jnp.sum(err * err).reshape(1, 1), (1, D))
                return carry

            lax.fori_loop(0, tm // 128, rows_chunk, 0)

    row = pl.BlockSpec((tm, D), lambda i, k: (i, 0))
    vec = pl.BlockSpec((1, D), lambda i, k: (0, 0))
    return pl.pallas_call(
        body, name="ff2_ln2_loss", grid=(S // tm, nk),
        in_specs=[pl.BlockSpec((tm, tk), lambda i, k: (i, k)), pl.BlockSpec((tk, D), lambda i, k: (k, 0)),
                  row, vec, vec, vec, vec, vec, row],
        out_specs=[row, row, pl.BlockSpec((8, D), lambda i, k: (0, 0))],
        out_shape=[jax.ShapeDtypeStruct((S, D), F32), jax.ShapeDtypeStruct((S, D), BF16),
                   jax.ShapeDtypeStruct((8, D), F32)],
        scratch_shapes=[pltpu.VMEM((tm, D), F32)],
        compiler_params=_params(2),
    )(a, w2_g, xhat1, g1, b1, b2, g2, be2, target)


def _grad_w(act, dout, name, ti, tj, sharded, after=None):
    m, n = act.shape[1], dout.shape[1]
    ns = n // N_CHIPS
    per = ns // tj if sharded else None

    def body(a_ref, b_ref, o_ref, at_scr):
        @pl.when(pl.program_id(1) == 0)
        def _():
            at_scr[...] = a_ref[...].T

        o_ref[...] = _dot(at_scr[...], b_ref[...])

    if sharded:
        out_spec = pl.BlockSpec((None, ti, tj), lambda i, j: (j // per, i, j % per))
        out_shape = jax.ShapeDtypeStruct((N_CHIPS, m, ns), F32)
    else:
        out_spec = pl.BlockSpec((ti, tj), lambda i, j: (i, j))
        out_shape = jax.ShapeDtypeStruct((m, n), F32)
    body, more_specs, more = _behind(body, 2, after)
    return pl.pallas_call(
        body, name=name, grid=(m // ti, n // tj),
        in_specs=[pl.BlockSpec((S, ti), lambda i, j: (0, i)), pl.BlockSpec((S, tj), lambda i, j: (0, j))] + more_specs,
        out_specs=out_spec, out_shape=out_shape,
        scratch_shapes=[pltpu.VMEM((ti, S), BF16)],
        compiler_params=_params(2),
    )(act, dout, *more)


def _d_ff1(dpre2b, w2_g, r, after=None):
    tn = 512

    def body(d_ref, w_ref, r_ref, o_ref, gb_ref):
        da = _dot_nt(d_ref[...], w_ref[...])
        dp = da * (2.0 * r_ref[...].astype(F32))
        o_ref[...] = dp.astype(BF16)
        gb_ref[...] = jnp.sum(dp, axis=0, keepdims=True)

    body, more_specs, more = _behind(body, 3, after)
    return pl.pallas_call(
        body, name="d_ff1", grid=(DFF // tn,),
        in_specs=[pl.BlockSpec((S, D), lambda j: (0, 0)), pl.BlockSpec((tn, D), lambda j: (j, 0)),
                  pl.BlockSpec((S, tn), lambda j: (0, j))] + more_specs,
        out_specs=[pl.BlockSpec((S, tn), lambda j: (0, j)), pl.BlockSpec((1, tn), lambda j: (0, j))],
        out_shape=[jax.ShapeDtypeStruct((S, DFF), BF16), jax.ShapeDtypeStruct((1, DFF), F32)],
        compiler_params=_params(1),
    )(dpre2b, w2_g, r, *more)


def _d_h1_ln1(dprea, w1_g, dpre2, xhat1, rstd1, g1, after=None):
    tm, tk = 512, 1024
    per = D // tk
    nk = DFF // tk

    def body(a_ref, w_ref, d2_ref, xh_ref, rs_ref, g_ref, d_ref, db_ref, st_ref, acc):
        i, k = pl.program_id(0), pl.program_id(1)

        @pl.when(k == 0)
        def _():
            acc[...] = jnp.zeros_like(acc)

        @pl.when((i == 0) & (k == 0))
        def _():
            st_ref[...] = jnp.zeros_like(st_ref)

        acc[...] += _dot_nt(a_ref[...], w_ref[...])

        @pl.when(k == nk - 1)
        def _():
            def rows_chunk(ci, carry):
                rows = pl.ds(pl.multiple_of(ci * 128, 128), 128)
                dh = ALPHA * d2_ref[rows, :] + acc[rows, :]
                xhat = xh_ref[rows, :]
                g = dh * g_ref[...]
                dpre = rs_ref[rows, 0:1] * (g - jnp.mean(g, axis=1, keepdims=True)
                                            - xhat * jnp.mean(g * xhat, axis=1, keepdims=True))
                d_ref[rows, :] = dpre
                db_ref[rows, :] = dpre.astype(BF16)
                st_ref[0:1, :] += jnp.sum(dh * xhat, axis=0, keepdims=True)
                st_ref[1:2, :] += jnp.sum(dh, axis=0, keepdims=True)
                return carry

            lax.fori_loop(0, tm // 128, rows_chunk, 0)

    row = pl.BlockSpec((tm, D), lambda i, k: (i, 0))
    body, more_specs, more = _behind(body, 6, after)
    return pl.pallas_call(
        body, name="d_h1_ln1", grid=(S // tm, nk),
        in_specs=[pl.BlockSpec((tm, tk), lambda i, k: (i, k)),
                  pl.BlockSpec((None, D, tk), lambda i, k: (k // per, 0, k % per)),
                  row, row, pl.BlockSpec((tm, 128), lambda i, k: (i, 0)), pl.BlockSpec((1, D), lambda i, k: (0, 0))]
        + more_specs,
        out_specs=[row, row, pl.BlockSpec((8, D), lambda i, k: (0, 0))],
        out_shape=[jax.ShapeDtypeStruct((S, D), F32), jax.ShapeDtypeStruct((S, D), BF16),
                   jax.ShapeDtypeStruct((8, D), F32)],
        scratch_shapes=[pltpu.VMEM((tm, D), F32)],
        compiler_params=_params(2),
    )(dprea, w1_g, dpre2, xhat1, rstd1, g1, *more)


def _d_merged(dpre1b, wout_g, proj, ya, yb):
    tm, tn = 512, 1024

    def body(d_ref, w_ref, ga_ref, gb_ref, ya_ref, yb_ref, dya_ref, dyb_ref, dga_ref, dgb_ref):
        dm = _dot_nt(d_ref[...], w_ref[...])
        sa = _sigmoid(ga_ref[...])
        sb = _sigmoid(gb_ref[...])
        dya_ref[...] = (dm * sa).astype(BF16)
        dyb_ref[...] = (dm * sb).astype(BF16)
        dga_ref[...] = (dm * ya_ref[...].astype(F32) * sa * (1.0 - sa)).astype(BF16)
        dgb_ref[...] = (dm * yb_ref[...].astype(F32) * sb * (1.0 - sb)).astype(BF16)

    tile = pl.BlockSpec((tm, tn), lambda i, j: (i, j))
    return pl.pallas_call(
        body, name="d_merged", grid=(S // tm, D // tn),
        in_specs=[pl.BlockSpec((tm, D), lambda i, j: (i, 0)), pl.BlockSpec((tn, D), lambda i, j: (j, 0)),
                  pl.BlockSpec((tm, tn), lambda i, j: (i, 5 + j)), pl.BlockSpec((tm, tn), lambda i, j: (i, 7 + j)),
                  tile, tile],
        out_specs=[tile] * 4,
        out_shape=[jax.ShapeDtypeStruct((S, D), BF16)] * 4,
        compiler_params=_params(2),
    )(dpre1b, wout_g, proj, proj, ya, yb)


def _d_branches(dya, dyb, wpa_g, wpb_g, after=None):
    tk = 512

    def body(da_ref, db_ref, wa_ref, wb_ref, oa_ref, ob_ref):
        @pl.when(pl.program_id(0) == 0)
        def _():
            oa_ref[...] = jnp.zeros_like(oa_ref)
            ob_ref[...] = jnp.zeros_like(ob_ref)

        oa_ref[...] += _dot_nt(da_ref[...], wa_ref[...])
        ob_ref[...] += _dot_nt(db_ref[...], wb_ref[...])

    body, more_specs, more = _behind(body, 4, after)
    return pl.pallas_call(
        body, name="d_branches", grid=(D // tk,),
        in_specs=[pl.BlockSpec((S, tk), lambda k: (0, k)), pl.BlockSpec((S, tk), lambda k: (0, k)),
                  pl.BlockSpec((None, DA, tk), lambda k: (k, 0, 0)), pl.BlockSpec((None, DB, tk), lambda k: (k, 0, 0))]
        + more_specs,
        out_specs=[pl.BlockSpec((S, DA), lambda k: (0, 0)), pl.BlockSpec((S, DB), lambda k: (0, 0))],
        out_shape=[jax.ShapeDtypeStruct((S, DA), F32), jax.ShapeDtypeStruct((S, DB), F32)],
        compiler_params=_params(1),
    )(dya, dyb, wpa_g, wpb_g, *more)


def _gmlp_bwd(proj, dgmlp, ws, ws_t, bsp_b, gain_v, bias_v):
    def body(u_ref, vb_ref, dg_ref, ws_ref, wst_ref, bsp_ref, g_ref, be_ref, duv_ref, gws_ref, gbs_ref, st_ref):
        @pl.when(pl.program_id(0) == 0)
        def _():
            gws_ref[...] = jnp.zeros_like(gws_ref)
            gbs_ref[...] = jnp.zeros_like(gbs_ref)
            st_ref[...] = jnp.zeros_like(st_ref)

        u, tu, u_act, vb, tv, rstd, xhat, vn = _gmlp_parts(u_ref, vb_ref, g_ref, be_ref)
        dg = dg_ref[...]
        dz = dg * u_act
        row = lax.broadcasted_iota(jnp.int32, (128, 128), 0)
        col = lax.broadcasted_iota(jnp.int32, (128, 128), 1)
        causal = row >= col
        causal_t = row <= col
        dvn_parts = []
        z_parts = []
        for g in range(NH):
            cols = slice(g * 128, (g + 1) * 128)
            vng = vn[:, cols].astype(BF16)
            dzg = dz[:, cols]
            dzb = dzg.astype(BF16)
            wsg = jnp.where(causal, ws_ref[g], 0.0).astype(BF16)
            wsg_t = jnp.where(causal_t, wst_ref[g], 0.0).astype(BF16)
            z_parts.append(_dot(wsg, vng) + bsp_ref[g])
            gws_ref[g] += jnp.where(causal, _dot_nt(dzb, vng), 0.0)
            gbs_ref[g] += jnp.broadcast_to(jnp.sum(dzg, axis=1, keepdims=True), (128, 128))
            dvn_parts.append(_dot(wsg_t, dzb))
        z = jnp.concatenate(z_parts, axis=1)
        dvn = jnp.concatenate(dvn_parts, axis=1)
        du = dg * z * _gelu_grad(u, tu)
        st_ref[0:1, :] += jnp.sum(dvn * xhat, axis=0, keepdims=True)
        st_ref[1:2, :] += jnp.sum(dvn, axis=0, keepdims=True)
        gg = dvn * g_ref[...]
        dgv = rstd * (gg - jnp.mean(gg, axis=1, keepdims=True) - xhat * jnp.mean(gg * xhat, axis=1, keepdims=True))
        dvb = dgv * _gelu_grad(vb, tv)
        duv_ref[:, 0:DB] = du.astype(BF16)
        duv_ref[:, DB:2 * DB] = dvb.astype(BF16)

    full3 = pl.BlockSpec((NH, 128, 128), lambda c: (0, 0, 0))
    vec = pl.BlockSpec((1, DB), lambda c: (0, 0))
    return pl.pallas_call(
        body, name="gmlp_bwd", grid=(NBLK,),
        in_specs=[pl.BlockSpec((128, DB), lambda c: (c, 3)), pl.BlockSpec((128, DB), lambda c: (c, 4)),
                  pl.BlockSpec((128, DB), lambda c: (c, 0)), full3, full3, full3, vec, vec],
        out_specs=[pl.BlockSpec((128, 2 * DB), lambda c: (c, 0)), full3, full3, pl.BlockSpec((8, DB), lambda c: (0, 0))],
        out_shape=[jax.ShapeDtypeStruct((S, 2 * DB), BF16), jax.ShapeDtypeStruct((NH, 128, 128), F32),
                   jax.ShapeDtypeStruct((NH, 128, 128), F32), jax.ShapeDtypeStruct((8, DB), F32)],
        compiler_params=_params(1),
    )(proj, proj, dgmlp, ws, ws_t, bsp_b, gain_v, bias_v)


def _rel_bias_grad(ds_sums):
    buckets = jnp.asarray(np.stack([_bucket_tile(d) for _, d in PATTERNS]))

    def body(bk_ref, ds_ref, o_ref):
        row = lax.broadcasted_iota(jnp.int32, (N_BUCKETS, 128), 0)
        lane = lax.broadcasted_iota(jnp.int32, (N_BUCKETS, 128), 1)

        def one_bucket(t, out):
            hits = [bk_ref[p] == t for p in range(3)]
            for h in range(NH):
                tot = jnp.zeros((128, 256), F32)
                for p in range(3):
                    tot = tot + jnp.where(hits[p], ds_ref[p, h], 0.0)
                out = jnp.where((row == t) & (lane == h), jnp.sum(tot), out)
            return out

        o_ref[...] = lax.fori_loop(0, N_BUCKETS, one_bucket, jnp.zeros((N_BUCKETS, 128), F32))

    return pl.pallas_call(
        body, name="rel_bias_grad",
        in_specs=[pl.BlockSpec(memory_space=pltpu.VMEM)] * 2, out_specs=pl.BlockSpec(memory_space=pltpu.VMEM),
        out_shape=jax.ShapeDtypeStruct((N_BUCKETS, 128), F32),
        compiler_params=pltpu.CompilerParams(vmem_limit_bytes=VMEM_LIMIT),
    )(buckets, ds_sums)


def _d_x(dproj, win_g, dpre1, after=None):
    tm, tk = 512, 2304
    per = 2304 // tk
    nk = DIN // tk

    def body(a_ref, w_ref, d_ref, o_ref, acc):
        k = pl.program_id(1)

        @pl.when(k == 0)
        def _():
            acc[...] = ALPHA * d_ref[...]

        acc[...] += _dot_nt(a_ref[...], w_ref[...])

        @pl.when(k == nk - 1)
        def _():
            o_ref[...] = acc[...]

    row = pl.BlockSpec((tm, D), lambda i, k: (i, 0))
    body, more_specs, more = _behind(body, 3, after)
    return pl.pallas_call(
        body, name="d_x", grid=(S // tm, nk),
        in_specs=[pl.BlockSpec((tm, tk), lambda i, k: (i, k)),
                  pl.BlockSpec((None, D, tk), lambda i, k: (k // per, 0, k % per)), row] + more_specs,
        out_specs=row, out_shape=jax.ShapeDtypeStruct((S, D), F32),
        scratch_shapes=[pltpu.VMEM((tm, D), F32)],
        compiler_params=_params(2),
    )(dproj, win_g, dpre1, *more)


def _adamw(w, g, m, v, name):
    rows, cols = w.shape
    tm = max(t for t in range(8, 257, 8) if rows % t == 0)

    def body(w_ref, g_ref, m_ref, v_ref, d_ref, nm_ref, nv_ref, go_ref):
        g = g_ref[...]
        m = ADAM_B1 * m_ref[...] + (1.0 - ADAM_B1) * g
        v = ADAM_B2 * v_ref[...] + (1.0 - ADAM_B2) * (g * g)
        m_hat = m / (1.0 - ADAM_B1 ** ADAM_STEP)
        v_hat = v / (1.0 - ADAM_B2 ** ADAM_STEP)
        d_ref[...] = -ADAM_LR * (m_hat / (jnp.sqrt(v_hat) + ADAM_EPS) + ADAM_WD * w_ref[...])
        nm_ref[...] = m
        nv_ref[...] = v
        go_ref[...] = g

    spec = pl.BlockSpec((tm, cols), lambda i: (i, 0))
    return pl.pallas_call(
        body, name=name, grid=(rows // tm,), in_specs=[spec] * 4, out_specs=[spec] * 4,
        out_shape=[jax.ShapeDtypeStruct((rows, cols), F32)] * 4, compiler_params=_params(1),
    )(w, g, m, v)


def _position():
    x, y, c = lax.axis_index("x"), lax.axis_index("y"), lax.axis_index("c")
    chips = [(1 - x, y), (x, 1 - y), (1 - x, 1 - y)]
    return x, y, c, chips


def _remote(src, dst, send_sems, recv_sems, k, to):
    return pltpu.make_async_remote_copy(src_ref=src, dst_ref=dst, send_sem=send_sems.at[k], recv_sem=recv_sems.at[k],
                                        device_id=to, device_id_type=MESH)


def _place_shard(w, name, after=None):
    rows, cols = w.shape
    tm = 256
    x, y = lax.axis_index("x"), lax.axis_index("y")

    def body(chip_ref, w_ref, o_ref):
        o_ref[...] = w_ref[...].astype(BF16)

    more_specs, more = ([ANY], [after]) if after is not None else ([], [])
    if after is not None:
        inner = body
        body = lambda chip_ref, w_ref, after_ref, o_ref: inner(chip_ref, w_ref, o_ref)
    return pl.pallas_call(
        body, name=name,
        grid_spec=pltpu.PrefetchScalarGridSpec(
            num_scalar_prefetch=1, grid=(rows // tm,),
            in_specs=[pl.BlockSpec((tm, cols), lambda i, chip: (i, 0))] + more_specs,
            out_specs=pl.BlockSpec((None, tm, cols), lambda i, chip: (chip[0], i, 0))),
        out_shape=jax.ShapeDtypeStruct((N_CHIPS, rows, cols), BF16),
        compiler_params=_params(1),
    )(jnp.reshape(2 * x + y, (1,)).astype(jnp.int32), w, *more)


def _to_bf16(x, name, after=None):
    tm = 256

    def body(x_ref, o_ref):
        o_ref[...] = x_ref[...].astype(BF16)

    spec = pl.BlockSpec((tm, x.shape[1]), lambda i: (i, 0))
    body, more_specs, more = _behind(body, 1, after)
    return pl.pallas_call(
        body, name=name, grid=(x.shape[0] // tm,), in_specs=[spec] + more_specs, out_specs=spec,
        out_shape=jax.ShapeDtypeStruct(x.shape, BF16), compiler_params=_params(1),
    )(x, *more)


HBM = pl.BlockSpec(memory_space=pltpu.HBM)
SEM = pl.BlockSpec(memory_space=pltpu.SEMAPHORE)
EFFECT = pltpu.SideEffectType.DATAFLOW_SIDE_EFFECTING


def _comm_call(name, body, bufs, sems_in, sems_out, after=None, token=False):
    nb, ns, no = len(bufs), len(sems_in), len(sems_out)
    n_in = nb + ns + (after is not None)

    def wrapped(*refs):
        body(refs[:nb], refs[nb:nb + ns], refs[n_in + nb:n_in + nb + no])
        if token:
            refs[-1][...] = jnp.zeros((8, 128), F32)

    outs = pl.pallas_call(
        wrapped, name=name,
        in_specs=[HBM] * nb + [SEM] * ns + ([ANY] if after is not None else []),
        out_specs=[HBM] * nb + [SEM] * no + ([pl.BlockSpec(memory_space=pltpu.VMEM)] if token else []),
        out_shape=[pltpu.HBM(b.shape, b.dtype) for b in bufs] + [pltpu.SemaphoreType.DMA((k,)) for k in sems_out]
        + ([jax.ShapeDtypeStruct((8, 128), F32)] if token else []),
        input_output_aliases={i: i for i in range(nb)},
        compiler_params=pltpu.CompilerParams(has_side_effects=EFFECT),
    )(*[pltpu.with_memory_space_constraint(b, pltpu.HBM) for b in bufs], *sems_in, *([after] if after is not None else []))
    return list(outs[:nb]), list(outs[nb:nb + no]), (outs[-1] if token else None)


RING_STAGES = {"ici_near": 2, "ici_far": 2, "d2d_near": 2, "d2d_far": 1}


def _ring_copies(buf, send_sems, recv_sems, k0, stage):
    x, y, c, _ = _position()
    hr = buf.shape[1] // 2
    qr = hr // 2
    half = lambda chip, h: buf.at[chip, pl.ds(h * hr, hr), :]
    quarter = lambda chip, h, q: buf.at[chip, pl.ds(h * hr + q * qr, qr), :]
    mine, x_chip, y_chip, far_chip = 2 * x + y, 2 * (1 - x) + y, 2 * x + (1 - y), 2 * (1 - x) + (1 - y)
    to_x, to_y, sibling = (1 - x, y, c), (x, 1 - y, c), (x, y, 1 - c)
    if stage == "ici_near":
        moves = [(half(mine, c), to_x, half(x_chip, c)), (half(mine, c), to_y, half(y_chip, c))]
    elif stage == "ici_far":
        moves = [(quarter(x_chip, c, 0), to_y, quarter(far_chip, c, 0)),
                 (quarter(y_chip, c, 1), to_x, quarter(far_chip, c, 1))]
    elif stage == "d2d_near":
        moves = [(half(x_chip, c), sibling, half(x_chip, 1 - c)), (half(y_chip, c), sibling, half(y_chip, 1 - c))]
    else:
        moves = [(half(far_chip, c), sibling, half(far_chip, 1 - c))]
    sends = [_remote(src, src, send_sems, recv_sems, k0 + i, to) for i, (src, to, _) in enumerate(moves)]
    arrivals = [_remote(got, got, send_sems, recv_sems, k0 + i, (x, y, c)) for i, (_, _, got) in enumerate(moves)]
    return sends, arrivals


def _ring_call(name, groups, actions, after=None):
    tags = list(dict.fromkeys(t for _, t, _ in actions))
    counts = {t: len(groups[t]["bufs"]) for t in tags}
    first = {t: sum(counts[u] for u in tags[:i]) for i, t in enumerate(tags)}
    waits = [(t, s) for v, t, s in actions if v == "wait"]
    starts = [(t, s) for v, t, s in actions if v == "start"]

    def body(bufs, sems_in, sems_out):
        for verb, t, s in actions:
            at, sems = (starts.index((t, s)), sems_out) if verb == "start" else (waits.index((t, s)), sems_in)
            for w in range(counts[t]):
                sends, arrivals = _ring_copies(bufs[first[t] + w], sems[2 * at], sems[2 * at + 1], RING_STAGES[s] * w, s)
                if verb == "start":
                    for cp in sends:
                        cp.start()
                else:
                    for cp in arrivals:
                        cp.wait_recv()
                    for cp in sends:
                        cp.wait_send()

    bufs, sems, token = _comm_call(
        name, body, [b for t in tags for b in groups[t]["bufs"]],
        [sem for t, s in waits for sem in groups[t]["sems"][s]],
        [RING_STAGES[s] * counts[t] for t, s in starts for _ in (0, 1)], after, token=True)
    for t in tags:
        groups[t]["bufs"] = bufs[first[t]:first[t] + counts[t]]
    for t, s in waits:
        del groups[t]["sems"][s]
    for i, (t, s) in enumerate(starts):
        groups[t]["sems"][s] = (sems[2 * i], sems[2 * i + 1])
    return token


def _cx_copies(src, dst, send_sems, recv_sems, k0):
    x, y, c, chips = _position()
    sends = [_remote(src.at[2 * cx + cy], dst.at[2 * x + y], send_sems, recv_sems, k0 + j, (cx, cy, c))
             for j, (cx, cy) in enumerate(chips)]
    arrivals = [_remote(dst.at[2 * cx + cy], dst.at[2 * cx + cy], send_sems, recv_sems, k0 + j, (x, y, c))
                for j, (cx, cy) in enumerate(chips)]
    return sends, arrivals


def _cx_start(name, pair_sums):
    n = len(pair_sums)
    landing = [lax.empty(p.shape, p.dtype) for p in pair_sums]

    def body(bufs, _, sems):
        for w in range(n):
            for cp in _cx_copies(bufs[w], bufs[n + w], sems[0], sems[1], 3 * w)[0]:
                cp.start()

    bufs, sems, token = _comm_call(name, body, list(pair_sums) + landing, [], [3 * n, 3 * n], token=True)
    return (bufs, sems), token


def _cx_wait(name, state, after):
    bufs, sems = state
    n = len(bufs) // 2

    def body(refs, sems_in, _):
        for w in range(n):
            sends, arrivals = _cx_copies(refs[w], refs[n + w], sems_in[0], sems_in[1], 3 * w)
            for cp in arrivals:
                cp.wait_recv()
            for cp in sends:
                cp.wait_send()

    bufs, _, _ = _comm_call(name, body, bufs, sems, [], after)
    return bufs[:n], bufs[n:]


def _px_copies(src, dst, send_sems, recv_sems, k):
    x, y, c, _ = _position()
    hr = src.shape[1] // 2
    send = _remote(src.at[:, pl.ds((1 - c) * hr, hr), :], dst, send_sems, recv_sems, k, (x, y, 1 - c))
    arrival = _remote(dst, dst, send_sems, recv_sems, k, (x, y, c))
    return send, arrival


def _px_start(name, grads):
    n = len(grads)
    landing = [lax.empty((N_CHIPS, g.shape[1] // 2, g.shape[2]), F32) for g in grads]

    def body(bufs, _, sems):
        for w in range(n):
            _px_copies(bufs[w], bufs[n + w], sems[0], sems[1], w)[0].start()

    bufs, sems, token = _comm_call(name, body, list(grads) + landing, [], [n, n], token=True)
    return (bufs, sems), token


def _px_wait(name, state, after):
    bufs, sems = state
    n = len(bufs) // 2

    def body(refs, sems_in, _):
        for w in range(n):
            send, arrival = _px_copies(refs[w], refs[n + w], sems_in[0], sems_in[1], w)
            arrival.wait_recv()
            send.wait_send()

    bufs, _, _ = _comm_call(name, body, bufs, sems, [], after)
    return bufs[:n], bufs[n:]


def _pair_sum(grad, got, name):
    _, rows, cols = grad.shape
    hr = rows // 2
    tm = min(hr, 256)
    nb = hr // tm
    c = lax.axis_index("c")

    def body(c_ref, g_ref, o_ref, out_ref):
        out_ref[...] = (g_ref[...] + o_ref[...]).astype(BF16)

    return pl.pallas_call(
        body, name=name,
        grid_spec=pltpu.PrefetchScalarGridSpec(
            num_scalar_prefetch=1, grid=(N_CHIPS, nb),
            in_specs=[pl.BlockSpec((None, tm, cols), lambda s, i, c_ref: (s, c_ref[0] * nb + i, 0)),
                      pl.BlockSpec((None, tm, cols), lambda s, i, c_ref: (s, i, 0))],
            out_specs=pl.BlockSpec((None, tm, cols), lambda s, i, c_ref: (s, i, 0))),
        out_shape=jax.ShapeDtypeStruct((N_CHIPS, hr, cols), BF16),
        compiler_params=_params(2),
    )(jnp.reshape(c, (1,)).astype(jnp.int32), grad, got)


def _chip_sum(parts, pair_sums, name):
    _, hr, cols = parts.shape
    tm = min(hr, 256)
    nb = hr // tm
    x, y, c = lax.axis_index("x"), lax.axis_index("y"), lax.axis_index("c")

    def body(pos_ref, p_ref, own_ref, o_ref):
        chip = pos_ref[0]
        own = own_ref[...].astype(F32)
        term = lambda s: jnp.where(chip == s, own, p_ref[s].astype(F32))
        o_ref[...] = ((term(0) + term(1)) + term(2)) + term(3)

    return pl.pallas_call(
        body, name=name,
        grid_spec=pltpu.PrefetchScalarGridSpec(
            num_scalar_prefetch=1, grid=(nb,),
            in_specs=[pl.BlockSpec((N_CHIPS, tm, cols), lambda i, pos: (0, i, 0)),
                      pl.BlockSpec((None, tm, cols), lambda i, pos: (pos[0], i, 0))],
            out_specs=pl.BlockSpec((tm, cols), lambda i, pos: (pos[1] * nb + i, 0))),
        out_shape=jax.ShapeDtypeStruct((2 * hr, cols), F32), compiler_params=_params(1),
    )(jnp.stack([2 * x + y, c]).astype(jnp.int32), parts, pair_sums)


def _share_halves(bufs, name):
    n = len(bufs)

    def body(*refs):
        outs = refs[n:2 * n]
        send_sems, recv_sems = refs[2 * n:]
        x, y, c, _ = _position()
        copies = []
        for w in range(n):
            hr = outs[w].shape[0] // 2
            mine = outs[w].at[pl.ds(c * hr, hr), :]
            cp = _remote(mine, mine, send_sems, recv_sems, w, (x, y, 1 - c))
            cp.start()
            copies.append(cp)
        for w in range(n):
            hr = outs[w].shape[0] // 2
            theirs = outs[w].at[pl.ds((1 - c) * hr, hr), :]
            _remote(theirs, theirs, send_sems, recv_sems, w, (x, y, c)).wait_recv()
        for cp in copies:
            cp.wait_send()

    return pl.pallas_call(
        body, name=name,
        in_specs=[ANY] * n, out_specs=[ANY] * n,
        out_shape=[jax.ShapeDtypeStruct(b.shape, b.dtype) for b in bufs],
        input_output_aliases={w: w for w in range(n)},
        scratch_shapes=[pltpu.SemaphoreType.DMA((n,)), pltpu.SemaphoreType.DMA((n,))],
    )(*bufs)


def _allreduce_small(g):
    rows = g.shape[0]

    def body(g_ref, o_ref, sib, slots, send_sems, recv_sems):
        x, y, c, chips = _position()
        me = (x, y, c)
        my_chip = 2 * x + y
        pair = _remote(g_ref, sib, send_sems, recv_sems, 0, (x, y, 1 - c))
        pair.start()
        pair.wait()
        slots[my_chip] = g_ref[...] + sib[...]
        sent = []
        for j, (cx, cy) in enumerate(chips):
            cp = _remote(slots.at[my_chip], slots.at[my_chip], send_sems, recv_sems, 1 + j, (cx, cy, c))
            cp.start()
            sent.append(cp)
        for j, (cx, cy) in enumerate(chips):
            got = slots.at[2 * cx + cy]
            _remote(got, got, send_sems, recv_sems, 1 + j, me).wait_recv()
        for cp in sent:
            cp.wait_send()
        o_ref[...] = ((slots[0] + slots[1]) + slots[2]) + slots[3]

    vm = pl.BlockSpec(memory_space=pltpu.VMEM)
    return pl.pallas_call(
        body, name="allreduce_small",
        in_specs=[vm], out_specs=vm, out_shape=jax.ShapeDtypeStruct((rows, 128), F32),
        scratch_shapes=[pltpu.VMEM((rows, 128), F32), pltpu.VMEM((N_CHIPS, rows, 128), F32),
                        pltpu.SemaphoreType.DMA((4,)), pltpu.SemaphoreType.DMA((4,))],
        compiler_params=pltpu.CompilerParams(vmem_limit_bytes=VMEM_LIMIT),
    )(g)


_SMALL =("rel_bias", "ln_v_gain", "ln_v_bias", "w_spatial", "b_spatial", "ln1_gain", "ln1_bias",
          "b_ff1", "b_ff2", "ln2_gain", "ln2_bias")
_SMALL_ROWS = 1200
_LOSS_AT = (152832 // 128, 0)


def _pack_small(parts):
    flat = jnp.concatenate([parts[k].reshape(-1).astype(F32) for k in _SMALL])
    flat = jnp.pad(flat, (0, _SMALL_ROWS * 128 - flat.shape[0]))
    return flat.reshape(_SMALL_ROWS, 128)


def _unpack_small(packed, like):
    flat = packed.reshape(-1)
    out, at = {}, 0
    for k in _SMALL:
        n = math.prod(like[k].shape)
        out[k] = flat[at:at + n].reshape(like[k].shape)
        at += n
    return out


def kernel(x, w_in, rel_bias, ln_v_gain, ln_v_bias, w_spatial, b_spatial, w_proj_a, w_proj_b, w_out, ln1_gain, ln1_bias, w_ff1, b_ff1, w_ff2, b_ff2, ln2_gain, ln2_bias, loss_target, m_w_in, m_rel_bias, m_ln_v_gain, m_ln_v_bias, m_w_spatial, m_b_spatial, m_w_proj_a, m_w_proj_b, m_w_out, m_ln1_gain, m_ln1_bias, m_w_ff1, m_b_ff1, m_w_ff2, m_b_ff2, m_ln2_gain, m_ln2_bias, v_w_in, v_rel_bias, v_ln_v_gain, v_ln_v_bias, v_w_spatial, v_b_spatial, v_w_proj_a, v_w_proj_b, v_w_out, v_ln1_gain, v_ln1_bias, v_w_ff1, v_b_ff1, v_w_ff2, v_b_ff2, v_ln2_gain, v_ln2_bias):
    args = dict(locals())
    big = ("w_in", "w_proj_a", "w_proj_b", "w_out", "w_ff1", "w_ff2")
    weights = ("w_in", "rel_bias", "ln_v_gain", "ln_v_bias", "w_spatial", "b_spatial", "w_proj_a", "w_proj_b", "w_out",
               "ln1_gain", "ln1_bias", "w_ff1", "b_ff1", "w_ff2", "b_ff2", "ln2_gain", "ln2_bias")

    xs = x[0]
    target = loss_target[0]

    ring = {"a": {"bufs": [_place_shard(w_in[0], "place_w_in")], "sems": {}}}
    tok = _ring_call("allgather_a_near", ring, [("start", "a", "ici_near")])
    placed = [_place_shard(args[k][0], f"place_{k}", after=tok) for k in big[1:]]
    for tag, bufs in (("b", placed[0:3]), ("c", placed[3:4]), ("d", placed[4:5])):
        ring[tag] = {"bufs": bufs, "sems": {}}
    xb = _to_bf16(xs, "x_to_bf16", after=tok)
    _ring_call("allgather_a_far", ring, [("wait", "a", "ici_near"), ("start", "a", "ici_far"), ("start", "a", "d2d_near"),
                                         ("start", "b", "ici_near"), ("start", "c", "ici_near")], after=xb)
    _ring_call("allgather_a_last", ring, [("wait", "a", "ici_far"), ("start", "a", "d2d_far")])
    _ring_call("allgather_a_done", ring, [("wait", "a", "d2d_near"), ("wait", "a", "d2d_far")])
    (win_g,) = ring["a"]["bufs"]

    proj = _proj(xb, win_g)
    _ring_call("allgather_b_far", ring, [("wait", "b", "ici_near"), ("start", "b", "ici_far"), ("start", "b", "d2d_near"),
                                         ("start", "d", "ici_near")], after=proj)
    ws = w_spatial[0]
    ws_t = jnp.transpose(ws, (0, 2, 1))
    bsp_b = jnp.broadcast_to(b_spatial[0][:, :, None], (NH, 128, 128))
    gmlp = _gmlp_fwd(proj, ws, bsp_b, ln_v_gain, ln_v_bias)
    attn, lse = _attention_fwd(proj, rel_bias)
    _ring_call("allgather_b_last_c_far", ring,
               [("wait", "b", "ici_far"), ("start", "b", "d2d_far"),
                ("wait", "c", "ici_near"), ("start", "c", "ici_far"), ("start", "c", "d2d_near")], after=attn)
    _ring_call("allgather_b_done", ring, [("wait", "b", "d2d_near"), ("wait", "b", "d2d_far")])
    wpa_g, wpb_g, wout_g = ring["b"]["bufs"]
    wout_full = wout_g.reshape(D, D)
    ya, yb, merged = _branch(attn, gmlp, wpa_g, wpb_g, proj)
    xhat1, rstd1, h1b = _out_ln1(merged, wout_full, xs, ln1_gain, ln1_bias)
    _ring_call("allgather_c_last_d_far", ring,
               [("wait", "c", "ici_far"), ("start", "c", "d2d_far"),
                ("wait", "d", "ici_near"), ("start", "d", "ici_far"), ("start", "d", "d2d_near")], after=h1b)
    _ring_call("allgather_c_done", ring, [("wait", "c", "d2d_near"), ("wait", "c", "d2d_far")])
    (w1_g,) = ring["c"]["bufs"]
    a, r = _ff1(h1b, w1_g, b_ff1)
    _ring_call("allgather_d_last", ring, [("wait", "d", "ici_far"), ("start", "d", "d2d_far")], after=a)
    _ring_call("allgather_d_done", ring, [("wait", "d", "d2d_near"), ("wait", "d", "d2d_far")])
    (w2_g,) = ring["d"]["bufs"]
    w2_full = w2_g.reshape(DFF, D)
    dpre2, dpre2b, st2 = _ff2_ln2_loss(a, w2_full, xhat1, ln1_gain, ln1_bias, b_ff2, ln2_gain, ln2_bias, target)

    def pair_and_chip(tag, state, after):
        local, from_sibling = _px_wait(f"pair_exchange_wait_{tag}", state, after)
        pair_sums = [_pair_sum(g, o, f"pair_sum_{tag}_{i}") for i, (g, o) in enumerate(zip(local, from_sibling))]
        return _cx_start(f"chip_exchange_start_{tag}", pair_sums)

    g_w2 = _grad_w(a, dpre2b, "grad_w_ff2", 512, 2048, False)
    px, tok = _px_start("pair_exchange_start_w_ff2", [g_w2.reshape(N_CHIPS, DFF // N_CHIPS, D)])
    dprea, g_b1 = _d_ff1(dpre2b, w2_full, r, after=tok)
    cx_w2, tok = pair_and_chip("w_ff2", px, dprea)
    g_w1 = _grad_w(h1b, dprea, "grad_w_ff1", 512, 2048, True, after=tok)
    px, tok = _px_start("pair_exchange_start_w_ff1", [g_w1])
    dpre1, dpre1b, st1 = _d_h1_ln1(dprea, w1_g, dpre2, xhat1, rstd1, ln1_gain, after=tok)
    cx_w1, tok = pair_and_chip("w_ff1", px, dpre1b)
    g_wout = _grad_w(merged, dpre1b, "grad_w_out", 512, 2048, False, after=tok)
    dya, dyb, dga, dgb = _d_merged(dpre1b, wout_full, proj, ya, yb)
    g_wpa = _grad_w(attn, dya, "grad_w_proj_a", 1024, 512, True)
    g_wpb = _grad_w(gmlp, dyb, "grad_w_proj_b", 1024, 512, True)
    px, tok = _px_start("pair_exchange_start_b", [g_wpa, g_wpb, g_wout.reshape(N_CHIPS, D // N_CHIPS, D)])
    dattn, dgmlp = _d_branches(dya, dyb, wpa_g, wpb_g, after=tok)
    duv, g_ws, g_bs, stv = _gmlp_bwd(proj, dgmlp, ws, ws_t, bsp_b, ln_v_gain, ln_v_bias)
    cx_b, tok = pair_and_chip("b", px, duv)
    dq, dk, dv, ds_sums = _attention_bwd(proj, dattn, attn, lse, rel_bias, after=tok)
    g_rb = _rel_bias_grad(ds_sums)[:, :NH]

    small_g = dict(rel_bias=g_rb, ln_v_gain=stv[0], ln_v_bias=stv[1], w_spatial=g_ws, b_spatial=g_bs[:, :, 0],
                   ln1_gain=st1[0], ln1_bias=st1[1], b_ff1=g_b1, b_ff2=st2[2], ln2_gain=st2[0], ln2_bias=st2[1])
    gs = _allreduce_small(_pack_small(small_g).at[_LOSS_AT].set(st2[3, 0]))
    ds_, ms_, vs_, _ = _adamw(_pack_small({k: args[k] for k in _SMALL}), gs,
                           _pack_small({k: args["m_" + k] for k in _SMALL}),
                           _pack_small({k: args["v_" + k] for k in _SMALL}), "adamw_small")
    like = {k: args[k] for k in _SMALL}
    grads, deltas, new_m, new_v = (_unpack_small(t, like) for t in (gs, ds_, ms_, vs_))

    dproj = jnp.concatenate([dq, dk, dv, duv, dga, dgb], axis=1)
    g_win = _grad_w(xb, dproj, "grad_w_in", 512, 2304, True, after=gs)
    px, tok = _px_start("pair_exchange_start_w_in", [g_win])
    grad_x = _d_x(dproj, win_g, dpre1, after=tok)
    cx_in, tok = pair_and_chip("w_in", px, grad_x)

    def reduce_finish(tag, state, names, after):
        pair_sums, from_chips = _cx_wait(f"chip_exchange_wait_{tag}", state, after)
        halves = [_chip_sum(p, own, f"chip_sum_{k}") for p, own, k in zip(from_chips, pair_sums, names)]
        last = None
        for k, g in zip(names, _share_halves(halves, f"share_halves_{tag}")):
            d_, m_, v_, g_ = _adamw(args[k][0], g, args["m_" + k][0], args["v_" + k][0], f"adamw_{k}")
            grads[k], deltas[k], new_m[k], new_v[k] = g_[None], d_[None], m_[None], v_[None]
            last = d_
        return last

    done = reduce_finish("w_ff2", cx_w2, ["w_ff2"], tok)
    done = reduce_finish("w_ff1", cx_w1, ["w_ff1"], done)
    done = reduce_finish("b", cx_b, ["w_proj_a", "w_proj_b", "w_out"], done)
    reduce_finish("w_in", cx_in, ["w_in"], done)

    loss = gs[_LOSS_AT] * (0.5 / D)
    return (loss, grad_x[None], *[grads[k] for k in weights], *[deltas[k] for k in weights],
            *[new_m[k] for k in weights], *[new_v[k] for k in weights])
```

```python
import functools
import math

import numpy as np
import jax
import jax.numpy as jnp
from jax import lax
from jax.experimental import pallas as pl
from jax.experimental.pallas import tpu as pltpu

F32 = jnp.float32
BF16 = jnp.bfloat16

S = 2048
D = 2048
DA = 1024
DB = 1024
DFF = 8192
DIN = 9216
NH = 8
HD = 128
NBLK = 16
PATTERNS = ((128, 1), (512, 4), (2048, 16))
N_BUCKETS = 32
MAX_DISTANCE = 2048
ALPHA = 2.0 ** 0.25
LN_EPS = 1e-5
NEG_INF = -1e30
SCALE = HD ** -0.5
N_CHIPS = 4

ADAM_LR = 0.001
ADAM_B1 = 0.9
ADAM_B2 = 0.999
ADAM_EPS = 1e-08
ADAM_WD = 0.01
ADAM_STEP = 10

VMEM_LIMIT = 56 * 1024 * 1024
MESH = pl.DeviceIdType.MESH
ANY = pl.BlockSpec(memory_space=pl.ANY)


def _params(n_axes, vmem=VMEM_LIMIT):
    return pltpu.CompilerParams(dimension_semantics=("arbitrary",) * n_axes, vmem_limit_bytes=vmem)


def _bucket_tile(dilation):
    qi = np.arange(128)[:, None]
    kj = np.arange(256)[None, :]
    n = np.clip(128 + qi - kj, 0, 128) * dilation
    max_exact = N_BUCKETS // 2
    nf = np.maximum(n, 1).astype(np.float32)
    large = max_exact + (np.log(nf / np.float32(max_exact)) / np.float32(math.log(MAX_DISTANCE / max_exact))
                         * np.float32(N_BUCKETS - max_exact)).astype(np.int32)
    large = np.minimum(large, N_BUCKETS - 1)
    return np.where(n < max_exact, n, large).astype(np.int32)


def _gelu(x):
    c = math.sqrt(2.0 / math.pi)
    t = jnp.tanh(c * (x + 0.044715 * x * x * x))
    return 0.5 * x * (1.0 + t), t


def _gelu_grad(x, t):
    c = math.sqrt(2.0 / math.pi)
    return 0.5 * (1.0 + t) + 0.5 * x * (1.0 - t * t) * c * (1.0 + 3.0 * 0.044715 * x * x)


def _sigmoid(x):
    return 1.0 / (1.0 + jnp.exp(-x))


def _dot(a, b):
    return jnp.dot(a, b, preferred_element_type=F32)


def _behind(body, n_in, after):
    if after is None:
        return body, [], []
    return (lambda *refs: body(*refs[:n_in], *refs[n_in + 1:])), [ANY], [after]


def _dot_nt(a, b):
    return lax.dot_general(a, b, (((1,), (1,)), ((), ())), preferred_element_type=F32)


def _proj(xb, win_g, shards, name, into=None):
    tn = 768
    per = 2304 // tn

    def body(shards_ref, x_ref, w_ref, *rest):
        rest[-1][...] = _dot(x_ref[...], w_ref[...])

    in_specs = [pl.BlockSpec((S, D), lambda j, sh: (0, 0)),
                pl.BlockSpec((None, D, tn), lambda j, sh: (sh[j // per], 0, j % per))]
    return pl.pallas_call(
        body, name=name,
        grid_spec=pltpu.PrefetchScalarGridSpec(
            num_scalar_prefetch=1, grid=(shards.shape[0] * per,),
            in_specs=in_specs + ([ANY] if into is not None else []),
            out_specs=pl.BlockSpec((S, tn), lambda j, sh: (0, sh[j // per] * per + j % per))),
        out_shape=jax.ShapeDtypeStruct((S, DIN), F32),
        input_output_aliases={3: 0} if into is not None else {},
        compiler_params=_params(1),
    )(shards, xb, win_g, *([into] if into is not None else []))


FWD_HEADS_PER_STEP = 4
BWD_HEADS_PER_STEP = 2


def _head_bias_tiles(rb_ref, bk_ref, bias_scr, first_head, hps):
    qi = lax.broadcasted_iota(jnp.int32, (128, 256), 0)
    kj = lax.broadcasted_iota(jnp.int32, (128, 256), 1)
    steps = 128 + qi - kj
    band = (steps >= 0) & (steps <= 128)
    bias_scr[...] = jnp.zeros_like(bias_scr)
    for p in range(len(PATTERNS)):
        bucket = bk_ref[p]

        def one_bucket(t, carry):
            hit = bucket == t
            for j in range(hps):
                bias_scr[p, j] = jnp.where(hit, rb_ref[t, first_head + j], bias_scr[p, j])
            return carry

        lax.fori_loop(0, N_BUCKETS, one_bucket, 0)
        for j in range(hps):
            bias_scr[p, j] = jnp.where(band, bias_scr[p, j], NEG_INF)


def _block_rows(b, dilation):
    nblk = NBLK // dilation
    r, n = b // nblk, b % nblk
    start = r + n * (128 * dilation)
    prev_start = jnp.maximum(start - 128 * dilation, r)
    if dilation == 1:
        return pl.ds(pl.multiple_of(start, 128), 128), pl.ds(pl.multiple_of(prev_start, 128), 128), n > 0
    return pl.ds(start, 128, stride=dilation), pl.ds(prev_start, 128, stride=dilation), n > 0


def _head_specs(first, hps):
    return [pl.BlockSpec((S, HD), lambda g, j=j: (0, first + g * hps + j)) for j in range(hps)]


def _heads_spec(hps):
    return pl.BlockSpec((S, hps * HD), lambda g: (0, g))


def _attention_fwd(proj, rel_bias):
    hps = FWD_HEADS_PER_STEP
    buckets = jnp.asarray(np.stack([_bucket_tile(d) for _, d in PATTERNS]))

    def body(rb_ref, bk_ref, *refs):
        q_refs, k_refs, v_refs = (refs[i * hps:(i + 1) * hps] for i in range(3))
        o_ref, lse_ref, bias_scr = refs[3 * hps:3 * hps + 3]
        acc_scrs, m_scrs, l_scrs = (refs[3 * hps + 3 + i * hps:3 * hps + 3 + (i + 1) * hps] for i in range(3))
        _head_bias_tiles(rb_ref, bk_ref, bias_scr, pl.program_id(0) * hps, hps)
        kj = lax.broadcasted_iota(jnp.int32, (128, 256), 1)
        for p, (_, d) in enumerate(PATTERNS):
            prev_blocks = NBLK // d > 1

            def block(b, carry):
                units = [(j,) + _block_rows(blk, d) for blk in (b, b + NBLK // 2) for j in range(hps)]
                scores = []
                for j, rows, prows, _ in units:
                    q = q_refs[j][rows, :].astype(BF16)
                    cur = _dot_nt(q, k_refs[j][rows, :].astype(BF16))
                    if prev_blocks:
                        cur = jnp.concatenate([_dot_nt(q, k_refs[j][prows, :].astype(BF16)), cur], axis=1)
                    scores.append(cur)
                soft = []
                for u, (j, _, _, has_prev) in enumerate(units):
                    if prev_blocks:
                        s = jnp.where((kj >= 128) | has_prev, scores[u] * SCALE + bias_scr[p, j], NEG_INF)
                    else:
                        s = scores[u] * SCALE + bias_scr[p, j, :, 128:256]
                    m = jnp.max(s, axis=1, keepdims=True)
                    e = jnp.exp(s - m)
                    soft.append((m, jnp.sum(e, axis=1, keepdims=True), e.astype(BF16)))
                outs = []
                for u, (j, rows, prows, _) in enumerate(units):
                    e = soft[u][2]
                    if prev_blocks:
                        outs.append(_dot(e[:, :128], v_refs[j][prows, :].astype(BF16))
                                    + _dot(e[:, 128:], v_refs[j][rows, :].astype(BF16)))
                    else:
                        outs.append(_dot(e, v_refs[j][rows, :].astype(BF16)))
                for u, (j, rows, _, _) in enumerate(units):
                    acc_scr, m_scr, l_scr = acc_scrs[j], m_scrs[j], l_scrs[j]
                    (m, den, _), o = soft[u], outs[u]
                    if p == 0:
                        acc_scr[rows, :] = o
                        m_scr[rows, :] = jnp.broadcast_to(m, (128, HD))
                        l_scr[rows, :] = jnp.broadcast_to(den, (128, HD))
                    else:
                        m_old = m_scr[rows, :]
                        m_new = jnp.maximum(m_old, m)
                        w_old, w_new = jnp.exp(m_old - m_new), jnp.exp(m - m_new)
                        acc_scr[rows, :] = acc_scr[rows, :] * w_old + o * w_new
                        l_scr[rows, :] = l_scr[rows, :] * w_old + den * w_new
                        m_scr[rows, :] = m_new
                return carry

            lax.fori_loop(0, NBLK // 2, block, 0)
        for j in range(hps):
            cols = slice(j * HD, (j + 1) * HD)
            den = l_scrs[j][...]
            o_ref[:, cols] = (acc_scrs[j][...] / den).astype(BF16)
            lse_ref[:, cols] = m_scrs[j][...] + jnp.log(den)

    return pl.pallas_call(
        body, name="attention_fwd", grid=(NH // hps,),
        in_specs=[pl.BlockSpec(memory_space=pltpu.SMEM), pl.BlockSpec((3, 128, 256), lambda g: (0, 0, 0))]
        + _head_specs(0, hps) + _head_specs(NH, hps) + _head_specs(2 * NH, hps),
        out_specs=[_heads_spec(hps), _heads_spec(hps)],
        out_shape=[jax.ShapeDtypeStruct((S, DA), BF16), jax.ShapeDtypeStruct((S, DA), F32)],
        scratch_shapes=[pltpu.VMEM((3, hps, 128, 256), F32)] + [pltpu.VMEM((S, HD), F32)] * (3 * hps),
        compiler_params=_params(1),
    )(rel_bias, buckets, *([proj] * (3 * hps)))


def _attention_bwd(proj, dattn, attn, lse, rel_bias, after=None):
    hps = BWD_HEADS_PER_STEP

    def body(rb_ref, bk_ref, *refs):
        q_refs, k_refs, v_refs, do_refs, o_refs, lse_refs = (refs[i * hps:(i + 1) * hps] for i in range(6))
        dq_ref, dk_ref, dv_ref, ds_ref, bias_scr = refs[6 * hps:6 * hps + 5]
        dl_scrs, dq_scrs, dk_scrs, dv_scrs = (refs[6 * hps + 5 + i * hps:6 * hps + 5 + (i + 1) * hps] for i in range(4))
        _head_bias_tiles(rb_ref, bk_ref, bias_scr, pl.program_id(0) * hps, hps)
        ds_ref[...] = jnp.zeros_like(ds_ref)
        for j in range(hps):
            dq_scrs[j][...] = jnp.zeros((S, HD), F32)
            dk_scrs[j][...] = jnp.zeros((S, HD), F32)
            dv_scrs[j][...] = jnp.zeros((S, HD), F32)
            prod = do_refs[j][...] * o_refs[j][...].astype(F32)
            dl_scrs[j][...] = jnp.broadcast_to(jnp.sum(prod, axis=1, keepdims=True), (S, HD))
        for p, (_, d) in enumerate(PATTERNS):
            prev_blocks = NBLK // d > 1

            def block(b, carry):
                units = [(j,) + _block_rows(blk, d) for blk in (b, b + NBLK // 2) for j in range(hps)]
                ops, raw = [], []
                for j, rows, prows, _ in units:
                    q, do = q_refs[j][rows, :].astype(BF16), do_refs[j][rows, :].astype(BF16)
                    kc, vc = k_refs[j][rows, :].astype(BF16), v_refs[j][rows, :].astype(BF16)
                    if prev_blocks:
                        kp, vp = k_refs[j][prows, :].astype(BF16), v_refs[j][prows, :].astype(BF16)
                        ops.append((q, do, kc, kp))
                        raw.append((_dot_nt(q, kc), _dot_nt(do, vc), _dot_nt(q, kp), _dot_nt(do, vp)))
                    else:
                        ops.append((q, do, kc))
                        raw.append((_dot_nt(q, kc), _dot_nt(do, vc)))
                probs = []
                for u, (j, rows, _, has_prev) in enumerate(units):
                    lse_b, dl_b = lse_refs[j][rows, :], dl_scrs[j][rows, :]
                    p_c = jnp.exp(raw[u][0] * SCALE + bias_scr[p, j, :, 128:256] - lse_b)
                    ds_c = p_c * (raw[u][1] - dl_b)
                    ds_ref[p, j, :, 128:256] += ds_c
                    if prev_blocks:
                        p_p = jnp.where(has_prev, jnp.exp(raw[u][2] * SCALE + bias_scr[p, j, :, 0:128] - lse_b), 0.0)
                        ds_p = p_p * (raw[u][3] - dl_b)
                        ds_ref[p, j, :, 0:128] += ds_p
                        probs.append((p_c, ds_c, p_p, ds_p))
                    else:
                        probs.append((p_c, ds_c))
                grads = []
                for u in range(len(units)):
                    q, do, kc = ops[u][:3]
                    p_c, ds_c = probs[u][:2]
                    dq = _dot(ds_c.astype(BF16), kc)
                    cur = (_dot(ds_c.T.astype(BF16), q) * SCALE, _dot(p_c.T.astype(BF16), do))
                    if prev_blocks:
                        p_p, ds_p = probs[u][2:]
                        dq = dq + _dot(ds_p.astype(BF16), ops[u][3])
                        cur = cur + (_dot(ds_p.T.astype(BF16), q) * SCALE, _dot(p_p.T.astype(BF16), do))
                    grads.append((dq * SCALE,) + cur)
                for u, (j, rows, prows, _) in enumerate(units):
                    dq_scrs[j][rows, :] += grads[u][0]
                    dk_scrs[j][rows, :] += grads[u][1]
                    dv_scrs[j][rows, :] += grads[u][2]
                    if prev_blocks:
                        dk_scrs[j][prows, :] += grads[u][3]
                        dv_scrs[j][prows, :] += grads[u][4]
                return carry

            lax.fori_loop(0, NBLK // 2, block, 0)
        for j in range(hps):
            cols = slice(j * HD, (j + 1) * HD)
            dq_ref[:, cols] = dq_scrs[j][...].astype(BF16)
            dk_ref[:, cols] = dk_scrs[j][...].astype(BF16)
            dv_ref[:, cols] = dv_scrs[j][...].astype(BF16)

    buckets = jnp.asarray(np.stack([_bucket_tile(d) for _, d in PATTERNS]))
    body, more_specs, more = _behind(body, 2 + 6 * hps, after)
    return pl.pallas_call(
        body, name="attention_bwd", grid=(NH // hps,),
        in_specs=[pl.BlockSpec(memory_space=pltpu.SMEM), pl.BlockSpec((3, 128, 256), lambda g: (0, 0, 0))]
        + _head_specs(0, hps) + _head_specs(NH, hps) + _head_specs(2 * NH, hps) + 3 * _head_specs(0, hps)
        + more_specs,
        out_specs=3 * [_heads_spec(hps)] + [pl.BlockSpec((3, hps, 128, 256), lambda g: (0, g, 0, 0))],
        out_shape=[jax.ShapeDtypeStruct((S, DA), BF16)] * 3 + [jax.ShapeDtypeStruct((3, NH, 128, 256), F32)],
        scratch_shapes=[pltpu.VMEM((3, hps, 128, 256), F32)] + [pltpu.VMEM((S, HD), F32)] * (4 * hps),
        compiler_params=_params(1),
    )(rel_bias, buckets, *([proj] * (3 * hps)), *([dattn] * hps), *([attn] * hps), *([lse] * hps), *more)


def _gmlp_parts(u_ref, vb_ref, g_ref, be_ref):
    u = u_ref[...]
    u_act, tu = _gelu(u)
    vb = vb_ref[...]
    gv, tv = _gelu(vb)
    mean = jnp.mean(gv, axis=1, keepdims=True)
    cen = gv - mean
    var = jnp.mean(cen * cen, axis=1, keepdims=True)
    rstd = lax.rsqrt(var + LN_EPS)
    xhat = cen * rstd
    vn = xhat * g_ref[...] + be_ref[...]
    return u, tu, u_act, vb, tv, rstd, xhat, vn


def _gmlp_fwd(proj, ws, bsp_b, gain_v, bias_v):
    def body(u_ref, vb_ref, ws_ref, bsp_ref, g_ref, be_ref, o_ref):
        _, _, u_act, _, _, _, _, vn = _gmlp_parts(u_ref, vb_ref, g_ref, be_ref)
        row = lax.broadcasted_iota(jnp.int32, (128, 128), 0)
        col = lax.broadcasted_iota(jnp.int32, (128, 128), 1)
        causal = row >= col
        for g in range(NH):
            cols = slice(g * 128, (g + 1) * 128)
            wsg = jnp.where(causal, ws_ref[g], 0.0).astype(BF16)
            z = _dot(wsg, vn[:, cols].astype(BF16)) + bsp_ref[g]
            o_ref[:, cols] = (u_act[:, cols] * z).astype(BF16)

    return pl.pallas_call(
        body, name="gmlp_fwd", grid=(NBLK,),
        in_specs=[pl.BlockSpec((128, DB), lambda c: (c, 3)), pl.BlockSpec((128, DB), lambda c: (c, 4)),
                  pl.BlockSpec((NH, 128, 128), lambda c: (0, 0, 0)), pl.BlockSpec((NH, 128, 128), lambda c: (0, 0, 0)),
                  pl.BlockSpec((1, DB), lambda c: (0, 0)), pl.BlockSpec((1, DB), lambda c: (0, 0))],
        out_specs=pl.BlockSpec((128, DB), lambda c: (c, 0)),
        out_shape=jax.ShapeDtypeStruct((S, DB), BF16),
        compiler_params=_params(1),
    )(proj, proj, ws, bsp_b, gain_v, bias_v)


def _branch(attn, gmlp, wpa_g, wpb_g, proj):
    tn = 512

    def body(a_ref, g_ref, wa_ref, wb_ref, ga_ref, gb_ref, ya_ref, yb_ref, mg_ref):
        ya = _dot(a_ref[...], wa_ref[...])
        yb = _dot(g_ref[...], wb_ref[...])
        ya_ref[...] = ya.astype(BF16)
        yb_ref[...] = yb.astype(BF16)
        mg_ref[...] = (_sigmoid(ga_ref[...]) * ya + _sigmoid(gb_ref[...]) * yb).astype(BF16)

    out = pl.BlockSpec((S, tn), lambda j: (0, j))
    return pl.pallas_call(
        body, name="branch", grid=(D // tn,),
        in_specs=[pl.BlockSpec((S, DA), lambda j: (0, 0)), pl.BlockSpec((S, DB), lambda j: (0, 0)),
                  pl.BlockSpec((None, DA, tn), lambda j: (j, 0, 0)), pl.BlockSpec((None, DB, tn), lambda j: (j, 0, 0)),
                  pl.BlockSpec((S, tn), lambda j: (0, 5120 // tn + j)), pl.BlockSpec((S, tn), lambda j: (0, 7168 // tn + j))],
        out_specs=[out, out, out],
        out_shape=[jax.ShapeDtypeStruct((S, D), BF16)] * 3,
        compiler_params=_params(1),
    )(attn, gmlp, wpa_g, wpb_g, proj, proj)


def _out_ln1(merged, wout_g, x, gain, bias):
    tm = 256

    def body(m_ref, w_ref, x_ref, g_ref, b_ref, xh_ref, rs_ref, h_ref):
        pre = ALPHA * x_ref[...] + _dot(m_ref[...], w_ref[...])
        mean = jnp.mean(pre, axis=1, keepdims=True)
        cen = pre - mean
        var = jnp.mean(cen * cen, axis=1, keepdims=True)
        rstd = lax.rsqrt(var + LN_EPS)
        xhat = cen * rstd
        xh_ref[...] = xhat
        rs_ref[...] = jnp.broadcast_to(rstd, (tm, 128))
        h_ref[...] = (xhat * g_ref[...] + b_ref[...]).astype(BF16)

    row = pl.BlockSpec((tm, D), lambda i: (i, 0))
    vec = pl.BlockSpec((1, D), lambda i: (0, 0))
    return pl.pallas_call(
        body, name="out_ln1", grid=(S // tm,),
        in_specs=[row, pl.BlockSpec((D, D), lambda i: (0, 0)), row, vec, vec],
        out_specs=[row, pl.BlockSpec((tm, 128), lambda i: (i, 0)), row],
        out_shape=[jax.ShapeDtypeStruct((S, D), F32), jax.ShapeDtypeStruct((S, 128), F32),
                   jax.ShapeDtypeStruct((S, D), BF16)],
        compiler_params=_params(1),
    )(merged, wout_g, x, gain, bias)


def _ff1(h1b, w1_g, b1):
    tn = 512
    per = D // tn

    def body(h_ref, w_ref, b_ref, a_ref, r_ref):
        r = jnp.maximum(_dot(h_ref[...], w_ref[...]) + b_ref[...], 0.0)
        r_ref[...] = r.astype(BF16)
        a_ref[...] = (r * r).astype(BF16)

    out = pl.BlockSpec((S, tn), lambda j: (0, j))
    return pl.pallas_call(
        body, name="ff1", grid=(DFF // tn,),
        in_specs=[pl.BlockSpec((S, D), lambda j: (0, 0)),
                  pl.BlockSpec((None, D, tn), lambda j: (j // per, 0, j % per)),
                  pl.BlockSpec((1, tn), lambda j: (0, j))],
        out_specs=[out, out],
        out_shape=[jax.ShapeDtypeStruct((S, DFF), BF16)] * 2,
        compiler_params=_params(1),
    )(h1b, w1_g, b1)


def _ff2_ln2_loss(a, w2_g, xhat1, g1, b1, b2, g2, be2, target):
    tm, tk = 512, 1024
    nk = DFF // tk

    def body(a_ref, w_ref, xh_ref, g1_ref, b1_ref, b2_ref, g2_ref, be2_ref, t_ref, d_ref, db_ref, st_ref, acc):
        i, k = pl.program_id(0), pl.program_id(1)

        @pl.when(k == 0)
        def _():
            acc[...] = jnp.zeros_like(acc)

        @pl.when((i == 0) & (k == 0))
        def _():
            st_ref[...] = jnp.zeros_like(st_ref)

        acc[...] += _dot(a_ref[...], w_ref[...])

        @pl.when(k == nk - 1)
        def _():
            def rows_chunk(ci, carry):
                rows = pl.ds(pl.multiple_of(ci * 128, 128), 128)
                h1 = xh_ref[rows, :] * g1_ref[...] + b1_ref[...]
                pre = ALPHA * h1 + acc[rows, :] + b2_ref[...]
                mean = jnp.mean(pre, axis=1, keepdims=True)
                cen = pre - mean
                var = jnp.mean(cen * cen, axis=1, keepdims=True)
                rstd = lax.rsqrt(var + LN_EPS)
                xhat = cen * rstd
                y = xhat * g2_ref[...] + be2_ref[...]
                err = y - t_ref[rows, :]
                dy = err * (1.0 / D)
                g = dy * g2_ref[...]
                dpre = rstd * (g - jnp.mean(g, axis=1, keepdims=True)
                               - xhat * jnp.mean(g * xhat, axis=1, keepdims=True))
                d_ref[rows, :] = dpre
                db_ref[rows, :] = dpre.astype(BF16)
                st_ref[0:1, :] += jnp.sum(dy * xhat, axis=0, keepdims=True)
                st_ref[1:2, :] += jnp.sum(dy, axis=0, keepdims=True)
                st_ref[2:3, :] += jnp.sum(dpre, axis=0, keepdims=True)
                st_ref[3:4, :] += jnp.broadcast_to(jnp.sum(err * err).reshape(1, 1), (1, D))
                return carry

            lax.fori_loop(0, tm // 128, rows_chunk, 0)

    row = pl.BlockSpec((tm, D), lambda i, k: (i, 0))
    vec = pl.BlockSpec((1, D), lambda i, k: (0, 0))
    return pl.pallas_call(
        body, name="ff2_ln2_loss", grid=(S // tm, nk),
        in_specs=[pl.BlockSpec((tm, tk), lambda i, k: (i, k)), pl.BlockSpec((tk, D), lambda i, k: (k, 0)),
                  row, vec, vec, vec, vec, vec, row],
        out_specs=[row, row, pl.BlockSpec((8, D), lambda i, k: (0, 0))],
        out_shape=[jax.ShapeDtypeStruct((S, D), F32), jax.ShapeDtypeStruct((S, D), BF16),
                   jax.ShapeDtypeStruct((8, D), F32)],
        scratch_shapes=[pltpu.VMEM((tm, D), F32)],
        compiler_params=_params(2),
    )(a, w2_g, xhat1, g1, b1, b2, g2, be2, target)


def _grad_w(act, dout, name, ti, tj, sharded, after=None):
    m, n = act.shape[1], dout.shape[1]
    ns = n // N_CHIPS
    per = ns // tj if sharded else None

    def body(a_ref, b_ref, o_ref, at_scr):
        @pl.when(pl.program_id(1) == 0)
        def _():
            at_scr[...] = a_ref[...].T

        o_ref[...] = _dot(at_scr[...], b_ref[...])

    if sharded:
        out_spec = pl.BlockSpec((None, ti, tj), lambda i, j: (j // per, i, j % per))
        out_shape = jax.ShapeDtypeStruct((N_CHIPS, m, ns), F32)
    else:
        out_spec = pl.BlockSpec((ti, tj), lambda i, j: (i, j))
        out_shape = jax.ShapeDtypeStruct((m, n), F32)
    body, more_specs, more = _behind(body, 2, after)
    return pl.pallas_call(
        body, name=name, grid=(m // ti, n // tj),
        in_specs=[pl.BlockSpec((S, ti), lambda i, j: (0, i)), pl.BlockSpec((S, tj), lambda i, j: (0, j))] + more_specs,
        out_specs=out_spec, out_shape=out_shape,
        scratch_shapes=[pltpu.VMEM((ti, S), BF16)],
        compiler_params=_params(2),
    )(act, dout, *more)


def _d_ff1(dpre2b, w2_g, r, after=None):
    tn = 512

    def body(d_ref, w_ref, r_ref, o_ref, gb_ref):
        da = _dot_nt(d_ref[...], w_ref[...])
        dp = da * (2.0 * r_ref[...].astype(F32))
        o_ref[...] = dp.astype(BF16)
        gb_ref[...] = jnp.sum(dp, axis=0, keepdims=True)

    body, more_specs, more = _behind(body, 3, after)
    return pl.pallas_call(
        body, name="d_ff1", grid=(DFF // tn,),
        in_specs=[pl.BlockSpec((S, D), lambda j: (0, 0)), pl.BlockSpec((tn, D), lambda j: (j, 0)),
                  pl.BlockSpec((S, tn), lambda j: (0, j))] + more_specs,
        out_specs=[pl.BlockSpec((S, tn), lambda j: (0, j)), pl.BlockSpec((1, tn), lambda j: (0, j))],
        out_shape=[jax.ShapeDtypeStruct((S, DFF), BF16), jax.ShapeDtypeStruct((1, DFF), F32)],
        compiler_params=_params(1),
    )(dpre2b, w2_g, r, *more)


def _d_h1_ln1(dprea, w1_g, dpre2, xhat1, rstd1, g1, after=None):
    tm, tk = 512, 1024
    per = D // tk
    nk = DFF // tk

    def body(a_ref, w_ref, d2_ref, xh_ref, rs_ref, g_ref, d_ref, db_ref, st_ref, acc):
        i, k = pl.program_id(0), pl.program_id(1)

        @pl.when(k == 0)
        def _():
            acc[...] = jnp.zeros_like(acc)

        @pl.when((i == 0) & (k == 0))
        def _():
            st_ref[...] = jnp.zeros_like(st_ref)

        acc[...] += _dot_nt(a_ref[...], w_ref[...])

        @pl.when(k == nk - 1)
        def _():
            def rows_chunk(ci, carry):
                rows = pl.ds(pl.multiple_of(ci * 128, 128), 128)
                dh = ALPHA * d2_ref[rows, :] + acc[rows, :]
                xhat = xh_ref[rows, :]
                g = dh * g_ref[...]
                dpre = rs_ref[rows, 0:1] * (g - jnp.mean(g, axis=1, keepdims=True)
                                            - xhat * jnp.mean(g * xhat, axis=1, keepdims=True))
                d_ref[rows, :] = dpre
                db_ref[rows, :] = dpre.astype(BF16)
                st_ref[0:1, :] += jnp.sum(dh * xhat, axis=0, keepdims=True)
                st_ref[1:2, :] += jnp.sum(dh, axis=0, keepdims=True)
                return carry

            lax.fori_loop(0, tm // 128, rows_chunk, 0)

    row = pl.BlockSpec((tm, D), lambda i, k: (i, 0))
    body, more_specs, more = _behind(body, 6, after)
    return pl.pallas_call(
        body, name="d_h1_ln1", grid=(S // tm, nk),
        in_specs=[pl.BlockSpec((tm, tk), lambda i, k: (i, k)),
                  pl.BlockSpec((None, D, tk), lambda i, k: (k // per, 0, k % per)),
                  row, row, pl.BlockSpec((tm, 128), lambda i, k: (i, 0)), pl.BlockSpec((1, D), lambda i, k: (0, 0))]
        + more_specs,
        out_specs=[row, row, pl.BlockSpec((8, D), lambda i, k: (0, 0))],
        out_shape=[jax.ShapeDtypeStruct((S, D), F32), jax.ShapeDtypeStruct((S, D), BF16),
                   jax.ShapeDtypeStruct((8, D), F32)],
        scratch_shapes=[pltpu.VMEM((tm, D), F32)],
        compiler_params=_params(2),
    )(dprea, w1_g, dpre2, xhat1, rstd1, g1, *more)


def _d_merged(dpre1b, wout_g, proj, ya, yb):
    tm, tn = 512, 1024

    def body(d_ref, w_ref, ga_ref, gb_ref, ya_ref, yb_ref, dya_ref, dyb_ref, dga_ref, dgb_ref):
        dm = _dot_nt(d_ref[...], w_ref[...])
        sa = _sigmoid(ga_ref[...])
        sb = _sigmoid(gb_ref[...])
        dya_ref[...] = (dm * sa).astype(BF16)
        dyb_ref[...] = (dm * sb).astype(BF16)
        dga_ref[...] = (dm * ya_ref[...].astype(F32) * sa * (1.0 - sa)).astype(BF16)
        dgb_ref[...] = (dm * yb_ref[...].astype(F32) * sb * (1.0 - sb)).astype(BF16)

    tile = pl.BlockSpec((tm, tn), lambda i, j: (i, j))
    return pl.pallas_call(
        body, name="d_merged", grid=(S // tm, D // tn),
        in_specs=[pl.BlockSpec((tm, D), lambda i, j: (i, 0)), pl.BlockSpec((tn, D), lambda i, j: (j, 0)),
                  pl.BlockSpec((tm, tn), lambda i, j: (i, 5 + j)), pl.BlockSpec((tm, tn), lambda i, j: (i, 7 + j)),
                  tile, tile],
        out_specs=[tile] * 4,
        out_shape=[jax.ShapeDtypeStruct((S, D), BF16)] * 4,
        compiler_params=_params(2),
    )(dpre1b, wout_g, proj, proj, ya, yb)


def _d_branches(dya, dyb, wpa_g, wpb_g, after=None):
    tk = 512

    def body(da_ref, db_ref, wa_ref, wb_ref, oa_ref, ob_ref):
        @pl.when(pl.program_id(0) == 0)
        def _():
            oa_ref[...] = jnp.zeros_like(oa_ref)
            ob_ref[...] = jnp.zeros_like(ob_ref)

        oa_ref[...] += _dot_nt(da_ref[...], wa_ref[...])
        ob_ref[...] += _dot_nt(db_ref[...], wb_ref[...])

    body, more_specs, more = _behind(body, 4, after)
    return pl.pallas_call(
        body, name="d_branches", grid=(D // tk,),
        in_specs=[pl.BlockSpec((S, tk), lambda k: (0, k)), pl.BlockSpec((S, tk), lambda k: (0, k)),
                  pl.BlockSpec((None, DA, tk), lambda k: (k, 0, 0)), pl.BlockSpec((None, DB, tk), lambda k: (k, 0, 0))]
        + more_specs,
        out_specs=[pl.BlockSpec((S, DA), lambda k: (0, 0)), pl.BlockSpec((S, DB), lambda k: (0, 0))],
        out_shape=[jax.ShapeDtypeStruct((S, DA), F32), jax.ShapeDtypeStruct((S, DB), F32)],
        compiler_params=_params(1),
    )(dya, dyb, wpa_g, wpb_g, *more)


def _gmlp_bwd(proj, dgmlp, ws, ws_t, bsp_b, gain_v, bias_v):
    def body(u_ref, vb_ref, dg_ref, ws_ref, wst_ref, bsp_ref, g_ref, be_ref, duv_ref, gws_ref, gbs_ref, st_ref):
        @pl.when(pl.program_id(0) == 0)
        def _():
            gws_ref[...] = jnp.zeros_like(gws_ref)
            gbs_ref[...] = jnp.zeros_like(gbs_ref)
            st_ref[...] = jnp.zeros_like(st_ref)

        u, tu, u_act, vb, tv, rstd, xhat, vn = _gmlp_parts(u_ref, vb_ref, g_ref, be_ref)
        dg = dg_ref[...]
        dz = dg * u_act
        row = lax.broadcasted_iota(jnp.int32, (128, 128), 0)
        col = lax.broadcasted_iota(jnp.int32, (128, 128), 1)
        causal = row >= col
        causal_t = row <= col
        dvn_parts = []
        z_parts = []
        for g in range(NH):
            cols = slice(g * 128, (g + 1) * 128)
            vng = vn[:, cols].astype(BF16)
            dzg = dz[:, cols]
            dzb = dzg.astype(BF16)
            wsg = jnp.where(causal, ws_ref[g], 0.0).astype(BF16)
            wsg_t = jnp.where(causal_t, wst_ref[g], 0.0).astype(BF16)
            z_parts.append(_dot(wsg, vng) + bsp_ref[g])
            gws_ref[g] += jnp.where(causal, _dot_nt(dzb, vng), 0.0)
            gbs_ref[g] += jnp.broadcast_to(jnp.sum(dzg, axis=1, keepdims=True), (128, 128))
            dvn_parts.append(_dot(wsg_t, dzb))
        z = jnp.concatenate(z_parts, axis=1)
        dvn = jnp.concatenate(dvn_parts, axis=1)
        du = dg * z * _gelu_grad(u, tu)
        st_ref[0:1, :] += jnp.sum(dvn * xhat, axis=0, keepdims=True)
        st_ref[1:2, :] += jnp.sum(dvn, axis=0, keepdims=True)
        gg = dvn * g_ref[...]
        dgv = rstd * (gg - jnp.mean(gg, axis=1, keepdims=True) - xhat * jnp.mean(gg * xhat, axis=1, keepdims=True))
        dvb = dgv * _gelu_grad(vb, tv)
        duv_ref[:, 0:DB] = du.astype(BF16)
        duv_ref[:, DB:2 * DB] = dvb.astype(BF16)

    full3 = pl.BlockSpec((NH, 128, 128), lambda c: (0, 0, 0))
    vec = pl.BlockSpec((1, DB), lambda c: (0, 0))
    return pl.pallas_call(
        body, name="gmlp_bwd", grid=(NBLK,),
        in_specs=[pl.BlockSpec((128, DB), lambda c: (c, 3)), pl.BlockSpec((128, DB), lambda c: (c, 4)),
                  pl.BlockSpec((128, DB), lambda c: (c, 0)), full3, full3, full3, vec, vec],
        out_specs=[pl.BlockSpec((128, 2 * DB), lambda c: (c, 0)), full3, full3, pl.BlockSpec((8, DB), lambda c: (0, 0))],
        out_shape=[jax.ShapeDtypeStruct((S, 2 * DB), BF16), jax.ShapeDtypeStruct((NH, 128, 128), F32),
                   jax.ShapeDtypeStruct((NH, 128, 128), F32), jax.ShapeDtypeStruct((8, DB), F32)],
        compiler_params=_params(1),
    )(proj, proj, dgmlp, ws, ws_t, bsp_b, gain_v, bias_v)


def _rel_bias_grad(ds_sums):
    buckets = jnp.asarray(np.stack([_bucket_tile(d) for _, d in PATTERNS]))

    def body(bk_ref, ds_ref, o_ref):
        row = lax.broadcasted_iota(jnp.int32, (N_BUCKETS, 128), 0)
        lane = lax.broadcasted_iota(jnp.int32, (N_BUCKETS, 128), 1)

        def one_bucket(t, out):
            hits = [bk_ref[p] == t for p in range(3)]
            for h in range(NH):
                tot = jnp.zeros((128, 256), F32)
                for p in range(3):
                    tot = tot + jnp.where(hits[p], ds_ref[p, h], 0.0)
                out = jnp.where((row == t) & (lane == h), jnp.sum(tot), out)
            return out

        o_ref[...] = lax.fori_loop(0, N_BUCKETS, one_bucket, jnp.zeros((N_BUCKETS, 128), F32))

    return pl.pallas_call(
        body, name="rel_bias_grad",
        in_specs=[pl.BlockSpec(memory_space=pltpu.VMEM)] * 2, out_specs=pl.BlockSpec(memory_space=pltpu.VMEM),
        out_shape=jax.ShapeDtypeStruct((N_BUCKETS, 128), F32),
        compiler_params=pltpu.CompilerParams(vmem_limit_bytes=VMEM_LIMIT),
    )(buckets, ds_sums)


def _d_x(dproj, win_g, dpre1, after=None):
    tm, tk = 512, 2304
    per = 2304 // tk
    nk = DIN // tk

    def body(a_ref, w_ref, d_ref, o_ref, acc):
        k = pl.program_id(1)

        @pl.when(k == 0)
        def _():
            acc[...] = ALPHA * d_ref[...]

        acc[...] += _dot_nt(a_ref[...], w_ref[...])

        @pl.when(k == nk - 1)
        def _():
            o_ref[...] = acc[...]

    row = pl.BlockSpec((tm, D), lambda i, k: (i, 0))
    body, more_specs, more = _behind(body, 3, after)
    return pl.pallas_call(
        body, name="d_x", grid=(S // tm, nk),
        in_specs=[pl.BlockSpec((tm, tk), lambda i, k: (i, k)),
                  pl.BlockSpec((None, D, tk), lambda i, k: (k // per, 0, k % per)), row] + more_specs,
        out_specs=row, out_shape=jax.ShapeDtypeStruct((S, D), F32),
        scratch_shapes=[pltpu.VMEM((tm, D), F32)],
        compiler_params=_params(2),
    )(dproj, win_g, dpre1, *more)


def _adamw(w, g, m, v, name):
    rows, cols = w.shape
    tm = max(t for t in range(8, 257, 8) if rows % t == 0)

    def body(w_ref, g_ref, m_ref, v_ref, d_ref, nm_ref, nv_ref, go_ref):
        g = g_ref[...]
        m = ADAM_B1 * m_ref[...] + (1.0 - ADAM_B1) * g
        v = ADAM_B2 * v_ref[...] + (1.0 - ADAM_B2) * (g * g)
        m_hat = m / (1.0 - ADAM_B1 ** ADAM_STEP)
        v_hat = v / (1.0 - ADAM_B2 ** ADAM_STEP)
        d_ref[...] = -ADAM_LR * (m_hat / (jnp.sqrt(v_hat) + ADAM_EPS) + ADAM_WD * w_ref[...])
        nm_ref[...] = m
        nv_ref[...] = v
        go_ref[...] = g

    spec = pl.BlockSpec((tm, cols), lambda i: (i, 0))
    return pl.pallas_call(
        body, name=name, grid=(rows // tm,), in_specs=[spec] * 4, out_specs=[spec] * 4,
        out_shape=[jax.ShapeDtypeStruct((rows, cols), F32)] * 4, compiler_params=_params(1),
    )(w, g, m, v)


def _position():
    x, y, c = lax.axis_index("x"), lax.axis_index("y"), lax.axis_index("c")
    chips = [(1 - x, y), (x, 1 - y), (1 - x, 1 - y)]
    return x, y, c, chips


def _remote(src, dst, send_sems, recv_sems, k, to):
    return pltpu.make_async_remote_copy(src_ref=src, dst_ref=dst, send_sem=send_sems.at[k], recv_sem=recv_sems.at[k],
                                        device_id=to, device_id_type=MESH)


def _place_shard(w, name, after=None):
    rows, cols = w.shape
    tm = 256
    x, y = lax.axis_index("x"), lax.axis_index("y")

    def body(chip_ref, w_ref, o_ref):
        o_ref[...] = w_ref[...].astype(BF16)

    more_specs, more = ([ANY], [after]) if after is not None else ([], [])
    if after is not None:
        inner = body
        body = lambda chip_ref, w_ref, after_ref, o_ref: inner(chip_ref, w_ref, o_ref)
    return pl.pallas_call(
        body, name=name,
        grid_spec=pltpu.PrefetchScalarGridSpec(
            num_scalar_prefetch=1, grid=(rows // tm,),
            in_specs=[pl.BlockSpec((tm, cols), lambda i, chip: (i, 0))] + more_specs,
            out_specs=pl.BlockSpec((None, tm, cols), lambda i, chip: (chip[0], i, 0))),
        out_shape=jax.ShapeDtypeStruct((N_CHIPS, rows, cols), BF16),
        compiler_params=_params(1),
    )(jnp.reshape(2 * x + y, (1,)).astype(jnp.int32), w, *more)


def _to_bf16(x, name, after=None):
    tm = 256

    def body(x_ref, o_ref):
        o_ref[...] = x_ref[...].astype(BF16)

    spec = pl.BlockSpec((tm, x.shape[1]), lambda i: (i, 0))
    body, more_specs, more = _behind(body, 1, after)
    return pl.pallas_call(
        body, name=name, grid=(x.shape[0] // tm,), in_specs=[spec] + more_specs, out_specs=spec,
        out_shape=jax.ShapeDtypeStruct(x.shape, BF16), compiler_params=_params(1),
    )(x, *more)


HBM = pl.BlockSpec(memory_space=pltpu.HBM)
SEM = pl.BlockSpec(memory_space=pltpu.SEMAPHORE)
EFFECT = pltpu.SideEffectType.DATAFLOW_SIDE_EFFECTING


def _comm_call(name, body, bufs, sems_in, sems_out, after=None, token=False):
    nb, ns, no = len(bufs), len(sems_in), len(sems_out)
    n_in = nb + ns + (after is not None)

    def wrapped(*refs):
        body(refs[:nb], refs[nb:nb + ns], refs[n_in + nb:n_in + nb + no])
        if token:
            refs[-1][...] = jnp.zeros((8, 128), F32)

    outs = pl.pallas_call(
        wrapped, name=name,
        in_specs=[HBM] * nb + [SEM] * ns + ([ANY] if after is not None else []),
        out_specs=[HBM] * nb + [SEM] * no + ([pl.BlockSpec(memory_space=pltpu.VMEM)] if token else []),
        out_shape=[pltpu.HBM(b.shape, b.dtype) for b in bufs] + [pltpu.SemaphoreType.DMA((k,)) for k in sems_out]
        + ([jax.ShapeDtypeStruct((8, 128), F32)] if token else []),
        input_output_aliases={i: i for i in range(nb)},
        compiler_params=pltpu.CompilerParams(has_side_effects=EFFECT),
    )(*[pltpu.with_memory_space_constraint(b, pltpu.HBM) for b in bufs], *sems_in, *([after] if after is not None else []))
    return list(outs[:nb]), list(outs[nb:nb + no]), (outs[-1] if token else None)


RING_STAGES = {"ici_near": 2, "ici_far": 2, "d2d_near": 2, "d2d_far": 1}


def _ring_copies(buf, send_sems, recv_sems, k0, stage):
    x, y, c, _ = _position()
    hr = buf.shape[1] // 2
    qr = hr // 2
    half = lambda chip, h: buf.at[chip, pl.ds(h * hr, hr), :]
    quarter = lambda chip, h, q: buf.at[chip, pl.ds(h * hr + q * qr, qr), :]
    mine, x_chip, y_chip, far_chip = 2 * x + y, 2 * (1 - x) + y, 2 * x + (1 - y), 2 * (1 - x) + (1 - y)
    to_x, to_y, sibling = (1 - x, y, c), (x, 1 - y, c), (x, y, 1 - c)
    if stage == "ici_near":
        moves = [(half(mine, c), to_x, half(x_chip, c)), (half(mine, c), to_y, half(y_chip, c))]
    elif stage == "ici_far":
        moves = [(quarter(x_chip, c, 0), to_y, quarter(far_chip, c, 0)),
                 (quarter(y_chip, c, 1), to_x, quarter(far_chip, c, 1))]
    elif stage == "d2d_near":
        moves = [(half(x_chip, c), sibling, half(x_chip, 1 - c)), (half(y_chip, c), sibling, half(y_chip, 1 - c))]
    else:
        moves = [(half(far_chip, c), sibling, half(far_chip, 1 - c))]
    sends = [_remote(src, src, send_sems, recv_sems, k0 + i, to) for i, (src, to, _) in enumerate(moves)]
    arrivals = [_remote(got, got, send_sems, recv_sems, k0 + i, (x, y, c)) for i, (_, _, got) in enumerate(moves)]
    return sends, arrivals


def _ring_call(name, groups, actions, after=None):
    tags = list(dict.fromkeys(t for _, t, _ in actions))
    counts = {t: len(groups[t]["bufs"]) for t in tags}
    first = {t: sum(counts[u] for u in tags[:i]) for i, t in enumerate(tags)}
    waits = [(t, s) for v, t, s in actions if v == "wait"]
    starts = [(t, s) for v, t, s in actions if v == "start"]

    def body(bufs, sems_in, sems_out):
        for verb, t, s in actions:
            at, sems = (starts.index((t, s)), sems_out) if verb == "start" else (waits.index((t, s)), sems_in)
            for w in range(counts[t]):
                sends, arrivals = _ring_copies(bufs[first[t] + w], sems[2 * at], sems[2 * at + 1], RING_STAGES[s] * w, s)
                if verb == "start":
                    for cp in sends:
                        cp.start()
                else:
                    for cp in arrivals:
                        cp.wait_recv()
                    for cp in sends:
                        cp.wait_send()

    bufs, sems, token = _comm_call(
        name, body, [b for t in tags for b in groups[t]["bufs"]],
        [sem for t, s in waits for sem in groups[t]["sems"][s]],
        [RING_STAGES[s] * counts[t] for t, s in starts for _ in (0, 1)], after, token=True)
    for t in tags:
        groups[t]["bufs"] = bufs[first[t]:first[t] + counts[t]]
    for t, s in waits:
        del groups[t]["sems"][s]
    for i, (t, s) in enumerate(starts):
        groups[t]["sems"][s] = (sems[2 * i], sems[2 * i + 1])
    return token


def _cx_copies(src, dst, send_sems, recv_sems, k0):
    x, y, c, chips = _position()
    sends = [_remote(src.at[2 * cx + cy], dst.at[2 * x + y], send_sems, recv_sems, k0 + j, (cx, cy, c))
             for j, (cx, cy) in enumerate(chips)]
    arrivals = [_remote(dst.at[2 * cx + cy], dst.at[2 * cx + cy], send_sems, recv_sems, k0 + j, (x, y, c))
                for j, (cx, cy) in enumerate(chips)]
    return sends, arrivals


def _cx_start(name, pair_sums):
    n = len(pair_sums)
    landing = [lax.empty(p.shape, p.dtype) for p in pair_sums]

    def body(bufs, _, sems):
        for w in range(n):
            for cp in _cx_copies(bufs[w], bufs[n + w], sems[0], sems[1], 3 * w)[0]:
                cp.start()

    bufs, sems, token = _comm_call(name, body, list(pair_sums) + landing, [], [3 * n, 3 * n], token=True)
    return (bufs, sems), token


def _cx_wait(name, state, after):
    bufs, sems = state
    n = len(bufs) // 2

    def body(refs, sems_in, _):
        for w in range(n):
            sends, arrivals = _cx_copies(refs[w], refs[n + w], sems_in[0], sems_in[1], 3 * w)
            for cp in arrivals:
                cp.wait_recv()
            for cp in sends:
                cp.wait_send()

    bufs, _, _ = _comm_call(name, body, bufs, sems, [], after)
    return bufs[:n], bufs[n:]


def _px_copies(src, dst, send_sems, recv_sems, k):
    x, y, c, _ = _position()
    hr = src.shape[1] // 2
    send = _remote(src.at[:, pl.ds((1 - c) * hr, hr), :], dst, send_sems, recv_sems, k, (x, y, 1 - c))
    arrival = _remote(dst, dst, send_sems, recv_sems, k, (x, y, c))
    return send, arrival


def _px_start(name, grads):
    n = len(grads)
    landing = [lax.empty((N_CHIPS, g.shape[1] // 2, g.shape[2]), F32) for g in grads]

    def body(bufs, _, sems):
        for w in range(n):
            _px_copies(bufs[w], bufs[n + w], sems[0], sems[1], w)[0].start()

    bufs, sems, token = _comm_call(name, body, list(grads) + landing, [], [n, n], token=True)
    return (bufs, sems), token


def _px_wait(name, state, after):
    bufs, sems = state
    n = len(bufs) // 2

    def body(refs, sems_in, _):
        for w in range(n):
            send, arrival = _px_copies(refs[w], refs[n + w], sems_in[0], sems_in[1], w)
            arrival.wait_recv()
            send.wait_send()

    bufs, _, _ = _comm_call(name, body, bufs, sems, [], after)
    return bufs[:n], bufs[n:]


def _pair_sum(grad, got, name):
    _, rows, cols = grad.shape
    hr = rows // 2
    tm = min(hr, 256)
    nb = hr // tm
    c = lax.axis_index("c")

    def body(c_ref, g_ref, o_ref, out_ref):
        out_ref[...] = (g_ref[...] + o_ref[...]).astype(BF16)

    return pl.pallas_call(
        body, name=name,
        grid_spec=pltpu.PrefetchScalarGridSpec(
            num_scalar_prefetch=1, grid=(N_CHIPS, nb),
            in_specs=[pl.BlockSpec((None, tm, cols), lambda s, i, c_ref: (s, c_ref[0] * nb + i, 0)),
                      pl.BlockSpec((None, tm, cols), lambda s, i, c_ref: (s, i, 0))],
            out_specs=pl.BlockSpec((None, tm, cols), lambda s, i, c_ref: (s, i, 0))),
        out_shape=jax.ShapeDtypeStruct((N_CHIPS, hr, cols), BF16),
        compiler_params=_params(2),
    )(jnp.reshape(c, (1,)).astype(jnp.int32), grad, got)


def _chip_sum(parts, pair_sums, name):
    _, hr, cols = parts.shape
    tm = min(hr, 256)
    nb = hr // tm
    x, y, c = lax.axis_index("x"), lax.axis_index("y"), lax.axis_index("c")

    def body(pos_ref, p_ref, own_ref, o_ref):
        chip = pos_ref[0]
        own = own_ref[...].astype(F32)
        term = lambda s: jnp.where(chip == s, own, p_ref[s].astype(F32))
        o_ref[...] = ((term(0) + term(1)) + term(2)) + term(3)

    return pl.pallas_call(
        body, name=name,
        grid_spec=pltpu.PrefetchScalarGridSpec(
            num_scalar_prefetch=1, grid=(nb,),
            in_specs=[pl.BlockSpec((N_CHIPS, tm, cols), lambda i, pos: (0, i, 0)),
                      pl.BlockSpec((None, tm, cols), lambda i, pos: (pos[0], i, 0))],
            out_specs=pl.BlockSpec((tm, cols), lambda i, pos: (pos[1] * nb + i, 0))),
        out_shape=jax.ShapeDtypeStruct((2 * hr, cols), F32), compiler_params=_params(1),
    )(jnp.stack([2 * x + y, c]).astype(jnp.int32), parts, pair_sums)


def _share_halves(bufs, name):
    n = len(bufs)

    def body(*refs):
        outs = refs[n:2 * n]
        send_sems, recv_sems = refs[2 * n:]
        x, y, c, _ = _position()
        copies = []
        for w in range(n):
            hr = outs[w].shape[0] // 2
            mine = outs[w].at[pl.ds(c * hr, hr), :]
            cp = _remote(mine, mine, send_sems, recv_sems, w, (x, y, 1 - c))
            cp.start()
            copies.append(cp)
        for w in range(n):
            hr = outs[w].shape[0] // 2
            theirs = outs[w].at[pl.ds((1 - c) * hr, hr), :]
            _remote(theirs, theirs, send_sems, recv_sems, w, (x, y, c)).wait_recv()
        for cp in copies:
            cp.wait_send()

    return pl.pallas_call(
        body, name=name,
        in_specs=[ANY] * n, out_specs=[ANY] * n,
        out_shape=[jax.ShapeDtypeStruct(b.shape, b.dtype) for b in bufs],
        input_output_aliases={w: w for w in range(n)},
        scratch_shapes=[pltpu.SemaphoreType.DMA((n,)), pltpu.SemaphoreType.DMA((n,))],
    )(*bufs)


def _allreduce_small(g):
    rows = g.shape[0]

    def body(g_ref, o_ref, sib, slots, send_sems, recv_sems):
        x, y, c, chips = _position()
        me = (x, y, c)
        my_chip = 2 * x + y
        pair = _remote(g_ref, sib, send_sems, recv_sems, 0, (x, y, 1 - c))
        pair.start()
        pair.wait()
        slots[my_chip] = g_ref[...] + sib[...]
        sent = []
        for j, (cx, cy) in enumerate(chips):
            cp = _remote(slots.at[my_chip], slots.at[my_chip], send_sems, recv_sems, 1 + j, (cx, cy, c))
            cp.start()
            sent.append(cp)
        for j, (cx, cy) in enumerate(chips):
            got = slots.at[2 * cx + cy]
            _remote(got, got, send_sems, recv_sems, 1 + j, me).wait_recv()
        for cp in sent:
            cp.wait_send()
        o_ref[...] = ((slots[0] + slots[1]) + slots[2]) + slots[3]

    vm = pl.BlockSpec(memory_space=pltpu.VMEM)
    return pl.pallas_call(
        body, name="allreduce_small",
        in_specs=[vm], out_specs=vm, out_shape=jax.ShapeDtypeStruct((rows, 128), F32),
        scratch_shapes=[pltpu.VMEM((rows, 128), F32), pltpu.VMEM((N_CHIPS, rows, 128), F32),
                        pltpu.SemaphoreType.DMA((4,)), pltpu.SemaphoreType.DMA((4,))],
        compiler_params=pltpu.CompilerParams(vmem_limit_bytes=VMEM_LIMIT),
    )(g)


_SMALL =("rel_bias", "ln_v_gain", "ln_v_bias", "w_spatial", "b_spatial", "ln1_gain", "ln1_bias",
          "b_ff1", "b_ff2", "ln2_gain", "ln2_bias")
_SMALL_ROWS = 1200
_LOSS_AT = (152832 // 128, 0)


def _pack_small(parts):
    flat = jnp.concatenate([parts[k].reshape(-1).astype(F32) for k in _SMALL])
    flat = jnp.pad(flat, (0, _SMALL_ROWS * 128 - flat.shape[0]))
    return flat.reshape(_SMALL_ROWS, 128)


def _unpack_small(packed, like):
    flat = packed.reshape(-1)
    out, at = {}, 0
    for k in _SMALL:
        n = math.prod(like[k].shape)
        out[k] = flat[at:at + n].reshape(like[k].shape)
        at += n
    return out


def kernel(x, w_in, rel_bias, ln_v_gain, ln_v_bias, w_spatial, b_spatial, w_proj_a, w_proj_b, w_out, ln1_gain, ln1_bias, w_ff1, b_ff1, w_ff2, b_ff2, ln2_gain, ln2_bias, loss_target, m_w_in, m_rel_bias, m_ln_v_gain, m_ln_v_bias, m_w_spatial, m_b_spatial, m_w_proj_a, m_w_proj_b, m_w_out, m_ln1_gain, m_ln1_bias, m_w_ff1, m_b_ff1, m_w_ff2, m_b_ff2, m_ln2_gain, m_ln2_bias, v_w_in, v_rel_bias, v_ln_v_gain, v_ln_v_bias, v_w_spatial, v_b_spatial, v_w_proj_a, v_w_proj_b, v_w_out, v_ln1_gain, v_ln1_bias, v_w_ff1, v_b_ff1, v_w_ff2, v_b_ff2, v_ln2_gain, v_ln2_bias):
    args = dict(locals())
    big = ("w_in", "w_proj_a", "w_proj_b", "w_out", "w_ff1", "w_ff2")
    weights = ("w_in", "rel_bias", "ln_v_gain", "ln_v_bias", "w_spatial", "b_spatial", "w_proj_a", "w_proj_b", "w_out",
               "ln1_gain", "ln1_bias", "w_ff1", "b_ff1", "w_ff2", "b_ff2", "ln2_gain", "ln2_bias")

    xs = x[0]
    target = loss_target[0]

    ring = {"a": {"bufs": [_place_shard(w_in[0], "place_w_in")], "sems": {}}}
    tok = _ring_call("allgather_a_near", ring, [("start", "a", "ici_near")])
    placed = [_place_shard(args[k][0], f"place_{k}", after=tok) for k in big[1:]]
    for tag, bufs in (("b", placed[0:3]), ("c", placed[3:4]), ("d", placed[4:5])):
        ring[tag] = {"bufs": bufs, "sems": {}}
    xb = _to_bf16(xs, "x_to_bf16", after=placed[4])
    _ring_call("allgather_a_far", ring, [("wait", "a", "ici_near"), ("start", "a", "ici_far"), ("start", "a", "d2d_near"),
                                         ("start", "b", "ici_near"), ("start", "c", "ici_near")], after=xb)
    _ring_call("allgather_a_near_done", ring, [("wait", "a", "d2d_near")])

    mx, my = lax.axis_index("x"), lax.axis_index("y")
    near = jnp.stack([2 * mx + my, 2 * (1 - mx) + my, 2 * mx + (1 - my)]).astype(jnp.int32)
    far = jnp.reshape(2 * (1 - mx) + (1 - my), (1,)).astype(jnp.int32)
    proj = _proj(xb, ring["a"]["bufs"][0], near, "proj_near")
    _ring_call("allgather_a_last", ring, [("wait", "a", "ici_far"), ("start", "a", "d2d_far")], after=proj)
    _ring_call("allgather_a_done", ring, [("wait", "a", "d2d_far")])
    (win_g,) = ring["a"]["bufs"]
    proj = _proj(xb, win_g, far, "proj_far", into=proj)
    _ring_call("allgather_b_far", ring, [("wait", "b", "ici_near"), ("start", "b", "ici_far"), ("start", "b", "d2d_near"),
                                         ("start", "d", "ici_near")], after=proj)
    ws = w_spatial[0]
    ws_t = jnp.transpose(ws, (0, 2, 1))
    bsp_b = jnp.broadcast_to(b_spatial[0][:, :, None], (NH, 128, 128))
    gmlp = _gmlp_fwd(proj, ws, bsp_b, ln_v_gain, ln_v_bias)
    attn, lse = _attention_fwd(proj, rel_bias)
    _ring_call("allgather_b_last_c_far", ring,
               [("wait", "b", "ici_far"), ("start", "b", "d2d_far"),
                ("wait", "c", "ici_near"), ("start", "c", "ici_far"), ("start", "c", "d2d_near")], after=attn)
    _ring_call("allgather_b_done", ring, [("wait", "b", "d2d_near"), ("wait", "b", "d2d_far")])
    wpa_g, wpb_g, wout_g = ring["b"]["bufs"]
    wout_full = wout_g.reshape(D, D)
    ya, yb, merged = _branch(attn, gmlp, wpa_g, wpb_g, proj)
    xhat1, rstd1, h1b = _out_ln1(merged, wout_full, xs, ln1_gain, ln1_bias)
    _ring_call("allgather_c_last_d_far", ring,
               [("wait", "c", "ici_far"), ("start", "c", "d2d_far"),
                ("wait", "d", "ici_near"), ("start", "d", "ici_far"), ("start", "d", "d2d_near")], after=h1b)
    _ring_call("allgather_c_done", ring, [("wait", "c", "d2d_near"), ("wait", "c", "d2d_far")])
    (w1_g,) = ring["c"]["bufs"]
    a, r = _ff1(h1b, w1_g, b_ff1)
    _ring_call("allgather_d_last", ring, [("wait", "d", "ici_far"), ("start", "d", "d2d_far")], after=a)
    _ring_call("allgather_d_done", ring, [("wait", "d", "d2d_near"), ("wait", "d", "d2d_far")])
    (w2_g,) = ring["d"]["bufs"]
    w2_full = w2_g.reshape(DFF, D)
    dpre2, dpre2b, st2 = _ff2_ln2_loss(a, w2_full, xhat1, ln1_gain, ln1_bias, b_ff2, ln2_gain, ln2_bias, target)

    def pair_and_chip(tag, state, after):
        local, from_sibling = _px_wait(f"pair_exchange_wait_{tag}", state, after)
        pair_sums = [_pair_sum(g, o, f"pair_sum_{tag}_{i}") for i, (g, o) in enumerate(zip(local, from_sibling))]
        return _cx_start(f"chip_exchange_start_{tag}", pair_sums)

    g_w2 = _grad_w(a, dpre2b, "grad_w_ff2", 512, 2048, False)
    px, tok = _px_start("pair_exchange_start_w_ff2", [g_w2.reshape(N_CHIPS, DFF // N_CHIPS, D)])
    dprea, g_b1 = _d_ff1(dpre2b, w2_full, r, after=tok)
    cx_w2, tok = pair_and_chip("w_ff2", px, dprea)
    g_w1 = _grad_w(h1b, dprea, "grad_w_ff1", 512, 2048, True, after=tok)
    px, tok = _px_start("pair_exchange_start_w_ff1", [g_w1])
    dpre1, dpre1b, st1 = _d_h1_ln1(dprea, w1_g, dpre2, xhat1, rstd1, ln1_gain, after=tok)
    cx_w1, tok = pair_and_chip("w_ff1", px, dpre1b)
    g_wout = _grad_w(merged, dpre1b, "grad_w_out", 512, 2048, False, after=tok)
    dya, dyb, dga, dgb = _d_merged(dpre1b, wout_full, proj, ya, yb)
    g_wpa = _grad_w(attn, dya, "grad_w_proj_a", 1024, 512, True)
    g_wpb = _grad_w(gmlp, dyb, "grad_w_proj_b", 1024, 512, True)
    px, tok = _px_start("pair_exchange_start_b", [g_wpa, g_wpb, g_wout.reshape(N_CHIPS, D // N_CHIPS, D)])
    dattn, dgmlp = _d_branches(dya, dyb, wpa_g, wpb_g, after=tok)
    duv, g_ws, g_bs, stv = _gmlp_bwd(proj, dgmlp, ws, ws_t, bsp_b, ln_v_gain, ln_v_bias)
    cx_b, tok = pair_and_chip("b", px, duv)
    dq, dk, dv, ds_sums = _attention_bwd(proj, dattn, attn, lse, rel_bias, after=tok)
    g_rb = _rel_bias_grad(ds_sums)[:, :NH]

    small_g = dict(rel_bias=g_rb, ln_v_gain=stv[0], ln_v_bias=stv[1], w_spatial=g_ws, b_spatial=g_bs[:, :, 0],
                   ln1_gain=st1[0], ln1_bias=st1[1], b_ff1=g_b1, b_ff2=st2[2], ln2_gain=st2[0], ln2_bias=st2[1])
    gs = _allreduce_small(_pack_small(small_g).at[_LOSS_AT].set(st2[3, 0]))
    ds_, ms_, vs_, _ = _adamw(_pack_small({k: args[k] for k in _SMALL}), gs,
                           _pack_small({k: args["m_" + k] for k in _SMALL}),
                           _pack_small({k: args["v_" + k] for k in _SMALL}), "adamw_small")
    like = {k: args[k] for k in _SMALL}
    grads, deltas, new_m, new_v = (_unpack_small(t, like) for t in (gs, ds_, ms_, vs_))

    dproj = jnp.concatenate([dq, dk, dv, duv, dga, dgb], axis=1)
    g_win = _grad_w(xb, dproj, "grad_w_in", 512, 2304, True, after=gs)
    px, tok = _px_start("pair_exchange_start_w_in", [g_win])
    grad_x = _d_x(dproj, win_g, dpre1, after=tok)
    cx_in, tok = pair_and_chip("w_in", px, grad_x)

    def reduce_finish(tag, state, names, after):
        pair_sums, from_chips = _cx_wait(f"chip_exchange_wait_{tag}", state, after)
        halves = [_chip_sum(p, own, f"chip_sum_{k}") for p, own, k in zip(from_chips, pair_sums, names)]
        last = None
        for k, g in zip(names, _share_halves(halves, f"share_halves_{tag}")):
            d_, m_, v_, g_ = _adamw(args[k][0], g, args["m_" + k][0], args["v_" + k][0], f"adamw_{k}")
            grads[k], deltas[k], new_m[k], new_v[k] = g_[None], d_[None], m_[None], v_[None]
            last = d_
        return last

    done = reduce_finish("w_ff2", cx_w2, ["w_ff2"], tok)
    done = reduce_finish("w_ff1", cx_w1, ["w_ff1"], done)
    done = reduce_finish("b", cx_b, ["w_proj_a", "w_proj_b", "w_out"], done)
    reduce_finish("w_in", cx_in, ["w_in"], done)

    loss = gs[_LOSS_AT] * (0.5 / D)
    return (loss, grad_x[None], *[grads[k] for k in weights], *[deltas[k] for k in weights],
            *[new_m[k] for k in weights], *[new_v[k] for k in weights])
```

```python
import functools
import math

import numpy as np
import jax
import jax.numpy as jnp
from jax import lax
from jax.experimental import pallas as pl
from jax.experimental.pallas import tpu as pltpu

F32 = jnp.float32
BF16 = jnp.bfloat16

S = 2048
D = 2048
DA = 1024
DB = 1024
DFF = 8192
DIN = 9216
NH = 8
HD = 128
NBLK = 16
PATTERNS = ((128, 1), (512, 4), (2048, 16))
N_BUCKETS = 32
MAX_DISTANCE = 2048
ALPHA = 2.0 ** 0.25
LN_EPS = 1e-5
NEG_INF = -1e30
SCALE = HD ** -0.5
N_CHIPS = 4

ADAM_LR = 0.001
ADAM_B1 = 0.9
ADAM_B2 = 0.999
ADAM_EPS = 1e-08
ADAM_WD = 0.01
ADAM_STEP = 10

VMEM_LIMIT = 56 * 1024 * 1024
MESH = pl.DeviceIdType.MESH
ANY = pl.BlockSpec(memory_space=pl.ANY)


def _params(n_axes, vmem=VMEM_LIMIT):
    return pltpu.CompilerParams(dimension_semantics=("arbitrary",) * n_axes, vmem_limit_bytes=vmem)


def _bucket_tile(dilation):
    qi = np.arange(128)[:, None]
    kj = np.arange(256)[None, :]
    n = np.clip(128 + qi - kj, 0, 128) * dilation
    max_exact = N_BUCKETS // 2
    nf = np.maximum(n, 1).astype(np.float32)
    large = max_exact + (np.log(nf / np.float32(max_exact)) / np.float32(math.log(MAX_DISTANCE / max_exact))
                         * np.float32(N_BUCKETS - max_exact)).astype(np.int32)
    large = np.minimum(large, N_BUCKETS - 1)
    return np.where(n < max_exact, n, large).astype(np.int32)


def _gelu(x):
    c = math.sqrt(2.0 / math.pi)
    t = jnp.tanh(c * (x + 0.044715 * x * x * x))
    return 0.5 * x * (1.0 + t), t


def _gelu_grad(x, t):
    c = math.sqrt(2.0 / math.pi)
    return 0.5 * (1.0 + t) + 0.5 * x * (1.0 - t * t) * c * (1.0 + 3.0 * 0.044715 * x * x)


def _sigmoid(x):
    return 1.0 / (1.0 + jnp.exp(-x))


def _dot(a, b):
    return jnp.dot(a, b, preferred_element_type=F32)


def _behind(body, n_in, after):
    if after is None:
        return body, [], []
    return (lambda *refs: body(*refs[:n_in], *refs[n_in + 1:])), [ANY], [after]


def _dot_nt(a, b):
    return lax.dot_general(a, b, (((1,), (1,)), ((), ())), preferred_element_type=F32)


def _proj(xb, win_g, shards, name, into=None):
    tn = 768
    per = 2304 // tn

    def body(shards_ref, x_ref, w_ref, *rest):
        rest[-1][...] = _dot(x_ref[...], w_ref[...])

    in_specs = [pl.BlockSpec((S, D), lambda j, sh: (0, 0)),
                pl.BlockSpec((None, D, tn), lambda j, sh: (sh[j // per], 0, j % per))]
    return pl.pallas_call(
        body, name=name,
        grid_spec=pltpu.PrefetchScalarGridSpec(
            num_scalar_prefetch=1, grid=(shards.shape[0] * per,),
            in_specs=in_specs + ([ANY] if into is not None else []),
            out_specs=pl.BlockSpec((S, tn), lambda j, sh: (0, sh[j // per] * per + j % per))),
        out_shape=jax.ShapeDtypeStruct((S, DIN), F32),
        input_output_aliases={3: 0} if into is not None else {},
        compiler_params=_params(1),
    )(shards, xb, win_g, *([into] if into is not None else []))


FWD_HEADS_PER_STEP = 4
BWD_HEADS_PER_STEP = 2


def _head_bias_tiles(rb_ref, bk_ref, bias_scr, first_head, hps):
    qi = lax.broadcasted_iota(jnp.int32, (128, 256), 0)
    kj = lax.broadcasted_iota(jnp.int32, (128, 256), 1)
    steps = 128 + qi - kj
    band = (steps >= 0) & (steps <= 128)
    bias_scr[...] = jnp.zeros_like(bias_scr)
    for p in range(len(PATTERNS)):
        bucket = bk_ref[p]

        def one_bucket(t, carry):
            hit = bucket == t
            for j in range(hps):
                bias_scr[p, j] = jnp.where(hit, rb_ref[t, first_head + j], bias_scr[p, j])
            return carry

        lax.fori_loop(0, N_BUCKETS, one_bucket, 0)
        for j in range(hps):
            bias_scr[p, j] = jnp.where(band, bias_scr[p, j], NEG_INF)


def _block_rows(b, dilation):
    nblk = NBLK // dilation
    r, n = b // nblk, b % nblk
    start = r + n * (128 * dilation)
    prev_start = jnp.maximum(start - 128 * dilation, r)
    if dilation == 1:
        return pl.ds(pl.multiple_of(start, 128), 128), pl.ds(pl.multiple_of(prev_start, 128), 128), n > 0
    return pl.ds(start, 128, stride=dilation), pl.ds(prev_start, 128, stride=dilation), n > 0


def _head_specs(first, hps):
    return [pl.BlockSpec((S, HD), lambda g, j=j: (0, first + g * hps + j)) for j in range(hps)]


def _heads_spec(hps):
    return pl.BlockSpec((S, hps * HD), lambda g: (0, g))


def _attention_fwd(proj, rel_bias):
    hps = FWD_HEADS_PER_STEP
    buckets = jnp.asarray(np.stack([_bucket_tile(d) for _, d in PATTERNS]))

    def body(rb_ref, bk_ref, *refs):
        q_refs, k_refs, v_refs = (refs[i * hps:(i + 1) * hps] for i in range(3))
        o_ref, lse_ref, bias_scr = refs[3 * hps:3 * hps + 3]
        acc_scrs, m_scrs, l_scrs = (refs[3 * hps + 3 + i * hps:3 * hps + 3 + (i + 1) * hps] for i in range(3))
        _head_bias_tiles(rb_ref, bk_ref, bias_scr, pl.program_id(0) * hps, hps)
        kj = lax.broadcasted_iota(jnp.int32, (128, 256), 1)
        for p, (_, d) in enumerate(PATTERNS):
            prev_blocks = NBLK // d > 1

            def block(b, carry):
                units = [(j,) + _block_rows(blk, d) for blk in (b, b + NBLK // 2) for j in range(hps)]
                scores = []
                for j, rows, prows, _ in units:
                    q = q_refs[j][rows, :].astype(BF16)
                    cur = _dot_nt(q, k_refs[j][rows, :].astype(BF16))
                    if prev_blocks:
                        cur = jnp.concatenate([_dot_nt(q, k_refs[j][prows, :].astype(BF16)), cur], axis=1)
                    scores.append(cur)
                soft = []
                for u, (j, _, _, has_prev) in enumerate(units):
                    if prev_blocks:
                        s = jnp.where((kj >= 128) | has_prev, scores[u] * SCALE + bias_scr[p, j], NEG_INF)
                    else:
                        s = scores[u] * SCALE + bias_scr[p, j, :, 128:256]
                    m = jnp.max(s, axis=1, keepdims=True)
                    e = jnp.exp(s - m)
                    soft.append((m, jnp.sum(e, axis=1, keepdims=True), e.astype(BF16)))
                outs = []
                for u, (j, rows, prows, _) in enumerate(units):
                    e = soft[u][2]
                    if prev_blocks:
                        outs.append(_dot(e[:, :128], v_refs[j][prows, :].astype(BF16))
                                    + _dot(e[:, 128:], v_refs[j][rows, :].astype(BF16)))
                    else:
                        outs.append(_dot(e, v_refs[j][rows, :].astype(BF16)))
                for u, (j, rows, _, _) in enumerate(units):
                    acc_scr, m_scr, l_scr = acc_scrs[j], m_scrs[j], l_scrs[j]
                    (m, den, _), o = soft[u], outs[u]
                    if p == 0:
                        acc_scr[rows, :] = o
                        m_scr[rows, :] = jnp.broadcast_to(m, (128, HD))
                        l_scr[rows, :] = jnp.broadcast_to(den, (128, HD))
                    else:
                        m_old = m_scr[rows, :]
                        m_new = jnp.maximum(m_old, m)
                        w_old, w_new = jnp.exp(m_old - m_new), jnp.exp(m - m_new)
                        acc_scr[rows, :] = acc_scr[rows, :] * w_old + o * w_new
                        l_scr[rows, :] = l_scr[rows, :] * w_old + den * w_new
                        m_scr[rows, :] = m_new
                return carry

            lax.fori_loop(0, NBLK // 2, block, 0)
        for j in range(hps):
            cols = slice(j * HD, (j + 1) * HD)
            den = l_scrs[j][...]
            o_ref[:, cols] = (acc_scrs[j][...] / den).astype(BF16)
            lse_ref[:, cols] = m_scrs[j][...] + jnp.log(den)

    return pl.pallas_call(
        body, name="attention_fwd", grid=(NH // hps,),
        in_specs=[pl.BlockSpec(memory_space=pltpu.SMEM), pl.BlockSpec((3, 128, 256), lambda g: (0, 0, 0))]
        + _head_specs(0, hps) + _head_specs(NH, hps) + _head_specs(2 * NH, hps),
        out_specs=[_heads_spec(hps), _heads_spec(hps)],
        out_shape=[jax.ShapeDtypeStruct((S, DA), BF16), jax.ShapeDtypeStruct((S, DA), F32)],
        scratch_shapes=[pltpu.VMEM((3, hps, 128, 256), F32)] + [pltpu.VMEM((S, HD), F32)] * (3 * hps),
        compiler_params=_params(1),
    )(rel_bias, buckets, *([proj] * (3 * hps)))


def _attention_bwd(proj, dattn, attn, lse, rel_bias, after=None):
    hps = BWD_HEADS_PER_STEP

    def body(rb_ref, bk_ref, *refs):
        q_refs, k_refs, v_refs, do_refs, o_refs, lse_refs = (refs[i * hps:(i + 1) * hps] for i in range(6))
        dq_ref, dk_ref, dv_ref, ds_ref, bias_scr = refs[6 * hps:6 * hps + 5]
        dl_scrs, dq_scrs, dk_scrs, dv_scrs = (refs[6 * hps + 5 + i * hps:6 * hps + 5 + (i + 1) * hps] for i in range(4))
        _head_bias_tiles(rb_ref, bk_ref, bias_scr, pl.program_id(0) * hps, hps)
        ds_ref[...] = jnp.zeros_like(ds_ref)
        for j in range(hps):
            dq_scrs[j][...] = jnp.zeros((S, HD), F32)
            dk_scrs[j][...] = jnp.zeros((S, HD), F32)
            dv_scrs[j][...] = jnp.zeros((S, HD), F32)
            prod = do_refs[j][...] * o_refs[j][...].astype(F32)
            dl_scrs[j][...] = jnp.broadcast_to(jnp.sum(prod, axis=1, keepdims=True), (S, HD))
        for p, (_, d) in enumerate(PATTERNS):
            prev_blocks = NBLK // d > 1

            def block(b, carry):
                units = [(j,) + _block_rows(blk, d) for blk in (b, b + NBLK // 2) for j in range(hps)]
                ops, raw = [], []
                for j, rows, prows, _ in units:
                    q, do = q_refs[j][rows, :].astype(BF16), do_refs[j][rows, :].astype(BF16)
                    kc, vc = k_refs[j][rows, :].astype(BF16), v_refs[j][rows, :].astype(BF16)
                    if prev_blocks:
                        kp, vp = k_refs[j][prows, :].astype(BF16), v_refs[j][prows, :].astype(BF16)
                        ops.append((q, do, kc, kp))
                        raw.append((_dot_nt(q, kc), _dot_nt(do, vc), _dot_nt(q, kp), _dot_nt(do, vp)))
                    else:
                        ops.append((q, do, kc))
                        raw.append((_dot_nt(q, kc), _dot_nt(do, vc)))
                probs = []
                for u, (j, rows, _, has_prev) in enumerate(units):
                    lse_b, dl_b = lse_refs[j][rows, :], dl_scrs[j][rows, :]
                    p_c = jnp.exp(raw[u][0] * SCALE + bias_scr[p, j, :, 128:256] - lse_b)
                    ds_c = p_c * (raw[u][1] - dl_b)
                    ds_ref[p, j, :, 128:256] += ds_c
                    if prev_blocks:
                        p_p = jnp.where(has_prev, jnp.exp(raw[u][2] * SCALE + bias_scr[p, j, :, 0:128] - lse_b), 0.0)
                        ds_p = p_p * (raw[u][3] - dl_b)
                        ds_ref[p, j, :, 0:128] += ds_p
                        probs.append((p_c, ds_c, p_p, ds_p))
                    else:
                        probs.append((p_c, ds_c))
                grads = []
                for u in range(len(units)):
                    q, do, kc = ops[u][:3]
                    p_c, ds_c = probs[u][:2]
                    dq = _dot(ds_c.astype(BF16), kc)
                    cur = (_dot(ds_c.T.astype(BF16), q) * SCALE, _dot(p_c.T.astype(BF16), do))
                    if prev_blocks:
                        p_p, ds_p = probs[u][2:]
                        dq = dq + _dot(ds_p.astype(BF16), ops[u][3])
                        cur = cur + (_dot(ds_p.T.astype(BF16), q) * SCALE, _dot(p_p.T.astype(BF16), do))
                    grads.append((dq * SCALE,) + cur)
                for u, (j, rows, prows, _) in enumerate(units):
                    dq_scrs[j][rows, :] += grads[u][0]
                    dk_scrs[j][rows, :] += grads[u][1]
                    dv_scrs[j][rows, :] += grads[u][2]
                    if prev_blocks:
                        dk_scrs[j][prows, :] += grads[u][3]
                        dv_scrs[j][prows, :] += grads[u][4]
                return carry

            lax.fori_loop(0, NBLK // 2, block, 0)
        for j in range(hps):
            cols = slice(j * HD, (j + 1) * HD)
            dq_ref[:, cols] = dq_scrs[j][...].astype(BF16)
            dk_ref[:, cols] = dk_scrs[j][...].astype(BF16)
            dv_ref[:, cols] = dv_scrs[j][...].astype(BF16)

    buckets = jnp.asarray(np.stack([_bucket_tile(d) for _, d in PATTERNS]))
    body, more_specs, more = _behind(body, 2 + 6 * hps, after)
    return pl.pallas_call(
        body, name="attention_bwd", grid=(NH // hps,),
        in_specs=[pl.BlockSpec(memory_space=pltpu.SMEM), pl.BlockSpec((3, 128, 256), lambda g: (0, 0, 0))]
        + _head_specs(0, hps) + _head_specs(NH, hps) + _head_specs(2 * NH, hps) + 3 * _head_specs(0, hps)
        + more_specs,
        out_specs=3 * [_heads_spec(hps)] + [pl.BlockSpec((3, hps, 128, 256), lambda g: (0, g, 0, 0))],
        out_shape=[jax.ShapeDtypeStruct((S, DA), BF16)] * 3 + [jax.ShapeDtypeStruct((3, NH, 128, 256), F32)],
        scratch_shapes=[pltpu.VMEM((3, hps, 128, 256), F32)] + [pltpu.VMEM((S, HD), F32)] * (4 * hps),
        compiler_params=_params(1),
    )(rel_bias, buckets, *([proj] * (3 * hps)), *([dattn] * hps), *([attn] * hps), *([lse] * hps), *more)


def _gmlp_parts(u_ref, vb_ref, g_ref, be_ref):
    u = u_ref[...]
    u_act, tu = _gelu(u)
    vb = vb_ref[...]
    gv, tv = _gelu(vb)
    mean = jnp.mean(gv, axis=1, keepdims=True)
    cen = gv - mean
    var = jnp.mean(cen * cen, axis=1, keepdims=True)
    rstd = lax.rsqrt(var + LN_EPS)
    xhat = cen * rstd
    vn = xhat * g_ref[...] + be_ref[...]
    return u, tu, u_act, vb, tv, rstd, xhat, vn


def _gmlp_fwd(proj, ws, bsp_b, gain_v, bias_v):
    def body(u_ref, vb_ref, ws_ref, bsp_ref, g_ref, be_ref, o_ref):
        _, _, u_act, _, _, _, _, vn = _gmlp_parts(u_ref, vb_ref, g_ref, be_ref)
        row = lax.broadcasted_iota(jnp.int32, (128, 128), 0)
        col = lax.broadcasted_iota(jnp.int32, (128, 128), 1)
        causal = row >= col
        for g in range(NH):
            cols = slice(g * 128, (g + 1) * 128)
            wsg = jnp.where(causal, ws_ref[g], 0.0).astype(BF16)
            z = _dot(wsg, vn[:, cols].astype(BF16)) + bsp_ref[g]
            o_ref[:, cols] = (u_act[:, cols] * z).astype(BF16)

    return pl.pallas_call(
        body, name="gmlp_fwd", grid=(NBLK,),
        in_specs=[pl.BlockSpec((128, DB), lambda c: (c, 3)), pl.BlockSpec((128, DB), lambda c: (c, 4)),
                  pl.BlockSpec((NH, 128, 128), lambda c: (0, 0, 0)), pl.BlockSpec((NH, 128, 128), lambda c: (0, 0, 0)),
                  pl.BlockSpec((1, DB), lambda c: (0, 0)), pl.BlockSpec((1, DB), lambda c: (0, 0))],
        out_specs=pl.BlockSpec((128, DB), lambda c: (c, 0)),
        out_shape=jax.ShapeDtypeStruct((S, DB), BF16),
        compiler_params=_params(1),
    )(proj, proj, ws, bsp_b, gain_v, bias_v)


def _branch(attn, gmlp, wpa_g, wpb_g, proj):
    tn = 512

    def body(a_ref, g_ref, wa_ref, wb_ref, ga_ref, gb_ref, ya_ref, yb_ref, mg_ref):
        ya = _dot(a_ref[...], wa_ref[...])
        yb = _dot(g_ref[...], wb_ref[...])
        ya_ref[...] = ya.astype(BF16)
        yb_ref[...] = yb.astype(BF16)
        mg_ref[...] = (_sigmoid(ga_ref[...]) * ya + _sigmoid(gb_ref[...]) * yb).astype(BF16)

    out = pl.BlockSpec((S, tn), lambda j: (0, j))
    return pl.pallas_call(
        body, name="branch", grid=(D // tn,),
        in_specs=[pl.BlockSpec((S, DA), lambda j: (0, 0)), pl.BlockSpec((S, DB), lambda j: (0, 0)),
                  pl.BlockSpec((None, DA, tn), lambda j: (j, 0, 0)), pl.BlockSpec((None, DB, tn), lambda j: (j, 0, 0)),
                  pl.BlockSpec((S, tn), lambda j: (0, 5120 // tn + j)), pl.BlockSpec((S, tn), lambda j: (0, 7168 // tn + j))],
        out_specs=[out, out, out],
        out_shape=[jax.ShapeDtypeStruct((S, D), BF16)] * 3,
        compiler_params=_params(1),
    )(attn, gmlp, wpa_g, wpb_g, proj, proj)


def _out_ln1(merged, wout_g, x, gain, bias):
    tm = 256

    def body(m_ref, w_ref, x_ref, g_ref, b_ref, xh_ref, rs_ref, h_ref):
        pre = ALPHA * x_ref[...] + _dot(m_ref[...], w_ref[...])
        mean = jnp.mean(pre, axis=1, keepdims=True)
        cen = pre - mean
        var = jnp.mean(cen * cen, axis=1, keepdims=True)
        rstd = lax.rsqrt(var + LN_EPS)
        xhat = cen * rstd
        xh_ref[...] = xhat
        rs_ref[...] = jnp.broadcast_to(rstd, (tm, 128))
        h_ref[...] = (xhat * g_ref[...] + b_ref[...]).astype(BF16)

    row = pl.BlockSpec((tm, D), lambda i: (i, 0))
    vec = pl.BlockSpec((1, D), lambda i: (0, 0))
    return pl.pallas_call(
        body, name="out_ln1", grid=(S // tm,),
        in_specs=[row, pl.BlockSpec((D, D), lambda i: (0, 0)), row, vec, vec],
        out_specs=[row, pl.BlockSpec((tm, 128), lambda i: (i, 0)), row],
        out_shape=[jax.ShapeDtypeStruct((S, D), F32), jax.ShapeDtypeStruct((S, 128), F32),
                   jax.ShapeDtypeStruct((S, D), BF16)],
        compiler_params=_params(1),
    )(merged, wout_g, x, gain, bias)


def _ff1(h1b, w1_g, b1):
    tn = 512
    per = D // tn

    def body(h_ref, w_ref, b_ref, a_ref, r_ref):
        r = jnp.maximum(_dot(h_ref[...], w_ref[...]) + b_ref[...], 0.0)
        r_ref[...] = r.astype(BF16)
        a_ref[...] = (r * r).astype(BF16)

    out = pl.BlockSpec((S, tn), lambda j: (0, j))
    return pl.pallas_call(
        body, name="ff1", grid=(DFF // tn,),
        in_specs=[pl.BlockSpec((S, D), lambda j: (0, 0)),
                  pl.BlockSpec((None, D, tn), lambda j: (j // per, 0, j % per)),
                  pl.BlockSpec((1, tn), lambda j: (0, j))],
        out_specs=[out, out],
        out_shape=[jax.ShapeDtypeStruct((S, DFF), BF16)] * 2,
        compiler_params=_params(1),
    )(h1b, w1_g, b1)


def _ff2_ln2_loss(a, w2_g, xhat1, g1, b1, b2, g2, be2, target):
    tm, tk = 512, 1024
    nk = DFF // tk

    def body(a_ref, w_ref, xh_ref, g1_ref, b1_ref, b2_ref, g2_ref, be2_ref, t_ref, d_ref, db_ref, st_ref, acc):
        i, k = pl.program_id(0), pl.program_id(1)

        @pl.when(k == 0)
        def _():
            acc[...] = jnp.zeros_like(acc)

        @pl.when((i == 0) & (k == 0))
        def _():
            st_ref[...] = jnp.zeros_like(st_ref)

        acc[...] += _dot(a_ref[...], w_ref[...])

        @pl.when(k == nk - 1)
        def _():
            def rows_chunk(ci, carry):
                rows = pl.ds(pl.multiple_of(ci * 128, 128), 128)
                h1 = xh_ref[rows, :] * g1_ref[...] + b1_ref[...]
                pre = ALPHA * h1 + acc[rows, :] + b2_ref[...]
                mean = jnp.mean(pre, axis=1, keepdims=True)
                cen = pre - mean
                var = jnp.mean(cen * cen, axis=1, keepdims=True)
                rstd = lax.rsqrt(var + LN_EPS)
                xhat = cen * rstd
                y = xhat * g2_ref[...] + be2_ref[...]
                err = y - t_ref[rows, :]
                dy = err * (1.0 / D)
                g = dy * g2_ref[...]
                dpre = rstd * (g - jnp.mean(g, axis=1, keepdims=True)
                               - xhat * jnp.mean(g * xhat, axis=1, keepdims=True))
                d_ref[rows, :] = dpre
                db_ref[rows, :] = dpre.astype(BF16)
                st_ref[0:1, :] += jnp.sum(dy * xhat, axis=0, keepdims=True)
                st_ref[1:2, :] += jnp.sum(dy, axis=0, keepdims=True)
                st_ref[2:3, :] += jnp.sum(dpre, axis=0, keepdims=True)
                st_ref[3:4, :] += jnp.broadcast_to(jnp.sum(err * err).reshape(1, 1), (1, D))
                return carry

            lax.fori_loop(0, tm // 128, rows_chunk, 0)

    row = pl.BlockSpec((tm, D), lambda i, k: (i, 0))
    vec = pl.BlockSpec((1, D), lambda i, k: (0, 0))
    return pl.pallas_call(
        body, name="ff2_ln2_loss", grid=(S // tm, nk),
        in_specs=[pl.BlockSpec((tm, tk), lambda i, k: (i, k)), pl.BlockSpec((tk, D), lambda i, k: (k, 0)),
                  row, vec, vec, vec, vec, vec, row],
        out_specs=[row, row, pl.BlockSpec((8, D), lambda i, k: (0, 0))],
        out_shape=[jax.ShapeDtypeStruct((S, D), F32), jax.ShapeDtypeStruct((S, D), BF16),
                   jax.ShapeDtypeStruct((8, D), F32)],
        scratch_shapes=[pltpu.VMEM((tm, D), F32)],
        compiler_params=_params(2),
    )(a, w2_g, xhat1, g1, b1, b2, g2, be2, target)


def _grad_w(act, dout, name, ti, tj, sharded, after=None):
    m, n = act.shape[1], dout.shape[1]
    ns = n // N_CHIPS
    per = ns // tj if sharded else None

    def body(a_ref, b_ref, o_ref, at_scr):
        @pl.when(pl.program_id(1) == 0)
        def _():
            at_scr[...] = a_ref[...].T

        o_ref[...] = _dot(at_scr[...], b_ref[...])

    if sharded:
        out_spec = pl.BlockSpec((None, ti, tj), lambda i, j: (j // per, i, j % per))
        out_shape = jax.ShapeDtypeStruct((N_CHIPS, m, ns), F32)
    else:
        out_spec = pl.BlockSpec((ti, tj), lambda i, j: (i, j))
        out_shape = jax.ShapeDtypeStruct((m, n), F32)
    body, more_specs, more = _behind(body, 2, after)
    return pl.pallas_call(
        body, name=name, grid=(m // ti, n // tj),
        in_specs=[pl.BlockSpec((S, ti), lambda i, j: (0, i)), pl.BlockSpec((S, tj), lambda i, j: (0, j))] + more_specs,
        out_specs=out_spec, out_shape=out_shape,
        scratch_shapes=[pltpu.VMEM((ti, S), BF16)],
        compiler_params=_params(2),
    )(act, dout, *more)


def _d_ff1(dpre2b, w2_g, r, after=None):
    tn = 512

    def body(d_ref, w_ref, r_ref, o_ref, gb_ref):
        da = _dot_nt(d_ref[...], w_ref[...])
        dp = da * (2.0 * r_ref[...].astype(F32))
        o_ref[...] = dp.astype(BF16)
        gb_ref[...] = jnp.sum(dp, axis=0, keepdims=True)

    body, more_specs, more = _behind(body, 3, after)
    return pl.pallas_call(
        body, name="d_ff1", grid=(DFF // tn,),
        in_specs=[pl.BlockSpec((S, D), lambda j: (0, 0)), pl.BlockSpec((tn, D), lambda j: (j, 0)),
                  pl.BlockSpec((S, tn), lambda j: (0, j))] + more_specs,
        out_specs=[pl.BlockSpec((S, tn), lambda j: (0, j)), pl.BlockSpec((1, tn), lambda j: (0, j))],
        out_shape=[jax.ShapeDtypeStruct((S, DFF), BF16), jax.ShapeDtypeStruct((1, DFF), F32)],
        compiler_params=_params(1),
    )(dpre2b, w2_g, r, *more)


def _d_h1_ln1(dprea, w1_g, dpre2, xhat1, rstd1, g1, after=None):
    tm, tk = 512, 1024
    per = D // tk
    nk = DFF // tk

    def body(a_ref, w_ref, d2_ref, xh_ref, rs_ref, g_ref, d_ref, db_ref, st_ref, acc):
        i, k = pl.program_id(0), pl.program_id(1)

        @pl.when(k == 0)
        def _():
            acc[...] = jnp.zeros_like(acc)

        @pl.when((i == 0) & (k == 0))
        def _():
            st_ref[...] = jnp.zeros_like(st_ref)

        acc[...] += _dot_nt(a_ref[...], w_ref[...])

        @pl.when(k == nk - 1)
        def _():
            def rows_chunk(ci, carry):
                rows = pl.ds(pl.multiple_of(ci * 128, 128), 128)
                dh = ALPHA * d2_ref[rows, :] + acc[rows, :]
                xhat = xh_ref[rows, :]
                g = dh * g_ref[...]
                dpre = rs_ref[rows, 0:1] * (g - jnp.mean(g, axis=1, keepdims=True)
                                            - xhat * jnp.mean(g * xhat, axis=1, keepdims=True))
                d_ref[rows, :] = dpre
                db_ref[rows, :] = dpre.astype(BF16)
                st_ref[0:1, :] += jnp.sum(dh * xhat, axis=0, keepdims=True)
                st_ref[1:2, :] += jnp.sum(dh, axis=0, keepdims=True)
                return carry

            lax.fori_loop(0, tm // 128, rows_chunk, 0)

    row = pl.BlockSpec((tm, D), lambda i, k: (i, 0))
    body, more_specs, more = _behind(body, 6, after)
    return pl.pallas_call(
        body, name="d_h1_ln1", grid=(S // tm, nk),
        in_specs=[pl.BlockSpec((tm, tk), lambda i, k: (i, k)),
                  pl.BlockSpec((None, D, tk), lambda i, k: (k // per, 0, k % per)),
                  row, row, pl.BlockSpec((tm, 128), lambda i, k: (i, 0)), pl.BlockSpec((1, D), lambda i, k: (0, 0))]
        + more_specs,
        out_specs=[row, row, pl.BlockSpec((8, D), lambda i, k: (0, 0))],
        out_shape=[jax.ShapeDtypeStruct((S, D), F32), jax.ShapeDtypeStruct((S, D), BF16),
                   jax.ShapeDtypeStruct((8, D), F32)],
        scratch_shapes=[pltpu.VMEM((tm, D), F32)],
        compiler_params=_params(2),
    )(dprea, w1_g, dpre2, xhat1, rstd1, g1, *more)


def _d_merged(dpre1b, wout_g, proj, ya, yb):
    tm, tn = 512, 1024

    def body(d_ref, w_ref, ga_ref, gb_ref, ya_ref, yb_ref, dya_ref, dyb_ref, dga_ref, dgb_ref):
        dm = _dot_nt(d_ref[...], w_ref[...])
        sa = _sigmoid(ga_ref[...])
        sb = _sigmoid(gb_ref[...])
        dya_ref[...] = (dm * sa).astype(BF16)
        dyb_ref[...] = (dm * sb).astype(BF16)
        dga_ref[...] = (dm * ya_ref[...].astype(F32) * sa * (1.0 - sa)).astype(BF16)
        dgb_ref[...] = (dm * yb_ref[...].astype(F32) * sb * (1.0 - sb)).astype(BF16)

    tile = pl.BlockSpec((tm, tn), lambda i, j: (i, j))
    return pl.pallas_call(
        body, name="d_merged", grid=(S // tm, D // tn),
        in_specs=[pl.BlockSpec((tm, D), lambda i, j: (i, 0)), pl.BlockSpec((tn, D), lambda i, j: (j, 0)),
                  pl.BlockSpec((tm, tn), lambda i, j: (i, 5 + j)), pl.BlockSpec((tm, tn), lambda i, j: (i, 7 + j)),
                  tile, tile],
        out_specs=[tile] * 4,
        out_shape=[jax.ShapeDtypeStruct((S, D), BF16)] * 4,
        compiler_params=_params(2),
    )(dpre1b, wout_g, proj, proj, ya, yb)


def _d_branches(dya, dyb, wpa_g, wpb_g, after=None):
    tk = 512

    def body(da_ref, db_ref, wa_ref, wb_ref, oa_ref, ob_ref):
        @pl.when(pl.program_id(0) == 0)
        def _():
            oa_ref[...] = jnp.zeros_like(oa_ref)
            ob_ref[...] = jnp.zeros_like(ob_ref)

        oa_ref[...] += _dot_nt(da_ref[...], wa_ref[...])
        ob_ref[...] += _dot_nt(db_ref[...], wb_ref[...])

    body, more_specs, more = _behind(body, 4, after)
    return pl.pallas_call(
        body, name="d_branches", grid=(D // tk,),
        in_specs=[pl.BlockSpec((S, tk), lambda k: (0, k)), pl.BlockSpec((S, tk), lambda k: (0, k)),
                  pl.BlockSpec((None, DA, tk), lambda k: (k, 0, 0)), pl.BlockSpec((None, DB, tk), lambda k: (k, 0, 0))]
        + more_specs,
        out_specs=[pl.BlockSpec((S, DA), lambda k: (0, 0)), pl.BlockSpec((S, DB), lambda k: (0, 0))],
        out_shape=[jax.ShapeDtypeStruct((S, DA), F32), jax.ShapeDtypeStruct((S, DB), F32)],
        compiler_params=_params(1),
    )(dya, dyb, wpa_g, wpb_g, *more)


def _gmlp_bwd(proj, dgmlp, ws, ws_t, bsp_b, gain_v, bias_v):
    def body(u_ref, vb_ref, dg_ref, ws_ref, wst_ref, bsp_ref, g_ref, be_ref, duv_ref, gws_ref, gbs_ref, st_ref):
        @pl.when(pl.program_id(0) == 0)
        def _():
            gws_ref[...] = jnp.zeros_like(gws_ref)
            gbs_ref[...] = jnp.zeros_like(gbs_ref)
            st_ref[...] = jnp.zeros_like(st_ref)

        u, tu, u_act, vb, tv, rstd, xhat, vn = _gmlp_parts(u_ref, vb_ref, g_ref, be_ref)
        dg = dg_ref[...]
        dz = dg * u_act
        row = lax.broadcasted_iota(jnp.int32, (128, 128), 0)
        col = lax.broadcasted_iota(jnp.int32, (128, 128), 1)
        causal = row >= col
        causal_t = row <= col
        dvn_parts = []
        z_parts = []
        for g in range(NH):
            cols = slice(g * 128, (g + 1) * 128)
            vng = vn[:, cols].astype(BF16)
            dzg = dz[:, cols]
            dzb = dzg.astype(BF16)
            wsg = jnp.where(causal, ws_ref[g], 0.0).astype(BF16)
            wsg_t = jnp.where(causal_t, wst_ref[g], 0.0).astype(BF16)
            z_parts.append(_dot(wsg, vng) + bsp_ref[g])
            gws_ref[g] += jnp.where(causal, _dot_nt(dzb, vng), 0.0)
            gbs_ref[g] += jnp.broadcast_to(jnp.sum(dzg, axis=1, keepdims=True), (128, 128))
            dvn_parts.append(_dot(wsg_t, dzb))
        z = jnp.concatenate(z_parts, axis=1)
        dvn = jnp.concatenate(dvn_parts, axis=1)
        du = dg * z * _gelu_grad(u, tu)
        st_ref[0:1, :] += jnp.sum(dvn * xhat, axis=0, keepdims=True)
        st_ref[1:2, :] += jnp.sum(dvn, axis=0, keepdims=True)
        gg = dvn * g_ref[...]
        dgv = rstd * (gg - jnp.mean(gg, axis=1, keepdims=True) - xhat * jnp.mean(gg * xhat, axis=1, keepdims=True))
        dvb = dgv * _gelu_grad(vb, tv)
        duv_ref[:, 0:DB] = du.astype(BF16)
        duv_ref[:, DB:2 * DB] = dvb.astype(BF16)

    full3 = pl.BlockSpec((NH, 128, 128), lambda c: (0, 0, 0))
    vec = pl.BlockSpec((1, DB), lambda c: (0, 0))
    return pl.pallas_call(
        body, name="gmlp_bwd", grid=(NBLK,),
        in_specs=[pl.BlockSpec((128, DB), lambda c: (c, 3)), pl.BlockSpec((128, DB), lambda c: (c, 4)),
                  pl.BlockSpec((128, DB), lambda c: (c, 0)), full3, full3, full3, vec, vec],
        out_specs=[pl.BlockSpec((128, 2 * DB), lambda c: (c, 0)), full3, full3, pl.BlockSpec((8, DB), lambda c: (0, 0))],
        out_shape=[jax.ShapeDtypeStruct((S, 2 * DB), BF16), jax.ShapeDtypeStruct((NH, 128, 128), F32),
                   jax.ShapeDtypeStruct((NH, 128, 128), F32), jax.ShapeDtypeStruct((8, DB), F32)],
        compiler_params=_params(1),
    )(proj, proj, dgmlp, ws, ws_t, bsp_b, gain_v, bias_v)


def _rel_bias_grad(ds_sums):
    buckets = jnp.asarray(np.stack([_bucket_tile(d) for _, d in PATTERNS]))

    def body(bk_ref, ds_ref, o_ref):
        row = lax.broadcasted_iota(jnp.int32, (N_BUCKETS, 128), 0)
        lane = lax.broadcasted_iota(jnp.int32, (N_BUCKETS, 128), 1)

        def one_bucket(t, out):
            hits = [bk_ref[p] == t for p in range(3)]
            for h in range(NH):
                tot = jnp.zeros((128, 256), F32)
                for p in range(3):
                    tot = tot + jnp.where(hits[p], ds_ref[p, h], 0.0)
                out = jnp.where((row == t) & (lane == h), jnp.sum(tot), out)
            return out

        o_ref[...] = lax.fori_loop(0, N_BUCKETS, one_bucket, jnp.zeros((N_BUCKETS, 128), F32))

    return pl.pallas_call(
        body, name="rel_bias_grad",
        in_specs=[pl.BlockSpec(memory_space=pltpu.VMEM)] * 2, out_specs=pl.BlockSpec(memory_space=pltpu.VMEM),
        out_shape=jax.ShapeDtypeStruct((N_BUCKETS, 128), F32),
        compiler_params=pltpu.CompilerParams(vmem_limit_bytes=VMEM_LIMIT),
    )(buckets, ds_sums)


def _d_x(dproj, win_g, dpre1, after=None):
    tm, tk = 512, 2304
    per = 2304 // tk
    nk = DIN // tk

    def body(a_ref, w_ref, d_ref, o_ref, acc):
        k = pl.program_id(1)

        @pl.when(k == 0)
        def _():
            acc[...] = ALPHA * d_ref[...]

        acc[...] += _dot_nt(a_ref[...], w_ref[...])

        @pl.when(k == nk - 1)
        def _():
            o_ref[...] = acc[...]

    row = pl.BlockSpec((tm, D), lambda i, k: (i, 0))
    body, more_specs, more = _behind(body, 3, after)
    return pl.pallas_call(
        body, name="d_x", grid=(S // tm, nk),
        in_specs=[pl.BlockSpec((tm, tk), lambda i, k: (i, k)),
                  pl.BlockSpec((None, D, tk), lambda i, k: (k // per, 0, k % per)), row] + more_specs,
        out_specs=row, out_shape=jax.ShapeDtypeStruct((S, D), F32),
        scratch_shapes=[pltpu.VMEM((tm, D), F32)],
        compiler_params=_params(2),
    )(dproj, win_g, dpre1, *more)


def _adamw(w, g, m, v, name):
    rows, cols = w.shape
    tm = max(t for t in range(8, 257, 8) if rows % t == 0)

    def body(w_ref, g_ref, m_ref, v_ref, d_ref, nm_ref, nv_ref, go_ref):
        g = g_ref[...]
        m = ADAM_B1 * m_ref[...] + (1.0 - ADAM_B1) * g
        v = ADAM_B2 * v_ref[...] + (1.0 - ADAM_B2) * (g * g)
        m_hat = m / (1.0 - ADAM_B1 ** ADAM_STEP)
        v_hat = v / (1.0 - ADAM_B2 ** ADAM_STEP)
        d_ref[...] = -ADAM_LR * (m_hat / (jnp.sqrt(v_hat) + ADAM_EPS) + ADAM_WD * w_ref[...])
        nm_ref[...] = m
        nv_ref[...] = v
        go_ref[...] = g

    spec = pl.BlockSpec((tm, cols), lambda i: (i, 0))
    return pl.pallas_call(
        body, name=name, grid=(rows // tm,), in_specs=[spec] * 4, out_specs=[spec] * 4,
        out_shape=[jax.ShapeDtypeStruct((rows, cols), F32)] * 4, compiler_params=_params(1),
    )(w, g, m, v)


def _position():
    x, y, c = lax.axis_index("x"), lax.axis_index("y"), lax.axis_index("c")
    chips = [(1 - x, y), (x, 1 - y), (1 - x, 1 - y)]
    return x, y, c, chips


def _remote(src, dst, send_sems, recv_sems, k, to):
    return pltpu.make_async_remote_copy(src_ref=src, dst_ref=dst, send_sem=send_sems.at[k], recv_sem=recv_sems.at[k],
                                        device_id=to, device_id_type=MESH)


def _place_shard(w, name, after=None):
    rows, cols = w.shape
    tm = 256
    x, y = lax.axis_index("x"), lax.axis_index("y")

    def body(chip_ref, w_ref, o_ref):
        o_ref[...] = w_ref[...].astype(BF16)

    more_specs, more = ([ANY], [after]) if after is not None else ([], [])
    if after is not None:
        inner = body
        body = lambda chip_ref, w_ref, after_ref, o_ref: inner(chip_ref, w_ref, o_ref)
    return pl.pallas_call(
        body, name=name,
        grid_spec=pltpu.PrefetchScalarGridSpec(
            num_scalar_prefetch=1, grid=(rows // tm,),
            in_specs=[pl.BlockSpec((tm, cols), lambda i, chip: (i, 0))] + more_specs,
            out_specs=pl.BlockSpec((None, tm, cols), lambda i, chip: (chip[0], i, 0))),
        out_shape=jax.ShapeDtypeStruct((N_CHIPS, rows, cols), BF16),
        compiler_params=_params(1),
    )(jnp.reshape(2 * x + y, (1,)).astype(jnp.int32), w, *more)


def _to_bf16(x, name, after=None):
    tm = 256

    def body(x_ref, o_ref):
        o_ref[...] = x_ref[...].astype(BF16)

    spec = pl.BlockSpec((tm, x.shape[1]), lambda i: (i, 0))
    body, more_specs, more = _behind(body, 1, after)
    return pl.pallas_call(
        body, name=name, grid=(x.shape[0] // tm,), in_specs=[spec] + more_specs, out_specs=spec,
        out_shape=jax.ShapeDtypeStruct(x.shape, BF16), compiler_params=_params(1),
    )(x, *more)


HBM = pl.BlockSpec(memory_space=pltpu.HBM)
SEM = pl.BlockSpec(memory_space=pltpu.SEMAPHORE)
EFFECT = pltpu.SideEffectType.DATAFLOW_SIDE_EFFECTING


def _comm_call(name, body, bufs, sems_in, sems_out, after=None, token=False):
    nb, ns, no = len(bufs), len(sems_in), len(sems_out)
    n_in = nb + ns + (after is not None)

    def wrapped(*refs):
        body(refs[:nb], refs[nb:nb + ns], refs[n_in + nb:n_in + nb + no])
        if token:
            refs[-1][...] = jnp.zeros((8, 128), F32)

    outs = pl.pallas_call(
        wrapped, name=name,
        in_specs=[HBM] * nb + [SEM] * ns + ([ANY] if after is not None else []),
        out_specs=[HBM] * nb + [SEM] * no + ([pl.BlockSpec(memory_space=pltpu.VMEM)] if token else []),
        out_shape=[pltpu.HBM(b.shape, b.dtype) for b in bufs] + [pltpu.SemaphoreType.DMA((k,)) for k in sems_out]
        + ([jax.ShapeDtypeStruct((8, 128), F32)] if token else []),
        input_output_aliases={i: i for i in range(nb)},
        compiler_params=pltpu.CompilerParams(has_side_effects=EFFECT),
    )(*[pltpu.with_memory_space_constraint(b, pltpu.HBM) for b in bufs], *sems_in, *([after] if after is not None else []))
    return list(outs[:nb]), list(outs[nb:nb + no]), (outs[-1] if token else None)


RING_STAGES = {"ici_near": 2, "ici_far": 2, "d2d_near": 2, "d2d_far": 1}


def _ring_copies(buf, send_sems, recv_sems, k0, stage):
    x, y, c, _ = _position()
    hr = buf.shape[1] // 2
    qr = hr // 2
    half = lambda chip, h: buf.at[chip, pl.ds(h * hr, hr), :]
    quarter = lambda chip, h, q: buf.at[chip, pl.ds(h * hr + q * qr, qr), :]
    mine, x_chip, y_chip, far_chip = 2 * x + y, 2 * (1 - x) + y, 2 * x + (1 - y), 2 * (1 - x) + (1 - y)
    to_x, to_y, sibling = (1 - x, y, c), (x, 1 - y, c), (x, y, 1 - c)
    if stage == "ici_near":
        moves = [(half(mine, c), to_x, half(x_chip, c)), (half(mine, c), to_y, half(y_chip, c))]
    elif stage == "ici_far":
        moves = [(quarter(x_chip, c, 0), to_y, quarter(far_chip, c, 0)),
                 (quarter(y_chip, c, 1), to_x, quarter(far_chip, c, 1))]
    elif stage == "d2d_near":
        moves = [(half(x_chip, c), sibling, half(x_chip, 1 - c)), (half(y_chip, c), sibling, half(y_chip, 1 - c))]
    else:
        moves = [(half(far_chip, c), sibling, half(far_chip, 1 - c))]
    sends = [_remote(src, src, send_sems, recv_sems, k0 + i, to) for i, (src, to, _) in enumerate(moves)]
    arrivals = [_remote(got, got, send_sems, recv_sems, k0 + i, (x, y, c)) for i, (_, _, got) in enumerate(moves)]
    return sends, arrivals


def _ring_call(name, groups, actions, after=None):
    tags = list(dict.fromkeys(t for _, t, _ in actions))
    counts = {t: len(groups[t]["bufs"]) for t in tags}
    first = {t: sum(counts[u] for u in tags[:i]) for i, t in enumerate(tags)}
    waits = [(t, s) for v, t, s in actions if v == "wait"]
    starts = [(t, s) for v, t, s in actions if v == "start"]

    def body(bufs, sems_in, sems_out):
        for verb, t, s in actions:
            at, sems = (starts.index((t, s)), sems_out) if verb == "start" else (waits.index((t, s)), sems_in)
            for w in range(counts[t]):
                sends, arrivals = _ring_copies(bufs[first[t] + w], sems[2 * at], sems[2 * at + 1], RING_STAGES[s] * w, s)
                if verb == "start":
                    for cp in sends:
                        cp.start()
                else:
                    for cp in arrivals:
                        cp.wait_recv()
                    for cp in sends:
                        cp.wait_send()

    bufs, sems, token = _comm_call(
        name, body, [b for t in tags for b in groups[t]["bufs"]],
        [sem for t, s in waits for sem in groups[t]["sems"][s]],
        [RING_STAGES[s] * counts[t] for t, s in starts for _ in (0, 1)], after, token=True)
    for t in tags:
        groups[t]["bufs"] = bufs[first[t]:first[t] + counts[t]]
    for t, s in waits:
        del groups[t]["sems"][s]
    for i, (t, s) in enumerate(starts):
        groups[t]["sems"][s] = (sems[2 * i], sems[2 * i + 1])
    return token


def _cx_copies(src, dst, send_sems, recv_sems, k0):
    x, y, c, chips = _position()
    sends = [_remote(src.at[2 * cx + cy], dst.at[2 * x + y], send_sems, recv_sems, k0 + j, (cx, cy, c))
             for j, (cx, cy) in enumerate(chips)]
    arrivals = [_remote(dst.at[2 * cx + cy], dst.at[2 * cx + cy], send_sems, recv_sems, k0 + j, (x, y, c))
                for j, (cx, cy) in enumerate(chips)]
    return sends, arrivals


def _cx_start(name, pair_sums):
    n = len(pair_sums)
    landing = [lax.empty(p.shape, p.dtype) for p in pair_sums]

    def body(bufs, _, sems):
        for w in range(n):
            for cp in _cx_copies(bufs[w], bufs[n + w], sems[0], sems[1], 3 * w)[0]:
                cp.start()

    bufs, sems, token = _comm_call(name, body, list(pair_sums) + landing, [], [3 * n, 3 * n], token=True)
    return (bufs, sems), token


def _cx_wait(name, state, after):
    bufs, sems = state
    n = len(bufs) // 2

    def body(refs, sems_in, _):
        for w in range(n):
            sends, arrivals = _cx_copies(refs[w], refs[n + w], sems_in[0], sems_in[1], 3 * w)
            for cp in arrivals:
                cp.wait_recv()
            for cp in sends:
                cp.wait_send()

    bufs, _, _ = _comm_call(name, body, bufs, sems, [], after)
    return bufs[:n], bufs[n:]


def _px_copies(src, dst, send_sems, recv_sems, k):
    x, y, c, _ = _position()
    hr = src.shape[1] // 2
    send = _remote(src.at[:, pl.ds((1 - c) * hr, hr), :], dst, send_sems, recv_sems, k, (x, y, 1 - c))
    arrival = _remote(dst, dst, send_sems, recv_sems, k, (x, y, c))
    return send, arrival


def _px_start(name, grads):
    n = len(grads)
    landing = [lax.empty((N_CHIPS, g.shape[1] // 2, g.shape[2]), F32) for g in grads]

    def body(bufs, _, sems):
        for w in range(n):
            _px_copies(bufs[w], bufs[n + w], sems[0], sems[1], w)[0].start()

    bufs, sems, token = _comm_call(name, body, list(grads) + landing, [], [n, n], token=True)
    return (bufs, sems), token


def _px_wait(name, state, after):
    bufs, sems = state
    n = len(bufs) // 2

    def body(refs, sems_in, _):
        for w in range(n):
            send, arrival = _px_copies(refs[w], refs[n + w], sems_in[0], sems_in[1], w)
            arrival.wait_recv()
            send.wait_send()

    bufs, _, _ = _comm_call(name, body, bufs, sems, [], after)
    return bufs[:n], bufs[n:]


def _pair_sum(grad, got, name):
    _, rows, cols = grad.shape
    hr = rows // 2
    tm = min(hr, 256)
    nb = hr // tm
    c = lax.axis_index("c")

    def body(c_ref, g_ref, o_ref, out_ref):
        out_ref[...] = (g_ref[...] + o_ref[...]).astype(BF16)

    return pl.pallas_call(
        body, name=name,
        grid_spec=pltpu.PrefetchScalarGridSpec(
            num_scalar_prefetch=1, grid=(N_CHIPS, nb),
            in_specs=[pl.BlockSpec((None, tm, cols), lambda s, i, c_ref: (s, c_ref[0] * nb + i, 0)),
                      pl.BlockSpec((None, tm, cols), lambda s, i, c_ref: (s, i, 0))],
            out_specs=pl.BlockSpec((None, tm, cols), lambda s, i, c_ref: (s, i, 0))),
        out_shape=jax.ShapeDtypeStruct((N_CHIPS, hr, cols), BF16),
        compiler_params=_params(2),
    )(jnp.reshape(c, (1,)).astype(jnp.int32), grad, got)


def _chip_sum(parts, pair_sums, name):
    _, hr, cols = parts.shape
    tm = min(hr, 256)
    nb = hr // tm
    x, y, c = lax.axis_index("x"), lax.axis_index("y"), lax.axis_index("c")

    def body(pos_ref, p_ref, own_ref, o_ref):
        chip = pos_ref[0]
        own = own_ref[...].astype(F32)
        term = lambda s: jnp.where(chip == s, own, p_ref[s].astype(F32))
        o_ref[...] = ((term(0) + term(1)) + term(2)) + term(3)

    return pl.pallas_call(
        body, name=name,
        grid_spec=pltpu.PrefetchScalarGridSpec(
            num_scalar_prefetch=1, grid=(nb,),
            in_specs=[pl.BlockSpec((N_CHIPS, tm, cols), lambda i, pos: (0, i, 0)),
                      pl.BlockSpec((None, tm, cols), lambda i, pos: (pos[0], i, 0))],
            out_specs=pl.BlockSpec((tm, cols), lambda i, pos: (pos[1] * nb + i, 0))),
        out_shape=jax.ShapeDtypeStruct((2 * hr, cols), F32), compiler_params=_params(1),
    )(jnp.stack([2 * x + y, c]).astype(jnp.int32), parts, pair_sums)


def _share_halves(bufs, name):
    n = len(bufs)

    def body(*refs):
        outs = refs[n:2 * n]
        send_sems, recv_sems = refs[2 * n:]
        x, y, c, _ = _position()
        copies = []
        for w in range(n):
            hr = outs[w].shape[0] // 2
            mine = outs[w].at[pl.ds(c * hr, hr), :]
            cp = _remote(mine, mine, send_sems, recv_sems, w, (x, y, 1 - c))
            cp.start()
            copies.append(cp)
        for w in range(n):
            hr = outs[w].shape[0] // 2
            theirs = outs[w].at[pl.ds((1 - c) * hr, hr), :]
            _remote(theirs, theirs, send_sems, recv_sems, w, (x, y, c)).wait_recv()
        for cp in copies:
            cp.wait_send()

    return pl.pallas_call(
        body, name=name,
        in_specs=[ANY] * n, out_specs=[ANY] * n,
        out_shape=[jax.ShapeDtypeStruct(b.shape, b.dtype) for b in bufs],
        input_output_aliases={w: w for w in range(n)},
        scratch_shapes=[pltpu.SemaphoreType.DMA((n,)), pltpu.SemaphoreType.DMA((n,))],
    )(*bufs)


def _allreduce_small(g):
    rows = g.shape[0]

    def body(g_ref, o_ref, sib, slots, send_sems, recv_sems):
        x, y, c, chips = _position()
        me = (x, y, c)
        my_chip = 2 * x + y
        pair = _remote(g_ref, sib, send_sems, recv_sems, 0, (x, y, 1 - c))
        pair.start()
        pair.wait()
        slots[my_chip] = g_ref[...] + sib[...]
        sent = []
        for j, (cx, cy) in enumerate(chips):
            cp = _remote(slots.at[my_chip], slots.at[my_chip], send_sems, recv_sems, 1 + j, (cx, cy, c))
            cp.start()
            sent.append(cp)
        for j, (cx, cy) in enumerate(chips):
            got = slots.at[2 * cx + cy]
            _remote(got, got, send_sems, recv_sems, 1 + j, me).wait_recv()
        for cp in sent:
            cp.wait_send()
        o_ref[...] = ((slots[0] + slots[1]) + slots[2]) + slots[3]

    vm = pl.BlockSpec(memory_space=pltpu.VMEM)
    return pl.pallas_call(
        body, name="allreduce_small",
        in_specs=[vm], out_specs=vm, out_shape=jax.ShapeDtypeStruct((rows, 128), F32),
        scratch_shapes=[pltpu.VMEM((rows, 128), F32), pltpu.VMEM((N_CHIPS, rows, 128), F32),
                        pltpu.SemaphoreType.DMA((4,)), pltpu.SemaphoreType.DMA((4,))],
        compiler_params=pltpu.CompilerParams(vmem_limit_bytes=VMEM_LIMIT),
    )(g)


_SMALL =("rel_bias", "ln_v_gain", "ln_v_bias", "w_spatial", "b_spatial", "ln1_gain", "ln1_bias",
          "b_ff1", "b_ff2", "ln2_gain", "ln2_bias")
_SMALL_ROWS = 1200
_LOSS_AT = (152832 // 128, 0)


def _pack_small(parts):
    flat = jnp.concatenate([parts[k].reshape(-1).astype(F32) for k in _SMALL])
    flat = jnp.pad(flat, (0, _SMALL_ROWS * 128 - flat.shape[0]))
    return flat.reshape(_SMALL_ROWS, 128)


def _unpack_small(packed, like):
    flat = packed.reshape(-1)
    out, at = {}, 0
    for k in _SMALL:
        n = math.prod(like[k].shape)
        out[k] = flat[at:at + n].reshape(like[k].shape)
        at += n
    return out


def kernel(x, w_in, rel_bias, ln_v_gain, ln_v_bias, w_spatial, b_spatial, w_proj_a, w_proj_b, w_out, ln1_gain, ln1_bias, w_ff1, b_ff1, w_ff2, b_ff2, ln2_gain, ln2_bias, loss_target, m_w_in, m_rel_bias, m_ln_v_gain, m_ln_v_bias, m_w_spatial, m_b_spatial, m_w_proj_a, m_w_proj_b, m_w_out, m_ln1_gain, m_ln1_bias, m_w_ff1, m_b_ff1, m_w_ff2, m_b_ff2, m_ln2_gain, m_ln2_bias, v_w_in, v_rel_bias, v_ln_v_gain, v_ln_v_bias, v_w_spatial, v_b_spatial, v_w_proj_a, v_w_proj_b, v_w_out, v_ln1_gain, v_ln1_bias, v_w_ff1, v_b_ff1, v_w_ff2, v_b_ff2, v_ln2_gain, v_ln2_bias):
    args = dict(locals())
    big = ("w_in", "w_proj_a", "w_proj_b", "w_out", "w_ff1", "w_ff2")
    weights = ("w_in", "rel_bias", "ln_v_gain", "ln_v_bias", "w_spatial", "b_spatial", "w_proj_a", "w_proj_b", "w_out",
               "ln1_gain", "ln1_bias", "w_ff1", "b_ff1", "w_ff2", "b_ff2", "ln2_gain", "ln2_bias")

    xs = x[0]
    target = loss_target[0]

    ring = {"a": {"bufs": [_place_shard(w_in[0], "place_w_in")], "sems": {}}}
    tok = _ring_call("allgather_a_near", ring, [("start", "a", "ici_near")])
    placed = [_place_shard(args[k][0], f"place_{k}", after=tok) for k in big[1:]]
    for tag, bufs in (("b", placed[0:3]), ("c", placed[3:4]), ("d", placed[4:5])):
        ring[tag] = {"bufs": bufs, "sems": {}}
    xb = _to_bf16(xs, "x_to_bf16", after=placed[4])

    mx, my = lax.axis_index("x"), lax.axis_index("y")
    own = jnp.reshape(2 * mx + my, (1,)).astype(jnp.int32)
    near = jnp.stack([2 * (1 - mx) + my, 2 * mx + (1 - my)]).astype(jnp.int32)
    far = jnp.reshape(2 * (1 - mx) + (1 - my), (1,)).astype(jnp.int32)
    proj = _proj(xb, ring["a"]["bufs"][0], own, "proj_own")
    _ring_call("allgather_a_far", ring, [("wait", "a", "ici_near"), ("start", "a", "ici_far"), ("start", "a", "d2d_near"),
                                         ("start", "b", "ici_near"), ("start", "c", "ici_near")], after=proj)
    _ring_call("allgather_a_near_done", ring, [("wait", "a", "d2d_near")])
    proj = _proj(xb, ring["a"]["bufs"][0], near, "proj_near", into=proj)
    _ring_call("allgather_a_last", ring, [("wait", "a", "ici_far"), ("start", "a", "d2d_far")], after=proj)
    _ring_call("allgather_a_done", ring, [("wait", "a", "d2d_far")])
    (win_g,) = ring["a"]["bufs"]
    proj = _proj(xb, win_g, far, "proj_far", into=proj)
    _ring_call("allgather_b_far", ring, [("wait", "b", "ici_near"), ("start", "b", "ici_far"), ("start", "b", "d2d_near")],
               after=proj)
    ws = w_spatial[0]
    ws_t = jnp.transpose(ws, (0, 2, 1))
    bsp_b = jnp.broadcast_to(b_spatial[0][:, :, None], (NH, 128, 128))
    gmlp = _gmlp_fwd(proj, ws, bsp_b, ln_v_gain, ln_v_bias)
    attn, lse = _attention_fwd(proj, rel_bias)
    _ring_call("allgather_b_last_c_far", ring,
               [("wait", "b", "ici_far"), ("start", "b", "d2d_far"),
                ("wait", "c", "ici_near"), ("start", "c", "ici_far"), ("start", "c", "d2d_near"),
                ("start", "d", "ici_near")], after=attn)
    _ring_call("allgather_b_done", ring, [("wait", "b", "d2d_near"), ("wait", "b", "d2d_far")])
    wpa_g, wpb_g, wout_g = ring["b"]["bufs"]
    wout_full = wout_g.reshape(D, D)
    ya, yb, merged = _branch(attn, gmlp, wpa_g, wpb_g, proj)
    xhat1, rstd1, h1b = _out_ln1(merged, wout_full, xs, ln1_gain, ln1_bias)
    _ring_call("allgather_c_last_d_far", ring,
               [("wait", "c", "ici_far"), ("start", "c", "d2d_far"),
                ("wait", "d", "ici_near"), ("start", "d", "ici_far"), ("start", "d", "d2d_near")], after=h1b)
    _ring_call("allgather_c_done", ring, [("wait", "c", "d2d_near"), ("wait", "c", "d2d_far")])
    (w1_g,) = ring["c"]["bufs"]
    a, r = _ff1(h1b, w1_g, b_ff1)
    _ring_call("allgather_d_last", ring, [("wait", "d", "ici_far"), ("start", "d", "d2d_far")], after=a)
    _ring_call("allgather_d_done", ring, [("wait", "d", "d2d_near"), ("wait", "d", "d2d_far")])
    (w2_g,) = ring["d"]["bufs"]
    w2_full = w2_g.reshape(DFF, D)
    dpre2, dpre2b, st2 = _ff2_ln2_loss(a, w2_full, xhat1, ln1_gain, ln1_bias, b_ff2, ln2_gain, ln2_bias, target)

    def pair_and_chip(tag, state, after):
        local, from_sibling = _px_wait(f"pair_exchange_wait_{tag}", state, after)
        pair_sums = [_pair_sum(g, o, f"pair_sum_{tag}_{i}") for i, (g, o) in enumerate(zip(local, from_sibling))]
        return _cx_start(f"chip_exchange_start_{tag}", pair_sums)

    g_w2 = _grad_w(a, dpre2b, "grad_w_ff2", 512, 2048, False)
    px, tok = _px_start("pair_exchange_start_w_ff2", [g_w2.reshape(N_CHIPS, DFF // N_CHIPS, D)])
    dprea, g_b1 = _d_ff1(dpre2b, w2_full, r, after=tok)
    cx_w2, tok = pair_and_chip("w_ff2", px, dprea)
    g_w1 = _grad_w(h1b, dprea, "grad_w_ff1", 512, 2048, True, after=tok)
    px, tok = _px_start("pair_exchange_start_w_ff1", [g_w1])
    dpre1, dpre1b, st1 = _d_h1_ln1(dprea, w1_g, dpre2, xhat1, rstd1, ln1_gain, after=tok)
    cx_w1, tok = pair_and_chip("w_ff1", px, dpre1b)
    g_wout = _grad_w(merged, dpre1b, "grad_w_out", 512, 2048, False, after=tok)
    dya, dyb, dga, dgb = _d_merged(dpre1b, wout_full, proj, ya, yb)
    g_wpa = _grad_w(attn, dya, "grad_w_proj_a", 1024, 512, True)
    g_wpb = _grad_w(gmlp, dyb, "grad_w_proj_b", 1024, 512, True)
    px, tok = _px_start("pair_exchange_start_b", [g_wpa, g_wpb, g_wout.reshape(N_CHIPS, D // N_CHIPS, D)])
    dattn, dgmlp = _d_branches(dya, dyb, wpa_g, wpb_g, after=tok)
    duv, g_ws, g_bs, stv = _gmlp_bwd(proj, dgmlp, ws, ws_t, bsp_b, ln_v_gain, ln_v_bias)
    cx_b, tok = pair_and_chip("b", px, duv)
    dq, dk, dv, ds_sums = _attention_bwd(proj, dattn, attn, lse, rel_bias, after=tok)
    g_rb = _rel_bias_grad(ds_sums)[:, :NH]

    small_g = dict(rel_bias=g_rb, ln_v_gain=stv[0], ln_v_bias=stv[1], w_spatial=g_ws, b_spatial=g_bs[:, :, 0],
                   ln1_gain=st1[0], ln1_bias=st1[1], b_ff1=g_b1, b_ff2=st2[2], ln2_gain=st2[0], ln2_bias=st2[1])
    gs = _allreduce_small(_pack_small(small_g).at[_LOSS_AT].set(st2[3, 0]))
    ds_, ms_, vs_, _ = _adamw(_pack_small({k: args[k] for k in _SMALL}), gs,
                           _pack_small({k: args["m_" + k] for k in _SMALL}),
                           _pack_small({k: args["v_" + k] for k in _SMALL}), "adamw_small")
    like = {k: args[k] for k in _SMALL}
    grads, deltas, new_m, new_v = (_unpack_small(t, like) for t in (gs, ds_, ms_, vs_))

    dproj = jnp.concatenate([dq, dk, dv, duv, dga, dgb], axis=1)
    g_win = _grad_w(xb, dproj, "grad_w_in", 512, 2304, True, after=gs)
    px, tok = _px_start("pair_exchange_start_w_in", [g_win])
    grad_x = _d_x(dproj, win_g, dpre1, after=tok)
    cx_in, tok = pair_and_chip("w_in", px, grad_x)

    def reduce_finish(tag, state, names, after):
        pair_sums, from_chips = _cx_wait(f"chip_exchange_wait_{tag}", state, after)
        halves = [_chip_sum(p, own, f"chip_sum_{k}") for p, own, k in zip(from_chips, pair_sums, names)]
        last = None
        for k, g in zip(names, _share_halves(halves, f"share_halves_{tag}")):
            d_, m_, v_, g_ = _adamw(args[k][0], g, args["m_" + k][0], args["v_" + k][0], f"adamw_{k}")
            grads[k], deltas[k], new_m[k], new_v[k] = g_[None], d_[None], m_[None], v_[None]
            last = d_
        return last

    done = reduce_finish("w_ff2", cx_w2, ["w_ff2"], tok)
    done = reduce_finish("w_ff1", cx_w1, ["w_ff1"], done)
    done = reduce_finish("b", cx_b, ["w_proj_a", "w_proj_b", "w_out"], done)
    reduce_finish("w_in", cx_in, ["w_in"], done)

    loss = gs[_LOSS_AT] * (0.5 / D)
    return (loss, grad_x[None], *[grads[k] for k in weights], *[deltas[k] for k in weights],
            *[new_m[k] for k in weights], *[new_v[k] for k in weights])
```

```python
import math

import numpy as np
import jax
import jax.numpy as jnp
from jax import lax
from jax.experimental import pallas as pl
from jax.experimental.pallas import tpu as pltpu

F32 = jnp.float32
BF16 = jnp.bfloat16

S = 2048
D = 2048
DA = 1024
DB = 1024
DFF = 8192
DIN = 9216
NH = 8
HD = 128
NBLK = 16
PATTERNS = ((128, 1), (512, 4), (2048, 16))
N_BUCKETS = 32
MAX_DISTANCE = 2048
ALPHA = 2.0 ** 0.25
LN_EPS = 1e-5
NEG_INF = -1e30
SCALE = HD ** -0.5
N_CHIPS = 4

ADAM_LR = 0.001
ADAM_B1 = 0.9
ADAM_B2 = 0.999
ADAM_EPS = 1e-08
ADAM_WD = 0.01
ADAM_STEP = 10

VMEM_LIMIT = 56 * 1024 * 1024
MESH = pl.DeviceIdType.MESH
ANY = pl.BlockSpec(memory_space=pl.ANY)


def _params(n_axes, vmem=VMEM_LIMIT):
    return pltpu.CompilerParams(dimension_semantics=("arbitrary",) * n_axes, vmem_limit_bytes=vmem)


def _bucket_tile(dilation):
    qi = np.arange(128)[:, None]
    kj = np.arange(256)[None, :]
    n = np.clip(128 + qi - kj, 0, 128) * dilation
    max_exact = N_BUCKETS // 2
    nf = np.maximum(n, 1).astype(np.float32)
    large = max_exact + (np.log(nf / np.float32(max_exact)) / np.float32(math.log(MAX_DISTANCE / max_exact))
                         * np.float32(N_BUCKETS - max_exact)).astype(np.int32)
    large = np.minimum(large, N_BUCKETS - 1)
    return np.where(n < max_exact, n, large).astype(np.int32)


def _gelu(x):
    c = math.sqrt(2.0 / math.pi)
    t = jnp.tanh(c * (x + 0.044715 * x * x * x))
    return 0.5 * x * (1.0 + t), t


def _gelu_grad(x, t):
    c = math.sqrt(2.0 / math.pi)
    return 0.5 * (1.0 + t) + 0.5 * x * (1.0 - t * t) * c * (1.0 + 3.0 * 0.044715 * x * x)


def _sigmoid(x):
    return 1.0 / (1.0 + jnp.exp(-x))


def _dot(a, b):
    return jnp.dot(a, b, preferred_element_type=F32)


def _behind(body, n_in, after):
    if after is None:
        return body, [], []
    return (lambda *refs: body(*refs[:n_in], *refs[n_in + 1:])), [ANY], [after]


def _dot_nt(a, b):
    return lax.dot_general(a, b, (((1,), (1,)), ((), ())), preferred_element_type=F32)


def _proj(xb, win_g, shards, name, into=None):
    tn = 768
    per = 2304 // tn

    def body(shards_ref, x_ref, w_ref, *rest):
        rest[-1][...] = _dot(x_ref[...], w_ref[...])

    in_specs = [pl.BlockSpec((S, D), lambda j, sh: (0, 0)),
                pl.BlockSpec((None, D, tn), lambda j, sh: (sh[j // per], 0, j % per))]
    return pl.pallas_call(
        body, name=name,
        grid_spec=pltpu.PrefetchScalarGridSpec(
            num_scalar_prefetch=1, grid=(shards.shape[0] * per,),
            in_specs=in_specs + ([ANY] if into is not None else []),
            out_specs=pl.BlockSpec((S, tn), lambda j, sh: (0, sh[j // per] * per + j % per))),
        out_shape=jax.ShapeDtypeStruct((S, DIN), F32),
        input_output_aliases={3: 0} if into is not None else {},
        compiler_params=_params(1),
    )(shards, xb, win_g, *([into] if into is not None else []))


FWD_HEADS_PER_STEP = 4
BWD_HEADS_PER_STEP = 2


def _head_bias_tiles(rb_ref, bk_ref, bias_scr, first_head, hps):
    qi = lax.broadcasted_iota(jnp.int32, (128, 256), 0)
    kj = lax.broadcasted_iota(jnp.int32, (128, 256), 1)
    steps = 128 + qi - kj
    band = (steps >= 0) & (steps <= 128)
    bias_scr[...] = jnp.zeros_like(bias_scr)
    for p in range(len(PATTERNS)):
        bucket = bk_ref[p]

        def one_bucket(t, carry):
            hit = bucket == t
            for j in range(hps):
                bias_scr[p, j] = jnp.where(hit, rb_ref[t, first_head + j], bias_scr[p, j])
            return carry

        lax.fori_loop(0, N_BUCKETS, one_bucket, 0)
        for j in range(hps):
            bias_scr[p, j] = jnp.where(band, bias_scr[p, j], NEG_INF)


def _block_rows(b, dilation):
    nblk = NBLK // dilation
    r, n = b // nblk, b % nblk
    start = r + n * (128 * dilation)
    prev_start = jnp.maximum(start - 128 * dilation, r)
    if dilation == 1:
        return pl.ds(pl.multiple_of(start, 128), 128), pl.ds(pl.multiple_of(prev_start, 128), 128), n > 0
    return pl.ds(start, 128, stride=dilation), pl.ds(prev_start, 128, stride=dilation), n > 0


def _head_specs(first, hps):
    return [pl.BlockSpec((S, HD), lambda g, j=j: (0, first + g * hps + j)) for j in range(hps)]


def _heads_spec(hps):
    return pl.BlockSpec((S, hps * HD), lambda g: (0, g))


def _attention_fwd(proj, rel_bias):
    hps = FWD_HEADS_PER_STEP
    buckets = jnp.asarray(np.stack([_bucket_tile(d) for _, d in PATTERNS]))

    def body(rb_ref, bk_ref, *refs):
        q_refs, k_refs, v_refs = (refs[i * hps:(i + 1) * hps] for i in range(3))
        o_ref, lse_ref, bias_scr = refs[3 * hps:3 * hps + 3]
        acc_scrs, m_scrs, l_scrs = (refs[3 * hps + 3 + i * hps:3 * hps + 3 + (i + 1) * hps] for i in range(3))
        _head_bias_tiles(rb_ref, bk_ref, bias_scr, pl.program_id(0) * hps, hps)
        kj = lax.broadcasted_iota(jnp.int32, (128, 256), 1)
        for p, (_, d) in enumerate(PATTERNS):
            prev_blocks = NBLK // d > 1

            def block(b, carry):
                units = [(j,) + _block_rows(blk, d) for blk in (b, b + NBLK // 2) for j in range(hps)]
                scores = []
                for j, rows, prows, _ in units:
                    q = q_refs[j][rows, :].astype(BF16)
                    cur = _dot_nt(q, k_refs[j][rows, :].astype(BF16))
                    if prev_blocks:
                        cur = jnp.concatenate([_dot_nt(q, k_refs[j][prows, :].astype(BF16)), cur], axis=1)
                    scores.append(cur)
                soft = []
                for u, (j, _, _, has_prev) in enumerate(units):
                    if prev_blocks:
                        s = jnp.where((kj >= 128) | has_prev, scores[u] * SCALE + bias_scr[p, j], NEG_INF)
                    else:
                        s = scores[u] * SCALE + bias_scr[p, j, :, 128:256]
                    m = jnp.max(s, axis=1, keepdims=True)
                    e = jnp.exp(s - m)
                    soft.append((m, jnp.sum(e, axis=1, keepdims=True), e.astype(BF16)))
                outs = []
                for u, (j, rows, prows, _) in enumerate(units):
                    e = soft[u][2]
                    if prev_blocks:
                        outs.append(_dot(e[:, :128], v_refs[j][prows, :].astype(BF16))
                                    + _dot(e[:, 128:], v_refs[j][rows, :].astype(BF16)))
                    else:
                        outs.append(_dot(e, v_refs[j][rows, :].astype(BF16)))
                for u, (j, rows, _, _) in enumerate(units):
                    acc_scr, m_scr, l_scr = acc_scrs[j], m_scrs[j], l_scrs[j]
                    (m, den, _), o = soft[u], outs[u]
                    if p == 0:
                        acc_scr[rows, :] = o
                        m_scr[rows, :] = jnp.broadcast_to(m, (128, HD))
                        l_scr[rows, :] = jnp.broadcast_to(den, (128, HD))
                    else:
                        m_old = m_scr[rows, :]
                        m_new = jnp.maximum(m_old, m)
                        w_old, w_new = jnp.exp(m_old - m_new), jnp.exp(m - m_new)
                        acc_scr[rows, :] = acc_scr[rows, :] * w_old + o * w_new
                        l_scr[rows, :] = l_scr[rows, :] * w_old + den * w_new
                        m_scr[rows, :] = m_new
                return carry

            lax.fori_loop(0, NBLK // 2, block, 0)
        for j in range(hps):
            cols = slice(j * HD, (j + 1) * HD)
            den = l_scrs[j][...]
            o_ref[:, cols] = (acc_scrs[j][...] / den).astype(BF16)
            lse_ref[:, cols] = m_scrs[j][...] + jnp.log(den)

    return pl.pallas_call(
        body, name="attention_fwd", grid=(NH // hps,),
        in_specs=[pl.BlockSpec(memory_space=pltpu.SMEM), pl.BlockSpec((3, 128, 256), lambda g: (0, 0, 0))]
        + _head_specs(0, hps) + _head_specs(NH, hps) + _head_specs(2 * NH, hps),
        out_specs=[_heads_spec(hps), _heads_spec(hps)],
        out_shape=[jax.ShapeDtypeStruct((S, DA), BF16), jax.ShapeDtypeStruct((S, DA), F32)],
        scratch_shapes=[pltpu.VMEM((3, hps, 128, 256), F32)] + [pltpu.VMEM((S, HD), F32)] * (3 * hps),
        compiler_params=_params(1),
    )(rel_bias, buckets, *([proj] * (3 * hps)))


def _attention_bwd(proj, dattn, attn, lse, rel_bias, after=None):
    hps = BWD_HEADS_PER_STEP

    def body(rb_ref, bk_ref, *refs):
        q_refs, k_refs, v_refs, do_refs, o_refs, lse_refs = (refs[i * hps:(i + 1) * hps] for i in range(6))
        dq_ref, dk_ref, dv_ref, ds_ref, bias_scr = refs[6 * hps:6 * hps + 5]
        dl_scrs, dq_scrs, dk_scrs, dv_scrs = (refs[6 * hps + 5 + i * hps:6 * hps + 5 + (i + 1) * hps] for i in range(4))
        _head_bias_tiles(rb_ref, bk_ref, bias_scr, pl.program_id(0) * hps, hps)
        ds_ref[...] = jnp.zeros_like(ds_ref)
        for j in range(hps):
            dq_scrs[j][...] = jnp.zeros((S, HD), F32)
            dk_scrs[j][...] = jnp.zeros((S, HD), F32)
            dv_scrs[j][...] = jnp.zeros((S, HD), F32)
            prod = do_refs[j][...] * o_refs[j][...].astype(F32)
            dl_scrs[j][...] = jnp.broadcast_to(jnp.sum(prod, axis=1, keepdims=True), (S, HD))
        for p, (_, d) in enumerate(PATTERNS):
            prev_blocks = NBLK // d > 1

            def block(b, carry):
                units = [(j,) + _block_rows(blk, d) for blk in (b, b + NBLK // 2) for j in range(hps)]
                ops, raw = [], []
                for j, rows, prows, _ in units:
                    q, do = q_refs[j][rows, :].astype(BF16), do_refs[j][rows, :].astype(BF16)
                    kc, vc = k_refs[j][rows, :].astype(BF16), v_refs[j][rows, :].astype(BF16)
                    if prev_blocks:
                        kp, vp = k_refs[j][prows, :].astype(BF16), v_refs[j][prows, :].astype(BF16)
                        ops.append((q, do, kc, kp))
                        raw.append((_dot_nt(q, kc), _dot_nt(do, vc), _dot_nt(q, kp), _dot_nt(do, vp)))
                    else:
                        ops.append((q, do, kc))
                        raw.append((_dot_nt(q, kc), _dot_nt(do, vc)))
                probs = []
                for u, (j, rows, _, has_prev) in enumerate(units):
                    lse_b, dl_b = lse_refs[j][rows, :], dl_scrs[j][rows, :]
                    p_c = jnp.exp(raw[u][0] * SCALE + bias_scr[p, j, :, 128:256] - lse_b)
                    ds_c = p_c * (raw[u][1] - dl_b)
                    ds_ref[p, j, :, 128:256] += ds_c
                    if prev_blocks:
                        p_p = jnp.where(has_prev, jnp.exp(raw[u][2] * SCALE + bias_scr[p, j, :, 0:128] - lse_b), 0.0)
                        ds_p = p_p * (raw[u][3] - dl_b)
                        ds_ref[p, j, :, 0:128] += ds_p
                        probs.append((p_c, ds_c, p_p, ds_p))
                    else:
                        probs.append((p_c, ds_c))
                grads = []
                for u in range(len(units)):
                    q, do, kc = ops[u][:3]
                    p_c, ds_c = probs[u][:2]
                    dq = _dot(ds_c.astype(BF16), kc)
                    cur = (_dot(ds_c.T.astype(BF16), q) * SCALE, _dot(p_c.T.astype(BF16), do))
                    if prev_blocks:
                        p_p, ds_p = probs[u][2:]
                        dq = dq + _dot(ds_p.astype(BF16), ops[u][3])
                        cur = cur + (_dot(ds_p.T.astype(BF16), q) * SCALE, _dot(p_p.T.astype(BF16), do))
                    grads.append((dq * SCALE,) + cur)
                for u, (j, rows, prows, _) in enumerate(units):
                    dq_scrs[j][rows, :] += grads[u][0]
                    dk_scrs[j][rows, :] += grads[u][1]
                    dv_scrs[j][rows, :] += grads[u][2]
                    if prev_blocks:
                        dk_scrs[j][prows, :] += grads[u][3]
                        dv_scrs[j][prows, :] += grads[u][4]
                return carry

            lax.fori_loop(0, NBLK // 2, block, 0)
        for j in range(hps):
            cols = slice(j * HD, (j + 1) * HD)
            dq_ref[:, cols] = dq_scrs[j][...].astype(BF16)
            dk_ref[:, cols] = dk_scrs[j][...].astype(BF16)
            dv_ref[:, cols] = dv_scrs[j][...].astype(BF16)

    buckets = jnp.asarray(np.stack([_bucket_tile(d) for _, d in PATTERNS]))
    body, more_specs, more = _behind(body, 2 + 6 * hps, after)
    return pl.pallas_call(
        body, name="attention_bwd", grid=(NH // hps,),
        in_specs=[pl.BlockSpec(memory_space=pltpu.SMEM), pl.BlockSpec((3, 128, 256), lambda g: (0, 0, 0))]
        + _head_specs(0, hps) + _head_specs(NH, hps) + _head_specs(2 * NH, hps) + 3 * _head_specs(0, hps)
        + more_specs,
        out_specs=3 * [_heads_spec(hps)] + [pl.BlockSpec((3, hps, 128, 256), lambda g: (0, g, 0, 0))],
        out_shape=[jax.ShapeDtypeStruct((S, DA), BF16)] * 3 + [jax.ShapeDtypeStruct((3, NH, 128, 256), F32)],
        scratch_shapes=[pltpu.VMEM((3, hps, 128, 256), F32)] + [pltpu.VMEM((S, HD), F32)] * (4 * hps),
        compiler_params=_params(1),
    )(rel_bias, buckets, *([proj] * (3 * hps)), *([dattn] * hps), *([attn] * hps), *([lse] * hps), *more)


def _gmlp_parts(u_ref, vb_ref, g_ref, be_ref):
    u = u_ref[...]
    u_act, tu = _gelu(u)
    vb = vb_ref[...]
    gv, tv = _gelu(vb)
    mean = jnp.mean(gv, axis=1, keepdims=True)
    cen = gv - mean
    var = jnp.mean(cen * cen, axis=1, keepdims=True)
    rstd = lax.rsqrt(var + LN_EPS)
    xhat = cen * rstd
    vn = xhat * g_ref[...] + be_ref[...]
    return u, tu, u_act, vb, tv, rstd, xhat, vn


def _gmlp_fwd(proj, ws, bsp_b, gain_v, bias_v):
    def body(u_ref, vb_ref, ws_ref, bsp_ref, g_ref, be_ref, o_ref):
        _, _, u_act, _, _, _, _, vn = _gmlp_parts(u_ref, vb_ref, g_ref, be_ref)
        row = lax.broadcasted_iota(jnp.int32, (128, 128), 0)
        col = lax.broadcasted_iota(jnp.int32, (128, 128), 1)
        causal = row >= col
        for g in range(NH):
            cols = slice(g * 128, (g + 1) * 128)
            wsg = jnp.where(causal, ws_ref[g], 0.0).astype(BF16)
            z = _dot(wsg, vn[:, cols].astype(BF16)) + bsp_ref[g]
            o_ref[:, cols] = (u_act[:, cols] * z).astype(BF16)

    return pl.pallas_call(
        body, name="gmlp_fwd", grid=(NBLK,),
        in_specs=[pl.BlockSpec((128, DB), lambda c: (c, 3)), pl.BlockSpec((128, DB), lambda c: (c, 4)),
                  pl.BlockSpec((NH, 128, 128), lambda c: (0, 0, 0)), pl.BlockSpec((NH, 128, 128), lambda c: (0, 0, 0)),
                  pl.BlockSpec((1, DB), lambda c: (0, 0)), pl.BlockSpec((1, DB), lambda c: (0, 0))],
        out_specs=pl.BlockSpec((128, DB), lambda c: (c, 0)),
        out_shape=jax.ShapeDtypeStruct((S, DB), BF16),
        compiler_params=_params(1),
    )(proj, proj, ws, bsp_b, gain_v, bias_v)


def _branch(attn, gmlp, wpa_g, wpb_g, proj):
    tn = 512

    def body(a_ref, g_ref, wa_ref, wb_ref, ga_ref, gb_ref, ya_ref, yb_ref, mg_ref):
        ya = _dot(a_ref[...], wa_ref[...])
        yb = _dot(g_ref[...], wb_ref[...])
        ya_ref[...] = ya.astype(BF16)
        yb_ref[...] = yb.astype(BF16)
        mg_ref[...] = (_sigmoid(ga_ref[...]) * ya + _sigmoid(gb_ref[...]) * yb).astype(BF16)

    out = pl.BlockSpec((S, tn), lambda j: (0, j))
    return pl.pallas_call(
        body, name="branch", grid=(D // tn,),
        in_specs=[pl.BlockSpec((S, DA), lambda j: (0, 0)), pl.BlockSpec((S, DB), lambda j: (0, 0)),
                  pl.BlockSpec((None, DA, tn), lambda j: (j, 0, 0)), pl.BlockSpec((None, DB, tn), lambda j: (j, 0, 0)),
                  pl.BlockSpec((S, tn), lambda j: (0, 5120 // tn + j)), pl.BlockSpec((S, tn), lambda j: (0, 7168 // tn + j))],
        out_specs=[out, out, out],
        out_shape=[jax.ShapeDtypeStruct((S, D), BF16)] * 3,
        compiler_params=_params(1),
    )(attn, gmlp, wpa_g, wpb_g, proj, proj)


def _out_ln1(merged, wout_g, x, gain, bias):
    tm = 256

    def body(m_ref, w_ref, x_ref, g_ref, b_ref, xh_ref, rs_ref, h_ref):
        pre = ALPHA * x_ref[...] + _dot(m_ref[...], w_ref[...])
        mean = jnp.mean(pre, axis=1, keepdims=True)
        cen = pre - mean
        var = jnp.mean(cen * cen, axis=1, keepdims=True)
        rstd = lax.rsqrt(var + LN_EPS)
        xhat = cen * rstd
        xh_ref[...] = xhat
        rs_ref[...] = jnp.broadcast_to(rstd, (tm, 128))
        h_ref[...] = (xhat * g_ref[...] + b_ref[...]).astype(BF16)

    row = pl.BlockSpec((tm, D), lambda i: (i, 0))
    vec = pl.BlockSpec((1, D), lambda i: (0, 0))
    return pl.pallas_call(
        body, name="out_ln1", grid=(S // tm,),
        in_specs=[row, pl.BlockSpec((D, D), lambda i: (0, 0)), row, vec, vec],
        out_specs=[row, pl.BlockSpec((tm, 128), lambda i: (i, 0)), row],
        out_shape=[jax.ShapeDtypeStruct((S, D), F32), jax.ShapeDtypeStruct((S, 128), F32),
                   jax.ShapeDtypeStruct((S, D), BF16)],
        compiler_params=_params(1),
    )(merged, wout_g, x, gain, bias)


def _ff1(h1b, w1_g, b1):
    tn = 512
    per = D // tn

    def body(h_ref, w_ref, b_ref, a_ref, r_ref):
        r = jnp.maximum(_dot(h_ref[...], w_ref[...]) + b_ref[...], 0.0)
        r_ref[...] = r.astype(BF16)
        a_ref[...] = (r * r).astype(BF16)

    out = pl.BlockSpec((S, tn), lambda j: (0, j))
    return pl.pallas_call(
        body, name="ff1", grid=(DFF // tn,),
        in_specs=[pl.BlockSpec((S, D), lambda j: (0, 0)),
                  pl.BlockSpec((None, D, tn), lambda j: (j // per, 0, j % per)),
                  pl.BlockSpec((1, tn), lambda j: (0, j))],
        out_specs=[out, out],
        out_shape=[jax.ShapeDtypeStruct((S, DFF), BF16)] * 2,
        compiler_params=_params(1),
    )(h1b, w1_g, b1)


def _ff2_ln2_loss(a, w2_g, xhat1, g1, b1, b2, g2, be2, target):
    tm, tk = 512, 1024
    nk = DFF // tk

    def body(a_ref, w_ref, xh_ref, g1_ref, b1_ref, b2_ref, g2_ref, be2_ref, t_ref, d_ref, db_ref, st_ref, acc):
        i, k = pl.program_id(0), pl.program_id(1)

        @pl.when(k == 0)
        def _():
            acc[...] = jnp.zeros_like(acc)

        @pl.when((i == 0) & (k == 0))
        def _():
            st_ref[...] = jnp.zeros_like(st_ref)

        acc[...] += _dot(a_ref[...], w_ref[...])

        @pl.when(k == nk - 1)
        def _():
            def rows_chunk(ci, carry):
                rows = pl.ds(pl.multiple_of(ci * 128, 128), 128)
                h1 = xh_ref[rows, :] * g1_ref[...] + b1_ref[...]
                pre = ALPHA * h1 + acc[rows, :] + b2_ref[...]
                mean = jnp.mean(pre, axis=1, keepdims=True)
                cen = pre - mean
                var = jnp.mean(cen * cen, axis=1, keepdims=True)
                rstd = lax.rsqrt(var + LN_EPS)
                xhat = cen * rstd
                y = xhat * g2_ref[...] + be2_ref[...]
                err = y - t_ref[rows, :]
                dy = err * (1.0 / D)
                g = dy * g2_ref[...]
                dpre = rstd * (g - jnp.mean(g, axis=1, keepdims=True)
                               - xhat * jnp.mean(g * xhat, axis=1, keepdims=True))
                d_ref[rows, :] = dpre
                db_ref[rows, :] = dpre.astype(BF16)
                st_ref[0:1, :] += jnp.sum(dy * xhat, axis=0, keepdims=True)
                st_ref[1:2, :] += jnp.sum(dy, axis=0, keepdims=True)
                st_ref[2:3, :] += jnp.sum(dpre, axis=0, keepdims=True)
                st_ref[3:4, :] += jnp.broadcast_to(jnp.sum(err * err).reshape(1, 1), (1, D))
                return carry

            lax.fori_loop(0, tm // 128, rows_chunk, 0)

    row = pl.BlockSpec((tm, D), lambda i, k: (i, 0))
    vec = pl.BlockSpec((1, D), lambda i, k: (0, 0))
    return pl.pallas_call(
        body, name="ff2_ln2_loss", grid=(S // tm, nk),
        in_specs=[pl.BlockSpec((tm, tk), lambda i, k: (i, k)), pl.BlockSpec((tk, D), lambda i, k: (k, 0)),
                  row, vec, vec, vec, vec, vec, row],
        out_specs=[row, row, pl.BlockSpec((8, D), lambda i, k: (0, 0))],
        out_shape=[jax.ShapeDtypeStruct((S, D), F32), jax.ShapeDtypeStruct((S, D), BF16),
                   jax.ShapeDtypeStruct((8, D), F32)],
        scratch_shapes=[pltpu.VMEM((tm, D), F32)],
        compiler_params=_params(2),
    )(a, w2_g, xhat1, g1, b1, b2, g2, be2, target)


def _grad_w(act, dout, name, ti, tj, sharded, after=None):
    m, n = act.shape[1], dout.shape[1]
    ns = n // N_CHIPS
    per = ns // tj if sharded else None

    def body(a_ref, b_ref, o_ref, at_scr):
        @pl.when(pl.program_id(1) == 0)
        def _():
            at_scr[...] = a_ref[...].T

        o_ref[...] = _dot(at_scr[...], b_ref[...])

    if sharded:
        out_spec = pl.BlockSpec((None, ti, tj), lambda i, j: (j // per, i, j % per))
        out_shape = jax.ShapeDtypeStruct((N_CHIPS, m, ns), F32)
    else:
        out_spec = pl.BlockSpec((ti, tj), lambda i, j: (i, j))
        out_shape = jax.ShapeDtypeStruct((m, n), F32)
    body, more_specs, more = _behind(body, 2, after)
    return pl.pallas_call(
        body, name=name, grid=(m // ti, n // tj),
        in_specs=[pl.BlockSpec((S, ti), lambda i, j: (0, i)), pl.BlockSpec((S, tj), lambda i, j: (0, j))] + more_specs,
        out_specs=out_spec, out_shape=out_shape,
        scratch_shapes=[pltpu.VMEM((ti, S), BF16)],
        compiler_params=_params(2),
    )(act, dout, *more)


def _d_ff1(dpre2b, w2_g, r, after=None):
    tn = 512

    def body(d_ref, w_ref, r_ref, o_ref, gb_ref):
        da = _dot_nt(d_ref[...], w_ref[...])
        dp = da * (2.0 * r_ref[...].astype(F32))
        o_ref[...] = dp.astype(BF16)
        gb_ref[...] = jnp.sum(dp, axis=0, keepdims=True)

    body, more_specs, more = _behind(body, 3, after)
    return pl.pallas_call(
        body, name="d_ff1", grid=(DFF // tn,),
        in_specs=[pl.BlockSpec((S, D), lambda j: (0, 0)), pl.BlockSpec((tn, D), lambda j: (j, 0)),
                  pl.BlockSpec((S, tn), lambda j: (0, j))] + more_specs,
        out_specs=[pl.BlockSpec((S, tn), lambda j: (0, j)), pl.BlockSpec((1, tn), lambda j: (0, j))],
        out_shape=[jax.ShapeDtypeStruct((S, DFF), BF16), jax.ShapeDtypeStruct((1, DFF), F32)],
        compiler_params=_params(1),
    )(dpre2b, w2_g, r, *more)


def _d_h1_ln1(dprea, w1_g, dpre2, xhat1, rstd1, g1, after=None):
    tm, tk = 512, 1024
    per = D // tk
    nk = DFF // tk

    def body(a_ref, w_ref, d2_ref, xh_ref, rs_ref, g_ref, d_ref, db_ref, st_ref, acc):
        i, k = pl.program_id(0), pl.program_id(1)

        @pl.when(k == 0)
        def _():
            acc[...] = jnp.zeros_like(acc)

        @pl.when((i == 0) & (k == 0))
        def _():
            st_ref[...] = jnp.zeros_like(st_ref)

        acc[...] += _dot_nt(a_ref[...], w_ref[...])

        @pl.when(k == nk - 1)
        def _():
            def rows_chunk(ci, carry):
                rows = pl.ds(pl.multiple_of(ci * 128, 128), 128)
                dh = ALPHA * d2_ref[rows, :] + acc[rows, :]
                xhat = xh_ref[rows, :]
                g = dh * g_ref[...]
                dpre = rs_ref[rows, 0:1] * (g - jnp.mean(g, axis=1, keepdims=True)
                                            - xhat * jnp.mean(g * xhat, axis=1, keepdims=True))
                d_ref[rows, :] = dpre
                db_ref[rows, :] = dpre.astype(BF16)
                st_ref[0:1, :] += jnp.sum(dh * xhat, axis=0, keepdims=True)
                st_ref[1:2, :] += jnp.sum(dh, axis=0, keepdims=True)
                return carry

            lax.fori_loop(0, tm // 128, rows_chunk, 0)

    row = pl.BlockSpec((tm, D), lambda i, k: (i, 0))
    body, more_specs, more = _behind(body, 6, after)
    return pl.pallas_call(
        body, name="d_h1_ln1", grid=(S // tm, nk),
        in_specs=[pl.BlockSpec((tm, tk), lambda i, k: (i, k)),
                  pl.BlockSpec((None, D, tk), lambda i, k: (k // per, 0, k % per)),
                  row, row, pl.BlockSpec((tm, 128), lambda i, k: (i, 0)), pl.BlockSpec((1, D), lambda i, k: (0, 0))]
        + more_specs,
        out_specs=[row, row, pl.BlockSpec((8, D), lambda i, k: (0, 0))],
        out_shape=[jax.ShapeDtypeStruct((S, D), F32), jax.ShapeDtypeStruct((S, D), BF16),
                   jax.ShapeDtypeStruct((8, D), F32)],
        scratch_shapes=[pltpu.VMEM((tm, D), F32)],
        compiler_params=_params(2),
    )(dprea, w1_g, dpre2, xhat1, rstd1, g1, *more)


def _d_merged(dpre1b, wout_g, proj, ya, yb):
    tm, tn = 512, 1024

    def body(d_ref, w_ref, ga_ref, gb_ref, ya_ref, yb_ref, dya_ref, dyb_ref, dga_ref, dgb_ref):
        dm = _dot_nt(d_ref[...], w_ref[...])
        sa = _sigmoid(ga_ref[...])
        sb = _sigmoid(gb_ref[...])
        dya_ref[...] = (dm * sa).astype(BF16)
        dyb_ref[...] = (dm * sb).astype(BF16)
        dga_ref[...] = (dm * ya_ref[...].astype(F32) * sa * (1.0 - sa)).astype(BF16)
        dgb_ref[...] = (dm * yb_ref[...].astype(F32) * sb * (1.0 - sb)).astype(BF16)

    tile = pl.BlockSpec((tm, tn), lambda i, j: (i, j))
    return pl.pallas_call(
        body, name="d_merged", grid=(S // tm, D // tn),
        in_specs=[pl.BlockSpec((tm, D), lambda i, j: (i, 0)), pl.BlockSpec((tn, D), lambda i, j: (j, 0)),
                  pl.BlockSpec((tm, tn), lambda i, j: (i, 5 + j)), pl.BlockSpec((tm, tn), lambda i, j: (i, 7 + j)),
                  tile, tile],
        out_specs=[tile] * 4,
        out_shape=[jax.ShapeDtypeStruct((S, D), BF16)] * 4,
        compiler_params=_params(2),
    )(dpre1b, wout_g, proj, proj, ya, yb)


def _d_branches(dya, dyb, wpa_g, wpb_g, after=None):
    tk = 512

    def body(da_ref, db_ref, wa_ref, wb_ref, oa_ref, ob_ref):
        @pl.when(pl.program_id(0) == 0)
        def _():
            oa_ref[...] = jnp.zeros_like(oa_ref)
            ob_ref[...] = jnp.zeros_like(ob_ref)

        oa_ref[...] += _dot_nt(da_ref[...], wa_ref[...])
        ob_ref[...] += _dot_nt(db_ref[...], wb_ref[...])

    body, more_specs, more = _behind(body, 4, after)
    return pl.pallas_call(
        body, name="d_branches", grid=(D // tk,),
        in_specs=[pl.BlockSpec((S, tk), lambda k: (0, k)), pl.BlockSpec((S, tk), lambda k: (0, k)),
                  pl.BlockSpec((None, DA, tk), lambda k: (k, 0, 0)), pl.BlockSpec((None, DB, tk), lambda k: (k, 0, 0))]
        + more_specs,
        out_specs=[pl.BlockSpec((S, DA), lambda k: (0, 0)), pl.BlockSpec((S, DB), lambda k: (0, 0))],
        out_shape=[jax.ShapeDtypeStruct((S, DA), F32), jax.ShapeDtypeStruct((S, DB), F32)],
        compiler_params=_params(1),
    )(dya, dyb, wpa_g, wpb_g, *more)


def _gmlp_bwd(proj, dgmlp, ws, ws_t, bsp_b, gain_v, bias_v):
    def body(u_ref, vb_ref, dg_ref, ws_ref, wst_ref, bsp_ref, g_ref, be_ref, duv_ref, gws_ref, gbs_ref, st_ref):
        @pl.when(pl.program_id(0) == 0)
        def _():
            gws_ref[...] = jnp.zeros_like(gws_ref)
            gbs_ref[...] = jnp.zeros_like(gbs_ref)
            st_ref[...] = jnp.zeros_like(st_ref)

        u, tu, u_act, vb, tv, rstd, xhat, vn = _gmlp_parts(u_ref, vb_ref, g_ref, be_ref)
        dg = dg_ref[...]
        dz = dg * u_act
        row = lax.broadcasted_iota(jnp.int32, (128, 128), 0)
        col = lax.broadcasted_iota(jnp.int32, (128, 128), 1)
        causal = row >= col
        causal_t = row <= col
        dvn_parts = []
        z_parts = []
        for g in range(NH):
            cols = slice(g * 128, (g + 1) * 128)
            vng = vn[:, cols].astype(BF16)
            dzg = dz[:, cols]
            dzb = dzg.astype(BF16)
            wsg = jnp.where(causal, ws_ref[g], 0.0).astype(BF16)
            wsg_t = jnp.where(causal_t, wst_ref[g], 0.0).astype(BF16)
            z_parts.append(_dot(wsg, vng) + bsp_ref[g])
            gws_ref[g] += jnp.where(causal, _dot_nt(dzb, vng), 0.0)
            gbs_ref[g] += jnp.broadcast_to(jnp.sum(dzg, axis=1, keepdims=True), (128, 128))
            dvn_parts.append(_dot(wsg_t, dzb))
        z = jnp.concatenate(z_parts, axis=1)
        dvn = jnp.concatenate(dvn_parts, axis=1)
        du = dg * z * _gelu_grad(u, tu)
        st_ref[0:1, :] += jnp.sum(dvn * xhat, axis=0, keepdims=True)
        st_ref[1:2, :] += jnp.sum(dvn, axis=0, keepdims=True)
        gg = dvn * g_ref[...]
        dgv = rstd * (gg - jnp.mean(gg, axis=1, keepdims=True) - xhat * jnp.mean(gg * xhat, axis=1, keepdims=True))
        dvb = dgv * _gelu_grad(vb, tv)
        duv_ref[:, 0:DB] = du.astype(BF16)
        duv_ref[:, DB:2 * DB] = dvb.astype(BF16)

    full3 = pl.BlockSpec((NH, 128, 128), lambda c: (0, 0, 0))
    vec = pl.BlockSpec((1, DB), lambda c: (0, 0))
    return pl.pallas_call(
        body, name="gmlp_bwd", grid=(NBLK,),
        in_specs=[pl.BlockSpec((128, DB), lambda c: (c, 3)), pl.BlockSpec((128, DB), lambda c: (c, 4)),
                  pl.BlockSpec((128, DB), lambda c: (c, 0)), full3, full3, full3, vec, vec],
        out_specs=[pl.BlockSpec((128, 2 * DB), lambda c: (c, 0)), full3, full3, pl.BlockSpec((8, DB), lambda c: (0, 0))],
        out_shape=[jax.ShapeDtypeStruct((S, 2 * DB), BF16), jax.ShapeDtypeStruct((NH, 128, 128), F32),
                   jax.ShapeDtypeStruct((NH, 128, 128), F32), jax.ShapeDtypeStruct((8, DB), F32)],
        compiler_params=_params(1),
    )(proj, proj, dgmlp, ws, ws_t, bsp_b, gain_v, bias_v)


def _rel_bias_grad(ds_sums):
    buckets = jnp.asarray(np.stack([_bucket_tile(d) for _, d in PATTERNS]))

    def body(bk_ref, ds_ref, o_ref):
        row = lax.broadcasted_iota(jnp.int32, (N_BUCKETS, 128), 0)
        lane = lax.broadcasted_iota(jnp.int32, (N_BUCKETS, 128), 1)

        def one_bucket(t, out):
            hits = [bk_ref[p] == t for p in range(3)]
            for h in range(NH):
                tot = jnp.zeros((128, 256), F32)
                for p in range(3):
                    tot = tot + jnp.where(hits[p], ds_ref[p, h], 0.0)
                out = jnp.where((row == t) & (lane == h), jnp.sum(tot), out)
            return out

        o_ref[...] = lax.fori_loop(0, N_BUCKETS, one_bucket, jnp.zeros((N_BUCKETS, 128), F32))

    return pl.pallas_call(
        body, name="rel_bias_grad",
        in_specs=[pl.BlockSpec(memory_space=pltpu.VMEM)] * 2, out_specs=pl.BlockSpec(memory_space=pltpu.VMEM),
        out_shape=jax.ShapeDtypeStruct((N_BUCKETS, 128), F32),
        compiler_params=pltpu.CompilerParams(vmem_limit_bytes=VMEM_LIMIT),
    )(buckets, ds_sums)


def _d_x(dproj, win_g, dpre1, after=None):
    tm, tk = 512, 2304
    per = 2304 // tk
    nk = DIN // tk

    def body(a_ref, w_ref, d_ref, o_ref, acc):
        k = pl.program_id(1)

        @pl.when(k == 0)
        def _():
            acc[...] = ALPHA * d_ref[...]

        acc[...] += _dot_nt(a_ref[...], w_ref[...])

        @pl.when(k == nk - 1)
        def _():
            o_ref[...] = acc[...]

    row = pl.BlockSpec((tm, D), lambda i, k: (i, 0))
    body, more_specs, more = _behind(body, 3, after)
    return pl.pallas_call(
        body, name="d_x", grid=(S // tm, nk),
        in_specs=[pl.BlockSpec((tm, tk), lambda i, k: (i, k)),
                  pl.BlockSpec((None, D, tk), lambda i, k: (k // per, 0, k % per)), row] + more_specs,
        out_specs=row, out_shape=jax.ShapeDtypeStruct((S, D), F32),
        scratch_shapes=[pltpu.VMEM((tm, D), F32)],
        compiler_params=_params(2),
    )(dproj, win_g, dpre1, *more)


def _adamw(w, g, m, v, name):
    rows, cols = w.shape
    tm = max(t for t in range(8, 257, 8) if rows % t == 0)

    def body(w_ref, g_ref, m_ref, v_ref, d_ref, nm_ref, nv_ref, go_ref):
        g = g_ref[...]
        m = ADAM_B1 * m_ref[...] + (1.0 - ADAM_B1) * g
        v = ADAM_B2 * v_ref[...] + (1.0 - ADAM_B2) * (g * g)
        m_hat = m / (1.0 - ADAM_B1 ** ADAM_STEP)
        v_hat = v / (1.0 - ADAM_B2 ** ADAM_STEP)
        d_ref[...] = -ADAM_LR * (m_hat / (jnp.sqrt(v_hat) + ADAM_EPS) + ADAM_WD * w_ref[...])
        nm_ref[...] = m
        nv_ref[...] = v
        go_ref[...] = g

    spec = pl.BlockSpec((tm, cols), lambda i: (i, 0))
    return pl.pallas_call(
        body, name=name, grid=(rows // tm,), in_specs=[spec] * 4, out_specs=[spec] * 4,
        out_shape=[jax.ShapeDtypeStruct((rows, cols), F32)] * 4, compiler_params=_params(1),
    )(w, g, m, v)


def _position():
    x, y, c = lax.axis_index("x"), lax.axis_index("y"), lax.axis_index("c")
    chips = [(1 - x, y), (x, 1 - y), (1 - x, 1 - y)]
    return x, y, c, chips


def _remote(src, dst, send_sems, recv_sems, k, to):
    return pltpu.make_async_remote_copy(src_ref=src, dst_ref=dst, send_sem=send_sems.at[k], recv_sem=recv_sems.at[k],
                                        device_id=to, device_id_type=MESH)


def _place_shard(w, name, after=None):
    rows, cols = w.shape
    tm = 256
    x, y = lax.axis_index("x"), lax.axis_index("y")

    def body(chip_ref, w_ref, o_ref):
        o_ref[...] = w_ref[...].astype(BF16)

    more_specs, more = ([ANY], [after]) if after is not None else ([], [])
    if after is not None:
        inner = body
        body = lambda chip_ref, w_ref, after_ref, o_ref: inner(chip_ref, w_ref, o_ref)
    return pl.pallas_call(
        body, name=name,
        grid_spec=pltpu.PrefetchScalarGridSpec(
            num_scalar_prefetch=1, grid=(rows // tm,),
            in_specs=[pl.BlockSpec((tm, cols), lambda i, chip: (i, 0))] + more_specs,
            out_specs=pl.BlockSpec((None, tm, cols), lambda i, chip: (chip[0], i, 0))),
        out_shape=jax.ShapeDtypeStruct((N_CHIPS, rows, cols), BF16),
        compiler_params=_params(1),
    )(jnp.reshape(2 * x + y, (1,)).astype(jnp.int32), w, *more)


def _to_bf16(x, name, after=None):
    tm = 256

    def body(x_ref, o_ref):
        o_ref[...] = x_ref[...].astype(BF16)

    spec = pl.BlockSpec((tm, x.shape[1]), lambda i: (i, 0))
    body, more_specs, more = _behind(body, 1, after)
    return pl.pallas_call(
        body, name=name, grid=(x.shape[0] // tm,), in_specs=[spec] + more_specs, out_specs=spec,
        out_shape=jax.ShapeDtypeStruct(x.shape, BF16), compiler_params=_params(1),
    )(x, *more)


HBM = pl.BlockSpec(memory_space=pltpu.HBM)
SEM = pl.BlockSpec(memory_space=pltpu.SEMAPHORE)
EFFECT = pltpu.SideEffectType.DATAFLOW_SIDE_EFFECTING


def _comm_call(name, body, bufs, sems_in, sems_out, after=None, token=False):
    nb, ns, no = len(bufs), len(sems_in), len(sems_out)
    n_in = nb + ns + (after is not None)

    def wrapped(*refs):
        body(refs[:nb], refs[nb:nb + ns], refs[n_in + nb:n_in + nb + no])
        if token:
            refs[-1][...] = jnp.zeros((8, 128), F32)

    outs = pl.pallas_call(
        wrapped, name=name,
        in_specs=[HBM] * nb + [SEM] * ns + ([ANY] if after is not None else []),
        out_specs=[HBM] * nb + [SEM] * no + ([pl.BlockSpec(memory_space=pltpu.VMEM)] if token else []),
        out_shape=[pltpu.HBM(b.shape, b.dtype) for b in bufs] + [pltpu.SemaphoreType.DMA((k,)) for k in sems_out]
        + ([jax.ShapeDtypeStruct((8, 128), F32)] if token else []),
        input_output_aliases={i: i for i in range(nb)},
        compiler_params=pltpu.CompilerParams(has_side_effects=EFFECT),
    )(*[pltpu.with_memory_space_constraint(b, pltpu.HBM) for b in bufs], *sems_in, *([after] if after is not None else []))
    return list(outs[:nb]), list(outs[nb:nb + no]), (outs[-1] if token else None)


RING_STAGES = {"ici_near": 2, "ici_far": 2, "d2d_near": 2, "d2d_far": 1}


def _ring_copies(buf, send_sems, recv_sems, k0, stage):
    x, y, c, _ = _position()
    hr = buf.shape[1] // 2
    qr = hr // 2
    half = lambda chip, h: buf.at[chip, pl.ds(h * hr, hr), :]
    quarter = lambda chip, h, q: buf.at[chip, pl.ds(h * hr + q * qr, qr), :]
    mine, x_chip, y_chip, far_chip = 2 * x + y, 2 * (1 - x) + y, 2 * x + (1 - y), 2 * (1 - x) + (1 - y)
    to_x, to_y, sibling = (1 - x, y, c), (x, 1 - y, c), (x, y, 1 - c)
    if stage == "ici_near":
        moves = [(half(mine, c), to_x, half(x_chip, c)), (half(mine, c), to_y, half(y_chip, c))]
    elif stage == "ici_far":
        moves = [(quarter(x_chip, c, 0), to_y, quarter(far_chip, c, 0)),
                 (quarter(y_chip, c, 1), to_x, quarter(far_chip, c, 1))]
    elif stage == "d2d_near":
        moves = [(half(x_chip, c), sibling, half(x_chip, 1 - c)), (half(y_chip, c), sibling, half(y_chip, 1 - c))]
    else:
        moves = [(half(far_chip, c), sibling, half(far_chip, 1 - c))]
    sends = [_remote(src, src, send_sems, recv_sems, k0 + i, to) for i, (src, to, _) in enumerate(moves)]
    arrivals = [_remote(got, got, send_sems, recv_sems, k0 + i, (x, y, c)) for i, (_, _, got) in enumerate(moves)]
    return sends, arrivals


def _ring_call(name, groups, actions, after=None):
    tags = list(dict.fromkeys(t for _, t, _ in actions))
    counts = {t: len(groups[t]["bufs"]) for t in tags}
    first = {t: sum(counts[u] for u in tags[:i]) for i, t in enumerate(tags)}
    waits = [(t, s) for v, t, s in actions if v == "wait"]
    starts = [(t, s) for v, t, s in actions if v == "start"]

    def body(bufs, sems_in, sems_out):
        for verb, t, s in actions:
            at, sems = (starts.index((t, s)), sems_out) if verb == "start" else (waits.index((t, s)), sems_in)
            for w in range(counts[t]):
                sends, arrivals = _ring_copies(bufs[first[t] + w], sems[2 * at], sems[2 * at + 1], RING_STAGES[s] * w, s)
                if verb == "start":
                    for cp in sends:
                        cp.start()
                else:
                    for cp in arrivals:
                        cp.wait_recv()
                    for cp in sends:
                        cp.wait_send()

    bufs, sems, token = _comm_call(
        name, body, [b for t in tags for b in groups[t]["bufs"]],
        [sem for t, s in waits for sem in groups[t]["sems"][s]],
        [RING_STAGES[s] * counts[t] for t, s in starts for _ in (0, 1)], after, token=True)
    for t in tags:
        groups[t]["bufs"] = bufs[first[t]:first[t] + counts[t]]
    for t, s in waits:
        del groups[t]["sems"][s]
    for i, (t, s) in enumerate(starts):
        groups[t]["sems"][s] = (sems[2 * i], sems[2 * i + 1])
    return token


def _cx_copies(src, dst, send_sems, recv_sems, k0):
    x, y, c, chips = _position()
    sends = [_remote(src.at[2 * cx + cy], dst.at[2 * x + y], send_sems, recv_sems, k0 + j, (cx, cy, c))
             for j, (cx, cy) in enumerate(chips)]
    arrivals = [_remote(dst.at[2 * cx + cy], dst.at[2 * cx + cy], send_sems, recv_sems, k0 + j, (x, y, c))
                for j, (cx, cy) in enumerate(chips)]
    return sends, arrivals


def _cx_start(name, pair_sums):
    n = len(pair_sums)
    landing = [lax.empty(p.shape, p.dtype) for p in pair_sums]

    def body(bufs, _, sems):
        for w in range(n):
            for cp in _cx_copies(bufs[w], bufs[n + w], sems[0], sems[1], 3 * w)[0]:
                cp.start()

    bufs, sems, token = _comm_call(name, body, list(pair_sums) + landing, [], [3 * n, 3 * n], token=True)
    return (bufs, sems), token


def _cx_wait(name, state, after):
    bufs, sems = state
    n = len(bufs) // 2

    def body(refs, sems_in, _):
        for w in range(n):
            sends, arrivals = _cx_copies(refs[w], refs[n + w], sems_in[0], sems_in[1], 3 * w)
            for cp in arrivals:
                cp.wait_recv()
            for cp in sends:
                cp.wait_send()

    bufs, _, _ = _comm_call(name, body, bufs, sems, [], after)
    return bufs[:n], bufs[n:]


def _px_copies(src, dst, send_sems, recv_sems, k):
    x, y, c, _ = _position()
    hr = src.shape[1] // 2
    send = _remote(src.at[:, pl.ds((1 - c) * hr, hr), :], dst, send_sems, recv_sems, k, (x, y, 1 - c))
    arrival = _remote(dst, dst, send_sems, recv_sems, k, (x, y, c))
    return send, arrival


def _px_start(name, grads):
    n = len(grads)
    landing = [lax.empty((N_CHIPS, g.shape[1] // 2, g.shape[2]), F32) for g in grads]

    def body(bufs, _, sems):
        for w in range(n):
            _px_copies(bufs[w], bufs[n + w], sems[0], sems[1], w)[0].start()

    bufs, sems, token = _comm_call(name, body, list(grads) + landing, [], [n, n], token=True)
    return (bufs, sems), token


def _px_wait(name, state, after):
    bufs, sems = state
    n = len(bufs) // 2

    def body(refs, sems_in, _):
        for w in range(n):
            send, arrival = _px_copies(refs[w], refs[n + w], sems_in[0], sems_in[1], w)
            arrival.wait_recv()
            send.wait_send()

    bufs, _, _ = _comm_call(name, body, bufs, sems, [], after)
    return bufs[:n], bufs[n:]


def _pair_sum(grad, got, name):
    _, rows, cols = grad.shape
    hr = rows // 2
    tm = min(hr, 256)
    nb = hr // tm
    c = lax.axis_index("c")

    def body(c_ref, g_ref, o_ref, out_ref):
        out_ref[...] = (g_ref[...] + o_ref[...]).astype(BF16)

    return pl.pallas_call(
        body, name=name,
        grid_spec=pltpu.PrefetchScalarGridSpec(
            num_scalar_prefetch=1, grid=(N_CHIPS, nb),
            in_specs=[pl.BlockSpec((None, tm, cols), lambda s, i, c_ref: (s, c_ref[0] * nb + i, 0)),
                      pl.BlockSpec((None, tm, cols), lambda s, i, c_ref: (s, i, 0))],
            out_specs=pl.BlockSpec((None, tm, cols), lambda s, i, c_ref: (s, i, 0))),
        out_shape=jax.ShapeDtypeStruct((N_CHIPS, hr, cols), BF16),
        compiler_params=_params(2),
    )(jnp.reshape(c, (1,)).astype(jnp.int32), grad, got)


def _chip_sum(parts, pair_sums, name):
    _, hr, cols = parts.shape
    tm = min(hr, 256)
    nb = hr // tm
    x, y, c = lax.axis_index("x"), lax.axis_index("y"), lax.axis_index("c")

    def body(pos_ref, p_ref, own_ref, o_ref):
        chip = pos_ref[0]
        own = own_ref[...].astype(F32)
        term = lambda s: jnp.where(chip == s, own, p_ref[s].astype(F32))
        o_ref[...] = ((term(0) + term(1)) + term(2)) + term(3)

    return pl.pallas_call(
        body, name=name,
        grid_spec=pltpu.PrefetchScalarGridSpec(
            num_scalar_prefetch=1, grid=(nb,),
            in_specs=[pl.BlockSpec((N_CHIPS, tm, cols), lambda i, pos: (0, i, 0)),
                      pl.BlockSpec((None, tm, cols), lambda i, pos: (pos[0], i, 0))],
            out_specs=pl.BlockSpec((tm, cols), lambda i, pos: (pos[1] * nb + i, 0))),
        out_shape=jax.ShapeDtypeStruct((2 * hr, cols), F32), compiler_params=_params(1),
    )(jnp.stack([2 * x + y, c]).astype(jnp.int32), parts, pair_sums)


def _share_halves(bufs, name):
    n = len(bufs)

    def body(*refs):
        outs = refs[n:2 * n]
        send_sems, recv_sems = refs[2 * n:]
        x, y, c, _ = _position()
        copies = []
        for w in range(n):
            hr = outs[w].shape[0] // 2
            mine = outs[w].at[pl.ds(c * hr, hr), :]
            cp = _remote(mine, mine, send_sems, recv_sems, w, (x, y, 1 - c))
            cp.start()
            copies.append(cp)
        for w in range(n):
            hr = outs[w].shape[0] // 2
            theirs = outs[w].at[pl.ds((1 - c) * hr, hr), :]
            _remote(theirs, theirs, send_sems, recv_sems, w, (x, y, c)).wait_recv()
        for cp in copies:
            cp.wait_send()

    return pl.pallas_call(
        body, name=name,
        in_specs=[ANY] * n, out_specs=[ANY] * n,
        out_shape=[jax.ShapeDtypeStruct(b.shape, b.dtype) for b in bufs],
        input_output_aliases={w: w for w in range(n)},
        scratch_shapes=[pltpu.SemaphoreType.DMA((n,)), pltpu.SemaphoreType.DMA((n,))],
    )(*bufs)


def _allreduce_small(g):
    rows = g.shape[0]
    half = rows // 2

    def body(g_ref, o_ref, sib, slots, send_sems, recv_sems):
        x, y, c, chips = _position()
        me, sibling = (x, y, c), (x, y, 1 - c)
        my_chip = 2 * x + y
        mine = pl.ds(pl.multiple_of(c * half, 8), half)
        theirs = pl.ds(pl.multiple_of((1 - c) * half, 8), half)
        pair = _remote(g_ref.at[theirs], sib, send_sems, recv_sems, 0, sibling)
        pair.start()
        pair.wait()
        slots[my_chip] = g_ref[mine, :] + sib[...]
        sent = []
        for j, (cx, cy) in enumerate(chips):
            cp = _remote(slots.at[my_chip], slots.at[my_chip], send_sems, recv_sems, 1 + j, (cx, cy, c))
            cp.start()
            sent.append(cp)
        for j, (cx, cy) in enumerate(chips):
            got = slots.at[2 * cx + cy]
            _remote(got, got, send_sems, recv_sems, 1 + j, me).wait_recv()
        for cp in sent:
            cp.wait_send()
        o_ref[mine, :] = ((slots[0] + slots[1]) + slots[2]) + slots[3]
        swap = _remote(o_ref.at[mine], o_ref.at[mine], send_sems, recv_sems, 4, sibling)
        swap.start()
        swap.wait()

    vm = pl.BlockSpec(memory_space=pltpu.VMEM)
    return pl.pallas_call(
        body, name="allreduce_small",
        in_specs=[vm], out_specs=vm, out_shape=jax.ShapeDtypeStruct((rows, 128), F32),
        scratch_shapes=[pltpu.VMEM((half, 128), F32), pltpu.VMEM((N_CHIPS, half, 128), F32),
                        pltpu.SemaphoreType.DMA((5,)), pltpu.SemaphoreType.DMA((5,))],
        compiler_params=pltpu.CompilerParams(vmem_limit_bytes=VMEM_LIMIT),
    )(g)


_SMALL =("rel_bias", "ln_v_gain", "ln_v_bias", "w_spatial", "b_spatial", "ln1_gain", "ln1_bias",
          "b_ff1", "b_ff2", "ln2_gain", "ln2_bias")
_SMALL_ROWS = 1200
_LOSS_AT = (152832 // 128, 0)


def _pack_small(parts):
    flat = jnp.concatenate([parts[k].reshape(-1).astype(F32) for k in _SMALL])
    flat = jnp.pad(flat, (0, _SMALL_ROWS * 128 - flat.shape[0]))
    return flat.reshape(_SMALL_ROWS, 128)


def _unpack_small(packed, like):
    flat = packed.reshape(-1)
    out, at = {}, 0
    for k in _SMALL:
        n = math.prod(like[k].shape)
        out[k] = flat[at:at + n].reshape(like[k].shape)
        at += n
    return out


def kernel(x, w_in, rel_bias, ln_v_gain, ln_v_bias, w_spatial, b_spatial, w_proj_a, w_proj_b, w_out, ln1_gain, ln1_bias, w_ff1, b_ff1, w_ff2, b_ff2, ln2_gain, ln2_bias, loss_target, m_w_in, m_rel_bias, m_ln_v_gain, m_ln_v_bias, m_w_spatial, m_b_spatial, m_w_proj_a, m_w_proj_b, m_w_out, m_ln1_gain, m_ln1_bias, m_w_ff1, m_b_ff1, m_w_ff2, m_b_ff2, m_ln2_gain, m_ln2_bias, v_w_in, v_rel_bias, v_ln_v_gain, v_ln_v_bias, v_w_spatial, v_b_spatial, v_w_proj_a, v_w_proj_b, v_w_out, v_ln1_gain, v_ln1_bias, v_w_ff1, v_b_ff1, v_w_ff2, v_b_ff2, v_ln2_gain, v_ln2_bias):
    args = dict(locals())
    big = ("w_in", "w_proj_a", "w_proj_b", "w_out", "w_ff1", "w_ff2")
    weights = ("w_in", "rel_bias", "ln_v_gain", "ln_v_bias", "w_spatial", "b_spatial", "w_proj_a", "w_proj_b", "w_out",
               "ln1_gain", "ln1_bias", "w_ff1", "b_ff1", "w_ff2", "b_ff2", "ln2_gain", "ln2_bias")

    xs = x[0]
    target = loss_target[0]

    ring = {"a": {"bufs": [_place_shard(w_in[0], "place_w_in")], "sems": {}}}
    tok = _ring_call("allgather_a_near", ring, [("start", "a", "ici_near")])
    placed = [_place_shard(args[k][0], f"place_{k}", after=tok) for k in big[1:]]
    for tag, bufs in (("b", placed[0:3]), ("c", placed[3:4]), ("d", placed[4:5])):
        ring[tag] = {"bufs": bufs, "sems": {}}
    xb = _to_bf16(xs, "x_to_bf16", after=placed[4])

    mx, my = lax.axis_index("x"), lax.axis_index("y")
    own = jnp.reshape(2 * mx + my, (1,)).astype(jnp.int32)
    near = jnp.stack([2 * (1 - mx) + my, 2 * mx + (1 - my)]).astype(jnp.int32)
    far = jnp.reshape(2 * (1 - mx) + (1 - my), (1,)).astype(jnp.int32)
    proj = _proj(xb, ring["a"]["bufs"][0], own, "proj_own")
    _ring_call("allgather_a_far", ring, [("wait", "a", "ici_near"), ("start", "a", "ici_far"), ("start", "a", "d2d_near"),
                                         ("start", "b", "ici_near"), ("start", "c", "ici_near")], after=proj)
    _ring_call("allgather_a_near_done", ring, [("wait", "a", "d2d_near")])
    proj = _proj(xb, ring["a"]["bufs"][0], near, "proj_near", into=proj)
    _ring_call("allgather_a_last", ring, [("wait", "a", "ici_far"), ("start", "a", "d2d_far")], after=proj)
    _ring_call("allgather_a_done", ring, [("wait", "a", "d2d_far")])
    (win_g,) = ring["a"]["bufs"]
    proj = _proj(xb, win_g, far, "proj_far", into=proj)
    _ring_call("allgather_b_far", ring, [("wait", "b", "ici_near"), ("start", "b", "ici_far"), ("start", "b", "d2d_near")],
               after=proj)
    ws = w_spatial[0]
    ws_t = jnp.transpose(ws, (0, 2, 1))
    bsp_b = jnp.broadcast_to(b_spatial[0][:, :, None], (NH, 128, 128))
    gmlp = _gmlp_fwd(proj, ws, bsp_b, ln_v_gain, ln_v_bias)
    attn, lse = _attention_fwd(proj, rel_bias)
    _ring_call("allgather_b_last_c_far", ring,
               [("wait", "b", "ici_far"), ("start", "b", "d2d_far"),
                ("wait", "c", "ici_near"), ("start", "c", "ici_far"), ("start", "c", "d2d_near"),
                ("start", "d", "ici_near")], after=attn)
    _ring_call("allgather_b_done", ring, [("wait", "b", "d2d_near"), ("wait", "b", "d2d_far")])
    wpa_g, wpb_g, wout_g = ring["b"]["bufs"]
    wout_full = wout_g.reshape(D, D)
    ya, yb, merged = _branch(attn, gmlp, wpa_g, wpb_g, proj)
    xhat1, rstd1, h1b = _out_ln1(merged, wout_full, xs, ln1_gain, ln1_bias)
    _ring_call("allgather_c_last_d_far", ring,
               [("wait", "c", "ici_far"), ("start", "c", "d2d_far"),
                ("wait", "d", "ici_near"), ("start", "d", "ici_far"), ("start", "d", "d2d_near")], after=h1b)
    _ring_call("allgather_c_done", ring, [("wait", "c", "d2d_near"), ("wait", "c", "d2d_far")])
    (w1_g,) = ring["c"]["bufs"]
    a, r = _ff1(h1b, w1_g, b_ff1)
    _ring_call("allgather_d_last", ring, [("wait", "d", "ici_far"), ("start", "d", "d2d_far")], after=a)
    _ring_call("allgather_d_done", ring, [("wait", "d", "d2d_near"), ("wait", "d", "d2d_far")])
    (w2_g,) = ring["d"]["bufs"]
    w2_full = w2_g.reshape(DFF, D)
    dpre2, dpre2b, st2 = _ff2_ln2_loss(a, w2_full, xhat1, ln1_gain, ln1_bias, b_ff2, ln2_gain, ln2_bias, target)

    def pair_and_chip(tag, state, after):
        local, from_sibling = _px_wait(f"pair_exchange_wait_{tag}", state, after)
        pair_sums = [_pair_sum(g, o, f"pair_sum_{tag}_{i}") for i, (g, o) in enumerate(zip(local, from_sibling))]
        return _cx_start(f"chip_exchange_start_{tag}", pair_sums)

    g_w2 = _grad_w(a, dpre2b, "grad_w_ff2", 512, 2048, False)
    px, tok = _px_start("pair_exchange_start_w_ff2", [g_w2.reshape(N_CHIPS, DFF // N_CHIPS, D)])
    dprea, g_b1 = _d_ff1(dpre2b, w2_full, r, after=tok)
    cx_w2, tok = pair_and_chip("w_ff2", px, dprea)
    g_w1 = _grad_w(h1b, dprea, "grad_w_ff1", 512, 2048, True, after=tok)
    px, tok = _px_start("pair_exchange_start_w_ff1", [g_w1])
    dpre1, dpre1b, st1 = _d_h1_ln1(dprea, w1_g, dpre2, xhat1, rstd1, ln1_gain, after=tok)
    cx_w1, tok = pair_and_chip("w_ff1", px, dpre1b)
    g_wout = _grad_w(merged, dpre1b, "grad_w_out", 512, 2048, False, after=tok)
    dya, dyb, dga, dgb = _d_merged(dpre1b, wout_full, proj, ya, yb)
    g_wpa = _grad_w(attn, dya, "grad_w_proj_a", 1024, 512, True)
    g_wpb = _grad_w(gmlp, dyb, "grad_w_proj_b", 1024, 512, True)
    px, tok = _px_start("pair_exchange_start_b", [g_wpa, g_wpb, g_wout.reshape(N_CHIPS, D // N_CHIPS, D)])
    dattn, dgmlp = _d_branches(dya, dyb, wpa_g, wpb_g, after=tok)
    duv, g_ws, g_bs, stv = _gmlp_bwd(proj, dgmlp, ws, ws_t, bsp_b, ln_v_gain, ln_v_bias)
    cx_b, tok = pair_and_chip("b", px, duv)
    dq, dk, dv, ds_sums = _attention_bwd(proj, dattn, attn, lse, rel_bias, after=tok)
    g_rb = _rel_bias_grad(ds_sums)[:, :NH]

    small_g = dict(rel_bias=g_rb, ln_v_gain=stv[0], ln_v_bias=stv[1], w_spatial=g_ws, b_spatial=g_bs[:, :, 0],
                   ln1_gain=st1[0], ln1_bias=st1[1], b_ff1=g_b1, b_ff2=st2[2], ln2_gain=st2[0], ln2_bias=st2[1])
    gs = _allreduce_small(_pack_small(small_g).at[_LOSS_AT].set(st2[3, 0]))
    ds_, ms_, vs_, _ = _adamw(_pack_small({k: args[k] for k in _SMALL}), gs,
                           _pack_small({k: args["m_" + k] for k in _SMALL}),
                           _pack_small({k: args["v_" + k] for k in _SMALL}), "adamw_small")
    like = {k: args[k] for k in _SMALL}
    grads, deltas, new_m, new_v = (_unpack_small(t, like) for t in (gs, ds_, ms_, vs_))

    dproj = jnp.concatenate([dq, dk, dv, duv, dga, dgb], axis=1)
    g_win = _grad_w(xb, dproj, "grad_w_in", 512, 2304, True, after=gs)
    px, tok = _px_start("pair_exchange_start_w_in", [g_win])
    grad_x = _d_x(dproj, win_g, dpre1, after=tok)
    cx_in, tok = pair_and_chip("w_in", px, grad_x)

    def reduce_finish(tag, state, names, after):
        pair_sums, from_chips = _cx_wait(f"chip_exchange_wait_{tag}", state, after)
        halves = [_chip_sum(p, own, f"chip_sum_{k}") for p, own, k in zip(from_chips, pair_sums, names)]
        last = None
        for k, g in zip(names, _share_halves(halves, f"share_halves_{tag}")):
            d_, m_, v_, g_ = _adamw(args[k][0], g, args["m_" + k][0], args["v_" + k][0], f"adamw_{k}")
            grads[k], deltas[k], new_m[k], new_v[k] = g_[None], d_[None], m_[None], v_[None]
            last = d_
        return last

    done = reduce_finish("w_ff2", cx_w2, ["w_ff2"], tok)
    done = reduce_finish("w_ff1", cx_w1, ["w_ff1"], done)
    done = reduce_finish("b", cx_b, ["w_proj_a", "w_proj_b", "w_out"], done)
    reduce_finish("w_in", cx_in, ["w_in"], done)

    loss = gs[_LOSS_AT] * (0.5 / D)
    return (loss, grad_x[None], *[grads[k] for k in weights], *[deltas[k] for k in weights],
            *[new_m[k] for k in weights], *[new_v[k] for k in weights])
```

```python
import math

import numpy as np
import jax
import jax.numpy as jnp
from jax import lax
from jax.experimental import pallas as pl
from jax.experimental.pallas import tpu as pltpu

F32 = jnp.float32
BF16 = jnp.bfloat16

S = 2048
D = 2048
DA = 1024
DB = 1024
DFF = 8192
DIN = 9216
NH = 8
HD = 128
NBLK = 16
PATTERNS = ((128, 1), (512, 4), (2048, 16))
N_BUCKETS = 32
MAX_DISTANCE = 2048
ALPHA = 2.0 ** 0.25
LN_EPS = 1e-5
NEG_INF = -1e30
SCALE = HD ** -0.5
N_CHIPS = 4

ADAM_LR = 0.001
ADAM_B1 = 0.9
ADAM_B2 = 0.999
ADAM_EPS = 1e-08
ADAM_WD = 0.01
ADAM_STEP = 10

VMEM_LIMIT = 56 * 1024 * 1024
MESH = pl.DeviceIdType.MESH
ANY = pl.BlockSpec(memory_space=pl.ANY)


def _params(n_axes, vmem=VMEM_LIMIT):
    return pltpu.CompilerParams(dimension_semantics=("arbitrary",) * n_axes, vmem_limit_bytes=vmem)


def _bucket_tile(dilation):
    qi = np.arange(128)[:, None]
    kj = np.arange(256)[None, :]
    n = np.clip(128 + qi - kj, 0, 128) * dilation
    max_exact = N_BUCKETS // 2
    nf = np.maximum(n, 1).astype(np.float32)
    large = max_exact + (np.log(nf / np.float32(max_exact)) / np.float32(math.log(MAX_DISTANCE / max_exact))
                         * np.float32(N_BUCKETS - max_exact)).astype(np.int32)
    large = np.minimum(large, N_BUCKETS - 1)
    return np.where(n < max_exact, n, large).astype(np.int32)


def _gelu(x):
    c = math.sqrt(2.0 / math.pi)
    t = jnp.tanh(c * (x + 0.044715 * x * x * x))
    return 0.5 * x * (1.0 + t), t


def _gelu_grad(x, t):
    c = math.sqrt(2.0 / math.pi)
    return 0.5 * (1.0 + t) + 0.5 * x * (1.0 - t * t) * c * (1.0 + 3.0 * 0.044715 * x * x)


def _sigmoid(x):
    return 1.0 / (1.0 + jnp.exp(-x))


def _dot(a, b):
    return jnp.dot(a, b, preferred_element_type=F32)


def _behind(body, n_in, after):
    if after is None:
        return body, [], []
    return (lambda *refs: body(*refs[:n_in], *refs[n_in + 1:])), [ANY], [after]


def _dot_nt(a, b):
    return lax.dot_general(a, b, (((1,), (1,)), ((), ())), preferred_element_type=F32)


def _proj(xb, win_g, shards, name, into=None):
    tn = 768
    per = 2304 // tn

    def body(shards_ref, x_ref, w_ref, *rest):
        rest[-1][...] = _dot(x_ref[...], w_ref[...])

    in_specs = [pl.BlockSpec((S, D), lambda j, sh: (0, 0)),
                pl.BlockSpec((None, D, tn), lambda j, sh: (sh[j // per], 0, j % per))]
    return pl.pallas_call(
        body, name=name,
        grid_spec=pltpu.PrefetchScalarGridSpec(
            num_scalar_prefetch=1, grid=(shards.shape[0] * per,),
            in_specs=in_specs + ([ANY] if into is not None else []),
            out_specs=pl.BlockSpec((S, tn), lambda j, sh: (0, sh[j // per] * per + j % per))),
        out_shape=jax.ShapeDtypeStruct((S, DIN), F32),
        input_output_aliases={3: 0} if into is not None else {},
        compiler_params=_params(1),
    )(shards, xb, win_g, *([into] if into is not None else []))


FWD_HEADS_PER_STEP = 4
BWD_HEADS_PER_STEP = 2


def _head_bias_tiles(rb_ref, bk_ref, bias_scr, first_head, hps):
    qi = lax.broadcasted_iota(jnp.int32, (128, 256), 0)
    kj = lax.broadcasted_iota(jnp.int32, (128, 256), 1)
    steps = 128 + qi - kj
    band = (steps >= 0) & (steps <= 128)
    bias_scr[...] = jnp.zeros_like(bias_scr)
    for p in range(len(PATTERNS)):
        bucket = bk_ref[p]

        def one_bucket(t, carry):
            hit = bucket == t
            for j in range(hps):
                bias_scr[p, j] = jnp.where(hit, rb_ref[t, first_head + j], bias_scr[p, j])
            return carry

        lax.fori_loop(0, N_BUCKETS, one_bucket, 0)
        for j in range(hps):
            bias_scr[p, j] = jnp.where(band, bias_scr[p, j], NEG_INF)


def _block_rows(b, dilation):
    nblk = NBLK // dilation
    r, n = b // nblk, b % nblk
    start = r + n * (128 * dilation)
    prev_start = jnp.maximum(start - 128 * dilation, r)
    if dilation == 1:
        return pl.ds(pl.multiple_of(start, 128), 128), pl.ds(pl.multiple_of(prev_start, 128), 128), n > 0
    return pl.ds(start, 128, stride=dilation), pl.ds(prev_start, 128, stride=dilation), n > 0


def _head_specs(first, hps):
    return [pl.BlockSpec((S, HD), lambda g, j=j: (0, first + g * hps + j)) for j in range(hps)]


def _heads_spec(hps):
    return pl.BlockSpec((S, hps * HD), lambda g: (0, g))


def _attention_fwd(proj, rel_bias):
    hps = FWD_HEADS_PER_STEP
    buckets = jnp.asarray(np.stack([_bucket_tile(d) for _, d in PATTERNS]))

    def body(rb_ref, bk_ref, *refs):
        q_refs, k_refs, v_refs = (refs[i * hps:(i + 1) * hps] for i in range(3))
        o_ref, lse_ref, bias_scr = refs[3 * hps:3 * hps + 3]
        acc_scrs, m_scrs, l_scrs = (refs[3 * hps + 3 + i * hps:3 * hps + 3 + (i + 1) * hps] for i in range(3))
        _head_bias_tiles(rb_ref, bk_ref, bias_scr, pl.program_id(0) * hps, hps)
        kj = lax.broadcasted_iota(jnp.int32, (128, 256), 1)
        for p, (_, d) in enumerate(PATTERNS):
            prev_blocks = NBLK // d > 1

            def block(b, carry):
                units = [(j,) + _block_rows(blk, d) for blk in (b, b + NBLK // 2) for j in range(hps)]
                scores = []
                for j, rows, prows, _ in units:
                    q = q_refs[j][rows, :].astype(BF16)
                    cur = _dot_nt(q, k_refs[j][rows, :].astype(BF16))
                    if prev_blocks:
                        cur = jnp.concatenate([_dot_nt(q, k_refs[j][prows, :].astype(BF16)), cur], axis=1)
                    scores.append(cur)
                soft = []
                for u, (j, _, _, has_prev) in enumerate(units):
                    if prev_blocks:
                        s = jnp.where((kj >= 128) | has_prev, scores[u] * SCALE + bias_scr[p, j], NEG_INF)
                    else:
                        s = scores[u] * SCALE + bias_scr[p, j, :, 128:256]
                    m = jnp.max(s, axis=1, keepdims=True)
                    e = jnp.exp(s - m)
                    soft.append((m, jnp.sum(e, axis=1, keepdims=True), e.astype(BF16)))
                outs = []
                for u, (j, rows, prows, _) in enumerate(units):
                    e = soft[u][2]
                    if prev_blocks:
                        outs.append(_dot(e[:, :128], v_refs[j][prows, :].astype(BF16))
                                    + _dot(e[:, 128:], v_refs[j][rows, :].astype(BF16)))
                    else:
                        outs.append(_dot(e, v_refs[j][rows, :].astype(BF16)))
                for u, (j, rows, _, _) in enumerate(units):
                    acc_scr, m_scr, l_scr = acc_scrs[j], m_scrs[j], l_scrs[j]
                    (m, den, _), o = soft[u], outs[u]
                    if p == 0:
                        acc_scr[rows, :] = o
                        m_scr[rows, :] = jnp.broadcast_to(m, (128, HD))
                        l_scr[rows, :] = jnp.broadcast_to(den, (128, HD))
                    else:
                        m_old = m_scr[rows, :]
                        m_new = jnp.maximum(m_old, m)
                        w_old, w_new = jnp.exp(m_old - m_new), jnp.exp(m - m_new)
                        acc_scr[rows, :] = acc_scr[rows, :] * w_old + o * w_new
                        l_scr[rows, :] = l_scr[rows, :] * w_old + den * w_new
                        m_scr[rows, :] = m_new
                return carry

            lax.fori_loop(0, NBLK // 2, block, 0)
        for j in range(hps):
            cols = slice(j * HD, (j + 1) * HD)
            den = l_scrs[j][...]
            o_ref[:, cols] = (acc_scrs[j][...] / den).astype(BF16)
            lse_ref[:, cols] = m_scrs[j][...] + jnp.log(den)

    return pl.pallas_call(
        body, name="attention_fwd", grid=(NH // hps,),
        in_specs=[pl.BlockSpec(memory_space=pltpu.SMEM), pl.BlockSpec((3, 128, 256), lambda g: (0, 0, 0))]
        + _head_specs(0, hps) + _head_specs(NH, hps) + _head_specs(2 * NH, hps),
        out_specs=[_heads_spec(hps), _heads_spec(hps)],
        out_shape=[jax.ShapeDtypeStruct((S, DA), BF16), jax.ShapeDtypeStruct((S, DA), F32)],
        scratch_shapes=[pltpu.VMEM((3, hps, 128, 256), F32)] + [pltpu.VMEM((S, HD), F32)] * (3 * hps),
        compiler_params=_params(1),
    )(rel_bias, buckets, *([proj] * (3 * hps)))


def _attention_bwd(proj, dattn, attn, lse, rel_bias, after=None):
    hps = BWD_HEADS_PER_STEP

    def body(rb_ref, bk_ref, *refs):
        q_refs, k_refs, v_refs, do_refs, o_refs, lse_refs = (refs[i * hps:(i + 1) * hps] for i in range(6))
        dq_ref, dk_ref, dv_ref, ds_ref, bias_scr = refs[6 * hps:6 * hps + 5]
        dl_scrs, dq_scrs, dk_scrs, dv_scrs = (refs[6 * hps + 5 + i * hps:6 * hps + 5 + (i + 1) * hps] for i in range(4))
        _head_bias_tiles(rb_ref, bk_ref, bias_scr, pl.program_id(0) * hps, hps)
        ds_ref[...] = jnp.zeros_like(ds_ref)
        for j in range(hps):
            dq_scrs[j][...] = jnp.zeros((S, HD), F32)
            dk_scrs[j][...] = jnp.zeros((S, HD), F32)
            dv_scrs[j][...] = jnp.zeros((S, HD), F32)
            prod = do_refs[j][...] * o_refs[j][...].astype(F32)
            dl_scrs[j][...] = jnp.broadcast_to(jnp.sum(prod, axis=1, keepdims=True), (S, HD))
        for p, (_, d) in enumerate(PATTERNS):
            prev_blocks = NBLK // d > 1

            def block(b, carry):
                units = [(j,) + _block_rows(blk, d) for blk in (b, b + NBLK // 2) for j in range(hps)]
                ops, raw = [], []
                for j, rows, prows, _ in units:
                    q, do = q_refs[j][rows, :].astype(BF16), do_refs[j][rows, :].astype(BF16)
                    kc, vc = k_refs[j][rows, :].astype(BF16), v_refs[j][rows, :].astype(BF16)
                    if prev_blocks:
                        kp, vp = k_refs[j][prows, :].astype(BF16), v_refs[j][prows, :].astype(BF16)
                        ops.append((q, do, kc, kp))
                        raw.append((_dot_nt(q, kc), _dot_nt(do, vc), _dot_nt(q, kp), _dot_nt(do, vp)))
                    else:
                        ops.append((q, do, kc))
                        raw.append((_dot_nt(q, kc), _dot_nt(do, vc)))
                probs = []
                for u, (j, rows, _, has_prev) in enumerate(units):
                    lse_b, dl_b = lse_refs[j][rows, :], dl_scrs[j][rows, :]
                    p_c = jnp.exp(raw[u][0] * SCALE + bias_scr[p, j, :, 128:256] - lse_b)
                    ds_c = p_c * (raw[u][1] - dl_b)
                    ds_ref[p, j, :, 128:256] += ds_c
                    if prev_blocks:
                        p_p = jnp.where(has_prev, jnp.exp(raw[u][2] * SCALE + bias_scr[p, j, :, 0:128] - lse_b), 0.0)
                        ds_p = p_p * (raw[u][3] - dl_b)
                        ds_ref[p, j, :, 0:128] += ds_p
                        probs.append((p_c, ds_c, p_p, ds_p))
                    else:
                        probs.append((p_c, ds_c))
                grads = []
                for u in range(len(units)):
                    q, do, kc = ops[u][:3]
                    p_c, ds_c = probs[u][:2]
                    dq = _dot(ds_c.astype(BF16), kc)
                    cur = (_dot(ds_c.T.astype(BF16), q) * SCALE, _dot(p_c.T.astype(BF16), do))
                    if prev_blocks:
                        p_p, ds_p = probs[u][2:]
                        dq = dq + _dot(ds_p.astype(BF16), ops[u][3])
                        cur = cur + (_dot(ds_p.T.astype(BF16), q) * SCALE, _dot(p_p.T.astype(BF16), do))
                    grads.append((dq * SCALE,) + cur)
                for u, (j, rows, prows, _) in enumerate(units):
                    dq_scrs[j][rows, :] += grads[u][0]
                    dk_scrs[j][rows, :] += grads[u][1]
                    dv_scrs[j][rows, :] += grads[u][2]
                    if prev_blocks:
                        dk_scrs[j][prows, :] += grads[u][3]
                        dv_scrs[j][prows, :] += grads[u][4]
                return carry

            lax.fori_loop(0, NBLK // 2, block, 0)
        for j in range(hps):
            cols = slice(j * HD, (j + 1) * HD)
            dq_ref[:, cols] = dq_scrs[j][...].astype(BF16)
            dk_ref[:, cols] = dk_scrs[j][...].astype(BF16)
            dv_ref[:, cols] = dv_scrs[j][...].astype(BF16)

    buckets = jnp.asarray(np.stack([_bucket_tile(d) for _, d in PATTERNS]))
    body, more_specs, more = _behind(body, 2 + 6 * hps, after)
    return pl.pallas_call(
        body, name="attention_bwd", grid=(NH // hps,),
        in_specs=[pl.BlockSpec(memory_space=pltpu.SMEM), pl.BlockSpec((3, 128, 256), lambda g: (0, 0, 0))]
        + _head_specs(0, hps) + _head_specs(NH, hps) + _head_specs(2 * NH, hps) + 3 * _head_specs(0, hps)
        + more_specs,
        out_specs=3 * [_heads_spec(hps)] + [pl.BlockSpec((3, hps, 128, 256), lambda g: (0, g, 0, 0))],
        out_shape=[jax.ShapeDtypeStruct((S, DA), BF16)] * 3 + [jax.ShapeDtypeStruct((3, NH, 128, 256), F32)],
        scratch_shapes=[pltpu.VMEM((3, hps, 128, 256), F32)] + [pltpu.VMEM((S, HD), F32)] * (4 * hps),
        compiler_params=_params(1),
    )(rel_bias, buckets, *([proj] * (3 * hps)), *([dattn] * hps), *([attn] * hps), *([lse] * hps), *more)


def _gmlp_parts(u_ref, vb_ref, g_ref, be_ref):
    u = u_ref[...]
    u_act, tu = _gelu(u)
    vb = vb_ref[...]
    gv, tv = _gelu(vb)
    mean = jnp.mean(gv, axis=1, keepdims=True)
    cen = gv - mean
    var = jnp.mean(cen * cen, axis=1, keepdims=True)
    rstd = lax.rsqrt(var + LN_EPS)
    xhat = cen * rstd
    vn = xhat * g_ref[...] + be_ref[...]
    return u, tu, u_act, vb, tv, rstd, xhat, vn


def _gmlp_fwd(proj, ws, bsp_b, gain_v, bias_v):
    def body(u_ref, vb_ref, ws_ref, bsp_ref, g_ref, be_ref, o_ref):
        _, _, u_act, _, _, _, _, vn = _gmlp_parts(u_ref, vb_ref, g_ref, be_ref)
        row = lax.broadcasted_iota(jnp.int32, (128, 128), 0)
        col = lax.broadcasted_iota(jnp.int32, (128, 128), 1)
        causal = row >= col
        for g in range(NH):
            cols = slice(g * 128, (g + 1) * 128)
            wsg = jnp.where(causal, ws_ref[g], 0.0).astype(BF16)
            z = _dot(wsg, vn[:, cols].astype(BF16)) + bsp_ref[g]
            o_ref[:, cols] = (u_act[:, cols] * z).astype(BF16)

    return pl.pallas_call(
        body, name="gmlp_fwd", grid=(NBLK,),
        in_specs=[pl.BlockSpec((128, DB), lambda c: (c, 3)), pl.BlockSpec((128, DB), lambda c: (c, 4)),
                  pl.BlockSpec((NH, 128, 128), lambda c: (0, 0, 0)), pl.BlockSpec((NH, 128, 128), lambda c: (0, 0, 0)),
                  pl.BlockSpec((1, DB), lambda c: (0, 0)), pl.BlockSpec((1, DB), lambda c: (0, 0))],
        out_specs=pl.BlockSpec((128, DB), lambda c: (c, 0)),
        out_shape=jax.ShapeDtypeStruct((S, DB), BF16),
        compiler_params=_params(1),
    )(proj, proj, ws, bsp_b, gain_v, bias_v)


def _branch(attn, gmlp, wpa_g, wpb_g, proj):
    tn = 512

    def body(a_ref, g_ref, wa_ref, wb_ref, ga_ref, gb_ref, ya_ref, yb_ref, mg_ref):
        ya = _dot(a_ref[...], wa_ref[...])
        yb = _dot(g_ref[...], wb_ref[...])
        ya_ref[...] = ya.astype(BF16)
        yb_ref[...] = yb.astype(BF16)
        mg_ref[...] = (_sigmoid(ga_ref[...]) * ya + _sigmoid(gb_ref[...]) * yb).astype(BF16)

    out = pl.BlockSpec((S, tn), lambda j: (0, j))
    return pl.pallas_call(
        body, name="branch", grid=(D // tn,),
        in_specs=[pl.BlockSpec((S, DA), lambda j: (0, 0)), pl.BlockSpec((S, DB), lambda j: (0, 0)),
                  pl.BlockSpec((None, DA, tn), lambda j: (j, 0, 0)), pl.BlockSpec((None, DB, tn), lambda j: (j, 0, 0)),
                  pl.BlockSpec((S, tn), lambda j: (0, 5120 // tn + j)), pl.BlockSpec((S, tn), lambda j: (0, 7168 // tn + j))],
        out_specs=[out, out, out],
        out_shape=[jax.ShapeDtypeStruct((S, D), BF16)] * 3,
        compiler_params=_params(1),
    )(attn, gmlp, wpa_g, wpb_g, proj, proj)


def _out_ln1(merged, wout_g, x, gain, bias):
    tm = 256

    def body(m_ref, w_ref, x_ref, g_ref, b_ref, xh_ref, rs_ref, h_ref):
        pre = ALPHA * x_ref[...] + _dot(m_ref[...], w_ref[...])
        mean = jnp.mean(pre, axis=1, keepdims=True)
        cen = pre - mean
        var = jnp.mean(cen * cen, axis=1, keepdims=True)
        rstd = lax.rsqrt(var + LN_EPS)
        xhat = cen * rstd
        xh_ref[...] = xhat
        rs_ref[...] = jnp.broadcast_to(rstd, (tm, 128))
        h_ref[...] = (xhat * g_ref[...] + b_ref[...]).astype(BF16)

    row = pl.BlockSpec((tm, D), lambda i: (i, 0))
    vec = pl.BlockSpec((1, D), lambda i: (0, 0))
    return pl.pallas_call(
        body, name="out_ln1", grid=(S // tm,),
        in_specs=[row, pl.BlockSpec((D, D), lambda i: (0, 0)), row, vec, vec],
        out_specs=[row, pl.BlockSpec((tm, 128), lambda i: (i, 0)), row],
        out_shape=[jax.ShapeDtypeStruct((S, D), F32), jax.ShapeDtypeStruct((S, 128), F32),
                   jax.ShapeDtypeStruct((S, D), BF16)],
        compiler_params=_params(1),
    )(merged, wout_g, x, gain, bias)


def _ff1(h1b, w1_g, b1):
    tn = 512
    per = D // tn

    def body(h_ref, w_ref, b_ref, a_ref, r_ref):
        r = jnp.maximum(_dot(h_ref[...], w_ref[...]) + b_ref[...], 0.0)
        r_ref[...] = r.astype(BF16)
        a_ref[...] = (r * r).astype(BF16)

    out = pl.BlockSpec((S, tn), lambda j: (0, j))
    return pl.pallas_call(
        body, name="ff1", grid=(DFF // tn,),
        in_specs=[pl.BlockSpec((S, D), lambda j: (0, 0)),
                  pl.BlockSpec((None, D, tn), lambda j: (j // per, 0, j % per)),
                  pl.BlockSpec((1, tn), lambda j: (0, j))],
        out_specs=[out, out],
        out_shape=[jax.ShapeDtypeStruct((S, DFF), BF16)] * 2,
        compiler_params=_params(1),
    )(h1b, w1_g, b1)


def _ff2_ln2_loss(a, w2_g, xhat1, g1, b1, b2, g2, be2, target):
    tm, tk = 512, 1024
    nk = DFF // tk

    def body(a_ref, w_ref, xh_ref, g1_ref, b1_ref, b2_ref, g2_ref, be2_ref, t_ref, d_ref, db_ref, st_ref, acc):
        i, k = pl.program_id(0), pl.program_id(1)

        @pl.when(k == 0)
        def _():
            acc[...] = jnp.zeros_like(acc)

        @pl.when((i == 0) & (k == 0))
        def _():
            st_ref[...] = jnp.zeros_like(st_ref)

        acc[...] += _dot(a_ref[...], w_ref[...])

        @pl.when(k == nk - 1)
        def _():
            def rows_chunk(ci, carry):
                rows = pl.ds(pl.multiple_of(ci * 128, 128), 128)
                h1 = xh_ref[rows, :] * g1_ref[...] + b1_ref[...]
                pre = ALPHA * h1 + acc[rows, :] + b2_ref[...]
                mean = jnp.mean(pre, axis=1, keepdims=True)
                cen = pre - mean
                var = jnp.mean(cen * cen, axis=1, keepdims=True)
                rstd = lax.rsqrt(var + LN_EPS)
                xhat = cen * rstd
                y = xhat * g2_ref[...] + be2_ref[...]
                err = y - t_ref[rows, :]
                dy = err * (1.0 / D)
                g = dy * g2_ref[...]
                dpre = rstd * (g - jnp.mean(g, axis=1, keepdims=True)
                               - xhat * jnp.mean(g * xhat, axis=1, keepdims=True))
                d_ref[rows, :] = dpre
                db_ref[rows, :] = dpre.astype(BF16)
                st_ref[0:1, :] += jnp.sum(dy * xhat, axis=0, keepdims=True)
                st_ref[1:2, :] += jnp.sum(dy, axis=0, keepdims=True)
                st_ref[2:3, :] += jnp.sum(dpre, axis=0, keepdims=True)
                st_ref[3:4, :] += jnp.broadcast_to(jnp.sum(err * err).reshape(1, 1), (1, D))
                return carry

            lax.fori_loop(0, tm // 128, rows_chunk, 0)

    row = pl.BlockSpec((tm, D), lambda i, k: (i, 0))
    vec = pl.BlockSpec((1, D), lambda i, k: (0, 0))
    return pl.pallas_call(
        body, name="ff2_ln2_loss", grid=(S // tm, nk),
        in_specs=[pl.BlockSpec((tm, tk), lambda i, k: (i, k)), pl.BlockSpec((tk, D), lambda i, k: (k, 0)),
                  row, vec, vec, vec, vec, vec, row],
        out_specs=[row, row, pl.BlockSpec((8, D), lambda i, k: (0, 0))],
        out_shape=[jax.ShapeDtypeStruct((S, D), F32), jax.ShapeDtypeStruct((S, D), BF16),
                   jax.ShapeDtypeStruct((8, D), F32)],
        scratch_shapes=[pltpu.VMEM((tm, D), F32)],
        compiler_params=_params(2),
    )(a, w2_g, xhat1, g1, b1, b2, g2, be2, target)


def _grad_w(act, dout, name, ti, tj, sharded, after=None):
    m, n = act.shape[1], dout.shape[1]
    ns = n // N_CHIPS
    per = ns // tj if sharded else None

    def body(a_ref, b_ref, o_ref, at_scr):
        @pl.when(pl.program_id(1) == 0)
        def _():
            at_scr[...] = a_ref[...].T

        o_ref[...] = _dot(at_scr[...], b_ref[...])

    if sharded:
        out_spec = pl.BlockSpec((None, ti, tj), lambda i, j: (j // per, i, j % per))
        out_shape = jax.ShapeDtypeStruct((N_CHIPS, m, ns), F32)
    else:
        out_spec = pl.BlockSpec((ti, tj), lambda i, j: (i, j))
        out_shape = jax.ShapeDtypeStruct((m, n), F32)
    body, more_specs, more = _behind(body, 2, after)
    return pl.pallas_call(
        body, name=name, grid=(m // ti, n // tj),
        in_specs=[pl.BlockSpec((S, ti), lambda i, j: (0, i)), pl.BlockSpec((S, tj), lambda i, j: (0, j))] + more_specs,
        out_specs=out_spec, out_shape=out_shape,
        scratch_shapes=[pltpu.VMEM((ti, S), BF16)],
        compiler_params=_params(2),
    )(act, dout, *more)


def _d_ff1(dpre2b, w2_g, r, after=None):
    tn = 512

    def body(d_ref, w_ref, r_ref, o_ref, gb_ref):
        da = _dot_nt(d_ref[...], w_ref[...])
        dp = da * (2.0 * r_ref[...].astype(F32))
        o_ref[...] = dp.astype(BF16)
        gb_ref[...] = jnp.sum(dp, axis=0, keepdims=True)

    body, more_specs, more = _behind(body, 3, after)
    return pl.pallas_call(
        body, name="d_ff1", grid=(DFF // tn,),
        in_specs=[pl.BlockSpec((S, D), lambda j: (0, 0)), pl.BlockSpec((tn, D), lambda j: (j, 0)),
                  pl.BlockSpec((S, tn), lambda j: (0, j))] + more_specs,
        out_specs=[pl.BlockSpec((S, tn), lambda j: (0, j)), pl.BlockSpec((1, tn), lambda j: (0, j))],
        out_shape=[jax.ShapeDtypeStruct((S, DFF), BF16), jax.ShapeDtypeStruct((1, DFF), F32)],
        compiler_params=_params(1),
    )(dpre2b, w2_g, r, *more)


def _d_h1_ln1(dprea, w1_g, dpre2, xhat1, rstd1, g1, after=None):
    tm, tk = 512, 1024
    per = D // tk
    nk = DFF // tk

    def body(a_ref, w_ref, d2_ref, xh_ref, rs_ref, g_ref, d_ref, db_ref, st_ref, acc):
        i, k = pl.program_id(0), pl.program_id(1)

        @pl.when(k == 0)
        def _():
            acc[...] = jnp.zeros_like(acc)

        @pl.when((i == 0) & (k == 0))
        def _():
            st_ref[...] = jnp.zeros_like(st_ref)

        acc[...] += _dot_nt(a_ref[...], w_ref[...])

        @pl.when(k == nk - 1)
        def _():
            def rows_chunk(ci, carry):
                rows = pl.ds(pl.multiple_of(ci * 128, 128), 128)
                dh = ALPHA * d2_ref[rows, :] + acc[rows, :]
                xhat = xh_ref[rows, :]
                g = dh * g_ref[...]
                dpre = rs_ref[rows, 0:1] * (g - jnp.mean(g, axis=1, keepdims=True)
                                            - xhat * jnp.mean(g * xhat, axis=1, keepdims=True))
                d_ref[rows, :] = dpre
                db_ref[rows, :] = dpre.astype(BF16)
                st_ref[0:1, :] += jnp.sum(dh * xhat, axis=0, keepdims=True)
                st_ref[1:2, :] += jnp.sum(dh, axis=0, keepdims=True)
                return carry

            lax.fori_loop(0, tm // 128, rows_chunk, 0)

    row = pl.BlockSpec((tm, D), lambda i, k: (i, 0))
    body, more_specs, more = _behind(body, 6, after)
    return pl.pallas_call(
        body, name="d_h1_ln1", grid=(S // tm, nk),
        in_specs=[pl.BlockSpec((tm, tk), lambda i, k: (i, k)),
                  pl.BlockSpec((None, D, tk), lambda i, k: (k // per, 0, k % per)),
                  row, row, pl.BlockSpec((tm, 128), lambda i, k: (i, 0)), pl.BlockSpec((1, D), lambda i, k: (0, 0))]
        + more_specs,
        out_specs=[row, row, pl.BlockSpec((8, D), lambda i, k: (0, 0))],
        out_shape=[jax.ShapeDtypeStruct((S, D), F32), jax.ShapeDtypeStruct((S, D), BF16),
                   jax.ShapeDtypeStruct((8, D), F32)],
        scratch_shapes=[pltpu.VMEM((tm, D), F32)],
        compiler_params=_params(2),
    )(dprea, w1_g, dpre2, xhat1, rstd1, g1, *more)


def _d_merged(dpre1b, wout_g, proj, ya, yb):
    tm, tn = 512, 1024

    def body(d_ref, w_ref, ga_ref, gb_ref, ya_ref, yb_ref, dya_ref, dyb_ref, dga_ref, dgb_ref):
        dm = _dot_nt(d_ref[...], w_ref[...])
        sa = _sigmoid(ga_ref[...])
        sb = _sigmoid(gb_ref[...])
        dya_ref[...] = (dm * sa).astype(BF16)
        dyb_ref[...] = (dm * sb).astype(BF16)
        dga_ref[...] = (dm * ya_ref[...].astype(F32) * sa * (1.0 - sa)).astype(BF16)
        dgb_ref[...] = (dm * yb_ref[...].astype(F32) * sb * (1.0 - sb)).astype(BF16)

    tile = pl.BlockSpec((tm, tn), lambda i, j: (i, j))
    return pl.pallas_call(
        body, name="d_merged", grid=(S // tm, D // tn),
        in_specs=[pl.BlockSpec((tm, D), lambda i, j: (i, 0)), pl.BlockSpec((tn, D), lambda i, j: (j, 0)),
                  pl.BlockSpec((tm, tn), lambda i, j: (i, 5 + j)), pl.BlockSpec((tm, tn), lambda i, j: (i, 7 + j)),
                  tile, tile],
        out_specs=[tile] * 4,
        out_shape=[jax.ShapeDtypeStruct((S, D), BF16)] * 4,
        compiler_params=_params(2),
    )(dpre1b, wout_g, proj, proj, ya, yb)


def _d_branches(dya, dyb, wpa_g, wpb_g, after=None):
    tk = 512

    def body(da_ref, db_ref, wa_ref, wb_ref, oa_ref, ob_ref):
        @pl.when(pl.program_id(0) == 0)
        def _():
            oa_ref[...] = jnp.zeros_like(oa_ref)
            ob_ref[...] = jnp.zeros_like(ob_ref)

        oa_ref[...] += _dot_nt(da_ref[...], wa_ref[...])
        ob_ref[...] += _dot_nt(db_ref[...], wb_ref[...])

    body, more_specs, more = _behind(body, 4, after)
    return pl.pallas_call(
        body, name="d_branches", grid=(D // tk,),
        in_specs=[pl.BlockSpec((S, tk), lambda k: (0, k)), pl.BlockSpec((S, tk), lambda k: (0, k)),
                  pl.BlockSpec((None, DA, tk), lambda k: (k, 0, 0)), pl.BlockSpec((None, DB, tk), lambda k: (k, 0, 0))]
        + more_specs,
        out_specs=[pl.BlockSpec((S, DA), lambda k: (0, 0)), pl.BlockSpec((S, DB), lambda k: (0, 0))],
        out_shape=[jax.ShapeDtypeStruct((S, DA), F32), jax.ShapeDtypeStruct((S, DB), F32)],
        compiler_params=_params(1),
    )(dya, dyb, wpa_g, wpb_g, *more)


def _gmlp_bwd(proj, dgmlp, ws, ws_t, bsp_b, gain_v, bias_v):
    def body(u_ref, vb_ref, dg_ref, ws_ref, wst_ref, bsp_ref, g_ref, be_ref, duv_ref, gws_ref, gbs_ref, st_ref):
        @pl.when(pl.program_id(0) == 0)
        def _():
            gws_ref[...] = jnp.zeros_like(gws_ref)
            gbs_ref[...] = jnp.zeros_like(gbs_ref)
            st_ref[...] = jnp.zeros_like(st_ref)

        u, tu, u_act, vb, tv, rstd, xhat, vn = _gmlp_parts(u_ref, vb_ref, g_ref, be_ref)
        dg = dg_ref[...]
        dz = dg * u_act
        row = lax.broadcasted_iota(jnp.int32, (128, 128), 0)
        col = lax.broadcasted_iota(jnp.int32, (128, 128), 1)
        causal = row >= col
        causal_t = row <= col
        dvn_parts = []
        z_parts = []
        for g in range(NH):
            cols = slice(g * 128, (g + 1) * 128)
            vng = vn[:, cols].astype(BF16)
            dzg = dz[:, cols]
            dzb = dzg.astype(BF16)
            wsg = jnp.where(causal, ws_ref[g], 0.0).astype(BF16)
            wsg_t = jnp.where(causal_t, wst_ref[g], 0.0).astype(BF16)
            z_parts.append(_dot(wsg, vng) + bsp_ref[g])
            gws_ref[g] += jnp.where(causal, _dot_nt(dzb, vng), 0.0)
            gbs_ref[g] += jnp.broadcast_to(jnp.sum(dzg, axis=1, keepdims=True), (128, 128))
            dvn_parts.append(_dot(wsg_t, dzb))
        z = jnp.concatenate(z_parts, axis=1)
        dvn = jnp.concatenate(dvn_parts, axis=1)
        du = dg * z * _gelu_grad(u, tu)
        st_ref[0:1, :] += jnp.sum(dvn * xhat, axis=0, keepdims=True)
        st_ref[1:2, :] += jnp.sum(dvn, axis=0, keepdims=True)
        gg = dvn * g_ref[...]
        dgv = rstd * (gg - jnp.mean(gg, axis=1, keepdims=True) - xhat * jnp.mean(gg * xhat, axis=1, keepdims=True))
        dvb = dgv * _gelu_grad(vb, tv)
        duv_ref[:, 0:DB] = du.astype(BF16)
        duv_ref[:, DB:2 * DB] = dvb.astype(BF16)

    full3 = pl.BlockSpec((NH, 128, 128), lambda c: (0, 0, 0))
    vec = pl.BlockSpec((1, DB), lambda c: (0, 0))
    return pl.pallas_call(
        body, name="gmlp_bwd", grid=(NBLK,),
        in_specs=[pl.BlockSpec((128, DB), lambda c: (c, 3)), pl.BlockSpec((128, DB), lambda c: (c, 4)),
                  pl.BlockSpec((128, DB), lambda c: (c, 0)), full3, full3, full3, vec, vec],
        out_specs=[pl.BlockSpec((128, 2 * DB), lambda c: (c, 0)), full3, full3, pl.BlockSpec((8, DB), lambda c: (0, 0))],
        out_shape=[jax.ShapeDtypeStruct((S, 2 * DB), BF16), jax.ShapeDtypeStruct((NH, 128, 128), F32),
                   jax.ShapeDtypeStruct((NH, 128, 128), F32), jax.ShapeDtypeStruct((8, DB), F32)],
        compiler_params=_params(1),
    )(proj, proj, dgmlp, ws, ws_t, bsp_b, gain_v, bias_v)


def _rel_bias_grad(ds_sums):
    buckets = jnp.asarray(np.stack([_bucket_tile(d) for _, d in PATTERNS]))

    def body(bk_ref, ds_ref, o_ref):
        row = lax.broadcasted_iota(jnp.int32, (N_BUCKETS, 128), 0)
        lane = lax.broadcasted_iota(jnp.int32, (N_BUCKETS, 128), 1)

        def one_bucket(t, out):
            hits = [bk_ref[p] == t for p in range(3)]
            for h in range(NH):
                tot = jnp.zeros((128, 256), F32)
                for p in range(3):
                    tot = tot + jnp.where(hits[p], ds_ref[p, h], 0.0)
                out = jnp.where((row == t) & (lane == h), jnp.sum(tot), out)
            return out

        o_ref[...] = lax.fori_loop(0, N_BUCKETS, one_bucket, jnp.zeros((N_BUCKETS, 128), F32))

    return pl.pallas_call(
        body, name="rel_bias_grad",
        in_specs=[pl.BlockSpec(memory_space=pltpu.VMEM)] * 2, out_specs=pl.BlockSpec(memory_space=pltpu.VMEM),
        out_shape=jax.ShapeDtypeStruct((N_BUCKETS, 128), F32),
        compiler_params=pltpu.CompilerParams(vmem_limit_bytes=VMEM_LIMIT),
    )(buckets, ds_sums)


def _d_x(dproj, win_g, dpre1, after=None):
    tm, tk = 512, 2304
    per = 2304 // tk
    nk = DIN // tk

    def body(a_ref, w_ref, d_ref, o_ref, acc):
        k = pl.program_id(1)

        @pl.when(k == 0)
        def _():
            acc[...] = ALPHA * d_ref[...]

        acc[...] += _dot_nt(a_ref[...], w_ref[...])

        @pl.when(k == nk - 1)
        def _():
            o_ref[...] = acc[...]

    row = pl.BlockSpec((tm, D), lambda i, k: (i, 0))
    body, more_specs, more = _behind(body, 3, after)
    return pl.pallas_call(
        body, name="d_x", grid=(S // tm, nk),
        in_specs=[pl.BlockSpec((tm, tk), lambda i, k: (i, k)),
                  pl.BlockSpec((None, D, tk), lambda i, k: (k // per, 0, k % per)), row] + more_specs,
        out_specs=row, out_shape=jax.ShapeDtypeStruct((S, D), F32),
        scratch_shapes=[pltpu.VMEM((tm, D), F32)],
        compiler_params=_params(2),
    )(dproj, win_g, dpre1, *more)


def _adamw(w, g, m, v, name):
    rows, cols = w.shape
    tm = max(t for t in range(8, 257, 8) if rows % t == 0)

    def body(w_ref, g_ref, m_ref, v_ref, d_ref, nm_ref, nv_ref, go_ref):
        g = g_ref[...]
        m = ADAM_B1 * m_ref[...] + (1.0 - ADAM_B1) * g
        v = ADAM_B2 * v_ref[...] + (1.0 - ADAM_B2) * (g * g)
        m_hat = m / (1.0 - ADAM_B1 ** ADAM_STEP)
        v_hat = v / (1.0 - ADAM_B2 ** ADAM_STEP)
        d_ref[...] = -ADAM_LR * (m_hat / (jnp.sqrt(v_hat) + ADAM_EPS) + ADAM_WD * w_ref[...])
        nm_ref[...] = m
        nv_ref[...] = v
        go_ref[...] = g

    spec = pl.BlockSpec((tm, cols), lambda i: (i, 0))
    return pl.pallas_call(
        body, name=name, grid=(rows // tm,), in_specs=[spec] * 4, out_specs=[spec] * 4,
        out_shape=[jax.ShapeDtypeStruct((rows, cols), F32)] * 4, compiler_params=_params(1),
    )(w, g, m, v)


def _position():
    x, y, c = lax.axis_index("x"), lax.axis_index("y"), lax.axis_index("c")
    chips = [(1 - x, y), (x, 1 - y), (1 - x, 1 - y)]
    return x, y, c, chips


def _remote(src, dst, send_sems, recv_sems, k, to):
    return pltpu.make_async_remote_copy(src_ref=src, dst_ref=dst, send_sem=send_sems.at[k], recv_sem=recv_sems.at[k],
                                        device_id=to, device_id_type=MESH)


def _place_shard(w, name, after=None):
    rows, cols = w.shape
    tm = 256
    x, y = lax.axis_index("x"), lax.axis_index("y")

    def body(chip_ref, w_ref, o_ref):
        o_ref[...] = w_ref[...].astype(BF16)

    more_specs, more = ([ANY], [after]) if after is not None else ([], [])
    if after is not None:
        inner = body
        body = lambda chip_ref, w_ref, after_ref, o_ref: inner(chip_ref, w_ref, o_ref)
    return pl.pallas_call(
        body, name=name,
        grid_spec=pltpu.PrefetchScalarGridSpec(
            num_scalar_prefetch=1, grid=(rows // tm,),
            in_specs=[pl.BlockSpec((tm, cols), lambda i, chip: (i, 0))] + more_specs,
            out_specs=pl.BlockSpec((None, tm, cols), lambda i, chip: (chip[0], i, 0))),
        out_shape=jax.ShapeDtypeStruct((N_CHIPS, rows, cols), BF16),
        compiler_params=_params(1),
    )(jnp.reshape(2 * x + y, (1,)).astype(jnp.int32), w, *more)


def _to_bf16(x, name, after=None):
    tm = 256

    def body(x_ref, o_ref):
        o_ref[...] = x_ref[...].astype(BF16)

    spec = pl.BlockSpec((tm, x.shape[1]), lambda i: (i, 0))
    body, more_specs, more = _behind(body, 1, after)
    return pl.pallas_call(
        body, name=name, grid=(x.shape[0] // tm,), in_specs=[spec] + more_specs, out_specs=spec,
        out_shape=jax.ShapeDtypeStruct(x.shape, BF16), compiler_params=_params(1),
    )(x, *more)


HBM = pl.BlockSpec(memory_space=pltpu.HBM)
SEM = pl.BlockSpec(memory_space=pltpu.SEMAPHORE)
EFFECT = pltpu.SideEffectType.DATAFLOW_SIDE_EFFECTING


def _comm_call(name, body, bufs, sems_in, sems_out, after=None, token=False):
    nb, ns, no = len(bufs), len(sems_in), len(sems_out)
    n_in = nb + ns + (after is not None)

    def wrapped(*refs):
        body(refs[:nb], refs[nb:nb + ns], refs[n_in + nb:n_in + nb + no])
        if token:
            refs[-1][...] = jnp.zeros((8, 128), F32)

    outs = pl.pallas_call(
        wrapped, name=name,
        in_specs=[HBM] * nb + [SEM] * ns + ([ANY] if after is not None else []),
        out_specs=[HBM] * nb + [SEM] * no + ([pl.BlockSpec(memory_space=pltpu.VMEM)] if token else []),
        out_shape=[pltpu.HBM(b.shape, b.dtype) for b in bufs] + [pltpu.SemaphoreType.DMA((k,)) for k in sems_out]
        + ([jax.ShapeDtypeStruct((8, 128), F32)] if token else []),
        input_output_aliases={i: i for i in range(nb)},
        compiler_params=pltpu.CompilerParams(has_side_effects=EFFECT),
    )(*[pltpu.with_memory_space_constraint(b, pltpu.HBM) for b in bufs], *sems_in, *([after] if after is not None else []))
    return list(outs[:nb]), list(outs[nb:nb + no]), (outs[-1] if token else None)


RING_STAGES = {"ici_near": 2, "ici_far": 2, "d2d_near": 2, "d2d_far": 1}


def _ring_copies(buf, send_sems, recv_sems, k0, stage):
    x, y, c, _ = _position()
    hr = buf.shape[1] // 2
    qr = hr // 2
    half = lambda chip, h: buf.at[chip, pl.ds(h * hr, hr), :]
    quarter = lambda chip, h, q: buf.at[chip, pl.ds(h * hr + q * qr, qr), :]
    mine, x_chip, y_chip, far_chip = 2 * x + y, 2 * (1 - x) + y, 2 * x + (1 - y), 2 * (1 - x) + (1 - y)
    to_x, to_y, sibling = (1 - x, y, c), (x, 1 - y, c), (x, y, 1 - c)
    if stage == "ici_near":
        moves = [(half(mine, c), to_x, half(x_chip, c)), (half(mine, c), to_y, half(y_chip, c))]
    elif stage == "ici_far":
        moves = [(quarter(x_chip, c, 0), to_y, quarter(far_chip, c, 0)),
                 (quarter(y_chip, c, 1), to_x, quarter(far_chip, c, 1))]
    elif stage == "d2d_near":
        moves = [(half(x_chip, c), sibling, half(x_chip, 1 - c)), (half(y_chip, c), sibling, half(y_chip, 1 - c))]
    else:
        moves = [(half(far_chip, c), sibling, half(far_chip, 1 - c))]
    sends = [_remote(src, src, send_sems, recv_sems, k0 + i, to) for i, (src, to, _) in enumerate(moves)]
    arrivals = [_remote(got, got, send_sems, recv_sems, k0 + i, (x, y, c)) for i, (_, _, got) in enumerate(moves)]
    return sends, arrivals


def _ring_call(name, groups, actions, after=None):
    tags = list(dict.fromkeys(t for _, t, _ in actions))
    counts = {t: len(groups[t]["bufs"]) for t in tags}
    first = {t: sum(counts[u] for u in tags[:i]) for i, t in enumerate(tags)}
    waits = [(t, s) for v, t, s in actions if v == "wait"]
    starts = [(t, s) for v, t, s in actions if v == "start"]

    def body(bufs, sems_in, sems_out):
        for verb, t, s in actions:
            at, sems = (starts.index((t, s)), sems_out) if verb == "start" else (waits.index((t, s)), sems_in)
            for w in range(counts[t]):
                sends, arrivals = _ring_copies(bufs[first[t] + w], sems[2 * at], sems[2 * at + 1], RING_STAGES[s] * w, s)
                if verb == "start":
                    for cp in sends:
                        cp.start()
                else:
                    for cp in arrivals:
                        cp.wait_recv()
                    for cp in sends:
                        cp.wait_send()

    bufs, sems, token = _comm_call(
        name, body, [b for t in tags for b in groups[t]["bufs"]],
        [sem for t, s in waits for sem in groups[t]["sems"][s]],
        [RING_STAGES[s] * counts[t] for t, s in starts for _ in (0, 1)], after, token=True)
    for t in tags:
        groups[t]["bufs"] = bufs[first[t]:first[t] + counts[t]]
    for t, s in waits:
        del groups[t]["sems"][s]
    for i, (t, s) in enumerate(starts):
        groups[t]["sems"][s] = (sems[2 * i], sems[2 * i + 1])
    return token


def _cx_copies(src, dst, send_sems, recv_sems, k0):
    x, y, c, chips = _position()
    sends = [_remote(src.at[2 * cx + cy], dst.at[2 * x + y], send_sems, recv_sems, k0 + j, (cx, cy, c))
             for j, (cx, cy) in enumerate(chips)]
    arrivals = [_remote(dst.at[2 * cx + cy], dst.at[2 * cx + cy], send_sems, recv_sems, k0 + j, (x, y, c))
                for j, (cx, cy) in enumerate(chips)]
    return sends, arrivals


def _cx_start(name, pair_sums):
    n = len(pair_sums)
    landing = [lax.empty(p.shape, p.dtype) for p in pair_sums]

    def body(bufs, _, sems):
        for w in range(n):
            for cp in _cx_copies(bufs[w], bufs[n + w], sems[0], sems[1], 3 * w)[0]:
                cp.start()

    bufs, sems, token = _comm_call(name, body, list(pair_sums) + landing, [], [3 * n, 3 * n], token=True)
    return (bufs, sems), token


def _cx_wait(name, state, after):
    bufs, sems = state
    n = len(bufs) // 2

    def body(refs, sems_in, _):
        for w in range(n):
            sends, arrivals = _cx_copies(refs[w], refs[n + w], sems_in[0], sems_in[1], 3 * w)
            for cp in arrivals:
                cp.wait_recv()
            for cp in sends:
                cp.wait_send()

    bufs, _, _ = _comm_call(name, body, bufs, sems, [], after)
    return bufs[:n], bufs[n:]


def _px_copies(src, dst, send_sems, recv_sems, k):
    x, y, c, _ = _position()
    hr = src.shape[1] // 2
    send = _remote(src.at[:, pl.ds((1 - c) * hr, hr), :], dst, send_sems, recv_sems, k, (x, y, 1 - c))
    arrival = _remote(dst, dst, send_sems, recv_sems, k, (x, y, c))
    return send, arrival


def _px_start(name, grads):
    n = len(grads)
    landing = [lax.empty((N_CHIPS, g.shape[1] // 2, g.shape[2]), F32) for g in grads]

    def body(bufs, _, sems):
        for w in range(n):
            _px_copies(bufs[w], bufs[n + w], sems[0], sems[1], w)[0].start()

    bufs, sems, token = _comm_call(name, body, list(grads) + landing, [], [n, n], token=True)
    return (bufs, sems), token


def _px_wait(name, state, after):
    bufs, sems = state
    n = len(bufs) // 2

    def body(refs, sems_in, _):
        for w in range(n):
            send, arrival = _px_copies(refs[w], refs[n + w], sems_in[0], sems_in[1], w)
            arrival.wait_recv()
            send.wait_send()

    bufs, _, _ = _comm_call(name, body, bufs, sems, [], after)
    return bufs[:n], bufs[n:]


def _pair_sum(grad, got, name):
    _, rows, cols = grad.shape
    hr = rows // 2
    tm = min(hr, 256)
    nb = hr // tm
    c = lax.axis_index("c")

    def body(c_ref, g_ref, o_ref, out_ref):
        out_ref[...] = (g_ref[...] + o_ref[...]).astype(BF16)

    return pl.pallas_call(
        body, name=name,
        grid_spec=pltpu.PrefetchScalarGridSpec(
            num_scalar_prefetch=1, grid=(N_CHIPS, nb),
            in_specs=[pl.BlockSpec((None, tm, cols), lambda s, i, c_ref: (s, c_ref[0] * nb + i, 0)),
                      pl.BlockSpec((None, tm, cols), lambda s, i, c_ref: (s, i, 0))],
            out_specs=pl.BlockSpec((None, tm, cols), lambda s, i, c_ref: (s, i, 0))),
        out_shape=jax.ShapeDtypeStruct((N_CHIPS, hr, cols), BF16),
        compiler_params=_params(2),
    )(jnp.reshape(c, (1,)).astype(jnp.int32), grad, got)


def _chip_sum(parts, pair_sums, name):
    _, hr, cols = parts.shape
    tm = min(hr, 256)
    nb = hr // tm
    x, y, c = lax.axis_index("x"), lax.axis_index("y"), lax.axis_index("c")

    def body(pos_ref, p_ref, own_ref, o_ref):
        chip = pos_ref[0]
        own = own_ref[...].astype(F32)
        term = lambda s: jnp.where(chip == s, own, p_ref[s].astype(F32))
        o_ref[...] = ((term(0) + term(1)) + term(2)) + term(3)

    return pl.pallas_call(
        body, name=name,
        grid_spec=pltpu.PrefetchScalarGridSpec(
            num_scalar_prefetch=1, grid=(nb,),
            in_specs=[pl.BlockSpec((N_CHIPS, tm, cols), lambda i, pos: (0, i, 0)),
                      pl.BlockSpec((None, tm, cols), lambda i, pos: (pos[0], i, 0))],
            out_specs=pl.BlockSpec((tm, cols), lambda i, pos: (pos[1] * nb + i, 0))),
        out_shape=jax.ShapeDtypeStruct((2 * hr, cols), F32), compiler_params=_params(1),
    )(jnp.stack([2 * x + y, c]).astype(jnp.int32), parts, pair_sums)


def _share_copies(buf, send_sems, recv_sems, k):
    x, y, c, _ = _position()
    hr = buf.shape[0] // 2
    mine, theirs = buf.at[pl.ds(c * hr, hr), :], buf.at[pl.ds((1 - c) * hr, hr), :]
    return (_remote(mine, mine, send_sems, recv_sems, k, (x, y, 1 - c)),
            _remote(theirs, theirs, send_sems, recv_sems, k, (x, y, c)))


def _share_start(name, bufs):
    n = len(bufs)

    def body(refs, _, sems):
        for w in range(n):
            _share_copies(refs[w], sems[0], sems[1], w)[0].start()

    bufs, sems, token = _comm_call(name, body, list(bufs), [], [n, n], token=True)
    return (bufs, sems), token


def _share_wait(name, state, after):
    bufs, sems = state

    def body(refs, sems_in, _):
        for w in range(len(bufs)):
            send, arrival = _share_copies(refs[w], sems_in[0], sems_in[1], w)
            arrival.wait_recv()
            send.wait_send()

    return _comm_call(name, body, bufs, sems, [], after)[0]


def _allreduce_small(g):
    rows = g.shape[0]
    half = rows // 2

    def body(g_ref, o_ref, sib, slots, send_sems, recv_sems):
        x, y, c, chips = _position()
        me, sibling = (x, y, c), (x, y, 1 - c)
        my_chip = 2 * x + y
        mine = pl.ds(pl.multiple_of(c * half, 8), half)
        theirs = pl.ds(pl.multiple_of((1 - c) * half, 8), half)
        pair = _remote(g_ref.at[theirs], sib, send_sems, recv_sems, 0, sibling)
        pair.start()
        pair.wait()
        slots[my_chip] = g_ref[mine, :] + sib[...]
        sent = []
        for j, (cx, cy) in enumerate(chips):
            cp = _remote(slots.at[my_chip], slots.at[my_chip], send_sems, recv_sems, 1 + j, (cx, cy, c))
            cp.start()
            sent.append(cp)
        for j, (cx, cy) in enumerate(chips):
            got = slots.at[2 * cx + cy]
            _remote(got, got, send_sems, recv_sems, 1 + j, me).wait_recv()
        for cp in sent:
            cp.wait_send()
        o_ref[mine, :] = ((slots[0] + slots[1]) + slots[2]) + slots[3]
        swap = _remote(o_ref.at[mine], o_ref.at[mine], send_sems, recv_sems, 4, sibling)
        swap.start()
        swap.wait()

    vm = pl.BlockSpec(memory_space=pltpu.VMEM)
    return pl.pallas_call(
        body, name="allreduce_small",
        in_specs=[vm], out_specs=vm, out_shape=jax.ShapeDtypeStruct((rows, 128), F32),
        scratch_shapes=[pltpu.VMEM((half, 128), F32), pltpu.VMEM((N_CHIPS, half, 128), F32),
                        pltpu.SemaphoreType.DMA((5,)), pltpu.SemaphoreType.DMA((5,))],
        compiler_params=pltpu.CompilerParams(vmem_limit_bytes=VMEM_LIMIT),
    )(g)


_SMALL =("rel_bias", "ln_v_gain", "ln_v_bias", "w_spatial", "b_spatial", "ln1_gain", "ln1_bias",
          "b_ff1", "b_ff2", "ln2_gain", "ln2_bias")
_SMALL_ROWS = 1200
_LOSS_AT = (152832 // 128, 0)


def _pack_small(parts):
    flat = jnp.concatenate([parts[k].reshape(-1).astype(F32) for k in _SMALL])
    flat = jnp.pad(flat, (0, _SMALL_ROWS * 128 - flat.shape[0]))
    return flat.reshape(_SMALL_ROWS, 128)


def _unpack_small(packed, like):
    flat = packed.reshape(-1)
    out, at = {}, 0
    for k in _SMALL:
        n = math.prod(like[k].shape)
        out[k] = flat[at:at + n].reshape(like[k].shape)
        at += n
    return out


def kernel(x, w_in, rel_bias, ln_v_gain, ln_v_bias, w_spatial, b_spatial, w_proj_a, w_proj_b, w_out, ln1_gain, ln1_bias, w_ff1, b_ff1, w_ff2, b_ff2, ln2_gain, ln2_bias, loss_target, m_w_in, m_rel_bias, m_ln_v_gain, m_ln_v_bias, m_w_spatial, m_b_spatial, m_w_proj_a, m_w_proj_b, m_w_out, m_ln1_gain, m_ln1_bias, m_w_ff1, m_b_ff1, m_w_ff2, m_b_ff2, m_ln2_gain, m_ln2_bias, v_w_in, v_rel_bias, v_ln_v_gain, v_ln_v_bias, v_w_spatial, v_b_spatial, v_w_proj_a, v_w_proj_b, v_w_out, v_ln1_gain, v_ln1_bias, v_w_ff1, v_b_ff1, v_w_ff2, v_b_ff2, v_ln2_gain, v_ln2_bias):
    args = dict(locals())
    big = ("w_in", "w_proj_a", "w_proj_b", "w_out", "w_ff1", "w_ff2")
    weights = ("w_in", "rel_bias", "ln_v_gain", "ln_v_bias", "w_spatial", "b_spatial", "w_proj_a", "w_proj_b", "w_out",
               "ln1_gain", "ln1_bias", "w_ff1", "b_ff1", "w_ff2", "b_ff2", "ln2_gain", "ln2_bias")

    xs = x[0]
    target = loss_target[0]

    ring = {"a": {"bufs": [_place_shard(w_in[0], "place_w_in")], "sems": {}}}
    tok = _ring_call("allgather_a_near", ring, [("start", "a", "ici_near")])
    placed = [_place_shard(args[k][0], f"place_{k}", after=tok) for k in big[1:]]
    for tag, bufs in (("b", placed[0:3]), ("c", placed[3:4]), ("d", placed[4:5])):
        ring[tag] = {"bufs": bufs, "sems": {}}
    xb = _to_bf16(xs, "x_to_bf16", after=placed[4])

    mx, my = lax.axis_index("x"), lax.axis_index("y")
    own = jnp.reshape(2 * mx + my, (1,)).astype(jnp.int32)
    near = jnp.stack([2 * (1 - mx) + my, 2 * mx + (1 - my)]).astype(jnp.int32)
    far = jnp.reshape(2 * (1 - mx) + (1 - my), (1,)).astype(jnp.int32)
    proj = _proj(xb, ring["a"]["bufs"][0], own, "proj_own")
    _ring_call("allgather_a_far", ring, [("wait", "a", "ici_near"), ("start", "a", "ici_far"), ("start", "a", "d2d_near"),
                                         ("start", "b", "ici_near"), ("start", "c", "ici_near")], after=proj)
    _ring_call("allgather_a_near_done", ring, [("wait", "a", "d2d_near")])
    proj = _proj(xb, ring["a"]["bufs"][0], near, "proj_near", into=proj)
    _ring_call("allgather_a_last", ring, [("wait", "a", "ici_far"), ("start", "a", "d2d_far")], after=proj)
    _ring_call("allgather_a_done", ring, [("wait", "a", "d2d_far")])
    (win_g,) = ring["a"]["bufs"]
    proj = _proj(xb, win_g, far, "proj_far", into=proj)
    _ring_call("allgather_b_far", ring, [("wait", "b", "ici_near"), ("start", "b", "ici_far"), ("start", "b", "d2d_near")],
               after=proj)
    ws = w_spatial[0]
    ws_t = jnp.transpose(ws, (0, 2, 1))
    bsp_b = jnp.broadcast_to(b_spatial[0][:, :, None], (NH, 128, 128))
    gmlp = _gmlp_fwd(proj, ws, bsp_b, ln_v_gain, ln_v_bias)
    attn, lse = _attention_fwd(proj, rel_bias)
    _ring_call("allgather_b_last_c_far", ring,
               [("wait", "b", "ici_far"), ("start", "b", "d2d_far"),
                ("wait", "c", "ici_near"), ("start", "c", "ici_far"), ("start", "c", "d2d_near"),
                ("start", "d", "ici_near")], after=attn)
    _ring_call("allgather_b_done", ring, [("wait", "b", "d2d_near"), ("wait", "b", "d2d_far")])
    wpa_g, wpb_g, wout_g = ring["b"]["bufs"]
    wout_full = wout_g.reshape(D, D)
    ya, yb, merged = _branch(attn, gmlp, wpa_g, wpb_g, proj)
    xhat1, rstd1, h1b = _out_ln1(merged, wout_full, xs, ln1_gain, ln1_bias)
    _ring_call("allgather_c_last_d_far", ring,
               [("wait", "c", "ici_far"), ("start", "c", "d2d_far"),
                ("wait", "d", "ici_near"), ("start", "d", "ici_far"), ("start", "d", "d2d_near")], after=h1b)
    _ring_call("allgather_c_done", ring, [("wait", "c", "d2d_near"), ("wait", "c", "d2d_far")])
    (w1_g,) = ring["c"]["bufs"]
    a, r = _ff1(h1b, w1_g, b_ff1)
    _ring_call("allgather_d_last", ring, [("wait", "d", "ici_far"), ("start", "d", "d2d_far")], after=a)
    _ring_call("allgather_d_done", ring, [("wait", "d", "d2d_near"), ("wait", "d", "d2d_far")])
    (w2_g,) = ring["d"]["bufs"]
    w2_full = w2_g.reshape(DFF, D)
    dpre2, dpre2b, st2 = _ff2_ln2_loss(a, w2_full, xhat1, ln1_gain, ln1_bias, b_ff2, ln2_gain, ln2_bias, target)

    def pair_and_chip(tag, state, after):
        local, from_sibling = _px_wait(f"pair_exchange_wait_{tag}", state, after)
        pair_sums = [_pair_sum(g, o, f"pair_sum_{tag}_{i}") for i, (g, o) in enumerate(zip(local, from_sibling))]
        return _cx_start(f"chip_exchange_start_{tag}", pair_sums)

    g_w2 = _grad_w(a, dpre2b, "grad_w_ff2", 512, 2048, False)
    px, tok = _px_start("pair_exchange_start_w_ff2", [g_w2.reshape(N_CHIPS, DFF // N_CHIPS, D)])
    dprea, g_b1 = _d_ff1(dpre2b, w2_full, r, after=tok)
    cx_w2, tok = pair_and_chip("w_ff2", px, dprea)
    g_w1 = _grad_w(h1b, dprea, "grad_w_ff1", 512, 2048, True, after=tok)
    px, tok = _px_start("pair_exchange_start_w_ff1", [g_w1])
    dpre1, dpre1b, st1 = _d_h1_ln1(dprea, w1_g, dpre2, xhat1, rstd1, ln1_gain, after=tok)
    cx_w1, tok = pair_and_chip("w_ff1", px, dpre1b)
    g_wout = _grad_w(merged, dpre1b, "grad_w_out", 512, 2048, False, after=tok)
    dya, dyb, dga, dgb = _d_merged(dpre1b, wout_full, proj, ya, yb)
    g_wpa = _grad_w(attn, dya, "grad_w_proj_a", 1024, 512, True)
    g_wpb = _grad_w(gmlp, dyb, "grad_w_proj_b", 1024, 512, True)
    px, tok = _px_start("pair_exchange_start_b", [g_wpa, g_wpb, g_wout.reshape(N_CHIPS, D // N_CHIPS, D)])
    dattn, dgmlp = _d_branches(dya, dyb, wpa_g, wpb_g, after=tok)
    duv, g_ws, g_bs, stv = _gmlp_bwd(proj, dgmlp, ws, ws_t, bsp_b, ln_v_gain, ln_v_bias)
    cx_b, tok = pair_and_chip("b", px, duv)
    dq, dk, dv, ds_sums = _attention_bwd(proj, dattn, attn, lse, rel_bias, after=tok)
    g_rb = _rel_bias_grad(ds_sums)[:, :NH]

    small_g = dict(rel_bias=g_rb, ln_v_gain=stv[0], ln_v_bias=stv[1], w_spatial=g_ws, b_spatial=g_bs[:, :, 0],
                   ln1_gain=st1[0], ln1_bias=st1[1], b_ff1=g_b1, b_ff2=st2[2], ln2_gain=st2[0], ln2_bias=st2[1])
    gs = _allreduce_small(_pack_small(small_g).at[_LOSS_AT].set(st2[3, 0]))
    ds_, ms_, vs_, _ = _adamw(_pack_small({k: args[k] for k in _SMALL}), gs,
                           _pack_small({k: args["m_" + k] for k in _SMALL}),
                           _pack_small({k: args["v_" + k] for k in _SMALL}), "adamw_small")
    like = {k: args[k] for k in _SMALL}
    grads, deltas, new_m, new_v = (_unpack_small(t, like) for t in (gs, ds_, ms_, vs_))

    dproj = jnp.concatenate([dq, dk, dv, duv, dga, dgb], axis=1)
    g_win = _grad_w(xb, dproj, "grad_w_in", 512, 2304, True, after=gs)
    px, tok = _px_start("pair_exchange_start_w_in", [g_win])
    grad_x = _d_x(dproj, win_g, dpre1, after=tok)
    cx_in, tok = pair_and_chip("w_in", px, grad_x)

    def chip_sums(tag, state, names, after):
        pair_sums, from_chips = _cx_wait(f"chip_exchange_wait_{tag}", state, after)
        halves = [_chip_sum(p, own, f"chip_sum_{k}") for p, own, k in zip(from_chips, pair_sums, names)]
        return _share_start(f"share_start_{tag}", halves)

    def adam(tag, state, names, after):
        last = None
        for k, g in zip(names, _share_wait(f"share_wait_{tag}", state, after)):
            d_, m_, v_, g_ = _adamw(args[k][0], g, args["m_" + k][0], args["v_" + k][0], f"adamw_{k}")
            grads[k], deltas[k], new_m[k], new_v[k] = g_[None], d_[None], m_[None], v_[None]
            last = d_
        return last

    sh_w2, tok = chip_sums("w_ff2", cx_w2, ["w_ff2"], tok)
    sh_w1, tok = chip_sums("w_ff1", cx_w1, ["w_ff1"], tok)
    sh_b, tok = chip_sums("b", cx_b, ["w_proj_a", "w_proj_b", "w_out"], tok)
    done = adam("w_ff2", sh_w2, ["w_ff2"], tok)
    done = adam("w_ff1", sh_w1, ["w_ff1"], done)
    done = adam("b", sh_b, ["w_proj_a", "w_proj_b", "w_out"], done)
    sh_in, tok = chip_sums("w_in", cx_in, ["w_in"], done)
    adam("w_in", sh_in, ["w_in"], tok)

    loss = gs[_LOSS_AT] * (0.5 / D)
    return (loss, grad_x[None], *[grads[k] for k in weights], *[deltas[k] for k in weights],
            *[new_m[k] for k in weights], *[new_v[k] for k in weights])
```

```python
import math

import numpy as np
import jax
import jax.numpy as jnp
from jax import lax
from jax.experimental import pallas as pl
from jax.experimental.pallas import tpu as pltpu

F32 = jnp.float32
BF16 = jnp.bfloat16

S = 2048
D = 2048
DA = 1024
DB = 1024
DFF = 8192
DIN = 9216
NH = 8
HD = 128
NBLK = 16
PATTERNS = ((128, 1), (512, 4), (2048, 16))
N_BUCKETS = 32
MAX_DISTANCE = 2048
ALPHA = 2.0 ** 0.25
LN_EPS = 1e-5
NEG_INF = -1e30
SCALE = HD ** -0.5
N_CHIPS = 4

ADAM_LR = 0.001
ADAM_B1 = 0.9
ADAM_B2 = 0.999
ADAM_EPS = 1e-08
ADAM_WD = 0.01
ADAM_STEP = 10

VMEM_LIMIT = 56 * 1024 * 1024
MESH = pl.DeviceIdType.MESH
ANY = pl.BlockSpec(memory_space=pl.ANY)


def _params(n_axes, vmem=VMEM_LIMIT):
    return pltpu.CompilerParams(dimension_semantics=("arbitrary",) * n_axes, vmem_limit_bytes=vmem)


def _bucket_tile(dilation):
    qi = np.arange(128)[:, None]
    kj = np.arange(256)[None, :]
    n = np.clip(128 + qi - kj, 0, 128) * dilation
    max_exact = N_BUCKETS // 2
    nf = np.maximum(n, 1).astype(np.float32)
    large = max_exact + (np.log(nf / np.float32(max_exact)) / np.float32(math.log(MAX_DISTANCE / max_exact))
                         * np.float32(N_BUCKETS - max_exact)).astype(np.int32)
    large = np.minimum(large, N_BUCKETS - 1)
    return np.where(n < max_exact, n, large).astype(np.int32)


def _gelu(x):
    c = math.sqrt(2.0 / math.pi)
    t = jnp.tanh(c * (x + 0.044715 * x * x * x))
    return 0.5 * x * (1.0 + t), t


def _gelu_grad(x, t):
    c = math.sqrt(2.0 / math.pi)
    return 0.5 * (1.0 + t) + 0.5 * x * (1.0 - t * t) * c * (1.0 + 3.0 * 0.044715 * x * x)


def _sigmoid(x):
    return 1.0 / (1.0 + jnp.exp(-x))


def _dot(a, b):
    return jnp.dot(a, b, preferred_element_type=F32)


def _behind(body, n_in, after):
    if after is None:
        return body, [], []
    return (lambda *refs: body(*refs[:n_in], *refs[n_in + 1:])), [ANY], [after]


def _dot_nt(a, b):
    return lax.dot_general(a, b, (((1,), (1,)), ((), ())), preferred_element_type=F32)


def _proj(xb, win_g, shards, name, into=None):
    tn = 768
    per = 2304 // tn

    def body(shards_ref, x_ref, w_ref, *rest):
        rest[-1][...] = _dot(x_ref[...], w_ref[...])

    in_specs = [pl.BlockSpec((S, D), lambda j, sh: (0, 0)),
                pl.BlockSpec((None, D, tn), lambda j, sh: (sh[j // per], 0, j % per))]
    return pl.pallas_call(
        body, name=name,
        grid_spec=pltpu.PrefetchScalarGridSpec(
            num_scalar_prefetch=1, grid=(shards.shape[0] * per,),
            in_specs=in_specs + ([ANY] if into is not None else []),
            out_specs=pl.BlockSpec((S, tn), lambda j, sh: (0, sh[j // per] * per + j % per))),
        out_shape=jax.ShapeDtypeStruct((S, DIN), F32),
        input_output_aliases={3: 0} if into is not None else {},
        compiler_params=_params(1),
    )(shards, xb, win_g, *([into] if into is not None else []))


FWD_HEADS_PER_STEP = 4
BWD_HEADS_PER_STEP = 2


def _head_bias_tiles(rb_ref, bk_ref, bias_scr, first_head, hps):
    qi = lax.broadcasted_iota(jnp.int32, (128, 256), 0)
    kj = lax.broadcasted_iota(jnp.int32, (128, 256), 1)
    steps = 128 + qi - kj
    band = (steps >= 0) & (steps <= 128)
    bias_scr[...] = jnp.zeros_like(bias_scr)
    for p in range(len(PATTERNS)):
        bucket = bk_ref[p]

        def one_bucket(t, carry):
            hit = bucket == t
            for j in range(hps):
                bias_scr[p, j] = jnp.where(hit, rb_ref[t, first_head + j], bias_scr[p, j])
            return carry

        lax.fori_loop(0, N_BUCKETS, one_bucket, 0)
        for j in range(hps):
            bias_scr[p, j] = jnp.where(band, bias_scr[p, j], NEG_INF)


def _block_rows(b, dilation):
    nblk = NBLK // dilation
    r, n = b // nblk, b % nblk
    start = r + n * (128 * dilation)
    prev_start = jnp.maximum(start - 128 * dilation, r)
    if dilation == 1:
        return pl.ds(pl.multiple_of(start, 128), 128), pl.ds(pl.multiple_of(prev_start, 128), 128), n > 0
    return pl.ds(start, 128, stride=dilation), pl.ds(prev_start, 128, stride=dilation), n > 0


def _head_specs(first, hps):
    return [pl.BlockSpec((S, HD), lambda g, j=j: (0, first + g * hps + j)) for j in range(hps)]


def _heads_spec(hps):
    return pl.BlockSpec((S, hps * HD), lambda g: (0, g))


def _attention_fwd(proj, rel_bias):
    hps = FWD_HEADS_PER_STEP
    buckets = jnp.asarray(np.stack([_bucket_tile(d) for _, d in PATTERNS]))

    def body(rb_ref, bk_ref, *refs):
        q_refs, k_refs, v_refs = (refs[i * hps:(i + 1) * hps] for i in range(3))
        o_ref, lse_ref, bias_scr = refs[3 * hps:3 * hps + 3]
        acc_scrs, m_scrs, l_scrs = (refs[3 * hps + 3 + i * hps:3 * hps + 3 + (i + 1) * hps] for i in range(3))
        _head_bias_tiles(rb_ref, bk_ref, bias_scr, pl.program_id(0) * hps, hps)
        kj = lax.broadcasted_iota(jnp.int32, (128, 256), 1)
        for p, (_, d) in enumerate(PATTERNS):
            prev_blocks = NBLK // d > 1

            def block(b, carry):
                units = [(j,) + _block_rows(blk, d) for blk in (b, b + NBLK // 2) for j in range(hps)]
                scores = []
                for j, rows, prows, _ in units:
                    q = q_refs[j][rows, :].astype(BF16)
                    cur = _dot_nt(q, k_refs[j][rows, :].astype(BF16))
                    if prev_blocks:
                        cur = jnp.concatenate([_dot_nt(q, k_refs[j][prows, :].astype(BF16)), cur], axis=1)
                    scores.append(cur)
                soft = []
                for u, (j, _, _, has_prev) in enumerate(units):
                    if prev_blocks:
                        s = jnp.where((kj >= 128) | has_prev, scores[u] * SCALE + bias_scr[p, j], NEG_INF)
                    else:
                        s = scores[u] * SCALE + bias_scr[p, j, :, 128:256]
                    m = jnp.max(s, axis=1, keepdims=True)
                    e = jnp.exp(s - m)
                    soft.append((m, jnp.sum(e, axis=1, keepdims=True), e.astype(BF16)))
                outs = []
                for u, (j, rows, prows, _) in enumerate(units):
                    e = soft[u][2]
                    if prev_blocks:
                        outs.append(_dot(e[:, :128], v_refs[j][prows, :].astype(BF16))
                                    + _dot(e[:, 128:], v_refs[j][rows, :].astype(BF16)))
                    else:
                        outs.append(_dot(e, v_refs[j][rows, :].astype(BF16)))
                for u, (j, rows, _, _) in enumerate(units):
                    acc_scr, m_scr, l_scr = acc_scrs[j], m_scrs[j], l_scrs[j]
                    (m, den, _), o = soft[u], outs[u]
                    if p == 0:
                        acc_scr[rows, :] = o
                        m_scr[rows, :] = jnp.broadcast_to(m, (128, HD))
                        l_scr[rows, :] = jnp.broadcast_to(den, (128, HD))
                    else:
                        m_old = m_scr[rows, :]
                        m_new = jnp.maximum(m_old, m)
                        w_old, w_new = jnp.exp(m_old - m_new), jnp.exp(m - m_new)
                        acc_scr[rows, :] = acc_scr[rows, :] * w_old + o * w_new
                        l_scr[rows, :] = l_scr[rows, :] * w_old + den * w_new
                        m_scr[rows, :] = m_new
                return carry

            lax.fori_loop(0, NBLK // 2, block, 0)
        for j in range(hps):
            cols = slice(j * HD, (j + 1) * HD)
            den = l_scrs[j][...]
            o_ref[:, cols] = (acc_scrs[j][...] / den).astype(BF16)
            lse_ref[:, cols] = m_scrs[j][...] + jnp.log(den)

    return pl.pallas_call(
        body, name="attention_fwd", grid=(NH // hps,),
        in_specs=[pl.BlockSpec(memory_space=pltpu.SMEM), pl.BlockSpec((3, 128, 256), lambda g: (0, 0, 0))]
        + _head_specs(0, hps) + _head_specs(NH, hps) + _head_specs(2 * NH, hps),
        out_specs=[_heads_spec(hps), _heads_spec(hps)],
        out_shape=[jax.ShapeDtypeStruct((S, DA), BF16), jax.ShapeDtypeStruct((S, DA), F32)],
        scratch_shapes=[pltpu.VMEM((3, hps, 128, 256), F32)] + [pltpu.VMEM((S, HD), F32)] * (3 * hps),
        compiler_params=_params(1),
    )(rel_bias, buckets, *([proj] * (3 * hps)))


def _attention_bwd(proj, dattn, attn, lse, rel_bias, after=None):
    hps = BWD_HEADS_PER_STEP

    def body(rb_ref, bk_ref, *refs):
        q_refs, k_refs, v_refs, do_refs, o_refs, lse_refs = (refs[i * hps:(i + 1) * hps] for i in range(6))
        dq_ref, dk_ref, dv_ref, ds_ref, bias_scr = refs[6 * hps:6 * hps + 5]
        dl_scrs, dq_scrs, dk_scrs, dv_scrs = (refs[6 * hps + 5 + i * hps:6 * hps + 5 + (i + 1) * hps] for i in range(4))
        _head_bias_tiles(rb_ref, bk_ref, bias_scr, pl.program_id(0) * hps, hps)
        ds_ref[...] = jnp.zeros_like(ds_ref)
        for j in range(hps):
            dq_scrs[j][...] = jnp.zeros((S, HD), F32)
            dk_scrs[j][...] = jnp.zeros((S, HD), F32)
            dv_scrs[j][...] = jnp.zeros((S, HD), F32)
            prod = do_refs[j][...] * o_refs[j][...].astype(F32)
            dl_scrs[j][...] = jnp.broadcast_to(jnp.sum(prod, axis=1, keepdims=True), (S, HD))
        for p, (_, d) in enumerate(PATTERNS):
            prev_blocks = NBLK // d > 1

            def block(b, carry):
                units = [(j,) + _block_rows(blk, d) for blk in (b, b + NBLK // 2) for j in range(hps)]
                ops, raw = [], []
                for j, rows, prows, _ in units:
                    q, do = q_refs[j][rows, :].astype(BF16), do_refs[j][rows, :].astype(BF16)
                    kc, vc = k_refs[j][rows, :].astype(BF16), v_refs[j][rows, :].astype(BF16)
                    if prev_blocks:
                        kp, vp = k_refs[j][prows, :].astype(BF16), v_refs[j][prows, :].astype(BF16)
                        ops.append((q, do, kc, kp))
                        raw.append((_dot_nt(q, kc), _dot_nt(do, vc), _dot_nt(q, kp), _dot_nt(do, vp)))
                    else:
                        ops.append((q, do, kc))
                        raw.append((_dot_nt(q, kc), _dot_nt(do, vc)))
                probs = []
                for u, (j, rows, _, has_prev) in enumerate(units):
                    lse_b, dl_b = lse_refs[j][rows, :], dl_scrs[j][rows, :]
                    p_c = jnp.exp(raw[u][0] * SCALE + bias_scr[p, j, :, 128:256] - lse_b)
                    ds_c = p_c * (raw[u][1] - dl_b)
                    ds_ref[p, j, :, 128:256] += ds_c
                    if prev_blocks:
                        p_p = jnp.where(has_prev, jnp.exp(raw[u][2] * SCALE + bias_scr[p, j, :, 0:128] - lse_b), 0.0)
                        ds_p = p_p * (raw[u][3] - dl_b)
                        ds_ref[p, j, :, 0:128] += ds_p
                        probs.append((p_c, ds_c, p_p, ds_p))
                    else:
                        probs.append((p_c, ds_c))
                grads = []
                for u in range(len(units)):
                    q, do, kc = ops[u][:3]
                    p_c, ds_c = probs[u][:2]
                    dq = _dot(ds_c.astype(BF16), kc)
                    cur = (_dot(ds_c.T.astype(BF16), q) * SCALE, _dot(p_c.T.astype(BF16), do))
                    if prev_blocks:
                        p_p, ds_p = probs[u][2:]
                        dq = dq + _dot(ds_p.astype(BF16), ops[u][3])
                        cur = cur + (_dot(ds_p.T.astype(BF16), q) * SCALE, _dot(p_p.T.astype(BF16), do))
                    grads.append((dq * SCALE,) + cur)
                for u, (j, rows, prows, _) in enumerate(units):
                    dq_scrs[j][rows, :] += grads[u][0]
                    dk_scrs[j][rows, :] += grads[u][1]
                    dv_scrs[j][rows, :] += grads[u][2]
                    if prev_blocks:
                        dk_scrs[j][prows, :] += grads[u][3]
                        dv_scrs[j][prows, :] += grads[u][4]
                return carry

            lax.fori_loop(0, NBLK // 2, block, 0)
        for j in range(hps):
            cols = slice(j * HD, (j + 1) * HD)
            dq_ref[:, cols] = dq_scrs[j][...].astype(BF16)
            dk_ref[:, cols] = dk_scrs[j][...].astype(BF16)
            dv_ref[:, cols] = dv_scrs[j][...].astype(BF16)

    buckets = jnp.asarray(np.stack([_bucket_tile(d) for _, d in PATTERNS]))
    body, more_specs, more = _behind(body, 2 + 6 * hps, after)
    return pl.pallas_call(
        body, name="attention_bwd", grid=(NH // hps,),
        in_specs=[pl.BlockSpec(memory_space=pltpu.SMEM), pl.BlockSpec((3, 128, 256), lambda g: (0, 0, 0))]
        + _head_specs(0, hps) + _head_specs(NH, hps) + _head_specs(2 * NH, hps) + 3 * _head_specs(0, hps)
        + more_specs,
        out_specs=3 * [_heads_spec(hps)] + [pl.BlockSpec((3, hps, 128, 256), lambda g: (0, g, 0, 0))],
        out_shape=[jax.ShapeDtypeStruct((S, DA), BF16)] * 3 + [jax.ShapeDtypeStruct((3, NH, 128, 256), F32)],
        scratch_shapes=[pltpu.VMEM((3, hps, 128, 256), F32)] + [pltpu.VMEM((S, HD), F32)] * (4 * hps),
        compiler_params=_params(1),
    )(rel_bias, buckets, *([proj] * (3 * hps)), *([dattn] * hps), *([attn] * hps), *([lse] * hps), *more)


def _gmlp_parts(u_ref, vb_ref, g_ref, be_ref):
    u = u_ref[...]
    u_act, tu = _gelu(u)
    vb = vb_ref[...]
    gv, tv = _gelu(vb)
    mean = jnp.mean(gv, axis=1, keepdims=True)
    cen = gv - mean
    var = jnp.mean(cen * cen, axis=1, keepdims=True)
    rstd = lax.rsqrt(var + LN_EPS)
    xhat = cen * rstd
    vn = xhat * g_ref[...] + be_ref[...]
    return u, tu, u_act, vb, tv, rstd, xhat, vn


def _gmlp_fwd(proj, ws, bsp_b, gain_v, bias_v):
    def body(u_ref, vb_ref, ws_ref, bsp_ref, g_ref, be_ref, o_ref):
        _, _, u_act, _, _, _, _, vn = _gmlp_parts(u_ref, vb_ref, g_ref, be_ref)
        row = lax.broadcasted_iota(jnp.int32, (128, 128), 0)
        col = lax.broadcasted_iota(jnp.int32, (128, 128), 1)
        causal = row >= col
        for g in range(NH):
            cols = slice(g * 128, (g + 1) * 128)
            wsg = jnp.where(causal, ws_ref[g], 0.0).astype(BF16)
            z = _dot(wsg, vn[:, cols].astype(BF16)) + bsp_ref[g]
            o_ref[:, cols] = (u_act[:, cols] * z).astype(BF16)

    return pl.pallas_call(
        body, name="gmlp_fwd", grid=(NBLK,),
        in_specs=[pl.BlockSpec((128, DB), lambda c: (c, 3)), pl.BlockSpec((128, DB), lambda c: (c, 4)),
                  pl.BlockSpec((NH, 128, 128), lambda c: (0, 0, 0)), pl.BlockSpec((NH, 128, 128), lambda c: (0, 0, 0)),
                  pl.BlockSpec((1, DB), lambda c: (0, 0)), pl.BlockSpec((1, DB), lambda c: (0, 0))],
        out_specs=pl.BlockSpec((128, DB), lambda c: (c, 0)),
        out_shape=jax.ShapeDtypeStruct((S, DB), BF16),
        compiler_params=_params(1),
    )(proj, proj, ws, bsp_b, gain_v, bias_v)


def _branch(attn, gmlp, wpa_g, wpb_g, proj):
    tn = 512

    def body(a_ref, g_ref, wa_ref, wb_ref, ga_ref, gb_ref, ya_ref, yb_ref, mg_ref):
        ya = _dot(a_ref[...], wa_ref[...])
        yb = _dot(g_ref[...], wb_ref[...])
        ya_ref[...] = ya.astype(BF16)
        yb_ref[...] = yb.astype(BF16)
        mg_ref[...] = (_sigmoid(ga_ref[...]) * ya + _sigmoid(gb_ref[...]) * yb).astype(BF16)

    out = pl.BlockSpec((S, tn), lambda j: (0, j))
    return pl.pallas_call(
        body, name="branch", grid=(D // tn,),
        in_specs=[pl.BlockSpec((S, DA), lambda j: (0, 0)), pl.BlockSpec((S, DB), lambda j: (0, 0)),
                  pl.BlockSpec((None, DA, tn), lambda j: (j, 0, 0)), pl.BlockSpec((None, DB, tn), lambda j: (j, 0, 0)),
                  pl.BlockSpec((S, tn), lambda j: (0, 5120 // tn + j)), pl.BlockSpec((S, tn), lambda j: (0, 7168 // tn + j))],
        out_specs=[out, out, out],
        out_shape=[jax.ShapeDtypeStruct((S, D), BF16)] * 3,
        compiler_params=_params(1),
    )(attn, gmlp, wpa_g, wpb_g, proj, proj)


def _out_ln1(merged, wout_g, x, gain, bias):
    tm = 256

    def body(m_ref, w_ref, x_ref, g_ref, b_ref, xh_ref, rs_ref, h_ref):
        pre = ALPHA * x_ref[...] + _dot(m_ref[...], w_ref[...])
        mean = jnp.mean(pre, axis=1, keepdims=True)
        cen = pre - mean
        var = jnp.mean(cen * cen, axis=1, keepdims=True)
        rstd = lax.rsqrt(var + LN_EPS)
        xhat = cen * rstd
        xh_ref[...] = xhat
        rs_ref[...] = jnp.broadcast_to(rstd, (tm, 128))
        h_ref[...] = (xhat * g_ref[...] + b_ref[...]).astype(BF16)

    row = pl.BlockSpec((tm, D), lambda i: (i, 0))
    vec = pl.BlockSpec((1, D), lambda i: (0, 0))
    return pl.pallas_call(
        body, name="out_ln1", grid=(S // tm,),
        in_specs=[row, pl.BlockSpec((D, D), lambda i: (0, 0)), row, vec, vec],
        out_specs=[row, pl.BlockSpec((tm, 128), lambda i: (i, 0)), row],
        out_shape=[jax.ShapeDtypeStruct((S, D), F32), jax.ShapeDtypeStruct((S, 128), F32),
                   jax.ShapeDtypeStruct((S, D), BF16)],
        compiler_params=_params(1),
    )(merged, wout_g, x, gain, bias)


def _ff1(h1b, w1_g, b1):
    tn = 512
    per = D // tn

    def body(h_ref, w_ref, b_ref, a_ref, r_ref):
        r = jnp.maximum(_dot(h_ref[...], w_ref[...]) + b_ref[...], 0.0)
        r_ref[...] = r.astype(BF16)
        a_ref[...] = (r * r).astype(BF16)

    out = pl.BlockSpec((S, tn), lambda j: (0, j))
    return pl.pallas_call(
        body, name="ff1", grid=(DFF // tn,),
        in_specs=[pl.BlockSpec((S, D), lambda j: (0, 0)),
                  pl.BlockSpec((None, D, tn), lambda j: (j // per, 0, j % per)),
                  pl.BlockSpec((1, tn), lambda j: (0, j))],
        out_specs=[out, out],
        out_shape=[jax.ShapeDtypeStruct((S, DFF), BF16)] * 2,
        compiler_params=_params(1),
    )(h1b, w1_g, b1)


def _ff2_ln2_loss(a, w2_g, xhat1, g1, b1, b2, g2, be2, target):
    tm, tk = 512, 1024
    nk = DFF // tk

    def body(a_ref, w_ref, xh_ref, g1_ref, b1_ref, b2_ref, g2_ref, be2_ref, t_ref, d_ref, db_ref, st_ref, acc):
        i, k = pl.program_id(0), pl.program_id(1)

        @pl.when(k == 0)
        def _():
            acc[...] = jnp.zeros_like(acc)

        @pl.when((i == 0) & (k == 0))
        def _():
            st_ref[...] = jnp.zeros_like(st_ref)

        acc[...] += _dot(a_ref[...], w_ref[...])

        @pl.when(k == nk - 1)
        def _():
            def rows_chunk(ci, carry):
                rows = pl.ds(pl.multiple_of(ci * 128, 128), 128)
                h1 = xh_ref[rows, :] * g1_ref[...] + b1_ref[...]
                pre = ALPHA * h1 + acc[rows, :] + b2_ref[...]
                mean = jnp.mean(pre, axis=1, keepdims=True)
                cen = pre - mean
                var = jnp.mean(cen * cen, axis=1, keepdims=True)
                rstd = lax.rsqrt(var + LN_EPS)
                xhat = cen * rstd
                y = xhat * g2_ref[...] + be2_ref[...]
                err = y - t_ref[rows, :]
                dy = err * (1.0 / D)
                g = dy * g2_ref[...]
                dpre = rstd * (g - jnp.mean(g, axis=1, keepdims=True)
                               - xhat * jnp.mean(g * xhat, axis=1, keepdims=True))
                d_ref[rows, :] = dpre
                db_ref[rows, :] = dpre.astype(BF16)
                st_ref[0:1, :] += jnp.sum(dy * xhat, axis=0, keepdims=True)
                st_ref[1:2, :] += jnp.sum(dy, axis=0, keepdims=True)
                st_ref[2:3, :] += jnp.sum(dpre, axis=0, keepdims=True)
                st_ref[3:4, :] += jnp.broadcast_to(jnp.sum(err * err).reshape(1, 1), (1, D))
                return carry

            lax.fori_loop(0, tm // 128, rows_chunk, 0)

    row = pl.BlockSpec((tm, D), lambda i, k: (i, 0))
    vec = pl.BlockSpec((1, D), lambda i, k: (0, 0))
    return pl.pallas_call(
        body, name="ff2_ln2_loss", grid=(S // tm, nk),
        in_specs=[pl.BlockSpec((tm, tk), lambda i, k: (i, k)), pl.BlockSpec((tk, D), lambda i, k: (k, 0)),
                  row, vec, vec, vec, vec, vec, row],
        out_specs=[row, row, pl.BlockSpec((8, D), lambda i, k: (0, 0))],
        out_shape=[jax.ShapeDtypeStruct((S, D), F32), jax.ShapeDtypeStruct((S, D), BF16),
                   jax.ShapeDtypeStruct((8, D), F32)],
        scratch_shapes=[pltpu.VMEM((tm, D), F32)],
        compiler_params=_params(2),
    )(a, w2_g, xhat1, g1, b1, b2, g2, be2, target)


def _grad_w(act, dout, name, ti, tj, sharded, after=None):
    m, n = act.shape[1], dout.shape[1]
    ns = n // N_CHIPS
    per = ns // tj if sharded else None

    def body(a_ref, b_ref, o_ref, at_scr):
        @pl.when(pl.program_id(1) == 0)
        def _():
            at_scr[...] = a_ref[...].T

        o_ref[...] = _dot(at_scr[...], b_ref[...])

    if sharded:
        out_spec = pl.BlockSpec((None, ti, tj), lambda i, j: (j // per, i, j % per))
        out_shape = jax.ShapeDtypeStruct((N_CHIPS, m, ns), F32)
    else:
        out_spec = pl.BlockSpec((ti, tj), lambda i, j: (i, j))
        out_shape = jax.ShapeDtypeStruct((m, n), F32)
    body, more_specs, more = _behind(body, 2, after)
    return pl.pallas_call(
        body, name=name, grid=(m // ti, n // tj),
        in_specs=[pl.BlockSpec((S, ti), lambda i, j: (0, i)), pl.BlockSpec((S, tj), lambda i, j: (0, j))] + more_specs,
        out_specs=out_spec, out_shape=out_shape,
        scratch_shapes=[pltpu.VMEM((ti, S), BF16)],
        compiler_params=_params(2),
    )(act, dout, *more)


def _d_ff1(dpre2b, w2_g, r, after=None):
    tn = 512

    def body(d_ref, w_ref, r_ref, o_ref, gb_ref):
        da = _dot_nt(d_ref[...], w_ref[...])
        dp = da * (2.0 * r_ref[...].astype(F32))
        o_ref[...] = dp.astype(BF16)
        gb_ref[...] = jnp.sum(dp, axis=0, keepdims=True)

    body, more_specs, more = _behind(body, 3, after)
    return pl.pallas_call(
        body, name="d_ff1", grid=(DFF // tn,),
        in_specs=[pl.BlockSpec((S, D), lambda j: (0, 0)), pl.BlockSpec((tn, D), lambda j: (j, 0)),
                  pl.BlockSpec((S, tn), lambda j: (0, j))] + more_specs,
        out_specs=[pl.BlockSpec((S, tn), lambda j: (0, j)), pl.BlockSpec((1, tn), lambda j: (0, j))],
        out_shape=[jax.ShapeDtypeStruct((S, DFF), BF16), jax.ShapeDtypeStruct((1, DFF), F32)],
        compiler_params=_params(1),
    )(dpre2b, w2_g, r, *more)


def _d_h1_ln1(dprea, w1_g, dpre2, xhat1, rstd1, g1, after=None):
    tm, tk = 512, 1024
    per = D // tk
    nk = DFF // tk

    def body(a_ref, w_ref, d2_ref, xh_ref, rs_ref, g_ref, d_ref, db_ref, st_ref, acc):
        i, k = pl.program_id(0), pl.program_id(1)

        @pl.when(k == 0)
        def _():
            acc[...] = jnp.zeros_like(acc)

        @pl.when((i == 0) & (k == 0))
        def _():
            st_ref[...] = jnp.zeros_like(st_ref)

        acc[...] += _dot_nt(a_ref[...], w_ref[...])

        @pl.when(k == nk - 1)
        def _():
            def rows_chunk(ci, carry):
                rows = pl.ds(pl.multiple_of(ci * 128, 128), 128)
                dh = ALPHA * d2_ref[rows, :] + acc[rows, :]
                xhat = xh_ref[rows, :]
                g = dh * g_ref[...]
                dpre = rs_ref[rows, 0:1] * (g - jnp.mean(g, axis=1, keepdims=True)
                                            - xhat * jnp.mean(g * xhat, axis=1, keepdims=True))
                d_ref[rows, :] = dpre
                db_ref[rows, :] = dpre.astype(BF16)
                st_ref[0:1, :] += jnp.sum(dh * xhat, axis=0, keepdims=True)
                st_ref[1:2, :] += jnp.sum(dh, axis=0, keepdims=True)
                return carry

            lax.fori_loop(0, tm // 128, rows_chunk, 0)

    row = pl.BlockSpec((tm, D), lambda i, k: (i, 0))
    body, more_specs, more = _behind(body, 6, after)
    return pl.pallas_call(
        body, name="d_h1_ln1", grid=(S // tm, nk),
        in_specs=[pl.BlockSpec((tm, tk), lambda i, k: (i, k)),
                  pl.BlockSpec((None, D, tk), lambda i, k: (k // per, 0, k % per)),
                  row, row, pl.BlockSpec((tm, 128), lambda i, k: (i, 0)), pl.BlockSpec((1, D), lambda i, k: (0, 0))]
        + more_specs,
        out_specs=[row, row, pl.BlockSpec((8, D), lambda i, k: (0, 0))],
        out_shape=[jax.ShapeDtypeStruct((S, D), F32), jax.ShapeDtypeStruct((S, D), BF16),
                   jax.ShapeDtypeStruct((8, D), F32)],
        scratch_shapes=[pltpu.VMEM((tm, D), F32)],
        compiler_params=_params(2),
    )(dprea, w1_g, dpre2, xhat1, rstd1, g1, *more)


def _d_merged(dpre1b, wout_g, proj, ya, yb):
    tm, tn = 512, 1024

    def body(d_ref, w_ref, ga_ref, gb_ref, ya_ref, yb_ref, dya_ref, dyb_ref, dga_ref, dgb_ref):
        dm = _dot_nt(d_ref[...], w_ref[...])
        sa = _sigmoid(ga_ref[...])
        sb = _sigmoid(gb_ref[...])
        dya_ref[...] = (dm * sa).astype(BF16)
        dyb_ref[...] = (dm * sb).astype(BF16)
        dga_ref[...] = (dm * ya_ref[...].astype(F32) * sa * (1.0 - sa)).astype(BF16)
        dgb_ref[...] = (dm * yb_ref[...].astype(F32) * sb * (1.0 - sb)).astype(BF16)

    tile = pl.BlockSpec((tm, tn), lambda i, j: (i, j))
    return pl.pallas_call(
        body, name="d_merged", grid=(S // tm, D // tn),
        in_specs=[pl.BlockSpec((tm, D), lambda i, j: (i, 0)), pl.BlockSpec((tn, D), lambda i, j: (j, 0)),
                  pl.BlockSpec((tm, tn), lambda i, j: (i, 5 + j)), pl.BlockSpec((tm, tn), lambda i, j: (i, 7 + j)),
                  tile, tile],
        out_specs=[tile] * 4,
        out_shape=[jax.ShapeDtypeStruct((S, D), BF16)] * 4,
        compiler_params=_params(2),
    )(dpre1b, wout_g, proj, proj, ya, yb)


def _d_branches(dya, dyb, wpa_g, wpb_g, after=None):
    tk = 512

    def body(da_ref, db_ref, wa_ref, wb_ref, oa_ref, ob_ref):
        @pl.when(pl.program_id(0) == 0)
        def _():
            oa_ref[...] = jnp.zeros_like(oa_ref)
            ob_ref[...] = jnp.zeros_like(ob_ref)

        oa_ref[...] += _dot_nt(da_ref[...], wa_ref[...])
        ob_ref[...] += _dot_nt(db_ref[...], wb_ref[...])

    body, more_specs, more = _behind(body, 4, after)
    return pl.pallas_call(
        body, name="d_branches", grid=(D // tk,),
        in_specs=[pl.BlockSpec((S, tk), lambda k: (0, k)), pl.BlockSpec((S, tk), lambda k: (0, k)),
                  pl.BlockSpec((None, DA, tk), lambda k: (k, 0, 0)), pl.BlockSpec((None, DB, tk), lambda k: (k, 0, 0))]
        + more_specs,
        out_specs=[pl.BlockSpec((S, DA), lambda k: (0, 0)), pl.BlockSpec((S, DB), lambda k: (0, 0))],
        out_shape=[jax.ShapeDtypeStruct((S, DA), F32), jax.ShapeDtypeStruct((S, DB), F32)],
        compiler_params=_params(1),
    )(dya, dyb, wpa_g, wpb_g, *more)


def _gmlp_bwd(proj, dgmlp, ws, ws_t, bsp_b, gain_v, bias_v):
    def body(u_ref, vb_ref, dg_ref, ws_ref, wst_ref, bsp_ref, g_ref, be_ref, duv_ref, gws_ref, gbs_ref, st_ref):
        @pl.when(pl.program_id(0) == 0)
        def _():
            gws_ref[...] = jnp.zeros_like(gws_ref)
            gbs_ref[...] = jnp.zeros_like(gbs_ref)
            st_ref[...] = jnp.zeros_like(st_ref)

        u, tu, u_act, vb, tv, rstd, xhat, vn = _gmlp_parts(u_ref, vb_ref, g_ref, be_ref)
        dg = dg_ref[...]
        dz = dg * u_act
        row = lax.broadcasted_iota(jnp.int32, (128, 128), 0)
        col = lax.broadcasted_iota(jnp.int32, (128, 128), 1)
        causal = row >= col
        causal_t = row <= col
        dvn_parts = []
        z_parts = []
        for g in range(NH):
            cols = slice(g * 128, (g + 1) * 128)
            vng = vn[:, cols].astype(BF16)
            dzg = dz[:, cols]
            dzb = dzg.astype(BF16)
            wsg = jnp.where(causal, ws_ref[g], 0.0).astype(BF16)
            wsg_t = jnp.where(causal_t, wst_ref[g], 0.0).astype(BF16)
            z_parts.append(_dot(wsg, vng) + bsp_ref[g])
            gws_ref[g] += jnp.where(causal, _dot_nt(dzb, vng), 0.0)
            gbs_ref[g] += jnp.broadcast_to(jnp.sum(dzg, axis=1, keepdims=True), (128, 128))
            dvn_parts.append(_dot(wsg_t, dzb))
        z = jnp.concatenate(z_parts, axis=1)
        dvn = jnp.concatenate(dvn_parts, axis=1)
        du = dg * z * _gelu_grad(u, tu)
        st_ref[0:1, :] += jnp.sum(dvn * xhat, axis=0, keepdims=True)
        st_ref[1:2, :] += jnp.sum(dvn, axis=0, keepdims=True)
        gg = dvn * g_ref[...]
        dgv = rstd * (gg - jnp.mean(gg, axis=1, keepdims=True) - xhat * jnp.mean(gg * xhat, axis=1, keepdims=True))
        dvb = dgv * _gelu_grad(vb, tv)
        duv_ref[:, 0:DB] = du.astype(BF16)
        duv_ref[:, DB:2 * DB] = dvb.astype(BF16)

    full3 = pl.BlockSpec((NH, 128, 128), lambda c: (0, 0, 0))
    vec = pl.BlockSpec((1, DB), lambda c: (0, 0))
    return pl.pallas_call(
        body, name="gmlp_bwd", grid=(NBLK,),
        in_specs=[pl.BlockSpec((128, DB), lambda c: (c, 3)), pl.BlockSpec((128, DB), lambda c: (c, 4)),
                  pl.BlockSpec((128, DB), lambda c: (c, 0)), full3, full3, full3, vec, vec],
        out_specs=[pl.BlockSpec((128, 2 * DB), lambda c: (c, 0)), full3, full3, pl.BlockSpec((8, DB), lambda c: (0, 0))],
        out_shape=[jax.ShapeDtypeStruct((S, 2 * DB), BF16), jax.ShapeDtypeStruct((NH, 128, 128), F32),
                   jax.ShapeDtypeStruct((NH, 128, 128), F32), jax.ShapeDtypeStruct((8, DB), F32)],
        compiler_params=_params(1),
    )(proj, proj, dgmlp, ws, ws_t, bsp_b, gain_v, bias_v)


def _rel_bias_grad(ds_sums):
    buckets = jnp.asarray(np.stack([_bucket_tile(d) for _, d in PATTERNS]))

    def body(bk_ref, ds_ref, o_ref):
        row = lax.broadcasted_iota(jnp.int32, (N_BUCKETS, 128), 0)
        lane = lax.broadcasted_iota(jnp.int32, (N_BUCKETS, 128), 1)

        def one_bucket(t, out):
            hits = [bk_ref[p] == t for p in range(3)]
            for h in range(NH):
                tot = jnp.zeros((128, 256), F32)
                for p in range(3):
                    tot = tot + jnp.where(hits[p], ds_ref[p, h], 0.0)
                out = jnp.where((row == t) & (lane == h), jnp.sum(tot), out)
            return out

        o_ref[...] = lax.fori_loop(0, N_BUCKETS, one_bucket, jnp.zeros((N_BUCKETS, 128), F32))

    return pl.pallas_call(
        body, name="rel_bias_grad",
        in_specs=[pl.BlockSpec(memory_space=pltpu.VMEM)] * 2, out_specs=pl.BlockSpec(memory_space=pltpu.VMEM),
        out_shape=jax.ShapeDtypeStruct((N_BUCKETS, 128), F32),
        compiler_params=pltpu.CompilerParams(vmem_limit_bytes=VMEM_LIMIT),
    )(buckets, ds_sums)


def _d_x(dproj, win_g, dpre1, after=None):
    tm, tk = 512, 2304
    per = 2304 // tk
    nk = DIN // tk

    def body(a_ref, w_ref, d_ref, o_ref, acc):
        k = pl.program_id(1)

        @pl.when(k == 0)
        def _():
            acc[...] = ALPHA * d_ref[...]

        acc[...] += _dot_nt(a_ref[...], w_ref[...])

        @pl.when(k == nk - 1)
        def _():
            o_ref[...] = acc[...]

    row = pl.BlockSpec((tm, D), lambda i, k: (i, 0))
    body, more_specs, more = _behind(body, 3, after)
    return pl.pallas_call(
        body, name="d_x", grid=(S // tm, nk),
        in_specs=[pl.BlockSpec((tm, tk), lambda i, k: (i, k)),
                  pl.BlockSpec((None, D, tk), lambda i, k: (k // per, 0, k % per)), row] + more_specs,
        out_specs=row, out_shape=jax.ShapeDtypeStruct((S, D), F32),
        scratch_shapes=[pltpu.VMEM((tm, D), F32)],
        compiler_params=_params(2),
    )(dproj, win_g, dpre1, *more)


def _adamw(w, g, m, v, name):
    rows, cols = w.shape
    tm = max(t for t in range(8, 257, 8) if rows % t == 0)

    def body(w_ref, g_ref, m_ref, v_ref, d_ref, nm_ref, nv_ref, go_ref):
        g = g_ref[...]
        m = ADAM_B1 * m_ref[...] + (1.0 - ADAM_B1) * g
        v = ADAM_B2 * v_ref[...] + (1.0 - ADAM_B2) * (g * g)
        m_hat = m / (1.0 - ADAM_B1 ** ADAM_STEP)
        v_hat = v / (1.0 - ADAM_B2 ** ADAM_STEP)
        d_ref[...] = -ADAM_LR * (m_hat / (jnp.sqrt(v_hat) + ADAM_EPS) + ADAM_WD * w_ref[...])
        nm_ref[...] = m
        nv_ref[...] = v
        go_ref[...] = g

    spec = pl.BlockSpec((tm, cols), lambda i: (i, 0))
    return pl.pallas_call(
        body, name=name, grid=(rows // tm,), in_specs=[spec] * 4, out_specs=[spec] * 4,
        out_shape=[jax.ShapeDtypeStruct((rows, cols), F32)] * 4, compiler_params=_params(1),
    )(w, g, m, v)


def _position():
    x, y, c = lax.axis_index("x"), lax.axis_index("y"), lax.axis_index("c")
    chips = [(1 - x, y), (x, 1 - y), (1 - x, 1 - y)]
    return x, y, c, chips


def _remote(src, dst, send_sems, recv_sems, k, to):
    return pltpu.make_async_remote_copy(src_ref=src, dst_ref=dst, send_sem=send_sems.at[k], recv_sem=recv_sems.at[k],
                                        device_id=to, device_id_type=MESH)


def _place_shard(w, name, after=None):
    rows, cols = w.shape
    tm = 256
    x, y = lax.axis_index("x"), lax.axis_index("y")

    def body(chip_ref, w_ref, o_ref):
        o_ref[...] = w_ref[...].astype(BF16)

    more_specs, more = ([ANY], [after]) if after is not None else ([], [])
    if after is not None:
        inner = body
        body = lambda chip_ref, w_ref, after_ref, o_ref: inner(chip_ref, w_ref, o_ref)
    return pl.pallas_call(
        body, name=name,
        grid_spec=pltpu.PrefetchScalarGridSpec(
            num_scalar_prefetch=1, grid=(rows // tm,),
            in_specs=[pl.BlockSpec((tm, cols), lambda i, chip: (i, 0))] + more_specs,
            out_specs=pl.BlockSpec((None, tm, cols), lambda i, chip: (chip[0], i, 0))),
        out_shape=jax.ShapeDtypeStruct((N_CHIPS, rows, cols), BF16),
        compiler_params=_params(1),
    )(jnp.reshape(2 * x + y, (1,)).astype(jnp.int32), w, *more)


def _to_bf16(x, name, after=None):
    tm = 256

    def body(x_ref, o_ref):
        o_ref[...] = x_ref[...].astype(BF16)

    spec = pl.BlockSpec((tm, x.shape[1]), lambda i: (i, 0))
    body, more_specs, more = _behind(body, 1, after)
    return pl.pallas_call(
        body, name=name, grid=(x.shape[0] // tm,), in_specs=[spec] + more_specs, out_specs=spec,
        out_shape=jax.ShapeDtypeStruct(x.shape, BF16), compiler_params=_params(1),
    )(x, *more)


HBM = pl.BlockSpec(memory_space=pltpu.HBM)
SEM = pl.BlockSpec(memory_space=pltpu.SEMAPHORE)
EFFECT = pltpu.SideEffectType.DATAFLOW_SIDE_EFFECTING


def _comm_call(name, body, bufs, sems_in, sems_out, after=None, token=False):
    nb, ns, no = len(bufs), len(sems_in), len(sems_out)
    n_in = nb + ns + (after is not None)

    def wrapped(*refs):
        body(refs[:nb], refs[nb:nb + ns], refs[n_in + nb:n_in + nb + no])
        if token:
            refs[-1][...] = jnp.zeros((8, 128), F32)

    outs = pl.pallas_call(
        wrapped, name=name,
        in_specs=[HBM] * nb + [SEM] * ns + ([ANY] if after is not None else []),
        out_specs=[HBM] * nb + [SEM] * no + ([pl.BlockSpec(memory_space=pltpu.VMEM)] if token else []),
        out_shape=[pltpu.HBM(b.shape, b.dtype) for b in bufs] + [pltpu.SemaphoreType.DMA((k,)) for k in sems_out]
        + ([jax.ShapeDtypeStruct((8, 128), F32)] if token else []),
        input_output_aliases={i: i for i in range(nb)},
        compiler_params=pltpu.CompilerParams(has_side_effects=EFFECT),
    )(*[pltpu.with_memory_space_constraint(b, pltpu.HBM) for b in bufs], *sems_in, *([after] if after is not None else []))
    return list(outs[:nb]), list(outs[nb:nb + no]), (outs[-1] if token else None)


RING_STAGES = {"ici_near": 2, "ici_far": 2, "d2d_near": 2, "d2d_far": 1}


def _ring_copies(buf, send_sems, recv_sems, k0, stage):
    x, y, c, _ = _position()
    hr = buf.shape[1] // 2
    qr = hr // 2
    half = lambda chip, h: buf.at[chip, pl.ds(h * hr, hr), :]
    quarter = lambda chip, h, q: buf.at[chip, pl.ds(h * hr + q * qr, qr), :]
    mine, x_chip, y_chip, far_chip = 2 * x + y, 2 * (1 - x) + y, 2 * x + (1 - y), 2 * (1 - x) + (1 - y)
    to_x, to_y, sibling = (1 - x, y, c), (x, 1 - y, c), (x, y, 1 - c)
    if stage == "ici_near":
        moves = [(half(mine, c), to_x, half(x_chip, c)), (half(mine, c), to_y, half(y_chip, c))]
    elif stage == "ici_far":
        moves = [(quarter(x_chip, c, 0), to_y, quarter(far_chip, c, 0)),
                 (quarter(y_chip, c, 1), to_x, quarter(far_chip, c, 1))]
    elif stage == "d2d_near":
        moves = [(half(x_chip, c), sibling, half(x_chip, 1 - c)), (half(y_chip, c), sibling, half(y_chip, 1 - c))]
    else:
        moves = [(half(far_chip, c), sibling, half(far_chip, 1 - c))]
    sends = [_remote(src, src, send_sems, recv_sems, k0 + i, to) for i, (src, to, _) in enumerate(moves)]
    arrivals = [_remote(got, got, send_sems, recv_sems, k0 + i, (x, y, c)) for i, (_, _, got) in enumerate(moves)]
    return sends, arrivals


def _ring_call(name, groups, actions, after=None):
    tags = list(dict.fromkeys(t for _, t, _ in actions))
    counts = {t: len(groups[t]["bufs"]) for t in tags}
    first = {t: sum(counts[u] for u in tags[:i]) for i, t in enumerate(tags)}
    waits = [(t, s) for v, t, s in actions if v == "wait"]
    starts = [(t, s) for v, t, s in actions if v == "start"]

    def body(bufs, sems_in, sems_out):
        for verb, t, s in actions:
            at, sems = (starts.index((t, s)), sems_out) if verb == "start" else (waits.index((t, s)), sems_in)
            for w in range(counts[t]):
                sends, arrivals = _ring_copies(bufs[first[t] + w], sems[2 * at], sems[2 * at + 1], RING_STAGES[s] * w, s)
                if verb == "start":
                    for cp in sends:
                        cp.start()
                else:
                    for cp in arrivals:
                        cp.wait_recv()
                    for cp in sends:
                        cp.wait_send()

    bufs, sems, token = _comm_call(
        name, body, [b for t in tags for b in groups[t]["bufs"]],
        [sem for t, s in waits for sem in groups[t]["sems"][s]],
        [RING_STAGES[s] * counts[t] for t, s in starts for _ in (0, 1)], after, token=True)
    for t in tags:
        groups[t]["bufs"] = bufs[first[t]:first[t] + counts[t]]
    for t, s in waits:
        del groups[t]["sems"][s]
    for i, (t, s) in enumerate(starts):
        groups[t]["sems"][s] = (sems[2 * i], sems[2 * i + 1])
    return token


def _cx_copies(src, dst, send_sems, recv_sems, k0):
    x, y, c, chips = _position()
    sends = [_remote(src.at[2 * cx + cy], dst.at[2 * x + y], send_sems, recv_sems, k0 + j, (cx, cy, c))
             for j, (cx, cy) in enumerate(chips)]
    arrivals = [_remote(dst.at[2 * cx + cy], dst.at[2 * cx + cy], send_sems, recv_sems, k0 + j, (x, y, c))
                for j, (cx, cy) in enumerate(chips)]
    return sends, arrivals


def _cx_start(name, pair_sums):
    n = len(pair_sums)
    landing = [lax.empty(p.shape, p.dtype) for p in pair_sums]

    def body(bufs, _, sems):
        for w in range(n):
            for cp in _cx_copies(bufs[w], bufs[n + w], sems[0], sems[1], 3 * w)[0]:
                cp.start()

    bufs, sems, token = _comm_call(name, body, list(pair_sums) + landing, [], [3 * n, 3 * n], token=True)
    return (bufs, sems), token


def _cx_wait(name, state, after):
    bufs, sems = state
    n = len(bufs) // 2

    def body(refs, sems_in, _):
        for w in range(n):
            sends, arrivals = _cx_copies(refs[w], refs[n + w], sems_in[0], sems_in[1], 3 * w)
            for cp in arrivals:
                cp.wait_recv()
            for cp in sends:
                cp.wait_send()

    bufs, _, _ = _comm_call(name, body, bufs, sems, [], after)
    return bufs[:n], bufs[n:]


def _px_copies(src, dst, send_sems, recv_sems, k):
    x, y, c, _ = _position()
    hr = src.shape[1] // 2
    send = _remote(src.at[:, pl.ds((1 - c) * hr, hr), :], dst, send_sems, recv_sems, k, (x, y, 1 - c))
    arrival = _remote(dst, dst, send_sems, recv_sems, k, (x, y, c))
    return send, arrival


def _px_start(name, grads):
    n = len(grads)
    landing = [lax.empty((N_CHIPS, g.shape[1] // 2, g.shape[2]), F32) for g in grads]

    def body(bufs, _, sems):
        for w in range(n):
            _px_copies(bufs[w], bufs[n + w], sems[0], sems[1], w)[0].start()

    bufs, sems, token = _comm_call(name, body, list(grads) + landing, [], [n, n], token=True)
    return (bufs, sems), token


def _px_wait(name, state, after):
    bufs, sems = state
    n = len(bufs) // 2

    def body(refs, sems_in, _):
        for w in range(n):
            send, arrival = _px_copies(refs[w], refs[n + w], sems_in[0], sems_in[1], w)
            arrival.wait_recv()
            send.wait_send()

    bufs, _, _ = _comm_call(name, body, bufs, sems, [], after)
    return bufs[:n], bufs[n:]


def _pair_sum(grad, got, name):
    _, rows, cols = grad.shape
    hr = rows // 2
    tm = min(hr, 256)
    nb = hr // tm
    c = lax.axis_index("c")

    def body(c_ref, g_ref, o_ref, out_ref):
        out_ref[...] = (g_ref[...] + o_ref[...]).astype(BF16)

    return pl.pallas_call(
        body, name=name,
        grid_spec=pltpu.PrefetchScalarGridSpec(
            num_scalar_prefetch=1, grid=(N_CHIPS, nb),
            in_specs=[pl.BlockSpec((None, tm, cols), lambda s, i, c_ref: (s, c_ref[0] * nb + i, 0)),
                      pl.BlockSpec((None, tm, cols), lambda s, i, c_ref: (s, i, 0))],
            out_specs=pl.BlockSpec((None, tm, cols), lambda s, i, c_ref: (s, i, 0))),
        out_shape=jax.ShapeDtypeStruct((N_CHIPS, hr, cols), BF16),
        compiler_params=_params(2),
    )(jnp.reshape(c, (1,)).astype(jnp.int32), grad, got)


def _chip_sum(parts, pair_sums, name):
    _, hr, cols = parts.shape
    tm = min(hr, 256)
    nb = hr // tm
    x, y, c = lax.axis_index("x"), lax.axis_index("y"), lax.axis_index("c")

    def body(pos_ref, p_ref, own_ref, o_ref):
        chip = pos_ref[0]
        own = own_ref[...].astype(F32)
        term = lambda s: jnp.where(chip == s, own, p_ref[s].astype(F32))
        o_ref[...] = ((term(0) + term(1)) + term(2)) + term(3)

    return pl.pallas_call(
        body, name=name,
        grid_spec=pltpu.PrefetchScalarGridSpec(
            num_scalar_prefetch=1, grid=(nb,),
            in_specs=[pl.BlockSpec((N_CHIPS, tm, cols), lambda i, pos: (0, i, 0)),
                      pl.BlockSpec((None, tm, cols), lambda i, pos: (pos[0], i, 0))],
            out_specs=pl.BlockSpec((tm, cols), lambda i, pos: (pos[1] * nb + i, 0))),
        out_shape=jax.ShapeDtypeStruct((2 * hr, cols), F32), compiler_params=_params(1),
    )(jnp.stack([2 * x + y, c]).astype(jnp.int32), parts, pair_sums)


def _share_copies(buf, send_sems, recv_sems, k):
    x, y, c, _ = _position()
    hr = buf.shape[0] // 2
    mine, theirs = buf.at[pl.ds(c * hr, hr), :], buf.at[pl.ds((1 - c) * hr, hr), :]
    return (_remote(mine, mine, send_sems, recv_sems, k, (x, y, 1 - c)),
            _remote(theirs, theirs, send_sems, recv_sems, k, (x, y, c)))


def _share_start(name, bufs):
    n = len(bufs)

    def body(refs, _, sems):
        for w in range(n):
            _share_copies(refs[w], sems[0], sems[1], w)[0].start()

    bufs, sems, token = _comm_call(name, body, list(bufs), [], [n, n], token=True)
    return (bufs, sems), token


def _share_wait(name, state, after):
    bufs, sems = state

    def body(refs, sems_in, _):
        for w in range(len(bufs)):
            send, arrival = _share_copies(refs[w], sems_in[0], sems_in[1], w)
            arrival.wait_recv()
            send.wait_send()

    return _comm_call(name, body, bufs, sems, [], after)[0]


def _allreduce_small(g):
    rows = g.shape[0]
    half = rows // 2

    def body(g_ref, o_ref, sib, slots, send_sems, recv_sems):
        x, y, c, chips = _position()
        me, sibling = (x, y, c), (x, y, 1 - c)
        my_chip = 2 * x + y
        mine = pl.ds(pl.multiple_of(c * half, 8), half)
        theirs = pl.ds(pl.multiple_of((1 - c) * half, 8), half)
        pair = _remote(g_ref.at[theirs], sib, send_sems, recv_sems, 0, sibling)
        pair.start()
        pair.wait()
        slots[my_chip] = g_ref[mine, :] + sib[...]
        sent = []
        for j, (cx, cy) in enumerate(chips):
            cp = _remote(slots.at[my_chip], slots.at[my_chip], send_sems, recv_sems, 1 + j, (cx, cy, c))
            cp.start()
            sent.append(cp)
        for j, (cx, cy) in enumerate(chips):
            got = slots.at[2 * cx + cy]
            _remote(got, got, send_sems, recv_sems, 1 + j, me).wait_recv()
        for cp in sent:
            cp.wait_send()
        o_ref[mine, :] = ((slots[0] + slots[1]) + slots[2]) + slots[3]
        swap = _remote(o_ref.at[mine], o_ref.at[mine], send_sems, recv_sems, 4, sibling)
        swap.start()
        swap.wait()

    vm = pl.BlockSpec(memory_space=pltpu.VMEM)
    return pl.pallas_call(
        body, name="allreduce_small",
        in_specs=[vm], out_specs=vm, out_shape=jax.ShapeDtypeStruct((rows, 128), F32),
        scratch_shapes=[pltpu.VMEM((half, 128), F32), pltpu.VMEM((N_CHIPS, half, 128), F32),
                        pltpu.SemaphoreType.DMA((5,)), pltpu.SemaphoreType.DMA((5,))],
        compiler_params=pltpu.CompilerParams(vmem_limit_bytes=VMEM_LIMIT),
    )(g)


_SMALL =("rel_bias", "ln_v_gain", "ln_v_bias", "w_spatial", "b_spatial", "ln1_gain", "ln1_bias",
          "b_ff1", "b_ff2", "ln2_gain", "ln2_bias")
_SMALL_ROWS = 1200
_LOSS_AT = (152832 // 128, 0)


def _pack_small(parts):
    flat = jnp.concatenate([parts[k].reshape(-1).astype(F32) for k in _SMALL])
    flat = jnp.pad(flat, (0, _SMALL_ROWS * 128 - flat.shape[0]))
    return flat.reshape(_SMALL_ROWS, 128)


def _unpack_small(packed, like):
    flat = packed.reshape(-1)
    out, at = {}, 0
    for k in _SMALL:
        n = math.prod(like[k].shape)
        out[k] = flat[at:at + n].reshape(like[k].shape)
        at += n
    return out


def kernel(x, w_in, rel_bias, ln_v_gain, ln_v_bias, w_spatial, b_spatial, w_proj_a, w_proj_b, w_out, ln1_gain, ln1_bias, w_ff1, b_ff1, w_ff2, b_ff2, ln2_gain, ln2_bias, loss_target, m_w_in, m_rel_bias, m_ln_v_gain, m_ln_v_bias, m_w_spatial, m_b_spatial, m_w_proj_a, m_w_proj_b, m_w_out, m_ln1_gain, m_ln1_bias, m_w_ff1, m_b_ff1, m_w_ff2, m_b_ff2, m_ln2_gain, m_ln2_bias, v_w_in, v_rel_bias, v_ln_v_gain, v_ln_v_bias, v_w_spatial, v_b_spatial, v_w_proj_a, v_w_proj_b, v_w_out, v_ln1_gain, v_ln1_bias, v_w_ff1, v_b_ff1, v_w_ff2, v_b_ff2, v_ln2_gain, v_ln2_bias):
    args = dict(locals())
    big = ("w_in", "w_proj_a", "w_proj_b", "w_out", "w_ff1", "w_ff2")
    weights = ("w_in", "rel_bias", "ln_v_gain", "ln_v_bias", "w_spatial", "b_spatial", "w_proj_a", "w_proj_b", "w_out",
               "ln1_gain", "ln1_bias", "w_ff1", "b_ff1", "w_ff2", "b_ff2", "ln2_gain", "ln2_bias")

    xs = x[0]
    target = loss_target[0]

    ring = {"a": {"bufs": [_place_shard(w_in[0], "place_w_in")], "sems": {}}}
    tok = _ring_call("allgather_a_near", ring, [("start", "a", "ici_near")])
    placed = [_place_shard(args[k][0], f"place_{k}", after=tok) for k in big[1:]]
    for tag, bufs in (("b", placed[0:3]), ("c", placed[3:4]), ("d", placed[4:5])):
        ring[tag] = {"bufs": bufs, "sems": {}}
    xb = _to_bf16(xs, "x_to_bf16", after=placed[4])

    mx, my = lax.axis_index("x"), lax.axis_index("y")
    own = jnp.reshape(2 * mx + my, (1,)).astype(jnp.int32)
    near = jnp.stack([2 * (1 - mx) + my, 2 * mx + (1 - my)]).astype(jnp.int32)
    far = jnp.reshape(2 * (1 - mx) + (1 - my), (1,)).astype(jnp.int32)
    proj = _proj(xb, ring["a"]["bufs"][0], own, "proj_own")
    _ring_call("allgather_a_far", ring, [("wait", "a", "ici_near"), ("start", "a", "ici_far"), ("start", "a", "d2d_near"),
                                         ("start", "b", "ici_near"), ("start", "c", "ici_near")], after=proj)
    _ring_call("allgather_a_near_done", ring, [("wait", "a", "d2d_near")])
    proj = _proj(xb, ring["a"]["bufs"][0], near, "proj_near", into=proj)
    _ring_call("allgather_a_last", ring, [("wait", "a", "ici_far"), ("start", "a", "d2d_far")], after=proj)
    _ring_call("allgather_a_done", ring, [("wait", "a", "d2d_far")])
    (win_g,) = ring["a"]["bufs"]
    proj = _proj(xb, win_g, far, "proj_far", into=proj)
    _ring_call("allgather_b_far", ring, [("wait", "b", "ici_near"), ("start", "b", "ici_far"), ("start", "b", "d2d_near")],
               after=proj)
    ws = w_spatial[0]
    ws_t = jnp.transpose(ws, (0, 2, 1))
    bsp_b = jnp.broadcast_to(b_spatial[0][:, :, None], (NH, 128, 128))
    gmlp = _gmlp_fwd(proj, ws, bsp_b, ln_v_gain, ln_v_bias)
    attn, lse = _attention_fwd(proj, rel_bias)
    _ring_call("allgather_b_last_c_far", ring,
               [("wait", "b", "ici_far"), ("start", "b", "d2d_far"),
                ("wait", "c", "ici_near"), ("start", "c", "ici_far"), ("start", "c", "d2d_near"),
                ("start", "d", "ici_near")], after=attn)
    _ring_call("allgather_b_done", ring, [("wait", "b", "d2d_near"), ("wait", "b", "d2d_far")])
    wpa_g, wpb_g, wout_g = ring["b"]["bufs"]
    wout_full = wout_g.reshape(D, D)
    ya, yb, merged = _branch(attn, gmlp, wpa_g, wpb_g, proj)
    xhat1, rstd1, h1b = _out_ln1(merged, wout_full, xs, ln1_gain, ln1_bias)
    _ring_call("allgather_c_last_d_far", ring,
               [("wait", "c", "ici_far"), ("start", "c", "d2d_far"),
                ("wait", "d", "ici_near"), ("start", "d", "ici_far"), ("start", "d", "d2d_near")], after=h1b)
    _ring_call("allgather_c_done", ring, [("wait", "c", "d2d_near"), ("wait", "c", "d2d_far")])
    (w1_g,) = ring["c"]["bufs"]
    a, r = _ff1(h1b, w1_g, b_ff1)
    _ring_call("allgather_d_last", ring, [("wait", "d", "ici_far"), ("start", "d", "d2d_far")], after=a)
    _ring_call("allgather_d_done", ring, [("wait", "d", "d2d_near"), ("wait", "d", "d2d_far")])
    (w2_g,) = ring["d"]["bufs"]
    w2_full = w2_g.reshape(DFF, D)
    dpre2, dpre2b, st2 = _ff2_ln2_loss(a, w2_full, xhat1, ln1_gain, ln1_bias, b_ff2, ln2_gain, ln2_bias, target)

    def pair_and_chip(tag, state, after):
        local, from_sibling = _px_wait(f"pair_exchange_wait_{tag}", state, after)
        pair_sums = [_pair_sum(g, o, f"pair_sum_{tag}_{i}") for i, (g, o) in enumerate(zip(local, from_sibling))]
        return _cx_start(f"chip_exchange_start_{tag}", pair_sums)

    g_w2 = _grad_w(a, dpre2b, "grad_w_ff2", 512, 2048, False)
    px, tok = _px_start("pair_exchange_start_w_ff2", [g_w2.reshape(N_CHIPS, DFF // N_CHIPS, D)])
    dprea, g_b1 = _d_ff1(dpre2b, w2_full, r, after=tok)
    cx_w2, tok = pair_and_chip("w_ff2", px, dprea)
    g_w1 = _grad_w(h1b, dprea, "grad_w_ff1", 512, 2048, True, after=tok)
    px, tok = _px_start("pair_exchange_start_w_ff1", [g_w1])
    dpre1, dpre1b, st1 = _d_h1_ln1(dprea, w1_g, dpre2, xhat1, rstd1, ln1_gain, after=tok)
    cx_w1, tok = pair_and_chip("w_ff1", px, dpre1b)
    g_wout = _grad_w(merged, dpre1b, "grad_w_out", 512, 2048, False, after=tok)
    dya, dyb, dga, dgb = _d_merged(dpre1b, wout_full, proj, ya, yb)
    g_wpa = _grad_w(attn, dya, "grad_w_proj_a", 1024, 512, True)
    g_wpb = _grad_w(gmlp, dyb, "grad_w_proj_b", 1024, 512, True)
    px, tok = _px_start("pair_exchange_start_b", [g_wpa, g_wpb, g_wout.reshape(N_CHIPS, D // N_CHIPS, D)])
    dattn, dgmlp = _d_branches(dya, dyb, wpa_g, wpb_g, after=tok)
    duv, g_ws, g_bs, stv = _gmlp_bwd(proj, dgmlp, ws, ws_t, bsp_b, ln_v_gain, ln_v_bias)
    cx_b, tok = pair_and_chip("b", px, duv)
    dq, dk, dv, ds_sums = _attention_bwd(proj, dattn, attn, lse, rel_bias, after=tok)
    g_rb = _rel_bias_grad(ds_sums)[:, :NH]

    small_g = dict(rel_bias=g_rb, ln_v_gain=stv[0], ln_v_bias=stv[1], w_spatial=g_ws, b_spatial=g_bs[:, :, 0],
                   ln1_gain=st1[0], ln1_bias=st1[1], b_ff1=g_b1, b_ff2=st2[2], ln2_gain=st2[0], ln2_bias=st2[1])
    gs = _allreduce_small(_pack_small(small_g).at[_LOSS_AT].set(st2[3, 0]))
    ds_, ms_, vs_, _ = _adamw(_pack_small({k: args[k] for k in _SMALL}), gs,
                           _pack_small({k: args["m_" + k] for k in _SMALL}),
                           _pack_small({k: args["v_" + k] for k in _SMALL}), "adamw_small")
    like = {k: args[k] for k in _SMALL}
    grads, deltas, new_m, new_v = (_unpack_small(t, like) for t in (gs, ds_, ms_, vs_))

    dproj = jnp.concatenate([dq, dk, dv, duv, dga, dgb], axis=1)
    g_win = _grad_w(xb, dproj, "grad_w_in", 512, 2304, True, after=gs)
    px, tok = _px_start("pair_exchange_start_w_in", [g_win])

    def chip_sums(tag, state, names, after):
        pair_sums, from_chips = _cx_wait(f"chip_exchange_wait_{tag}", state, after)
        halves = [_chip_sum(p, own, f"chip_sum_{k}") for p, own, k in zip(from_chips, pair_sums, names)]
        return _share_start(f"share_start_{tag}", halves)

    def adam(tag, state, names, after):
        last = None
        for k, g in zip(names, _share_wait(f"share_wait_{tag}", state, after)):
            d_, m_, v_, g_ = _adamw(args[k][0], g, args["m_" + k][0], args["v_" + k][0], f"adamw_{k}")
            grads[k], deltas[k], new_m[k], new_v[k] = g_[None], d_[None], m_[None], v_[None]
            last = d_
        return last

    sh_w2, tok = chip_sums("w_ff2", cx_w2, ["w_ff2"], tok)
    sh_w1, tok = chip_sums("w_ff1", cx_w1, ["w_ff1"], tok)
    sh_b, tok = chip_sums("b", cx_b, ["w_proj_a", "w_proj_b", "w_out"], tok)
    cx_in, tok = pair_and_chip("w_in", px, tok)
    grad_x = _d_x(dproj, win_g, dpre1, after=tok)
    done = adam("w_ff2", sh_w2, ["w_ff2"], grad_x)
    done = adam("w_ff1", sh_w1, ["w_ff1"], done)
    done = adam("b", sh_b, ["w_proj_a", "w_proj_b", "w_out"], done)
    sh_in, tok = chip_sums("w_in", cx_in, ["w_in"], done)
    adam("w_in", sh_in, ["w_in"], tok)

    loss = gs[_LOSS_AT] * (0.5 / D)
    return (loss, grad_x[None], *[grads[k] for k in weights], *[deltas[k] for k in weights],
            *[new_m[k] for k in weights], *[new_v[k] for k in weights])
```

```python
import math

import numpy as np
import jax
import jax.numpy as jnp
from jax import lax
from jax.experimental import pallas as pl
from jax.experimental.pallas import tpu as pltpu

F32 = jnp.float32
BF16 = jnp.bfloat16

S = 2048
D = 2048
DA = 1024
DB = 1024
DFF = 8192
DIN = 9216
NH = 8
HD = 128
NBLK = 16
PATTERNS = ((128, 1), (512, 4), (2048, 16))
N_BUCKETS = 32
MAX_DISTANCE = 2048
ALPHA = 2.0 ** 0.25
LN_EPS = 1e-5
NEG_INF = -1e30
SCALE = HD ** -0.5
N_CHIPS = 4

ADAM_LR = 0.001
ADAM_B1 = 0.9
ADAM_B2 = 0.999
ADAM_EPS = 1e-08
ADAM_WD = 0.01
ADAM_STEP = 10

VMEM_LIMIT = 56 * 1024 * 1024
MESH = pl.DeviceIdType.MESH
ANY = pl.BlockSpec(memory_space=pl.ANY)


def _params(n_axes, vmem=VMEM_LIMIT):
    return pltpu.CompilerParams(dimension_semantics=("arbitrary",) * n_axes, vmem_limit_bytes=vmem)


def _bucket_tile(dilation):
    qi = np.arange(128)[:, None]
    kj = np.arange(256)[None, :]
    n = np.clip(128 + qi - kj, 0, 128) * dilation
    max_exact = N_BUCKETS // 2
    nf = np.maximum(n, 1).astype(np.float32)
    large = max_exact + (np.log(nf / np.float32(max_exact)) / np.float32(math.log(MAX_DISTANCE / max_exact))
                         * np.float32(N_BUCKETS - max_exact)).astype(np.int32)
    large = np.minimum(large, N_BUCKETS - 1)
    return np.where(n < max_exact, n, large).astype(np.int32)


def _gelu(x):
    c = math.sqrt(2.0 / math.pi)
    t = jnp.tanh(c * (x + 0.044715 * x * x * x))
    return 0.5 * x * (1.0 + t), t


def _gelu_grad(x, t):
    c = math.sqrt(2.0 / math.pi)
    return 0.5 * (1.0 + t) + 0.5 * x * (1.0 - t * t) * c * (1.0 + 3.0 * 0.044715 * x * x)


def _sigmoid(x):
    return 1.0 / (1.0 + jnp.exp(-x))


def _dot(a, b):
    return jnp.dot(a, b, preferred_element_type=F32)


def _behind(body, n_in, after):
    if after is None:
        return body, [], []
    return (lambda *refs: body(*refs[:n_in], *refs[n_in + 1:])), [ANY], [after]


def _dot_nt(a, b):
    return lax.dot_general(a, b, (((1,), (1,)), ((), ())), preferred_element_type=F32)


def _proj(xb, win_g, shards, name, into=None):
    tn = 768
    per = 2304 // tn

    def body(shards_ref, x_ref, w_ref, *rest):
        rest[-1][...] = _dot(x_ref[...], w_ref[...])

    in_specs = [pl.BlockSpec((S, D), lambda j, sh: (0, 0)),
                pl.BlockSpec((None, D, tn), lambda j, sh: (sh[j // per], 0, j % per))]
    return pl.pallas_call(
        body, name=name,
        grid_spec=pltpu.PrefetchScalarGridSpec(
            num_scalar_prefetch=1, grid=(shards.shape[0] * per,),
            in_specs=in_specs + ([ANY] if into is not None else []),
            out_specs=pl.BlockSpec((S, tn), lambda j, sh: (0, sh[j // per] * per + j % per))),
        out_shape=jax.ShapeDtypeStruct((S, DIN), F32),
        input_output_aliases={3: 0} if into is not None else {},
        compiler_params=_params(1),
    )(shards, xb, win_g, *([into] if into is not None else []))


FWD_HEADS_PER_STEP = 4
BWD_HEADS_PER_STEP = 2


def _head_bias_tiles(rb_ref, bk_ref, bias_scr, first_head, hps):
    qi = lax.broadcasted_iota(jnp.int32, (128, 256), 0)
    kj = lax.broadcasted_iota(jnp.int32, (128, 256), 1)
    steps = 128 + qi - kj
    band = (steps >= 0) & (steps <= 128)
    bias_scr[...] = jnp.zeros_like(bias_scr)
    for p in range(len(PATTERNS)):
        bucket = bk_ref[p]

        def one_bucket(t, carry):
            hit = bucket == t
            for j in range(hps):
                bias_scr[p, j] = jnp.where(hit, rb_ref[t, first_head + j], bias_scr[p, j])
            return carry

        lax.fori_loop(0, N_BUCKETS, one_bucket, 0)
        for j in range(hps):
            bias_scr[p, j] = jnp.where(band, bias_scr[p, j], NEG_INF)


def _block_rows(b, dilation):
    nblk = NBLK // dilation
    r, n = b // nblk, b % nblk
    start = r + n * (128 * dilation)
    prev_start = jnp.maximum(start - 128 * dilation, r)
    if dilation == 1:
        return pl.ds(pl.multiple_of(start, 128), 128), pl.ds(pl.multiple_of(prev_start, 128), 128), n > 0
    return pl.ds(start, 128, stride=dilation), pl.ds(prev_start, 128, stride=dilation), n > 0


def _head_specs(first, hps):
    return [pl.BlockSpec((S, HD), lambda g, j=j: (0, first + g * hps + j)) for j in range(hps)]


def _heads_spec(hps):
    return pl.BlockSpec((S, hps * HD), lambda g: (0, g))


def _attention_fwd(proj, rel_bias):
    hps = FWD_HEADS_PER_STEP
    buckets = jnp.asarray(np.stack([_bucket_tile(d) for _, d in PATTERNS]))

    def body(rb_ref, bk_ref, *refs):
        q_refs, k_refs, v_refs = (refs[i * hps:(i + 1) * hps] for i in range(3))
        o_ref, lse_ref, bias_scr = refs[3 * hps:3 * hps + 3]
        acc_scrs, m_scrs, l_scrs = (refs[3 * hps + 3 + i * hps:3 * hps + 3 + (i + 1) * hps] for i in range(3))
        _head_bias_tiles(rb_ref, bk_ref, bias_scr, pl.program_id(0) * hps, hps)
        kj = lax.broadcasted_iota(jnp.int32, (128, 256), 1)
        for p, (_, d) in enumerate(PATTERNS):
            prev_blocks = NBLK // d > 1

            def block(b, carry):
                units = [(j,) + _block_rows(blk, d) for blk in (b, b + NBLK // 2) for j in range(hps)]
                scores = []
                for j, rows, prows, _ in units:
                    q = q_refs[j][rows, :].astype(BF16)
                    cur = _dot_nt(q, k_refs[j][rows, :].astype(BF16))
                    if prev_blocks:
                        cur = jnp.concatenate([_dot_nt(q, k_refs[j][prows, :].astype(BF16)), cur], axis=1)
                    scores.append(cur)
                soft = []
                for u, (j, _, _, has_prev) in enumerate(units):
                    if prev_blocks:
                        s = jnp.where((kj >= 128) | has_prev, scores[u] * SCALE + bias_scr[p, j], NEG_INF)
                    else:
                        s = scores[u] * SCALE + bias_scr[p, j, :, 128:256]
                    m = jnp.max(s, axis=1, keepdims=True)
                    e = jnp.exp(s - m)
                    soft.append((m, jnp.sum(e, axis=1, keepdims=True), e.astype(BF16)))
                outs = []
                for u, (j, rows, prows, _) in enumerate(units):
                    e = soft[u][2]
                    if prev_blocks:
                        outs.append(_dot(e[:, :128], v_refs[j][prows, :].astype(BF16))
                                    + _dot(e[:, 128:], v_refs[j][rows, :].astype(BF16)))
                    else:
                        outs.append(_dot(e, v_refs[j][rows, :].astype(BF16)))
                for u, (j, rows, _, _) in enumerate(units):
                    acc_scr, m_scr, l_scr = acc_scrs[j], m_scrs[j], l_scrs[j]
                    (m, den, _), o = soft[u], outs[u]
                    if p == 0:
                        acc_scr[rows, :] = o
                        m_scr[rows, :] = jnp.broadcast_to(m, (128, HD))
                        l_scr[rows, :] = jnp.broadcast_to(den, (128, HD))
                    else:
                        m_old = m_scr[rows, :]
                        m_new = jnp.maximum(m_old, m)
                        w_old, w_new = jnp.exp(m_old - m_new), jnp.exp(m - m_new)
                        acc_scr[rows, :] = acc_scr[rows, :] * w_old + o * w_new
                        l_scr[rows, :] = l_scr[rows, :] * w_old + den * w_new
                        m_scr[rows, :] = m_new
                return carry

            lax.fori_loop(0, NBLK // 2, block, 0)
        for j in range(hps):
            cols = slice(j * HD, (j + 1) * HD)
            den = l_scrs[j][...]
            o_ref[:, cols] = (acc_scrs[j][...] / den).astype(BF16)
            lse_ref[:, cols] = m_scrs[j][...] + jnp.log(den)

    return pl.pallas_call(
        body, name="attention_fwd", grid=(NH // hps,),
        in_specs=[pl.BlockSpec(memory_space=pltpu.SMEM), pl.BlockSpec((3, 128, 256), lambda g: (0, 0, 0))]
        + _head_specs(0, hps) + _head_specs(NH, hps) + _head_specs(2 * NH, hps),
        out_specs=[_heads_spec(hps), _heads_spec(hps)],
        out_shape=[jax.ShapeDtypeStruct((S, DA), BF16), jax.ShapeDtypeStruct((S, DA), F32)],
        scratch_shapes=[pltpu.VMEM((3, hps, 128, 256), F32)] + [pltpu.VMEM((S, HD), F32)] * (3 * hps),
        compiler_params=_params(1),
    )(rel_bias, buckets, *([proj] * (3 * hps)))


def _attention_bwd(proj, dattn, attn, lse, rel_bias, after=None):
    hps = BWD_HEADS_PER_STEP

    def body(rb_ref, bk_ref, *refs):
        q_refs, k_refs, v_refs, do_refs, o_refs, lse_refs = (refs[i * hps:(i + 1) * hps] for i in range(6))
        dq_ref, dk_ref, dv_ref, ds_ref, bias_scr = refs[6 * hps:6 * hps + 5]
        dl_scrs, dq_scrs, dk_scrs, dv_scrs = (refs[6 * hps + 5 + i * hps:6 * hps + 5 + (i + 1) * hps] for i in range(4))
        _head_bias_tiles(rb_ref, bk_ref, bias_scr, pl.program_id(0) * hps, hps)
        ds_ref[...] = jnp.zeros_like(ds_ref)
        for j in range(hps):
            dq_scrs[j][...] = jnp.zeros((S, HD), F32)
            dk_scrs[j][...] = jnp.zeros((S, HD), F32)
            dv_scrs[j][...] = jnp.zeros((S, HD), F32)
            prod = do_refs[j][...] * o_refs[j][...].astype(F32)
            dl_scrs[j][...] = jnp.broadcast_to(jnp.sum(prod, axis=1, keepdims=True), (S, HD))
        for p, (_, d) in enumerate(PATTERNS):
            prev_blocks = NBLK // d > 1

            def block(b, carry):
                units = [(j,) + _block_rows(blk, d) for blk in (b, b + NBLK // 2) for j in range(hps)]
                ops, raw = [], []
                for j, rows, prows, _ in units:
                    q, do = q_refs[j][rows, :].astype(BF16), do_refs[j][rows, :].astype(BF16)
                    kc, vc = k_refs[j][rows, :].astype(BF16), v_refs[j][rows, :].astype(BF16)
                    if prev_blocks:
                        kp, vp = k_refs[j][prows, :].astype(BF16), v_refs[j][prows, :].astype(BF16)
                        ops.append((q, do, kc, kp))
                        raw.append((_dot_nt(q, kc), _dot_nt(do, vc), _dot_nt(q, kp), _dot_nt(do, vp)))
                    else:
                        ops.append((q, do, kc))
                        raw.append((_dot_nt(q, kc), _dot_nt(do, vc)))
                probs = []
                for u, (j, rows, _, has_prev) in enumerate(units):
                    lse_b, dl_b = lse_refs[j][rows, :], dl_scrs[j][rows, :]
                    p_c = jnp.exp(raw[u][0] * SCALE + bias_scr[p, j, :, 128:256] - lse_b)
                    ds_c = p_c * (raw[u][1] - dl_b)
                    ds_ref[p, j, :, 128:256] += ds_c
                    if prev_blocks:
                        p_p = jnp.where(has_prev, jnp.exp(raw[u][2] * SCALE + bias_scr[p, j, :, 0:128] - lse_b), 0.0)
                        ds_p = p_p * (raw[u][3] - dl_b)
                        ds_ref[p, j, :, 0:128] += ds_p
                        probs.append((p_c, ds_c, p_p, ds_p))
                    else:
                        probs.append((p_c, ds_c))
                grads = []
                for u in range(len(units)):
                    q, do, kc = ops[u][:3]
                    p_c, ds_c = probs[u][:2]
                    dq = _dot(ds_c.astype(BF16), kc)
                    cur = (_dot(ds_c.T.astype(BF16), q) * SCALE, _dot(p_c.T.astype(BF16), do))
                    if prev_blocks:
                        p_p, ds_p = probs[u][2:]
                        dq = dq + _dot(ds_p.astype(BF16), ops[u][3])
                        cur = cur + (_dot(ds_p.T.astype(BF16), q) * SCALE, _dot(p_p.T.astype(BF16), do))
                    grads.append((dq * SCALE,) + cur)
                for u, (j, rows, prows, _) in enumerate(units):
                    dq_scrs[j][rows, :] += grads[u][0]
                    dk_scrs[j][rows, :] += grads[u][1]
                    dv_scrs[j][rows, :] += grads[u][2]
                    if prev_blocks:
                        dk_scrs[j][prows, :] += grads[u][3]
                        dv_scrs[j][prows, :] += grads[u][4]
                return carry

            lax.fori_loop(0, NBLK // 2, block, 0)
        for j in range(hps):
            cols = slice(j * HD, (j + 1) * HD)
            dq_ref[:, cols] = dq_scrs[j][...].astype(BF16)
            dk_ref[:, cols] = dk_scrs[j][...].astype(BF16)
            dv_ref[:, cols] = dv_scrs[j][...].astype(BF16)

    buckets = jnp.asarray(np.stack([_bucket_tile(d) for _, d in PATTERNS]))
    body, more_specs, more = _behind(body, 2 + 6 * hps, after)
    return pl.pallas_call(
        body, name="attention_bwd", grid=(NH // hps,),
        in_specs=[pl.BlockSpec(memory_space=pltpu.SMEM), pl.BlockSpec((3, 128, 256), lambda g: (0, 0, 0))]
        + _head_specs(0, hps) + _head_specs(NH, hps) + _head_specs(2 * NH, hps) + 3 * _head_specs(0, hps)
        + more_specs,
        out_specs=3 * [_heads_spec(hps)] + [pl.BlockSpec((3, hps, 128, 256), lambda g: (0, g, 0, 0))],
        out_shape=[jax.ShapeDtypeStruct((S, DA), BF16)] * 3 + [jax.ShapeDtypeStruct((3, NH, 128, 256), F32)],
        scratch_shapes=[pltpu.VMEM((3, hps, 128, 256), F32)] + [pltpu.VMEM((S, HD), F32)] * (4 * hps),
        compiler_params=_params(1),
    )(rel_bias, buckets, *([proj] * (3 * hps)), *([dattn] * hps), *([attn] * hps), *([lse] * hps), *more)


def _gmlp_parts(u_ref, vb_ref, g_ref, be_ref):
    u = u_ref[...]
    u_act, tu = _gelu(u)
    vb = vb_ref[...]
    gv, tv = _gelu(vb)
    mean = jnp.mean(gv, axis=1, keepdims=True)
    cen = gv - mean
    var = jnp.mean(cen * cen, axis=1, keepdims=True)
    rstd = lax.rsqrt(var + LN_EPS)
    xhat = cen * rstd
    vn = xhat * g_ref[...] + be_ref[...]
    return u, tu, u_act, vb, tv, rstd, xhat, vn


def _gmlp_fwd(proj, ws, bsp_b, gain_v, bias_v):
    def body(u_ref, vb_ref, ws_ref, bsp_ref, g_ref, be_ref, o_ref):
        _, _, u_act, _, _, _, _, vn = _gmlp_parts(u_ref, vb_ref, g_ref, be_ref)
        row = lax.broadcasted_iota(jnp.int32, (128, 128), 0)
        col = lax.broadcasted_iota(jnp.int32, (128, 128), 1)
        causal = row >= col
        for g in range(NH):
            cols = slice(g * 128, (g + 1) * 128)
            wsg = jnp.where(causal, ws_ref[g], 0.0).astype(BF16)
            z = _dot(wsg, vn[:, cols].astype(BF16)) + bsp_ref[g]
            o_ref[:, cols] = (u_act[:, cols] * z).astype(BF16)

    return pl.pallas_call(
        body, name="gmlp_fwd", grid=(NBLK,),
        in_specs=[pl.BlockSpec((128, DB), lambda c: (c, 3)), pl.BlockSpec((128, DB), lambda c: (c, 4)),
                  pl.BlockSpec((NH, 128, 128), lambda c: (0, 0, 0)), pl.BlockSpec((NH, 128, 128), lambda c: (0, 0, 0)),
                  pl.BlockSpec((1, DB), lambda c: (0, 0)), pl.BlockSpec((1, DB), lambda c: (0, 0))],
        out_specs=pl.BlockSpec((128, DB), lambda c: (c, 0)),
        out_shape=jax.ShapeDtypeStruct((S, DB), BF16),
        compiler_params=_params(1),
    )(proj, proj, ws, bsp_b, gain_v, bias_v)


def _branch(attn, gmlp, wpa_g, wpb_g, proj):
    tn = 512

    def body(a_ref, g_ref, wa_ref, wb_ref, ga_ref, gb_ref, ya_ref, yb_ref, mg_ref):
        ya = _dot(a_ref[...], wa_ref[...])
        yb = _dot(g_ref[...], wb_ref[...])
        ya_ref[...] = ya.astype(BF16)
        yb_ref[...] = yb.astype(BF16)
        mg_ref[...] = (_sigmoid(ga_ref[...]) * ya + _sigmoid(gb_ref[...]) * yb).astype(BF16)

    out = pl.BlockSpec((S, tn), lambda j: (0, j))
    return pl.pallas_call(
        body, name="branch", grid=(D // tn,),
        in_specs=[pl.BlockSpec((S, DA), lambda j: (0, 0)), pl.BlockSpec((S, DB), lambda j: (0, 0)),
                  pl.BlockSpec((None, DA, tn), lambda j: (j, 0, 0)), pl.BlockSpec((None, DB, tn), lambda j: (j, 0, 0)),
                  pl.BlockSpec((S, tn), lambda j: (0, 5120 // tn + j)), pl.BlockSpec((S, tn), lambda j: (0, 7168 // tn + j))],
        out_specs=[out, out, out],
        out_shape=[jax.ShapeDtypeStruct((S, D), BF16)] * 3,
        compiler_params=_params(1),
    )(attn, gmlp, wpa_g, wpb_g, proj, proj)


def _out_ln1(merged, wout_g, x, gain, bias):
    tm = 256

    def body(m_ref, w_ref, x_ref, g_ref, b_ref, xh_ref, rs_ref, h_ref):
        pre = ALPHA * x_ref[...] + _dot(m_ref[...], w_ref[...])
        mean = jnp.mean(pre, axis=1, keepdims=True)
        cen = pre - mean
        var = jnp.mean(cen * cen, axis=1, keepdims=True)
        rstd = lax.rsqrt(var + LN_EPS)
        xhat = cen * rstd
        xh_ref[...] = xhat
        rs_ref[...] = jnp.broadcast_to(rstd, (tm, 128))
        h_ref[...] = (xhat * g_ref[...] + b_ref[...]).astype(BF16)

    row = pl.BlockSpec((tm, D), lambda i: (i, 0))
    vec = pl.BlockSpec((1, D), lambda i: (0, 0))
    return pl.pallas_call(
        body, name="out_ln1", grid=(S // tm,),
        in_specs=[row, pl.BlockSpec((D, D), lambda i: (0, 0)), row, vec, vec],
        out_specs=[row, pl.BlockSpec((tm, 128), lambda i: (i, 0)), row],
        out_shape=[jax.ShapeDtypeStruct((S, D), F32), jax.ShapeDtypeStruct((S, 128), F32),
                   jax.ShapeDtypeStruct((S, D), BF16)],
        compiler_params=_params(1),
    )(merged, wout_g, x, gain, bias)


def _ff1(h1b, w1_g, b1):
    tn = 512
    per = D // tn

    def body(h_ref, w_ref, b_ref, a_ref, r_ref):
        r = jnp.maximum(_dot(h_ref[...], w_ref[...]) + b_ref[...], 0.0)
        r_ref[...] = r.astype(BF16)
        a_ref[...] = (r * r).astype(BF16)

    out = pl.BlockSpec((S, tn), lambda j: (0, j))
    return pl.pallas_call(
        body, name="ff1", grid=(DFF // tn,),
        in_specs=[pl.BlockSpec((S, D), lambda j: (0, 0)),
                  pl.BlockSpec((None, D, tn), lambda j: (j // per, 0, j % per)),
                  pl.BlockSpec((1, tn), lambda j: (0, j))],
        out_specs=[out, out],
        out_shape=[jax.ShapeDtypeStruct((S, DFF), BF16)] * 2,
        compiler_params=_params(1),
    )(h1b, w1_g, b1)


def _ff2_ln2_loss(a, w2_g, xhat1, g1, b1, b2, g2, be2, target):
    tm, tk = 512, 1024
    nk = DFF // tk

    def body(a_ref, w_ref, xh_ref, g1_ref, b1_ref, b2_ref, g2_ref, be2_ref, t_ref, d_ref, db_ref, st_ref, acc):
        i, k = pl.program_id(0), pl.program_id(1)

        @pl.when(k == 0)
        def _():
            acc[...] = jnp.zeros_like(acc)

        @pl.when((i == 0) & (k == 0))
        def _():
            st_ref[...] = jnp.zeros_like(st_ref)

        acc[...] += _dot(a_ref[...], w_ref[...])

        @pl.when(k == nk - 1)
        def _():
            def rows_chunk(ci, carry):
                rows = pl.ds(pl.multiple_of(ci * 128, 128), 128)
                h1 = xh_ref[rows, :] * g1_ref[...] + b1_ref[...]
                pre = ALPHA * h1 + acc[rows, :] + b2_ref[...]
                mean = jnp.mean(pre, axis=1, keepdims=True)
                cen = pre - mean
                var = jnp.mean(cen * cen, axis=1, keepdims=True)
                rstd = lax.rsqrt(var + LN_EPS)
                xhat = cen * rstd
                y = xhat * g2_ref[...] + be2_ref[...]
                err = y - t_ref[rows, :]
                dy = err * (1.0 / D)
                g = dy * g2_ref[...]
                dpre = rstd * (g - jnp.mean(g, axis=1, keepdims=True)
                               - xhat * jnp.mean(g * xhat, axis=1, keepdims=True))
                d_ref[rows, :] = dpre
                db_ref[rows, :] = dpre.astype(BF16)
                st_ref[0:1, :] += jnp.sum(dy * xhat, axis=0, keepdims=True)
                st_ref[1:2, :] += jnp.sum(dy, axis=0, keepdims=True)
                st_ref[2:3, :] += jnp.sum(dpre, axis=0, keepdims=True)
                st_ref[3:4, :] += jnp.broadcast_to(jnp.sum(err * err).reshape(1, 1), (1, D))
                return carry

            lax.fori_loop(0, tm // 128, rows_chunk, 0)

    row = pl.BlockSpec((tm, D), lambda i, k: (i, 0))
    vec = pl.BlockSpec((1, D), lambda i, k: (0, 0))
    return pl.pallas_call(
        body, name="ff2_ln2_loss", grid=(S // tm, nk),
        in_specs=[pl.BlockSpec((tm, tk), lambda i, k: (i, k)), pl.BlockSpec((tk, D), lambda i, k: (k, 0)),
                  row, vec, vec, vec, vec, vec, row],
        out_specs=[row, row, pl.BlockSpec((8, D), lambda i, k: (0, 0))],
        out_shape=[jax.ShapeDtypeStruct((S, D), F32), jax.ShapeDtypeStruct((S, D), BF16),
                   jax.ShapeDtypeStruct((8, D), F32)],
        scratch_shapes=[pltpu.VMEM((tm, D), F32)],
        compiler_params=_params(2),
    )(a, w2_g, xhat1, g1, b1, b2, g2, be2, target)


def _grad_w(act, dout, name, ti, tj, sharded, after=None):
    m, n = act.shape[1], dout.shape[1]
    ns = n // N_CHIPS
    per = ns // tj if sharded else None

    def body(a_ref, b_ref, o_ref, at_scr):
        @pl.when(pl.program_id(1) == 0)
        def _():
            at_scr[...] = a_ref[...].T

        o_ref[...] = _dot(at_scr[...], b_ref[...]).astype(BF16)

    if sharded:
        out_spec = pl.BlockSpec((None, ti, tj), lambda i, j: (j // per, i, j % per))
        out_shape = jax.ShapeDtypeStruct((N_CHIPS, m, ns), BF16)
    else:
        out_spec = pl.BlockSpec((ti, tj), lambda i, j: (i, j))
        out_shape = jax.ShapeDtypeStruct((m, n), BF16)
    body, more_specs, more = _behind(body, 2, after)
    return pl.pallas_call(
        body, name=name, grid=(m // ti, n // tj),
        in_specs=[pl.BlockSpec((S, ti), lambda i, j: (0, i)), pl.BlockSpec((S, tj), lambda i, j: (0, j))] + more_specs,
        out_specs=out_spec, out_shape=out_shape,
        scratch_shapes=[pltpu.VMEM((ti, S), BF16)],
        compiler_params=_params(2),
    )(act, dout, *more)


def _d_ff1(dpre2b, w2_g, r, after=None):
    tn = 512

    def body(d_ref, w_ref, r_ref, o_ref, gb_ref):
        da = _dot_nt(d_ref[...], w_ref[...])
        dp = da * (2.0 * r_ref[...].astype(F32))
        o_ref[...] = dp.astype(BF16)
        gb_ref[...] = jnp.sum(dp, axis=0, keepdims=True)

    body, more_specs, more = _behind(body, 3, after)
    return pl.pallas_call(
        body, name="d_ff1", grid=(DFF // tn,),
        in_specs=[pl.BlockSpec((S, D), lambda j: (0, 0)), pl.BlockSpec((tn, D), lambda j: (j, 0)),
                  pl.BlockSpec((S, tn), lambda j: (0, j))] + more_specs,
        out_specs=[pl.BlockSpec((S, tn), lambda j: (0, j)), pl.BlockSpec((1, tn), lambda j: (0, j))],
        out_shape=[jax.ShapeDtypeStruct((S, DFF), BF16), jax.ShapeDtypeStruct((1, DFF), F32)],
        compiler_params=_params(1),
    )(dpre2b, w2_g, r, *more)


def _d_h1_ln1(dprea, w1_g, dpre2, xhat1, rstd1, g1, after=None):
    tm, tk = 512, 1024
    per = D // tk
    nk = DFF // tk

    def body(a_ref, w_ref, d2_ref, xh_ref, rs_ref, g_ref, d_ref, db_ref, st_ref, acc):
        i, k = pl.program_id(0), pl.program_id(1)

        @pl.when(k == 0)
        def _():
            acc[...] = jnp.zeros_like(acc)

        @pl.when((i == 0) & (k == 0))
        def _():
            st_ref[...] = jnp.zeros_like(st_ref)

        acc[...] += _dot_nt(a_ref[...], w_ref[...])

        @pl.when(k == nk - 1)
        def _():
            def rows_chunk(ci, carry):
                rows = pl.ds(pl.multiple_of(ci * 128, 128), 128)
                dh = ALPHA * d2_ref[rows, :] + acc[rows, :]
                xhat = xh_ref[rows, :]
                g = dh * g_ref[...]
                dpre = rs_ref[rows, 0:1] * (g - jnp.mean(g, axis=1, keepdims=True)
                                            - xhat * jnp.mean(g * xhat, axis=1, keepdims=True))
                d_ref[rows, :] = dpre
                db_ref[rows, :] = dpre.astype(BF16)
                st_ref[0:1, :] += jnp.sum(dh * xhat, axis=0, keepdims=True)
                st_ref[1:2, :] += jnp.sum(dh, axis=0, keepdims=True)
                return carry

            lax.fori_loop(0, tm // 128, rows_chunk, 0)

    row = pl.BlockSpec((tm, D), lambda i, k: (i, 0))
    body, more_specs, more = _behind(body, 6, after)
    return pl.pallas_call(
        body, name="d_h1_ln1", grid=(S // tm, nk),
        in_specs=[pl.BlockSpec((tm, tk), lambda i, k: (i, k)),
                  pl.BlockSpec((None, D, tk), lambda i, k: (k // per, 0, k % per)),
                  row, row, pl.BlockSpec((tm, 128), lambda i, k: (i, 0)), pl.BlockSpec((1, D), lambda i, k: (0, 0))]
        + more_specs,
        out_specs=[row, row, pl.BlockSpec((8, D), lambda i, k: (0, 0))],
        out_shape=[jax.ShapeDtypeStruct((S, D), F32), jax.ShapeDtypeStruct((S, D), BF16),
                   jax.ShapeDtypeStruct((8, D), F32)],
        scratch_shapes=[pltpu.VMEM((tm, D), F32)],
        compiler_params=_params(2),
    )(dprea, w1_g, dpre2, xhat1, rstd1, g1, *more)


def _d_merged(dpre1b, wout_g, proj, ya, yb):
    tm, tn = 512, 1024

    def body(d_ref, w_ref, ga_ref, gb_ref, ya_ref, yb_ref, dya_ref, dyb_ref, dga_ref, dgb_ref):
        dm = _dot_nt(d_ref[...], w_ref[...])
        sa = _sigmoid(ga_ref[...])
        sb = _sigmoid(gb_ref[...])
        dya_ref[...] = (dm * sa).astype(BF16)
        dyb_ref[...] = (dm * sb).astype(BF16)
        dga_ref[...] = (dm * ya_ref[...].astype(F32) * sa * (1.0 - sa)).astype(BF16)
        dgb_ref[...] = (dm * yb_ref[...].astype(F32) * sb * (1.0 - sb)).astype(BF16)

    tile = pl.BlockSpec((tm, tn), lambda i, j: (i, j))
    return pl.pallas_call(
        body, name="d_merged", grid=(S // tm, D // tn),
        in_specs=[pl.BlockSpec((tm, D), lambda i, j: (i, 0)), pl.BlockSpec((tn, D), lambda i, j: (j, 0)),
                  pl.BlockSpec((tm, tn), lambda i, j: (i, 5 + j)), pl.BlockSpec((tm, tn), lambda i, j: (i, 7 + j)),
                  tile, tile],
        out_specs=[tile] * 4,
        out_shape=[jax.ShapeDtypeStruct((S, D), BF16)] * 4,
        compiler_params=_params(2),
    )(dpre1b, wout_g, proj, proj, ya, yb)


def _d_branches(dya, dyb, wpa_g, wpb_g, after=None):
    tk = 512

    def body(da_ref, db_ref, wa_ref, wb_ref, oa_ref, ob_ref):
        @pl.when(pl.program_id(0) == 0)
        def _():
            oa_ref[...] = jnp.zeros_like(oa_ref)
            ob_ref[...] = jnp.zeros_like(ob_ref)

        oa_ref[...] += _dot_nt(da_ref[...], wa_ref[...])
        ob_ref[...] += _dot_nt(db_ref[...], wb_ref[...])

    body, more_specs, more = _behind(body, 4, after)
    return pl.pallas_call(
        body, name="d_branches", grid=(D // tk,),
        in_specs=[pl.BlockSpec((S, tk), lambda k: (0, k)), pl.BlockSpec((S, tk), lambda k: (0, k)),
                  pl.BlockSpec((None, DA, tk), lambda k: (k, 0, 0)), pl.BlockSpec((None, DB, tk), lambda k: (k, 0, 0))]
        + more_specs,
        out_specs=[pl.BlockSpec((S, DA), lambda k: (0, 0)), pl.BlockSpec((S, DB), lambda k: (0, 0))],
        out_shape=[jax.ShapeDtypeStruct((S, DA), F32), jax.ShapeDtypeStruct((S, DB), F32)],
        compiler_params=_params(1),
    )(dya, dyb, wpa_g, wpb_g, *more)


def _gmlp_bwd(proj, dgmlp, ws, ws_t, bsp_b, gain_v, bias_v):
    def body(u_ref, vb_ref, dg_ref, ws_ref, wst_ref, bsp_ref, g_ref, be_ref, duv_ref, gws_ref, gbs_ref, st_ref):
        @pl.when(pl.program_id(0) == 0)
        def _():
            gws_ref[...] = jnp.zeros_like(gws_ref)
            gbs_ref[...] = jnp.zeros_like(gbs_ref)
            st_ref[...] = jnp.zeros_like(st_ref)

        u, tu, u_act, vb, tv, rstd, xhat, vn = _gmlp_parts(u_ref, vb_ref, g_ref, be_ref)
        dg = dg_ref[...]
        dz = dg * u_act
        row = lax.broadcasted_iota(jnp.int32, (128, 128), 0)
        col = lax.broadcasted_iota(jnp.int32, (128, 128), 1)
        causal = row >= col
        causal_t = row <= col
        dvn_parts = []
        z_parts = []
        for g in range(NH):
            cols = slice(g * 128, (g + 1) * 128)
            vng = vn[:, cols].astype(BF16)
            dzg = dz[:, cols]
            dzb = dzg.astype(BF16)
            wsg = jnp.where(causal, ws_ref[g], 0.0).astype(BF16)
            wsg_t = jnp.where(causal_t, wst_ref[g], 0.0).astype(BF16)
            z_parts.append(_dot(wsg, vng) + bsp_ref[g])
            gws_ref[g] += jnp.where(causal, _dot_nt(dzb, vng), 0.0)
            gbs_ref[g] += jnp.broadcast_to(jnp.sum(dzg, axis=1, keepdims=True), (128, 128))
            dvn_parts.append(_dot(wsg_t, dzb))
        z = jnp.concatenate(z_parts, axis=1)
        dvn = jnp.concatenate(dvn_parts, axis=1)
        du = dg * z * _gelu_grad(u, tu)
        st_ref[0:1, :] += jnp.sum(dvn * xhat, axis=0, keepdims=True)
        st_ref[1:2, :] += jnp.sum(dvn, axis=0, keepdims=True)
        gg = dvn * g_ref[...]
        dgv = rstd * (gg - jnp.mean(gg, axis=1, keepdims=True) - xhat * jnp.mean(gg * xhat, axis=1, keepdims=True))
        dvb = dgv * _gelu_grad(vb, tv)
        duv_ref[:, 0:DB] = du.astype(BF16)
        duv_ref[:, DB:2 * DB] = dvb.astype(BF16)

    full3 = pl.BlockSpec((NH, 128, 128), lambda c: (0, 0, 0))
    vec = pl.BlockSpec((1, DB), lambda c: (0, 0))
    return pl.pallas_call(
        body, name="gmlp_bwd", grid=(NBLK,),
        in_specs=[pl.BlockSpec((128, DB), lambda c: (c, 3)), pl.BlockSpec((128, DB), lambda c: (c, 4)),
                  pl.BlockSpec((128, DB), lambda c: (c, 0)), full3, full3, full3, vec, vec],
        out_specs=[pl.BlockSpec((128, 2 * DB), lambda c: (c, 0)), full3, full3, pl.BlockSpec((8, DB), lambda c: (0, 0))],
        out_shape=[jax.ShapeDtypeStruct((S, 2 * DB), BF16), jax.ShapeDtypeStruct((NH, 128, 128), F32),
                   jax.ShapeDtypeStruct((NH, 128, 128), F32), jax.ShapeDtypeStruct((8, DB), F32)],
        compiler_params=_params(1),
    )(proj, proj, dgmlp, ws, ws_t, bsp_b, gain_v, bias_v)


def _rel_bias_grad(ds_sums):
    buckets = jnp.asarray(np.stack([_bucket_tile(d) for _, d in PATTERNS]))

    def body(bk_ref, ds_ref, o_ref):
        row = lax.broadcasted_iota(jnp.int32, (N_BUCKETS, 128), 0)
        lane = lax.broadcasted_iota(jnp.int32, (N_BUCKETS, 128), 1)

        def one_bucket(t, out):
            hits = [bk_ref[p] == t for p in range(3)]
            for h in range(NH):
                tot = jnp.zeros((128, 256), F32)
                for p in range(3):
                    tot = tot + jnp.where(hits[p], ds_ref[p, h], 0.0)
                out = jnp.where((row == t) & (lane == h), jnp.sum(tot), out)
            return out

        o_ref[...] = lax.fori_loop(0, N_BUCKETS, one_bucket, jnp.zeros((N_BUCKETS, 128), F32))

    return pl.pallas_call(
        body, name="rel_bias_grad",
        in_specs=[pl.BlockSpec(memory_space=pltpu.VMEM)] * 2, out_specs=pl.BlockSpec(memory_space=pltpu.VMEM),
        out_shape=jax.ShapeDtypeStruct((N_BUCKETS, 128), F32),
        compiler_params=pltpu.CompilerParams(vmem_limit_bytes=VMEM_LIMIT),
    )(buckets, ds_sums)


def _d_x(dproj, win_g, dpre1, after=None):
    tm, tk = 512, 2304
    per = 2304 // tk
    nk = DIN // tk

    def body(a_ref, w_ref, d_ref, o_ref, acc):
        k = pl.program_id(1)

        @pl.when(k == 0)
        def _():
            acc[...] = ALPHA * d_ref[...]

        acc[...] += _dot_nt(a_ref[...], w_ref[...])

        @pl.when(k == nk - 1)
        def _():
            o_ref[...] = acc[...]

    row = pl.BlockSpec((tm, D), lambda i, k: (i, 0))
    body, more_specs, more = _behind(body, 3, after)
    return pl.pallas_call(
        body, name="d_x", grid=(S // tm, nk),
        in_specs=[pl.BlockSpec((tm, tk), lambda i, k: (i, k)),
                  pl.BlockSpec((None, D, tk), lambda i, k: (k // per, 0, k % per)), row] + more_specs,
        out_specs=row, out_shape=jax.ShapeDtypeStruct((S, D), F32),
        scratch_shapes=[pltpu.VMEM((tm, D), F32)],
        compiler_params=_params(2),
    )(dproj, win_g, dpre1, *more)


def _adamw(w, g, m, v, name):
    rows, cols = w.shape
    tm = max(t for t in range(8, 257, 8) if rows % t == 0)

    def body(w_ref, g_ref, m_ref, v_ref, d_ref, nm_ref, nv_ref, go_ref):
        g = g_ref[...]
        m = ADAM_B1 * m_ref[...] + (1.0 - ADAM_B1) * g
        v = ADAM_B2 * v_ref[...] + (1.0 - ADAM_B2) * (g * g)
        m_hat = m / (1.0 - ADAM_B1 ** ADAM_STEP)
        v_hat = v / (1.0 - ADAM_B2 ** ADAM_STEP)
        d_ref[...] = -ADAM_LR * (m_hat / (jnp.sqrt(v_hat) + ADAM_EPS) + ADAM_WD * w_ref[...])
        nm_ref[...] = m
        nv_ref[...] = v
        go_ref[...] = g

    spec = pl.BlockSpec((tm, cols), lambda i: (i, 0))
    return pl.pallas_call(
        body, name=name, grid=(rows // tm,), in_specs=[spec] * 4, out_specs=[spec] * 4,
        out_shape=[jax.ShapeDtypeStruct((rows, cols), F32)] * 4, compiler_params=_params(1),
    )(w, g, m, v)


def _position():
    x, y, c = lax.axis_index("x"), lax.axis_index("y"), lax.axis_index("c")
    chips = [(1 - x, y), (x, 1 - y), (1 - x, 1 - y)]
    return x, y, c, chips


def _remote(src, dst, send_sems, recv_sems, k, to):
    return pltpu.make_async_remote_copy(src_ref=src, dst_ref=dst, send_sem=send_sems.at[k], recv_sem=recv_sems.at[k],
                                        device_id=to, device_id_type=MESH)


def _place_shard(w, name, after=None):
    rows, cols = w.shape
    tm = 256
    x, y = lax.axis_index("x"), lax.axis_index("y")

    def body(chip_ref, w_ref, o_ref):
        o_ref[...] = w_ref[...].astype(BF16)

    more_specs, more = ([ANY], [after]) if after is not None else ([], [])
    if after is not None:
        inner = body
        body = lambda chip_ref, w_ref, after_ref, o_ref: inner(chip_ref, w_ref, o_ref)
    return pl.pallas_call(
        body, name=name,
        grid_spec=pltpu.PrefetchScalarGridSpec(
            num_scalar_prefetch=1, grid=(rows // tm,),
            in_specs=[pl.BlockSpec((tm, cols), lambda i, chip: (i, 0))] + more_specs,
            out_specs=pl.BlockSpec((None, tm, cols), lambda i, chip: (chip[0], i, 0))),
        out_shape=jax.ShapeDtypeStruct((N_CHIPS, rows, cols), BF16),
        compiler_params=_params(1),
    )(jnp.reshape(2 * x + y, (1,)).astype(jnp.int32), w, *more)


def _to_bf16(x, name, after=None):
    tm = 256

    def body(x_ref, o_ref):
        o_ref[...] = x_ref[...].astype(BF16)

    spec = pl.BlockSpec((tm, x.shape[1]), lambda i: (i, 0))
    body, more_specs, more = _behind(body, 1, after)
    return pl.pallas_call(
        body, name=name, grid=(x.shape[0] // tm,), in_specs=[spec] + more_specs, out_specs=spec,
        out_shape=jax.ShapeDtypeStruct(x.shape, BF16), compiler_params=_params(1),
    )(x, *more)


HBM = pl.BlockSpec(memory_space=pltpu.HBM)
SEM = pl.BlockSpec(memory_space=pltpu.SEMAPHORE)
EFFECT = pltpu.SideEffectType.DATAFLOW_SIDE_EFFECTING


def _comm_call(name, body, bufs, sems_in, sems_out, after=None, token=False):
    nb, ns, no = len(bufs), len(sems_in), len(sems_out)
    n_in = nb + ns + (after is not None)

    def wrapped(*refs):
        body(refs[:nb], refs[nb:nb + ns], refs[n_in + nb:n_in + nb + no])
        if token:
            refs[-1][...] = jnp.zeros((8, 128), F32)

    outs = pl.pallas_call(
        wrapped, name=name,
        in_specs=[HBM] * nb + [SEM] * ns + ([ANY] if after is not None else []),
        out_specs=[HBM] * nb + [SEM] * no + ([pl.BlockSpec(memory_space=pltpu.VMEM)] if token else []),
        out_shape=[pltpu.HBM(b.shape, b.dtype) for b in bufs] + [pltpu.SemaphoreType.DMA((k,)) for k in sems_out]
        + ([jax.ShapeDtypeStruct((8, 128), F32)] if token else []),
        input_output_aliases={i: i for i in range(nb)},
        compiler_params=pltpu.CompilerParams(has_side_effects=EFFECT),
    )(*[pltpu.with_memory_space_constraint(b, pltpu.HBM) for b in bufs], *sems_in, *([after] if after is not None else []))
    return list(outs[:nb]), list(outs[nb:nb + no]), (outs[-1] if token else None)


RING_STAGES = {"ici_near": 2, "ici_far": 2, "d2d_near": 2, "d2d_far": 1}


def _ring_copies(buf, send_sems, recv_sems, k0, stage):
    x, y, c, _ = _position()
    hr = buf.shape[1] // 2
    qr = hr // 2
    half = lambda chip, h: buf.at[chip, pl.ds(h * hr, hr), :]
    quarter = lambda chip, h, q: buf.at[chip, pl.ds(h * hr + q * qr, qr), :]
    mine, x_chip, y_chip, far_chip = 2 * x + y, 2 * (1 - x) + y, 2 * x + (1 - y), 2 * (1 - x) + (1 - y)
    to_x, to_y, sibling = (1 - x, y, c), (x, 1 - y, c), (x, y, 1 - c)
    if stage == "ici_near":
        moves = [(half(mine, c), to_x, half(x_chip, c)), (half(mine, c), to_y, half(y_chip, c))]
    elif stage == "ici_far":
        moves = [(quarter(x_chip, c, 0), to_y, quarter(far_chip, c, 0)),
                 (quarter(y_chip, c, 1), to_x, quarter(far_chip, c, 1))]
    elif stage == "d2d_near":
        moves = [(half(x_chip, c), sibling, half(x_chip, 1 - c)), (half(y_chip, c), sibling, half(y_chip, 1 - c))]
    else:
        moves = [(half(far_chip, c), sibling, half(far_chip, 1 - c))]
    sends = [_remote(src, src, send_sems, recv_sems, k0 + i, to) for i, (src, to, _) in enumerate(moves)]
    arrivals = [_remote(got, got, send_sems, recv_sems, k0 + i, (x, y, c)) for i, (_, _, got) in enumerate(moves)]
    return sends, arrivals


def _ring_call(name, groups, actions, after=None):
    tags = list(dict.fromkeys(t for _, t, _ in actions))
    counts = {t: len(groups[t]["bufs"]) for t in tags}
    first = {t: sum(counts[u] for u in tags[:i]) for i, t in enumerate(tags)}
    waits = [(t, s) for v, t, s in actions if v == "wait"]
    starts = [(t, s) for v, t, s in actions if v == "start"]

    def body(bufs, sems_in, sems_out):
        for verb, t, s in actions:
            at, sems = (starts.index((t, s)), sems_out) if verb == "start" else (waits.index((t, s)), sems_in)
            for w in range(counts[t]):
                sends, arrivals = _ring_copies(bufs[first[t] + w], sems[2 * at], sems[2 * at + 1], RING_STAGES[s] * w, s)
                if verb == "start":
                    for cp in sends:
                        cp.start()
                else:
                    for cp in arrivals:
                        cp.wait_recv()
                    for cp in sends:
                        cp.wait_send()

    bufs, sems, token = _comm_call(
        name, body, [b for t in tags for b in groups[t]["bufs"]],
        [sem for t, s in waits for sem in groups[t]["sems"][s]],
        [RING_STAGES[s] * counts[t] for t, s in starts for _ in (0, 1)], after, token=True)
    for t in tags:
        groups[t]["bufs"] = bufs[first[t]:first[t] + counts[t]]
    for t, s in waits:
        del groups[t]["sems"][s]
    for i, (t, s) in enumerate(starts):
        groups[t]["sems"][s] = (sems[2 * i], sems[2 * i + 1])
    return token


def _cx_copies(src, dst, send_sems, recv_sems, k0):
    x, y, c, chips = _position()
    sends = [_remote(src.at[2 * cx + cy], dst.at[2 * x + y], send_sems, recv_sems, k0 + j, (cx, cy, c))
             for j, (cx, cy) in enumerate(chips)]
    arrivals = [_remote(dst.at[2 * cx + cy], dst.at[2 * cx + cy], send_sems, recv_sems, k0 + j, (x, y, c))
                for j, (cx, cy) in enumerate(chips)]
    return sends, arrivals


def _cx_start(name, pair_sums):
    n = len(pair_sums)
    landing = [lax.empty(p.shape, p.dtype) for p in pair_sums]

    def body(bufs, _, sems):
        for w in range(n):
            for cp in _cx_copies(bufs[w], bufs[n + w], sems[0], sems[1], 3 * w)[0]:
                cp.start()

    bufs, sems, token = _comm_call(name, body, list(pair_sums) + landing, [], [3 * n, 3 * n], token=True)
    return (bufs, sems), token


def _cx_wait(name, state, after):
    bufs, sems = state
    n = len(bufs) // 2

    def body(refs, sems_in, _):
        for w in range(n):
            sends, arrivals = _cx_copies(refs[w], refs[n + w], sems_in[0], sems_in[1], 3 * w)
            for cp in arrivals:
                cp.wait_recv()
            for cp in sends:
                cp.wait_send()

    bufs, _, _ = _comm_call(name, body, bufs, sems, [], after)
    return bufs[:n], bufs[n:]


def _px_copies(src, dst, send_sems, recv_sems, k):
    x, y, c, _ = _position()
    hr = src.shape[1] // 2
    send = _remote(src.at[:, pl.ds((1 - c) * hr, hr), :], dst, send_sems, recv_sems, k, (x, y, 1 - c))
    arrival = _remote(dst, dst, send_sems, recv_sems, k, (x, y, c))
    return send, arrival


def _px_start(name, grads):
    n = len(grads)
    landing = [lax.empty((N_CHIPS, g.shape[1] // 2, g.shape[2]), g.dtype) for g in grads]

    def body(bufs, _, sems):
        for w in range(n):
            _px_copies(bufs[w], bufs[n + w], sems[0], sems[1], w)[0].start()

    bufs, sems, token = _comm_call(name, body, list(grads) + landing, [], [n, n], token=True)
    return (bufs, sems), token


def _px_wait(name, state, after):
    bufs, sems = state
    n = len(bufs) // 2

    def body(refs, sems_in, _):
        for w in range(n):
            send, arrival = _px_copies(refs[w], refs[n + w], sems_in[0], sems_in[1], w)
            arrival.wait_recv()
            send.wait_send()

    bufs, _, _ = _comm_call(name, body, bufs, sems, [], after)
    return bufs[:n], bufs[n:]


def _pair_sum(grad, got, name):
    _, rows, cols = grad.shape
    hr = rows // 2
    tm = min(hr, 256)
    nb = hr // tm
    c = lax.axis_index("c")

    def body(c_ref, g_ref, o_ref, out_ref):
        out_ref[...] = (g_ref[...].astype(F32) + o_ref[...].astype(F32)).astype(BF16)

    return pl.pallas_call(
        body, name=name,
        grid_spec=pltpu.PrefetchScalarGridSpec(
            num_scalar_prefetch=1, grid=(N_CHIPS, nb),
            in_specs=[pl.BlockSpec((None, tm, cols), lambda s, i, c_ref: (s, c_ref[0] * nb + i, 0)),
                      pl.BlockSpec((None, tm, cols), lambda s, i, c_ref: (s, i, 0))],
            out_specs=pl.BlockSpec((None, tm, cols), lambda s, i, c_ref: (s, i, 0))),
        out_shape=jax.ShapeDtypeStruct((N_CHIPS, hr, cols), BF16),
        compiler_params=_params(2),
    )(jnp.reshape(c, (1,)).astype(jnp.int32), grad, got)


def _chip_sum(parts, pair_sums, name):
    _, hr, cols = parts.shape
    tm = min(hr, 256)
    nb = hr // tm
    x, y, c = lax.axis_index("x"), lax.axis_index("y"), lax.axis_index("c")

    def body(pos_ref, p_ref, own_ref, o_ref):
        chip = pos_ref[0]
        own = own_ref[...].astype(F32)
        term = lambda s: jnp.where(chip == s, own, p_ref[s].astype(F32))
        o_ref[...] = ((term(0) + term(1)) + term(2)) + term(3)

    return pl.pallas_call(
        body, name=name,
        grid_spec=pltpu.PrefetchScalarGridSpec(
            num_scalar_prefetch=1, grid=(nb,),
            in_specs=[pl.BlockSpec((N_CHIPS, tm, cols), lambda i, pos: (0, i, 0)),
                      pl.BlockSpec((None, tm, cols), lambda i, pos: (pos[0], i, 0))],
            out_specs=pl.BlockSpec((tm, cols), lambda i, pos: (pos[1] * nb + i, 0))),
        out_shape=jax.ShapeDtypeStruct((2 * hr, cols), F32), compiler_params=_params(1),
    )(jnp.stack([2 * x + y, c]).astype(jnp.int32), parts, pair_sums)


def _share_copies(buf, send_sems, recv_sems, k):
    x, y, c, _ = _position()
    hr = buf.shape[0] // 2
    mine, theirs = buf.at[pl.ds(c * hr, hr), :], buf.at[pl.ds((1 - c) * hr, hr), :]
    return (_remote(mine, mine, send_sems, recv_sems, k, (x, y, 1 - c)),
            _remote(theirs, theirs, send_sems, recv_sems, k, (x, y, c)))


def _share_start(name, bufs):
    n = len(bufs)

    def body(refs, _, sems):
        for w in range(n):
            _share_copies(refs[w], sems[0], sems[1], w)[0].start()

    bufs, sems, token = _comm_call(name, body, list(bufs), [], [n, n], token=True)
    return (bufs, sems), token


def _share_wait(name, state, after):
    bufs, sems = state

    def body(refs, sems_in, _):
        for w in range(len(bufs)):
            send, arrival = _share_copies(refs[w], sems_in[0], sems_in[1], w)
            arrival.wait_recv()
            send.wait_send()

    return _comm_call(name, body, bufs, sems, [], after)[0]


def _allreduce_small(g):
    rows = g.shape[0]
    half = rows // 2

    def body(g_ref, o_ref, sib, slots, send_sems, recv_sems):
        x, y, c, chips = _position()
        me, sibling = (x, y, c), (x, y, 1 - c)
        my_chip = 2 * x + y
        mine = pl.ds(pl.multiple_of(c * half, 8), half)
        theirs = pl.ds(pl.multiple_of((1 - c) * half, 8), half)
        pair = _remote(g_ref.at[theirs], sib, send_sems, recv_sems, 0, sibling)
        pair.start()
        pair.wait()
        slots[my_chip] = g_ref[mine, :] + sib[...]
        sent = []
        for j, (cx, cy) in enumerate(chips):
            cp = _remote(slots.at[my_chip], slots.at[my_chip], send_sems, recv_sems, 1 + j, (cx, cy, c))
            cp.start()
            sent.append(cp)
        for j, (cx, cy) in enumerate(chips):
            got = slots.at[2 * cx + cy]
            _remote(got, got, send_sems, recv_sems, 1 + j, me).wait_recv()
        for cp in sent:
            cp.wait_send()
        o_ref[mine, :] = ((slots[0] + slots[1]) + slots[2]) + slots[3]
        swap = _remote(o_ref.at[mine], o_ref.at[mine], send_sems, recv_sems, 4, sibling)
        swap.start()
        swap.wait()

    vm = pl.BlockSpec(memory_space=pltpu.VMEM)
    return pl.pallas_call(
        body, name="allreduce_small",
        in_specs=[vm], out_specs=vm, out_shape=jax.ShapeDtypeStruct((rows, 128), F32),
        scratch_shapes=[pltpu.VMEM((half, 128), F32), pltpu.VMEM((N_CHIPS, half, 128), F32),
                        pltpu.SemaphoreType.DMA((5,)), pltpu.SemaphoreType.DMA((5,))],
        compiler_params=pltpu.CompilerParams(vmem_limit_bytes=VMEM_LIMIT),
    )(g)


_SMALL =("rel_bias", "ln_v_gain", "ln_v_bias", "w_spatial", "b_spatial", "ln1_gain", "ln1_bias",
          "b_ff1", "b_ff2", "ln2_gain", "ln2_bias")
_SMALL_ROWS = 1200
_LOSS_AT = (152832 // 128, 0)


def _pack_small(parts):
    flat = jnp.concatenate([parts[k].reshape(-1).astype(F32) for k in _SMALL])
    flat = jnp.pad(flat, (0, _SMALL_ROWS * 128 - flat.shape[0]))
    return flat.reshape(_SMALL_ROWS, 128)


def _unpack_small(packed, like):
    flat = packed.reshape(-1)
    out, at = {}, 0
    for k in _SMALL:
        n = math.prod(like[k].shape)
        out[k] = flat[at:at + n].reshape(like[k].shape)
        at += n
    return out


def kernel(x, w_in, rel_bias, ln_v_gain, ln_v_bias, w_spatial, b_spatial, w_proj_a, w_proj_b, w_out, ln1_gain, ln1_bias, w_ff1, b_ff1, w_ff2, b_ff2, ln2_gain, ln2_bias, loss_target, m_w_in, m_rel_bias, m_ln_v_gain, m_ln_v_bias, m_w_spatial, m_b_spatial, m_w_proj_a, m_w_proj_b, m_w_out, m_ln1_gain, m_ln1_bias, m_w_ff1, m_b_ff1, m_w_ff2, m_b_ff2, m_ln2_gain, m_ln2_bias, v_w_in, v_rel_bias, v_ln_v_gain, v_ln_v_bias, v_w_spatial, v_b_spatial, v_w_proj_a, v_w_proj_b, v_w_out, v_ln1_gain, v_ln1_bias, v_w_ff1, v_b_ff1, v_w_ff2, v_b_ff2, v_ln2_gain, v_ln2_bias):
    args = dict(locals())
    big = ("w_in", "w_proj_a", "w_proj_b", "w_out", "w_ff1", "w_ff2")
    weights = ("w_in", "rel_bias", "ln_v_gain", "ln_v_bias", "w_spatial", "b_spatial", "w_proj_a", "w_proj_b", "w_out",
               "ln1_gain", "ln1_bias", "w_ff1", "b_ff1", "w_ff2", "b_ff2", "ln2_gain", "ln2_bias")

    xs = x[0]
    target = loss_target[0]

    ring = {"a": {"bufs": [_place_shard(w_in[0], "place_w_in")], "sems": {}}}
    tok = _ring_call("allgather_a_near", ring, [("start", "a", "ici_near")])
    placed = [_place_shard(args[k][0], f"place_{k}", after=tok) for k in big[1:]]
    for tag, bufs in (("b", placed[0:3]), ("c", placed[3:4]), ("d", placed[4:5])):
        ring[tag] = {"bufs": bufs, "sems": {}}
    xb = _to_bf16(xs, "x_to_bf16", after=placed[4])

    mx, my = lax.axis_index("x"), lax.axis_index("y")
    own = jnp.reshape(2 * mx + my, (1,)).astype(jnp.int32)
    near = jnp.stack([2 * (1 - mx) + my, 2 * mx + (1 - my)]).astype(jnp.int32)
    far = jnp.reshape(2 * (1 - mx) + (1 - my), (1,)).astype(jnp.int32)
    proj = _proj(xb, ring["a"]["bufs"][0], own, "proj_own")
    _ring_call("allgather_a_far", ring, [("wait", "a", "ici_near"), ("start", "a", "ici_far"), ("start", "a", "d2d_near"),
                                         ("start", "b", "ici_near"), ("start", "c", "ici_near")], after=proj)
    _ring_call("allgather_a_near_done", ring, [("wait", "a", "d2d_near")])
    proj = _proj(xb, ring["a"]["bufs"][0], near, "proj_near", into=proj)
    _ring_call("allgather_a_last", ring, [("wait", "a", "ici_far"), ("start", "a", "d2d_far")], after=proj)
    _ring_call("allgather_a_done", ring, [("wait", "a", "d2d_far")])
    (win_g,) = ring["a"]["bufs"]
    proj = _proj(xb, win_g, far, "proj_far", into=proj)
    _ring_call("allgather_b_far", ring, [("wait", "b", "ici_near"), ("start", "b", "ici_far"), ("start", "b", "d2d_near")],
               after=proj)
    ws = w_spatial[0]
    ws_t = jnp.transpose(ws, (0, 2, 1))
    bsp_b = jnp.broadcast_to(b_spatial[0][:, :, None], (NH, 128, 128))
    gmlp = _gmlp_fwd(proj, ws, bsp_b, ln_v_gain, ln_v_bias)
    attn, lse = _attention_fwd(proj, rel_bias)
    _ring_call("allgather_b_last_c_far", ring,
               [("wait", "b", "ici_far"), ("start", "b", "d2d_far"),
                ("wait", "c", "ici_near"), ("start", "c", "ici_far"), ("start", "c", "d2d_near"),
                ("start", "d", "ici_near")], after=attn)
    _ring_call("allgather_b_done", ring, [("wait", "b", "d2d_near"), ("wait", "b", "d2d_far")])
    wpa_g, wpb_g, wout_g = ring["b"]["bufs"]
    wout_full = wout_g.reshape(D, D)
    ya, yb, merged = _branch(attn, gmlp, wpa_g, wpb_g, proj)
    xhat1, rstd1, h1b = _out_ln1(merged, wout_full, xs, ln1_gain, ln1_bias)
    _ring_call("allgather_c_last_d_far", ring,
               [("wait", "c", "ici_far"), ("start", "c", "d2d_far"),
                ("wait", "d", "ici_near"), ("start", "d", "ici_far"), ("start", "d", "d2d_near")], after=h1b)
    _ring_call("allgather_c_done", ring, [("wait", "c", "d2d_near"), ("wait", "c", "d2d_far")])
    (w1_g,) = ring["c"]["bufs"]
    a, r = _ff1(h1b, w1_g, b_ff1)
    _ring_call("allgather_d_last", ring, [("wait", "d", "ici_far"), ("start", "d", "d2d_far")], after=a)
    _ring_call("allgather_d_done", ring, [("wait", "d", "d2d_near"), ("wait", "d", "d2d_far")])
    (w2_g,) = ring["d"]["bufs"]
    w2_full = w2_g.reshape(DFF, D)
    dpre2, dpre2b, st2 = _ff2_ln2_loss(a, w2_full, xhat1, ln1_gain, ln1_bias, b_ff2, ln2_gain, ln2_bias, target)

    def pair_and_chip(tag, state, after):
        local, from_sibling = _px_wait(f"pair_exchange_wait_{tag}", state, after)
        pair_sums = [_pair_sum(g, o, f"pair_sum_{tag}_{i}") for i, (g, o) in enumerate(zip(local, from_sibling))]
        return _cx_start(f"chip_exchange_start_{tag}", pair_sums)

    g_w2 = _grad_w(a, dpre2b, "grad_w_ff2", 512, 2048, False)
    px, tok = _px_start("pair_exchange_start_w_ff2", [g_w2.reshape(N_CHIPS, DFF // N_CHIPS, D)])
    dprea, g_b1 = _d_ff1(dpre2b, w2_full, r, after=tok)
    cx_w2, tok = pair_and_chip("w_ff2", px, dprea)
    g_w1 = _grad_w(h1b, dprea, "grad_w_ff1", 512, 2048, True, after=tok)
    px, tok = _px_start("pair_exchange_start_w_ff1", [g_w1])
    dpre1, dpre1b, st1 = _d_h1_ln1(dprea, w1_g, dpre2, xhat1, rstd1, ln1_gain, after=tok)
    cx_w1, tok = pair_and_chip("w_ff1", px, dpre1b)
    g_wout = _grad_w(merged, dpre1b, "grad_w_out", 512, 2048, False, after=tok)
    dya, dyb, dga, dgb = _d_merged(dpre1b, wout_full, proj, ya, yb)
    g_wpa = _grad_w(attn, dya, "grad_w_proj_a", 1024, 512, True)
    g_wpb = _grad_w(gmlp, dyb, "grad_w_proj_b", 1024, 512, True)
    px, tok = _px_start("pair_exchange_start_b", [g_wpa, g_wpb, g_wout.reshape(N_CHIPS, D // N_CHIPS, D)])
    dattn, dgmlp = _d_branches(dya, dyb, wpa_g, wpb_g, after=tok)
    duv, g_ws, g_bs, stv = _gmlp_bwd(proj, dgmlp, ws, ws_t, bsp_b, ln_v_gain, ln_v_bias)
    cx_b, tok = pair_and_chip("b", px, duv)
    dq, dk, dv, ds_sums = _attention_bwd(proj, dattn, attn, lse, rel_bias, after=tok)
    g_rb = _rel_bias_grad(ds_sums)[:, :NH]

    small_g = dict(rel_bias=g_rb, ln_v_gain=stv[0], ln_v_bias=stv[1], w_spatial=g_ws, b_spatial=g_bs[:, :, 0],
                   ln1_gain=st1[0], ln1_bias=st1[1], b_ff1=g_b1, b_ff2=st2[2], ln2_gain=st2[0], ln2_bias=st2[1])
    gs = _allreduce_small(_pack_small(small_g).at[_LOSS_AT].set(st2[3, 0]))
    ds_, ms_, vs_, _ = _adamw(_pack_small({k: args[k] for k in _SMALL}), gs,
                           _pack_small({k: args["m_" + k] for k in _SMALL}),
                           _pack_small({k: args["v_" + k] for k in _SMALL}), "adamw_small")
    like = {k: args[k] for k in _SMALL}
    grads, deltas, new_m, new_v = (_unpack_small(t, like) for t in (gs, ds_, ms_, vs_))

    dproj = jnp.concatenate([dq, dk, dv, duv, dga, dgb], axis=1)
    g_win = _grad_w(xb, dproj, "grad_w_in", 512, 2304, True, after=gs)
    px, tok = _px_start("pair_exchange_start_w_in", [g_win])

    def chip_sums(tag, state, names, after):
        pair_sums, from_chips = _cx_wait(f"chip_exchange_wait_{tag}", state, after)
        halves = [_chip_sum(p, own, f"chip_sum_{k}") for p, own, k in zip(from_chips, pair_sums, names)]
        return _share_start(f"share_start_{tag}", halves)

    def adam(tag, state, names, after):
        last = None
        for k, g in zip(names, _share_wait(f"share_wait_{tag}", state, after)):
            d_, m_, v_, g_ = _adamw(args[k][0], g, args["m_" + k][0], args["v_" + k][0], f"adamw_{k}")
            grads[k], deltas[k], new_m[k], new_v[k] = g_[None], d_[None], m_[None], v_[None]
            last = d_
        return last

    sh_w2, tok = chip_sums("w_ff2", cx_w2, ["w_ff2"], tok)
    sh_w1, tok = chip_sums("w_ff1", cx_w1, ["w_ff1"], tok)
    sh_b, tok = chip_sums("b", cx_b, ["w_proj_a", "w_proj_b", "w_out"], tok)
    cx_in, tok = pair_and_chip("w_in", px, tok)
    grad_x = _d_x(dproj, win_g, dpre1, after=tok)
    done = adam("w_ff2", sh_w2, ["w_ff2"], grad_x)
    done = adam("w_ff1", sh_w1, ["w_ff1"], done)
    done = adam("b", sh_b, ["w_proj_a", "w_proj_b", "w_out"], done)
    sh_in, tok = chip_sums("w_in", cx_in, ["w_in"], done)
    adam("w_in", sh_in, ["w_in"], tok)

    loss = gs[_LOSS_AT] * (0.5 / D)
    return (loss, grad_x[None], *[grads[k] for k in weights], *[deltas[k] for k in weights],
            *[new_m[k] for k in weights], *[new_v[k] for k in weights])
```

```python
import math

import numpy as np
import jax
import jax.numpy as jnp
from jax import lax
from jax.experimental import pallas as pl
from jax.experimental.pallas import tpu as pltpu

F32 = jnp.float32
BF16 = jnp.bfloat16

S = 2048
D = 2048
DA = 1024
DB = 1024
DFF = 8192
DIN = 9216
NH = 8
HD = 128
NBLK = 16
PATTERNS = ((128, 1), (512, 4), (2048, 16))
N_BUCKETS = 32
MAX_DISTANCE = 2048
ALPHA = 2.0 ** 0.25
LN_EPS = 1e-5
NEG_INF = -1e30
SCALE = HD ** -0.5
N_CHIPS = 4

ADAM_LR = 0.001
ADAM_B1 = 0.9
ADAM_B2 = 0.999
ADAM_EPS = 1e-08
ADAM_WD = 0.01
ADAM_STEP = 10

VMEM_LIMIT = 56 * 1024 * 1024
MESH = pl.DeviceIdType.MESH
ANY = pl.BlockSpec(memory_space=pl.ANY)


def _params(n_axes, vmem=VMEM_LIMIT):
    return pltpu.CompilerParams(dimension_semantics=("arbitrary",) * n_axes, vmem_limit_bytes=vmem)


def _bucket_tile(dilation):
    qi = np.arange(128)[:, None]
    kj = np.arange(256)[None, :]
    n = np.clip(128 + qi - kj, 0, 128) * dilation
    max_exact = N_BUCKETS // 2
    nf = np.maximum(n, 1).astype(np.float32)
    large = max_exact + (np.log(nf / np.float32(max_exact)) / np.float32(math.log(MAX_DISTANCE / max_exact))
                         * np.float32(N_BUCKETS - max_exact)).astype(np.int32)
    large = np.minimum(large, N_BUCKETS - 1)
    return np.where(n < max_exact, n, large).astype(np.int32)


def _gelu(x):
    c = math.sqrt(2.0 / math.pi)
    t = jnp.tanh(c * (x + 0.044715 * x * x * x))
    return 0.5 * x * (1.0 + t), t


def _gelu_grad(x, t):
    c = math.sqrt(2.0 / math.pi)
    return 0.5 * (1.0 + t) + 0.5 * x * (1.0 - t * t) * c * (1.0 + 3.0 * 0.044715 * x * x)


def _sigmoid(x):
    return 1.0 / (1.0 + jnp.exp(-x))


def _dot(a, b):
    return jnp.dot(a, b, preferred_element_type=F32)


def _behind(body, n_in, after):
    if after is None:
        return body, [], []
    return (lambda *refs: body(*refs[:n_in], *refs[n_in + 1:])), [ANY], [after]


def _dot_nt(a, b):
    return lax.dot_general(a, b, (((1,), (1,)), ((), ())), preferred_element_type=F32)


def _proj(xb, win_g, shards, name, into=None):
    tn = 768
    per = 2304 // tn

    def body(shards_ref, x_ref, w_ref, *rest):
        rest[-1][...] = _dot(x_ref[...], w_ref[...])

    in_specs = [pl.BlockSpec((S, D), lambda j, sh: (0, 0)),
                pl.BlockSpec((None, D, tn), lambda j, sh: (sh[j // per], 0, j % per))]
    return pl.pallas_call(
        body, name=name,
        grid_spec=pltpu.PrefetchScalarGridSpec(
            num_scalar_prefetch=1, grid=(shards.shape[0] * per,),
            in_specs=in_specs + ([ANY] if into is not None else []),
            out_specs=pl.BlockSpec((S, tn), lambda j, sh: (0, sh[j // per] * per + j % per))),
        out_shape=jax.ShapeDtypeStruct((S, DIN), F32),
        input_output_aliases={3: 0} if into is not None else {},
        compiler_params=_params(1),
    )(shards, xb, win_g, *([into] if into is not None else []))


FWD_HEADS_PER_STEP = 4
BWD_HEADS_PER_STEP = 2


def _head_bias_tiles(rb_ref, bk_ref, bias_scr, first_head, hps):
    qi = lax.broadcasted_iota(jnp.int32, (128, 256), 0)
    kj = lax.broadcasted_iota(jnp.int32, (128, 256), 1)
    steps = 128 + qi - kj
    band = (steps >= 0) & (steps <= 128)
    bias_scr[...] = jnp.zeros_like(bias_scr)
    for p in range(len(PATTERNS)):
        bucket = bk_ref[p]

        def one_bucket(t, carry):
            hit = bucket == t
            for j in range(hps):
                bias_scr[p, j] = jnp.where(hit, rb_ref[t, first_head + j], bias_scr[p, j])
            return carry

        lax.fori_loop(0, N_BUCKETS, one_bucket, 0)
        for j in range(hps):
            bias_scr[p, j] = jnp.where(band, bias_scr[p, j], NEG_INF)


def _block_rows(b, dilation):
    nblk = NBLK // dilation
    r, n = b // nblk, b % nblk
    start = r + n * (128 * dilation)
    prev_start = jnp.maximum(start - 128 * dilation, r)
    if dilation == 1:
        return pl.ds(pl.multiple_of(start, 128), 128), pl.ds(pl.multiple_of(prev_start, 128), 128), n > 0
    return pl.ds(start, 128, stride=dilation), pl.ds(prev_start, 128, stride=dilation), n > 0


def _head_specs(first, hps):
    return [pl.BlockSpec((S, HD), lambda g, j=j: (0, first + g * hps + j)) for j in range(hps)]


def _heads_spec(hps):
    return pl.BlockSpec((S, hps * HD), lambda g: (0, g))


def _attention_fwd(proj, rel_bias):
    hps = FWD_HEADS_PER_STEP
    buckets = jnp.asarray(np.stack([_bucket_tile(d) for _, d in PATTERNS]))

    def body(rb_ref, bk_ref, *refs):
        q_refs, k_refs, v_refs = (refs[i * hps:(i + 1) * hps] for i in range(3))
        o_ref, lse_ref, bias_scr = refs[3 * hps:3 * hps + 3]
        acc_scrs, m_scrs, l_scrs = (refs[3 * hps + 3 + i * hps:3 * hps + 3 + (i + 1) * hps] for i in range(3))
        _head_bias_tiles(rb_ref, bk_ref, bias_scr, pl.program_id(0) * hps, hps)
        kj = lax.broadcasted_iota(jnp.int32, (128, 256), 1)
        for p, (_, d) in enumerate(PATTERNS):
            prev_blocks = NBLK // d > 1

            def block(b, carry):
                units = [(j,) + _block_rows(blk, d) for blk in (b, b + NBLK // 2) for j in range(hps)]
                scores = []
                for j, rows, prows, _ in units:
                    q = q_refs[j][rows, :].astype(BF16)
                    cur = _dot_nt(q, k_refs[j][rows, :].astype(BF16))
                    if prev_blocks:
                        cur = jnp.concatenate([_dot_nt(q, k_refs[j][prows, :].astype(BF16)), cur], axis=1)
                    scores.append(cur)
                soft = []
                for u, (j, _, _, has_prev) in enumerate(units):
                    if prev_blocks:
                        s = jnp.where((kj >= 128) | has_prev, scores[u] * SCALE + bias_scr[p, j], NEG_INF)
                    else:
                        s = scores[u] * SCALE + bias_scr[p, j, :, 128:256]
                    m = jnp.max(s, axis=1, keepdims=True)
                    e = jnp.exp(s - m)
                    soft.append((m, jnp.sum(e, axis=1, keepdims=True), e.astype(BF16)))
                outs = []
                for u, (j, rows, prows, _) in enumerate(units):
                    e = soft[u][2]
                    if prev_blocks:
                        outs.append(_dot(e[:, :128], v_refs[j][prows, :].astype(BF16))
                                    + _dot(e[:, 128:], v_refs[j][rows, :].astype(BF16)))
                    else:
                        outs.append(_dot(e, v_refs[j][rows, :].astype(BF16)))
                for u, (j, rows, _, _) in enumerate(units):
                    acc_scr, m_scr, l_scr = acc_scrs[j], m_scrs[j], l_scrs[j]
                    (m, den, _), o = soft[u], outs[u]
                    if p == 0:
                        acc_scr[rows, :] = o
                        m_scr[rows, :] = jnp.broadcast_to(m, (128, HD))
                        l_scr[rows, :] = jnp.broadcast_to(den, (128, HD))
                    else:
                        m_old = m_scr[rows, :]
                        m_new = jnp.maximum(m_old, m)
                        w_old, w_new = jnp.exp(m_old - m_new), jnp.exp(m - m_new)
                        acc_scr[rows, :] = acc_scr[rows, :] * w_old + o * w_new
                        l_scr[rows, :] = l_scr[rows, :] * w_old + den * w_new
                        m_scr[rows, :] = m_new
                return carry

            lax.fori_loop(0, NBLK // 2, block, 0)
        for j in range(hps):
            cols = slice(j * HD, (j + 1) * HD)
            den = l_scrs[j][...]
            o_ref[:, cols] = (acc_scrs[j][...] / den).astype(BF16)
            lse_ref[:, cols] = m_scrs[j][...] + jnp.log(den)

    return pl.pallas_call(
        body, name="attention_fwd", grid=(NH // hps,),
        in_specs=[pl.BlockSpec(memory_space=pltpu.SMEM), pl.BlockSpec((3, 128, 256), lambda g: (0, 0, 0))]
        + _head_specs(0, hps) + _head_specs(NH, hps) + _head_specs(2 * NH, hps),
        out_specs=[_heads_spec(hps), _heads_spec(hps)],
        out_shape=[jax.ShapeDtypeStruct((S, DA), BF16), jax.ShapeDtypeStruct((S, DA), F32)],
        scratch_shapes=[pltpu.VMEM((3, hps, 128, 256), F32)] + [pltpu.VMEM((S, HD), F32)] * (3 * hps),
        compiler_params=_params(1),
    )(rel_bias, buckets, *([proj] * (3 * hps)))


def _attention_bwd(proj, dattn, attn, lse, rel_bias, after=None):
    hps = BWD_HEADS_PER_STEP

    def body(rb_ref, bk_ref, *refs):
        q_refs, k_refs, v_refs, do_refs, o_refs, lse_refs = (refs[i * hps:(i + 1) * hps] for i in range(6))
        dq_ref, dk_ref, dv_ref, ds_ref, bias_scr = refs[6 * hps:6 * hps + 5]
        dl_scrs, dq_scrs, dk_scrs, dv_scrs = (refs[6 * hps + 5 + i * hps:6 * hps + 5 + (i + 1) * hps] for i in range(4))
        _head_bias_tiles(rb_ref, bk_ref, bias_scr, pl.program_id(0) * hps, hps)
        ds_ref[...] = jnp.zeros_like(ds_ref)
        for j in range(hps):
            dq_scrs[j][...] = jnp.zeros((S, HD), F32)
            dk_scrs[j][...] = jnp.zeros((S, HD), F32)
            dv_scrs[j][...] = jnp.zeros((S, HD), F32)
            prod = do_refs[j][...] * o_refs[j][...].astype(F32)
            dl_scrs[j][...] = jnp.broadcast_to(jnp.sum(prod, axis=1, keepdims=True), (S, HD))
        for p, (_, d) in enumerate(PATTERNS):
            prev_blocks = NBLK // d > 1

            def block(b, carry):
                units = [(j,) + _block_rows(blk, d) for blk in (b, b + NBLK // 2) for j in range(hps)]
                ops, raw = [], []
                for j, rows, prows, _ in units:
                    q, do = q_refs[j][rows, :].astype(BF16), do_refs[j][rows, :].astype(BF16)
                    kc, vc = k_refs[j][rows, :].astype(BF16), v_refs[j][rows, :].astype(BF16)
                    if prev_blocks:
                        kp, vp = k_refs[j][prows, :].astype(BF16), v_refs[j][prows, :].astype(BF16)
                        ops.append((q, do, kc, kp))
                        raw.append((_dot_nt(q, kc), _dot_nt(do, vc), _dot_nt(q, kp), _dot_nt(do, vp)))
                    else:
                        ops.append((q, do, kc))
                        raw.append((_dot_nt(q, kc), _dot_nt(do, vc)))
                probs = []
                for u, (j, rows, _, has_prev) in enumerate(units):
                    lse_b, dl_b = lse_refs[j][rows, :], dl_scrs[j][rows, :]
                    p_c = jnp.exp(raw[u][0] * SCALE + bias_scr[p, j, :, 128:256] - lse_b)
                    ds_c = p_c * (raw[u][1] - dl_b)
                    ds_ref[p, j, :, 128:256] += ds_c
                    if prev_blocks:
                        p_p = jnp.where(has_prev, jnp.exp(raw[u][2] * SCALE + bias_scr[p, j, :, 0:128] - lse_b), 0.0)
                        ds_p = p_p * (raw[u][3] - dl_b)
                        ds_ref[p, j, :, 0:128] += ds_p
                        probs.append((p_c, ds_c, p_p, ds_p))
                    else:
                        probs.append((p_c, ds_c))
                grads = []
                for u in range(len(units)):
                    q, do, kc = ops[u][:3]
                    p_c, ds_c = probs[u][:2]
                    dq = _dot(ds_c.astype(BF16), kc)
                    cur = (_dot(ds_c.T.astype(BF16), q) * SCALE, _dot(p_c.T.astype(BF16), do))
                    if prev_blocks:
                        p_p, ds_p = probs[u][2:]
                        dq = dq + _dot(ds_p.astype(BF16), ops[u][3])
                        cur = cur + (_dot(ds_p.T.astype(BF16), q) * SCALE, _dot(p_p.T.astype(BF16), do))
                    grads.append((dq * SCALE,) + cur)
                for u, (j, rows, prows, _) in enumerate(units):
                    dq_scrs[j][rows, :] += grads[u][0]
                    dk_scrs[j][rows, :] += grads[u][1]
                    dv_scrs[j][rows, :] += grads[u][2]
                    if prev_blocks:
                        dk_scrs[j][prows, :] += grads[u][3]
                        dv_scrs[j][prows, :] += grads[u][4]
                return carry

            lax.fori_loop(0, NBLK // 2, block, 0)
        for j in range(hps):
            cols = slice(j * HD, (j + 1) * HD)
            dq_ref[:, cols] = dq_scrs[j][...].astype(BF16)
            dk_ref[:, cols] = dk_scrs[j][...].astype(BF16)
            dv_ref[:, cols] = dv_scrs[j][...].astype(BF16)

    buckets = jnp.asarray(np.stack([_bucket_tile(d) for _, d in PATTERNS]))
    body, more_specs, more = _behind(body, 2 + 6 * hps, after)
    return pl.pallas_call(
        body, name="attention_bwd", grid=(NH // hps,),
        in_specs=[pl.BlockSpec(memory_space=pltpu.SMEM), pl.BlockSpec((3, 128, 256), lambda g: (0, 0, 0))]
        + _head_specs(0, hps) + _head_specs(NH, hps) + _head_specs(2 * NH, hps) + 3 * _head_specs(0, hps)
        + more_specs,
        out_specs=3 * [_heads_spec(hps)] + [pl.BlockSpec((3, hps, 128, 256), lambda g: (0, g, 0, 0))],
        out_shape=[jax.ShapeDtypeStruct((S, DA), BF16)] * 3 + [jax.ShapeDtypeStruct((3, NH, 128, 256), F32)],
        scratch_shapes=[pltpu.VMEM((3, hps, 128, 256), F32)] + [pltpu.VMEM((S, HD), F32)] * (4 * hps),
        compiler_params=_params(1),
    )(rel_bias, buckets, *([proj] * (3 * hps)), *([dattn] * hps), *([attn] * hps), *([lse] * hps), *more)


def _gmlp_parts(u_ref, vb_ref, g_ref, be_ref):
    u = u_ref[...]
    u_act, tu = _gelu(u)
    vb = vb_ref[...]
    gv, tv = _gelu(vb)
    mean = jnp.mean(gv, axis=1, keepdims=True)
    cen = gv - mean
    var = jnp.mean(cen * cen, axis=1, keepdims=True)
    rstd = lax.rsqrt(var + LN_EPS)
    xhat = cen * rstd
    vn = xhat * g_ref[...] + be_ref[...]
    return u, tu, u_act, vb, tv, rstd, xhat, vn


def _gmlp_fwd(proj, ws, bsp_b, gain_v, bias_v):
    def body(u_ref, vb_ref, ws_ref, bsp_ref, g_ref, be_ref, o_ref):
        _, _, u_act, _, _, _, _, vn = _gmlp_parts(u_ref, vb_ref, g_ref, be_ref)
        row = lax.broadcasted_iota(jnp.int32, (128, 128), 0)
        col = lax.broadcasted_iota(jnp.int32, (128, 128), 1)
        causal = row >= col
        for g in range(NH):
            cols = slice(g * 128, (g + 1) * 128)
            wsg = jnp.where(causal, ws_ref[g], 0.0).astype(BF16)
            z = _dot(wsg, vn[:, cols].astype(BF16)) + bsp_ref[g]
            o_ref[:, cols] = (u_act[:, cols] * z).astype(BF16)

    return pl.pallas_call(
        body, name="gmlp_fwd", grid=(NBLK,),
        in_specs=[pl.BlockSpec((128, DB), lambda c: (c, 3)), pl.BlockSpec((128, DB), lambda c: (c, 4)),
                  pl.BlockSpec((NH, 128, 128), lambda c: (0, 0, 0)), pl.BlockSpec((NH, 128, 128), lambda c: (0, 0, 0)),
                  pl.BlockSpec((1, DB), lambda c: (0, 0)), pl.BlockSpec((1, DB), lambda c: (0, 0))],
        out_specs=pl.BlockSpec((128, DB), lambda c: (c, 0)),
        out_shape=jax.ShapeDtypeStruct((S, DB), BF16),
        compiler_params=_params(1),
    )(proj, proj, ws, bsp_b, gain_v, bias_v)


def _branch(attn, gmlp, wpa_g, wpb_g, proj):
    tn = 512

    def body(a_ref, g_ref, wa_ref, wb_ref, ga_ref, gb_ref, ya_ref, yb_ref, mg_ref):
        ya = _dot(a_ref[...], wa_ref[...])
        yb = _dot(g_ref[...], wb_ref[...])
        ya_ref[...] = ya.astype(BF16)
        yb_ref[...] = yb.astype(BF16)
        mg_ref[...] = (_sigmoid(ga_ref[...]) * ya + _sigmoid(gb_ref[...]) * yb).astype(BF16)

    out = pl.BlockSpec((S, tn), lambda j: (0, j))
    return pl.pallas_call(
        body, name="branch", grid=(D // tn,),
        in_specs=[pl.BlockSpec((S, DA), lambda j: (0, 0)), pl.BlockSpec((S, DB), lambda j: (0, 0)),
                  pl.BlockSpec((None, DA, tn), lambda j: (j, 0, 0)), pl.BlockSpec((None, DB, tn), lambda j: (j, 0, 0)),
                  pl.BlockSpec((S, tn), lambda j: (0, 5120 // tn + j)), pl.BlockSpec((S, tn), lambda j: (0, 7168 // tn + j))],
        out_specs=[out, out, out],
        out_shape=[jax.ShapeDtypeStruct((S, D), BF16)] * 3,
        compiler_params=_params(1),
    )(attn, gmlp, wpa_g, wpb_g, proj, proj)


def _out_ln1(merged, wout_g, x, gain, bias):
    tm = 256

    def body(m_ref, w_ref, x_ref, g_ref, b_ref, xh_ref, rs_ref, h_ref):
        pre = ALPHA * x_ref[...] + _dot(m_ref[...], w_ref[...])
        mean = jnp.mean(pre, axis=1, keepdims=True)
        cen = pre - mean
        var = jnp.mean(cen * cen, axis=1, keepdims=True)
        rstd = lax.rsqrt(var + LN_EPS)
        xhat = cen * rstd
        xh_ref[...] = xhat
        rs_ref[...] = jnp.broadcast_to(rstd, (tm, 128))
        h_ref[...] = (xhat * g_ref[...] + b_ref[...]).astype(BF16)

    row = pl.BlockSpec((tm, D), lambda i: (i, 0))
    vec = pl.BlockSpec((1, D), lambda i: (0, 0))
    return pl.pallas_call(
        body, name="out_ln1", grid=(S // tm,),
        in_specs=[row, pl.BlockSpec((D, D), lambda i: (0, 0)), row, vec, vec],
        out_specs=[row, pl.BlockSpec((tm, 128), lambda i: (i, 0)), row],
        out_shape=[jax.ShapeDtypeStruct((S, D), F32), jax.ShapeDtypeStruct((S, 128), F32),
                   jax.ShapeDtypeStruct((S, D), BF16)],
        compiler_params=_params(1),
    )(merged, wout_g, x, gain, bias)


def _ff1(h1b, w1_g, b1):
    tn = 512
    per = D // tn

    def body(h_ref, w_ref, b_ref, a_ref, r_ref):
        r = jnp.maximum(_dot(h_ref[...], w_ref[...]) + b_ref[...], 0.0)
        r_ref[...] = r.astype(BF16)
        a_ref[...] = (r * r).astype(BF16)

    out = pl.BlockSpec((S, tn), lambda j: (0, j))
    return pl.pallas_call(
        body, name="ff1", grid=(DFF // tn,),
        in_specs=[pl.BlockSpec((S, D), lambda j: (0, 0)),
                  pl.BlockSpec((None, D, tn), lambda j: (j // per, 0, j % per)),
                  pl.BlockSpec((1, tn), lambda j: (0, j))],
        out_specs=[out, out],
        out_shape=[jax.ShapeDtypeStruct((S, DFF), BF16)] * 2,
        compiler_params=_params(1),
    )(h1b, w1_g, b1)


def _ff2_ln2_loss(a, w2_g, xhat1, g1, b1, b2, g2, be2, target):
    tm, tk = 512, 1024
    nk = DFF // tk

    def body(a_ref, w_ref, xh_ref, g1_ref, b1_ref, b2_ref, g2_ref, be2_ref, t_ref, d_ref, db_ref, st_ref, acc):
        i, k = pl.program_id(0), pl.program_id(1)

        @pl.when(k == 0)
        def _():
            acc[...] = jnp.zeros_like(acc)

        @pl.when((i == 0) & (k == 0))
        def _():
            st_ref[...] = jnp.zeros_like(st_ref)

        acc[...] += _dot(a_ref[...], w_ref[...])

        @pl.when(k == nk - 1)
        def _():
            def rows_chunk(ci, carry):
                rows = pl.ds(pl.multiple_of(ci * 128, 128), 128)
                h1 = xh_ref[rows, :] * g1_ref[...] + b1_ref[...]
                pre = ALPHA * h1 + acc[rows, :] + b2_ref[...]
                mean = jnp.mean(pre, axis=1, keepdims=True)
                cen = pre - mean
                var = jnp.mean(cen * cen, axis=1, keepdims=True)
                rstd = lax.rsqrt(var + LN_EPS)
                xhat = cen * rstd
                y = xhat * g2_ref[...] + be2_ref[...]
                err = y - t_ref[rows, :]
                dy = err * (1.0 / D)
                g = dy * g2_ref[...]
                dpre = rstd * (g - jnp.mean(g, axis=1, keepdims=True)
                               - xhat * jnp.mean(g * xhat, axis=1, keepdims=True))
                d_ref[rows, :] = dpre
                db_ref[rows, :] = dpre.astype(BF16)
                st_ref[0:1, :] += jnp.sum(dy * xhat, axis=0, keepdims=True)
                st_ref[1:2, :] += jnp.sum(dy, axis=0, keepdims=True)
                st_ref[2:3, :] += jnp.sum(dpre, axis=0, keepdims=True)
                st_ref[3:4, :] += jnp.broadcast_to(jnp.sum(err * err).reshape(1, 1), (1, D))
                return carry

            lax.fori_loop(0, tm // 128, rows_chunk, 0)

    row = pl.BlockSpec((tm, D), lambda i, k: (i, 0))
    vec = pl.BlockSpec((1, D), lambda i, k: (0, 0))
    return pl.pallas_call(
        body, name="ff2_ln2_loss", grid=(S // tm, nk),
        in_specs=[pl.BlockSpec((tm, tk), lambda i, k: (i, k)), pl.BlockSpec((tk, D), lambda i, k: (k, 0)),
                  row, vec, vec, vec, vec, vec, row],
        out_specs=[row, row, pl.BlockSpec((8, D), lambda i, k: (0, 0))],
        out_shape=[jax.ShapeDtypeStruct((S, D), F32), jax.ShapeDtypeStruct((S, D), BF16),
                   jax.ShapeDtypeStruct((8, D), F32)],
        scratch_shapes=[pltpu.VMEM((tm, D), F32)],
        compiler_params=_params(2),
    )(a, w2_g, xhat1, g1, b1, b2, g2, be2, target)


def _grad_w(act, dout, name, ti, tj, sharded, after=None):
    m, n = act.shape[1], dout.shape[1]
    ns = n // N_CHIPS
    per = ns // tj if sharded else None

    def body(a_ref, b_ref, o_ref, at_scr):
        @pl.when(pl.program_id(1) == 0)
        def _():
            at_scr[...] = a_ref[...].T

        o_ref[...] = _dot(at_scr[...], b_ref[...]).astype(BF16)

    if sharded:
        out_spec = pl.BlockSpec((None, ti, tj), lambda i, j: (j // per, i, j % per))
        out_shape = jax.ShapeDtypeStruct((N_CHIPS, m, ns), BF16)
    else:
        out_spec = pl.BlockSpec((ti, tj), lambda i, j: (i, j))
        out_shape = jax.ShapeDtypeStruct((m, n), BF16)
    body, more_specs, more = _behind(body, 2, after)
    return pl.pallas_call(
        body, name=name, grid=(m // ti, n // tj),
        in_specs=[pl.BlockSpec((S, ti), lambda i, j: (0, i)), pl.BlockSpec((S, tj), lambda i, j: (0, j))] + more_specs,
        out_specs=out_spec, out_shape=out_shape,
        scratch_shapes=[pltpu.VMEM((ti, S), BF16)],
        compiler_params=_params(2),
    )(act, dout, *more)


def _d_ff1(dpre2b, w2_g, r, after=None):
    tn = 512

    def body(d_ref, w_ref, r_ref, o_ref, gb_ref):
        da = _dot_nt(d_ref[...], w_ref[...])
        dp = da * (2.0 * r_ref[...].astype(F32))
        o_ref[...] = dp.astype(BF16)
        gb_ref[...] = jnp.sum(dp, axis=0, keepdims=True)

    body, more_specs, more = _behind(body, 3, after)
    return pl.pallas_call(
        body, name="d_ff1", grid=(DFF // tn,),
        in_specs=[pl.BlockSpec((S, D), lambda j: (0, 0)), pl.BlockSpec((tn, D), lambda j: (j, 0)),
                  pl.BlockSpec((S, tn), lambda j: (0, j))] + more_specs,
        out_specs=[pl.BlockSpec((S, tn), lambda j: (0, j)), pl.BlockSpec((1, tn), lambda j: (0, j))],
        out_shape=[jax.ShapeDtypeStruct((S, DFF), BF16), jax.ShapeDtypeStruct((1, DFF), F32)],
        compiler_params=_params(1),
    )(dpre2b, w2_g, r, *more)


def _d_h1_ln1(dprea, w1_g, dpre2, xhat1, rstd1, g1, after=None):
    tm, tk = 512, 1024
    per = D // tk
    nk = DFF // tk

    def body(a_ref, w_ref, d2_ref, xh_ref, rs_ref, g_ref, d_ref, db_ref, st_ref, acc):
        i, k = pl.program_id(0), pl.program_id(1)

        @pl.when(k == 0)
        def _():
            acc[...] = jnp.zeros_like(acc)

        @pl.when((i == 0) & (k == 0))
        def _():
            st_ref[...] = jnp.zeros_like(st_ref)

        acc[...] += _dot_nt(a_ref[...], w_ref[...])

        @pl.when(k == nk - 1)
        def _():
            def rows_chunk(ci, carry):
                rows = pl.ds(pl.multiple_of(ci * 128, 128), 128)
                dh = ALPHA * d2_ref[rows, :] + acc[rows, :]
                xhat = xh_ref[rows, :]
                g = dh * g_ref[...]
                dpre = rs_ref[rows, 0:1] * (g - jnp.mean(g, axis=1, keepdims=True)
                                            - xhat * jnp.mean(g * xhat, axis=1, keepdims=True))
                d_ref[rows, :] = dpre
                db_ref[rows, :] = dpre.astype(BF16)
                st_ref[0:1, :] += jnp.sum(dh * xhat, axis=0, keepdims=True)
                st_ref[1:2, :] += jnp.sum(dh, axis=0, keepdims=True)
                return carry

            lax.fori_loop(0, tm // 128, rows_chunk, 0)

    row = pl.BlockSpec((tm, D), lambda i, k: (i, 0))
    body, more_specs, more = _behind(body, 6, after)
    return pl.pallas_call(
        body, name="d_h1_ln1", grid=(S // tm, nk),
        in_specs=[pl.BlockSpec((tm, tk), lambda i, k: (i, k)),
                  pl.BlockSpec((None, D, tk), lambda i, k: (k // per, 0, k % per)),
                  row, row, pl.BlockSpec((tm, 128), lambda i, k: (i, 0)), pl.BlockSpec((1, D), lambda i, k: (0, 0))]
        + more_specs,
        out_specs=[row, row, pl.BlockSpec((8, D), lambda i, k: (0, 0))],
        out_shape=[jax.ShapeDtypeStruct((S, D), F32), jax.ShapeDtypeStruct((S, D), BF16),
                   jax.ShapeDtypeStruct((8, D), F32)],
        scratch_shapes=[pltpu.VMEM((tm, D), F32)],
        compiler_params=_params(2),
    )(dprea, w1_g, dpre2, xhat1, rstd1, g1, *more)


def _d_merged(dpre1b, wout_g, proj, ya, yb):
    tm, tn = 512, 1024

    def body(d_ref, w_ref, ga_ref, gb_ref, ya_ref, yb_ref, dya_ref, dyb_ref, dga_ref, dgb_ref):
        dm = _dot_nt(d_ref[...], w_ref[...])
        sa = _sigmoid(ga_ref[...])
        sb = _sigmoid(gb_ref[...])
        dya_ref[...] = (dm * sa).astype(BF16)
        dyb_ref[...] = (dm * sb).astype(BF16)
        dga_ref[...] = (dm * ya_ref[...].astype(F32) * sa * (1.0 - sa)).astype(BF16)
        dgb_ref[...] = (dm * yb_ref[...].astype(F32) * sb * (1.0 - sb)).astype(BF16)

    tile = pl.BlockSpec((tm, tn), lambda i, j: (i, j))
    return pl.pallas_call(
        body, name="d_merged", grid=(S // tm, D // tn),
        in_specs=[pl.BlockSpec((tm, D), lambda i, j: (i, 0)), pl.BlockSpec((tn, D), lambda i, j: (j, 0)),
                  pl.BlockSpec((tm, tn), lambda i, j: (i, 5 + j)), pl.BlockSpec((tm, tn), lambda i, j: (i, 7 + j)),
                  tile, tile],
        out_specs=[tile] * 4,
        out_shape=[jax.ShapeDtypeStruct((S, D), BF16)] * 4,
        compiler_params=_params(2),
    )(dpre1b, wout_g, proj, proj, ya, yb)


def _d_branches(dya, dyb, wpa_g, wpb_g, after=None):
    tk = 512

    def body(da_ref, db_ref, wa_ref, wb_ref, oa_ref, ob_ref):
        @pl.when(pl.program_id(0) == 0)
        def _():
            oa_ref[...] = jnp.zeros_like(oa_ref)
            ob_ref[...] = jnp.zeros_like(ob_ref)

        oa_ref[...] += _dot_nt(da_ref[...], wa_ref[...])
        ob_ref[...] += _dot_nt(db_ref[...], wb_ref[...])

    body, more_specs, more = _behind(body, 4, after)
    return pl.pallas_call(
        body, name="d_branches", grid=(D // tk,),
        in_specs=[pl.BlockSpec((S, tk), lambda k: (0, k)), pl.BlockSpec((S, tk), lambda k: (0, k)),
                  pl.BlockSpec((None, DA, tk), lambda k: (k, 0, 0)), pl.BlockSpec((None, DB, tk), lambda k: (k, 0, 0))]
        + more_specs,
        out_specs=[pl.BlockSpec((S, DA), lambda k: (0, 0)), pl.BlockSpec((S, DB), lambda k: (0, 0))],
        out_shape=[jax.ShapeDtypeStruct((S, DA), F32), jax.ShapeDtypeStruct((S, DB), F32)],
        compiler_params=_params(1),
    )(dya, dyb, wpa_g, wpb_g, *more)


def _gmlp_bwd(proj, dgmlp, ws, ws_t, bsp_b, gain_v, bias_v):
    def body(u_ref, vb_ref, dg_ref, ws_ref, wst_ref, bsp_ref, g_ref, be_ref, duv_ref, gws_ref, gbs_ref, st_ref):
        @pl.when(pl.program_id(0) == 0)
        def _():
            gws_ref[...] = jnp.zeros_like(gws_ref)
            gbs_ref[...] = jnp.zeros_like(gbs_ref)
            st_ref[...] = jnp.zeros_like(st_ref)

        u, tu, u_act, vb, tv, rstd, xhat, vn = _gmlp_parts(u_ref, vb_ref, g_ref, be_ref)
        dg = dg_ref[...]
        dz = dg * u_act
        row = lax.broadcasted_iota(jnp.int32, (128, 128), 0)
        col = lax.broadcasted_iota(jnp.int32, (128, 128), 1)
        causal = row >= col
        causal_t = row <= col
        dvn_parts = []
        z_parts = []
        for g in range(NH):
            cols = slice(g * 128, (g + 1) * 128)
            vng = vn[:, cols].astype(BF16)
            dzg = dz[:, cols]
            dzb = dzg.astype(BF16)
            wsg = jnp.where(causal, ws_ref[g], 0.0).astype(BF16)
            wsg_t = jnp.where(causal_t, wst_ref[g], 0.0).astype(BF16)
            z_parts.append(_dot(wsg, vng) + bsp_ref[g])
            gws_ref[g] += jnp.where(causal, _dot_nt(dzb, vng), 0.0)
            gbs_ref[g] += jnp.broadcast_to(jnp.sum(dzg, axis=1, keepdims=True), (128, 128))
            dvn_parts.append(_dot(wsg_t, dzb))
        z = jnp.concatenate(z_parts, axis=1)
        dvn = jnp.concatenate(dvn_parts, axis=1)
        du = dg * z * _gelu_grad(u, tu)
        st_ref[0:1, :] += jnp.sum(dvn * xhat, axis=0, keepdims=True)
        st_ref[1:2, :] += jnp.sum(dvn, axis=0, keepdims=True)
        gg = dvn * g_ref[...]
        dgv = rstd * (gg - jnp.mean(gg, axis=1, keepdims=True) - xhat * jnp.mean(gg * xhat, axis=1, keepdims=True))
        dvb = dgv * _gelu_grad(vb, tv)
        duv_ref[:, 0:DB] = du.astype(BF16)
        duv_ref[:, DB:2 * DB] = dvb.astype(BF16)

    full3 = pl.BlockSpec((NH, 128, 128), lambda c: (0, 0, 0))
    vec = pl.BlockSpec((1, DB), lambda c: (0, 0))
    return pl.pallas_call(
        body, name="gmlp_bwd", grid=(NBLK,),
        in_specs=[pl.BlockSpec((128, DB), lambda c: (c, 3)), pl.BlockSpec((128, DB), lambda c: (c, 4)),
                  pl.BlockSpec((128, DB), lambda c: (c, 0)), full3, full3, full3, vec, vec],
        out_specs=[pl.BlockSpec((128, 2 * DB), lambda c: (c, 0)), full3, full3, pl.BlockSpec((8, DB), lambda c: (0, 0))],
        out_shape=[jax.ShapeDtypeStruct((S, 2 * DB), BF16), jax.ShapeDtypeStruct((NH, 128, 128), F32),
                   jax.ShapeDtypeStruct((NH, 128, 128), F32), jax.ShapeDtypeStruct((8, DB), F32)],
        compiler_params=_params(1),
    )(proj, proj, dgmlp, ws, ws_t, bsp_b, gain_v, bias_v)


def _rel_bias_grad(ds_sums):
    buckets = jnp.asarray(np.stack([_bucket_tile(d) for _, d in PATTERNS]))

    def body(bk_ref, ds_ref, o_ref):
        row = lax.broadcasted_iota(jnp.int32, (N_BUCKETS, 128), 0)
        lane = lax.broadcasted_iota(jnp.int32, (N_BUCKETS, 128), 1)

        def one_bucket(t, out):
            hits = [bk_ref[p] == t for p in range(3)]
            for h in range(NH):
                tot = jnp.zeros((128, 256), F32)
                for p in range(3):
                    tot = tot + jnp.where(hits[p], ds_ref[p, h], 0.0)
                out = jnp.where((row == t) & (lane == h), jnp.sum(tot), out)
            return out

        o_ref[...] = lax.fori_loop(0, N_BUCKETS, one_bucket, jnp.zeros((N_BUCKETS, 128), F32))

    return pl.pallas_call(
        body, name="rel_bias_grad",
        in_specs=[pl.BlockSpec(memory_space=pltpu.VMEM)] * 2, out_specs=pl.BlockSpec(memory_space=pltpu.VMEM),
        out_shape=jax.ShapeDtypeStruct((N_BUCKETS, 128), F32),
        compiler_params=pltpu.CompilerParams(vmem_limit_bytes=VMEM_LIMIT),
    )(buckets, ds_sums)


def _d_x(dproj, win_g, dpre1, after=None):
    tm, tk = 512, 2304
    per = 2304 // tk
    nk = DIN // tk

    def body(a_ref, w_ref, d_ref, o_ref, acc):
        k = pl.program_id(1)

        @pl.when(k == 0)
        def _():
            acc[...] = ALPHA * d_ref[...]

        acc[...] += _dot_nt(a_ref[...], w_ref[...])

        @pl.when(k == nk - 1)
        def _():
            o_ref[...] = acc[...]

    row = pl.BlockSpec((tm, D), lambda i, k: (i, 0))
    body, more_specs, more = _behind(body, 3, after)
    return pl.pallas_call(
        body, name="d_x", grid=(S // tm, nk),
        in_specs=[pl.BlockSpec((tm, tk), lambda i, k: (i, k)),
                  pl.BlockSpec((None, D, tk), lambda i, k: (k // per, 0, k % per)), row] + more_specs,
        out_specs=row, out_shape=jax.ShapeDtypeStruct((S, D), F32),
        scratch_shapes=[pltpu.VMEM((tm, D), F32)],
        compiler_params=_params(2),
    )(dproj, win_g, dpre1, *more)


def _adamw(w, g, m, v, name):
    rows, cols = w.shape
    tm = max(t for t in range(8, 257, 8) if rows % t == 0)

    def body(w_ref, g_ref, m_ref, v_ref, d_ref, nm_ref, nv_ref, go_ref):
        g = g_ref[...]
        m = ADAM_B1 * m_ref[...] + (1.0 - ADAM_B1) * g
        v = ADAM_B2 * v_ref[...] + (1.0 - ADAM_B2) * (g * g)
        m_hat = m / (1.0 - ADAM_B1 ** ADAM_STEP)
        v_hat = v / (1.0 - ADAM_B2 ** ADAM_STEP)
        d_ref[...] = -ADAM_LR * (m_hat / (jnp.sqrt(v_hat) + ADAM_EPS) + ADAM_WD * w_ref[...])
        nm_ref[...] = m
        nv_ref[...] = v
        go_ref[...] = g

    spec = pl.BlockSpec((tm, cols), lambda i: (i, 0))
    return pl.pallas_call(
        body, name=name, grid=(rows // tm,), in_specs=[spec] * 4, out_specs=[spec] * 4,
        out_shape=[jax.ShapeDtypeStruct((rows, cols), F32)] * 4, compiler_params=_params(1),
    )(w, g, m, v)


def _position():
    x, y, c = lax.axis_index("x"), lax.axis_index("y"), lax.axis_index("c")
    chips = [(1 - x, y), (x, 1 - y), (1 - x, 1 - y)]
    return x, y, c, chips


def _remote(src, dst, send_sems, recv_sems, k, to):
    return pltpu.make_async_remote_copy(src_ref=src, dst_ref=dst, send_sem=send_sems.at[k], recv_sem=recv_sems.at[k],
                                        device_id=to, device_id_type=MESH)


def _place_shard(w, name, after=None):
    rows, cols = w.shape
    tm = 256
    x, y = lax.axis_index("x"), lax.axis_index("y")

    def body(chip_ref, w_ref, o_ref):
        o_ref[...] = w_ref[...].astype(BF16)

    more_specs, more = ([ANY], [after]) if after is not None else ([], [])
    if after is not None:
        inner = body
        body = lambda chip_ref, w_ref, after_ref, o_ref: inner(chip_ref, w_ref, o_ref)
    return pl.pallas_call(
        body, name=name,
        grid_spec=pltpu.PrefetchScalarGridSpec(
            num_scalar_prefetch=1, grid=(rows // tm,),
            in_specs=[pl.BlockSpec((tm, cols), lambda i, chip: (i, 0))] + more_specs,
            out_specs=pl.BlockSpec((None, tm, cols), lambda i, chip: (chip[0], i, 0))),
        out_shape=jax.ShapeDtypeStruct((N_CHIPS, rows, cols), BF16),
        compiler_params=_params(1),
    )(jnp.reshape(2 * x + y, (1,)).astype(jnp.int32), w, *more)


def _to_bf16(x, name, after=None):
    tm = 256

    def body(x_ref, o_ref):
        o_ref[...] = x_ref[...].astype(BF16)

    spec = pl.BlockSpec((tm, x.shape[1]), lambda i: (i, 0))
    body, more_specs, more = _behind(body, 1, after)
    return pl.pallas_call(
        body, name=name, grid=(x.shape[0] // tm,), in_specs=[spec] + more_specs, out_specs=spec,
        out_shape=jax.ShapeDtypeStruct(x.shape, BF16), compiler_params=_params(1),
    )(x, *more)


HBM = pl.BlockSpec(memory_space=pltpu.HBM)
SEM = pl.BlockSpec(memory_space=pltpu.SEMAPHORE)
EFFECT = pltpu.SideEffectType.DATAFLOW_SIDE_EFFECTING


def _comm_call(name, body, bufs, sems_in, sems_out, after=None, token=False):
    nb, ns, no = len(bufs), len(sems_in), len(sems_out)
    n_in = nb + ns + (after is not None)

    def wrapped(*refs):
        body(refs[:nb], refs[nb:nb + ns], refs[n_in + nb:n_in + nb + no])
        if token:
            refs[-1][...] = jnp.zeros((8, 128), F32)

    outs = pl.pallas_call(
        wrapped, name=name,
        in_specs=[HBM] * nb + [SEM] * ns + ([ANY] if after is not None else []),
        out_specs=[HBM] * nb + [SEM] * no + ([pl.BlockSpec(memory_space=pltpu.VMEM)] if token else []),
        out_shape=[pltpu.HBM(b.shape, b.dtype) for b in bufs] + [pltpu.SemaphoreType.DMA((k,)) for k in sems_out]
        + ([jax.ShapeDtypeStruct((8, 128), F32)] if token else []),
        input_output_aliases={i: i for i in range(nb)},
        compiler_params=pltpu.CompilerParams(has_side_effects=EFFECT),
    )(*[pltpu.with_memory_space_constraint(b, pltpu.HBM) for b in bufs], *sems_in, *([after] if after is not None else []))
    return list(outs[:nb]), list(outs[nb:nb + no]), (outs[-1] if token else None)


RING_STAGES = {"ici_near": 2, "ici_far": 2, "d2d_near": 2, "d2d_far": 1, "ici_all": 3, "d2d_all": 3}


def _ring_copies(buf, send_sems, recv_sems, k0, stage):
    x, y, c, _ = _position()
    hr = buf.shape[1] // 2
    qr = hr // 2
    half = lambda chip, h: buf.at[chip, pl.ds(h * hr, hr), :]
    quarter = lambda chip, h, q: buf.at[chip, pl.ds(h * hr + q * qr, qr), :]
    mine, x_chip, y_chip, far_chip = 2 * x + y, 2 * (1 - x) + y, 2 * x + (1 - y), 2 * (1 - x) + (1 - y)
    to_x, to_y, sibling = (1 - x, y, c), (x, 1 - y, c), (x, y, 1 - c)
    near = [(half(mine, c), to_x, half(x_chip, c)), (half(mine, c), to_y, half(y_chip, c))]
    d2d_near = [(half(x_chip, c), sibling, half(x_chip, 1 - c)), (half(y_chip, c), sibling, half(y_chip, 1 - c))]
    d2d_far = [(half(far_chip, c), sibling, half(far_chip, 1 - c))]
    moves = {
        "ici_near": near,
        "ici_far": [(quarter(x_chip, c, 0), to_y, quarter(far_chip, c, 0)),
                    (quarter(y_chip, c, 1), to_x, quarter(far_chip, c, 1))],
        "d2d_near": d2d_near,
        "d2d_far": d2d_far,
        "ici_all": near + [(half(mine, c), (1 - x, 1 - y, c), half(far_chip, c))],
        "d2d_all": d2d_near + d2d_far,
    }[stage]
    sends = [_remote(src, src, send_sems, recv_sems, k0 + i, to) for i, (src, to, _) in enumerate(moves)]
    arrivals = [_remote(got, got, send_sems, recv_sems, k0 + i, (x, y, c)) for i, (_, _, got) in enumerate(moves)]
    return sends, arrivals


def _ring_call(name, groups, actions, after=None):
    tags = list(dict.fromkeys(t for _, t, _ in actions))
    counts = {t: len(groups[t]["bufs"]) for t in tags}
    first = {t: sum(counts[u] for u in tags[:i]) for i, t in enumerate(tags)}
    waits = [(t, s) for v, t, s in actions if v == "wait"]
    starts = [(t, s) for v, t, s in actions if v == "start"]

    def body(bufs, sems_in, sems_out):
        for verb, t, s in actions:
            at, sems = (starts.index((t, s)), sems_out) if verb == "start" else (waits.index((t, s)), sems_in)
            for w in range(counts[t]):
                sends, arrivals = _ring_copies(bufs[first[t] + w], sems[2 * at], sems[2 * at + 1], RING_STAGES[s] * w, s)
                if verb == "start":
                    for cp in sends:
                        cp.start()
                else:
                    for cp in arrivals:
                        cp.wait_recv()
                    for cp in sends:
                        cp.wait_send()

    bufs, sems, token = _comm_call(
        name, body, [b for t in tags for b in groups[t]["bufs"]],
        [sem for t, s in waits for sem in groups[t]["sems"][s]],
        [RING_STAGES[s] * counts[t] for t, s in starts for _ in (0, 1)], after, token=True)
    for t in tags:
        groups[t]["bufs"] = bufs[first[t]:first[t] + counts[t]]
    for t, s in waits:
        del groups[t]["sems"][s]
    for i, (t, s) in enumerate(starts):
        groups[t]["sems"][s] = (sems[2 * i], sems[2 * i + 1])
    return token


def _cx_copies(src, dst, send_sems, recv_sems, k0):
    x, y, c, chips = _position()
    sends = [_remote(src.at[2 * cx + cy], dst.at[2 * x + y], send_sems, recv_sems, k0 + j, (cx, cy, c))
             for j, (cx, cy) in enumerate(chips)]
    arrivals = [_remote(dst.at[2 * cx + cy], dst.at[2 * cx + cy], send_sems, recv_sems, k0 + j, (x, y, c))
                for j, (cx, cy) in enumerate(chips)]
    return sends, arrivals


def _cx_start(name, pair_sums):
    n = len(pair_sums)
    landing = [lax.empty(p.shape, p.dtype) for p in pair_sums]

    def body(bufs, _, sems):
        for w in range(n):
            for cp in _cx_copies(bufs[w], bufs[n + w], sems[0], sems[1], 3 * w)[0]:
                cp.start()

    bufs, sems, token = _comm_call(name, body, list(pair_sums) + landing, [], [3 * n, 3 * n], token=True)
    return (bufs, sems), token


def _cx_wait(name, state, after):
    bufs, sems = state
    n = len(bufs) // 2

    def body(refs, sems_in, _):
        for w in range(n):
            sends, arrivals = _cx_copies(refs[w], refs[n + w], sems_in[0], sems_in[1], 3 * w)
            for cp in arrivals:
                cp.wait_recv()
            for cp in sends:
                cp.wait_send()

    bufs, _, _ = _comm_call(name, body, bufs, sems, [], after)
    return bufs[:n], bufs[n:]


def _px_copies(src, dst, send_sems, recv_sems, k):
    x, y, c, _ = _position()
    hr = src.shape[1] // 2
    send = _remote(src.at[:, pl.ds((1 - c) * hr, hr), :], dst, send_sems, recv_sems, k, (x, y, 1 - c))
    arrival = _remote(dst, dst, send_sems, recv_sems, k, (x, y, c))
    return send, arrival


def _px_start(name, grads):
    n = len(grads)
    landing = [lax.empty((N_CHIPS, g.shape[1] // 2, g.shape[2]), g.dtype) for g in grads]

    def body(bufs, _, sems):
        for w in range(n):
            _px_copies(bufs[w], bufs[n + w], sems[0], sems[1], w)[0].start()

    bufs, sems, token = _comm_call(name, body, list(grads) + landing, [], [n, n], token=True)
    return (bufs, sems), token


def _px_wait(name, state, after):
    bufs, sems = state
    n = len(bufs) // 2

    def body(refs, sems_in, _):
        for w in range(n):
            send, arrival = _px_copies(refs[w], refs[n + w], sems_in[0], sems_in[1], w)
            arrival.wait_recv()
            send.wait_send()

    bufs, _, _ = _comm_call(name, body, bufs, sems, [], after)
    return bufs[:n], bufs[n:]


def _pair_sum(grad, got, name):
    _, rows, cols = grad.shape
    hr = rows // 2
    tm = min(hr, 512)
    nb = hr // tm
    c = lax.axis_index("c")

    def body(c_ref, g_ref, o_ref, out_ref):
        out_ref[...] = (g_ref[...].astype(F32) + o_ref[...].astype(F32)).astype(BF16)

    return pl.pallas_call(
        body, name=name,
        grid_spec=pltpu.PrefetchScalarGridSpec(
            num_scalar_prefetch=1, grid=(N_CHIPS, nb),
            in_specs=[pl.BlockSpec((None, tm, cols), lambda s, i, c_ref: (s, c_ref[0] * nb + i, 0)),
                      pl.BlockSpec((None, tm, cols), lambda s, i, c_ref: (s, i, 0))],
            out_specs=pl.BlockSpec((None, tm, cols), lambda s, i, c_ref: (s, i, 0))),
        out_shape=jax.ShapeDtypeStruct((N_CHIPS, hr, cols), BF16),
        compiler_params=_params(2),
    )(jnp.reshape(c, (1,)).astype(jnp.int32), grad, got)


def _chip_sum(parts, pair_sums, name):
    _, hr, cols = parts.shape
    tm = min(hr, 256)
    nb = hr // tm
    x, y, c = lax.axis_index("x"), lax.axis_index("y"), lax.axis_index("c")

    def body(pos_ref, p_ref, own_ref, o_ref):
        chip = pos_ref[0]
        own = own_ref[...].astype(F32)
        term = lambda s: jnp.where(chip == s, own, p_ref[s].astype(F32))
        o_ref[...] = ((term(0) + term(1)) + term(2)) + term(3)

    return pl.pallas_call(
        body, name=name,
        grid_spec=pltpu.PrefetchScalarGridSpec(
            num_scalar_prefetch=1, grid=(nb,),
            in_specs=[pl.BlockSpec((N_CHIPS, tm, cols), lambda i, pos: (0, i, 0)),
                      pl.BlockSpec((None, tm, cols), lambda i, pos: (pos[0], i, 0))],
            out_specs=pl.BlockSpec((tm, cols), lambda i, pos: (pos[1] * nb + i, 0))),
        out_shape=jax.ShapeDtypeStruct((2 * hr, cols), F32), compiler_params=_params(1),
    )(jnp.stack([2 * x + y, c]).astype(jnp.int32), parts, pair_sums)


def _share_copies(buf, send_sems, recv_sems, k):
    x, y, c, _ = _position()
    hr = buf.shape[0] // 2
    mine, theirs = buf.at[pl.ds(c * hr, hr), :], buf.at[pl.ds((1 - c) * hr, hr), :]
    return (_remote(mine, mine, send_sems, recv_sems, k, (x, y, 1 - c)),
            _remote(theirs, theirs, send_sems, recv_sems, k, (x, y, c)))


def _share_start(name, bufs):
    n = len(bufs)

    def body(refs, _, sems):
        for w in range(n):
            _share_copies(refs[w], sems[0], sems[1], w)[0].start()

    bufs, sems, token = _comm_call(name, body, list(bufs), [], [n, n], token=True)
    return (bufs, sems), token


def _share_wait(name, state, after):
    bufs, sems = state

    def body(refs, sems_in, _):
        for w in range(len(bufs)):
            send, arrival = _share_copies(refs[w], sems_in[0], sems_in[1], w)
            arrival.wait_recv()
            send.wait_send()

    return _comm_call(name, body, bufs, sems, [], after)[0]


def _allreduce_small(g):
    rows = g.shape[0]
    half = rows // 2

    def body(g_ref, o_ref, sib, slots, send_sems, recv_sems):
        x, y, c, chips = _position()
        me, sibling = (x, y, c), (x, y, 1 - c)
        my_chip = 2 * x + y
        mine = pl.ds(pl.multiple_of(c * half, 8), half)
        theirs = pl.ds(pl.multiple_of((1 - c) * half, 8), half)
        pair = _remote(g_ref.at[theirs], sib, send_sems, recv_sems, 0, sibling)
        pair.start()
        pair.wait()
        slots[my_chip] = g_ref[mine, :] + sib[...]
        sent = []
        for j, (cx, cy) in enumerate(chips):
            cp = _remote(slots.at[my_chip], slots.at[my_chip], send_sems, recv_sems, 1 + j, (cx, cy, c))
            cp.start()
            sent.append(cp)
        for j, (cx, cy) in enumerate(chips):
            got = slots.at[2 * cx + cy]
            _remote(got, got, send_sems, recv_sems, 1 + j, me).wait_recv()
        for cp in sent:
            cp.wait_send()
        o_ref[mine, :] = ((slots[0] + slots[1]) + slots[2]) + slots[3]
        swap = _remote(o_ref.at[mine], o_ref.at[mine], send_sems, recv_sems, 4, sibling)
        swap.start()
        swap.wait()

    vm = pl.BlockSpec(memory_space=pltpu.VMEM)
    return pl.pallas_call(
        body, name="allreduce_small",
        in_specs=[vm], out_specs=vm, out_shape=jax.ShapeDtypeStruct((rows, 128), F32),
        scratch_shapes=[pltpu.VMEM((half, 128), F32), pltpu.VMEM((N_CHIPS, half, 128), F32),
                        pltpu.SemaphoreType.DMA((5,)), pltpu.SemaphoreType.DMA((5,))],
        compiler_params=pltpu.CompilerParams(vmem_limit_bytes=VMEM_LIMIT),
    )(g)


_SMALL =("rel_bias", "ln_v_gain", "ln_v_bias", "w_spatial", "b_spatial", "ln1_gain", "ln1_bias",
          "b_ff1", "b_ff2", "ln2_gain", "ln2_bias")
_SMALL_ROWS = 1200
_LOSS_AT = (152832 // 128, 0)


def _pack_small(parts):
    flat = jnp.concatenate([parts[k].reshape(-1).astype(F32) for k in _SMALL])
    flat = jnp.pad(flat, (0, _SMALL_ROWS * 128 - flat.shape[0]))
    return flat.reshape(_SMALL_ROWS, 128)


def _unpack_small(packed, like):
    flat = packed.reshape(-1)
    out, at = {}, 0
    for k in _SMALL:
        n = math.prod(like[k].shape)
        out[k] = flat[at:at + n].reshape(like[k].shape)
        at += n
    return out


def kernel(x, w_in, rel_bias, ln_v_gain, ln_v_bias, w_spatial, b_spatial, w_proj_a, w_proj_b, w_out, ln1_gain, ln1_bias, w_ff1, b_ff1, w_ff2, b_ff2, ln2_gain, ln2_bias, loss_target, m_w_in, m_rel_bias, m_ln_v_gain, m_ln_v_bias, m_w_spatial, m_b_spatial, m_w_proj_a, m_w_proj_b, m_w_out, m_ln1_gain, m_ln1_bias, m_w_ff1, m_b_ff1, m_w_ff2, m_b_ff2, m_ln2_gain, m_ln2_bias, v_w_in, v_rel_bias, v_ln_v_gain, v_ln_v_bias, v_w_spatial, v_b_spatial, v_w_proj_a, v_w_proj_b, v_w_out, v_ln1_gain, v_ln1_bias, v_w_ff1, v_b_ff1, v_w_ff2, v_b_ff2, v_ln2_gain, v_ln2_bias):
    args = dict(locals())
    big = ("w_in", "w_proj_a", "w_proj_b", "w_out", "w_ff1", "w_ff2")
    weights = ("w_in", "rel_bias", "ln_v_gain", "ln_v_bias", "w_spatial", "b_spatial", "w_proj_a", "w_proj_b", "w_out",
               "ln1_gain", "ln1_bias", "w_ff1", "b_ff1", "w_ff2", "b_ff2", "ln2_gain", "ln2_bias")

    xs = x[0]
    target = loss_target[0]

    ring = {"a": {"bufs": [_place_shard(w_in[0], "place_w_in")], "sems": {}}}
    tok = _ring_call("allgather_a_near", ring, [("start", "a", "ici_near")])
    placed = [_place_shard(args[k][0], f"place_{k}", after=tok) for k in big[1:]]
    for tag, bufs in (("b", placed[0:3]), ("c", placed[3:4]), ("d", placed[4:5])):
        ring[tag] = {"bufs": bufs, "sems": {}}
    xb = _to_bf16(xs, "x_to_bf16", after=placed[4])

    mx, my = lax.axis_index("x"), lax.axis_index("y")
    own = jnp.reshape(2 * mx + my, (1,)).astype(jnp.int32)
    near = jnp.stack([2 * (1 - mx) + my, 2 * mx + (1 - my)]).astype(jnp.int32)
    far = jnp.reshape(2 * (1 - mx) + (1 - my), (1,)).astype(jnp.int32)
    proj = _proj(xb, ring["a"]["bufs"][0], own, "proj_own")
    _ring_call("allgather_a_far", ring, [("wait", "a", "ici_near"), ("start", "a", "ici_far"), ("start", "a", "d2d_near"),
                                         ("start", "b", "ici_all"), ("start", "c", "ici_near"),
                                         ("start", "d", "ici_near")], after=proj)
    _ring_call("allgather_a_near_done", ring, [("wait", "a", "d2d_near")])
    proj = _proj(xb, ring["a"]["bufs"][0], near, "proj_near", into=proj)
    _ring_call("allgather_a_last", ring, [("wait", "a", "ici_far"), ("start", "a", "d2d_far")], after=proj)
    _ring_call("allgather_a_done", ring, [("wait", "a", "d2d_far")])
    (win_g,) = ring["a"]["bufs"]
    proj = _proj(xb, win_g, far, "proj_far", into=proj)
    _ring_call("allgather_b_pass", ring, [("wait", "b", "ici_all"), ("start", "b", "d2d_all")], after=proj)
    ws = w_spatial[0]
    ws_t = jnp.transpose(ws, (0, 2, 1))
    bsp_b = jnp.broadcast_to(b_spatial[0][:, :, None], (NH, 128, 128))
    gmlp = _gmlp_fwd(proj, ws, bsp_b, ln_v_gain, ln_v_bias)
    attn, lse = _attention_fwd(proj, rel_bias)
    _ring_call("allgather_b_done_c_far", ring,
               [("wait", "b", "d2d_all"),
                ("wait", "c", "ici_near"), ("start", "c", "ici_far"), ("start", "c", "d2d_near")], after=attn)
    wpa_g, wpb_g, wout_g = ring["b"]["bufs"]
    wout_full = wout_g.reshape(D, D)
    ya, yb, merged = _branch(attn, gmlp, wpa_g, wpb_g, proj)
    xhat1, rstd1, h1b = _out_ln1(merged, wout_full, xs, ln1_gain, ln1_bias)
    _ring_call("allgather_c_last_d_far", ring,
               [("wait", "c", "ici_far"), ("start", "c", "d2d_far"),
                ("wait", "d", "ici_near"), ("start", "d", "ici_far"), ("start", "d", "d2d_near")], after=h1b)
    _ring_call("allgather_c_done", ring, [("wait", "c", "d2d_near"), ("wait", "c", "d2d_far")])
    (w1_g,) = ring["c"]["bufs"]
    a, r = _ff1(h1b, w1_g, b_ff1)
    _ring_call("allgather_d_last", ring, [("wait", "d", "ici_far"), ("start", "d", "d2d_far")], after=a)
    _ring_call("allgather_d_done", ring, [("wait", "d", "d2d_near"), ("wait", "d", "d2d_far")])
    (w2_g,) = ring["d"]["bufs"]
    w2_full = w2_g.reshape(DFF, D)
    dpre2, dpre2b, st2 = _ff2_ln2_loss(a, w2_full, xhat1, ln1_gain, ln1_bias, b_ff2, ln2_gain, ln2_bias, target)

    def pair_and_chip(tag, state, after):
        local, from_sibling = _px_wait(f"pair_exchange_wait_{tag}", state, after)
        pair_sums = [_pair_sum(g, o, f"pair_sum_{tag}_{i}") for i, (g, o) in enumerate(zip(local, from_sibling))]
        return _cx_start(f"chip_exchange_start_{tag}", pair_sums)

    g_w2 = _grad_w(a, dpre2b, "grad_w_ff2", 512, 2048, False)
    px, tok = _px_start("pair_exchange_start_w_ff2", [g_w2.reshape(N_CHIPS, DFF // N_CHIPS, D)])
    dprea, g_b1 = _d_ff1(dpre2b, w2_full, r, after=tok)
    cx_w2, tok = pair_and_chip("w_ff2", px, dprea)
    g_w1 = _grad_w(h1b, dprea, "grad_w_ff1", 512, 2048, True, after=tok)
    px, tok = _px_start("pair_exchange_start_w_ff1", [g_w1])
    dpre1, dpre1b, st1 = _d_h1_ln1(dprea, w1_g, dpre2, xhat1, rstd1, ln1_gain, after=tok)
    cx_w1, tok = pair_and_chip("w_ff1", px, dpre1b)
    g_wout = _grad_w(merged, dpre1b, "grad_w_out", 512, 2048, False, after=tok)
    dya, dyb, dga, dgb = _d_merged(dpre1b, wout_full, proj, ya, yb)
    g_wpa = _grad_w(attn, dya, "grad_w_proj_a", 1024, 512, True)
    g_wpb = _grad_w(gmlp, dyb, "grad_w_proj_b", 1024, 512, True)
    px, tok = _px_start("pair_exchange_start_b", [g_wpa, g_wpb, g_wout.reshape(N_CHIPS, D // N_CHIPS, D)])
    dattn, dgmlp = _d_branches(dya, dyb, wpa_g, wpb_g, after=tok)
    duv, g_ws, g_bs, stv = _gmlp_bwd(proj, dgmlp, ws, ws_t, bsp_b, ln_v_gain, ln_v_bias)
    cx_b, tok = pair_and_chip("b", px, duv)
    dq, dk, dv, ds_sums = _attention_bwd(proj, dattn, attn, lse, rel_bias, after=tok)
    g_rb = _rel_bias_grad(ds_sums)[:, :NH]

    small_g = dict(rel_bias=g_rb, ln_v_gain=stv[0], ln_v_bias=stv[1], w_spatial=g_ws, b_spatial=g_bs[:, :, 0],
                   ln1_gain=st1[0], ln1_bias=st1[1], b_ff1=g_b1, b_ff2=st2[2], ln2_gain=st2[0], ln2_bias=st2[1])
    gs = _allreduce_small(_pack_small(small_g).at[_LOSS_AT].set(st2[3, 0]))
    ds_, ms_, vs_, _ = _adamw(_pack_small({k: args[k] for k in _SMALL}), gs,
                           _pack_small({k: args["m_" + k] for k in _SMALL}),
                           _pack_small({k: args["v_" + k] for k in _SMALL}), "adamw_small")
    like = {k: args[k] for k in _SMALL}
    grads, deltas, new_m, new_v = (_unpack_small(t, like) for t in (gs, ds_, ms_, vs_))

    dproj = jnp.concatenate([dq, dk, dv, duv, dga, dgb], axis=1)
    g_win = _grad_w(xb, dproj, "grad_w_in", 512, 2304, True, after=gs)
    px, tok = _px_start("pair_exchange_start_w_in", [g_win])

    def chip_sums(tag, state, names, after):
        pair_sums, from_chips = _cx_wait(f"chip_exchange_wait_{tag}", state, after)
        halves = [_chip_sum(p, own, f"chip_sum_{k}") for p, own, k in zip(from_chips, pair_sums, names)]
        return _share_start(f"share_start_{tag}", halves)

    def adam(tag, state, names, after):
        last = None
        for k, g in zip(names, _share_wait(f"share_wait_{tag}", state, after)):
            d_, m_, v_, g_ = _adamw(args[k][0], g, args["m_" + k][0], args["v_" + k][0], f"adamw_{k}")
            grads[k], deltas[k], new_m[k], new_v[k] = g_[None], d_[None], m_[None], v_[None]
            last = d_
        return last

    sh_w2, tok = chip_sums("w_ff2", cx_w2, ["w_ff2"], tok)
    sh_w1, tok = chip_sums("w_ff1", cx_w1, ["w_ff1"], tok)
    sh_b, tok = chip_sums("b", cx_b, ["w_proj_a", "w_proj_b", "w_out"], tok)
    cx_in, tok = pair_and_chip("w_in", px, tok)
    grad_x = _d_x(dproj, win_g, dpre1, after=tok)
    done = adam("w_ff2", sh_w2, ["w_ff2"], grad_x)
    done = adam("w_ff1", sh_w1, ["w_ff1"], done)
    done = adam("b", sh_b, ["w_proj_a", "w_proj_b", "w_out"], done)
    sh_in, tok = chip_sums("w_in", cx_in, ["w_in"], done)
    adam("w_in", sh_in, ["w_in"], tok)

    loss = gs[_LOSS_AT] * (0.5 / D)
    return (loss, grad_x[None], *[grads[k] for k in weights], *[deltas[k] for k in weights],
            *[new_m[k] for k in weights], *[new_v[k] for k in weights])
```

```python
import math

import numpy as np
import jax
import jax.numpy as jnp
from jax import lax
from jax.experimental import pallas as pl
from jax.experimental.pallas import tpu as pltpu

F32 = jnp.float32
BF16 = jnp.bfloat16

S = 2048
D = 2048
DA = 1024
DB = 1024
DFF = 8192
DIN = 9216
NH = 8
HD = 128
NBLK = 16
PATTERNS = ((128, 1), (512, 4), (2048, 16))
N_BUCKETS = 32
MAX_DISTANCE = 2048
ALPHA = 2.0 ** 0.25
LN_EPS = 1e-5
NEG_INF = -1e30
SCALE = HD ** -0.5
N_CHIPS = 4

ADAM_LR = 0.001
ADAM_B1 = 0.9
ADAM_B2 = 0.999
ADAM_EPS = 1e-08
ADAM_WD = 0.01
ADAM_STEP = 10

VMEM_LIMIT = 56 * 1024 * 1024
MESH = pl.DeviceIdType.MESH
ANY = pl.BlockSpec(memory_space=pl.ANY)


def _params(n_axes, vmem=VMEM_LIMIT):
    return pltpu.CompilerParams(dimension_semantics=("arbitrary",) * n_axes, vmem_limit_bytes=vmem)


def _bucket_tile(dilation):
    qi = np.arange(128)[:, None]
    kj = np.arange(256)[None, :]
    n = np.clip(128 + qi - kj, 0, 128) * dilation
    max_exact = N_BUCKETS // 2
    nf = np.maximum(n, 1).astype(np.float32)
    large = max_exact + (np.log(nf / np.float32(max_exact)) / np.float32(math.log(MAX_DISTANCE / max_exact))
                         * np.float32(N_BUCKETS - max_exact)).astype(np.int32)
    large = np.minimum(large, N_BUCKETS - 1)
    return np.where(n < max_exact, n, large).astype(np.int32)


def _gelu(x):
    c = math.sqrt(2.0 / math.pi)
    t = jnp.tanh(c * (x + 0.044715 * x * x * x))
    return 0.5 * x * (1.0 + t), t


def _gelu_grad(x, t):
    c = math.sqrt(2.0 / math.pi)
    return 0.5 * (1.0 + t) + 0.5 * x * (1.0 - t * t) * c * (1.0 + 3.0 * 0.044715 * x * x)


def _sigmoid(x):
    return 1.0 / (1.0 + jnp.exp(-x))


def _dot(a, b):
    return jnp.dot(a, b, preferred_element_type=F32)


def _behind(body, n_in, after):
    if after is None:
        return body, [], []
    return (lambda *refs: body(*refs[:n_in], *refs[n_in + 1:])), [ANY], [after]


def _dot_nt(a, b):
    return lax.dot_general(a, b, (((1,), (1,)), ((), ())), preferred_element_type=F32)


def _proj(xb, win_g, shards, name, into=None):
    tn = 768
    per = 2304 // tn

    def body(shards_ref, x_ref, w_ref, *rest):
        rest[-1][...] = _dot(x_ref[...], w_ref[...])

    in_specs = [pl.BlockSpec((S, D), lambda j, sh: (0, 0)),
                pl.BlockSpec((None, D, tn), lambda j, sh: (sh[j // per], 0, j % per))]
    return pl.pallas_call(
        body, name=name,
        grid_spec=pltpu.PrefetchScalarGridSpec(
            num_scalar_prefetch=1, grid=(shards.shape[0] * per,),
            in_specs=in_specs + ([ANY] if into is not None else []),
            out_specs=pl.BlockSpec((S, tn), lambda j, sh: (0, sh[j // per] * per + j % per))),
        out_shape=jax.ShapeDtypeStruct((S, DIN), F32),
        input_output_aliases={3: 0} if into is not None else {},
        compiler_params=_params(1),
    )(shards, xb, win_g, *([into] if into is not None else []))


FWD_HEADS_PER_STEP = 4
BWD_HEADS_PER_STEP = 2


def _head_bias_tiles(rb_ref, bk_ref, bias_scr, first_head, hps):
    qi = lax.broadcasted_iota(jnp.int32, (128, 256), 0)
    kj = lax.broadcasted_iota(jnp.int32, (128, 256), 1)
    steps = 128 + qi - kj
    band = (steps >= 0) & (steps <= 128)
    bias_scr[...] = jnp.zeros_like(bias_scr)
    for p in range(len(PATTERNS)):
        bucket = bk_ref[p]

        def one_bucket(t, carry):
            hit = bucket == t
            for j in range(hps):
                bias_scr[p, j] = jnp.where(hit, rb_ref[t, first_head + j], bias_scr[p, j])
            return carry

        lax.fori_loop(0, N_BUCKETS, one_bucket, 0)
        for j in range(hps):
            bias_scr[p, j] = jnp.where(band, bias_scr[p, j], NEG_INF)


def _block_rows(b, dilation):
    nblk = NBLK // dilation
    r, n = b // nblk, b % nblk
    start = r + n * (128 * dilation)
    prev_start = jnp.maximum(start - 128 * dilation, r)
    if dilation == 1:
        return pl.ds(pl.multiple_of(start, 128), 128), pl.ds(pl.multiple_of(prev_start, 128), 128), n > 0
    return pl.ds(start, 128, stride=dilation), pl.ds(prev_start, 128, stride=dilation), n > 0


def _head_specs(first, hps):
    return [pl.BlockSpec((S, HD), lambda g, j=j: (0, first + g * hps + j)) for j in range(hps)]


def _heads_spec(hps):
    return pl.BlockSpec((S, hps * HD), lambda g: (0, g))


def _attention_fwd(proj, rel_bias):
    hps = FWD_HEADS_PER_STEP
    buckets = jnp.asarray(np.stack([_bucket_tile(d) for _, d in PATTERNS]))

    def body(rb_ref, bk_ref, *refs):
        q_refs, k_refs, v_refs = (refs[i * hps:(i + 1) * hps] for i in range(3))
        o_ref, lse_ref, bias_scr = refs[3 * hps:3 * hps + 3]
        acc_scrs, m_scrs, l_scrs = (refs[3 * hps + 3 + i * hps:3 * hps + 3 + (i + 1) * hps] for i in range(3))
        _head_bias_tiles(rb_ref, bk_ref, bias_scr, pl.program_id(0) * hps, hps)
        kj = lax.broadcasted_iota(jnp.int32, (128, 256), 1)
        for p, (_, d) in enumerate(PATTERNS):
            prev_blocks = NBLK // d > 1

            def block(b, carry):
                units = [(j,) + _block_rows(blk, d) for blk in (b, b + NBLK // 2) for j in range(hps)]
                scores = []
                for j, rows, prows, _ in units:
                    q = q_refs[j][rows, :].astype(BF16)
                    cur = _dot_nt(q, k_refs[j][rows, :].astype(BF16))
                    if prev_blocks:
                        cur = jnp.concatenate([_dot_nt(q, k_refs[j][prows, :].astype(BF16)), cur], axis=1)
                    scores.append(cur)
                soft = []
                for u, (j, _, _, has_prev) in enumerate(units):
                    if prev_blocks:
                        s = jnp.where((kj >= 128) | has_prev, scores[u] * SCALE + bias_scr[p, j], NEG_INF)
                    else:
                        s = scores[u] * SCALE + bias_scr[p, j, :, 128:256]
                    m = jnp.max(s, axis=1, keepdims=True)
                    e = jnp.exp(s - m)
                    soft.append((m, jnp.sum(e, axis=1, keepdims=True), e.astype(BF16)))
                outs = []
                for u, (j, rows, prows, _) in enumerate(units):
                    e = soft[u][2]
                    if prev_blocks:
                        outs.append(_dot(e[:, :128], v_refs[j][prows, :].astype(BF16))
                                    + _dot(e[:, 128:], v_refs[j][rows, :].astype(BF16)))
                    else:
                        outs.append(_dot(e, v_refs[j][rows, :].astype(BF16)))
                for u, (j, rows, _, _) in enumerate(units):
                    acc_scr, m_scr, l_scr = acc_scrs[j], m_scrs[j], l_scrs[j]
                    (m, den, _), o = soft[u], outs[u]
                    if p == 0:
                        acc_scr[rows, :] = o
                        m_scr[rows, :] = jnp.broadcast_to(m, (128, HD))
                        l_scr[rows, :] = jnp.broadcast_to(den, (128, HD))
                    else:
                        m_old = m_scr[rows, :]
                        m_new = jnp.maximum(m_old, m)
                        w_old, w_new = jnp.exp(m_old - m_new), jnp.exp(m - m_new)
                        acc_scr[rows, :] = acc_scr[rows, :] * w_old + o * w_new
                        l_scr[rows, :] = l_scr[rows, :] * w_old + den * w_new
                        m_scr[rows, :] = m_new
                return carry

            lax.fori_loop(0, NBLK // 2, block, 0)
        for j in range(hps):
            cols = slice(j * HD, (j + 1) * HD)
            den = l_scrs[j][...]
            o_ref[:, cols] = (acc_scrs[j][...] / den).astype(BF16)
            lse_ref[:, cols] = m_scrs[j][...] + jnp.log(den)

    return pl.pallas_call(
        body, name="attention_fwd", grid=(NH // hps,),
        in_specs=[pl.BlockSpec(memory_space=pltpu.SMEM), pl.BlockSpec((3, 128, 256), lambda g: (0, 0, 0))]
        + _head_specs(0, hps) + _head_specs(NH, hps) + _head_specs(2 * NH, hps),
        out_specs=[_heads_spec(hps), _heads_spec(hps)],
        out_shape=[jax.ShapeDtypeStruct((S, DA), BF16), jax.ShapeDtypeStruct((S, DA), F32)],
        scratch_shapes=[pltpu.VMEM((3, hps, 128, 256), F32)] + [pltpu.VMEM((S, HD), F32)] * (3 * hps),
        compiler_params=_params(1),
    )(rel_bias, buckets, *([proj] * (3 * hps)))


def _attention_bwd(proj, dattn, attn, lse, rel_bias, after=None):
    hps = BWD_HEADS_PER_STEP

    def body(rb_ref, bk_ref, *refs):
        q_refs, k_refs, v_refs, do_refs, o_refs, lse_refs = (refs[i * hps:(i + 1) * hps] for i in range(6))
        dq_ref, dk_ref, dv_ref, ds_ref, bias_scr = refs[6 * hps:6 * hps + 5]
        dl_scrs, dq_scrs, dk_scrs, dv_scrs = (refs[6 * hps + 5 + i * hps:6 * hps + 5 + (i + 1) * hps] for i in range(4))
        _head_bias_tiles(rb_ref, bk_ref, bias_scr, pl.program_id(0) * hps, hps)
        ds_ref[...] = jnp.zeros_like(ds_ref)
        for j in range(hps):
            dq_scrs[j][...] = jnp.zeros((S, HD), F32)
            dk_scrs[j][...] = jnp.zeros((S, HD), F32)
            dv_scrs[j][...] = jnp.zeros((S, HD), F32)
            prod = do_refs[j][...] * o_refs[j][...].astype(F32)
            dl_scrs[j][...] = jnp.broadcast_to(jnp.sum(prod, axis=1, keepdims=True), (S, HD))
        for p, (_, d) in enumerate(PATTERNS):
            prev_blocks = NBLK // d > 1

            def block(b, carry):
                units = [(j,) + _block_rows(blk, d) for blk in (b, b + NBLK // 2) for j in range(hps)]
                ops, raw = [], []
                for j, rows, prows, _ in units:
                    q, do = q_refs[j][rows, :].astype(BF16), do_refs[j][rows, :].astype(BF16)
                    kc, vc = k_refs[j][rows, :].astype(BF16), v_refs[j][rows, :].astype(BF16)
                    if prev_blocks:
                        kp, vp = k_refs[j][prows, :].astype(BF16), v_refs[j][prows, :].astype(BF16)
                        ops.append((q, do, kc, kp))
                        raw.append((_dot_nt(q, kc), _dot_nt(do, vc), _dot_nt(q, kp), _dot_nt(do, vp)))
                    else:
                        ops.append((q, do, kc))
                        raw.append((_dot_nt(q, kc), _dot_nt(do, vc)))
                probs = []
                for u, (j, rows, _, has_prev) in enumerate(units):
                    lse_b, dl_b = lse_refs[j][rows, :], dl_scrs[j][rows, :]
                    p_c = jnp.exp(raw[u][0] * SCALE + bias_scr[p, j, :, 128:256] - lse_b)
                    ds_c = p_c * (raw[u][1] - dl_b)
                    ds_ref[p, j, :, 128:256] += ds_c
                    if prev_blocks:
                        p_p = jnp.where(has_prev, jnp.exp(raw[u][2] * SCALE + bias_scr[p, j, :, 0:128] - lse_b), 0.0)
                        ds_p = p_p * (raw[u][3] - dl_b)
                        ds_ref[p, j, :, 0:128] += ds_p
                        probs.append((p_c, ds_c, p_p, ds_p))
                    else:
                        probs.append((p_c, ds_c))
                grads = []
                for u in range(len(units)):
                    q, do, kc = ops[u][:3]
                    p_c, ds_c = probs[u][:2]
                    dq = _dot(ds_c.astype(BF16), kc)
                    cur = (_dot(ds_c.T.astype(BF16), q) * SCALE, _dot(p_c.T.astype(BF16), do))
                    if prev_blocks:
                        p_p, ds_p = probs[u][2:]
                        dq = dq + _dot(ds_p.astype(BF16), ops[u][3])
                        cur = cur + (_dot(ds_p.T.astype(BF16), q) * SCALE, _dot(p_p.T.astype(BF16), do))
                    grads.append((dq * SCALE,) + cur)
                for u, (j, rows, prows, _) in enumerate(units):
                    dq_scrs[j][rows, :] += grads[u][0]
                    dk_scrs[j][rows, :] += grads[u][1]
                    dv_scrs[j][rows, :] += grads[u][2]
                    if prev_blocks:
                        dk_scrs[j][prows, :] += grads[u][3]
                        dv_scrs[j][prows, :] += grads[u][4]
                return carry

            lax.fori_loop(0, NBLK // 2, block, 0)
        for j in range(hps):
            cols = slice(j * HD, (j + 1) * HD)
            dq_ref[:, cols] = dq_scrs[j][...].astype(BF16)
            dk_ref[:, cols] = dk_scrs[j][...].astype(BF16)
            dv_ref[:, cols] = dv_scrs[j][...].astype(BF16)

    buckets = jnp.asarray(np.stack([_bucket_tile(d) for _, d in PATTERNS]))
    body, more_specs, more = _behind(body, 2 + 6 * hps, after)
    return pl.pallas_call(
        body, name="attention_bwd", grid=(NH // hps,),
        in_specs=[pl.BlockSpec(memory_space=pltpu.SMEM), pl.BlockSpec((3, 128, 256), lambda g: (0, 0, 0))]
        + _head_specs(0, hps) + _head_specs(NH, hps) + _head_specs(2 * NH, hps) + 3 * _head_specs(0, hps)
        + more_specs,
        out_specs=3 * [_heads_spec(hps)] + [pl.BlockSpec((3, hps, 128, 256), lambda g: (0, g, 0, 0))],
        out_shape=[jax.ShapeDtypeStruct((S, DA), BF16)] * 3 + [jax.ShapeDtypeStruct((3, NH, 128, 256), F32)],
        scratch_shapes=[pltpu.VMEM((3, hps, 128, 256), F32)] + [pltpu.VMEM((S, HD), F32)] * (4 * hps),
        compiler_params=_params(1),
    )(rel_bias, buckets, *([proj] * (3 * hps)), *([dattn] * hps), *([attn] * hps), *([lse] * hps), *more)


def _gmlp_parts(u_ref, vb_ref, g_ref, be_ref):
    u = u_ref[...]
    u_act, tu = _gelu(u)
    vb = vb_ref[...]
    gv, tv = _gelu(vb)
    mean = jnp.mean(gv, axis=1, keepdims=True)
    cen = gv - mean
    var = jnp.mean(cen * cen, axis=1, keepdims=True)
    rstd = lax.rsqrt(var + LN_EPS)
    xhat = cen * rstd
    vn = xhat * g_ref[...] + be_ref[...]
    return u, tu, u_act, vb, tv, rstd, xhat, vn


def _gmlp_fwd(proj, ws, bsp_b, gain_v, bias_v):
    def body(u_ref, vb_ref, ws_ref, bsp_ref, g_ref, be_ref, o_ref):
        _, _, u_act, _, _, _, _, vn = _gmlp_parts(u_ref, vb_ref, g_ref, be_ref)
        row = lax.broadcasted_iota(jnp.int32, (128, 128), 0)
        col = lax.broadcasted_iota(jnp.int32, (128, 128), 1)
        causal = row >= col
        for g in range(NH):
            cols = slice(g * 128, (g + 1) * 128)
            wsg = jnp.where(causal, ws_ref[g], 0.0).astype(BF16)
            z = _dot(wsg, vn[:, cols].astype(BF16)) + bsp_ref[g]
            o_ref[:, cols] = (u_act[:, cols] * z).astype(BF16)

    return pl.pallas_call(
        body, name="gmlp_fwd", grid=(NBLK,),
        in_specs=[pl.BlockSpec((128, DB), lambda c: (c, 3)), pl.BlockSpec((128, DB), lambda c: (c, 4)),
                  pl.BlockSpec((NH, 128, 128), lambda c: (0, 0, 0)), pl.BlockSpec((NH, 128, 128), lambda c: (0, 0, 0)),
                  pl.BlockSpec((1, DB), lambda c: (0, 0)), pl.BlockSpec((1, DB), lambda c: (0, 0))],
        out_specs=pl.BlockSpec((128, DB), lambda c: (c, 0)),
        out_shape=jax.ShapeDtypeStruct((S, DB), BF16),
        compiler_params=_params(1),
    )(proj, proj, ws, bsp_b, gain_v, bias_v)


def _branch(attn, gmlp, wpa_g, wpb_g, proj):
    tn = 512

    def body(a_ref, g_ref, wa_ref, wb_ref, ga_ref, gb_ref, ya_ref, yb_ref, mg_ref):
        ya = _dot(a_ref[...], wa_ref[...])
        yb = _dot(g_ref[...], wb_ref[...])
        ya_ref[...] = ya.astype(BF16)
        yb_ref[...] = yb.astype(BF16)
        mg_ref[...] = (_sigmoid(ga_ref[...]) * ya + _sigmoid(gb_ref[...]) * yb).astype(BF16)

    out = pl.BlockSpec((S, tn), lambda j: (0, j))
    return pl.pallas_call(
        body, name="branch", grid=(D // tn,),
        in_specs=[pl.BlockSpec((S, DA), lambda j: (0, 0)), pl.BlockSpec((S, DB), lambda j: (0, 0)),
                  pl.BlockSpec((None, DA, tn), lambda j: (j, 0, 0)), pl.BlockSpec((None, DB, tn), lambda j: (j, 0, 0)),
                  pl.BlockSpec((S, tn), lambda j: (0, 5120 // tn + j)), pl.BlockSpec((S, tn), lambda j: (0, 7168 // tn + j))],
        out_specs=[out, out, out],
        out_shape=[jax.ShapeDtypeStruct((S, D), BF16)] * 3,
        compiler_params=_params(1),
    )(attn, gmlp, wpa_g, wpb_g, proj, proj)


def _out_ln1(merged, wout_g, x, gain, bias):
    tm = 256

    def body(m_ref, w_ref, x_ref, g_ref, b_ref, xh_ref, rs_ref, h_ref):
        pre = ALPHA * x_ref[...] + _dot(m_ref[...], w_ref[...])
        mean = jnp.mean(pre, axis=1, keepdims=True)
        cen = pre - mean
        var = jnp.mean(cen * cen, axis=1, keepdims=True)
        rstd = lax.rsqrt(var + LN_EPS)
        xhat = cen * rstd
        xh_ref[...] = xhat
        rs_ref[...] = jnp.broadcast_to(rstd, (tm, 128))
        h_ref[...] = (xhat * g_ref[...] + b_ref[...]).astype(BF16)

    row = pl.BlockSpec((tm, D), lambda i: (i, 0))
    vec = pl.BlockSpec((1, D), lambda i: (0, 0))
    return pl.pallas_call(
        body, name="out_ln1", grid=(S // tm,),
        in_specs=[row, pl.BlockSpec((D, D), lambda i: (0, 0)), row, vec, vec],
        out_specs=[row, pl.BlockSpec((tm, 128), lambda i: (i, 0)), row],
        out_shape=[jax.ShapeDtypeStruct((S, D), F32), jax.ShapeDtypeStruct((S, 128), F32),
                   jax.ShapeDtypeStruct((S, D), BF16)],
        compiler_params=_params(1),
    )(merged, wout_g, x, gain, bias)


def _ff1(h1b, w1_g, b1):
    tn = 512
    per = D // tn

    def body(h_ref, w_ref, b_ref, a_ref, r_ref):
        r = jnp.maximum(_dot(h_ref[...], w_ref[...]) + b_ref[...], 0.0)
        r_ref[...] = r.astype(BF16)
        a_ref[...] = (r * r).astype(BF16)

    out = pl.BlockSpec((S, tn), lambda j: (0, j))
    return pl.pallas_call(
        body, name="ff1", grid=(DFF // tn,),
        in_specs=[pl.BlockSpec((S, D), lambda j: (0, 0)),
                  pl.BlockSpec((None, D, tn), lambda j: (j // per, 0, j % per)),
                  pl.BlockSpec((1, tn), lambda j: (0, j))],
        out_specs=[out, out],
        out_shape=[jax.ShapeDtypeStruct((S, DFF), BF16)] * 2,
        compiler_params=_params(1),
    )(h1b, w1_g, b1)


def _ff2_ln2_loss(a, w2_g, xhat1, g1, b1, b2, g2, be2, target):
    tm, tk = 512, 1024
    nk = DFF // tk

    def body(a_ref, w_ref, xh_ref, g1_ref, b1_ref, b2_ref, g2_ref, be2_ref, t_ref, d_ref, db_ref, st_ref, acc):
        i, k = pl.program_id(0), pl.program_id(1)

        @pl.when(k == 0)
        def _():
            acc[...] = jnp.zeros_like(acc)

        @pl.when((i == 0) & (k == 0))
        def _():
            st_ref[...] = jnp.zeros_like(st_ref)

        acc[...] += _dot(a_ref[...], w_ref[...])

        @pl.when(k == nk - 1)
        def _():
            def rows_chunk(ci, carry):
                rows = pl.ds(pl.multiple_of(ci * 128, 128), 128)
                h1 = xh_ref[rows, :] * g1_ref[...] + b1_ref[...]
                pre = ALPHA * h1 + acc[rows, :] + b2_ref[...]
                mean = jnp.mean(pre, axis=1, keepdims=True)
                cen = pre - mean
                var = jnp.mean(cen * cen, axis=1, keepdims=True)
                rstd = lax.rsqrt(var + LN_EPS)
                xhat = cen * rstd
                y = xhat * g2_ref[...] + be2_ref[...]
                err = y - t_ref[rows, :]
                dy = err * (1.0 / D)
                g = dy * g2_ref[...]
                dpre = rstd * (g - jnp.mean(g, axis=1, keepdims=True)
                               - xhat * jnp.mean(g * xhat, axis=1, keepdims=True))
                d_ref[rows, :] = dpre
                db_ref[rows, :] = dpre.astype(BF16)
                st_ref[0:1, :] += jnp.sum(dy * xhat, axis=0, keepdims=True)
                st_ref[1:2, :] += jnp.sum(dy, axis=0, keepdims=True)
                st_ref[2:3, :] += jnp.sum(dpre, axis=0, keepdims=True)
                st_ref[3:4, :] += jnp.broadcast_to(jnp.sum(err * err).reshape(1, 1), (1, D))
                return carry

            lax.fori_loop(0, tm // 128, rows_chunk, 0)

    row = pl.BlockSpec((tm, D), lambda i, k: (i, 0))
    vec = pl.BlockSpec((1, D), lambda i, k: (0, 0))
    return pl.pallas_call(
        body, name="ff2_ln2_loss", grid=(S // tm, nk),
        in_specs=[pl.BlockSpec((tm, tk), lambda i, k: (i, k)), pl.BlockSpec((tk, D), lambda i, k: (k, 0)),
                  row, vec, vec, vec, vec, vec, row],
        out_specs=[row, row, pl.BlockSpec((8, D), lambda i, k: (0, 0))],
        out_shape=[jax.ShapeDtypeStruct((S, D), F32), jax.ShapeDtypeStruct((S, D), BF16),
                   jax.ShapeDtypeStruct((8, D), F32)],
        scratch_shapes=[pltpu.VMEM((tm, D), F32)],
        compiler_params=_params(2),
    )(a, w2_g, xhat1, g1, b1, b2, g2, be2, target)


def _grad_w(act, dout, name, ti, tj, sharded, after=None):
    m, n = act.shape[1], dout.shape[1]
    ns = n // N_CHIPS
    per = ns // tj if sharded else None

    def body(a_ref, b_ref, o_ref, at_scr):
        @pl.when(pl.program_id(1) == 0)
        def _():
            at_scr[...] = a_ref[...].T

        o_ref[...] = _dot(at_scr[...], b_ref[...]).astype(BF16)

    if sharded:
        out_spec = pl.BlockSpec((None, ti, tj), lambda i, j: (j // per, i, j % per))
        out_shape = jax.ShapeDtypeStruct((N_CHIPS, m, ns), BF16)
    else:
        out_spec = pl.BlockSpec((ti, tj), lambda i, j: (i, j))
        out_shape = jax.ShapeDtypeStruct((m, n), BF16)
    body, more_specs, more = _behind(body, 2, after)
    return pl.pallas_call(
        body, name=name, grid=(m // ti, n // tj),
        in_specs=[pl.BlockSpec((S, ti), lambda i, j: (0, i)), pl.BlockSpec((S, tj), lambda i, j: (0, j))] + more_specs,
        out_specs=out_spec, out_shape=out_shape,
        scratch_shapes=[pltpu.VMEM((ti, S), BF16)],
        compiler_params=_params(2),
    )(act, dout, *more)


def _d_ff1(dpre2b, w2_g, r, after=None):
    tn = 512

    def body(d_ref, w_ref, r_ref, o_ref, gb_ref):
        da = _dot_nt(d_ref[...], w_ref[...])
        dp = da * (2.0 * r_ref[...].astype(F32))
        o_ref[...] = dp.astype(BF16)
        gb_ref[...] = jnp.sum(dp, axis=0, keepdims=True)

    body, more_specs, more = _behind(body, 3, after)
    return pl.pallas_call(
        body, name="d_ff1", grid=(DFF // tn,),
        in_specs=[pl.BlockSpec((S, D), lambda j: (0, 0)), pl.BlockSpec((tn, D), lambda j: (j, 0)),
                  pl.BlockSpec((S, tn), lambda j: (0, j))] + more_specs,
        out_specs=[pl.BlockSpec((S, tn), lambda j: (0, j)), pl.BlockSpec((1, tn), lambda j: (0, j))],
        out_shape=[jax.ShapeDtypeStruct((S, DFF), BF16), jax.ShapeDtypeStruct((1, DFF), F32)],
        compiler_params=_params(1),
    )(dpre2b, w2_g, r, *more)


def _d_h1_ln1(dprea, w1_g, dpre2, xhat1, rstd1, g1, after=None):
    tm, tk = 512, 1024
    per = D // tk
    nk = DFF // tk

    def body(a_ref, w_ref, d2_ref, xh_ref, rs_ref, g_ref, d_ref, db_ref, st_ref, acc):
        i, k = pl.program_id(0), pl.program_id(1)

        @pl.when(k == 0)
        def _():
            acc[...] = jnp.zeros_like(acc)

        @pl.when((i == 0) & (k == 0))
        def _():
            st_ref[...] = jnp.zeros_like(st_ref)

        acc[...] += _dot_nt(a_ref[...], w_ref[...])

        @pl.when(k == nk - 1)
        def _():
            def rows_chunk(ci, carry):
                rows = pl.ds(pl.multiple_of(ci * 128, 128), 128)
                dh = ALPHA * d2_ref[rows, :] + acc[rows, :]
                xhat = xh_ref[rows, :]
                g = dh * g_ref[...]
                dpre = rs_ref[rows, 0:1] * (g - jnp.mean(g, axis=1, keepdims=True)
                                            - xhat * jnp.mean(g * xhat, axis=1, keepdims=True))
                d_ref[rows, :] = dpre
                db_ref[rows, :] = dpre.astype(BF16)
                st_ref[0:1, :] += jnp.sum(dh * xhat, axis=0, keepdims=True)
                st_ref[1:2, :] += jnp.sum(dh, axis=0, keepdims=True)
                return carry

            lax.fori_loop(0, tm // 128, rows_chunk, 0)

    row = pl.BlockSpec((tm, D), lambda i, k: (i, 0))
    body, more_specs, more = _behind(body, 6, after)
    return pl.pallas_call(
        body, name="d_h1_ln1", grid=(S // tm, nk),
        in_specs=[pl.BlockSpec((tm, tk), lambda i, k: (i, k)),
                  pl.BlockSpec((None, D, tk), lambda i, k: (k // per, 0, k % per)),
                  row, row, pl.BlockSpec((tm, 128), lambda i, k: (i, 0)), pl.BlockSpec((1, D), lambda i, k: (0, 0))]
        + more_specs,
        out_specs=[row, row, pl.BlockSpec((8, D), lambda i, k: (0, 0))],
        out_shape=[jax.ShapeDtypeStruct((S, D), F32), jax.ShapeDtypeStruct((S, D), BF16),
                   jax.ShapeDtypeStruct((8, D), F32)],
        scratch_shapes=[pltpu.VMEM((tm, D), F32)],
        compiler_params=_params(2),
    )(dprea, w1_g, dpre2, xhat1, rstd1, g1, *more)


def _d_merged(dpre1b, wout_g, proj, ya, yb):
    tm, tn = 512, 1024

    def body(d_ref, w_ref, ga_ref, gb_ref, ya_ref, yb_ref, dya_ref, dyb_ref, dga_ref, dgb_ref):
        dm = _dot_nt(d_ref[...], w_ref[...])
        sa = _sigmoid(ga_ref[...])
        sb = _sigmoid(gb_ref[...])
        dya_ref[...] = (dm * sa).astype(BF16)
        dyb_ref[...] = (dm * sb).astype(BF16)
        dga_ref[...] = (dm * ya_ref[...].astype(F32) * sa * (1.0 - sa)).astype(BF16)
        dgb_ref[...] = (dm * yb_ref[...].astype(F32) * sb * (1.0 - sb)).astype(BF16)

    tile = pl.BlockSpec((tm, tn), lambda i, j: (i, j))
    return pl.pallas_call(
        body, name="d_merged", grid=(S // tm, D // tn),
        in_specs=[pl.BlockSpec((tm, D), lambda i, j: (i, 0)), pl.BlockSpec((tn, D), lambda i, j: (j, 0)),
                  pl.BlockSpec((tm, tn), lambda i, j: (i, 5 + j)), pl.BlockSpec((tm, tn), lambda i, j: (i, 7 + j)),
                  tile, tile],
        out_specs=[tile] * 4,
        out_shape=[jax.ShapeDtypeStruct((S, D), BF16)] * 4,
        compiler_params=_params(2),
    )(dpre1b, wout_g, proj, proj, ya, yb)


def _d_branches(dya, dyb, wpa_g, wpb_g, after=None):
    tk = 512

    def body(da_ref, db_ref, wa_ref, wb_ref, oa_ref, ob_ref):
        @pl.when(pl.program_id(0) == 0)
        def _():
            oa_ref[...] = jnp.zeros_like(oa_ref)
            ob_ref[...] = jnp.zeros_like(ob_ref)

        oa_ref[...] += _dot_nt(da_ref[...], wa_ref[...])
        ob_ref[...] += _dot_nt(db_ref[...], wb_ref[...])

    body, more_specs, more = _behind(body, 4, after)
    return pl.pallas_call(
        body, name="d_branches", grid=(D // tk,),
        in_specs=[pl.BlockSpec((S, tk), lambda k: (0, k)), pl.BlockSpec((S, tk), lambda k: (0, k)),
                  pl.BlockSpec((None, DA, tk), lambda k: (k, 0, 0)), pl.BlockSpec((None, DB, tk), lambda k: (k, 0, 0))]
        + more_specs,
        out_specs=[pl.BlockSpec((S, DA), lambda k: (0, 0)), pl.BlockSpec((S, DB), lambda k: (0, 0))],
        out_shape=[jax.ShapeDtypeStruct((S, DA), F32), jax.ShapeDtypeStruct((S, DB), F32)],
        compiler_params=_params(1),
    )(dya, dyb, wpa_g, wpb_g, *more)


def _gmlp_bwd(proj, dgmlp, ws, ws_t, bsp_b, gain_v, bias_v):
    def body(u_ref, vb_ref, dg_ref, ws_ref, wst_ref, bsp_ref, g_ref, be_ref, duv_ref, gws_ref, gbs_ref, st_ref):
        @pl.when(pl.program_id(0) == 0)
        def _():
            gws_ref[...] = jnp.zeros_like(gws_ref)
            gbs_ref[...] = jnp.zeros_like(gbs_ref)
            st_ref[...] = jnp.zeros_like(st_ref)

        u, tu, u_act, vb, tv, rstd, xhat, vn = _gmlp_parts(u_ref, vb_ref, g_ref, be_ref)
        dg = dg_ref[...]
        dz = dg * u_act
        row = lax.broadcasted_iota(jnp.int32, (128, 128), 0)
        col = lax.broadcasted_iota(jnp.int32, (128, 128), 1)
        causal = row >= col
        causal_t = row <= col
        dvn_parts = []
        z_parts = []
        for g in range(NH):
            cols = slice(g * 128, (g + 1) * 128)
            vng = vn[:, cols].astype(BF16)
            dzg = dz[:, cols]
            dzb = dzg.astype(BF16)
            wsg = jnp.where(causal, ws_ref[g], 0.0).astype(BF16)
            wsg_t = jnp.where(causal_t, wst_ref[g], 0.0).astype(BF16)
            z_parts.append(_dot(wsg, vng) + bsp_ref[g])
            gws_ref[g] += jnp.where(causal, _dot_nt(dzb, vng), 0.0)
            gbs_ref[g] += jnp.broadcast_to(jnp.sum(dzg, axis=1, keepdims=True), (128, 128))
            dvn_parts.append(_dot(wsg_t, dzb))
        z = jnp.concatenate(z_parts, axis=1)
        dvn = jnp.concatenate(dvn_parts, axis=1)
        du = dg * z * _gelu_grad(u, tu)
        st_ref[0:1, :] += jnp.sum(dvn * xhat, axis=0, keepdims=True)
        st_ref[1:2, :] += jnp.sum(dvn, axis=0, keepdims=True)
        gg = dvn * g_ref[...]
        dgv = rstd * (gg - jnp.mean(gg, axis=1, keepdims=True) - xhat * jnp.mean(gg * xhat, axis=1, keepdims=True))
        dvb = dgv * _gelu_grad(vb, tv)
        duv_ref[:, 0:DB] = du.astype(BF16)
        duv_ref[:, DB:2 * DB] = dvb.astype(BF16)

    full3 = pl.BlockSpec((NH, 128, 128), lambda c: (0, 0, 0))
    vec = pl.BlockSpec((1, DB), lambda c: (0, 0))
    return pl.pallas_call(
        body, name="gmlp_bwd", grid=(NBLK,),
        in_specs=[pl.BlockSpec((128, DB), lambda c: (c, 3)), pl.BlockSpec((128, DB), lambda c: (c, 4)),
                  pl.BlockSpec((128, DB), lambda c: (c, 0)), full3, full3, full3, vec, vec],
        out_specs=[pl.BlockSpec((128, 2 * DB), lambda c: (c, 0)), full3, full3, pl.BlockSpec((8, DB), lambda c: (0, 0))],
        out_shape=[jax.ShapeDtypeStruct((S, 2 * DB), BF16), jax.ShapeDtypeStruct((NH, 128, 128), F32),
                   jax.ShapeDtypeStruct((NH, 128, 128), F32), jax.ShapeDtypeStruct((8, DB), F32)],
        compiler_params=_params(1),
    )(proj, proj, dgmlp, ws, ws_t, bsp_b, gain_v, bias_v)


def _rel_bias_grad(ds_sums):
    buckets = jnp.asarray(np.stack([_bucket_tile(d) for _, d in PATTERNS]))

    def body(bk_ref, ds_ref, o_ref):
        row = lax.broadcasted_iota(jnp.int32, (N_BUCKETS, 128), 0)
        lane = lax.broadcasted_iota(jnp.int32, (N_BUCKETS, 128), 1)

        def one_bucket(t, out):
            hits = [bk_ref[p] == t for p in range(3)]
            for h in range(NH):
                tot = jnp.zeros((128, 256), F32)
                for p in range(3):
                    tot = tot + jnp.where(hits[p], ds_ref[p, h], 0.0)
                out = jnp.where((row == t) & (lane == h), jnp.sum(tot), out)
            return out

        o_ref[...] = lax.fori_loop(0, N_BUCKETS, one_bucket, jnp.zeros((N_BUCKETS, 128), F32))

    return pl.pallas_call(
        body, name="rel_bias_grad",
        in_specs=[pl.BlockSpec(memory_space=pltpu.VMEM)] * 2, out_specs=pl.BlockSpec(memory_space=pltpu.VMEM),
        out_shape=jax.ShapeDtypeStruct((N_BUCKETS, 128), F32),
        compiler_params=pltpu.CompilerParams(vmem_limit_bytes=VMEM_LIMIT),
    )(buckets, ds_sums)


def _d_x(dproj, win_g, dpre1, after=None):
    tm, tk = 512, 2304
    per = 2304 // tk
    nk = DIN // tk

    def body(a_ref, w_ref, d_ref, o_ref, acc):
        k = pl.program_id(1)

        @pl.when(k == 0)
        def _():
            acc[...] = ALPHA * d_ref[...]

        acc[...] += _dot_nt(a_ref[...], w_ref[...])

        @pl.when(k == nk - 1)
        def _():
            o_ref[...] = acc[...]

    row = pl.BlockSpec((tm, D), lambda i, k: (i, 0))
    body, more_specs, more = _behind(body, 3, after)
    return pl.pallas_call(
        body, name="d_x", grid=(S // tm, nk),
        in_specs=[pl.BlockSpec((tm, tk), lambda i, k: (i, k)),
                  pl.BlockSpec((None, D, tk), lambda i, k: (k // per, 0, k % per)), row] + more_specs,
        out_specs=row, out_shape=jax.ShapeDtypeStruct((S, D), F32),
        scratch_shapes=[pltpu.VMEM((tm, D), F32)],
        compiler_params=_params(2),
    )(dproj, win_g, dpre1, *more)


def _adamw(w, g, m, v, name):
    rows, cols = w.shape
    tm = max(t for t in range(8, 257, 8) if rows % t == 0)

    def body(w_ref, g_ref, m_ref, v_ref, d_ref, nm_ref, nv_ref, go_ref):
        g = g_ref[...]
        m = ADAM_B1 * m_ref[...] + (1.0 - ADAM_B1) * g
        v = ADAM_B2 * v_ref[...] + (1.0 - ADAM_B2) * (g * g)
        m_hat = m / (1.0 - ADAM_B1 ** ADAM_STEP)
        v_hat = v / (1.0 - ADAM_B2 ** ADAM_STEP)
        d_ref[...] = -ADAM_LR * (m_hat / (jnp.sqrt(v_hat) + ADAM_EPS) + ADAM_WD * w_ref[...])
        nm_ref[...] = m
        nv_ref[...] = v
        go_ref[...] = g

    spec = pl.BlockSpec((tm, cols), lambda i: (i, 0))
    return pl.pallas_call(
        body, name=name, grid=(rows // tm,), in_specs=[spec] * 4, out_specs=[spec] * 4,
        out_shape=[jax.ShapeDtypeStruct((rows, cols), F32)] * 4, compiler_params=_params(1),
    )(w, g, m, v)


def _position():
    x, y, c = lax.axis_index("x"), lax.axis_index("y"), lax.axis_index("c")
    chips = [(1 - x, y), (x, 1 - y), (1 - x, 1 - y)]
    return x, y, c, chips


def _remote(src, dst, send_sems, recv_sems, k, to):
    return pltpu.make_async_remote_copy(src_ref=src, dst_ref=dst, send_sem=send_sems.at[k], recv_sem=recv_sems.at[k],
                                        device_id=to, device_id_type=MESH)


def _place_shard(w, name, after=None):
    rows, cols = w.shape
    tm = 256
    x, y = lax.axis_index("x"), lax.axis_index("y")

    def body(chip_ref, w_ref, o_ref):
        o_ref[...] = w_ref[...].astype(BF16)

    more_specs, more = ([ANY], [after]) if after is not None else ([], [])
    if after is not None:
        inner = body
        body = lambda chip_ref, w_ref, after_ref, o_ref: inner(chip_ref, w_ref, o_ref)
    return pl.pallas_call(
        body, name=name,
        grid_spec=pltpu.PrefetchScalarGridSpec(
            num_scalar_prefetch=1, grid=(rows // tm,),
            in_specs=[pl.BlockSpec((tm, cols), lambda i, chip: (i, 0))] + more_specs,
            out_specs=pl.BlockSpec((None, tm, cols), lambda i, chip: (chip[0], i, 0))),
        out_shape=jax.ShapeDtypeStruct((N_CHIPS, rows, cols), BF16),
        compiler_params=_params(1),
    )(jnp.reshape(2 * x + y, (1,)).astype(jnp.int32), w, *more)


def _to_bf16(x, name, after=None):
    tm = 256

    def body(x_ref, o_ref):
        o_ref[...] = x_ref[...].astype(BF16)

    spec = pl.BlockSpec((tm, x.shape[1]), lambda i: (i, 0))
    body, more_specs, more = _behind(body, 1, after)
    return pl.pallas_call(
        body, name=name, grid=(x.shape[0] // tm,), in_specs=[spec] + more_specs, out_specs=spec,
        out_shape=jax.ShapeDtypeStruct(x.shape, BF16), compiler_params=_params(1),
    )(x, *more)


HBM = pl.BlockSpec(memory_space=pltpu.HBM)
SEM = pl.BlockSpec(memory_space=pltpu.SEMAPHORE)
EFFECT = pltpu.SideEffectType.DATAFLOW_SIDE_EFFECTING


def _comm_call(name, body, bufs, sems_in, sems_out, after=None, token=False):
    nb, ns, no = len(bufs), len(sems_in), len(sems_out)
    n_in = nb + ns + (after is not None)

    def wrapped(*refs):
        body(refs[:nb], refs[nb:nb + ns], refs[n_in + nb:n_in + nb + no])
        if token:
            refs[-1][...] = jnp.zeros((8, 128), F32)

    outs = pl.pallas_call(
        wrapped, name=name,
        in_specs=[HBM] * nb + [SEM] * ns + ([ANY] if after is not None else []),
        out_specs=[HBM] * nb + [SEM] * no + ([pl.BlockSpec(memory_space=pltpu.VMEM)] if token else []),
        out_shape=[pltpu.HBM(b.shape, b.dtype) for b in bufs] + [pltpu.SemaphoreType.DMA((k,)) for k in sems_out]
        + ([jax.ShapeDtypeStruct((8, 128), F32)] if token else []),
        input_output_aliases={i: i for i in range(nb)},
        compiler_params=pltpu.CompilerParams(has_side_effects=EFFECT),
    )(*[pltpu.with_memory_space_constraint(b, pltpu.HBM) for b in bufs], *sems_in, *([after] if after is not None else []))
    return list(outs[:nb]), list(outs[nb:nb + no]), (outs[-1] if token else None)


RING_STAGES = {"ici_near": 2, "ici_far": 2, "d2d_near": 2, "d2d_far": 1}


def _ring_copies(buf, send_sems, recv_sems, k0, stage):
    x, y, c, _ = _position()
    hr = buf.shape[1] // 2
    qr = hr // 2
    half = lambda chip, h: buf.at[chip, pl.ds(h * hr, hr), :]
    quarter = lambda chip, h, q: buf.at[chip, pl.ds(h * hr + q * qr, qr), :]
    mine, x_chip, y_chip, far_chip = 2 * x + y, 2 * (1 - x) + y, 2 * x + (1 - y), 2 * (1 - x) + (1 - y)
    to_x, to_y, sibling = (1 - x, y, c), (x, 1 - y, c), (x, y, 1 - c)
    if stage == "ici_near":
        moves = [(half(mine, c), to_x, half(x_chip, c)), (half(mine, c), to_y, half(y_chip, c))]
    elif stage == "ici_far":
        moves = [(quarter(x_chip, c, 0), to_y, quarter(far_chip, c, 0)),
                 (quarter(y_chip, c, 1), to_x, quarter(far_chip, c, 1))]
    elif stage == "d2d_near":
        moves = [(half(x_chip, c), sibling, half(x_chip, 1 - c)), (half(y_chip, c), sibling, half(y_chip, 1 - c))]
    else:
        moves = [(half(far_chip, c), sibling, half(far_chip, 1 - c))]
    sends = [_remote(src, src, send_sems, recv_sems, k0 + i, to) for i, (src, to, _) in enumerate(moves)]
    arrivals = [_remote(got, got, send_sems, recv_sems, k0 + i, (x, y, c)) for i, (_, _, got) in enumerate(moves)]
    return sends, arrivals


def _ring_call(name, groups, actions, after=None):
    tags = list(dict.fromkeys(t for _, t, _ in actions))
    counts = {t: len(groups[t]["bufs"]) for t in tags}
    first = {t: sum(counts[u] for u in tags[:i]) for i, t in enumerate(tags)}
    waits = [(t, s) for v, t, s in actions if v == "wait"]
    starts = [(t, s) for v, t, s in actions if v == "start"]

    def body(bufs, sems_in, sems_out):
        for verb, t, s in actions:
            at, sems = (starts.index((t, s)), sems_out) if verb == "start" else (waits.index((t, s)), sems_in)
            for w in range(counts[t]):
                sends, arrivals = _ring_copies(bufs[first[t] + w], sems[2 * at], sems[2 * at + 1], RING_STAGES[s] * w, s)
                if verb == "start":
                    for cp in sends:
                        cp.start()
                else:
                    for cp in arrivals:
                        cp.wait_recv()
                    for cp in sends:
                        cp.wait_send()

    bufs, sems, token = _comm_call(
        name, body, [b for t in tags for b in groups[t]["bufs"]],
        [sem for t, s in waits for sem in groups[t]["sems"][s]],
        [RING_STAGES[s] * counts[t] for t, s in starts for _ in (0, 1)], after, token=True)
    for t in tags:
        groups[t]["bufs"] = bufs[first[t]:first[t] + counts[t]]
    for t, s in waits:
        del groups[t]["sems"][s]
    for i, (t, s) in enumerate(starts):
        groups[t]["sems"][s] = (sems[2 * i], sems[2 * i + 1])
    return token


def _cx_copies(src, dst, send_sems, recv_sems, k0):
    x, y, c, chips = _position()
    sends = [_remote(src.at[2 * cx + cy], dst.at[2 * x + y], send_sems, recv_sems, k0 + j, (cx, cy, c))
             for j, (cx, cy) in enumerate(chips)]
    arrivals = [_remote(dst.at[2 * cx + cy], dst.at[2 * cx + cy], send_sems, recv_sems, k0 + j, (x, y, c))
                for j, (cx, cy) in enumerate(chips)]
    return sends, arrivals


def _cx_start(name, pair_sums):
    n = len(pair_sums)
    landing = [lax.empty(p.shape, p.dtype) for p in pair_sums]

    def body(bufs, _, sems):
        for w in range(n):
            for cp in _cx_copies(bufs[w], bufs[n + w], sems[0], sems[1], 3 * w)[0]:
                cp.start()

    bufs, sems, token = _comm_call(name, body, list(pair_sums) + landing, [], [3 * n, 3 * n], token=True)
    return (bufs, sems), token


def _cx_wait(name, state, after):
    bufs, sems = state
    n = len(bufs) // 2

    def body(refs, sems_in, _):
        for w in range(n):
            sends, arrivals = _cx_copies(refs[w], refs[n + w], sems_in[0], sems_in[1], 3 * w)
            for cp in arrivals:
                cp.wait_recv()
            for cp in sends:
                cp.wait_send()

    bufs, _, _ = _comm_call(name, body, bufs, sems, [], after)
    return bufs[:n], bufs[n:]


def _px_copies(src, dst, send_sems, recv_sems, k):
    x, y, c, _ = _position()
    hr = src.shape[1] // 2
    send = _remote(src.at[:, pl.ds((1 - c) * hr, hr), :], dst, send_sems, recv_sems, k, (x, y, 1 - c))
    arrival = _remote(dst, dst, send_sems, recv_sems, k, (x, y, c))
    return send, arrival


def _px_start(name, grads):
    n = len(grads)
    landing = [lax.empty((N_CHIPS, g.shape[1] // 2, g.shape[2]), g.dtype) for g in grads]

    def body(bufs, _, sems):
        for w in range(n):
            _px_copies(bufs[w], bufs[n + w], sems[0], sems[1], w)[0].start()

    bufs, sems, token = _comm_call(name, body, list(grads) + landing, [], [n, n], token=True)
    return (bufs, sems), token


def _px_wait(name, state, after):
    bufs, sems = state
    n = len(bufs) // 2

    def body(refs, sems_in, _):
        for w in range(n):
            send, arrival = _px_copies(refs[w], refs[n + w], sems_in[0], sems_in[1], w)
            arrival.wait_recv()
            send.wait_send()

    bufs, _, _ = _comm_call(name, body, bufs, sems, [], after)
    return bufs[:n], bufs[n:]


def _pair_sum(grad, got, name):
    _, rows, cols = grad.shape
    hr = rows // 2
    tm = min(hr, 512)
    nb = hr // tm
    c = lax.axis_index("c")

    def body(c_ref, g_ref, o_ref, out_ref):
        out_ref[...] = (g_ref[...].astype(F32) + o_ref[...].astype(F32)).astype(BF16)

    return pl.pallas_call(
        body, name=name,
        grid_spec=pltpu.PrefetchScalarGridSpec(
            num_scalar_prefetch=1, grid=(N_CHIPS, nb),
            in_specs=[pl.BlockSpec((None, tm, cols), lambda s, i, c_ref: (s, c_ref[0] * nb + i, 0)),
                      pl.BlockSpec((None, tm, cols), lambda s, i, c_ref: (s, i, 0))],
            out_specs=pl.BlockSpec((None, tm, cols), lambda s, i, c_ref: (s, i, 0))),
        out_shape=jax.ShapeDtypeStruct((N_CHIPS, hr, cols), BF16),
        compiler_params=_params(2),
    )(jnp.reshape(c, (1,)).astype(jnp.int32), grad, got)


def _chip_sum(parts, pair_sums, name):
    _, hr, cols = parts.shape
    tm = min(hr, 512)
    nb = hr // tm
    x, y, c = lax.axis_index("x"), lax.axis_index("y"), lax.axis_index("c")

    def body(pos_ref, p_ref, own_ref, o_ref):
        chip = pos_ref[0]
        own = own_ref[...].astype(F32)
        term = lambda s: jnp.where(chip == s, own, p_ref[s].astype(F32))
        o_ref[...] = ((term(0) + term(1)) + term(2)) + term(3)

    return pl.pallas_call(
        body, name=name,
        grid_spec=pltpu.PrefetchScalarGridSpec(
            num_scalar_prefetch=1, grid=(nb,),
            in_specs=[pl.BlockSpec((N_CHIPS, tm, cols), lambda i, pos: (0, i, 0)),
                      pl.BlockSpec((None, tm, cols), lambda i, pos: (pos[0], i, 0))],
            out_specs=pl.BlockSpec((tm, cols), lambda i, pos: (pos[1] * nb + i, 0))),
        out_shape=jax.ShapeDtypeStruct((2 * hr, cols), F32), compiler_params=_params(1),
    )(jnp.stack([2 * x + y, c]).astype(jnp.int32), parts, pair_sums)


def _share_copies(buf, send_sems, recv_sems, k):
    x, y, c, _ = _position()
    hr = buf.shape[0] // 2
    mine, theirs = buf.at[pl.ds(c * hr, hr), :], buf.at[pl.ds((1 - c) * hr, hr), :]
    return (_remote(mine, mine, send_sems, recv_sems, k, (x, y, 1 - c)),
            _remote(theirs, theirs, send_sems, recv_sems, k, (x, y, c)))


def _share_start(name, bufs):
    n = len(bufs)

    def body(refs, _, sems):
        for w in range(n):
            _share_copies(refs[w], sems[0], sems[1], w)[0].start()

    bufs, sems, token = _comm_call(name, body, list(bufs), [], [n, n], token=True)
    return (bufs, sems), token


def _share_wait(name, state, after):
    bufs, sems = state

    def body(refs, sems_in, _):
        for w in range(len(bufs)):
            send, arrival = _share_copies(refs[w], sems_in[0], sems_in[1], w)
            arrival.wait_recv()
            send.wait_send()

    return _comm_call(name, body, bufs, sems, [], after)[0]


def _allreduce_small(g):
    rows = g.shape[0]
    half = rows // 2

    def body(g_ref, o_ref, sib, slots, send_sems, recv_sems):
        x, y, c, chips = _position()
        me, sibling = (x, y, c), (x, y, 1 - c)
        my_chip = 2 * x + y
        mine = pl.ds(pl.multiple_of(c * half, 8), half)
        theirs = pl.ds(pl.multiple_of((1 - c) * half, 8), half)
        pair = _remote(g_ref.at[theirs], sib, send_sems, recv_sems, 0, sibling)
        pair.start()
        pair.wait()
        slots[my_chip] = g_ref[mine, :] + sib[...]
        sent = []
        for j, (cx, cy) in enumerate(chips):
            cp = _remote(slots.at[my_chip], slots.at[my_chip], send_sems, recv_sems, 1 + j, (cx, cy, c))
            cp.start()
            sent.append(cp)
        for j, (cx, cy) in enumerate(chips):
            got = slots.at[2 * cx + cy]
            _remote(got, got, send_sems, recv_sems, 1 + j, me).wait_recv()
        for cp in sent:
            cp.wait_send()
        o_ref[mine, :] = ((slots[0] + slots[1]) + slots[2]) + slots[3]
        swap = _remote(o_ref.at[mine], o_ref.at[mine], send_sems, recv_sems, 4, sibling)
        swap.start()
        swap.wait()

    vm = pl.BlockSpec(memory_space=pltpu.VMEM)
    return pl.pallas_call(
        body, name="allreduce_small",
        in_specs=[vm], out_specs=vm, out_shape=jax.ShapeDtypeStruct((rows, 128), F32),
        scratch_shapes=[pltpu.VMEM((half, 128), F32), pltpu.VMEM((N_CHIPS, half, 128), F32),
                        pltpu.SemaphoreType.DMA((5,)), pltpu.SemaphoreType.DMA((5,))],
        compiler_params=pltpu.CompilerParams(vmem_limit_bytes=VMEM_LIMIT),
    )(g)


_SMALL =("rel_bias", "ln_v_gain", "ln_v_bias", "w_spatial", "b_spatial", "ln1_gain", "ln1_bias",
          "b_ff1", "b_ff2", "ln2_gain", "ln2_bias")
_SMALL_ROWS = 1200
_LOSS_AT = (152832 // 128, 0)


def _pack_small(parts):
    flat = jnp.concatenate([parts[k].reshape(-1).astype(F32) for k in _SMALL])
    flat = jnp.pad(flat, (0, _SMALL_ROWS * 128 - flat.shape[0]))
    return flat.reshape(_SMALL_ROWS, 128)


def _unpack_small(packed, like):
    flat = packed.reshape(-1)
    out, at = {}, 0
    for k in _SMALL:
        n = math.prod(like[k].shape)
        out[k] = flat[at:at + n].reshape(like[k].shape)
        at += n
    return out


def kernel(x, w_in, rel_bias, ln_v_gain, ln_v_bias, w_spatial, b_spatial, w_proj_a, w_proj_b, w_out, ln1_gain, ln1_bias, w_ff1, b_ff1, w_ff2, b_ff2, ln2_gain, ln2_bias, loss_target, m_w_in, m_rel_bias, m_ln_v_gain, m_ln_v_bias, m_w_spatial, m_b_spatial, m_w_proj_a, m_w_proj_b, m_w_out, m_ln1_gain, m_ln1_bias, m_w_ff1, m_b_ff1, m_w_ff2, m_b_ff2, m_ln2_gain, m_ln2_bias, v_w_in, v_rel_bias, v_ln_v_gain, v_ln_v_bias, v_w_spatial, v_b_spatial, v_w_proj_a, v_w_proj_b, v_w_out, v_ln1_gain, v_ln1_bias, v_w_ff1, v_b_ff1, v_w_ff2, v_b_ff2, v_ln2_gain, v_ln2_bias):
    args = dict(locals())
    big = ("w_in", "w_proj_a", "w_proj_b", "w_out", "w_ff1", "w_ff2")
    weights = ("w_in", "rel_bias", "ln_v_gain", "ln_v_bias", "w_spatial", "b_spatial", "w_proj_a", "w_proj_b", "w_out",
               "ln1_gain", "ln1_bias", "w_ff1", "b_ff1", "w_ff2", "b_ff2", "ln2_gain", "ln2_bias")

    xs = x[0]
    target = loss_target[0]

    ring = {"a": {"bufs": [_place_shard(w_in[0], "place_w_in")], "sems": {}}}
    tok = _ring_call("allgather_a_near", ring, [("start", "a", "ici_near")])
    placed = [_place_shard(args[k][0], f"place_{k}", after=tok) for k in big[1:]]
    for tag, bufs in (("b", placed[0:3]), ("c", placed[3:4]), ("d", placed[4:5])):
        ring[tag] = {"bufs": bufs, "sems": {}}
    xb = _to_bf16(xs, "x_to_bf16", after=placed[4])

    mx, my = lax.axis_index("x"), lax.axis_index("y")
    own = jnp.reshape(2 * mx + my, (1,)).astype(jnp.int32)
    near = jnp.stack([2 * (1 - mx) + my, 2 * mx + (1 - my)]).astype(jnp.int32)
    far = jnp.reshape(2 * (1 - mx) + (1 - my), (1,)).astype(jnp.int32)
    proj = _proj(xb, ring["a"]["bufs"][0], own, "proj_own")
    _ring_call("allgather_a_far", ring, [("wait", "a", "ici_near"), ("start", "a", "ici_far"), ("start", "a", "d2d_near"),
                                         ("start", "b", "ici_near"), ("start", "c", "ici_near")], after=proj)
    _ring_call("allgather_a_near_done", ring, [("wait", "a", "d2d_near")])
    proj = _proj(xb, ring["a"]["bufs"][0], near, "proj_near", into=proj)
    _ring_call("allgather_a_last", ring, [("wait", "a", "ici_far"), ("start", "a", "d2d_far")], after=proj)
    _ring_call("allgather_a_done", ring, [("wait", "a", "d2d_far")])
    (win_g,) = ring["a"]["bufs"]
    proj = _proj(xb, win_g, far, "proj_far", into=proj)
    _ring_call("allgather_b_far", ring, [("wait", "b", "ici_near"), ("start", "b", "ici_far"), ("start", "b", "d2d_near")],
               after=proj)
    ws = w_spatial[0]
    ws_t = jnp.transpose(ws, (0, 2, 1))
    bsp_b = jnp.broadcast_to(b_spatial[0][:, :, None], (NH, 128, 128))
    gmlp = _gmlp_fwd(proj, ws, bsp_b, ln_v_gain, ln_v_bias)
    attn, lse = _attention_fwd(proj, rel_bias)
    _ring_call("allgather_b_last_c_far", ring,
               [("wait", "b", "ici_far"), ("start", "b", "d2d_far"),
                ("wait", "c", "ici_near"), ("start", "c", "ici_far"), ("start", "c", "d2d_near"),
                ("start", "d", "ici_near")], after=attn)
    _ring_call("allgather_b_done", ring, [("wait", "b", "d2d_near"), ("wait", "b", "d2d_far")])
    wpa_g, wpb_g, wout_g = ring["b"]["bufs"]
    wout_full = wout_g.reshape(D, D)
    ya, yb, merged = _branch(attn, gmlp, wpa_g, wpb_g, proj)
    xhat1, rstd1, h1b = _out_ln1(merged, wout_full, xs, ln1_gain, ln1_bias)
    _ring_call("allgather_c_last_d_far", ring,
               [("wait", "c", "ici_far"), ("start", "c", "d2d_far"),
                ("wait", "d", "ici_near"), ("start", "d", "ici_far"), ("start", "d", "d2d_near")], after=h1b)
    _ring_call("allgather_c_done", ring, [("wait", "c", "d2d_near"), ("wait", "c", "d2d_far")])
    (w1_g,) = ring["c"]["bufs"]
    a, r = _ff1(h1b, w1_g, b_ff1)
    _ring_call("allgather_d_last", ring, [("wait", "d", "ici_far"), ("start", "d", "d2d_far")], after=a)
    _ring_call("allgather_d_done", ring, [("wait", "d", "d2d_near"), ("wait", "d", "d2d_far")])
    (w2_g,) = ring["d"]["bufs"]
    w2_full = w2_g.reshape(DFF, D)
    dpre2, dpre2b, st2 = _ff2_ln2_loss(a, w2_full, xhat1, ln1_gain, ln1_bias, b_ff2, ln2_gain, ln2_bias, target)

    def pair_and_chip(tag, state, after):
        local, from_sibling = _px_wait(f"pair_exchange_wait_{tag}", state, after)
        pair_sums = [_pair_sum(g, o, f"pair_sum_{tag}_{i}") for i, (g, o) in enumerate(zip(local, from_sibling))]
        return _cx_start(f"chip_exchange_start_{tag}", pair_sums)

    g_w2 = _grad_w(a, dpre2b, "grad_w_ff2", 512, 2048, False)
    px, tok = _px_start("pair_exchange_start_w_ff2", [g_w2.reshape(N_CHIPS, DFF // N_CHIPS, D)])
    dprea, g_b1 = _d_ff1(dpre2b, w2_full, r, after=tok)
    cx_w2, tok = pair_and_chip("w_ff2", px, dprea)
    g_w1 = _grad_w(h1b, dprea, "grad_w_ff1", 512, 2048, True, after=tok)
    px, tok = _px_start("pair_exchange_start_w_ff1", [g_w1])
    dpre1, dpre1b, st1 = _d_h1_ln1(dprea, w1_g, dpre2, xhat1, rstd1, ln1_gain, after=tok)
    cx_w1, tok = pair_and_chip("w_ff1", px, dpre1b)
    g_wout = _grad_w(merged, dpre1b, "grad_w_out", 512, 2048, False, after=tok)
    dya, dyb, dga, dgb = _d_merged(dpre1b, wout_full, proj, ya, yb)
    g_wpa = _grad_w(attn, dya, "grad_w_proj_a", 1024, 512, True)
    g_wpb = _grad_w(gmlp, dyb, "grad_w_proj_b", 1024, 512, True)
    px, tok = _px_start("pair_exchange_start_b", [g_wpa, g_wpb, g_wout.reshape(N_CHIPS, D // N_CHIPS, D)])
    dattn, dgmlp = _d_branches(dya, dyb, wpa_g, wpb_g, after=tok)
    duv, g_ws, g_bs, stv = _gmlp_bwd(proj, dgmlp, ws, ws_t, bsp_b, ln_v_gain, ln_v_bias)
    cx_b, tok = pair_and_chip("b", px, duv)
    dq, dk, dv, ds_sums = _attention_bwd(proj, dattn, attn, lse, rel_bias, after=tok)
    g_rb = _rel_bias_grad(ds_sums)[:, :NH]

    small_g = dict(rel_bias=g_rb, ln_v_gain=stv[0], ln_v_bias=stv[1], w_spatial=g_ws, b_spatial=g_bs[:, :, 0],
                   ln1_gain=st1[0], ln1_bias=st1[1], b_ff1=g_b1, b_ff2=st2[2], ln2_gain=st2[0], ln2_bias=st2[1])
    gs = _allreduce_small(_pack_small(small_g).at[_LOSS_AT].set(st2[3, 0]))
    ds_, ms_, vs_, _ = _adamw(_pack_small({k: args[k] for k in _SMALL}), gs,
                           _pack_small({k: args["m_" + k] for k in _SMALL}),
                           _pack_small({k: args["v_" + k] for k in _SMALL}), "adamw_small")
    like = {k: args[k] for k in _SMALL}
    grads, deltas, new_m, new_v = (_unpack_small(t, like) for t in (gs, ds_, ms_, vs_))

    dproj = jnp.concatenate([dq, dk, dv, duv, dga, dgb], axis=1)
    g_win = _grad_w(xb, dproj, "grad_w_in", 512, 2304, True, after=gs)
    px, tok = _px_start("pair_exchange_start_w_in", [g_win])

    def chip_sums(tag, state, names, after):
        pair_sums, from_chips = _cx_wait(f"chip_exchange_wait_{tag}", state, after)
        halves = [_chip_sum(p, own, f"chip_sum_{k}") for p, own, k in zip(from_chips, pair_sums, names)]
        return _share_start(f"share_start_{tag}", halves)

    def adam(tag, state, names, after):
        last = None
        for k, g in zip(names, _share_wait(f"share_wait_{tag}", state, after)):
            d_, m_, v_, g_ = _adamw(args[k][0], g, args["m_" + k][0], args["v_" + k][0], f"adamw_{k}")
            grads[k], deltas[k], new_m[k], new_v[k] = g_[None], d_[None], m_[None], v_[None]
            last = d_
        return last

    sh_w2, tok = chip_sums("w_ff2", cx_w2, ["w_ff2"], tok)
    sh_w1, tok = chip_sums("w_ff1", cx_w1, ["w_ff1"], tok)
    sh_b, tok = chip_sums("b", cx_b, ["w_proj_a", "w_proj_b", "w_out"], tok)
    cx_in, tok = pair_and_chip("w_in", px, tok)
    grad_x = _d_x(dproj, win_g, dpre1, after=tok)
    done = adam("w_ff2", sh_w2, ["w_ff2"], grad_x)
    done = adam("w_ff1", sh_w1, ["w_ff1"], done)
    done = adam("b", sh_b, ["w_proj_a", "w_proj_b", "w_out"], done)
    sh_in, tok = chip_sums("w_in", cx_in, ["w_in"], done)
    adam("w_in", sh_in, ["w_in"], tok)

    loss = gs[_LOSS_AT] * (0.5 / D)
    return (loss, grad_x[None], *[grads[k] for k in weights], *[deltas[k] for k in weights],
            *[new_m[k] for k in weights], *[new_v[k] for k in weights])
```

```python
import math

import numpy as np
import jax
import jax.numpy as jnp
from jax import lax
from jax.experimental import pallas as pl
from jax.experimental.pallas import tpu as pltpu

F32 = jnp.float32
BF16 = jnp.bfloat16

S = 2048
D = 2048
DA = 1024
DB = 1024
DFF = 8192
DIN = 9216
NH = 8
HD = 128
NBLK = 16
PATTERNS = ((128, 1), (512, 4), (2048, 16))
N_BUCKETS = 32
MAX_DISTANCE = 2048
ALPHA = 2.0 ** 0.25
LN_EPS = 1e-5
NEG_INF = -1e30
SCALE = HD ** -0.5
N_CHIPS = 4

ADAM_LR = 0.001
ADAM_B1 = 0.9
ADAM_B2 = 0.999
ADAM_EPS = 1e-08
ADAM_WD = 0.01
ADAM_STEP = 10

VMEM_LIMIT = 56 * 1024 * 1024
MESH = pl.DeviceIdType.MESH
ANY = pl.BlockSpec(memory_space=pl.ANY)


def _params(n_axes, vmem=VMEM_LIMIT):
    return pltpu.CompilerParams(dimension_semantics=("arbitrary",) * n_axes, vmem_limit_bytes=vmem)


def _bucket_tile(dilation):
    qi = np.arange(128)[:, None]
    kj = np.arange(256)[None, :]
    n = np.clip(128 + qi - kj, 0, 128) * dilation
    max_exact = N_BUCKETS // 2
    nf = np.maximum(n, 1).astype(np.float32)
    large = max_exact + (np.log(nf / np.float32(max_exact)) / np.float32(math.log(MAX_DISTANCE / max_exact))
                         * np.float32(N_BUCKETS - max_exact)).astype(np.int32)
    large = np.minimum(large, N_BUCKETS - 1)
    return np.where(n < max_exact, n, large).astype(np.int32)


def _gelu(x):
    c = math.sqrt(2.0 / math.pi)
    t = jnp.tanh(c * (x + 0.044715 * x * x * x))
    return 0.5 * x * (1.0 + t), t


def _gelu_grad(x, t):
    c = math.sqrt(2.0 / math.pi)
    return 0.5 * (1.0 + t) + 0.5 * x * (1.0 - t * t) * c * (1.0 + 3.0 * 0.044715 * x * x)


def _sigmoid(x):
    return 1.0 / (1.0 + jnp.exp(-x))


def _dot(a, b):
    return jnp.dot(a, b, preferred_element_type=F32)


def _behind(body, n_in, after):
    if after is None:
        return body, [], []
    return (lambda *refs: body(*refs[:n_in], *refs[n_in + 1:])), [ANY], [after]


def _dot_nt(a, b):
    return lax.dot_general(a, b, (((1,), (1,)), ((), ())), preferred_element_type=F32)


def _proj(xb, win_g, shards, name, into=None):
    tn = 768
    per = 2304 // tn

    def body(shards_ref, x_ref, w_ref, *rest):
        rest[-1][...] = _dot(x_ref[...], w_ref[...])

    in_specs = [pl.BlockSpec((S, D), lambda j, sh: (0, 0)),
                pl.BlockSpec((None, D, tn), lambda j, sh: (sh[j // per], 0, j % per))]
    return pl.pallas_call(
        body, name=name,
        grid_spec=pltpu.PrefetchScalarGridSpec(
            num_scalar_prefetch=1, grid=(shards.shape[0] * per,),
            in_specs=in_specs + ([ANY] if into is not None else []),
            out_specs=pl.BlockSpec((S, tn), lambda j, sh: (0, sh[j // per] * per + j % per))),
        out_shape=jax.ShapeDtypeStruct((S, DIN), F32),
        input_output_aliases={3: 0} if into is not None else {},
        compiler_params=_params(1),
    )(shards, xb, win_g, *([into] if into is not None else []))


FWD_HEADS_PER_STEP = 4
BWD_HEADS_PER_STEP = 2


def _head_bias_tiles(rb_ref, bk_ref, bias_scr, first_head, hps):
    qi = lax.broadcasted_iota(jnp.int32, (128, 256), 0)
    kj = lax.broadcasted_iota(jnp.int32, (128, 256), 1)
    steps = 128 + qi - kj
    band = (steps >= 0) & (steps <= 128)
    bias_scr[...] = jnp.zeros_like(bias_scr)
    for p in range(len(PATTERNS)):
        bucket = bk_ref[p]

        def one_bucket(t, carry):
            hit = bucket == t
            for j in range(hps):
                bias_scr[p, j] = jnp.where(hit, rb_ref[t, first_head + j], bias_scr[p, j])
            return carry

        lax.fori_loop(0, N_BUCKETS, one_bucket, 0)
        for j in range(hps):
            bias_scr[p, j] = jnp.where(band, bias_scr[p, j], NEG_INF)


def _block_rows(b, dilation):
    nblk = NBLK // dilation
    r, n = b // nblk, b % nblk
    start = r + n * (128 * dilation)
    prev_start = jnp.maximum(start - 128 * dilation, r)
    if dilation == 1:
        return pl.ds(pl.multiple_of(start, 128), 128), pl.ds(pl.multiple_of(prev_start, 128), 128), n > 0
    return pl.ds(start, 128, stride=dilation), pl.ds(prev_start, 128, stride=dilation), n > 0


def _head_specs(first, hps):
    return [pl.BlockSpec((S, HD), lambda g, j=j: (0, first + g * hps + j)) for j in range(hps)]


def _heads_spec(hps):
    return pl.BlockSpec((S, hps * HD), lambda g: (0, g))


def _attention_fwd(proj, rel_bias):
    hps = FWD_HEADS_PER_STEP
    buckets = jnp.asarray(np.stack([_bucket_tile(d) for _, d in PATTERNS]))

    def body(rb_ref, bk_ref, *refs):
        q_refs, k_refs, v_refs = (refs[i * hps:(i + 1) * hps] for i in range(3))
        o_ref, lse_ref, bias_scr = refs[3 * hps:3 * hps + 3]
        acc_scrs, m_scrs, l_scrs = (refs[3 * hps + 3 + i * hps:3 * hps + 3 + (i + 1) * hps] for i in range(3))
        _head_bias_tiles(rb_ref, bk_ref, bias_scr, pl.program_id(0) * hps, hps)
        kj = lax.broadcasted_iota(jnp.int32, (128, 256), 1)
        for p, (_, d) in enumerate(PATTERNS):
            prev_blocks = NBLK // d > 1

            def block(b, carry):
                units = [(j,) + _block_rows(blk, d) for blk in (b, b + NBLK // 2) for j in range(hps)]
                scores = []
                for j, rows, prows, _ in units:
                    q = q_refs[j][rows, :].astype(BF16)
                    cur = _dot_nt(q, k_refs[j][rows, :].astype(BF16))
                    if prev_blocks:
                        cur = jnp.concatenate([_dot_nt(q, k_refs[j][prows, :].astype(BF16)), cur], axis=1)
                    scores.append(cur)
                soft = []
                for u, (j, _, _, has_prev) in enumerate(units):
                    if prev_blocks:
                        s = jnp.where((kj >= 128) | has_prev, scores[u] * SCALE + bias_scr[p, j], NEG_INF)
                    else:
                        s = scores[u] * SCALE + bias_scr[p, j, :, 128:256]
                    m = jnp.max(s, axis=1, keepdims=True)
                    e = jnp.exp(s - m)
                    soft.append((m, jnp.sum(e, axis=1, keepdims=True), e.astype(BF16)))
                outs = []
                for u, (j, rows, prows, _) in enumerate(units):
                    e = soft[u][2]
                    if prev_blocks:
                        outs.append(_dot(e[:, :128], v_refs[j][prows, :].astype(BF16))
                                    + _dot(e[:, 128:], v_refs[j][rows, :].astype(BF16)))
                    else:
                        outs.append(_dot(e, v_refs[j][rows, :].astype(BF16)))
                for u, (j, rows, _, _) in enumerate(units):
                    acc_scr, m_scr, l_scr = acc_scrs[j], m_scrs[j], l_scrs[j]
                    (m, den, _), o = soft[u], outs[u]
                    if p == 0:
                        acc_scr[rows, :] = o
                        m_scr[rows, :] = jnp.broadcast_to(m, (128, HD))
                        l_scr[rows, :] = jnp.broadcast_to(den, (128, HD))
                    else:
                        m_old = m_scr[rows, :]
                        m_new = jnp.maximum(m_old, m)
                        w_old, w_new = jnp.exp(m_old - m_new), jnp.exp(m - m_new)
                        acc_scr[rows, :] = acc_scr[rows, :] * w_old + o * w_new
                        l_scr[rows, :] = l_scr[rows, :] * w_old + den * w_new
                        m_scr[rows, :] = m_new
                return carry

            lax.fori_loop(0, NBLK // 2, block, 0)
        for j in range(hps):
            cols = slice(j * HD, (j + 1) * HD)
            den = l_scrs[j][...]
            o_ref[:, cols] = (acc_scrs[j][...] / den).astype(BF16)
            lse_ref[:, cols] = m_scrs[j][...] + jnp.log(den)

    return pl.pallas_call(
        body, name="attention_fwd", grid=(NH // hps,),
        in_specs=[pl.BlockSpec(memory_space=pltpu.SMEM), pl.BlockSpec((3, 128, 256), lambda g: (0, 0, 0))]
        + _head_specs(0, hps) + _head_specs(NH, hps) + _head_specs(2 * NH, hps),
        out_specs=[_heads_spec(hps), _heads_spec(hps)],
        out_shape=[jax.ShapeDtypeStruct((S, DA), BF16), jax.ShapeDtypeStruct((S, DA), F32)],
        scratch_shapes=[pltpu.VMEM((3, hps, 128, 256), F32)] + [pltpu.VMEM((S, HD), F32)] * (3 * hps),
        compiler_params=_params(1),
    )(rel_bias, buckets, *([proj] * (3 * hps)))


def _attention_bwd(proj, dattn, attn, lse, rel_bias, after=None):
    hps = BWD_HEADS_PER_STEP

    def body(rb_ref, bk_ref, *refs):
        q_refs, k_refs, v_refs, do_refs, o_refs, lse_refs = (refs[i * hps:(i + 1) * hps] for i in range(6))
        dq_ref, dk_ref, dv_ref, ds_ref, bias_scr = refs[6 * hps:6 * hps + 5]
        dl_scrs, dq_scrs, dk_scrs, dv_scrs = (refs[6 * hps + 5 + i * hps:6 * hps + 5 + (i + 1) * hps] for i in range(4))
        _head_bias_tiles(rb_ref, bk_ref, bias_scr, pl.program_id(0) * hps, hps)
        ds_ref[...] = jnp.zeros_like(ds_ref)
        for j in range(hps):
            dq_scrs[j][...] = jnp.zeros((S, HD), F32)
            dk_scrs[j][...] = jnp.zeros((S, HD), F32)
            dv_scrs[j][...] = jnp.zeros((S, HD), F32)
            prod = do_refs[j][...] * o_refs[j][...].astype(F32)
            dl_scrs[j][...] = jnp.broadcast_to(jnp.sum(prod, axis=1, keepdims=True), (S, HD))
        for p, (_, d) in enumerate(PATTERNS):
            prev_blocks = NBLK // d > 1

            def block(b, carry):
                units = [(j,) + _block_rows(b + i * (NBLK // 4), d) for i in range(4) for j in range(hps)]
                ops, raw = [], []
                for j, rows, prows, _ in units:
                    q, do = q_refs[j][rows, :].astype(BF16), do_refs[j][rows, :].astype(BF16)
                    kc, vc = k_refs[j][rows, :].astype(BF16), v_refs[j][rows, :].astype(BF16)
                    if prev_blocks:
                        kp, vp = k_refs[j][prows, :].astype(BF16), v_refs[j][prows, :].astype(BF16)
                        ops.append((q, do, kc, kp))
                        raw.append((_dot_nt(q, kc), _dot_nt(do, vc), _dot_nt(q, kp), _dot_nt(do, vp)))
                    else:
                        ops.append((q, do, kc))
                        raw.append((_dot_nt(q, kc), _dot_nt(do, vc)))
                probs = []
                for u, (j, rows, _, has_prev) in enumerate(units):
                    lse_b, dl_b = lse_refs[j][rows, :], dl_scrs[j][rows, :]
                    p_c = jnp.exp(raw[u][0] * SCALE + bias_scr[p, j, :, 128:256] - lse_b)
                    ds_c = p_c * (raw[u][1] - dl_b)
                    ds_ref[p, j, :, 128:256] += ds_c
                    if prev_blocks:
                        p_p = jnp.where(has_prev, jnp.exp(raw[u][2] * SCALE + bias_scr[p, j, :, 0:128] - lse_b), 0.0)
                        ds_p = p_p * (raw[u][3] - dl_b)
                        ds_ref[p, j, :, 0:128] += ds_p
                        probs.append((p_c, ds_c, p_p, ds_p))
                    else:
                        probs.append((p_c, ds_c))
                grads = []
                for u in range(len(units)):
                    q, do, kc = ops[u][:3]
                    p_c, ds_c = probs[u][:2]
                    dq = _dot(ds_c.astype(BF16), kc)
                    cur = (_dot(ds_c.T.astype(BF16), q) * SCALE, _dot(p_c.T.astype(BF16), do))
                    if prev_blocks:
                        p_p, ds_p = probs[u][2:]
                        dq = dq + _dot(ds_p.astype(BF16), ops[u][3])
                        cur = cur + (_dot(ds_p.T.astype(BF16), q) * SCALE, _dot(p_p.T.astype(BF16), do))
                    grads.append((dq * SCALE,) + cur)
                for u, (j, rows, prows, _) in enumerate(units):
                    dq_scrs[j][rows, :] += grads[u][0]
                    dk_scrs[j][rows, :] += grads[u][1]
                    dv_scrs[j][rows, :] += grads[u][2]
                    if prev_blocks:
                        dk_scrs[j][prows, :] += grads[u][3]
                        dv_scrs[j][prows, :] += grads[u][4]
                return carry

            lax.fori_loop(0, NBLK // 4, block, 0)
        for j in range(hps):
            cols = slice(j * HD, (j + 1) * HD)
            dq_ref[:, cols] = dq_scrs[j][...].astype(BF16)
            dk_ref[:, cols] = dk_scrs[j][...].astype(BF16)
            dv_ref[:, cols] = dv_scrs[j][...].astype(BF16)

    buckets = jnp.asarray(np.stack([_bucket_tile(d) for _, d in PATTERNS]))
    body, more_specs, more = _behind(body, 2 + 6 * hps, after)
    return pl.pallas_call(
        body, name="attention_bwd", grid=(NH // hps,),
        in_specs=[pl.BlockSpec(memory_space=pltpu.SMEM), pl.BlockSpec((3, 128, 256), lambda g: (0, 0, 0))]
        + _head_specs(0, hps) + _head_specs(NH, hps) + _head_specs(2 * NH, hps) + 3 * _head_specs(0, hps)
        + more_specs,
        out_specs=3 * [_heads_spec(hps)] + [pl.BlockSpec((3, hps, 128, 256), lambda g: (0, g, 0, 0))],
        out_shape=[jax.ShapeDtypeStruct((S, DA), BF16)] * 3 + [jax.ShapeDtypeStruct((3, NH, 128, 256), F32)],
        scratch_shapes=[pltpu.VMEM((3, hps, 128, 256), F32)] + [pltpu.VMEM((S, HD), F32)] * (4 * hps),
        compiler_params=_params(1),
    )(rel_bias, buckets, *([proj] * (3 * hps)), *([dattn] * hps), *([attn] * hps), *([lse] * hps), *more)


def _gmlp_parts(u_ref, vb_ref, g_ref, be_ref):
    u = u_ref[...]
    u_act, tu = _gelu(u)
    vb = vb_ref[...]
    gv, tv = _gelu(vb)
    mean = jnp.mean(gv, axis=1, keepdims=True)
    cen = gv - mean
    var = jnp.mean(cen * cen, axis=1, keepdims=True)
    rstd = lax.rsqrt(var + LN_EPS)
    xhat = cen * rstd
    vn = xhat * g_ref[...] + be_ref[...]
    return u, tu, u_act, vb, tv, rstd, xhat, vn


def _gmlp_fwd(proj, ws, bsp_b, gain_v, bias_v):
    def body(u_ref, vb_ref, ws_ref, bsp_ref, g_ref, be_ref, o_ref):
        _, _, u_act, _, _, _, _, vn = _gmlp_parts(u_ref, vb_ref, g_ref, be_ref)
        row = lax.broadcasted_iota(jnp.int32, (128, 128), 0)
        col = lax.broadcasted_iota(jnp.int32, (128, 128), 1)
        causal = row >= col
        for g in range(NH):
            cols = slice(g * 128, (g + 1) * 128)
            wsg = jnp.where(causal, ws_ref[g], 0.0).astype(BF16)
            z = _dot(wsg, vn[:, cols].astype(BF16)) + bsp_ref[g]
            o_ref[:, cols] = (u_act[:, cols] * z).astype(BF16)

    return pl.pallas_call(
        body, name="gmlp_fwd", grid=(NBLK,),
        in_specs=[pl.BlockSpec((128, DB), lambda c: (c, 3)), pl.BlockSpec((128, DB), lambda c: (c, 4)),
                  pl.BlockSpec((NH, 128, 128), lambda c: (0, 0, 0)), pl.BlockSpec((NH, 128, 128), lambda c: (0, 0, 0)),
                  pl.BlockSpec((1, DB), lambda c: (0, 0)), pl.BlockSpec((1, DB), lambda c: (0, 0))],
        out_specs=pl.BlockSpec((128, DB), lambda c: (c, 0)),
        out_shape=jax.ShapeDtypeStruct((S, DB), BF16),
        compiler_params=_params(1),
    )(proj, proj, ws, bsp_b, gain_v, bias_v)


def _branch(attn, gmlp, wpa_g, wpb_g, proj):
    tn = 512

    def body(a_ref, g_ref, wa_ref, wb_ref, ga_ref, gb_ref, ya_ref, yb_ref, mg_ref):
        ya = _dot(a_ref[...], wa_ref[...])
        yb = _dot(g_ref[...], wb_ref[...])
        ya_ref[...] = ya.astype(BF16)
        yb_ref[...] = yb.astype(BF16)
        mg_ref[...] = (_sigmoid(ga_ref[...]) * ya + _sigmoid(gb_ref[...]) * yb).astype(BF16)

    out = pl.BlockSpec((S, tn), lambda j: (0, j))
    return pl.pallas_call(
        body, name="branch", grid=(D // tn,),
        in_specs=[pl.BlockSpec((S, DA), lambda j: (0, 0)), pl.BlockSpec((S, DB), lambda j: (0, 0)),
                  pl.BlockSpec((None, DA, tn), lambda j: (j, 0, 0)), pl.BlockSpec((None, DB, tn), lambda j: (j, 0, 0)),
                  pl.BlockSpec((S, tn), lambda j: (0, 5120 // tn + j)), pl.BlockSpec((S, tn), lambda j: (0, 7168 // tn + j))],
        out_specs=[out, out, out],
        out_shape=[jax.ShapeDtypeStruct((S, D), BF16)] * 3,
        compiler_params=_params(1),
    )(attn, gmlp, wpa_g, wpb_g, proj, proj)


def _out_ln1(merged, wout_g, x, gain, bias):
    tm = 256

    def body(m_ref, w_ref, x_ref, g_ref, b_ref, xh_ref, rs_ref, h_ref):
        pre = ALPHA * x_ref[...] + _dot(m_ref[...], w_ref[...])
        mean = jnp.mean(pre, axis=1, keepdims=True)
        cen = pre - mean
        var = jnp.mean(cen * cen, axis=1, keepdims=True)
        rstd = lax.rsqrt(var + LN_EPS)
        xhat = cen * rstd
        xh_ref[...] = xhat
        rs_ref[...] = jnp.broadcast_to(rstd, (tm, 128))
        h_ref[...] = (xhat * g_ref[...] + b_ref[...]).astype(BF16)

    row = pl.BlockSpec((tm, D), lambda i: (i, 0))
    vec = pl.BlockSpec((1, D), lambda i: (0, 0))
    return pl.pallas_call(
        body, name="out_ln1", grid=(S // tm,),
        in_specs=[row, pl.BlockSpec((D, D), lambda i: (0, 0)), row, vec, vec],
        out_specs=[row, pl.BlockSpec((tm, 128), lambda i: (i, 0)), row],
        out_shape=[jax.ShapeDtypeStruct((S, D), F32), jax.ShapeDtypeStruct((S, 128), F32),
                   jax.ShapeDtypeStruct((S, D), BF16)],
        compiler_params=_params(1),
    )(merged, wout_g, x, gain, bias)


def _ff1(h1b, w1_g, b1):
    tn = 512
    per = D // tn

    def body(h_ref, w_ref, b_ref, a_ref, r_ref):
        r = jnp.maximum(_dot(h_ref[...], w_ref[...]) + b_ref[...], 0.0)
        r_ref[...] = r.astype(BF16)
        a_ref[...] = (r * r).astype(BF16)

    out = pl.BlockSpec((S, tn), lambda j: (0, j))
    return pl.pallas_call(
        body, name="ff1", grid=(DFF // tn,),
        in_specs=[pl.BlockSpec((S, D), lambda j: (0, 0)),
                  pl.BlockSpec((None, D, tn), lambda j: (j // per, 0, j % per)),
                  pl.BlockSpec((1, tn), lambda j: (0, j))],
        out_specs=[out, out],
        out_shape=[jax.ShapeDtypeStruct((S, DFF), BF16)] * 2,
        compiler_params=_params(1),
    )(h1b, w1_g, b1)


def _ff2_ln2_loss(a, w2_g, xhat1, g1, b1, b2, g2, be2, target):
    tm, tk = 512, 1024
    nk = DFF // tk

    def body(a_ref, w_ref, xh_ref, g1_ref, b1_ref, b2_ref, g2_ref, be2_ref, t_ref, d_ref, db_ref, st_ref, acc):
        i, k = pl.program_id(0), pl.program_id(1)

        @pl.when(k == 0)
        def _():
            acc[...] = jnp.zeros_like(acc)

        @pl.when((i == 0) & (k == 0))
        def _():
            st_ref[...] = jnp.zeros_like(st_ref)

        acc[...] += _dot(a_ref[...], w_ref[...])

        @pl.when(k == nk - 1)
        def _():
            def rows_chunk(ci, carry):
                rows = pl.ds(pl.multiple_of(ci * 128, 128), 128)
                h1 = xh_ref[rows, :] * g1_ref[...] + b1_ref[...]
                pre = ALPHA * h1 + acc[rows, :] + b2_ref[...]
                mean = jnp.mean(pre, axis=1, keepdims=True)
                cen = pre - mean
                var = jnp.mean(cen * cen, axis=1, keepdims=True)
                rstd = lax.rsqrt(var + LN_EPS)
                xhat = cen * rstd
                y = xhat * g2_ref[...] + be2_ref[...]
                err = y - t_ref[rows, :]
                dy = err * (1.0 / D)
                g = dy * g2_ref[...]
                dpre = rstd * (g - jnp.mean(g, axis=1, keepdims=True)
                               - xhat * jnp.mean(g * xhat, axis=1, keepdims=True))
                d_ref[rows, :] = dpre
                db_ref[rows, :] = dpre.astype(BF16)
                st_ref[0:1, :] += jnp.sum(dy * xhat, axis=0, keepdims=True)
                st_ref[1:2, :] += jnp.sum(dy, axis=0, keepdims=True)
                st_ref[2:3, :] += jnp.sum(dpre, axis=0, keepdims=True)
                st_ref[3:4, :] += jnp.broadcast_to(jnp.sum(err * err).reshape(1, 1), (1, D))
                return carry

            lax.fori_loop(0, tm // 128, rows_chunk, 0)

    row = pl.BlockSpec((tm, D), lambda i, k: (i, 0))
    vec = pl.BlockSpec((1, D), lambda i, k: (0, 0))
    return pl.pallas_call(
        body, name="ff2_ln2_loss", grid=(S // tm, nk),
        in_specs=[pl.BlockSpec((tm, tk), lambda i, k: (i, k)), pl.BlockSpec((tk, D), lambda i, k: (k, 0)),
                  row, vec, vec, vec, vec, vec, row],
        out_specs=[row, row, pl.BlockSpec((8, D), lambda i, k: (0, 0))],
        out_shape=[jax.ShapeDtypeStruct((S, D), F32), jax.ShapeDtypeStruct((S, D), BF16),
                   jax.ShapeDtypeStruct((8, D), F32)],
        scratch_shapes=[pltpu.VMEM((tm, D), F32)],
        compiler_params=_params(2),
    )(a, w2_g, xhat1, g1, b1, b2, g2, be2, target)


def _grad_w(act, dout, name, ti, tj, sharded, after=None):
    m, n = act.shape[1], dout.shape[1]
    ns = n // N_CHIPS
    per = ns // tj if sharded else None

    def body(a_ref, b_ref, o_ref, at_scr):
        @pl.when(pl.program_id(1) == 0)
        def _():
            at_scr[...] = a_ref[...].T

        o_ref[...] = _dot(at_scr[...], b_ref[...]).astype(BF16)

    if sharded:
        out_spec = pl.BlockSpec((None, ti, tj), lambda i, j: (j // per, i, j % per))
        out_shape = jax.ShapeDtypeStruct((N_CHIPS, m, ns), BF16)
    else:
        out_spec = pl.BlockSpec((ti, tj), lambda i, j: (i, j))
        out_shape = jax.ShapeDtypeStruct((m, n), BF16)
    body, more_specs, more = _behind(body, 2, after)
    return pl.pallas_call(
        body, name=name, grid=(m // ti, n // tj),
        in_specs=[pl.BlockSpec((S, ti), lambda i, j: (0, i)), pl.BlockSpec((S, tj), lambda i, j: (0, j))] + more_specs,
        out_specs=out_spec, out_shape=out_shape,
        scratch_shapes=[pltpu.VMEM((ti, S), BF16)],
        compiler_params=_params(2),
    )(act, dout, *more)


def _d_ff1(dpre2b, w2_g, r, after=None):
    tn = 512

    def body(d_ref, w_ref, r_ref, o_ref, gb_ref):
        da = _dot_nt(d_ref[...], w_ref[...])
        dp = da * (2.0 * r_ref[...].astype(F32))
        o_ref[...] = dp.astype(BF16)
        gb_ref[...] = jnp.sum(dp, axis=0, keepdims=True)

    body, more_specs, more = _behind(body, 3, after)
    return pl.pallas_call(
        body, name="d_ff1", grid=(DFF // tn,),
        in_specs=[pl.BlockSpec((S, D), lambda j: (0, 0)), pl.BlockSpec((tn, D), lambda j: (j, 0)),
                  pl.BlockSpec((S, tn), lambda j: (0, j))] + more_specs,
        out_specs=[pl.BlockSpec((S, tn), lambda j: (0, j)), pl.BlockSpec((1, tn), lambda j: (0, j))],
        out_shape=[jax.ShapeDtypeStruct((S, DFF), BF16), jax.ShapeDtypeStruct((1, DFF), F32)],
        compiler_params=_params(1),
    )(dpre2b, w2_g, r, *more)


def _d_h1_ln1(dprea, w1_g, dpre2, xhat1, rstd1, g1, after=None):
    tm, tk = 512, 1024
    per = D // tk
    nk = DFF // tk

    def body(a_ref, w_ref, d2_ref, xh_ref, rs_ref, g_ref, d_ref, db_ref, st_ref, acc):
        i, k = pl.program_id(0), pl.program_id(1)

        @pl.when(k == 0)
        def _():
            acc[...] = jnp.zeros_like(acc)

        @pl.when((i == 0) & (k == 0))
        def _():
            st_ref[...] = jnp.zeros_like(st_ref)

        acc[...] += _dot_nt(a_ref[...], w_ref[...])

        @pl.when(k == nk - 1)
        def _():
            def rows_chunk(ci, carry):
                rows = pl.ds(pl.multiple_of(ci * 128, 128), 128)
                dh = ALPHA * d2_ref[rows, :] + acc[rows, :]
                xhat = xh_ref[rows, :]
                g = dh * g_ref[...]
                dpre = rs_ref[rows, 0:1] * (g - jnp.mean(g, axis=1, keepdims=True)
                                            - xhat * jnp.mean(g * xhat, axis=1, keepdims=True))
                d_ref[rows, :] = dpre
                db_ref[rows, :] = dpre.astype(BF16)
                st_ref[0:1, :] += jnp.sum(dh * xhat, axis=0, keepdims=True)
                st_ref[1:2, :] += jnp.sum(dh, axis=0, keepdims=True)
                return carry

            lax.fori_loop(0, tm // 128, rows_chunk, 0)

    row = pl.BlockSpec((tm, D), lambda i, k: (i, 0))
    body, more_specs, more = _behind(body, 6, after)
    return pl.pallas_call(
        body, name="d_h1_ln1", grid=(S // tm, nk),
        in_specs=[pl.BlockSpec((tm, tk), lambda i, k: (i, k)),
                  pl.BlockSpec((None, D, tk), lambda i, k: (k // per, 0, k % per)),
                  row, row, pl.BlockSpec((tm, 128), lambda i, k: (i, 0)), pl.BlockSpec((1, D), lambda i, k: (0, 0))]
        + more_specs,
        out_specs=[row, row, pl.BlockSpec((8, D), lambda i, k: (0, 0))],
        out_shape=[jax.ShapeDtypeStruct((S, D), F32), jax.ShapeDtypeStruct((S, D), BF16),
                   jax.ShapeDtypeStruct((8, D), F32)],
        scratch_shapes=[pltpu.VMEM((tm, D), F32)],
        compiler_params=_params(2),
    )(dprea, w1_g, dpre2, xhat1, rstd1, g1, *more)


def _d_merged(dpre1b, wout_g, proj, ya, yb):
    tm, tn = 512, 1024

    def body(d_ref, w_ref, ga_ref, gb_ref, ya_ref, yb_ref, dya_ref, dyb_ref, dga_ref, dgb_ref):
        dm = _dot_nt(d_ref[...], w_ref[...])
        sa = _sigmoid(ga_ref[...])
        sb = _sigmoid(gb_ref[...])
        dya_ref[...] = (dm * sa).astype(BF16)
        dyb_ref[...] = (dm * sb).astype(BF16)
        dga_ref[...] = (dm * ya_ref[...].astype(F32) * sa * (1.0 - sa)).astype(BF16)
        dgb_ref[...] = (dm * yb_ref[...].astype(F32) * sb * (1.0 - sb)).astype(BF16)

    tile = pl.BlockSpec((tm, tn), lambda i, j: (i, j))
    return pl.pallas_call(
        body, name="d_merged", grid=(S // tm, D // tn),
        in_specs=[pl.BlockSpec((tm, D), lambda i, j: (i, 0)), pl.BlockSpec((tn, D), lambda i, j: (j, 0)),
                  pl.BlockSpec((tm, tn), lambda i, j: (i, 5 + j)), pl.BlockSpec((tm, tn), lambda i, j: (i, 7 + j)),
                  tile, tile],
        out_specs=[tile] * 4,
        out_shape=[jax.ShapeDtypeStruct((S, D), BF16)] * 4,
        compiler_params=_params(2),
    )(dpre1b, wout_g, proj, proj, ya, yb)


def _d_branches(dya, dyb, wpa_g, wpb_g, after=None):
    tk = 512

    def body(da_ref, db_ref, wa_ref, wb_ref, oa_ref, ob_ref):
        @pl.when(pl.program_id(0) == 0)
        def _():
            oa_ref[...] = jnp.zeros_like(oa_ref)
            ob_ref[...] = jnp.zeros_like(ob_ref)

        oa_ref[...] += _dot_nt(da_ref[...], wa_ref[...])
        ob_ref[...] += _dot_nt(db_ref[...], wb_ref[...])

    body, more_specs, more = _behind(body, 4, after)
    return pl.pallas_call(
        body, name="d_branches", grid=(D // tk,),
        in_specs=[pl.BlockSpec((S, tk), lambda k: (0, k)), pl.BlockSpec((S, tk), lambda k: (0, k)),
                  pl.BlockSpec((None, DA, tk), lambda k: (k, 0, 0)), pl.BlockSpec((None, DB, tk), lambda k: (k, 0, 0))]
        + more_specs,
        out_specs=[pl.BlockSpec((S, DA), lambda k: (0, 0)), pl.BlockSpec((S, DB), lambda k: (0, 0))],
        out_shape=[jax.ShapeDtypeStruct((S, DA), F32), jax.ShapeDtypeStruct((S, DB), F32)],
        compiler_params=_params(1),
    )(dya, dyb, wpa_g, wpb_g, *more)


def _gmlp_bwd(proj, dgmlp, ws, ws_t, bsp_b, gain_v, bias_v):
    def body(u_ref, vb_ref, dg_ref, ws_ref, wst_ref, bsp_ref, g_ref, be_ref, duv_ref, gws_ref, gbs_ref, st_ref):
        @pl.when(pl.program_id(0) == 0)
        def _():
            gws_ref[...] = jnp.zeros_like(gws_ref)
            gbs_ref[...] = jnp.zeros_like(gbs_ref)
            st_ref[...] = jnp.zeros_like(st_ref)

        u, tu, u_act, vb, tv, rstd, xhat, vn = _gmlp_parts(u_ref, vb_ref, g_ref, be_ref)
        dg = dg_ref[...]
        dz = dg * u_act
        row = lax.broadcasted_iota(jnp.int32, (128, 128), 0)
        col = lax.broadcasted_iota(jnp.int32, (128, 128), 1)
        causal = row >= col
        causal_t = row <= col
        dvn_parts = []
        z_parts = []
        for g in range(NH):
            cols = slice(g * 128, (g + 1) * 128)
            vng = vn[:, cols].astype(BF16)
            dzg = dz[:, cols]
            dzb = dzg.astype(BF16)
            wsg = jnp.where(causal, ws_ref[g], 0.0).astype(BF16)
            wsg_t = jnp.where(causal_t, wst_ref[g], 0.0).astype(BF16)
            z_parts.append(_dot(wsg, vng) + bsp_ref[g])
            gws_ref[g] += jnp.where(causal, _dot_nt(dzb, vng), 0.0)
            gbs_ref[g] += jnp.broadcast_to(jnp.sum(dzg, axis=1, keepdims=True), (128, 128))
            dvn_parts.append(_dot(wsg_t, dzb))
        z = jnp.concatenate(z_parts, axis=1)
        dvn = jnp.concatenate(dvn_parts, axis=1)
        du = dg * z * _gelu_grad(u, tu)
        st_ref[0:1, :] += jnp.sum(dvn * xhat, axis=0, keepdims=True)
        st_ref[1:2, :] += jnp.sum(dvn, axis=0, keepdims=True)
        gg = dvn * g_ref[...]
        dgv = rstd * (gg - jnp.mean(gg, axis=1, keepdims=True) - xhat * jnp.mean(gg * xhat, axis=1, keepdims=True))
        dvb = dgv * _gelu_grad(vb, tv)
        duv_ref[:, 0:DB] = du.astype(BF16)
        duv_ref[:, DB:2 * DB] = dvb.astype(BF16)

    full3 = pl.BlockSpec((NH, 128, 128), lambda c: (0, 0, 0))
    vec = pl.BlockSpec((1, DB), lambda c: (0, 0))
    return pl.pallas_call(
        body, name="gmlp_bwd", grid=(NBLK,),
        in_specs=[pl.BlockSpec((128, DB), lambda c: (c, 3)), pl.BlockSpec((128, DB), lambda c: (c, 4)),
                  pl.BlockSpec((128, DB), lambda c: (c, 0)), full3, full3, full3, vec, vec],
        out_specs=[pl.BlockSpec((128, 2 * DB), lambda c: (c, 0)), full3, full3, pl.BlockSpec((8, DB), lambda c: (0, 0))],
        out_shape=[jax.ShapeDtypeStruct((S, 2 * DB), BF16), jax.ShapeDtypeStruct((NH, 128, 128), F32),
                   jax.ShapeDtypeStruct((NH, 128, 128), F32), jax.ShapeDtypeStruct((8, DB), F32)],
        compiler_params=_params(1),
    )(proj, proj, dgmlp, ws, ws_t, bsp_b, gain_v, bias_v)


def _rel_bias_grad(ds_sums):
    buckets = jnp.asarray(np.stack([_bucket_tile(d) for _, d in PATTERNS]))

    def body(bk_ref, ds_ref, o_ref):
        row = lax.broadcasted_iota(jnp.int32, (N_BUCKETS, 128), 0)
        lane = lax.broadcasted_iota(jnp.int32, (N_BUCKETS, 128), 1)

        def one_bucket(t, out):
            hits = [bk_ref[p] == t for p in range(3)]
            for h in range(NH):
                tot = jnp.zeros((128, 256), F32)
                for p in range(3):
                    tot = tot + jnp.where(hits[p], ds_ref[p, h], 0.0)
                out = jnp.where((row == t) & (lane == h), jnp.sum(tot), out)
            return out

        o_ref[...] = lax.fori_loop(0, N_BUCKETS, one_bucket, jnp.zeros((N_BUCKETS, 128), F32))

    return pl.pallas_call(
        body, name="rel_bias_grad",
        in_specs=[pl.BlockSpec(memory_space=pltpu.VMEM)] * 2, out_specs=pl.BlockSpec(memory_space=pltpu.VMEM),
        out_shape=jax.ShapeDtypeStruct((N_BUCKETS, 128), F32),
        compiler_params=pltpu.CompilerParams(vmem_limit_bytes=VMEM_LIMIT),
    )(buckets, ds_sums)


def _d_x(dproj, win_g, dpre1, after=None):
    tm, tn = 512, 512
    ws = DIN // N_CHIPS

    def body(a_ref, w_ref, d_ref, o_ref):
        acc = ALPHA * d_ref[...]
        for s in range(N_CHIPS):
            acc = acc + _dot_nt(a_ref[:, s * ws:(s + 1) * ws], w_ref[s])
        o_ref[...] = acc

    tile = pl.BlockSpec((tm, tn), lambda i, j: (i, j))
    body, more_specs, more = _behind(body, 3, after)
    return pl.pallas_call(
        body, name="d_x", grid=(S // tm, D // tn),
        in_specs=[pl.BlockSpec((tm, DIN), lambda i, j: (i, 0)),
                  pl.BlockSpec((N_CHIPS, tn, ws), lambda i, j: (0, j, 0)), tile] + more_specs,
        out_specs=tile, out_shape=jax.ShapeDtypeStruct((S, D), F32),
        compiler_params=_params(2),
    )(dproj, win_g, dpre1, *more)


def _adamw(w, g, m, v, name):
    rows, cols = w.shape
    tm = max(t for t in range(8, 257, 8) if rows % t == 0)

    def body(w_ref, g_ref, m_ref, v_ref, d_ref, nm_ref, nv_ref, go_ref):
        g = g_ref[...]
        m = ADAM_B1 * m_ref[...] + (1.0 - ADAM_B1) * g
        v = ADAM_B2 * v_ref[...] + (1.0 - ADAM_B2) * (g * g)
        m_hat = m / (1.0 - ADAM_B1 ** ADAM_STEP)
        v_hat = v / (1.0 - ADAM_B2 ** ADAM_STEP)
        d_ref[...] = -ADAM_LR * (m_hat / (jnp.sqrt(v_hat) + ADAM_EPS) + ADAM_WD * w_ref[...])
        nm_ref[...] = m
        nv_ref[...] = v
        go_ref[...] = g

    spec = pl.BlockSpec((tm, cols), lambda i: (i, 0))
    return pl.pallas_call(
        body, name=name, grid=(rows // tm,), in_specs=[spec] * 4, out_specs=[spec] * 4,
        out_shape=[jax.ShapeDtypeStruct((rows, cols), F32)] * 4, compiler_params=_params(1),
    )(w, g, m, v)


def _position():
    x, y, c = lax.axis_index("x"), lax.axis_index("y"), lax.axis_index("c")
    chips = [(1 - x, y), (x, 1 - y), (1 - x, 1 - y)]
    return x, y, c, chips


def _remote(src, dst, send_sems, recv_sems, k, to):
    return pltpu.make_async_remote_copy(src_ref=src, dst_ref=dst, send_sem=send_sems.at[k], recv_sem=recv_sems.at[k],
                                        device_id=to, device_id_type=MESH)


def _place_shard(w, name, after=None):
    rows, cols = w.shape
    tm = 256
    x, y = lax.axis_index("x"), lax.axis_index("y")

    def body(chip_ref, w_ref, o_ref):
        o_ref[...] = w_ref[...].astype(BF16)

    more_specs, more = ([ANY], [after]) if after is not None else ([], [])
    if after is not None:
        inner = body
        body = lambda chip_ref, w_ref, after_ref, o_ref: inner(chip_ref, w_ref, o_ref)
    return pl.pallas_call(
        body, name=name,
        grid_spec=pltpu.PrefetchScalarGridSpec(
            num_scalar_prefetch=1, grid=(rows // tm,),
            in_specs=[pl.BlockSpec((tm, cols), lambda i, chip: (i, 0))] + more_specs,
            out_specs=pl.BlockSpec((None, tm, cols), lambda i, chip: (chip[0], i, 0))),
        out_shape=jax.ShapeDtypeStruct((N_CHIPS, rows, cols), BF16),
        compiler_params=_params(1),
    )(jnp.reshape(2 * x + y, (1,)).astype(jnp.int32), w, *more)


def _to_bf16(x, name, after=None):
    tm = 256

    def body(x_ref, o_ref):
        o_ref[...] = x_ref[...].astype(BF16)

    spec = pl.BlockSpec((tm, x.shape[1]), lambda i: (i, 0))
    body, more_specs, more = _behind(body, 1, after)
    return pl.pallas_call(
        body, name=name, grid=(x.shape[0] // tm,), in_specs=[spec] + more_specs, out_specs=spec,
        out_shape=jax.ShapeDtypeStruct(x.shape, BF16), compiler_params=_params(1),
    )(x, *more)


HBM = pl.BlockSpec(memory_space=pltpu.HBM)
SEM = pl.BlockSpec(memory_space=pltpu.SEMAPHORE)
EFFECT = pltpu.SideEffectType.DATAFLOW_SIDE_EFFECTING


def _comm_call(name, body, bufs, sems_in, sems_out, after=None, token=False):
    nb, ns, no = len(bufs), len(sems_in), len(sems_out)
    n_in = nb + ns + (after is not None)

    def wrapped(*refs):
        body(refs[:nb], refs[nb:nb + ns], refs[n_in + nb:n_in + nb + no])
        if token:
            refs[-1][...] = jnp.zeros((8, 128), F32)

    outs = pl.pallas_call(
        wrapped, name=name,
        in_specs=[HBM] * nb + [SEM] * ns + ([ANY] if after is not None else []),
        out_specs=[HBM] * nb + [SEM] * no + ([pl.BlockSpec(memory_space=pltpu.VMEM)] if token else []),
        out_shape=[pltpu.HBM(b.shape, b.dtype) for b in bufs] + [pltpu.SemaphoreType.DMA((k,)) for k in sems_out]
        + ([jax.ShapeDtypeStruct((8, 128), F32)] if token else []),
        input_output_aliases={i: i for i in range(nb)},
        compiler_params=pltpu.CompilerParams(has_side_effects=EFFECT),
    )(*[pltpu.with_memory_space_constraint(b, pltpu.HBM) for b in bufs], *sems_in, *([after] if after is not None else []))
    return list(outs[:nb]), list(outs[nb:nb + no]), (outs[-1] if token else None)


RING_STAGES = {"ici_near": 2, "ici_far": 2, "d2d_near": 2, "d2d_far": 1}


def _ring_copies(buf, send_sems, recv_sems, k0, stage):
    x, y, c, _ = _position()
    hr = buf.shape[1] // 2
    qr = hr // 2
    half = lambda chip, h: buf.at[chip, pl.ds(h * hr, hr), :]
    quarter = lambda chip, h, q: buf.at[chip, pl.ds(h * hr + q * qr, qr), :]
    mine, x_chip, y_chip, far_chip = 2 * x + y, 2 * (1 - x) + y, 2 * x + (1 - y), 2 * (1 - x) + (1 - y)
    to_x, to_y, sibling = (1 - x, y, c), (x, 1 - y, c), (x, y, 1 - c)
    if stage == "ici_near":
        moves = [(half(mine, c), to_x, half(x_chip, c)), (half(mine, c), to_y, half(y_chip, c))]
    elif stage == "ici_far":
        moves = [(quarter(x_chip, c, 0), to_y, quarter(far_chip, c, 0)),
                 (quarter(y_chip, c, 1), to_x, quarter(far_chip, c, 1))]
    elif stage == "d2d_near":
        moves = [(half(x_chip, c), sibling, half(x_chip, 1 - c)), (half(y_chip, c), sibling, half(y_chip, 1 - c))]
    else:
        moves = [(half(far_chip, c), sibling, half(far_chip, 1 - c))]
    sends = [_remote(src, src, send_sems, recv_sems, k0 + i, to) for i, (src, to, _) in enumerate(moves)]
    arrivals = [_remote(got, got, send_sems, recv_sems, k0 + i, (x, y, c)) for i, (_, _, got) in enumerate(moves)]
    return sends, arrivals


def _ring_call(name, groups, actions, after=None):
    tags = list(dict.fromkeys(t for _, t, _ in actions))
    counts = {t: len(groups[t]["bufs"]) for t in tags}
    first = {t: sum(counts[u] for u in tags[:i]) for i, t in enumerate(tags)}
    waits = [(t, s) for v, t, s in actions if v == "wait"]
    starts = [(t, s) for v, t, s in actions if v == "start"]

    def body(bufs, sems_in, sems_out):
        for verb, t, s in actions:
            at, sems = (starts.index((t, s)), sems_out) if verb == "start" else (waits.index((t, s)), sems_in)
            for w in range(counts[t]):
                sends, arrivals = _ring_copies(bufs[first[t] + w], sems[2 * at], sems[2 * at + 1], RING_STAGES[s] * w, s)
                if verb == "start":
                    for cp in sends:
                        cp.start()
                else:
                    for cp in arrivals:
                        cp.wait_recv()
                    for cp in sends:
                        cp.wait_send()

    bufs, sems, token = _comm_call(
        name, body, [b for t in tags for b in groups[t]["bufs"]],
        [sem for t, s in waits for sem in groups[t]["sems"][s]],
        [RING_STAGES[s] * counts[t] for t, s in starts for _ in (0, 1)], after, token=True)
    for t in tags:
        groups[t]["bufs"] = bufs[first[t]:first[t] + counts[t]]
    for t, s in waits:
        del groups[t]["sems"][s]
    for i, (t, s) in enumerate(starts):
        groups[t]["sems"][s] = (sems[2 * i], sems[2 * i + 1])
    return token


def _cx_copies(src, dst, send_sems, recv_sems, k0):
    x, y, c, chips = _position()
    sends = [_remote(src.at[2 * cx + cy], dst.at[2 * x + y], send_sems, recv_sems, k0 + j, (cx, cy, c))
             for j, (cx, cy) in enumerate(chips)]
    arrivals = [_remote(dst.at[2 * cx + cy], dst.at[2 * cx + cy], send_sems, recv_sems, k0 + j, (x, y, c))
                for j, (cx, cy) in enumerate(chips)]
    return sends, arrivals


def _cx_start(name, pair_sums):
    n = len(pair_sums)
    landing = [lax.empty(p.shape, p.dtype) for p in pair_sums]

    def body(bufs, _, sems):
        for w in range(n):
            for cp in _cx_copies(bufs[w], bufs[n + w], sems[0], sems[1], 3 * w)[0]:
                cp.start()

    bufs, sems, token = _comm_call(name, body, list(pair_sums) + landing, [], [3 * n, 3 * n], token=True)
    return (bufs, sems), token


def _cx_wait(name, state, after):
    bufs, sems = state
    n = len(bufs) // 2

    def body(refs, sems_in, _):
        for w in range(n):
            sends, arrivals = _cx_copies(refs[w], refs[n + w], sems_in[0], sems_in[1], 3 * w)
            for cp in arrivals:
                cp.wait_recv()
            for cp in sends:
                cp.wait_send()

    bufs, _, _ = _comm_call(name, body, bufs, sems, [], after)
    return bufs[:n], bufs[n:]


def _px_copies(src, dst, send_sems, recv_sems, k):
    x, y, c, _ = _position()
    hr = src.shape[1] // 2
    send = _remote(src.at[:, pl.ds((1 - c) * hr, hr), :], dst, send_sems, recv_sems, k, (x, y, 1 - c))
    arrival = _remote(dst, dst, send_sems, recv_sems, k, (x, y, c))
    return send, arrival


def _px_start(name, grads):
    n = len(grads)
    landing = [lax.empty((N_CHIPS, g.shape[1] // 2, g.shape[2]), g.dtype) for g in grads]

    def body(bufs, _, sems):
        for w in range(n):
            _px_copies(bufs[w], bufs[n + w], sems[0], sems[1], w)[0].start()

    bufs, sems, token = _comm_call(name, body, list(grads) + landing, [], [n, n], token=True)
    return (bufs, sems), token


def _px_wait(name, state, after):
    bufs, sems = state
    n = len(bufs) // 2

    def body(refs, sems_in, _):
        for w in range(n):
            send, arrival = _px_copies(refs[w], refs[n + w], sems_in[0], sems_in[1], w)
            arrival.wait_recv()
            send.wait_send()

    bufs, _, _ = _comm_call(name, body, bufs, sems, [], after)
    return bufs[:n], bufs[n:]


def _pair_sum(grad, got, name):
    _, rows, cols = grad.shape
    hr = rows // 2
    tm = min(hr, 512)
    nb = hr // tm
    c = lax.axis_index("c")

    def body(c_ref, g_ref, o_ref, out_ref):
        out_ref[...] = (g_ref[...].astype(F32) + o_ref[...].astype(F32)).astype(BF16)

    return pl.pallas_call(
        body, name=name,
        grid_spec=pltpu.PrefetchScalarGridSpec(
            num_scalar_prefetch=1, grid=(N_CHIPS, nb),
            in_specs=[pl.BlockSpec((None, tm, cols), lambda s, i, c_ref: (s, c_ref[0] * nb + i, 0)),
                      pl.BlockSpec((None, tm, cols), lambda s, i, c_ref: (s, i, 0))],
            out_specs=pl.BlockSpec((None, tm, cols), lambda s, i, c_ref: (s, i, 0))),
        out_shape=jax.ShapeDtypeStruct((N_CHIPS, hr, cols), BF16),
        compiler_params=_params(2),
    )(jnp.reshape(c, (1,)).astype(jnp.int32), grad, got)


def _chip_sum(parts, pair_sums, name):
    _, hr, cols = parts.shape
    tm = min(hr, 512)
    nb = hr // tm
    x, y, c = lax.axis_index("x"), lax.axis_index("y"), lax.axis_index("c")

    def body(pos_ref, p_ref, own_ref, o_ref):
        chip = pos_ref[0]
        own = own_ref[...].astype(F32)
        term = lambda s: jnp.where(chip == s, own, p_ref[s].astype(F32))
        o_ref[...] = ((term(0) + term(1)) + term(2)) + term(3)

    return pl.pallas_call(
        body, name=name,
        grid_spec=pltpu.PrefetchScalarGridSpec(
            num_scalar_prefetch=1, grid=(nb,),
            in_specs=[pl.BlockSpec((N_CHIPS, tm, cols), lambda i, pos: (0, i, 0)),
                      pl.BlockSpec((None, tm, cols), lambda i, pos: (pos[0], i, 0))],
            out_specs=pl.BlockSpec((tm, cols), lambda i, pos: (pos[1] * nb + i, 0))),
        out_shape=jax.ShapeDtypeStruct((2 * hr, cols), F32), compiler_params=_params(1),
    )(jnp.stack([2 * x + y, c]).astype(jnp.int32), parts, pair_sums)


def _share_copies(buf, send_sems, recv_sems, k):
    x, y, c, _ = _position()
    hr = buf.shape[0] // 2
    mine, theirs = buf.at[pl.ds(c * hr, hr), :], buf.at[pl.ds((1 - c) * hr, hr), :]
    return (_remote(mine, mine, send_sems, recv_sems, k, (x, y, 1 - c)),
            _remote(theirs, theirs, send_sems, recv_sems, k, (x, y, c)))


def _share_start(name, bufs):
    n = len(bufs)

    def body(refs, _, sems):
        for w in range(n):
            _share_copies(refs[w], sems[0], sems[1], w)[0].start()

    bufs, sems, token = _comm_call(name, body, list(bufs), [], [n, n], token=True)
    return (bufs, sems), token


def _share_wait(name, state, after):
    bufs, sems = state

    def body(refs, sems_in, _):
        for w in range(len(bufs)):
            send, arrival = _share_copies(refs[w], sems_in[0], sems_in[1], w)
            arrival.wait_recv()
            send.wait_send()

    return _comm_call(name, body, bufs, sems, [], after)[0]


def _allreduce_small(g):
    rows = g.shape[0]
    half = rows // 2

    def body(g_ref, o_ref, sib, slots, send_sems, recv_sems):
        x, y, c, chips = _position()
        me, sibling = (x, y, c), (x, y, 1 - c)
        my_chip = 2 * x + y
        mine = pl.ds(pl.multiple_of(c * half, 8), half)
        theirs = pl.ds(pl.multiple_of((1 - c) * half, 8), half)
        pair = _remote(g_ref.at[theirs], sib, send_sems, recv_sems, 0, sibling)
        pair.start()
        pair.wait()
        slots[my_chip] = g_ref[mine, :] + sib[...]
        sent = []
        for j, (cx, cy) in enumerate(chips):
            cp = _remote(slots.at[my_chip], slots.at[my_chip], send_sems, recv_sems, 1 + j, (cx, cy, c))
            cp.start()
            sent.append(cp)
        for j, (cx, cy) in enumerate(chips):
            got = slots.at[2 * cx + cy]
            _remote(got, got, send_sems, recv_sems, 1 + j, me).wait_recv()
        for cp in sent:
            cp.wait_send()
        o_ref[mine, :] = ((slots[0] + slots[1]) + slots[2]) + slots[3]
        swap = _remote(o_ref.at[mine], o_ref.at[mine], send_sems, recv_sems, 4, sibling)
        swap.start()
        swap.wait()

    vm = pl.BlockSpec(memory_space=pltpu.VMEM)
    return pl.pallas_call(
        body, name="allreduce_small",
        in_specs=[vm], out_specs=vm, out_shape=jax.ShapeDtypeStruct((rows, 128), F32),
        scratch_shapes=[pltpu.VMEM((half, 128), F32), pltpu.VMEM((N_CHIPS, half, 128), F32),
                        pltpu.SemaphoreType.DMA((5,)), pltpu.SemaphoreType.DMA((5,))],
        compiler_params=pltpu.CompilerParams(vmem_limit_bytes=VMEM_LIMIT),
    )(g)


_SMALL =("rel_bias", "ln_v_gain", "ln_v_bias", "w_spatial", "b_spatial", "ln1_gain", "ln1_bias",
          "b_ff1", "b_ff2", "ln2_gain", "ln2_bias")
_SMALL_ROWS = 1200
_LOSS_AT = (152832 // 128, 0)


def _pack_small(parts):
    flat = jnp.concatenate([parts[k].reshape(-1).astype(F32) for k in _SMALL])
    flat = jnp.pad(flat, (0, _SMALL_ROWS * 128 - flat.shape[0]))
    return flat.reshape(_SMALL_ROWS, 128)


def _unpack_small(packed, like):
    flat = packed.reshape(-1)
    out, at = {}, 0
    for k in _SMALL:
        n = math.prod(like[k].shape)
        out[k] = flat[at:at + n].reshape(like[k].shape)
        at += n
    return out


def kernel(x, w_in, rel_bias, ln_v_gain, ln_v_bias, w_spatial, b_spatial, w_proj_a, w_proj_b, w_out, ln1_gain, ln1_bias, w_ff1, b_ff1, w_ff2, b_ff2, ln2_gain, ln2_bias, loss_target, m_w_in, m_rel_bias, m_ln_v_gain, m_ln_v_bias, m_w_spatial, m_b_spatial, m_w_proj_a, m_w_proj_b, m_w_out, m_ln1_gain, m_ln1_bias, m_w_ff1, m_b_ff1, m_w_ff2, m_b_ff2, m_ln2_gain, m_ln2_bias, v_w_in, v_rel_bias, v_ln_v_gain, v_ln_v_bias, v_w_spatial, v_b_spatial, v_w_proj_a, v_w_proj_b, v_w_out, v_ln1_gain, v_ln1_bias, v_w_ff1, v_b_ff1, v_w_ff2, v_b_ff2, v_ln2_gain, v_ln2_bias):
    args = dict(locals())
    big = ("w_in", "w_proj_a", "w_proj_b", "w_out", "w_ff1", "w_ff2")
    weights = ("w_in", "rel_bias", "ln_v_gain", "ln_v_bias", "w_spatial", "b_spatial", "w_proj_a", "w_proj_b", "w_out",
               "ln1_gain", "ln1_bias", "w_ff1", "b_ff1", "w_ff2", "b_ff2", "ln2_gain", "ln2_bias")

    xs = x[0]
    target = loss_target[0]

    ring = {"a": {"bufs": [_place_shard(w_in[0], "place_w_in")], "sems": {}}}
    tok = _ring_call("allgather_a_near", ring, [("start", "a", "ici_near")])
    placed = [_place_shard(args[k][0], f"place_{k}", after=tok) for k in big[1:]]
    for tag, bufs in (("b", placed[0:3]), ("c", placed[3:4]), ("d", placed[4:5])):
        ring[tag] = {"bufs": bufs, "sems": {}}
    xb = _to_bf16(xs, "x_to_bf16", after=placed[4])

    mx, my = lax.axis_index("x"), lax.axis_index("y")
    own = jnp.reshape(2 * mx + my, (1,)).astype(jnp.int32)
    near = jnp.stack([2 * (1 - mx) + my, 2 * mx + (1 - my)]).astype(jnp.int32)
    far = jnp.reshape(2 * (1 - mx) + (1 - my), (1,)).astype(jnp.int32)
    proj = _proj(xb, ring["a"]["bufs"][0], own, "proj_own")
    _ring_call("allgather_a_far", ring, [("wait", "a", "ici_near"), ("start", "a", "ici_far"), ("start", "a", "d2d_near"),
                                         ("start", "b", "ici_near"), ("start", "c", "ici_near")], after=proj)
    _ring_call("allgather_a_near_done", ring, [("wait", "a", "d2d_near")])
    proj = _proj(xb, ring["a"]["bufs"][0], near, "proj_near", into=proj)
    _ring_call("allgather_a_last", ring, [("wait", "a", "ici_far"), ("start", "a", "d2d_far")], after=proj)
    _ring_call("allgather_a_done", ring, [("wait", "a", "d2d_far")])
    (win_g,) = ring["a"]["bufs"]
    proj = _proj(xb, win_g, far, "proj_far", into=proj)
    _ring_call("allgather_b_far", ring, [("wait", "b", "ici_near"), ("start", "b", "ici_far"), ("start", "b", "d2d_near")],
               after=proj)
    ws = w_spatial[0]
    ws_t = jnp.transpose(ws, (0, 2, 1))
    bsp_b = jnp.broadcast_to(b_spatial[0][:, :, None], (NH, 128, 128))
    gmlp = _gmlp_fwd(proj, ws, bsp_b, ln_v_gain, ln_v_bias)
    attn, lse = _attention_fwd(proj, rel_bias)
    _ring_call("allgather_b_last_c_far", ring,
               [("wait", "b", "ici_far"), ("start", "b", "d2d_far"),
                ("wait", "c", "ici_near"), ("start", "c", "ici_far"), ("start", "c", "d2d_near"),
                ("start", "d", "ici_near")], after=attn)
    _ring_call("allgather_b_done", ring, [("wait", "b", "d2d_near"), ("wait", "b", "d2d_far")])
    wpa_g, wpb_g, wout_g = ring["b"]["bufs"]
    wout_full = wout_g.reshape(D, D)
    ya, yb, merged = _branch(attn, gmlp, wpa_g, wpb_g, proj)
    xhat1, rstd1, h1b = _out_ln1(merged, wout_full, xs, ln1_gain, ln1_bias)
    _ring_call("allgather_c_last_d_far", ring,
               [("wait", "c", "ici_far"), ("start", "c", "d2d_far"),
                ("wait", "d", "ici_near"), ("start", "d", "ici_far"), ("start", "d", "d2d_near")], after=h1b)
    _ring_call("allgather_c_done", ring, [("wait", "c", "d2d_near"), ("wait", "c", "d2d_far")])
    (w1_g,) = ring["c"]["bufs"]
    a, r = _ff1(h1b, w1_g, b_ff1)
    _ring_call("allgather_d_last", ring, [("wait", "d", "ici_far"), ("start", "d", "d2d_far")], after=a)
    _ring_call("allgather_d_done", ring, [("wait", "d", "d2d_near"), ("wait", "d", "d2d_far")])
    (w2_g,) = ring["d"]["bufs"]
    w2_full = w2_g.reshape(DFF, D)
    dpre2, dpre2b, st2 = _ff2_ln2_loss(a, w2_full, xhat1, ln1_gain, ln1_bias, b_ff2, ln2_gain, ln2_bias, target)

    def pair_and_chip(tag, state, after):
        local, from_sibling = _px_wait(f"pair_exchange_wait_{tag}", state, after)
        pair_sums = [_pair_sum(g, o, f"pair_sum_{tag}_{i}") for i, (g, o) in enumerate(zip(local, from_sibling))]
        return _cx_start(f"chip_exchange_start_{tag}", pair_sums)

    g_w2 = _grad_w(a, dpre2b, "grad_w_ff2", 512, 2048, False)
    px, tok = _px_start("pair_exchange_start_w_ff2", [g_w2.reshape(N_CHIPS, DFF // N_CHIPS, D)])
    dprea, g_b1 = _d_ff1(dpre2b, w2_full, r, after=tok)
    cx_w2, tok = pair_and_chip("w_ff2", px, dprea)
    g_w1 = _grad_w(h1b, dprea, "grad_w_ff1", 512, 2048, True, after=tok)
    px, tok = _px_start("pair_exchange_start_w_ff1", [g_w1])
    dpre1, dpre1b, st1 = _d_h1_ln1(dprea, w1_g, dpre2, xhat1, rstd1, ln1_gain, after=tok)
    cx_w1, tok = pair_and_chip("w_ff1", px, dpre1b)
    g_wout = _grad_w(merged, dpre1b, "grad_w_out", 512, 2048, False, after=tok)
    dya, dyb, dga, dgb = _d_merged(dpre1b, wout_full, proj, ya, yb)
    g_wpa = _grad_w(attn, dya, "grad_w_proj_a", 1024, 512, True)
    g_wpb = _grad_w(gmlp, dyb, "grad_w_proj_b", 1024, 512, True)
    px, tok = _px_start("pair_exchange_start_b", [g_wpa, g_wpb, g_wout.reshape(N_CHIPS, D // N_CHIPS, D)])
    dattn, dgmlp = _d_branches(dya, dyb, wpa_g, wpb_g, after=tok)
    duv, g_ws, g_bs, stv = _gmlp_bwd(proj, dgmlp, ws, ws_t, bsp_b, ln_v_gain, ln_v_bias)
    cx_b, tok = pair_and_chip("b", px, duv)
    dq, dk, dv, ds_sums = _attention_bwd(proj, dattn, attn, lse, rel_bias, after=tok)
    g_rb = _rel_bias_grad(ds_sums)[:, :NH]

    small_g = dict(rel_bias=g_rb, ln_v_gain=stv[0], ln_v_bias=stv[1], w_spatial=g_ws, b_spatial=g_bs[:, :, 0],
                   ln1_gain=st1[0], ln1_bias=st1[1], b_ff1=g_b1, b_ff2=st2[2], ln2_gain=st2[0], ln2_bias=st2[1])
    gs = _allreduce_small(_pack_small(small_g).at[_LOSS_AT].set(st2[3, 0]))
    ds_, ms_, vs_, _ = _adamw(_pack_small({k: args[k] for k in _SMALL}), gs,
                           _pack_small({k: args["m_" + k] for k in _SMALL}),
                           _pack_small({k: args["v_" + k] for k in _SMALL}), "adamw_small")
    like = {k: args[k] for k in _SMALL}
    grads, deltas, new_m, new_v = (_unpack_small(t, like) for t in (gs, ds_, ms_, vs_))

    dproj = jnp.concatenate([dq, dk, dv, duv, dga, dgb], axis=1)
    g_win = _grad_w(xb, dproj, "grad_w_in", 512, 2304, True, after=gs)
    px, tok = _px_start("pair_exchange_start_w_in", [g_win])

    def chip_sums(tag, state, names, after):
        pair_sums, from_chips = _cx_wait(f"chip_exchange_wait_{tag}", state, after)
        halves = [_chip_sum(p, own, f"chip_sum_{k}") for p, own, k in zip(from_chips, pair_sums, names)]
        return _share_start(f"share_start_{tag}", halves)

    def adam(tag, state, names, after):
        last = None
        for k, g in zip(names, _share_wait(f"share_wait_{tag}", state, after)):
            d_, m_, v_, g_ = _adamw(args[k][0], g, args["m_" + k][0], args["v_" + k][0], f"adamw_{k}")
            grads[k], deltas[k], new_m[k], new_v[k] = g_[None], d_[None], m_[None], v_[None]
            last = d_
        return last

    sh_w2, tok = chip_sums("w_ff2", cx_w2, ["w_ff2"], tok)
    sh_w1, tok = chip_sums("w_ff1", cx_w1, ["w_ff1"], tok)
    sh_b, tok = chip_sums("b", cx_b, ["w_proj_a", "w_proj_b", "w_out"], tok)
    cx_in, tok = pair_and_chip("w_in", px, tok)
    grad_x = _d_x(dproj, win_g, dpre1, after=tok)
    done = adam("w_ff2", sh_w2, ["w_ff2"], grad_x)
    done = adam("w_ff1", sh_w1, ["w_ff1"], done)
    done = adam("b", sh_b, ["w_proj_a", "w_proj_b", "w_out"], done)
    sh_in, tok = chip_sums("w_in", cx_in, ["w_in"], done)
    adam("w_in", sh_in, ["w_in"], tok)

    loss = gs[_LOSS_AT] * (0.5 / D)
    return (loss, grad_x[None], *[grads[k] for k in weights], *[deltas[k] for k in weights],
            *[new_m[k] for k in weights], *[new_v[k] for k in weights])
```

```python
import math

import numpy as np
import jax
import jax.numpy as jnp
from jax import lax
from jax.experimental import pallas as pl
from jax.experimental.pallas import tpu as pltpu

F32 = jnp.float32
BF16 = jnp.bfloat16

S = 2048
D = 2048
DA = 1024
DB = 1024
DFF = 8192
DIN = 9216
NH = 8
HD = 128
NBLK = 16
PATTERNS = ((128, 1), (512, 4), (2048, 16))
N_BUCKETS = 32
MAX_DISTANCE = 2048
ALPHA = 2.0 ** 0.25
LN_EPS = 1e-5
NEG_INF = -1e30
SCALE = HD ** -0.5
N_CHIPS = 4

ADAM_LR = 0.001
ADAM_B1 = 0.9
ADAM_B2 = 0.999
ADAM_EPS = 1e-08
ADAM_WD = 0.01
ADAM_STEP = 10

VMEM_LIMIT = 56 * 1024 * 1024
MESH = pl.DeviceIdType.MESH
ANY = pl.BlockSpec(memory_space=pl.ANY)


def _params(n_axes, vmem=VMEM_LIMIT):
    return pltpu.CompilerParams(dimension_semantics=("arbitrary",) * n_axes, vmem_limit_bytes=vmem)


def _bucket_tile(dilation):
    qi = np.arange(128)[:, None]
    kj = np.arange(256)[None, :]
    n = np.clip(128 + qi - kj, 0, 128) * dilation
    max_exact = N_BUCKETS // 2
    nf = np.maximum(n, 1).astype(np.float32)
    large = max_exact + (np.log(nf / np.float32(max_exact)) / np.float32(math.log(MAX_DISTANCE / max_exact))
                         * np.float32(N_BUCKETS - max_exact)).astype(np.int32)
    large = np.minimum(large, N_BUCKETS - 1)
    return np.where(n < max_exact, n, large).astype(np.int32)


def _gelu(x):
    c = math.sqrt(2.0 / math.pi)
    t = jnp.tanh(c * (x + 0.044715 * x * x * x))
    return 0.5 * x * (1.0 + t), t


def _gelu_grad(x, t):
    c = math.sqrt(2.0 / math.pi)
    return 0.5 * (1.0 + t) + 0.5 * x * (1.0 - t * t) * c * (1.0 + 3.0 * 0.044715 * x * x)


def _sigmoid(x):
    return 1.0 / (1.0 + jnp.exp(-x))


def _dot(a, b):
    return jnp.dot(a, b, preferred_element_type=F32)


def _behind(body, n_in, after):
    if after is None:
        return body, [], []
    return (lambda *refs: body(*refs[:n_in], *refs[n_in + 1:])), [ANY], [after]


def _dot_nt(a, b):
    return lax.dot_general(a, b, (((1,), (1,)), ((), ())), preferred_element_type=F32)


def _proj(xb, win_g, shards, name, into=None):
    tn = 768
    per = 2304 // tn

    def body(shards_ref, x_ref, w_ref, *rest):
        rest[-1][...] = _dot(x_ref[...], w_ref[...])

    in_specs = [pl.BlockSpec((S, D), lambda j, sh: (0, 0)),
                pl.BlockSpec((None, D, tn), lambda j, sh: (sh[j // per], 0, j % per))]
    return pl.pallas_call(
        body, name=name,
        grid_spec=pltpu.PrefetchScalarGridSpec(
            num_scalar_prefetch=1, grid=(shards.shape[0] * per,),
            in_specs=in_specs + ([ANY] if into is not None else []),
            out_specs=pl.BlockSpec((S, tn), lambda j, sh: (0, sh[j // per] * per + j % per))),
        out_shape=jax.ShapeDtypeStruct((S, DIN), F32),
        input_output_aliases={3: 0} if into is not None else {},
        compiler_params=_params(1),
    )(shards, xb, win_g, *([into] if into is not None else []))


FWD_HEADS_PER_STEP = 4
BWD_HEADS_PER_STEP = 2


def _head_bias_tiles(rb_ref, bk_ref, bias_scr, first_head, hps):
    qi = lax.broadcasted_iota(jnp.int32, (128, 256), 0)
    kj = lax.broadcasted_iota(jnp.int32, (128, 256), 1)
    steps = 128 + qi - kj
    band = (steps >= 0) & (steps <= 128)
    bias_scr[...] = jnp.zeros_like(bias_scr)
    for p in range(len(PATTERNS)):
        bucket = bk_ref[p]

        def one_bucket(t, carry):
            hit = bucket == t
            for j in range(hps):
                bias_scr[p, j] = jnp.where(hit, rb_ref[t, first_head + j], bias_scr[p, j])
            return carry

        lax.fori_loop(0, N_BUCKETS, one_bucket, 0)
        for j in range(hps):
            bias_scr[p, j] = jnp.where(band, bias_scr[p, j], NEG_INF)


def _block_rows(b, dilation):
    nblk = NBLK // dilation
    r, n = b // nblk, b % nblk
    start = r + n * (128 * dilation)
    prev_start = jnp.maximum(start - 128 * dilation, r)
    if dilation == 1:
        return pl.ds(pl.multiple_of(start, 128), 128), pl.ds(pl.multiple_of(prev_start, 128), 128), n > 0
    return pl.ds(start, 128, stride=dilation), pl.ds(prev_start, 128, stride=dilation), n > 0


def _head_specs(first, hps):
    return [pl.BlockSpec((S, HD), lambda g, j=j: (0, first + g * hps + j)) for j in range(hps)]


def _heads_spec(hps):
    return pl.BlockSpec((S, hps * HD), lambda g: (0, g))


def _attention_fwd(proj, rel_bias):
    hps = FWD_HEADS_PER_STEP
    buckets = jnp.asarray(np.stack([_bucket_tile(d) for _, d in PATTERNS]))

    def body(rb_ref, bk_ref, *refs):
        q_refs, k_refs, v_refs = (refs[i * hps:(i + 1) * hps] for i in range(3))
        o_ref, lse_ref, bias_scr = refs[3 * hps:3 * hps + 3]
        acc_scrs, m_scrs, l_scrs = (refs[3 * hps + 3 + i * hps:3 * hps + 3 + (i + 1) * hps] for i in range(3))
        _head_bias_tiles(rb_ref, bk_ref, bias_scr, pl.program_id(0) * hps, hps)
        kj = lax.broadcasted_iota(jnp.int32, (128, 256), 1)
        for p, (_, d) in enumerate(PATTERNS):
            prev_blocks = NBLK // d > 1

            def block(b, carry):
                units = [(j,) + _block_rows(b + i * (NBLK // 4), d) for i in range(4) for j in range(hps)]
                scores = []
                for j, rows, prows, _ in units:
                    q = q_refs[j][rows, :].astype(BF16)
                    cur = _dot_nt(q, k_refs[j][rows, :].astype(BF16))
                    if prev_blocks:
                        cur = jnp.concatenate([_dot_nt(q, k_refs[j][prows, :].astype(BF16)), cur], axis=1)
                    scores.append(cur)
                soft = []
                for u, (j, _, _, has_prev) in enumerate(units):
                    if prev_blocks:
                        s = jnp.where((kj >= 128) | has_prev, scores[u] * SCALE + bias_scr[p, j], NEG_INF)
                    else:
                        s = scores[u] * SCALE + bias_scr[p, j, :, 128:256]
                    m = jnp.max(s, axis=1, keepdims=True)
                    e = jnp.exp(s - m)
                    soft.append((m, jnp.sum(e, axis=1, keepdims=True), e.astype(BF16)))
                outs = []
                for u, (j, rows, prows, _) in enumerate(units):
                    e = soft[u][2]
                    if prev_blocks:
                        outs.append(_dot(e[:, :128], v_refs[j][prows, :].astype(BF16))
                                    + _dot(e[:, 128:], v_refs[j][rows, :].astype(BF16)))
                    else:
                        outs.append(_dot(e, v_refs[j][rows, :].astype(BF16)))
                for u, (j, rows, _, _) in enumerate(units):
                    acc_scr, m_scr, l_scr = acc_scrs[j], m_scrs[j], l_scrs[j]
                    (m, den, _), o = soft[u], outs[u]
                    if p == 0:
                        acc_scr[rows, :] = o
                        m_scr[rows, :] = jnp.broadcast_to(m, (128, HD))
                        l_scr[rows, :] = jnp.broadcast_to(den, (128, HD))
                    else:
                        m_old = m_scr[rows, :]
                        m_new = jnp.maximum(m_old, m)
                        w_old, w_new = jnp.exp(m_old - m_new), jnp.exp(m - m_new)
                        acc_scr[rows, :] = acc_scr[rows, :] * w_old + o * w_new
                        l_scr[rows, :] = l_scr[rows, :] * w_old + den * w_new
                        m_scr[rows, :] = m_new
                return carry

            lax.fori_loop(0, NBLK // 4, block, 0)
        for j in range(hps):
            cols = slice(j * HD, (j + 1) * HD)
            den = l_scrs[j][...]
            o_ref[:, cols] = (acc_scrs[j][...] / den).astype(BF16)
            lse_ref[:, cols] = m_scrs[j][...] + jnp.log(den)

    return pl.pallas_call(
        body, name="attention_fwd", grid=(NH // hps,),
        in_specs=[pl.BlockSpec(memory_space=pltpu.SMEM), pl.BlockSpec((3, 128, 256), lambda g: (0, 0, 0))]
        + _head_specs(0, hps) + _head_specs(NH, hps) + _head_specs(2 * NH, hps),
        out_specs=[_heads_spec(hps), _heads_spec(hps)],
        out_shape=[jax.ShapeDtypeStruct((S, DA), BF16), jax.ShapeDtypeStruct((S, DA), F32)],
        scratch_shapes=[pltpu.VMEM((3, hps, 128, 256), F32)] + [pltpu.VMEM((S, HD), F32)] * (3 * hps),
        compiler_params=_params(1),
    )(rel_bias, buckets, *([proj] * (3 * hps)))


def _attention_bwd(proj, dattn, attn, lse, rel_bias, after=None):
    hps = BWD_HEADS_PER_STEP

    def body(rb_ref, bk_ref, *refs):
        q_refs, k_refs, v_refs, do_refs, o_refs, lse_refs = (refs[i * hps:(i + 1) * hps] for i in range(6))
        dq_ref, dk_ref, dv_ref, ds_ref, bias_scr = refs[6 * hps:6 * hps + 5]
        dl_scrs, dq_scrs, dk_scrs, dv_scrs = (refs[6 * hps + 5 + i * hps:6 * hps + 5 + (i + 1) * hps] for i in range(4))
        _head_bias_tiles(rb_ref, bk_ref, bias_scr, pl.program_id(0) * hps, hps)
        ds_ref[...] = jnp.zeros_like(ds_ref)
        for j in range(hps):
            dq_scrs[j][...] = jnp.zeros((S, HD), F32)
            dk_scrs[j][...] = jnp.zeros((S, HD), F32)
            dv_scrs[j][...] = jnp.zeros((S, HD), F32)
            prod = do_refs[j][...] * o_refs[j][...].astype(F32)
            dl_scrs[j][...] = jnp.broadcast_to(jnp.sum(prod, axis=1, keepdims=True), (S, HD))
        for p, (_, d) in enumerate(PATTERNS):
            prev_blocks = NBLK // d > 1

            def block(b, carry):
                units = [(j,) + _block_rows(b + i * (NBLK // 4), d) for i in range(4) for j in range(hps)]
                ops, raw = [], []
                for j, rows, prows, _ in units:
                    q, do = q_refs[j][rows, :].astype(BF16), do_refs[j][rows, :].astype(BF16)
                    kc, vc = k_refs[j][rows, :].astype(BF16), v_refs[j][rows, :].astype(BF16)
                    if prev_blocks:
                        kp, vp = k_refs[j][prows, :].astype(BF16), v_refs[j][prows, :].astype(BF16)
                        ops.append((q, do, kc, kp))
                        raw.append((_dot_nt(q, kc), _dot_nt(do, vc), _dot_nt(q, kp), _dot_nt(do, vp)))
                    else:
                        ops.append((q, do, kc))
                        raw.append((_dot_nt(q, kc), _dot_nt(do, vc)))
                probs = []
                for u, (j, rows, _, has_prev) in enumerate(units):
                    lse_b, dl_b = lse_refs[j][rows, :], dl_scrs[j][rows, :]
                    p_c = jnp.exp(raw[u][0] * SCALE + bias_scr[p, j, :, 128:256] - lse_b)
                    ds_c = p_c * (raw[u][1] - dl_b)
                    ds_ref[p, j, :, 128:256] += ds_c
                    if prev_blocks:
                        p_p = jnp.where(has_prev, jnp.exp(raw[u][2] * SCALE + bias_scr[p, j, :, 0:128] - lse_b), 0.0)
                        ds_p = p_p * (raw[u][3] - dl_b)
                        ds_ref[p, j, :, 0:128] += ds_p
                        probs.append((p_c, ds_c, p_p, ds_p))
                    else:
                        probs.append((p_c, ds_c))
                grads = []
                for u in range(len(units)):
                    q, do, kc = ops[u][:3]
                    p_c, ds_c = probs[u][:2]
                    dq = _dot(ds_c.astype(BF16), kc)
                    cur = (_dot(ds_c.T.astype(BF16), q) * SCALE, _dot(p_c.T.astype(BF16), do))
                    if prev_blocks:
                        p_p, ds_p = probs[u][2:]
                        dq = dq + _dot(ds_p.astype(BF16), ops[u][3])
                        cur = cur + (_dot(ds_p.T.astype(BF16), q) * SCALE, _dot(p_p.T.astype(BF16), do))
                    grads.append((dq * SCALE,) + cur)
                for u, (j, rows, prows, _) in enumerate(units):
                    dq_scrs[j][rows, :] += grads[u][0]
                    dk_scrs[j][rows, :] += grads[u][1]
                    dv_scrs[j][rows, :] += grads[u][2]
                    if prev_blocks:
                        dk_scrs[j][prows, :] += grads[u][3]
                        dv_scrs[j][prows, :] += grads[u][4]
                return carry

            lax.fori_loop(0, NBLK // 4, block, 0)
        for j in range(hps):
            cols = slice(j * HD, (j + 1) * HD)
            dq_ref[:, cols] = dq_scrs[j][...].astype(BF16)
            dk_ref[:, cols] = dk_scrs[j][...].astype(BF16)
            dv_ref[:, cols] = dv_scrs[j][...].astype(BF16)

    buckets = jnp.asarray(np.stack([_bucket_tile(d) for _, d in PATTERNS]))
    body, more_specs, more = _behind(body, 2 + 6 * hps, after)
    return pl.pallas_call(
        body, name="attention_bwd", grid=(NH // hps,),
        in_specs=[pl.BlockSpec(memory_space=pltpu.SMEM), pl.BlockSpec((3, 128, 256), lambda g: (0, 0, 0))]
        + _head_specs(0, hps) + _head_specs(NH, hps) + _head_specs(2 * NH, hps) + 3 * _head_specs(0, hps)
        + more_specs,
        out_specs=3 * [_heads_spec(hps)] + [pl.BlockSpec((3, hps, 128, 256), lambda g: (0, g, 0, 0))],
        out_shape=[jax.ShapeDtypeStruct((S, DA), BF16)] * 3 + [jax.ShapeDtypeStruct((3, NH, 128, 256), F32)],
        scratch_shapes=[pltpu.VMEM((3, hps, 128, 256), F32)] + [pltpu.VMEM((S, HD), F32)] * (4 * hps),
        compiler_params=_params(1),
    )(rel_bias, buckets, *([proj] * (3 * hps)), *([dattn] * hps), *([attn] * hps), *([lse] * hps), *more)


def _gmlp_parts(u_ref, vb_ref, g_ref, be_ref):
    u = u_ref[...]
    u_act, tu = _gelu(u)
    vb = vb_ref[...]
    gv, tv = _gelu(vb)
    mean = jnp.mean(gv, axis=1, keepdims=True)
    cen = gv - mean
    var = jnp.mean(cen * cen, axis=1, keepdims=True)
    rstd = lax.rsqrt(var + LN_EPS)
    xhat = cen * rstd
    vn = xhat * g_ref[...] + be_ref[...]
    return u, tu, u_act, vb, tv, rstd, xhat, vn


def _gmlp_fwd(proj, ws, bsp_b, gain_v, bias_v):
    def body(u_ref, vb_ref, ws_ref, bsp_ref, g_ref, be_ref, o_ref):
        _, _, u_act, _, _, _, _, vn = _gmlp_parts(u_ref, vb_ref, g_ref, be_ref)
        row = lax.broadcasted_iota(jnp.int32, (128, 128), 0)
        col = lax.broadcasted_iota(jnp.int32, (128, 128), 1)
        causal = row >= col
        for g in range(NH):
            cols = slice(g * 128, (g + 1) * 128)
            wsg = jnp.where(causal, ws_ref[g], 0.0).astype(BF16)
            z = _dot(wsg, vn[:, cols].astype(BF16)) + bsp_ref[g]
            o_ref[:, cols] = (u_act[:, cols] * z).astype(BF16)

    return pl.pallas_call(
        body, name="gmlp_fwd", grid=(NBLK,),
        in_specs=[pl.BlockSpec((128, DB), lambda c: (c, 3)), pl.BlockSpec((128, DB), lambda c: (c, 4)),
                  pl.BlockSpec((NH, 128, 128), lambda c: (0, 0, 0)), pl.BlockSpec((NH, 128, 128), lambda c: (0, 0, 0)),
                  pl.BlockSpec((1, DB), lambda c: (0, 0)), pl.BlockSpec((1, DB), lambda c: (0, 0))],
        out_specs=pl.BlockSpec((128, DB), lambda c: (c, 0)),
        out_shape=jax.ShapeDtypeStruct((S, DB), BF16),
        compiler_params=_params(1),
    )(proj, proj, ws, bsp_b, gain_v, bias_v)


def _branch(attn, gmlp, wpa_g, wpb_g, proj):
    tn = 512

    def body(a_ref, g_ref, wa_ref, wb_ref, ga_ref, gb_ref, ya_ref, yb_ref, mg_ref):
        ya = _dot(a_ref[...], wa_ref[...])
        yb = _dot(g_ref[...], wb_ref[...])
        ya_ref[...] = ya.astype(BF16)
        yb_ref[...] = yb.astype(BF16)
        mg_ref[...] = (_sigmoid(ga_ref[...]) * ya + _sigmoid(gb_ref[...]) * yb).astype(BF16)

    out = pl.BlockSpec((S, tn), lambda j: (0, j))
    return pl.pallas_call(
        body, name="branch", grid=(D // tn,),
        in_specs=[pl.BlockSpec((S, DA), lambda j: (0, 0)), pl.BlockSpec((S, DB), lambda j: (0, 0)),
                  pl.BlockSpec((None, DA, tn), lambda j: (j, 0, 0)), pl.BlockSpec((None, DB, tn), lambda j: (j, 0, 0)),
                  pl.BlockSpec((S, tn), lambda j: (0, 5120 // tn + j)), pl.BlockSpec((S, tn), lambda j: (0, 7168 // tn + j))],
        out_specs=[out, out, out],
        out_shape=[jax.ShapeDtypeStruct((S, D), BF16)] * 3,
        compiler_params=_params(1),
    )(attn, gmlp, wpa_g, wpb_g, proj, proj)


def _out_ln1(merged, wout_g, x, gain, bias):
    tm = 256

    def body(m_ref, w_ref, x_ref, g_ref, b_ref, xh_ref, rs_ref, h_ref):
        pre = ALPHA * x_ref[...] + _dot(m_ref[...], w_ref[...])
        mean = jnp.mean(pre, axis=1, keepdims=True)
        cen = pre - mean
        var = jnp.mean(cen * cen, axis=1, keepdims=True)
        rstd = lax.rsqrt(var + LN_EPS)
        xhat = cen * rstd
        xh_ref[...] = xhat
        rs_ref[...] = jnp.broadcast_to(rstd, (tm, 128))
        h_ref[...] = (xhat * g_ref[...] + b_ref[...]).astype(BF16)

    row = pl.BlockSpec((tm, D), lambda i: (i, 0))
    vec = pl.BlockSpec((1, D), lambda i: (0, 0))
    return pl.pallas_call(
        body, name="out_ln1", grid=(S // tm,),
        in_specs=[row, pl.BlockSpec((D, D), lambda i: (0, 0)), row, vec, vec],
        out_specs=[row, pl.BlockSpec((tm, 128), lambda i: (i, 0)), row],
        out_shape=[jax.ShapeDtypeStruct((S, D), F32), jax.ShapeDtypeStruct((S, 128), F32),
                   jax.ShapeDtypeStruct((S, D), BF16)],
        compiler_params=_params(1),
    )(merged, wout_g, x, gain, bias)


def _ff1(h1b, w1_g, b1):
    tn = 512
    per = D // tn

    def body(h_ref, w_ref, b_ref, a_ref, r_ref):
        r = jnp.maximum(_dot(h_ref[...], w_ref[...]) + b_ref[...], 0.0)
        r_ref[...] = r.astype(BF16)
        a_ref[...] = (r * r).astype(BF16)

    out = pl.BlockSpec((S, tn), lambda j: (0, j))
    return pl.pallas_call(
        body, name="ff1", grid=(DFF // tn,),
        in_specs=[pl.BlockSpec((S, D), lambda j: (0, 0)),
                  pl.BlockSpec((None, D, tn), lambda j: (j // per, 0, j % per)),
                  pl.BlockSpec((1, tn), lambda j: (0, j))],
        out_specs=[out, out],
        out_shape=[jax.ShapeDtypeStruct((S, DFF), BF16)] * 2,
        compiler_params=_params(1),
    )(h1b, w1_g, b1)


def _ff2_ln2_loss(a, w2_g, xhat1, g1, b1, b2, g2, be2, target):
    tm, tk = 512, 1024
    nk = DFF // tk

    def body(a_ref, w_ref, xh_ref, g1_ref, b1_ref, b2_ref, g2_ref, be2_ref, t_ref, d_ref, db_ref, st_ref, acc):
        i, k = pl.program_id(0), pl.program_id(1)

        @pl.when(k == 0)
        def _():
            acc[...] = jnp.zeros_like(acc)

        @pl.when((i == 0) & (k == 0))
        def _():
            st_ref[...] = jnp.zeros_like(st_ref)

        acc[...] += _dot(a_ref[...], w_ref[...])

        @pl.when(k == nk - 1)
        def _():
            def rows_chunk(ci, carry):
                rows = pl.ds(pl.multiple_of(ci * 128, 128), 128)
                h1 = xh_ref[rows, :] * g1_ref[...] + b1_ref[...]
                pre = ALPHA * h1 + acc[rows, :] + b2_ref[...]
                mean = jnp.mean(pre, axis=1, keepdims=True)
                cen = pre - mean
                var = jnp.mean(cen * cen, axis=1, keepdims=True)
                rstd = lax.rsqrt(var + LN_EPS)
                xhat = cen * rstd
                y = xhat * g2_ref[...] + be2_ref[...]
                err = y - t_ref[rows, :]
                dy = err * (1.0 / D)
                g = dy * g2_ref[...]
                dpre = rstd * (g - jnp.mean(g, axis=1, keepdims=True)
                               - xhat * jnp.mean(g * xhat, axis=1, keepdims=True))
                d_ref[rows, :] = dpre
                db_ref[rows, :] = dpre.astype(BF16)
                st_ref[0:1, :] += jnp.sum(dy * xhat, axis=0, keepdims=True)
                st_ref[1:2, :] += jnp.sum(dy, axis=0, keepdims=True)
                st_ref[2:3, :] += jnp.sum(dpre, axis=0, keepdims=True)
                st_ref[3:4, :] += jnp.broadcast_to(jnp.sum(err * err).reshape(1, 1), (1, D))
                return carry

            lax.fori_loop(0, tm // 128, rows_chunk, 0)

    row = pl.BlockSpec((tm, D), lambda i, k: (i, 0))
    vec = pl.BlockSpec((1, D), lambda i, k: (0, 0))
    return pl.pallas_call(
        body, name="ff2_ln2_loss", grid=(S // tm, nk),
        in_specs=[pl.BlockSpec((tm, tk), lambda i, k: (i, k)), pl.BlockSpec((tk, D), lambda i, k: (k, 0)),
                  row, vec, vec, vec, vec, vec, row],
        out_specs=[row, row, pl.BlockSpec((8, D), lambda i, k: (0, 0))],
        out_shape=[jax.ShapeDtypeStruct((S, D), F32), jax.ShapeDtypeStruct((S, D), BF16),
                   jax.ShapeDtypeStruct((8, D), F32)],
        scratch_shapes=[pltpu.VMEM((tm, D), F32)],
        compiler_params=_params(2),
    )(a, w2_g, xhat1, g1, b1, b2, g2, be2, target)


def _grad_w(act, dout, name, ti, tj, sharded, after=None):
    m, n = act.shape[1], dout.shape[1]
    ns = n // N_CHIPS
    per = ns // tj if sharded else None

    def body(a_ref, b_ref, o_ref, at_scr):
        @pl.when(pl.program_id(1) == 0)
        def _():
            at_scr[...] = a_ref[...].T

        o_ref[...] = _dot(at_scr[...], b_ref[...]).astype(BF16)

    if sharded:
        out_spec = pl.BlockSpec((None, ti, tj), lambda i, j: (j // per, i, j % per))
        out_shape = jax.ShapeDtypeStruct((N_CHIPS, m, ns), BF16)
    else:
        out_spec = pl.BlockSpec((ti, tj), lambda i, j: (i, j))
        out_shape = jax.ShapeDtypeStruct((m, n), BF16)
    body, more_specs, more = _behind(body, 2, after)
    return pl.pallas_call(
        body, name=name, grid=(m // ti, n // tj),
        in_specs=[pl.BlockSpec((S, ti), lambda i, j: (0, i)), pl.BlockSpec((S, tj), lambda i, j: (0, j))] + more_specs,
        out_specs=out_spec, out_shape=out_shape,
        scratch_shapes=[pltpu.VMEM((ti, S), BF16)],
        compiler_params=_params(2),
    )(act, dout, *more)


def _d_ff1(dpre2b, w2_g, r, after=None):
    tn = 512

    def body(d_ref, w_ref, r_ref, o_ref, gb_ref):
        da = _dot_nt(d_ref[...], w_ref[...])
        dp = da * (2.0 * r_ref[...].astype(F32))
        o_ref[...] = dp.astype(BF16)
        gb_ref[...] = jnp.sum(dp, axis=0, keepdims=True)

    body, more_specs, more = _behind(body, 3, after)
    return pl.pallas_call(
        body, name="d_ff1", grid=(DFF // tn,),
        in_specs=[pl.BlockSpec((S, D), lambda j: (0, 0)), pl.BlockSpec((tn, D), lambda j: (j, 0)),
                  pl.BlockSpec((S, tn), lambda j: (0, j))] + more_specs,
        out_specs=[pl.BlockSpec((S, tn), lambda j: (0, j)), pl.BlockSpec((1, tn), lambda j: (0, j))],
        out_shape=[jax.ShapeDtypeStruct((S, DFF), BF16), jax.ShapeDtypeStruct((1, DFF), F32)],
        compiler_params=_params(1),
    )(dpre2b, w2_g, r, *more)


def _d_h1_ln1(dprea, w1_g, dpre2, xhat1, rstd1, g1, after=None):
    tm, tk = 512, 1024
    per = D // tk
    nk = DFF // tk

    def body(a_ref, w_ref, d2_ref, xh_ref, rs_ref, g_ref, d_ref, db_ref, st_ref, acc):
        i, k = pl.program_id(0), pl.program_id(1)

        @pl.when(k == 0)
        def _():
            acc[...] = jnp.zeros_like(acc)

        @pl.when((i == 0) & (k == 0))
        def _():
            st_ref[...] = jnp.zeros_like(st_ref)

        acc[...] += _dot_nt(a_ref[...], w_ref[...])

        @pl.when(k == nk - 1)
        def _():
            def rows_chunk(ci, carry):
                rows = pl.ds(pl.multiple_of(ci * 128, 128), 128)
                dh = ALPHA * d2_ref[rows, :] + acc[rows, :]
                xhat = xh_ref[rows, :]
                g = dh * g_ref[...]
                dpre = rs_ref[rows, 0:1] * (g - jnp.mean(g, axis=1, keepdims=True)
                                            - xhat * jnp.mean(g * xhat, axis=1, keepdims=True))
                d_ref[rows, :] = dpre
                db_ref[rows, :] = dpre.astype(BF16)
                st_ref[0:1, :] += jnp.sum(dh * xhat, axis=0, keepdims=True)
                st_ref[1:2, :] += jnp.sum(dh, axis=0, keepdims=True)
                return carry

            lax.fori_loop(0, tm // 128, rows_chunk, 0)

    row = pl.BlockSpec((tm, D), lambda i, k: (i, 0))
    body, more_specs, more = _behind(body, 6, after)
    return pl.pallas_call(
        body, name="d_h1_ln1", grid=(S // tm, nk),
        in_specs=[pl.BlockSpec((tm, tk), lambda i, k: (i, k)),
                  pl.BlockSpec((None, D, tk), lambda i, k: (k // per, 0, k % per)),
                  row, row, pl.BlockSpec((tm, 128), lambda i, k: (i, 0)), pl.BlockSpec((1, D), lambda i, k: (0, 0))]
        + more_specs,
        out_specs=[row, row, pl.BlockSpec((8, D), lambda i, k: (0, 0))],
        out_shape=[jax.ShapeDtypeStruct((S, D), F32), jax.ShapeDtypeStruct((S, D), BF16),
                   jax.ShapeDtypeStruct((8, D), F32)],
        scratch_shapes=[pltpu.VMEM((tm, D), F32)],
        compiler_params=_params(2),
    )(dprea, w1_g, dpre2, xhat1, rstd1, g1, *more)


def _d_merged(dpre1b, wout_g, proj, ya, yb):
    tm, tn = 512, 1024

    def body(d_ref, w_ref, ga_ref, gb_ref, ya_ref, yb_ref, dya_ref, dyb_ref, dga_ref, dgb_ref):
        dm = _dot_nt(d_ref[...], w_ref[...])
        sa = _sigmoid(ga_ref[...])
        sb = _sigmoid(gb_ref[...])
        dya_ref[...] = (dm * sa).astype(BF16)
        dyb_ref[...] = (dm * sb).astype(BF16)
        dga_ref[...] = (dm * ya_ref[...].astype(F32) * sa * (1.0 - sa)).astype(BF16)
        dgb_ref[...] = (dm * yb_ref[...].astype(F32) * sb * (1.0 - sb)).astype(BF16)

    tile = pl.BlockSpec((tm, tn), lambda i, j: (i, j))
    return pl.pallas_call(
        body, name="d_merged", grid=(S // tm, D // tn),
        in_specs=[pl.BlockSpec((tm, D), lambda i, j: (i, 0)), pl.BlockSpec((tn, D), lambda i, j: (j, 0)),
                  pl.BlockSpec((tm, tn), lambda i, j: (i, 5 + j)), pl.BlockSpec((tm, tn), lambda i, j: (i, 7 + j)),
                  tile, tile],
        out_specs=[tile] * 4,
        out_shape=[jax.ShapeDtypeStruct((S, D), BF16)] * 4,
        compiler_params=_params(2),
    )(dpre1b, wout_g, proj, proj, ya, yb)


def _d_branches(dya, dyb, wpa_g, wpb_g, after=None):
    tk = 512

    def body(da_ref, db_ref, wa_ref, wb_ref, oa_ref, ob_ref):
        @pl.when(pl.program_id(0) == 0)
        def _():
            oa_ref[...] = jnp.zeros_like(oa_ref)
            ob_ref[...] = jnp.zeros_like(ob_ref)

        oa_ref[...] += _dot_nt(da_ref[...], wa_ref[...])
        ob_ref[...] += _dot_nt(db_ref[...], wb_ref[...])

    body, more_specs, more = _behind(body, 4, after)
    return pl.pallas_call(
        body, name="d_branches", grid=(D // tk,),
        in_specs=[pl.BlockSpec((S, tk), lambda k: (0, k)), pl.BlockSpec((S, tk), lambda k: (0, k)),
                  pl.BlockSpec((None, DA, tk), lambda k: (k, 0, 0)), pl.BlockSpec((None, DB, tk), lambda k: (k, 0, 0))]
        + more_specs,
        out_specs=[pl.BlockSpec((S, DA), lambda k: (0, 0)), pl.BlockSpec((S, DB), lambda k: (0, 0))],
        out_shape=[jax.ShapeDtypeStruct((S, DA), F32), jax.ShapeDtypeStruct((S, DB), F32)],
        compiler_params=_params(1),
    )(dya, dyb, wpa_g, wpb_g, *more)


def _gmlp_bwd(proj, dgmlp, ws, ws_t, bsp_b, gain_v, bias_v):
    def body(u_ref, vb_ref, dg_ref, ws_ref, wst_ref, bsp_ref, g_ref, be_ref, duv_ref, gws_ref, gbs_ref, st_ref):
        @pl.when(pl.program_id(0) == 0)
        def _():
            gws_ref[...] = jnp.zeros_like(gws_ref)
            gbs_ref[...] = jnp.zeros_like(gbs_ref)
            st_ref[...] = jnp.zeros_like(st_ref)

        u, tu, u_act, vb, tv, rstd, xhat, vn = _gmlp_parts(u_ref, vb_ref, g_ref, be_ref)
        dg = dg_ref[...]
        dz = dg * u_act
        row = lax.broadcasted_iota(jnp.int32, (128, 128), 0)
        col = lax.broadcasted_iota(jnp.int32, (128, 128), 1)
        causal = row >= col
        causal_t = row <= col
        dvn_parts = []
        z_parts = []
        for g in range(NH):
            cols = slice(g * 128, (g + 1) * 128)
            vng = vn[:, cols].astype(BF16)
            dzg = dz[:, cols]
            dzb = dzg.astype(BF16)
            wsg = jnp.where(causal, ws_ref[g], 0.0).astype(BF16)
            wsg_t = jnp.where(causal_t, wst_ref[g], 0.0).astype(BF16)
            z_parts.append(_dot(wsg, vng) + bsp_ref[g])
            gws_ref[g] += jnp.where(causal, _dot_nt(dzb, vng), 0.0)
            gbs_ref[g] += jnp.broadcast_to(jnp.sum(dzg, axis=1, keepdims=True), (128, 128))
            dvn_parts.append(_dot(wsg_t, dzb))
        z = jnp.concatenate(z_parts, axis=1)
        dvn = jnp.concatenate(dvn_parts, axis=1)
        du = dg * z * _gelu_grad(u, tu)
        st_ref[0:1, :] += jnp.sum(dvn * xhat, axis=0, keepdims=True)
        st_ref[1:2, :] += jnp.sum(dvn, axis=0, keepdims=True)
        gg = dvn * g_ref[...]
        dgv = rstd * (gg - jnp.mean(gg, axis=1, keepdims=True) - xhat * jnp.mean(gg * xhat, axis=1, keepdims=True))
        dvb = dgv * _gelu_grad(vb, tv)
        duv_ref[:, 0:DB] = du.astype(BF16)
        duv_ref[:, DB:2 * DB] = dvb.astype(BF16)

    full3 = pl.BlockSpec((NH, 128, 128), lambda c: (0, 0, 0))
    vec = pl.BlockSpec((1, DB), lambda c: (0, 0))
    return pl.pallas_call(
        body, name="gmlp_bwd", grid=(NBLK,),
        in_specs=[pl.BlockSpec((128, DB), lambda c: (c, 3)), pl.BlockSpec((128, DB), lambda c: (c, 4)),
                  pl.BlockSpec((128, DB), lambda c: (c, 0)), full3, full3, full3, vec, vec],
        out_specs=[pl.BlockSpec((128, 2 * DB), lambda c: (c, 0)), full3, full3, pl.BlockSpec((8, DB), lambda c: (0, 0))],
        out_shape=[jax.ShapeDtypeStruct((S, 2 * DB), BF16), jax.ShapeDtypeStruct((NH, 128, 128), F32),
                   jax.ShapeDtypeStruct((NH, 128, 128), F32), jax.ShapeDtypeStruct((8, DB), F32)],
        compiler_params=_params(1),
    )(proj, proj, dgmlp, ws, ws_t, bsp_b, gain_v, bias_v)


def _rel_bias_grad(ds_sums):
    buckets = jnp.asarray(np.stack([_bucket_tile(d) for _, d in PATTERNS]))

    def body(bk_ref, ds_ref, o_ref):
        row = lax.broadcasted_iota(jnp.int32, (N_BUCKETS, 128), 0)
        lane = lax.broadcasted_iota(jnp.int32, (N_BUCKETS, 128), 1)

        def one_bucket(t, out):
            hits = [bk_ref[p] == t for p in range(3)]
            for h in range(NH):
                tot = jnp.zeros((128, 256), F32)
                for p in range(3):
                    tot = tot + jnp.where(hits[p], ds_ref[p, h], 0.0)
                out = jnp.where((row == t) & (lane == h), jnp.sum(tot), out)
            return out

        o_ref[...] = lax.fori_loop(0, N_BUCKETS, one_bucket, jnp.zeros((N_BUCKETS, 128), F32))

    return pl.pallas_call(
        body, name="rel_bias_grad",
        in_specs=[pl.BlockSpec(memory_space=pltpu.VMEM)] * 2, out_specs=pl.BlockSpec(memory_space=pltpu.VMEM),
        out_shape=jax.ShapeDtypeStruct((N_BUCKETS, 128), F32),
        compiler_params=pltpu.CompilerParams(vmem_limit_bytes=VMEM_LIMIT),
    )(buckets, ds_sums)


def _d_x(dproj, win_g, dpre1, after=None):
    tm, tn = 512, 512
    ws = DIN // N_CHIPS

    def body(a_ref, w_ref, d_ref, o_ref):
        acc = ALPHA * d_ref[...]
        for s in range(N_CHIPS):
            acc = acc + _dot_nt(a_ref[:, s * ws:(s + 1) * ws], w_ref[s])
        o_ref[...] = acc

    tile = pl.BlockSpec((tm, tn), lambda i, j: (i, j))
    body, more_specs, more = _behind(body, 3, after)
    return pl.pallas_call(
        body, name="d_x", grid=(S // tm, D // tn),
        in_specs=[pl.BlockSpec((tm, DIN), lambda i, j: (i, 0)),
                  pl.BlockSpec((N_CHIPS, tn, ws), lambda i, j: (0, j, 0)), tile] + more_specs,
        out_specs=tile, out_shape=jax.ShapeDtypeStruct((S, D), F32),
        compiler_params=_params(2),
    )(dproj, win_g, dpre1, *more)


def _adamw(w, g, m, v, name):
    rows, cols = w.shape
    tm = max(t for t in range(8, 257, 8) if rows % t == 0)

    def body(w_ref, g_ref, m_ref, v_ref, d_ref, nm_ref, nv_ref, go_ref):
        g = g_ref[...]
        m = ADAM_B1 * m_ref[...] + (1.0 - ADAM_B1) * g
        v = ADAM_B2 * v_ref[...] + (1.0 - ADAM_B2) * (g * g)
        m_hat = m / (1.0 - ADAM_B1 ** ADAM_STEP)
        v_hat = v / (1.0 - ADAM_B2 ** ADAM_STEP)
        d_ref[...] = -ADAM_LR * (m_hat / (jnp.sqrt(v_hat) + ADAM_EPS) + ADAM_WD * w_ref[...])
        nm_ref[...] = m
        nv_ref[...] = v
        go_ref[...] = g

    spec = pl.BlockSpec((tm, cols), lambda i: (i, 0))
    return pl.pallas_call(
        body, name=name, grid=(rows // tm,), in_specs=[spec] * 4, out_specs=[spec] * 4,
        out_shape=[jax.ShapeDtypeStruct((rows, cols), F32)] * 4, compiler_params=_params(1),
    )(w, g, m, v)


def _position():
    x, y, c = lax.axis_index("x"), lax.axis_index("y"), lax.axis_index("c")
    chips = [(1 - x, y), (x, 1 - y), (1 - x, 1 - y)]
    return x, y, c, chips


def _remote(src, dst, send_sems, recv_sems, k, to):
    return pltpu.make_async_remote_copy(src_ref=src, dst_ref=dst, send_sem=send_sems.at[k], recv_sem=recv_sems.at[k],
                                        device_id=to, device_id_type=MESH)


def _place_shard(w, name, after=None):
    rows, cols = w.shape
    tm = 256
    x, y = lax.axis_index("x"), lax.axis_index("y")

    def body(chip_ref, w_ref, o_ref):
        o_ref[...] = w_ref[...].astype(BF16)

    more_specs, more = ([ANY], [after]) if after is not None else ([], [])
    if after is not None:
        inner = body
        body = lambda chip_ref, w_ref, after_ref, o_ref: inner(chip_ref, w_ref, o_ref)
    return pl.pallas_call(
        body, name=name,
        grid_spec=pltpu.PrefetchScalarGridSpec(
            num_scalar_prefetch=1, grid=(rows // tm,),
            in_specs=[pl.BlockSpec((tm, cols), lambda i, chip: (i, 0))] + more_specs,
            out_specs=pl.BlockSpec((None, tm, cols), lambda i, chip: (chip[0], i, 0))),
        out_shape=jax.ShapeDtypeStruct((N_CHIPS, rows, cols), BF16),
        compiler_params=_params(1),
    )(jnp.reshape(2 * x + y, (1,)).astype(jnp.int32), w, *more)


def _to_bf16(x, name, after=None):
    tm = 256

    def body(x_ref, o_ref):
        o_ref[...] = x_ref[...].astype(BF16)

    spec = pl.BlockSpec((tm, x.shape[1]), lambda i: (i, 0))
    body, more_specs, more = _behind(body, 1, after)
    return pl.pallas_call(
        body, name=name, grid=(x.shape[0] // tm,), in_specs=[spec] + more_specs, out_specs=spec,
        out_shape=jax.ShapeDtypeStruct(x.shape, BF16), compiler_params=_params(1),
    )(x, *more)


HBM = pl.BlockSpec(memory_space=pltpu.HBM)
SEM = pl.BlockSpec(memory_space=pltpu.SEMAPHORE)
EFFECT = pltpu.SideEffectType.DATAFLOW_SIDE_EFFECTING


def _comm_call(name, body, bufs, sems_in, sems_out, after=None, token=False):
    nb, ns, no = len(bufs), len(sems_in), len(sems_out)
    n_in = nb + ns + (after is not None)

    def wrapped(*refs):
        body(refs[:nb], refs[nb:nb + ns], refs[n_in + nb:n_in + nb + no])
        if token:
            refs[-1][...] = jnp.zeros((8, 128), F32)

    outs = pl.pallas_call(
        wrapped, name=name,
        in_specs=[HBM] * nb + [SEM] * ns + ([ANY] if after is not None else []),
        out_specs=[HBM] * nb + [SEM] * no + ([pl.BlockSpec(memory_space=pltpu.VMEM)] if token else []),
        out_shape=[pltpu.HBM(b.shape, b.dtype) for b in bufs] + [pltpu.SemaphoreType.DMA((k,)) for k in sems_out]
        + ([jax.ShapeDtypeStruct((8, 128), F32)] if token else []),
        input_output_aliases={i: i for i in range(nb)},
        compiler_params=pltpu.CompilerParams(has_side_effects=EFFECT),
    )(*[pltpu.with_memory_space_constraint(b, pltpu.HBM) for b in bufs], *sems_in, *([after] if after is not None else []))
    return list(outs[:nb]), list(outs[nb:nb + no]), (outs[-1] if token else None)


RING_STAGES = {"ici_near": 2, "ici_far": 2, "d2d_near": 2, "d2d_far": 1}


def _ring_copies(buf, send_sems, recv_sems, k0, stage):
    x, y, c, _ = _position()
    hr = buf.shape[1] // 2
    qr = hr // 2
    half = lambda chip, h: buf.at[chip, pl.ds(h * hr, hr), :]
    quarter = lambda chip, h, q: buf.at[chip, pl.ds(h * hr + q * qr, qr), :]
    mine, x_chip, y_chip, far_chip = 2 * x + y, 2 * (1 - x) + y, 2 * x + (1 - y), 2 * (1 - x) + (1 - y)
    to_x, to_y, sibling = (1 - x, y, c), (x, 1 - y, c), (x, y, 1 - c)
    if stage == "ici_near":
        moves = [(half(mine, c), to_x, half(x_chip, c)), (half(mine, c), to_y, half(y_chip, c))]
    elif stage == "ici_far":
        moves = [(quarter(x_chip, c, 0), to_y, quarter(far_chip, c, 0)),
                 (quarter(y_chip, c, 1), to_x, quarter(far_chip, c, 1))]
    elif stage == "d2d_near":
        moves = [(half(x_chip, c), sibling, half(x_chip, 1 - c)), (half(y_chip, c), sibling, half(y_chip, 1 - c))]
    else:
        moves = [(half(far_chip, c), sibling, half(far_chip, 1 - c))]
    sends = [_remote(src, src, send_sems, recv_sems, k0 + i, to) for i, (src, to, _) in enumerate(moves)]
    arrivals = [_remote(got, got, send_sems, recv_sems, k0 + i, (x, y, c)) for i, (_, _, got) in enumerate(moves)]
    return sends, arrivals


def _ring_call(name, groups, actions, after=None):
    tags = list(dict.fromkeys(t for _, t, _ in actions))
    counts = {t: len(groups[t]["bufs"]) for t in tags}
    first = {t: sum(counts[u] for u in tags[:i]) for i, t in enumerate(tags)}
    waits = [(t, s) for v, t, s in actions if v == "wait"]
    starts = [(t, s) for v, t, s in actions if v == "start"]

    def body(bufs, sems_in, sems_out):
        for verb, t, s in actions:
            at, sems = (starts.index((t, s)), sems_out) if verb == "start" else (waits.index((t, s)), sems_in)
            for w in range(counts[t]):
                sends, arrivals = _ring_copies(bufs[first[t] + w], sems[2 * at], sems[2 * at + 1], RING_STAGES[s] * w, s)
                if verb == "start":
                    for cp in sends:
                        cp.start()
                else:
                    for cp in arrivals:
                        cp.wait_recv()
                    for cp in sends:
                        cp.wait_send()

    bufs, sems, token = _comm_call(
        name, body, [b for t in tags for b in groups[t]["bufs"]],
        [sem for t, s in waits for sem in groups[t]["sems"][s]],
        [RING_STAGES[s] * counts[t] for t, s in starts for _ in (0, 1)], after, token=True)
    for t in tags:
        groups[t]["bufs"] = bufs[first[t]:first[t] + counts[t]]
    for t, s in waits:
        del groups[t]["sems"][s]
    for i, (t, s) in enumerate(starts):
        groups[t]["sems"][s] = (sems[2 * i], sems[2 * i + 1])
    return token


def _cx_copies(src, dst, send_sems, recv_sems, k0):
    x, y, c, chips = _position()
    sends = [_remote(src.at[2 * cx + cy], dst.at[2 * x + y], send_sems, recv_sems, k0 + j, (cx, cy, c))
             for j, (cx, cy) in enumerate(chips)]
    arrivals = [_remote(dst.at[2 * cx + cy], dst.at[2 * cx + cy], send_sems, recv_sems, k0 + j, (x, y, c))
                for j, (cx, cy) in enumerate(chips)]
    return sends, arrivals


def _cx_start(name, pair_sums):
    n = len(pair_sums)
    landing = [lax.empty(p.shape, p.dtype) for p in pair_sums]

    def body(bufs, _, sems):
        for w in range(n):
            for cp in _cx_copies(bufs[w], bufs[n + w], sems[0], sems[1], 3 * w)[0]:
                cp.start()

    bufs, sems, token = _comm_call(name, body, list(pair_sums) + landing, [], [3 * n, 3 * n], token=True)
    return (bufs, sems), token


def _cx_wait(name, state, after):
    bufs, sems = state
    n = len(bufs) // 2

    def body(refs, sems_in, _):
        for w in range(n):
            sends, arrivals = _cx_copies(refs[w], refs[n + w], sems_in[0], sems_in[1], 3 * w)
            for cp in arrivals:
                cp.wait_recv()
            for cp in sends:
                cp.wait_send()

    bufs, _, _ = _comm_call(name, body, bufs, sems, [], after)
    return bufs[:n], bufs[n:]


def _px_copies(src, dst, send_sems, recv_sems, k):
    x, y, c, _ = _position()
    hr = src.shape[1] // 2
    send = _remote(src.at[:, pl.ds((1 - c) * hr, hr), :], dst, send_sems, recv_sems, k, (x, y, 1 - c))
    arrival = _remote(dst, dst, send_sems, recv_sems, k, (x, y, c))
    return send, arrival


def _px_start(name, grads):
    n = len(grads)
    landing = [lax.empty((N_CHIPS, g.shape[1] // 2, g.shape[2]), g.dtype) for g in grads]

    def body(bufs, _, sems):
        for w in range(n):
            _px_copies(bufs[w], bufs[n + w], sems[0], sems[1], w)[0].start()

    bufs, sems, token = _comm_call(name, body, list(grads) + landing, [], [n, n], token=True)
    return (bufs, sems), token


def _px_wait(name, state, after):
    bufs, sems = state
    n = len(bufs) // 2

    def body(refs, sems_in, _):
        for w in range(n):
            send, arrival = _px_copies(refs[w], refs[n + w], sems_in[0], sems_in[1], w)
            arrival.wait_recv()
            send.wait_send()

    bufs, _, _ = _comm_call(name, body, bufs, sems, [], after)
    return bufs[:n], bufs[n:]


def _pair_sum(grad, got, name):
    _, rows, cols = grad.shape
    hr = rows // 2
    tm = min(hr, 512)
    nb = hr // tm
    c = lax.axis_index("c")

    def body(c_ref, g_ref, o_ref, out_ref):
        out_ref[...] = (g_ref[...].astype(F32) + o_ref[...].astype(F32)).astype(BF16)

    return pl.pallas_call(
        body, name=name,
        grid_spec=pltpu.PrefetchScalarGridSpec(
            num_scalar_prefetch=1, grid=(N_CHIPS, nb),
            in_specs=[pl.BlockSpec((None, tm, cols), lambda s, i, c_ref: (s, c_ref[0] * nb + i, 0)),
                      pl.BlockSpec((None, tm, cols), lambda s, i, c_ref: (s, i, 0))],
            out_specs=pl.BlockSpec((None, tm, cols), lambda s, i, c_ref: (s, i, 0))),
        out_shape=jax.ShapeDtypeStruct((N_CHIPS, hr, cols), BF16),
        compiler_params=_params(2),
    )(jnp.reshape(c, (1,)).astype(jnp.int32), grad, got)


def _chip_sum(parts, pair_sums, name):
    _, hr, cols = parts.shape
    tm = min(hr, 512)
    nb = hr // tm
    x, y, c = lax.axis_index("x"), lax.axis_index("y"), lax.axis_index("c")

    def body(pos_ref, p_ref, own_ref, o_ref):
        chip = pos_ref[0]
        own = own_ref[...].astype(F32)
        term = lambda s: jnp.where(chip == s, own, p_ref[s].astype(F32))
        o_ref[...] = ((term(0) + term(1)) + term(2)) + term(3)

    return pl.pallas_call(
        body, name=name,
        grid_spec=pltpu.PrefetchScalarGridSpec(
            num_scalar_prefetch=1, grid=(nb,),
            in_specs=[pl.BlockSpec((N_CHIPS, tm, cols), lambda i, pos: (0, i, 0)),
                      pl.BlockSpec((None, tm, cols), lambda i, pos: (pos[0], i, 0))],
            out_specs=pl.BlockSpec((tm, cols), lambda i, pos: (pos[1] * nb + i, 0))),
        out_shape=jax.ShapeDtypeStruct((2 * hr, cols), F32), compiler_params=_params(1),
    )(jnp.stack([2 * x + y, c]).astype(jnp.int32), parts, pair_sums)


def _share_copies(buf, send_sems, recv_sems, k):
    x, y, c, _ = _position()
    hr = buf.shape[0] // 2
    mine, theirs = buf.at[pl.ds(c * hr, hr), :], buf.at[pl.ds((1 - c) * hr, hr), :]
    return (_remote(mine, mine, send_sems, recv_sems, k, (x, y, 1 - c)),
            _remote(theirs, theirs, send_sems, recv_sems, k, (x, y, c)))


def _share_start(name, bufs):
    n = len(bufs)

    def body(refs, _, sems):
        for w in range(n):
            _share_copies(refs[w], sems[0], sems[1], w)[0].start()

    bufs, sems, token = _comm_call(name, body, list(bufs), [], [n, n], token=True)
    return (bufs, sems), token


def _share_wait(name, state, after):
    bufs, sems = state

    def body(refs, sems_in, _):
        for w in range(len(bufs)):
            send, arrival = _share_copies(refs[w], sems_in[0], sems_in[1], w)
            arrival.wait_recv()
            send.wait_send()

    return _comm_call(name, body, bufs, sems, [], after)[0]


def _allreduce_small(g):
    rows = g.shape[0]
    half = rows // 2

    def body(g_ref, o_ref, sib, slots, send_sems, recv_sems):
        x, y, c, chips = _position()
        me, sibling = (x, y, c), (x, y, 1 - c)
        my_chip = 2 * x + y
        mine = pl.ds(pl.multiple_of(c * half, 8), half)
        theirs = pl.ds(pl.multiple_of((1 - c) * half, 8), half)
        pair = _remote(g_ref.at[theirs], sib, send_sems, recv_sems, 0, sibling)
        pair.start()
        pair.wait()
        slots[my_chip] = g_ref[mine, :] + sib[...]
        sent = []
        for j, (cx, cy) in enumerate(chips):
            cp = _remote(slots.at[my_chip], slots.at[my_chip], send_sems, recv_sems, 1 + j, (cx, cy, c))
            cp.start()
            sent.append(cp)
        for j, (cx, cy) in enumerate(chips):
            got = slots.at[2 * cx + cy]
            _remote(got, got, send_sems, recv_sems, 1 + j, me).wait_recv()
        for cp in sent:
            cp.wait_send()
        o_ref[mine, :] = ((slots[0] + slots[1]) + slots[2]) + slots[3]
        swap = _remote(o_ref.at[mine], o_ref.at[mine], send_sems, recv_sems, 4, sibling)
        swap.start()
        swap.wait()

    vm = pl.BlockSpec(memory_space=pltpu.VMEM)
    return pl.pallas_call(
        body, name="allreduce_small",
        in_specs=[vm], out_specs=vm, out_shape=jax.ShapeDtypeStruct((rows, 128), F32),
        scratch_shapes=[pltpu.VMEM((half, 128), F32), pltpu.VMEM((N_CHIPS, half, 128), F32),
                        pltpu.SemaphoreType.DMA((5,)), pltpu.SemaphoreType.DMA((5,))],
        compiler_params=pltpu.CompilerParams(vmem_limit_bytes=VMEM_LIMIT),
    )(g)


_SMALL =("rel_bias", "ln_v_gain", "ln_v_bias", "w_spatial", "b_spatial", "ln1_gain", "ln1_bias",
          "b_ff1", "b_ff2", "ln2_gain", "ln2_bias")
_SMALL_ROWS = 1200
_LOSS_AT = (152832 // 128, 0)


def _pack_small(parts):
    flat = jnp.concatenate([parts[k].reshape(-1).astype(F32) for k in _SMALL])
    flat = jnp.pad(flat, (0, _SMALL_ROWS * 128 - flat.shape[0]))
    return flat.reshape(_SMALL_ROWS, 128)


def _unpack_small(packed, like):
    flat = packed.reshape(-1)
    out, at = {}, 0
    for k in _SMALL:
        n = math.prod(like[k].shape)
        out[k] = flat[at:at + n].reshape(like[k].shape)
        at += n
    return out


def kernel(x, w_in, rel_bias, ln_v_gain, ln_v_bias, w_spatial, b_spatial, w_proj_a, w_proj_b, w_out, ln1_gain, ln1_bias, w_ff1, b_ff1, w_ff2, b_ff2, ln2_gain, ln2_bias, loss_target, m_w_in, m_rel_bias, m_ln_v_gain, m_ln_v_bias, m_w_spatial, m_b_spatial, m_w_proj_a, m_w_proj_b, m_w_out, m_ln1_gain, m_ln1_bias, m_w_ff1, m_b_ff1, m_w_ff2, m_b_ff2, m_ln2_gain, m_ln2_bias, v_w_in, v_rel_bias, v_ln_v_gain, v_ln_v_bias, v_w_spatial, v_b_spatial, v_w_proj_a, v_w_proj_b, v_w_out, v_ln1_gain, v_ln1_bias, v_w_ff1, v_b_ff1, v_w_ff2, v_b_ff2, v_ln2_gain, v_ln2_bias):
    args = dict(locals())
    big = ("w_in", "w_proj_a", "w_proj_b", "w_out", "w_ff1", "w_ff2")
    weights = ("w_in", "rel_bias", "ln_v_gain", "ln_v_bias", "w_spatial", "b_spatial", "w_proj_a", "w_proj_b", "w_out",
               "ln1_gain", "ln1_bias", "w_ff1", "b_ff1", "w_ff2", "b_ff2", "ln2_gain", "ln2_bias")

    xs = x[0]
    target = loss_target[0]

    ring = {"a": {"bufs": [_place_shard(w_in[0], "place_w_in")], "sems": {}}}
    tok = _ring_call("allgather_a_near", ring, [("start", "a", "ici_near")])
    placed = [_place_shard(args[k][0], f"place_{k}", after=tok) for k in big[1:]]
    for tag, bufs in (("b", placed[0:3]), ("c", placed[3:4]), ("d", placed[4:5])):
        ring[tag] = {"bufs": bufs, "sems": {}}
    xb = _to_bf16(xs, "x_to_bf16", after=placed[4])

    mx, my = lax.axis_index("x"), lax.axis_index("y")
    own = jnp.reshape(2 * mx + my, (1,)).astype(jnp.int32)
    near = jnp.stack([2 * (1 - mx) + my, 2 * mx + (1 - my)]).astype(jnp.int32)
    far = jnp.reshape(2 * (1 - mx) + (1 - my), (1,)).astype(jnp.int32)
    proj = _proj(xb, ring["a"]["bufs"][0], own, "proj_own")
    _ring_call("allgather_a_far", ring, [("wait", "a", "ici_near"), ("start", "a", "ici_far"), ("start", "a", "d2d_near"),
                                         ("start", "b", "ici_near"), ("start", "c", "ici_near")], after=proj)
    _ring_call("allgather_a_near_done", ring, [("wait", "a", "d2d_near")])
    proj = _proj(xb, ring["a"]["bufs"][0], near, "proj_near", into=proj)
    _ring_call("allgather_a_last", ring, [("wait", "a", "ici_far"), ("start", "a", "d2d_far")], after=proj)
    _ring_call("allgather_a_done", ring, [("wait", "a", "d2d_far")])
    (win_g,) = ring["a"]["bufs"]
    proj = _proj(xb, win_g, far, "proj_far", into=proj)
    _ring_call("allgather_b_far", ring, [("wait", "b", "ici_near"), ("start", "b", "ici_far"), ("start", "b", "d2d_near")],
               after=proj)
    ws = w_spatial[0]
    ws_t = jnp.transpose(ws, (0, 2, 1))
    bsp_b = jnp.broadcast_to(b_spatial[0][:, :, None], (NH, 128, 128))
    gmlp = _gmlp_fwd(proj, ws, bsp_b, ln_v_gain, ln_v_bias)
    attn, lse = _attention_fwd(proj, rel_bias)
    _ring_call("allgather_b_last_c_far", ring,
               [("wait", "b", "ici_far"), ("start", "b", "d2d_far"),
                ("wait", "c", "ici_near"), ("start", "c", "ici_far"), ("start", "c", "d2d_near"),
                ("start", "d", "ici_near")], after=attn)
    _ring_call("allgather_b_done", ring, [("wait", "b", "d2d_near"), ("wait", "b", "d2d_far")])
    wpa_g, wpb_g, wout_g = ring["b"]["bufs"]
    wout_full = wout_g.reshape(D, D)
    ya, yb, merged = _branch(attn, gmlp, wpa_g, wpb_g, proj)
    xhat1, rstd1, h1b = _out_ln1(merged, wout_full, xs, ln1_gain, ln1_bias)
    _ring_call("allgather_c_last_d_far", ring,
               [("wait", "c", "ici_far"), ("start", "c", "d2d_far"),
                ("wait", "d", "ici_near"), ("start", "d", "ici_far"), ("start", "d", "d2d_near")], after=h1b)
    _ring_call("allgather_c_done", ring, [("wait", "c", "d2d_near"), ("wait", "c", "d2d_far")])
    (w1_g,) = ring["c"]["bufs"]
    a, r = _ff1(h1b, w1_g, b_ff1)
    _ring_call("allgather_d_last", ring, [("wait", "d", "ici_far"), ("start", "d", "d2d_far")], after=a)
    _ring_call("allgather_d_done", ring, [("wait", "d", "d2d_near"), ("wait", "d", "d2d_far")])
    (w2_g,) = ring["d"]["bufs"]
    w2_full = w2_g.reshape(DFF, D)
    dpre2, dpre2b, st2 = _ff2_ln2_loss(a, w2_full, xhat1, ln1_gain, ln1_bias, b_ff2, ln2_gain, ln2_bias, target)

    def pair_and_chip(tag, state, after):
        local, from_sibling = _px_wait(f"pair_exchange_wait_{tag}", state, after)
        pair_sums = [_pair_sum(g, o, f"pair_sum_{tag}_{i}") for i, (g, o) in enumerate(zip(local, from_sibling))]
        return _cx_start(f"chip_exchange_start_{tag}", pair_sums)

    g_w2 = _grad_w(a, dpre2b, "grad_w_ff2", 512, 2048, False)
    px, tok = _px_start("pair_exchange_start_w_ff2", [g_w2.reshape(N_CHIPS, DFF // N_CHIPS, D)])
    dprea, g_b1 = _d_ff1(dpre2b, w2_full, r, after=tok)
    cx_w2, tok = pair_and_chip("w_ff2", px, dprea)
    g_w1 = _grad_w(h1b, dprea, "grad_w_ff1", 512, 2048, True, after=tok)
    px, tok = _px_start("pair_exchange_start_w_ff1", [g_w1])
    dpre1, dpre1b, st1 = _d_h1_ln1(dprea, w1_g, dpre2, xhat1, rstd1, ln1_gain, after=tok)
    cx_w1, tok = pair_and_chip("w_ff1", px, dpre1b)
    g_wout = _grad_w(merged, dpre1b, "grad_w_out", 512, 2048, False, after=tok)
    dya, dyb, dga, dgb = _d_merged(dpre1b, wout_full, proj, ya, yb)
    g_wpa = _grad_w(attn, dya, "grad_w_proj_a", 1024, 512, True)
    g_wpb = _grad_w(gmlp, dyb, "grad_w_proj_b", 1024, 512, True)
    px, tok = _px_start("pair_exchange_start_b", [g_wpa, g_wpb, g_wout.reshape(N_CHIPS, D // N_CHIPS, D)])
    dattn, dgmlp = _d_branches(dya, dyb, wpa_g, wpb_g, after=tok)
    duv, g_ws, g_bs, stv = _gmlp_bwd(proj, dgmlp, ws, ws_t, bsp_b, ln_v_gain, ln_v_bias)
    cx_b, tok = pair_and_chip("b", px, duv)
    dq, dk, dv, ds_sums = _attention_bwd(proj, dattn, attn, lse, rel_bias, after=tok)
    g_rb = _rel_bias_grad(ds_sums)[:, :NH]

    small_g = dict(rel_bias=g_rb, ln_v_gain=stv[0], ln_v_bias=stv[1], w_spatial=g_ws, b_spatial=g_bs[:, :, 0],
                   ln1_gain=st1[0], ln1_bias=st1[1], b_ff1=g_b1, b_ff2=st2[2], ln2_gain=st2[0], ln2_bias=st2[1])
    gs = _allreduce_small(_pack_small(small_g).at[_LOSS_AT].set(st2[3, 0]))
    ds_, ms_, vs_, _ = _adamw(_pack_small({k: args[k] for k in _SMALL}), gs,
                           _pack_small({k: args["m_" + k] for k in _SMALL}),
                           _pack_small({k: args["v_" + k] for k in _SMALL}), "adamw_small")
    like = {k: args[k] for k in _SMALL}
    grads, deltas, new_m, new_v = (_unpack_small(t, like) for t in (gs, ds_, ms_, vs_))

    dproj = jnp.concatenate([dq, dk, dv, duv, dga, dgb], axis=1)
    g_win = _grad_w(xb, dproj, "grad_w_in", 512, 2304, True, after=gs)
    px, tok = _px_start("pair_exchange_start_w_in", [g_win])

    def chip_sums(tag, state, names, after):
        pair_sums, from_chips = _cx_wait(f"chip_exchange_wait_{tag}", state, after)
        halves = [_chip_sum(p, own, f"chip_sum_{k}") for p, own, k in zip(from_chips, pair_sums, names)]
        return _share_start(f"share_start_{tag}", halves)

    def adam(tag, state, names, after):
        last = None
        for k, g in zip(names, _share_wait(f"share_wait_{tag}", state, after)):
            d_, m_, v_, g_ = _adamw(args[k][0], g, args["m_" + k][0], args["v_" + k][0], f"adamw_{k}")
            grads[k], deltas[k], new_m[k], new_v[k] = g_[None], d_[None], m_[None], v_[None]
            last = d_
        return last

    sh_w2, tok = chip_sums("w_ff2", cx_w2, ["w_ff2"], tok)
    sh_w1, tok = chip_sums("w_ff1", cx_w1, ["w_ff1"], tok)
    sh_b, tok = chip_sums("b", cx_b, ["w_proj_a", "w_proj_b", "w_out"], tok)
    cx_in, tok = pair_and_chip("w_in", px, tok)
    grad_x = _d_x(dproj, win_g, dpre1, after=tok)
    done = adam("w_ff2", sh_w2, ["w_ff2"], grad_x)
    done = adam("w_ff1", sh_w1, ["w_ff1"], done)
    done = adam("b", sh_b, ["w_proj_a", "w_proj_b", "w_out"], done)
    sh_in, tok = chip_sums("w_in", cx_in, ["w_in"], done)
    adam("w_in", sh_in, ["w_in"], tok)

    loss = gs[_LOSS_AT] * (0.5 / D)
    return (loss, grad_x[None], *[grads[k] for k in weights], *[deltas[k] for k in weights],
            *[new_m[k] for k in weights], *[new_v[k] for k in weights])
```

```python
import math

import numpy as np
import jax
import jax.numpy as jnp
from jax import lax
from jax.experimental import pallas as pl
from jax.experimental.pallas import tpu as pltpu

F32 = jnp.float32
BF16 = jnp.bfloat16

S = 2048
D = 2048
DA = 1024
DB = 1024
DFF = 8192
DIN = 9216
NH = 8
HD = 128
NBLK = 16
PATTERNS = ((128, 1), (512, 4), (2048, 16))
N_BUCKETS = 32
MAX_DISTANCE = 2048
ALPHA = 2.0 ** 0.25
LN_EPS = 1e-5
NEG_INF = -1e30
SCALE = HD ** -0.5
N_CHIPS = 4

ADAM_LR = 0.001
ADAM_B1 = 0.9
ADAM_B2 = 0.999
ADAM_EPS = 1e-08
ADAM_WD = 0.01
ADAM_STEP = 10

VMEM_LIMIT = 56 * 1024 * 1024
MESH = pl.DeviceIdType.MESH
ANY = pl.BlockSpec(memory_space=pl.ANY)


def _params(n_axes, vmem=VMEM_LIMIT):
    return pltpu.CompilerParams(dimension_semantics=("arbitrary",) * n_axes, vmem_limit_bytes=vmem)


def _bucket_tile(dilation):
    qi = np.arange(128)[:, None]
    kj = np.arange(256)[None, :]
    n = np.clip(128 + qi - kj, 0, 128) * dilation
    max_exact = N_BUCKETS // 2
    nf = np.maximum(n, 1).astype(np.float32)
    large = max_exact + (np.log(nf / np.float32(max_exact)) / np.float32(math.log(MAX_DISTANCE / max_exact))
                         * np.float32(N_BUCKETS - max_exact)).astype(np.int32)
    large = np.minimum(large, N_BUCKETS - 1)
    return np.where(n < max_exact, n, large).astype(np.int32)


def _gelu(x):
    c = math.sqrt(2.0 / math.pi)
    t = jnp.tanh(c * (x + 0.044715 * x * x * x))
    return 0.5 * x * (1.0 + t), t


def _gelu_grad(x, t):
    c = math.sqrt(2.0 / math.pi)
    return 0.5 * (1.0 + t) + 0.5 * x * (1.0 - t * t) * c * (1.0 + 3.0 * 0.044715 * x * x)


def _sigmoid(x):
    return 1.0 / (1.0 + jnp.exp(-x))


def _dot(a, b):
    return jnp.dot(a, b, preferred_element_type=F32)


def _behind(body, n_in, after):
    if after is None:
        return body, [], []
    return (lambda *refs: body(*refs[:n_in], *refs[n_in + 1:])), [ANY], [after]


def _dot_nt(a, b):
    return lax.dot_general(a, b, (((1,), (1,)), ((), ())), preferred_element_type=F32)


def _proj(xb, win_g, shards, name, into=None):
    tn = 768
    per = 2304 // tn

    def body(shards_ref, x_ref, w_ref, *rest):
        rest[-1][...] = _dot(x_ref[...], w_ref[...])

    in_specs = [pl.BlockSpec((S, D), lambda j, sh: (0, 0)),
                pl.BlockSpec((None, D, tn), lambda j, sh: (sh[j // per], 0, j % per))]
    return pl.pallas_call(
        body, name=name,
        grid_spec=pltpu.PrefetchScalarGridSpec(
            num_scalar_prefetch=1, grid=(shards.shape[0] * per,),
            in_specs=in_specs + ([ANY] if into is not None else []),
            out_specs=pl.BlockSpec((S, tn), lambda j, sh: (0, sh[j // per] * per + j % per))),
        out_shape=jax.ShapeDtypeStruct((S, DIN), F32),
        input_output_aliases={3: 0} if into is not None else {},
        compiler_params=_params(1),
    )(shards, xb, win_g, *([into] if into is not None else []))


FWD_HEADS_PER_STEP = 4
BWD_HEADS_PER_STEP = 2


def _bias_tiles(rel_bias):
    buckets = jnp.asarray(np.stack([_bucket_tile(d) for _, d in PATTERNS]))

    def body(rb_ref, bk_ref, o_ref):
        qi = lax.broadcasted_iota(jnp.int32, (128, 256), 0)
        kj = lax.broadcasted_iota(jnp.int32, (128, 256), 1)
        steps = 128 + qi - kj
        band = (steps >= 0) & (steps <= 128)
        o_ref[...] = jnp.zeros_like(o_ref)
        for p in range(len(PATTERNS)):
            bucket = bk_ref[p]

            def one_bucket(t, carry):
                hit = bucket == t
                for h in range(NH):
                    o_ref[p, h] = jnp.where(hit, rb_ref[t, h], o_ref[p, h])
                return carry

            lax.fori_loop(0, N_BUCKETS, one_bucket, 0)
            for h in range(NH):
                o_ref[p, h] = jnp.where(band, o_ref[p, h], NEG_INF)

    return pl.pallas_call(
        body, name="bias_tiles",
        in_specs=[pl.BlockSpec(memory_space=pltpu.SMEM), pl.BlockSpec(memory_space=pltpu.VMEM)],
        out_specs=pl.BlockSpec(memory_space=pltpu.VMEM),
        out_shape=jax.ShapeDtypeStruct((len(PATTERNS), NH, 128, 256), F32),
        compiler_params=pltpu.CompilerParams(vmem_limit_bytes=VMEM_LIMIT),
    )(rel_bias, buckets)


def _block_rows(b, dilation):
    nblk = NBLK // dilation
    r, n = b // nblk, b % nblk
    start = r + n * (128 * dilation)
    prev_start = jnp.maximum(start - 128 * dilation, r)
    if dilation == 1:
        return pl.ds(pl.multiple_of(start, 128), 128), pl.ds(pl.multiple_of(prev_start, 128), 128), n > 0
    return pl.ds(start, 128, stride=dilation), pl.ds(prev_start, 128, stride=dilation), n > 0


def _head_specs(first, hps):
    return [pl.BlockSpec((S, HD), lambda g, j=j: (0, first + g * hps + j)) for j in range(hps)]


def _bias_spec(hps):
    return pl.BlockSpec((len(PATTERNS), hps, 128, 256), lambda g: (0, g, 0, 0))


def _heads_spec(hps):
    return pl.BlockSpec((S, hps * HD), lambda g: (0, g))


def _attention_fwd(proj, bias):
    hps = FWD_HEADS_PER_STEP

    def body(bias_ref, *refs):
        q_refs, k_refs, v_refs = (refs[i * hps:(i + 1) * hps] for i in range(3))
        o_ref, lse_ref = refs[3 * hps:3 * hps + 2]
        acc_scrs, m_scrs, l_scrs = (refs[3 * hps + 2 + i * hps:3 * hps + 2 + (i + 1) * hps] for i in range(3))
        kj = lax.broadcasted_iota(jnp.int32, (128, 256), 1)
        for p, (_, d) in enumerate(PATTERNS):
            prev_blocks = NBLK // d > 1

            def block(b, carry):
                units = [(j,) + _block_rows(blk, d) for blk in (b, b + NBLK // 2) for j in range(hps)]
                scores = []
                for j, rows, prows, _ in units:
                    q = q_refs[j][rows, :].astype(BF16)
                    cur = _dot_nt(q, k_refs[j][rows, :].astype(BF16))
                    if prev_blocks:
                        cur = jnp.concatenate([_dot_nt(q, k_refs[j][prows, :].astype(BF16)), cur], axis=1)
                    scores.append(cur)
                soft = []
                for u, (j, _, _, has_prev) in enumerate(units):
                    if prev_blocks:
                        s = jnp.where((kj >= 128) | has_prev, scores[u] * SCALE + bias_ref[p, j], NEG_INF)
                    else:
                        s = scores[u] * SCALE + bias_ref[p, j, :, 128:256]
                    m = jnp.max(s, axis=1, keepdims=True)
                    e = jnp.exp(s - m)
                    soft.append((m, jnp.sum(e, axis=1, keepdims=True), e.astype(BF16)))
                outs = []
                for u, (j, rows, prows, _) in enumerate(units):
                    e = soft[u][2]
                    if prev_blocks:
                        outs.append(_dot(e[:, :128], v_refs[j][prows, :].astype(BF16))
                                    + _dot(e[:, 128:], v_refs[j][rows, :].astype(BF16)))
                    else:
                        outs.append(_dot(e, v_refs[j][rows, :].astype(BF16)))
                for u, (j, rows, _, _) in enumerate(units):
                    acc_scr, m_scr, l_scr = acc_scrs[j], m_scrs[j], l_scrs[j]
                    (m, den, _), o = soft[u], outs[u]
                    if p == 0:
                        acc_scr[rows, :] = o
                        m_scr[rows, :] = jnp.broadcast_to(m, (128, HD))
                        l_scr[rows, :] = jnp.broadcast_to(den, (128, HD))
                    else:
                        m_old = m_scr[rows, :]
                        m_new = jnp.maximum(m_old, m)
                        w_old, w_new = jnp.exp(m_old - m_new), jnp.exp(m - m_new)
                        acc_scr[rows, :] = acc_scr[rows, :] * w_old + o * w_new
                        l_scr[rows, :] = l_scr[rows, :] * w_old + den * w_new
                        m_scr[rows, :] = m_new
                return carry

            lax.fori_loop(0, NBLK // 2, block, 0)
        for j in range(hps):
            cols = slice(j * HD, (j + 1) * HD)
            den = l_scrs[j][...]
            o_ref[:, cols] = (acc_scrs[j][...] / den).astype(BF16)
            lse_ref[:, cols] = m_scrs[j][...] + jnp.log(den)

    return pl.pallas_call(
        body, name="attention_fwd", grid=(NH // hps,),
        in_specs=[_bias_spec(hps)] + _head_specs(0, hps) + _head_specs(NH, hps) + _head_specs(2 * NH, hps),
        out_specs=[_heads_spec(hps), _heads_spec(hps)],
        out_shape=[jax.ShapeDtypeStruct((S, DA), BF16), jax.ShapeDtypeStruct((S, DA), F32)],
        scratch_shapes=[pltpu.VMEM((S, HD), F32)] * (3 * hps),
        compiler_params=_params(1),
    )(bias, *([proj] * (3 * hps)))


def _attention_bwd(proj, dattn, attn, lse, bias, after=None):
    hps = BWD_HEADS_PER_STEP

    def body(bias_ref, *refs):
        q_refs, k_refs, v_refs, do_refs, o_refs, lse_refs = (refs[i * hps:(i + 1) * hps] for i in range(6))
        dq_ref, dk_ref, dv_ref, ds_ref = refs[6 * hps:6 * hps + 4]
        dl_scrs, dq_scrs, dk_scrs, dv_scrs = (refs[6 * hps + 4 + i * hps:6 * hps + 4 + (i + 1) * hps] for i in range(4))
        ds_ref[...] = jnp.zeros_like(ds_ref)
        for j in range(hps):
            dq_scrs[j][...] = jnp.zeros((S, HD), F32)
            dk_scrs[j][...] = jnp.zeros((S, HD), F32)
            dv_scrs[j][...] = jnp.zeros((S, HD), F32)
            prod = do_refs[j][...] * o_refs[j][...].astype(F32)
            dl_scrs[j][...] = jnp.broadcast_to(jnp.sum(prod, axis=1, keepdims=True), (S, HD))
        for p, (_, d) in enumerate(PATTERNS):
            prev_blocks = NBLK // d > 1

            def block(b, carry):
                units = [(j,) + _block_rows(b + i * (NBLK // 4), d) for i in range(4) for j in range(hps)]
                ops, raw = [], []
                for j, rows, prows, _ in units:
                    q, do = q_refs[j][rows, :].astype(BF16), do_refs[j][rows, :].astype(BF16)
                    kc, vc = k_refs[j][rows, :].astype(BF16), v_refs[j][rows, :].astype(BF16)
                    if prev_blocks:
                        kp, vp = k_refs[j][prows, :].astype(BF16), v_refs[j][prows, :].astype(BF16)
                        ops.append((q, do, kc, kp))
                        raw.append((_dot_nt(q, kc), _dot_nt(do, vc), _dot_nt(q, kp), _dot_nt(do, vp)))
                    else:
                        ops.append((q, do, kc))
                        raw.append((_dot_nt(q, kc), _dot_nt(do, vc)))
                probs = []
                for u, (j, rows, _, has_prev) in enumerate(units):
                    lse_b, dl_b = lse_refs[j][rows, :], dl_scrs[j][rows, :]
                    p_c = jnp.exp(raw[u][0] * SCALE + bias_ref[p, j, :, 128:256] - lse_b)
                    ds_c = p_c * (raw[u][1] - dl_b)
                    ds_ref[p, j, :, 128:256] += ds_c
                    if prev_blocks:
                        p_p = jnp.where(has_prev, jnp.exp(raw[u][2] * SCALE + bias_ref[p, j, :, 0:128] - lse_b), 0.0)
                        ds_p = p_p * (raw[u][3] - dl_b)
                        ds_ref[p, j, :, 0:128] += ds_p
                        probs.append((p_c, ds_c, p_p, ds_p))
                    else:
                        probs.append((p_c, ds_c))
                grads = []
                for u in range(len(units)):
                    q, do, kc = ops[u][:3]
                    p_c, ds_c = probs[u][:2]
                    dq = _dot(ds_c.astype(BF16), kc)
                    cur = (_dot(ds_c.T.astype(BF16), q) * SCALE, _dot(p_c.T.astype(BF16), do))
                    if prev_blocks:
                        p_p, ds_p = probs[u][2:]
                        dq = dq + _dot(ds_p.astype(BF16), ops[u][3])
                        cur = cur + (_dot(ds_p.T.astype(BF16), q) * SCALE, _dot(p_p.T.astype(BF16), do))
                    grads.append((dq * SCALE,) + cur)
                for u, (j, rows, prows, _) in enumerate(units):
                    dq_scrs[j][rows, :] += grads[u][0]
                    dk_scrs[j][rows, :] += grads[u][1]
                    dv_scrs[j][rows, :] += grads[u][2]
                    if prev_blocks:
                        dk_scrs[j][prows, :] += grads[u][3]
                        dv_scrs[j][prows, :] += grads[u][4]
                return carry

            lax.fori_loop(0, NBLK // 4, block, 0)
        for j in range(hps):
            cols = slice(j * HD, (j + 1) * HD)
            dq_ref[:, cols] = dq_scrs[j][...].astype(BF16)
            dk_ref[:, cols] = dk_scrs[j][...].astype(BF16)
            dv_ref[:, cols] = dv_scrs[j][...].astype(BF16)

    body, more_specs, more = _behind(body, 1 + 6 * hps, after)
    return pl.pallas_call(
        body, name="attention_bwd", grid=(NH // hps,),
        in_specs=[_bias_spec(hps)]
        + _head_specs(0, hps) + _head_specs(NH, hps) + _head_specs(2 * NH, hps) + 3 * _head_specs(0, hps)
        + more_specs,
        out_specs=3 * [_heads_spec(hps)] + [pl.BlockSpec((3, hps, 128, 256), lambda g: (0, g, 0, 0))],
        out_shape=[jax.ShapeDtypeStruct((S, DA), BF16)] * 3 + [jax.ShapeDtypeStruct((3, NH, 128, 256), F32)],
        scratch_shapes=[pltpu.VMEM((S, HD), F32)] * (4 * hps),
        compiler_params=_params(1),
    )(bias, *([proj] * (3 * hps)), *([dattn] * hps), *([attn] * hps), *([lse] * hps), *more)


def _gmlp_parts(u_ref, vb_ref, g_ref, be_ref):
    u = u_ref[...]
    u_act, tu = _gelu(u)
    vb = vb_ref[...]
    gv, tv = _gelu(vb)
    mean = jnp.mean(gv, axis=1, keepdims=True)
    cen = gv - mean
    var = jnp.mean(cen * cen, axis=1, keepdims=True)
    rstd = lax.rsqrt(var + LN_EPS)
    xhat = cen * rstd
    vn = xhat * g_ref[...] + be_ref[...]
    return u, tu, u_act, vb, tv, rstd, xhat, vn


def _gmlp_fwd(proj, ws, bsp_b, gain_v, bias_v):
    def body(u_ref, vb_ref, ws_ref, bsp_ref, g_ref, be_ref, o_ref):
        _, _, u_act, _, _, _, _, vn = _gmlp_parts(u_ref, vb_ref, g_ref, be_ref)
        row = lax.broadcasted_iota(jnp.int32, (128, 128), 0)
        col = lax.broadcasted_iota(jnp.int32, (128, 128), 1)
        causal = row >= col
        for g in range(NH):
            cols = slice(g * 128, (g + 1) * 128)
            wsg = jnp.where(causal, ws_ref[g], 0.0).astype(BF16)
            z = _dot(wsg, vn[:, cols].astype(BF16)) + bsp_ref[g]
            o_ref[:, cols] = (u_act[:, cols] * z).astype(BF16)

    return pl.pallas_call(
        body, name="gmlp_fwd", grid=(NBLK,),
        in_specs=[pl.BlockSpec((128, DB), lambda c: (c, 3)), pl.BlockSpec((128, DB), lambda c: (c, 4)),
                  pl.BlockSpec((NH, 128, 128), lambda c: (0, 0, 0)), pl.BlockSpec((NH, 128, 128), lambda c: (0, 0, 0)),
                  pl.BlockSpec((1, DB), lambda c: (0, 0)), pl.BlockSpec((1, DB), lambda c: (0, 0))],
        out_specs=pl.BlockSpec((128, DB), lambda c: (c, 0)),
        out_shape=jax.ShapeDtypeStruct((S, DB), BF16),
        compiler_params=_params(1),
    )(proj, proj, ws, bsp_b, gain_v, bias_v)


def _branch(attn, gmlp, wpa_g, wpb_g, proj):
    tn = 512

    def body(a_ref, g_ref, wa_ref, wb_ref, ga_ref, gb_ref, ya_ref, yb_ref, mg_ref):
        ya = _dot(a_ref[...], wa_ref[...])
        yb = _dot(g_ref[...], wb_ref[...])
        ya_ref[...] = ya.astype(BF16)
        yb_ref[...] = yb.astype(BF16)
        mg_ref[...] = (_sigmoid(ga_ref[...]) * ya + _sigmoid(gb_ref[...]) * yb).astype(BF16)

    out = pl.BlockSpec((S, tn), lambda j: (0, j))
    return pl.pallas_call(
        body, name="branch", grid=(D // tn,),
        in_specs=[pl.BlockSpec((S, DA), lambda j: (0, 0)), pl.BlockSpec((S, DB), lambda j: (0, 0)),
                  pl.BlockSpec((None, DA, tn), lambda j: (j, 0, 0)), pl.BlockSpec((None, DB, tn), lambda j: (j, 0, 0)),
                  pl.BlockSpec((S, tn), lambda j: (0, 5120 // tn + j)), pl.BlockSpec((S, tn), lambda j: (0, 7168 // tn + j))],
        out_specs=[out, out, out],
        out_shape=[jax.ShapeDtypeStruct((S, D), BF16)] * 3,
        compiler_params=_params(1),
    )(attn, gmlp, wpa_g, wpb_g, proj, proj)


def _out_ln1(merged, wout_g, x, gain, bias):
    tm = 256

    def body(m_ref, w_ref, x_ref, g_ref, b_ref, xh_ref, rs_ref, h_ref):
        pre = ALPHA * x_ref[...] + _dot(m_ref[...], w_ref[...])
        mean = jnp.mean(pre, axis=1, keepdims=True)
        cen = pre - mean
        var = jnp.mean(cen * cen, axis=1, keepdims=True)
        rstd = lax.rsqrt(var + LN_EPS)
        xhat = cen * rstd
        xh_ref[...] = xhat
        rs_ref[...] = jnp.broadcast_to(rstd, (tm, 128))
        h_ref[...] = (xhat * g_ref[...] + b_ref[...]).astype(BF16)

    row = pl.BlockSpec((tm, D), lambda i: (i, 0))
    vec = pl.BlockSpec((1, D), lambda i: (0, 0))
    return pl.pallas_call(
        body, name="out_ln1", grid=(S // tm,),
        in_specs=[row, pl.BlockSpec((D, D), lambda i: (0, 0)), row, vec, vec],
        out_specs=[row, pl.BlockSpec((tm, 128), lambda i: (i, 0)), row],
        out_shape=[jax.ShapeDtypeStruct((S, D), F32), jax.ShapeDtypeStruct((S, 128), F32),
                   jax.ShapeDtypeStruct((S, D), BF16)],
        compiler_params=_params(1),
    )(merged, wout_g, x, gain, bias)


def _ff1(h1b, w1_g, b1):
    tn = 512
    per = D // tn

    def body(h_ref, w_ref, b_ref, a_ref, r_ref):
        r = jnp.maximum(_dot(h_ref[...], w_ref[...]) + b_ref[...], 0.0)
        r_ref[...] = r.astype(BF16)
        a_ref[...] = (r * r).astype(BF16)

    out = pl.BlockSpec((S, tn), lambda j: (0, j))
    return pl.pallas_call(
        body, name="ff1", grid=(DFF // tn,),
        in_specs=[pl.BlockSpec((S, D), lambda j: (0, 0)),
                  pl.BlockSpec((None, D, tn), lambda j: (j // per, 0, j % per)),
                  pl.BlockSpec((1, tn), lambda j: (0, j))],
        out_specs=[out, out],
        out_shape=[jax.ShapeDtypeStruct((S, DFF), BF16)] * 2,
        compiler_params=_params(1),
    )(h1b, w1_g, b1)


def _ff2_ln2_loss(a, w2_g, xhat1, g1, b1, b2, g2, be2, target):
    tm, tk = 512, 1024
    nk = DFF // tk

    def body(a_ref, w_ref, xh_ref, g1_ref, b1_ref, b2_ref, g2_ref, be2_ref, t_ref, d_ref, db_ref, st_ref, acc):
        i, k = pl.program_id(0), pl.program_id(1)

        @pl.when(k == 0)
        def _():
            acc[...] = jnp.zeros_like(acc)

        @pl.when((i == 0) & (k == 0))
        def _():
            st_ref[...] = jnp.zeros_like(st_ref)

        acc[...] += _dot(a_ref[...], w_ref[...])

        @pl.when(k == nk - 1)
        def _():
            def rows_chunk(ci, carry):
                rows = pl.ds(pl.multiple_of(ci * 128, 128), 128)
                h1 = xh_ref[rows, :] * g1_ref[...] + b1_ref[...]
                pre = ALPHA * h1 + acc[rows, :] + b2_ref[...]
                mean = jnp.mean(pre, axis=1, keepdims=True)
                cen = pre - mean
                var = jnp.mean(cen * cen, axis=1, keepdims=True)
                rstd = lax.rsqrt(var + LN_EPS)
                xhat = cen * rstd
                y = xhat * g2_ref[...] + be2_ref[...]
                err = y - t_ref[rows, :]
                dy = err * (1.0 / D)
                g = dy * g2_ref[...]
                dpre = rstd * (g - jnp.mean(g, axis=1, keepdims=True)
                               - xhat * jnp.mean(g * xhat, axis=1, keepdims=True))
                d_ref[rows, :] = dpre
                db_ref[rows, :] = dpre.astype(BF16)
                st_ref[0:1, :] += jnp.sum(dy * xhat, axis=0, keepdims=True)
                st_ref[1:2, :] += jnp.sum(dy, axis=0, keepdims=True)
                st_ref[2:3, :] += jnp.sum(dpre, axis=0, keepdims=True)
                st_ref[3:4, :] += jnp.broadcast_to(jnp.sum(err * err).reshape(1, 1), (1, D))
                return carry

            lax.fori_loop(0, tm // 128, rows_chunk, 0)

    row = pl.BlockSpec((tm, D), lambda i, k: (i, 0))
    vec = pl.BlockSpec((1, D), lambda i, k: (0, 0))
    return pl.pallas_call(
        body, name="ff2_ln2_loss", grid=(S // tm, nk),
        in_specs=[pl.BlockSpec((tm, tk), lambda i, k: (i, k)), pl.BlockSpec((tk, D), lambda i, k: (k, 0)),
                  row, vec, vec, vec, vec, vec, row],
        out_specs=[row, row, pl.BlockSpec((8, D), lambda i, k: (0, 0))],
        out_shape=[jax.ShapeDtypeStruct((S, D), F32), jax.ShapeDtypeStruct((S, D), BF16),
                   jax.ShapeDtypeStruct((8, D), F32)],
        scratch_shapes=[pltpu.VMEM((tm, D), F32)],
        compiler_params=_params(2),
    )(a, w2_g, xhat1, g1, b1, b2, g2, be2, target)


def _grad_w(act, dout, name, ti, tj, sharded, after=None):
    m, n = act.shape[1], dout.shape[1]
    ns = n // N_CHIPS
    per = ns // tj if sharded else None

    def body(a_ref, b_ref, o_ref, at_scr):
        @pl.when(pl.program_id(1) == 0)
        def _():
            at_scr[...] = a_ref[...].T

        o_ref[...] = _dot(at_scr[...], b_ref[...]).astype(BF16)

    if sharded:
        out_spec = pl.BlockSpec((None, ti, tj), lambda i, j: (j // per, i, j % per))
        out_shape = jax.ShapeDtypeStruct((N_CHIPS, m, ns), BF16)
    else:
        out_spec = pl.BlockSpec((ti, tj), lambda i, j: (i, j))
        out_shape = jax.ShapeDtypeStruct((m, n), BF16)
    body, more_specs, more = _behind(body, 2, after)
    return pl.pallas_call(
        body, name=name, grid=(m // ti, n // tj),
        in_specs=[pl.BlockSpec((S, ti), lambda i, j: (0, i)), pl.BlockSpec((S, tj), lambda i, j: (0, j))] + more_specs,
        out_specs=out_spec, out_shape=out_shape,
        scratch_shapes=[pltpu.VMEM((ti, S), BF16)],
        compiler_params=_params(2),
    )(act, dout, *more)


def _d_ff1(dpre2b, w2_g, r, after=None):
    tn = 512

    def body(d_ref, w_ref, r_ref, o_ref, gb_ref):
        da = _dot_nt(d_ref[...], w_ref[...])
        dp = da * (2.0 * r_ref[...].astype(F32))
        o_ref[...] = dp.astype(BF16)
        gb_ref[...] = jnp.sum(dp, axis=0, keepdims=True)

    body, more_specs, more = _behind(body, 3, after)
    return pl.pallas_call(
        body, name="d_ff1", grid=(DFF // tn,),
        in_specs=[pl.BlockSpec((S, D), lambda j: (0, 0)), pl.BlockSpec((tn, D), lambda j: (j, 0)),
                  pl.BlockSpec((S, tn), lambda j: (0, j))] + more_specs,
        out_specs=[pl.BlockSpec((S, tn), lambda j: (0, j)), pl.BlockSpec((1, tn), lambda j: (0, j))],
        out_shape=[jax.ShapeDtypeStruct((S, DFF), BF16), jax.ShapeDtypeStruct((1, DFF), F32)],
        compiler_params=_params(1),
    )(dpre2b, w2_g, r, *more)


def _d_h1_ln1(dprea, w1_g, dpre2, xhat1, rstd1, g1, after=None):
    tm, tk = 512, 1024
    per = D // tk
    nk = DFF // tk

    def body(a_ref, w_ref, d2_ref, xh_ref, rs_ref, g_ref, d_ref, db_ref, st_ref, acc):
        i, k = pl.program_id(0), pl.program_id(1)

        @pl.when(k == 0)
        def _():
            acc[...] = jnp.zeros_like(acc)

        @pl.when((i == 0) & (k == 0))
        def _():
            st_ref[...] = jnp.zeros_like(st_ref)

        acc[...] += _dot_nt(a_ref[...], w_ref[...])

        @pl.when(k == nk - 1)
        def _():
            def rows_chunk(ci, carry):
                rows = pl.ds(pl.multiple_of(ci * 128, 128), 128)
                dh = ALPHA * d2_ref[rows, :] + acc[rows, :]
                xhat = xh_ref[rows, :]
                g = dh * g_ref[...]
                dpre = rs_ref[rows, 0:1] * (g - jnp.mean(g, axis=1, keepdims=True)
                                            - xhat * jnp.mean(g * xhat, axis=1, keepdims=True))
                d_ref[rows, :] = dpre
                db_ref[rows, :] = dpre.astype(BF16)
                st_ref[0:1, :] += jnp.sum(dh * xhat, axis=0, keepdims=True)
                st_ref[1:2, :] += jnp.sum(dh, axis=0, keepdims=True)
                return carry

            lax.fori_loop(0, tm // 128, rows_chunk, 0)

    row = pl.BlockSpec((tm, D), lambda i, k: (i, 0))
    body, more_specs, more = _behind(body, 6, after)
    return pl.pallas_call(
        body, name="d_h1_ln1", grid=(S // tm, nk),
        in_specs=[pl.BlockSpec((tm, tk), lambda i, k: (i, k)),
                  pl.BlockSpec((None, D, tk), lambda i, k: (k // per, 0, k % per)),
                  row, row, pl.BlockSpec((tm, 128), lambda i, k: (i, 0)), pl.BlockSpec((1, D), lambda i, k: (0, 0))]
        + more_specs,
        out_specs=[row, row, pl.BlockSpec((8, D), lambda i, k: (0, 0))],
        out_shape=[jax.ShapeDtypeStruct((S, D), F32), jax.ShapeDtypeStruct((S, D), BF16),
                   jax.ShapeDtypeStruct((8, D), F32)],
        scratch_shapes=[pltpu.VMEM((tm, D), F32)],
        compiler_params=_params(2),
    )(dprea, w1_g, dpre2, xhat1, rstd1, g1, *more)


def _d_merged(dpre1b, wout_g, proj, ya, yb):
    tm, tn = 512, 1024

    def body(d_ref, w_ref, ga_ref, gb_ref, ya_ref, yb_ref, dya_ref, dyb_ref, dga_ref, dgb_ref):
        dm = _dot_nt(d_ref[...], w_ref[...])
        sa = _sigmoid(ga_ref[...])
        sb = _sigmoid(gb_ref[...])
        dya_ref[...] = (dm * sa).astype(BF16)
        dyb_ref[...] = (dm * sb).astype(BF16)
        dga_ref[...] = (dm * ya_ref[...].astype(F32) * sa * (1.0 - sa)).astype(BF16)
        dgb_ref[...] = (dm * yb_ref[...].astype(F32) * sb * (1.0 - sb)).astype(BF16)

    tile = pl.BlockSpec((tm, tn), lambda i, j: (i, j))
    return pl.pallas_call(
        body, name="d_merged", grid=(S // tm, D // tn),
        in_specs=[pl.BlockSpec((tm, D), lambda i, j: (i, 0)), pl.BlockSpec((tn, D), lambda i, j: (j, 0)),
                  pl.BlockSpec((tm, tn), lambda i, j: (i, 5 + j)), pl.BlockSpec((tm, tn), lambda i, j: (i, 7 + j)),
                  tile, tile],
        out_specs=[tile] * 4,
        out_shape=[jax.ShapeDtypeStruct((S, D), BF16)] * 4,
        compiler_params=_params(2),
    )(dpre1b, wout_g, proj, proj, ya, yb)


def _d_branches(dya, dyb, wpa_g, wpb_g, after=None):
    tm = 512
    ws = D // N_CHIPS

    def body(da_ref, db_ref, wa_ref, wb_ref, oa_ref, ob_ref):
        for d_ref, w_ref, o_ref in ((da_ref, wa_ref, oa_ref), (db_ref, wb_ref, ob_ref)):
            acc = _dot_nt(d_ref[:, 0:ws], w_ref[0])
            for s in range(1, N_CHIPS):
                acc = acc + _dot_nt(d_ref[:, s * ws:(s + 1) * ws], w_ref[s])
            o_ref[...] = acc

    rows = lambda width: pl.BlockSpec((tm, width), lambda i: (i, 0))
    whole = lambda n: pl.BlockSpec((N_CHIPS, n, ws), lambda i: (0, 0, 0))
    body, more_specs, more = _behind(body, 4, after)
    return pl.pallas_call(
        body, name="d_branches", grid=(S // tm,),
        in_specs=[rows(D), rows(D), whole(DA), whole(DB)] + more_specs,
        out_specs=[rows(DA), rows(DB)],
        out_shape=[jax.ShapeDtypeStruct((S, DA), F32), jax.ShapeDtypeStruct((S, DB), F32)],
        compiler_params=_params(1),
    )(dya, dyb, wpa_g, wpb_g, *more)


def _gmlp_bwd(proj, dgmlp, ws, ws_t, bsp_b, gain_v, bias_v):
    def body(u_ref, vb_ref, dg_ref, ws_ref, wst_ref, bsp_ref, g_ref, be_ref, duv_ref, gws_ref, gbs_ref, st_ref):
        @pl.when(pl.program_id(0) == 0)
        def _():
            gws_ref[...] = jnp.zeros_like(gws_ref)
            gbs_ref[...] = jnp.zeros_like(gbs_ref)
            st_ref[...] = jnp.zeros_like(st_ref)

        u, tu, u_act, vb, tv, rstd, xhat, vn = _gmlp_parts(u_ref, vb_ref, g_ref, be_ref)
        dg = dg_ref[...]
        dz = dg * u_act
        row = lax.broadcasted_iota(jnp.int32, (128, 128), 0)
        col = lax.broadcasted_iota(jnp.int32, (128, 128), 1)
        causal = row >= col
        causal_t = row <= col
        dvn_parts = []
        z_parts = []
        for g in range(NH):
            cols = slice(g * 128, (g + 1) * 128)
            vng = vn[:, cols].astype(BF16)
            dzg = dz[:, cols]
            dzb = dzg.astype(BF16)
            wsg = jnp.where(causal, ws_ref[g], 0.0).astype(BF16)
            wsg_t = jnp.where(causal_t, wst_ref[g], 0.0).astype(BF16)
            z_parts.append(_dot(wsg, vng) + bsp_ref[g])
            gws_ref[g] += jnp.where(causal, _dot_nt(dzb, vng), 0.0)
            gbs_ref[g] += jnp.broadcast_to(jnp.sum(dzg, axis=1, keepdims=True), (128, 128))
            dvn_parts.append(_dot(wsg_t, dzb))
        z = jnp.concatenate(z_parts, axis=1)
        dvn = jnp.concatenate(dvn_parts, axis=1)
        du = dg * z * _gelu_grad(u, tu)
        st_ref[0:1, :] += jnp.sum(dvn * xhat, axis=0, keepdims=True)
        st_ref[1:2, :] += jnp.sum(dvn, axis=0, keepdims=True)
        gg = dvn * g_ref[...]
        dgv = rstd * (gg - jnp.mean(gg, axis=1, keepdims=True) - xhat * jnp.mean(gg * xhat, axis=1, keepdims=True))
        dvb = dgv * _gelu_grad(vb, tv)
        duv_ref[:, 0:DB] = du.astype(BF16)
        duv_ref[:, DB:2 * DB] = dvb.astype(BF16)

    full3 = pl.BlockSpec((NH, 128, 128), lambda c: (0, 0, 0))
    vec = pl.BlockSpec((1, DB), lambda c: (0, 0))
    return pl.pallas_call(
        body, name="gmlp_bwd", grid=(NBLK,),
        in_specs=[pl.BlockSpec((128, DB), lambda c: (c, 3)), pl.BlockSpec((128, DB), lambda c: (c, 4)),
                  pl.BlockSpec((128, DB), lambda c: (c, 0)), full3, full3, full3, vec, vec],
        out_specs=[pl.BlockSpec((128, 2 * DB), lambda c: (c, 0)), full3, full3, pl.BlockSpec((8, DB), lambda c: (0, 0))],
        out_shape=[jax.ShapeDtypeStruct((S, 2 * DB), BF16), jax.ShapeDtypeStruct((NH, 128, 128), F32),
                   jax.ShapeDtypeStruct((NH, 128, 128), F32), jax.ShapeDtypeStruct((8, DB), F32)],
        compiler_params=_params(1),
    )(proj, proj, dgmlp, ws, ws_t, bsp_b, gain_v, bias_v)


def _rel_bias_grad(ds_sums):
    buckets = jnp.asarray(np.stack([_bucket_tile(d) for _, d in PATTERNS]))

    def body(bk_ref, ds_ref, o_ref):
        row = lax.broadcasted_iota(jnp.int32, (N_BUCKETS, 128), 0)
        lane = lax.broadcasted_iota(jnp.int32, (N_BUCKETS, 128), 1)

        def one_bucket(t, out):
            hits = [bk_ref[p] == t for p in range(3)]
            for h in range(NH):
                tot = jnp.zeros((128, 256), F32)
                for p in range(3):
                    tot = tot + jnp.where(hits[p], ds_ref[p, h], 0.0)
                out = jnp.where((row == t) & (lane == h), jnp.sum(tot), out)
            return out

        o_ref[...] = lax.fori_loop(0, N_BUCKETS, one_bucket, jnp.zeros((N_BUCKETS, 128), F32))

    return pl.pallas_call(
        body, name="rel_bias_grad",
        in_specs=[pl.BlockSpec(memory_space=pltpu.VMEM)] * 2, out_specs=pl.BlockSpec(memory_space=pltpu.VMEM),
        out_shape=jax.ShapeDtypeStruct((N_BUCKETS, 128), F32),
        compiler_params=pltpu.CompilerParams(vmem_limit_bytes=VMEM_LIMIT),
    )(buckets, ds_sums)


def _d_x(dproj, win_g, dpre1, after=None):
    tm, tn = 512, 512
    ws = DIN // N_CHIPS

    def body(a_ref, w_ref, d_ref, o_ref):
        acc = ALPHA * d_ref[...]
        for s in range(N_CHIPS):
            acc = acc + _dot_nt(a_ref[:, s * ws:(s + 1) * ws], w_ref[s])
        o_ref[...] = acc

    tile = pl.BlockSpec((tm, tn), lambda i, j: (i, j))
    body, more_specs, more = _behind(body, 3, after)
    return pl.pallas_call(
        body, name="d_x", grid=(S // tm, D // tn),
        in_specs=[pl.BlockSpec((tm, DIN), lambda i, j: (i, 0)),
                  pl.BlockSpec((N_CHIPS, tn, ws), lambda i, j: (0, j, 0)), tile] + more_specs,
        out_specs=tile, out_shape=jax.ShapeDtypeStruct((S, D), F32),
        compiler_params=_params(2),
    )(dproj, win_g, dpre1, *more)


def _adamw(w, g, m, v, name):
    rows, cols = w.shape
    tm = max(t for t in range(8, 257, 8) if rows % t == 0)

    def body(w_ref, g_ref, m_ref, v_ref, d_ref, nm_ref, nv_ref, go_ref):
        g = g_ref[...]
        m = ADAM_B1 * m_ref[...] + (1.0 - ADAM_B1) * g
        v = ADAM_B2 * v_ref[...] + (1.0 - ADAM_B2) * (g * g)
        m_hat = m / (1.0 - ADAM_B1 ** ADAM_STEP)
        v_hat = v / (1.0 - ADAM_B2 ** ADAM_STEP)
        d_ref[...] = -ADAM_LR * (m_hat / (jnp.sqrt(v_hat) + ADAM_EPS) + ADAM_WD * w_ref[...])
        nm_ref[...] = m
        nv_ref[...] = v
        go_ref[...] = g

    spec = pl.BlockSpec((tm, cols), lambda i: (i, 0))
    return pl.pallas_call(
        body, name=name, grid=(rows // tm,), in_specs=[spec] * 4, out_specs=[spec] * 4,
        out_shape=[jax.ShapeDtypeStruct((rows, cols), F32)] * 4, compiler_params=_params(1),
    )(w, g, m, v)


def _position():
    x, y, c = lax.axis_index("x"), lax.axis_index("y"), lax.axis_index("c")
    chips = [(1 - x, y), (x, 1 - y), (1 - x, 1 - y)]
    return x, y, c, chips


def _remote(src, dst, send_sems, recv_sems, k, to):
    return pltpu.make_async_remote_copy(src_ref=src, dst_ref=dst, send_sem=send_sems.at[k], recv_sem=recv_sems.at[k],
                                        device_id=to, device_id_type=MESH)


def _place_shard(w, name, after=None):
    rows, cols = w.shape
    tm = 256
    x, y = lax.axis_index("x"), lax.axis_index("y")

    def body(chip_ref, w_ref, o_ref):
        o_ref[...] = w_ref[...].astype(BF16)

    more_specs, more = ([ANY], [after]) if after is not None else ([], [])
    if after is not None:
        inner = body
        body = lambda chip_ref, w_ref, after_ref, o_ref: inner(chip_ref, w_ref, o_ref)
    return pl.pallas_call(
        body, name=name,
        grid_spec=pltpu.PrefetchScalarGridSpec(
            num_scalar_prefetch=1, grid=(rows // tm,),
            in_specs=[pl.BlockSpec((tm, cols), lambda i, chip: (i, 0))] + more_specs,
            out_specs=pl.BlockSpec((None, tm, cols), lambda i, chip: (chip[0], i, 0))),
        out_shape=jax.ShapeDtypeStruct((N_CHIPS, rows, cols), BF16),
        compiler_params=_params(1),
    )(jnp.reshape(2 * x + y, (1,)).astype(jnp.int32), w, *more)


def _to_bf16(x, name, after=None):
    tm = 256

    def body(x_ref, o_ref):
        o_ref[...] = x_ref[...].astype(BF16)

    spec = pl.BlockSpec((tm, x.shape[1]), lambda i: (i, 0))
    body, more_specs, more = _behind(body, 1, after)
    return pl.pallas_call(
        body, name=name, grid=(x.shape[0] // tm,), in_specs=[spec] + more_specs, out_specs=spec,
        out_shape=jax.ShapeDtypeStruct(x.shape, BF16), compiler_params=_params(1),
    )(x, *more)


HBM = pl.BlockSpec(memory_space=pltpu.HBM)
SEM = pl.BlockSpec(memory_space=pltpu.SEMAPHORE)
EFFECT = pltpu.SideEffectType.DATAFLOW_SIDE_EFFECTING


def _comm_call(name, body, bufs, sems_in, sems_out, after=None, token=False):
    nb, ns, no = len(bufs), len(sems_in), len(sems_out)
    n_in = nb + ns + (after is not None)

    def wrapped(*refs):
        body(refs[:nb], refs[nb:nb + ns], refs[n_in + nb:n_in + nb + no])
        if token:
            refs[-1][...] = jnp.zeros((8, 128), F32)

    outs = pl.pallas_call(
        wrapped, name=name,
        in_specs=[HBM] * nb + [SEM] * ns + ([ANY] if after is not None else []),
        out_specs=[HBM] * nb + [SEM] * no + ([pl.BlockSpec(memory_space=pltpu.VMEM)] if token else []),
        out_shape=[pltpu.HBM(b.shape, b.dtype) for b in bufs] + [pltpu.SemaphoreType.DMA((k,)) for k in sems_out]
        + ([jax.ShapeDtypeStruct((8, 128), F32)] if token else []),
        input_output_aliases={i: i for i in range(nb)},
        compiler_params=pltpu.CompilerParams(has_side_effects=EFFECT),
    )(*[pltpu.with_memory_space_constraint(b, pltpu.HBM) for b in bufs], *sems_in, *([after] if after is not None else []))
    return list(outs[:nb]), list(outs[nb:nb + no]), (outs[-1] if token else None)


RING_STAGES = {"ici_near": 2, "ici_far": 2, "d2d_near": 2, "d2d_far": 1}


def _ring_copies(buf, send_sems, recv_sems, k0, stage):
    x, y, c, _ = _position()
    hr = buf.shape[1] // 2
    qr = hr // 2
    half = lambda chip, h: buf.at[chip, pl.ds(h * hr, hr), :]
    quarter = lambda chip, h, q: buf.at[chip, pl.ds(h * hr + q * qr, qr), :]
    mine, x_chip, y_chip, far_chip = 2 * x + y, 2 * (1 - x) + y, 2 * x + (1 - y), 2 * (1 - x) + (1 - y)
    to_x, to_y, sibling = (1 - x, y, c), (x, 1 - y, c), (x, y, 1 - c)
    if stage == "ici_near":
        moves = [(half(mine, c), to_x, half(x_chip, c)), (half(mine, c), to_y, half(y_chip, c))]
    elif stage == "ici_far":
        moves = [(quarter(x_chip, c, 0), to_y, quarter(far_chip, c, 0)),
                 (quarter(y_chip, c, 1), to_x, quarter(far_chip, c, 1))]
    elif stage == "d2d_near":
        moves = [(half(x_chip, c), sibling, half(x_chip, 1 - c)), (half(y_chip, c), sibling, half(y_chip, 1 - c))]
    else:
        moves = [(half(far_chip, c), sibling, half(far_chip, 1 - c))]
    sends = [_remote(src, src, send_sems, recv_sems, k0 + i, to) for i, (src, to, _) in enumerate(moves)]
    arrivals = [_remote(got, got, send_sems, recv_sems, k0 + i, (x, y, c)) for i, (_, _, got) in enumerate(moves)]
    return sends, arrivals


def _ring_call(name, groups, actions, after=None):
    tags = list(dict.fromkeys(t for _, t, _ in actions))
    counts = {t: len(groups[t]["bufs"]) for t in tags}
    first = {t: sum(counts[u] for u in tags[:i]) for i, t in enumerate(tags)}
    waits = [(t, s) for v, t, s in actions if v == "wait"]
    starts = [(t, s) for v, t, s in actions if v == "start"]

    def body(bufs, sems_in, sems_out):
        for verb, t, s in actions:
            at, sems = (starts.index((t, s)), sems_out) if verb == "start" else (waits.index((t, s)), sems_in)
            for w in range(counts[t]):
                sends, arrivals = _ring_copies(bufs[first[t] + w], sems[2 * at], sems[2 * at + 1], RING_STAGES[s] * w, s)
                if verb == "start":
                    for cp in sends:
                        cp.start()
                else:
                    for cp in arrivals:
                        cp.wait_recv()
                    for cp in sends:
                        cp.wait_send()

    bufs, sems, token = _comm_call(
        name, body, [b for t in tags for b in groups[t]["bufs"]],
        [sem for t, s in waits for sem in groups[t]["sems"][s]],
        [RING_STAGES[s] * counts[t] for t, s in starts for _ in (0, 1)], after, token=True)
    for t in tags:
        groups[t]["bufs"] = bufs[first[t]:first[t] + counts[t]]
    for t, s in waits:
        del groups[t]["sems"][s]
    for i, (t, s) in enumerate(starts):
        groups[t]["sems"][s] = (sems[2 * i], sems[2 * i + 1])
    return token


def _cx_copies(src, dst, send_sems, recv_sems, k0):
    x, y, c, chips = _position()
    sends = [_remote(src.at[2 * cx + cy], dst.at[2 * x + y], send_sems, recv_sems, k0 + j, (cx, cy, c))
             for j, (cx, cy) in enumerate(chips)]
    arrivals = [_remote(dst.at[2 * cx + cy], dst.at[2 * cx + cy], send_sems, recv_sems, k0 + j, (x, y, c))
                for j, (cx, cy) in enumerate(chips)]
    return sends, arrivals


def _cx_start(name, pair_sums):
    n = len(pair_sums)
    landing = [lax.empty(p.shape, p.dtype) for p in pair_sums]

    def body(bufs, _, sems):
        for w in range(n):
            for cp in _cx_copies(bufs[w], bufs[n + w], sems[0], sems[1], 3 * w)[0]:
                cp.start()

    bufs, sems, token = _comm_call(name, body, list(pair_sums) + landing, [], [3 * n, 3 * n], token=True)
    return (bufs, sems), token


def _cx_wait(name, state, after):
    bufs, sems = state
    n = len(bufs) // 2

    def body(refs, sems_in, _):
        for w in range(n):
            sends, arrivals = _cx_copies(refs[w], refs[n + w], sems_in[0], sems_in[1], 3 * w)
            for cp in arrivals:
                cp.wait_recv()
            for cp in sends:
                cp.wait_send()

    bufs, _, _ = _comm_call(name, body, bufs, sems, [], after)
    return bufs[:n], bufs[n:]


def _px_copies(src, dst, send_sems, recv_sems, k):
    x, y, c, _ = _position()
    hr = src.shape[1] // 2
    send = _remote(src.at[:, pl.ds((1 - c) * hr, hr), :], dst, send_sems, recv_sems, k, (x, y, 1 - c))
    arrival = _remote(dst, dst, send_sems, recv_sems, k, (x, y, c))
    return send, arrival


def _px_start(name, grads):
    n = len(grads)
    landing = [lax.empty((N_CHIPS, g.shape[1] // 2, g.shape[2]), g.dtype) for g in grads]

    def body(bufs, _, sems):
        for w in range(n):
            _px_copies(bufs[w], bufs[n + w], sems[0], sems[1], w)[0].start()

    bufs, sems, token = _comm_call(name, body, list(grads) + landing, [], [n, n], token=True)
    return (bufs, sems), token


def _px_wait(name, state, after):
    bufs, sems = state
    n = len(bufs) // 2

    def body(refs, sems_in, _):
        for w in range(n):
            send, arrival = _px_copies(refs[w], refs[n + w], sems_in[0], sems_in[1], w)
            arrival.wait_recv()
            send.wait_send()

    bufs, _, _ = _comm_call(name, body, bufs, sems, [], after)
    return bufs[:n], bufs[n:]


def _pair_sum(grad, got, name):
    _, rows, cols = grad.shape
    hr = rows // 2
    tm = min(hr, 512)
    nb = hr // tm
    c = lax.axis_index("c")

    def body(c_ref, g_ref, o_ref, out_ref):
        out_ref[...] = (g_ref[...].astype(F32) + o_ref[...].astype(F32)).astype(BF16)

    return pl.pallas_call(
        body, name=name,
        grid_spec=pltpu.PrefetchScalarGridSpec(
            num_scalar_prefetch=1, grid=(N_CHIPS, nb),
            in_specs=[pl.BlockSpec((None, tm, cols), lambda s, i, c_ref: (s, c_ref[0] * nb + i, 0)),
                      pl.BlockSpec((None, tm, cols), lambda s, i, c_ref: (s, i, 0))],
            out_specs=pl.BlockSpec((None, tm, cols), lambda s, i, c_ref: (s, i, 0))),
        out_shape=jax.ShapeDtypeStruct((N_CHIPS, hr, cols), BF16),
        compiler_params=_params(2),
    )(jnp.reshape(c, (1,)).astype(jnp.int32), grad, got)


def _chip_sum(parts, pair_sums, name):
    _, hr, cols = parts.shape
    tm = min(hr, 512)
    nb = hr // tm
    x, y, c = lax.axis_index("x"), lax.axis_index("y"), lax.axis_index("c")

    def body(pos_ref, p_ref, own_ref, o_ref):
        chip = pos_ref[0]
        own = own_ref[...].astype(F32)
        term = lambda s: jnp.where(chip == s, own, p_ref[s].astype(F32))
        o_ref[...] = ((term(0) + term(1)) + term(2)) + term(3)

    return pl.pallas_call(
        body, name=name,
        grid_spec=pltpu.PrefetchScalarGridSpec(
            num_scalar_prefetch=1, grid=(nb,),
            in_specs=[pl.BlockSpec((N_CHIPS, tm, cols), lambda i, pos: (0, i, 0)),
                      pl.BlockSpec((None, tm, cols), lambda i, pos: (pos[0], i, 0))],
            out_specs=pl.BlockSpec((tm, cols), lambda i, pos: (pos[1] * nb + i, 0))),
        out_shape=jax.ShapeDtypeStruct((2 * hr, cols), F32), compiler_params=_params(1),
    )(jnp.stack([2 * x + y, c]).astype(jnp.int32), parts, pair_sums)


def _share_copies(buf, send_sems, recv_sems, k):
    x, y, c, _ = _position()
    hr = buf.shape[0] // 2
    mine, theirs = buf.at[pl.ds(c * hr, hr), :], buf.at[pl.ds((1 - c) * hr, hr), :]
    return (_remote(mine, mine, send_sems, recv_sems, k, (x, y, 1 - c)),
            _remote(theirs, theirs, send_sems, recv_sems, k, (x, y, c)))


def _share_start(name, bufs):
    n = len(bufs)

    def body(refs, _, sems):
        for w in range(n):
            _share_copies(refs[w], sems[0], sems[1], w)[0].start()

    bufs, sems, token = _comm_call(name, body, list(bufs), [], [n, n], token=True)
    return (bufs, sems), token


def _share_wait(name, state, after):
    bufs, sems = state

    def body(refs, sems_in, _):
        for w in range(len(bufs)):
            send, arrival = _share_copies(refs[w], sems_in[0], sems_in[1], w)
            arrival.wait_recv()
            send.wait_send()

    return _comm_call(name, body, bufs, sems, [], after)[0]


def _allreduce_small(g):
    rows = g.shape[0]
    half = rows // 2

    def body(g_ref, o_ref, sib, slots, send_sems, recv_sems):
        x, y, c, chips = _position()
        me, sibling = (x, y, c), (x, y, 1 - c)
        my_chip = 2 * x + y
        mine = pl.ds(pl.multiple_of(c * half, 8), half)
        theirs = pl.ds(pl.multiple_of((1 - c) * half, 8), half)
        pair = _remote(g_ref.at[theirs], sib, send_sems, recv_sems, 0, sibling)
        pair.start()
        pair.wait()
        slots[my_chip] = g_ref[mine, :] + sib[...]
        sent = []
        for j, (cx, cy) in enumerate(chips):
            cp = _remote(slots.at[my_chip], slots.at[my_chip], send_sems, recv_sems, 1 + j, (cx, cy, c))
            cp.start()
            sent.append(cp)
        for j, (cx, cy) in enumerate(chips):
            got = slots.at[2 * cx + cy]
            _remote(got, got, send_sems, recv_sems, 1 + j, me).wait_recv()
        for cp in sent:
            cp.wait_send()
        o_ref[mine, :] = ((slots[0] + slots[1]) + slots[2]) + slots[3]
        swap = _remote(o_ref.at[mine], o_ref.at[mine], send_sems, recv_sems, 4, sibling)
        swap.start()
        swap.wait()

    vm = pl.BlockSpec(memory_space=pltpu.VMEM)
    return pl.pallas_call(
        body, name="allreduce_small",
        in_specs=[vm], out_specs=vm, out_shape=jax.ShapeDtypeStruct((rows, 128), F32),
        scratch_shapes=[pltpu.VMEM((half, 128), F32), pltpu.VMEM((N_CHIPS, half, 128), F32),
                        pltpu.SemaphoreType.DMA((5,)), pltpu.SemaphoreType.DMA((5,))],
        compiler_params=pltpu.CompilerParams(vmem_limit_bytes=VMEM_LIMIT),
    )(g)


_SMALL =("rel_bias", "ln_v_gain", "ln_v_bias", "w_spatial", "b_spatial", "ln1_gain", "ln1_bias",
          "b_ff1", "b_ff2", "ln2_gain", "ln2_bias")
_SMALL_ROWS = 1200
_LOSS_AT = (152832 // 128, 0)


def _pack_small(parts):
    flat = jnp.concatenate([parts[k].reshape(-1).astype(F32) for k in _SMALL])
    flat = jnp.pad(flat, (0, _SMALL_ROWS * 128 - flat.shape[0]))
    return flat.reshape(_SMALL_ROWS, 128)


def _unpack_small(packed, like):
    flat = packed.reshape(-1)
    out, at = {}, 0
    for k in _SMALL:
        n = math.prod(like[k].shape)
        out[k] = flat[at:at + n].reshape(like[k].shape)
        at += n
    return out


def kernel(x, w_in, rel_bias, ln_v_gain, ln_v_bias, w_spatial, b_spatial, w_proj_a, w_proj_b, w_out, ln1_gain, ln1_bias, w_ff1, b_ff1, w_ff2, b_ff2, ln2_gain, ln2_bias, loss_target, m_w_in, m_rel_bias, m_ln_v_gain, m_ln_v_bias, m_w_spatial, m_b_spatial, m_w_proj_a, m_w_proj_b, m_w_out, m_ln1_gain, m_ln1_bias, m_w_ff1, m_b_ff1, m_w_ff2, m_b_ff2, m_ln2_gain, m_ln2_bias, v_w_in, v_rel_bias, v_ln_v_gain, v_ln_v_bias, v_w_spatial, v_b_spatial, v_w_proj_a, v_w_proj_b, v_w_out, v_ln1_gain, v_ln1_bias, v_w_ff1, v_b_ff1, v_w_ff2, v_b_ff2, v_ln2_gain, v_ln2_bias):
    args = dict(locals())
    big = ("w_in", "w_proj_a", "w_proj_b", "w_out", "w_ff1", "w_ff2")
    weights = ("w_in", "rel_bias", "ln_v_gain", "ln_v_bias", "w_spatial", "b_spatial", "w_proj_a", "w_proj_b", "w_out",
               "ln1_gain", "ln1_bias", "w_ff1", "b_ff1", "w_ff2", "b_ff2", "ln2_gain", "ln2_bias")

    xs = x[0]
    target = loss_target[0]

    ring = {"a": {"bufs": [_place_shard(w_in[0], "place_w_in")], "sems": {}}}
    tok = _ring_call("allgather_a_near", ring, [("start", "a", "ici_near")])
    placed = [_place_shard(args[k][0], f"place_{k}", after=tok) for k in big[1:]]
    for tag, bufs in (("b", placed[0:3]), ("c", placed[3:4]), ("d", placed[4:5])):
        ring[tag] = {"bufs": bufs, "sems": {}}
    xb = _to_bf16(xs, "x_to_bf16", after=placed[4])

    mx, my = lax.axis_index("x"), lax.axis_index("y")
    own = jnp.reshape(2 * mx + my, (1,)).astype(jnp.int32)
    near = jnp.stack([2 * (1 - mx) + my, 2 * mx + (1 - my)]).astype(jnp.int32)
    far = jnp.reshape(2 * (1 - mx) + (1 - my), (1,)).astype(jnp.int32)
    proj = _proj(xb, ring["a"]["bufs"][0], own, "proj_own")
    _ring_call("allgather_a_far", ring, [("wait", "a", "ici_near"), ("start", "a", "ici_far"), ("start", "a", "d2d_near"),
                                         ("start", "b", "ici_near"), ("start", "c", "ici_near")], after=proj)
    _ring_call("allgather_a_near_done", ring, [("wait", "a", "d2d_near")])
    proj = _proj(xb, ring["a"]["bufs"][0], near, "proj_near", into=proj)
    _ring_call("allgather_a_last", ring, [("wait", "a", "ici_far"), ("start", "a", "d2d_far")], after=proj)
    _ring_call("allgather_a_done", ring, [("wait", "a", "d2d_far")])
    (win_g,) = ring["a"]["bufs"]
    proj = _proj(xb, win_g, far, "proj_far", into=proj)
    _ring_call("allgather_b_far", ring, [("wait", "b", "ici_near"), ("start", "b", "ici_far"), ("start", "b", "d2d_near")],
               after=proj)
    ws = w_spatial[0]
    ws_t = jnp.transpose(ws, (0, 2, 1))
    bsp_b = jnp.broadcast_to(b_spatial[0][:, :, None], (NH, 128, 128))
    gmlp = _gmlp_fwd(proj, ws, bsp_b, ln_v_gain, ln_v_bias)
    bias = _bias_tiles(rel_bias)
    attn, lse = _attention_fwd(proj, bias)
    _ring_call("allgather_b_last_c_far", ring,
               [("wait", "b", "ici_far"), ("start", "b", "d2d_far"),
                ("wait", "c", "ici_near"), ("start", "c", "ici_far"), ("start", "c", "d2d_near"),
                ("start", "d", "ici_near")], after=attn)
    _ring_call("allgather_b_done", ring, [("wait", "b", "d2d_near"), ("wait", "b", "d2d_far")])
    wpa_g, wpb_g, wout_g = ring["b"]["bufs"]
    wout_full = wout_g.reshape(D, D)
    ya, yb, merged = _branch(attn, gmlp, wpa_g, wpb_g, proj)
    xhat1, rstd1, h1b = _out_ln1(merged, wout_full, xs, ln1_gain, ln1_bias)
    _ring_call("allgather_c_last_d_far", ring,
               [("wait", "c", "ici_far"), ("start", "c", "d2d_far"),
                ("wait", "d", "ici_near"), ("start", "d", "ici_far"), ("start", "d", "d2d_near")], after=h1b)
    _ring_call("allgather_c_done", ring, [("wait", "c", "d2d_near"), ("wait", "c", "d2d_far")])
    (w1_g,) = ring["c"]["bufs"]
    a, r = _ff1(h1b, w1_g, b_ff1)
    _ring_call("allgather_d_last", ring, [("wait", "d", "ici_far"), ("start", "d", "d2d_far")], after=a)
    _ring_call("allgather_d_done", ring, [("wait", "d", "d2d_near"), ("wait", "d", "d2d_far")])
    (w2_g,) = ring["d"]["bufs"]
    w2_full = w2_g.reshape(DFF, D)
    dpre2, dpre2b, st2 = _ff2_ln2_loss(a, w2_full, xhat1, ln1_gain, ln1_bias, b_ff2, ln2_gain, ln2_bias, target)

    def pair_and_chip(tag, state, after):
        local, from_sibling = _px_wait(f"pair_exchange_wait_{tag}", state, after)
        pair_sums = [_pair_sum(g, o, f"pair_sum_{tag}_{i}") for i, (g, o) in enumerate(zip(local, from_sibling))]
        return _cx_start(f"chip_exchange_start_{tag}", pair_sums)

    g_w2 = _grad_w(a, dpre2b, "grad_w_ff2", 512, 2048, False)
    px, tok = _px_start("pair_exchange_start_w_ff2", [g_w2.reshape(N_CHIPS, DFF // N_CHIPS, D)])
    dprea, g_b1 = _d_ff1(dpre2b, w2_full, r, after=tok)
    cx_w2, tok = pair_and_chip("w_ff2", px, dprea)
    g_w1 = _grad_w(h1b, dprea, "grad_w_ff1", 512, 2048, True, after=tok)
    px, tok = _px_start("pair_exchange_start_w_ff1", [g_w1])
    dpre1, dpre1b, st1 = _d_h1_ln1(dprea, w1_g, dpre2, xhat1, rstd1, ln1_gain, after=tok)
    cx_w1, tok = pair_and_chip("w_ff1", px, dpre1b)
    g_wout = _grad_w(merged, dpre1b, "grad_w_out", 512, 2048, False, after=tok)
    dya, dyb, dga, dgb = _d_merged(dpre1b, wout_full, proj, ya, yb)
    g_wpa = _grad_w(attn, dya, "grad_w_proj_a", 1024, 512, True)
    g_wpb = _grad_w(gmlp, dyb, "grad_w_proj_b", 1024, 512, True)
    px, tok = _px_start("pair_exchange_start_b", [g_wpa, g_wpb, g_wout.reshape(N_CHIPS, D // N_CHIPS, D)])
    dattn, dgmlp = _d_branches(dya, dyb, wpa_g, wpb_g, after=tok)
    duv, g_ws, g_bs, stv = _gmlp_bwd(proj, dgmlp, ws, ws_t, bsp_b, ln_v_gain, ln_v_bias)
    cx_b, tok = pair_and_chip("b", px, duv)
    dq, dk, dv, ds_sums = _attention_bwd(proj, dattn, attn, lse, bias, after=tok)
    g_rb = _rel_bias_grad(ds_sums)[:, :NH]

    small_g = dict(rel_bias=g_rb, ln_v_gain=stv[0], ln_v_bias=stv[1], w_spatial=g_ws, b_spatial=g_bs[:, :, 0],
                   ln1_gain=st1[0], ln1_bias=st1[1], b_ff1=g_b1, b_ff2=st2[2], ln2_gain=st2[0], ln2_bias=st2[1])
    gs = _allreduce_small(_pack_small(small_g).at[_LOSS_AT].set(st2[3, 0]))
    ds_, ms_, vs_, _ = _adamw(_pack_small({k: args[k] for k in _SMALL}), gs,
                           _pack_small({k: args["m_" + k] for k in _SMALL}),
                           _pack_small({k: args["v_" + k] for k in _SMALL}), "adamw_small")
    like = {k: args[k] for k in _SMALL}
    grads, deltas, new_m, new_v = (_unpack_small(t, like) for t in (gs, ds_, ms_, vs_))

    dproj = jnp.concatenate([dq, dk, dv, duv, dga, dgb], axis=1)
    g_win = _grad_w(xb, dproj, "grad_w_in", 512, 2304, True, after=gs)
    px, tok = _px_start("pair_exchange_start_w_in", [g_win])

    def chip_sums(tag, state, names, after):
        pair_sums, from_chips = _cx_wait(f"chip_exchange_wait_{tag}", state, after)
        halves = [_chip_sum(p, own, f"chip_sum_{k}") for p, own, k in zip(from_chips, pair_sums, names)]
        return _share_start(f"share_start_{tag}", halves)

    def adam(tag, state, names, after):
        last = None
        for k, g in zip(names, _share_wait(f"share_wait_{tag}", state, after)):
            d_, m_, v_, g_ = _adamw(args[k][0], g, args["m_" + k][0], args["v_" + k][0], f"adamw_{k}")
            grads[k], deltas[k], new_m[k], new_v[k] = g_[None], d_[None], m_[None], v_[None]
            last = d_
        return last

    sh_w2, tok = chip_sums("w_ff2", cx_w2, ["w_ff2"], tok)
    sh_w1, tok = chip_sums("w_ff1", cx_w1, ["w_ff1"], tok)
    sh_b, tok = chip_sums("b", cx_b, ["w_proj_a", "w_proj_b", "w_out"], tok)
    cx_in, tok = pair_and_chip("w_in", px, tok)
    grad_x = _d_x(dproj, win_g, dpre1, after=tok)
    done = adam("w_ff2", sh_w2, ["w_ff2"], grad_x)
    done = adam("w_ff1", sh_w1, ["w_ff1"], done)
    done = adam("b", sh_b, ["w_proj_a", "w_proj_b", "w_out"], done)
    sh_in, tok = chip_sums("w_in", cx_in, ["w_in"], done)
    adam("w_in", sh_in, ["w_in"], tok)

    loss = gs[_LOSS_AT] * (0.5 / D)
    return (loss, grad_x[None], *[grads[k] for k in weights], *[deltas[k] for k in weights],
            *[new_m[k] for k in weights], *[new_v[k] for k in weights])
```

```python
import math

import numpy as np
import jax
import jax.numpy as jnp
from jax import lax
from jax.experimental import pallas as pl
from jax.experimental.pallas import tpu as pltpu

F32 = jnp.float32
BF16 = jnp.bfloat16

S = 2048
D = 2048
DA = 1024
DB = 1024
DFF = 8192
DIN = 9216
NH = 8
HD = 128
NBLK = 16
PATTERNS = ((128, 1), (512, 4), (2048, 16))
N_BUCKETS = 32
MAX_DISTANCE = 2048
ALPHA = 2.0 ** 0.25
LN_EPS = 1e-5
NEG_INF = -1e30
SCALE = HD ** -0.5
N_CHIPS = 4

ADAM_LR = 0.001
ADAM_B1 = 0.9
ADAM_B2 = 0.999
ADAM_EPS = 1e-08
ADAM_WD = 0.01
ADAM_STEP = 10

VMEM_LIMIT = 56 * 1024 * 1024
MESH = pl.DeviceIdType.MESH
ANY = pl.BlockSpec(memory_space=pl.ANY)


def _params(n_axes, vmem=VMEM_LIMIT):
    return pltpu.CompilerParams(dimension_semantics=("arbitrary",) * n_axes, vmem_limit_bytes=vmem)


def _bucket_tile(dilation):
    qi = np.arange(128)[:, None]
    kj = np.arange(256)[None, :]
    n = np.clip(128 + qi - kj, 0, 128) * dilation
    max_exact = N_BUCKETS // 2
    nf = np.maximum(n, 1).astype(np.float32)
    large = max_exact + (np.log(nf / np.float32(max_exact)) / np.float32(math.log(MAX_DISTANCE / max_exact))
                         * np.float32(N_BUCKETS - max_exact)).astype(np.int32)
    large = np.minimum(large, N_BUCKETS - 1)
    return np.where(n < max_exact, n, large).astype(np.int32)


def _gelu(x):
    c = math.sqrt(2.0 / math.pi)
    t = jnp.tanh(c * (x + 0.044715 * x * x * x))
    return 0.5 * x * (1.0 + t), t


def _gelu_grad(x, t):
    c = math.sqrt(2.0 / math.pi)
    return 0.5 * (1.0 + t) + 0.5 * x * (1.0 - t * t) * c * (1.0 + 3.0 * 0.044715 * x * x)


def _sigmoid(x):
    return 1.0 / (1.0 + jnp.exp(-x))


def _dot(a, b):
    return jnp.dot(a, b, preferred_element_type=F32)


def _behind(body, n_in, after):
    if after is None:
        return body, [], []
    return (lambda *refs: body(*refs[:n_in], *refs[n_in + 1:])), [ANY], [after]


def _dot_nt(a, b):
    return lax.dot_general(a, b, (((1,), (1,)), ((), ())), preferred_element_type=F32)


def _proj(xb, win_g, shards, name, into=None):
    tn = 768
    per = 2304 // tn

    def body(shards_ref, x_ref, w_ref, *rest):
        rest[-1][...] = _dot(x_ref[...], w_ref[...])

    in_specs = [pl.BlockSpec((S, D), lambda j, sh: (0, 0)),
                pl.BlockSpec((None, D, tn), lambda j, sh: (sh[j // per], 0, j % per))]
    return pl.pallas_call(
        body, name=name,
        grid_spec=pltpu.PrefetchScalarGridSpec(
            num_scalar_prefetch=1, grid=(shards.shape[0] * per,),
            in_specs=in_specs + ([ANY] if into is not None else []),
            out_specs=pl.BlockSpec((S, tn), lambda j, sh: (0, sh[j // per] * per + j % per))),
        out_shape=jax.ShapeDtypeStruct((S, DIN), F32),
        input_output_aliases={3: 0} if into is not None else {},
        compiler_params=_params(1),
    )(shards, xb, win_g, *([into] if into is not None else []))


FWD_HEADS_PER_STEP = 4
BWD_HEADS_PER_STEP = 2


def _bias_tiles(rel_bias):
    buckets = jnp.asarray(np.stack([_bucket_tile(d) for _, d in PATTERNS]))

    def body(rb_ref, bk_ref, o_ref):
        qi = lax.broadcasted_iota(jnp.int32, (128, 256), 0)
        kj = lax.broadcasted_iota(jnp.int32, (128, 256), 1)
        steps = 128 + qi - kj
        band = (steps >= 0) & (steps <= 128)
        o_ref[...] = jnp.zeros_like(o_ref)
        for p in range(len(PATTERNS)):
            bucket = bk_ref[p]

            def one_bucket(t, carry):
                hit = bucket == t
                for h in range(NH):
                    o_ref[p, h] = jnp.where(hit, rb_ref[t, h], o_ref[p, h])
                return carry

            lax.fori_loop(0, N_BUCKETS, one_bucket, 0)
            for h in range(NH):
                o_ref[p, h] = jnp.where(band, o_ref[p, h], NEG_INF)

    return pl.pallas_call(
        body, name="bias_tiles",
        in_specs=[pl.BlockSpec(memory_space=pltpu.SMEM), pl.BlockSpec(memory_space=pltpu.VMEM)],
        out_specs=pl.BlockSpec(memory_space=pltpu.VMEM),
        out_shape=jax.ShapeDtypeStruct((len(PATTERNS), NH, 128, 256), F32),
        compiler_params=pltpu.CompilerParams(vmem_limit_bytes=VMEM_LIMIT),
    )(rel_bias, buckets)


def _block_rows(b, dilation):
    nblk = NBLK // dilation
    r, n = b // nblk, b % nblk
    start = r + n * (128 * dilation)
    prev_start = jnp.maximum(start - 128 * dilation, r)
    if dilation == 1:
        return pl.ds(pl.multiple_of(start, 128), 128), pl.ds(pl.multiple_of(prev_start, 128), 128), n > 0
    return pl.ds(start, 128, stride=dilation), pl.ds(prev_start, 128, stride=dilation), n > 0


def _head_specs(first, hps):
    return [pl.BlockSpec((S, HD), lambda g, j=j: (0, first + g * hps + j)) for j in range(hps)]


def _bias_spec(hps):
    return pl.BlockSpec((len(PATTERNS), hps, 128, 256), lambda g: (0, g, 0, 0))


def _heads_spec(hps):
    return pl.BlockSpec((S, hps * HD), lambda g: (0, g))


def _attention_fwd(proj, bias):
    hps = FWD_HEADS_PER_STEP

    def body(bias_ref, *refs):
        q_refs, k_refs, v_refs = (refs[i * hps:(i + 1) * hps] for i in range(3))
        o_ref, lse_ref = refs[3 * hps:3 * hps + 2]
        acc_scrs, m_scrs, l_scrs = (refs[3 * hps + 2 + i * hps:3 * hps + 2 + (i + 1) * hps] for i in range(3))
        kj = lax.broadcasted_iota(jnp.int32, (128, 256), 1)
        for p, (_, d) in enumerate(PATTERNS):
            prev_blocks = NBLK // d > 1

            def block(b, carry):
                units = [(j,) + _block_rows(blk, d) for blk in (b, b + NBLK // 2) for j in range(hps)]
                scores = []
                for j, rows, prows, _ in units:
                    q = q_refs[j][rows, :].astype(BF16)
                    cur = _dot_nt(q, k_refs[j][rows, :].astype(BF16))
                    if prev_blocks:
                        cur = jnp.concatenate([_dot_nt(q, k_refs[j][prows, :].astype(BF16)), cur], axis=1)
                    scores.append(cur)
                soft = []
                for u, (j, _, _, has_prev) in enumerate(units):
                    if prev_blocks:
                        s = jnp.where((kj >= 128) | has_prev, scores[u] * SCALE + bias_ref[p, j], NEG_INF)
                    else:
                        s = scores[u] * SCALE + bias_ref[p, j, :, 128:256]
                    m = jnp.max(s, axis=1, keepdims=True)
                    e = jnp.exp(s - m)
                    soft.append((m, jnp.sum(e, axis=1, keepdims=True), e.astype(BF16)))
                outs = []
                for u, (j, rows, prows, _) in enumerate(units):
                    e = soft[u][2]
                    if prev_blocks:
                        outs.append(_dot(e[:, :128], v_refs[j][prows, :].astype(BF16))
                                    + _dot(e[:, 128:], v_refs[j][rows, :].astype(BF16)))
                    else:
                        outs.append(_dot(e, v_refs[j][rows, :].astype(BF16)))
                for u, (j, rows, _, _) in enumerate(units):
                    acc_scr, m_scr, l_scr = acc_scrs[j], m_scrs[j], l_scrs[j]
                    (m, den, _), o = soft[u], outs[u]
                    if p == 0:
                        acc_scr[rows, :] = o
                        m_scr[rows, :] = jnp.broadcast_to(m, (128, HD))
                        l_scr[rows, :] = jnp.broadcast_to(den, (128, HD))
                    else:
                        m_old = m_scr[rows, :]
                        m_new = jnp.maximum(m_old, m)
                        w_old, w_new = jnp.exp(m_old - m_new), jnp.exp(m - m_new)
                        acc_scr[rows, :] = acc_scr[rows, :] * w_old + o * w_new
                        l_scr[rows, :] = l_scr[rows, :] * w_old + den * w_new
                        m_scr[rows, :] = m_new
                return carry

            lax.fori_loop(0, NBLK // 2, block, 0)
        for j in range(hps):
            cols = slice(j * HD, (j + 1) * HD)
            den = l_scrs[j][...]
            o_ref[:, cols] = (acc_scrs[j][...] / den).astype(BF16)
            lse_ref[:, cols] = m_scrs[j][...] + jnp.log(den)

    return pl.pallas_call(
        body, name="attention_fwd", grid=(NH // hps,),
        in_specs=[_bias_spec(hps)] + _head_specs(0, hps) + _head_specs(NH, hps) + _head_specs(2 * NH, hps),
        out_specs=[_heads_spec(hps), _heads_spec(hps)],
        out_shape=[jax.ShapeDtypeStruct((S, DA), BF16), jax.ShapeDtypeStruct((S, DA), F32)],
        scratch_shapes=[pltpu.VMEM((S, HD), F32)] * (3 * hps),
        compiler_params=_params(1),
    )(bias, *([proj] * (3 * hps)))


def _attention_bwd(proj, dattn, attn, lse, bias, after=None):
    hps = BWD_HEADS_PER_STEP

    def body(bias_ref, *refs):
        q_refs, k_refs, v_refs, do_refs, o_refs, lse_refs = (refs[i * hps:(i + 1) * hps] for i in range(6))
        dq_ref, dk_ref, dv_ref, ds_ref = refs[6 * hps:6 * hps + 4]
        dl_scrs, dq_scrs, dk_scrs, dv_scrs = (refs[6 * hps + 4 + i * hps:6 * hps + 4 + (i + 1) * hps] for i in range(4))
        ds_ref[...] = jnp.zeros_like(ds_ref)
        for j in range(hps):
            dq_scrs[j][...] = jnp.zeros((S, HD), F32)
            dk_scrs[j][...] = jnp.zeros((S, HD), F32)
            dv_scrs[j][...] = jnp.zeros((S, HD), F32)
            prod = do_refs[j][...] * o_refs[j][...].astype(F32)
            dl_scrs[j][...] = jnp.broadcast_to(jnp.sum(prod, axis=1, keepdims=True), (S, HD))
        for p, (_, d) in enumerate(PATTERNS):
            prev_blocks = NBLK // d > 1

            def block(b, carry):
                units = [(j,) + _block_rows(b + i * (NBLK // 4), d) for i in range(4) for j in range(hps)]
                ops, raw = [], []
                for j, rows, prows, _ in units:
                    q, do = q_refs[j][rows, :].astype(BF16), do_refs[j][rows, :].astype(BF16)
                    kc, vc = k_refs[j][rows, :].astype(BF16), v_refs[j][rows, :].astype(BF16)
                    if prev_blocks:
                        kp, vp = k_refs[j][prows, :].astype(BF16), v_refs[j][prows, :].astype(BF16)
                        ops.append((q, do, kc, kp))
                        raw.append((_dot_nt(q, kc), _dot_nt(do, vc), _dot_nt(q, kp), _dot_nt(do, vp)))
                    else:
                        ops.append((q, do, kc))
                        raw.append((_dot_nt(q, kc), _dot_nt(do, vc)))
                probs = []
                for u, (j, rows, _, has_prev) in enumerate(units):
                    lse_b, dl_b = lse_refs[j][rows, :], dl_scrs[j][rows, :]
                    p_c = jnp.exp(raw[u][0] * SCALE + bias_ref[p, j, :, 128:256] - lse_b)
                    ds_c = p_c * (raw[u][1] - dl_b)
                    ds_ref[p, j, :, 128:256] += ds_c
                    if prev_blocks:
                        p_p = jnp.where(has_prev, jnp.exp(raw[u][2] * SCALE + bias_ref[p, j, :, 0:128] - lse_b), 0.0)
                        ds_p = p_p * (raw[u][3] - dl_b)
                        ds_ref[p, j, :, 0:128] += ds_p
                        probs.append((p_c, ds_c, p_p, ds_p))
                    else:
                        probs.append((p_c, ds_c))
                grads = []
                for u in range(len(units)):
                    q, do, kc = ops[u][:3]
                    p_c, ds_c = probs[u][:2]
                    dq = _dot(ds_c.astype(BF16), kc)
                    cur = (_dot(ds_c.T.astype(BF16), q) * SCALE, _dot(p_c.T.astype(BF16), do))
                    if prev_blocks:
                        p_p, ds_p = probs[u][2:]
                        dq = dq + _dot(ds_p.astype(BF16), ops[u][3])
                        cur = cur + (_dot(ds_p.T.astype(BF16), q) * SCALE, _dot(p_p.T.astype(BF16), do))
                    grads.append((dq * SCALE,) + cur)
                for u, (j, rows, prows, _) in enumerate(units):
                    dq_scrs[j][rows, :] += grads[u][0]
                    dk_scrs[j][rows, :] += grads[u][1]
                    dv_scrs[j][rows, :] += grads[u][2]
                    if prev_blocks:
                        dk_scrs[j][prows, :] += grads[u][3]
                        dv_scrs[j][prows, :] += grads[u][4]
                return carry

            lax.fori_loop(0, NBLK // 4, block, 0)
        for j in range(hps):
            cols = slice(j * HD, (j + 1) * HD)
            dq_ref[:, cols] = dq_scrs[j][...].astype(BF16)
            dk_ref[:, cols] = dk_scrs[j][...].astype(BF16)
            dv_ref[:, cols] = dv_scrs[j][...].astype(BF16)

    body, more_specs, more = _behind(body, 1 + 6 * hps, after)
    return pl.pallas_call(
        body, name="attention_bwd", grid=(NH // hps,),
        in_specs=[_bias_spec(hps)]
        + _head_specs(0, hps) + _head_specs(NH, hps) + _head_specs(2 * NH, hps) + 3 * _head_specs(0, hps)
        + more_specs,
        out_specs=3 * [_heads_spec(hps)] + [pl.BlockSpec((3, hps, 128, 256), lambda g: (0, g, 0, 0))],
        out_shape=[jax.ShapeDtypeStruct((S, DA), BF16)] * 3 + [jax.ShapeDtypeStruct((3, NH, 128, 256), F32)],
        scratch_shapes=[pltpu.VMEM((S, HD), F32)] * (4 * hps),
        compiler_params=_params(1),
    )(bias, *([proj] * (3 * hps)), *([dattn] * hps), *([attn] * hps), *([lse] * hps), *more)


def _gmlp_parts(u_ref, vb_ref, g_ref, be_ref):
    u = u_ref[...]
    u_act, tu = _gelu(u)
    vb = vb_ref[...]
    gv, tv = _gelu(vb)
    mean = jnp.mean(gv, axis=1, keepdims=True)
    cen = gv - mean
    var = jnp.mean(cen * cen, axis=1, keepdims=True)
    rstd = lax.rsqrt(var + LN_EPS)
    xhat = cen * rstd
    vn = xhat * g_ref[...] + be_ref[...]
    return u, tu, u_act, vb, tv, rstd, xhat, vn


def _gmlp_fwd(proj, ws, bsp_b, gain_v, bias_v):
    def body(u_ref, vb_ref, ws_ref, bsp_ref, g_ref, be_ref, o_ref):
        _, _, u_act, _, _, _, _, vn = _gmlp_parts(u_ref, vb_ref, g_ref, be_ref)
        row = lax.broadcasted_iota(jnp.int32, (128, 128), 0)
        col = lax.broadcasted_iota(jnp.int32, (128, 128), 1)
        causal = row >= col
        for g in range(NH):
            cols = slice(g * 128, (g + 1) * 128)
            wsg = jnp.where(causal, ws_ref[g], 0.0).astype(BF16)
            z = _dot(wsg, vn[:, cols].astype(BF16)) + bsp_ref[g]
            o_ref[:, cols] = (u_act[:, cols] * z).astype(BF16)

    return pl.pallas_call(
        body, name="gmlp_fwd", grid=(NBLK,),
        in_specs=[pl.BlockSpec((128, DB), lambda c: (c, 3)), pl.BlockSpec((128, DB), lambda c: (c, 4)),
                  pl.BlockSpec((NH, 128, 128), lambda c: (0, 0, 0)), pl.BlockSpec((NH, 128, 128), lambda c: (0, 0, 0)),
                  pl.BlockSpec((1, DB), lambda c: (0, 0)), pl.BlockSpec((1, DB), lambda c: (0, 0))],
        out_specs=pl.BlockSpec((128, DB), lambda c: (c, 0)),
        out_shape=jax.ShapeDtypeStruct((S, DB), BF16),
        compiler_params=_params(1),
    )(proj, proj, ws, bsp_b, gain_v, bias_v)


def _branch(attn, gmlp, wpa_g, wpb_g, proj):
    tn = 512

    def body(a_ref, g_ref, wa_ref, wb_ref, ga_ref, gb_ref, ya_ref, yb_ref, mg_ref):
        ya = _dot(a_ref[...], wa_ref[...])
        yb = _dot(g_ref[...], wb_ref[...])
        ya_ref[...] = ya.astype(BF16)
        yb_ref[...] = yb.astype(BF16)
        mg_ref[...] = (_sigmoid(ga_ref[...]) * ya + _sigmoid(gb_ref[...]) * yb).astype(BF16)

    out = pl.BlockSpec((S, tn), lambda j: (0, j))
    return pl.pallas_call(
        body, name="branch", grid=(D // tn,),
        in_specs=[pl.BlockSpec((S, DA), lambda j: (0, 0)), pl.BlockSpec((S, DB), lambda j: (0, 0)),
                  pl.BlockSpec((None, DA, tn), lambda j: (j, 0, 0)), pl.BlockSpec((None, DB, tn), lambda j: (j, 0, 0)),
                  pl.BlockSpec((S, tn), lambda j: (0, 5120 // tn + j)), pl.BlockSpec((S, tn), lambda j: (0, 7168 // tn + j))],
        out_specs=[out, out, out],
        out_shape=[jax.ShapeDtypeStruct((S, D), BF16)] * 3,
        compiler_params=_params(1),
    )(attn, gmlp, wpa_g, wpb_g, proj, proj)


def _out_ln1(merged, wout_g, x, gain, bias):
    tm = 256

    def body(m_ref, w_ref, x_ref, g_ref, b_ref, xh_ref, rs_ref, h_ref):
        pre = ALPHA * x_ref[...] + _dot(m_ref[...], w_ref[...])
        mean = jnp.mean(pre, axis=1, keepdims=True)
        cen = pre - mean
        var = jnp.mean(cen * cen, axis=1, keepdims=True)
        rstd = lax.rsqrt(var + LN_EPS)
        xhat = cen * rstd
        xh_ref[...] = xhat
        rs_ref[...] = jnp.broadcast_to(rstd, (tm, 128))
        h_ref[...] = (xhat * g_ref[...] + b_ref[...]).astype(BF16)

    row = pl.BlockSpec((tm, D), lambda i: (i, 0))
    vec = pl.BlockSpec((1, D), lambda i: (0, 0))
    return pl.pallas_call(
        body, name="out_ln1", grid=(S // tm,),
        in_specs=[row, pl.BlockSpec((D, D), lambda i: (0, 0)), row, vec, vec],
        out_specs=[row, pl.BlockSpec((tm, 128), lambda i: (i, 0)), row],
        out_shape=[jax.ShapeDtypeStruct((S, D), F32), jax.ShapeDtypeStruct((S, 128), F32),
                   jax.ShapeDtypeStruct((S, D), BF16)],
        compiler_params=_params(1),
    )(merged, wout_g, x, gain, bias)


def _ff1(h1b, w1_g, b1, half, name, into=None, after=None):
    tn = 512
    per = D // tn
    steps = DFF // tn // 2
    first = half * steps

    def body(h_ref, w_ref, b_ref, *rest):
        a_ref, r_ref = rest[-2:]
        r = jnp.maximum(_dot(h_ref[...], w_ref[...]) + b_ref[...], 0.0)
        r_ref[...] = r.astype(BF16)
        a_ref[...] = (r * r).astype(BF16)

    out = pl.BlockSpec((S, tn), lambda j: (0, first + j))
    extra = list(into) if into is not None else []
    if after is not None:
        extra.append(after)
    return pl.pallas_call(
        body, name=name, grid=(steps,),
        in_specs=[pl.BlockSpec((S, D), lambda j: (0, 0)),
                  pl.BlockSpec((None, D, tn), lambda j: ((first + j) // per, 0, (first + j) % per)),
                  pl.BlockSpec((1, tn), lambda j: (0, first + j))] + [ANY] * len(extra),
        out_specs=[out, out],
        out_shape=[jax.ShapeDtypeStruct((S, DFF), BF16)] * 2,
        input_output_aliases={3: 0, 4: 1} if into is not None else {},
        compiler_params=_params(1),
    )(h1b, w1_g, b1, *extra)


def _ff2_ln2_loss(a, w2_g, xhat1, g1, b1, b2, g2, be2, target):
    tm, tk = 512, 1024
    nk = DFF // tk

    def body(a_ref, w_ref, xh_ref, g1_ref, b1_ref, b2_ref, g2_ref, be2_ref, t_ref, d_ref, db_ref, st_ref, acc):
        i, k = pl.program_id(0), pl.program_id(1)

        @pl.when(k == 0)
        def _():
            acc[...] = jnp.zeros_like(acc)

        @pl.when((i == 0) & (k == 0))
        def _():
            st_ref[...] = jnp.zeros_like(st_ref)

        acc[...] += _dot(a_ref[...], w_ref[...])

        @pl.when(k == nk - 1)
        def _():
            def rows_chunk(ci, carry):
                rows = pl.ds(pl.multiple_of(ci * 128, 128), 128)
                h1 = xh_ref[rows, :] * g1_ref[...] + b1_ref[...]
                pre = ALPHA * h1 + acc[rows, :] + b2_ref[...]
                mean = jnp.mean(pre, axis=1, keepdims=True)
                cen = pre - mean
                var = jnp.mean(cen * cen, axis=1, keepdims=True)
                rstd = lax.rsqrt(var + LN_EPS)
                xhat = cen * rstd
                y = xhat * g2_ref[...] + be2_ref[...]
                err = y - t_ref[rows, :]
                dy = err * (1.0 / D)
                g = dy * g2_ref[...]
                dpre = rstd * (g - jnp.mean(g, axis=1, keepdims=True)
                               - xhat * jnp.mean(g * xhat, axis=1, keepdims=True))
                d_ref[rows, :] = dpre
                db_ref[rows, :] = dpre.astype(BF16)
                st_ref[0:1, :] += jnp.sum(dy * xhat, axis=0, keepdims=True)
                st_ref[1:2, :] += jnp.sum(dy, axis=0, keepdims=True)
                st_ref[2:3, :] += jnp.sum(dpre, axis=0, keepdims=True)
                st_ref[3:4, :] += jnp.broadcast_to(jnp.sum(err * err).reshape(1, 1), (1, D))
                return carry

            lax.fori_loop(0, tm // 128, rows_chunk, 0)

    row = pl.BlockSpec((tm, D), lambda i, k: (i, 0))
    vec = pl.BlockSpec((1, D), lambda i, k: (0, 0))
    return pl.pallas_call(
        body, name="ff2_ln2_loss", grid=(S // tm, nk),
        in_specs=[pl.BlockSpec((tm, tk), lambda i, k: (i, k)), pl.BlockSpec((tk, D), lambda i, k: (k, 0)),
                  row, vec, vec, vec, vec, vec, row],
        out_specs=[row, row, pl.BlockSpec((8, D), lambda i, k: (0, 0))],
        out_shape=[jax.ShapeDtypeStruct((S, D), F32), jax.ShapeDtypeStruct((S, D), BF16),
                   jax.ShapeDtypeStruct((8, D), F32)],
        scratch_shapes=[pltpu.VMEM((tm, D), F32)],
        compiler_params=_params(2),
    )(a, w2_g, xhat1, g1, b1, b2, g2, be2, target)


def _grad_w(act, dout, name, ti, tj, sharded, after=None):
    m, n = act.shape[1], dout.shape[1]
    ns = n // N_CHIPS
    per = ns // tj if sharded else None

    def body(a_ref, b_ref, o_ref, at_scr):
        @pl.when(pl.program_id(1) == 0)
        def _():
            at_scr[...] = a_ref[...].T

        o_ref[...] = _dot(at_scr[...], b_ref[...]).astype(BF16)

    if sharded:
        out_spec = pl.BlockSpec((None, ti, tj), lambda i, j: (j // per, i, j % per))
        out_shape = jax.ShapeDtypeStruct((N_CHIPS, m, ns), BF16)
    else:
        out_spec = pl.BlockSpec((ti, tj), lambda i, j: (i, j))
        out_shape = jax.ShapeDtypeStruct((m, n), BF16)
    body, more_specs, more = _behind(body, 2, after)
    return pl.pallas_call(
        body, name=name, grid=(m // ti, n // tj),
        in_specs=[pl.BlockSpec((S, ti), lambda i, j: (0, i)), pl.BlockSpec((S, tj), lambda i, j: (0, j))] + more_specs,
        out_specs=out_spec, out_shape=out_shape,
        scratch_shapes=[pltpu.VMEM((ti, S), BF16)],
        compiler_params=_params(2),
    )(act, dout, *more)


def _d_ff1(dpre2b, w2_g, r, after=None):
    tn = 512

    def body(d_ref, w_ref, r_ref, o_ref, gb_ref):
        da = _dot_nt(d_ref[...], w_ref[...])
        dp = da * (2.0 * r_ref[...].astype(F32))
        o_ref[...] = dp.astype(BF16)
        gb_ref[...] = jnp.sum(dp, axis=0, keepdims=True)

    body, more_specs, more = _behind(body, 3, after)
    return pl.pallas_call(
        body, name="d_ff1", grid=(DFF // tn,),
        in_specs=[pl.BlockSpec((S, D), lambda j: (0, 0)), pl.BlockSpec((tn, D), lambda j: (j, 0)),
                  pl.BlockSpec((S, tn), lambda j: (0, j))] + more_specs,
        out_specs=[pl.BlockSpec((S, tn), lambda j: (0, j)), pl.BlockSpec((1, tn), lambda j: (0, j))],
        out_shape=[jax.ShapeDtypeStruct((S, DFF), BF16), jax.ShapeDtypeStruct((1, DFF), F32)],
        compiler_params=_params(1),
    )(dpre2b, w2_g, r, *more)


def _d_h1_ln1(dprea, w1_g, dpre2, xhat1, rstd1, g1, after=None):
    tm, tk = 512, 1024
    per = D // tk
    nk = DFF // tk

    def body(a_ref, w_ref, d2_ref, xh_ref, rs_ref, g_ref, d_ref, db_ref, st_ref, acc):
        i, k = pl.program_id(0), pl.program_id(1)

        @pl.when(k == 0)
        def _():
            acc[...] = jnp.zeros_like(acc)

        @pl.when((i == 0) & (k == 0))
        def _():
            st_ref[...] = jnp.zeros_like(st_ref)

        acc[...] += _dot_nt(a_ref[...], w_ref[...])

        @pl.when(k == nk - 1)
        def _():
            def rows_chunk(ci, carry):
                rows = pl.ds(pl.multiple_of(ci * 128, 128), 128)
                dh = ALPHA * d2_ref[rows, :] + acc[rows, :]
                xhat = xh_ref[rows, :]
                g = dh * g_ref[...]
                dpre = rs_ref[rows, 0:1] * (g - jnp.mean(g, axis=1, keepdims=True)
                                            - xhat * jnp.mean(g * xhat, axis=1, keepdims=True))
                d_ref[rows, :] = dpre
                db_ref[rows, :] = dpre.astype(BF16)
                st_ref[0:1, :] += jnp.sum(dh * xhat, axis=0, keepdims=True)
                st_ref[1:2, :] += jnp.sum(dh, axis=0, keepdims=True)
                return carry

            lax.fori_loop(0, tm // 128, rows_chunk, 0)

    row = pl.BlockSpec((tm, D), lambda i, k: (i, 0))
    body, more_specs, more = _behind(body, 6, after)
    return pl.pallas_call(
        body, name="d_h1_ln1", grid=(S // tm, nk),
        in_specs=[pl.BlockSpec((tm, tk), lambda i, k: (i, k)),
                  pl.BlockSpec((None, D, tk), lambda i, k: (k // per, 0, k % per)),
                  row, row, pl.BlockSpec((tm, 128), lambda i, k: (i, 0)), pl.BlockSpec((1, D), lambda i, k: (0, 0))]
        + more_specs,
        out_specs=[row, row, pl.BlockSpec((8, D), lambda i, k: (0, 0))],
        out_shape=[jax.ShapeDtypeStruct((S, D), F32), jax.ShapeDtypeStruct((S, D), BF16),
                   jax.ShapeDtypeStruct((8, D), F32)],
        scratch_shapes=[pltpu.VMEM((tm, D), F32)],
        compiler_params=_params(2),
    )(dprea, w1_g, dpre2, xhat1, rstd1, g1, *more)


def _d_merged(dpre1b, wout_g, proj, ya, yb):
    tm, tn = 512, 1024

    def body(d_ref, w_ref, ga_ref, gb_ref, ya_ref, yb_ref, dya_ref, dyb_ref, dga_ref, dgb_ref):
        dm = _dot_nt(d_ref[...], w_ref[...])
        sa = _sigmoid(ga_ref[...])
        sb = _sigmoid(gb_ref[...])
        dya_ref[...] = (dm * sa).astype(BF16)
        dyb_ref[...] = (dm * sb).astype(BF16)
        dga_ref[...] = (dm * ya_ref[...].astype(F32) * sa * (1.0 - sa)).astype(BF16)
        dgb_ref[...] = (dm * yb_ref[...].astype(F32) * sb * (1.0 - sb)).astype(BF16)

    tile = pl.BlockSpec((tm, tn), lambda i, j: (i, j))
    return pl.pallas_call(
        body, name="d_merged", grid=(S // tm, D // tn),
        in_specs=[pl.BlockSpec((tm, D), lambda i, j: (i, 0)), pl.BlockSpec((tn, D), lambda i, j: (j, 0)),
                  pl.BlockSpec((tm, tn), lambda i, j: (i, 5 + j)), pl.BlockSpec((tm, tn), lambda i, j: (i, 7 + j)),
                  tile, tile],
        out_specs=[tile] * 4,
        out_shape=[jax.ShapeDtypeStruct((S, D), BF16)] * 4,
        compiler_params=_params(2),
    )(dpre1b, wout_g, proj, proj, ya, yb)


def _d_branches(dya, dyb, wpa_g, wpb_g, after=None):
    tm = 512
    ws = D // N_CHIPS

    def body(da_ref, db_ref, wa_ref, wb_ref, oa_ref, ob_ref):
        for d_ref, w_ref, o_ref in ((da_ref, wa_ref, oa_ref), (db_ref, wb_ref, ob_ref)):
            acc = _dot_nt(d_ref[:, 0:ws], w_ref[0])
            for s in range(1, N_CHIPS):
                acc = acc + _dot_nt(d_ref[:, s * ws:(s + 1) * ws], w_ref[s])
            o_ref[...] = acc

    rows = lambda width: pl.BlockSpec((tm, width), lambda i: (i, 0))
    whole = lambda n: pl.BlockSpec((N_CHIPS, n, ws), lambda i: (0, 0, 0))
    body, more_specs, more = _behind(body, 4, after)
    return pl.pallas_call(
        body, name="d_branches", grid=(S // tm,),
        in_specs=[rows(D), rows(D), whole(DA), whole(DB)] + more_specs,
        out_specs=[rows(DA), rows(DB)],
        out_shape=[jax.ShapeDtypeStruct((S, DA), F32), jax.ShapeDtypeStruct((S, DB), F32)],
        compiler_params=_params(1),
    )(dya, dyb, wpa_g, wpb_g, *more)


def _gmlp_bwd(proj, dgmlp, ws, ws_t, bsp_b, gain_v, bias_v):
    def body(u_ref, vb_ref, dg_ref, ws_ref, wst_ref, bsp_ref, g_ref, be_ref, duv_ref, gws_ref, gbs_ref, st_ref):
        @pl.when(pl.program_id(0) == 0)
        def _():
            gws_ref[...] = jnp.zeros_like(gws_ref)
            gbs_ref[...] = jnp.zeros_like(gbs_ref)
            st_ref[...] = jnp.zeros_like(st_ref)

        u, tu, u_act, vb, tv, rstd, xhat, vn = _gmlp_parts(u_ref, vb_ref, g_ref, be_ref)
        dg = dg_ref[...]
        dz = dg * u_act
        row = lax.broadcasted_iota(jnp.int32, (128, 128), 0)
        col = lax.broadcasted_iota(jnp.int32, (128, 128), 1)
        causal = row >= col
        causal_t = row <= col
        dvn_parts = []
        z_parts = []
        for g in range(NH):
            cols = slice(g * 128, (g + 1) * 128)
            vng = vn[:, cols].astype(BF16)
            dzg = dz[:, cols]
            dzb = dzg.astype(BF16)
            wsg = jnp.where(causal, ws_ref[g], 0.0).astype(BF16)
            wsg_t = jnp.where(causal_t, wst_ref[g], 0.0).astype(BF16)
            z_parts.append(_dot(wsg, vng) + bsp_ref[g])
            gws_ref[g] += jnp.where(causal, _dot_nt(dzb, vng), 0.0)
            gbs_ref[g] += jnp.broadcast_to(jnp.sum(dzg, axis=1, keepdims=True), (128, 128))
            dvn_parts.append(_dot(wsg_t, dzb))
        z = jnp.concatenate(z_parts, axis=1)
        dvn = jnp.concatenate(dvn_parts, axis=1)
        du = dg * z * _gelu_grad(u, tu)
        st_ref[0:1, :] += jnp.sum(dvn * xhat, axis=0, keepdims=True)
        st_ref[1:2, :] += jnp.sum(dvn, axis=0, keepdims=True)
        gg = dvn * g_ref[...]
        dgv = rstd * (gg - jnp.mean(gg, axis=1, keepdims=True) - xhat * jnp.mean(gg * xhat, axis=1, keepdims=True))
        dvb = dgv * _gelu_grad(vb, tv)
        duv_ref[:, 0:DB] = du.astype(BF16)
        duv_ref[:, DB:2 * DB] = dvb.astype(BF16)

    full3 = pl.BlockSpec((NH, 128, 128), lambda c: (0, 0, 0))
    vec = pl.BlockSpec((1, DB), lambda c: (0, 0))
    return pl.pallas_call(
        body, name="gmlp_bwd", grid=(NBLK,),
        in_specs=[pl.BlockSpec((128, DB), lambda c: (c, 3)), pl.BlockSpec((128, DB), lambda c: (c, 4)),
                  pl.BlockSpec((128, DB), lambda c: (c, 0)), full3, full3, full3, vec, vec],
        out_specs=[pl.BlockSpec((128, 2 * DB), lambda c: (c, 0)), full3, full3, pl.BlockSpec((8, DB), lambda c: (0, 0))],
        out_shape=[jax.ShapeDtypeStruct((S, 2 * DB), BF16), jax.ShapeDtypeStruct((NH, 128, 128), F32),
                   jax.ShapeDtypeStruct((NH, 128, 128), F32), jax.ShapeDtypeStruct((8, DB), F32)],
        compiler_params=_params(1),
    )(proj, proj, dgmlp, ws, ws_t, bsp_b, gain_v, bias_v)


def _rel_bias_grad(ds_sums):
    buckets = jnp.asarray(np.stack([_bucket_tile(d) for _, d in PATTERNS]))

    def body(bk_ref, ds_ref, o_ref):
        row = lax.broadcasted_iota(jnp.int32, (N_BUCKETS, 128), 0)
        lane = lax.broadcasted_iota(jnp.int32, (N_BUCKETS, 128), 1)

        def one_bucket(t, out):
            hits = [bk_ref[p] == t for p in range(3)]
            for h in range(NH):
                tot = jnp.zeros((128, 256), F32)
                for p in range(3):
                    tot = tot + jnp.where(hits[p], ds_ref[p, h], 0.0)
                out = jnp.where((row == t) & (lane == h), jnp.sum(tot), out)
            return out

        o_ref[...] = lax.fori_loop(0, N_BUCKETS, one_bucket, jnp.zeros((N_BUCKETS, 128), F32))

    return pl.pallas_call(
        body, name="rel_bias_grad",
        in_specs=[pl.BlockSpec(memory_space=pltpu.VMEM)] * 2, out_specs=pl.BlockSpec(memory_space=pltpu.VMEM),
        out_shape=jax.ShapeDtypeStruct((N_BUCKETS, 128), F32),
        compiler_params=pltpu.CompilerParams(vmem_limit_bytes=VMEM_LIMIT),
    )(buckets, ds_sums)


def _d_x(dproj, win_g, dpre1, after=None):
    tm, tn = 512, 512
    ws = DIN // N_CHIPS

    def body(a_ref, w_ref, d_ref, o_ref):
        acc = ALPHA * d_ref[...]
        for s in range(N_CHIPS):
            acc = acc + _dot_nt(a_ref[:, s * ws:(s + 1) * ws], w_ref[s])
        o_ref[...] = acc

    tile = pl.BlockSpec((tm, tn), lambda i, j: (i, j))
    body, more_specs, more = _behind(body, 3, after)
    return pl.pallas_call(
        body, name="d_x", grid=(S // tm, D // tn),
        in_specs=[pl.BlockSpec((tm, DIN), lambda i, j: (i, 0)),
                  pl.BlockSpec((N_CHIPS, tn, ws), lambda i, j: (0, j, 0)), tile] + more_specs,
        out_specs=tile, out_shape=jax.ShapeDtypeStruct((S, D), F32),
        compiler_params=_params(2),
    )(dproj, win_g, dpre1, *more)


def _adamw(w, g, m, v, name):
    rows, cols = w.shape
    tm = max(t for t in range(8, 257, 8) if rows % t == 0)

    def body(w_ref, g_ref, m_ref, v_ref, d_ref, nm_ref, nv_ref, go_ref):
        g = g_ref[...]
        m = ADAM_B1 * m_ref[...] + (1.0 - ADAM_B1) * g
        v = ADAM_B2 * v_ref[...] + (1.0 - ADAM_B2) * (g * g)
        m_hat = m / (1.0 - ADAM_B1 ** ADAM_STEP)
        v_hat = v / (1.0 - ADAM_B2 ** ADAM_STEP)
        d_ref[...] = -ADAM_LR * (m_hat / (jnp.sqrt(v_hat) + ADAM_EPS) + ADAM_WD * w_ref[...])
        nm_ref[...] = m
        nv_ref[...] = v
        go_ref[...] = g

    spec = pl.BlockSpec((tm, cols), lambda i: (i, 0))
    return pl.pallas_call(
        body, name=name, grid=(rows // tm,), in_specs=[spec] * 4, out_specs=[spec] * 4,
        out_shape=[jax.ShapeDtypeStruct((rows, cols), F32)] * 4, compiler_params=_params(1),
    )(w, g, m, v)


def _position():
    x, y, c = lax.axis_index("x"), lax.axis_index("y"), lax.axis_index("c")
    chips = [(1 - x, y), (x, 1 - y), (1 - x, 1 - y)]
    return x, y, c, chips


def _remote(src, dst, send_sems, recv_sems, k, to):
    return pltpu.make_async_remote_copy(src_ref=src, dst_ref=dst, send_sem=send_sems.at[k], recv_sem=recv_sems.at[k],
                                        device_id=to, device_id_type=MESH)


def _place_shard(w, name, after=None):
    rows, cols = w.shape
    tm = 256
    x, y = lax.axis_index("x"), lax.axis_index("y")

    def body(chip_ref, w_ref, o_ref):
        o_ref[...] = w_ref[...].astype(BF16)

    more_specs, more = ([ANY], [after]) if after is not None else ([], [])
    if after is not None:
        inner = body
        body = lambda chip_ref, w_ref, after_ref, o_ref: inner(chip_ref, w_ref, o_ref)
    return pl.pallas_call(
        body, name=name,
        grid_spec=pltpu.PrefetchScalarGridSpec(
            num_scalar_prefetch=1, grid=(rows // tm,),
            in_specs=[pl.BlockSpec((tm, cols), lambda i, chip: (i, 0))] + more_specs,
            out_specs=pl.BlockSpec((None, tm, cols), lambda i, chip: (chip[0], i, 0))),
        out_shape=jax.ShapeDtypeStruct((N_CHIPS, rows, cols), BF16),
        compiler_params=_params(1),
    )(jnp.reshape(2 * x + y, (1,)).astype(jnp.int32), w, *more)


def _to_bf16(x, name, after=None):
    tm = 256

    def body(x_ref, o_ref):
        o_ref[...] = x_ref[...].astype(BF16)

    spec = pl.BlockSpec((tm, x.shape[1]), lambda i: (i, 0))
    body, more_specs, more = _behind(body, 1, after)
    return pl.pallas_call(
        body, name=name, grid=(x.shape[0] // tm,), in_specs=[spec] + more_specs, out_specs=spec,
        out_shape=jax.ShapeDtypeStruct(x.shape, BF16), compiler_params=_params(1),
    )(x, *more)


HBM = pl.BlockSpec(memory_space=pltpu.HBM)
SEM = pl.BlockSpec(memory_space=pltpu.SEMAPHORE)
EFFECT = pltpu.SideEffectType.DATAFLOW_SIDE_EFFECTING


def _comm_call(name, body, bufs, sems_in, sems_out, after=None, token=False):
    nb, ns, no = len(bufs), len(sems_in), len(sems_out)
    n_in = nb + ns + (after is not None)

    def wrapped(*refs):
        body(refs[:nb], refs[nb:nb + ns], refs[n_in + nb:n_in + nb + no])
        if token:
            refs[-1][...] = jnp.zeros((8, 128), F32)

    outs = pl.pallas_call(
        wrapped, name=name,
        in_specs=[HBM] * nb + [SEM] * ns + ([ANY] if after is not None else []),
        out_specs=[HBM] * nb + [SEM] * no + ([pl.BlockSpec(memory_space=pltpu.VMEM)] if token else []),
        out_shape=[pltpu.HBM(b.shape, b.dtype) for b in bufs] + [pltpu.SemaphoreType.DMA((k,)) for k in sems_out]
        + ([jax.ShapeDtypeStruct((8, 128), F32)] if token else []),
        input_output_aliases={i: i for i in range(nb)},
        compiler_params=pltpu.CompilerParams(has_side_effects=EFFECT),
    )(*[pltpu.with_memory_space_constraint(b, pltpu.HBM) for b in bufs], *sems_in, *([after] if after is not None else []))
    return list(outs[:nb]), list(outs[nb:nb + no]), (outs[-1] if token else None)


RING_STAGES = {"ici_near": 2, "ici_far": 2, "d2d_near": 2, "d2d_far": 1}


def _ring_copies(buf, send_sems, recv_sems, k0, stage):
    x, y, c, _ = _position()
    hr = buf.shape[1] // 2
    qr = hr // 2
    half = lambda chip, h: buf.at[chip, pl.ds(h * hr, hr), :]
    quarter = lambda chip, h, q: buf.at[chip, pl.ds(h * hr + q * qr, qr), :]
    mine, x_chip, y_chip, far_chip = 2 * x + y, 2 * (1 - x) + y, 2 * x + (1 - y), 2 * (1 - x) + (1 - y)
    to_x, to_y, sibling = (1 - x, y, c), (x, 1 - y, c), (x, y, 1 - c)
    if stage == "ici_near":
        moves = [(half(mine, c), to_x, half(x_chip, c)), (half(mine, c), to_y, half(y_chip, c))]
    elif stage == "ici_far":
        moves = [(quarter(x_chip, c, 0), to_y, quarter(far_chip, c, 0)),
                 (quarter(y_chip, c, 1), to_x, quarter(far_chip, c, 1))]
    elif stage == "d2d_near":
        moves = [(half(x_chip, c), sibling, half(x_chip, 1 - c)), (half(y_chip, c), sibling, half(y_chip, 1 - c))]
    else:
        moves = [(half(far_chip, c), sibling, half(far_chip, 1 - c))]
    sends = [_remote(src, src, send_sems, recv_sems, k0 + i, to) for i, (src, to, _) in enumerate(moves)]
    arrivals = [_remote(got, got, send_sems, recv_sems, k0 + i, (x, y, c)) for i, (_, _, got) in enumerate(moves)]
    return sends, arrivals


def _ring_call(name, groups, actions, after=None):
    tags = list(dict.fromkeys(t for _, t, _ in actions))
    counts = {t: len(groups[t]["bufs"]) for t in tags}
    first = {t: sum(counts[u] for u in tags[:i]) for i, t in enumerate(tags)}
    waits = [(t, s) for v, t, s in actions if v == "wait"]
    starts = [(t, s) for v, t, s in actions if v == "start"]

    def body(bufs, sems_in, sems_out):
        for verb, t, s in actions:
            at, sems = (starts.index((t, s)), sems_out) if verb == "start" else (waits.index((t, s)), sems_in)
            for w in range(counts[t]):
                sends, arrivals = _ring_copies(bufs[first[t] + w], sems[2 * at], sems[2 * at + 1], RING_STAGES[s] * w, s)
                if verb == "start":
                    for cp in sends:
                        cp.start()
                else:
                    for cp in arrivals:
                        cp.wait_recv()
                    for cp in sends:
                        cp.wait_send()

    bufs, sems, token = _comm_call(
        name, body, [b for t in tags for b in groups[t]["bufs"]],
        [sem for t, s in waits for sem in groups[t]["sems"][s]],
        [RING_STAGES[s] * counts[t] for t, s in starts for _ in (0, 1)], after, token=True)
    for t in tags:
        groups[t]["bufs"] = bufs[first[t]:first[t] + counts[t]]
    for t, s in waits:
        del groups[t]["sems"][s]
    for i, (t, s) in enumerate(starts):
        groups[t]["sems"][s] = (sems[2 * i], sems[2 * i + 1])
    return token


def _cx_copies(src, dst, send_sems, recv_sems, k0):
    x, y, c, chips = _position()
    sends = [_remote(src.at[2 * cx + cy], dst.at[2 * x + y], send_sems, recv_sems, k0 + j, (cx, cy, c))
             for j, (cx, cy) in enumerate(chips)]
    arrivals = [_remote(dst.at[2 * cx + cy], dst.at[2 * cx + cy], send_sems, recv_sems, k0 + j, (x, y, c))
                for j, (cx, cy) in enumerate(chips)]
    return sends, arrivals


def _cx_start(name, pair_sums):
    n = len(pair_sums)
    landing = [lax.empty(p.shape, p.dtype) for p in pair_sums]

    def body(bufs, _, sems):
        for w in range(n):
            for cp in _cx_copies(bufs[w], bufs[n + w], sems[0], sems[1], 3 * w)[0]:
                cp.start()

    bufs, sems, token = _comm_call(name, body, list(pair_sums) + landing, [], [3 * n, 3 * n], token=True)
    return (bufs, sems), token


def _cx_wait(name, state, after):
    bufs, sems = state
    n = len(bufs) // 2

    def body(refs, sems_in, _):
        for w in range(n):
            sends, arrivals = _cx_copies(refs[w], refs[n + w], sems_in[0], sems_in[1], 3 * w)
            for cp in arrivals:
                cp.wait_recv()
            for cp in sends:
                cp.wait_send()

    bufs, _, _ = _comm_call(name, body, bufs, sems, [], after)
    return bufs[:n], bufs[n:]


def _px_copies(src, dst, send_sems, recv_sems, k):
    x, y, c, _ = _position()
    hr = src.shape[1] // 2
    send = _remote(src.at[:, pl.ds((1 - c) * hr, hr), :], dst, send_sems, recv_sems, k, (x, y, 1 - c))
    arrival = _remote(dst, dst, send_sems, recv_sems, k, (x, y, c))
    return send, arrival


def _px_start(name, grads):
    n = len(grads)
    landing = [lax.empty((N_CHIPS, g.shape[1] // 2, g.shape[2]), g.dtype) for g in grads]

    def body(bufs, _, sems):
        for w in range(n):
            _px_copies(bufs[w], bufs[n + w], sems[0], sems[1], w)[0].start()

    bufs, sems, token = _comm_call(name, body, list(grads) + landing, [], [n, n], token=True)
    return (bufs, sems), token


def _px_wait(name, state, after):
    bufs, sems = state
    n = len(bufs) // 2

    def body(refs, sems_in, _):
        for w in range(n):
            send, arrival = _px_copies(refs[w], refs[n + w], sems_in[0], sems_in[1], w)
            arrival.wait_recv()
            send.wait_send()

    bufs, _, _ = _comm_call(name, body, bufs, sems, [], after)
    return bufs[:n], bufs[n:]


def _pair_sum(grad, got, name):
    _, rows, cols = grad.shape
    hr = rows // 2
    tm = min(hr, 512)
    nb = hr // tm
    c = lax.axis_index("c")

    def body(c_ref, g_ref, o_ref, out_ref):
        out_ref[...] = (g_ref[...].astype(F32) + o_ref[...].astype(F32)).astype(BF16)

    return pl.pallas_call(
        body, name=name,
        grid_spec=pltpu.PrefetchScalarGridSpec(
            num_scalar_prefetch=1, grid=(N_CHIPS, nb),
            in_specs=[pl.BlockSpec((None, tm, cols), lambda s, i, c_ref: (s, c_ref[0] * nb + i, 0)),
                      pl.BlockSpec((None, tm, cols), lambda s, i, c_ref: (s, i, 0))],
            out_specs=pl.BlockSpec((None, tm, cols), lambda s, i, c_ref: (s, i, 0))),
        out_shape=jax.ShapeDtypeStruct((N_CHIPS, hr, cols), BF16),
        compiler_params=_params(2),
    )(jnp.reshape(c, (1,)).astype(jnp.int32), grad, got)


def _chip_sum(parts, pair_sums, name):
    _, hr, cols = parts.shape
    tm = min(hr, 512)
    nb = hr // tm
    x, y, c = lax.axis_index("x"), lax.axis_index("y"), lax.axis_index("c")

    def body(pos_ref, p_ref, own_ref, o_ref):
        chip = pos_ref[0]
        own = own_ref[...].astype(F32)
        term = lambda s: jnp.where(chip == s, own, p_ref[s].astype(F32))
        o_ref[...] = ((term(0) + term(1)) + term(2)) + term(3)

    return pl.pallas_call(
        body, name=name,
        grid_spec=pltpu.PrefetchScalarGridSpec(
            num_scalar_prefetch=1, grid=(nb,),
            in_specs=[pl.BlockSpec((N_CHIPS, tm, cols), lambda i, pos: (0, i, 0)),
                      pl.BlockSpec((None, tm, cols), lambda i, pos: (pos[0], i, 0))],
            out_specs=pl.BlockSpec((tm, cols), lambda i, pos: (pos[1] * nb + i, 0))),
        out_shape=jax.ShapeDtypeStruct((2 * hr, cols), F32), compiler_params=_params(1),
    )(jnp.stack([2 * x + y, c]).astype(jnp.int32), parts, pair_sums)


def _share_copies(buf, send_sems, recv_sems, k):
    x, y, c, _ = _position()
    hr = buf.shape[0] // 2
    mine, theirs = buf.at[pl.ds(c * hr, hr), :], buf.at[pl.ds((1 - c) * hr, hr), :]
    return (_remote(mine, mine, send_sems, recv_sems, k, (x, y, 1 - c)),
            _remote(theirs, theirs, send_sems, recv_sems, k, (x, y, c)))


def _share_start(name, bufs):
    n = len(bufs)

    def body(refs, _, sems):
        for w in range(n):
            _share_copies(refs[w], sems[0], sems[1], w)[0].start()

    bufs, sems, token = _comm_call(name, body, list(bufs), [], [n, n], token=True)
    return (bufs, sems), token


def _share_wait(name, state, after):
    bufs, sems = state

    def body(refs, sems_in, _):
        for w in range(len(bufs)):
            send, arrival = _share_copies(refs[w], sems_in[0], sems_in[1], w)
            arrival.wait_recv()
            send.wait_send()

    return _comm_call(name, body, bufs, sems, [], after)[0]


def _allreduce_small(g):
    rows = g.shape[0]
    half = rows // 2

    def body(g_ref, o_ref, sib, slots, send_sems, recv_sems):
        x, y, c, chips = _position()
        me, sibling = (x, y, c), (x, y, 1 - c)
        my_chip = 2 * x + y
        mine = pl.ds(pl.multiple_of(c * half, 8), half)
        theirs = pl.ds(pl.multiple_of((1 - c) * half, 8), half)
        pair = _remote(g_ref.at[theirs], sib, send_sems, recv_sems, 0, sibling)
        pair.start()
        pair.wait()
        slots[my_chip] = g_ref[mine, :] + sib[...]
        sent = []
        for j, (cx, cy) in enumerate(chips):
            cp = _remote(slots.at[my_chip], slots.at[my_chip], send_sems, recv_sems, 1 + j, (cx, cy, c))
            cp.start()
            sent.append(cp)
        for j, (cx, cy) in enumerate(chips):
            got = slots.at[2 * cx + cy]
            _remote(got, got, send_sems, recv_sems, 1 + j, me).wait_recv()
        for cp in sent:
            cp.wait_send()
        o_ref[mine, :] = ((slots[0] + slots[1]) + slots[2]) + slots[3]
        swap = _remote(o_ref.at[mine], o_ref.at[mine], send_sems, recv_sems, 4, sibling)
        swap.start()
        swap.wait()

    vm = pl.BlockSpec(memory_space=pltpu.VMEM)
    return pl.pallas_call(
        body, name="allreduce_small",
        in_specs=[vm], out_specs=vm, out_shape=jax.ShapeDtypeStruct((rows, 128), F32),
        scratch_shapes=[pltpu.VMEM((half, 128), F32), pltpu.VMEM((N_CHIPS, half, 128), F32),
                        pltpu.SemaphoreType.DMA((5,)), pltpu.SemaphoreType.DMA((5,))],
        compiler_params=pltpu.CompilerParams(vmem_limit_bytes=VMEM_LIMIT),
    )(g)


_SMALL =("rel_bias", "ln_v_gain", "ln_v_bias", "w_spatial", "b_spatial", "ln1_gain", "ln1_bias",
          "b_ff1", "b_ff2", "ln2_gain", "ln2_bias")
_SMALL_ROWS = 1200
_LOSS_AT = (152832 // 128, 0)


def _pack_small(parts):
    flat = jnp.concatenate([parts[k].reshape(-1).astype(F32) for k in _SMALL])
    flat = jnp.pad(flat, (0, _SMALL_ROWS * 128 - flat.shape[0]))
    return flat.reshape(_SMALL_ROWS, 128)


def _unpack_small(packed, like):
    flat = packed.reshape(-1)
    out, at = {}, 0
    for k in _SMALL:
        n = math.prod(like[k].shape)
        out[k] = flat[at:at + n].reshape(like[k].shape)
        at += n
    return out


def kernel(x, w_in, rel_bias, ln_v_gain, ln_v_bias, w_spatial, b_spatial, w_proj_a, w_proj_b, w_out, ln1_gain, ln1_bias, w_ff1, b_ff1, w_ff2, b_ff2, ln2_gain, ln2_bias, loss_target, m_w_in, m_rel_bias, m_ln_v_gain, m_ln_v_bias, m_w_spatial, m_b_spatial, m_w_proj_a, m_w_proj_b, m_w_out, m_ln1_gain, m_ln1_bias, m_w_ff1, m_b_ff1, m_w_ff2, m_b_ff2, m_ln2_gain, m_ln2_bias, v_w_in, v_rel_bias, v_ln_v_gain, v_ln_v_bias, v_w_spatial, v_b_spatial, v_w_proj_a, v_w_proj_b, v_w_out, v_ln1_gain, v_ln1_bias, v_w_ff1, v_b_ff1, v_w_ff2, v_b_ff2, v_ln2_gain, v_ln2_bias):
    args = dict(locals())
    big = ("w_in", "w_proj_a", "w_proj_b", "w_out", "w_ff1", "w_ff2")
    weights = ("w_in", "rel_bias", "ln_v_gain", "ln_v_bias", "w_spatial", "b_spatial", "w_proj_a", "w_proj_b", "w_out",
               "ln1_gain", "ln1_bias", "w_ff1", "b_ff1", "w_ff2", "b_ff2", "ln2_gain", "ln2_bias")

    xs = x[0]
    target = loss_target[0]

    ring = {"a": {"bufs": [_place_shard(w_in[0], "place_w_in")], "sems": {}}}
    tok = _ring_call("allgather_a_near", ring, [("start", "a", "ici_near")])
    placed = [_place_shard(args[k][0], f"place_{k}", after=tok) for k in big[1:]]
    for tag, bufs in (("b", placed[0:3]), ("c", placed[3:4]), ("d", placed[4:5])):
        ring[tag] = {"bufs": bufs, "sems": {}}
    xb = _to_bf16(xs, "x_to_bf16", after=placed[4])

    mx, my = lax.axis_index("x"), lax.axis_index("y")
    own = jnp.reshape(2 * mx + my, (1,)).astype(jnp.int32)
    near = jnp.stack([2 * (1 - mx) + my, 2 * mx + (1 - my)]).astype(jnp.int32)
    far = jnp.reshape(2 * (1 - mx) + (1 - my), (1,)).astype(jnp.int32)
    proj = _proj(xb, ring["a"]["bufs"][0], own, "proj_own")
    _ring_call("allgather_a_far", ring, [("wait", "a", "ici_near"), ("start", "a", "ici_far"), ("start", "a", "d2d_near"),
                                         ("start", "b", "ici_near"), ("start", "c", "ici_near")], after=proj)
    _ring_call("allgather_a_near_done", ring, [("wait", "a", "d2d_near")])
    proj = _proj(xb, ring["a"]["bufs"][0], near, "proj_near", into=proj)
    _ring_call("allgather_a_last", ring, [("wait", "a", "ici_far"), ("start", "a", "d2d_far")], after=proj)
    _ring_call("allgather_a_done", ring, [("wait", "a", "d2d_far")])
    (win_g,) = ring["a"]["bufs"]
    proj = _proj(xb, win_g, far, "proj_far", into=proj)
    _ring_call("allgather_b_far", ring, [("wait", "b", "ici_near"), ("start", "b", "ici_far"), ("start", "b", "d2d_near")],
               after=proj)
    ws = w_spatial[0]
    ws_t = jnp.transpose(ws, (0, 2, 1))
    bsp_b = jnp.broadcast_to(b_spatial[0][:, :, None], (NH, 128, 128))
    gmlp = _gmlp_fwd(proj, ws, bsp_b, ln_v_gain, ln_v_bias)
    bias = _bias_tiles(rel_bias)
    attn, lse = _attention_fwd(proj, bias)
    _ring_call("allgather_b_last_c_far", ring,
               [("wait", "b", "ici_far"), ("start", "b", "d2d_far"),
                ("wait", "c", "ici_near"), ("start", "c", "ici_far"), ("start", "c", "d2d_near"),
                ("start", "d", "ici_near")], after=attn)
    _ring_call("allgather_b_done", ring, [("wait", "b", "d2d_near"), ("wait", "b", "d2d_far")])
    wpa_g, wpb_g, wout_g = ring["b"]["bufs"]
    wout_full = wout_g.reshape(D, D)
    ya, yb, merged = _branch(attn, gmlp, wpa_g, wpb_g, proj)
    xhat1, rstd1, h1b = _out_ln1(merged, wout_full, xs, ln1_gain, ln1_bias)
    _ring_call("allgather_c_last", ring, [("wait", "c", "ici_far"), ("start", "c", "d2d_far")], after=h1b)
    _ring_call("allgather_c_done", ring, [("wait", "c", "d2d_near"), ("wait", "c", "d2d_far")])
    (w1_g,) = ring["c"]["bufs"]
    a, r = _ff1(h1b, w1_g, b_ff1, 0, "ff1_first")
    tok = _ring_call("allgather_d_far", ring,
                     [("wait", "d", "ici_near"), ("start", "d", "ici_far"), ("start", "d", "d2d_near")], after=a)
    a, r = _ff1(h1b, w1_g, b_ff1, 1, "ff1_second", into=(a, r), after=tok)
    _ring_call("allgather_d_last", ring, [("wait", "d", "ici_far"), ("start", "d", "d2d_far")], after=a)
    _ring_call("allgather_d_done", ring, [("wait", "d", "d2d_near"), ("wait", "d", "d2d_far")])
    (w2_g,) = ring["d"]["bufs"]
    w2_full = w2_g.reshape(DFF, D)
    dpre2, dpre2b, st2 = _ff2_ln2_loss(a, w2_full, xhat1, ln1_gain, ln1_bias, b_ff2, ln2_gain, ln2_bias, target)

    def pair_and_chip(tag, state, after):
        local, from_sibling = _px_wait(f"pair_exchange_wait_{tag}", state, after)
        pair_sums = [_pair_sum(g, o, f"pair_sum_{tag}_{i}") for i, (g, o) in enumerate(zip(local, from_sibling))]
        return _cx_start(f"chip_exchange_start_{tag}", pair_sums)

    g_w2 = _grad_w(a, dpre2b, "grad_w_ff2", 512, 2048, False)
    px, tok = _px_start("pair_exchange_start_w_ff2", [g_w2.reshape(N_CHIPS, DFF // N_CHIPS, D)])
    dprea, g_b1 = _d_ff1(dpre2b, w2_full, r, after=tok)
    cx_w2, tok = pair_and_chip("w_ff2", px, dprea)
    g_w1 = _grad_w(h1b, dprea, "grad_w_ff1", 512, 2048, True, after=tok)
    px, tok = _px_start("pair_exchange_start_w_ff1", [g_w1])
    dpre1, dpre1b, st1 = _d_h1_ln1(dprea, w1_g, dpre2, xhat1, rstd1, ln1_gain, after=tok)
    cx_w1, tok = pair_and_chip("w_ff1", px, dpre1b)
    g_wout = _grad_w(merged, dpre1b, "grad_w_out", 512, 2048, False, after=tok)
    dya, dyb, dga, dgb = _d_merged(dpre1b, wout_full, proj, ya, yb)
    g_wpa = _grad_w(attn, dya, "grad_w_proj_a", 1024, 512, True)
    g_wpb = _grad_w(gmlp, dyb, "grad_w_proj_b", 1024, 512, True)
    px, tok = _px_start("pair_exchange_start_b", [g_wpa, g_wpb, g_wout.reshape(N_CHIPS, D // N_CHIPS, D)])
    dattn, dgmlp = _d_branches(dya, dyb, wpa_g, wpb_g, after=tok)
    duv, g_ws, g_bs, stv = _gmlp_bwd(proj, dgmlp, ws, ws_t, bsp_b, ln_v_gain, ln_v_bias)
    cx_b, tok = pair_and_chip("b", px, duv)
    dq, dk, dv, ds_sums = _attention_bwd(proj, dattn, attn, lse, bias, after=tok)
    g_rb = _rel_bias_grad(ds_sums)[:, :NH]

    small_g = dict(rel_bias=g_rb, ln_v_gain=stv[0], ln_v_bias=stv[1], w_spatial=g_ws, b_spatial=g_bs[:, :, 0],
                   ln1_gain=st1[0], ln1_bias=st1[1], b_ff1=g_b1, b_ff2=st2[2], ln2_gain=st2[0], ln2_bias=st2[1])
    gs = _allreduce_small(_pack_small(small_g).at[_LOSS_AT].set(st2[3, 0]))
    ds_, ms_, vs_, _ = _adamw(_pack_small({k: args[k] for k in _SMALL}), gs,
                           _pack_small({k: args["m_" + k] for k in _SMALL}),
                           _pack_small({k: args["v_" + k] for k in _SMALL}), "adamw_small")
    like = {k: args[k] for k in _SMALL}
    grads, deltas, new_m, new_v = (_unpack_small(t, like) for t in (gs, ds_, ms_, vs_))

    dproj = jnp.concatenate([dq, dk, dv, duv, dga, dgb], axis=1)
    g_win = _grad_w(xb, dproj, "grad_w_in", 512, 2304, True, after=gs)
    px, tok = _px_start("pair_exchange_start_w_in", [g_win])

    def chip_sums(tag, state, names, after):
        pair_sums, from_chips = _cx_wait(f"chip_exchange_wait_{tag}", state, after)
        halves = [_chip_sum(p, own, f"chip_sum_{k}") for p, own, k in zip(from_chips, pair_sums, names)]
        return _share_start(f"share_start_{tag}", halves)

    def adam(tag, state, names, after):
        last = None
        for k, g in zip(names, _share_wait(f"share_wait_{tag}", state, after)):
            d_, m_, v_, g_ = _adamw(args[k][0], g, args["m_" + k][0], args["v_" + k][0], f"adamw_{k}")
            grads[k], deltas[k], new_m[k], new_v[k] = g_[None], d_[None], m_[None], v_[None]
            last = d_
        return last

    sh_w2, tok = chip_sums("w_ff2", cx_w2, ["w_ff2"], tok)
    sh_w1, tok = chip_sums("w_ff1", cx_w1, ["w_ff1"], tok)
    sh_b, tok = chip_sums("b", cx_b, ["w_proj_a", "w_proj_b", "w_out"], tok)
    cx_in, tok = pair_and_chip("w_in", px, tok)
    grad_x = _d_x(dproj, win_g, dpre1, after=tok)
    done = adam("w_ff2", sh_w2, ["w_ff2"], grad_x)
    done = adam("w_ff1", sh_w1, ["w_ff1"], done)
    done = adam("b", sh_b, ["w_proj_a", "w_proj_b", "w_out"], done)
    sh_in, tok = chip_sums("w_in", cx_in, ["w_in"], done)
    adam("w_in", sh_in, ["w_in"], tok)

    loss = gs[_LOSS_AT] * (0.5 / D)
    return (loss, grad_x[None], *[grads[k] for k in weights], *[deltas[k] for k in weights],
            *[new_m[k] for k in weights], *[new_v[k] for k in weights])
```

```python
import math

import numpy as np
import jax
import jax.numpy as jnp
from jax import lax
from jax.experimental import pallas as pl
from jax.experimental.pallas import tpu as pltpu

F32 = jnp.float32
BF16 = jnp.bfloat16

S = 2048
D = 2048
DA = 1024
DB = 1024
DFF = 8192
DIN = 9216
NH = 8
HD = 128
NBLK = 16
PATTERNS = ((128, 1), (512, 4), (2048, 16))
N_BUCKETS = 32
MAX_DISTANCE = 2048
ALPHA = 2.0 ** 0.25
LN_EPS = 1e-5
NEG_INF = -1e30
SCALE = HD ** -0.5
N_CHIPS = 4

ADAM_LR = 0.001
ADAM_B1 = 0.9
ADAM_B2 = 0.999
ADAM_EPS = 1e-08
ADAM_WD = 0.01
ADAM_STEP = 10

VMEM_LIMIT = 56 * 1024 * 1024
MESH = pl.DeviceIdType.MESH
ANY = pl.BlockSpec(memory_space=pl.ANY)


def _params(n_axes, vmem=VMEM_LIMIT):
    return pltpu.CompilerParams(dimension_semantics=("arbitrary",) * n_axes, vmem_limit_bytes=vmem)


def _bucket_tile(dilation):
    qi = np.arange(128)[:, None]
    kj = np.arange(256)[None, :]
    n = np.clip(128 + qi - kj, 0, 128) * dilation
    max_exact = N_BUCKETS // 2
    nf = np.maximum(n, 1).astype(np.float32)
    large = max_exact + (np.log(nf / np.float32(max_exact)) / np.float32(math.log(MAX_DISTANCE / max_exact))
                         * np.float32(N_BUCKETS - max_exact)).astype(np.int32)
    large = np.minimum(large, N_BUCKETS - 1)
    return np.where(n < max_exact, n, large).astype(np.int32)


def _gelu(x):
    c = math.sqrt(2.0 / math.pi)
    t = jnp.tanh(c * (x + 0.044715 * x * x * x))
    return 0.5 * x * (1.0 + t), t


def _gelu_grad(x, t):
    c = math.sqrt(2.0 / math.pi)
    return 0.5 * (1.0 + t) + 0.5 * x * (1.0 - t * t) * c * (1.0 + 3.0 * 0.044715 * x * x)


def _sigmoid(x):
    return 1.0 / (1.0 + jnp.exp(-x))


def _dot(a, b):
    return jnp.dot(a, b, preferred_element_type=F32)


def _behind(body, n_in, after):
    if after is None:
        return body, [], []
    return (lambda *refs: body(*refs[:n_in], *refs[n_in + 1:])), [ANY], [after]


def _dot_nt(a, b):
    return lax.dot_general(a, b, (((1,), (1,)), ((), ())), preferred_element_type=F32)


def _proj(xb, win_g, shards, name, into=None):
    tn = 768
    per = 2304 // tn

    def body(shards_ref, x_ref, w_ref, *rest):
        rest[-1][...] = _dot(x_ref[...], w_ref[...])

    in_specs = [pl.BlockSpec((S, D), lambda j, sh: (0, 0)),
                pl.BlockSpec((None, D, tn), lambda j, sh: (sh[j // per], 0, j % per))]
    return pl.pallas_call(
        body, name=name,
        grid_spec=pltpu.PrefetchScalarGridSpec(
            num_scalar_prefetch=1, grid=(shards.shape[0] * per,),
            in_specs=in_specs + ([ANY] if into is not None else []),
            out_specs=pl.BlockSpec((S, tn), lambda j, sh: (0, sh[j // per] * per + j % per))),
        out_shape=jax.ShapeDtypeStruct((S, DIN), F32),
        input_output_aliases={3: 0} if into is not None else {},
        compiler_params=_params(1),
    )(shards, xb, win_g, *([into] if into is not None else []))


FWD_HEADS_PER_STEP = 4
BWD_HEADS_PER_STEP = 2


def _bias_tiles(rel_bias):
    buckets = jnp.asarray(np.stack([_bucket_tile(d) for _, d in PATTERNS]))

    def body(rb_ref, bk_ref, o_ref):
        qi = lax.broadcasted_iota(jnp.int32, (128, 256), 0)
        kj = lax.broadcasted_iota(jnp.int32, (128, 256), 1)
        steps = 128 + qi - kj
        band = (steps >= 0) & (steps <= 128)
        o_ref[...] = jnp.zeros_like(o_ref)
        for p in range(len(PATTERNS)):
            bucket = bk_ref[p]

            def one_bucket(t, carry):
                hit = bucket == t
                for h in range(NH):
                    o_ref[p, h] = jnp.where(hit, rb_ref[t, h], o_ref[p, h])
                return carry

            lax.fori_loop(0, N_BUCKETS, one_bucket, 0)
            for h in range(NH):
                o_ref[p, h] = jnp.where(band, o_ref[p, h], NEG_INF)

    return pl.pallas_call(
        body, name="bias_tiles",
        in_specs=[pl.BlockSpec(memory_space=pltpu.SMEM), pl.BlockSpec(memory_space=pltpu.VMEM)],
        out_specs=pl.BlockSpec(memory_space=pltpu.VMEM),
        out_shape=jax.ShapeDtypeStruct((len(PATTERNS), NH, 128, 256), F32),
        compiler_params=pltpu.CompilerParams(vmem_limit_bytes=VMEM_LIMIT),
    )(rel_bias, buckets)


def _block_rows(b, dilation):
    nblk = NBLK // dilation
    r, n = b // nblk, b % nblk
    start = r + n * (128 * dilation)
    prev_start = jnp.maximum(start - 128 * dilation, r)
    if dilation == 1:
        return pl.ds(pl.multiple_of(start, 128), 128), pl.ds(pl.multiple_of(prev_start, 128), 128), n > 0
    return pl.ds(start, 128, stride=dilation), pl.ds(prev_start, 128, stride=dilation), n > 0


def _head_specs(first, hps):
    return [pl.BlockSpec((S, HD), lambda g, j=j: (0, first + g * hps + j)) for j in range(hps)]


def _bias_spec(hps):
    return pl.BlockSpec((len(PATTERNS), hps, 128, 256), lambda g: (0, g, 0, 0))


def _heads_spec(hps):
    return pl.BlockSpec((S, hps * HD), lambda g: (0, g))


def _attention_fwd(proj, bias):
    hps = FWD_HEADS_PER_STEP

    def body(bias_ref, *refs):
        q_refs, k_refs, v_refs = (refs[i * hps:(i + 1) * hps] for i in range(3))
        o_ref, lse_ref = refs[3 * hps:3 * hps + 2]
        acc_scrs, m_scrs, l_scrs = (refs[3 * hps + 2 + i * hps:3 * hps + 2 + (i + 1) * hps] for i in range(3))
        kj = lax.broadcasted_iota(jnp.int32, (128, 256), 1)
        for p, (_, d) in enumerate(PATTERNS):
            prev_blocks = NBLK // d > 1

            def block(b, carry):
                units = [(j,) + _block_rows(blk, d) for blk in (b, b + NBLK // 2) for j in range(hps)]
                scores = []
                for j, rows, prows, _ in units:
                    q = q_refs[j][rows, :].astype(BF16)
                    cur = _dot_nt(q, k_refs[j][rows, :].astype(BF16))
                    if prev_blocks:
                        cur = jnp.concatenate([_dot_nt(q, k_refs[j][prows, :].astype(BF16)), cur], axis=1)
                    scores.append(cur)
                soft = []
                for u, (j, _, _, has_prev) in enumerate(units):
                    if prev_blocks:
                        s = jnp.where((kj >= 128) | has_prev, scores[u] * SCALE + bias_ref[p, j], NEG_INF)
                    else:
                        s = scores[u] * SCALE + bias_ref[p, j, :, 128:256]
                    m = jnp.max(s, axis=1, keepdims=True)
                    e = jnp.exp(s - m)
                    soft.append((m, jnp.sum(e, axis=1, keepdims=True), e.astype(BF16)))
                outs = []
                for u, (j, rows, prows, _) in enumerate(units):
                    e = soft[u][2]
                    if prev_blocks:
                        outs.append(_dot(e[:, :128], v_refs[j][prows, :].astype(BF16))
                                    + _dot(e[:, 128:], v_refs[j][rows, :].astype(BF16)))
                    else:
                        outs.append(_dot(e, v_refs[j][rows, :].astype(BF16)))
                for u, (j, rows, _, _) in enumerate(units):
                    acc_scr, m_scr, l_scr = acc_scrs[j], m_scrs[j], l_scrs[j]
                    (m, den, _), o = soft[u], outs[u]
                    if p == 0:
                        acc_scr[rows, :] = o
                        m_scr[rows, :] = jnp.broadcast_to(m, (128, HD))
                        l_scr[rows, :] = jnp.broadcast_to(den, (128, HD))
                    else:
                        m_old = m_scr[rows, :]
                        m_new = jnp.maximum(m_old, m)
                        w_old, w_new = jnp.exp(m_old - m_new), jnp.exp(m - m_new)
                        acc_scr[rows, :] = acc_scr[rows, :] * w_old + o * w_new
                        l_scr[rows, :] = l_scr[rows, :] * w_old + den * w_new
                        m_scr[rows, :] = m_new
                return carry

            lax.fori_loop(0, NBLK // 2, block, 0)
        for j in range(hps):
            cols = slice(j * HD, (j + 1) * HD)
            den = l_scrs[j][...]
            o_ref[:, cols] = (acc_scrs[j][...] / den).astype(BF16)
            lse_ref[:, cols] = m_scrs[j][...] + jnp.log(den)

    return pl.pallas_call(
        body, name="attention_fwd", grid=(NH // hps,),
        in_specs=[_bias_spec(hps)] + _head_specs(0, hps) + _head_specs(NH, hps) + _head_specs(2 * NH, hps),
        out_specs=[_heads_spec(hps), _heads_spec(hps)],
        out_shape=[jax.ShapeDtypeStruct((S, DA), BF16), jax.ShapeDtypeStruct((S, DA), F32)],
        scratch_shapes=[pltpu.VMEM((S, HD), F32)] * (3 * hps),
        compiler_params=_params(1),
    )(bias, *([proj] * (3 * hps)))


def _attention_bwd(proj, dattn, attn, lse, bias, after=None):
    hps = BWD_HEADS_PER_STEP

    def body(bias_ref, *refs):
        q_refs, k_refs, v_refs, do_refs, o_refs, lse_refs = (refs[i * hps:(i + 1) * hps] for i in range(6))
        dq_ref, dk_ref, dv_ref, ds_ref = refs[6 * hps:6 * hps + 4]
        dl_scrs, dq_scrs, dk_scrs, dv_scrs = (refs[6 * hps + 4 + i * hps:6 * hps + 4 + (i + 1) * hps] for i in range(4))
        ds_ref[...] = jnp.zeros_like(ds_ref)
        for j in range(hps):
            dq_scrs[j][...] = jnp.zeros((S, HD), F32)
            dk_scrs[j][...] = jnp.zeros((S, HD), F32)
            dv_scrs[j][...] = jnp.zeros((S, HD), F32)
            prod = do_refs[j][...] * o_refs[j][...].astype(F32)
            dl_scrs[j][...] = jnp.broadcast_to(jnp.sum(prod, axis=1, keepdims=True), (S, HD))
        for p, (_, d) in enumerate(PATTERNS):
            prev_blocks = NBLK // d > 1

            def block(b, carry):
                units = [(j,) + _block_rows(b + i * (NBLK // 4), d) for i in range(4) for j in range(hps)]
                ops, raw = [], []
                for j, rows, prows, _ in units:
                    q, do = q_refs[j][rows, :].astype(BF16), do_refs[j][rows, :].astype(BF16)
                    kc, vc = k_refs[j][rows, :].astype(BF16), v_refs[j][rows, :].astype(BF16)
                    if prev_blocks:
                        kp, vp = k_refs[j][prows, :].astype(BF16), v_refs[j][prows, :].astype(BF16)
                        ops.append((q, do, kc, kp))
                        raw.append((_dot_nt(q, kc), _dot_nt(do, vc), _dot_nt(q, kp), _dot_nt(do, vp)))
                    else:
                        ops.append((q, do, kc))
                        raw.append((_dot_nt(q, kc), _dot_nt(do, vc)))
                probs = []
                for u, (j, rows, _, has_prev) in enumerate(units):
                    lse_b, dl_b = lse_refs[j][rows, :], dl_scrs[j][rows, :]
                    p_c = jnp.exp(raw[u][0] * SCALE + bias_ref[p, j, :, 128:256] - lse_b)
                    ds_c = p_c * (raw[u][1] - dl_b)
                    ds_ref[p, j, :, 128:256] += ds_c
                    if prev_blocks:
                        p_p = jnp.where(has_prev, jnp.exp(raw[u][2] * SCALE + bias_ref[p, j, :, 0:128] - lse_b), 0.0)
                        ds_p = p_p * (raw[u][3] - dl_b)
                        ds_ref[p, j, :, 0:128] += ds_p
                        probs.append((p_c, ds_c, p_p, ds_p))
                    else:
                        probs.append((p_c, ds_c))
                grads = []
                for u in range(len(units)):
                    q, do, kc = ops[u][:3]
                    p_c, ds_c = probs[u][:2]
                    dq = _dot(ds_c.astype(BF16), kc)
                    cur = (_dot(ds_c.T.astype(BF16), q) * SCALE, _dot(p_c.T.astype(BF16), do))
                    if prev_blocks:
                        p_p, ds_p = probs[u][2:]
                        dq = dq + _dot(ds_p.astype(BF16), ops[u][3])
                        cur = cur + (_dot(ds_p.T.astype(BF16), q) * SCALE, _dot(p_p.T.astype(BF16), do))
                    grads.append((dq * SCALE,) + cur)
                for u, (j, rows, prows, _) in enumerate(units):
                    dq_scrs[j][rows, :] += grads[u][0]
                    dk_scrs[j][rows, :] += grads[u][1]
                    dv_scrs[j][rows, :] += grads[u][2]
                    if prev_blocks:
                        dk_scrs[j][prows, :] += grads[u][3]
                        dv_scrs[j][prows, :] += grads[u][4]
                return carry

            lax.fori_loop(0, NBLK // 4, block, 0)
        for j in range(hps):
            cols = slice(j * HD, (j + 1) * HD)
            dq_ref[:, cols] = dq_scrs[j][...].astype(BF16)
            dk_ref[:, cols] = dk_scrs[j][...].astype(BF16)
            dv_ref[:, cols] = dv_scrs[j][...].astype(BF16)

    body, more_specs, more = _behind(body, 1 + 6 * hps, after)
    return pl.pallas_call(
        body, name="attention_bwd", grid=(NH // hps,),
        in_specs=[_bias_spec(hps)]
        + _head_specs(0, hps) + _head_specs(NH, hps) + _head_specs(2 * NH, hps) + 3 * _head_specs(0, hps)
        + more_specs,
        out_specs=3 * [_heads_spec(hps)] + [pl.BlockSpec((3, hps, 128, 256), lambda g: (0, g, 0, 0))],
        out_shape=[jax.ShapeDtypeStruct((S, DA), BF16)] * 3 + [jax.ShapeDtypeStruct((3, NH, 128, 256), F32)],
        scratch_shapes=[pltpu.VMEM((S, HD), F32)] * (4 * hps),
        compiler_params=_params(1),
    )(bias, *([proj] * (3 * hps)), *([dattn] * hps), *([attn] * hps), *([lse] * hps), *more)


def _gmlp_parts(u_ref, vb_ref, g_ref, be_ref):
    u = u_ref[...]
    u_act, tu = _gelu(u)
    vb = vb_ref[...]
    gv, tv = _gelu(vb)
    mean = jnp.mean(gv, axis=1, keepdims=True)
    cen = gv - mean
    var = jnp.mean(cen * cen, axis=1, keepdims=True)
    rstd = lax.rsqrt(var + LN_EPS)
    xhat = cen * rstd
    vn = xhat * g_ref[...] + be_ref[...]
    return u, tu, u_act, vb, tv, rstd, xhat, vn


def _gmlp_fwd(proj, ws, bsp_b, gain_v, bias_v):
    def body(u_ref, vb_ref, ws_ref, bsp_ref, g_ref, be_ref, o_ref):
        _, _, u_act, _, _, _, _, vn = _gmlp_parts(u_ref, vb_ref, g_ref, be_ref)
        row = lax.broadcasted_iota(jnp.int32, (128, 128), 0)
        col = lax.broadcasted_iota(jnp.int32, (128, 128), 1)
        causal = row >= col
        for g in range(NH):
            cols = slice(g * 128, (g + 1) * 128)
            wsg = jnp.where(causal, ws_ref[g], 0.0).astype(BF16)
            z = _dot(wsg, vn[:, cols].astype(BF16)) + bsp_ref[g]
            o_ref[:, cols] = (u_act[:, cols] * z).astype(BF16)

    return pl.pallas_call(
        body, name="gmlp_fwd", grid=(NBLK,),
        in_specs=[pl.BlockSpec((128, DB), lambda c: (c, 3)), pl.BlockSpec((128, DB), lambda c: (c, 4)),
                  pl.BlockSpec((NH, 128, 128), lambda c: (0, 0, 0)), pl.BlockSpec((NH, 128, 128), lambda c: (0, 0, 0)),
                  pl.BlockSpec((1, DB), lambda c: (0, 0)), pl.BlockSpec((1, DB), lambda c: (0, 0))],
        out_specs=pl.BlockSpec((128, DB), lambda c: (c, 0)),
        out_shape=jax.ShapeDtypeStruct((S, DB), BF16),
        compiler_params=_params(1),
    )(proj, proj, ws, bsp_b, gain_v, bias_v)


def _branch(attn, gmlp, wpa_g, wpb_g, proj):
    tn = 512

    def body(a_ref, g_ref, wa_ref, wb_ref, ga_ref, gb_ref, ya_ref, yb_ref, mg_ref):
        ya = _dot(a_ref[...], wa_ref[...])
        yb = _dot(g_ref[...], wb_ref[...])
        ya_ref[...] = ya.astype(BF16)
        yb_ref[...] = yb.astype(BF16)
        mg_ref[...] = (_sigmoid(ga_ref[...]) * ya + _sigmoid(gb_ref[...]) * yb).astype(BF16)

    out = pl.BlockSpec((S, tn), lambda j: (0, j))
    return pl.pallas_call(
        body, name="branch", grid=(D // tn,),
        in_specs=[pl.BlockSpec((S, DA), lambda j: (0, 0)), pl.BlockSpec((S, DB), lambda j: (0, 0)),
                  pl.BlockSpec((None, DA, tn), lambda j: (j, 0, 0)), pl.BlockSpec((None, DB, tn), lambda j: (j, 0, 0)),
                  pl.BlockSpec((S, tn), lambda j: (0, 5120 // tn + j)), pl.BlockSpec((S, tn), lambda j: (0, 7168 // tn + j))],
        out_specs=[out, out, out],
        out_shape=[jax.ShapeDtypeStruct((S, D), BF16)] * 3,
        compiler_params=_params(1),
    )(attn, gmlp, wpa_g, wpb_g, proj, proj)


def _out_ln1(merged, wout_g, x, gain, bias):
    tm = 256

    def body(m_ref, w_ref, x_ref, g_ref, b_ref, xh_ref, rs_ref, h_ref):
        pre = ALPHA * x_ref[...] + _dot(m_ref[...], w_ref[...])
        mean = jnp.mean(pre, axis=1, keepdims=True)
        cen = pre - mean
        var = jnp.mean(cen * cen, axis=1, keepdims=True)
        rstd = lax.rsqrt(var + LN_EPS)
        xhat = cen * rstd
        xh_ref[...] = xhat
        rs_ref[...] = jnp.broadcast_to(rstd, (tm, 128))
        h_ref[...] = (xhat * g_ref[...] + b_ref[...]).astype(BF16)

    row = pl.BlockSpec((tm, D), lambda i: (i, 0))
    vec = pl.BlockSpec((1, D), lambda i: (0, 0))
    return pl.pallas_call(
        body, name="out_ln1", grid=(S // tm,),
        in_specs=[row, pl.BlockSpec((D, D), lambda i: (0, 0)), row, vec, vec],
        out_specs=[row, pl.BlockSpec((tm, 128), lambda i: (i, 0)), row],
        out_shape=[jax.ShapeDtypeStruct((S, D), F32), jax.ShapeDtypeStruct((S, 128), F32),
                   jax.ShapeDtypeStruct((S, D), BF16)],
        compiler_params=_params(1),
    )(merged, wout_g, x, gain, bias)


def _ff1(h1b, w1_g, b1, half, name, into=None, after=None):
    tn = 512
    per = D // tn
    steps = DFF // tn // 2
    first = half * steps

    def body(h_ref, w_ref, b_ref, *rest):
        a_ref, r_ref = rest[-2:]
        r = jnp.maximum(_dot(h_ref[...], w_ref[...]) + b_ref[...], 0.0)
        r_ref[...] = r.astype(BF16)
        a_ref[...] = (r * r).astype(BF16)

    out = pl.BlockSpec((S, tn), lambda j: (0, first + j))
    extra = list(into) if into is not None else []
    if after is not None:
        extra.append(after)
    return pl.pallas_call(
        body, name=name, grid=(steps,),
        in_specs=[pl.BlockSpec((S, D), lambda j: (0, 0)),
                  pl.BlockSpec((None, D, tn), lambda j: ((first + j) // per, 0, (first + j) % per)),
                  pl.BlockSpec((1, tn), lambda j: (0, first + j))] + [ANY] * len(extra),
        out_specs=[out, out],
        out_shape=[jax.ShapeDtypeStruct((S, DFF), BF16)] * 2,
        input_output_aliases={3: 0, 4: 1} if into is not None else {},
        compiler_params=_params(1),
    )(h1b, w1_g, b1, *extra)


def _ff2_ln2_loss(a, w2_g, xhat1, g1, b1, b2, g2, be2, target):
    tm, tk = 512, 1024
    nk = DFF // tk

    def body(a_ref, w_ref, xh_ref, g1_ref, b1_ref, b2_ref, g2_ref, be2_ref, t_ref, d_ref, db_ref, st_ref, acc):
        i, k = pl.program_id(0), pl.program_id(1)

        @pl.when(k == 0)
        def _():
            acc[...] = jnp.zeros_like(acc)

        @pl.when((i == 0) & (k == 0))
        def _():
            st_ref[...] = jnp.zeros_like(st_ref)

        acc[...] += _dot(a_ref[...], w_ref[...])

        @pl.when(k == nk - 1)
        def _():
            def rows_chunk(ci, carry):
                rows = pl.ds(pl.multiple_of(ci * 128, 128), 128)
                h1 = xh_ref[rows, :] * g1_ref[...] + b1_ref[...]
                pre = ALPHA * h1 + acc[rows, :] + b2_ref[...]
                mean = jnp.mean(pre, axis=1, keepdims=True)
                cen = pre - mean
                var = jnp.mean(cen * cen, axis=1, keepdims=True)
                rstd = lax.rsqrt(var + LN_EPS)
                xhat = cen * rstd
                y = xhat * g2_ref[...] + be2_ref[...]
                err = y - t_ref[rows, :]
                dy = err * (1.0 / D)
                g = dy * g2_ref[...]
                dpre = rstd * (g - jnp.mean(g, axis=1, keepdims=True)
                               - xhat * jnp.mean(g * xhat, axis=1, keepdims=True))
                d_ref[rows, :] = dpre
                db_ref[rows, :] = dpre.astype(BF16)
                st_ref[0:1, :] += jnp.sum(dy * xhat, axis=0, keepdims=True)
                st_ref[1:2, :] += jnp.sum(dy, axis=0, keepdims=True)
                st_ref[2:3, :] += jnp.sum(dpre, axis=0, keepdims=True)
                st_ref[3:4, :] += jnp.broadcast_to(jnp.sum(err * err).reshape(1, 1), (1, D))
                return carry

            lax.fori_loop(0, tm // 128, rows_chunk, 0)

    row = pl.BlockSpec((tm, D), lambda i, k: (i, 0))
    vec = pl.BlockSpec((1, D), lambda i, k: (0, 0))
    return pl.pallas_call(
        body, name="ff2_ln2_loss", grid=(S // tm, nk),
        in_specs=[pl.BlockSpec((tm, tk), lambda i, k: (i, k)), pl.BlockSpec((tk, D), lambda i, k: (k, 0)),
                  row, vec, vec, vec, vec, vec, row],
        out_specs=[row, row, pl.BlockSpec((8, D), lambda i, k: (0, 0))],
        out_shape=[jax.ShapeDtypeStruct((S, D), F32), jax.ShapeDtypeStruct((S, D), BF16),
                   jax.ShapeDtypeStruct((8, D), F32)],
        scratch_shapes=[pltpu.VMEM((tm, D), F32)],
        compiler_params=_params(2),
    )(a, w2_g, xhat1, g1, b1, b2, g2, be2, target)


def _grad_w(act, dout, name, ti, tj, sharded, after=None):
    m, n = act.shape[1], dout.shape[1]
    ns = n // N_CHIPS
    per = ns // tj if sharded else None

    def body(a_ref, b_ref, o_ref, at_scr):
        @pl.when(pl.program_id(1) == 0)
        def _():
            at_scr[...] = a_ref[...].T

        o_ref[...] = _dot(at_scr[...], b_ref[...]).astype(BF16)

    if sharded:
        out_spec = pl.BlockSpec((None, ti, tj), lambda i, j: (j // per, i, j % per))
        out_shape = jax.ShapeDtypeStruct((N_CHIPS, m, ns), BF16)
    else:
        out_spec = pl.BlockSpec((ti, tj), lambda i, j: (i, j))
        out_shape = jax.ShapeDtypeStruct((m, n), BF16)
    body, more_specs, more = _behind(body, 2, after)
    return pl.pallas_call(
        body, name=name, grid=(m // ti, n // tj),
        in_specs=[pl.BlockSpec((S, ti), lambda i, j: (0, i)), pl.BlockSpec((S, tj), lambda i, j: (0, j))] + more_specs,
        out_specs=out_spec, out_shape=out_shape,
        scratch_shapes=[pltpu.VMEM((ti, S), BF16)],
        compiler_params=_params(2),
    )(act, dout, *more)


def _d_ff1(dpre2b, w2_g, r, after=None):
    tn = 512

    def body(d_ref, w_ref, r_ref, o_ref, gb_ref):
        da = _dot_nt(d_ref[...], w_ref[...])
        dp = da * (2.0 * r_ref[...].astype(F32))
        o_ref[...] = dp.astype(BF16)
        gb_ref[...] = jnp.sum(dp, axis=0, keepdims=True)

    body, more_specs, more = _behind(body, 3, after)
    return pl.pallas_call(
        body, name="d_ff1", grid=(DFF // tn,),
        in_specs=[pl.BlockSpec((S, D), lambda j: (0, 0)), pl.BlockSpec((tn, D), lambda j: (j, 0)),
                  pl.BlockSpec((S, tn), lambda j: (0, j))] + more_specs,
        out_specs=[pl.BlockSpec((S, tn), lambda j: (0, j)), pl.BlockSpec((1, tn), lambda j: (0, j))],
        out_shape=[jax.ShapeDtypeStruct((S, DFF), BF16), jax.ShapeDtypeStruct((1, DFF), F32)],
        compiler_params=_params(1),
    )(dpre2b, w2_g, r, *more)


def _d_h1_ln1(dprea, w1_g, dpre2, xhat1, rstd1, g1, after=None):
    tm, tk = 512, 1024
    per = D // tk
    nk = DFF // tk

    def body(a_ref, w_ref, d2_ref, xh_ref, rs_ref, g_ref, d_ref, db_ref, st_ref, acc):
        i, k = pl.program_id(0), pl.program_id(1)

        @pl.when(k == 0)
        def _():
            acc[...] = jnp.zeros_like(acc)

        @pl.when((i == 0) & (k == 0))
        def _():
            st_ref[...] = jnp.zeros_like(st_ref)

        acc[...] += _dot_nt(a_ref[...], w_ref[...])

        @pl.when(k == nk - 1)
        def _():
            def rows_chunk(ci, carry):
                rows = pl.ds(pl.multiple_of(ci * 128, 128), 128)
                dh = ALPHA * d2_ref[rows, :] + acc[rows, :]
                xhat = xh_ref[rows, :]
                g = dh * g_ref[...]
                dpre = rs_ref[rows, 0:1] * (g - jnp.mean(g, axis=1, keepdims=True)
                                            - xhat * jnp.mean(g * xhat, axis=1, keepdims=True))
                d_ref[rows, :] = dpre
                db_ref[rows, :] = dpre.astype(BF16)
                st_ref[0:1, :] += jnp.sum(dh * xhat, axis=0, keepdims=True)
                st_ref[1:2, :] += jnp.sum(dh, axis=0, keepdims=True)
                return carry

            lax.fori_loop(0, tm // 128, rows_chunk, 0)

    row = pl.BlockSpec((tm, D), lambda i, k: (i, 0))
    body, more_specs, more = _behind(body, 6, after)
    return pl.pallas_call(
        body, name="d_h1_ln1", grid=(S // tm, nk),
        in_specs=[pl.BlockSpec((tm, tk), lambda i, k: (i, k)),
                  pl.BlockSpec((None, D, tk), lambda i, k: (k // per, 0, k % per)),
                  row, row, pl.BlockSpec((tm, 128), lambda i, k: (i, 0)), pl.BlockSpec((1, D), lambda i, k: (0, 0))]
        + more_specs,
        out_specs=[row, row, pl.BlockSpec((8, D), lambda i, k: (0, 0))],
        out_shape=[jax.ShapeDtypeStruct((S, D), F32), jax.ShapeDtypeStruct((S, D), BF16),
                   jax.ShapeDtypeStruct((8, D), F32)],
        scratch_shapes=[pltpu.VMEM((tm, D), F32)],
        compiler_params=_params(2),
    )(dprea, w1_g, dpre2, xhat1, rstd1, g1, *more)


def _d_merged(dpre1b, wout_g, proj, ya, yb):
    tm, tn = 512, 1024

    def body(d_ref, w_ref, ga_ref, gb_ref, ya_ref, yb_ref, dya_ref, dyb_ref, dga_ref, dgb_ref):
        dm = _dot_nt(d_ref[...], w_ref[...])
        sa = _sigmoid(ga_ref[...])
        sb = _sigmoid(gb_ref[...])
        dya_ref[...] = (dm * sa).astype(BF16)
        dyb_ref[...] = (dm * sb).astype(BF16)
        dga_ref[...] = (dm * ya_ref[...].astype(F32) * sa * (1.0 - sa)).astype(BF16)
        dgb_ref[...] = (dm * yb_ref[...].astype(F32) * sb * (1.0 - sb)).astype(BF16)

    tile = pl.BlockSpec((tm, tn), lambda i, j: (i, j))
    return pl.pallas_call(
        body, name="d_merged", grid=(S // tm, D // tn),
        in_specs=[pl.BlockSpec((tm, D), lambda i, j: (i, 0)), pl.BlockSpec((tn, D), lambda i, j: (j, 0)),
                  pl.BlockSpec((tm, tn), lambda i, j: (i, 5 + j)), pl.BlockSpec((tm, tn), lambda i, j: (i, 7 + j)),
                  tile, tile],
        out_specs=[tile] * 4,
        out_shape=[jax.ShapeDtypeStruct((S, D), BF16)] * 4,
        compiler_params=_params(2),
    )(dpre1b, wout_g, proj, proj, ya, yb)


def _d_branches(dya, dyb, wpa_g, wpb_g, after=None):
    tm = 512
    ws = D // N_CHIPS

    def body(da_ref, db_ref, wa_ref, wb_ref, oa_ref, ob_ref):
        for d_ref, w_ref, o_ref in ((da_ref, wa_ref, oa_ref), (db_ref, wb_ref, ob_ref)):
            acc = _dot_nt(d_ref[:, 0:ws], w_ref[0])
            for s in range(1, N_CHIPS):
                acc = acc + _dot_nt(d_ref[:, s * ws:(s + 1) * ws], w_ref[s])
            o_ref[...] = acc

    rows = lambda width: pl.BlockSpec((tm, width), lambda i: (i, 0))
    whole = lambda n: pl.BlockSpec((N_CHIPS, n, ws), lambda i: (0, 0, 0))
    body, more_specs, more = _behind(body, 4, after)
    return pl.pallas_call(
        body, name="d_branches", grid=(S // tm,),
        in_specs=[rows(D), rows(D), whole(DA), whole(DB)] + more_specs,
        out_specs=[rows(DA), rows(DB)],
        out_shape=[jax.ShapeDtypeStruct((S, DA), F32), jax.ShapeDtypeStruct((S, DB), F32)],
        compiler_params=_params(1),
    )(dya, dyb, wpa_g, wpb_g, *more)


def _gmlp_bwd(proj, dgmlp, ws, ws_t, bsp_b, gain_v, bias_v):
    def body(u_ref, vb_ref, dg_ref, ws_ref, wst_ref, bsp_ref, g_ref, be_ref, duv_ref, gws_ref, gbs_ref, st_ref):
        @pl.when(pl.program_id(0) == 0)
        def _():
            gws_ref[...] = jnp.zeros_like(gws_ref)
            gbs_ref[...] = jnp.zeros_like(gbs_ref)
            st_ref[...] = jnp.zeros_like(st_ref)

        u, tu, u_act, vb, tv, rstd, xhat, vn = _gmlp_parts(u_ref, vb_ref, g_ref, be_ref)
        dg = dg_ref[...]
        dz = dg * u_act
        row = lax.broadcasted_iota(jnp.int32, (128, 128), 0)
        col = lax.broadcasted_iota(jnp.int32, (128, 128), 1)
        causal = row >= col
        causal_t = row <= col
        dvn_parts = []
        z_parts = []
        for g in range(NH):
            cols = slice(g * 128, (g + 1) * 128)
            vng = vn[:, cols].astype(BF16)
            dzg = dz[:, cols]
            dzb = dzg.astype(BF16)
            wsg = jnp.where(causal, ws_ref[g], 0.0).astype(BF16)
            wsg_t = jnp.where(causal_t, wst_ref[g], 0.0).astype(BF16)
            z_parts.append(_dot(wsg, vng) + bsp_ref[g])
            gws_ref[g] += jnp.where(causal, _dot_nt(dzb, vng), 0.0)
            gbs_ref[g] += jnp.broadcast_to(jnp.sum(dzg, axis=1, keepdims=True), (128, 128))
            dvn_parts.append(_dot(wsg_t, dzb))
        z = jnp.concatenate(z_parts, axis=1)
        dvn = jnp.concatenate(dvn_parts, axis=1)
        du = dg * z * _gelu_grad(u, tu)
        st_ref[0:1, :] += jnp.sum(dvn * xhat, axis=0, keepdims=True)
        st_ref[1:2, :] += jnp.sum(dvn, axis=0, keepdims=True)
        gg = dvn * g_ref[...]
        dgv = rstd * (gg - jnp.mean(gg, axis=1, keepdims=True) - xhat * jnp.mean(gg * xhat, axis=1, keepdims=True))
        dvb = dgv * _gelu_grad(vb, tv)
        duv_ref[:, 0:DB] = du.astype(BF16)
        duv_ref[:, DB:2 * DB] = dvb.astype(BF16)

    full3 = pl.BlockSpec((NH, 128, 128), lambda c: (0, 0, 0))
    vec = pl.BlockSpec((1, DB), lambda c: (0, 0))
    return pl.pallas_call(
        body, name="gmlp_bwd", grid=(NBLK,),
        in_specs=[pl.BlockSpec((128, DB), lambda c: (c, 3)), pl.BlockSpec((128, DB), lambda c: (c, 4)),
                  pl.BlockSpec((128, DB), lambda c: (c, 0)), full3, full3, full3, vec, vec],
        out_specs=[pl.BlockSpec((128, 2 * DB), lambda c: (c, 0)), full3, full3, pl.BlockSpec((8, DB), lambda c: (0, 0))],
        out_shape=[jax.ShapeDtypeStruct((S, 2 * DB), BF16), jax.ShapeDtypeStruct((NH, 128, 128), F32),
                   jax.ShapeDtypeStruct((NH, 128, 128), F32), jax.ShapeDtypeStruct((8, DB), F32)],
        compiler_params=_params(1),
    )(proj, proj, dgmlp, ws, ws_t, bsp_b, gain_v, bias_v)


def _rel_bias_grad(ds_sums):
    buckets = jnp.asarray(np.stack([_bucket_tile(d) for _, d in PATTERNS]))

    def body(bk_ref, ds_ref, o_ref):
        row = lax.broadcasted_iota(jnp.int32, (N_BUCKETS, 128), 0)
        lane = lax.broadcasted_iota(jnp.int32, (N_BUCKETS, 128), 1)

        def one_bucket(t, out):
            hits = [bk_ref[p] == t for p in range(3)]
            for h in range(NH):
                tot = jnp.zeros((128, 256), F32)
                for p in range(3):
                    tot = tot + jnp.where(hits[p], ds_ref[p, h], 0.0)
                out = jnp.where((row == t) & (lane == h), jnp.sum(tot), out)
            return out

        o_ref[...] = lax.fori_loop(0, N_BUCKETS, one_bucket, jnp.zeros((N_BUCKETS, 128), F32))

    return pl.pallas_call(
        body, name="rel_bias_grad",
        in_specs=[pl.BlockSpec(memory_space=pltpu.VMEM)] * 2, out_specs=pl.BlockSpec(memory_space=pltpu.VMEM),
        out_shape=jax.ShapeDtypeStruct((N_BUCKETS, 128), F32),
        compiler_params=pltpu.CompilerParams(vmem_limit_bytes=VMEM_LIMIT),
    )(buckets, ds_sums)


def _d_x(dproj, win_g, dpre1, after=None):
    tm, tn = 512, 512
    ws = DIN // N_CHIPS

    def body(a_ref, w_ref, d_ref, o_ref):
        acc = ALPHA * d_ref[...]
        for s in range(N_CHIPS):
            acc = acc + _dot_nt(a_ref[:, s * ws:(s + 1) * ws], w_ref[s])
        o_ref[...] = acc

    tile = pl.BlockSpec((tm, tn), lambda i, j: (i, j))
    body, more_specs, more = _behind(body, 3, after)
    return pl.pallas_call(
        body, name="d_x", grid=(S // tm, D // tn),
        in_specs=[pl.BlockSpec((tm, DIN), lambda i, j: (i, 0)),
                  pl.BlockSpec((N_CHIPS, tn, ws), lambda i, j: (0, j, 0)), tile] + more_specs,
        out_specs=tile, out_shape=jax.ShapeDtypeStruct((S, D), F32),
        compiler_params=_params(2),
    )(dproj, win_g, dpre1, *more)


def _adamw(w, g, m, v, name, after=None):
    rows, cols = w.shape
    tm = max(t for t in range(8, 257, 8) if rows % t == 0)

    def body(w_ref, g_ref, m_ref, v_ref, d_ref, nm_ref, nv_ref, go_ref):
        g = g_ref[...]
        m = ADAM_B1 * m_ref[...] + (1.0 - ADAM_B1) * g
        v = ADAM_B2 * v_ref[...] + (1.0 - ADAM_B2) * (g * g)
        m_hat = m / (1.0 - ADAM_B1 ** ADAM_STEP)
        v_hat = v / (1.0 - ADAM_B2 ** ADAM_STEP)
        d_ref[...] = -ADAM_LR * (m_hat / (jnp.sqrt(v_hat) + ADAM_EPS) + ADAM_WD * w_ref[...])
        nm_ref[...] = m
        nv_ref[...] = v
        go_ref[...] = g

    spec = pl.BlockSpec((tm, cols), lambda i: (i, 0))
    body, more_specs, more = _behind(body, 4, after)
    return pl.pallas_call(
        body, name=name, grid=(rows // tm,), in_specs=[spec] * 4 + more_specs, out_specs=[spec] * 4,
        out_shape=[jax.ShapeDtypeStruct((rows, cols), F32)] * 4, compiler_params=_params(1),
    )(w, g, m, v, *more)


def _position():
    x, y, c = lax.axis_index("x"), lax.axis_index("y"), lax.axis_index("c")
    chips = [(1 - x, y), (x, 1 - y), (1 - x, 1 - y)]
    return x, y, c, chips


def _remote(src, dst, send_sems, recv_sems, k, to):
    return pltpu.make_async_remote_copy(src_ref=src, dst_ref=dst, send_sem=send_sems.at[k], recv_sem=recv_sems.at[k],
                                        device_id=to, device_id_type=MESH)


def _place_shard(w, name, after=None):
    rows, cols = w.shape
    tm = 256
    x, y = lax.axis_index("x"), lax.axis_index("y")

    def body(chip_ref, w_ref, o_ref):
        o_ref[...] = w_ref[...].astype(BF16)

    more_specs, more = ([ANY], [after]) if after is not None else ([], [])
    if after is not None:
        inner = body
        body = lambda chip_ref, w_ref, after_ref, o_ref: inner(chip_ref, w_ref, o_ref)
    return pl.pallas_call(
        body, name=name,
        grid_spec=pltpu.PrefetchScalarGridSpec(
            num_scalar_prefetch=1, grid=(rows // tm,),
            in_specs=[pl.BlockSpec((tm, cols), lambda i, chip: (i, 0))] + more_specs,
            out_specs=pl.BlockSpec((None, tm, cols), lambda i, chip: (chip[0], i, 0))),
        out_shape=jax.ShapeDtypeStruct((N_CHIPS, rows, cols), BF16),
        compiler_params=_params(1),
    )(jnp.reshape(2 * x + y, (1,)).astype(jnp.int32), w, *more)


def _to_bf16(x, name, after=None):
    tm = 256

    def body(x_ref, o_ref):
        o_ref[...] = x_ref[...].astype(BF16)

    spec = pl.BlockSpec((tm, x.shape[1]), lambda i: (i, 0))
    body, more_specs, more = _behind(body, 1, after)
    return pl.pallas_call(
        body, name=name, grid=(x.shape[0] // tm,), in_specs=[spec] + more_specs, out_specs=spec,
        out_shape=jax.ShapeDtypeStruct(x.shape, BF16), compiler_params=_params(1),
    )(x, *more)


HBM = pl.BlockSpec(memory_space=pltpu.HBM)
SEM = pl.BlockSpec(memory_space=pltpu.SEMAPHORE)
EFFECT = pltpu.SideEffectType.DATAFLOW_SIDE_EFFECTING


def _comm_call(name, body, bufs, sems_in, sems_out, after=None, token=False):
    nb, ns, no = len(bufs), len(sems_in), len(sems_out)
    n_in = nb + ns + (after is not None)

    def wrapped(*refs):
        body(refs[:nb], refs[nb:nb + ns], refs[n_in + nb:n_in + nb + no])
        if token:
            refs[-1][...] = jnp.zeros((8, 128), F32)

    outs = pl.pallas_call(
        wrapped, name=name,
        in_specs=[HBM] * nb + [SEM] * ns + ([ANY] if after is not None else []),
        out_specs=[HBM] * nb + [SEM] * no + ([pl.BlockSpec(memory_space=pltpu.VMEM)] if token else []),
        out_shape=[pltpu.HBM(b.shape, b.dtype) for b in bufs] + [pltpu.SemaphoreType.DMA((k,)) for k in sems_out]
        + ([jax.ShapeDtypeStruct((8, 128), F32)] if token else []),
        input_output_aliases={i: i for i in range(nb)},
        compiler_params=pltpu.CompilerParams(has_side_effects=EFFECT),
    )(*[pltpu.with_memory_space_constraint(b, pltpu.HBM) for b in bufs], *sems_in, *([after] if after is not None else []))
    return list(outs[:nb]), list(outs[nb:nb + no]), (outs[-1] if token else None)


RING_STAGES = {"ici_near": 2, "ici_far": 2, "d2d_near": 2, "d2d_far": 1}


def _ring_copies(buf, send_sems, recv_sems, k0, stage):
    x, y, c, _ = _position()
    hr = buf.shape[1] // 2
    qr = hr // 2
    half = lambda chip, h: buf.at[chip, pl.ds(h * hr, hr), :]
    quarter = lambda chip, h, q: buf.at[chip, pl.ds(h * hr + q * qr, qr), :]
    mine, x_chip, y_chip, far_chip = 2 * x + y, 2 * (1 - x) + y, 2 * x + (1 - y), 2 * (1 - x) + (1 - y)
    to_x, to_y, sibling = (1 - x, y, c), (x, 1 - y, c), (x, y, 1 - c)
    if stage == "ici_near":
        moves = [(half(mine, c), to_x, half(x_chip, c)), (half(mine, c), to_y, half(y_chip, c))]
    elif stage == "ici_far":
        moves = [(quarter(x_chip, c, 0), to_y, quarter(far_chip, c, 0)),
                 (quarter(y_chip, c, 1), to_x, quarter(far_chip, c, 1))]
    elif stage == "d2d_near":
        moves = [(half(x_chip, c), sibling, half(x_chip, 1 - c)), (half(y_chip, c), sibling, half(y_chip, 1 - c))]
    else:
        moves = [(half(far_chip, c), sibling, half(far_chip, 1 - c))]
    sends = [_remote(src, src, send_sems, recv_sems, k0 + i, to) for i, (src, to, _) in enumerate(moves)]
    arrivals = [_remote(got, got, send_sems, recv_sems, k0 + i, (x, y, c)) for i, (_, _, got) in enumerate(moves)]
    return sends, arrivals


def _ring_call(name, groups, actions, after=None):
    tags = list(dict.fromkeys(t for _, t, _ in actions))
    counts = {t: len(groups[t]["bufs"]) for t in tags}
    first = {t: sum(counts[u] for u in tags[:i]) for i, t in enumerate(tags)}
    waits = [(t, s) for v, t, s in actions if v == "wait"]
    starts = [(t, s) for v, t, s in actions if v == "start"]

    def body(bufs, sems_in, sems_out):
        for verb, t, s in actions:
            at, sems = (starts.index((t, s)), sems_out) if verb == "start" else (waits.index((t, s)), sems_in)
            for w in range(counts[t]):
                sends, arrivals = _ring_copies(bufs[first[t] + w], sems[2 * at], sems[2 * at + 1], RING_STAGES[s] * w, s)
                if verb == "start":
                    for cp in sends:
                        cp.start()
                else:
                    for cp in arrivals:
                        cp.wait_recv()
                    for cp in sends:
                        cp.wait_send()

    bufs, sems, token = _comm_call(
        name, body, [b for t in tags for b in groups[t]["bufs"]],
        [sem for t, s in waits for sem in groups[t]["sems"][s]],
        [RING_STAGES[s] * counts[t] for t, s in starts for _ in (0, 1)], after, token=True)
    for t in tags:
        groups[t]["bufs"] = bufs[first[t]:first[t] + counts[t]]
    for t, s in waits:
        del groups[t]["sems"][s]
    for i, (t, s) in enumerate(starts):
        groups[t]["sems"][s] = (sems[2 * i], sems[2 * i + 1])
    return token


def _cx_copies(src, dst, send_sems, recv_sems, k0):
    x, y, c, chips = _position()
    sends = [_remote(src.at[2 * cx + cy], dst.at[2 * x + y], send_sems, recv_sems, k0 + j, (cx, cy, c))
             for j, (cx, cy) in enumerate(chips)]
    arrivals = [_remote(dst.at[2 * cx + cy], dst.at[2 * cx + cy], send_sems, recv_sems, k0 + j, (x, y, c))
                for j, (cx, cy) in enumerate(chips)]
    return sends, arrivals


def _cx_start(name, pair_sums):
    n = len(pair_sums)
    landing = [lax.empty(p.shape, p.dtype) for p in pair_sums]

    def body(bufs, _, sems):
        for w in range(n):
            for cp in _cx_copies(bufs[w], bufs[n + w], sems[0], sems[1], 3 * w)[0]:
                cp.start()

    bufs, sems, token = _comm_call(name, body, list(pair_sums) + landing, [], [3 * n, 3 * n], token=True)
    return (bufs, sems), token


def _cx_wait(name, state, after):
    bufs, sems = state
    n = len(bufs) // 2

    def body(refs, sems_in, _):
        for w in range(n):
            sends, arrivals = _cx_copies(refs[w], refs[n + w], sems_in[0], sems_in[1], 3 * w)
            for cp in arrivals:
                cp.wait_recv()
            for cp in sends:
                cp.wait_send()

    bufs, _, _ = _comm_call(name, body, bufs, sems, [], after)
    return bufs[:n], bufs[n:]


def _px_copies(src, dst, send_sems, recv_sems, k):
    x, y, c, _ = _position()
    hr = src.shape[1] // 2
    send = _remote(src.at[:, pl.ds((1 - c) * hr, hr), :], dst, send_sems, recv_sems, k, (x, y, 1 - c))
    arrival = _remote(dst, dst, send_sems, recv_sems, k, (x, y, c))
    return send, arrival


def _px_start(name, grads):
    n = len(grads)
    landing = [lax.empty((N_CHIPS, g.shape[1] // 2, g.shape[2]), g.dtype) for g in grads]

    def body(bufs, _, sems):
        for w in range(n):
            _px_copies(bufs[w], bufs[n + w], sems[0], sems[1], w)[0].start()

    bufs, sems, token = _comm_call(name, body, list(grads) + landing, [], [n, n], token=True)
    return (bufs, sems), token


def _px_wait(name, state, after):
    bufs, sems = state
    n = len(bufs) // 2

    def body(refs, sems_in, _):
        for w in range(n):
            send, arrival = _px_copies(refs[w], refs[n + w], sems_in[0], sems_in[1], w)
            arrival.wait_recv()
            send.wait_send()

    bufs, _, _ = _comm_call(name, body, bufs, sems, [], after)
    return bufs[:n], bufs[n:]


def _pair_sum(grad, got, name):
    _, rows, cols = grad.shape
    hr = rows // 2
    tm = min(hr, 512)
    nb = hr // tm
    c = lax.axis_index("c")

    def body(c_ref, g_ref, o_ref, out_ref):
        out_ref[...] = (g_ref[...].astype(F32) + o_ref[...].astype(F32)).astype(BF16)

    return pl.pallas_call(
        body, name=name,
        grid_spec=pltpu.PrefetchScalarGridSpec(
            num_scalar_prefetch=1, grid=(N_CHIPS, nb),
            in_specs=[pl.BlockSpec((None, tm, cols), lambda s, i, c_ref: (s, c_ref[0] * nb + i, 0)),
                      pl.BlockSpec((None, tm, cols), lambda s, i, c_ref: (s, i, 0))],
            out_specs=pl.BlockSpec((None, tm, cols), lambda s, i, c_ref: (s, i, 0))),
        out_shape=jax.ShapeDtypeStruct((N_CHIPS, hr, cols), BF16),
        compiler_params=_params(2),
    )(jnp.reshape(c, (1,)).astype(jnp.int32), grad, got)


def _chip_sum(parts, pair_sums, name):
    _, hr, cols = parts.shape
    tm = min(hr, 512)
    nb = hr // tm
    x, y, c = lax.axis_index("x"), lax.axis_index("y"), lax.axis_index("c")

    def body(pos_ref, p_ref, own_ref, o_ref):
        chip = pos_ref[0]
        own = own_ref[...].astype(F32)
        term = lambda s: jnp.where(chip == s, own, p_ref[s].astype(F32))
        o_ref[...] = ((term(0) + term(1)) + term(2)) + term(3)

    return pl.pallas_call(
        body, name=name,
        grid_spec=pltpu.PrefetchScalarGridSpec(
            num_scalar_prefetch=1, grid=(nb,),
            in_specs=[pl.BlockSpec((N_CHIPS, tm, cols), lambda i, pos: (0, i, 0)),
                      pl.BlockSpec((None, tm, cols), lambda i, pos: (pos[0], i, 0))],
            out_specs=pl.BlockSpec((tm, cols), lambda i, pos: (pos[1] * nb + i, 0))),
        out_shape=jax.ShapeDtypeStruct((2 * hr, cols), F32), compiler_params=_params(1),
    )(jnp.stack([2 * x + y, c]).astype(jnp.int32), parts, pair_sums)


def _share_copies(buf, send_sems, recv_sems, k):
    x, y, c, _ = _position()
    hr = buf.shape[0] // 2
    mine, theirs = buf.at[pl.ds(c * hr, hr), :], buf.at[pl.ds((1 - c) * hr, hr), :]
    return (_remote(mine, mine, send_sems, recv_sems, k, (x, y, 1 - c)),
            _remote(theirs, theirs, send_sems, recv_sems, k, (x, y, c)))


def _share_start(name, bufs):
    n = len(bufs)

    def body(refs, _, sems):
        for w in range(n):
            _share_copies(refs[w], sems[0], sems[1], w)[0].start()

    bufs, sems, token = _comm_call(name, body, list(bufs), [], [n, n], token=True)
    return (bufs, sems), token


def _share_wait(name, state, after):
    bufs, sems = state

    def body(refs, sems_in, _):
        for w in range(len(bufs)):
            send, arrival = _share_copies(refs[w], sems_in[0], sems_in[1], w)
            arrival.wait_recv()
            send.wait_send()

    return _comm_call(name, body, bufs, sems, [], after)[0]


def _allreduce_small(g):
    rows = g.shape[0]
    half = rows // 2

    def body(g_ref, o_ref, sib, slots, send_sems, recv_sems):
        x, y, c, chips = _position()
        me, sibling = (x, y, c), (x, y, 1 - c)
        my_chip = 2 * x + y
        mine = pl.ds(pl.multiple_of(c * half, 8), half)
        theirs = pl.ds(pl.multiple_of((1 - c) * half, 8), half)
        pair = _remote(g_ref.at[theirs], sib, send_sems, recv_sems, 0, sibling)
        pair.start()
        pair.wait()
        slots[my_chip] = g_ref[mine, :] + sib[...]
        sent = []
        for j, (cx, cy) in enumerate(chips):
            cp = _remote(slots.at[my_chip], slots.at[my_chip], send_sems, recv_sems, 1 + j, (cx, cy, c))
            cp.start()
            sent.append(cp)
        for j, (cx, cy) in enumerate(chips):
            got = slots.at[2 * cx + cy]
            _remote(got, got, send_sems, recv_sems, 1 + j, me).wait_recv()
        for cp in sent:
            cp.wait_send()
        o_ref[mine, :] = ((slots[0] + slots[1]) + slots[2]) + slots[3]
        swap = _remote(o_ref.at[mine], o_ref.at[mine], send_sems, recv_sems, 4, sibling)
        swap.start()
        swap.wait()

    vm = pl.BlockSpec(memory_space=pltpu.VMEM)
    return pl.pallas_call(
        body, name="allreduce_small",
        in_specs=[vm], out_specs=vm, out_shape=jax.ShapeDtypeStruct((rows, 128), F32),
        scratch_shapes=[pltpu.VMEM((half, 128), F32), pltpu.VMEM((N_CHIPS, half, 128), F32),
                        pltpu.SemaphoreType.DMA((5,)), pltpu.SemaphoreType.DMA((5,))],
        compiler_params=pltpu.CompilerParams(vmem_limit_bytes=VMEM_LIMIT),
    )(g)


_SMALL =("rel_bias", "ln_v_gain", "ln_v_bias", "w_spatial", "b_spatial", "ln1_gain", "ln1_bias",
          "b_ff1", "b_ff2", "ln2_gain", "ln2_bias")
_SMALL_ROWS = 1200
_LOSS_AT = (152832 // 128, 0)


def _pack_small(parts):
    flat = jnp.concatenate([parts[k].reshape(-1).astype(F32) for k in _SMALL])
    flat = jnp.pad(flat, (0, _SMALL_ROWS * 128 - flat.shape[0]))
    return flat.reshape(_SMALL_ROWS, 128)


def _unpack_small(packed, like):
    flat = packed.reshape(-1)
    out, at = {}, 0
    for k in _SMALL:
        n = math.prod(like[k].shape)
        out[k] = flat[at:at + n].reshape(like[k].shape)
        at += n
    return out


def kernel(x, w_in, rel_bias, ln_v_gain, ln_v_bias, w_spatial, b_spatial, w_proj_a, w_proj_b, w_out, ln1_gain, ln1_bias, w_ff1, b_ff1, w_ff2, b_ff2, ln2_gain, ln2_bias, loss_target, m_w_in, m_rel_bias, m_ln_v_gain, m_ln_v_bias, m_w_spatial, m_b_spatial, m_w_proj_a, m_w_proj_b, m_w_out, m_ln1_gain, m_ln1_bias, m_w_ff1, m_b_ff1, m_w_ff2, m_b_ff2, m_ln2_gain, m_ln2_bias, v_w_in, v_rel_bias, v_ln_v_gain, v_ln_v_bias, v_w_spatial, v_b_spatial, v_w_proj_a, v_w_proj_b, v_w_out, v_ln1_gain, v_ln1_bias, v_w_ff1, v_b_ff1, v_w_ff2, v_b_ff2, v_ln2_gain, v_ln2_bias):
    args = dict(locals())
    big = ("w_in", "w_proj_a", "w_proj_b", "w_out", "w_ff1", "w_ff2")
    weights = ("w_in", "rel_bias", "ln_v_gain", "ln_v_bias", "w_spatial", "b_spatial", "w_proj_a", "w_proj_b", "w_out",
               "ln1_gain", "ln1_bias", "w_ff1", "b_ff1", "w_ff2", "b_ff2", "ln2_gain", "ln2_bias")

    xs = x[0]
    target = loss_target[0]

    ring = {"a": {"bufs": [_place_shard(w_in[0], "place_w_in")], "sems": {}}}
    tok = _ring_call("allgather_a_near", ring, [("start", "a", "ici_near")])
    placed = [_place_shard(args[k][0], f"place_{k}", after=tok) for k in big[1:]]
    for tag, bufs in (("b", placed[0:3]), ("c", placed[3:4]), ("d", placed[4:5])):
        ring[tag] = {"bufs": bufs, "sems": {}}
    xb = _to_bf16(xs, "x_to_bf16", after=placed[4])

    mx, my = lax.axis_index("x"), lax.axis_index("y")
    own = jnp.reshape(2 * mx + my, (1,)).astype(jnp.int32)
    near = jnp.stack([2 * (1 - mx) + my, 2 * mx + (1 - my)]).astype(jnp.int32)
    far = jnp.reshape(2 * (1 - mx) + (1 - my), (1,)).astype(jnp.int32)
    proj = _proj(xb, ring["a"]["bufs"][0], own, "proj_own")
    _ring_call("allgather_a_far", ring, [("wait", "a", "ici_near"), ("start", "a", "ici_far"), ("start", "a", "d2d_near"),
                                         ("start", "b", "ici_near"), ("start", "c", "ici_near")], after=proj)
    _ring_call("allgather_a_near_done", ring, [("wait", "a", "d2d_near")])
    proj = _proj(xb, ring["a"]["bufs"][0], near, "proj_near", into=proj)
    _ring_call("allgather_a_last", ring, [("wait", "a", "ici_far"), ("start", "a", "d2d_far")], after=proj)
    _ring_call("allgather_a_done", ring, [("wait", "a", "d2d_far")])
    (win_g,) = ring["a"]["bufs"]
    proj = _proj(xb, win_g, far, "proj_far", into=proj)
    _ring_call("allgather_b_far", ring, [("wait", "b", "ici_near"), ("start", "b", "ici_far"), ("start", "b", "d2d_near")],
               after=proj)
    ws = w_spatial[0]
    ws_t = jnp.transpose(ws, (0, 2, 1))
    bsp_b = jnp.broadcast_to(b_spatial[0][:, :, None], (NH, 128, 128))
    gmlp = _gmlp_fwd(proj, ws, bsp_b, ln_v_gain, ln_v_bias)
    bias = _bias_tiles(rel_bias)
    attn, lse = _attention_fwd(proj, bias)
    _ring_call("allgather_b_last_c_far", ring,
               [("wait", "b", "ici_far"), ("start", "b", "d2d_far"),
                ("wait", "c", "ici_near"), ("start", "c", "ici_far"), ("start", "c", "d2d_near"),
                ("start", "d", "ici_near")], after=attn)
    _ring_call("allgather_b_done", ring, [("wait", "b", "d2d_near"), ("wait", "b", "d2d_far")])
    wpa_g, wpb_g, wout_g = ring["b"]["bufs"]
    wout_full = wout_g.reshape(D, D)
    ya, yb, merged = _branch(attn, gmlp, wpa_g, wpb_g, proj)
    xhat1, rstd1, h1b = _out_ln1(merged, wout_full, xs, ln1_gain, ln1_bias)
    _ring_call("allgather_c_last", ring, [("wait", "c", "ici_far"), ("start", "c", "d2d_far")], after=h1b)
    _ring_call("allgather_c_done", ring, [("wait", "c", "d2d_near"), ("wait", "c", "d2d_far")])
    (w1_g,) = ring["c"]["bufs"]
    a, r = _ff1(h1b, w1_g, b_ff1, 0, "ff1_first")
    tok = _ring_call("allgather_d_far", ring,
                     [("wait", "d", "ici_near"), ("start", "d", "ici_far"), ("start", "d", "d2d_near")], after=a)
    a, r = _ff1(h1b, w1_g, b_ff1, 1, "ff1_second", into=(a, r), after=tok)
    _ring_call("allgather_d_last", ring, [("wait", "d", "ici_far"), ("start", "d", "d2d_far")], after=a)
    _ring_call("allgather_d_done", ring, [("wait", "d", "d2d_near"), ("wait", "d", "d2d_far")])
    (w2_g,) = ring["d"]["bufs"]
    w2_full = w2_g.reshape(DFF, D)
    dpre2, dpre2b, st2 = _ff2_ln2_loss(a, w2_full, xhat1, ln1_gain, ln1_bias, b_ff2, ln2_gain, ln2_bias, target)

    def pair_and_chip(tag, state, after):
        local, from_sibling = _px_wait(f"pair_exchange_wait_{tag}", state, after)
        pair_sums = [_pair_sum(g, o, f"pair_sum_{tag}_{i}") for i, (g, o) in enumerate(zip(local, from_sibling))]
        return _cx_start(f"chip_exchange_start_{tag}", pair_sums)

    g_w2 = _grad_w(a, dpre2b, "grad_w_ff2", 512, 2048, False)
    px, tok = _px_start("pair_exchange_start_w_ff2", [g_w2.reshape(N_CHIPS, DFF // N_CHIPS, D)])
    dprea, g_b1 = _d_ff1(dpre2b, w2_full, r, after=tok)
    cx_w2, tok = pair_and_chip("w_ff2", px, dprea)
    g_w1 = _grad_w(h1b, dprea, "grad_w_ff1", 512, 2048, True, after=tok)
    px, tok = _px_start("pair_exchange_start_w_ff1", [g_w1])
    dpre1, dpre1b, st1 = _d_h1_ln1(dprea, w1_g, dpre2, xhat1, rstd1, ln1_gain, after=tok)
    cx_w1, tok = pair_and_chip("w_ff1", px, dpre1b)
    g_wout = _grad_w(merged, dpre1b, "grad_w_out", 512, 2048, False, after=tok)
    dya, dyb, dga, dgb = _d_merged(dpre1b, wout_full, proj, ya, yb)
    g_wpa = _grad_w(attn, dya, "grad_w_proj_a", 1024, 512, True)
    g_wpb = _grad_w(gmlp, dyb, "grad_w_proj_b", 1024, 512, True)
    px, tok = _px_start("pair_exchange_start_b", [g_wpa, g_wpb, g_wout.reshape(N_CHIPS, D // N_CHIPS, D)])
    dattn, dgmlp = _d_branches(dya, dyb, wpa_g, wpb_g, after=tok)
    duv, g_ws, g_bs, stv = _gmlp_bwd(proj, dgmlp, ws, ws_t, bsp_b, ln_v_gain, ln_v_bias)
    cx_b, tok = pair_and_chip("b", px, duv)
    dq, dk, dv, ds_sums = _attention_bwd(proj, dattn, attn, lse, bias, after=tok)
    g_rb = _rel_bias_grad(ds_sums)[:, :NH]

    small_g = dict(rel_bias=g_rb, ln_v_gain=stv[0], ln_v_bias=stv[1], w_spatial=g_ws, b_spatial=g_bs[:, :, 0],
                   ln1_gain=st1[0], ln1_bias=st1[1], b_ff1=g_b1, b_ff2=st2[2], ln2_gain=st2[0], ln2_bias=st2[1])
    gs = _allreduce_small(_pack_small(small_g).at[_LOSS_AT].set(st2[3, 0]))
    ds_, ms_, vs_, _ = _adamw(_pack_small({k: args[k] for k in _SMALL}), gs,
                           _pack_small({k: args["m_" + k] for k in _SMALL}),
                           _pack_small({k: args["v_" + k] for k in _SMALL}), "adamw_small")
    like = {k: args[k] for k in _SMALL}
    grads, deltas, new_m, new_v = (_unpack_small(t, like) for t in (gs, ds_, ms_, vs_))

    dproj = jnp.concatenate([dq, dk, dv, duv, dga, dgb], axis=1)
    g_win = _grad_w(xb, dproj, "grad_w_in", 512, 2304, True, after=gs)
    px, tok = _px_start("pair_exchange_start_w_in", [g_win])

    def chip_sums(tag, state, names, after):
        pair_sums, from_chips = _cx_wait(f"chip_exchange_wait_{tag}", state, after)
        halves = [_chip_sum(p, own, f"chip_sum_{k}") for p, own, k in zip(from_chips, pair_sums, names)]
        return _share_start(f"share_start_{tag}", halves)

    def adam_one(k, g, after=None):
        d_, m_, v_, g_ = _adamw(args[k][0], g, args["m_" + k][0], args["v_" + k][0], f"adamw_{k}", after=after)
        grads[k], deltas[k], new_m[k], new_v[k] = g_[None], d_[None], m_[None], v_[None]
        return d_

    def adam(tag, state, names, after):
        last = None
        for k, g in zip(names, _share_wait(f"share_wait_{tag}", state, after)):
            last = adam_one(k, g)
        return last

    sh_w2, tok = chip_sums("w_ff2", cx_w2, ["w_ff2"], tok)
    sh_w1, tok = chip_sums("w_ff1", cx_w1, ["w_ff1"], tok)
    sh_b, tok = chip_sums("b", cx_b, ["w_proj_a", "w_proj_b", "w_out"], tok)
    cx_in, tok = pair_and_chip("w_in", px, tok)
    grad_x = _d_x(dproj, win_g, dpre1, after=tok)
    done = adam("w_ff2", sh_w2, ["w_ff2"], grad_x)
    done = adam("w_ff1", sh_w1, ["w_ff1"], done)
    g_wpa_full, g_wpb_full, g_wout_full = _share_wait("share_wait_b", sh_b, done)
    done = adam_one("w_out", g_wout_full)
    sh_in, tok = chip_sums("w_in", cx_in, ["w_in"], done)
    done = adam_one("w_proj_a", g_wpa_full, after=tok)
    done = adam_one("w_proj_b", g_wpb_full, after=done)
    adam("w_in", sh_in, ["w_in"], done)

    loss = gs[_LOSS_AT] * (0.5 / D)
    return (loss, grad_x[None], *[grads[k] for k in weights], *[deltas[k] for k in weights],
            *[new_m[k] for k in weights], *[new_v[k] for k in weights])
```

```python
import math

import numpy as np
import jax
import jax.numpy as jnp
from jax import lax
from jax.experimental import pallas as pl
from jax.experimental.pallas import tpu as pltpu

F32 = jnp.float32
BF16 = jnp.bfloat16

S = 2048
D = 2048
DA = 1024
DB = 1024
DFF = 8192
DIN = 9216
NH = 8
HD = 128
NBLK = 16
PATTERNS = ((128, 1), (512, 4), (2048, 16))
N_BUCKETS = 32
MAX_DISTANCE = 2048
ALPHA = 2.0 ** 0.25
LN_EPS = 1e-5
NEG_INF = -1e30
SCALE = HD ** -0.5
N_CHIPS = 4

ADAM_LR = 0.001
ADAM_B1 = 0.9
ADAM_B2 = 0.999
ADAM_EPS = 1e-08
ADAM_WD = 0.01
ADAM_STEP = 10

VMEM_LIMIT = 56 * 1024 * 1024
MESH = pl.DeviceIdType.MESH
ANY = pl.BlockSpec(memory_space=pl.ANY)


def _params(n_axes, vmem=VMEM_LIMIT):
    return pltpu.CompilerParams(dimension_semantics=("arbitrary",) * n_axes, vmem_limit_bytes=vmem)


def _bucket_tile(dilation):
    qi = np.arange(128)[:, None]
    kj = np.arange(256)[None, :]
    n = np.clip(128 + qi - kj, 0, 128) * dilation
    max_exact = N_BUCKETS // 2
    nf = np.maximum(n, 1).astype(np.float32)
    large = max_exact + (np.log(nf / np.float32(max_exact)) / np.float32(math.log(MAX_DISTANCE / max_exact))
                         * np.float32(N_BUCKETS - max_exact)).astype(np.int32)
    large = np.minimum(large, N_BUCKETS - 1)
    return np.where(n < max_exact, n, large).astype(np.int32)


def _gelu(x):
    c = math.sqrt(2.0 / math.pi)
    t = jnp.tanh(c * (x + 0.044715 * x * x * x))
    return 0.5 * x * (1.0 + t), t


def _gelu_grad(x, t):
    c = math.sqrt(2.0 / math.pi)
    return 0.5 * (1.0 + t) + 0.5 * x * (1.0 - t * t) * c * (1.0 + 3.0 * 0.044715 * x * x)


def _sigmoid(x):
    return 1.0 / (1.0 + jnp.exp(-x))


def _dot(a, b):
    return jnp.dot(a, b, preferred_element_type=F32)


def _behind(body, n_in, after):
    if after is None:
        return body, [], []
    return (lambda *refs: body(*refs[:n_in], *refs[n_in + 1:])), [ANY], [after]


def _dot_nt(a, b):
    return lax.dot_general(a, b, (((1,), (1,)), ((), ())), preferred_element_type=F32)


def _proj(xb, win_g, shards, name, into=None):
    tn = 768
    per = 2304 // tn

    def body(shards_ref, x_ref, w_ref, *rest):
        rest[-1][...] = _dot(x_ref[...], w_ref[...])

    in_specs = [pl.BlockSpec((S, D), lambda j, sh: (0, 0)),
                pl.BlockSpec((None, D, tn), lambda j, sh: (sh[j // per], 0, j % per))]
    return pl.pallas_call(
        body, name=name,
        grid_spec=pltpu.PrefetchScalarGridSpec(
            num_scalar_prefetch=1, grid=(shards.shape[0] * per,),
            in_specs=in_specs + ([ANY] if into is not None else []),
            out_specs=pl.BlockSpec((S, tn), lambda j, sh: (0, sh[j // per] * per + j % per))),
        out_shape=jax.ShapeDtypeStruct((S, DIN), F32),
        input_output_aliases={3: 0} if into is not None else {},
        compiler_params=_params(1),
    )(shards, xb, win_g, *([into] if into is not None else []))


FWD_HEADS_PER_STEP = 4
BWD_HEADS_PER_STEP = 2


def _bias_tiles(rel_bias):
    buckets = jnp.asarray(np.stack([_bucket_tile(d) for _, d in PATTERNS]))

    def body(rb_ref, bk_ref, o_ref):
        qi = lax.broadcasted_iota(jnp.int32, (128, 256), 0)
        kj = lax.broadcasted_iota(jnp.int32, (128, 256), 1)
        steps = 128 + qi - kj
        band = (steps >= 0) & (steps <= 128)
        o_ref[...] = jnp.zeros_like(o_ref)
        for p in range(len(PATTERNS)):
            bucket = bk_ref[p]

            def one_bucket(t, carry):
                hit = bucket == t
                for h in range(NH):
                    o_ref[p, h] = jnp.where(hit, rb_ref[t, h], o_ref[p, h])
                return carry

            lax.fori_loop(0, N_BUCKETS, one_bucket, 0)
            for h in range(NH):
                o_ref[p, h] = jnp.where(band, o_ref[p, h], NEG_INF)

    return pl.pallas_call(
        body, name="bias_tiles",
        in_specs=[pl.BlockSpec(memory_space=pltpu.SMEM), pl.BlockSpec(memory_space=pltpu.VMEM)],
        out_specs=pl.BlockSpec(memory_space=pltpu.VMEM),
        out_shape=jax.ShapeDtypeStruct((len(PATTERNS), NH, 128, 256), F32),
        compiler_params=pltpu.CompilerParams(vmem_limit_bytes=VMEM_LIMIT),
    )(rel_bias, buckets)


def _block_rows(b, dilation):
    nblk = NBLK // dilation
    r, n = b // nblk, b % nblk
    start = r + n * (128 * dilation)
    prev_start = jnp.maximum(start - 128 * dilation, r)
    if dilation == 1:
        return pl.ds(pl.multiple_of(start, 128), 128), pl.ds(pl.multiple_of(prev_start, 128), 128), n > 0
    return pl.ds(start, 128, stride=dilation), pl.ds(prev_start, 128, stride=dilation), n > 0


def _head_specs(first, hps):
    return [pl.BlockSpec((S, HD), lambda g, j=j: (0, first + g * hps + j)) for j in range(hps)]


def _bias_spec(hps):
    return pl.BlockSpec((len(PATTERNS), hps, 128, 256), lambda g: (0, g, 0, 0))


def _heads_spec(hps):
    return pl.BlockSpec((S, hps * HD), lambda g: (0, g))


def _attention_fwd(proj, bias):
    hps = FWD_HEADS_PER_STEP

    def body(bias_ref, *refs):
        q_refs, k_refs, v_refs = (refs[i * hps:(i + 1) * hps] for i in range(3))
        o_ref, lse_ref = refs[3 * hps:3 * hps + 2]
        acc_scrs, m_scrs, l_scrs = (refs[3 * hps + 2 + i * hps:3 * hps + 2 + (i + 1) * hps] for i in range(3))
        kj = lax.broadcasted_iota(jnp.int32, (128, 256), 1)
        for p, (_, d) in enumerate(PATTERNS):
            prev_blocks = NBLK // d > 1

            def block(b, carry):
                units = [(j,) + _block_rows(blk, d) for blk in (b, b + NBLK // 2) for j in range(hps)]
                scores = []
                for j, rows, prows, _ in units:
                    q = q_refs[j][rows, :].astype(BF16)
                    cur = _dot_nt(q, k_refs[j][rows, :].astype(BF16))
                    if prev_blocks:
                        cur = jnp.concatenate([_dot_nt(q, k_refs[j][prows, :].astype(BF16)), cur], axis=1)
                    scores.append(cur)
                soft = []
                for u, (j, _, _, has_prev) in enumerate(units):
                    if prev_blocks:
                        s = jnp.where((kj >= 128) | has_prev, scores[u] * SCALE + bias_ref[p, j], NEG_INF)
                    else:
                        s = scores[u] * SCALE + bias_ref[p, j, :, 128:256]
                    m = jnp.max(s, axis=1, keepdims=True)
                    e = jnp.exp(s - m)
                    soft.append((m, jnp.sum(e, axis=1, keepdims=True), e.astype(BF16)))
                outs = []
                for u, (j, rows, prows, _) in enumerate(units):
                    e = soft[u][2]
                    if prev_blocks:
                        outs.append(_dot(e[:, :128], v_refs[j][prows, :].astype(BF16))
                                    + _dot(e[:, 128:], v_refs[j][rows, :].astype(BF16)))
                    else:
                        outs.append(_dot(e, v_refs[j][rows, :].astype(BF16)))
                for u, (j, rows, _, _) in enumerate(units):
                    acc_scr, m_scr, l_scr = acc_scrs[j], m_scrs[j], l_scrs[j]
                    (m, den, _), o = soft[u], outs[u]
                    if p == 0:
                        acc_scr[rows, :] = o
                        m_scr[rows, :] = jnp.broadcast_to(m, (128, HD))
                        l_scr[rows, :] = jnp.broadcast_to(den, (128, HD))
                    else:
                        m_old = m_scr[rows, :]
                        m_new = jnp.maximum(m_old, m)
                        w_old, w_new = jnp.exp(m_old - m_new), jnp.exp(m - m_new)
                        acc_scr[rows, :] = acc_scr[rows, :] * w_old + o * w_new
                        l_scr[rows, :] = l_scr[rows, :] * w_old + den * w_new
                        m_scr[rows, :] = m_new
                return carry

            lax.fori_loop(0, NBLK // 2, block, 0)
        for j in range(hps):
            cols = slice(j * HD, (j + 1) * HD)
            den = l_scrs[j][...]
            o_ref[:, cols] = (acc_scrs[j][...] / den).astype(BF16)
            lse_ref[:, cols] = m_scrs[j][...] + jnp.log(den)

    return pl.pallas_call(
        body, name="attention_fwd", grid=(NH // hps,),
        in_specs=[_bias_spec(hps)] + _head_specs(0, hps) + _head_specs(NH, hps) + _head_specs(2 * NH, hps),
        out_specs=[_heads_spec(hps), _heads_spec(hps)],
        out_shape=[jax.ShapeDtypeStruct((S, DA), BF16), jax.ShapeDtypeStruct((S, DA), F32)],
        scratch_shapes=[pltpu.VMEM((S, HD), F32)] * (3 * hps),
        compiler_params=_params(1),
    )(bias, *([proj] * (3 * hps)))


def _attention_bwd(proj, dattn, attn, lse, bias, after=None):
    hps = BWD_HEADS_PER_STEP

    def body(bias_ref, *refs):
        q_refs, k_refs, v_refs, do_refs, o_refs, lse_refs = (refs[i * hps:(i + 1) * hps] for i in range(6))
        dq_ref, dk_ref, dv_ref, ds_ref = refs[6 * hps:6 * hps + 4]
        dl_scrs, dq_scrs, dk_scrs, dv_scrs = (refs[6 * hps + 4 + i * hps:6 * hps + 4 + (i + 1) * hps] for i in range(4))
        ds_ref[...] = jnp.zeros_like(ds_ref)
        for j in range(hps):
            dq_scrs[j][...] = jnp.zeros((S, HD), F32)
            dk_scrs[j][...] = jnp.zeros((S, HD), F32)
            dv_scrs[j][...] = jnp.zeros((S, HD), F32)
            prod = do_refs[j][...] * o_refs[j][...].astype(F32)
            dl_scrs[j][...] = jnp.broadcast_to(jnp.sum(prod, axis=1, keepdims=True), (S, HD))
        for p, (_, d) in enumerate(PATTERNS):
            prev_blocks = NBLK // d > 1

            def block(b, carry):
                units = [(j,) + _block_rows(b + i * (NBLK // 4), d) for i in range(4) for j in range(hps)]
                ops, raw = [], []
                for j, rows, prows, _ in units:
                    q, do = q_refs[j][rows, :].astype(BF16), do_refs[j][rows, :].astype(BF16)
                    kc, vc = k_refs[j][rows, :].astype(BF16), v_refs[j][rows, :].astype(BF16)
                    if prev_blocks:
                        kp, vp = k_refs[j][prows, :].astype(BF16), v_refs[j][prows, :].astype(BF16)
                        ops.append((q, do, kc, kp))
                        raw.append((_dot_nt(q, kc), _dot_nt(do, vc), _dot_nt(q, kp), _dot_nt(do, vp)))
                    else:
                        ops.append((q, do, kc))
                        raw.append((_dot_nt(q, kc), _dot_nt(do, vc)))
                probs = []
                for u, (j, rows, _, has_prev) in enumerate(units):
                    lse_b, dl_b = lse_refs[j][rows, :], dl_scrs[j][rows, :]
                    p_c = jnp.exp(raw[u][0] * SCALE + bias_ref[p, j, :, 128:256] - lse_b)
                    ds_c = p_c * (raw[u][1] - dl_b)
                    ds_ref[p, j, :, 128:256] += ds_c
                    if prev_blocks:
                        p_p = jnp.where(has_prev, jnp.exp(raw[u][2] * SCALE + bias_ref[p, j, :, 0:128] - lse_b), 0.0)
                        ds_p = p_p * (raw[u][3] - dl_b)
                        ds_ref[p, j, :, 0:128] += ds_p
                        probs.append((p_c, ds_c, p_p, ds_p))
                    else:
                        probs.append((p_c, ds_c))
                grads = []
                for u in range(len(units)):
                    q, do, kc = ops[u][:3]
                    p_c, ds_c = probs[u][:2]
                    dq = _dot(ds_c.astype(BF16), kc)
                    cur = (_dot(ds_c.T.astype(BF16), q) * SCALE, _dot(p_c.T.astype(BF16), do))
                    if prev_blocks:
                        p_p, ds_p = probs[u][2:]
                        dq = dq + _dot(ds_p.astype(BF16), ops[u][3])
                        cur = cur + (_dot(ds_p.T.astype(BF16), q) * SCALE, _dot(p_p.T.astype(BF16), do))
                    grads.append((dq * SCALE,) + cur)
                for u, (j, rows, prows, _) in enumerate(units):
                    dq_scrs[j][rows, :] += grads[u][0]
                    dk_scrs[j][rows, :] += grads[u][1]
                    dv_scrs[j][rows, :] += grads[u][2]
                    if prev_blocks:
                        dk_scrs[j][prows, :] += grads[u][3]
                        dv_scrs[j][prows, :] += grads[u][4]
                return carry

            lax.fori_loop(0, NBLK // 4, block, 0)
        for j in range(hps):
            cols = slice(j * HD, (j + 1) * HD)
            dq_ref[:, cols] = dq_scrs[j][...].astype(BF16)
            dk_ref[:, cols] = dk_scrs[j][...].astype(BF16)
            dv_ref[:, cols] = dv_scrs[j][...].astype(BF16)

    body, more_specs, more = _behind(body, 1 + 6 * hps, after)
    return pl.pallas_call(
        body, name="attention_bwd", grid=(NH // hps,),
        in_specs=[_bias_spec(hps)]
        + _head_specs(0, hps) + _head_specs(NH, hps) + _head_specs(2 * NH, hps) + 3 * _head_specs(0, hps)
        + more_specs,
        out_specs=3 * [_heads_spec(hps)] + [pl.BlockSpec((3, hps, 128, 256), lambda g: (0, g, 0, 0))],
        out_shape=[jax.ShapeDtypeStruct((S, DA), BF16)] * 3 + [jax.ShapeDtypeStruct((3, NH, 128, 256), F32)],
        scratch_shapes=[pltpu.VMEM((S, HD), F32)] * (4 * hps),
        compiler_params=_params(1),
    )(bias, *([proj] * (3 * hps)), *([dattn] * hps), *([attn] * hps), *([lse] * hps), *more)


def _gmlp_parts(u_ref, vb_ref, g_ref, be_ref):
    u = u_ref[...]
    u_act, tu = _gelu(u)
    vb = vb_ref[...]
    gv, tv = _gelu(vb)
    mean = jnp.mean(gv, axis=1, keepdims=True)
    cen = gv - mean
    var = jnp.mean(cen * cen, axis=1, keepdims=True)
    rstd = lax.rsqrt(var + LN_EPS)
    xhat = cen * rstd
    vn = xhat * g_ref[...] + be_ref[...]
    return u, tu, u_act, vb, tv, rstd, xhat, vn


def _gmlp_fwd(proj, ws, bsp_b, gain_v, bias_v):
    def body(u_ref, vb_ref, ws_ref, bsp_ref, g_ref, be_ref, o_ref):
        _, _, u_act, _, _, _, _, vn = _gmlp_parts(u_ref, vb_ref, g_ref, be_ref)
        row = lax.broadcasted_iota(jnp.int32, (128, 128), 0)
        col = lax.broadcasted_iota(jnp.int32, (128, 128), 1)
        causal = row >= col
        for g in range(NH):
            cols = slice(g * 128, (g + 1) * 128)
            wsg = jnp.where(causal, ws_ref[g], 0.0).astype(BF16)
            z = _dot(wsg, vn[:, cols].astype(BF16)) + bsp_ref[g]
            o_ref[:, cols] = (u_act[:, cols] * z).astype(BF16)

    return pl.pallas_call(
        body, name="gmlp_fwd", grid=(NBLK,),
        in_specs=[pl.BlockSpec((128, DB), lambda c: (c, 3)), pl.BlockSpec((128, DB), lambda c: (c, 4)),
                  pl.BlockSpec((NH, 128, 128), lambda c: (0, 0, 0)), pl.BlockSpec((NH, 128, 128), lambda c: (0, 0, 0)),
                  pl.BlockSpec((1, DB), lambda c: (0, 0)), pl.BlockSpec((1, DB), lambda c: (0, 0))],
        out_specs=pl.BlockSpec((128, DB), lambda c: (c, 0)),
        out_shape=jax.ShapeDtypeStruct((S, DB), BF16),
        compiler_params=_params(1),
    )(proj, proj, ws, bsp_b, gain_v, bias_v)


def _branch(attn, gmlp, wpa_g, wpb_g, proj):
    tn = 512

    def body(a_ref, g_ref, wa_ref, wb_ref, ga_ref, gb_ref, ya_ref, yb_ref, mg_ref):
        ya = _dot(a_ref[...], wa_ref[...])
        yb = _dot(g_ref[...], wb_ref[...])
        ya_ref[...] = ya.astype(BF16)
        yb_ref[...] = yb.astype(BF16)
        mg_ref[...] = (_sigmoid(ga_ref[...]) * ya + _sigmoid(gb_ref[...]) * yb).astype(BF16)

    out = pl.BlockSpec((S, tn), lambda j: (0, j))
    return pl.pallas_call(
        body, name="branch", grid=(D // tn,),
        in_specs=[pl.BlockSpec((S, DA), lambda j: (0, 0)), pl.BlockSpec((S, DB), lambda j: (0, 0)),
                  pl.BlockSpec((None, DA, tn), lambda j: (j, 0, 0)), pl.BlockSpec((None, DB, tn), lambda j: (j, 0, 0)),
                  pl.BlockSpec((S, tn), lambda j: (0, 5120 // tn + j)), pl.BlockSpec((S, tn), lambda j: (0, 7168 // tn + j))],
        out_specs=[out, out, out],
        out_shape=[jax.ShapeDtypeStruct((S, D), BF16)] * 3,
        compiler_params=_params(1),
    )(attn, gmlp, wpa_g, wpb_g, proj, proj)


def _out_ln1(merged, wout_g, x, gain, bias):
    tm = 256

    def body(m_ref, w_ref, x_ref, g_ref, b_ref, xh_ref, rs_ref, h_ref):
        pre = ALPHA * x_ref[...] + _dot(m_ref[...], w_ref[...])
        mean = jnp.mean(pre, axis=1, keepdims=True)
        cen = pre - mean
        var = jnp.mean(cen * cen, axis=1, keepdims=True)
        rstd = lax.rsqrt(var + LN_EPS)
        xhat = cen * rstd
        xh_ref[...] = xhat
        rs_ref[...] = jnp.broadcast_to(rstd, (tm, 128))
        h_ref[...] = (xhat * g_ref[...] + b_ref[...]).astype(BF16)

    row = pl.BlockSpec((tm, D), lambda i: (i, 0))
    vec = pl.BlockSpec((1, D), lambda i: (0, 0))
    return pl.pallas_call(
        body, name="out_ln1", grid=(S // tm,),
        in_specs=[row, pl.BlockSpec((D, D), lambda i: (0, 0)), row, vec, vec],
        out_specs=[row, pl.BlockSpec((tm, 128), lambda i: (i, 0)), row],
        out_shape=[jax.ShapeDtypeStruct((S, D), F32), jax.ShapeDtypeStruct((S, 128), F32),
                   jax.ShapeDtypeStruct((S, D), BF16)],
        compiler_params=_params(1),
    )(merged, wout_g, x, gain, bias)


def _ff1(h1b, w1_g, b1, half, name, into=None, after=None):
    tn = 512
    per = D // tn
    steps = DFF // tn // 2
    first = half * steps

    def body(h_ref, w_ref, b_ref, *rest):
        a_ref, r_ref = rest[-2:]
        r = jnp.maximum(_dot(h_ref[...], w_ref[...]) + b_ref[...], 0.0)
        r_ref[...] = r.astype(BF16)
        a_ref[...] = (r * r).astype(BF16)

    out = pl.BlockSpec((S, tn), lambda j: (0, first + j))
    extra = list(into) if into is not None else []
    if after is not None:
        extra.append(after)
    return pl.pallas_call(
        body, name=name, grid=(steps,),
        in_specs=[pl.BlockSpec((S, D), lambda j: (0, 0)),
                  pl.BlockSpec((None, D, tn), lambda j: ((first + j) // per, 0, (first + j) % per)),
                  pl.BlockSpec((1, tn), lambda j: (0, first + j))] + [ANY] * len(extra),
        out_specs=[out, out],
        out_shape=[jax.ShapeDtypeStruct((S, DFF), BF16)] * 2,
        input_output_aliases={3: 0, 4: 1} if into is not None else {},
        compiler_params=_params(1),
    )(h1b, w1_g, b1, *extra)


def _ff2_ln2_loss(a, w2_g, xhat1, g1, b1, b2, g2, be2, target):
    tm, tk = 512, 1024
    nk = DFF // tk

    def body(a_ref, w_ref, xh_ref, g1_ref, b1_ref, b2_ref, g2_ref, be2_ref, t_ref, d_ref, db_ref, st_ref, acc):
        i, k = pl.program_id(0), pl.program_id(1)

        @pl.when(k == 0)
        def _():
            acc[...] = jnp.zeros_like(acc)

        @pl.when((i == 0) & (k == 0))
        def _():
            st_ref[...] = jnp.zeros_like(st_ref)

        acc[...] += _dot(a_ref[...], w_ref[...])

        @pl.when(k == nk - 1)
        def _():
            def rows_chunk(ci, carry):
                rows = pl.ds(pl.multiple_of(ci * 128, 128), 128)
                h1 = xh_ref[rows, :] * g1_ref[...] + b1_ref[...]
                pre = ALPHA * h1 + acc[rows, :] + b2_ref[...]
                mean = jnp.mean(pre, axis=1, keepdims=True)
                cen = pre - mean
                var = jnp.mean(cen * cen, axis=1, keepdims=True)
                rstd = lax.rsqrt(var + LN_EPS)
                xhat = cen * rstd
                y = xhat * g2_ref[...] + be2_ref[...]
                err = y - t_ref[rows, :]
                dy = err * (1.0 / D)
                g = dy * g2_ref[...]
                dpre = rstd * (g - jnp.mean(g, axis=1, keepdims=True)
                               - xhat * jnp.mean(g * xhat, axis=1, keepdims=True))
                d_ref[rows, :] = dpre
                db_ref[rows, :] = dpre.astype(BF16)
                st_ref[0:1, :] += jnp.sum(dy * xhat, axis=0, keepdims=True)
                st_ref[1:2, :] += jnp.sum(dy, axis=0, keepdims=True)
                st_ref[2:3, :] += jnp.sum(dpre, axis=0, keepdims=True)
                st_ref[3:4, :] += jnp.broadcast_to(jnp.sum(err * err).reshape(1, 1), (1, D))
                return carry

            lax.fori_loop(0, tm // 128, rows_chunk, 0)

    row = pl.BlockSpec((tm, D), lambda i, k: (i, 0))
    vec = pl.BlockSpec((1, D), lambda i, k: (0, 0))
    return pl.pallas_call(
        body, name="ff2_ln2_loss", grid=(S // tm, nk),
        in_specs=[pl.BlockSpec((tm, tk), lambda i, k: (i, k)), pl.BlockSpec((tk, D), lambda i, k: (k, 0)),
                  row, vec, vec, vec, vec, vec, row],
        out_specs=[row, row, pl.BlockSpec((8, D), lambda i, k: (0, 0))],
        out_shape=[jax.ShapeDtypeStruct((S, D), F32), jax.ShapeDtypeStruct((S, D), BF16),
                   jax.ShapeDtypeStruct((8, D), F32)],
        scratch_shapes=[pltpu.VMEM((tm, D), F32)],
        compiler_params=_params(2),
    )(a, w2_g, xhat1, g1, b1, b2, g2, be2, target)


def _grad_w(act, dout, name, ti, tj, sharded, after=None):
    m, n = act.shape[1], dout.shape[1]
    ns = n // N_CHIPS
    per = ns // tj if sharded else None

    def body(a_ref, b_ref, o_ref, at_scr):
        @pl.when(pl.program_id(1) == 0)
        def _():
            at_scr[...] = a_ref[...].T

        o_ref[...] = _dot(at_scr[...], b_ref[...]).astype(BF16)

    if sharded:
        out_spec = pl.BlockSpec((None, ti, tj), lambda i, j: (j // per, i, j % per))
        out_shape = jax.ShapeDtypeStruct((N_CHIPS, m, ns), BF16)
    else:
        out_spec = pl.BlockSpec((ti, tj), lambda i, j: (i, j))
        out_shape = jax.ShapeDtypeStruct((m, n), BF16)
    body, more_specs, more = _behind(body, 2, after)
    return pl.pallas_call(
        body, name=name, grid=(m // ti, n // tj),
        in_specs=[pl.BlockSpec((S, ti), lambda i, j: (0, i)), pl.BlockSpec((S, tj), lambda i, j: (0, j))] + more_specs,
        out_specs=out_spec, out_shape=out_shape,
        scratch_shapes=[pltpu.VMEM((ti, S), BF16)],
        compiler_params=_params(2),
    )(act, dout, *more)


def _d_ff1(dpre2b, w2_g, r, after=None):
    tn = 512

    def body(d_ref, w_ref, r_ref, o_ref, gb_ref):
        da = _dot_nt(d_ref[...], w_ref[...])
        dp = da * (2.0 * r_ref[...].astype(F32))
        o_ref[...] = dp.astype(BF16)
        gb_ref[...] = jnp.sum(dp, axis=0, keepdims=True)

    body, more_specs, more = _behind(body, 3, after)
    return pl.pallas_call(
        body, name="d_ff1", grid=(DFF // tn,),
        in_specs=[pl.BlockSpec((S, D), lambda j: (0, 0)), pl.BlockSpec((tn, D), lambda j: (j, 0)),
                  pl.BlockSpec((S, tn), lambda j: (0, j))] + more_specs,
        out_specs=[pl.BlockSpec((S, tn), lambda j: (0, j)), pl.BlockSpec((1, tn), lambda j: (0, j))],
        out_shape=[jax.ShapeDtypeStruct((S, DFF), BF16), jax.ShapeDtypeStruct((1, DFF), F32)],
        compiler_params=_params(1),
    )(dpre2b, w2_g, r, *more)


def _d_h1_ln1(dprea, w1_g, dpre2, xhat1, rstd1, g1, after=None):
    tm, tk = 512, 1024
    per = D // tk
    nk = DFF // tk

    def body(a_ref, w_ref, d2_ref, xh_ref, rs_ref, g_ref, d_ref, db_ref, st_ref, acc):
        i, k = pl.program_id(0), pl.program_id(1)

        @pl.when(k == 0)
        def _():
            acc[...] = jnp.zeros_like(acc)

        @pl.when((i == 0) & (k == 0))
        def _():
            st_ref[...] = jnp.zeros_like(st_ref)

        acc[...] += _dot_nt(a_ref[...], w_ref[...])

        @pl.when(k == nk - 1)
        def _():
            def rows_chunk(ci, carry):
                rows = pl.ds(pl.multiple_of(ci * 128, 128), 128)
                dh = ALPHA * d2_ref[rows, :] + acc[rows, :]
                xhat = xh_ref[rows, :]
                g = dh * g_ref[...]
                dpre = rs_ref[rows, 0:1] * (g - jnp.mean(g, axis=1, keepdims=True)
                                            - xhat * jnp.mean(g * xhat, axis=1, keepdims=True))
                d_ref[rows, :] = dpre
                db_ref[rows, :] = dpre.astype(BF16)
                st_ref[0:1, :] += jnp.sum(dh * xhat, axis=0, keepdims=True)
                st_ref[1:2, :] += jnp.sum(dh, axis=0, keepdims=True)
                return carry

            lax.fori_loop(0, tm // 128, rows_chunk, 0)

    row = pl.BlockSpec((tm, D), lambda i, k: (i, 0))
    body, more_specs, more = _behind(body, 6, after)
    return pl.pallas_call(
        body, name="d_h1_ln1", grid=(S // tm, nk),
        in_specs=[pl.BlockSpec((tm, tk), lambda i, k: (i, k)),
                  pl.BlockSpec((None, D, tk), lambda i, k: (k // per, 0, k % per)),
                  row, row, pl.BlockSpec((tm, 128), lambda i, k: (i, 0)), pl.BlockSpec((1, D), lambda i, k: (0, 0))]
        + more_specs,
        out_specs=[row, row, pl.BlockSpec((8, D), lambda i, k: (0, 0))],
        out_shape=[jax.ShapeDtypeStruct((S, D), F32), jax.ShapeDtypeStruct((S, D), BF16),
                   jax.ShapeDtypeStruct((8, D), F32)],
        scratch_shapes=[pltpu.VMEM((tm, D), F32)],
        compiler_params=_params(2),
    )(dprea, w1_g, dpre2, xhat1, rstd1, g1, *more)


def _d_merged(dpre1b, wout_g, proj, ya, yb):
    tm, tn = 512, 1024

    def body(d_ref, w_ref, ga_ref, gb_ref, ya_ref, yb_ref, dya_ref, dyb_ref, dga_ref, dgb_ref):
        dm = _dot_nt(d_ref[...], w_ref[...])
        sa = _sigmoid(ga_ref[...])
        sb = _sigmoid(gb_ref[...])
        dya_ref[...] = (dm * sa).astype(BF16)
        dyb_ref[...] = (dm * sb).astype(BF16)
        dga_ref[...] = (dm * ya_ref[...].astype(F32) * sa * (1.0 - sa)).astype(BF16)
        dgb_ref[...] = (dm * yb_ref[...].astype(F32) * sb * (1.0 - sb)).astype(BF16)

    tile = pl.BlockSpec((tm, tn), lambda j, i: (i, j))
    return pl.pallas_call(
        body, name="d_merged", grid=(D // tn, S // tm),
        in_specs=[pl.BlockSpec((tm, D), lambda j, i: (i, 0)), pl.BlockSpec((tn, D), lambda j, i: (j, 0)),
                  pl.BlockSpec((tm, tn), lambda j, i: (i, 5 + j)), pl.BlockSpec((tm, tn), lambda j, i: (i, 7 + j)),
                  tile, tile],
        out_specs=[tile] * 4,
        out_shape=[jax.ShapeDtypeStruct((S, D), BF16)] * 4,
        compiler_params=_params(2),
    )(dpre1b, wout_g, proj, proj, ya, yb)


def _d_branches(dya, dyb, wpa_g, wpb_g, after=None):
    tm = 512
    ws = D // N_CHIPS

    def body(da_ref, db_ref, wa_ref, wb_ref, oa_ref, ob_ref):
        for d_ref, w_ref, o_ref in ((da_ref, wa_ref, oa_ref), (db_ref, wb_ref, ob_ref)):
            acc = _dot_nt(d_ref[:, 0:ws], w_ref[0])
            for s in range(1, N_CHIPS):
                acc = acc + _dot_nt(d_ref[:, s * ws:(s + 1) * ws], w_ref[s])
            o_ref[...] = acc

    rows = lambda width: pl.BlockSpec((tm, width), lambda i: (i, 0))
    whole = lambda n: pl.BlockSpec((N_CHIPS, n, ws), lambda i: (0, 0, 0))
    body, more_specs, more = _behind(body, 4, after)
    return pl.pallas_call(
        body, name="d_branches", grid=(S // tm,),
        in_specs=[rows(D), rows(D), whole(DA), whole(DB)] + more_specs,
        out_specs=[rows(DA), rows(DB)],
        out_shape=[jax.ShapeDtypeStruct((S, DA), F32), jax.ShapeDtypeStruct((S, DB), F32)],
        compiler_params=_params(1),
    )(dya, dyb, wpa_g, wpb_g, *more)


def _gmlp_bwd(proj, dgmlp, ws, ws_t, bsp_b, gain_v, bias_v):
    def body(u_ref, vb_ref, dg_ref, ws_ref, wst_ref, bsp_ref, g_ref, be_ref, duv_ref, gws_ref, gbs_ref, st_ref):
        @pl.when(pl.program_id(0) == 0)
        def _():
            gws_ref[...] = jnp.zeros_like(gws_ref)
            gbs_ref[...] = jnp.zeros_like(gbs_ref)
            st_ref[...] = jnp.zeros_like(st_ref)

        u, tu, u_act, vb, tv, rstd, xhat, vn = _gmlp_parts(u_ref, vb_ref, g_ref, be_ref)
        dg = dg_ref[...]
        dz = dg * u_act
        row = lax.broadcasted_iota(jnp.int32, (128, 128), 0)
        col = lax.broadcasted_iota(jnp.int32, (128, 128), 1)
        causal = row >= col
        causal_t = row <= col
        dvn_parts = []
        z_parts = []
        for g in range(NH):
            cols = slice(g * 128, (g + 1) * 128)
            vng = vn[:, cols].astype(BF16)
            dzg = dz[:, cols]
            dzb = dzg.astype(BF16)
            wsg = jnp.where(causal, ws_ref[g], 0.0).astype(BF16)
            wsg_t = jnp.where(causal_t, wst_ref[g], 0.0).astype(BF16)
            z_parts.append(_dot(wsg, vng) + bsp_ref[g])
            gws_ref[g] += jnp.where(causal, _dot_nt(dzb, vng), 0.0)
            gbs_ref[g] += jnp.broadcast_to(jnp.sum(dzg, axis=1, keepdims=True), (128, 128))
            dvn_parts.append(_dot(wsg_t, dzb))
        z = jnp.concatenate(z_parts, axis=1)
        dvn = jnp.concatenate(dvn_parts, axis=1)
        du = dg * z * _gelu_grad(u, tu)
        st_ref[0:1, :] += jnp.sum(dvn * xhat, axis=0, keepdims=True)
        st_ref[1:2, :] += jnp.sum(dvn, axis=0, keepdims=True)
        gg = dvn * g_ref[...]
        dgv = rstd * (gg - jnp.mean(gg, axis=1, keepdims=True) - xhat * jnp.mean(gg * xhat, axis=1, keepdims=True))
        dvb = dgv * _gelu_grad(vb, tv)
        duv_ref[:, 0:DB] = du.astype(BF16)
        duv_ref[:, DB:2 * DB] = dvb.astype(BF16)

    full3 = pl.BlockSpec((NH, 128, 128), lambda c: (0, 0, 0))
    vec = pl.BlockSpec((1, DB), lambda c: (0, 0))
    return pl.pallas_call(
        body, name="gmlp_bwd", grid=(NBLK,),
        in_specs=[pl.BlockSpec((128, DB), lambda c: (c, 3)), pl.BlockSpec((128, DB), lambda c: (c, 4)),
                  pl.BlockSpec((128, DB), lambda c: (c, 0)), full3, full3, full3, vec, vec],
        out_specs=[pl.BlockSpec((128, 2 * DB), lambda c: (c, 0)), full3, full3, pl.BlockSpec((8, DB), lambda c: (0, 0))],
        out_shape=[jax.ShapeDtypeStruct((S, 2 * DB), BF16), jax.ShapeDtypeStruct((NH, 128, 128), F32),
                   jax.ShapeDtypeStruct((NH, 128, 128), F32), jax.ShapeDtypeStruct((8, DB), F32)],
        compiler_params=_params(1),
    )(proj, proj, dgmlp, ws, ws_t, bsp_b, gain_v, bias_v)


def _rel_bias_grad(ds_sums):
    buckets = jnp.asarray(np.stack([_bucket_tile(d) for _, d in PATTERNS]))

    def body(bk_ref, ds_ref, o_ref):
        row = lax.broadcasted_iota(jnp.int32, (N_BUCKETS, 128), 0)
        lane = lax.broadcasted_iota(jnp.int32, (N_BUCKETS, 128), 1)

        def one_bucket(t, out):
            hits = [bk_ref[p] == t for p in range(3)]
            for h in range(NH):
                tot = jnp.zeros((128, 256), F32)
                for p in range(3):
                    tot = tot + jnp.where(hits[p], ds_ref[p, h], 0.0)
                out = jnp.where((row == t) & (lane == h), jnp.sum(tot), out)
            return out

        o_ref[...] = lax.fori_loop(0, N_BUCKETS, one_bucket, jnp.zeros((N_BUCKETS, 128), F32))

    return pl.pallas_call(
        body, name="rel_bias_grad",
        in_specs=[pl.BlockSpec(memory_space=pltpu.VMEM)] * 2, out_specs=pl.BlockSpec(memory_space=pltpu.VMEM),
        out_shape=jax.ShapeDtypeStruct((N_BUCKETS, 128), F32),
        compiler_params=pltpu.CompilerParams(vmem_limit_bytes=VMEM_LIMIT),
    )(buckets, ds_sums)


def _d_x(dproj, win_g, dpre1, after=None):
    tm, tn = 512, 512
    ws = DIN // N_CHIPS

    def body(a_ref, w_ref, d_ref, o_ref):
        acc = ALPHA * d_ref[...]
        for s in range(N_CHIPS):
            acc = acc + _dot_nt(a_ref[:, s * ws:(s + 1) * ws], w_ref[s])
        o_ref[...] = acc

    tile = pl.BlockSpec((tm, tn), lambda i, j: (i, j))
    body, more_specs, more = _behind(body, 3, after)
    return pl.pallas_call(
        body, name="d_x", grid=(S // tm, D // tn),
        in_specs=[pl.BlockSpec((tm, DIN), lambda i, j: (i, 0)),
                  pl.BlockSpec((N_CHIPS, tn, ws), lambda i, j: (0, j, 0)), tile] + more_specs,
        out_specs=tile, out_shape=jax.ShapeDtypeStruct((S, D), F32),
        compiler_params=_params(2),
    )(dproj, win_g, dpre1, *more)


def _adamw(w, g, m, v, name, after=None):
    rows, cols = w.shape
    tm = max(t for t in range(8, 257, 8) if rows % t == 0)

    def body(w_ref, g_ref, m_ref, v_ref, d_ref, nm_ref, nv_ref, go_ref):
        g = g_ref[...]
        m = ADAM_B1 * m_ref[...] + (1.0 - ADAM_B1) * g
        v = ADAM_B2 * v_ref[...] + (1.0 - ADAM_B2) * (g * g)
        m_hat = m / (1.0 - ADAM_B1 ** ADAM_STEP)
        v_hat = v / (1.0 - ADAM_B2 ** ADAM_STEP)
        d_ref[...] = -ADAM_LR * (m_hat / (jnp.sqrt(v_hat) + ADAM_EPS) + ADAM_WD * w_ref[...])
        nm_ref[...] = m
        nv_ref[...] = v
        go_ref[...] = g

    spec = pl.BlockSpec((tm, cols), lambda i: (i, 0))
    body, more_specs, more = _behind(body, 4, after)
    return pl.pallas_call(
        body, name=name, grid=(rows // tm,), in_specs=[spec] * 4 + more_specs, out_specs=[spec] * 4,
        out_shape=[jax.ShapeDtypeStruct((rows, cols), F32)] * 4, compiler_params=_params(1),
    )(w, g, m, v, *more)


def _position():
    x, y, c = lax.axis_index("x"), lax.axis_index("y"), lax.axis_index("c")
    chips = [(1 - x, y), (x, 1 - y), (1 - x, 1 - y)]
    return x, y, c, chips


def _remote(src, dst, send_sems, recv_sems, k, to):
    return pltpu.make_async_remote_copy(src_ref=src, dst_ref=dst, send_sem=send_sems.at[k], recv_sem=recv_sems.at[k],
                                        device_id=to, device_id_type=MESH)


def _place_shard(w, name, after=None):
    rows, cols = w.shape
    tm = 256
    x, y = lax.axis_index("x"), lax.axis_index("y")

    def body(chip_ref, w_ref, o_ref):
        o_ref[...] = w_ref[...].astype(BF16)

    more_specs, more = ([ANY], [after]) if after is not None else ([], [])
    if after is not None:
        inner = body
        body = lambda chip_ref, w_ref, after_ref, o_ref: inner(chip_ref, w_ref, o_ref)
    return pl.pallas_call(
        body, name=name,
        grid_spec=pltpu.PrefetchScalarGridSpec(
            num_scalar_prefetch=1, grid=(rows // tm,),
            in_specs=[pl.BlockSpec((tm, cols), lambda i, chip: (i, 0))] + more_specs,
            out_specs=pl.BlockSpec((None, tm, cols), lambda i, chip: (chip[0], i, 0))),
        out_shape=jax.ShapeDtypeStruct((N_CHIPS, rows, cols), BF16),
        compiler_params=_params(1),
    )(jnp.reshape(2 * x + y, (1,)).astype(jnp.int32), w, *more)


def _to_bf16(x, name, after=None):
    tm = 256

    def body(x_ref, o_ref):
        o_ref[...] = x_ref[...].astype(BF16)

    spec = pl.BlockSpec((tm, x.shape[1]), lambda i: (i, 0))
    body, more_specs, more = _behind(body, 1, after)
    return pl.pallas_call(
        body, name=name, grid=(x.shape[0] // tm,), in_specs=[spec] + more_specs, out_specs=spec,
        out_shape=jax.ShapeDtypeStruct(x.shape, BF16), compiler_params=_params(1),
    )(x, *more)


HBM = pl.BlockSpec(memory_space=pltpu.HBM)
SEM = pl.BlockSpec(memory_space=pltpu.SEMAPHORE)
EFFECT = pltpu.SideEffectType.DATAFLOW_SIDE_EFFECTING


def _comm_call(name, body, bufs, sems_in, sems_out, after=None, token=False):
    nb, ns, no = len(bufs), len(sems_in), len(sems_out)
    n_in = nb + ns + (after is not None)

    def wrapped(*refs):
        body(refs[:nb], refs[nb:nb + ns], refs[n_in + nb:n_in + nb + no])
        if token:
            refs[-1][...] = jnp.zeros((8, 128), F32)

    outs = pl.pallas_call(
        wrapped, name=name,
        in_specs=[HBM] * nb + [SEM] * ns + ([ANY] if after is not None else []),
        out_specs=[HBM] * nb + [SEM] * no + ([pl.BlockSpec(memory_space=pltpu.VMEM)] if token else []),
        out_shape=[pltpu.HBM(b.shape, b.dtype) for b in bufs] + [pltpu.SemaphoreType.DMA((k,)) for k in sems_out]
        + ([jax.ShapeDtypeStruct((8, 128), F32)] if token else []),
        input_output_aliases={i: i for i in range(nb)},
        compiler_params=pltpu.CompilerParams(has_side_effects=EFFECT),
    )(*[pltpu.with_memory_space_constraint(b, pltpu.HBM) for b in bufs], *sems_in, *([after] if after is not None else []))
    return list(outs[:nb]), list(outs[nb:nb + no]), (outs[-1] if token else None)


RING_STAGES = {"ici_near": 2, "ici_far": 2, "d2d_near": 2, "d2d_far": 1}


def _ring_copies(buf, send_sems, recv_sems, k0, stage):
    x, y, c, _ = _position()
    hr = buf.shape[1] // 2
    qr = hr // 2
    half = lambda chip, h: buf.at[chip, pl.ds(h * hr, hr), :]
    quarter = lambda chip, h, q: buf.at[chip, pl.ds(h * hr + q * qr, qr), :]
    mine, x_chip, y_chip, far_chip = 2 * x + y, 2 * (1 - x) + y, 2 * x + (1 - y), 2 * (1 - x) + (1 - y)
    to_x, to_y, sibling = (1 - x, y, c), (x, 1 - y, c), (x, y, 1 - c)
    if stage == "ici_near":
        moves = [(half(mine, c), to_x, half(x_chip, c)), (half(mine, c), to_y, half(y_chip, c))]
    elif stage == "ici_far":
        moves = [(quarter(x_chip, c, 0), to_y, quarter(far_chip, c, 0)),
                 (quarter(y_chip, c, 1), to_x, quarter(far_chip, c, 1))]
    elif stage == "d2d_near":
        moves = [(half(x_chip, c), sibling, half(x_chip, 1 - c)), (half(y_chip, c), sibling, half(y_chip, 1 - c))]
    else:
        moves = [(half(far_chip, c), sibling, half(far_chip, 1 - c))]
    sends = [_remote(src, src, send_sems, recv_sems, k0 + i, to) for i, (src, to, _) in enumerate(moves)]
    arrivals = [_remote(got, got, send_sems, recv_sems, k0 + i, (x, y, c)) for i, (_, _, got) in enumerate(moves)]
    return sends, arrivals


def _ring_call(name, groups, actions, after=None):
    tags = list(dict.fromkeys(t for _, t, _ in actions))
    counts = {t: len(groups[t]["bufs"]) for t in tags}
    first = {t: sum(counts[u] for u in tags[:i]) for i, t in enumerate(tags)}
    waits = [(t, s) for v, t, s in actions if v == "wait"]
    starts = [(t, s) for v, t, s in actions if v == "start"]

    def body(bufs, sems_in, sems_out):
        for verb, t, s in actions:
            at, sems = (starts.index((t, s)), sems_out) if verb == "start" else (waits.index((t, s)), sems_in)
            for w in range(counts[t]):
                sends, arrivals = _ring_copies(bufs[first[t] + w], sems[2 * at], sems[2 * at + 1], RING_STAGES[s] * w, s)
                if verb == "start":
                    for cp in sends:
                        cp.start()
                else:
                    for cp in arrivals:
                        cp.wait_recv()
                    for cp in sends:
                        cp.wait_send()

    bufs, sems, token = _comm_call(
        name, body, [b for t in tags for b in groups[t]["bufs"]],
        [sem for t, s in waits for sem in groups[t]["sems"][s]],
        [RING_STAGES[s] * counts[t] for t, s in starts for _ in (0, 1)], after, token=True)
    for t in tags:
        groups[t]["bufs"] = bufs[first[t]:first[t] + counts[t]]
    for t, s in waits:
        del groups[t]["sems"][s]
    for i, (t, s) in enumerate(starts):
        groups[t]["sems"][s] = (sems[2 * i], sems[2 * i + 1])
    return token


def _cx_copies(src, dst, send_sems, recv_sems, k0):
    x, y, c, chips = _position()
    sends = [_remote(src.at[2 * cx + cy], dst.at[2 * x + y], send_sems, recv_sems, k0 + j, (cx, cy, c))
             for j, (cx, cy) in enumerate(chips)]
    arrivals = [_remote(dst.at[2 * cx + cy], dst.at[2 * cx + cy], send_sems, recv_sems, k0 + j, (x, y, c))
                for j, (cx, cy) in enumerate(chips)]
    return sends, arrivals


def _cx_start(name, pair_sums):
    n = len(pair_sums)
    landing = [lax.empty(p.shape, p.dtype) for p in pair_sums]

    def body(bufs, _, sems):
        for w in range(n):
            for cp in _cx_copies(bufs[w], bufs[n + w], sems[0], sems[1], 3 * w)[0]:
                cp.start()

    bufs, sems, token = _comm_call(name, body, list(pair_sums) + landing, [], [3 * n, 3 * n], token=True)
    return (bufs, sems), token


def _cx_wait(name, state, after):
    bufs, sems = state
    n = len(bufs) // 2

    def body(refs, sems_in, _):
        for w in range(n):
            sends, arrivals = _cx_copies(refs[w], refs[n + w], sems_in[0], sems_in[1], 3 * w)
            for cp in arrivals:
                cp.wait_recv()
            for cp in sends:
                cp.wait_send()

    bufs, _, _ = _comm_call(name, body, bufs, sems, [], after)
    return bufs[:n], bufs[n:]


def _px_copies(src, dst, send_sems, recv_sems, k):
    x, y, c, _ = _position()
    hr = src.shape[1] // 2
    send = _remote(src.at[:, pl.ds((1 - c) * hr, hr), :], dst, send_sems, recv_sems, k, (x, y, 1 - c))
    arrival = _remote(dst, dst, send_sems, recv_sems, k, (x, y, c))
    return send, arrival


def _px_start(name, grads):
    n = len(grads)
    landing = [lax.empty((N_CHIPS, g.shape[1] // 2, g.shape[2]), g.dtype) for g in grads]

    def body(bufs, _, sems):
        for w in range(n):
            _px_copies(bufs[w], bufs[n + w], sems[0], sems[1], w)[0].start()

    bufs, sems, token = _comm_call(name, body, list(grads) + landing, [], [n, n], token=True)
    return (bufs, sems), token


def _px_wait(name, state, after):
    bufs, sems = state
    n = len(bufs) // 2

    def body(refs, sems_in, _):
        for w in range(n):
            send, arrival = _px_copies(refs[w], refs[n + w], sems_in[0], sems_in[1], w)
            arrival.wait_recv()
            send.wait_send()

    bufs, _, _ = _comm_call(name, body, bufs, sems, [], after)
    return bufs[:n], bufs[n:]


def _pair_sum(grad, got, name):
    _, rows, cols = grad.shape
    hr = rows // 2
    tm = min(hr, 512)
    nb = hr // tm
    c = lax.axis_index("c")

    def body(c_ref, g_ref, o_ref, out_ref):
        out_ref[...] = (g_ref[...].astype(F32) + o_ref[...].astype(F32)).astype(BF16)

    return pl.pallas_call(
        body, name=name,
        grid_spec=pltpu.PrefetchScalarGridSpec(
            num_scalar_prefetch=1, grid=(N_CHIPS, nb),
            in_specs=[pl.BlockSpec((None, tm, cols), lambda s, i, c_ref: (s, c_ref[0] * nb + i, 0)),
                      pl.BlockSpec((None, tm, cols), lambda s, i, c_ref: (s, i, 0))],
            out_specs=pl.BlockSpec((None, tm, cols), lambda s, i, c_ref: (s, i, 0))),
        out_shape=jax.ShapeDtypeStruct((N_CHIPS, hr, cols), BF16),
        compiler_params=_params(2),
    )(jnp.reshape(c, (1,)).astype(jnp.int32), grad, got)


def _chip_sum(parts, pair_sums, name):
    _, hr, cols = parts.shape
    tm = min(hr, 512)
    nb = hr // tm
    x, y, c = lax.axis_index("x"), lax.axis_index("y"), lax.axis_index("c")

    def body(pos_ref, p_ref, own_ref, o_ref):
        chip = pos_ref[0]
        own = own_ref[...].astype(F32)
        term = lambda s: jnp.where(chip == s, own, p_ref[s].astype(F32))
        o_ref[...] = ((term(0) + term(1)) + term(2)) + term(3)

    return pl.pallas_call(
        body, name=name,
        grid_spec=pltpu.PrefetchScalarGridSpec(
            num_scalar_prefetch=1, grid=(nb,),
            in_specs=[pl.BlockSpec((N_CHIPS, tm, cols), lambda i, pos: (0, i, 0)),
                      pl.BlockSpec((None, tm, cols), lambda i, pos: (pos[0], i, 0))],
            out_specs=pl.BlockSpec((tm, cols), lambda i, pos: (pos[1] * nb + i, 0))),
        out_shape=jax.ShapeDtypeStruct((2 * hr, cols), F32), compiler_params=_params(1),
    )(jnp.stack([2 * x + y, c]).astype(jnp.int32), parts, pair_sums)


def _share_copies(buf, send_sems, recv_sems, k):
    x, y, c, _ = _position()
    hr = buf.shape[0] // 2
    mine, theirs = buf.at[pl.ds(c * hr, hr), :], buf.at[pl.ds((1 - c) * hr, hr), :]
    return (_remote(mine, mine, send_sems, recv_sems, k, (x, y, 1 - c)),
            _remote(theirs, theirs, send_sems, recv_sems, k, (x, y, c)))


def _share_start(name, bufs):
    n = len(bufs)

    def body(refs, _, sems):
        for w in range(n):
            _share_copies(refs[w], sems[0], sems[1], w)[0].start()

    bufs, sems, token = _comm_call(name, body, list(bufs), [], [n, n], token=True)
    return (bufs, sems), token


def _share_wait(name, state, after):
    bufs, sems = state

    def body(refs, sems_in, _):
        for w in range(len(bufs)):
            send, arrival = _share_copies(refs[w], sems_in[0], sems_in[1], w)
            arrival.wait_recv()
            send.wait_send()

    return _comm_call(name, body, bufs, sems, [], after)[0]


def _allreduce_small(g):
    rows = g.shape[0]
    half = rows // 2

    def body(g_ref, o_ref, sib, slots, send_sems, recv_sems):
        x, y, c, chips = _position()
        me, sibling = (x, y, c), (x, y, 1 - c)
        my_chip = 2 * x + y
        mine = pl.ds(pl.multiple_of(c * half, 8), half)
        theirs = pl.ds(pl.multiple_of((1 - c) * half, 8), half)
        pair = _remote(g_ref.at[theirs], sib, send_sems, recv_sems, 0, sibling)
        pair.start()
        pair.wait()
        slots[my_chip] = g_ref[mine, :] + sib[...]
        sent = []
        for j, (cx, cy) in enumerate(chips):
            cp = _remote(slots.at[my_chip], slots.at[my_chip], send_sems, recv_sems, 1 + j, (cx, cy, c))
            cp.start()
            sent.append(cp)
        for j, (cx, cy) in enumerate(chips):
            got = slots.at[2 * cx + cy]
            _remote(got, got, send_sems, recv_sems, 1 + j, me).wait_recv()
        for cp in sent:
            cp.wait_send()
        o_ref[mine, :] = ((slots[0] + slots[1]) + slots[2]) + slots[3]
        swap = _remote(o_ref.at[mine], o_ref.at[mine], send_sems, recv_sems, 4, sibling)
        swap.start()
        swap.wait()

    vm = pl.BlockSpec(memory_space=pltpu.VMEM)
    return pl.pallas_call(
        body, name="allreduce_small",
        in_specs=[vm], out_specs=vm, out_shape=jax.ShapeDtypeStruct((rows, 128), F32),
        scratch_shapes=[pltpu.VMEM((half, 128), F32), pltpu.VMEM((N_CHIPS, half, 128), F32),
                        pltpu.SemaphoreType.DMA((5,)), pltpu.SemaphoreType.DMA((5,))],
        compiler_params=pltpu.CompilerParams(vmem_limit_bytes=VMEM_LIMIT),
    )(g)


_SMALL =("rel_bias", "ln_v_gain", "ln_v_bias", "w_spatial", "b_spatial", "ln1_gain", "ln1_bias",
          "b_ff1", "b_ff2", "ln2_gain", "ln2_bias")
_SMALL_ROWS = 1200
_LOSS_AT = (152832 // 128, 0)


def _pack_small(parts):
    flat = jnp.concatenate([parts[k].reshape(-1).astype(F32) for k in _SMALL])
    flat = jnp.pad(flat, (0, _SMALL_ROWS * 128 - flat.shape[0]))
    return flat.reshape(_SMALL_ROWS, 128)


def _unpack_small(packed, like):
    flat = packed.reshape(-1)
    out, at = {}, 0
    for k in _SMALL:
        n = math.prod(like[k].shape)
        out[k] = flat[at:at + n].reshape(like[k].shape)
        at += n
    return out


def kernel(x, w_in, rel_bias, ln_v_gain, ln_v_bias, w_spatial, b_spatial, w_proj_a, w_proj_b, w_out, ln1_gain, ln1_bias, w_ff1, b_ff1, w_ff2, b_ff2, ln2_gain, ln2_bias, loss_target, m_w_in, m_rel_bias, m_ln_v_gain, m_ln_v_bias, m_w_spatial, m_b_spatial, m_w_proj_a, m_w_proj_b, m_w_out, m_ln1_gain, m_ln1_bias, m_w_ff1, m_b_ff1, m_w_ff2, m_b_ff2, m_ln2_gain, m_ln2_bias, v_w_in, v_rel_bias, v_ln_v_gain, v_ln_v_bias, v_w_spatial, v_b_spatial, v_w_proj_a, v_w_proj_b, v_w_out, v_ln1_gain, v_ln1_bias, v_w_ff1, v_b_ff1, v_w_ff2, v_b_ff2, v_ln2_gain, v_ln2_bias):
    args = dict(locals())
    big = ("w_in", "w_proj_a", "w_proj_b", "w_out", "w_ff1", "w_ff2")
    weights = ("w_in", "rel_bias", "ln_v_gain", "ln_v_bias", "w_spatial", "b_spatial", "w_proj_a", "w_proj_b", "w_out",
               "ln1_gain", "ln1_bias", "w_ff1", "b_ff1", "w_ff2", "b_ff2", "ln2_gain", "ln2_bias")

    xs = x[0]
    target = loss_target[0]

    ring = {"a": {"bufs": [_place_shard(w_in[0], "place_w_in")], "sems": {}}}
    tok = _ring_call("allgather_a_near", ring, [("start", "a", "ici_near")])
    placed = [_place_shard(args[k][0], f"place_{k}", after=tok) for k in big[1:]]
    for tag, bufs in (("b", placed[0:3]), ("c", placed[3:4]), ("d", placed[4:5])):
        ring[tag] = {"bufs": bufs, "sems": {}}
    xb = _to_bf16(xs, "x_to_bf16", after=placed[4])

    mx, my = lax.axis_index("x"), lax.axis_index("y")
    own = jnp.reshape(2 * mx + my, (1,)).astype(jnp.int32)
    near = jnp.stack([2 * (1 - mx) + my, 2 * mx + (1 - my)]).astype(jnp.int32)
    far = jnp.reshape(2 * (1 - mx) + (1 - my), (1,)).astype(jnp.int32)
    proj = _proj(xb, ring["a"]["bufs"][0], own, "proj_own")
    _ring_call("allgather_a_far", ring, [("wait", "a", "ici_near"), ("start", "a", "ici_far"), ("start", "a", "d2d_near"),
                                         ("start", "b", "ici_near"), ("start", "c", "ici_near")], after=proj)
    _ring_call("allgather_a_near_done", ring, [("wait", "a", "d2d_near")])
    proj = _proj(xb, ring["a"]["bufs"][0], near, "proj_near", into=proj)
    _ring_call("allgather_a_last", ring, [("wait", "a", "ici_far"), ("start", "a", "d2d_far")], after=proj)
    _ring_call("allgather_a_done", ring, [("wait", "a", "d2d_far")])
    (win_g,) = ring["a"]["bufs"]
    proj = _proj(xb, win_g, far, "proj_far", into=proj)
    _ring_call("allgather_b_far", ring, [("wait", "b", "ici_near"), ("start", "b", "ici_far"), ("start", "b", "d2d_near")],
               after=proj)
    ws = w_spatial[0]
    ws_t = jnp.transpose(ws, (0, 2, 1))
    bsp_b = jnp.broadcast_to(b_spatial[0][:, :, None], (NH, 128, 128))
    gmlp = _gmlp_fwd(proj, ws, bsp_b, ln_v_gain, ln_v_bias)
    bias = _bias_tiles(rel_bias)
    attn, lse = _attention_fwd(proj, bias)
    _ring_call("allgather_b_last_c_far", ring,
               [("wait", "b", "ici_far"), ("start", "b", "d2d_far"),
                ("wait", "c", "ici_near"), ("start", "c", "ici_far"), ("start", "c", "d2d_near"),
                ("start", "d", "ici_near")], after=attn)
    _ring_call("allgather_b_done", ring, [("wait", "b", "d2d_near"), ("wait", "b", "d2d_far")])
    wpa_g, wpb_g, wout_g = ring["b"]["bufs"]
    wout_full = wout_g.reshape(D, D)
    ya, yb, merged = _branch(attn, gmlp, wpa_g, wpb_g, proj)
    xhat1, rstd1, h1b = _out_ln1(merged, wout_full, xs, ln1_gain, ln1_bias)
    _ring_call("allgather_c_last", ring, [("wait", "c", "ici_far"), ("start", "c", "d2d_far")], after=h1b)
    _ring_call("allgather_c_done", ring, [("wait", "c", "d2d_near"), ("wait", "c", "d2d_far")])
    (w1_g,) = ring["c"]["bufs"]
    a, r = _ff1(h1b, w1_g, b_ff1, 0, "ff1_first")
    tok = _ring_call("allgather_d_far", ring,
                     [("wait", "d", "ici_near"), ("start", "d", "ici_far"), ("start", "d", "d2d_near")], after=a)
    a, r = _ff1(h1b, w1_g, b_ff1, 1, "ff1_second", into=(a, r), after=tok)
    _ring_call("allgather_d_last", ring, [("wait", "d", "ici_far"), ("start", "d", "d2d_far")], after=a)
    _ring_call("allgather_d_done", ring, [("wait", "d", "d2d_near"), ("wait", "d", "d2d_far")])
    (w2_g,) = ring["d"]["bufs"]
    w2_full = w2_g.reshape(DFF, D)
    dpre2, dpre2b, st2 = _ff2_ln2_loss(a, w2_full, xhat1, ln1_gain, ln1_bias, b_ff2, ln2_gain, ln2_bias, target)

    def pair_and_chip(tag, state, after):
        local, from_sibling = _px_wait(f"pair_exchange_wait_{tag}", state, after)
        pair_sums = [_pair_sum(g, o, f"pair_sum_{tag}_{i}") for i, (g, o) in enumerate(zip(local, from_sibling))]
        return _cx_start(f"chip_exchange_start_{tag}", pair_sums)

    g_w2 = _grad_w(a, dpre2b, "grad_w_ff2", 512, 2048, False)
    px, tok = _px_start("pair_exchange_start_w_ff2", [g_w2.reshape(N_CHIPS, DFF // N_CHIPS, D)])
    dprea, g_b1 = _d_ff1(dpre2b, w2_full, r, after=tok)
    cx_w2, tok = pair_and_chip("w_ff2", px, dprea)
    g_w1 = _grad_w(h1b, dprea, "grad_w_ff1", 512, 2048, True, after=tok)
    px, tok = _px_start("pair_exchange_start_w_ff1", [g_w1])
    dpre1, dpre1b, st1 = _d_h1_ln1(dprea, w1_g, dpre2, xhat1, rstd1, ln1_gain, after=tok)
    cx_w1, tok = pair_and_chip("w_ff1", px, dpre1b)
    g_wout = _grad_w(merged, dpre1b, "grad_w_out", 512, 2048, False, after=tok)
    dya, dyb, dga, dgb = _d_merged(dpre1b, wout_full, proj, ya, yb)
    g_wpa = _grad_w(attn, dya, "grad_w_proj_a", 1024, 512, True)
    g_wpb = _grad_w(gmlp, dyb, "grad_w_proj_b", 1024, 512, True)
    px, tok = _px_start("pair_exchange_start_b", [g_wpa, g_wpb, g_wout.reshape(N_CHIPS, D // N_CHIPS, D)])
    dattn, dgmlp = _d_branches(dya, dyb, wpa_g, wpb_g, after=tok)
    duv, g_ws, g_bs, stv = _gmlp_bwd(proj, dgmlp, ws, ws_t, bsp_b, ln_v_gain, ln_v_bias)
    cx_b, tok = pair_and_chip("b", px, duv)
    dq, dk, dv, ds_sums = _attention_bwd(proj, dattn, attn, lse, bias, after=tok)
    g_rb = _rel_bias_grad(ds_sums)[:, :NH]

    small_g = dict(rel_bias=g_rb, ln_v_gain=stv[0], ln_v_bias=stv[1], w_spatial=g_ws, b_spatial=g_bs[:, :, 0],
                   ln1_gain=st1[0], ln1_bias=st1[1], b_ff1=g_b1, b_ff2=st2[2], ln2_gain=st2[0], ln2_bias=st2[1])
    gs = _allreduce_small(_pack_small(small_g).at[_LOSS_AT].set(st2[3, 0]))
    ds_, ms_, vs_, _ = _adamw(_pack_small({k: args[k] for k in _SMALL}), gs,
                           _pack_small({k: args["m_" + k] for k in _SMALL}),
                           _pack_small({k: args["v_" + k] for k in _SMALL}), "adamw_small")
    like = {k: args[k] for k in _SMALL}
    grads, deltas, new_m, new_v = (_unpack_small(t, like) for t in (gs, ds_, ms_, vs_))

    dproj = jnp.concatenate([dq, dk, dv, duv, dga, dgb], axis=1)
    g_win = _grad_w(xb, dproj, "grad_w_in", 512, 2304, True, after=gs)
    px, tok = _px_start("pair_exchange_start_w_in", [g_win])

    def chip_sums(tag, state, names, after):
        pair_sums, from_chips = _cx_wait(f"chip_exchange_wait_{tag}", state, after)
        halves = [_chip_sum(p, own, f"chip_sum_{k}") for p, own, k in zip(from_chips, pair_sums, names)]
        return _share_start(f"share_start_{tag}", halves)

    def adam_one(k, g, after=None):
        d_, m_, v_, g_ = _adamw(args[k][0], g, args["m_" + k][0], args["v_" + k][0], f"adamw_{k}", after=after)
        grads[k], deltas[k], new_m[k], new_v[k] = g_[None], d_[None], m_[None], v_[None]
        return d_

    def adam(tag, state, names, after):
        last = None
        for k, g in zip(names, _share_wait(f"share_wait_{tag}", state, after)):
            last = adam_one(k, g)
        return last

    sh_w2, tok = chip_sums("w_ff2", cx_w2, ["w_ff2"], tok)
    sh_w1, tok = chip_sums("w_ff1", cx_w1, ["w_ff1"], tok)
    sh_b, tok = chip_sums("b", cx_b, ["w_proj_a", "w_proj_b", "w_out"], tok)
    cx_in, tok = pair_and_chip("w_in", px, tok)
    grad_x = _d_x(dproj, win_g, dpre1, after=tok)
    done = adam("w_ff2", sh_w2, ["w_ff2"], grad_x)
    done = adam("w_ff1", sh_w1, ["w_ff1"], done)
    g_wpa_full, g_wpb_full, g_wout_full = _share_wait("share_wait_b", sh_b, done)
    done = adam_one("w_out", g_wout_full)
    sh_in, tok = chip_sums("w_in", cx_in, ["w_in"], done)
    done = adam_one("w_proj_a", g_wpa_full, after=tok)
    done = adam_one("w_proj_b", g_wpb_full, after=done)
    adam("w_in", sh_in, ["w_in"], done)

    loss = gs[_LOSS_AT] * (0.5 / D)
    return (loss, grad_x[None], *[grads[k] for k in weights], *[deltas[k] for k in weights],
            *[new_m[k] for k in weights], *[new_v[k] for k in weights])
```

```python
import math

import numpy as np
import jax
import jax.numpy as jnp
from jax import lax
from jax.experimental import pallas as pl
from jax.experimental.pallas import tpu as pltpu

F32 = jnp.float32
BF16 = jnp.bfloat16

S = 2048
D = 2048
DA = 1024
DB = 1024
DFF = 8192
DIN = 9216
NH = 8
HD = 128
NBLK = 16
PATTERNS = ((128, 1), (512, 4), (2048, 16))
N_BUCKETS = 32
MAX_DISTANCE = 2048
ALPHA = 2.0 ** 0.25
LN_EPS = 1e-5
NEG_INF = -1e30
SCALE = HD ** -0.5
N_CHIPS = 4

ADAM_LR = 0.001
ADAM_B1 = 0.9
ADAM_B2 = 0.999
ADAM_EPS = 1e-08
ADAM_WD = 0.01
ADAM_STEP = 10

VMEM_LIMIT = 56 * 1024 * 1024
MESH = pl.DeviceIdType.MESH
ANY = pl.BlockSpec(memory_space=pl.ANY)


def _params(n_axes, vmem=VMEM_LIMIT):
    return pltpu.CompilerParams(dimension_semantics=("arbitrary",) * n_axes, vmem_limit_bytes=vmem)


def _bucket_tile(dilation):
    qi = np.arange(128)[:, None]
    kj = np.arange(256)[None, :]
    n = np.clip(128 + qi - kj, 0, 128) * dilation
    max_exact = N_BUCKETS // 2
    nf = np.maximum(n, 1).astype(np.float32)
    large = max_exact + (np.log(nf / np.float32(max_exact)) / np.float32(math.log(MAX_DISTANCE / max_exact))
                         * np.float32(N_BUCKETS - max_exact)).astype(np.int32)
    large = np.minimum(large, N_BUCKETS - 1)
    return np.where(n < max_exact, n, large).astype(np.int32)


def _gelu(x):
    c = math.sqrt(2.0 / math.pi)
    t = jnp.tanh(c * (x + 0.044715 * x * x * x))
    return 0.5 * x * (1.0 + t), t


def _gelu_grad(x, t):
    c = math.sqrt(2.0 / math.pi)
    return 0.5 * (1.0 + t) + 0.5 * x * (1.0 - t * t) * c * (1.0 + 3.0 * 0.044715 * x * x)


def _sigmoid(x):
    return 1.0 / (1.0 + jnp.exp(-x))


def _dot(a, b):
    return jnp.dot(a, b, preferred_element_type=F32)


def _behind(body, n_in, after):
    if after is None:
        return body, [], []
    return (lambda *refs: body(*refs[:n_in], *refs[n_in + 1:])), [ANY], [after]


def _dot_nt(a, b):
    return lax.dot_general(a, b, (((1,), (1,)), ((), ())), preferred_element_type=F32)


def _proj(xb, win_g, shards, name, into=None):
    tn = 768
    per = 2304 // tn

    def body(shards_ref, x_ref, w_ref, *rest):
        rest[-1][...] = _dot(x_ref[...], w_ref[...])

    in_specs = [pl.BlockSpec((S, D), lambda j, sh: (0, 0)),
                pl.BlockSpec((None, D, tn), lambda j, sh: (sh[j // per], 0, j % per))]
    return pl.pallas_call(
        body, name=name,
        grid_spec=pltpu.PrefetchScalarGridSpec(
            num_scalar_prefetch=1, grid=(shards.shape[0] * per,),
            in_specs=in_specs + ([ANY] if into is not None else []),
            out_specs=pl.BlockSpec((S, tn), lambda j, sh: (0, sh[j // per] * per + j % per))),
        out_shape=jax.ShapeDtypeStruct((S, DIN), F32),
        input_output_aliases={3: 0} if into is not None else {},
        compiler_params=_params(1),
    )(shards, xb, win_g, *([into] if into is not None else []))


FWD_HEADS_PER_STEP = 4
BWD_HEADS_PER_STEP = 2


def _bias_tiles(rel_bias):
    buckets = jnp.asarray(np.stack([_bucket_tile(d) for _, d in PATTERNS]))

    def body(rb_ref, bk_ref, o_ref):
        qi = lax.broadcasted_iota(jnp.int32, (128, 256), 0)
        kj = lax.broadcasted_iota(jnp.int32, (128, 256), 1)
        steps = 128 + qi - kj
        band = (steps >= 0) & (steps <= 128)
        o_ref[...] = jnp.zeros_like(o_ref)
        for p in range(len(PATTERNS)):
            bucket = bk_ref[p]

            def one_bucket(t, carry):
                hit = bucket == t
                for h in range(NH):
                    o_ref[p, h] = jnp.where(hit, rb_ref[t, h], o_ref[p, h])
                return carry

            lax.fori_loop(0, N_BUCKETS, one_bucket, 0)
            for h in range(NH):
                o_ref[p, h] = jnp.where(band, o_ref[p, h], NEG_INF)

    return pl.pallas_call(
        body, name="bias_tiles",
        in_specs=[pl.BlockSpec(memory_space=pltpu.SMEM), pl.BlockSpec(memory_space=pltpu.VMEM)],
        out_specs=pl.BlockSpec(memory_space=pltpu.VMEM),
        out_shape=jax.ShapeDtypeStruct((len(PATTERNS), NH, 128, 256), F32),
        compiler_params=pltpu.CompilerParams(vmem_limit_bytes=VMEM_LIMIT),
    )(rel_bias, buckets)


def _block_rows(b, dilation):
    nblk = NBLK // dilation
    r, n = b // nblk, b % nblk
    start = r + n * (128 * dilation)
    prev_start = jnp.maximum(start - 128 * dilation, r)
    if dilation == 1:
        return pl.ds(pl.multiple_of(start, 128), 128), pl.ds(pl.multiple_of(prev_start, 128), 128), n > 0
    return pl.ds(start, 128, stride=dilation), pl.ds(prev_start, 128, stride=dilation), n > 0


def _head_specs(first, hps):
    return [pl.BlockSpec((S, HD), lambda g, j=j: (0, first + g * hps + j)) for j in range(hps)]


def _bias_spec(hps):
    return pl.BlockSpec((len(PATTERNS), hps, 128, 256), lambda g: (0, g, 0, 0))


def _heads_spec(hps):
    return pl.BlockSpec((S, hps * HD), lambda g: (0, g))


def _attention_fwd(proj, bias, after=None):
    hps = FWD_HEADS_PER_STEP

    def body(bias_ref, *refs):
        q_refs, k_refs, v_refs = (refs[i * hps:(i + 1) * hps] for i in range(3))
        o_ref, lse_ref = refs[3 * hps:3 * hps + 2]
        acc_scrs, m_scrs, l_scrs = (refs[3 * hps + 2 + i * hps:3 * hps + 2 + (i + 1) * hps] for i in range(3))
        kj = lax.broadcasted_iota(jnp.int32, (128, 256), 1)
        for p, (_, d) in enumerate(PATTERNS):
            prev_blocks = NBLK // d > 1

            def block(b, carry):
                units = [(j,) + _block_rows(blk, d) for blk in (b, b + NBLK // 2) for j in range(hps)]
                scores = []
                for j, rows, prows, _ in units:
                    q = q_refs[j][rows, :].astype(BF16)
                    cur = _dot_nt(q, k_refs[j][rows, :].astype(BF16))
                    if prev_blocks:
                        cur = jnp.concatenate([_dot_nt(q, k_refs[j][prows, :].astype(BF16)), cur], axis=1)
                    scores.append(cur)
                soft = []
                for u, (j, _, _, has_prev) in enumerate(units):
                    if prev_blocks:
                        s = jnp.where((kj >= 128) | has_prev, scores[u] * SCALE + bias_ref[p, j], NEG_INF)
                    else:
                        s = scores[u] * SCALE + bias_ref[p, j, :, 128:256]
                    m = jnp.max(s, axis=1, keepdims=True)
                    e = jnp.exp(s - m)
                    soft.append((m, jnp.sum(e, axis=1, keepdims=True), e.astype(BF16)))
                outs = []
                for u, (j, rows, prows, _) in enumerate(units):
                    e = soft[u][2]
                    if prev_blocks:
                        outs.append(_dot(e[:, :128], v_refs[j][prows, :].astype(BF16))
                                    + _dot(e[:, 128:], v_refs[j][rows, :].astype(BF16)))
                    else:
                        outs.append(_dot(e, v_refs[j][rows, :].astype(BF16)))
                for u, (j, rows, _, _) in enumerate(units):
                    acc_scr, m_scr, l_scr = acc_scrs[j], m_scrs[j], l_scrs[j]
                    (m, den, _), o = soft[u], outs[u]
                    if p == 0:
                        acc_scr[rows, :] = o
                        m_scr[rows, :] = jnp.broadcast_to(m, (128, HD))
                        l_scr[rows, :] = jnp.broadcast_to(den, (128, HD))
                    else:
                        m_old = m_scr[rows, :]
                        m_new = jnp.maximum(m_old, m)
                        w_old, w_new = jnp.exp(m_old - m_new), jnp.exp(m - m_new)
                        acc_scr[rows, :] = acc_scr[rows, :] * w_old + o * w_new
                        l_scr[rows, :] = l_scr[rows, :] * w_old + den * w_new
                        m_scr[rows, :] = m_new
                return carry

            lax.fori_loop(0, NBLK // 2, block, 0)
        for j in range(hps):
            cols = slice(j * HD, (j + 1) * HD)
            den = l_scrs[j][...]
            o_ref[:, cols] = (acc_scrs[j][...] / den).astype(BF16)
            lse_ref[:, cols] = m_scrs[j][...] + jnp.log(den)

    body, more_specs, more = _behind(body, 1 + 3 * hps, after)
    return pl.pallas_call(
        body, name="attention_fwd", grid=(NH // hps,),
        in_specs=[_bias_spec(hps)] + _head_specs(0, hps) + _head_specs(NH, hps) + _head_specs(2 * NH, hps) + more_specs,
        out_specs=[_heads_spec(hps), _heads_spec(hps)],
        out_shape=[jax.ShapeDtypeStruct((S, DA), BF16), jax.ShapeDtypeStruct((S, DA), F32)],
        scratch_shapes=[pltpu.VMEM((S, HD), F32)] * (3 * hps),
        compiler_params=_params(1),
    )(bias, *([proj] * (3 * hps)), *more)


def _attention_bwd(proj, dattn, attn, lse, bias, after=None):
    hps = BWD_HEADS_PER_STEP

    def body(bias_ref, *refs):
        q_refs, k_refs, v_refs, do_refs, o_refs, lse_refs = (refs[i * hps:(i + 1) * hps] for i in range(6))
        dq_ref, dk_ref, dv_ref, ds_ref = refs[6 * hps:6 * hps + 4]
        dl_scrs, dq_scrs, dk_scrs, dv_scrs = (refs[6 * hps + 4 + i * hps:6 * hps + 4 + (i + 1) * hps] for i in range(4))
        ds_ref[...] = jnp.zeros_like(ds_ref)
        for j in range(hps):
            dq_scrs[j][...] = jnp.zeros((S, HD), F32)
            dk_scrs[j][...] = jnp.zeros((S, HD), F32)
            dv_scrs[j][...] = jnp.zeros((S, HD), F32)
            prod = do_refs[j][...] * o_refs[j][...].astype(F32)
            dl_scrs[j][...] = jnp.broadcast_to(jnp.sum(prod, axis=1, keepdims=True), (S, HD))
        for p, (_, d) in enumerate(PATTERNS):
            prev_blocks = NBLK // d > 1

            def block(b, carry):
                units = [(j,) + _block_rows(b + i * (NBLK // 4), d) for i in range(4) for j in range(hps)]
                ops, raw = [], []
                for j, rows, prows, _ in units:
                    q, do = q_refs[j][rows, :].astype(BF16), do_refs[j][rows, :].astype(BF16)
                    kc, vc = k_refs[j][rows, :].astype(BF16), v_refs[j][rows, :].astype(BF16)
                    if prev_blocks:
                        kp, vp = k_refs[j][prows, :].astype(BF16), v_refs[j][prows, :].astype(BF16)
                        ops.append((q, do, kc, kp))
                        raw.append((_dot_nt(q, kc), _dot_nt(do, vc), _dot_nt(q, kp), _dot_nt(do, vp)))
                    else:
                        ops.append((q, do, kc))
                        raw.append((_dot_nt(q, kc), _dot_nt(do, vc)))
                probs = []
                for u, (j, rows, _, has_prev) in enumerate(units):
                    lse_b, dl_b = lse_refs[j][rows, :], dl_scrs[j][rows, :]
                    p_c = jnp.exp(raw[u][0] * SCALE + bias_ref[p, j, :, 128:256] - lse_b)
                    ds_c = p_c * (raw[u][1] - dl_b)
                    ds_ref[p, j, :, 128:256] += ds_c
                    if prev_blocks:
                        p_p = jnp.where(has_prev, jnp.exp(raw[u][2] * SCALE + bias_ref[p, j, :, 0:128] - lse_b), 0.0)
                        ds_p = p_p * (raw[u][3] - dl_b)
                        ds_ref[p, j, :, 0:128] += ds_p
                        probs.append((p_c, ds_c, p_p, ds_p))
                    else:
                        probs.append((p_c, ds_c))
                grads = []
                for u in range(len(units)):
                    q, do, kc = ops[u][:3]
                    p_c, ds_c = probs[u][:2]
                    dq = _dot(ds_c.astype(BF16), kc)
                    cur = (_dot(ds_c.T.astype(BF16), q) * SCALE, _dot(p_c.T.astype(BF16), do))
                    if prev_blocks:
                        p_p, ds_p = probs[u][2:]
                        dq = dq + _dot(ds_p.astype(BF16), ops[u][3])
                        cur = cur + (_dot(ds_p.T.astype(BF16), q) * SCALE, _dot(p_p.T.astype(BF16), do))
                    grads.append((dq * SCALE,) + cur)
                for u, (j, rows, prows, _) in enumerate(units):
                    dq_scrs[j][rows, :] += grads[u][0]
                    dk_scrs[j][rows, :] += grads[u][1]
                    dv_scrs[j][rows, :] += grads[u][2]
                    if prev_blocks:
                        dk_scrs[j][prows, :] += grads[u][3]
                        dv_scrs[j][prows, :] += grads[u][4]
                return carry

            lax.fori_loop(0, NBLK // 4, block, 0)
        for j in range(hps):
            cols = slice(j * HD, (j + 1) * HD)
            dq_ref[:, cols] = dq_scrs[j][...].astype(BF16)
            dk_ref[:, cols] = dk_scrs[j][...].astype(BF16)
            dv_ref[:, cols] = dv_scrs[j][...].astype(BF16)

    body, more_specs, more = _behind(body, 1 + 6 * hps, after)
    return pl.pallas_call(
        body, name="attention_bwd", grid=(NH // hps,),
        in_specs=[_bias_spec(hps)]
        + _head_specs(0, hps) + _head_specs(NH, hps) + _head_specs(2 * NH, hps) + 3 * _head_specs(0, hps)
        + more_specs,
        out_specs=3 * [_heads_spec(hps)] + [pl.BlockSpec((3, hps, 128, 256), lambda g: (0, g, 0, 0))],
        out_shape=[jax.ShapeDtypeStruct((S, DA), BF16)] * 3 + [jax.ShapeDtypeStruct((3, NH, 128, 256), F32)],
        scratch_shapes=[pltpu.VMEM((S, HD), F32)] * (4 * hps),
        compiler_params=_params(1),
    )(bias, *([proj] * (3 * hps)), *([dattn] * hps), *([attn] * hps), *([lse] * hps), *more)


def _gmlp_parts(u_ref, vb_ref, g_ref, be_ref):
    u = u_ref[...]
    u_act, tu = _gelu(u)
    vb = vb_ref[...]
    gv, tv = _gelu(vb)
    mean = jnp.mean(gv, axis=1, keepdims=True)
    cen = gv - mean
    var = jnp.mean(cen * cen, axis=1, keepdims=True)
    rstd = lax.rsqrt(var + LN_EPS)
    xhat = cen * rstd
    vn = xhat * g_ref[...] + be_ref[...]
    return u, tu, u_act, vb, tv, rstd, xhat, vn


def _gmlp_fwd(proj, ws, bsp_b, gain_v, bias_v):
    def body(u_ref, vb_ref, ws_ref, bsp_ref, g_ref, be_ref, o_ref):
        _, _, u_act, _, _, _, _, vn = _gmlp_parts(u_ref, vb_ref, g_ref, be_ref)
        row = lax.broadcasted_iota(jnp.int32, (128, 128), 0)
        col = lax.broadcasted_iota(jnp.int32, (128, 128), 1)
        causal = row >= col
        for g in range(NH):
            cols = slice(g * 128, (g + 1) * 128)
            wsg = jnp.where(causal, ws_ref[g], 0.0).astype(BF16)
            z = _dot(wsg, vn[:, cols].astype(BF16)) + bsp_ref[g]
            o_ref[:, cols] = (u_act[:, cols] * z).astype(BF16)

    return pl.pallas_call(
        body, name="gmlp_fwd", grid=(NBLK,),
        in_specs=[pl.BlockSpec((128, DB), lambda c: (c, 3)), pl.BlockSpec((128, DB), lambda c: (c, 4)),
                  pl.BlockSpec((NH, 128, 128), lambda c: (0, 0, 0)), pl.BlockSpec((NH, 128, 128), lambda c: (0, 0, 0)),
                  pl.BlockSpec((1, DB), lambda c: (0, 0)), pl.BlockSpec((1, DB), lambda c: (0, 0))],
        out_specs=pl.BlockSpec((128, DB), lambda c: (c, 0)),
        out_shape=jax.ShapeDtypeStruct((S, DB), BF16),
        compiler_params=_params(1),
    )(proj, proj, ws, bsp_b, gain_v, bias_v)


def _branch(attn, gmlp, wpa_g, wpb_g, proj):
    tn = 512

    def body(a_ref, g_ref, wa_ref, wb_ref, ga_ref, gb_ref, ya_ref, yb_ref, mg_ref):
        ya = _dot(a_ref[...], wa_ref[...])
        yb = _dot(g_ref[...], wb_ref[...])
        ya_ref[...] = ya.astype(BF16)
        yb_ref[...] = yb.astype(BF16)
        mg_ref[...] = (_sigmoid(ga_ref[...]) * ya + _sigmoid(gb_ref[...]) * yb).astype(BF16)

    out = pl.BlockSpec((S, tn), lambda j: (0, j))
    return pl.pallas_call(
        body, name="branch", grid=(D // tn,),
        in_specs=[pl.BlockSpec((S, DA), lambda j: (0, 0)), pl.BlockSpec((S, DB), lambda j: (0, 0)),
                  pl.BlockSpec((None, DA, tn), lambda j: (j, 0, 0)), pl.BlockSpec((None, DB, tn), lambda j: (j, 0, 0)),
                  pl.BlockSpec((S, tn), lambda j: (0, 5120 // tn + j)), pl.BlockSpec((S, tn), lambda j: (0, 7168 // tn + j))],
        out_specs=[out, out, out],
        out_shape=[jax.ShapeDtypeStruct((S, D), BF16)] * 3,
        compiler_params=_params(1),
    )(attn, gmlp, wpa_g, wpb_g, proj, proj)


def _out_ln1(merged, wout_g, x, gain, bias):
    tm = 256

    def body(m_ref, w_ref, x_ref, g_ref, b_ref, xh_ref, rs_ref, h_ref):
        pre = ALPHA * x_ref[...] + _dot(m_ref[...], w_ref[...])
        mean = jnp.mean(pre, axis=1, keepdims=True)
        cen = pre - mean
        var = jnp.mean(cen * cen, axis=1, keepdims=True)
        rstd = lax.rsqrt(var + LN_EPS)
        xhat = cen * rstd
        xh_ref[...] = xhat
        rs_ref[...] = jnp.broadcast_to(rstd, (tm, 128))
        h_ref[...] = (xhat * g_ref[...] + b_ref[...]).astype(BF16)

    row = pl.BlockSpec((tm, D), lambda i: (i, 0))
    vec = pl.BlockSpec((1, D), lambda i: (0, 0))
    return pl.pallas_call(
        body, name="out_ln1", grid=(S // tm,),
        in_specs=[row, pl.BlockSpec((D, D), lambda i: (0, 0)), row, vec, vec],
        out_specs=[row, pl.BlockSpec((tm, 128), lambda i: (i, 0)), row],
        out_shape=[jax.ShapeDtypeStruct((S, D), F32), jax.ShapeDtypeStruct((S, 128), F32),
                   jax.ShapeDtypeStruct((S, D), BF16)],
        compiler_params=_params(1),
    )(merged, wout_g, x, gain, bias)


def _ff1(h1b, w1_g, b1, half, name, into=None, after=None):
    tn = 512
    per = D // tn
    steps = DFF // tn // 2
    first = half * steps

    def body(h_ref, w_ref, b_ref, *rest):
        a_ref, r_ref = rest[-2:]
        r = jnp.maximum(_dot(h_ref[...], w_ref[...]) + b_ref[...], 0.0)
        r_ref[...] = r.astype(BF16)
        a_ref[...] = (r * r).astype(BF16)

    out = pl.BlockSpec((S, tn), lambda j: (0, first + j))
    extra = list(into) if into is not None else []
    if after is not None:
        extra.append(after)
    return pl.pallas_call(
        body, name=name, grid=(steps,),
        in_specs=[pl.BlockSpec((S, D), lambda j: (0, 0)),
                  pl.BlockSpec((None, D, tn), lambda j: ((first + j) // per, 0, (first + j) % per)),
                  pl.BlockSpec((1, tn), lambda j: (0, first + j))] + [ANY] * len(extra),
        out_specs=[out, out],
        out_shape=[jax.ShapeDtypeStruct((S, DFF), BF16)] * 2,
        input_output_aliases={3: 0, 4: 1} if into is not None else {},
        compiler_params=_params(1),
    )(h1b, w1_g, b1, *extra)


def _ff2_ln2_loss(a, w2_g, xhat1, g1, b1, b2, g2, be2, target):
    tm, tk = 512, 1024
    nk = DFF // tk

    def body(a_ref, w_ref, xh_ref, g1_ref, b1_ref, b2_ref, g2_ref, be2_ref, t_ref, d_ref, db_ref, st_ref, acc):
        i, k = pl.program_id(0), pl.program_id(1)

        @pl.when(k == 0)
        def _():
            acc[...] = jnp.zeros_like(acc)

        @pl.when((i == 0) & (k == 0))
        def _():
            st_ref[...] = jnp.zeros_like(st_ref)

        acc[...] += _dot(a_ref[...], w_ref[...])

        @pl.when(k == nk - 1)
        def _():
            def rows_chunk(ci, carry):
                rows = pl.ds(pl.multiple_of(ci * 128, 128), 128)
                h1 = xh_ref[rows, :] * g1_ref[...] + b1_ref[...]
                pre = ALPHA * h1 + acc[rows, :] + b2_ref[...]
                mean = jnp.mean(pre, axis=1, keepdims=True)
                cen = pre - mean
                var = jnp.mean(cen * cen, axis=1, keepdims=True)
                rstd = lax.rsqrt(var + LN_EPS)
                xhat = cen * rstd
                y = xhat * g2_ref[...] + be2_ref[...]
                err = y - t_ref[rows, :]
                dy = err * (1.0 / D)
                g = dy * g2_ref[...]
                dpre = rstd * (g - jnp.mean(g, axis=1, keepdims=True)
                               - xhat * jnp.mean(g * xhat, axis=1, keepdims=True))
                d_ref[rows, :] = dpre
                db_ref[rows, :] = dpre.astype(BF16)
                st_ref[0:1, :] += jnp.sum(dy * xhat, axis=0, keepdims=True)
                st_ref[1:2, :] += jnp.sum(dy, axis=0, keepdims=True)
                st_ref[2:3, :] += jnp.sum(dpre, axis=0, keepdims=True)
                st_ref[3:4, :] += jnp.broadcast_to(jnp.sum(err * err).reshape(1, 1), (1, D))
                return carry

            lax.fori_loop(0, tm // 128, rows_chunk, 0)

    row = pl.BlockSpec((tm, D), lambda i, k: (i, 0))
    vec = pl.BlockSpec((1, D), lambda i, k: (0, 0))
    return pl.pallas_call(
        body, name="ff2_ln2_loss", grid=(S // tm, nk),
        in_specs=[pl.BlockSpec((tm, tk), lambda i, k: (i, k)), pl.BlockSpec((tk, D), lambda i, k: (k, 0)),
                  row, vec, vec, vec, vec, vec, row],
        out_specs=[row, row, pl.BlockSpec((8, D), lambda i, k: (0, 0))],
        out_shape=[jax.ShapeDtypeStruct((S, D), F32), jax.ShapeDtypeStruct((S, D), BF16),
                   jax.ShapeDtypeStruct((8, D), F32)],
        scratch_shapes=[pltpu.VMEM((tm, D), F32)],
        compiler_params=_params(2),
    )(a, w2_g, xhat1, g1, b1, b2, g2, be2, target)


def _grad_w(act, dout, name, ti, tj, sharded, after=None):
    m, n = act.shape[1], dout.shape[1]
    ns = n // N_CHIPS
    per = ns // tj if sharded else None

    def body(a_ref, b_ref, o_ref, at_scr):
        @pl.when(pl.program_id(1) == 0)
        def _():
            at_scr[...] = a_ref[...].T

        o_ref[...] = _dot(at_scr[...], b_ref[...]).astype(BF16)

    if sharded:
        out_spec = pl.BlockSpec((None, ti, tj), lambda i, j: (j // per, i, j % per))
        out_shape = jax.ShapeDtypeStruct((N_CHIPS, m, ns), BF16)
    else:
        out_spec = pl.BlockSpec((ti, tj), lambda i, j: (i, j))
        out_shape = jax.ShapeDtypeStruct((m, n), BF16)
    body, more_specs, more = _behind(body, 2, after)
    return pl.pallas_call(
        body, name=name, grid=(m // ti, n // tj),
        in_specs=[pl.BlockSpec((S, ti), lambda i, j: (0, i)), pl.BlockSpec((S, tj), lambda i, j: (0, j))] + more_specs,
        out_specs=out_spec, out_shape=out_shape,
        scratch_shapes=[pltpu.VMEM((ti, S), BF16)],
        compiler_params=_params(2),
    )(act, dout, *more)


def _d_ff1(dpre2b, w2_g, r, after=None):
    tn = 512

    def body(d_ref, w_ref, r_ref, o_ref, gb_ref):
        da = _dot_nt(d_ref[...], w_ref[...])
        dp = da * (2.0 * r_ref[...].astype(F32))
        o_ref[...] = dp.astype(BF16)
        gb_ref[...] = jnp.sum(dp, axis=0, keepdims=True)

    body, more_specs, more = _behind(body, 3, after)
    return pl.pallas_call(
        body, name="d_ff1", grid=(DFF // tn,),
        in_specs=[pl.BlockSpec((S, D), lambda j: (0, 0)), pl.BlockSpec((tn, D), lambda j: (j, 0)),
                  pl.BlockSpec((S, tn), lambda j: (0, j))] + more_specs,
        out_specs=[pl.BlockSpec((S, tn), lambda j: (0, j)), pl.BlockSpec((1, tn), lambda j: (0, j))],
        out_shape=[jax.ShapeDtypeStruct((S, DFF), BF16), jax.ShapeDtypeStruct((1, DFF), F32)],
        compiler_params=_params(1),
    )(dpre2b, w2_g, r, *more)


def _d_h1_ln1(dprea, w1_g, dpre2, xhat1, rstd1, g1, after=None):
    tm, tk = 512, 1024
    per = D // tk
    nk = DFF // tk

    def body(a_ref, w_ref, d2_ref, xh_ref, rs_ref, g_ref, d_ref, db_ref, st_ref, acc):
        i, k = pl.program_id(0), pl.program_id(1)

        @pl.when(k == 0)
        def _():
            acc[...] = jnp.zeros_like(acc)

        @pl.when((i == 0) & (k == 0))
        def _():
            st_ref[...] = jnp.zeros_like(st_ref)

        acc[...] += _dot_nt(a_ref[...], w_ref[...])

        @pl.when(k == nk - 1)
        def _():
            def rows_chunk(ci, carry):
                rows = pl.ds(pl.multiple_of(ci * 128, 128), 128)
                dh = ALPHA * d2_ref[rows, :] + acc[rows, :]
                xhat = xh_ref[rows, :]
                g = dh * g_ref[...]
                dpre = rs_ref[rows, 0:1] * (g - jnp.mean(g, axis=1, keepdims=True)
                                            - xhat * jnp.mean(g * xhat, axis=1, keepdims=True))
                d_ref[rows, :] = dpre
                db_ref[rows, :] = dpre.astype(BF16)
                st_ref[0:1, :] += jnp.sum(dh * xhat, axis=0, keepdims=True)
                st_ref[1:2, :] += jnp.sum(dh, axis=0, keepdims=True)
                return carry

            lax.fori_loop(0, tm // 128, rows_chunk, 0)

    row = pl.BlockSpec((tm, D), lambda i, k: (i, 0))
    body, more_specs, more = _behind(body, 6, after)
    return pl.pallas_call(
        body, name="d_h1_ln1", grid=(S // tm, nk),
        in_specs=[pl.BlockSpec((tm, tk), lambda i, k: (i, k)),
                  pl.BlockSpec((None, D, tk), lambda i, k: (k // per, 0, k % per)),
                  row, row, pl.BlockSpec((tm, 128), lambda i, k: (i, 0)), pl.BlockSpec((1, D), lambda i, k: (0, 0))]
        + more_specs,
        out_specs=[row, row, pl.BlockSpec((8, D), lambda i, k: (0, 0))],
        out_shape=[jax.ShapeDtypeStruct((S, D), F32), jax.ShapeDtypeStruct((S, D), BF16),
                   jax.ShapeDtypeStruct((8, D), F32)],
        scratch_shapes=[pltpu.VMEM((tm, D), F32)],
        compiler_params=_params(2),
    )(dprea, w1_g, dpre2, xhat1, rstd1, g1, *more)


def _d_merged(dpre1b, wout_g, proj, ya, yb):
    tm, tn = 512, 1024

    def body(d_ref, w_ref, ga_ref, gb_ref, ya_ref, yb_ref, dya_ref, dyb_ref, dga_ref, dgb_ref):
        dm = _dot_nt(d_ref[...], w_ref[...])
        sa = _sigmoid(ga_ref[...])
        sb = _sigmoid(gb_ref[...])
        dya_ref[...] = (dm * sa).astype(BF16)
        dyb_ref[...] = (dm * sb).astype(BF16)
        dga_ref[...] = (dm * ya_ref[...].astype(F32) * sa * (1.0 - sa)).astype(BF16)
        dgb_ref[...] = (dm * yb_ref[...].astype(F32) * sb * (1.0 - sb)).astype(BF16)

    tile = pl.BlockSpec((tm, tn), lambda j, i: (i, j))
    return pl.pallas_call(
        body, name="d_merged", grid=(D // tn, S // tm),
        in_specs=[pl.BlockSpec((tm, D), lambda j, i: (i, 0)), pl.BlockSpec((tn, D), lambda j, i: (j, 0)),
                  pl.BlockSpec((tm, tn), lambda j, i: (i, 5 + j)), pl.BlockSpec((tm, tn), lambda j, i: (i, 7 + j)),
                  tile, tile],
        out_specs=[tile] * 4,
        out_shape=[jax.ShapeDtypeStruct((S, D), BF16)] * 4,
        compiler_params=_params(2),
    )(dpre1b, wout_g, proj, proj, ya, yb)


def _d_branches(dya, dyb, wpa_g, wpb_g, after=None):
    tm = 512
    ws = D // N_CHIPS

    def body(da_ref, db_ref, wa_ref, wb_ref, oa_ref, ob_ref):
        for d_ref, w_ref, o_ref in ((da_ref, wa_ref, oa_ref), (db_ref, wb_ref, ob_ref)):
            acc = _dot_nt(d_ref[:, 0:ws], w_ref[0])
            for s in range(1, N_CHIPS):
                acc = acc + _dot_nt(d_ref[:, s * ws:(s + 1) * ws], w_ref[s])
            o_ref[...] = acc

    rows = lambda width: pl.BlockSpec((tm, width), lambda i: (i, 0))
    whole = lambda n: pl.BlockSpec((N_CHIPS, n, ws), lambda i: (0, 0, 0))
    body, more_specs, more = _behind(body, 4, after)
    return pl.pallas_call(
        body, name="d_branches", grid=(S // tm,),
        in_specs=[rows(D), rows(D), whole(DA), whole(DB)] + more_specs,
        out_specs=[rows(DA), rows(DB)],
        out_shape=[jax.ShapeDtypeStruct((S, DA), F32), jax.ShapeDtypeStruct((S, DB), F32)],
        compiler_params=_params(1),
    )(dya, dyb, wpa_g, wpb_g, *more)


def _gmlp_bwd(proj, dgmlp, ws, ws_t, bsp_b, gain_v, bias_v):
    def body(u_ref, vb_ref, dg_ref, ws_ref, wst_ref, bsp_ref, g_ref, be_ref, duv_ref, gws_ref, gbs_ref, st_ref):
        @pl.when(pl.program_id(0) == 0)
        def _():
            gws_ref[...] = jnp.zeros_like(gws_ref)
            gbs_ref[...] = jnp.zeros_like(gbs_ref)
            st_ref[...] = jnp.zeros_like(st_ref)

        u, tu, u_act, vb, tv, rstd, xhat, vn = _gmlp_parts(u_ref, vb_ref, g_ref, be_ref)
        dg = dg_ref[...]
        dz = dg * u_act
        row = lax.broadcasted_iota(jnp.int32, (128, 128), 0)
        col = lax.broadcasted_iota(jnp.int32, (128, 128), 1)
        causal = row >= col
        causal_t = row <= col
        dvn_parts = []
        z_parts = []
        for g in range(NH):
            cols = slice(g * 128, (g + 1) * 128)
            vng = vn[:, cols].astype(BF16)
            dzg = dz[:, cols]
            dzb = dzg.astype(BF16)
            wsg = jnp.where(causal, ws_ref[g], 0.0).astype(BF16)
            wsg_t = jnp.where(causal_t, wst_ref[g], 0.0).astype(BF16)
            z_parts.append(_dot(wsg, vng) + bsp_ref[g])
            gws_ref[g] += jnp.where(causal, _dot_nt(dzb, vng), 0.0)
            gbs_ref[g] += jnp.broadcast_to(jnp.sum(dzg, axis=1, keepdims=True), (128, 128))
            dvn_parts.append(_dot(wsg_t, dzb))
        z = jnp.concatenate(z_parts, axis=1)
        dvn = jnp.concatenate(dvn_parts, axis=1)
        du = dg * z * _gelu_grad(u, tu)
        st_ref[0:1, :] += jnp.sum(dvn * xhat, axis=0, keepdims=True)
        st_ref[1:2, :] += jnp.sum(dvn, axis=0, keepdims=True)
        gg = dvn * g_ref[...]
        dgv = rstd * (gg - jnp.mean(gg, axis=1, keepdims=True) - xhat * jnp.mean(gg * xhat, axis=1, keepdims=True))
        dvb = dgv * _gelu_grad(vb, tv)
        duv_ref[:, 0:DB] = du.astype(BF16)
        duv_ref[:, DB:2 * DB] = dvb.astype(BF16)

    full3 = pl.BlockSpec((NH, 128, 128), lambda c: (0, 0, 0))
    vec = pl.BlockSpec((1, DB), lambda c: (0, 0))
    return pl.pallas_call(
        body, name="gmlp_bwd", grid=(NBLK,),
        in_specs=[pl.BlockSpec((128, DB), lambda c: (c, 3)), pl.BlockSpec((128, DB), lambda c: (c, 4)),
                  pl.BlockSpec((128, DB), lambda c: (c, 0)), full3, full3, full3, vec, vec],
        out_specs=[pl.BlockSpec((128, 2 * DB), lambda c: (c, 0)), full3, full3, pl.BlockSpec((8, DB), lambda c: (0, 0))],
        out_shape=[jax.ShapeDtypeStruct((S, 2 * DB), BF16), jax.ShapeDtypeStruct((NH, 128, 128), F32),
                   jax.ShapeDtypeStruct((NH, 128, 128), F32), jax.ShapeDtypeStruct((8, DB), F32)],
        compiler_params=_params(1),
    )(proj, proj, dgmlp, ws, ws_t, bsp_b, gain_v, bias_v)


def _rel_bias_grad(ds_sums):
    buckets = jnp.asarray(np.stack([_bucket_tile(d) for _, d in PATTERNS]))

    def body(bk_ref, ds_ref, o_ref):
        row = lax.broadcasted_iota(jnp.int32, (N_BUCKETS, 128), 0)
        lane = lax.broadcasted_iota(jnp.int32, (N_BUCKETS, 128), 1)

        def one_bucket(t, out):
            hits = [bk_ref[p] == t for p in range(3)]
            for h in range(NH):
                tot = jnp.zeros((128, 256), F32)
                for p in range(3):
                    tot = tot + jnp.where(hits[p], ds_ref[p, h], 0.0)
                out = jnp.where((row == t) & (lane == h), jnp.sum(tot), out)
            return out

        o_ref[...] = lax.fori_loop(0, N_BUCKETS, one_bucket, jnp.zeros((N_BUCKETS, 128), F32))

    return pl.pallas_call(
        body, name="rel_bias_grad",
        in_specs=[pl.BlockSpec(memory_space=pltpu.VMEM)] * 2, out_specs=pl.BlockSpec(memory_space=pltpu.VMEM),
        out_shape=jax.ShapeDtypeStruct((N_BUCKETS, 128), F32),
        compiler_params=pltpu.CompilerParams(vmem_limit_bytes=VMEM_LIMIT),
    )(buckets, ds_sums)


def _d_x(dproj, win_g, dpre1, after=None):
    tm, tn = 512, 512
    ws = DIN // N_CHIPS

    def body(a_ref, w_ref, d_ref, o_ref):
        acc = ALPHA * d_ref[...]
        for s in range(N_CHIPS):
            acc = acc + _dot_nt(a_ref[:, s * ws:(s + 1) * ws], w_ref[s])
        o_ref[...] = acc

    tile = pl.BlockSpec((tm, tn), lambda i, j: (i, j))
    body, more_specs, more = _behind(body, 3, after)
    return pl.pallas_call(
        body, name="d_x", grid=(S // tm, D // tn),
        in_specs=[pl.BlockSpec((tm, DIN), lambda i, j: (i, 0)),
                  pl.BlockSpec((N_CHIPS, tn, ws), lambda i, j: (0, j, 0)), tile] + more_specs,
        out_specs=tile, out_shape=jax.ShapeDtypeStruct((S, D), F32),
        compiler_params=_params(2),
    )(dproj, win_g, dpre1, *more)


def _adamw(w, g, m, v, name, after=None):
    rows, cols = w.shape
    tm = max(t for t in range(8, 257, 8) if rows % t == 0)

    def body(w_ref, g_ref, m_ref, v_ref, d_ref, nm_ref, nv_ref, go_ref):
        g = g_ref[...]
        m = ADAM_B1 * m_ref[...] + (1.0 - ADAM_B1) * g
        v = ADAM_B2 * v_ref[...] + (1.0 - ADAM_B2) * (g * g)
        m_hat = m / (1.0 - ADAM_B1 ** ADAM_STEP)
        v_hat = v / (1.0 - ADAM_B2 ** ADAM_STEP)
        d_ref[...] = -ADAM_LR * (m_hat / (jnp.sqrt(v_hat) + ADAM_EPS) + ADAM_WD * w_ref[...])
        nm_ref[...] = m
        nv_ref[...] = v
        go_ref[...] = g

    spec = pl.BlockSpec((tm, cols), lambda i: (i, 0))
    body, more_specs, more = _behind(body, 4, after)
    return pl.pallas_call(
        body, name=name, grid=(rows // tm,), in_specs=[spec] * 4 + more_specs, out_specs=[spec] * 4,
        out_shape=[jax.ShapeDtypeStruct((rows, cols), F32)] * 4, compiler_params=_params(1),
    )(w, g, m, v, *more)


def _position():
    x, y, c = lax.axis_index("x"), lax.axis_index("y"), lax.axis_index("c")
    chips = [(1 - x, y), (x, 1 - y), (1 - x, 1 - y)]
    return x, y, c, chips


def _remote(src, dst, send_sems, recv_sems, k, to):
    return pltpu.make_async_remote_copy(src_ref=src, dst_ref=dst, send_sem=send_sems.at[k], recv_sem=recv_sems.at[k],
                                        device_id=to, device_id_type=MESH)


def _place_shard(w, name, after=None):
    rows, cols = w.shape
    tm = 256
    x, y = lax.axis_index("x"), lax.axis_index("y")

    def body(chip_ref, w_ref, o_ref):
        o_ref[...] = w_ref[...].astype(BF16)

    more_specs, more = ([ANY], [after]) if after is not None else ([], [])
    if after is not None:
        inner = body
        body = lambda chip_ref, w_ref, after_ref, o_ref: inner(chip_ref, w_ref, o_ref)
    return pl.pallas_call(
        body, name=name,
        grid_spec=pltpu.PrefetchScalarGridSpec(
            num_scalar_prefetch=1, grid=(rows // tm,),
            in_specs=[pl.BlockSpec((tm, cols), lambda i, chip: (i, 0))] + more_specs,
            out_specs=pl.BlockSpec((None, tm, cols), lambda i, chip: (chip[0], i, 0))),
        out_shape=jax.ShapeDtypeStruct((N_CHIPS, rows, cols), BF16),
        compiler_params=_params(1),
    )(jnp.reshape(2 * x + y, (1,)).astype(jnp.int32), w, *more)


def _to_bf16(x, name, after=None):
    tm = 256

    def body(x_ref, o_ref):
        o_ref[...] = x_ref[...].astype(BF16)

    spec = pl.BlockSpec((tm, x.shape[1]), lambda i: (i, 0))
    body, more_specs, more = _behind(body, 1, after)
    return pl.pallas_call(
        body, name=name, grid=(x.shape[0] // tm,), in_specs=[spec] + more_specs, out_specs=spec,
        out_shape=jax.ShapeDtypeStruct(x.shape, BF16), compiler_params=_params(1),
    )(x, *more)


HBM = pl.BlockSpec(memory_space=pltpu.HBM)
SEM = pl.BlockSpec(memory_space=pltpu.SEMAPHORE)
EFFECT = pltpu.SideEffectType.DATAFLOW_SIDE_EFFECTING


def _comm_call(name, body, bufs, sems_in, sems_out, after=None, token=False):
    nb, ns, no = len(bufs), len(sems_in), len(sems_out)
    n_in = nb + ns + (after is not None)

    def wrapped(*refs):
        body(refs[:nb], refs[nb:nb + ns], refs[n_in + nb:n_in + nb + no])
        if token:
            refs[-1][...] = jnp.zeros((8, 128), F32)

    outs = pl.pallas_call(
        wrapped, name=name,
        in_specs=[HBM] * nb + [SEM] * ns + ([ANY] if after is not None else []),
        out_specs=[HBM] * nb + [SEM] * no + ([pl.BlockSpec(memory_space=pltpu.VMEM)] if token else []),
        out_shape=[pltpu.HBM(b.shape, b.dtype) for b in bufs] + [pltpu.SemaphoreType.DMA((k,)) for k in sems_out]
        + ([jax.ShapeDtypeStruct((8, 128), F32)] if token else []),
        input_output_aliases={i: i for i in range(nb)},
        compiler_params=pltpu.CompilerParams(has_side_effects=EFFECT),
    )(*[pltpu.with_memory_space_constraint(b, pltpu.HBM) for b in bufs], *sems_in, *([after] if after is not None else []))
    return list(outs[:nb]), list(outs[nb:nb + no]), (outs[-1] if token else None)


RING_STAGES = {"ici_near": 2, "ici_far": 2, "d2d_near": 2, "d2d_far": 1}


def _ring_copies(buf, send_sems, recv_sems, k0, stage):
    x, y, c, _ = _position()
    hr = buf.shape[1] // 2
    qr = hr // 2
    half = lambda chip, h: buf.at[chip, pl.ds(h * hr, hr), :]
    quarter = lambda chip, h, q: buf.at[chip, pl.ds(h * hr + q * qr, qr), :]
    mine, x_chip, y_chip, far_chip = 2 * x + y, 2 * (1 - x) + y, 2 * x + (1 - y), 2 * (1 - x) + (1 - y)
    to_x, to_y, sibling = (1 - x, y, c), (x, 1 - y, c), (x, y, 1 - c)
    if stage == "ici_near":
        moves = [(half(mine, c), to_x, half(x_chip, c)), (half(mine, c), to_y, half(y_chip, c))]
    elif stage == "ici_far":
        moves = [(quarter(x_chip, c, 0), to_y, quarter(far_chip, c, 0)),
                 (quarter(y_chip, c, 1), to_x, quarter(far_chip, c, 1))]
    elif stage == "d2d_near":
        moves = [(half(x_chip, c), sibling, half(x_chip, 1 - c)), (half(y_chip, c), sibling, half(y_chip, 1 - c))]
    else:
        moves = [(half(far_chip, c), sibling, half(far_chip, 1 - c))]
    sends = [_remote(src, src, send_sems, recv_sems, k0 + i, to) for i, (src, to, _) in enumerate(moves)]
    arrivals = [_remote(got, got, send_sems, recv_sems, k0 + i, (x, y, c)) for i, (_, _, got) in enumerate(moves)]
    return sends, arrivals


def _ring_call(name, groups, actions, after=None):
    tags = list(dict.fromkeys(t for _, t, _ in actions))
    counts = {t: len(groups[t]["bufs"]) for t in tags}
    first = {t: sum(counts[u] for u in tags[:i]) for i, t in enumerate(tags)}
    waits = [(t, s) for v, t, s in actions if v == "wait"]
    starts = [(t, s) for v, t, s in actions if v == "start"]

    def body(bufs, sems_in, sems_out):
        for verb, t, s in actions:
            at, sems = (starts.index((t, s)), sems_out) if verb == "start" else (waits.index((t, s)), sems_in)
            for w in range(counts[t]):
                sends, arrivals = _ring_copies(bufs[first[t] + w], sems[2 * at], sems[2 * at + 1], RING_STAGES[s] * w, s)
                if verb == "start":
                    for cp in sends:
                        cp.start()
                else:
                    for cp in arrivals:
                        cp.wait_recv()
                    for cp in sends:
                        cp.wait_send()

    bufs, sems, token = _comm_call(
        name, body, [b for t in tags for b in groups[t]["bufs"]],
        [sem for t, s in waits for sem in groups[t]["sems"][s]],
        [RING_STAGES[s] * counts[t] for t, s in starts for _ in (0, 1)], after, token=True)
    for t in tags:
        groups[t]["bufs"] = bufs[first[t]:first[t] + counts[t]]
    for t, s in waits:
        del groups[t]["sems"][s]
    for i, (t, s) in enumerate(starts):
        groups[t]["sems"][s] = (sems[2 * i], sems[2 * i + 1])
    return token


def _cx_copies(src, dst, send_sems, recv_sems, k0):
    x, y, c, chips = _position()
    sends = [_remote(src.at[2 * cx + cy], dst.at[2 * x + y], send_sems, recv_sems, k0 + j, (cx, cy, c))
             for j, (cx, cy) in enumerate(chips)]
    arrivals = [_remote(dst.at[2 * cx + cy], dst.at[2 * cx + cy], send_sems, recv_sems, k0 + j, (x, y, c))
                for j, (cx, cy) in enumerate(chips)]
    return sends, arrivals


def _cx_start(name, pair_sums):
    n = len(pair_sums)
    landing = [lax.empty(p.shape, p.dtype) for p in pair_sums]

    def body(bufs, _, sems):
        for w in range(n):
            for cp in _cx_copies(bufs[w], bufs[n + w], sems[0], sems[1], 3 * w)[0]:
                cp.start()

    bufs, sems, token = _comm_call(name, body, list(pair_sums) + landing, [], [3 * n, 3 * n], token=True)
    return (bufs, sems), token


def _cx_wait(name, state, after):
    bufs, sems = state
    n = len(bufs) // 2

    def body(refs, sems_in, _):
        for w in range(n):
            sends, arrivals = _cx_copies(refs[w], refs[n + w], sems_in[0], sems_in[1], 3 * w)
            for cp in arrivals:
                cp.wait_recv()
            for cp in sends:
                cp.wait_send()

    bufs, _, _ = _comm_call(name, body, bufs, sems, [], after)
    return bufs[:n], bufs[n:]


def _px_copies(src, dst, send_sems, recv_sems, k):
    x, y, c, _ = _position()
    hr = src.shape[1] // 2
    send = _remote(src.at[:, pl.ds((1 - c) * hr, hr), :], dst, send_sems, recv_sems, k, (x, y, 1 - c))
    arrival = _remote(dst, dst, send_sems, recv_sems, k, (x, y, c))
    return send, arrival


def _px_start(name, grads):
    n = len(grads)
    landing = [lax.empty((N_CHIPS, g.shape[1] // 2, g.shape[2]), g.dtype) for g in grads]

    def body(bufs, _, sems):
        for w in range(n):
            _px_copies(bufs[w], bufs[n + w], sems[0], sems[1], w)[0].start()

    bufs, sems, token = _comm_call(name, body, list(grads) + landing, [], [n, n], token=True)
    return (bufs, sems), token


def _px_wait(name, state, after):
    bufs, sems = state
    n = len(bufs) // 2

    def body(refs, sems_in, _):
        for w in range(n):
            send, arrival = _px_copies(refs[w], refs[n + w], sems_in[0], sems_in[1], w)
            arrival.wait_recv()
            send.wait_send()

    bufs, _, _ = _comm_call(name, body, bufs, sems, [], after)
    return bufs[:n], bufs[n:]


def _pair_sum(grad, got, name):
    _, rows, cols = grad.shape
    hr = rows // 2
    tm = min(hr, 512)
    nb = hr // tm
    c = lax.axis_index("c")

    def body(c_ref, g_ref, o_ref, out_ref):
        out_ref[...] = (g_ref[...].astype(F32) + o_ref[...].astype(F32)).astype(BF16)

    return pl.pallas_call(
        body, name=name,
        grid_spec=pltpu.PrefetchScalarGridSpec(
            num_scalar_prefetch=1, grid=(N_CHIPS, nb),
            in_specs=[pl.BlockSpec((None, tm, cols), lambda s, i, c_ref: (s, c_ref[0] * nb + i, 0)),
                      pl.BlockSpec((None, tm, cols), lambda s, i, c_ref: (s, i, 0))],
            out_specs=pl.BlockSpec((None, tm, cols), lambda s, i, c_ref: (s, i, 0))),
        out_shape=jax.ShapeDtypeStruct((N_CHIPS, hr, cols), BF16),
        compiler_params=_params(2),
    )(jnp.reshape(c, (1,)).astype(jnp.int32), grad, got)


def _chip_sum(parts, pair_sums, name):
    _, hr, cols = parts.shape
    tm = min(hr, 512)
    nb = hr // tm
    x, y, c = lax.axis_index("x"), lax.axis_index("y"), lax.axis_index("c")

    def body(pos_ref, p_ref, own_ref, o_ref):
        chip = pos_ref[0]
        own = own_ref[...].astype(F32)
        term = lambda s: jnp.where(chip == s, own, p_ref[s].astype(F32))
        o_ref[...] = ((term(0) + term(1)) + term(2)) + term(3)

    return pl.pallas_call(
        body, name=name,
        grid_spec=pltpu.PrefetchScalarGridSpec(
            num_scalar_prefetch=1, grid=(nb,),
            in_specs=[pl.BlockSpec((N_CHIPS, tm, cols), lambda i, pos: (0, i, 0)),
                      pl.BlockSpec((None, tm, cols), lambda i, pos: (pos[0], i, 0))],
            out_specs=pl.BlockSpec((tm, cols), lambda i, pos: (pos[1] * nb + i, 0))),
        out_shape=jax.ShapeDtypeStruct((2 * hr, cols), F32), compiler_params=_params(1),
    )(jnp.stack([2 * x + y, c]).astype(jnp.int32), parts, pair_sums)


def _share_copies(buf, send_sems, recv_sems, k):
    x, y, c, _ = _position()
    hr = buf.shape[0] // 2
    mine, theirs = buf.at[pl.ds(c * hr, hr), :], buf.at[pl.ds((1 - c) * hr, hr), :]
    return (_remote(mine, mine, send_sems, recv_sems, k, (x, y, 1 - c)),
            _remote(theirs, theirs, send_sems, recv_sems, k, (x, y, c)))


def _share_start(name, bufs):
    n = len(bufs)

    def body(refs, _, sems):
        for w in range(n):
            _share_copies(refs[w], sems[0], sems[1], w)[0].start()

    bufs, sems, token = _comm_call(name, body, list(bufs), [], [n, n], token=True)
    return (bufs, sems), token


def _share_wait(name, state, after):
    bufs, sems = state

    def body(refs, sems_in, _):
        for w in range(len(bufs)):
            send, arrival = _share_copies(refs[w], sems_in[0], sems_in[1], w)
            arrival.wait_recv()
            send.wait_send()

    return _comm_call(name, body, bufs, sems, [], after)[0]


def _allreduce_small(g):
    rows = g.shape[0]
    half = rows // 2

    def body(g_ref, o_ref, sib, slots, send_sems, recv_sems):
        x, y, c, chips = _position()
        me, sibling = (x, y, c), (x, y, 1 - c)
        my_chip = 2 * x + y
        mine = pl.ds(pl.multiple_of(c * half, 8), half)
        theirs = pl.ds(pl.multiple_of((1 - c) * half, 8), half)
        pair = _remote(g_ref.at[theirs], sib, send_sems, recv_sems, 0, sibling)
        pair.start()
        pair.wait()
        slots[my_chip] = g_ref[mine, :] + sib[...]
        sent = []
        for j, (cx, cy) in enumerate(chips):
            cp = _remote(slots.at[my_chip], slots.at[my_chip], send_sems, recv_sems, 1 + j, (cx, cy, c))
            cp.start()
            sent.append(cp)
        for j, (cx, cy) in enumerate(chips):
            got = slots.at[2 * cx + cy]
            _remote(got, got, send_sems, recv_sems, 1 + j, me).wait_recv()
        for cp in sent:
            cp.wait_send()
        o_ref[mine, :] = ((slots[0] + slots[1]) + slots[2]) + slots[3]
        swap = _remote(o_ref.at[mine], o_ref.at[mine], send_sems, recv_sems, 4, sibling)
        swap.start()
        swap.wait()

    vm = pl.BlockSpec(memory_space=pltpu.VMEM)
    return pl.pallas_call(
        body, name="allreduce_small",
        in_specs=[vm], out_specs=vm, out_shape=jax.ShapeDtypeStruct((rows, 128), F32),
        scratch_shapes=[pltpu.VMEM((half, 128), F32), pltpu.VMEM((N_CHIPS, half, 128), F32),
                        pltpu.SemaphoreType.DMA((5,)), pltpu.SemaphoreType.DMA((5,))],
        compiler_params=pltpu.CompilerParams(vmem_limit_bytes=VMEM_LIMIT),
    )(g)


_SMALL =("rel_bias", "ln_v_gain", "ln_v_bias", "w_spatial", "b_spatial", "ln1_gain", "ln1_bias",
          "b_ff1", "b_ff2", "ln2_gain", "ln2_bias")
_SMALL_ROWS = 1200
_LOSS_AT = (152832 // 128, 0)


def _pack_small(parts):
    flat = jnp.concatenate([parts[k].reshape(-1).astype(F32) for k in _SMALL])
    flat = jnp.pad(flat, (0, _SMALL_ROWS * 128 - flat.shape[0]))
    return flat.reshape(_SMALL_ROWS, 128)


def _unpack_small(packed, like):
    flat = packed.reshape(-1)
    out, at = {}, 0
    for k in _SMALL:
        n = math.prod(like[k].shape)
        out[k] = flat[at:at + n].reshape(like[k].shape)
        at += n
    return out


def kernel(x, w_in, rel_bias, ln_v_gain, ln_v_bias, w_spatial, b_spatial, w_proj_a, w_proj_b, w_out, ln1_gain, ln1_bias, w_ff1, b_ff1, w_ff2, b_ff2, ln2_gain, ln2_bias, loss_target, m_w_in, m_rel_bias, m_ln_v_gain, m_ln_v_bias, m_w_spatial, m_b_spatial, m_w_proj_a, m_w_proj_b, m_w_out, m_ln1_gain, m_ln1_bias, m_w_ff1, m_b_ff1, m_w_ff2, m_b_ff2, m_ln2_gain, m_ln2_bias, v_w_in, v_rel_bias, v_ln_v_gain, v_ln_v_bias, v_w_spatial, v_b_spatial, v_w_proj_a, v_w_proj_b, v_w_out, v_ln1_gain, v_ln1_bias, v_w_ff1, v_b_ff1, v_w_ff2, v_b_ff2, v_ln2_gain, v_ln2_bias):
    args = dict(locals())
    big = ("w_in", "w_proj_a", "w_proj_b", "w_out", "w_ff1", "w_ff2")
    weights = ("w_in", "rel_bias", "ln_v_gain", "ln_v_bias", "w_spatial", "b_spatial", "w_proj_a", "w_proj_b", "w_out",
               "ln1_gain", "ln1_bias", "w_ff1", "b_ff1", "w_ff2", "b_ff2", "ln2_gain", "ln2_bias")

    xs = x[0]
    target = loss_target[0]

    ring = {"a": {"bufs": [_place_shard(w_in[0], "place_w_in")], "sems": {}}}
    tok = _ring_call("allgather_a_near", ring, [("start", "a", "ici_near")])
    placed = [_place_shard(args[k][0], f"place_{k}", after=tok) for k in big[1:]]
    for tag, bufs in (("b", placed[0:3]), ("c", placed[3:4]), ("d", placed[4:5])):
        ring[tag] = {"bufs": bufs, "sems": {}}
    xb = _to_bf16(xs, "x_to_bf16", after=placed[4])

    mx, my = lax.axis_index("x"), lax.axis_index("y")
    own = jnp.reshape(2 * mx + my, (1,)).astype(jnp.int32)
    near = jnp.stack([2 * (1 - mx) + my, 2 * mx + (1 - my)]).astype(jnp.int32)
    far = jnp.reshape(2 * (1 - mx) + (1 - my), (1,)).astype(jnp.int32)
    proj = _proj(xb, ring["a"]["bufs"][0], own, "proj_own")
    _ring_call("allgather_a_far", ring, [("wait", "a", "ici_near"), ("start", "a", "ici_far"), ("start", "a", "d2d_near"),
                                         ("start", "b", "ici_near"), ("start", "c", "ici_near")], after=proj)
    _ring_call("allgather_a_near_done", ring, [("wait", "a", "d2d_near")])
    proj = _proj(xb, ring["a"]["bufs"][0], near, "proj_near", into=proj)
    _ring_call("allgather_a_last", ring, [("wait", "a", "ici_far"), ("start", "a", "d2d_far")], after=proj)
    _ring_call("allgather_a_done", ring, [("wait", "a", "d2d_far")])
    (win_g,) = ring["a"]["bufs"]
    proj = _proj(xb, win_g, far, "proj_far", into=proj)
    _ring_call("allgather_b_far", ring, [("wait", "b", "ici_near"), ("start", "b", "ici_far"), ("start", "b", "d2d_near")],
               after=proj)
    ws = w_spatial[0]
    ws_t = jnp.transpose(ws, (0, 2, 1))
    bsp_b = jnp.broadcast_to(b_spatial[0][:, :, None], (NH, 128, 128))
    gmlp = _gmlp_fwd(proj, ws, bsp_b, ln_v_gain, ln_v_bias)
    bias = _bias_tiles(rel_bias)
    attn, lse = _attention_fwd(proj, bias, after=gmlp)
    _ring_call("allgather_b_last_c_far", ring,
               [("wait", "b", "ici_far"), ("start", "b", "d2d_far"),
                ("wait", "c", "ici_near"), ("start", "c", "ici_far"), ("start", "c", "d2d_near"),
                ("start", "d", "ici_near")], after=attn)
    _ring_call("allgather_b_done", ring, [("wait", "b", "d2d_near"), ("wait", "b", "d2d_far")])
    wpa_g, wpb_g, wout_g = ring["b"]["bufs"]
    wout_full = wout_g.reshape(D, D)
    ya, yb, merged = _branch(attn, gmlp, wpa_g, wpb_g, proj)
    xhat1, rstd1, h1b = _out_ln1(merged, wout_full, xs, ln1_gain, ln1_bias)
    _ring_call("allgather_c_last", ring, [("wait", "c", "ici_far"), ("start", "c", "d2d_far")], after=h1b)
    _ring_call("allgather_c_done", ring, [("wait", "c", "d2d_near"), ("wait", "c", "d2d_far")])
    (w1_g,) = ring["c"]["bufs"]
    a, r = _ff1(h1b, w1_g, b_ff1, 0, "ff1_first")
    tok = _ring_call("allgather_d_far", ring,
                     [("wait", "d", "ici_near"), ("start", "d", "ici_far"), ("start", "d", "d2d_near")], after=a)
    a, r = _ff1(h1b, w1_g, b_ff1, 1, "ff1_second", into=(a, r), after=tok)
    _ring_call("allgather_d_last", ring, [("wait", "d", "ici_far"), ("start", "d", "d2d_far")], after=a)
    _ring_call("allgather_d_done", ring, [("wait", "d", "d2d_near"), ("wait", "d", "d2d_far")])
    (w2_g,) = ring["d"]["bufs"]
    w2_full = w2_g.reshape(DFF, D)
    dpre2, dpre2b, st2 = _ff2_ln2_loss(a, w2_full, xhat1, ln1_gain, ln1_bias, b_ff2, ln2_gain, ln2_bias, target)

    def pair_and_chip(tag, state, after):
        local, from_sibling = _px_wait(f"pair_exchange_wait_{tag}", state, after)
        pair_sums = [_pair_sum(g, o, f"pair_sum_{tag}_{i}") for i, (g, o) in enumerate(zip(local, from_sibling))]
        return _cx_start(f"chip_exchange_start_{tag}", pair_sums)

    g_w2 = _grad_w(a, dpre2b, "grad_w_ff2", 512, 2048, False)
    px, tok = _px_start("pair_exchange_start_w_ff2", [g_w2.reshape(N_CHIPS, DFF // N_CHIPS, D)])
    dprea, g_b1 = _d_ff1(dpre2b, w2_full, r, after=tok)
    cx_w2, tok = pair_and_chip("w_ff2", px, dprea)
    g_w1 = _grad_w(h1b, dprea, "grad_w_ff1", 512, 2048, True, after=tok)
    px, tok = _px_start("pair_exchange_start_w_ff1", [g_w1])
    dpre1, dpre1b, st1 = _d_h1_ln1(dprea, w1_g, dpre2, xhat1, rstd1, ln1_gain, after=tok)
    cx_w1, tok = pair_and_chip("w_ff1", px, dpre1b)
    g_wout = _grad_w(merged, dpre1b, "grad_w_out", 512, 2048, False, after=tok)
    dya, dyb, dga, dgb = _d_merged(dpre1b, wout_full, proj, ya, yb)
    g_wpa = _grad_w(attn, dya, "grad_w_proj_a", 1024, 512, True)
    g_wpb = _grad_w(gmlp, dyb, "grad_w_proj_b", 1024, 512, True)
    px, tok = _px_start("pair_exchange_start_b", [g_wpa, g_wpb, g_wout.reshape(N_CHIPS, D // N_CHIPS, D)])
    dattn, dgmlp = _d_branches(dya, dyb, wpa_g, wpb_g, after=tok)
    duv, g_ws, g_bs, stv = _gmlp_bwd(proj, dgmlp, ws, ws_t, bsp_b, ln_v_gain, ln_v_bias)
    cx_b, tok = pair_and_chip("b", px, duv)
    dq, dk, dv, ds_sums = _attention_bwd(proj, dattn, attn, lse, bias, after=tok)
    g_rb = _rel_bias_grad(ds_sums)[:, :NH]

    small_g = dict(rel_bias=g_rb, ln_v_gain=stv[0], ln_v_bias=stv[1], w_spatial=g_ws, b_spatial=g_bs[:, :, 0],
                   ln1_gain=st1[0], ln1_bias=st1[1], b_ff1=g_b1, b_ff2=st2[2], ln2_gain=st2[0], ln2_bias=st2[1])
    gs = _allreduce_small(_pack_small(small_g).at[_LOSS_AT].set(st2[3, 0]))
    ds_, ms_, vs_, _ = _adamw(_pack_small({k: args[k] for k in _SMALL}), gs,
                           _pack_small({k: args["m_" + k] for k in _SMALL}),
                           _pack_small({k: args["v_" + k] for k in _SMALL}), "adamw_small")
    like = {k: args[k] for k in _SMALL}
    grads, deltas, new_m, new_v = (_unpack_small(t, like) for t in (gs, ds_, ms_, vs_))

    dproj = jnp.concatenate([dq, dk, dv, duv, dga, dgb], axis=1)
    g_win = _grad_w(xb, dproj, "grad_w_in", 512, 2304, True, after=gs)
    px, tok = _px_start("pair_exchange_start_w_in", [g_win])

    def chip_sums(tag, state, names, after):
        pair_sums, from_chips = _cx_wait(f"chip_exchange_wait_{tag}", state, after)
        halves = [_chip_sum(p, own, f"chip_sum_{k}") for p, own, k in zip(from_chips, pair_sums, names)]
        return _share_start(f"share_start_{tag}", halves)

    def adam_one(k, g, after=None):
        d_, m_, v_, g_ = _adamw(args[k][0], g, args["m_" + k][0], args["v_" + k][0], f"adamw_{k}", after=after)
        grads[k], deltas[k], new_m[k], new_v[k] = g_[None], d_[None], m_[None], v_[None]
        return d_

    def adam(tag, state, names, after):
        last = None
        for k, g in zip(names, _share_wait(f"share_wait_{tag}", state, after)):
            last = adam_one(k, g)
        return last

    sh_w2, tok = chip_sums("w_ff2", cx_w2, ["w_ff2"], tok)
    sh_w1, tok = chip_sums("w_ff1", cx_w1, ["w_ff1"], tok)
    sh_b, tok = chip_sums("b", cx_b, ["w_proj_a", "w_proj_b", "w_out"], tok)
    cx_in, tok = pair_and_chip("w_in", px, tok)
    grad_x = _d_x(dproj, win_g, dpre1, after=tok)
    done = adam("w_ff2", sh_w2, ["w_ff2"], grad_x)
    done = adam("w_ff1", sh_w1, ["w_ff1"], done)
    g_wpa_full, g_wpb_full, g_wout_full = _share_wait("share_wait_b", sh_b, done)
    done = adam_one("w_out", g_wout_full)
    sh_in, tok = chip_sums("w_in", cx_in, ["w_in"], done)
    done = adam_one("w_proj_a", g_wpa_full, after=tok)
    done = adam_one("w_proj_b", g_wpb_full, after=done)
    adam("w_in", sh_in, ["w_in"], done)

    loss = gs[_LOSS_AT] * (0.5 / D)
    return (loss, grad_x[None], *[grads[k] for k in weights], *[deltas[k] for k in weights],
            *[new_m[k] for k in weights], *[new_v[k] for k in weights])
```

```python
import math

import numpy as np
import jax
import jax.numpy as jnp
from jax import lax
from jax.experimental import pallas as pl
from jax.experimental.pallas import tpu as pltpu

F32 = jnp.float32
BF16 = jnp.bfloat16

S = 2048
D = 2048
DA = 1024
DB = 1024
DFF = 8192
DIN = 9216
NH = 8
HD = 128
NBLK = 16
PATTERNS = ((128, 1), (512, 4), (2048, 16))
N_BUCKETS = 32
MAX_DISTANCE = 2048
ALPHA = 2.0 ** 0.25
LN_EPS = 1e-5
NEG_INF = -1e30
SCALE = HD ** -0.5
N_CHIPS = 4

ADAM_LR = 0.001
ADAM_B1 = 0.9
ADAM_B2 = 0.999
ADAM_EPS = 1e-08
ADAM_WD = 0.01
ADAM_STEP = 10

VMEM_LIMIT = 56 * 1024 * 1024
MESH = pl.DeviceIdType.MESH
ANY = pl.BlockSpec(memory_space=pl.ANY)


def _params(n_axes, vmem=VMEM_LIMIT):
    return pltpu.CompilerParams(dimension_semantics=("arbitrary",) * n_axes, vmem_limit_bytes=vmem)


def _bucket_tile(dilation):
    qi = np.arange(128)[:, None]
    kj = np.arange(256)[None, :]
    n = np.clip(128 + qi - kj, 0, 128) * dilation
    max_exact = N_BUCKETS // 2
    nf = np.maximum(n, 1).astype(np.float32)
    large = max_exact + (np.log(nf / np.float32(max_exact)) / np.float32(math.log(MAX_DISTANCE / max_exact))
                         * np.float32(N_BUCKETS - max_exact)).astype(np.int32)
    large = np.minimum(large, N_BUCKETS - 1)
    return np.where(n < max_exact, n, large).astype(np.int32)


def _gelu(x):
    c = math.sqrt(2.0 / math.pi)
    t = jnp.tanh(c * (x + 0.044715 * x * x * x))
    return 0.5 * x * (1.0 + t), t


def _gelu_grad(x, t):
    c = math.sqrt(2.0 / math.pi)
    return 0.5 * (1.0 + t) + 0.5 * x * (1.0 - t * t) * c * (1.0 + 3.0 * 0.044715 * x * x)


def _sigmoid(x):
    return 1.0 / (1.0 + jnp.exp(-x))


def _dot(a, b):
    return jnp.dot(a, b, preferred_element_type=F32)


def _behind(body, n_in, after):
    if after is None:
        return body, [], []
    return (lambda *refs: body(*refs[:n_in], *refs[n_in + 1:])), [ANY], [after]


def _dot_nt(a, b):
    return lax.dot_general(a, b, (((1,), (1,)), ((), ())), preferred_element_type=F32)


def _proj(xb, win_g, shards, name, into=None):
    tn = 768
    per = 2304 // tn

    def body(shards_ref, x_ref, w_ref, *rest):
        rest[-1][...] = _dot(x_ref[...], w_ref[...])

    in_specs = [pl.BlockSpec((S, D), lambda j, sh: (0, 0)),
                pl.BlockSpec((None, D, tn), lambda j, sh: (sh[j // per], 0, j % per))]
    return pl.pallas_call(
        body, name=name,
        grid_spec=pltpu.PrefetchScalarGridSpec(
            num_scalar_prefetch=1, grid=(shards.shape[0] * per,),
            in_specs=in_specs + ([ANY] if into is not None else []),
            out_specs=pl.BlockSpec((S, tn), lambda j, sh: (0, sh[j // per] * per + j % per))),
        out_shape=jax.ShapeDtypeStruct((S, DIN), F32),
        input_output_aliases={3: 0} if into is not None else {},
        compiler_params=_params(1),
    )(shards, xb, win_g, *([into] if into is not None else []))


FWD_HEADS_PER_STEP = 4
BWD_HEADS_PER_STEP = 2


def _bias_tiles(rel_bias, after=None):
    buckets = jnp.asarray(np.stack([_bucket_tile(d) for _, d in PATTERNS]))

    def body(rb_ref, bk_ref, o_ref):
        qi = lax.broadcasted_iota(jnp.int32, (128, 256), 0)
        kj = lax.broadcasted_iota(jnp.int32, (128, 256), 1)
        steps = 128 + qi - kj
        band = (steps >= 0) & (steps <= 128)
        o_ref[...] = jnp.zeros_like(o_ref)
        for p in range(len(PATTERNS)):
            bucket = bk_ref[p]

            def one_bucket(t, carry):
                hit = bucket == t
                for h in range(NH):
                    o_ref[p, h] = jnp.where(hit, rb_ref[t, h], o_ref[p, h])
                return carry

            lax.fori_loop(0, N_BUCKETS, one_bucket, 0)
            for h in range(NH):
                o_ref[p, h] = jnp.where(band, o_ref[p, h], NEG_INF)

    body, more_specs, more = _behind(body, 2, after)
    return pl.pallas_call(
        body, name="bias_tiles",
        in_specs=[pl.BlockSpec(memory_space=pltpu.SMEM), pl.BlockSpec(memory_space=pltpu.VMEM)] + more_specs,
        out_specs=pl.BlockSpec(memory_space=pltpu.VMEM),
        out_shape=jax.ShapeDtypeStruct((len(PATTERNS), NH, 128, 256), F32),
        compiler_params=pltpu.CompilerParams(vmem_limit_bytes=VMEM_LIMIT),
    )(rel_bias, buckets, *more)


def _block_rows(b, dilation):
    nblk = NBLK // dilation
    r, n = b // nblk, b % nblk
    start = r + n * (128 * dilation)
    prev_start = jnp.maximum(start - 128 * dilation, r)
    if dilation == 1:
        return pl.ds(pl.multiple_of(start, 128), 128), pl.ds(pl.multiple_of(prev_start, 128), 128), n > 0
    return pl.ds(start, 128, stride=dilation), pl.ds(prev_start, 128, stride=dilation), n > 0


def _head_specs(first, hps):
    return [pl.BlockSpec((S, HD), lambda g, j=j: (0, first + g * hps + j)) for j in range(hps)]


def _bias_spec(hps):
    return pl.BlockSpec((len(PATTERNS), hps, 128, 256), lambda g: (0, g, 0, 0))


def _heads_spec(hps):
    return pl.BlockSpec((S, hps * HD), lambda g: (0, g))


def _attention_fwd(proj, bias):
    hps = FWD_HEADS_PER_STEP

    def body(bias_ref, *refs):
        q_refs, k_refs, v_refs = (refs[i * hps:(i + 1) * hps] for i in range(3))
        o_ref, lse_ref = refs[3 * hps:3 * hps + 2]
        acc_scrs, m_scrs, l_scrs = (refs[3 * hps + 2 + i * hps:3 * hps + 2 + (i + 1) * hps] for i in range(3))
        kj = lax.broadcasted_iota(jnp.int32, (128, 256), 1)
        for p, (_, d) in enumerate(PATTERNS):
            prev_blocks = NBLK // d > 1

            def block(b, carry):
                units = [(j,) + _block_rows(blk, d) for blk in (b, b + NBLK // 2) for j in range(hps)]
                scores = []
                for j, rows, prows, _ in units:
                    q = q_refs[j][rows, :].astype(BF16)
                    cur = _dot_nt(q, k_refs[j][rows, :].astype(BF16))
                    if prev_blocks:
                        cur = jnp.concatenate([_dot_nt(q, k_refs[j][prows, :].astype(BF16)), cur], axis=1)
                    scores.append(cur)
                soft = []
                for u, (j, _, _, has_prev) in enumerate(units):
                    if prev_blocks:
                        s = jnp.where((kj >= 128) | has_prev, scores[u] * SCALE + bias_ref[p, j], NEG_INF)
                    else:
                        s = scores[u] * SCALE + bias_ref[p, j, :, 128:256]
                    m = jnp.max(s, axis=1, keepdims=True)
                    e = jnp.exp(s - m)
                    soft.append((m, jnp.sum(e, axis=1, keepdims=True), e.astype(BF16)))
                outs = []
                for u, (j, rows, prows, _) in enumerate(units):
                    e = soft[u][2]
                    if prev_blocks:
                        outs.append(_dot(e[:, :128], v_refs[j][prows, :].astype(BF16))
                                    + _dot(e[:, 128:], v_refs[j][rows, :].astype(BF16)))
                    else:
                        outs.append(_dot(e, v_refs[j][rows, :].astype(BF16)))
                for u, (j, rows, _, _) in enumerate(units):
                    acc_scr, m_scr, l_scr = acc_scrs[j], m_scrs[j], l_scrs[j]
                    (m, den, _), o = soft[u], outs[u]
                    if p == 0:
                        acc_scr[rows, :] = o
                        m_scr[rows, :] = jnp.broadcast_to(m, (128, HD))
                        l_scr[rows, :] = jnp.broadcast_to(den, (128, HD))
                    else:
                        m_old = m_scr[rows, :]
                        m_new = jnp.maximum(m_old, m)
                        w_old, w_new = jnp.exp(m_old - m_new), jnp.exp(m - m_new)
                        acc_scr[rows, :] = acc_scr[rows, :] * w_old + o * w_new
                        l_scr[rows, :] = l_scr[rows, :] * w_old + den * w_new
                        m_scr[rows, :] = m_new
                return carry

            lax.fori_loop(0, NBLK // 2, block, 0)
        for j in range(hps):
            cols = slice(j * HD, (j + 1) * HD)
            den = l_scrs[j][...]
            o_ref[:, cols] = (acc_scrs[j][...] / den).astype(BF16)
            lse_ref[:, cols] = m_scrs[j][...] + jnp.log(den)

    return pl.pallas_call(
        body, name="attention_fwd", grid=(NH // hps,),
        in_specs=[_bias_spec(hps)] + _head_specs(0, hps) + _head_specs(NH, hps) + _head_specs(2 * NH, hps),
        out_specs=[_heads_spec(hps), _heads_spec(hps)],
        out_shape=[jax.ShapeDtypeStruct((S, DA), BF16), jax.ShapeDtypeStruct((S, DA), F32)],
        scratch_shapes=[pltpu.VMEM((S, HD), F32)] * (3 * hps),
        compiler_params=_params(1),
    )(bias, *([proj] * (3 * hps)))


def _attention_bwd(proj, dattn, attn, lse, bias, after=None):
    hps = BWD_HEADS_PER_STEP

    def body(bias_ref, *refs):
        q_refs, k_refs, v_refs, do_refs, o_refs, lse_refs = (refs[i * hps:(i + 1) * hps] for i in range(6))
        dq_ref, dk_ref, dv_ref, ds_ref = refs[6 * hps:6 * hps + 4]
        dl_scrs, dq_scrs, dk_scrs, dv_scrs = (refs[6 * hps + 4 + i * hps:6 * hps + 4 + (i + 1) * hps] for i in range(4))
        ds_ref[...] = jnp.zeros_like(ds_ref)
        for j in range(hps):
            dq_scrs[j][...] = jnp.zeros((S, HD), F32)
            dk_scrs[j][...] = jnp.zeros((S, HD), F32)
            dv_scrs[j][...] = jnp.zeros((S, HD), F32)
            prod = do_refs[j][...] * o_refs[j][...].astype(F32)
            dl_scrs[j][...] = jnp.broadcast_to(jnp.sum(prod, axis=1, keepdims=True), (S, HD))
        for p, (_, d) in enumerate(PATTERNS):
            prev_blocks = NBLK // d > 1

            def block(b, carry):
                units = [(j,) + _block_rows(b + i * (NBLK // 4), d) for i in range(4) for j in range(hps)]
                ops, raw = [], []
                for j, rows, prows, _ in units:
                    q, do = q_refs[j][rows, :].astype(BF16), do_refs[j][rows, :].astype(BF16)
                    kc, vc = k_refs[j][rows, :].astype(BF16), v_refs[j][rows, :].astype(BF16)
                    if prev_blocks:
                        kp, vp = k_refs[j][prows, :].astype(BF16), v_refs[j][prows, :].astype(BF16)
                        ops.append((q, do, kc, kp))
                        raw.append((_dot_nt(q, kc), _dot_nt(do, vc), _dot_nt(q, kp), _dot_nt(do, vp)))
                    else:
                        ops.append((q, do, kc))
                        raw.append((_dot_nt(q, kc), _dot_nt(do, vc)))
                probs = []
                for u, (j, rows, _, has_prev) in enumerate(units):
                    lse_b, dl_b = lse_refs[j][rows, :], dl_scrs[j][rows, :]
                    p_c = jnp.exp(raw[u][0] * SCALE + bias_ref[p, j, :, 128:256] - lse_b)
                    ds_c = p_c * (raw[u][1] - dl_b)
                    ds_ref[p, j, :, 128:256] += ds_c
                    if prev_blocks:
                        p_p = jnp.where(has_prev, jnp.exp(raw[u][2] * SCALE + bias_ref[p, j, :, 0:128] - lse_b), 0.0)
                        ds_p = p_p * (raw[u][3] - dl_b)
                        ds_ref[p, j, :, 0:128] += ds_p
                        probs.append((p_c, ds_c, p_p, ds_p))
                    else:
                        probs.append((p_c, ds_c))
                grads = []
                for u in range(len(units)):
                    q, do, kc = ops[u][:3]
                    p_c, ds_c = probs[u][:2]
                    dq = _dot(ds_c.astype(BF16), kc)
                    cur = (_dot(ds_c.T.astype(BF16), q) * SCALE, _dot(p_c.T.astype(BF16), do))
                    if prev_blocks:
                        p_p, ds_p = probs[u][2:]
                        dq = dq + _dot(ds_p.astype(BF16), ops[u][3])
                        cur = cur + (_dot(ds_p.T.astype(BF16), q) * SCALE, _dot(p_p.T.astype(BF16), do))
                    grads.append((dq * SCALE,) + cur)
                for u, (j, rows, prows, _) in enumerate(units):
                    dq_scrs[j][rows, :] += grads[u][0]
                    dk_scrs[j][rows, :] += grads[u][1]
                    dv_scrs[j][rows, :] += grads[u][2]
                    if prev_blocks:
                        dk_scrs[j][prows, :] += grads[u][3]
                        dv_scrs[j][prows, :] += grads[u][4]
                return carry

            lax.fori_loop(0, NBLK // 4, block, 0)
        for j in range(hps):
            cols = slice(j * HD, (j + 1) * HD)
            dq_ref[:, cols] = dq_scrs[j][...].astype(BF16)
            dk_ref[:, cols] = dk_scrs[j][...].astype(BF16)
            dv_ref[:, cols] = dv_scrs[j][...].astype(BF16)

    body, more_specs, more = _behind(body, 1 + 6 * hps, after)
    return pl.pallas_call(
        body, name="attention_bwd", grid=(NH // hps,),
        in_specs=[_bias_spec(hps)]
        + _head_specs(0, hps) + _head_specs(NH, hps) + _head_specs(2 * NH, hps) + 3 * _head_specs(0, hps)
        + more_specs,
        out_specs=3 * [_heads_spec(hps)] + [pl.BlockSpec((3, hps, 128, 256), lambda g: (0, g, 0, 0))],
        out_shape=[jax.ShapeDtypeStruct((S, DA), BF16)] * 3 + [jax.ShapeDtypeStruct((3, NH, 128, 256), F32)],
        scratch_shapes=[pltpu.VMEM((S, HD), F32)] * (4 * hps),
        compiler_params=_params(1),
    )(bias, *([proj] * (3 * hps)), *([dattn] * hps), *([attn] * hps), *([lse] * hps), *more)


def _gmlp_parts(u_ref, vb_ref, g_ref, be_ref):
    u = u_ref[...]
    u_act, tu = _gelu(u)
    vb = vb_ref[...]
    gv, tv = _gelu(vb)
    mean = jnp.mean(gv, axis=1, keepdims=True)
    cen = gv - mean
    var = jnp.mean(cen * cen, axis=1, keepdims=True)
    rstd = lax.rsqrt(var + LN_EPS)
    xhat = cen * rstd
    vn = xhat * g_ref[...] + be_ref[...]
    return u, tu, u_act, vb, tv, rstd, xhat, vn


def _gmlp_fwd(proj, ws, bsp_b, gain_v, bias_v):
    def body(u_ref, vb_ref, ws_ref, bsp_ref, g_ref, be_ref, o_ref):
        _, _, u_act, _, _, _, _, vn = _gmlp_parts(u_ref, vb_ref, g_ref, be_ref)
        row = lax.broadcasted_iota(jnp.int32, (128, 128), 0)
        col = lax.broadcasted_iota(jnp.int32, (128, 128), 1)
        causal = row >= col
        for g in range(NH):
            cols = slice(g * 128, (g + 1) * 128)
            wsg = jnp.where(causal, ws_ref[g], 0.0).astype(BF16)
            z = _dot(wsg, vn[:, cols].astype(BF16)) + bsp_ref[g]
            o_ref[:, cols] = (u_act[:, cols] * z).astype(BF16)

    return pl.pallas_call(
        body, name="gmlp_fwd", grid=(NBLK,),
        in_specs=[pl.BlockSpec((128, DB), lambda c: (c, 3)), pl.BlockSpec((128, DB), lambda c: (c, 4)),
                  pl.BlockSpec((NH, 128, 128), lambda c: (0, 0, 0)), pl.BlockSpec((NH, 128, 128), lambda c: (0, 0, 0)),
                  pl.BlockSpec((1, DB), lambda c: (0, 0)), pl.BlockSpec((1, DB), lambda c: (0, 0))],
        out_specs=pl.BlockSpec((128, DB), lambda c: (c, 0)),
        out_shape=jax.ShapeDtypeStruct((S, DB), BF16),
        compiler_params=_params(1),
    )(proj, proj, ws, bsp_b, gain_v, bias_v)


def _branch(attn, gmlp, wpa_g, wpb_g, proj):
    tn = 512

    def body(a_ref, g_ref, wa_ref, wb_ref, ga_ref, gb_ref, ya_ref, yb_ref, mg_ref):
        ya = _dot(a_ref[...], wa_ref[...])
        yb = _dot(g_ref[...], wb_ref[...])
        ya_ref[...] = ya.astype(BF16)
        yb_ref[...] = yb.astype(BF16)
        mg_ref[...] = (_sigmoid(ga_ref[...]) * ya + _sigmoid(gb_ref[...]) * yb).astype(BF16)

    out = pl.BlockSpec((S, tn), lambda j: (0, j))
    return pl.pallas_call(
        body, name="branch", grid=(D // tn,),
        in_specs=[pl.BlockSpec((S, DA), lambda j: (0, 0)), pl.BlockSpec((S, DB), lambda j: (0, 0)),
                  pl.BlockSpec((None, DA, tn), lambda j: (j, 0, 0)), pl.BlockSpec((None, DB, tn), lambda j: (j, 0, 0)),
                  pl.BlockSpec((S, tn), lambda j: (0, 5120 // tn + j)), pl.BlockSpec((S, tn), lambda j: (0, 7168 // tn + j))],
        out_specs=[out, out, out],
        out_shape=[jax.ShapeDtypeStruct((S, D), BF16)] * 3,
        compiler_params=_params(1),
    )(attn, gmlp, wpa_g, wpb_g, proj, proj)


def _out_ln1(merged, wout_g, x, gain, bias):
    tm = 256

    def body(m_ref, w_ref, x_ref, g_ref, b_ref, xh_ref, rs_ref, h_ref):
        pre = ALPHA * x_ref[...] + _dot(m_ref[...], w_ref[...])
        mean = jnp.mean(pre, axis=1, keepdims=True)
        cen = pre - mean
        var = jnp.mean(cen * cen, axis=1, keepdims=True)
        rstd = lax.rsqrt(var + LN_EPS)
        xhat = cen * rstd
        xh_ref[...] = xhat
        rs_ref[...] = jnp.broadcast_to(rstd, (tm, 128))
        h_ref[...] = (xhat * g_ref[...] + b_ref[...]).astype(BF16)

    row = pl.BlockSpec((tm, D), lambda i: (i, 0))
    vec = pl.BlockSpec((1, D), lambda i: (0, 0))
    return pl.pallas_call(
        body, name="out_ln1", grid=(S // tm,),
        in_specs=[row, pl.BlockSpec((D, D), lambda i: (0, 0)), row, vec, vec],
        out_specs=[row, pl.BlockSpec((tm, 128), lambda i: (i, 0)), row],
        out_shape=[jax.ShapeDtypeStruct((S, D), F32), jax.ShapeDtypeStruct((S, 128), F32),
                   jax.ShapeDtypeStruct((S, D), BF16)],
        compiler_params=_params(1),
    )(merged, wout_g, x, gain, bias)


def _ff1(h1b, w1_g, b1, half, name, into=None, after=None):
    tn = 512
    per = D // tn
    steps = DFF // tn // 2
    first = half * steps

    def body(h_ref, w_ref, b_ref, *rest):
        a_ref, r_ref = rest[-2:]
        r = jnp.maximum(_dot(h_ref[...], w_ref[...]) + b_ref[...], 0.0)
        r_ref[...] = r.astype(BF16)
        a_ref[...] = (r * r).astype(BF16)

    out = pl.BlockSpec((S, tn), lambda j: (0, first + j))
    extra = list(into) if into is not None else []
    if after is not None:
        extra.append(after)
    return pl.pallas_call(
        body, name=name, grid=(steps,),
        in_specs=[pl.BlockSpec((S, D), lambda j: (0, 0)),
                  pl.BlockSpec((None, D, tn), lambda j: ((first + j) // per, 0, (first + j) % per)),
                  pl.BlockSpec((1, tn), lambda j: (0, first + j))] + [ANY] * len(extra),
        out_specs=[out, out],
        out_shape=[jax.ShapeDtypeStruct((S, DFF), BF16)] * 2,
        input_output_aliases={3: 0, 4: 1} if into is not None else {},
        compiler_params=_params(1),
    )(h1b, w1_g, b1, *extra)


def _ff2_ln2_loss(a, w2_g, xhat1, g1, b1, b2, g2, be2, target):
    tm, tk = 512, 1024
    nk = DFF // tk

    def body(a_ref, w_ref, xh_ref, g1_ref, b1_ref, b2_ref, g2_ref, be2_ref, t_ref, d_ref, db_ref, st_ref, acc):
        i, k = pl.program_id(0), pl.program_id(1)

        @pl.when(k == 0)
        def _():
            acc[...] = jnp.zeros_like(acc)

        @pl.when((i == 0) & (k == 0))
        def _():
            st_ref[...] = jnp.zeros_like(st_ref)

        acc[...] += _dot(a_ref[...], w_ref[...])

        @pl.when(k == nk - 1)
        def _():
            def rows_chunk(ci, carry):
                rows = pl.ds(pl.multiple_of(ci * 128, 128), 128)
                h1 = xh_ref[rows, :] * g1_ref[...] + b1_ref[...]
                pre = ALPHA * h1 + acc[rows, :] + b2_ref[...]
                mean = jnp.mean(pre, axis=1, keepdims=True)
                cen = pre - mean
                var = jnp.mean(cen * cen, axis=1, keepdims=True)
                rstd = lax.rsqrt(var + LN_EPS)
                xhat = cen * rstd
                y = xhat * g2_ref[...] + be2_ref[...]
                err = y - t_ref[rows, :]
                dy = err * (1.0 / D)
                g = dy * g2_ref[...]
                dpre = rstd * (g - jnp.mean(g, axis=1, keepdims=True)
                               - xhat * jnp.mean(g * xhat, axis=1, keepdims=True))
                d_ref[rows, :] = dpre
                db_ref[rows, :] = dpre.astype(BF16)
                st_ref[0:1, :] += jnp.sum(dy * xhat, axis=0, keepdims=True)
                st_ref[1:2, :] += jnp.sum(dy, axis=0, keepdims=True)
                st_ref[2:3, :] += jnp.sum(dpre, axis=0, keepdims=True)
                st_ref[3:4, :] += jnp.broadcast_to(jnp.sum(err * err).reshape(1, 1), (1, D))
                return carry

            lax.fori_loop(0, tm // 128, rows_chunk, 0)

    row = pl.BlockSpec((tm, D), lambda i, k: (i, 0))
    vec = pl.BlockSpec((1, D), lambda i, k: (0, 0))
    return pl.pallas_call(
        body, name="ff2_ln2_loss", grid=(S // tm, nk),
        in_specs=[pl.BlockSpec((tm, tk), lambda i, k: (i, k)), pl.BlockSpec((tk, D), lambda i, k: (k, 0)),
                  row, vec, vec, vec, vec, vec, row],
        out_specs=[row, row, pl.BlockSpec((8, D), lambda i, k: (0, 0))],
        out_shape=[jax.ShapeDtypeStruct((S, D), F32), jax.ShapeDtypeStruct((S, D), BF16),
                   jax.ShapeDtypeStruct((8, D), F32)],
        scratch_shapes=[pltpu.VMEM((tm, D), F32)],
        compiler_params=_params(2),
    )(a, w2_g, xhat1, g1, b1, b2, g2, be2, target)


def _grad_w(act, dout, name, ti, tj, sharded, after=None):
    m, n = act.shape[1], dout.shape[1]
    ns = n // N_CHIPS
    per = ns // tj if sharded else None

    def body(a_ref, b_ref, o_ref, at_scr):
        @pl.when(pl.program_id(1) == 0)
        def _():
            at_scr[...] = a_ref[...].T

        o_ref[...] = _dot(at_scr[...], b_ref[...]).astype(BF16)

    if sharded:
        out_spec = pl.BlockSpec((None, ti, tj), lambda i, j: (j // per, i, j % per))
        out_shape = jax.ShapeDtypeStruct((N_CHIPS, m, ns), BF16)
    else:
        out_spec = pl.BlockSpec((ti, tj), lambda i, j: (i, j))
        out_shape = jax.ShapeDtypeStruct((m, n), BF16)
    body, more_specs, more = _behind(body, 2, after)
    return pl.pallas_call(
        body, name=name, grid=(m // ti, n // tj),
        in_specs=[pl.BlockSpec((S, ti), lambda i, j: (0, i)), pl.BlockSpec((S, tj), lambda i, j: (0, j))] + more_specs,
        out_specs=out_spec, out_shape=out_shape,
        scratch_shapes=[pltpu.VMEM((ti, S), BF16)],
        compiler_params=_params(2),
    )(act, dout, *more)


def _d_ff1(dpre2b, w2_g, r, after=None):
    tn = 512

    def body(d_ref, w_ref, r_ref, o_ref, gb_ref):
        da = _dot_nt(d_ref[...], w_ref[...])
        dp = da * (2.0 * r_ref[...].astype(F32))
        o_ref[...] = dp.astype(BF16)
        gb_ref[...] = jnp.sum(dp, axis=0, keepdims=True)

    body, more_specs, more = _behind(body, 3, after)
    return pl.pallas_call(
        body, name="d_ff1", grid=(DFF // tn,),
        in_specs=[pl.BlockSpec((S, D), lambda j: (0, 0)), pl.BlockSpec((tn, D), lambda j: (j, 0)),
                  pl.BlockSpec((S, tn), lambda j: (0, j))] + more_specs,
        out_specs=[pl.BlockSpec((S, tn), lambda j: (0, j)), pl.BlockSpec((1, tn), lambda j: (0, j))],
        out_shape=[jax.ShapeDtypeStruct((S, DFF), BF16), jax.ShapeDtypeStruct((1, DFF), F32)],
        compiler_params=_params(1),
    )(dpre2b, w2_g, r, *more)


def _d_h1_ln1(dprea, w1_g, dpre2, xhat1, rstd1, g1, after=None):
    tm, tk = 512, 1024
    per = D // tk
    nk = DFF // tk

    def body(a_ref, w_ref, d2_ref, xh_ref, rs_ref, g_ref, d_ref, db_ref, st_ref, acc):
        i, k = pl.program_id(0), pl.program_id(1)

        @pl.when(k == 0)
        def _():
            acc[...] = jnp.zeros_like(acc)

        @pl.when((i == 0) & (k == 0))
        def _():
            st_ref[...] = jnp.zeros_like(st_ref)

        acc[...] += _dot_nt(a_ref[...], w_ref[...])

        @pl.when(k == nk - 1)
        def _():
            def rows_chunk(ci, carry):
                rows = pl.ds(pl.multiple_of(ci * 128, 128), 128)
                dh = ALPHA * d2_ref[rows, :] + acc[rows, :]
                xhat = xh_ref[rows, :]
                g = dh * g_ref[...]
                dpre = rs_ref[rows, 0:1] * (g - jnp.mean(g, axis=1, keepdims=True)
                                            - xhat * jnp.mean(g * xhat, axis=1, keepdims=True))
                d_ref[rows, :] = dpre
                db_ref[rows, :] = dpre.astype(BF16)
                st_ref[0:1, :] += jnp.sum(dh * xhat, axis=0, keepdims=True)
                st_ref[1:2, :] += jnp.sum(dh, axis=0, keepdims=True)
                return carry

            lax.fori_loop(0, tm // 128, rows_chunk, 0)

    row = pl.BlockSpec((tm, D), lambda i, k: (i, 0))
    body, more_specs, more = _behind(body, 6, after)
    return pl.pallas_call(
        body, name="d_h1_ln1", grid=(S // tm, nk),
        in_specs=[pl.BlockSpec((tm, tk), lambda i, k: (i, k)),
                  pl.BlockSpec((None, D, tk), lambda i, k: (k // per, 0, k % per)),
                  row, row, pl.BlockSpec((tm, 128), lambda i, k: (i, 0)), pl.BlockSpec((1, D), lambda i, k: (0, 0))]
        + more_specs,
        out_specs=[row, row, pl.BlockSpec((8, D), lambda i, k: (0, 0))],
        out_shape=[jax.ShapeDtypeStruct((S, D), F32), jax.ShapeDtypeStruct((S, D), BF16),
                   jax.ShapeDtypeStruct((8, D), F32)],
        scratch_shapes=[pltpu.VMEM((tm, D), F32)],
        compiler_params=_params(2),
    )(dprea, w1_g, dpre2, xhat1, rstd1, g1, *more)


def _d_merged(dpre1b, wout_g, proj, ya, yb):
    tm, tn = 512, 1024

    def body(d_ref, w_ref, ga_ref, gb_ref, ya_ref, yb_ref, dya_ref, dyb_ref, dga_ref, dgb_ref):
        dm = _dot_nt(d_ref[...], w_ref[...])
        sa = _sigmoid(ga_ref[...])
        sb = _sigmoid(gb_ref[...])
        dya_ref[...] = (dm * sa).astype(BF16)
        dyb_ref[...] = (dm * sb).astype(BF16)
        dga_ref[...] = (dm * ya_ref[...].astype(F32) * sa * (1.0 - sa)).astype(BF16)
        dgb_ref[...] = (dm * yb_ref[...].astype(F32) * sb * (1.0 - sb)).astype(BF16)

    tile = pl.BlockSpec((tm, tn), lambda j, i: (i, j))
    return pl.pallas_call(
        body, name="d_merged", grid=(D // tn, S // tm),
        in_specs=[pl.BlockSpec((tm, D), lambda j, i: (i, 0)), pl.BlockSpec((tn, D), lambda j, i: (j, 0)),
                  pl.BlockSpec((tm, tn), lambda j, i: (i, 5 + j)), pl.BlockSpec((tm, tn), lambda j, i: (i, 7 + j)),
                  tile, tile],
        out_specs=[tile] * 4,
        out_shape=[jax.ShapeDtypeStruct((S, D), BF16)] * 4,
        compiler_params=_params(2),
    )(dpre1b, wout_g, proj, proj, ya, yb)


def _d_branches(dya, dyb, wpa_g, wpb_g, after=None):
    tm = 512
    ws = D // N_CHIPS

    def body(da_ref, db_ref, wa_ref, wb_ref, oa_ref, ob_ref):
        for d_ref, w_ref, o_ref in ((da_ref, wa_ref, oa_ref), (db_ref, wb_ref, ob_ref)):
            acc = _dot_nt(d_ref[:, 0:ws], w_ref[0])
            for s in range(1, N_CHIPS):
                acc = acc + _dot_nt(d_ref[:, s * ws:(s + 1) * ws], w_ref[s])
            o_ref[...] = acc

    rows = lambda width: pl.BlockSpec((tm, width), lambda i: (i, 0))
    whole = lambda n: pl.BlockSpec((N_CHIPS, n, ws), lambda i: (0, 0, 0))
    body, more_specs, more = _behind(body, 4, after)
    return pl.pallas_call(
        body, name="d_branches", grid=(S // tm,),
        in_specs=[rows(D), rows(D), whole(DA), whole(DB)] + more_specs,
        out_specs=[rows(DA), rows(DB)],
        out_shape=[jax.ShapeDtypeStruct((S, DA), F32), jax.ShapeDtypeStruct((S, DB), F32)],
        compiler_params=_params(1),
    )(dya, dyb, wpa_g, wpb_g, *more)


def _gmlp_bwd(proj, dgmlp, ws, ws_t, bsp_b, gain_v, bias_v):
    def body(u_ref, vb_ref, dg_ref, ws_ref, wst_ref, bsp_ref, g_ref, be_ref, duv_ref, gws_ref, gbs_ref, st_ref):
        @pl.when(pl.program_id(0) == 0)
        def _():
            gws_ref[...] = jnp.zeros_like(gws_ref)
            gbs_ref[...] = jnp.zeros_like(gbs_ref)
            st_ref[...] = jnp.zeros_like(st_ref)

        u, tu, u_act, vb, tv, rstd, xhat, vn = _gmlp_parts(u_ref, vb_ref, g_ref, be_ref)
        dg = dg_ref[...]
        dz = dg * u_act
        row = lax.broadcasted_iota(jnp.int32, (128, 128), 0)
        col = lax.broadcasted_iota(jnp.int32, (128, 128), 1)
        causal = row >= col
        causal_t = row <= col
        dvn_parts = []
        z_parts = []
        for g in range(NH):
            cols = slice(g * 128, (g + 1) * 128)
            vng = vn[:, cols].astype(BF16)
            dzg = dz[:, cols]
            dzb = dzg.astype(BF16)
            wsg = jnp.where(causal, ws_ref[g], 0.0).astype(BF16)
            wsg_t = jnp.where(causal_t, wst_ref[g], 0.0).astype(BF16)
            z_parts.append(_dot(wsg, vng) + bsp_ref[g])
            gws_ref[g] += jnp.where(causal, _dot_nt(dzb, vng), 0.0)
            gbs_ref[g] += jnp.broadcast_to(jnp.sum(dzg, axis=1, keepdims=True), (128, 128))
            dvn_parts.append(_dot(wsg_t, dzb))
        z = jnp.concatenate(z_parts, axis=1)
        dvn = jnp.concatenate(dvn_parts, axis=1)
        du = dg * z * _gelu_grad(u, tu)
        st_ref[0:1, :] += jnp.sum(dvn * xhat, axis=0, keepdims=True)
        st_ref[1:2, :] += jnp.sum(dvn, axis=0, keepdims=True)
        gg = dvn * g_ref[...]
        dgv = rstd * (gg - jnp.mean(gg, axis=1, keepdims=True) - xhat * jnp.mean(gg * xhat, axis=1, keepdims=True))
        dvb = dgv * _gelu_grad(vb, tv)
        duv_ref[:, 0:DB] = du.astype(BF16)
        duv_ref[:, DB:2 * DB] = dvb.astype(BF16)

    full3 = pl.BlockSpec((NH, 128, 128), lambda c: (0, 0, 0))
    vec = pl.BlockSpec((1, DB), lambda c: (0, 0))
    return pl.pallas_call(
        body, name="gmlp_bwd", grid=(NBLK,),
        in_specs=[pl.BlockSpec((128, DB), lambda c: (c, 3)), pl.BlockSpec((128, DB), lambda c: (c, 4)),
                  pl.BlockSpec((128, DB), lambda c: (c, 0)), full3, full3, full3, vec, vec],
        out_specs=[pl.BlockSpec((128, 2 * DB), lambda c: (c, 0)), full3, full3, pl.BlockSpec((8, DB), lambda c: (0, 0))],
        out_shape=[jax.ShapeDtypeStruct((S, 2 * DB), BF16), jax.ShapeDtypeStruct((NH, 128, 128), F32),
                   jax.ShapeDtypeStruct((NH, 128, 128), F32), jax.ShapeDtypeStruct((8, DB), F32)],
        compiler_params=_params(1),
    )(proj, proj, dgmlp, ws, ws_t, bsp_b, gain_v, bias_v)


def _rel_bias_grad(ds_sums):
    buckets = jnp.asarray(np.stack([_bucket_tile(d) for _, d in PATTERNS]))

    def body(bk_ref, ds_ref, o_ref):
        row = lax.broadcasted_iota(jnp.int32, (N_BUCKETS, 128), 0)
        lane = lax.broadcasted_iota(jnp.int32, (N_BUCKETS, 128), 1)

        def one_bucket(t, out):
            hits = [bk_ref[p] == t for p in range(3)]
            for h in range(NH):
                tot = jnp.zeros((128, 256), F32)
                for p in range(3):
                    tot = tot + jnp.where(hits[p], ds_ref[p, h], 0.0)
                out = jnp.where((row == t) & (lane == h), jnp.sum(tot), out)
            return out

        o_ref[...] = lax.fori_loop(0, N_BUCKETS, one_bucket, jnp.zeros((N_BUCKETS, 128), F32))

    return pl.pallas_call(
        body, name="rel_bias_grad",
        in_specs=[pl.BlockSpec(memory_space=pltpu.VMEM)] * 2, out_specs=pl.BlockSpec(memory_space=pltpu.VMEM),
        out_shape=jax.ShapeDtypeStruct((N_BUCKETS, 128), F32),
        compiler_params=pltpu.CompilerParams(vmem_limit_bytes=VMEM_LIMIT),
    )(buckets, ds_sums)


def _d_x(dproj, win_g, dpre1, after=None):
    tm, tn = 512, 512
    ws = DIN // N_CHIPS

    def body(a_ref, w_ref, d_ref, o_ref):
        acc = ALPHA * d_ref[...]
        for s in range(N_CHIPS):
            acc = acc + _dot_nt(a_ref[:, s * ws:(s + 1) * ws], w_ref[s])
        o_ref[...] = acc

    tile = pl.BlockSpec((tm, tn), lambda i, j: (i, j))
    body, more_specs, more = _behind(body, 3, after)
    return pl.pallas_call(
        body, name="d_x", grid=(S // tm, D // tn),
        in_specs=[pl.BlockSpec((tm, DIN), lambda i, j: (i, 0)),
                  pl.BlockSpec((N_CHIPS, tn, ws), lambda i, j: (0, j, 0)), tile] + more_specs,
        out_specs=tile, out_shape=jax.ShapeDtypeStruct((S, D), F32),
        compiler_params=_params(2),
    )(dproj, win_g, dpre1, *more)


def _adamw(w, g, m, v, name, after=None):
    rows, cols = w.shape
    tm = max(t for t in range(8, 257, 8) if rows % t == 0)

    def body(w_ref, g_ref, m_ref, v_ref, d_ref, nm_ref, nv_ref, go_ref):
        g = g_ref[...]
        m = ADAM_B1 * m_ref[...] + (1.0 - ADAM_B1) * g
        v = ADAM_B2 * v_ref[...] + (1.0 - ADAM_B2) * (g * g)
        m_hat = m / (1.0 - ADAM_B1 ** ADAM_STEP)
        v_hat = v / (1.0 - ADAM_B2 ** ADAM_STEP)
        d_ref[...] = -ADAM_LR * (m_hat / (jnp.sqrt(v_hat) + ADAM_EPS) + ADAM_WD * w_ref[...])
        nm_ref[...] = m
        nv_ref[...] = v
        go_ref[...] = g

    spec = pl.BlockSpec((tm, cols), lambda i: (i, 0))
    body, more_specs, more = _behind(body, 4, after)
    return pl.pallas_call(
        body, name=name, grid=(rows // tm,), in_specs=[spec] * 4 + more_specs, out_specs=[spec] * 4,
        out_shape=[jax.ShapeDtypeStruct((rows, cols), F32)] * 4, compiler_params=_params(1),
    )(w, g, m, v, *more)


def _position():
    x, y, c = lax.axis_index("x"), lax.axis_index("y"), lax.axis_index("c")
    chips = [(1 - x, y), (x, 1 - y), (1 - x, 1 - y)]
    return x, y, c, chips


def _remote(src, dst, send_sems, recv_sems, k, to):
    return pltpu.make_async_remote_copy(src_ref=src, dst_ref=dst, send_sem=send_sems.at[k], recv_sem=recv_sems.at[k],
                                        device_id=to, device_id_type=MESH)


def _place_shard(w, name, after=None):
    rows, cols = w.shape
    tm = 256
    x, y = lax.axis_index("x"), lax.axis_index("y")

    def body(chip_ref, w_ref, o_ref):
        o_ref[...] = w_ref[...].astype(BF16)

    more_specs, more = ([ANY], [after]) if after is not None else ([], [])
    if after is not None:
        inner = body
        body = lambda chip_ref, w_ref, after_ref, o_ref: inner(chip_ref, w_ref, o_ref)
    return pl.pallas_call(
        body, name=name,
        grid_spec=pltpu.PrefetchScalarGridSpec(
            num_scalar_prefetch=1, grid=(rows // tm,),
            in_specs=[pl.BlockSpec((tm, cols), lambda i, chip: (i, 0))] + more_specs,
            out_specs=pl.BlockSpec((None, tm, cols), lambda i, chip: (chip[0], i, 0))),
        out_shape=jax.ShapeDtypeStruct((N_CHIPS, rows, cols), BF16),
        compiler_params=_params(1),
    )(jnp.reshape(2 * x + y, (1,)).astype(jnp.int32), w, *more)


def _to_bf16(x, name, after=None):
    tm = 256

    def body(x_ref, o_ref):
        o_ref[...] = x_ref[...].astype(BF16)

    spec = pl.BlockSpec((tm, x.shape[1]), lambda i: (i, 0))
    body, more_specs, more = _behind(body, 1, after)
    return pl.pallas_call(
        body, name=name, grid=(x.shape[0] // tm,), in_specs=[spec] + more_specs, out_specs=spec,
        out_shape=jax.ShapeDtypeStruct(x.shape, BF16), compiler_params=_params(1),
    )(x, *more)


HBM = pl.BlockSpec(memory_space=pltpu.HBM)
SEM = pl.BlockSpec(memory_space=pltpu.SEMAPHORE)
EFFECT = pltpu.SideEffectType.DATAFLOW_SIDE_EFFECTING


def _comm_call(name, body, bufs, sems_in, sems_out, after=None, token=False):
    nb, ns, no = len(bufs), len(sems_in), len(sems_out)
    n_in = nb + ns + (after is not None)

    def wrapped(*refs):
        body(refs[:nb], refs[nb:nb + ns], refs[n_in + nb:n_in + nb + no])
        if token:
            refs[-1][...] = jnp.zeros((8, 128), F32)

    outs = pl.pallas_call(
        wrapped, name=name,
        in_specs=[HBM] * nb + [SEM] * ns + ([ANY] if after is not None else []),
        out_specs=[HBM] * nb + [SEM] * no + ([pl.BlockSpec(memory_space=pltpu.VMEM)] if token else []),
        out_shape=[pltpu.HBM(b.shape, b.dtype) for b in bufs] + [pltpu.SemaphoreType.DMA((k,)) for k in sems_out]
        + ([jax.ShapeDtypeStruct((8, 128), F32)] if token else []),
        input_output_aliases={i: i for i in range(nb)},
        compiler_params=pltpu.CompilerParams(has_side_effects=EFFECT),
    )(*[pltpu.with_memory_space_constraint(b, pltpu.HBM) for b in bufs], *sems_in, *([after] if after is not None else []))
    return list(outs[:nb]), list(outs[nb:nb + no]), (outs[-1] if token else None)


RING_STAGES = {"ici_near": 2, "ici_far": 2, "d2d_near": 2, "d2d_far": 1}


def _ring_copies(buf, send_sems, recv_sems, k0, stage):
    x, y, c, _ = _position()
    hr = buf.shape[1] // 2
    qr = hr // 2
    half = lambda chip, h: buf.at[chip, pl.ds(h * hr, hr), :]
    quarter = lambda chip, h, q: buf.at[chip, pl.ds(h * hr + q * qr, qr), :]
    mine, x_chip, y_chip, far_chip = 2 * x + y, 2 * (1 - x) + y, 2 * x + (1 - y), 2 * (1 - x) + (1 - y)
    to_x, to_y, sibling = (1 - x, y, c), (x, 1 - y, c), (x, y, 1 - c)
    if stage == "ici_near":
        moves = [(half(mine, c), to_x, half(x_chip, c)), (half(mine, c), to_y, half(y_chip, c))]
    elif stage == "ici_far":
        moves = [(quarter(x_chip, c, 0), to_y, quarter(far_chip, c, 0)),
                 (quarter(y_chip, c, 1), to_x, quarter(far_chip, c, 1))]
    elif stage == "d2d_near":
        moves = [(half(x_chip, c), sibling, half(x_chip, 1 - c)), (half(y_chip, c), sibling, half(y_chip, 1 - c))]
    else:
        moves = [(half(far_chip, c), sibling, half(far_chip, 1 - c))]
    sends = [_remote(src, src, send_sems, recv_sems, k0 + i, to) for i, (src, to, _) in enumerate(moves)]
    arrivals = [_remote(got, got, send_sems, recv_sems, k0 + i, (x, y, c)) for i, (_, _, got) in enumerate(moves)]
    return sends, arrivals


def _ring_call(name, groups, actions, after=None):
    tags = list(dict.fromkeys(t for _, t, _ in actions))
    counts = {t: len(groups[t]["bufs"]) for t in tags}
    first = {t: sum(counts[u] for u in tags[:i]) for i, t in enumerate(tags)}
    waits = [(t, s) for v, t, s in actions if v == "wait"]
    starts = [(t, s) for v, t, s in actions if v == "start"]

    def body(bufs, sems_in, sems_out):
        for verb, t, s in actions:
            at, sems = (starts.index((t, s)), sems_out) if verb == "start" else (waits.index((t, s)), sems_in)
            for w in range(counts[t]):
                sends, arrivals = _ring_copies(bufs[first[t] + w], sems[2 * at], sems[2 * at + 1], RING_STAGES[s] * w, s)
                if verb == "start":
                    for cp in sends:
                        cp.start()
                else:
                    for cp in arrivals:
                        cp.wait_recv()
                    for cp in sends:
                        cp.wait_send()

    bufs, sems, token = _comm_call(
        name, body, [b for t in tags for b in groups[t]["bufs"]],
        [sem for t, s in waits for sem in groups[t]["sems"][s]],
        [RING_STAGES[s] * counts[t] for t, s in starts for _ in (0, 1)], after, token=True)
    for t in tags:
        groups[t]["bufs"] = bufs[first[t]:first[t] + counts[t]]
    for t, s in waits:
        del groups[t]["sems"][s]
    for i, (t, s) in enumerate(starts):
        groups[t]["sems"][s] = (sems[2 * i], sems[2 * i + 1])
    return token


def _cx_copies(src, dst, send_sems, recv_sems, k0):
    x, y, c, chips = _position()
    sends = [_remote(src.at[2 * cx + cy], dst.at[2 * x + y], send_sems, recv_sems, k0 + j, (cx, cy, c))
             for j, (cx, cy) in enumerate(chips)]
    arrivals = [_remote(dst.at[2 * cx + cy], dst.at[2 * cx + cy], send_sems, recv_sems, k0 + j, (x, y, c))
                for j, (cx, cy) in enumerate(chips)]
    return sends, arrivals


def _cx_start(name, pair_sums):
    n = len(pair_sums)
    landing = [lax.empty(p.shape, p.dtype) for p in pair_sums]

    def body(bufs, _, sems):
        for w in range(n):
            for cp in _cx_copies(bufs[w], bufs[n + w], sems[0], sems[1], 3 * w)[0]:
                cp.start()

    bufs, sems, token = _comm_call(name, body, list(pair_sums) + landing, [], [3 * n, 3 * n], token=True)
    return (bufs, sems), token


def _cx_wait(name, state, after):
    bufs, sems = state
    n = len(bufs) // 2

    def body(refs, sems_in, _):
        for w in range(n):
            sends, arrivals = _cx_copies(refs[w], refs[n + w], sems_in[0], sems_in[1], 3 * w)
            for cp in arrivals:
                cp.wait_recv()
            for cp in sends:
                cp.wait_send()

    bufs, _, _ = _comm_call(name, body, bufs, sems, [], after)
    return bufs[:n], bufs[n:]


def _px_copies(src, dst, send_sems, recv_sems, k):
    x, y, c, _ = _position()
    hr = src.shape[1] // 2
    send = _remote(src.at[:, pl.ds((1 - c) * hr, hr), :], dst, send_sems, recv_sems, k, (x, y, 1 - c))
    arrival = _remote(dst, dst, send_sems, recv_sems, k, (x, y, c))
    return send, arrival


def _px_start(name, grads):
    n = len(grads)
    landing = [lax.empty((N_CHIPS, g.shape[1] // 2, g.shape[2]), g.dtype) for g in grads]

    def body(bufs, _, sems):
        for w in range(n):
            _px_copies(bufs[w], bufs[n + w], sems[0], sems[1], w)[0].start()

    bufs, sems, token = _comm_call(name, body, list(grads) + landing, [], [n, n], token=True)
    return (bufs, sems), token


def _px_wait(name, state, after):
    bufs, sems = state
    n = len(bufs) // 2

    def body(refs, sems_in, _):
        for w in range(n):
            send, arrival = _px_copies(refs[w], refs[n + w], sems_in[0], sems_in[1], w)
            arrival.wait_recv()
            send.wait_send()

    bufs, _, _ = _comm_call(name, body, bufs, sems, [], after)
    return bufs[:n], bufs[n:]


def _pair_sum(grad, got, name):
    _, rows, cols = grad.shape
    hr = rows // 2
    tm = min(hr, 512)
    nb = hr // tm
    c = lax.axis_index("c")

    def body(c_ref, g_ref, o_ref, out_ref):
        out_ref[...] = (g_ref[...].astype(F32) + o_ref[...].astype(F32)).astype(BF16)

    return pl.pallas_call(
        body, name=name,
        grid_spec=pltpu.PrefetchScalarGridSpec(
            num_scalar_prefetch=1, grid=(N_CHIPS, nb),
            in_specs=[pl.BlockSpec((None, tm, cols), lambda s, i, c_ref: (s, c_ref[0] * nb + i, 0)),
                      pl.BlockSpec((None, tm, cols), lambda s, i, c_ref: (s, i, 0))],
            out_specs=pl.BlockSpec((None, tm, cols), lambda s, i, c_ref: (s, i, 0))),
        out_shape=jax.ShapeDtypeStruct((N_CHIPS, hr, cols), BF16),
        compiler_params=_params(2),
    )(jnp.reshape(c, (1,)).astype(jnp.int32), grad, got)


def _chip_sum(parts, pair_sums, name):
    _, hr, cols = parts.shape
    tm = min(hr, 512)
    nb = hr // tm
    x, y, c = lax.axis_index("x"), lax.axis_index("y"), lax.axis_index("c")

    def body(pos_ref, p_ref, own_ref, o_ref):
        chip = pos_ref[0]
        own = own_ref[...].astype(F32)
        term = lambda s: jnp.where(chip == s, own, p_ref[s].astype(F32))
        o_ref[...] = ((term(0) + term(1)) + term(2)) + term(3)

    return pl.pallas_call(
        body, name=name,
        grid_spec=pltpu.PrefetchScalarGridSpec(
            num_scalar_prefetch=1, grid=(nb,),
            in_specs=[pl.BlockSpec((N_CHIPS, tm, cols), lambda i, pos: (0, i, 0)),
                      pl.BlockSpec((None, tm, cols), lambda i, pos: (pos[0], i, 0))],
            out_specs=pl.BlockSpec((tm, cols), lambda i, pos: (pos[1] * nb + i, 0))),
        out_shape=jax.ShapeDtypeStruct((2 * hr, cols), F32), compiler_params=_params(1),
    )(jnp.stack([2 * x + y, c]).astype(jnp.int32), parts, pair_sums)


def _share_copies(buf, send_sems, recv_sems, k):
    x, y, c, _ = _position()
    hr = buf.shape[0] // 2
    mine, theirs = buf.at[pl.ds(c * hr, hr), :], buf.at[pl.ds((1 - c) * hr, hr), :]
    return (_remote(mine, mine, send_sems, recv_sems, k, (x, y, 1 - c)),
            _remote(theirs, theirs, send_sems, recv_sems, k, (x, y, c)))


def _share_start(name, bufs):
    n = len(bufs)

    def body(refs, _, sems):
        for w in range(n):
            _share_copies(refs[w], sems[0], sems[1], w)[0].start()

    bufs, sems, token = _comm_call(name, body, list(bufs), [], [n, n], token=True)
    return (bufs, sems), token


def _share_wait(name, state, after):
    bufs, sems = state

    def body(refs, sems_in, _):
        for w in range(len(bufs)):
            send, arrival = _share_copies(refs[w], sems_in[0], sems_in[1], w)
            arrival.wait_recv()
            send.wait_send()

    return _comm_call(name, body, bufs, sems, [], after)[0]


def _allreduce_small(g):
    rows = g.shape[0]
    half = rows // 2

    def body(g_ref, o_ref, sib, slots, send_sems, recv_sems):
        x, y, c, chips = _position()
        me, sibling = (x, y, c), (x, y, 1 - c)
        my_chip = 2 * x + y
        mine = pl.ds(pl.multiple_of(c * half, 8), half)
        theirs = pl.ds(pl.multiple_of((1 - c) * half, 8), half)
        pair = _remote(g_ref.at[theirs], sib, send_sems, recv_sems, 0, sibling)
        pair.start()
        pair.wait()
        slots[my_chip] = g_ref[mine, :] + sib[...]
        sent = []
        for j, (cx, cy) in enumerate(chips):
            cp = _remote(slots.at[my_chip], slots.at[my_chip], send_sems, recv_sems, 1 + j, (cx, cy, c))
            cp.start()
            sent.append(cp)
        for j, (cx, cy) in enumerate(chips):
            got = slots.at[2 * cx + cy]
            _remote(got, got, send_sems, recv_sems, 1 + j, me).wait_recv()
        for cp in sent:
            cp.wait_send()
        o_ref[mine, :] = ((slots[0] + slots[1]) + slots[2]) + slots[3]
        swap = _remote(o_ref.at[mine], o_ref.at[mine], send_sems, recv_sems, 4, sibling)
        swap.start()
        swap.wait()

    vm = pl.BlockSpec(memory_space=pltpu.VMEM)
    return pl.pallas_call(
        body, name="allreduce_small",
        in_specs=[vm], out_specs=vm, out_shape=jax.ShapeDtypeStruct((rows, 128), F32),
        scratch_shapes=[pltpu.VMEM((half, 128), F32), pltpu.VMEM((N_CHIPS, half, 128), F32),
                        pltpu.SemaphoreType.DMA((5,)), pltpu.SemaphoreType.DMA((5,))],
        compiler_params=pltpu.CompilerParams(vmem_limit_bytes=VMEM_LIMIT),
    )(g)


_SMALL =("rel_bias", "ln_v_gain", "ln_v_bias", "w_spatial", "b_spatial", "ln1_gain", "ln1_bias",
          "b_ff1", "b_ff2", "ln2_gain", "ln2_bias")
_SMALL_ROWS = 1200
_LOSS_AT = (152832 // 128, 0)


def _pack_small(parts):
    flat = jnp.concatenate([parts[k].reshape(-1).astype(F32) for k in _SMALL])
    flat = jnp.pad(flat, (0, _SMALL_ROWS * 128 - flat.shape[0]))
    return flat.reshape(_SMALL_ROWS, 128)


def _unpack_small(packed, like):
    flat = packed.reshape(-1)
    out, at = {}, 0
    for k in _SMALL:
        n = math.prod(like[k].shape)
        out[k] = flat[at:at + n].reshape(like[k].shape)
        at += n
    return out


def kernel(x, w_in, rel_bias, ln_v_gain, ln_v_bias, w_spatial, b_spatial, w_proj_a, w_proj_b, w_out, ln1_gain, ln1_bias, w_ff1, b_ff1, w_ff2, b_ff2, ln2_gain, ln2_bias, loss_target, m_w_in, m_rel_bias, m_ln_v_gain, m_ln_v_bias, m_w_spatial, m_b_spatial, m_w_proj_a, m_w_proj_b, m_w_out, m_ln1_gain, m_ln1_bias, m_w_ff1, m_b_ff1, m_w_ff2, m_b_ff2, m_ln2_gain, m_ln2_bias, v_w_in, v_rel_bias, v_ln_v_gain, v_ln_v_bias, v_w_spatial, v_b_spatial, v_w_proj_a, v_w_proj_b, v_w_out, v_ln1_gain, v_ln1_bias, v_w_ff1, v_b_ff1, v_w_ff2, v_b_ff2, v_ln2_gain, v_ln2_bias):
    args = dict(locals())
    big = ("w_in", "w_proj_a", "w_proj_b", "w_out", "w_ff1", "w_ff2")
    weights = ("w_in", "rel_bias", "ln_v_gain", "ln_v_bias", "w_spatial", "b_spatial", "w_proj_a", "w_proj_b", "w_out",
               "ln1_gain", "ln1_bias", "w_ff1", "b_ff1", "w_ff2", "b_ff2", "ln2_gain", "ln2_bias")

    xs = x[0]
    target = loss_target[0]

    ring = {"a": {"bufs": [_place_shard(w_in[0], "place_w_in")], "sems": {}}}
    tok = _ring_call("allgather_a_near", ring, [("start", "a", "ici_near")])
    placed = [_place_shard(args[k][0], f"place_{k}", after=tok) for k in big[1:]]
    for tag, bufs in (("b", placed[0:3]), ("c", placed[3:4]), ("d", placed[4:5])):
        ring[tag] = {"bufs": bufs, "sems": {}}
    bias = _bias_tiles(rel_bias, after=placed[4])
    xb = _to_bf16(xs, "x_to_bf16", after=bias)

    mx, my = lax.axis_index("x"), lax.axis_index("y")
    own = jnp.reshape(2 * mx + my, (1,)).astype(jnp.int32)
    near = jnp.stack([2 * (1 - mx) + my, 2 * mx + (1 - my)]).astype(jnp.int32)
    far = jnp.reshape(2 * (1 - mx) + (1 - my), (1,)).astype(jnp.int32)
    proj = _proj(xb, ring["a"]["bufs"][0], own, "proj_own")
    _ring_call("allgather_a_far", ring, [("wait", "a", "ici_near"), ("start", "a", "ici_far"), ("start", "a", "d2d_near"),
                                         ("start", "b", "ici_near"), ("start", "c", "ici_near")], after=proj)
    _ring_call("allgather_a_near_done", ring, [("wait", "a", "d2d_near")])
    proj = _proj(xb, ring["a"]["bufs"][0], near, "proj_near", into=proj)
    _ring_call("allgather_a_last", ring, [("wait", "a", "ici_far"), ("start", "a", "d2d_far")], after=proj)
    _ring_call("allgather_a_done", ring, [("wait", "a", "d2d_far")])
    (win_g,) = ring["a"]["bufs"]
    proj = _proj(xb, win_g, far, "proj_far", into=proj)
    _ring_call("allgather_b_far", ring, [("wait", "b", "ici_near"), ("start", "b", "ici_far"), ("start", "b", "d2d_near")],
               after=proj)
    ws = w_spatial[0]
    ws_t = jnp.transpose(ws, (0, 2, 1))
    bsp_b = jnp.broadcast_to(b_spatial[0][:, :, None], (NH, 128, 128))
    gmlp = _gmlp_fwd(proj, ws, bsp_b, ln_v_gain, ln_v_bias)
    attn, lse = _attention_fwd(proj, bias)
    _ring_call("allgather_b_last_c_far", ring,
               [("wait", "b", "ici_far"), ("start", "b", "d2d_far"),
                ("wait", "c", "ici_near"), ("start", "c", "ici_far"), ("start", "c", "d2d_near"),
                ("start", "d", "ici_near")], after=attn)
    _ring_call("allgather_b_done", ring, [("wait", "b", "d2d_near"), ("wait", "b", "d2d_far")])
    wpa_g, wpb_g, wout_g = ring["b"]["bufs"]
    wout_full = wout_g.reshape(D, D)
    ya, yb, merged = _branch(attn, gmlp, wpa_g, wpb_g, proj)
    xhat1, rstd1, h1b = _out_ln1(merged, wout_full, xs, ln1_gain, ln1_bias)
    _ring_call("allgather_c_last", ring, [("wait", "c", "ici_far"), ("start", "c", "d2d_far")], after=h1b)
    _ring_call("allgather_c_done", ring, [("wait", "c", "d2d_near"), ("wait", "c", "d2d_far")])
    (w1_g,) = ring["c"]["bufs"]
    a, r = _ff1(h1b, w1_g, b_ff1, 0, "ff1_first")
    tok = _ring_call("allgather_d_far", ring,
                     [("wait", "d", "ici_near"), ("start", "d", "ici_far"), ("start", "d", "d2d_near")], after=a)
    a, r = _ff1(h1b, w1_g, b_ff1, 1, "ff1_second", into=(a, r), after=tok)
    _ring_call("allgather_d_last", ring, [("wait", "d", "ici_far"), ("start", "d", "d2d_far")], after=a)
    _ring_call("allgather_d_done", ring, [("wait", "d", "d2d_near"), ("wait", "d", "d2d_far")])
    (w2_g,) = ring["d"]["bufs"]
    w2_full = w2_g.reshape(DFF, D)
    dpre2, dpre2b, st2 = _ff2_ln2_loss(a, w2_full, xhat1, ln1_gain, ln1_bias, b_ff2, ln2_gain, ln2_bias, target)

    def pair_and_chip(tag, state, after):
        local, from_sibling = _px_wait(f"pair_exchange_wait_{tag}", state, after)
        pair_sums = [_pair_sum(g, o, f"pair_sum_{tag}_{i}") for i, (g, o) in enumerate(zip(local, from_sibling))]
        return _cx_start(f"chip_exchange_start_{tag}", pair_sums)

    g_w2 = _grad_w(a, dpre2b, "grad_w_ff2", 512, 2048, False)
    px, tok = _px_start("pair_exchange_start_w_ff2", [g_w2.reshape(N_CHIPS, DFF // N_CHIPS, D)])
    dprea, g_b1 = _d_ff1(dpre2b, w2_full, r, after=tok)
    cx_w2, tok = pair_and_chip("w_ff2", px, dprea)
    g_w1 = _grad_w(h1b, dprea, "grad_w_ff1", 512, 2048, True, after=tok)
    px, tok = _px_start("pair_exchange_start_w_ff1", [g_w1])
    dpre1, dpre1b, st1 = _d_h1_ln1(dprea, w1_g, dpre2, xhat1, rstd1, ln1_gain, after=tok)
    cx_w1, tok = pair_and_chip("w_ff1", px, dpre1b)
    g_wout = _grad_w(merged, dpre1b, "grad_w_out", 512, 2048, False, after=tok)
    dya, dyb, dga, dgb = _d_merged(dpre1b, wout_full, proj, ya, yb)
    g_wpa = _grad_w(attn, dya, "grad_w_proj_a", 1024, 512, True)
    g_wpb = _grad_w(gmlp, dyb, "grad_w_proj_b", 1024, 512, True)
    px, tok = _px_start("pair_exchange_start_b", [g_wpa, g_wpb, g_wout.reshape(N_CHIPS, D // N_CHIPS, D)])
    dattn, dgmlp = _d_branches(dya, dyb, wpa_g, wpb_g, after=tok)
    duv, g_ws, g_bs, stv = _gmlp_bwd(proj, dgmlp, ws, ws_t, bsp_b, ln_v_gain, ln_v_bias)
    cx_b, tok = pair_and_chip("b", px, duv)
    dq, dk, dv, ds_sums = _attention_bwd(proj, dattn, attn, lse, bias, after=tok)
    g_rb = _rel_bias_grad(ds_sums)[:, :NH]

    small_g = dict(rel_bias=g_rb, ln_v_gain=stv[0], ln_v_bias=stv[1], w_spatial=g_ws, b_spatial=g_bs[:, :, 0],
                   ln1_gain=st1[0], ln1_bias=st1[1], b_ff1=g_b1, b_ff2=st2[2], ln2_gain=st2[0], ln2_bias=st2[1])
    gs = _allreduce_small(_pack_small(small_g).at[_LOSS_AT].set(st2[3, 0]))
    ds_, ms_, vs_, _ = _adamw(_pack_small({k: args[k] for k in _SMALL}), gs,
                           _pack_small({k: args["m_" + k] for k in _SMALL}),
                           _pack_small({k: args["v_" + k] for k in _SMALL}), "adamw_small")
    like = {k: args[k] for k in _SMALL}
    grads, deltas, new_m, new_v = (_unpack_small(t, like) for t in (gs, ds_, ms_, vs_))

    dproj = jnp.concatenate([dq, dk, dv, duv, dga, dgb], axis=1)
    g_win = _grad_w(xb, dproj, "grad_w_in", 512, 2304, True, after=gs)
    px, tok = _px_start("pair_exchange_start_w_in", [g_win])

    def chip_sums(tag, state, names, after):
        pair_sums, from_chips = _cx_wait(f"chip_exchange_wait_{tag}", state, after)
        halves = [_chip_sum(p, own, f"chip_sum_{k}") for p, own, k in zip(from_chips, pair_sums, names)]
        return _share_start(f"share_start_{tag}", halves)

    def adam_one(k, g, after=None):
        d_, m_, v_, g_ = _adamw(args[k][0], g, args["m_" + k][0], args["v_" + k][0], f"adamw_{k}", after=after)
        grads[k], deltas[k], new_m[k], new_v[k] = g_[None], d_[None], m_[None], v_[None]
        return d_

    def adam(tag, state, names, after):
        last = None
        for k, g in zip(names, _share_wait(f"share_wait_{tag}", state, after)):
            last = adam_one(k, g)
        return last

    sh_w2, tok = chip_sums("w_ff2", cx_w2, ["w_ff2"], tok)
    sh_w1, tok = chip_sums("w_ff1", cx_w1, ["w_ff1"], tok)
    sh_b, tok = chip_sums("b", cx_b, ["w_proj_a", "w_proj_b", "w_out"], tok)
    cx_in, tok = pair_and_chip("w_in", px, tok)
    grad_x = _d_x(dproj, win_g, dpre1, after=tok)
    done = adam("w_ff2", sh_w2, ["w_ff2"], grad_x)
    done = adam("w_ff1", sh_w1, ["w_ff1"], done)
    g_wpa_full, g_wpb_full, g_wout_full = _share_wait("share_wait_b", sh_b, done)
    done = adam_one("w_out", g_wout_full)
    sh_in, tok = chip_sums("w_in", cx_in, ["w_in"], done)
    done = adam_one("w_proj_a", g_wpa_full, after=tok)
    done = adam_one("w_proj_b", g_wpb_full, after=done)
    adam("w_in", sh_in, ["w_in"], done)

    loss = gs[_LOSS_AT] * (0.5 / D)
    return (loss, grad_x[None], *[grads[k] for k in weights], *[deltas[k] for k in weights],
            *[new_m[k] for k in weights], *[new_v[k] for k in weights])
```

```python
import math

import numpy as np
import jax
import jax.numpy as jnp
from jax import lax
from jax.experimental import pallas as pl
from jax.experimental.pallas import tpu as pltpu

F32 = jnp.float32
BF16 = jnp.bfloat16

S = 2048
D = 2048
DA = 1024
DB = 1024
DFF = 8192
DIN = 9216
NH = 8
HD = 128
NBLK = 16
PATTERNS = ((128, 1), (512, 4), (2048, 16))
N_BUCKETS = 32
MAX_DISTANCE = 2048
ALPHA = 2.0 ** 0.25
LN_EPS = 1e-5
NEG_INF = -1e30
SCALE = HD ** -0.5
N_CHIPS = 4

ADAM_LR = 0.001
ADAM_B1 = 0.9
ADAM_B2 = 0.999
ADAM_EPS = 1e-08
ADAM_WD = 0.01
ADAM_STEP = 10

VMEM_LIMIT = 56 * 1024 * 1024
MESH = pl.DeviceIdType.MESH
ANY = pl.BlockSpec(memory_space=pl.ANY)


def _params(n_axes, vmem=VMEM_LIMIT):
    return pltpu.CompilerParams(dimension_semantics=("arbitrary",) * n_axes, vmem_limit_bytes=vmem)


def _bucket_tile(dilation):
    qi = np.arange(128)[:, None]
    kj = np.arange(256)[None, :]
    n = np.clip(128 + qi - kj, 0, 128) * dilation
    max_exact = N_BUCKETS // 2
    nf = np.maximum(n, 1).astype(np.float32)
    large = max_exact + (np.log(nf / np.float32(max_exact)) / np.float32(math.log(MAX_DISTANCE / max_exact))
                         * np.float32(N_BUCKETS - max_exact)).astype(np.int32)
    large = np.minimum(large, N_BUCKETS - 1)
    return np.where(n < max_exact, n, large).astype(np.int32)


def _gelu(x):
    c = math.sqrt(2.0 / math.pi)
    t = jnp.tanh(c * (x + 0.044715 * x * x * x))
    return 0.5 * x * (1.0 + t), t


def _gelu_grad(x, t):
    c = math.sqrt(2.0 / math.pi)
    return 0.5 * (1.0 + t) + 0.5 * x * (1.0 - t * t) * c * (1.0 + 3.0 * 0.044715 * x * x)


def _sigmoid(x):
    return 1.0 / (1.0 + jnp.exp(-x))


def _dot(a, b):
    return jnp.dot(a, b, preferred_element_type=F32)


def _behind(body, n_in, after):
    if after is None:
        return body, [], []
    return (lambda *refs: body(*refs[:n_in], *refs[n_in + 1:])), [ANY], [after]


def _dot_nt(a, b):
    return lax.dot_general(a, b, (((1,), (1,)), ((), ())), preferred_element_type=F32)


def _proj(xb, win_g, shards, name, into=None):
    tn = 768
    per = 2304 // tn

    def body(shards_ref, x_ref, w_ref, *rest):
        rest[-1][...] = _dot(x_ref[...], w_ref[...])

    in_specs = [pl.BlockSpec((S, D), lambda j, sh: (0, 0)),
                pl.BlockSpec((None, D, tn), lambda j, sh: (sh[j // per], 0, j % per))]
    return pl.pallas_call(
        body, name=name,
        grid_spec=pltpu.PrefetchScalarGridSpec(
            num_scalar_prefetch=1, grid=(shards.shape[0] * per,),
            in_specs=in_specs + ([ANY] if into is not None else []),
            out_specs=pl.BlockSpec((S, tn), lambda j, sh: (0, sh[j // per] * per + j % per))),
        out_shape=jax.ShapeDtypeStruct((S, DIN), F32),
        input_output_aliases={3: 0} if into is not None else {},
        compiler_params=_params(1),
    )(shards, xb, win_g, *([into] if into is not None else []))


FWD_HEADS_PER_STEP = 4
BWD_HEADS_PER_STEP = 2


def _bias_tiles(rel_bias):
    buckets = jnp.asarray(np.stack([_bucket_tile(d) for _, d in PATTERNS]))

    def body(rb_ref, bk_ref, o_ref):
        qi = lax.broadcasted_iota(jnp.int32, (128, 256), 0)
        kj = lax.broadcasted_iota(jnp.int32, (128, 256), 1)
        steps = 128 + qi - kj
        band = (steps >= 0) & (steps <= 128)
        o_ref[...] = jnp.zeros_like(o_ref)
        for p in range(len(PATTERNS)):
            bucket = bk_ref[p]

            def one_bucket(t, carry):
                hit = bucket == t
                for h in range(NH):
                    o_ref[p, h] = jnp.where(hit, rb_ref[t, h], o_ref[p, h])
                return carry

            lax.fori_loop(0, N_BUCKETS, one_bucket, 0)
            for h in range(NH):
                o_ref[p, h] = jnp.where(band, o_ref[p, h], NEG_INF)

    return pl.pallas_call(
        body, name="bias_tiles",
        in_specs=[pl.BlockSpec(memory_space=pltpu.SMEM), pl.BlockSpec(memory_space=pltpu.VMEM)],
        out_specs=pl.BlockSpec(memory_space=pltpu.VMEM),
        out_shape=jax.ShapeDtypeStruct((len(PATTERNS), NH, 128, 256), F32),
        compiler_params=pltpu.CompilerParams(vmem_limit_bytes=VMEM_LIMIT),
    )(rel_bias, buckets)


def _block_rows(b, dilation):
    nblk = NBLK // dilation
    r, n = b // nblk, b % nblk
    start = r + n * (128 * dilation)
    prev_start = jnp.maximum(start - 128 * dilation, r)
    if dilation == 1:
        return pl.ds(pl.multiple_of(start, 128), 128), pl.ds(pl.multiple_of(prev_start, 128), 128), n > 0
    return pl.ds(start, 128, stride=dilation), pl.ds(prev_start, 128, stride=dilation), n > 0


def _head_specs(first, hps):
    return [pl.BlockSpec((S, HD), lambda g, j=j: (0, first + g * hps + j)) for j in range(hps)]


def _bias_spec(hps):
    return pl.BlockSpec((len(PATTERNS), hps, 128, 256), lambda g: (0, g, 0, 0))


def _heads_spec(hps):
    return pl.BlockSpec((S, hps * HD), lambda g: (0, g))


def _attention_fwd(proj, bias):
    hps = FWD_HEADS_PER_STEP

    def body(bias_ref, *refs):
        q_refs, k_refs, v_refs = (refs[i * hps:(i + 1) * hps] for i in range(3))
        o_ref, lse_ref = refs[3 * hps:3 * hps + 2]
        acc_scrs, m_scrs, l_scrs = (refs[3 * hps + 2 + i * hps:3 * hps + 2 + (i + 1) * hps] for i in range(3))
        kj = lax.broadcasted_iota(jnp.int32, (128, 256), 1)
        for p, (_, d) in enumerate(PATTERNS):
            prev_blocks = NBLK // d > 1

            def block(b, carry):
                units = [(j,) + _block_rows(blk, d) for blk in (b, b + NBLK // 2) for j in range(hps)]
                scores = []
                for j, rows, prows, _ in units:
                    q = q_refs[j][rows, :].astype(BF16)
                    cur = _dot_nt(q, k_refs[j][rows, :].astype(BF16))
                    if prev_blocks:
                        cur = jnp.concatenate([_dot_nt(q, k_refs[j][prows, :].astype(BF16)), cur], axis=1)
                    scores.append(cur)
                soft = []
                for u, (j, _, _, has_prev) in enumerate(units):
                    if prev_blocks:
                        s = jnp.where((kj >= 128) | has_prev, scores[u] * SCALE + bias_ref[p, j], NEG_INF)
                    else:
                        s = scores[u] * SCALE + bias_ref[p, j, :, 128:256]
                    m = jnp.max(s, axis=1, keepdims=True)
                    e = jnp.exp(s - m)
                    soft.append((m, jnp.sum(e, axis=1, keepdims=True), e.astype(BF16)))
                outs = []
                for u, (j, rows, prows, _) in enumerate(units):
                    e = soft[u][2]
                    if prev_blocks:
                        outs.append(_dot(e[:, :128], v_refs[j][prows, :].astype(BF16))
                                    + _dot(e[:, 128:], v_refs[j][rows, :].astype(BF16)))
                    else:
                        outs.append(_dot(e, v_refs[j][rows, :].astype(BF16)))
                for u, (j, rows, _, _) in enumerate(units):
                    acc_scr, m_scr, l_scr = acc_scrs[j], m_scrs[j], l_scrs[j]
                    (m, den, _), o = soft[u], outs[u]
                    if p == 0:
                        acc_scr[rows, :] = o
                        m_scr[rows, :] = jnp.broadcast_to(m, (128, HD))
                        l_scr[rows, :] = jnp.broadcast_to(den, (128, HD))
                    else:
                        m_old = m_scr[rows, :]
                        m_new = jnp.maximum(m_old, m)
                        w_old, w_new = jnp.exp(m_old - m_new), jnp.exp(m - m_new)
                        acc_scr[rows, :] = acc_scr[rows, :] * w_old + o * w_new
                        l_scr[rows, :] = l_scr[rows, :] * w_old + den * w_new
                        m_scr[rows, :] = m_new
                return carry

            lax.fori_loop(0, NBLK // 2, block, 0)
        for j in range(hps):
            cols = slice(j * HD, (j + 1) * HD)
            den = l_scrs[j][...]
            o_ref[:, cols] = (acc_scrs[j][...] / den).astype(BF16)
            lse_ref[:, cols] = m_scrs[j][...] + jnp.log(den)

    return pl.pallas_call(
        body, name="attention_fwd", grid=(NH // hps,),
        in_specs=[_bias_spec(hps)] + _head_specs(0, hps) + _head_specs(NH, hps) + _head_specs(2 * NH, hps),
        out_specs=[_heads_spec(hps), _heads_spec(hps)],
        out_shape=[jax.ShapeDtypeStruct((S, DA), BF16), jax.ShapeDtypeStruct((S, DA), F32)],
        scratch_shapes=[pltpu.VMEM((S, HD), F32)] * (3 * hps),
        compiler_params=_params(1),
    )(bias, *([proj] * (3 * hps)))


def _attention_bwd(proj, dattn, attn, lse, bias, after=None):
    hps = BWD_HEADS_PER_STEP

    def body(bias_ref, *refs):
        q_refs, k_refs, v_refs, do_refs, o_refs, lse_refs = (refs[i * hps:(i + 1) * hps] for i in range(6))
        dq_ref, dk_ref, dv_ref, ds_ref = refs[6 * hps:6 * hps + 4]
        dl_scrs, dq_scrs, dk_scrs, dv_scrs = (refs[6 * hps + 4 + i * hps:6 * hps + 4 + (i + 1) * hps] for i in range(4))
        ds_ref[...] = jnp.zeros_like(ds_ref)
        for j in range(hps):
            dq_scrs[j][...] = jnp.zeros((S, HD), F32)
            dk_scrs[j][...] = jnp.zeros((S, HD), F32)
            dv_scrs[j][...] = jnp.zeros((S, HD), F32)
            prod = do_refs[j][...] * o_refs[j][...].astype(F32)
            dl_scrs[j][...] = jnp.broadcast_to(jnp.sum(prod, axis=1, keepdims=True), (S, HD))
        for p, (_, d) in enumerate(PATTERNS):
            prev_blocks = NBLK // d > 1

            def block(b, carry):
                units = [(j,) + _block_rows(b + i * (NBLK // 4), d) for i in range(4) for j in range(hps)]
                ops, raw = [], []
                for j, rows, prows, _ in units:
                    q, do = q_refs[j][rows, :].astype(BF16), do_refs[j][rows, :].astype(BF16)
                    kc, vc = k_refs[j][rows, :].astype(BF16), v_refs[j][rows, :].astype(BF16)
                    if prev_blocks:
                        kp, vp = k_refs[j][prows, :].astype(BF16), v_refs[j][prows, :].astype(BF16)
                        ops.append((q, do, kc, kp))
                        raw.append((_dot_nt(q, kc), _dot_nt(do, vc), _dot_nt(q, kp), _dot_nt(do, vp)))
                    else:
                        ops.append((q, do, kc))
                        raw.append((_dot_nt(q, kc), _dot_nt(do, vc)))
                probs = []
                for u, (j, rows, _, has_prev) in enumerate(units):
                    lse_b, dl_b = lse_refs[j][rows, :], dl_scrs[j][rows, :]
                    p_c = jnp.exp(raw[u][0] * SCALE + bias_ref[p, j, :, 128:256] - lse_b)
                    ds_c = p_c * (raw[u][1] - dl_b)
                    ds_ref[p, j, :, 128:256] += ds_c
                    if prev_blocks:
                        p_p = jnp.where(has_prev, jnp.exp(raw[u][2] * SCALE + bias_ref[p, j, :, 0:128] - lse_b), 0.0)
                        ds_p = p_p * (raw[u][3] - dl_b)
                        ds_ref[p, j, :, 0:128] += ds_p
                        probs.append((p_c, ds_c, p_p, ds_p))
                    else:
                        probs.append((p_c, ds_c))
                grads = []
                for u in range(len(units)):
                    q, do, kc = ops[u][:3]
                    p_c, ds_c = probs[u][:2]
                    dq = _dot(ds_c.astype(BF16), kc)
                    cur = (_dot(ds_c.T.astype(BF16), q) * SCALE, _dot(p_c.T.astype(BF16), do))
                    if prev_blocks:
                        p_p, ds_p = probs[u][2:]
                        dq = dq + _dot(ds_p.astype(BF16), ops[u][3])
                        cur = cur + (_dot(ds_p.T.astype(BF16), q) * SCALE, _dot(p_p.T.astype(BF16), do))
                    grads.append((dq * SCALE,) + cur)
                for u, (j, rows, prows, _) in enumerate(units):
                    dq_scrs[j][rows, :] += grads[u][0]
                    dk_scrs[j][rows, :] += grads[u][1]
                    dv_scrs[j][rows, :] += grads[u][2]
                    if prev_blocks:
                        dk_scrs[j][prows, :] += grads[u][3]
                        dv_scrs[j][prows, :] += grads[u][4]
                return carry

            lax.fori_loop(0, NBLK // 4, block, 0)
        for j in range(hps):
            cols = slice(j * HD, (j + 1) * HD)
            dq_ref[:, cols] = dq_scrs[j][...].astype(BF16)
            dk_ref[:, cols] = dk_scrs[j][...].astype(BF16)
            dv_ref[:, cols] = dv_scrs[j][...].astype(BF16)

    body, more_specs, more = _behind(body, 1 + 6 * hps, after)
    return pl.pallas_call(
        body, name="attention_bwd", grid=(NH // hps,),
        in_specs=[_bias_spec(hps)]
        + _head_specs(0, hps) + _head_specs(NH, hps) + _head_specs(2 * NH, hps) + 3 * _head_specs(0, hps)
        + more_specs,
        out_specs=3 * [_heads_spec(hps)] + [pl.BlockSpec((3, hps, 128, 256), lambda g: (0, g, 0, 0))],
        out_shape=[jax.ShapeDtypeStruct((S, DA), BF16)] * 3 + [jax.ShapeDtypeStruct((3, NH, 128, 256), F32)],
        scratch_shapes=[pltpu.VMEM((S, HD), F32)] * (4 * hps),
        compiler_params=_params(1),
    )(bias, *([proj] * (3 * hps)), *([dattn] * hps), *([attn] * hps), *([lse] * hps), *more)


def _gmlp_parts(u_ref, vb_ref, g_ref, be_ref):
    u = u_ref[...]
    u_act, tu = _gelu(u)
    vb = vb_ref[...]
    gv, tv = _gelu(vb)
    mean = jnp.mean(gv, axis=1, keepdims=True)
    cen = gv - mean
    var = jnp.mean(cen * cen, axis=1, keepdims=True)
    rstd = lax.rsqrt(var + LN_EPS)
    xhat = cen * rstd
    vn = xhat * g_ref[...] + be_ref[...]
    return u, tu, u_act, vb, tv, rstd, xhat, vn


def _gmlp_fwd(proj, ws, bsp_b, gain_v, bias_v):
    def body(u_ref, vb_ref, ws_ref, bsp_ref, g_ref, be_ref, o_ref):
        _, _, u_act, _, _, _, _, vn = _gmlp_parts(u_ref, vb_ref, g_ref, be_ref)
        row = lax.broadcasted_iota(jnp.int32, (128, 128), 0)
        col = lax.broadcasted_iota(jnp.int32, (128, 128), 1)
        causal = row >= col
        for g in range(NH):
            cols = slice(g * 128, (g + 1) * 128)
            wsg = jnp.where(causal, ws_ref[g], 0.0).astype(BF16)
            z = _dot(wsg, vn[:, cols].astype(BF16)) + bsp_ref[g]
            o_ref[:, cols] = (u_act[:, cols] * z).astype(BF16)

    return pl.pallas_call(
        body, name="gmlp_fwd", grid=(NBLK,),
        in_specs=[pl.BlockSpec((128, DB), lambda c: (c, 3)), pl.BlockSpec((128, DB), lambda c: (c, 4)),
                  pl.BlockSpec((NH, 128, 128), lambda c: (0, 0, 0)), pl.BlockSpec((NH, 128, 128), lambda c: (0, 0, 0)),
                  pl.BlockSpec((1, DB), lambda c: (0, 0)), pl.BlockSpec((1, DB), lambda c: (0, 0))],
        out_specs=pl.BlockSpec((128, DB), lambda c: (c, 0)),
        out_shape=jax.ShapeDtypeStruct((S, DB), BF16),
        compiler_params=_params(1),
    )(proj, proj, ws, bsp_b, gain_v, bias_v)


def _branch(attn, gmlp, wpa_g, wpb_g, proj):
    tn = 512

    def body(a_ref, g_ref, wa_ref, wb_ref, ga_ref, gb_ref, ya_ref, yb_ref, mg_ref):
        ya = _dot(a_ref[...], wa_ref[...])
        yb = _dot(g_ref[...], wb_ref[...])
        ya_ref[...] = ya.astype(BF16)
        yb_ref[...] = yb.astype(BF16)
        mg_ref[...] = (_sigmoid(ga_ref[...]) * ya + _sigmoid(gb_ref[...]) * yb).astype(BF16)

    out = pl.BlockSpec((S, tn), lambda j: (0, j))
    return pl.pallas_call(
        body, name="branch", grid=(D // tn,),
        in_specs=[pl.BlockSpec((S, DA), lambda j: (0, 0)), pl.BlockSpec((S, DB), lambda j: (0, 0)),
                  pl.BlockSpec((None, DA, tn), lambda j: (j, 0, 0)), pl.BlockSpec((None, DB, tn), lambda j: (j, 0, 0)),
                  pl.BlockSpec((S, tn), lambda j: (0, 5120 // tn + j)), pl.BlockSpec((S, tn), lambda j: (0, 7168 // tn + j))],
        out_specs=[out, out, out],
        out_shape=[jax.ShapeDtypeStruct((S, D), BF16)] * 3,
        compiler_params=_params(1),
    )(attn, gmlp, wpa_g, wpb_g, proj, proj)


def _out_ln1(merged, wout_g, x, gain, bias):
    tm = 256

    def body(m_ref, w_ref, x_ref, g_ref, b_ref, xh_ref, rs_ref, h_ref):
        pre = ALPHA * x_ref[...] + _dot(m_ref[...], w_ref[...])
        mean = jnp.mean(pre, axis=1, keepdims=True)
        cen = pre - mean
        var = jnp.mean(cen * cen, axis=1, keepdims=True)
        rstd = lax.rsqrt(var + LN_EPS)
        xhat = cen * rstd
        xh_ref[...] = xhat
        rs_ref[...] = jnp.broadcast_to(rstd, (tm, 128))
        h_ref[...] = (xhat * g_ref[...] + b_ref[...]).astype(BF16)

    row = pl.BlockSpec((tm, D), lambda i: (i, 0))
    vec = pl.BlockSpec((1, D), lambda i: (0, 0))
    return pl.pallas_call(
        body, name="out_ln1", grid=(S // tm,),
        in_specs=[row, pl.BlockSpec((D, D), lambda i: (0, 0)), row, vec, vec],
        out_specs=[row, pl.BlockSpec((tm, 128), lambda i: (i, 0)), row],
        out_shape=[jax.ShapeDtypeStruct((S, D), F32), jax.ShapeDtypeStruct((S, 128), F32),
                   jax.ShapeDtypeStruct((S, D), BF16)],
        compiler_params=_params(1),
    )(merged, wout_g, x, gain, bias)


def _ff1(h1b, w1_g, b1, half, name, into=None, after=None):
    tn = 512
    per = D // tn
    steps = DFF // tn // 2
    first = half * steps

    def body(h_ref, w_ref, b_ref, *rest):
        a_ref, r_ref = rest[-2:]
        r = jnp.maximum(_dot(h_ref[...], w_ref[...]) + b_ref[...], 0.0)
        r_ref[...] = r.astype(BF16)
        a_ref[...] = (r * r).astype(BF16)

    out = pl.BlockSpec((S, tn), lambda j: (0, first + j))
    extra = list(into) if into is not None else []
    if after is not None:
        extra.append(after)
    return pl.pallas_call(
        body, name=name, grid=(steps,),
        in_specs=[pl.BlockSpec((S, D), lambda j: (0, 0)),
                  pl.BlockSpec((None, D, tn), lambda j: ((first + j) // per, 0, (first + j) % per)),
                  pl.BlockSpec((1, tn), lambda j: (0, first + j))] + [ANY] * len(extra),
        out_specs=[out, out],
        out_shape=[jax.ShapeDtypeStruct((S, DFF), BF16)] * 2,
        input_output_aliases={3: 0, 4: 1} if into is not None else {},
        compiler_params=_params(1),
    )(h1b, w1_g, b1, *extra)


def _ff2_ln2_loss(a, w2_g, xhat1, g1, b1, b2, g2, be2, target):
    tm, tk = 512, 1024
    nk = DFF // tk

    def body(a_ref, w_ref, xh_ref, g1_ref, b1_ref, b2_ref, g2_ref, be2_ref, t_ref, d_ref, db_ref, st_ref, acc):
        i, k = pl.program_id(0), pl.program_id(1)

        @pl.when(k == 0)
        def _():
            acc[...] = jnp.zeros_like(acc)

        @pl.when((i == 0) & (k == 0))
        def _():
            st_ref[...] = jnp.zeros_like(st_ref)

        acc[...] += _dot(a_ref[...], w_ref[...])

        @pl.when(k == nk - 1)
        def _():
            def rows_chunk(ci, carry):
                rows = pl.ds(pl.multiple_of(ci * 128, 128), 128)
                pre = xh_ref[rows, :] * (ALPHA * g1_ref[...]) + (acc[rows, :] + (ALPHA * b1_ref[...] + b2_ref[...]))
                mean = jnp.mean(pre, axis=1, keepdims=True)
                cen = pre - mean
                var = jnp.mean(cen * cen, axis=1, keepdims=True)
                rstd = lax.rsqrt(var + LN_EPS)
                xhat = cen * rstd
                y = xhat * g2_ref[...] + be2_ref[...]
                err = y - t_ref[rows, :]
                g = err * (g2_ref[...] * (1.0 / D))
                dpre = rstd * (g - jnp.mean(g, axis=1, keepdims=True)
                               - xhat * jnp.mean(g * xhat, axis=1, keepdims=True))
                d_ref[rows, :] = dpre
                db_ref[rows, :] = dpre.astype(BF16)
                st_ref[0:1, :] += jnp.sum(err * xhat, axis=0, keepdims=True) * (1.0 / D)
                st_ref[1:2, :] += jnp.sum(err, axis=0, keepdims=True) * (1.0 / D)
                st_ref[2:3, :] += jnp.sum(dpre, axis=0, keepdims=True)
                st_ref[3:4, :] += jnp.broadcast_to(jnp.sum(err * err).reshape(1, 1), (1, D))
                return carry

            lax.fori_loop(0, tm // 128, rows_chunk, 0)

    row = pl.BlockSpec((tm, D), lambda i, k: (i, 0))
    vec = pl.BlockSpec((1, D), lambda i, k: (0, 0))
    return pl.pallas_call(
        body, name="ff2_ln2_loss", grid=(S // tm, nk),
        in_specs=[pl.BlockSpec((tm, tk), lambda i, k: (i, k)), pl.BlockSpec((tk, D), lambda i, k: (k, 0)),
                  row, vec, vec, vec, vec, vec, row],
        out_specs=[row, row, pl.BlockSpec((8, D), lambda i, k: (0, 0))],
        out_shape=[jax.ShapeDtypeStruct((S, D), F32), jax.ShapeDtypeStruct((S, D), BF16),
                   jax.ShapeDtypeStruct((8, D), F32)],
        scratch_shapes=[pltpu.VMEM((tm, D), F32)],
        compiler_params=_params(2),
    )(a, w2_g, xhat1, g1, b1, b2, g2, be2, target)


def _grad_w(act, dout, name, ti, tj, sharded, after=None):
    m, n = act.shape[1], dout.shape[1]
    ns = n // N_CHIPS
    per = ns // tj if sharded else None

    def body(a_ref, b_ref, o_ref, at_scr):
        @pl.when(pl.program_id(1) == 0)
        def _():
            at_scr[...] = a_ref[...].T

        o_ref[...] = _dot(at_scr[...], b_ref[...]).astype(BF16)

    if sharded:
        out_spec = pl.BlockSpec((None, ti, tj), lambda i, j: (j // per, i, j % per))
        out_shape = jax.ShapeDtypeStruct((N_CHIPS, m, ns), BF16)
    else:
        out_spec = pl.BlockSpec((ti, tj), lambda i, j: (i, j))
        out_shape = jax.ShapeDtypeStruct((m, n), BF16)
    body, more_specs, more = _behind(body, 2, after)
    return pl.pallas_call(
        body, name=name, grid=(m // ti, n // tj),
        in_specs=[pl.BlockSpec((S, ti), lambda i, j: (0, i)), pl.BlockSpec((S, tj), lambda i, j: (0, j))] + more_specs,
        out_specs=out_spec, out_shape=out_shape,
        scratch_shapes=[pltpu.VMEM((ti, S), BF16)],
        compiler_params=_params(2),
    )(act, dout, *more)


def _d_ff1(dpre2b, w2_g, r, after=None):
    tn = 512

    def body(d_ref, w_ref, r_ref, o_ref, gb_ref):
        da = _dot_nt(d_ref[...], w_ref[...])
        dp = da * (2.0 * r_ref[...].astype(F32))
        o_ref[...] = dp.astype(BF16)
        gb_ref[...] = jnp.sum(dp, axis=0, keepdims=True)

    body, more_specs, more = _behind(body, 3, after)
    return pl.pallas_call(
        body, name="d_ff1", grid=(DFF // tn,),
        in_specs=[pl.BlockSpec((S, D), lambda j: (0, 0)), pl.BlockSpec((tn, D), lambda j: (j, 0)),
                  pl.BlockSpec((S, tn), lambda j: (0, j))] + more_specs,
        out_specs=[pl.BlockSpec((S, tn), lambda j: (0, j)), pl.BlockSpec((1, tn), lambda j: (0, j))],
        out_shape=[jax.ShapeDtypeStruct((S, DFF), BF16), jax.ShapeDtypeStruct((1, DFF), F32)],
        compiler_params=_params(1),
    )(dpre2b, w2_g, r, *more)


def _d_h1_ln1(dprea, w1_g, dpre2, xhat1, rstd1, g1, after=None):
    tm, tk = 512, 1024
    per = D // tk
    nk = DFF // tk

    def body(a_ref, w_ref, d2_ref, xh_ref, rs_ref, g_ref, d_ref, db_ref, st_ref, acc):
        i, k = pl.program_id(0), pl.program_id(1)

        @pl.when(k == 0)
        def _():
            acc[...] = jnp.zeros_like(acc)

        @pl.when((i == 0) & (k == 0))
        def _():
            st_ref[...] = jnp.zeros_like(st_ref)

        acc[...] += _dot_nt(a_ref[...], w_ref[...])

        @pl.when(k == nk - 1)
        def _():
            def rows_chunk(ci, carry):
                rows = pl.ds(pl.multiple_of(ci * 128, 128), 128)
                dh = ALPHA * d2_ref[rows, :] + acc[rows, :]
                xhat = xh_ref[rows, :]
                g = dh * g_ref[...]
                dpre = rs_ref[rows, 0:1] * (g - jnp.mean(g, axis=1, keepdims=True)
                                            - xhat * jnp.mean(g * xhat, axis=1, keepdims=True))
                d_ref[rows, :] = dpre
                db_ref[rows, :] = dpre.astype(BF16)
                st_ref[0:1, :] += jnp.sum(dh * xhat, axis=0, keepdims=True)
                st_ref[1:2, :] += jnp.sum(dh, axis=0, keepdims=True)
                return carry

            lax.fori_loop(0, tm // 128, rows_chunk, 0)

    row = pl.BlockSpec((tm, D), lambda i, k: (i, 0))
    body, more_specs, more = _behind(body, 6, after)
    return pl.pallas_call(
        body, name="d_h1_ln1", grid=(S // tm, nk),
        in_specs=[pl.BlockSpec((tm, tk), lambda i, k: (i, k)),
                  pl.BlockSpec((None, D, tk), lambda i, k: (k // per, 0, k % per)),
                  row, row, pl.BlockSpec((tm, 128), lambda i, k: (i, 0)), pl.BlockSpec((1, D), lambda i, k: (0, 0))]
        + more_specs,
        out_specs=[row, row, pl.BlockSpec((8, D), lambda i, k: (0, 0))],
        out_shape=[jax.ShapeDtypeStruct((S, D), F32), jax.ShapeDtypeStruct((S, D), BF16),
                   jax.ShapeDtypeStruct((8, D), F32)],
        scratch_shapes=[pltpu.VMEM((tm, D), F32)],
        compiler_params=_params(2),
    )(dprea, w1_g, dpre2, xhat1, rstd1, g1, *more)


def _d_merged(dpre1b, wout_g, proj, ya, yb):
    tm, tn = 512, 1024

    def body(d_ref, w_ref, ga_ref, gb_ref, ya_ref, yb_ref, dya_ref, dyb_ref, dga_ref, dgb_ref):
        dm = _dot_nt(d_ref[...], w_ref[...])
        sa = _sigmoid(ga_ref[...])
        sb = _sigmoid(gb_ref[...])
        dya_ref[...] = (dm * sa).astype(BF16)
        dyb_ref[...] = (dm * sb).astype(BF16)
        dga_ref[...] = (dm * ya_ref[...].astype(F32) * sa * (1.0 - sa)).astype(BF16)
        dgb_ref[...] = (dm * yb_ref[...].astype(F32) * sb * (1.0 - sb)).astype(BF16)

    tile = pl.BlockSpec((tm, tn), lambda j, i: (i, j))
    return pl.pallas_call(
        body, name="d_merged", grid=(D // tn, S // tm),
        in_specs=[pl.BlockSpec((tm, D), lambda j, i: (i, 0)), pl.BlockSpec((tn, D), lambda j, i: (j, 0)),
                  pl.BlockSpec((tm, tn), lambda j, i: (i, 5 + j)), pl.BlockSpec((tm, tn), lambda j, i: (i, 7 + j)),
                  tile, tile],
        out_specs=[tile] * 4,
        out_shape=[jax.ShapeDtypeStruct((S, D), BF16)] * 4,
        compiler_params=_params(2),
    )(dpre1b, wout_g, proj, proj, ya, yb)


def _d_branches(dya, dyb, wpa_g, wpb_g, after=None):
    tm = 512
    ws = D // N_CHIPS

    def body(da_ref, db_ref, wa_ref, wb_ref, oa_ref, ob_ref):
        for d_ref, w_ref, o_ref in ((da_ref, wa_ref, oa_ref), (db_ref, wb_ref, ob_ref)):
            acc = _dot_nt(d_ref[:, 0:ws], w_ref[0])
            for s in range(1, N_CHIPS):
                acc = acc + _dot_nt(d_ref[:, s * ws:(s + 1) * ws], w_ref[s])
            o_ref[...] = acc

    rows = lambda width: pl.BlockSpec((tm, width), lambda i: (i, 0))
    whole = lambda n: pl.BlockSpec((N_CHIPS, n, ws), lambda i: (0, 0, 0))
    body, more_specs, more = _behind(body, 4, after)
    return pl.pallas_call(
        body, name="d_branches", grid=(S // tm,),
        in_specs=[rows(D), rows(D), whole(DA), whole(DB)] + more_specs,
        out_specs=[rows(DA), rows(DB)],
        out_shape=[jax.ShapeDtypeStruct((S, DA), F32), jax.ShapeDtypeStruct((S, DB), F32)],
        compiler_params=_params(1),
    )(dya, dyb, wpa_g, wpb_g, *more)


def _gmlp_bwd(proj, dgmlp, ws, ws_t, bsp_b, gain_v, bias_v):
    def body(u_ref, vb_ref, dg_ref, ws_ref, wst_ref, bsp_ref, g_ref, be_ref, duv_ref, gws_ref, gbs_ref, st_ref):
        @pl.when(pl.program_id(0) == 0)
        def _():
            gws_ref[...] = jnp.zeros_like(gws_ref)
            gbs_ref[...] = jnp.zeros_like(gbs_ref)
            st_ref[...] = jnp.zeros_like(st_ref)

        u, tu, u_act, vb, tv, rstd, xhat, vn = _gmlp_parts(u_ref, vb_ref, g_ref, be_ref)
        dg = dg_ref[...]
        dz = dg * u_act
        row = lax.broadcasted_iota(jnp.int32, (128, 128), 0)
        col = lax.broadcasted_iota(jnp.int32, (128, 128), 1)
        causal = row >= col
        causal_t = row <= col
        dvn_parts = []
        z_parts = []
        for g in range(NH):
            cols = slice(g * 128, (g + 1) * 128)
            vng = vn[:, cols].astype(BF16)
            dzg = dz[:, cols]
            dzb = dzg.astype(BF16)
            wsg = jnp.where(causal, ws_ref[g], 0.0).astype(BF16)
            wsg_t = jnp.where(causal_t, wst_ref[g], 0.0).astype(BF16)
            z_parts.append(_dot(wsg, vng) + bsp_ref[g])
            gws_ref[g] += jnp.where(causal, _dot_nt(dzb, vng), 0.0)
            gbs_ref[g] += jnp.broadcast_to(jnp.sum(dzg, axis=1, keepdims=True), (128, 128))
            dvn_parts.append(_dot(wsg_t, dzb))
        z = jnp.concatenate(z_parts, axis=1)
        dvn = jnp.concatenate(dvn_parts, axis=1)
        du = dg * z * _gelu_grad(u, tu)
        st_ref[0:1, :] += jnp.sum(dvn * xhat, axis=0, keepdims=True)
        st_ref[1:2, :] += jnp.sum(dvn, axis=0, keepdims=True)
        gg = dvn * g_ref[...]
        dgv = rstd * (gg - jnp.mean(gg, axis=1, keepdims=True) - xhat * jnp.mean(gg * xhat, axis=1, keepdims=True))
        dvb = dgv * _gelu_grad(vb, tv)
        duv_ref[:, 0:DB] = du.astype(BF16)
        duv_ref[:, DB:2 * DB] = dvb.astype(BF16)

    full3 = pl.BlockSpec((NH, 128, 128), lambda c: (0, 0, 0))
    vec = pl.BlockSpec((1, DB), lambda c: (0, 0))
    return pl.pallas_call(
        body, name="gmlp_bwd", grid=(NBLK,),
        in_specs=[pl.BlockSpec((128, DB), lambda c: (c, 3)), pl.BlockSpec((128, DB), lambda c: (c, 4)),
                  pl.BlockSpec((128, DB), lambda c: (c, 0)), full3, full3, full3, vec, vec],
        out_specs=[pl.BlockSpec((128, 2 * DB), lambda c: (c, 0)), full3, full3, pl.BlockSpec((8, DB), lambda c: (0, 0))],
        out_shape=[jax.ShapeDtypeStruct((S, 2 * DB), BF16), jax.ShapeDtypeStruct((NH, 128, 128), F32),
                   jax.ShapeDtypeStruct((NH, 128, 128), F32), jax.ShapeDtypeStruct((8, DB), F32)],
        compiler_params=_params(1),
    )(proj, proj, dgmlp, ws, ws_t, bsp_b, gain_v, bias_v)


def _rel_bias_grad(ds_sums):
    buckets = jnp.asarray(np.stack([_bucket_tile(d) for _, d in PATTERNS]))

    def body(bk_ref, ds_ref, o_ref):
        row = lax.broadcasted_iota(jnp.int32, (N_BUCKETS, 128), 0)
        lane = lax.broadcasted_iota(jnp.int32, (N_BUCKETS, 128), 1)

        def one_bucket(t, out):
            hits = [bk_ref[p] == t for p in range(3)]
            for h in range(NH):
                tot = jnp.zeros((128, 256), F32)
                for p in range(3):
                    tot = tot + jnp.where(hits[p], ds_ref[p, h], 0.0)
                out = jnp.where((row == t) & (lane == h), jnp.sum(tot), out)
            return out

        o_ref[...] = lax.fori_loop(0, N_BUCKETS, one_bucket, jnp.zeros((N_BUCKETS, 128), F32))

    return pl.pallas_call(
        body, name="rel_bias_grad",
        in_specs=[pl.BlockSpec(memory_space=pltpu.VMEM)] * 2, out_specs=pl.BlockSpec(memory_space=pltpu.VMEM),
        out_shape=jax.ShapeDtypeStruct((N_BUCKETS, 128), F32),
        compiler_params=pltpu.CompilerParams(vmem_limit_bytes=VMEM_LIMIT),
    )(buckets, ds_sums)


def _d_x(dproj, win_g, dpre1, after=None):
    tm, tn = 512, 512
    ws = DIN // N_CHIPS

    def body(a_ref, w_ref, d_ref, o_ref):
        acc = ALPHA * d_ref[...]
        for s in range(N_CHIPS):
            acc = acc + _dot_nt(a_ref[:, s * ws:(s + 1) * ws], w_ref[s])
        o_ref[...] = acc

    tile = pl.BlockSpec((tm, tn), lambda i, j: (i, j))
    body, more_specs, more = _behind(body, 3, after)
    return pl.pallas_call(
        body, name="d_x", grid=(S // tm, D // tn),
        in_specs=[pl.BlockSpec((tm, DIN), lambda i, j: (i, 0)),
                  pl.BlockSpec((N_CHIPS, tn, ws), lambda i, j: (0, j, 0)), tile] + more_specs,
        out_specs=tile, out_shape=jax.ShapeDtypeStruct((S, D), F32),
        compiler_params=_params(2),
    )(dproj, win_g, dpre1, *more)


def _adamw(w, g, m, v, name, after=None):
    rows, cols = w.shape
    tm = max(t for t in range(8, 257, 8) if rows % t == 0)

    def body(w_ref, g_ref, m_ref, v_ref, d_ref, nm_ref, nv_ref, go_ref):
        g = g_ref[...]
        m = ADAM_B1 * m_ref[...] + (1.0 - ADAM_B1) * g
        v = ADAM_B2 * v_ref[...] + (1.0 - ADAM_B2) * (g * g)
        m_hat = m / (1.0 - ADAM_B1 ** ADAM_STEP)
        v_hat = v / (1.0 - ADAM_B2 ** ADAM_STEP)
        d_ref[...] = -ADAM_LR * (m_hat / (jnp.sqrt(v_hat) + ADAM_EPS) + ADAM_WD * w_ref[...])
        nm_ref[...] = m
        nv_ref[...] = v
        go_ref[...] = g

    spec = pl.BlockSpec((tm, cols), lambda i: (i, 0))
    body, more_specs, more = _behind(body, 4, after)
    return pl.pallas_call(
        body, name=name, grid=(rows // tm,), in_specs=[spec] * 4 + more_specs, out_specs=[spec] * 4,
        out_shape=[jax.ShapeDtypeStruct((rows, cols), F32)] * 4, compiler_params=_params(1),
    )(w, g, m, v, *more)


def _position():
    x, y, c = lax.axis_index("x"), lax.axis_index("y"), lax.axis_index("c")
    chips = [(1 - x, y), (x, 1 - y), (1 - x, 1 - y)]
    return x, y, c, chips


def _remote(src, dst, send_sems, recv_sems, k, to):
    return pltpu.make_async_remote_copy(src_ref=src, dst_ref=dst, send_sem=send_sems.at[k], recv_sem=recv_sems.at[k],
                                        device_id=to, device_id_type=MESH)


def _place_shard(w, name, after=None):
    rows, cols = w.shape
    tm = 256
    x, y = lax.axis_index("x"), lax.axis_index("y")

    def body(chip_ref, w_ref, o_ref):
        o_ref[...] = w_ref[...].astype(BF16)

    more_specs, more = ([ANY], [after]) if after is not None else ([], [])
    if after is not None:
        inner = body
        body = lambda chip_ref, w_ref, after_ref, o_ref: inner(chip_ref, w_ref, o_ref)
    return pl.pallas_call(
        body, name=name,
        grid_spec=pltpu.PrefetchScalarGridSpec(
            num_scalar_prefetch=1, grid=(rows // tm,),
            in_specs=[pl.BlockSpec((tm, cols), lambda i, chip: (i, 0))] + more_specs,
            out_specs=pl.BlockSpec((None, tm, cols), lambda i, chip: (chip[0], i, 0))),
        out_shape=jax.ShapeDtypeStruct((N_CHIPS, rows, cols), BF16),
        compiler_params=_params(1),
    )(jnp.reshape(2 * x + y, (1,)).astype(jnp.int32), w, *more)


def _to_bf16(x, name, after=None):
    tm = 256

    def body(x_ref, o_ref):
        o_ref[...] = x_ref[...].astype(BF16)

    spec = pl.BlockSpec((tm, x.shape[1]), lambda i: (i, 0))
    body, more_specs, more = _behind(body, 1, after)
    return pl.pallas_call(
        body, name=name, grid=(x.shape[0] // tm,), in_specs=[spec] + more_specs, out_specs=spec,
        out_shape=jax.ShapeDtypeStruct(x.shape, BF16), compiler_params=_params(1),
    )(x, *more)


HBM = pl.BlockSpec(memory_space=pltpu.HBM)
SEM = pl.BlockSpec(memory_space=pltpu.SEMAPHORE)
EFFECT = pltpu.SideEffectType.DATAFLOW_SIDE_EFFECTING


def _comm_call(name, body, bufs, sems_in, sems_out, after=None, token=False):
    nb, ns, no = len(bufs), len(sems_in), len(sems_out)
    n_in = nb + ns + (after is not None)

    def wrapped(*refs):
        body(refs[:nb], refs[nb:nb + ns], refs[n_in + nb:n_in + nb + no])
        if token:
            refs[-1][...] = jnp.zeros((8, 128), F32)

    outs = pl.pallas_call(
        wrapped, name=name,
        in_specs=[HBM] * nb + [SEM] * ns + ([ANY] if after is not None else []),
        out_specs=[HBM] * nb + [SEM] * no + ([pl.BlockSpec(memory_space=pltpu.VMEM)] if token else []),
        out_shape=[pltpu.HBM(b.shape, b.dtype) for b in bufs] + [pltpu.SemaphoreType.DMA((k,)) for k in sems_out]
        + ([jax.ShapeDtypeStruct((8, 128), F32)] if token else []),
        input_output_aliases={i: i for i in range(nb)},
        compiler_params=pltpu.CompilerParams(has_side_effects=EFFECT),
    )(*[pltpu.with_memory_space_constraint(b, pltpu.HBM) for b in bufs], *sems_in, *([after] if after is not None else []))
    return list(outs[:nb]), list(outs[nb:nb + no]), (outs[-1] if token else None)


RING_STAGES = {"ici_near": 2, "ici_far": 2, "d2d_near": 2, "d2d_far": 1}


def _ring_copies(buf, send_sems, recv_sems, k0, stage):
    x, y, c, _ = _position()
    hr = buf.shape[1] // 2
    qr = hr // 2
    half = lambda chip, h: buf.at[chip, pl.ds(h * hr, hr), :]
    quarter = lambda chip, h, q: buf.at[chip, pl.ds(h * hr + q * qr, qr), :]
    mine, x_chip, y_chip, far_chip = 2 * x + y, 2 * (1 - x) + y, 2 * x + (1 - y), 2 * (1 - x) + (1 - y)
    to_x, to_y, sibling = (1 - x, y, c), (x, 1 - y, c), (x, y, 1 - c)
    if stage == "ici_near":
        moves = [(half(mine, c), to_x, half(x_chip, c)), (half(mine, c), to_y, half(y_chip, c))]
    elif stage == "ici_far":
        moves = [(quarter(x_chip, c, 0), to_y, quarter(far_chip, c, 0)),
                 (quarter(y_chip, c, 1), to_x, quarter(far_chip, c, 1))]
    elif stage == "d2d_near":
        moves = [(half(x_chip, c), sibling, half(x_chip, 1 - c)), (half(y_chip, c), sibling, half(y_chip, 1 - c))]
    else:
        moves = [(half(far_chip, c), sibling, half(far_chip, 1 - c))]
    sends = [_remote(src, src, send_sems, recv_sems, k0 + i, to) for i, (src, to, _) in enumerate(moves)]
    arrivals = [_remote(got, got, send_sems, recv_sems, k0 + i, (x, y, c)) for i, (_, _, got) in enumerate(moves)]
    return sends, arrivals


def _ring_call(name, groups, actions, after=None):
    tags = list(dict.fromkeys(t for _, t, _ in actions))
    counts = {t: len(groups[t]["bufs"]) for t in tags}
    first = {t: sum(counts[u] for u in tags[:i]) for i, t in enumerate(tags)}
    waits = [(t, s) for v, t, s in actions if v == "wait"]
    starts = [(t, s) for v, t, s in actions if v == "start"]

    def body(bufs, sems_in, sems_out):
        for verb, t, s in actions:
            at, sems = (starts.index((t, s)), sems_out) if verb == "start" else (waits.index((t, s)), sems_in)
            for w in range(counts[t]):
                sends, arrivals = _ring_copies(bufs[first[t] + w], sems[2 * at], sems[2 * at + 1], RING_STAGES[s] * w, s)
                if verb == "start":
                    for cp in sends:
                        cp.start()
                else:
                    for cp in arrivals:
                        cp.wait_recv()
                    for cp in sends:
                        cp.wait_send()

    bufs, sems, token = _comm_call(
        name, body, [b for t in tags for b in groups[t]["bufs"]],
        [sem for t, s in waits for sem in groups[t]["sems"][s]],
        [RING_STAGES[s] * counts[t] for t, s in starts for _ in (0, 1)], after, token=True)
    for t in tags:
        groups[t]["bufs"] = bufs[first[t]:first[t] + counts[t]]
    for t, s in waits:
        del groups[t]["sems"][s]
    for i, (t, s) in enumerate(starts):
        groups[t]["sems"][s] = (sems[2 * i], sems[2 * i + 1])
    return token


def _cx_copies(src, dst, send_sems, recv_sems, k0):
    x, y, c, chips = _position()
    sends = [_remote(src.at[2 * cx + cy], dst.at[2 * x + y], send_sems, recv_sems, k0 + j, (cx, cy, c))
             for j, (cx, cy) in enumerate(chips)]
    arrivals = [_remote(dst.at[2 * cx + cy], dst.at[2 * cx + cy], send_sems, recv_sems, k0 + j, (x, y, c))
                for j, (cx, cy) in enumerate(chips)]
    return sends, arrivals


def _cx_start(name, pair_sums):
    n = len(pair_sums)
    landing = [lax.empty(p.shape, p.dtype) for p in pair_sums]

    def body(bufs, _, sems):
        for w in range(n):
            for cp in _cx_copies(bufs[w], bufs[n + w], sems[0], sems[1], 3 * w)[0]:
                cp.start()

    bufs, sems, token = _comm_call(name, body, list(pair_sums) + landing, [], [3 * n, 3 * n], token=True)
    return (bufs, sems), token


def _cx_wait(name, state, after):
    bufs, sems = state
    n = len(bufs) // 2

    def body(refs, sems_in, _):
        for w in range(n):
            sends, arrivals = _cx_copies(refs[w], refs[n + w], sems_in[0], sems_in[1], 3 * w)
            for cp in arrivals:
                cp.wait_recv()
            for cp in sends:
                cp.wait_send()

    bufs, _, _ = _comm_call(name, body, bufs, sems, [], after)
    return bufs[:n], bufs[n:]


def _px_copies(src, dst, send_sems, recv_sems, k):
    x, y, c, _ = _position()
    hr = src.shape[1] // 2
    send = _remote(src.at[:, pl.ds((1 - c) * hr, hr), :], dst, send_sems, recv_sems, k, (x, y, 1 - c))
    arrival = _remote(dst, dst, send_sems, recv_sems, k, (x, y, c))
    return send, arrival


def _px_start(name, grads):
    n = len(grads)
    landing = [lax.empty((N_CHIPS, g.shape[1] // 2, g.shape[2]), g.dtype) for g in grads]

    def body(bufs, _, sems):
        for w in range(n):
            _px_copies(bufs[w], bufs[n + w], sems[0], sems[1], w)[0].start()

    bufs, sems, token = _comm_call(name, body, list(grads) + landing, [], [n, n], token=True)
    return (bufs, sems), token


def _px_wait(name, state, after):
    bufs, sems = state
    n = len(bufs) // 2

    def body(refs, sems_in, _):
        for w in range(n):
            send, arrival = _px_copies(refs[w], refs[n + w], sems_in[0], sems_in[1], w)
            arrival.wait_recv()
            send.wait_send()

    bufs, _, _ = _comm_call(name, body, bufs, sems, [], after)
    return bufs[:n], bufs[n:]


def _pair_sum(grad, got, name):
    _, rows, cols = grad.shape
    hr = rows // 2
    tm = min(hr, 512)
    nb = hr // tm
    c = lax.axis_index("c")

    def body(c_ref, g_ref, o_ref, out_ref):
        out_ref[...] = (g_ref[...].astype(F32) + o_ref[...].astype(F32)).astype(BF16)

    return pl.pallas_call(
        body, name=name,
        grid_spec=pltpu.PrefetchScalarGridSpec(
            num_scalar_prefetch=1, grid=(N_CHIPS, nb),
            in_specs=[pl.BlockSpec((None, tm, cols), lambda s, i, c_ref: (s, c_ref[0] * nb + i, 0)),
                      pl.BlockSpec((None, tm, cols), lambda s, i, c_ref: (s, i, 0))],
            out_specs=pl.BlockSpec((None, tm, cols), lambda s, i, c_ref: (s, i, 0))),
        out_shape=jax.ShapeDtypeStruct((N_CHIPS, hr, cols), BF16),
        compiler_params=_params(2),
    )(jnp.reshape(c, (1,)).astype(jnp.int32), grad, got)


def _chip_sum(parts, pair_sums, name):
    _, hr, cols = parts.shape
    tm = min(hr, 512)
    nb = hr // tm
    x, y, c = lax.axis_index("x"), lax.axis_index("y"), lax.axis_index("c")

    def body(pos_ref, p_ref, own_ref, o_ref):
        chip = pos_ref[0]
        own = own_ref[...].astype(F32)
        term = lambda s: jnp.where(chip == s, own, p_ref[s].astype(F32))
        o_ref[...] = ((term(0) + term(1)) + term(2)) + term(3)

    return pl.pallas_call(
        body, name=name,
        grid_spec=pltpu.PrefetchScalarGridSpec(
            num_scalar_prefetch=1, grid=(nb,),
            in_specs=[pl.BlockSpec((N_CHIPS, tm, cols), lambda i, pos: (0, i, 0)),
                      pl.BlockSpec((None, tm, cols), lambda i, pos: (pos[0], i, 0))],
            out_specs=pl.BlockSpec((tm, cols), lambda i, pos: (pos[1] * nb + i, 0))),
        out_shape=jax.ShapeDtypeStruct((2 * hr, cols), F32), compiler_params=_params(1),
    )(jnp.stack([2 * x + y, c]).astype(jnp.int32), parts, pair_sums)


def _share_copies(buf, send_sems, recv_sems, k):
    x, y, c, _ = _position()
    hr = buf.shape[0] // 2
    mine, theirs = buf.at[pl.ds(c * hr, hr), :], buf.at[pl.ds((1 - c) * hr, hr), :]
    return (_remote(mine, mine, send_sems, recv_sems, k, (x, y, 1 - c)),
            _remote(theirs, theirs, send_sems, recv_sems, k, (x, y, c)))


def _share_start(name, bufs):
    n = len(bufs)

    def body(refs, _, sems):
        for w in range(n):
            _share_copies(refs[w], sems[0], sems[1], w)[0].start()

    bufs, sems, token = _comm_call(name, body, list(bufs), [], [n, n], token=True)
    return (bufs, sems), token


def _share_wait(name, state, after):
    bufs, sems = state

    def body(refs, sems_in, _):
        for w in range(len(bufs)):
            send, arrival = _share_copies(refs[w], sems_in[0], sems_in[1], w)
            arrival.wait_recv()
            send.wait_send()

    return _comm_call(name, body, bufs, sems, [], after)[0]


def _allreduce_small(g):
    rows = g.shape[0]
    half = rows // 2

    def body(g_ref, o_ref, sib, slots, send_sems, recv_sems):
        x, y, c, chips = _position()
        me, sibling = (x, y, c), (x, y, 1 - c)
        my_chip = 2 * x + y
        mine = pl.ds(pl.multiple_of(c * half, 8), half)
        theirs = pl.ds(pl.multiple_of((1 - c) * half, 8), half)
        pair = _remote(g_ref.at[theirs], sib, send_sems, recv_sems, 0, sibling)
        pair.start()
        pair.wait()
        slots[my_chip] = g_ref[mine, :] + sib[...]
        sent = []
        for j, (cx, cy) in enumerate(chips):
            cp = _remote(slots.at[my_chip], slots.at[my_chip], send_sems, recv_sems, 1 + j, (cx, cy, c))
            cp.start()
            sent.append(cp)
        for j, (cx, cy) in enumerate(chips):
            got = slots.at[2 * cx + cy]
            _remote(got, got, send_sems, recv_sems, 1 + j, me).wait_recv()
        for cp in sent:
            cp.wait_send()
        o_ref[mine, :] = ((slots[0] + slots[1]) + slots[2]) + slots[3]
        swap = _remote(o_ref.at[mine], o_ref.at[mine], send_sems, recv_sems, 4, sibling)
        swap.start()
        swap.wait()

    vm = pl.BlockSpec(memory_space=pltpu.VMEM)
    return pl.pallas_call(
        body, name="allreduce_small",
        in_specs=[vm], out_specs=vm, out_shape=jax.ShapeDtypeStruct((rows, 128), F32),
        scratch_shapes=[pltpu.VMEM((half, 128), F32), pltpu.VMEM((N_CHIPS, half, 128), F32),
                        pltpu.SemaphoreType.DMA((5,)), pltpu.SemaphoreType.DMA((5,))],
        compiler_params=pltpu.CompilerParams(vmem_limit_bytes=VMEM_LIMIT),
    )(g)


_SMALL =("rel_bias", "ln_v_gain", "ln_v_bias", "w_spatial", "b_spatial", "ln1_gain", "ln1_bias",
          "b_ff1", "b_ff2", "ln2_gain", "ln2_bias")
_SMALL_ROWS = 1200
_LOSS_AT = (152832 // 128, 0)


def _pack_small(parts):
    flat = jnp.concatenate([parts[k].reshape(-1).astype(F32) for k in _SMALL])
    flat = jnp.pad(flat, (0, _SMALL_ROWS * 128 - flat.shape[0]))
    return flat.reshape(_SMALL_ROWS, 128)


def _unpack_small(packed, like):
    flat = packed.reshape(-1)
    out, at = {}, 0
    for k in _SMALL:
        n = math.prod(like[k].shape)
        out[k] = flat[at:at + n].reshape(like[k].shape)
        at += n
    return out


def kernel(x, w_in, rel_bias, ln_v_gain, ln_v_bias, w_spatial, b_spatial, w_proj_a, w_proj_b, w_out, ln1_gain, ln1_bias, w_ff1, b_ff1, w_ff2, b_ff2, ln2_gain, ln2_bias, loss_target, m_w_in, m_rel_bias, m_ln_v_gain, m_ln_v_bias, m_w_spatial, m_b_spatial, m_w_proj_a, m_w_proj_b, m_w_out, m_ln1_gain, m_ln1_bias, m_w_ff1, m_b_ff1, m_w_ff2, m_b_ff2, m_ln2_gain, m_ln2_bias, v_w_in, v_rel_bias, v_ln_v_gain, v_ln_v_bias, v_w_spatial, v_b_spatial, v_w_proj_a, v_w_proj_b, v_w_out, v_ln1_gain, v_ln1_bias, v_w_ff1, v_b_ff1, v_w_ff2, v_b_ff2, v_ln2_gain, v_ln2_bias):
    args = dict(locals())
    big = ("w_in", "w_proj_a", "w_proj_b", "w_out", "w_ff1", "w_ff2")
    weights = ("w_in", "rel_bias", "ln_v_gain", "ln_v_bias", "w_spatial", "b_spatial", "w_proj_a", "w_proj_b", "w_out",
               "ln1_gain", "ln1_bias", "w_ff1", "b_ff1", "w_ff2", "b_ff2", "ln2_gain", "ln2_bias")

    xs = x[0]
    target = loss_target[0]

    ring = {"a": {"bufs": [_place_shard(w_in[0], "place_w_in")], "sems": {}}}
    tok = _ring_call("allgather_a_near", ring, [("start", "a", "ici_near")])
    placed = [_place_shard(args[k][0], f"place_{k}", after=tok) for k in big[1:]]
    for tag, bufs in (("b", placed[0:3]), ("c", placed[3:4]), ("d", placed[4:5])):
        ring[tag] = {"bufs": bufs, "sems": {}}
    xb = _to_bf16(xs, "x_to_bf16", after=placed[4])

    mx, my = lax.axis_index("x"), lax.axis_index("y")
    own = jnp.reshape(2 * mx + my, (1,)).astype(jnp.int32)
    near = jnp.stack([2 * (1 - mx) + my, 2 * mx + (1 - my)]).astype(jnp.int32)
    far = jnp.reshape(2 * (1 - mx) + (1 - my), (1,)).astype(jnp.int32)
    proj = _proj(xb, ring["a"]["bufs"][0], own, "proj_own")
    _ring_call("allgather_a_far", ring, [("wait", "a", "ici_near"), ("start", "a", "ici_far"), ("start", "a", "d2d_near"),
                                         ("start", "b", "ici_near"), ("start", "c", "ici_near")], after=proj)
    _ring_call("allgather_a_near_done", ring, [("wait", "a", "d2d_near")])
    proj = _proj(xb, ring["a"]["bufs"][0], near, "proj_near", into=proj)
    _ring_call("allgather_a_last", ring, [("wait", "a", "ici_far"), ("start", "a", "d2d_far")], after=proj)
    _ring_call("allgather_a_done", ring, [("wait", "a", "d2d_far")])
    (win_g,) = ring["a"]["bufs"]
    proj = _proj(xb, win_g, far, "proj_far", into=proj)
    _ring_call("allgather_b_far", ring, [("wait", "b", "ici_near"), ("start", "b", "ici_far"), ("start", "b", "d2d_near")],
               after=proj)
    ws = w_spatial[0]
    ws_t = jnp.transpose(ws, (0, 2, 1))
    bsp_b = jnp.broadcast_to(b_spatial[0][:, :, None], (NH, 128, 128))
    gmlp = _gmlp_fwd(proj, ws, bsp_b, ln_v_gain, ln_v_bias)
    bias = _bias_tiles(rel_bias)
    attn, lse = _attention_fwd(proj, bias)
    _ring_call("allgather_b_last_c_far", ring,
               [("wait", "b", "ici_far"), ("start", "b", "d2d_far"),
                ("wait", "c", "ici_near"), ("start", "c", "ici_far"), ("start", "c", "d2d_near"),
                ("start", "d", "ici_near")], after=attn)
    _ring_call("allgather_b_done", ring, [("wait", "b", "d2d_near"), ("wait", "b", "d2d_far")])
    wpa_g, wpb_g, wout_g = ring["b"]["bufs"]
    wout_full = wout_g.reshape(D, D)
    ya, yb, merged = _branch(attn, gmlp, wpa_g, wpb_g, proj)
    xhat1, rstd1, h1b = _out_ln1(merged, wout_full, xs, ln1_gain, ln1_bias)
    _ring_call("allgather_c_last", ring, [("wait", "c", "ici_far"), ("start", "c", "d2d_far")], after=h1b)
    _ring_call("allgather_c_done", ring, [("wait", "c", "d2d_near"), ("wait", "c", "d2d_far")])
    (w1_g,) = ring["c"]["bufs"]
    a, r = _ff1(h1b, w1_g, b_ff1, 0, "ff1_first")
    tok = _ring_call("allgather_d_far", ring,
                     [("wait", "d", "ici_near"), ("start", "d", "ici_far"), ("start", "d", "d2d_near")], after=a)
    a, r = _ff1(h1b, w1_g, b_ff1, 1, "ff1_second", into=(a, r), after=tok)
    _ring_call("allgather_d_last", ring, [("wait", "d", "ici_far"), ("start", "d", "d2d_far")], after=a)
    _ring_call("allgather_d_done", ring, [("wait", "d", "d2d_near"), ("wait", "d", "d2d_far")])
    (w2_g,) = ring["d"]["bufs"]
    w2_full = w2_g.reshape(DFF, D)
    dpre2, dpre2b, st2 = _ff2_ln2_loss(a, w2_full, xhat1, ln1_gain, ln1_bias, b_ff2, ln2_gain, ln2_bias, target)

    def pair_and_chip(tag, state, after):
        local, from_sibling = _px_wait(f"pair_exchange_wait_{tag}", state, after)
        pair_sums = [_pair_sum(g, o, f"pair_sum_{tag}_{i}") for i, (g, o) in enumerate(zip(local, from_sibling))]
        return _cx_start(f"chip_exchange_start_{tag}", pair_sums)

    g_w2 = _grad_w(a, dpre2b, "grad_w_ff2", 512, 2048, False)
    px, tok = _px_start("pair_exchange_start_w_ff2", [g_w2.reshape(N_CHIPS, DFF // N_CHIPS, D)])
    dprea, g_b1 = _d_ff1(dpre2b, w2_full, r, after=tok)
    cx_w2, tok = pair_and_chip("w_ff2", px, dprea)
    g_w1 = _grad_w(h1b, dprea, "grad_w_ff1", 512, 2048, True, after=tok)
    px, tok = _px_start("pair_exchange_start_w_ff1", [g_w1])
    dpre1, dpre1b, st1 = _d_h1_ln1(dprea, w1_g, dpre2, xhat1, rstd1, ln1_gain, after=tok)
    cx_w1, tok = pair_and_chip("w_ff1", px, dpre1b)
    g_wout = _grad_w(merged, dpre1b, "grad_w_out", 512, 2048, False, after=tok)
    dya, dyb, dga, dgb = _d_merged(dpre1b, wout_full, proj, ya, yb)
    g_wpa = _grad_w(attn, dya, "grad_w_proj_a", 1024, 512, True)
    g_wpb = _grad_w(gmlp, dyb, "grad_w_proj_b", 1024, 512, True)
    px, tok = _px_start("pair_exchange_start_b", [g_wpa, g_wpb, g_wout.reshape(N_CHIPS, D // N_CHIPS, D)])
    dattn, dgmlp = _d_branches(dya, dyb, wpa_g, wpb_g, after=tok)
    duv, g_ws, g_bs, stv = _gmlp_bwd(proj, dgmlp, ws, ws_t, bsp_b, ln_v_gain, ln_v_bias)
    cx_b, tok = pair_and_chip("b", px, duv)
    dq, dk, dv, ds_sums = _attention_bwd(proj, dattn, attn, lse, bias, after=tok)
    g_rb = _rel_bias_grad(ds_sums)[:, :NH]

    small_g = dict(rel_bias=g_rb, ln_v_gain=stv[0], ln_v_bias=stv[1], w_spatial=g_ws, b_spatial=g_bs[:, :, 0],
                   ln1_gain=st1[0], ln1_bias=st1[1], b_ff1=g_b1, b_ff2=st2[2], ln2_gain=st2[0], ln2_bias=st2[1])
    gs = _allreduce_small(_pack_small(small_g).at[_LOSS_AT].set(st2[3, 0]))
    ds_, ms_, vs_, _ = _adamw(_pack_small({k: args[k] for k in _SMALL}), gs,
                           _pack_small({k: args["m_" + k] for k in _SMALL}),
                           _pack_small({k: args["v_" + k] for k in _SMALL}), "adamw_small")
    like = {k: args[k] for k in _SMALL}
    grads, deltas, new_m, new_v = (_unpack_small(t, like) for t in (gs, ds_, ms_, vs_))

    dproj = jnp.concatenate([dq, dk, dv, duv, dga, dgb], axis=1)
    g_win = _grad_w(xb, dproj, "grad_w_in", 512, 2304, True, after=gs)
    px, tok = _px_start("pair_exchange_start_w_in", [g_win])

    def chip_sums(tag, state, names, after):
        pair_sums, from_chips = _cx_wait(f"chip_exchange_wait_{tag}", state, after)
        halves = [_chip_sum(p, own, f"chip_sum_{k}") for p, own, k in zip(from_chips, pair_sums, names)]
        return _share_start(f"share_start_{tag}", halves)

    def adam_one(k, g, after=None):
        d_, m_, v_, g_ = _adamw(args[k][0], g, args["m_" + k][0], args["v_" + k][0], f"adamw_{k}", after=after)
        grads[k], deltas[k], new_m[k], new_v[k] = g_[None], d_[None], m_[None], v_[None]
        return d_

    def adam(tag, state, names, after):
        last = None
        for k, g in zip(names, _share_wait(f"share_wait_{tag}", state, after)):
            last = adam_one(k, g)
        return last

    sh_w2, tok = chip_sums("w_ff2", cx_w2, ["w_ff2"], tok)
    sh_w1, tok = chip_sums("w_ff1", cx_w1, ["w_ff1"], tok)
    sh_b, tok = chip_sums("b", cx_b, ["w_proj_a", "w_proj_b", "w_out"], tok)
    cx_in, tok = pair_and_chip("w_in", px, tok)
    grad_x = _d_x(dproj, win_g, dpre1, after=tok)
    done = adam("w_ff2", sh_w2, ["w_ff2"], grad_x)
    done = adam("w_ff1", sh_w1, ["w_ff1"], done)
    g_wpa_full, g_wpb_full, g_wout_full = _share_wait("share_wait_b", sh_b, done)
    done = adam_one("w_out", g_wout_full)
    sh_in, tok = chip_sums("w_in", cx_in, ["w_in"], done)
    done = adam_one("w_proj_a", g_wpa_full, after=tok)
    done = adam_one("w_proj_b", g_wpb_full, after=done)
    adam("w_in", sh_in, ["w_in"], done)

    loss = gs[_LOSS_AT] * (0.5 / D)
    return (loss, grad_x[None], *[grads[k] for k in weights], *[deltas[k] for k in weights],
            *[new_m[k] for k in weights], *[new_v[k] for k in weights])
```

```python
import math

import numpy as np
import jax
import jax.numpy as jnp
from jax import lax
from jax.experimental import pallas as pl
from jax.experimental.pallas import tpu as pltpu

F32 = jnp.float32
BF16 = jnp.bfloat16

S = 2048
D = 2048
DA = 1024
DB = 1024
DFF = 8192
DIN = 9216
NH = 8
HD = 128
NBLK = 16
PATTERNS = ((128, 1), (512, 4), (2048, 16))
N_BUCKETS = 32
MAX_DISTANCE = 2048
ALPHA = 2.0 ** 0.25
LN_EPS = 1e-5
NEG_INF = -1e30
SCALE = HD ** -0.5
N_CHIPS = 4

ADAM_LR = 0.001
ADAM_B1 = 0.9
ADAM_B2 = 0.999
ADAM_EPS = 1e-08
ADAM_WD = 0.01
ADAM_STEP = 10

VMEM_LIMIT = 56 * 1024 * 1024
MESH = pl.DeviceIdType.MESH
ANY = pl.BlockSpec(memory_space=pl.ANY)


def _params(n_axes, vmem=VMEM_LIMIT):
    return pltpu.CompilerParams(dimension_semantics=("arbitrary",) * n_axes, vmem_limit_bytes=vmem)


def _bucket_tile(dilation):
    qi = np.arange(128)[:, None]
    kj = np.arange(256)[None, :]
    n = np.clip(128 + qi - kj, 0, 128) * dilation
    max_exact = N_BUCKETS // 2
    nf = np.maximum(n, 1).astype(np.float32)
    large = max_exact + (np.log(nf / np.float32(max_exact)) / np.float32(math.log(MAX_DISTANCE / max_exact))
                         * np.float32(N_BUCKETS - max_exact)).astype(np.int32)
    large = np.minimum(large, N_BUCKETS - 1)
    return np.where(n < max_exact, n, large).astype(np.int32)


def _gelu(x):
    c = math.sqrt(2.0 / math.pi)
    t = jnp.tanh(c * (x + 0.044715 * x * x * x))
    return 0.5 * x * (1.0 + t), t


def _gelu_grad(x, t):
    c = math.sqrt(2.0 / math.pi)
    return 0.5 * (1.0 + t) + 0.5 * x * (1.0 - t * t) * c * (1.0 + 3.0 * 0.044715 * x * x)


def _sigmoid(x):
    return 1.0 / (1.0 + jnp.exp(-x))


def _dot(a, b):
    return jnp.dot(a, b, preferred_element_type=F32)


def _behind(body, n_in, after):
    if after is None:
        return body, [], []
    return (lambda *refs: body(*refs[:n_in], *refs[n_in + 1:])), [ANY], [after]


def _dot_nt(a, b):
    return lax.dot_general(a, b, (((1,), (1,)), ((), ())), preferred_element_type=F32)


def _proj(xb, win_g, shards, name, into=None):
    tn = 768
    per = 2304 // tn

    def body(shards_ref, x_ref, w_ref, *rest):
        rest[-1][...] = _dot(x_ref[...], w_ref[...])

    in_specs = [pl.BlockSpec((S, D), lambda j, sh: (0, 0)),
                pl.BlockSpec((None, D, tn), lambda j, sh: (sh[j // per], 0, j % per))]
    return pl.pallas_call(
        body, name=name,
        grid_spec=pltpu.PrefetchScalarGridSpec(
            num_scalar_prefetch=1, grid=(shards.shape[0] * per,),
            in_specs=in_specs + ([ANY] if into is not None else []),
            out_specs=pl.BlockSpec((S, tn), lambda j, sh: (0, sh[j // per] * per + j % per))),
        out_shape=jax.ShapeDtypeStruct((S, DIN), F32),
        input_output_aliases={3: 0} if into is not None else {},
        compiler_params=_params(1),
    )(shards, xb, win_g, *([into] if into is not None else []))


FWD_HEADS_PER_STEP = 4
BWD_HEADS_PER_STEP = 2


def _bias_tiles(rel_bias):
    buckets = jnp.asarray(np.stack([_bucket_tile(d) for _, d in PATTERNS]))

    def body(rb_ref, bk_ref, o_ref):
        qi = lax.broadcasted_iota(jnp.int32, (128, 256), 0)
        kj = lax.broadcasted_iota(jnp.int32, (128, 256), 1)
        steps = 128 + qi - kj
        band = (steps >= 0) & (steps <= 128)
        o_ref[...] = jnp.zeros_like(o_ref)
        for p in range(len(PATTERNS)):
            bucket = bk_ref[p]

            def one_bucket(t, carry):
                hit = bucket == t
                for h in range(NH):
                    o_ref[p, h] = jnp.where(hit, rb_ref[t, h], o_ref[p, h])
                return carry

            lax.fori_loop(0, N_BUCKETS, one_bucket, 0)
            for h in range(NH):
                o_ref[p, h] = jnp.where(band, o_ref[p, h], NEG_INF)

    return pl.pallas_call(
        body, name="bias_tiles",
        in_specs=[pl.BlockSpec(memory_space=pltpu.SMEM), pl.BlockSpec(memory_space=pltpu.VMEM)],
        out_specs=pl.BlockSpec(memory_space=pltpu.VMEM),
        out_shape=jax.ShapeDtypeStruct((len(PATTERNS), NH, 128, 256), F32),
        compiler_params=pltpu.CompilerParams(vmem_limit_bytes=VMEM_LIMIT),
    )(rel_bias, buckets)


def _block_rows(b, dilation):
    nblk = NBLK // dilation
    r, n = b // nblk, b % nblk
    start = r + n * (128 * dilation)
    prev_start = jnp.maximum(start - 128 * dilation, r)
    if dilation == 1:
        return pl.ds(pl.multiple_of(start, 128), 128), pl.ds(pl.multiple_of(prev_start, 128), 128), n > 0
    return pl.ds(start, 128, stride=dilation), pl.ds(prev_start, 128, stride=dilation), n > 0


def _head_specs(first, hps):
    return [pl.BlockSpec((S, HD), lambda g, j=j: (0, first + g * hps + j)) for j in range(hps)]


def _bias_spec(hps):
    return pl.BlockSpec((len(PATTERNS), hps, 128, 256), lambda g: (0, g, 0, 0))


def _heads_spec(hps):
    return pl.BlockSpec((S, hps * HD), lambda g: (0, g))


def _attention_fwd(proj, bias):
    hps = FWD_HEADS_PER_STEP

    def body(bias_ref, *refs):
        q_refs, k_refs, v_refs = (refs[i * hps:(i + 1) * hps] for i in range(3))
        o_ref, lse_ref = refs[3 * hps:3 * hps + 2]
        acc_scrs, m_scrs, l_scrs = (refs[3 * hps + 2 + i * hps:3 * hps + 2 + (i + 1) * hps] for i in range(3))
        kj = lax.broadcasted_iota(jnp.int32, (128, 256), 1)
        for p, (_, d) in enumerate(PATTERNS):
            prev_blocks = NBLK // d > 1

            def block(b, carry):
                units = [(j,) + _block_rows(blk, d) for blk in (b, b + NBLK // 2) for j in range(hps)]
                scores = []
                for j, rows, prows, _ in units:
                    q = q_refs[j][rows, :].astype(BF16)
                    cur = _dot_nt(q, k_refs[j][rows, :].astype(BF16))
                    if prev_blocks:
                        cur = jnp.concatenate([_dot_nt(q, k_refs[j][prows, :].astype(BF16)), cur], axis=1)
                    scores.append(cur)
                soft = []
                for u, (j, _, _, has_prev) in enumerate(units):
                    if prev_blocks:
                        s = jnp.where((kj >= 128) | has_prev, scores[u] * SCALE + bias_ref[p, j], NEG_INF)
                    else:
                        s = scores[u] * SCALE + bias_ref[p, j, :, 128:256]
                    m = jnp.max(s, axis=1, keepdims=True)
                    e = jnp.exp(s - m)
                    soft.append((m, jnp.sum(e, axis=1, keepdims=True), e.astype(BF16)))
                outs = []
                for u, (j, rows, prows, _) in enumerate(units):
                    e = soft[u][2]
                    if prev_blocks:
                        outs.append(_dot(e[:, :128], v_refs[j][prows, :].astype(BF16))
                                    + _dot(e[:, 128:], v_refs[j][rows, :].astype(BF16)))
                    else:
                        outs.append(_dot(e, v_refs[j][rows, :].astype(BF16)))
                for u, (j, rows, _, _) in enumerate(units):
                    acc_scr, m_scr, l_scr = acc_scrs[j], m_scrs[j], l_scrs[j]
                    (m, den, _), o = soft[u], outs[u]
                    if p == 0:
                        acc_scr[rows, :] = o
                        m_scr[rows, :] = jnp.broadcast_to(m, (128, HD))
                        l_scr[rows, :] = jnp.broadcast_to(den, (128, HD))
                    else:
                        m_old = m_scr[rows, :]
                        m_new = jnp.maximum(m_old, m)
                        w_old, w_new = jnp.exp(m_old - m_new), jnp.exp(m - m_new)
                        acc_scr[rows, :] = acc_scr[rows, :] * w_old + o * w_new
                        l_scr[rows, :] = l_scr[rows, :] * w_old + den * w_new
                        m_scr[rows, :] = m_new
                return carry

            lax.fori_loop(0, NBLK // 2, block, 0)
        for j in range(hps):
            cols = slice(j * HD, (j + 1) * HD)
            den = l_scrs[j][...]
            o_ref[:, cols] = (acc_scrs[j][...] / den).astype(BF16)
            lse_ref[:, cols] = m_scrs[j][...] + jnp.log(den)

    return pl.pallas_call(
        body, name="attention_fwd", grid=(NH // hps,),
        in_specs=[_bias_spec(hps)] + _head_specs(0, hps) + _head_specs(NH, hps) + _head_specs(2 * NH, hps),
        out_specs=[_heads_spec(hps), _heads_spec(hps)],
        out_shape=[jax.ShapeDtypeStruct((S, DA), BF16), jax.ShapeDtypeStruct((S, DA), F32)],
        scratch_shapes=[pltpu.VMEM((S, HD), F32)] * (3 * hps),
        compiler_params=_params(1),
    )(bias, *([proj] * (3 * hps)))


def _attention_bwd(proj, dattn, attn, lse, bias, after=None):
    hps = BWD_HEADS_PER_STEP

    def body(bias_ref, *refs):
        q_refs, k_refs, v_refs, do_refs, o_refs, lse_refs = (refs[i * hps:(i + 1) * hps] for i in range(6))
        dq_ref, dk_ref, dv_ref, ds_ref = refs[6 * hps:6 * hps + 4]
        dl_scrs, dq_scrs, dk_scrs, dv_scrs = (refs[6 * hps + 4 + i * hps:6 * hps + 4 + (i + 1) * hps] for i in range(4))
        ds_ref[...] = jnp.zeros_like(ds_ref)
        for j in range(hps):
            dq_scrs[j][...] = jnp.zeros((S, HD), F32)
            dk_scrs[j][...] = jnp.zeros((S, HD), F32)
            dv_scrs[j][...] = jnp.zeros((S, HD), F32)
            prod = do_refs[j][...] * o_refs[j][...].astype(F32)
            dl_scrs[j][...] = jnp.broadcast_to(jnp.sum(prod, axis=1, keepdims=True), (S, HD))
        for p, (_, d) in enumerate(PATTERNS):
            prev_blocks = NBLK // d > 1

            def block(b, carry):
                units = [(j,) + _block_rows(b + i * (NBLK // 4), d) for i in range(4) for j in range(hps)]
                ops, raw = [], []
                for j, rows, prows, _ in units:
                    q, do = q_refs[j][rows, :].astype(BF16), do_refs[j][rows, :].astype(BF16)
                    kc, vc = k_refs[j][rows, :].astype(BF16), v_refs[j][rows, :].astype(BF16)
                    if prev_blocks:
                        kp, vp = k_refs[j][prows, :].astype(BF16), v_refs[j][prows, :].astype(BF16)
                        ops.append((q, do, kc, kp))
                        raw.append((_dot_nt(q, kc), _dot_nt(do, vc), _dot_nt(q, kp), _dot_nt(do, vp)))
                    else:
                        ops.append((q, do, kc))
                        raw.append((_dot_nt(q, kc), _dot_nt(do, vc)))
                probs = []
                for u, (j, rows, _, has_prev) in enumerate(units):
                    lse_b, dl_b = lse_refs[j][rows, :], dl_scrs[j][rows, :]
                    p_c = jnp.exp(raw[u][0] * SCALE + bias_ref[p, j, :, 128:256] - lse_b)
                    ds_c = p_c * (raw[u][1] - dl_b)
                    ds_ref[p, j, :, 128:256] += ds_c
                    if prev_blocks:
                        p_p = jnp.where(has_prev, jnp.exp(raw[u][2] * SCALE + bias_ref[p, j, :, 0:128] - lse_b), 0.0)
                        ds_p = p_p * (raw[u][3] - dl_b)
                        ds_ref[p, j, :, 0:128] += ds_p
                        probs.append((p_c, ds_c, p_p, ds_p))
                    else:
                        probs.append((p_c, ds_c))
                grads = []
                for u in range(len(units)):
                    q, do, kc = ops[u][:3]
                    p_c, ds_c = probs[u][:2]
                    dq = _dot(ds_c.astype(BF16), kc)
                    cur = (_dot(ds_c.T.astype(BF16), q) * SCALE, _dot(p_c.T.astype(BF16), do))
                    if prev_blocks:
                        p_p, ds_p = probs[u][2:]
                        dq = dq + _dot(ds_p.astype(BF16), ops[u][3])
                        cur = cur + (_dot(ds_p.T.astype(BF16), q) * SCALE, _dot(p_p.T.astype(BF16), do))
                    grads.append((dq * SCALE,) + cur)
                for u, (j, rows, prows, _) in enumerate(units):
                    dq_scrs[j][rows, :] += grads[u][0]
                    dk_scrs[j][rows, :] += grads[u][1]
                    dv_scrs[j][rows, :] += grads[u][2]
                    if prev_blocks:
                        dk_scrs[j][prows, :] += grads[u][3]
                        dv_scrs[j][prows, :] += grads[u][4]
                return carry

            lax.fori_loop(0, NBLK // 4, block, 0)
        for j in range(hps):
            cols = slice(j * HD, (j + 1) * HD)
            dq_ref[:, cols] = dq_scrs[j][...].astype(BF16)
            dk_ref[:, cols] = dk_scrs[j][...].astype(BF16)
            dv_ref[:, cols] = dv_scrs[j][...].astype(BF16)

    body, more_specs, more = _behind(body, 1 + 6 * hps, after)
    return pl.pallas_call(
        body, name="attention_bwd", grid=(NH // hps,),
        in_specs=[_bias_spec(hps)]
        + _head_specs(0, hps) + _head_specs(NH, hps) + _head_specs(2 * NH, hps) + 3 * _head_specs(0, hps)
        + more_specs,
        out_specs=3 * [_heads_spec(hps)] + [pl.BlockSpec((3, hps, 128, 256), lambda g: (0, g, 0, 0))],
        out_shape=[jax.ShapeDtypeStruct((S, DA), BF16)] * 3 + [jax.ShapeDtypeStruct((3, NH, 128, 256), F32)],
        scratch_shapes=[pltpu.VMEM((S, HD), F32)] * (4 * hps),
        compiler_params=_params(1),
    )(bias, *([proj] * (3 * hps)), *([dattn] * hps), *([attn] * hps), *([lse] * hps), *more)


def _gmlp_parts(u_ref, vb_ref, g_ref, be_ref):
    u = u_ref[...]
    u_act, tu = _gelu(u)
    vb = vb_ref[...]
    gv, tv = _gelu(vb)
    mean = jnp.mean(gv, axis=1, keepdims=True)
    cen = gv - mean
    var = jnp.mean(cen * cen, axis=1, keepdims=True)
    rstd = lax.rsqrt(var + LN_EPS)
    xhat = cen * rstd
    vn = xhat * g_ref[...] + be_ref[...]
    return u, tu, u_act, vb, tv, rstd, xhat, vn


def _gmlp_fwd(proj, ws, bsp_b, gain_v, bias_v):
    def body(u_ref, vb_ref, ws_ref, bsp_ref, g_ref, be_ref, o_ref):
        _, _, u_act, _, _, _, _, vn = _gmlp_parts(u_ref, vb_ref, g_ref, be_ref)
        row = lax.broadcasted_iota(jnp.int32, (128, 128), 0)
        col = lax.broadcasted_iota(jnp.int32, (128, 128), 1)
        causal = row >= col
        for g in range(NH):
            cols = slice(g * 128, (g + 1) * 128)
            wsg = jnp.where(causal, ws_ref[g], 0.0).astype(BF16)
            z = _dot(wsg, vn[:, cols].astype(BF16)) + bsp_ref[g]
            o_ref[:, cols] = (u_act[:, cols] * z).astype(BF16)

    return pl.pallas_call(
        body, name="gmlp_fwd", grid=(NBLK,),
        in_specs=[pl.BlockSpec((128, DB), lambda c: (c, 3)), pl.BlockSpec((128, DB), lambda c: (c, 4)),
                  pl.BlockSpec((NH, 128, 128), lambda c: (0, 0, 0)), pl.BlockSpec((NH, 128, 128), lambda c: (0, 0, 0)),
                  pl.BlockSpec((1, DB), lambda c: (0, 0)), pl.BlockSpec((1, DB), lambda c: (0, 0))],
        out_specs=pl.BlockSpec((128, DB), lambda c: (c, 0)),
        out_shape=jax.ShapeDtypeStruct((S, DB), BF16),
        compiler_params=_params(1),
    )(proj, proj, ws, bsp_b, gain_v, bias_v)


def _branch(attn, gmlp, wpa_g, wpb_g, proj):
    tn = 512

    def body(a_ref, g_ref, wa_ref, wb_ref, ga_ref, gb_ref, ya_ref, yb_ref, mg_ref):
        ya = _dot(a_ref[...], wa_ref[...])
        yb = _dot(g_ref[...], wb_ref[...])
        ya_ref[...] = ya.astype(BF16)
        yb_ref[...] = yb.astype(BF16)
        mg_ref[...] = (_sigmoid(ga_ref[...]) * ya + _sigmoid(gb_ref[...]) * yb).astype(BF16)

    out = pl.BlockSpec((S, tn), lambda j: (0, j))
    return pl.pallas_call(
        body, name="branch", grid=(D // tn,),
        in_specs=[pl.BlockSpec((S, DA), lambda j: (0, 0)), pl.BlockSpec((S, DB), lambda j: (0, 0)),
                  pl.BlockSpec((None, DA, tn), lambda j: (j, 0, 0)), pl.BlockSpec((None, DB, tn), lambda j: (j, 0, 0)),
                  pl.BlockSpec((S, tn), lambda j: (0, 5120 // tn + j)), pl.BlockSpec((S, tn), lambda j: (0, 7168 // tn + j))],
        out_specs=[out, out, out],
        out_shape=[jax.ShapeDtypeStruct((S, D), BF16)] * 3,
        compiler_params=_params(1),
    )(attn, gmlp, wpa_g, wpb_g, proj, proj)


def _out_ln1(merged, wout_g, x, gain, bias):
    tm = 256

    def body(m_ref, w_ref, x_ref, g_ref, b_ref, xh_ref, rs_ref, h_ref):
        pre = ALPHA * x_ref[...] + _dot(m_ref[...], w_ref[...])
        mean = jnp.mean(pre, axis=1, keepdims=True)
        cen = pre - mean
        var = jnp.mean(cen * cen, axis=1, keepdims=True)
        rstd = lax.rsqrt(var + LN_EPS)
        xhat = cen * rstd
        xh_ref[...] = xhat
        rs_ref[...] = jnp.broadcast_to(rstd, (tm, 128))
        h_ref[...] = (xhat * g_ref[...] + b_ref[...]).astype(BF16)

    row = pl.BlockSpec((tm, D), lambda i: (i, 0))
    vec = pl.BlockSpec((1, D), lambda i: (0, 0))
    return pl.pallas_call(
        body, name="out_ln1", grid=(S // tm,),
        in_specs=[row, pl.BlockSpec((D, D), lambda i: (0, 0)), row, vec, vec],
        out_specs=[row, pl.BlockSpec((tm, 128), lambda i: (i, 0)), row],
        out_shape=[jax.ShapeDtypeStruct((S, D), F32), jax.ShapeDtypeStruct((S, 128), F32),
                   jax.ShapeDtypeStruct((S, D), BF16)],
        compiler_params=_params(1),
    )(merged, wout_g, x, gain, bias)


def _ff1(h1b, w1_g, b1, half, name, into=None, after=None):
    tn = 512
    per = D // tn
    steps = DFF // tn // 2
    first = half * steps

    def body(h_ref, w_ref, b_ref, *rest):
        a_ref, r_ref = rest[-2:]
        r = jnp.maximum(_dot(h_ref[...], w_ref[...]) + b_ref[...], 0.0)
        r_ref[...] = r.astype(BF16)
        a_ref[...] = (r * r).astype(BF16)

    out = pl.BlockSpec((S, tn), lambda j: (0, first + j))
    extra = list(into) if into is not None else []
    if after is not None:
        extra.append(after)
    return pl.pallas_call(
        body, name=name, grid=(steps,),
        in_specs=[pl.BlockSpec((S, D), lambda j: (0, 0)),
                  pl.BlockSpec((None, D, tn), lambda j: ((first + j) // per, 0, (first + j) % per)),
                  pl.BlockSpec((1, tn), lambda j: (0, first + j))] + [ANY] * len(extra),
        out_specs=[out, out],
        out_shape=[jax.ShapeDtypeStruct((S, DFF), BF16)] * 2,
        input_output_aliases={3: 0, 4: 1} if into is not None else {},
        compiler_params=_params(1),
    )(h1b, w1_g, b1, *extra)


def _ff2_ln2_loss(a, w2_g, xhat1, g1, b1, b2, g2, be2, target):
    tm, tk = 512, 1024
    nk = DFF // tk

    def body(a_ref, w_ref, xh_ref, g1_ref, b1_ref, b2_ref, g2_ref, be2_ref, t_ref, d_ref, db_ref, st_ref, acc):
        i, k = pl.program_id(0), pl.program_id(1)

        @pl.when(k == 0)
        def _():
            acc[...] = jnp.zeros_like(acc)

        @pl.when((i == 0) & (k == 0))
        def _():
            st_ref[...] = jnp.zeros_like(st_ref)

        acc[...] += _dot(a_ref[...], w_ref[...])

        @pl.when(k == nk - 1)
        def _():
            def rows_chunk(ci, carry):
                rows = pl.ds(pl.multiple_of(ci * 128, 128), 128)
                h1 = xh_ref[rows, :] * g1_ref[...] + b1_ref[...]
                pre = ALPHA * h1 + acc[rows, :] + b2_ref[...]
                mean = jnp.mean(pre, axis=1, keepdims=True)
                cen = pre - mean
                var = jnp.mean(cen * cen, axis=1, keepdims=True)
                rstd = lax.rsqrt(var + LN_EPS)
                xhat = cen * rstd
                y = xhat * g2_ref[...] + be2_ref[...]
                err = y - t_ref[rows, :]
                dy = err * (1.0 / D)
                g = dy * g2_ref[...]
                dpre = rstd * (g - jnp.mean(g, axis=1, keepdims=True)
                               - xhat * jnp.mean(g * xhat, axis=1, keepdims=True))
                d_ref[rows, :] = dpre
                db_ref[rows, :] = dpre.astype(BF16)
                st_ref[0:1, :] += jnp.sum(dy * xhat, axis=0, keepdims=True)
                st_ref[1:2, :] += jnp.sum(dy, axis=0, keepdims=True)
                st_ref[2:3, :] += jnp.sum(dpre, axis=0, keepdims=True)
                st_ref[3:4, :] += jnp.broadcast_to(jnp.sum(err * err).reshape(1, 1), (1, D))
                return carry

            lax.fori_loop(0, tm // 128, rows_chunk, 0)

    row = pl.BlockSpec((tm, D), lambda i, k: (i, 0))
    vec = pl.BlockSpec((1, D), lambda i, k: (0, 0))
    return pl.pallas_call(
        body, name="ff2_ln2_loss", grid=(S // tm, nk),
        in_specs=[pl.BlockSpec((tm, tk), lambda i, k: (i, k)), pl.BlockSpec((tk, D), lambda i, k: (k, 0)),
                  row, vec, vec, vec, vec, vec, row],
        out_specs=[row, row, pl.BlockSpec((8, D), lambda i, k: (0, 0))],
        out_shape=[jax.ShapeDtypeStruct((S, D), F32), jax.ShapeDtypeStruct((S, D), BF16),
                   jax.ShapeDtypeStruct((8, D), F32)],
        scratch_shapes=[pltpu.VMEM((tm, D), F32)],
        compiler_params=_params(2),
    )(a, w2_g, xhat1, g1, b1, b2, g2, be2, target)


def _grad_w(act, dout, name, ti, tj, sharded, after=None):
    m, n = act.shape[1], dout.shape[1]
    ns = n // N_CHIPS
    per = ns // tj if sharded else None

    def body(a_ref, b_ref, o_ref, at_scr):
        o_ref[...] = lax.dot_general(a_ref[...], b_ref[...], (((0,), (0,)), ((), ())),
                                     preferred_element_type=F32).astype(BF16)

    if sharded:
        out_spec = pl.BlockSpec((None, ti, tj), lambda i, j: (j // per, i, j % per))
        out_shape = jax.ShapeDtypeStruct((N_CHIPS, m, ns), BF16)
    else:
        out_spec = pl.BlockSpec((ti, tj), lambda i, j: (i, j))
        out_shape = jax.ShapeDtypeStruct((m, n), BF16)
    body, more_specs, more = _behind(body, 2, after)
    return pl.pallas_call(
        body, name=name, grid=(m // ti, n // tj),
        in_specs=[pl.BlockSpec((S, ti), lambda i, j: (0, i)), pl.BlockSpec((S, tj), lambda i, j: (0, j))] + more_specs,
        out_specs=out_spec, out_shape=out_shape,
        scratch_shapes=[pltpu.VMEM((ti, S), BF16)],
        compiler_params=_params(2),
    )(act, dout, *more)


def _d_ff1(dpre2b, w2_g, r, after=None):
    tn = 512

    def body(d_ref, w_ref, r_ref, o_ref, gb_ref):
        da = _dot_nt(d_ref[...], w_ref[...])
        dp = da * (2.0 * r_ref[...].astype(F32))
        o_ref[...] = dp.astype(BF16)
        gb_ref[...] = jnp.sum(dp, axis=0, keepdims=True)

    body, more_specs, more = _behind(body, 3, after)
    return pl.pallas_call(
        body, name="d_ff1", grid=(DFF // tn,),
        in_specs=[pl.BlockSpec((S, D), lambda j: (0, 0)), pl.BlockSpec((tn, D), lambda j: (j, 0)),
                  pl.BlockSpec((S, tn), lambda j: (0, j))] + more_specs,
        out_specs=[pl.BlockSpec((S, tn), lambda j: (0, j)), pl.BlockSpec((1, tn), lambda j: (0, j))],
        out_shape=[jax.ShapeDtypeStruct((S, DFF), BF16), jax.ShapeDtypeStruct((1, DFF), F32)],
        compiler_params=_params(1),
    )(dpre2b, w2_g, r, *more)


def _d_h1_ln1(dprea, w1_g, dpre2, xhat1, rstd1, g1, after=None):
    tm, tk = 512, 1024
    per = D // tk
    nk = DFF // tk

    def body(a_ref, w_ref, d2_ref, xh_ref, rs_ref, g_ref, d_ref, db_ref, st_ref, acc):
        i, k = pl.program_id(0), pl.program_id(1)

        @pl.when(k == 0)
        def _():
            acc[...] = jnp.zeros_like(acc)

        @pl.when((i == 0) & (k == 0))
        def _():
            st_ref[...] = jnp.zeros_like(st_ref)

        acc[...] += _dot_nt(a_ref[...], w_ref[...])

        @pl.when(k == nk - 1)
        def _():
            def rows_chunk(ci, carry):
                rows = pl.ds(pl.multiple_of(ci * 128, 128), 128)
                dh = ALPHA * d2_ref[rows, :] + acc[rows, :]
                xhat = xh_ref[rows, :]
                g = dh * g_ref[...]
                dpre = rs_ref[rows, 0:1] * (g - jnp.mean(g, axis=1, keepdims=True)
                                            - xhat * jnp.mean(g * xhat, axis=1, keepdims=True))
                d_ref[rows, :] = dpre
                db_ref[rows, :] = dpre.astype(BF16)
                st_ref[0:1, :] += jnp.sum(dh * xhat, axis=0, keepdims=True)
                st_ref[1:2, :] += jnp.sum(dh, axis=0, keepdims=True)
                return carry

            lax.fori_loop(0, tm // 128, rows_chunk, 0)

    row = pl.BlockSpec((tm, D), lambda i, k: (i, 0))
    body, more_specs, more = _behind(body, 6, after)
    return pl.pallas_call(
        body, name="d_h1_ln1", grid=(S // tm, nk),
        in_specs=[pl.BlockSpec((tm, tk), lambda i, k: (i, k)),
                  pl.BlockSpec((None, D, tk), lambda i, k: (k // per, 0, k % per)),
                  row, row, pl.BlockSpec((tm, 128), lambda i, k: (i, 0)), pl.BlockSpec((1, D), lambda i, k: (0, 0))]
        + more_specs,
        out_specs=[row, row, pl.BlockSpec((8, D), lambda i, k: (0, 0))],
        out_shape=[jax.ShapeDtypeStruct((S, D), F32), jax.ShapeDtypeStruct((S, D), BF16),
                   jax.ShapeDtypeStruct((8, D), F32)],
        scratch_shapes=[pltpu.VMEM((tm, D), F32)],
        compiler_params=_params(2),
    )(dprea, w1_g, dpre2, xhat1, rstd1, g1, *more)


def _d_merged(dpre1b, wout_g, proj, ya, yb):
    tm, tn = 512, 1024

    def body(d_ref, w_ref, ga_ref, gb_ref, ya_ref, yb_ref, dya_ref, dyb_ref, dga_ref, dgb_ref):
        dm = _dot_nt(d_ref[...], w_ref[...])
        sa = _sigmoid(ga_ref[...])
        sb = _sigmoid(gb_ref[...])
        dya_ref[...] = (dm * sa).astype(BF16)
        dyb_ref[...] = (dm * sb).astype(BF16)
        dga_ref[...] = (dm * ya_ref[...].astype(F32) * sa * (1.0 - sa)).astype(BF16)
        dgb_ref[...] = (dm * yb_ref[...].astype(F32) * sb * (1.0 - sb)).astype(BF16)

    tile = pl.BlockSpec((tm, tn), lambda j, i: (i, j))
    return pl.pallas_call(
        body, name="d_merged", grid=(D // tn, S // tm),
        in_specs=[pl.BlockSpec((tm, D), lambda j, i: (i, 0)), pl.BlockSpec((tn, D), lambda j, i: (j, 0)),
                  pl.BlockSpec((tm, tn), lambda j, i: (i, 5 + j)), pl.BlockSpec((tm, tn), lambda j, i: (i, 7 + j)),
                  tile, tile],
        out_specs=[tile] * 4,
        out_shape=[jax.ShapeDtypeStruct((S, D), BF16)] * 4,
        compiler_params=_params(2),
    )(dpre1b, wout_g, proj, proj, ya, yb)


def _d_branches(dya, dyb, wpa_g, wpb_g, after=None):
    tm = 512
    ws = D // N_CHIPS

    def body(da_ref, db_ref, wa_ref, wb_ref, oa_ref, ob_ref):
        for d_ref, w_ref, o_ref in ((da_ref, wa_ref, oa_ref), (db_ref, wb_ref, ob_ref)):
            acc = _dot_nt(d_ref[:, 0:ws], w_ref[0])
            for s in range(1, N_CHIPS):
                acc = acc + _dot_nt(d_ref[:, s * ws:(s + 1) * ws], w_ref[s])
            o_ref[...] = acc

    rows = lambda width: pl.BlockSpec((tm, width), lambda i: (i, 0))
    whole = lambda n: pl.BlockSpec((N_CHIPS, n, ws), lambda i: (0, 0, 0))
    body, more_specs, more = _behind(body, 4, after)
    return pl.pallas_call(
        body, name="d_branches", grid=(S // tm,),
        in_specs=[rows(D), rows(D), whole(DA), whole(DB)] + more_specs,
        out_specs=[rows(DA), rows(DB)],
        out_shape=[jax.ShapeDtypeStruct((S, DA), F32), jax.ShapeDtypeStruct((S, DB), F32)],
        compiler_params=_params(1),
    )(dya, dyb, wpa_g, wpb_g, *more)


def _gmlp_bwd(proj, dgmlp, ws, ws_t, bsp_b, gain_v, bias_v):
    def body(u_ref, vb_ref, dg_ref, ws_ref, wst_ref, bsp_ref, g_ref, be_ref, duv_ref, gws_ref, gbs_ref, st_ref):
        @pl.when(pl.program_id(0) == 0)
        def _():
            gws_ref[...] = jnp.zeros_like(gws_ref)
            gbs_ref[...] = jnp.zeros_like(gbs_ref)
            st_ref[...] = jnp.zeros_like(st_ref)

        u, tu, u_act, vb, tv, rstd, xhat, vn = _gmlp_parts(u_ref, vb_ref, g_ref, be_ref)
        dg = dg_ref[...]
        dz = dg * u_act
        row = lax.broadcasted_iota(jnp.int32, (128, 128), 0)
        col = lax.broadcasted_iota(jnp.int32, (128, 128), 1)
        causal = row >= col
        causal_t = row <= col
        dvn_parts = []
        z_parts = []
        for g in range(NH):
            cols = slice(g * 128, (g + 1) * 128)
            vng = vn[:, cols].astype(BF16)
            dzg = dz[:, cols]
            dzb = dzg.astype(BF16)
            wsg = jnp.where(causal, ws_ref[g], 0.0).astype(BF16)
            wsg_t = jnp.where(causal_t, wst_ref[g], 0.0).astype(BF16)
            z_parts.append(_dot(wsg, vng) + bsp_ref[g])
            gws_ref[g] += jnp.where(causal, _dot_nt(dzb, vng), 0.0)
            gbs_ref[g] += jnp.broadcast_to(jnp.sum(dzg, axis=1, keepdims=True), (128, 128))
            dvn_parts.append(_dot(wsg_t, dzb))
        z = jnp.concatenate(z_parts, axis=1)
        dvn = jnp.concatenate(dvn_parts, axis=1)
        du = dg * z * _gelu_grad(u, tu)
        st_ref[0:1, :] += jnp.sum(dvn * xhat, axis=0, keepdims=True)
        st_ref[1:2, :] += jnp.sum(dvn, axis=0, keepdims=True)
        gg = dvn * g_ref[...]
        dgv = rstd * (gg - jnp.mean(gg, axis=1, keepdims=True) - xhat * jnp.mean(gg * xhat, axis=1, keepdims=True))
        dvb = dgv * _gelu_grad(vb, tv)
        duv_ref[:, 0:DB] = du.astype(BF16)
        duv_ref[:, DB:2 * DB] = dvb.astype(BF16)

    full3 = pl.BlockSpec((NH, 128, 128), lambda c: (0, 0, 0))
    vec = pl.BlockSpec((1, DB), lambda c: (0, 0))
    return pl.pallas_call(
        body, name="gmlp_bwd", grid=(NBLK,),
        in_specs=[pl.BlockSpec((128, DB), lambda c: (c, 3)), pl.BlockSpec((128, DB), lambda c: (c, 4)),
                  pl.BlockSpec((128, DB), lambda c: (c, 0)), full3, full3, full3, vec, vec],
        out_specs=[pl.BlockSpec((128, 2 * DB), lambda c: (c, 0)), full3, full3, pl.BlockSpec((8, DB), lambda c: (0, 0))],
        out_shape=[jax.ShapeDtypeStruct((S, 2 * DB), BF16), jax.ShapeDtypeStruct((NH, 128, 128), F32),
                   jax.ShapeDtypeStruct((NH, 128, 128), F32), jax.ShapeDtypeStruct((8, DB), F32)],
        compiler_params=_params(1),
    )(proj, proj, dgmlp, ws, ws_t, bsp_b, gain_v, bias_v)


def _rel_bias_grad(ds_sums):
    buckets = jnp.asarray(np.stack([_bucket_tile(d) for _, d in PATTERNS]))

    def body(bk_ref, ds_ref, o_ref):
        row = lax.broadcasted_iota(jnp.int32, (N_BUCKETS, 128), 0)
        lane = lax.broadcasted_iota(jnp.int32, (N_BUCKETS, 128), 1)

        def one_bucket(t, out):
            hits = [bk_ref[p] == t for p in range(3)]
            for h in range(NH):
                tot = jnp.zeros((128, 256), F32)
                for p in range(3):
                    tot = tot + jnp.where(hits[p], ds_ref[p, h], 0.0)
                out = jnp.where((row == t) & (lane == h), jnp.sum(tot), out)
            return out

        o_ref[...] = lax.fori_loop(0, N_BUCKETS, one_bucket, jnp.zeros((N_BUCKETS, 128), F32))

    return pl.pallas_call(
        body, name="rel_bias_grad",
        in_specs=[pl.BlockSpec(memory_space=pltpu.VMEM)] * 2, out_specs=pl.BlockSpec(memory_space=pltpu.VMEM),
        out_shape=jax.ShapeDtypeStruct((N_BUCKETS, 128), F32),
        compiler_params=pltpu.CompilerParams(vmem_limit_bytes=VMEM_LIMIT),
    )(buckets, ds_sums)


def _d_x(dproj, win_g, dpre1, after=None):
    tm, tn = 512, 512
    ws = DIN // N_CHIPS

    def body(a_ref, w_ref, d_ref, o_ref):
        acc = ALPHA * d_ref[...]
        for s in range(N_CHIPS):
            acc = acc + _dot_nt(a_ref[:, s * ws:(s + 1) * ws], w_ref[s])
        o_ref[...] = acc

    tile = pl.BlockSpec((tm, tn), lambda i, j: (i, j))
    body, more_specs, more = _behind(body, 3, after)
    return pl.pallas_call(
        body, name="d_x", grid=(S // tm, D // tn),
        in_specs=[pl.BlockSpec((tm, DIN), lambda i, j: (i, 0)),
                  pl.BlockSpec((N_CHIPS, tn, ws), lambda i, j: (0, j, 0)), tile] + more_specs,
        out_specs=tile, out_shape=jax.ShapeDtypeStruct((S, D), F32),
        compiler_params=_params(2),
    )(dproj, win_g, dpre1, *more)


def _adamw(w, g, m, v, name, after=None):
    rows, cols = w.shape
    tm = max(t for t in range(8, 257, 8) if rows % t == 0)

    def body(w_ref, g_ref, m_ref, v_ref, d_ref, nm_ref, nv_ref, go_ref):
        g = g_ref[...]
        m = ADAM_B1 * m_ref[...] + (1.0 - ADAM_B1) * g
        v = ADAM_B2 * v_ref[...] + (1.0 - ADAM_B2) * (g * g)
        m_hat = m / (1.0 - ADAM_B1 ** ADAM_STEP)
        v_hat = v / (1.0 - ADAM_B2 ** ADAM_STEP)
        d_ref[...] = -ADAM_LR * (m_hat / (jnp.sqrt(v_hat) + ADAM_EPS) + ADAM_WD * w_ref[...])
        nm_ref[...] = m
        nv_ref[...] = v
        go_ref[...] = g

    spec = pl.BlockSpec((tm, cols), lambda i: (i, 0))
    body, more_specs, more = _behind(body, 4, after)
    return pl.pallas_call(
        body, name=name, grid=(rows // tm,), in_specs=[spec] * 4 + more_specs, out_specs=[spec] * 4,
        out_shape=[jax.ShapeDtypeStruct((rows, cols), F32)] * 4, compiler_params=_params(1),
    )(w, g, m, v, *more)


def _position():
    x, y, c = lax.axis_index("x"), lax.axis_index("y"), lax.axis_index("c")
    chips = [(1 - x, y), (x, 1 - y), (1 - x, 1 - y)]
    return x, y, c, chips


def _remote(src, dst, send_sems, recv_sems, k, to):
    return pltpu.make_async_remote_copy(src_ref=src, dst_ref=dst, send_sem=send_sems.at[k], recv_sem=recv_sems.at[k],
                                        device_id=to, device_id_type=MESH)


def _place_shard(w, name, after=None):
    rows, cols = w.shape
    tm = 256
    x, y = lax.axis_index("x"), lax.axis_index("y")

    def body(chip_ref, w_ref, o_ref):
        o_ref[...] = w_ref[...].astype(BF16)

    more_specs, more = ([ANY], [after]) if after is not None else ([], [])
    if after is not None:
        inner = body
        body = lambda chip_ref, w_ref, after_ref, o_ref: inner(chip_ref, w_ref, o_ref)
    return pl.pallas_call(
        body, name=name,
        grid_spec=pltpu.PrefetchScalarGridSpec(
            num_scalar_prefetch=1, grid=(rows // tm,),
            in_specs=[pl.BlockSpec((tm, cols), lambda i, chip: (i, 0))] + more_specs,
            out_specs=pl.BlockSpec((None, tm, cols), lambda i, chip: (chip[0], i, 0))),
        out_shape=jax.ShapeDtypeStruct((N_CHIPS, rows, cols), BF16),
        compiler_params=_params(1),
    )(jnp.reshape(2 * x + y, (1,)).astype(jnp.int32), w, *more)


def _to_bf16(x, name, after=None):
    tm = 256

    def body(x_ref, o_ref):
        o_ref[...] = x_ref[...].astype(BF16)

    spec = pl.BlockSpec((tm, x.shape[1]), lambda i: (i, 0))
    body, more_specs, more = _behind(body, 1, after)
    return pl.pallas_call(
        body, name=name, grid=(x.shape[0] // tm,), in_specs=[spec] + more_specs, out_specs=spec,
        out_shape=jax.ShapeDtypeStruct(x.shape, BF16), compiler_params=_params(1),
    )(x, *more)


HBM = pl.BlockSpec(memory_space=pltpu.HBM)
SEM = pl.BlockSpec(memory_space=pltpu.SEMAPHORE)
EFFECT = pltpu.SideEffectType.DATAFLOW_SIDE_EFFECTING


def _comm_call(name, body, bufs, sems_in, sems_out, after=None, token=False):
    nb, ns, no = len(bufs), len(sems_in), len(sems_out)
    n_in = nb + ns + (after is not None)

    def wrapped(*refs):
        body(refs[:nb], refs[nb:nb + ns], refs[n_in + nb:n_in + nb + no])
        if token:
            refs[-1][...] = jnp.zeros((8, 128), F32)

    outs = pl.pallas_call(
        wrapped, name=name,
        in_specs=[HBM] * nb + [SEM] * ns + ([ANY] if after is not None else []),
        out_specs=[HBM] * nb + [SEM] * no + ([pl.BlockSpec(memory_space=pltpu.VMEM)] if token else []),
        out_shape=[pltpu.HBM(b.shape, b.dtype) for b in bufs] + [pltpu.SemaphoreType.DMA((k,)) for k in sems_out]
        + ([jax.ShapeDtypeStruct((8, 128), F32)] if token else []),
        input_output_aliases={i: i for i in range(nb)},
        compiler_params=pltpu.CompilerParams(has_side_effects=EFFECT),
    )(*[pltpu.with_memory_space_constraint(b, pltpu.HBM) for b in bufs], *sems_in, *([after] if after is not None else []))
    return list(outs[:nb]), list(outs[nb:nb + no]), (outs[-1] if token else None)


RING_STAGES = {"ici_near": 2, "ici_far": 2, "d2d_near": 2, "d2d_far": 1}


def _ring_copies(buf, send_sems, recv_sems, k0, stage):
    x, y, c, _ = _position()
    hr = buf.shape[1] // 2
    qr = hr // 2
    half = lambda chip, h: buf.at[chip, pl.ds(h * hr, hr), :]
    quarter = lambda chip, h, q: buf.at[chip, pl.ds(h * hr + q * qr, qr), :]
    mine, x_chip, y_chip, far_chip = 2 * x + y, 2 * (1 - x) + y, 2 * x + (1 - y), 2 * (1 - x) + (1 - y)
    to_x, to_y, sibling = (1 - x, y, c), (x, 1 - y, c), (x, y, 1 - c)
    if stage == "ici_near":
        moves = [(half(mine, c), to_x, half(x_chip, c)), (half(mine, c), to_y, half(y_chip, c))]
    elif stage == "ici_far":
        moves = [(quarter(x_chip, c, 0), to_y, quarter(far_chip, c, 0)),
                 (quarter(y_chip, c, 1), to_x, quarter(far_chip, c, 1))]
    elif stage == "d2d_near":
        moves = [(half(x_chip, c), sibling, half(x_chip, 1 - c)), (half(y_chip, c), sibling, half(y_chip, 1 - c))]
    else:
        moves = [(half(far_chip, c), sibling, half(far_chip, 1 - c))]
    sends = [_remote(src, src, send_sems, recv_sems, k0 + i, to) for i, (src, to, _) in enumerate(moves)]
    arrivals = [_remote(got, got, send_sems, recv_sems, k0 + i, (x, y, c)) for i, (_, _, got) in enumerate(moves)]
    return sends, arrivals


def _ring_call(name, groups, actions, after=None):
    tags = list(dict.fromkeys(t for _, t, _ in actions))
    counts = {t: len(groups[t]["bufs"]) for t in tags}
    first = {t: sum(counts[u] for u in tags[:i]) for i, t in enumerate(tags)}
    waits = [(t, s) for v, t, s in actions if v == "wait"]
    starts = [(t, s) for v, t, s in actions if v == "start"]

    def body(bufs, sems_in, sems_out):
        for verb, t, s in actions:
            at, sems = (starts.index((t, s)), sems_out) if verb == "start" else (waits.index((t, s)), sems_in)
            for w in range(counts[t]):
                sends, arrivals = _ring_copies(bufs[first[t] + w], sems[2 * at], sems[2 * at + 1], RING_STAGES[s] * w, s)
                if verb == "start":
                    for cp in sends:
                        cp.start()
                else:
                    for cp in arrivals:
                        cp.wait_recv()
                    for cp in sends:
                        cp.wait_send()

    bufs, sems, token = _comm_call(
        name, body, [b for t in tags for b in groups[t]["bufs"]],
        [sem for t, s in waits for sem in groups[t]["sems"][s]],
        [RING_STAGES[s] * counts[t] for t, s in starts for _ in (0, 1)], after, token=True)
    for t in tags:
        groups[t]["bufs"] = bufs[first[t]:first[t] + counts[t]]
    for t, s in waits:
        del groups[t]["sems"][s]
    for i, (t, s) in enumerate(starts):
        groups[t]["sems"][s] = (sems[2 * i], sems[2 * i + 1])
    return token


def _cx_copies(src, dst, send_sems, recv_sems, k0):
    x, y, c, chips = _position()
    sends = [_remote(src.at[2 * cx + cy], dst.at[2 * x + y], send_sems, recv_sems, k0 + j, (cx, cy, c))
             for j, (cx, cy) in enumerate(chips)]
    arrivals = [_remote(dst.at[2 * cx + cy], dst.at[2 * cx + cy], send_sems, recv_sems, k0 + j, (x, y, c))
                for j, (cx, cy) in enumerate(chips)]
    return sends, arrivals


def _cx_start(name, pair_sums):
    n = len(pair_sums)
    landing = [lax.empty(p.shape, p.dtype) for p in pair_sums]

    def body(bufs, _, sems):
        for w in range(n):
            for cp in _cx_copies(bufs[w], bufs[n + w], sems[0], sems[1], 3 * w)[0]:
                cp.start()

    bufs, sems, token = _comm_call(name, body, list(pair_sums) + landing, [], [3 * n, 3 * n], token=True)
    return (bufs, sems), token


def _cx_wait(name, state, after):
    bufs, sems = state
    n = len(bufs) // 2

    def body(refs, sems_in, _):
        for w in range(n):
            sends, arrivals = _cx_copies(refs[w], refs[n + w], sems_in[0], sems_in[1], 3 * w)
            for cp in arrivals:
                cp.wait_recv()
            for cp in sends:
                cp.wait_send()

    bufs, _, _ = _comm_call(name, body, bufs, sems, [], after)
    return bufs[:n], bufs[n:]


def _px_copies(src, dst, send_sems, recv_sems, k):
    x, y, c, _ = _position()
    hr = src.shape[1] // 2
    send = _remote(src.at[:, pl.ds((1 - c) * hr, hr), :], dst, send_sems, recv_sems, k, (x, y, 1 - c))
    arrival = _remote(dst, dst, send_sems, recv_sems, k, (x, y, c))
    return send, arrival


def _px_start(name, grads):
    n = len(grads)
    landing = [lax.empty((N_CHIPS, g.shape[1] // 2, g.shape[2]), g.dtype) for g in grads]

    def body(bufs, _, sems):
        for w in range(n):
            _px_copies(bufs[w], bufs[n + w], sems[0], sems[1], w)[0].start()

    bufs, sems, token = _comm_call(name, body, list(grads) + landing, [], [n, n], token=True)
    return (bufs, sems), token


def _px_wait(name, state, after):
    bufs, sems = state
    n = len(bufs) // 2

    def body(refs, sems_in, _):
        for w in range(n):
            send, arrival = _px_copies(refs[w], refs[n + w], sems_in[0], sems_in[1], w)
            arrival.wait_recv()
            send.wait_send()

    bufs, _, _ = _comm_call(name, body, bufs, sems, [], after)
    return bufs[:n], bufs[n:]


def _pair_sum(grad, got, name):
    _, rows, cols = grad.shape
    hr = rows // 2
    tm = min(hr, 512)
    nb = hr // tm
    c = lax.axis_index("c")

    def body(c_ref, g_ref, o_ref, out_ref):
        out_ref[...] = (g_ref[...].astype(F32) + o_ref[...].astype(F32)).astype(BF16)

    return pl.pallas_call(
        body, name=name,
        grid_spec=pltpu.PrefetchScalarGridSpec(
            num_scalar_prefetch=1, grid=(N_CHIPS, nb),
            in_specs=[pl.BlockSpec((None, tm, cols), lambda s, i, c_ref: (s, c_ref[0] * nb + i, 0)),
                      pl.BlockSpec((None, tm, cols), lambda s, i, c_ref: (s, i, 0))],
            out_specs=pl.BlockSpec((None, tm, cols), lambda s, i, c_ref: (s, i, 0))),
        out_shape=jax.ShapeDtypeStruct((N_CHIPS, hr, cols), BF16),
        compiler_params=_params(2),
    )(jnp.reshape(c, (1,)).astype(jnp.int32), grad, got)


def _chip_sum(parts, pair_sums, name):
    _, hr, cols = parts.shape
    tm = min(hr, 512)
    nb = hr // tm
    x, y, c = lax.axis_index("x"), lax.axis_index("y"), lax.axis_index("c")

    def body(pos_ref, p_ref, own_ref, o_ref):
        chip = pos_ref[0]
        own = own_ref[...].astype(F32)
        term = lambda s: jnp.where(chip == s, own, p_ref[s].astype(F32))
        o_ref[...] = ((term(0) + term(1)) + term(2)) + term(3)

    return pl.pallas_call(
        body, name=name,
        grid_spec=pltpu.PrefetchScalarGridSpec(
            num_scalar_prefetch=1, grid=(nb,),
            in_specs=[pl.BlockSpec((N_CHIPS, tm, cols), lambda i, pos: (0, i, 0)),
                      pl.BlockSpec((None, tm, cols), lambda i, pos: (pos[0], i, 0))],
            out_specs=pl.BlockSpec((tm, cols), lambda i, pos: (pos[1] * nb + i, 0))),
        out_shape=jax.ShapeDtypeStruct((2 * hr, cols), F32), compiler_params=_params(1),
    )(jnp.stack([2 * x + y, c]).astype(jnp.int32), parts, pair_sums)


def _share_copies(buf, send_sems, recv_sems, k):
    x, y, c, _ = _position()
    hr = buf.shape[0] // 2
    mine, theirs = buf.at[pl.ds(c * hr, hr), :], buf.at[pl.ds((1 - c) * hr, hr), :]
    return (_remote(mine, mine, send_sems, recv_sems, k, (x, y, 1 - c)),
            _remote(theirs, theirs, send_sems, recv_sems, k, (x, y, c)))


def _share_start(name, bufs):
    n = len(bufs)

    def body(refs, _, sems):
        for w in range(n):
            _share_copies(refs[w], sems[0], sems[1], w)[0].start()

    bufs, sems, token = _comm_call(name, body, list(bufs), [], [n, n], token=True)
    return (bufs, sems), token


def _share_wait(name, state, after):
    bufs, sems = state

    def body(refs, sems_in, _):
        for w in range(len(bufs)):
            send, arrival = _share_copies(refs[w], sems_in[0], sems_in[1], w)
            arrival.wait_recv()
            send.wait_send()

    return _comm_call(name, body, bufs, sems, [], after)[0]


def _allreduce_small(g):
    rows = g.shape[0]
    half = rows // 2

    def body(g_ref, o_ref, sib, slots, send_sems, recv_sems):
        x, y, c, chips = _position()
        me, sibling = (x, y, c), (x, y, 1 - c)
        my_chip = 2 * x + y
        mine = pl.ds(pl.multiple_of(c * half, 8), half)
        theirs = pl.ds(pl.multiple_of((1 - c) * half, 8), half)
        pair = _remote(g_ref.at[theirs], sib, send_sems, recv_sems, 0, sibling)
        pair.start()
        pair.wait()
        slots[my_chip] = g_ref[mine, :] + sib[...]
        sent = []
        for j, (cx, cy) in enumerate(chips):
            cp = _remote(slots.at[my_chip], slots.at[my_chip], send_sems, recv_sems, 1 + j, (cx, cy, c))
            cp.start()
            sent.append(cp)
        for j, (cx, cy) in enumerate(chips):
            got = slots.at[2 * cx + cy]
            _remote(got, got, send_sems, recv_sems, 1 + j, me).wait_recv()
        for cp in sent:
            cp.wait_send()
        o_ref[mine, :] = ((slots[0] + slots[1]) + slots[2]) + slots[3]
        swap = _remote(o_ref.at[mine], o_ref.at[mine], send_sems, recv_sems, 4, sibling)
        swap.start()
        swap.wait()

    vm = pl.BlockSpec(memory_space=pltpu.VMEM)
    return pl.pallas_call(
        body, name="allreduce_small",
        in_specs=[vm], out_specs=vm, out_shape=jax.ShapeDtypeStruct((rows, 128), F32),
        scratch_shapes=[pltpu.VMEM((half, 128), F32), pltpu.VMEM((N_CHIPS, half, 128), F32),
                        pltpu.SemaphoreType.DMA((5,)), pltpu.SemaphoreType.DMA((5,))],
        compiler_params=pltpu.CompilerParams(vmem_limit_bytes=VMEM_LIMIT),
    )(g)


_SMALL =("rel_bias", "ln_v_gain", "ln_v_bias", "w_spatial", "b_spatial", "ln1_gain", "ln1_bias",
          "b_ff1", "b_ff2", "ln2_gain", "ln2_bias")
_SMALL_ROWS = 1200
_LOSS_AT = (152832 // 128, 0)


def _pack_small(parts):
    flat = jnp.concatenate([parts[k].reshape(-1).astype(F32) for k in _SMALL])
    flat = jnp.pad(flat, (0, _SMALL_ROWS * 128 - flat.shape[0]))
    return flat.reshape(_SMALL_ROWS, 128)


def _unpack_small(packed, like):
    flat = packed.reshape(-1)
    out, at = {}, 0
    for k in _SMALL:
        n = math.prod(like[k].shape)
        out[k] = flat[at:at + n].reshape(like[k].shape)
        at += n
    return out


def kernel(x, w_in, rel_bias, ln_v_gain, ln_v_bias, w_spatial, b_spatial, w_proj_a, w_proj_b, w_out, ln1_gain, ln1_bias, w_ff1, b_ff1, w_ff2, b_ff2, ln2_gain, ln2_bias, loss_target, m_w_in, m_rel_bias, m_ln_v_gain, m_ln_v_bias, m_w_spatial, m_b_spatial, m_w_proj_a, m_w_proj_b, m_w_out, m_ln1_gain, m_ln1_bias, m_w_ff1, m_b_ff1, m_w_ff2, m_b_ff2, m_ln2_gain, m_ln2_bias, v_w_in, v_rel_bias, v_ln_v_gain, v_ln_v_bias, v_w_spatial, v_b_spatial, v_w_proj_a, v_w_proj_b, v_w_out, v_ln1_gain, v_ln1_bias, v_w_ff1, v_b_ff1, v_w_ff2, v_b_ff2, v_ln2_gain, v_ln2_bias):
    args = dict(locals())
    big = ("w_in", "w_proj_a", "w_proj_b", "w_out", "w_ff1", "w_ff2")
    weights = ("w_in", "rel_bias", "ln_v_gain", "ln_v_bias", "w_spatial", "b_spatial", "w_proj_a", "w_proj_b", "w_out",
               "ln1_gain", "ln1_bias", "w_ff1", "b_ff1", "w_ff2", "b_ff2", "ln2_gain", "ln2_bias")

    xs = x[0]
    target = loss_target[0]

    ring = {"a": {"bufs": [_place_shard(w_in[0], "place_w_in")], "sems": {}}}
    tok = _ring_call("allgather_a_near", ring, [("start", "a", "ici_near")])
    placed = [_place_shard(args[k][0], f"place_{k}", after=tok) for k in big[1:]]
    for tag, bufs in (("b", placed[0:3]), ("c", placed[3:4]), ("d", placed[4:5])):
        ring[tag] = {"bufs": bufs, "sems": {}}
    xb = _to_bf16(xs, "x_to_bf16", after=placed[4])

    mx, my = lax.axis_index("x"), lax.axis_index("y")
    own = jnp.reshape(2 * mx + my, (1,)).astype(jnp.int32)
    near = jnp.stack([2 * (1 - mx) + my, 2 * mx + (1 - my)]).astype(jnp.int32)
    far = jnp.reshape(2 * (1 - mx) + (1 - my), (1,)).astype(jnp.int32)
    proj = _proj(xb, ring["a"]["bufs"][0], own, "proj_own")
    _ring_call("allgather_a_far", ring, [("wait", "a", "ici_near"), ("start", "a", "ici_far"), ("start", "a", "d2d_near"),
                                         ("start", "b", "ici_near"), ("start", "c", "ici_near")], after=proj)
    _ring_call("allgather_a_near_done", ring, [("wait", "a", "d2d_near")])
    proj = _proj(xb, ring["a"]["bufs"][0], near, "proj_near", into=proj)
    _ring_call("allgather_a_last", ring, [("wait", "a", "ici_far"), ("start", "a", "d2d_far")], after=proj)
    _ring_call("allgather_a_done", ring, [("wait", "a", "d2d_far")])
    (win_g,) = ring["a"]["bufs"]
    proj = _proj(xb, win_g, far, "proj_far", into=proj)
    _ring_call("allgather_b_far", ring, [("wait", "b", "ici_near"), ("start", "b", "ici_far"), ("start", "b", "d2d_near")],
               after=proj)
    ws = w_spatial[0]
    ws_t = jnp.transpose(ws, (0, 2, 1))
    bsp_b = jnp.broadcast_to(b_spatial[0][:, :, None], (NH, 128, 128))
    gmlp = _gmlp_fwd(proj, ws, bsp_b, ln_v_gain, ln_v_bias)
    bias = _bias_tiles(rel_bias)
    attn, lse = _attention_fwd(proj, bias)
    _ring_call("allgather_b_last_c_far", ring,
               [("wait", "b", "ici_far"), ("start", "b", "d2d_far"),
                ("wait", "c", "ici_near"), ("start", "c", "ici_far"), ("start", "c", "d2d_near"),
                ("start", "d", "ici_near")], after=attn)
    _ring_call("allgather_b_done", ring, [("wait", "b", "d2d_near"), ("wait", "b", "d2d_far")])
    wpa_g, wpb_g, wout_g = ring["b"]["bufs"]
    wout_full = wout_g.reshape(D, D)
    ya, yb, merged = _branch(attn, gmlp, wpa_g, wpb_g, proj)
    xhat1, rstd1, h1b = _out_ln1(merged, wout_full, xs, ln1_gain, ln1_bias)
    _ring_call("allgather_c_last", ring, [("wait", "c", "ici_far"), ("start", "c", "d2d_far")], after=h1b)
    _ring_call("allgather_c_done", ring, [("wait", "c", "d2d_near"), ("wait", "c", "d2d_far")])
    (w1_g,) = ring["c"]["bufs"]
    a, r = _ff1(h1b, w1_g, b_ff1, 0, "ff1_first")
    tok = _ring_call("allgather_d_far", ring,
                     [("wait", "d", "ici_near"), ("start", "d", "ici_far"), ("start", "d", "d2d_near")], after=a)
    a, r = _ff1(h1b, w1_g, b_ff1, 1, "ff1_second", into=(a, r), after=tok)
    _ring_call("allgather_d_last", ring, [("wait", "d", "ici_far"), ("start", "d", "d2d_far")], after=a)
    _ring_call("allgather_d_done", ring, [("wait", "d", "d2d_near"), ("wait", "d", "d2d_far")])
    (w2_g,) = ring["d"]["bufs"]
    w2_full = w2_g.reshape(DFF, D)
    dpre2, dpre2b, st2 = _ff2_ln2_loss(a, w2_full, xhat1, ln1_gain, ln1_bias, b_ff2, ln2_gain, ln2_bias, target)

    def pair_and_chip(tag, state, after):
        local, from_sibling = _px_wait(f"pair_exchange_wait_{tag}", state, after)
        pair_sums = [_pair_sum(g, o, f"pair_sum_{tag}_{i}") for i, (g, o) in enumerate(zip(local, from_sibling))]
        return _cx_start(f"chip_exchange_start_{tag}", pair_sums)

    g_w2 = _grad_w(a, dpre2b, "grad_w_ff2", 512, 2048, False)
    px, tok = _px_start("pair_exchange_start_w_ff2", [g_w2.reshape(N_CHIPS, DFF // N_CHIPS, D)])
    dprea, g_b1 = _d_ff1(dpre2b, w2_full, r, after=tok)
    cx_w2, tok = pair_and_chip("w_ff2", px, dprea)
    g_w1 = _grad_w(h1b, dprea, "grad_w_ff1", 512, 2048, True, after=tok)
    px, tok = _px_start("pair_exchange_start_w_ff1", [g_w1])
    dpre1, dpre1b, st1 = _d_h1_ln1(dprea, w1_g, dpre2, xhat1, rstd1, ln1_gain, after=tok)
    cx_w1, tok = pair_and_chip("w_ff1", px, dpre1b)
    g_wout = _grad_w(merged, dpre1b, "grad_w_out", 512, 2048, False, after=tok)
    dya, dyb, dga, dgb = _d_merged(dpre1b, wout_full, proj, ya, yb)
    g_wpa = _grad_w(attn, dya, "grad_w_proj_a", 1024, 512, True)
    g_wpb = _grad_w(gmlp, dyb, "grad_w_proj_b", 1024, 512, True)
    px, tok = _px_start("pair_exchange_start_b", [g_wpa, g_wpb, g_wout.reshape(N_CHIPS, D // N_CHIPS, D)])
    dattn, dgmlp = _d_branches(dya, dyb, wpa_g, wpb_g, after=tok)
    duv, g_ws, g_bs, stv = _gmlp_bwd(proj, dgmlp, ws, ws_t, bsp_b, ln_v_gain, ln_v_bias)
    cx_b, tok = pair_and_chip("b", px, duv)
    dq, dk, dv, ds_sums = _attention_bwd(proj, dattn, attn, lse, bias, after=tok)
    g_rb = _rel_bias_grad(ds_sums)[:, :NH]

    small_g = dict(rel_bias=g_rb, ln_v_gain=stv[0], ln_v_bias=stv[1], w_spatial=g_ws, b_spatial=g_bs[:, :, 0],
                   ln1_gain=st1[0], ln1_bias=st1[1], b_ff1=g_b1, b_ff2=st2[2], ln2_gain=st2[0], ln2_bias=st2[1])
    gs = _allreduce_small(_pack_small(small_g).at[_LOSS_AT].set(st2[3, 0]))
    ds_, ms_, vs_, _ = _adamw(_pack_small({k: args[k] for k in _SMALL}), gs,
                           _pack_small({k: args["m_" + k] for k in _SMALL}),
                           _pack_small({k: args["v_" + k] for k in _SMALL}), "adamw_small")
    like = {k: args[k] for k in _SMALL}
    grads, deltas, new_m, new_v = (_unpack_small(t, like) for t in (gs, ds_, ms_, vs_))

    dproj = jnp.concatenate([dq, dk, dv, duv, dga, dgb], axis=1)
    g_win = _grad_w(xb, dproj, "grad_w_in", 512, 2304, True, after=gs)
    px, tok = _px_start("pair_exchange_start_w_in", [g_win])

    def chip_sums(tag, state, names, after):
        pair_sums, from_chips = _cx_wait(f"chip_exchange_wait_{tag}", state, after)
        halves = [_chip_sum(p, own, f"chip_sum_{k}") for p, own, k in zip(from_chips, pair_sums, names)]
        return _share_start(f"share_start_{tag}", halves)

    def adam_one(k, g, after=None):
        d_, m_, v_, g_ = _adamw(args[k][0], g, args["m_" + k][0], args["v_" + k][0], f"adamw_{k}", after=after)
        grads[k], deltas[k], new_m[k], new_v[k] = g_[None], d_[None], m_[None], v_[None]
        return d_

    def adam(tag, state, names, after):
        last = None
        for k, g in zip(names, _share_wait(f"share_wait_{tag}", state, after)):
            last = adam_one(k, g)
        return last

    sh_w2, tok = chip_sums("w_ff2", cx_w2, ["w_ff2"], tok)
    sh_w1, tok = chip_sums("w_ff1", cx_w1, ["w_ff1"], tok)
    sh_b, tok = chip_sums("b", cx_b, ["w_proj_a", "w_proj_b", "w_out"], tok)
    cx_in, tok = pair_and_chip("w_in", px, tok)
    grad_x = _d_x(dproj, win_g, dpre1, after=tok)
    done = adam("w_ff2", sh_w2, ["w_ff2"], grad_x)
    done = adam("w_ff1", sh_w1, ["w_ff1"], done)
    g_wpa_full, g_wpb_full, g_wout_full = _share_wait("share_wait_b", sh_b, done)
    done = adam_one("w_out", g_wout_full)
    sh_in, tok = chip_sums("w_in", cx_in, ["w_in"], done)
    done = adam_one("w_proj_a", g_wpa_full, after=tok)
    done = adam_one("w_proj_b", g_wpb_full, after=done)
    adam("w_in", sh_in, ["w_in"], done)

    loss = gs[_LOSS_AT] * (0.5 / D)
    return (loss, grad_x[None], *[grads[k] for k in weights], *[deltas[k] for k in weights],
            *[new_m[k] for k in weights], *[new_v[k] for k in weights])
```
